```python
import math
import jax, jax.numpy as jnp
from jax import lax
import numpy as np

D_MODEL = 2048
BATCH = 8
SEQ = 2048
DEPTH = 1

CHUNK = 64
Q_BLOCK = 128

N_HEADS = 16
QK_NOPE = 128
QK_ROPE = 64
V_HEAD = 128
Q_LORA = 512
KV_LORA = 512
ROPE_THETA = 10000.0
ATTN_SCALE = (QK_NOPE + QK_ROPE) ** -0.5

CONV_WIDTH = D_MODEL
CONV_K = 3

D_FF = ((8 * D_MODEL + 3 * 256 - 1) // (3 * 256)) * 256

COL_Q_A = Q_LORA
COL_KV_A = KV_LORA
COL_K_ROPE = QK_ROPE
COL_CONV_B = CONV_WIDTH
COL_CONV_C = CONV_WIDTH
COL_CONV_X = CONV_WIDTH
COL_GATE_A = D_MODEL
COL_GATE_B = D_MODEL
D_IN_ALL = COL_Q_A + COL_KV_A + COL_K_ROPE + COL_CONV_B + COL_CONV_C + COL_CONV_X + COL_GATE_A + COL_GATE_B
SPLIT_POINTS = (
    COL_Q_A,
    COL_Q_A + COL_KV_A,
    COL_Q_A + COL_KV_A + COL_K_ROPE,
    COL_Q_A + COL_KV_A + COL_K_ROPE + COL_CONV_B,
    COL_Q_A + COL_KV_A + COL_K_ROPE + COL_CONV_B + COL_CONV_C,
    COL_Q_A + COL_KV_A + COL_K_ROPE + COL_CONV_B + COL_CONV_C + COL_CONV_X,
    COL_Q_A + COL_KV_A + COL_K_ROPE + COL_CONV_B + COL_CONV_C + COL_CONV_X + COL_GATE_A,
)

DEEPNORM_ALPHA = (2.0 * DEPTH) ** 0.25
DEEPNORM_BETA = (8.0 * DEPTH) ** -0.25
LN_EPS = 1e-5
RMS_EPS = 1e-6

kernel_name = "hybrid_mla_shortconv_swiglu_deepnorm_adaln"


def layer_norm(x, g, b):
    xf = x.astype(jnp.float32)
    mu = jnp.mean(xf, axis=-1, keepdims=True)
    var = jnp.mean(jnp.square(xf - mu), axis=-1, keepdims=True)
    y = (xf - mu) * lax.rsqrt(var + LN_EPS)
    return (y * g.astype(jnp.float32) + b.astype(jnp.float32)).astype(x.dtype)


def rms_norm(x, g):
    xf = x.astype(jnp.float32)
    y = xf * lax.rsqrt(jnp.mean(jnp.square(xf), axis=-1, keepdims=True) + RMS_EPS)
    return (y * g.astype(jnp.float32)).astype(x.dtype)


def rope_tables(positions):
    inv_freq = 1.0 / (ROPE_THETA ** (jnp.arange(0, QK_ROPE, 2, dtype=jnp.float32) / QK_ROPE))
    ang = positions.astype(jnp.float32)[..., None] * inv_freq
    return jnp.cos(ang), jnp.sin(ang)


def apply_rope(x, cos, sin):
    xf = x.astype(jnp.float32)
    x1, x2 = jnp.split(xf, 2, axis=-1)
    out = jnp.concatenate([x1 * cos - x2 * sin, x2 * cos + x1 * sin], axis=-1)
    return out.astype(x.dtype)


def chunk_causal_mla_attention(q_nope, q_rope, k_nope, k_rope, v):
    S = q_nope.shape[1]
    outs = []
    for i in range(S // Q_BLOCK):
        q0 = i * Q_BLOCK
        L = q0 + Q_BLOCK
        s = (jnp.einsum('bqhd,bkhd->bhqk', q_nope[:, q0:L], k_nope[:, :L])
             + jnp.einsum('bqhr,bkr->bhqk', q_rope[:, q0:L], k_rope[:, :L]))
        s = s.astype(jnp.float32) * ATTN_SCALE
        q_chunk = (q0 + jnp.arange(Q_BLOCK)) // CHUNK
        k_chunk = jnp.arange(L) // CHUNK
        allowed = k_chunk[None, :] <= q_chunk[:, None]
        s = jnp.where(allowed[None, None], s, jnp.float32(-1e30))
        p = jax.nn.softmax(s, axis=-1).astype(v.dtype)
        outs.append(jnp.einsum('bhqk,bkhd->bqhd', p, v[:, :L]))
    return jnp.concatenate(outs, axis=1)


def causal_depthwise_conv(z, w_conv):
    S = z.shape[1]
    zp = jnp.pad(z, ((0, 0), (CONV_K - 1, 0), (0, 0)))
    out = zp[:, 0:S] * w_conv[0]
    for k in range(1, CONV_K):
        out = out + zp[:, k:k + S] * w_conv[k]
    return out


def _fwd_setup_inputs(seed: int = 0) -> dict:
    key = jax.random.key(seed)
    ks = jax.random.split(key, 24)
    f32 = jnp.float32

    def nrm(k, shape, scale):
        return jax.random.normal(k, shape, f32) * scale

    x = nrm(ks[0], (BATCH, SEQ, D_MODEL), 1.0)
    c = nrm(ks[1], (BATCH, D_MODEL), 1.0)
    offsets = jax.random.randint(ks[2], (BATCH, 1), 0, 64, dtype=jnp.int32) * CHUNK
    positions = (offsets + jnp.arange(SEQ, dtype=jnp.int32)[None, :]).astype(jnp.int32)

    inputs = {
        "x": x,
        "c": c,
        "positions": positions,
        "w_ada": nrm(ks[3], (DEPTH, D_MODEL, 6 * D_MODEL), 0.5 * D_MODEL ** -0.5),
        "b_ada": nrm(ks[4], (DEPTH, 6 * D_MODEL), 0.01),
        "w_in": nrm(ks[5], (DEPTH, D_MODEL, D_IN_ALL), D_MODEL ** -0.5),
        "g_q_a": 1.0 + nrm(ks[6], (DEPTH, Q_LORA), 0.02),
        "w_q_b": nrm(ks[7], (DEPTH, Q_LORA, N_HEADS * (QK_NOPE + QK_ROPE)), Q_LORA ** -0.5),
        "g_kv_a": 1.0 + nrm(ks[8], (DEPTH, KV_LORA), 0.02),
        "w_kv_b": nrm(ks[9], (DEPTH, KV_LORA, N_HEADS * (QK_NOPE + V_HEAD)), KV_LORA ** -0.5),
        "w_o_a": nrm(ks[10], (DEPTH, N_HEADS * V_HEAD, D_MODEL), (N_HEADS * V_HEAD) ** -0.5 * DEEPNORM_BETA),
        "w_conv": nrm(ks[11], (DEPTH, CONV_K, CONV_WIDTH), CONV_K ** -0.5),
        "w_o_b": nrm(ks[12], (DEPTH, CONV_WIDTH, D_MODEL), CONV_WIDTH ** -0.5 * DEEPNORM_BETA),
        "w_o": nrm(ks[13], (DEPTH, D_MODEL, D_MODEL), D_MODEL ** -0.5 * DEEPNORM_BETA),
        "ln1_g": 1.0 + nrm(ks[14], (DEPTH, D_MODEL), 0.02),
        "ln1_b": nrm(ks[15], (DEPTH, D_MODEL), 0.02),
        "w_ffn_in": nrm(ks[16], (DEPTH, D_MODEL, 2 * D_FF), D_MODEL ** -0.5),
        "w_ffn_out": nrm(ks[17], (DEPTH, D_FF, D_MODEL), D_FF ** -0.5 * DEEPNORM_BETA),
        "ln2_g": 1.0 + nrm(ks[18], (DEPTH, D_MODEL), 0.02),
        "ln2_b": nrm(ks[19], (DEPTH, D_MODEL), 0.02),
    }
    return inputs


def _fwd_reference(x, c, positions, w_ada, b_ada, w_in, g_q_a, w_q_b, g_kv_a, w_kv_b, w_o_a,
              w_conv, w_o_b, w_o, ln1_g, ln1_b, w_ffn_in, w_ffn_out, ln2_g, ln2_b):
    B, S, D = x.shape
    cos, sin = rope_tables(positions)
    cos_q, sin_q = cos[:, :, None, :], sin[:, :, None, :]
    c_act = jax.nn.silu(c)

    for l in range(DEPTH):
        mod = c_act @ w_ada[l] + b_ada[l]
        shift1, scale1, gate1, shift2, scale2, gate2 = [m[:, None, :] for m in jnp.split(mod, 6, axis=-1)]

        u = x * (1.0 + scale1) + shift1
        proj = u @ w_in[l]
        q_a, kv_a, k_rope, conv_b, conv_c, conv_x, gate_a, gate_b = jnp.split(proj, SPLIT_POINTS, axis=-1)

        q = (rms_norm(q_a, g_q_a[l]) @ w_q_b[l]).reshape(B, S, N_HEADS, QK_NOPE + QK_ROPE)
        q_nope, q_rope = q[..., :QK_NOPE], apply_rope(q[..., QK_NOPE:], cos_q, sin_q)
        kv = (rms_norm(kv_a, g_kv_a[l]) @ w_kv_b[l]).reshape(B, S, N_HEADS, QK_NOPE + V_HEAD)
        k_nope, v = kv[..., :QK_NOPE], kv[..., QK_NOPE:]
        k_rope = apply_rope(k_rope, cos, sin)
        attn = chunk_causal_mla_attention(q_nope, q_rope, k_nope, k_rope, v)
        y_a = attn.reshape(B, S, N_HEADS * V_HEAD) @ w_o_a[l]

        z = conv_c * conv_x
        y_b = (conv_b * causal_depthwise_conv(z, w_conv[l])) @ w_o_b[l]

        merged = jax.nn.sigmoid(gate_a) * y_a + jax.nn.sigmoid(gate_b) * y_b
        mix_out = merged @ w_o[l]
        x = layer_norm(DEEPNORM_ALPHA * x + gate1 * mix_out, ln1_g[l], ln1_b[l])

        u2 = x * (1.0 + scale2) + shift2
        h_gate, h_up = jnp.split(u2 @ w_ffn_in[l], 2, axis=-1)
        ffn_out = (jax.nn.silu(h_gate) * h_up) @ w_ffn_out[l]
        x = layer_norm(DEEPNORM_ALPHA * x + gate2 * ffn_out, ln2_g[l], ln2_b[l])

    return x


import jax as _jax
import jax.numpy as _jnp

TWIN_FORMAT = 'train_step'
FWD_PARAMS = ['x', 'c', 'positions', 'w_ada', 'b_ada', 'w_in', 'g_q_a', 'w_q_b', 'g_kv_a', 'w_kv_b', 'w_o_a', 'w_conv', 'w_o_b', 'w_o', 'ln1_g', 'ln1_b', 'w_ffn_in', 'w_ffn_out', 'ln2_g', 'ln2_b']
TWIN_WEIGHTS = ['w_ada', 'b_ada', 'w_in', 'g_q_a', 'w_q_b', 'g_kv_a', 'w_kv_b', 'w_o_a', 'w_conv', 'w_o_b', 'w_o', 'ln1_g', 'ln1_b', 'w_ffn_in', 'w_ffn_out', 'ln2_g', 'ln2_b']
TWIN_DIFF_INPUT = 'x'
TWIN_INPUTS = ['x', 'c', 'positions', 'w_ada', 'b_ada', 'w_in', 'g_q_a', 'w_q_b', 'g_kv_a', 'w_kv_b', 'w_o_a', 'w_conv', 'w_o_b', 'w_o', 'ln1_g', 'ln1_b', 'w_ffn_in', 'w_ffn_out', 'ln2_g', 'ln2_b', 'loss_target', 'm_w_ada', 'm_b_ada', 'm_w_in', 'm_g_q_a', 'm_w_q_b', 'm_g_kv_a', 'm_w_kv_b', 'm_w_o_a', 'm_w_conv', 'm_w_o_b', 'm_w_o', 'm_ln1_g', 'm_ln1_b', 'm_w_ffn_in', 'm_w_ffn_out', 'm_ln2_g', 'm_ln2_b', 'v_w_ada', 'v_b_ada', 'v_w_in', 'v_g_q_a', 'v_w_q_b', 'v_g_kv_a', 'v_w_kv_b', 'v_w_o_a', 'v_w_conv', 'v_w_o_b', 'v_w_o', 'v_ln1_g', 'v_ln1_b', 'v_w_ffn_in', 'v_w_ffn_out', 'v_ln2_g', 'v_ln2_b']
TWIN_OUTPUTS = ['loss', 'grad_x', 'grad_w_ada', 'grad_b_ada', 'grad_w_in', 'grad_g_q_a', 'grad_w_q_b', 'grad_g_kv_a', 'grad_w_kv_b', 'grad_w_o_a', 'grad_w_conv', 'grad_w_o_b', 'grad_w_o', 'grad_ln1_g', 'grad_ln1_b', 'grad_w_ffn_in', 'grad_w_ffn_out', 'grad_ln2_g', 'grad_ln2_b', 'delta_w_ada', 'delta_b_ada', 'delta_w_in', 'delta_g_q_a', 'delta_w_q_b', 'delta_g_kv_a', 'delta_w_kv_b', 'delta_w_o_a', 'delta_w_conv', 'delta_w_o_b', 'delta_w_o', 'delta_ln1_g', 'delta_ln1_b', 'delta_w_ffn_in', 'delta_w_ffn_out', 'delta_ln2_g', 'delta_ln2_b', 'new_m_w_ada', 'new_m_b_ada', 'new_m_w_in', 'new_m_g_q_a', 'new_m_w_q_b', 'new_m_g_kv_a', 'new_m_w_kv_b', 'new_m_w_o_a', 'new_m_w_conv', 'new_m_w_o_b', 'new_m_w_o', 'new_m_ln1_g', 'new_m_ln1_b', 'new_m_w_ffn_in', 'new_m_w_ffn_out', 'new_m_ln2_g', 'new_m_ln2_b', 'new_v_w_ada', 'new_v_b_ada', 'new_v_w_in', 'new_v_g_q_a', 'new_v_w_q_b', 'new_v_g_kv_a', 'new_v_w_kv_b', 'new_v_w_o_a', 'new_v_w_conv', 'new_v_w_o_b', 'new_v_w_o', 'new_v_ln1_g', 'new_v_ln1_b', 'new_v_w_ffn_in', 'new_v_w_ffn_out', 'new_v_ln2_g', 'new_v_ln2_b']
TWIN_LEAF_KINDS = {'loss': 'loss', 'grad_x': 'grad_x', 'grad_w_ada': 'grad_w', 'grad_b_ada': 'grad_w', 'grad_w_in': 'grad_w', 'grad_g_q_a': 'grad_w', 'grad_w_q_b': 'grad_w', 'grad_g_kv_a': 'grad_w', 'grad_w_kv_b': 'grad_w', 'grad_w_o_a': 'grad_w', 'grad_w_conv': 'grad_w', 'grad_w_o_b': 'grad_w', 'grad_w_o': 'grad_w', 'grad_ln1_g': 'grad_w', 'grad_ln1_b': 'grad_w', 'grad_w_ffn_in': 'grad_w', 'grad_w_ffn_out': 'grad_w', 'grad_ln2_g': 'grad_w', 'grad_ln2_b': 'grad_w', 'delta_w_ada': 'delta_w', 'delta_b_ada': 'delta_w', 'delta_w_in': 'delta_w', 'delta_g_q_a': 'delta_w', 'delta_w_q_b': 'delta_w', 'delta_g_kv_a': 'delta_w', 'delta_w_kv_b': 'delta_w', 'delta_w_o_a': 'delta_w', 'delta_w_conv': 'delta_w', 'delta_w_o_b': 'delta_w', 'delta_w_o': 'delta_w', 'delta_ln1_g': 'delta_w', 'delta_ln1_b': 'delta_w', 'delta_w_ffn_in': 'delta_w', 'delta_w_ffn_out': 'delta_w', 'delta_ln2_g': 'delta_w', 'delta_ln2_b': 'delta_w', 'new_m_w_ada': 'new_m', 'new_m_b_ada': 'new_m', 'new_m_w_in': 'new_m', 'new_m_g_q_a': 'new_m', 'new_m_w_q_b': 'new_m', 'new_m_g_kv_a': 'new_m', 'new_m_w_kv_b': 'new_m', 'new_m_w_o_a': 'new_m', 'new_m_w_conv': 'new_m', 'new_m_w_o_b': 'new_m', 'new_m_w_o': 'new_m', 'new_m_ln1_g': 'new_m', 'new_m_ln1_b': 'new_m', 'new_m_w_ffn_in': 'new_m', 'new_m_w_ffn_out': 'new_m', 'new_m_ln2_g': 'new_m', 'new_m_ln2_b': 'new_m', 'new_v_w_ada': 'new_v', 'new_v_b_ada': 'new_v', 'new_v_w_in': 'new_v', 'new_v_g_q_a': 'new_v', 'new_v_w_q_b': 'new_v', 'new_v_g_kv_a': 'new_v', 'new_v_w_kv_b': 'new_v', 'new_v_w_o_a': 'new_v', 'new_v_w_conv': 'new_v', 'new_v_w_o_b': 'new_v', 'new_v_w_o': 'new_v', 'new_v_ln1_g': 'new_v', 'new_v_ln1_b': 'new_v', 'new_v_w_ffn_in': 'new_v', 'new_v_w_ffn_out': 'new_v', 'new_v_ln2_g': 'new_v', 'new_v_ln2_b': 'new_v'}


def _forward(args):
    return _fwd_reference(*[args[k] for k in FWD_PARAMS])


def _output_shape():
    out = _jax.eval_shape(lambda: _forward(_fwd_setup_inputs(0)))
    return out.shape, out.dtype

N_MICROBATCH = 1
ADAM_LR = 0.001
ADAM_B1 = 0.9
ADAM_B2 = 0.999
ADAM_EPS = 1e-08
ADAM_WD = 0.01
ADAM_STEP = 10
PER_EXAMPLE_BATCH_AXIS = {'x': 0, 'c': 0, 'positions': 0, 'loss_target': 0}
SHARED_INPUTS = []
_WEIGHT_DTYPES = {'w_ada': _jnp.float32, 'b_ada': _jnp.float32, 'w_in': _jnp.float32, 'g_q_a': _jnp.float32, 'w_q_b': _jnp.float32, 'g_kv_a': _jnp.float32, 'w_kv_b': _jnp.float32, 'w_o_a': _jnp.float32, 'w_conv': _jnp.float32, 'w_o_b': _jnp.float32, 'w_o': _jnp.float32, 'ln1_g': _jnp.float32, 'ln1_b': _jnp.float32, 'w_ffn_in': _jnp.float32, 'w_ffn_out': _jnp.float32, 'ln2_g': _jnp.float32, 'ln2_b': _jnp.float32}
MOMENT_SCALE = {'w_ada': 7.702271e-03, 'b_ada': 1.265092e-02, 'w_in': 3.067973e-03, 'g_q_a': 8.619743e-04, 'w_q_b': 3.478129e-04, 'g_kv_a': 2.502186e-03, 'w_kv_b': 7.578144e-04, 'w_o_a': 1.847214e-03, 'w_conv': 4.094223e-03, 'w_o_b': 6.698073e-03, 'w_o': 6.973318e-03, 'ln1_g': 2.928238e-01, 'ln1_b': 1.401769e-01, 'w_ffn_in': 4.101233e-03, 'w_ffn_out': 1.124803e-02, 'ln2_g': 8.007227e+00, 'ln2_b': 2.047889e-01}


def _to_microbatches(a, axis):
    t = _jnp.moveaxis(a, axis, 0)
    t = t.reshape((N_MICROBATCH, t.shape[0] // N_MICROBATCH) + t.shape[1:])
    return _jnp.moveaxis(t, 1, axis + 1)


def setup_inputs(seed: int = 0) -> dict:
    inp = _fwd_setup_inputs(seed)
    key = _jax.random.fold_in(_jax.random.key(seed), 7919)
    shape, _ = _output_shape()
    out = dict(inp)
    out["loss_target"] = _jax.random.normal(_jax.random.fold_in(key, 0), shape, _jnp.float32)
    for i, name in enumerate(TWIN_WEIGHTS):
        w = inp[name].astype(_jnp.float32)
        if MOMENT_SCALE is None:
            s = _jnp.sqrt(_jnp.mean(_jnp.square(w)) + 1e-30)
        else:
            s = MOMENT_SCALE[name]
        km, kv = _jax.random.split(_jax.random.fold_in(key, i + 1))
        out[name] = w
        out["m_" + name] = s * _jax.random.normal(km, w.shape, _jnp.float32)
        out["v_" + name] = (s * s) * _jax.random.uniform(kv, w.shape, _jnp.float32, 0.5, 1.5)
    if N_MICROBATCH > 1:
        for name, axis in PER_EXAMPLE_BATCH_AXIS.items():
            out[name] = _to_microbatches(out[name], axis)
    return {'x': out['x'], 'c': out['c'], 'positions': out['positions'], 'w_ada': out['w_ada'], 'b_ada': out['b_ada'], 'w_in': out['w_in'], 'g_q_a': out['g_q_a'], 'w_q_b': out['w_q_b'], 'g_kv_a': out['g_kv_a'], 'w_kv_b': out['w_kv_b'], 'w_o_a': out['w_o_a'], 'w_conv': out['w_conv'], 'w_o_b': out['w_o_b'], 'w_o': out['w_o'], 'ln1_g': out['ln1_g'], 'ln1_b': out['ln1_b'], 'w_ffn_in': out['w_ffn_in'], 'w_ffn_out': out['w_ffn_out'], 'ln2_g': out['ln2_g'], 'ln2_b': out['ln2_b'], 'loss_target': out['loss_target'], 'm_w_ada': out['m_w_ada'], 'm_b_ada': out['m_b_ada'], 'm_w_in': out['m_w_in'], 'm_g_q_a': out['m_g_q_a'], 'm_w_q_b': out['m_w_q_b'], 'm_g_kv_a': out['m_g_kv_a'], 'm_w_kv_b': out['m_w_kv_b'], 'm_w_o_a': out['m_w_o_a'], 'm_w_conv': out['m_w_conv'], 'm_w_o_b': out['m_w_o_b'], 'm_w_o': out['m_w_o'], 'm_ln1_g': out['m_ln1_g'], 'm_ln1_b': out['m_ln1_b'], 'm_w_ffn_in': out['m_w_ffn_in'], 'm_w_ffn_out': out['m_w_ffn_out'], 'm_ln2_g': out['m_ln2_g'], 'm_ln2_b': out['m_ln2_b'], 'v_w_ada': out['v_w_ada'], 'v_b_ada': out['v_b_ada'], 'v_w_in': out['v_w_in'], 'v_g_q_a': out['v_g_q_a'], 'v_w_q_b': out['v_w_q_b'], 'v_g_kv_a': out['v_g_kv_a'], 'v_w_kv_b': out['v_w_kv_b'], 'v_w_o_a': out['v_w_o_a'], 'v_w_conv': out['v_w_conv'], 'v_w_o_b': out['v_w_o_b'], 'v_w_o': out['v_w_o'], 'v_ln1_g': out['v_ln1_g'], 'v_ln1_b': out['v_ln1_b'], 'v_w_ffn_in': out['v_w_ffn_in'], 'v_w_ffn_out': out['v_w_ffn_out'], 'v_ln2_g': out['v_ln2_g'], 'v_ln2_b': out['v_ln2_b']}


def _loss(weights, diff, rest, loss_target):
    with _jax.named_scope("forward"):
        args = {**rest, TWIN_DIFF_INPUT: diff, **{k: w.astype(_WEIGHT_DTYPES[k]) for k, w in weights.items()}}
        y = _forward(args)
    with _jax.named_scope("loss_head"):
        err = _jnp.square(y.astype(_jnp.float32) - loss_target)
        return 0.5 * _jnp.sum(_jnp.mean(err, axis=-1)) if err.ndim else 0.5 * err


def _adamw(w, g, m, v):
    m = ADAM_B1 * m + (1.0 - ADAM_B1) * g
    v = ADAM_B2 * v + (1.0 - ADAM_B2) * _jnp.square(g)
    m_hat = m / (1.0 - ADAM_B1 ** ADAM_STEP)
    v_hat = v / (1.0 - ADAM_B2 ** ADAM_STEP)
    delta = -ADAM_LR * (m_hat / (_jnp.sqrt(v_hat) + ADAM_EPS) + ADAM_WD * w)
    return delta, m, v


def reference(x, c, positions, w_ada, b_ada, w_in, g_q_a, w_q_b, g_kv_a, w_kv_b, w_o_a, w_conv, w_o_b, w_o, ln1_g, ln1_b, w_ffn_in, w_ffn_out, ln2_g, ln2_b, loss_target, m_w_ada, m_b_ada, m_w_in, m_g_q_a, m_w_q_b, m_g_kv_a, m_w_kv_b, m_w_o_a, m_w_conv, m_w_o_b, m_w_o, m_ln1_g, m_ln1_b, m_w_ffn_in, m_w_ffn_out, m_ln2_g, m_ln2_b, v_w_ada, v_b_ada, v_w_in, v_g_q_a, v_w_q_b, v_g_kv_a, v_w_kv_b, v_w_o_a, v_w_conv, v_w_o_b, v_w_o, v_ln1_g, v_ln1_b, v_w_ffn_in, v_w_ffn_out, v_ln2_g, v_ln2_b):
    given = dict(x=x, c=c, positions=positions, w_ada=w_ada, b_ada=b_ada, w_in=w_in, g_q_a=g_q_a, w_q_b=w_q_b, g_kv_a=g_kv_a, w_kv_b=w_kv_b, w_o_a=w_o_a, w_conv=w_conv, w_o_b=w_o_b, w_o=w_o, ln1_g=ln1_g, ln1_b=ln1_b, w_ffn_in=w_ffn_in, w_ffn_out=w_ffn_out, ln2_g=ln2_g, ln2_b=ln2_b, loss_target=loss_target, m_w_ada=m_w_ada, m_b_ada=m_b_ada, m_w_in=m_w_in, m_g_q_a=m_g_q_a, m_w_q_b=m_w_q_b, m_g_kv_a=m_g_kv_a, m_w_kv_b=m_w_kv_b, m_w_o_a=m_w_o_a, m_w_conv=m_w_conv, m_w_o_b=m_w_o_b, m_w_o=m_w_o, m_ln1_g=m_ln1_g, m_ln1_b=m_ln1_b, m_w_ffn_in=m_w_ffn_in, m_w_ffn_out=m_w_ffn_out, m_ln2_g=m_ln2_g, m_ln2_b=m_ln2_b, v_w_ada=v_w_ada, v_b_ada=v_b_ada, v_w_in=v_w_in, v_g_q_a=v_g_q_a, v_w_q_b=v_w_q_b, v_g_kv_a=v_g_kv_a, v_w_kv_b=v_w_kv_b, v_w_o_a=v_w_o_a, v_w_conv=v_w_conv, v_w_o_b=v_w_o_b, v_w_o=v_w_o, v_ln1_g=v_ln1_g, v_ln1_b=v_ln1_b, v_w_ffn_in=v_w_ffn_in, v_w_ffn_out=v_w_ffn_out, v_ln2_g=v_ln2_g, v_ln2_b=v_ln2_b)
    weights = {n: given[n] for n in TWIN_WEIGHTS}
    shared = {n: given[n] for n in SHARED_INPUTS}
    per_example = {n: given[n] for n in ['x', 'c', 'positions']}
    grad_fn = _jax.value_and_grad(_loss, argnums=(0, 1))

    def one_microbatch(ex, loss_target):
        ex = dict(ex)
        diff = ex.pop(TWIN_DIFF_INPUT)
        return grad_fn(weights, diff, {**shared, **ex}, loss_target)

    if N_MICROBATCH == 1:
        loss, (grad_w, grad_x) = one_microbatch(per_example, given["loss_target"])
    else:
        def body(carry, xs):
            loss_sum, grad_sum = carry
            l_k, (gw_k, gx_k) = one_microbatch(xs[0], xs[1])
            with _jax.named_scope("update"):
                return (loss_sum + l_k, _jax.tree.map(_jnp.add, grad_sum, gw_k)), gx_k

        init = (_jnp.zeros((), _jnp.float32), _jax.tree.map(_jnp.zeros_like, weights))
        (loss, grad_w), grad_x = _jax.lax.scan(body, init, (per_example, given["loss_target"]))
    with _jax.named_scope("update"):
        delta_w, new_m, new_v = {}, {}, {}
        for n in TWIN_WEIGHTS:
            delta_w[n], new_m[n], new_v[n] = _adamw(weights[n], grad_w[n], given["m_" + n], given["v_" + n])
    return (loss, grad_x, *[grad_w[n] for n in TWIN_WEIGHTS], *[delta_w[n] for n in TWIN_WEIGHTS],
            *[new_m[n] for n in TWIN_WEIGHTS], *[new_v[n] for n in TWIN_WEIGHTS])
```

```python
import functools

import jax
import jax.numpy as jnp
from jax import lax
from jax.experimental import pallas as pl
from jax.experimental.pallas import tpu as pltpu

F32, BF16 = jnp.float32, jnp.bfloat16
N_HEADS, QK_NOPE, QK_ROPE, V_HEAD = 16, 128, 64, 128
Q_LORA, KV_LORA = 512, 512
QK_PAD = 256
QKV_A = 1152
CHUNK_SHIFT = 6
ATTN_SCALE = (QK_NOPE + QK_ROPE) ** -0.5
ROPE_THETA = 10000.0
ALPHA = 2.0 ** 0.25
LN_EPS, RMS_EPS = 1e-5, 1e-6
ADAM_LR, ADAM_B1, ADAM_B2, ADAM_EPS, ADAM_WD, ADAM_STEP = 0.001, 0.9, 0.999, 1e-08, 0.01, 10
ADAM_C1 = 1.0 - ADAM_B1 ** ADAM_STEP
ADAM_C2 = 1.0 - ADAM_B2 ** ADAM_STEP
VMEM_LIMIT = 56 * 1024 * 1024
MESH = pl.DeviceIdType.MESH
ANY = pl.BlockSpec(memory_space=pl.ANY)
NT = (((1,), (1,)), ((), ()))
TN = (((0,), (0,)), ((), ()))
NN = (((1,), (0,)), ((), ()))


def _params(sem=None):
    return pltpu.CompilerParams(dimension_semantics=sem, vmem_limit_bytes=VMEM_LIMIT)


def _pick(n, cands=(1024, 512, 384, 256, 128)):
    for t in cands:
        if n % t == 0:
            return t
    return n


def _row_tile(rows, row_bytes, budget, mult=8):
    best = mult
    for t in range(mult, rows + 1, mult):
        if rows % t == 0 and t * row_bytes <= budget:
            best = t
    return best


def _sigmoid(x):
    return jax.nn.sigmoid(x)


def _matmul(a, b, mode, out_dtype, name, add=None):
    if mode == "nn":
        (M, K), N, dims = a.shape, b.shape[1], NN
    elif mode == "nt":
        (M, K), N, dims = a.shape, b.shape[0], NT
    else:
        (K, M), N, dims = a.shape, b.shape[1], TN
    tm, tn, tk = _pick(M), _pick(N), _pick(K)
    nk = K // tk
    a_spec = (pl.BlockSpec((tk, tm), lambda i, j, k: (k, i)) if mode == "tn"
              else pl.BlockSpec((tm, tk), lambda i, j, k: (i, k)))
    b_spec = (pl.BlockSpec((tn, tk), lambda i, j, k: (j, k)) if mode == "nt"
              else pl.BlockSpec((tk, tn), lambda i, j, k: (k, j)))
    o_spec = pl.BlockSpec((tm, tn), lambda i, j, k: (i, j))
    has_add = add is not None

    def body(*refs):
        a_ref, b_ref = refs[0], refs[1]
        add_ref = refs[2] if has_add else None
        o_ref, acc_ref = refs[-2], refs[-1]
        k = pl.program_id(2)

        @pl.when(k == 0)
        def _():
            acc_ref[...] = jnp.zeros_like(acc_ref)

        acc_ref[...] += lax.dot_general(a_ref[...], b_ref[...], dims, preferred_element_type=F32)

        @pl.when(k == nk - 1)
        def _():
            r = acc_ref[...]
            if has_add:
                r = r + add_ref[...]
            o_ref[...] = r.astype(o_ref.dtype)

    ins = [a, b] + ([add] if has_add else [])
    in_specs = [a_spec, b_spec] + ([o_spec] if has_add else [])
    return pl.pallas_call(
        body, name=name, grid=(M // tm, N // tn, nk),
        in_specs=in_specs, out_specs=o_spec,
        out_shape=jax.ShapeDtypeStruct((M, N), out_dtype),
        scratch_shapes=[pltpu.VMEM((tm, tn), F32)],
        compiler_params=_params(("parallel", "parallel", "arbitrary")),
    )(*ins)


def _rows(body, name, n_rows, tm, ins, outs, accs=()):
    grid = (n_rows // tm,)
    per8 = tm // 8
    last8 = n_rows // 8 - 1
    arrays, in_specs = [], []
    for spec in ins:
        kind, arr = spec[0], spec[1]
        arrays.append(arr)
        if kind == "row":
            _, _, cb, w = spec
            in_specs.append(pl.BlockSpec((tm, w), lambda i, cb=cb: (i, cb)))
        elif kind == "full":
            in_specs.append(pl.BlockSpec(arr.shape, lambda i, nd=arr.ndim: (0,) * nd))
        elif kind == "prev":
            _, _, cb, w = spec
            in_specs.append(pl.BlockSpec((8, w), lambda i, cb=cb: (jnp.maximum(i * per8 - 1, 0), cb)))
        else:
            _, _, cb, w = spec
            in_specs.append(pl.BlockSpec((8, w), lambda i, cb=cb: (jnp.minimum((i + 1) * per8, last8), cb)))
    out_shape = [jax.ShapeDtypeStruct((n_rows, w), dt) for (w, dt) in outs]
    out_specs = [pl.BlockSpec((tm, w), lambda i: (i, 0)) for (w, _) in outs]
    out_shape += [jax.ShapeDtypeStruct(s, F32) for s in accs]
    out_specs += [pl.BlockSpec(s, lambda i, nd=len(s): (0,) * nd) for s in accs]
    n_in, n_out = len(ins), len(outs)

    def kernel_body(*refs):
        body(pl.program_id(0), refs[:n_in], refs[n_in:n_in + n_out], refs[n_in + n_out:])

    res = pl.pallas_call(
        kernel_body, name=name, grid=grid, in_specs=in_specs, out_specs=out_specs, out_shape=out_shape,
        compiler_params=_params(("arbitrary",)),
    )(*arrays)
    return res


def _acc_add(i, ref, val):
    @pl.when(i == 0)
    def _():
        ref[...] = val

    @pl.when(i > 0)
    def _():
        ref[...] += val


def _rope(t, tab, sign):
    c, sa, sb = tab[:, 0:128], tab[:, 128:256], tab[:, 256:384]
    rot = pltpu.roll(t, 96, 1) * sa + pltpu.roll(t, 32, 1) * sb
    return t * c + rot if sign > 0 else t * c - rot


def _ln_stats(r):
    mu = jnp.mean(r, axis=-1, keepdims=True)
    d = r - mu
    var = jnp.mean(d * d, axis=-1, keepdims=True)
    rstd = lax.rsqrt(var + LN_EPS)
    return d * rstd, rstd


def _ln_bwd(dxh, xh, rstd):
    m1 = jnp.mean(dxh, axis=-1, keepdims=True)
    m2 = jnp.mean(dxh * xh, axis=-1, keepdims=True)
    return rstd * (dxh - m1 - xh * m2)


def _modulate(x, scale, shift, name):
    S, D = x.shape

    def body(i, ins, outs, accs):
        outs[0][...] = (ins[0][...] * (1.0 + ins[1][...]) + ins[2][...]).astype(BF16)

    return _rows(body, name, S, _pick(S, (256, 128)), [("row", x, 0, D), ("full", scale), ("full", shift)], [(D, BF16)])[0]


def _rms_fwd(pq, tab, g_q, g_kv):
    S = pq.shape[0]

    def body(i, ins, outs, accs):
        pq_ref, tab_ref, gq_ref, gkv_ref = ins

        def rms(x, g):
            return x * lax.rsqrt(jnp.mean(x * x, axis=-1, keepdims=True) + RMS_EPS) * g

        outs[0][...] = rms(pq_ref[:, 0:Q_LORA], gq_ref[...]).astype(BF16)
        outs[1][...] = rms(pq_ref[:, Q_LORA:Q_LORA + KV_LORA], gkv_ref[...]).astype(BF16)
        outs[2][...] = _rope(pq_ref[:, Q_LORA + KV_LORA:QKV_A], tab_ref[...], 1).astype(BF16)

    return _rows(body, "rms_fwd", S, _pick(S, (256, 128)),
                 [("row", pq, 0, QKV_A), ("row", tab, 0, 384), ("full", g_q), ("full", g_kv)],
                 [(Q_LORA, BF16), (KV_LORA, BF16), (128, BF16)])


def _q_rope(q, tab):
    S, W = q.shape

    def body(i, ins, outs, accs):
        q_ref, tab_ref = ins
        t = tab_ref[...]
        for h in range(N_HEADS):
            lo = h * QK_PAD
            outs[0][:, lo:lo + 128] = q_ref[:, lo:lo + 128].astype(BF16)
            outs[0][:, lo + 128:lo + 256] = _rope(q_ref[:, lo + 128:lo + 256], t, 1).astype(BF16)

    return _rows(body, "q_rope", S, _pick(S, (256, 128)), [("row", q, 0, W), ("row", tab, 0, 384)], [(W, BF16)])[0]


def _allowed(q0, k0, bq):
    row = q0 + lax.broadcasted_iota(jnp.int32, (bq, bq), 0)
    col = k0 + lax.broadcasted_iota(jnp.int32, (bq, bq), 1)
    return (col >> CHUNK_SHIFT) <= (row >> CHUNK_SHIFT)


def _attn_fwd(q, kv, kr):
    S = q.shape[0]
    bq = min(256, S)
    nq = S // bq

    def body(q_ref, kn_ref, v_ref, kr_ref, o_ref, lse_ref):
        qi = pl.program_id(1)
        qv = q_ref[...]

        def step(j, carry):
            m, l, acc = carry
            off = pl.multiple_of(j * bq, bq)
            k = jnp.concatenate([kn_ref[pl.ds(off, bq), :], kr_ref[pl.ds(off, bq), :]], axis=1)
            s = lax.dot_general(qv, k, NT, preferred_element_type=F32) * ATTN_SCALE
            s = jnp.where(_allowed(qi * bq, off, bq), s, -1e30)
            m_new = jnp.maximum(m, jnp.max(s, axis=1, keepdims=True))
            a = jnp.exp(m - m_new)
            p = jnp.exp(s - m_new)
            l = a * l + jnp.sum(p, axis=1, keepdims=True)
            acc = a * acc + jnp.dot(p.astype(BF16), v_ref[pl.ds(off, bq), :], preferred_element_type=F32)
            return m_new, l, acc

        init = (jnp.full((bq, 1), -1e30, F32), jnp.zeros((bq, 1), F32), jnp.zeros((bq, V_HEAD), F32))
        m, l, acc = lax.fori_loop(0, qi + 1, step, init)
        o_ref[...] = (acc / l).astype(BF16)
        lse_ref[0] = m + jnp.log(l)

    return pl.pallas_call(
        body, name="attn_fwd", grid=(N_HEADS, nq),
        in_specs=[pl.BlockSpec((bq, QK_PAD), lambda h, i: (i, h)),
                  pl.BlockSpec((S, 128), lambda h, i: (0, 2 * h)),
                  pl.BlockSpec((S, 128), lambda h, i: (0, 2 * h + 1)),
                  pl.BlockSpec((S, 128), lambda h, i: (0, 0))],
        out_specs=[pl.BlockSpec((bq, V_HEAD), lambda h, i: (i, h)),
                   pl.BlockSpec((1, bq, 1), lambda h, i: (h, i, 0))],
        out_shape=[jax.ShapeDtypeStruct((S, N_HEADS * V_HEAD), BF16),
                   jax.ShapeDtypeStruct((N_HEADS, S, 1), F32)],
        compiler_params=_params(("arbitrary", "arbitrary")),
    )(q, kv, kv, kr)


def _attn_bwd(q, kv, kr, do, o, lse, tab):
    S = q.shape[0]
    bq = min(256, S)
    nq = S // bq

    def body(q_ref, kn_ref, v_ref, kr_ref, do_ref, o_ref, lse_ref, tab_ref, dq_ref, dkv_ref, dkr_ref,
             dq_acc, dk_acc, dv_acc):
        h = pl.program_id(0)
        dq_acc[...] = jnp.zeros_like(dq_acc)
        dk_acc[...] = jnp.zeros_like(dk_acc)
        dv_acc[...] = jnp.zeros_like(dv_acc)

        def kv_step(j, _):
            offj = pl.multiple_of(j * bq, bq)
            rows_j = pl.ds(offj, bq)
            k = jnp.concatenate([kn_ref[rows_j, :], kr_ref[rows_j, :]], axis=1)
            v = v_ref[rows_j, :]

            def q_step(i, _):
                offi = pl.multiple_of(i * bq, bq)
                rows_i = pl.ds(offi, bq)
                qv, dov = q_ref[rows_i, :], do_ref[rows_i, :]
                delta = jnp.sum(dov.astype(F32) * o_ref[rows_i, :].astype(F32), axis=1, keepdims=True)
                s = lax.dot_general(qv, k, NT, preferred_element_type=F32) * ATTN_SCALE
                s = jnp.where(_allowed(offi, offj, bq), s, -1e30)
                p = jnp.exp(s - lse_ref[0, rows_i, :])
                dv_acc[rows_j, :] += lax.dot_general(p.astype(BF16), dov, TN, preferred_element_type=F32)
                dp = lax.dot_general(dov, v, NT, preferred_element_type=F32)
                ds = (p * (dp - delta) * ATTN_SCALE).astype(BF16)
                dk_acc[rows_j, :] += lax.dot_general(ds, qv, TN, preferred_element_type=F32)
                dq_acc[rows_i, :] += jnp.dot(ds, k, preferred_element_type=F32)
                return 0

            lax.fori_loop(j, nq, q_step, 0)
            return 0

        lax.fori_loop(0, nq, kv_step, 0)

        for r in range(nq):
            rows = slice(r * bq, (r + 1) * bq)
            dq_ref[rows, 0:128] = dq_acc[rows, 0:128].astype(BF16)
            dq_ref[rows, 128:256] = _rope(dq_acc[rows, 128:256], tab_ref[rows, :], -1).astype(BF16)
        dkv_ref[:, 0:128] = dk_acc[:, 0:128].astype(BF16)
        dkv_ref[:, 128:256] = dv_acc[...].astype(BF16)

        @pl.when(h == 0)
        def _():
            dkr_ref[...] = dk_acc[:, 128:256]

        @pl.when(h > 0)
        def _():
            dkr_ref[...] += dk_acc[:, 128:256]

        @pl.when(h == N_HEADS - 1)
        def _():
            for r in range(nq):
                rows = slice(r * bq, (r + 1) * bq)
                dkr_ref[rows, :] = _rope(dkr_ref[rows, :], tab_ref[rows, :], -1)

    W = N_HEADS * QK_PAD
    return pl.pallas_call(
        body, name="attn_bwd", grid=(N_HEADS,),
        in_specs=[pl.BlockSpec((S, QK_PAD), lambda h: (0, h)),
                  pl.BlockSpec((S, 128), lambda h: (0, 2 * h)),
                  pl.BlockSpec((S, 128), lambda h: (0, 2 * h + 1)),
                  pl.BlockSpec((S, 128), lambda h: (0, 0)),
                  pl.BlockSpec((S, V_HEAD), lambda h: (0, h)),
                  pl.BlockSpec((S, V_HEAD), lambda h: (0, h)),
                  pl.BlockSpec((1, S, 1), lambda h: (h, 0, 0)),
                  pl.BlockSpec((S, 384), lambda h: (0, 0))],
        out_specs=[pl.BlockSpec((S, QK_PAD), lambda h: (0, h)),
                   pl.BlockSpec((S, QK_PAD), lambda h: (0, h)),
                   pl.BlockSpec((S, 128), lambda h: (0, 0))],
        out_shape=[jax.ShapeDtypeStruct((S, W), BF16), jax.ShapeDtypeStruct((S, W), BF16),
                   jax.ShapeDtypeStruct((S, 128), F32)],
        scratch_shapes=[pltpu.VMEM((S, QK_PAD), F32), pltpu.VMEM((S, QK_PAD), F32), pltpu.VMEM((S, V_HEAD), F32)],
        compiler_params=_params(("arbitrary",)),
    )(q, kv, kv, kr, do, o, lse, tab)


def _shift_down(cur, prev8, i, n):
    tm = cur.shape[0]
    prev8 = jnp.where(i == 0, jnp.zeros_like(prev8), prev8)
    full = jnp.concatenate([prev8, cur], axis=0)
    return pltpu.roll(full, n, 0)[8:8 + tm, :]


def _shift_up(cur, next8, i, last, n):
    tm = cur.shape[0]
    next8 = jnp.where(i == last, jnp.zeros_like(next8), next8)
    full = jnp.concatenate([cur, next8], axis=0)
    return pltpu.roll(full, tm + 8 - n, 0)[0:tm, :]


def _conv_fwd(pc, w_conv):
    S, D = pc.shape[0], pc.shape[1] // 3
    tm = _pick(S, (256, 128))

    def body(i, ins, outs, accs):
        b_ref, c_ref, x_ref, cp_ref, xp_ref, w_ref = ins
        z = c_ref[...] * x_ref[...]
        zp = cp_ref[...] * xp_ref[...]
        cz = w_ref[0:1, :] * _shift_down(z, zp, i, 2) + w_ref[1:2, :] * _shift_down(z, zp, i, 1) + w_ref[2:3, :] * z
        outs[0][...] = (b_ref[...] * cz).astype(BF16)

    return _rows(body, "conv_fwd", S, tm,
                 [("row", pc, 0, D), ("row", pc, 1, D), ("row", pc, 2, D), ("prev", pc, 1, D), ("prev", pc, 2, D),
                  ("full", w_conv)], [(D, BF16)])[0]


def _conv_bwd(dhb, pc, w_conv):
    S, D = dhb.shape
    tm = _pick(S, (256, 128))
    last = S // tm - 1

    def body(i, ins, outs, accs):
        g_ref, b_ref, c_ref, x_ref, cp_ref, xp_ref, gn_ref, bn_ref, w_ref = ins
        w0, w1, w2 = w_ref[0:1, :], w_ref[1:2, :], w_ref[2:3, :]
        c, x, g = c_ref[...], x_ref[...], g_ref[...]
        z = c * x
        zp = cp_ref[...] * xp_ref[...]
        z1, z2 = _shift_down(z, zp, i, 1), _shift_down(z, zp, i, 2)
        cz = w0 * z2 + w1 * z1 + w2 * z
        dcz = g * b_ref[...]
        dczn = gn_ref[...] * bn_ref[...]
        dz = w2 * dcz + w1 * _shift_up(dcz, dczn, i, last, 1) + w0 * _shift_up(dcz, dczn, i, last, 2)
        outs[0][:, 0:D] = (g * cz).astype(BF16)
        outs[0][:, D:2 * D] = (dz * x).astype(BF16)
        outs[0][:, 2 * D:3 * D] = (dz * c).astype(BF16)
        dw = jnp.concatenate([jnp.sum(dcz * z2, axis=0, keepdims=True), jnp.sum(dcz * z1, axis=0, keepdims=True),
                              jnp.sum(dcz * z, axis=0, keepdims=True)], axis=0)
        _acc_add(i, accs[0], dw)

    return _rows(body, "conv_bwd", S, tm,
                 [("row", dhb, 0, D), ("row", pc, 0, D), ("row", pc, 1, D), ("row", pc, 2, D),
                  ("prev", pc, 1, D), ("prev", pc, 2, D), ("next", dhb, 0, D), ("next", pc, 0, D), ("full", w_conv)],
                 [(3 * D, BF16)], [(3, D)])


def _merge_fwd(y_a, y_b, pg):
    S, D = y_a.shape

    def body(i, ins, outs, accs):
        ya, yb, ga, gb = ins
        outs[0][...] = (_sigmoid(ga[...]) * ya[...] + _sigmoid(gb[...]) * yb[...]).astype(BF16)

    return _rows(body, "merge_fwd", S, _pick(S, (256, 128)),
                 [("row", y_a, 0, D), ("row", y_b, 0, D), ("row", pg, 0, D), ("row", pg, 1, D)], [(D, BF16)])[0]


def _merge_bwd(dm, y_a, y_b, pg):
    S, D = dm.shape

    def body(i, ins, outs, accs):
        d, ya, yb = ins[0][...], ins[1][...], ins[2][...]
        sa, sb = _sigmoid(ins[3][...]), _sigmoid(ins[4][...])
        outs[0][...] = (d * sa).astype(BF16)
        outs[1][...] = (d * sb).astype(BF16)
        outs[2][:, 0:D] = (d * ya * (sa * (1.0 - sa))).astype(BF16)
        outs[2][:, D:2 * D] = (d * yb * (sb * (1.0 - sb))).astype(BF16)

    return _rows(body, "merge_bwd", S, _pick(S, (256, 128)),
                 [("row", dm, 0, D), ("row", y_a, 0, D), ("row", y_b, 0, D), ("row", pg, 0, D), ("row", pg, 1, D)],
                 [(D, BF16), (D, BF16), (2 * D, BF16)])


def _ln1_fwd(x, mix, gate1, g, b, scale2, shift2):
    S, D = x.shape

    def body(i, ins, outs, accs):
        x_ref, mix_ref, gate_ref, g_ref, b_ref, sc_ref, sh_ref = ins
        xh, _ = _ln_stats(ALPHA * x_ref[...] + gate_ref[...] * mix_ref[...])
        x1 = xh * g_ref[...] + b_ref[...]
        outs[0][...] = x1
        outs[1][...] = (x1 * (1.0 + sc_ref[...]) + sh_ref[...]).astype(BF16)

    return _rows(body, "ln1_fwd", S, _pick(S, (256, 128)),
                 [("row", x, 0, D), ("row", mix, 0, D), ("full", gate1), ("full", g), ("full", b),
                  ("full", scale2), ("full", shift2)], [(D, F32), (D, BF16)])


def _swiglu_fwd(hh):
    S, F = hh.shape[0], hh.shape[1] // 2

    def body(i, ins, outs, accs):
        hg = ins[0][...]
        outs[0][...] = (hg * _sigmoid(hg) * ins[1][...]).astype(BF16)

    return _rows(body, "swiglu_fwd", S, _pick(S, (128,)), [("row", hh, 0, F), ("row", hh, 1, F)], [(F, BF16)])[0]


def _swiglu_bwd(dact, hh):
    S, F = dact.shape

    def body(i, ins, outs, accs):
        d, hg, hu = ins[0][...], ins[1][...], ins[2][...]
        sg = _sigmoid(hg)
        outs[0][:, 0:F] = (d * hu * (sg * (1.0 + hg * (1.0 - sg)))).astype(BF16)
        outs[0][:, F:2 * F] = (d * (hg * sg)).astype(BF16)

    return _rows(body, "swiglu_bwd", S, _pick(S, (128,)),
                 [("row", dact, 0, F), ("row", hh, 0, F), ("row", hh, 1, F)], [(2 * F, BF16)])[0]


def _ln2_loss_bwd(x1, ffn, gate2, g, b, target):
    S, D = x1.shape

    def body(i, ins, outs, accs):
        x1_ref, f_ref, gate_ref, g_ref, b_ref, t_ref = ins
        f = f_ref[...]
        xh, rstd = _ln_stats(ALPHA * x1_ref[...] + gate_ref[...] * f)
        e = xh * g_ref[...] + b_ref[...] - t_ref[...]
        dy = e * (1.0 / D)
        dr = _ln_bwd(dy * g_ref[...], xh, rstd)
        outs[0][...] = (gate_ref[...] * dr).astype(BF16)
        outs[1][...] = ALPHA * dr
        _acc_add(i, accs[0], jnp.full((1, 128), (0.5 / D) * jnp.sum(e * e), F32))
        _acc_add(i, accs[1], jnp.sum(dy * xh, axis=0, keepdims=True))
        _acc_add(i, accs[2], jnp.sum(dy, axis=0, keepdims=True))
        _acc_add(i, accs[3], jnp.sum(dr * f, axis=0, keepdims=True))

    return _rows(body, "ln2_loss_bwd", S, _pick(S, (256, 128)),
                 [("row", x1, 0, D), ("row", ffn, 0, D), ("full", gate2), ("full", g), ("full", b), ("row", target, 0, D)],
                 [(D, BF16), (D, F32)], [(1, 128), (1, D), (1, D), (1, D)])


def _ln1_bwd(x, mix, dx1a, du2, gate1, g, b, scale2):
    S, D = x.shape

    def body(i, ins, outs, accs):
        x_ref, mix_ref, da_ref, du_ref, gate_ref, g_ref, b_ref, sc_ref = ins
        mix, du = mix_ref[...], du_ref[...]
        xh, rstd = _ln_stats(ALPHA * x_ref[...] + gate_ref[...] * mix)
        x1 = xh * g_ref[...] + b_ref[...]
        dx1 = da_ref[...] + du * (1.0 + sc_ref[...])
        dr = _ln_bwd(dx1 * g_ref[...], xh, rstd)
        outs[0][...] = (gate_ref[...] * dr).astype(BF16)
        outs[1][...] = ALPHA * dr
        _acc_add(i, accs[0], jnp.sum(du, axis=0, keepdims=True))
        _acc_add(i, accs[1], jnp.sum(du * x1, axis=0, keepdims=True))
        _acc_add(i, accs[2], jnp.sum(dx1 * xh, axis=0, keepdims=True))
        _acc_add(i, accs[3], jnp.sum(dx1, axis=0, keepdims=True))
        _acc_add(i, accs[4], jnp.sum(dr * mix, axis=0, keepdims=True))

    return _rows(body, "ln1_bwd", S, _pick(S, (256, 128)),
                 [("row", x, 0, D), ("row", mix, 0, D), ("row", dx1a, 0, D), ("row", du2, 0, D),
                  ("full", gate1), ("full", g), ("full", b), ("full", scale2)],
                 [(D, BF16), (D, F32)], [(1, D)] * 5)


def _rms_bwd(d_rq, d_rkv, pq, dkr, g_q, g_kv):
    S = pq.shape[0]

    def body(i, ins, outs, accs):
        dq_ref, dkv_ref, pq_ref, dkr_ref, gq_ref, gkv_ref = ins

        def rms_bwd(dy, x, g):
            r = lax.rsqrt(jnp.mean(x * x, axis=-1, keepdims=True) + RMS_EPS)
            dyg = dy * g
            dx = r * dyg - x * (r * r * r) * jnp.mean(dyg * x, axis=-1, keepdims=True)
            return dx, jnp.sum(dy * (x * r), axis=0, keepdims=True)

        dxq, dgq = rms_bwd(dq_ref[...], pq_ref[:, 0:Q_LORA], gq_ref[...])
        dxkv, dgkv = rms_bwd(dkv_ref[...], pq_ref[:, Q_LORA:Q_LORA + KV_LORA], gkv_ref[...])
        outs[0][:, 0:Q_LORA] = dxq.astype(BF16)
        outs[0][:, Q_LORA:Q_LORA + KV_LORA] = dxkv.astype(BF16)
        outs[0][:, Q_LORA + KV_LORA:QKV_A] = dkr_ref[...].astype(BF16)
        _acc_add(i, accs[0], dgq)
        _acc_add(i, accs[1], dgkv)

    return _rows(body, "rms_bwd", S, _pick(S, (256, 128)),
                 [("row", d_rq, 0, Q_LORA), ("row", d_rkv, 0, KV_LORA), ("row", pq, 0, QKV_A), ("row", dkr, 0, 128),
                  ("full", g_q), ("full", g_kv)], [(QKV_A, BF16)], [(1, Q_LORA), (1, KV_LORA)])


def _dx_final(dxa, du, x, scale1):
    S, D = x.shape

    def body(i, ins, outs, accs):
        du = ins[1][...]
        outs[0][...] = ins[0][...] + du * (1.0 + ins[3][...])
        _acc_add(i, accs[0], jnp.sum(du, axis=0, keepdims=True))
        _acc_add(i, accs[1], jnp.sum(du * ins[2][...], axis=0, keepdims=True))

    return _rows(body, "dx_final", S, _pick(S, (256, 128)),
                 [("row", dxa, 0, D), ("row", du, 0, D), ("row", x, 0, D), ("full", scale1)],
                 [(D, F32)], [(1, D), (1, D)])


def _ada_fwd(c_all, w, bias):
    B, D = c_all.shape
    NA = w.shape[1]
    tn = _pick(NA, (512, 256, 128))

    def body(c_ref, w_ref, b_ref, o_ref):
        cv = c_ref[...]
        ca = (cv * _sigmoid(cv)).astype(BF16)
        o_ref[...] = jnp.dot(ca, w_ref[...].astype(BF16), preferred_element_type=F32) + b_ref[...]

    return pl.pallas_call(
        body, name="ada_fwd", grid=(NA // tn,),
        in_specs=[pl.BlockSpec((B, D), lambda j: (0, 0)), pl.BlockSpec((D, tn), lambda j: (0, j)),
                  pl.BlockSpec((1, tn), lambda j: (0, j))],
        out_specs=pl.BlockSpec((B, tn), lambda j: (0, j)),
        out_shape=jax.ShapeDtypeStruct((B, NA), F32),
        compiler_params=_params(("arbitrary",)),
    )(c_all, w, bias)


def _ada_bwd(c_all, dmod):
    B, D = c_all.shape
    NA = dmod.shape[1]
    tn = _pick(NA, (512, 256, 128))

    def body(c_ref, d_ref, o_ref):
        cv = c_ref[...]
        ca = (cv * _sigmoid(cv)).astype(BF16)
        o_ref[...] = lax.dot_general(ca, d_ref[...].astype(BF16), TN, preferred_element_type=F32)

    return pl.pallas_call(
        body, name="ada_bwd", grid=(NA // tn,),
        in_specs=[pl.BlockSpec((B, D), lambda j: (0, 0)), pl.BlockSpec((B, tn), lambda j: (0, j))],
        out_specs=pl.BlockSpec((D, tn), lambda j: (0, j)),
        out_shape=jax.ShapeDtypeStruct((D, NA), F32),
        compiler_params=_params(("arbitrary",)),
    )(c_all, dmod)


def _sum8(parts):
    _, R, N = parts.shape

    def body(p_ref, o_ref):
        acc = p_ref[0]
        for d in range(1, 8):
            acc = acc + p_ref[d]
        o_ref[...] = acc

    return pl.pallas_call(body, name="sum8", out_shape=jax.ShapeDtypeStruct((R, N), F32),
                          compiler_params=_params())(parts)


def _adam_math(w, g, m, v):
    m = ADAM_B1 * m + (1.0 - ADAM_B1) * g
    v = ADAM_B2 * v + (1.0 - ADAM_B2) * (g * g)
    delta = -ADAM_LR * ((m / ADAM_C1) / (jnp.sqrt(v / ADAM_C2) + ADAM_EPS) + ADAM_WD * w)
    return delta, m, v


def _adam(name, w, m, v, g_parts):
    R, C = w.shape
    tm = _row_tile(R, C * 4, 1 << 20)
    n = len(g_parts)

    def body(*refs):
        w_ref, m_ref, v_ref = refs[0], refs[1], refs[2]
        g = refs[3][...]
        for r in refs[4:3 + n]:
            g = g + r[...]
        g_ref, d_ref, nm_ref, nv_ref = refs[3 + n:]
        delta, nm, nv = _adam_math(w_ref[...], g, m_ref[...], v_ref[...])
        g_ref[...] = g
        d_ref[...] = delta
        nm_ref[...] = nm
        nv_ref[...] = nv

    spec = pl.BlockSpec((tm, C), lambda i: (i, 0))
    return pl.pallas_call(
        body, name=name, grid=(R // tm,), in_specs=[spec] * (3 + n), out_specs=[spec] * 4,
        out_shape=[jax.ShapeDtypeStruct((R, C), F32)] * 4, compiler_params=_params(("parallel",)),
    )(w, m, v, *g_parts)


def _adam_small(name, w, m, v, g):
    def body(w_ref, m_ref, v_ref, g_ref, d_ref, nm_ref, nv_ref):
        delta, nm, nv = _adam_math(w_ref[...], g_ref[...], m_ref[...], v_ref[...])
        d_ref[...] = delta
        nm_ref[...] = nm
        nv_ref[...] = nv

    return pl.pallas_call(body, name=name, out_shape=[jax.ShapeDtypeStruct(w.shape, F32)] * 3,
                          compiler_params=_params())(w, m, v, g)


def _place():
    return lax.axis_index("x"), lax.axis_index("y"), lax.axis_index("c")


def _other_chips(x, y):
    return [(1 - x, y), (x, 1 - y), (1 - x, 1 - y)]


def _all_gather8(blk, name):
    R, N = blk.shape

    def body(x_ref, out_ref, send_sems, recv_sems, local_sem):
        x, y, c = _place()
        me = 4 * x + 2 * y + c
        mine = pltpu.make_async_copy(x_ref, out_ref.at[me], local_sem)
        mine.start()
        flips = [(j >> 2 & 1, j >> 1 & 1, j & 1) for j in range(1, 8)]
        peers = [((1 - x) if fx else x, (1 - y) if fy else y, (1 - c) if fc else c) for fx, fy, fc in flips]
        sends = []
        for j, peer in enumerate(peers):
            cp = pltpu.make_async_remote_copy(src_ref=x_ref, dst_ref=out_ref.at[me], send_sem=send_sems.at[j],
                                              recv_sem=recv_sems.at[j], device_id=peer, device_id_type=MESH)
            cp.start()
            sends.append(cp)
        for j, (px, py, pc) in enumerate(peers):
            pltpu.make_async_remote_copy(src_ref=x_ref, dst_ref=out_ref.at[4 * px + 2 * py + pc],
                                         send_sem=send_sems.at[j], recv_sem=recv_sems.at[j],
                                         device_id=(px, py, pc), device_id_type=MESH).wait_recv()
        for cp in sends:
            cp.wait_send()
        mine.wait()

    return pl.pallas_call(
        body, name=name, out_shape=jax.ShapeDtypeStruct((8, R, N), F32),
        in_specs=[pl.BlockSpec(memory_space=pltpu.VMEM)], out_specs=pl.BlockSpec(memory_space=pltpu.VMEM),
        scratch_shapes=[pltpu.SemaphoreType.DMA((7,)), pltpu.SemaphoreType.DMA((7,)), pltpu.SemaphoreType.DMA],
        compiler_params=_params(),
    )(blk)


def _chip_exchange(arrs, name, scatter):
    n = len(arrs)

    def body(*refs):
        ins, outs = refs[:n], refs[n:2 * n]
        send_sems, recv_sems, local_sems = refs[2 * n:]
        x, y, c = _place()
        me = 2 * x + y
        chips = _other_chips(x, y)
        started = []
        for k in range(n):
            own = ins[k].at[me] if scatter else ins[k]
            loc = pltpu.make_async_copy(own, outs[k].at[me], local_sems.at[k])
            loc.start()
            started.append(loc)
            for j, (px, py) in enumerate(chips):
                src = ins[k].at[2 * px + py] if scatter else ins[k]
                cp = pltpu.make_async_remote_copy(src_ref=src, dst_ref=outs[k].at[me], send_sem=send_sems.at[3 * k + j],
                                                  recv_sem=recv_sems.at[3 * k + j], device_id=(px, py, c), device_id_type=MESH)
                cp.start()
                started.append(cp)
        for k in range(n):
            for j, (px, py) in enumerate(chips):
                src = ins[k].at[me] if scatter else ins[k]
                pltpu.make_async_remote_copy(src_ref=src, dst_ref=outs[k].at[2 * px + py], send_sem=send_sems.at[3 * k + j],
                                             recv_sem=recv_sems.at[3 * k + j], device_id=(px, py, c),
                                             device_id_type=MESH).wait_recv()
        for k in range(n):
            started[4 * k].wait()
            for j in range(3):
                started[4 * k + 1 + j].wait_send()

    out_shape = [jax.ShapeDtypeStruct(a.shape if scatter else (4,) + a.shape, a.dtype) for a in arrs]
    return pl.pallas_call(
        body, name=name, out_shape=out_shape, in_specs=[ANY] * n, out_specs=[ANY] * n,
        scratch_shapes=[pltpu.SemaphoreType.DMA((3 * n,)), pltpu.SemaphoreType.DMA((3 * n,)), pltpu.SemaphoreType.DMA((n,))],
        compiler_params=_params(),
    )(*arrs)


def _sibling_exchange(arrs, name):
    n = len(arrs)

    def body(*refs):
        ins, outs = refs[:n], refs[n:2 * n]
        send_sems, recv_sems = refs[2 * n:]
        x, y, c = _place()
        cps = [pltpu.make_async_remote_copy(src_ref=ins[k], dst_ref=outs[k], send_sem=send_sems.at[k],
                                            recv_sem=recv_sems.at[k], device_id=(x, y, 1 - c), device_id_type=MESH)
               for k in range(n)]
        for cp in cps:
            cp.start()
        for cp in cps:
            cp.wait()

    return pl.pallas_call(
        body, name=name, out_shape=[jax.ShapeDtypeStruct(a.shape, a.dtype) for a in arrs],
        in_specs=[ANY] * n, out_specs=[ANY] * n,
        scratch_shapes=[pltpu.SemaphoreType.DMA((n,)), pltpu.SemaphoreType.DMA((n,))],
        compiler_params=_params(),
    )(*arrs)


def _sum4(parts, name):
    _, R, C = parts.shape
    tm = _row_tile(R, C * 4, 1 << 20, 16)

    def body(p_ref, o_ref):
        acc = p_ref[0].astype(F32)
        for d in range(1, 4):
            acc = acc + p_ref[d].astype(F32)
        o_ref[...] = acc

    return pl.pallas_call(
        body, name=name, grid=(R // tm,), in_specs=[pl.BlockSpec((4, tm, C), lambda i: (0, i, 0))],
        out_specs=pl.BlockSpec((tm, C), lambda i: (i, 0)), out_shape=jax.ShapeDtypeStruct((R, C), F32),
        compiler_params=_params(("parallel",)),
    )(parts)


def kernel(x, c, positions, w_ada, b_ada, w_in, g_q_a, w_q_b, g_kv_a, w_kv_b, w_o_a, w_conv, w_o_b, w_o, ln1_g, ln1_b, w_ffn_in, w_ffn_out, ln2_g, ln2_b, loss_target, m_w_ada, m_b_ada, m_w_in, m_g_q_a, m_w_q_b, m_g_kv_a, m_w_kv_b, m_w_o_a, m_w_conv, m_w_o_b, m_w_o, m_ln1_g, m_ln1_b, m_w_ffn_in, m_w_ffn_out, m_ln2_g, m_ln2_b, v_w_ada, v_b_ada, v_w_in, v_g_q_a, v_w_q_b, v_g_kv_a, v_w_kv_b, v_w_o_a, v_w_conv, v_w_o_b, v_w_o, v_ln1_g, v_ln1_b, v_w_ffn_in, v_w_ffn_out, v_ln2_g, v_ln2_b):
    S, D = x.shape[1], x.shape[2]
    F = w_ffn_out.shape[1] * 4
    ax, ay, ac = _place()
    chip = 2 * ax + ay
    dev = 4 * ax + 2 * ay + ac
    x2, tgt = x[0], loss_target[0]
    w_ada2, w_in2, w_q_b2, w_kv_b2 = w_ada[0], w_in[0], w_q_b[0], w_kv_b[0]
    w_o_a2, w_o_b2, w_o2, w_ffn_in2, w_ffn_out2 = w_o_a[0], w_o_b[0], w_o[0], w_ffn_in[0], w_ffn_out[0]
    NA = w_ada2.shape[1]
    CW = w_conv.shape[2]

    inv_freq = 1.0 / (ROPE_THETA ** (jnp.arange(0, QK_ROPE, 2, dtype=F32) / QK_ROPE))
    ang = positions[0].astype(F32)[:, None] * inv_freq
    cos, sin = jnp.cos(ang), jnp.sin(ang)
    z32, z64, z96 = jnp.zeros((S, 32), F32), jnp.zeros((S, 64), F32), jnp.zeros((S, 96), F32)
    tab = jnp.concatenate([cos, cos, z64, -sin, z96, z32, sin, z64], axis=1)

    shards = [w.astype(BF16) for w in (w_in2, w_q_b2, w_kv_b2, w_o_a2, w_o_b2, w_o2, w_ffn_in2, w_ffn_out2)]
    g_in, g_qb, g_kvb, g_oa, g_ob, g_o, g_fi, g_fo = _chip_exchange(shards, "gather_weights", scatter=False)

    def cols(g):
        return jnp.transpose(g, (1, 0, 2)).reshape(g.shape[1], 4 * g.shape[2])

    W_in = cols(g_in)
    n_qkv = Q_LORA + KV_LORA + QK_ROPE
    W_qkv = jnp.pad(W_in[:, :n_qkv], ((0, 0), (0, QKV_A - n_qkv)))
    W_conv = W_in[:, n_qkv:n_qkv + 3 * D]
    W_gate = W_in[:, n_qkv + 3 * D:]
    W_qb = jnp.pad(cols(g_qb).reshape(Q_LORA, N_HEADS, QK_NOPE + QK_ROPE),
                   ((0, 0), (0, 0), (0, QK_PAD - QK_NOPE - QK_ROPE))).reshape(Q_LORA, N_HEADS * QK_PAD)
    W_kvb = cols(g_kvb)
    W_oa, W_ob, W_o = (g.reshape(-1, D) for g in (g_oa, g_ob, g_o))
    W_fi = cols(g_fi)
    W_fo = g_fo.reshape(F, D)

    c_all = _all_gather8(c, "gather_c").reshape(8, D)
    wconv_all = _all_gather8(w_conv[0], "gather_wconv")
    w_conv_full = jnp.transpose(wconv_all[0::2], (1, 0, 2)).reshape(3, D)
    b_sh = lax.dynamic_slice(b_ada, (0, chip * NA), (1, NA))
    mod_sh = _ada_fwd(c_all, w_ada2, b_sh)
    mod_all = _all_gather8(mod_sh, "gather_mod")
    mod = lax.dynamic_slice(mod_all[0::2], (0, dev, 0), (4, 1, NA)).reshape(6, D)
    shift1, scale1, gate1, shift2, scale2, gate2 = (mod[k:k + 1] for k in range(6))

    u = _modulate(x2, scale1, shift1, "modulate1")
    pq = _matmul(u, W_qkv, "nn", F32, "proj_qkv")
    pc = _matmul(u, W_conv, "nn", F32, "proj_conv")
    pg = _matmul(u, W_gate, "nn", F32, "proj_gate")
    rq, rkv, kr = _rms_fwd(pq, tab, g_q_a, g_kv_a)
    q = _q_rope(_matmul(rq, W_qb, "nn", F32, "q_b"), tab)
    kv = _matmul(rkv, W_kvb, "nn", BF16, "kv_b")
    o, lse = _attn_fwd(q, kv, kr)
    y_a = _matmul(o, W_oa, "nn", F32, "o_a")
    hb = _conv_fwd(pc, w_conv_full)
    y_b = _matmul(hb, W_ob, "nn", F32, "o_b")
    merged = _merge_fwd(y_a, y_b, pg)
    mix = _matmul(merged, W_o, "nn", F32, "w_o")
    x1, u2 = _ln1_fwd(x2, mix, gate1, ln1_g, ln1_b, scale2, shift2)
    hh = _matmul(u2, W_fi, "nn", F32, "ffn_in")
    act = _swiglu_fwd(hh)
    ffn = _matmul(act, W_fo, "nn", F32, "ffn_out")

    dffn, dx1a, loss_acc, d_ln2_g, d_ln2_b, d_gate2 = _ln2_loss_bwd(x1, ffn, gate2, ln2_g, ln2_b, tgt)
    loss = lax.psum(loss_acc[0, 0], ("x", "y", "c"))
    dW_fo = _matmul(act, dffn, "tn", BF16, "d_w_ffn_out")
    dact = _matmul(dffn, W_fo, "nt", F32, "d_act")
    dhh = _swiglu_bwd(dact, hh)
    dW_fi = _matmul(u2, dhh, "tn", BF16, "d_w_ffn_in")
    du2 = _matmul(dhh, W_fi, "nt", F32, "d_u2")
    dmix, dxa, d_shift2, d_scale2, d_ln1_g, d_ln1_b, d_gate1 = _ln1_bwd(x2, mix, dx1a, du2, gate1, ln1_g, ln1_b, scale2)
    dW_o = _matmul(merged, dmix, "tn", BF16, "d_w_o")
    dmerged = _matmul(dmix, W_o, "nt", F32, "d_merged")
    dy_a, dy_b, dgate = _merge_bwd(dmerged, y_a, y_b, pg)
    dW_oa = _matmul(o, dy_a, "tn", BF16, "d_w_o_a")
    do = _matmul(dy_a, W_oa, "nt", BF16, "d_o")
    dW_ob = _matmul(hb, dy_b, "tn", BF16, "d_w_o_b")
    dhb = _matmul(dy_b, W_ob, "nt", F32, "d_hb")
    dconv, d_wconv = _conv_bwd(dhb, pc, w_conv_full)
    dq, dkv, dkr = _attn_bwd(q, kv, kr, do, o, lse, tab)
    dW_qb = _matmul(rq, dq, "tn", BF16, "d_w_q_b")
    d_rq = _matmul(dq, W_qb, "nt", F32, "d_rq")
    dW_kvb = _matmul(rkv, dkv, "tn", BF16, "d_w_kv_b")
    d_rkv = _matmul(dkv, W_kvb, "nt", F32, "d_rkv")
    dqkv, d_g_q, d_g_kv = _rms_bwd(d_rq, d_rkv, pq, dkr, g_q_a, g_kv_a)
    dW_qkv = _matmul(u, dqkv, "tn", BF16, "d_w_qkv")
    dW_conv = _matmul(u, dconv, "tn", BF16, "d_w_conv")
    dW_gate = _matmul(u, dgate, "tn", BF16, "d_w_gate")
    du = _matmul(dqkv, W_qkv, "nt", F32, "d_u_qkv")
    du = _matmul(dconv, W_conv, "nt", F32, "d_u_conv", add=du)
    du = _matmul(dgate, W_gate, "nt", F32, "d_u_gate", add=du)
    grad_x, d_shift1, d_scale1 = _dx_final(dxa, du, x2, scale1)

    def pad_d(v):
        return jnp.pad(v, ((0, 0), (0, D - v.shape[1])))

    small = jnp.concatenate([d_ln1_g, d_ln1_b, d_ln2_g, d_ln2_b, pad_d(d_g_q), pad_d(d_g_kv), d_wconv,
                             d_shift1, d_scale1, d_gate1, d_shift2, d_scale2, d_gate2, jnp.zeros((1, D), F32)], axis=0)
    small_all = _all_gather8(small, "gather_small")
    small_sum = _sum8(small_all)
    g_ln1_g, g_ln1_b, g_ln2_g, g_ln2_b = (small_sum[k:k + 1] for k in range(4))
    g_g_q, g_g_kv = small_sum[4:5, :Q_LORA], small_sum[5:6, :KV_LORA]
    g_wconv = lax.dynamic_slice(small_sum[6:9], (0, chip * CW), (3, CW))
    g_b_ada = small_sum[9:15].reshape(1, 6 * D)
    dmod_all = small_all[:, 9:15, :].reshape(8, 6 * D)
    g_w_ada = _ada_bwd(c_all, lax.dynamic_slice(dmod_all, (0, chip * NA), (8, NA)))

    def uncols(g):
        return jnp.transpose(g.reshape(g.shape[0], 4, g.shape[1] // 4), (1, 0, 2))

    dW_in = jnp.concatenate([dW_qkv[:, :n_qkv], dW_conv, dW_gate], axis=1)
    dW_qb_u = dW_qb.reshape(Q_LORA, N_HEADS, QK_PAD)[:, :, :QK_NOPE + QK_ROPE].reshape(Q_LORA, -1)
    parts = [uncols(dW_in), uncols(dW_qb_u), uncols(dW_kvb), dW_oa.reshape(4, -1, D), dW_ob.reshape(4, -1, D),
             dW_o.reshape(4, -1, D), uncols(dW_fi), dW_fo.reshape(4, -1, D)]
    names = ["w_in", "w_q_b", "w_kv_b", "w_o_a", "w_o_b", "w_o", "w_ffn_in", "w_ffn_out"]
    recv = _chip_exchange(parts, "scatter_grads", scatter=True)
    half = [_sum4(r, "sum4_" + nm) for r, nm in zip(recv, names)]
    other = _sibling_exchange(half, "sibling_grads")

    big = {}
    ws = dict(w_in=(w_in2, m_w_in[0], v_w_in[0]), w_q_b=(w_q_b2, m_w_q_b[0], v_w_q_b[0]),
              w_kv_b=(w_kv_b2, m_w_kv_b[0], v_w_kv_b[0]), w_o_a=(w_o_a2, m_w_o_a[0], v_w_o_a[0]),
              w_o_b=(w_o_b2, m_w_o_b[0], v_w_o_b[0]), w_o=(w_o2, m_w_o[0], v_w_o[0]),
              w_ffn_in=(w_ffn_in2, m_w_ffn_in[0], v_w_ffn_in[0]), w_ffn_out=(w_ffn_out2, m_w_ffn_out[0], v_w_ffn_out[0]))
    for nm, a, b in zip(names, half, other):
        w_, m_, v_ = ws[nm]
        big[nm] = _adam("adam_" + nm, w_, m_, v_, [a, b])
    big["w_ada"] = _adam("adam_w_ada", w_ada2, m_w_ada[0], v_w_ada[0], [g_w_ada])
    sm = {}
    for nm, w_, m_, v_, g_ in [("b_ada", b_ada, m_b_ada, v_b_ada, g_b_ada), ("g_q_a", g_q_a, m_g_q_a, v_g_q_a, g_g_q),
                               ("g_kv_a", g_kv_a, m_g_kv_a, v_g_kv_a, g_g_kv),
                               ("w_conv", w_conv[0], m_w_conv[0], v_w_conv[0], g_wconv),
                               ("ln1_g", ln1_g, m_ln1_g, v_ln1_g, g_ln1_g), ("ln1_b", ln1_b, m_ln1_b, v_ln1_b, g_ln1_b),
                               ("ln2_g", ln2_g, m_ln2_g, v_ln2_g, g_ln2_g), ("ln2_b", ln2_b, m_ln2_b, v_ln2_b, g_ln2_b)]:
        sm[nm] = (g_,) + tuple(_adam_small("adam_" + nm, w_, m_, v_, g_))

    order = ["w_ada", "b_ada", "w_in", "g_q_a", "w_q_b", "g_kv_a", "w_kv_b", "w_o_a", "w_conv", "w_o_b", "w_o",
             "ln1_g", "ln1_b", "w_ffn_in", "w_ffn_out", "ln2_g", "ln2_b"]
    lead = {"b_ada", "g_q_a", "g_kv_a", "ln1_g", "ln1_b", "ln2_g", "ln2_b"}

    def leaf(nm, k):
        val = big[nm][k] if nm in big else sm[nm][k]
        return val if nm in lead else val[None]

    outs = [loss, grad_x[None]]
    for k in range(4):
        outs += [leaf(nm, k) for nm in order]
    return tuple(outs)
```

```python
import functools

import jax
import jax.numpy as jnp
from jax import lax
from jax.experimental import pallas as pl
from jax.experimental.pallas import tpu as pltpu

F32, BF16 = jnp.float32, jnp.bfloat16
N_HEADS, QK_NOPE, QK_ROPE, V_HEAD = 16, 128, 64, 128
Q_LORA, KV_LORA = 512, 512
QK_PAD = 256
QKV_A = 1152
CHUNK_SHIFT = 6
ATTN_SCALE = (QK_NOPE + QK_ROPE) ** -0.5
ROPE_THETA = 10000.0
ALPHA = 2.0 ** 0.25
LN_EPS, RMS_EPS = 1e-5, 1e-6
ADAM_LR, ADAM_B1, ADAM_B2, ADAM_EPS, ADAM_WD, ADAM_STEP = 0.001, 0.9, 0.999, 1e-08, 0.01, 10
ADAM_C1 = 1.0 - ADAM_B1 ** ADAM_STEP
ADAM_C2 = 1.0 - ADAM_B2 ** ADAM_STEP
VMEM_LIMIT = 56 * 1024 * 1024
MESH = pl.DeviceIdType.MESH
ANY = pl.BlockSpec(memory_space=pl.ANY)
NT = (((1,), (1,)), ((), ()))
TN = (((0,), (0,)), ((), ()))
NN = (((1,), (0,)), ((), ()))


def _params(sem=None):
    return pltpu.CompilerParams(dimension_semantics=sem, vmem_limit_bytes=VMEM_LIMIT)


def _pick(n, cands=(1024, 512, 384, 256, 128)):
    for t in cands:
        if n % t == 0:
            return t
    return n


def _row_tile(rows, row_bytes, budget, mult=8):
    best = mult
    for t in range(mult, rows + 1, mult):
        if rows % t == 0 and t * row_bytes <= budget:
            best = t
    return best


def _sigmoid(x):
    return jax.nn.sigmoid(x)


def _matmul(a, b, mode, out_dtype, name, add=None):
    if mode == "nn":
        (M, K), N, dims = a.shape, b.shape[1], NN
    elif mode == "nt":
        (M, K), N, dims = a.shape, b.shape[0], NT
    else:
        (K, M), N, dims = a.shape, b.shape[1], TN
    tm, tn, tk = _pick(M), _pick(N), _pick(K)
    nk = K // tk
    a_spec = (pl.BlockSpec((tk, tm), lambda i, j, k: (k, i)) if mode == "tn"
              else pl.BlockSpec((tm, tk), lambda i, j, k: (i, k)))
    b_spec = (pl.BlockSpec((tn, tk), lambda i, j, k: (j, k)) if mode == "nt"
              else pl.BlockSpec((tk, tn), lambda i, j, k: (k, j)))
    o_spec = pl.BlockSpec((tm, tn), lambda i, j, k: (i, j))
    has_add = add is not None

    def body(*refs):
        a_ref, b_ref = refs[0], refs[1]
        add_ref = refs[2] if has_add else None
        o_ref, acc_ref = refs[-2], refs[-1]
        k = pl.program_id(2)

        @pl.when(k == 0)
        def _():
            acc_ref[...] = jnp.zeros_like(acc_ref)

        acc_ref[...] += lax.dot_general(a_ref[...], b_ref[...], dims, preferred_element_type=F32)

        @pl.when(k == nk - 1)
        def _():
            r = acc_ref[...]
            if has_add:
                r = r + add_ref[...]
            o_ref[...] = r.astype(o_ref.dtype)

    ins = [a, b] + ([add] if has_add else [])
    in_specs = [a_spec, b_spec] + ([o_spec] if has_add else [])
    return pl.pallas_call(
        body, name=name, grid=(M // tm, N // tn, nk),
        in_specs=in_specs, out_specs=o_spec,
        out_shape=jax.ShapeDtypeStruct((M, N), out_dtype),
        scratch_shapes=[pltpu.VMEM((tm, tn), F32)],
        compiler_params=_params(("parallel", "parallel", "arbitrary")),
    )(*ins)


def _rows(body, name, n_rows, tm, ins, outs, accs=()):
    grid = (n_rows // tm,)
    per8 = tm // 8
    last8 = n_rows // 8 - 1
    arrays, in_specs = [], []
    for spec in ins:
        kind, arr = spec[0], spec[1]
        arrays.append(arr)
        if kind == "row":
            _, _, cb, w = spec
            in_specs.append(pl.BlockSpec((tm, w), lambda i, cb=cb: (i, cb)))
        elif kind == "full":
            in_specs.append(pl.BlockSpec(arr.shape, lambda i, nd=arr.ndim: (0,) * nd))
        elif kind == "prev":
            _, _, cb, w = spec
            in_specs.append(pl.BlockSpec((8, w), lambda i, cb=cb: (jnp.maximum(i * per8 - 1, 0), cb)))
        else:
            _, _, cb, w = spec
            in_specs.append(pl.BlockSpec((8, w), lambda i, cb=cb: (jnp.minimum((i + 1) * per8, last8), cb)))
    out_shape = [jax.ShapeDtypeStruct((n_rows, w), dt) for (w, dt) in outs]
    out_specs = [pl.BlockSpec((tm, w), lambda i: (i, 0)) for (w, _) in outs]
    out_shape += [jax.ShapeDtypeStruct(s, F32) for s in accs]
    out_specs += [pl.BlockSpec(s, lambda i, nd=len(s): (0,) * nd) for s in accs]
    n_in, n_out = len(ins), len(outs)

    def kernel_body(*refs):
        body(pl.program_id(0), refs[:n_in], refs[n_in:n_in + n_out], refs[n_in + n_out:])

    res = pl.pallas_call(
        kernel_body, name=name, grid=grid, in_specs=in_specs, out_specs=out_specs, out_shape=out_shape,
        compiler_params=_params(("arbitrary",)),
    )(*arrays)
    return res


def _acc_add(i, ref, val):
    @pl.when(i == 0)
    def _():
        ref[...] = val

    @pl.when(i > 0)
    def _():
        ref[...] += val


def _rope(t, tab, sign):
    c, sa, sb = tab[:, 0:128], tab[:, 128:256], tab[:, 256:384]
    rot = pltpu.roll(t, 96, 1) * sa + pltpu.roll(t, 32, 1) * sb
    return t * c + rot if sign > 0 else t * c - rot


def _ln_stats(r):
    mu = jnp.mean(r, axis=-1, keepdims=True)
    d = r - mu
    var = jnp.mean(d * d, axis=-1, keepdims=True)
    rstd = lax.rsqrt(var + LN_EPS)
    return d * rstd, rstd


def _ln_bwd(dxh, xh, rstd):
    m1 = jnp.mean(dxh, axis=-1, keepdims=True)
    m2 = jnp.mean(dxh * xh, axis=-1, keepdims=True)
    return rstd * (dxh - m1 - xh * m2)


def _modulate(x, scale, shift, name):
    S, D = x.shape

    def body(i, ins, outs, accs):
        outs[0][...] = (ins[0][...] * (1.0 + ins[1][...]) + ins[2][...]).astype(BF16)

    return _rows(body, name, S, _pick(S, (256, 128)), [("row", x, 0, D), ("full", scale), ("full", shift)], [(D, BF16)])[0]


def _rms_fwd(pq, tab, g_q, g_kv):
    S = pq.shape[0]

    def body(i, ins, outs, accs):
        pq_ref, tab_ref, gq_ref, gkv_ref = ins

        def rms(x, g):
            return x * lax.rsqrt(jnp.mean(x * x, axis=-1, keepdims=True) + RMS_EPS) * g

        outs[0][...] = rms(pq_ref[:, 0:Q_LORA], gq_ref[...]).astype(BF16)
        outs[1][...] = rms(pq_ref[:, Q_LORA:Q_LORA + KV_LORA], gkv_ref[...]).astype(BF16)
        outs[2][...] = _rope(pq_ref[:, Q_LORA + KV_LORA:QKV_A], tab_ref[...], 1).astype(BF16)

    return _rows(body, "rms_fwd", S, _pick(S, (256, 128)),
                 [("row", pq, 0, QKV_A), ("row", tab, 0, 384), ("full", g_q), ("full", g_kv)],
                 [(Q_LORA, BF16), (KV_LORA, BF16), (128, BF16)])


def _q_rope(q, tab):
    S, W = q.shape

    def body(i, ins, outs, accs):
        q_ref, tab_ref = ins
        t = tab_ref[...]
        for h in range(N_HEADS):
            lo = h * QK_PAD
            outs[0][:, lo:lo + 128] = q_ref[:, lo:lo + 128].astype(BF16)
            outs[0][:, lo + 128:lo + 256] = _rope(q_ref[:, lo + 128:lo + 256], t, 1).astype(BF16)

    return _rows(body, "q_rope", S, _pick(S, (256, 128)), [("row", q, 0, W), ("row", tab, 0, 384)], [(W, BF16)])[0]


def _allowed(q0, k0, bq):
    row = q0 + lax.broadcasted_iota(jnp.int32, (bq, bq), 0)
    col = k0 + lax.broadcasted_iota(jnp.int32, (bq, bq), 1)
    return (col >> CHUNK_SHIFT) <= (row >> CHUNK_SHIFT)


def _attn_fwd(q, kv, kr):
    S = q.shape[0]
    bq = min(256, S)
    nq = S // bq

    def body(q_ref, kn_ref, v_ref, kr_ref, o_ref, lse_ref):
        qi = pl.program_id(1)
        qv = q_ref[...]

        def step(j, carry):
            m, l, acc = carry
            off = pl.multiple_of(j * bq, bq)
            k = jnp.concatenate([kn_ref[pl.ds(off, bq), :], kr_ref[pl.ds(off, bq), :]], axis=1)
            s = lax.dot_general(qv, k, NT, preferred_element_type=F32) * ATTN_SCALE
            s = jnp.where(_allowed(qi * bq, off, bq), s, -1e30)
            m_new = jnp.maximum(m, jnp.max(s, axis=1, keepdims=True))
            a = jnp.exp(m - m_new)
            p = jnp.exp(s - m_new)
            l = a * l + jnp.sum(p, axis=1, keepdims=True)
            acc = a * acc + jnp.dot(p.astype(BF16), v_ref[pl.ds(off, bq), :], preferred_element_type=F32)
            return m_new, l, acc

        init = (jnp.full((bq, 1), -1e30, F32), jnp.zeros((bq, 1), F32), jnp.zeros((bq, V_HEAD), F32))
        m, l, acc = lax.fori_loop(0, qi + 1, step, init)
        o_ref[...] = (acc / l).astype(BF16)
        lse_ref[0] = m + jnp.log(l)

    return pl.pallas_call(
        body, name="attn_fwd", grid=(N_HEADS, nq),
        in_specs=[pl.BlockSpec((bq, QK_PAD), lambda h, i: (i, h)),
                  pl.BlockSpec((S, 128), lambda h, i: (0, 2 * h)),
                  pl.BlockSpec((S, 128), lambda h, i: (0, 2 * h + 1)),
                  pl.BlockSpec((S, 128), lambda h, i: (0, 0))],
        out_specs=[pl.BlockSpec((bq, V_HEAD), lambda h, i: (i, h)),
                   pl.BlockSpec((1, bq, 1), lambda h, i: (h, i, 0))],
        out_shape=[jax.ShapeDtypeStruct((S, N_HEADS * V_HEAD), BF16),
                   jax.ShapeDtypeStruct((N_HEADS, S, 1), F32)],
        compiler_params=_params(("arbitrary", "arbitrary")),
    )(q, kv, kv, kr)


def _attn_bwd(q, kv, kr, do, o, lse, tab):
    S = q.shape[0]
    bq = min(256, S)
    nq = S // bq

    def body(q_ref, kn_ref, v_ref, kr_ref, do_ref, o_ref, lse_ref, tab_ref, dq_ref, dkv_ref, dkr_ref,
             dq_acc, dk_acc, dv_acc):
        h = pl.program_id(0)
        dq_acc[...] = jnp.zeros_like(dq_acc)
        dk_acc[...] = jnp.zeros_like(dk_acc)
        dv_acc[...] = jnp.zeros_like(dv_acc)

        def kv_step(j, _):
            offj = pl.multiple_of(j * bq, bq)
            rows_j = pl.ds(offj, bq)
            k = jnp.concatenate([kn_ref[rows_j, :], kr_ref[rows_j, :]], axis=1)
            v = v_ref[rows_j, :]

            def q_step(i, _):
                offi = pl.multiple_of(i * bq, bq)
                rows_i = pl.ds(offi, bq)
                qv, dov = q_ref[rows_i, :], do_ref[rows_i, :]
                delta = jnp.sum(dov.astype(F32) * o_ref[rows_i, :].astype(F32), axis=1, keepdims=True)
                s = lax.dot_general(qv, k, NT, preferred_element_type=F32) * ATTN_SCALE
                s = jnp.where(_allowed(offi, offj, bq), s, -1e30)
                p = jnp.exp(s - lse_ref[0, rows_i, :])
                dv_acc[rows_j, :] += lax.dot_general(p.astype(BF16), dov, TN, preferred_element_type=F32)
                dp = lax.dot_general(dov, v, NT, preferred_element_type=F32)
                ds = (p * (dp - delta) * ATTN_SCALE).astype(BF16)
                dk_acc[rows_j, :] += lax.dot_general(ds, qv, TN, preferred_element_type=F32)
                dq_acc[rows_i, :] += jnp.dot(ds, k, preferred_element_type=F32)
                return 0

            lax.fori_loop(j, nq, q_step, 0)
            return 0

        lax.fori_loop(0, nq, kv_step, 0)

        for r in range(nq):
            rows = slice(r * bq, (r + 1) * bq)
            dq_ref[rows, 0:128] = dq_acc[rows, 0:128].astype(BF16)
            dq_ref[rows, 128:256] = _rope(dq_acc[rows, 128:256], tab_ref[rows, :], -1).astype(BF16)
        dkv_ref[:, 0:128] = dk_acc[:, 0:128].astype(BF16)
        dkv_ref[:, 128:256] = dv_acc[...].astype(BF16)

        @pl.when(h == 0)
        def _():
            dkr_ref[...] = dk_acc[:, 128:256]

        @pl.when(h > 0)
        def _():
            dkr_ref[...] += dk_acc[:, 128:256]

        @pl.when(h == N_HEADS - 1)
        def _():
            for r in range(nq):
                rows = slice(r * bq, (r + 1) * bq)
                dkr_ref[rows, :] = _rope(dkr_ref[rows, :], tab_ref[rows, :], -1)

    W = N_HEADS * QK_PAD
    return pl.pallas_call(
        body, name="attn_bwd", grid=(N_HEADS,),
        in_specs=[pl.BlockSpec((S, QK_PAD), lambda h: (0, h)),
                  pl.BlockSpec((S, 128), lambda h: (0, 2 * h)),
                  pl.BlockSpec((S, 128), lambda h: (0, 2 * h + 1)),
                  pl.BlockSpec((S, 128), lambda h: (0, 0)),
                  pl.BlockSpec((S, V_HEAD), lambda h: (0, h)),
                  pl.BlockSpec((S, V_HEAD), lambda h: (0, h)),
                  pl.BlockSpec((1, S, 1), lambda h: (h, 0, 0)),
                  pl.BlockSpec((S, 384), lambda h: (0, 0))],
        out_specs=[pl.BlockSpec((S, QK_PAD), lambda h: (0, h)),
                   pl.BlockSpec((S, QK_PAD), lambda h: (0, h)),
                   pl.BlockSpec((S, 128), lambda h: (0, 0))],
        out_shape=[jax.ShapeDtypeStruct((S, W), BF16), jax.ShapeDtypeStruct((S, W), BF16),
                   jax.ShapeDtypeStruct((S, 128), F32)],
        scratch_shapes=[pltpu.VMEM((S, QK_PAD), F32), pltpu.VMEM((S, QK_PAD), F32), pltpu.VMEM((S, V_HEAD), F32)],
        compiler_params=_params(("arbitrary",)),
    )(q, kv, kv, kr, do, o, lse, tab)


def _shift_down(cur, prev8, i, n):
    tm = cur.shape[0]
    prev8 = jnp.where(i == 0, jnp.zeros_like(prev8), prev8)
    full = jnp.concatenate([prev8, cur], axis=0)
    return pltpu.roll(full, n, 0)[8:8 + tm, :]


def _shift_up(cur, next8, i, last, n):
    tm = cur.shape[0]
    next8 = jnp.where(i == last, jnp.zeros_like(next8), next8)
    full = jnp.concatenate([cur, next8], axis=0)
    return pltpu.roll(full, tm + 8 - n, 0)[0:tm, :]


def _conv_fwd(pc, w_conv):
    S, D = pc.shape[0], pc.shape[1] // 3
    tm = _pick(S, (256, 128))

    def body(i, ins, outs, accs):
        b_ref, c_ref, x_ref, cp_ref, xp_ref, w_ref = ins
        z = c_ref[...] * x_ref[...]
        zp = cp_ref[...] * xp_ref[...]
        cz = w_ref[0:1, :] * _shift_down(z, zp, i, 2) + w_ref[1:2, :] * _shift_down(z, zp, i, 1) + w_ref[2:3, :] * z
        outs[0][...] = (b_ref[...] * cz).astype(BF16)

    return _rows(body, "conv_fwd", S, tm,
                 [("row", pc, 0, D), ("row", pc, 1, D), ("row", pc, 2, D), ("prev", pc, 1, D), ("prev", pc, 2, D),
                  ("full", w_conv)], [(D, BF16)])[0]


def _conv_bwd(dhb, pc, w_conv):
    S, D = dhb.shape
    tm = _pick(S, (256, 128))
    last = S // tm - 1

    def body(i, ins, outs, accs):
        g_ref, b_ref, c_ref, x_ref, cp_ref, xp_ref, gn_ref, bn_ref, w_ref = ins
        w0, w1, w2 = w_ref[0:1, :], w_ref[1:2, :], w_ref[2:3, :]
        c, x, g = c_ref[...], x_ref[...], g_ref[...]
        z = c * x
        zp = cp_ref[...] * xp_ref[...]
        z1, z2 = _shift_down(z, zp, i, 1), _shift_down(z, zp, i, 2)
        cz = w0 * z2 + w1 * z1 + w2 * z
        dcz = g * b_ref[...]
        dczn = gn_ref[...] * bn_ref[...]
        dz = w2 * dcz + w1 * _shift_up(dcz, dczn, i, last, 1) + w0 * _shift_up(dcz, dczn, i, last, 2)
        outs[0][:, 0:D] = (g * cz).astype(BF16)
        outs[0][:, D:2 * D] = (dz * x).astype(BF16)
        outs[0][:, 2 * D:3 * D] = (dz * c).astype(BF16)
        dw = jnp.concatenate([jnp.sum(dcz * z2, axis=0, keepdims=True), jnp.sum(dcz * z1, axis=0, keepdims=True),
                              jnp.sum(dcz * z, axis=0, keepdims=True)], axis=0)
        _acc_add(i, accs[0], dw)

    return _rows(body, "conv_bwd", S, tm,
                 [("row", dhb, 0, D), ("row", pc, 0, D), ("row", pc, 1, D), ("row", pc, 2, D),
                  ("prev", pc, 1, D), ("prev", pc, 2, D), ("next", dhb, 0, D), ("next", pc, 0, D), ("full", w_conv)],
                 [(3 * D, BF16)], [(3, D)])


def _merge_fwd(y_a, y_b, pg):
    S, D = y_a.shape

    def body(i, ins, outs, accs):
        ya, yb, ga, gb = ins
        outs[0][...] = (_sigmoid(ga[...]) * ya[...] + _sigmoid(gb[...]) * yb[...]).astype(BF16)

    return _rows(body, "merge_fwd", S, _pick(S, (256, 128)),
                 [("row", y_a, 0, D), ("row", y_b, 0, D), ("row", pg, 0, D), ("row", pg, 1, D)], [(D, BF16)])[0]


def _merge_bwd(dm, y_a, y_b, pg):
    S, D = dm.shape

    def body(i, ins, outs, accs):
        d, ya, yb = ins[0][...], ins[1][...], ins[2][...]
        sa, sb = _sigmoid(ins[3][...]), _sigmoid(ins[4][...])
        outs[0][...] = (d * sa).astype(BF16)
        outs[1][...] = (d * sb).astype(BF16)
        outs[2][:, 0:D] = (d * ya * (sa * (1.0 - sa))).astype(BF16)
        outs[2][:, D:2 * D] = (d * yb * (sb * (1.0 - sb))).astype(BF16)

    return _rows(body, "merge_bwd", S, _pick(S, (256, 128)),
                 [("row", dm, 0, D), ("row", y_a, 0, D), ("row", y_b, 0, D), ("row", pg, 0, D), ("row", pg, 1, D)],
                 [(D, BF16), (D, BF16), (2 * D, BF16)])


def _ln1_fwd(x, mix, gate1, g, b, scale2, shift2):
    S, D = x.shape

    def body(i, ins, outs, accs):
        x_ref, mix_ref, gate_ref, g_ref, b_ref, sc_ref, sh_ref = ins
        xh, _ = _ln_stats(ALPHA * x_ref[...] + gate_ref[...] * mix_ref[...])
        x1 = xh * g_ref[...] + b_ref[...]
        outs[0][...] = x1
        outs[1][...] = (x1 * (1.0 + sc_ref[...]) + sh_ref[...]).astype(BF16)

    return _rows(body, "ln1_fwd", S, _pick(S, (256, 128)),
                 [("row", x, 0, D), ("row", mix, 0, D), ("full", gate1), ("full", g), ("full", b),
                  ("full", scale2), ("full", shift2)], [(D, F32), (D, BF16)])


def _swiglu_fwd(hh):
    S, F = hh.shape[0], hh.shape[1] // 2

    def body(i, ins, outs, accs):
        hg = ins[0][...]
        outs[0][...] = (hg * _sigmoid(hg) * ins[1][...]).astype(BF16)

    return _rows(body, "swiglu_fwd", S, _pick(S, (128,)), [("row", hh, 0, F), ("row", hh, 1, F)], [(F, BF16)])[0]


def _swiglu_bwd(dact, hh):
    S, F = dact.shape

    def body(i, ins, outs, accs):
        d, hg, hu = ins[0][...], ins[1][...], ins[2][...]
        sg = _sigmoid(hg)
        outs[0][:, 0:F] = (d * hu * (sg * (1.0 + hg * (1.0 - sg)))).astype(BF16)
        outs[0][:, F:2 * F] = (d * (hg * sg)).astype(BF16)

    return _rows(body, "swiglu_bwd", S, _pick(S, (128,)),
                 [("row", dact, 0, F), ("row", hh, 0, F), ("row", hh, 1, F)], [(2 * F, BF16)])[0]


def _ln2_loss_bwd(x1, ffn, gate2, g, b, target):
    S, D = x1.shape

    def body(i, ins, outs, accs):
        x1_ref, f_ref, gate_ref, g_ref, b_ref, t_ref = ins
        f = f_ref[...]
        xh, rstd = _ln_stats(ALPHA * x1_ref[...] + gate_ref[...] * f)
        e = xh * g_ref[...] + b_ref[...] - t_ref[...]
        dy = e * (1.0 / D)
        dr = _ln_bwd(dy * g_ref[...], xh, rstd)
        outs[0][...] = (gate_ref[...] * dr).astype(BF16)
        outs[1][...] = ALPHA * dr
        _acc_add(i, accs[0], jnp.full((1, 128), (0.5 / D) * jnp.sum(e * e), F32))
        _acc_add(i, accs[1], jnp.sum(dy * xh, axis=0, keepdims=True))
        _acc_add(i, accs[2], jnp.sum(dy, axis=0, keepdims=True))
        _acc_add(i, accs[3], jnp.sum(dr * f, axis=0, keepdims=True))

    return _rows(body, "ln2_loss_bwd", S, _pick(S, (256, 128)),
                 [("row", x1, 0, D), ("row", ffn, 0, D), ("full", gate2), ("full", g), ("full", b), ("row", target, 0, D)],
                 [(D, BF16), (D, F32)], [(1, 128), (1, D), (1, D), (1, D)])


def _ln1_bwd(x, mix, dx1a, du2, gate1, g, b, scale2):
    S, D = x.shape

    def body(i, ins, outs, accs):
        x_ref, mix_ref, da_ref, du_ref, gate_ref, g_ref, b_ref, sc_ref = ins
        mix, du = mix_ref[...], du_ref[...]
        xh, rstd = _ln_stats(ALPHA * x_ref[...] + gate_ref[...] * mix)
        x1 = xh * g_ref[...] + b_ref[...]
        dx1 = da_ref[...] + du * (1.0 + sc_ref[...])
        dr = _ln_bwd(dx1 * g_ref[...], xh, rstd)
        outs[0][...] = (gate_ref[...] * dr).astype(BF16)
        outs[1][...] = ALPHA * dr
        _acc_add(i, accs[0], jnp.sum(du, axis=0, keepdims=True))
        _acc_add(i, accs[1], jnp.sum(du * x1, axis=0, keepdims=True))
        _acc_add(i, accs[2], jnp.sum(dx1 * xh, axis=0, keepdims=True))
        _acc_add(i, accs[3], jnp.sum(dx1, axis=0, keepdims=True))
        _acc_add(i, accs[4], jnp.sum(dr * mix, axis=0, keepdims=True))

    return _rows(body, "ln1_bwd", S, _pick(S, (256, 128)),
                 [("row", x, 0, D), ("row", mix, 0, D), ("row", dx1a, 0, D), ("row", du2, 0, D),
                  ("full", gate1), ("full", g), ("full", b), ("full", scale2)],
                 [(D, BF16), (D, F32)], [(1, D)] * 5)


def _rms_bwd(d_rq, d_rkv, pq, dkr, g_q, g_kv):
    S = pq.shape[0]

    def body(i, ins, outs, accs):
        dq_ref, dkv_ref, pq_ref, dkr_ref, gq_ref, gkv_ref = ins

        def rms_bwd(dy, x, g):
            r = lax.rsqrt(jnp.mean(x * x, axis=-1, keepdims=True) + RMS_EPS)
            dyg = dy * g
            dx = r * dyg - x * (r * r * r) * jnp.mean(dyg * x, axis=-1, keepdims=True)
            return dx, jnp.sum(dy * (x * r), axis=0, keepdims=True)

        dxq, dgq = rms_bwd(dq_ref[...], pq_ref[:, 0:Q_LORA], gq_ref[...])
        dxkv, dgkv = rms_bwd(dkv_ref[...], pq_ref[:, Q_LORA:Q_LORA + KV_LORA], gkv_ref[...])
        outs[0][:, 0:Q_LORA] = dxq.astype(BF16)
        outs[0][:, Q_LORA:Q_LORA + KV_LORA] = dxkv.astype(BF16)
        outs[0][:, Q_LORA + KV_LORA:QKV_A] = dkr_ref[...].astype(BF16)
        _acc_add(i, accs[0], dgq)
        _acc_add(i, accs[1], dgkv)

    return _rows(body, "rms_bwd", S, _pick(S, (256, 128)),
                 [("row", d_rq, 0, Q_LORA), ("row", d_rkv, 0, KV_LORA), ("row", pq, 0, QKV_A), ("row", dkr, 0, 128),
                  ("full", g_q), ("full", g_kv)], [(QKV_A, BF16)], [(1, Q_LORA), (1, KV_LORA)])


def _dx_final(dxa, du, x, scale1):
    S, D = x.shape

    def body(i, ins, outs, accs):
        du = ins[1][...]
        outs[0][...] = ins[0][...] + du * (1.0 + ins[3][...])
        _acc_add(i, accs[0], jnp.sum(du, axis=0, keepdims=True))
        _acc_add(i, accs[1], jnp.sum(du * ins[2][...], axis=0, keepdims=True))

    return _rows(body, "dx_final", S, _pick(S, (256, 128)),
                 [("row", dxa, 0, D), ("row", du, 0, D), ("row", x, 0, D), ("full", scale1)],
                 [(D, F32)], [(1, D), (1, D)])


def _ada_fwd(c_all, w, bias):
    B, D = c_all.shape
    NA = w.shape[1]
    tn = _pick(NA, (512, 256, 128))

    def body(c_ref, w_ref, b_ref, o_ref):
        cv = c_ref[...]
        ca = (cv * _sigmoid(cv)).astype(BF16)
        o_ref[...] = jnp.dot(ca, w_ref[...].astype(BF16), preferred_element_type=F32) + b_ref[...]

    return pl.pallas_call(
        body, name="ada_fwd", grid=(NA // tn,),
        in_specs=[pl.BlockSpec((B, D), lambda j: (0, 0)), pl.BlockSpec((D, tn), lambda j: (0, j)),
                  pl.BlockSpec((1, tn), lambda j: (0, j))],
        out_specs=pl.BlockSpec((B, tn), lambda j: (0, j)),
        out_shape=jax.ShapeDtypeStruct((B, NA), F32),
        compiler_params=_params(("arbitrary",)),
    )(c_all, w, bias)


def _ada_bwd(c_all, dmod):
    B, D = c_all.shape
    NA = dmod.shape[1]
    tn = _pick(NA, (512, 256, 128))

    def body(c_ref, d_ref, o_ref):
        cv = c_ref[...]
        ca = (cv * _sigmoid(cv)).astype(BF16)
        o_ref[...] = lax.dot_general(ca, d_ref[...].astype(BF16), TN, preferred_element_type=F32)

    return pl.pallas_call(
        body, name="ada_bwd", grid=(NA // tn,),
        in_specs=[pl.BlockSpec((B, D), lambda j: (0, 0)), pl.BlockSpec((B, tn), lambda j: (0, j))],
        out_specs=pl.BlockSpec((D, tn), lambda j: (0, j)),
        out_shape=jax.ShapeDtypeStruct((D, NA), F32),
        compiler_params=_params(("arbitrary",)),
    )(c_all, dmod)


def _sum8(parts):
    _, R, N = parts.shape

    def body(p_ref, o_ref):
        acc = p_ref[0]
        for d in range(1, 8):
            acc = acc + p_ref[d]
        o_ref[...] = acc

    return pl.pallas_call(body, name="sum8", out_shape=jax.ShapeDtypeStruct((R, N), F32),
                          compiler_params=_params())(parts)


def _adam_math(w, g, m, v):
    m = ADAM_B1 * m + (1.0 - ADAM_B1) * g
    v = ADAM_B2 * v + (1.0 - ADAM_B2) * (g * g)
    delta = -ADAM_LR * ((m / ADAM_C1) / (jnp.sqrt(v / ADAM_C2) + ADAM_EPS) + ADAM_WD * w)
    return delta, m, v


def _adam(name, w, m, v, g_parts):
    R, C = w.shape
    tm = _row_tile(R, C * 4, 1 << 20)
    n = len(g_parts)

    def body(*refs):
        w_ref, m_ref, v_ref = refs[0], refs[1], refs[2]
        g = refs[3][...]
        for r in refs[4:3 + n]:
            g = g + r[...]
        g_ref, d_ref, nm_ref, nv_ref = refs[3 + n:]
        delta, nm, nv = _adam_math(w_ref[...], g, m_ref[...], v_ref[...])
        g_ref[...] = g
        d_ref[...] = delta
        nm_ref[...] = nm
        nv_ref[...] = nv

    spec = pl.BlockSpec((tm, C), lambda i: (i, 0))
    return pl.pallas_call(
        body, name=name, grid=(R // tm,), in_specs=[spec] * (3 + n), out_specs=[spec] * 4,
        out_shape=[jax.ShapeDtypeStruct((R, C), F32)] * 4, compiler_params=_params(("parallel",)),
    )(w, m, v, *g_parts)


def _adam_small(name, w, m, v, g):
    def body(w_ref, m_ref, v_ref, g_ref, d_ref, nm_ref, nv_ref):
        delta, nm, nv = _adam_math(w_ref[...], g_ref[...], m_ref[...], v_ref[...])
        d_ref[...] = delta
        nm_ref[...] = nm
        nv_ref[...] = nv

    return pl.pallas_call(body, name=name, out_shape=[jax.ShapeDtypeStruct(w.shape, F32)] * 3,
                          compiler_params=_params())(w, m, v, g)


def _place():
    return lax.axis_index("x"), lax.axis_index("y"), lax.axis_index("c")


def _other_chips(x, y):
    return [(1 - x, y), (x, 1 - y), (1 - x, 1 - y)]


def _all_gather8(blk, name):
    R, N = blk.shape

    def body(x_ref, out_ref, send_sems, recv_sems, local_sem):
        x, y, c = _place()
        me = 4 * x + 2 * y + c
        mine = pltpu.make_async_copy(x_ref, out_ref.at[me], local_sem)
        mine.start()
        flips = [(j >> 2 & 1, j >> 1 & 1, j & 1) for j in range(1, 8)]
        peers = [((1 - x) if fx else x, (1 - y) if fy else y, (1 - c) if fc else c) for fx, fy, fc in flips]
        sends = []
        for j, peer in enumerate(peers):
            cp = pltpu.make_async_remote_copy(src_ref=x_ref, dst_ref=out_ref.at[me], send_sem=send_sems.at[j],
                                              recv_sem=recv_sems.at[j], device_id=peer, device_id_type=MESH)
            cp.start()
            sends.append(cp)
        for j, (px, py, pc) in enumerate(peers):
            pltpu.make_async_remote_copy(src_ref=x_ref, dst_ref=out_ref.at[4 * px + 2 * py + pc],
                                         send_sem=send_sems.at[j], recv_sem=recv_sems.at[j],
                                         device_id=(px, py, pc), device_id_type=MESH).wait_recv()
        for cp in sends:
            cp.wait_send()
        mine.wait()

    return pl.pallas_call(
        body, name=name, out_shape=jax.ShapeDtypeStruct((8, R, N), F32),
        in_specs=[pl.BlockSpec(memory_space=pltpu.VMEM)], out_specs=pl.BlockSpec(memory_space=pltpu.VMEM),
        scratch_shapes=[pltpu.SemaphoreType.DMA((7,)), pltpu.SemaphoreType.DMA((7,)), pltpu.SemaphoreType.DMA],
        compiler_params=_params(),
    )(blk)


def _chip_exchange(arrs, name, scatter):
    n = len(arrs)

    def body(*refs):
        ins, outs = refs[:n], refs[n:2 * n]
        send_sems, recv_sems, local_sems = refs[2 * n:]
        x, y, c = _place()
        me = 2 * x + y
        chips = _other_chips(x, y)
        started = []
        for k in range(n):
            own = ins[k].at[me] if scatter else ins[k]
            loc = pltpu.make_async_copy(own, outs[k].at[me], local_sems.at[k])
            loc.start()
            started.append(loc)
            for j, (px, py) in enumerate(chips):
                src = ins[k].at[2 * px + py] if scatter else ins[k]
                cp = pltpu.make_async_remote_copy(src_ref=src, dst_ref=outs[k].at[me], send_sem=send_sems.at[3 * k + j],
                                                  recv_sem=recv_sems.at[3 * k + j], device_id=(px, py, c), device_id_type=MESH)
                cp.start()
                started.append(cp)
        for k in range(n):
            for j, (px, py) in enumerate(chips):
                src = ins[k].at[me] if scatter else ins[k]
                pltpu.make_async_remote_copy(src_ref=src, dst_ref=outs[k].at[2 * px + py], send_sem=send_sems.at[3 * k + j],
                                             recv_sem=recv_sems.at[3 * k + j], device_id=(px, py, c),
                                             device_id_type=MESH).wait_recv()
        for k in range(n):
            started[4 * k].wait()
            for j in range(3):
                started[4 * k + 1 + j].wait_send()

    out_shape = [jax.ShapeDtypeStruct(a.shape if scatter else (4,) + a.shape, a.dtype) for a in arrs]
    return pl.pallas_call(
        body, name=name, out_shape=out_shape, in_specs=[ANY] * n, out_specs=[ANY] * n,
        scratch_shapes=[pltpu.SemaphoreType.DMA((3 * n,)), pltpu.SemaphoreType.DMA((3 * n,)), pltpu.SemaphoreType.DMA((n,))],
        compiler_params=_params(),
    )(*arrs)


def _gather2(shards, name):
    n = len(shards)

    def body(*refs):
        ins, outs = refs[:n], refs[n:2 * n]
        s1, r1, s2, r2, loc = refs[2 * n:]
        x, y, c = _place()
        me = 2 * x + y
        chips = _other_chips(x, y)
        sib = (x, y, 1 - c)

        def ici(k, j, slab, to):
            return pltpu.make_async_remote_copy(src_ref=ins[k].at[c], dst_ref=outs[k].at[slab, c], send_sem=s1.at[3 * k + j],
                                                recv_sem=r1.at[3 * k + j], device_id=to, device_id_type=MESH)

        def d2d(k, j, slab, half):
            return pltpu.make_async_remote_copy(src_ref=outs[k].at[slab, half], dst_ref=outs[k].at[slab, half],
                                                send_sem=s2.at[3 * k + j], recv_sem=r2.at[3 * k + j],
                                                device_id=sib, device_id_type=MESH)

        pend = []
        for k in range(n):
            own = pltpu.make_async_copy(ins[k], outs[k].at[me], loc.at[k])
            own.start()
            pend.append(own)
            for j, (px, py) in enumerate(chips):
                ici(k, j, me, (px, py, c)).start()
        for k in range(n):
            for j, (px, py) in enumerate(chips):
                ici(k, j, 2 * px + py, (px, py, c)).wait_recv()
                d2d(k, j, 2 * px + py, c).start()
        for k in range(n):
            for j, (px, py) in enumerate(chips):
                d2d(k, j, 2 * px + py, 1 - c).wait_recv()
        for k in range(n):
            pend[k].wait()
            for j, (px, py) in enumerate(chips):
                ici(k, j, me, (px, py, c)).wait_send()
                d2d(k, j, 2 * px + py, c).wait_send()

    return pl.pallas_call(
        body, name=name, out_shape=[jax.ShapeDtypeStruct((4,) + a.shape, a.dtype) for a in shards],
        in_specs=[ANY] * n, out_specs=[ANY] * n,
        scratch_shapes=[pltpu.SemaphoreType.DMA((3 * n,))] * 4 + [pltpu.SemaphoreType.DMA((n,))],
        compiler_params=_params(),
    )(*shards)


def _pair_exchange(parts, name):
    n = len(parts)

    def body(*refs):
        ins, mine, sib = refs[:n], refs[n:2 * n], refs[2 * n:3 * n]
        send_sems, recv_sems, loc = refs[3 * n:]
        x, y, c = _place()
        cps, locs = [], []
        for k in range(n):
            for p in range(4):
                lc = pltpu.make_async_copy(ins[k].at[p, c], mine[k].at[p], loc.at[4 * k + p])
                lc.start()
                locs.append(lc)
                cp = pltpu.make_async_remote_copy(src_ref=ins[k].at[p, 1 - c], dst_ref=sib[k].at[p],
                                                  send_sem=send_sems.at[4 * k + p], recv_sem=recv_sems.at[4 * k + p],
                                                  device_id=(x, y, 1 - c), device_id_type=MESH)
                cp.start()
                cps.append(cp)
        for cp in cps:
            cp.wait()
        for lc in locs:
            lc.wait()

    half = [jax.ShapeDtypeStruct((4,) + a.shape[2:], a.dtype) for a in parts]
    return pl.pallas_call(
        body, name=name, out_shape=half + half, in_specs=[ANY] * n, out_specs=[ANY] * (2 * n),
        scratch_shapes=[pltpu.SemaphoreType.DMA((4 * n,))] * 3,
        compiler_params=_params(),
    )(*parts)


def _add2(a, b, name):
    P4, Rh, C = a.shape
    tm = _row_tile(Rh, C * 4, 1 << 20, 16)

    def body(a_ref, b_ref, o_ref):
        o_ref[...] = (a_ref[...].astype(F32) + b_ref[...].astype(F32)).astype(BF16)

    spec = pl.BlockSpec((1, tm, C), lambda p, i: (p, i, 0))
    return pl.pallas_call(
        body, name=name, grid=(P4, Rh // tm), in_specs=[spec, spec], out_specs=spec,
        out_shape=jax.ShapeDtypeStruct(a.shape, BF16), compiler_params=_params(("parallel", "parallel")),
    )(a, b)


def _sibling_merge(halves, name):
    n = len(halves)

    def body(*refs):
        ins, outs = refs[:n], refs[n:2 * n]
        send_sems, recv_sems, loc = refs[2 * n:]
        x, y, c = _place()
        cps, locs = [], []
        for k in range(n):
            lc = pltpu.make_async_copy(ins[k], outs[k].at[c], loc.at[k])
            lc.start()
            locs.append(lc)
            cp = pltpu.make_async_remote_copy(src_ref=ins[k], dst_ref=outs[k].at[c], send_sem=send_sems.at[k],
                                              recv_sem=recv_sems.at[k], device_id=(x, y, 1 - c), device_id_type=MESH)
            cp.start()
            cps.append(cp)
        for k in range(n):
            pltpu.make_async_remote_copy(src_ref=ins[k], dst_ref=outs[k].at[1 - c], send_sem=send_sems.at[k],
                                         recv_sem=recv_sems.at[k], device_id=(x, y, 1 - c), device_id_type=MESH).wait_recv()
        for k in range(n):
            cps[k].wait_send()
            locs[k].wait()

    return pl.pallas_call(
        body, name=name, out_shape=[jax.ShapeDtypeStruct((2,) + a.shape, a.dtype) for a in halves],
        in_specs=[ANY] * n, out_specs=[ANY] * n,
        scratch_shapes=[pltpu.SemaphoreType.DMA((n,))] * 3,
        compiler_params=_params(),
    )(*halves)


def _sum4(parts, name):
    _, R, C = parts.shape
    tm = _row_tile(R, C * 4, 1 << 20, 16)

    def body(p_ref, o_ref):
        acc = p_ref[0].astype(F32)
        for d in range(1, 4):
            acc = acc + p_ref[d].astype(F32)
        o_ref[...] = acc

    return pl.pallas_call(
        body, name=name, grid=(R // tm,), in_specs=[pl.BlockSpec((4, tm, C), lambda i: (0, i, 0))],
        out_specs=pl.BlockSpec((tm, C), lambda i: (i, 0)), out_shape=jax.ShapeDtypeStruct((R, C), F32),
        compiler_params=_params(("parallel",)),
    )(parts)


def kernel(x, c, positions, w_ada, b_ada, w_in, g_q_a, w_q_b, g_kv_a, w_kv_b, w_o_a, w_conv, w_o_b, w_o, ln1_g, ln1_b, w_ffn_in, w_ffn_out, ln2_g, ln2_b, loss_target, m_w_ada, m_b_ada, m_w_in, m_g_q_a, m_w_q_b, m_g_kv_a, m_w_kv_b, m_w_o_a, m_w_conv, m_w_o_b, m_w_o, m_ln1_g, m_ln1_b, m_w_ffn_in, m_w_ffn_out, m_ln2_g, m_ln2_b, v_w_ada, v_b_ada, v_w_in, v_g_q_a, v_w_q_b, v_g_kv_a, v_w_kv_b, v_w_o_a, v_w_conv, v_w_o_b, v_w_o, v_ln1_g, v_ln1_b, v_w_ffn_in, v_w_ffn_out, v_ln2_g, v_ln2_b):
    S, D = x.shape[1], x.shape[2]
    F = w_ffn_out.shape[1] * 4
    ax, ay, ac = _place()
    chip = 2 * ax + ay
    dev = 4 * ax + 2 * ay + ac
    x2, tgt = x[0], loss_target[0]
    w_ada2, w_in2, w_q_b2, w_kv_b2 = w_ada[0], w_in[0], w_q_b[0], w_kv_b[0]
    w_o_a2, w_o_b2, w_o2, w_ffn_in2, w_ffn_out2 = w_o_a[0], w_o_b[0], w_o[0], w_ffn_in[0], w_ffn_out[0]
    NA = w_ada2.shape[1]
    CW = w_conv.shape[2]

    inv_freq = 1.0 / (ROPE_THETA ** (jnp.arange(0, QK_ROPE, 2, dtype=F32) / QK_ROPE))
    ang = positions[0].astype(F32)[:, None] * inv_freq
    cos, sin = jnp.cos(ang), jnp.sin(ang)
    z32, z64, z96 = jnp.zeros((S, 32), F32), jnp.zeros((S, 64), F32), jnp.zeros((S, 96), F32)
    tab = jnp.concatenate([cos, cos, z64, -sin, z96, z32, sin, z64], axis=1)

    def halves(a):
        return a.reshape(2, a.shape[0] // 2, a.shape[1])

    shards = [halves(w.astype(BF16)) for w in (w_in2, w_q_b2, w_kv_b2, w_o_a2, w_o_b2, w_o2, w_ffn_in2, w_ffn_out2)]
    g_in, g_qb, g_kvb, g_oa, g_ob, g_o, g_fi, g_fo = (
        g.reshape(4, 2 * g.shape[2], g.shape[3]) for g in _gather2(shards, "gather_weights"))

    def cols(g):
        return jnp.transpose(g, (1, 0, 2)).reshape(g.shape[1], 4 * g.shape[2])

    W_in = cols(g_in)
    n_qkv = Q_LORA + KV_LORA + QK_ROPE
    W_qkv = jnp.pad(W_in[:, :n_qkv], ((0, 0), (0, QKV_A - n_qkv)))
    W_conv = W_in[:, n_qkv:n_qkv + 3 * D]
    W_gate = W_in[:, n_qkv + 3 * D:]
    W_qb = jnp.pad(cols(g_qb).reshape(Q_LORA, N_HEADS, QK_NOPE + QK_ROPE),
                   ((0, 0), (0, 0), (0, QK_PAD - QK_NOPE - QK_ROPE))).reshape(Q_LORA, N_HEADS * QK_PAD)
    W_kvb = cols(g_kvb)
    W_oa, W_ob, W_o = (g.reshape(-1, D) for g in (g_oa, g_ob, g_o))
    W_fi = cols(g_fi)
    W_fo = g_fo.reshape(F, D)

    c_all = _all_gather8(c, "gather_c").reshape(8, D)
    wconv_all = _all_gather8(w_conv[0], "gather_wconv")
    w_conv_full = jnp.transpose(wconv_all[0::2], (1, 0, 2)).reshape(3, D)
    b_sh = lax.dynamic_slice(b_ada, (0, chip * NA), (1, NA))
    mod_sh = _ada_fwd(c_all, w_ada2, b_sh)
    mod_all = _all_gather8(mod_sh, "gather_mod")
    mod = lax.dynamic_slice(mod_all[0::2], (0, dev, 0), (4, 1, NA)).reshape(6, D)
    shift1, scale1, gate1, shift2, scale2, gate2 = (mod[k:k + 1] for k in range(6))

    u = _modulate(x2, scale1, shift1, "modulate1")
    pq = _matmul(u, W_qkv, "nn", F32, "proj_qkv")
    pc = _matmul(u, W_conv, "nn", F32, "proj_conv")
    pg = _matmul(u, W_gate, "nn", F32, "proj_gate")
    rq, rkv, kr = _rms_fwd(pq, tab, g_q_a, g_kv_a)
    q = _q_rope(_matmul(rq, W_qb, "nn", F32, "q_b"), tab)
    kv = _matmul(rkv, W_kvb, "nn", BF16, "kv_b")
    o, lse = _attn_fwd(q, kv, kr)
    y_a = _matmul(o, W_oa, "nn", F32, "o_a")
    hb = _conv_fwd(pc, w_conv_full)
    y_b = _matmul(hb, W_ob, "nn", F32, "o_b")
    merged = _merge_fwd(y_a, y_b, pg)
    mix = _matmul(merged, W_o, "nn", F32, "w_o")
    x1, u2 = _ln1_fwd(x2, mix, gate1, ln1_g, ln1_b, scale2, shift2)
    hh = _matmul(u2, W_fi, "nn", F32, "ffn_in")
    act = _swiglu_fwd(hh)
    ffn = _matmul(act, W_fo, "nn", F32, "ffn_out")

    dffn, dx1a, loss_acc, d_ln2_g, d_ln2_b, d_gate2 = _ln2_loss_bwd(x1, ffn, gate2, ln2_g, ln2_b, tgt)
    loss = lax.psum(loss_acc[0, 0], ("x", "y", "c"))
    dW_fo = _matmul(act, dffn, "tn", BF16, "d_w_ffn_out")
    dact = _matmul(dffn, W_fo, "nt", F32, "d_act")
    dhh = _swiglu_bwd(dact, hh)
    dW_fi = _matmul(u2, dhh, "tn", BF16, "d_w_ffn_in")
    du2 = _matmul(dhh, W_fi, "nt", F32, "d_u2")
    dmix, dxa, d_shift2, d_scale2, d_ln1_g, d_ln1_b, d_gate1 = _ln1_bwd(x2, mix, dx1a, du2, gate1, ln1_g, ln1_b, scale2)
    dW_o = _matmul(merged, dmix, "tn", BF16, "d_w_o")
    dmerged = _matmul(dmix, W_o, "nt", F32, "d_merged")
    dy_a, dy_b, dgate = _merge_bwd(dmerged, y_a, y_b, pg)
    dW_oa = _matmul(o, dy_a, "tn", BF16, "d_w_o_a")
    do = _matmul(dy_a, W_oa, "nt", BF16, "d_o")
    dW_ob = _matmul(hb, dy_b, "tn", BF16, "d_w_o_b")
    dhb = _matmul(dy_b, W_ob, "nt", F32, "d_hb")
    dconv, d_wconv = _conv_bwd(dhb, pc, w_conv_full)
    dq, dkv, dkr = _attn_bwd(q, kv, kr, do, o, lse, tab)
    dW_qb = _matmul(rq, dq, "tn", BF16, "d_w_q_b")
    d_rq = _matmul(dq, W_qb, "nt", F32, "d_rq")
    dW_kvb = _matmul(rkv, dkv, "tn", BF16, "d_w_kv_b")
    d_rkv = _matmul(dkv, W_kvb, "nt", F32, "d_rkv")
    dqkv, d_g_q, d_g_kv = _rms_bwd(d_rq, d_rkv, pq, dkr, g_q_a, g_kv_a)
    dW_qkv = _matmul(u, dqkv, "tn", BF16, "d_w_qkv")
    dW_conv = _matmul(u, dconv, "tn", BF16, "d_w_conv")
    dW_gate = _matmul(u, dgate, "tn", BF16, "d_w_gate")
    du = _matmul(dqkv, W_qkv, "nt", F32, "d_u_qkv")
    du = _matmul(dconv, W_conv, "nt", F32, "d_u_conv", add=du)
    du = _matmul(dgate, W_gate, "nt", F32, "d_u_gate", add=du)
    grad_x, d_shift1, d_scale1 = _dx_final(dxa, du, x2, scale1)

    def pad_d(v):
        return jnp.pad(v, ((0, 0), (0, D - v.shape[1])))

    small = jnp.concatenate([d_ln1_g, d_ln1_b, d_ln2_g, d_ln2_b, pad_d(d_g_q), pad_d(d_g_kv), d_wconv,
                             d_shift1, d_scale1, d_gate1, d_shift2, d_scale2, d_gate2, jnp.zeros((1, D), F32)], axis=0)
    small_all = _all_gather8(small, "gather_small")
    small_sum = _sum8(small_all)
    g_ln1_g, g_ln1_b, g_ln2_g, g_ln2_b = (small_sum[k:k + 1] for k in range(4))
    g_g_q, g_g_kv = small_sum[4:5, :Q_LORA], small_sum[5:6, :KV_LORA]
    g_wconv = lax.dynamic_slice(small_sum[6:9], (0, chip * CW), (3, CW))
    g_b_ada = small_sum[9:15].reshape(1, 6 * D)
    dmod_all = small_all[:, 9:15, :].reshape(8, 6 * D)
    g_w_ada = _ada_bwd(c_all, lax.dynamic_slice(dmod_all, (0, chip * NA), (8, NA)))

    def uncols(g):
        return jnp.transpose(g.reshape(g.shape[0], 4, g.shape[1] // 4), (1, 0, 2))

    dW_in = jnp.concatenate([dW_qkv[:, :n_qkv], dW_conv, dW_gate], axis=1)
    dW_qb_u = dW_qb.reshape(Q_LORA, N_HEADS, QK_PAD)[:, :, :QK_NOPE + QK_ROPE].reshape(Q_LORA, -1)
    parts = [uncols(dW_in), uncols(dW_qb_u), uncols(dW_kvb), dW_oa.reshape(4, -1, D), dW_ob.reshape(4, -1, D),
             dW_o.reshape(4, -1, D), uncols(dW_fi), dW_fo.reshape(4, -1, D)]
    names = ["w_in", "w_q_b", "w_kv_b", "w_o_a", "w_o_b", "w_o", "w_ffn_in", "w_ffn_out"]
    parts = [p.reshape(4, 2, p.shape[1] // 2, p.shape[2]) for p in parts]
    ex = _pair_exchange(parts, "pair_grads")
    pre = [_add2(a, b, "add2_" + nm) for a, b, nm in zip(ex[:8], ex[8:], names)]
    recv = _chip_exchange(pre, "scatter_grads", scatter=True)
    fin = [_sum4(r, "sum4_" + nm) for r, nm in zip(recv, names)]
    full = [f.reshape(2 * f.shape[1], f.shape[2]) for f in _sibling_merge(fin, "sibling_grads")]

    big = {}
    ws = dict(w_in=(w_in2, m_w_in[0], v_w_in[0]), w_q_b=(w_q_b2, m_w_q_b[0], v_w_q_b[0]),
              w_kv_b=(w_kv_b2, m_w_kv_b[0], v_w_kv_b[0]), w_o_a=(w_o_a2, m_w_o_a[0], v_w_o_a[0]),
              w_o_b=(w_o_b2, m_w_o_b[0], v_w_o_b[0]), w_o=(w_o2, m_w_o[0], v_w_o[0]),
              w_ffn_in=(w_ffn_in2, m_w_ffn_in[0], v_w_ffn_in[0]), w_ffn_out=(w_ffn_out2, m_w_ffn_out[0], v_w_ffn_out[0]))
    for nm, g_ in zip(names, full):
        w_, m_, v_ = ws[nm]
        big[nm] = _adam("adam_" + nm, w_, m_, v_, [g_])
    big["w_ada"] = _adam("adam_w_ada", w_ada2, m_w_ada[0], v_w_ada[0], [g_w_ada])
    sm = {}
    for nm, w_, m_, v_, g_ in [("b_ada", b_ada, m_b_ada, v_b_ada, g_b_ada), ("g_q_a", g_q_a, m_g_q_a, v_g_q_a, g_g_q),
                               ("g_kv_a", g_kv_a, m_g_kv_a, v_g_kv_a, g_g_kv),
                               ("w_conv", w_conv[0], m_w_conv[0], v_w_conv[0], g_wconv),
                               ("ln1_g", ln1_g, m_ln1_g, v_ln1_g, g_ln1_g), ("ln1_b", ln1_b, m_ln1_b, v_ln1_b, g_ln1_b),
                               ("ln2_g", ln2_g, m_ln2_g, v_ln2_g, g_ln2_g), ("ln2_b", ln2_b, m_ln2_b, v_ln2_b, g_ln2_b)]:
        sm[nm] = (g_,) + tuple(_adam_small("adam_" + nm, w_, m_, v_, g_))

    order = ["w_ada", "b_ada", "w_in", "g_q_a", "w_q_b", "g_kv_a", "w_kv_b", "w_o_a", "w_conv", "w_o_b", "w_o",
             "ln1_g", "ln1_b", "w_ffn_in", "w_ffn_out", "ln2_g", "ln2_b"]
    lead = {"b_ada", "g_q_a", "g_kv_a", "ln1_g", "ln1_b", "ln2_g", "ln2_b"}

    def leaf(nm, k):
        val = big[nm][k] if nm in big else sm[nm][k]
        return val if nm in lead else val[None]

    outs = [loss, grad_x[None]]
    for k in range(4):
        outs += [leaf(nm, k) for nm in order]
    return tuple(outs)
```

```python
import functools

import jax
import jax.numpy as jnp
from jax import lax
from jax.experimental import pallas as pl
from jax.experimental.pallas import tpu as pltpu

F32, BF16 = jnp.float32, jnp.bfloat16
N_HEADS, QK_NOPE, QK_ROPE, V_HEAD = 16, 128, 64, 128
Q_LORA, KV_LORA = 512, 512
QK_PAD = 256
QKV_A = 1152
CHUNK_SHIFT = 6
ATTN_SCALE = (QK_NOPE + QK_ROPE) ** -0.5
ROPE_THETA = 10000.0
ALPHA = 2.0 ** 0.25
LN_EPS, RMS_EPS = 1e-5, 1e-6
ADAM_LR, ADAM_B1, ADAM_B2, ADAM_EPS, ADAM_WD, ADAM_STEP = 0.001, 0.9, 0.999, 1e-08, 0.01, 10
ADAM_C1 = 1.0 - ADAM_B1 ** ADAM_STEP
ADAM_C2 = 1.0 - ADAM_B2 ** ADAM_STEP
VMEM_LIMIT = 56 * 1024 * 1024
MESH = pl.DeviceIdType.MESH
ANY = pl.BlockSpec(memory_space=pl.ANY)
NT = (((1,), (1,)), ((), ()))
TN = (((0,), (0,)), ((), ()))
NN = (((1,), (0,)), ((), ()))


def _params(sem=None):
    return pltpu.CompilerParams(dimension_semantics=sem, vmem_limit_bytes=VMEM_LIMIT)


def _pick(n, cands=(1024, 512, 384, 256, 128)):
    for t in cands:
        if n % t == 0:
            return t
    return n


def _row_tile(rows, row_bytes, budget, mult=8):
    best = mult
    for t in range(mult, rows + 1, mult):
        if rows % t == 0 and t * row_bytes <= budget:
            best = t
    return best


def _sigmoid(x):
    return jax.nn.sigmoid(x)


def _matmul(a, b, mode, out_dtype, name, add=None):
    if mode == "nn":
        (M, K), N, dims = a.shape, b.shape[1], NN
    elif mode == "nt":
        (M, K), N, dims = a.shape, b.shape[0], NT
    else:
        (K, M), N, dims = a.shape, b.shape[1], TN
    tm, tn, tk = _pick(M), _pick(N), _pick(K)
    nk = K // tk
    a_spec = (pl.BlockSpec((tk, tm), lambda i, j, k: (k, i)) if mode == "tn"
              else pl.BlockSpec((tm, tk), lambda i, j, k: (i, k)))
    b_spec = (pl.BlockSpec((tn, tk), lambda i, j, k: (j, k)) if mode == "nt"
              else pl.BlockSpec((tk, tn), lambda i, j, k: (k, j)))
    o_spec = pl.BlockSpec((tm, tn), lambda i, j, k: (i, j))
    has_add = add is not None

    def body(*refs):
        a_ref, b_ref = refs[0], refs[1]
        add_ref = refs[2] if has_add else None
        o_ref, acc_ref = refs[-2], refs[-1]
        k = pl.program_id(2)

        @pl.when(k == 0)
        def _():
            acc_ref[...] = jnp.zeros_like(acc_ref)

        acc_ref[...] += lax.dot_general(a_ref[...], b_ref[...], dims, preferred_element_type=F32)

        @pl.when(k == nk - 1)
        def _():
            r = acc_ref[...]
            if has_add:
                r = r + add_ref[...]
            o_ref[...] = r.astype(o_ref.dtype)

    ins = [a, b] + ([add] if has_add else [])
    in_specs = [a_spec, b_spec] + ([o_spec] if has_add else [])
    return pl.pallas_call(
        body, name=name, grid=(M // tm, N // tn, nk),
        in_specs=in_specs, out_specs=o_spec,
        out_shape=jax.ShapeDtypeStruct((M, N), out_dtype),
        scratch_shapes=[pltpu.VMEM((tm, tn), F32)],
        compiler_params=_params(("parallel", "parallel", "arbitrary")),
    )(*ins)


def _rows(body, name, n_rows, tm, ins, outs, accs=()):
    grid = (n_rows // tm,)
    per8 = tm // 8
    last8 = n_rows // 8 - 1
    arrays, in_specs = [], []
    for spec in ins:
        kind, arr = spec[0], spec[1]
        arrays.append(arr)
        if kind == "row":
            _, _, cb, w = spec
            in_specs.append(pl.BlockSpec((tm, w), lambda i, cb=cb: (i, cb)))
        elif kind == "full":
            in_specs.append(pl.BlockSpec(arr.shape, lambda i, nd=arr.ndim: (0,) * nd))
        elif kind == "prev":
            _, _, cb, w = spec
            in_specs.append(pl.BlockSpec((8, w), lambda i, cb=cb: (jnp.maximum(i * per8 - 1, 0), cb)))
        else:
            _, _, cb, w = spec
            in_specs.append(pl.BlockSpec((8, w), lambda i, cb=cb: (jnp.minimum((i + 1) * per8, last8), cb)))
    out_shape = [jax.ShapeDtypeStruct((n_rows, w), dt) for (w, dt) in outs]
    out_specs = [pl.BlockSpec((tm, w), lambda i: (i, 0)) for (w, _) in outs]
    out_shape += [jax.ShapeDtypeStruct(s, F32) for s in accs]
    out_specs += [pl.BlockSpec(s, lambda i, nd=len(s): (0,) * nd) for s in accs]
    n_in, n_out = len(ins), len(outs)

    def kernel_body(*refs):
        body(pl.program_id(0), refs[:n_in], refs[n_in:n_in + n_out], refs[n_in + n_out:])

    res = pl.pallas_call(
        kernel_body, name=name, grid=grid, in_specs=in_specs, out_specs=out_specs, out_shape=out_shape,
        compiler_params=_params(("arbitrary",)),
    )(*arrays)
    return res


def _acc_add(i, ref, val):
    @pl.when(i == 0)
    def _():
        ref[...] = val

    @pl.when(i > 0)
    def _():
        ref[...] += val


def _rope(t, tab, sign):
    c, sa, sb = tab[:, 0:128], tab[:, 128:256], tab[:, 256:384]
    rot = pltpu.roll(t, 96, 1) * sa + pltpu.roll(t, 32, 1) * sb
    return t * c + rot if sign > 0 else t * c - rot


def _ln_stats(r):
    mu = jnp.mean(r, axis=-1, keepdims=True)
    d = r - mu
    var = jnp.mean(d * d, axis=-1, keepdims=True)
    rstd = lax.rsqrt(var + LN_EPS)
    return d * rstd, rstd


def _ln_bwd(dxh, xh, rstd):
    m1 = jnp.mean(dxh, axis=-1, keepdims=True)
    m2 = jnp.mean(dxh * xh, axis=-1, keepdims=True)
    return rstd * (dxh - m1 - xh * m2)


def _modulate(x, scale, shift, name):
    S, D = x.shape

    def body(i, ins, outs, accs):
        outs[0][...] = (ins[0][...] * (1.0 + ins[1][...]) + ins[2][...]).astype(BF16)

    return _rows(body, name, S, _pick(S, (256, 128)), [("row", x, 0, D), ("full", scale), ("full", shift)], [(D, BF16)])[0]


def _rms_fwd(pq, tab, g_q, g_kv):
    S = pq.shape[0]

    def body(i, ins, outs, accs):
        pq_ref, tab_ref, gq_ref, gkv_ref = ins

        def rms(x, g):
            return x * lax.rsqrt(jnp.mean(x * x, axis=-1, keepdims=True) + RMS_EPS) * g

        outs[0][...] = rms(pq_ref[:, 0:Q_LORA], gq_ref[...]).astype(BF16)
        outs[1][...] = rms(pq_ref[:, Q_LORA:Q_LORA + KV_LORA], gkv_ref[...]).astype(BF16)
        outs[2][...] = _rope(pq_ref[:, Q_LORA + KV_LORA:QKV_A], tab_ref[...], 1).astype(BF16)

    return _rows(body, "rms_fwd", S, _pick(S, (256, 128)),
                 [("row", pq, 0, QKV_A), ("row", tab, 0, 384), ("full", g_q), ("full", g_kv)],
                 [(Q_LORA, BF16), (KV_LORA, BF16), (128, BF16)])


def _q_rope(q, tab):
    S, W = q.shape

    def body(i, ins, outs, accs):
        q_ref, tab_ref = ins
        t = tab_ref[...]
        for h in range(N_HEADS):
            lo = h * QK_PAD
            outs[0][:, lo:lo + 128] = q_ref[:, lo:lo + 128].astype(BF16)
            outs[0][:, lo + 128:lo + 256] = _rope(q_ref[:, lo + 128:lo + 256], t, 1).astype(BF16)

    return _rows(body, "q_rope", S, _pick(S, (256, 128)), [("row", q, 0, W), ("row", tab, 0, 384)], [(W, BF16)])[0]


def _allowed(q0, k0, bq):
    row = q0 + lax.broadcasted_iota(jnp.int32, (bq, bq), 0)
    col = k0 + lax.broadcasted_iota(jnp.int32, (bq, bq), 1)
    return (col >> CHUNK_SHIFT) <= (row >> CHUNK_SHIFT)


def _attn_fwd(q, kv, kr):
    S = q.shape[0]
    bq = min(256, S)
    nq = S // bq

    def body(q_ref, kn_ref, v_ref, kr_ref, o_ref, lse_ref):
        qi = pl.program_id(1)
        qv = q_ref[...]

        def step(j, carry):
            m, l, acc = carry
            off = pl.multiple_of(j * bq, bq)
            k = jnp.concatenate([kn_ref[pl.ds(off, bq), :], kr_ref[pl.ds(off, bq), :]], axis=1)
            s = lax.dot_general(qv, k, NT, preferred_element_type=F32) * ATTN_SCALE
            s = jnp.where(_allowed(qi * bq, off, bq), s, -1e30)
            m_new = jnp.maximum(m, jnp.max(s, axis=1, keepdims=True))
            a = jnp.exp(m - m_new)
            p = jnp.exp(s - m_new)
            l = a * l + jnp.sum(p, axis=1, keepdims=True)
            acc = a * acc + jnp.dot(p.astype(BF16), v_ref[pl.ds(off, bq), :], preferred_element_type=F32)
            return m_new, l, acc

        init = (jnp.full((bq, 1), -1e30, F32), jnp.zeros((bq, 1), F32), jnp.zeros((bq, V_HEAD), F32))
        m, l, acc = lax.fori_loop(0, qi + 1, step, init)
        o_ref[...] = (acc / l).astype(BF16)
        lse_ref[0] = m + jnp.log(l)

    return pl.pallas_call(
        body, name="attn_fwd", grid=(N_HEADS, nq),
        in_specs=[pl.BlockSpec((bq, QK_PAD), lambda h, i: (i, h)),
                  pl.BlockSpec((S, 128), lambda h, i: (0, 2 * h)),
                  pl.BlockSpec((S, 128), lambda h, i: (0, 2 * h + 1)),
                  pl.BlockSpec((S, 128), lambda h, i: (0, 0))],
        out_specs=[pl.BlockSpec((bq, V_HEAD), lambda h, i: (i, h)),
                   pl.BlockSpec((1, bq, 1), lambda h, i: (h, i, 0))],
        out_shape=[jax.ShapeDtypeStruct((S, N_HEADS * V_HEAD), BF16),
                   jax.ShapeDtypeStruct((N_HEADS, S, 1), F32)],
        compiler_params=_params(("arbitrary", "arbitrary")),
    )(q, kv, kv, kr)


def _attn_bwd(q, kv, kr, do, o, lse, tab):
    S = q.shape[0]
    bq = min(256, S)
    nq = S // bq

    def body(q_ref, kn_ref, v_ref, kr_ref, do_ref, o_ref, lse_ref, tab_ref, dq_ref, dkv_ref, dkr_ref,
             dq_acc, dk_acc, dv_acc):
        h = pl.program_id(0)
        dq_acc[...] = jnp.zeros_like(dq_acc)
        dk_acc[...] = jnp.zeros_like(dk_acc)
        dv_acc[...] = jnp.zeros_like(dv_acc)

        def kv_step(j, _):
            offj = pl.multiple_of(j * bq, bq)
            rows_j = pl.ds(offj, bq)
            k = jnp.concatenate([kn_ref[rows_j, :], kr_ref[rows_j, :]], axis=1)
            v = v_ref[rows_j, :]

            def q_step(i, _):
                offi = pl.multiple_of(i * bq, bq)
                rows_i = pl.ds(offi, bq)
                qv, dov = q_ref[rows_i, :], do_ref[rows_i, :]
                delta = jnp.sum(dov.astype(F32) * o_ref[rows_i, :].astype(F32), axis=1, keepdims=True)
                s = lax.dot_general(qv, k, NT, preferred_element_type=F32) * ATTN_SCALE
                s = jnp.where(_allowed(offi, offj, bq), s, -1e30)
                p = jnp.exp(s - lse_ref[0, rows_i, :])
                dv_acc[rows_j, :] += lax.dot_general(p.astype(BF16), dov, TN, preferred_element_type=F32)
                dp = lax.dot_general(dov, v, NT, preferred_element_type=F32)
                ds = (p * (dp - delta) * ATTN_SCALE).astype(BF16)
                dk_acc[rows_j, :] += lax.dot_general(ds, qv, TN, preferred_element_type=F32)
                dq_acc[rows_i, :] += jnp.dot(ds, k, preferred_element_type=F32)
                return 0

            lax.fori_loop(j, nq, q_step, 0)
            return 0

        lax.fori_loop(0, nq, kv_step, 0)

        for r in range(nq):
            rows = slice(r * bq, (r + 1) * bq)
            dq_ref[rows, 0:128] = dq_acc[rows, 0:128].astype(BF16)
            dq_ref[rows, 128:256] = _rope(dq_acc[rows, 128:256], tab_ref[rows, :], -1).astype(BF16)
        dkv_ref[:, 0:128] = dk_acc[:, 0:128].astype(BF16)
        dkv_ref[:, 128:256] = dv_acc[...].astype(BF16)

        @pl.when(h == 0)
        def _():
            dkr_ref[...] = dk_acc[:, 128:256]

        @pl.when(h > 0)
        def _():
            dkr_ref[...] += dk_acc[:, 128:256]

        @pl.when(h == N_HEADS - 1)
        def _():
            for r in range(nq):
                rows = slice(r * bq, (r + 1) * bq)
                dkr_ref[rows, :] = _rope(dkr_ref[rows, :], tab_ref[rows, :], -1)

    W = N_HEADS * QK_PAD
    return pl.pallas_call(
        body, name="attn_bwd", grid=(N_HEADS,),
        in_specs=[pl.BlockSpec((S, QK_PAD), lambda h: (0, h)),
                  pl.BlockSpec((S, 128), lambda h: (0, 2 * h)),
                  pl.BlockSpec((S, 128), lambda h: (0, 2 * h + 1)),
                  pl.BlockSpec((S, 128), lambda h: (0, 0)),
                  pl.BlockSpec((S, V_HEAD), lambda h: (0, h)),
                  pl.BlockSpec((S, V_HEAD), lambda h: (0, h)),
                  pl.BlockSpec((1, S, 1), lambda h: (h, 0, 0)),
                  pl.BlockSpec((S, 384), lambda h: (0, 0))],
        out_specs=[pl.BlockSpec((S, QK_PAD), lambda h: (0, h)),
                   pl.BlockSpec((S, QK_PAD), lambda h: (0, h)),
                   pl.BlockSpec((S, 128), lambda h: (0, 0))],
        out_shape=[jax.ShapeDtypeStruct((S, W), BF16), jax.ShapeDtypeStruct((S, W), BF16),
                   jax.ShapeDtypeStruct((S, 128), F32)],
        scratch_shapes=[pltpu.VMEM((S, QK_PAD), F32), pltpu.VMEM((S, QK_PAD), F32), pltpu.VMEM((S, V_HEAD), F32)],
        compiler_params=_params(("arbitrary",)),
    )(q, kv, kv, kr, do, o, lse, tab)


def _shift_down(cur, prev8, i, n):
    tm = cur.shape[0]
    prev8 = jnp.where(i == 0, jnp.zeros_like(prev8), prev8)
    full = jnp.concatenate([prev8, cur], axis=0)
    return pltpu.roll(full, n, 0)[8:8 + tm, :]


def _shift_up(cur, next8, i, last, n):
    tm = cur.shape[0]
    next8 = jnp.where(i == last, jnp.zeros_like(next8), next8)
    full = jnp.concatenate([cur, next8], axis=0)
    return pltpu.roll(full, tm + 8 - n, 0)[0:tm, :]


def _conv_fwd(pc, w_conv):
    S, D = pc.shape[0], pc.shape[1] // 3
    tm = _pick(S, (256, 128))

    def body(i, ins, outs, accs):
        b_ref, c_ref, x_ref, cp_ref, xp_ref, w_ref = ins
        z = c_ref[...] * x_ref[...]
        zp = cp_ref[...] * xp_ref[...]
        cz = w_ref[0:1, :] * _shift_down(z, zp, i, 2) + w_ref[1:2, :] * _shift_down(z, zp, i, 1) + w_ref[2:3, :] * z
        outs[0][...] = (b_ref[...] * cz).astype(BF16)

    return _rows(body, "conv_fwd", S, tm,
                 [("row", pc, 0, D), ("row", pc, 1, D), ("row", pc, 2, D), ("prev", pc, 1, D), ("prev", pc, 2, D),
                  ("full", w_conv)], [(D, BF16)])[0]


def _conv_bwd(dhb, pc, w_conv):
    S, D = dhb.shape
    tm = _pick(S, (256, 128))
    last = S // tm - 1

    def body(i, ins, outs, accs):
        g_ref, b_ref, c_ref, x_ref, cp_ref, xp_ref, gn_ref, bn_ref, w_ref = ins
        w0, w1, w2 = w_ref[0:1, :], w_ref[1:2, :], w_ref[2:3, :]
        c, x, g = c_ref[...], x_ref[...], g_ref[...]
        z = c * x
        zp = cp_ref[...] * xp_ref[...]
        z1, z2 = _shift_down(z, zp, i, 1), _shift_down(z, zp, i, 2)
        cz = w0 * z2 + w1 * z1 + w2 * z
        dcz = g * b_ref[...]
        dczn = gn_ref[...] * bn_ref[...]
        dz = w2 * dcz + w1 * _shift_up(dcz, dczn, i, last, 1) + w0 * _shift_up(dcz, dczn, i, last, 2)
        outs[0][:, 0:D] = (g * cz).astype(BF16)
        outs[0][:, D:2 * D] = (dz * x).astype(BF16)
        outs[0][:, 2 * D:3 * D] = (dz * c).astype(BF16)
        dw = jnp.concatenate([jnp.sum(dcz * z2, axis=0, keepdims=True), jnp.sum(dcz * z1, axis=0, keepdims=True),
                              jnp.sum(dcz * z, axis=0, keepdims=True)], axis=0)
        _acc_add(i, accs[0], dw)

    return _rows(body, "conv_bwd", S, tm,
                 [("row", dhb, 0, D), ("row", pc, 0, D), ("row", pc, 1, D), ("row", pc, 2, D),
                  ("prev", pc, 1, D), ("prev", pc, 2, D), ("next", dhb, 0, D), ("next", pc, 0, D), ("full", w_conv)],
                 [(3 * D, BF16)], [(3, D)])


def _merge_fwd(y_a, y_b, pg):
    S, D = y_a.shape

    def body(i, ins, outs, accs):
        ya, yb, ga, gb = ins
        outs[0][...] = (_sigmoid(ga[...]) * ya[...] + _sigmoid(gb[...]) * yb[...]).astype(BF16)

    return _rows(body, "merge_fwd", S, _pick(S, (256, 128)),
                 [("row", y_a, 0, D), ("row", y_b, 0, D), ("row", pg, 0, D), ("row", pg, 1, D)], [(D, BF16)])[0]


def _merge_bwd(dm, y_a, y_b, pg):
    S, D = dm.shape

    def body(i, ins, outs, accs):
        d, ya, yb = ins[0][...], ins[1][...], ins[2][...]
        sa, sb = _sigmoid(ins[3][...]), _sigmoid(ins[4][...])
        outs[0][...] = (d * sa).astype(BF16)
        outs[1][...] = (d * sb).astype(BF16)
        outs[2][:, 0:D] = (d * ya * (sa * (1.0 - sa))).astype(BF16)
        outs[2][:, D:2 * D] = (d * yb * (sb * (1.0 - sb))).astype(BF16)

    return _rows(body, "merge_bwd", S, _pick(S, (256, 128)),
                 [("row", dm, 0, D), ("row", y_a, 0, D), ("row", y_b, 0, D), ("row", pg, 0, D), ("row", pg, 1, D)],
                 [(D, BF16), (D, BF16), (2 * D, BF16)])


def _ln1_fwd(x, mix, gate1, g, b, scale2, shift2):
    S, D = x.shape

    def body(i, ins, outs, accs):
        x_ref, mix_ref, gate_ref, g_ref, b_ref, sc_ref, sh_ref = ins
        xh, _ = _ln_stats(ALPHA * x_ref[...] + gate_ref[...] * mix_ref[...])
        x1 = xh * g_ref[...] + b_ref[...]
        outs[0][...] = x1
        outs[1][...] = (x1 * (1.0 + sc_ref[...]) + sh_ref[...]).astype(BF16)

    return _rows(body, "ln1_fwd", S, _pick(S, (256, 128)),
                 [("row", x, 0, D), ("row", mix, 0, D), ("full", gate1), ("full", g), ("full", b),
                  ("full", scale2), ("full", shift2)], [(D, F32), (D, BF16)])


def _swiglu_fwd(hh):
    S, F = hh.shape[0], hh.shape[1] // 2

    def body(i, ins, outs, accs):
        hg = ins[0][...]
        outs[0][...] = (hg * _sigmoid(hg) * ins[1][...]).astype(BF16)

    return _rows(body, "swiglu_fwd", S, _pick(S, (128,)), [("row", hh, 0, F), ("row", hh, 1, F)], [(F, BF16)])[0]


def _swiglu_bwd(dact, hh):
    S, F = dact.shape

    def body(i, ins, outs, accs):
        d, hg, hu = ins[0][...], ins[1][...], ins[2][...]
        sg = _sigmoid(hg)
        outs[0][:, 0:F] = (d * hu * (sg * (1.0 + hg * (1.0 - sg)))).astype(BF16)
        outs[0][:, F:2 * F] = (d * (hg * sg)).astype(BF16)

    return _rows(body, "swiglu_bwd", S, _pick(S, (128,)),
                 [("row", dact, 0, F), ("row", hh, 0, F), ("row", hh, 1, F)], [(2 * F, BF16)])[0]


def _ln2_loss_bwd(x1, ffn, gate2, g, b, target):
    S, D = x1.shape

    def body(i, ins, outs, accs):
        x1_ref, f_ref, gate_ref, g_ref, b_ref, t_ref = ins
        f = f_ref[...]
        xh, rstd = _ln_stats(ALPHA * x1_ref[...] + gate_ref[...] * f)
        e = xh * g_ref[...] + b_ref[...] - t_ref[...]
        dy = e * (1.0 / D)
        dr = _ln_bwd(dy * g_ref[...], xh, rstd)
        outs[0][...] = (gate_ref[...] * dr).astype(BF16)
        outs[1][...] = ALPHA * dr
        _acc_add(i, accs[0], jnp.full((1, 128), (0.5 / D) * jnp.sum(e * e), F32))
        _acc_add(i, accs[1], jnp.sum(dy * xh, axis=0, keepdims=True))
        _acc_add(i, accs[2], jnp.sum(dy, axis=0, keepdims=True))
        _acc_add(i, accs[3], jnp.sum(dr * f, axis=0, keepdims=True))

    return _rows(body, "ln2_loss_bwd", S, _pick(S, (256, 128)),
                 [("row", x1, 0, D), ("row", ffn, 0, D), ("full", gate2), ("full", g), ("full", b), ("row", target, 0, D)],
                 [(D, BF16), (D, F32)], [(1, 128), (1, D), (1, D), (1, D)])


def _ln1_bwd(x, mix, dx1a, du2, gate1, g, b, scale2):
    S, D = x.shape

    def body(i, ins, outs, accs):
        x_ref, mix_ref, da_ref, du_ref, gate_ref, g_ref, b_ref, sc_ref = ins
        mix, du = mix_ref[...], du_ref[...]
        xh, rstd = _ln_stats(ALPHA * x_ref[...] + gate_ref[...] * mix)
        x1 = xh * g_ref[...] + b_ref[...]
        dx1 = da_ref[...] + du * (1.0 + sc_ref[...])
        dr = _ln_bwd(dx1 * g_ref[...], xh, rstd)
        outs[0][...] = (gate_ref[...] * dr).astype(BF16)
        outs[1][...] = ALPHA * dr
        _acc_add(i, accs[0], jnp.sum(du, axis=0, keepdims=True))
        _acc_add(i, accs[1], jnp.sum(du * x1, axis=0, keepdims=True))
        _acc_add(i, accs[2], jnp.sum(dx1 * xh, axis=0, keepdims=True))
        _acc_add(i, accs[3], jnp.sum(dx1, axis=0, keepdims=True))
        _acc_add(i, accs[4], jnp.sum(dr * mix, axis=0, keepdims=True))

    return _rows(body, "ln1_bwd", S, _pick(S, (256, 128)),
                 [("row", x, 0, D), ("row", mix, 0, D), ("row", dx1a, 0, D), ("row", du2, 0, D),
                  ("full", gate1), ("full", g), ("full", b), ("full", scale2)],
                 [(D, BF16), (D, F32)], [(1, D)] * 5)


def _rms_bwd(d_rq, d_rkv, pq, dkr, g_q, g_kv):
    S = pq.shape[0]

    def body(i, ins, outs, accs):
        dq_ref, dkv_ref, pq_ref, dkr_ref, gq_ref, gkv_ref = ins

        def rms_bwd(dy, x, g):
            r = lax.rsqrt(jnp.mean(x * x, axis=-1, keepdims=True) + RMS_EPS)
            dyg = dy * g
            dx = r * dyg - x * (r * r * r) * jnp.mean(dyg * x, axis=-1, keepdims=True)
            return dx, jnp.sum(dy * (x * r), axis=0, keepdims=True)

        dxq, dgq = rms_bwd(dq_ref[...], pq_ref[:, 0:Q_LORA], gq_ref[...])
        dxkv, dgkv = rms_bwd(dkv_ref[...], pq_ref[:, Q_LORA:Q_LORA + KV_LORA], gkv_ref[...])
        outs[0][:, 0:Q_LORA] = dxq.astype(BF16)
        outs[0][:, Q_LORA:Q_LORA + KV_LORA] = dxkv.astype(BF16)
        outs[0][:, Q_LORA + KV_LORA:QKV_A] = dkr_ref[...].astype(BF16)
        _acc_add(i, accs[0], dgq)
        _acc_add(i, accs[1], dgkv)

    return _rows(body, "rms_bwd", S, _pick(S, (256, 128)),
                 [("row", d_rq, 0, Q_LORA), ("row", d_rkv, 0, KV_LORA), ("row", pq, 0, QKV_A), ("row", dkr, 0, 128),
                  ("full", g_q), ("full", g_kv)], [(QKV_A, BF16)], [(1, Q_LORA), (1, KV_LORA)])


def _dx_final(dxa, du, x, scale1):
    S, D = x.shape

    def body(i, ins, outs, accs):
        du = ins[1][...]
        outs[0][...] = ins[0][...] + du * (1.0 + ins[3][...])
        _acc_add(i, accs[0], jnp.sum(du, axis=0, keepdims=True))
        _acc_add(i, accs[1], jnp.sum(du * ins[2][...], axis=0, keepdims=True))

    return _rows(body, "dx_final", S, _pick(S, (256, 128)),
                 [("row", dxa, 0, D), ("row", du, 0, D), ("row", x, 0, D), ("full", scale1)],
                 [(D, F32)], [(1, D), (1, D)])


def _ada_fwd(c_all, w, bias):
    B, D = c_all.shape
    NA = w.shape[1]
    tn = _pick(NA, (512, 256, 128))

    def body(c_ref, w_ref, b_ref, o_ref):
        cv = c_ref[...]
        ca = (cv * _sigmoid(cv)).astype(BF16)
        o_ref[...] = jnp.dot(ca, w_ref[...].astype(BF16), preferred_element_type=F32) + b_ref[...]

    return pl.pallas_call(
        body, name="ada_fwd", grid=(NA // tn,),
        in_specs=[pl.BlockSpec((B, D), lambda j: (0, 0)), pl.BlockSpec((D, tn), lambda j: (0, j)),
                  pl.BlockSpec((1, tn), lambda j: (0, j))],
        out_specs=pl.BlockSpec((B, tn), lambda j: (0, j)),
        out_shape=jax.ShapeDtypeStruct((B, NA), F32),
        compiler_params=_params(("arbitrary",)),
    )(c_all, w, bias)


def _ada_bwd(c_all, dmod):
    B, D = c_all.shape
    NA = dmod.shape[1]
    tn = _pick(NA, (512, 256, 128))

    def body(c_ref, d_ref, o_ref):
        cv = c_ref[...]
        ca = (cv * _sigmoid(cv)).astype(BF16)
        o_ref[...] = lax.dot_general(ca, d_ref[...].astype(BF16), TN, preferred_element_type=F32)

    return pl.pallas_call(
        body, name="ada_bwd", grid=(NA // tn,),
        in_specs=[pl.BlockSpec((B, D), lambda j: (0, 0)), pl.BlockSpec((B, tn), lambda j: (0, j))],
        out_specs=pl.BlockSpec((D, tn), lambda j: (0, j)),
        out_shape=jax.ShapeDtypeStruct((D, NA), F32),
        compiler_params=_params(("arbitrary",)),
    )(c_all, dmod)


def _sum8(parts):
    _, R, N = parts.shape

    def body(p_ref, o_ref):
        acc = p_ref[0]
        for d in range(1, 8):
            acc = acc + p_ref[d]
        o_ref[...] = acc

    return pl.pallas_call(body, name="sum8", out_shape=jax.ShapeDtypeStruct((R, N), F32),
                          compiler_params=_params())(parts)


def _adam_math(w, g, m, v):
    m = ADAM_B1 * m + (1.0 - ADAM_B1) * g
    v = ADAM_B2 * v + (1.0 - ADAM_B2) * (g * g)
    delta = -ADAM_LR * ((m / ADAM_C1) / (jnp.sqrt(v / ADAM_C2) + ADAM_EPS) + ADAM_WD * w)
    return delta, m, v


def _adam(name, w, m, v, g_parts):
    R, C = w.shape
    tm = _row_tile(R, C * 4, 1 << 20)
    n = len(g_parts)

    def body(*refs):
        w_ref, m_ref, v_ref = refs[0], refs[1], refs[2]
        g = refs[3][...]
        for r in refs[4:3 + n]:
            g = g + r[...]
        g_ref, d_ref, nm_ref, nv_ref = refs[3 + n:]
        delta, nm, nv = _adam_math(w_ref[...], g, m_ref[...], v_ref[...])
        g_ref[...] = g
        d_ref[...] = delta
        nm_ref[...] = nm
        nv_ref[...] = nv

    spec = pl.BlockSpec((tm, C), lambda i: (i, 0))
    return pl.pallas_call(
        body, name=name, grid=(R // tm,), in_specs=[spec] * (3 + n), out_specs=[spec] * 4,
        out_shape=[jax.ShapeDtypeStruct((R, C), F32)] * 4, compiler_params=_params(("parallel",)),
    )(w, m, v, *g_parts)


def _adam_halves(name, w, m, v, mine, other, core):
    R, C = w.shape
    Rh = R // 2
    tm = _row_tile(Rh, C * 4, 1 << 20)
    nh = Rh // tm

    def body(c_ref, w_ref, m_ref, v_ref, a_ref, b_ref, g_ref, d_ref, nm_ref, nv_ref):
        g = jnp.where(pl.program_id(0) // nh == c_ref[0], a_ref[...], b_ref[...])
        delta, nm, nv = _adam_math(w_ref[...], g, m_ref[...], v_ref[...])
        g_ref[...] = g
        d_ref[...] = delta
        nm_ref[...] = nm
        nv_ref[...] = nv

    spec = pl.BlockSpec((tm, C), lambda i, c_ref: (i, 0))
    a_spec = pl.BlockSpec((tm, C), lambda i, c_ref: (jnp.where(i // nh == c_ref[0], i % nh, 0), 0))
    b_spec = pl.BlockSpec((tm, C), lambda i, c_ref: (jnp.where(i // nh == c_ref[0], 0, i % nh), 0))
    return pl.pallas_call(
        body, name=name, out_shape=[jax.ShapeDtypeStruct((R, C), F32)] * 4,
        grid_spec=pltpu.PrefetchScalarGridSpec(num_scalar_prefetch=1, grid=(R // tm,),
                                               in_specs=[spec, spec, spec, a_spec, b_spec], out_specs=[spec] * 4),
        compiler_params=_params(("arbitrary",)),
    )(core, w, m, v, mine, other)


def _adam_small(name, w, m, v, g):
    def body(w_ref, m_ref, v_ref, g_ref, d_ref, nm_ref, nv_ref):
        delta, nm, nv = _adam_math(w_ref[...], g_ref[...], m_ref[...], v_ref[...])
        d_ref[...] = delta
        nm_ref[...] = nm
        nv_ref[...] = nv

    return pl.pallas_call(body, name=name, out_shape=[jax.ShapeDtypeStruct(w.shape, F32)] * 3,
                          compiler_params=_params())(w, m, v, g)


def _place():
    return lax.axis_index("x"), lax.axis_index("y"), lax.axis_index("c")


def _other_chips(x, y):
    return [(1 - x, y), (x, 1 - y), (1 - x, 1 - y)]


def _all_gather8(blk, name):
    R, N = blk.shape

    def body(x_ref, out_ref, send_sems, recv_sems, local_sem):
        x, y, c = _place()
        me = 4 * x + 2 * y + c
        mine = pltpu.make_async_copy(x_ref, out_ref.at[me], local_sem)
        mine.start()
        flips = [(j >> 2 & 1, j >> 1 & 1, j & 1) for j in range(1, 8)]
        peers = [((1 - x) if fx else x, (1 - y) if fy else y, (1 - c) if fc else c) for fx, fy, fc in flips]
        sends = []
        for j, peer in enumerate(peers):
            cp = pltpu.make_async_remote_copy(src_ref=x_ref, dst_ref=out_ref.at[me], send_sem=send_sems.at[j],
                                              recv_sem=recv_sems.at[j], device_id=peer, device_id_type=MESH)
            cp.start()
            sends.append(cp)
        for j, (px, py, pc) in enumerate(peers):
            pltpu.make_async_remote_copy(src_ref=x_ref, dst_ref=out_ref.at[4 * px + 2 * py + pc],
                                         send_sem=send_sems.at[j], recv_sem=recv_sems.at[j],
                                         device_id=(px, py, pc), device_id_type=MESH).wait_recv()
        for cp in sends:
            cp.wait_send()
        mine.wait()

    return pl.pallas_call(
        body, name=name, out_shape=jax.ShapeDtypeStruct((8, R, N), F32),
        in_specs=[pl.BlockSpec(memory_space=pltpu.VMEM)], out_specs=pl.BlockSpec(memory_space=pltpu.VMEM),
        scratch_shapes=[pltpu.SemaphoreType.DMA((7,)), pltpu.SemaphoreType.DMA((7,)), pltpu.SemaphoreType.DMA],
        compiler_params=_params(),
    )(blk)


def _scatter_chips(arrs, name):
    n = len(arrs)

    def body(*refs):
        ins, outs = refs[:n], refs[n:2 * n]
        send_sems, recv_sems = refs[2 * n:]
        x, y, c = _place()
        chips = _other_chips(x, y)

        def copy(k, j):
            px, py = chips[j]
            return pltpu.make_async_remote_copy(src_ref=ins[k].at[2 * px + py], dst_ref=outs[k].at[j],
                                                send_sem=send_sems.at[3 * k + j], recv_sem=recv_sems.at[3 * k + j],
                                                device_id=(px, py, c), device_id_type=MESH)

        for k in range(n):
            for j in range(3):
                copy(k, j).start()
        for k in range(n):
            for j in range(3):
                copy(k, j).wait()

    return pl.pallas_call(
        body, name=name, out_shape=[jax.ShapeDtypeStruct((3,) + a.shape[1:], a.dtype) for a in arrs],
        in_specs=[ANY] * n, out_specs=[ANY] * n,
        scratch_shapes=[pltpu.SemaphoreType.DMA((3 * n,)), pltpu.SemaphoreType.DMA((3 * n,))],
        compiler_params=_params(),
    )(*arrs)


def _gather2(shards, name):
    n = len(shards)

    def body(*refs):
        ins, outs = refs[:n], refs[n:2 * n]
        s1, r1, s2, r2, loc = refs[2 * n:]
        x, y, c = _place()
        me = 2 * x + y
        chips = _other_chips(x, y)
        sib = (x, y, 1 - c)

        def ici(k, j, slab, to):
            return pltpu.make_async_remote_copy(src_ref=ins[k].at[c], dst_ref=outs[k].at[slab, c], send_sem=s1.at[3 * k + j],
                                                recv_sem=r1.at[3 * k + j], device_id=to, device_id_type=MESH)

        def d2d(k, j, slab, half):
            return pltpu.make_async_remote_copy(src_ref=outs[k].at[slab, half], dst_ref=outs[k].at[slab, half],
                                                send_sem=s2.at[3 * k + j], recv_sem=r2.at[3 * k + j],
                                                device_id=sib, device_id_type=MESH)

        def own(k):
            return pltpu.make_async_remote_copy(src_ref=ins[k], dst_ref=outs[k].at[me], send_sem=loc.at[2 * k],
                                                recv_sem=loc.at[2 * k + 1], device_id=sib, device_id_type=MESH)

        for k in range(n):
            own(k).start()
            for j, (px, py) in enumerate(chips):
                ici(k, j, me, (px, py, c)).start()
        for k in range(n):
            for j, (px, py) in enumerate(chips):
                ici(k, j, 2 * px + py, (px, py, c)).wait_recv()
                d2d(k, j, 2 * px + py, c).start()
        for k in range(n):
            for j, (px, py) in enumerate(chips):
                d2d(k, j, 2 * px + py, 1 - c).wait_recv()
        for k in range(n):
            own(k).wait()
            for j, (px, py) in enumerate(chips):
                ici(k, j, me, (px, py, c)).wait_send()
                d2d(k, j, 2 * px + py, c).wait_send()

    return pl.pallas_call(
        body, name=name, out_shape=[jax.ShapeDtypeStruct((4,) + a.shape, a.dtype) for a in shards],
        in_specs=[ANY] * n, out_specs=[ANY] * n,
        scratch_shapes=[pltpu.SemaphoreType.DMA((3 * n,))] * 4 + [pltpu.SemaphoreType.DMA((2 * n,))],
        compiler_params=_params(),
    )(*shards)


def _pair_exchange(parts, name):
    n = len(parts)

    def body(*refs):
        ins, sib = refs[:n], refs[n:2 * n]
        send_sems, recv_sems = refs[2 * n:]
        x, y, c = _place()
        cps = []
        for k in range(n):
            for p in range(4):
                cp = pltpu.make_async_remote_copy(src_ref=ins[k].at[p, 1 - c], dst_ref=sib[k].at[p],
                                                  send_sem=send_sems.at[4 * k + p], recv_sem=recv_sems.at[4 * k + p],
                                                  device_id=(x, y, 1 - c), device_id_type=MESH)
                cp.start()
                cps.append(cp)
        for cp in cps:
            cp.wait()

    return pl.pallas_call(
        body, name=name, out_shape=[jax.ShapeDtypeStruct((4,) + a.shape[2:], a.dtype) for a in parts],
        in_specs=[ANY] * n, out_specs=[ANY] * n,
        scratch_shapes=[pltpu.SemaphoreType.DMA((4 * n,))] * 2,
        compiler_params=_params(),
    )(*parts)


def _sibling_exchange(arrs, name):
    n = len(arrs)

    def body(*refs):
        ins, outs = refs[:n], refs[n:2 * n]
        send_sems, recv_sems = refs[2 * n:]
        x, y, c = _place()
        cps = [pltpu.make_async_remote_copy(src_ref=ins[k], dst_ref=outs[k], send_sem=send_sems.at[k],
                                            recv_sem=recv_sems.at[k], device_id=(x, y, 1 - c), device_id_type=MESH)
               for k in range(n)]
        for cp in cps:
            cp.start()
        for cp in cps:
            cp.wait()

    return pl.pallas_call(
        body, name=name, out_shape=[jax.ShapeDtypeStruct(a.shape, a.dtype) for a in arrs],
        in_specs=[ANY] * n, out_specs=[ANY] * n,
        scratch_shapes=[pltpu.SemaphoreType.DMA((n,)), pltpu.SemaphoreType.DMA((n,))],
        compiler_params=_params(),
    )(*arrs)


def _add_pair(parts, sib, core, name):
    P4, _, Rh, C = parts.shape
    tm = _row_tile(Rh, C * 4, 1 << 20, 16)

    def body(c_ref, a_ref, b_ref, o_ref):
        o_ref[...] = (a_ref[0].astype(F32) + b_ref[...].astype(F32)).astype(BF16)

    spec = pl.BlockSpec((1, tm, C), lambda p, i, c_ref: (p, i, 0))
    return pl.pallas_call(
        body, name=name, out_shape=jax.ShapeDtypeStruct((P4, Rh, C), BF16),
        grid_spec=pltpu.PrefetchScalarGridSpec(
            num_scalar_prefetch=1, grid=(P4, Rh // tm),
            in_specs=[pl.BlockSpec((1, 1, tm, C), lambda p, i, c_ref: (p, c_ref[0], i, 0)), spec], out_specs=spec),
        compiler_params=_params(("parallel", "parallel")),
    )(core, parts, sib)


def _sum_slabs(pre, recv, chip, name):
    _, Rh, C = pre.shape
    tm = _row_tile(Rh, C * 4, 1 << 20, 16)

    def body(me_ref, own_ref, r_ref, o_ref):
        acc = own_ref[0].astype(F32)
        for j in range(3):
            acc = acc + r_ref[j].astype(F32)
        o_ref[...] = acc

    return pl.pallas_call(
        body, name=name, out_shape=jax.ShapeDtypeStruct((Rh, C), F32),
        grid_spec=pltpu.PrefetchScalarGridSpec(
            num_scalar_prefetch=1, grid=(Rh // tm,),
            in_specs=[pl.BlockSpec((1, tm, C), lambda i, me_ref: (me_ref[0], i, 0)),
                      pl.BlockSpec((3, tm, C), lambda i, me_ref: (0, i, 0))],
            out_specs=pl.BlockSpec((tm, C), lambda i, me_ref: (i, 0))),
        compiler_params=_params(("parallel",)),
    )(chip, pre, recv)


def kernel(x, c, positions, w_ada, b_ada, w_in, g_q_a, w_q_b, g_kv_a, w_kv_b, w_o_a, w_conv, w_o_b, w_o, ln1_g, ln1_b, w_ffn_in, w_ffn_out, ln2_g, ln2_b, loss_target, m_w_ada, m_b_ada, m_w_in, m_g_q_a, m_w_q_b, m_g_kv_a, m_w_kv_b, m_w_o_a, m_w_conv, m_w_o_b, m_w_o, m_ln1_g, m_ln1_b, m_w_ffn_in, m_w_ffn_out, m_ln2_g, m_ln2_b, v_w_ada, v_b_ada, v_w_in, v_g_q_a, v_w_q_b, v_g_kv_a, v_w_kv_b, v_w_o_a, v_w_conv, v_w_o_b, v_w_o, v_ln1_g, v_ln1_b, v_w_ffn_in, v_w_ffn_out, v_ln2_g, v_ln2_b):
    S, D = x.shape[1], x.shape[2]
    F = w_ffn_out.shape[1] * 4
    ax, ay, ac = _place()
    chip = 2 * ax + ay
    dev = 4 * ax + 2 * ay + ac
    x2, tgt = x[0], loss_target[0]
    w_ada2, w_in2, w_q_b2, w_kv_b2 = w_ada[0], w_in[0], w_q_b[0], w_kv_b[0]
    w_o_a2, w_o_b2, w_o2, w_ffn_in2, w_ffn_out2 = w_o_a[0], w_o_b[0], w_o[0], w_ffn_in[0], w_ffn_out[0]
    NA = w_ada2.shape[1]
    CW = w_conv.shape[2]

    inv_freq = 1.0 / (ROPE_THETA ** (jnp.arange(0, QK_ROPE, 2, dtype=F32) / QK_ROPE))
    ang = positions[0].astype(F32)[:, None] * inv_freq
    cos, sin = jnp.cos(ang), jnp.sin(ang)
    z32, z64, z96 = jnp.zeros((S, 32), F32), jnp.zeros((S, 64), F32), jnp.zeros((S, 96), F32)
    tab = jnp.concatenate([cos, cos, z64, -sin, z96, z32, sin, z64], axis=1)

    def halves(a):
        return a.reshape(2, a.shape[0] // 2, a.shape[1])

    shards = [halves(w.astype(BF16)) for w in (w_in2, w_q_b2, w_kv_b2, w_o_a2, w_o_b2, w_o2, w_ffn_in2, w_ffn_out2)]
    g_in, g_qb, g_kvb, g_oa, g_ob, g_o, g_fi, g_fo = (
        g.reshape(4, 2 * g.shape[2], g.shape[3]) for g in _gather2(shards, "gather_weights"))

    def cols(g):
        return jnp.transpose(g, (1, 0, 2)).reshape(g.shape[1], 4 * g.shape[2])

    W_in = cols(g_in)
    n_qkv = Q_LORA + KV_LORA + QK_ROPE
    W_qkv = jnp.pad(W_in[:, :n_qkv], ((0, 0), (0, QKV_A - n_qkv)))
    W_conv = W_in[:, n_qkv:n_qkv + 3 * D]
    W_gate = W_in[:, n_qkv + 3 * D:]
    W_qb = jnp.pad(cols(g_qb).reshape(Q_LORA, N_HEADS, QK_NOPE + QK_ROPE),
                   ((0, 0), (0, 0), (0, QK_PAD - QK_NOPE - QK_ROPE))).reshape(Q_LORA, N_HEADS * QK_PAD)
    W_kvb = cols(g_kvb)
    W_oa, W_ob, W_o = (g.reshape(-1, D) for g in (g_oa, g_ob, g_o))
    W_fi = cols(g_fi)
    W_fo = g_fo.reshape(F, D)

    c_all = _all_gather8(c, "gather_c").reshape(8, D)
    wconv_all = _all_gather8(w_conv[0], "gather_wconv")
    w_conv_full = jnp.transpose(wconv_all[0::2], (1, 0, 2)).reshape(3, D)
    b_sh = lax.dynamic_slice(b_ada, (0, chip * NA), (1, NA))
    mod_sh = _ada_fwd(c_all, w_ada2, b_sh)
    mod_all = _all_gather8(mod_sh, "gather_mod")
    mod = lax.dynamic_slice(mod_all[0::2], (0, dev, 0), (4, 1, NA)).reshape(6, D)
    shift1, scale1, gate1, shift2, scale2, gate2 = (mod[k:k + 1] for k in range(6))

    u = _modulate(x2, scale1, shift1, "modulate1")
    pq = _matmul(u, W_qkv, "nn", F32, "proj_qkv")
    pc = _matmul(u, W_conv, "nn", F32, "proj_conv")
    pg = _matmul(u, W_gate, "nn", F32, "proj_gate")
    rq, rkv, kr = _rms_fwd(pq, tab, g_q_a, g_kv_a)
    q = _q_rope(_matmul(rq, W_qb, "nn", F32, "q_b"), tab)
    kv = _matmul(rkv, W_kvb, "nn", BF16, "kv_b")
    o, lse = _attn_fwd(q, kv, kr)
    y_a = _matmul(o, W_oa, "nn", F32, "o_a")
    hb = _conv_fwd(pc, w_conv_full)
    y_b = _matmul(hb, W_ob, "nn", F32, "o_b")
    merged = _merge_fwd(y_a, y_b, pg)
    mix = _matmul(merged, W_o, "nn", F32, "w_o")
    x1, u2 = _ln1_fwd(x2, mix, gate1, ln1_g, ln1_b, scale2, shift2)
    hh = _matmul(u2, W_fi, "nn", F32, "ffn_in")
    act = _swiglu_fwd(hh)
    ffn = _matmul(act, W_fo, "nn", F32, "ffn_out")

    dffn, dx1a, loss_acc, d_ln2_g, d_ln2_b, d_gate2 = _ln2_loss_bwd(x1, ffn, gate2, ln2_g, ln2_b, tgt)
    loss = lax.psum(loss_acc[0, 0], ("x", "y", "c"))
    dW_fo = _matmul(act, dffn, "tn", BF16, "d_w_ffn_out")
    dact = _matmul(dffn, W_fo, "nt", F32, "d_act")
    dhh = _swiglu_bwd(dact, hh)
    dW_fi = _matmul(u2, dhh, "tn", BF16, "d_w_ffn_in")
    du2 = _matmul(dhh, W_fi, "nt", F32, "d_u2")
    dmix, dxa, d_shift2, d_scale2, d_ln1_g, d_ln1_b, d_gate1 = _ln1_bwd(x2, mix, dx1a, du2, gate1, ln1_g, ln1_b, scale2)
    dW_o = _matmul(merged, dmix, "tn", BF16, "d_w_o")
    dmerged = _matmul(dmix, W_o, "nt", F32, "d_merged")
    dy_a, dy_b, dgate = _merge_bwd(dmerged, y_a, y_b, pg)
    dW_oa = _matmul(o, dy_a, "tn", BF16, "d_w_o_a")
    do = _matmul(dy_a, W_oa, "nt", BF16, "d_o")
    dW_ob = _matmul(hb, dy_b, "tn", BF16, "d_w_o_b")
    dhb = _matmul(dy_b, W_ob, "nt", F32, "d_hb")
    dconv, d_wconv = _conv_bwd(dhb, pc, w_conv_full)
    dq, dkv, dkr = _attn_bwd(q, kv, kr, do, o, lse, tab)
    dW_qb = _matmul(rq, dq, "tn", BF16, "d_w_q_b")
    d_rq = _matmul(dq, W_qb, "nt", F32, "d_rq")
    dW_kvb = _matmul(rkv, dkv, "tn", BF16, "d_w_kv_b")
    d_rkv = _matmul(dkv, W_kvb, "nt", F32, "d_rkv")
    dqkv, d_g_q, d_g_kv = _rms_bwd(d_rq, d_rkv, pq, dkr, g_q_a, g_kv_a)
    dW_qkv = _matmul(u, dqkv, "tn", BF16, "d_w_qkv")
    dW_conv = _matmul(u, dconv, "tn", BF16, "d_w_conv")
    dW_gate = _matmul(u, dgate, "tn", BF16, "d_w_gate")
    du = _matmul(dqkv, W_qkv, "nt", F32, "d_u_qkv")
    du = _matmul(dconv, W_conv, "nt", F32, "d_u_conv", add=du)
    du = _matmul(dgate, W_gate, "nt", F32, "d_u_gate", add=du)
    grad_x, d_shift1, d_scale1 = _dx_final(dxa, du, x2, scale1)

    def pad_d(v):
        return jnp.pad(v, ((0, 0), (0, D - v.shape[1])))

    small = jnp.concatenate([d_ln1_g, d_ln1_b, d_ln2_g, d_ln2_b, pad_d(d_g_q), pad_d(d_g_kv), d_wconv,
                             d_shift1, d_scale1, d_gate1, d_shift2, d_scale2, d_gate2, jnp.zeros((1, D), F32)], axis=0)
    small_all = _all_gather8(small, "gather_small")
    small_sum = _sum8(small_all)
    g_ln1_g, g_ln1_b, g_ln2_g, g_ln2_b = (small_sum[k:k + 1] for k in range(4))
    g_g_q, g_g_kv = small_sum[4:5, :Q_LORA], small_sum[5:6, :KV_LORA]
    g_wconv = lax.dynamic_slice(small_sum[6:9], (0, chip * CW), (3, CW))
    g_b_ada = small_sum[9:15].reshape(1, 6 * D)
    dmod_all = small_all[:, 9:15, :].reshape(8, 6 * D)
    g_w_ada = _ada_bwd(c_all, lax.dynamic_slice(dmod_all, (0, chip * NA), (8, NA)))

    def uncols(g):
        return jnp.transpose(g.reshape(g.shape[0], 4, g.shape[1] // 4), (1, 0, 2))

    dW_in = jnp.concatenate([dW_qkv[:, :n_qkv], dW_conv, dW_gate], axis=1)
    dW_qb_u = dW_qb.reshape(Q_LORA, N_HEADS, QK_PAD)[:, :, :QK_NOPE + QK_ROPE].reshape(Q_LORA, -1)
    parts = [uncols(dW_in), uncols(dW_qb_u), uncols(dW_kvb), dW_oa.reshape(4, -1, D), dW_ob.reshape(4, -1, D),
             dW_o.reshape(4, -1, D), uncols(dW_fi), dW_fo.reshape(4, -1, D)]
    names = ["w_in", "w_q_b", "w_kv_b", "w_o_a", "w_o_b", "w_o", "w_ffn_in", "w_ffn_out"]
    parts = [p.reshape(4, 2, p.shape[1] // 2, p.shape[2]) for p in parts]
    core_i = ac.astype(jnp.int32).reshape(1)
    chip_i = chip.astype(jnp.int32).reshape(1)
    sib = _pair_exchange(parts, "pair_grads")
    pre = [_add_pair(a, b, core_i, "add_pair_" + nm) for a, b, nm in zip(parts, sib, names)]
    recv = _scatter_chips(pre, "scatter_grads")
    fin = [_sum_slabs(a, r, chip_i, "sum_slabs_" + nm) for a, r, nm in zip(pre, recv, names)]
    fin_sib = _sibling_exchange(fin, "sibling_grads")

    big = {}
    ws = dict(w_in=(w_in2, m_w_in[0], v_w_in[0]), w_q_b=(w_q_b2, m_w_q_b[0], v_w_q_b[0]),
              w_kv_b=(w_kv_b2, m_w_kv_b[0], v_w_kv_b[0]), w_o_a=(w_o_a2, m_w_o_a[0], v_w_o_a[0]),
              w_o_b=(w_o_b2, m_w_o_b[0], v_w_o_b[0]), w_o=(w_o2, m_w_o[0], v_w_o[0]),
              w_ffn_in=(w_ffn_in2, m_w_ffn_in[0], v_w_ffn_in[0]), w_ffn_out=(w_ffn_out2, m_w_ffn_out[0], v_w_ffn_out[0]))
    for nm, a, b in zip(names, fin, fin_sib):
        w_, m_, v_ = ws[nm]
        big[nm] = _adam_halves("adam_" + nm, w_, m_, v_, a, b, core_i)
    big["w_ada"] = _adam("adam_w_ada", w_ada2, m_w_ada[0], v_w_ada[0], [g_w_ada])
    sm = {}
    for nm, w_, m_, v_, g_ in [("b_ada", b_ada, m_b_ada, v_b_ada, g_b_ada), ("g_q_a", g_q_a, m_g_q_a, v_g_q_a, g_g_q),
                               ("g_kv_a", g_kv_a, m_g_kv_a, v_g_kv_a, g_g_kv),
                               ("w_conv", w_conv[0], m_w_conv[0], v_w_conv[0], g_wconv),
                               ("ln1_g", ln1_g, m_ln1_g, v_ln1_g, g_ln1_g), ("ln1_b", ln1_b, m_ln1_b, v_ln1_b, g_ln1_b),
                               ("ln2_g", ln2_g, m_ln2_g, v_ln2_g, g_ln2_g), ("ln2_b", ln2_b, m_ln2_b, v_ln2_b, g_ln2_b)]:
        sm[nm] = (g_,) + tuple(_adam_small("adam_" + nm, w_, m_, v_, g_))

    order = ["w_ada", "b_ada", "w_in", "g_q_a", "w_q_b", "g_kv_a", "w_kv_b", "w_o_a", "w_conv", "w_o_b", "w_o",
             "ln1_g", "ln1_b", "w_ffn_in", "w_ffn_out", "ln2_g", "ln2_b"]
    lead = {"b_ada", "g_q_a", "g_kv_a", "ln1_g", "ln1_b", "ln2_g", "ln2_b"}

    def leaf(nm, k):
        val = big[nm][k] if nm in big else sm[nm][k]
        return val if nm in lead else val[None]

    outs = [loss, grad_x[None]]
    for k in range(4):
        outs += [leaf(nm, k) for nm in order]
    return tuple(outs)
```

```python
import functools

import jax
import jax.numpy as jnp
from jax import lax
from jax.experimental import pallas as pl
from jax.experimental.pallas import tpu as pltpu

F32, BF16 = jnp.float32, jnp.bfloat16
N_HEADS, QK_NOPE, QK_ROPE, V_HEAD = 16, 128, 64, 128
Q_LORA, KV_LORA = 512, 512
QK_PAD = 256
QKV_A = 1152
CHUNK_SHIFT = 6
ATTN_SCALE = (QK_NOPE + QK_ROPE) ** -0.5
ROPE_THETA = 10000.0
ALPHA = 2.0 ** 0.25
LN_EPS, RMS_EPS = 1e-5, 1e-6
ADAM_LR, ADAM_B1, ADAM_B2, ADAM_EPS, ADAM_WD, ADAM_STEP = 0.001, 0.9, 0.999, 1e-08, 0.01, 10
ADAM_C1 = 1.0 - ADAM_B1 ** ADAM_STEP
ADAM_C2 = 1.0 - ADAM_B2 ** ADAM_STEP
VMEM_LIMIT = 56 * 1024 * 1024
MESH = pl.DeviceIdType.MESH
ANY = pl.BlockSpec(memory_space=pl.ANY)
NT = (((1,), (1,)), ((), ()))
TN = (((0,), (0,)), ((), ()))
NN = (((1,), (0,)), ((), ()))


def _params(sem=None):
    return pltpu.CompilerParams(dimension_semantics=sem, vmem_limit_bytes=VMEM_LIMIT)


def _pick(n, cands=(1024, 512, 384, 256, 128)):
    for t in cands:
        if n % t == 0:
            return t
    return n


def _row_tile(rows, row_bytes, budget, mult=8):
    best = mult
    for t in range(mult, rows + 1, mult):
        if rows % t == 0 and t * row_bytes <= budget:
            best = t
    return best


def _sigmoid(x):
    return jax.nn.sigmoid(x)


class _Plan:
    def __init__(self, ins, outs, sems, start, finish):
        self.ins, self.outs, self.sems, self.start, self.finish = list(ins), list(outs), list(sems), start, finish


def _run_plan(plan, name):
    n_in, n_out = len(plan.ins), len(plan.outs)

    def body(*refs):
        ins, outs, sems = refs[:n_in], refs[n_in:n_in + n_out], refs[n_in + n_out:]
        plan.start(ins, outs, sems)
        plan.finish(ins, outs, sems)

    return pl.pallas_call(body, name=name, out_shape=plan.outs, in_specs=[ANY] * n_in, out_specs=[ANY] * n_out,
                          scratch_shapes=plan.sems, compiler_params=_params())(*plan.ins)


def _matmul(a, b, mode, out_dtype, name, add=None, carry=None):
    if mode == "nn":
        (M, K), N, dims = a.shape, b.shape[1], NN
    elif mode == "nt":
        (M, K), N, dims = a.shape, b.shape[0], NT
    else:
        (K, M), N, dims = a.shape, b.shape[1], TN
    tm, tn, tk = _pick(M), _pick(N), _pick(K)
    nk = K // tk
    a_spec = (pl.BlockSpec((tk, tm), lambda i, j, k: (k, i)) if mode == "tn"
              else pl.BlockSpec((tm, tk), lambda i, j, k: (i, k)))
    b_spec = (pl.BlockSpec((tn, tk), lambda i, j, k: (j, k)) if mode == "nt"
              else pl.BlockSpec((tk, tn), lambda i, j, k: (k, j)))
    o_spec = pl.BlockSpec((tm, tn), lambda i, j, k: (i, j))
    has_add = add is not None
    n_ci = len(carry.ins) if carry else 0
    n_co = len(carry.outs) if carry else 0
    n_in = 2 + has_add
    grid = (M // tm, N // tn, nk)

    def body(*refs):
        a_ref, b_ref = refs[0], refs[1]
        add_ref = refs[2] if has_add else None
        o_ref = refs[n_in + n_ci]
        acc_ref = refs[n_in + n_ci + 1 + n_co]
        c_ins = refs[n_in:n_in + n_ci]
        c_outs = refs[n_in + n_ci + 1:n_in + n_ci + 1 + n_co]
        c_sems = refs[n_in + n_ci + 2 + n_co:]
        i, j, k = pl.program_id(0), pl.program_id(1), pl.program_id(2)

        if carry:
            @pl.when((i == 0) & (j == 0) & (k == 0))
            def _():
                carry.start(c_ins, c_outs, c_sems)

        @pl.when(k == 0)
        def _():
            acc_ref[...] = jnp.zeros_like(acc_ref)

        acc_ref[...] += lax.dot_general(a_ref[...], b_ref[...], dims, preferred_element_type=F32)

        @pl.when(k == nk - 1)
        def _():
            r = acc_ref[...]
            if has_add:
                r = r + add_ref[...]
            o_ref[...] = r.astype(o_ref.dtype)

        if carry:
            @pl.when((i == grid[0] - 1) & (j == grid[1] - 1) & (k == nk - 1))
            def _():
                carry.finish(c_ins, c_outs, c_sems)

    ins = [a, b] + ([add] if has_add else []) + (carry.ins if carry else [])
    in_specs = [a_spec, b_spec] + ([o_spec] if has_add else []) + [ANY] * n_ci
    res = pl.pallas_call(
        body, name=name, grid=grid,
        in_specs=in_specs, out_specs=[o_spec] + [ANY] * n_co,
        out_shape=[jax.ShapeDtypeStruct((M, N), out_dtype)] + (carry.outs if carry else []),
        scratch_shapes=[pltpu.VMEM((tm, tn), F32)] + (carry.sems if carry else []),
        compiler_params=_params(("arbitrary",) * 3 if carry else ("parallel", "parallel", "arbitrary")),
    )(*ins)
    return (res[0], res[1:]) if carry else res[0]


def _rows(body, name, n_rows, tm, ins, outs, accs=()):
    grid = (n_rows // tm,)
    per8 = tm // 8
    last8 = n_rows // 8 - 1
    arrays, in_specs = [], []
    for spec in ins:
        kind, arr = spec[0], spec[1]
        arrays.append(arr)
        if kind == "row":
            _, _, cb, w = spec
            in_specs.append(pl.BlockSpec((tm, w), lambda i, cb=cb: (i, cb)))
        elif kind == "full":
            in_specs.append(pl.BlockSpec(arr.shape, lambda i, nd=arr.ndim: (0,) * nd))
        elif kind == "prev":
            _, _, cb, w = spec
            in_specs.append(pl.BlockSpec((8, w), lambda i, cb=cb: (jnp.maximum(i * per8 - 1, 0), cb)))
        else:
            _, _, cb, w = spec
            in_specs.append(pl.BlockSpec((8, w), lambda i, cb=cb: (jnp.minimum((i + 1) * per8, last8), cb)))
    out_shape = [jax.ShapeDtypeStruct((n_rows, w), dt) for (w, dt) in outs]
    out_specs = [pl.BlockSpec((tm, w), lambda i: (i, 0)) for (w, _) in outs]
    out_shape += [jax.ShapeDtypeStruct(s, F32) for s in accs]
    out_specs += [pl.BlockSpec(s, lambda i, nd=len(s): (0,) * nd) for s in accs]
    n_in, n_out = len(ins), len(outs)

    def kernel_body(*refs):
        body(pl.program_id(0), refs[:n_in], refs[n_in:n_in + n_out], refs[n_in + n_out:])

    res = pl.pallas_call(
        kernel_body, name=name, grid=grid, in_specs=in_specs, out_specs=out_specs, out_shape=out_shape,
        compiler_params=_params(("arbitrary",)),
    )(*arrays)
    return res


def _acc_add(i, ref, val):
    @pl.when(i == 0)
    def _():
        ref[...] = val

    @pl.when(i > 0)
    def _():
        ref[...] += val


def _rope(t, tab, sign):
    c, sa, sb = tab[:, 0:128], tab[:, 128:256], tab[:, 256:384]
    rot = pltpu.roll(t, 96, 1) * sa + pltpu.roll(t, 32, 1) * sb
    return t * c + rot if sign > 0 else t * c - rot


def _ln_stats(r):
    mu = jnp.mean(r, axis=-1, keepdims=True)
    d = r - mu
    var = jnp.mean(d * d, axis=-1, keepdims=True)
    rstd = lax.rsqrt(var + LN_EPS)
    return d * rstd, rstd


def _ln_bwd(dxh, xh, rstd):
    m1 = jnp.mean(dxh, axis=-1, keepdims=True)
    m2 = jnp.mean(dxh * xh, axis=-1, keepdims=True)
    return rstd * (dxh - m1 - xh * m2)


def _modulate(x, scale, shift, name):
    S, D = x.shape

    def body(i, ins, outs, accs):
        outs[0][...] = (ins[0][...] * (1.0 + ins[1][...]) + ins[2][...]).astype(BF16)

    return _rows(body, name, S, _pick(S, (256, 128)), [("row", x, 0, D), ("full", scale), ("full", shift)], [(D, BF16)])[0]


def _rms_fwd(pq, tab, g_q, g_kv):
    S = pq.shape[0]

    def body(i, ins, outs, accs):
        pq_ref, tab_ref, gq_ref, gkv_ref = ins

        def rms(x, g):
            return x * lax.rsqrt(jnp.mean(x * x, axis=-1, keepdims=True) + RMS_EPS) * g

        outs[0][...] = rms(pq_ref[:, 0:Q_LORA], gq_ref[...]).astype(BF16)
        outs[1][...] = rms(pq_ref[:, Q_LORA:Q_LORA + KV_LORA], gkv_ref[...]).astype(BF16)
        outs[2][...] = _rope(pq_ref[:, Q_LORA + KV_LORA:QKV_A], tab_ref[...], 1).astype(BF16)

    return _rows(body, "rms_fwd", S, _pick(S, (256, 128)),
                 [("row", pq, 0, QKV_A), ("row", tab, 0, 384), ("full", g_q), ("full", g_kv)],
                 [(Q_LORA, BF16), (KV_LORA, BF16), (128, BF16)])


def _q_rope(q, tab):
    S, W = q.shape

    def body(i, ins, outs, accs):
        q_ref, tab_ref = ins
        t = tab_ref[...]
        for h in range(N_HEADS):
            lo = h * QK_PAD
            outs[0][:, lo:lo + 128] = q_ref[:, lo:lo + 128].astype(BF16)
            outs[0][:, lo + 128:lo + 256] = _rope(q_ref[:, lo + 128:lo + 256], t, 1).astype(BF16)

    return _rows(body, "q_rope", S, _pick(S, (256, 128)), [("row", q, 0, W), ("row", tab, 0, 384)], [(W, BF16)])[0]


def _allowed(q0, k0, bq):
    row = q0 + lax.broadcasted_iota(jnp.int32, (bq, bq), 0)
    col = k0 + lax.broadcasted_iota(jnp.int32, (bq, bq), 1)
    return (col >> CHUNK_SHIFT) <= (row >> CHUNK_SHIFT)


def _attn_fwd(q, kv, kr, carry=None):
    S = q.shape[0]
    bq = min(256, S)
    nq = S // bq
    n_ci = len(carry.ins) if carry else 0
    n_co = len(carry.outs) if carry else 0

    def body(*refs):
        q_ref, kn_ref, v_ref, kr_ref = refs[:4]
        o_ref, lse_ref = refs[4 + n_ci:6 + n_ci]
        c_ins, c_outs, c_sems = refs[4:4 + n_ci], refs[6 + n_ci:6 + n_ci + n_co], refs[6 + n_ci + n_co:]
        qi = pl.program_id(1)
        if carry:
            @pl.when((pl.program_id(0) == 0) & (qi == 0))
            def _():
                carry.start(c_ins, c_outs, c_sems)

        qv = q_ref[...]

        def step(j, carry):
            m, l, acc = carry
            off = pl.multiple_of(j * bq, bq)
            k = jnp.concatenate([kn_ref[pl.ds(off, bq), :], kr_ref[pl.ds(off, bq), :]], axis=1)
            s = lax.dot_general(qv, k, NT, preferred_element_type=F32) * ATTN_SCALE
            s = jnp.where(_allowed(qi * bq, off, bq), s, -1e30)
            m_new = jnp.maximum(m, jnp.max(s, axis=1, keepdims=True))
            a = jnp.exp(m - m_new)
            p = jnp.exp(s - m_new)
            l = a * l + jnp.sum(p, axis=1, keepdims=True)
            acc = a * acc + jnp.dot(p.astype(BF16), v_ref[pl.ds(off, bq), :], preferred_element_type=F32)
            return m_new, l, acc

        init = (jnp.full((bq, 1), -1e30, F32), jnp.zeros((bq, 1), F32), jnp.zeros((bq, V_HEAD), F32))
        m, l, acc = lax.fori_loop(0, qi + 1, step, init)
        o_ref[...] = (acc / l).astype(BF16)
        lse_ref[0] = m + jnp.log(l)
        if carry:
            @pl.when((pl.program_id(0) == N_HEADS - 1) & (qi == nq - 1))
            def _():
                carry.finish(c_ins, c_outs, c_sems)

    res = pl.pallas_call(
        body, name="attn_fwd", grid=(N_HEADS, nq),
        in_specs=[pl.BlockSpec((bq, QK_PAD), lambda h, i: (i, h)),
                  pl.BlockSpec((S, 128), lambda h, i: (0, 2 * h)),
                  pl.BlockSpec((S, 128), lambda h, i: (0, 2 * h + 1)),
                  pl.BlockSpec((S, 128), lambda h, i: (0, 0))] + [ANY] * n_ci,
        out_specs=[pl.BlockSpec((bq, V_HEAD), lambda h, i: (i, h)),
                   pl.BlockSpec((1, bq, 1), lambda h, i: (h, i, 0))] + [ANY] * n_co,
        out_shape=[jax.ShapeDtypeStruct((S, N_HEADS * V_HEAD), BF16),
                   jax.ShapeDtypeStruct((N_HEADS, S, 1), F32)] + (carry.outs if carry else []),
        scratch_shapes=carry.sems if carry else [],
        compiler_params=_params(("arbitrary", "arbitrary")),
    )(q, kv, kv, kr, *(carry.ins if carry else []))
    return res[0], res[1], res[2:]


def _attn_bwd(q, kv, kr, do, o, lse, tab, carry=None):
    S = q.shape[0]
    bq = min(256, S)
    nq = S // bq

    n_ci = len(carry.ins) if carry else 0
    n_co = len(carry.outs) if carry else 0

    def body(*refs):
        q_ref, kn_ref, v_ref, kr_ref, do_ref, o_ref, lse_ref, tab_ref = refs[:8]
        dq_ref, dkv_ref, dkr_ref = refs[8 + n_ci:11 + n_ci]
        dq_acc, dk_acc, dv_acc = refs[11 + n_ci + n_co:14 + n_ci + n_co]
        c_ins, c_outs, c_sems = refs[8:8 + n_ci], refs[11 + n_ci:11 + n_ci + n_co], refs[14 + n_ci + n_co:]
        h = pl.program_id(0)
        if carry:
            @pl.when(h == 0)
            def _():
                carry.start(c_ins, c_outs, c_sems)

        dq_acc[...] = jnp.zeros_like(dq_acc)
        dk_acc[...] = jnp.zeros_like(dk_acc)
        dv_acc[...] = jnp.zeros_like(dv_acc)

        def kv_step(j, _):
            offj = pl.multiple_of(j * bq, bq)
            rows_j = pl.ds(offj, bq)
            k = jnp.concatenate([kn_ref[rows_j, :], kr_ref[rows_j, :]], axis=1)
            v = v_ref[rows_j, :]

            def q_step(i, _):
                offi = pl.multiple_of(i * bq, bq)
                rows_i = pl.ds(offi, bq)
                qv, dov = q_ref[rows_i, :], do_ref[rows_i, :]
                delta = jnp.sum(dov.astype(F32) * o_ref[rows_i, :].astype(F32), axis=1, keepdims=True)
                s = lax.dot_general(qv, k, NT, preferred_element_type=F32) * ATTN_SCALE
                s = jnp.where(_allowed(offi, offj, bq), s, -1e30)
                p = jnp.exp(s - lse_ref[0, rows_i, :])
                dv_acc[rows_j, :] += lax.dot_general(p.astype(BF16), dov, TN, preferred_element_type=F32)
                dp = lax.dot_general(dov, v, NT, preferred_element_type=F32)
                ds = (p * (dp - delta) * ATTN_SCALE).astype(BF16)
                dk_acc[rows_j, :] += lax.dot_general(ds, qv, TN, preferred_element_type=F32)
                dq_acc[rows_i, :] += jnp.dot(ds, k, preferred_element_type=F32)
                return 0

            lax.fori_loop(j, nq, q_step, 0)
            return 0

        lax.fori_loop(0, nq, kv_step, 0)

        for r in range(nq):
            rows = slice(r * bq, (r + 1) * bq)
            dq_ref[rows, 0:128] = dq_acc[rows, 0:128].astype(BF16)
            dq_ref[rows, 128:256] = _rope(dq_acc[rows, 128:256], tab_ref[rows, :], -1).astype(BF16)
        dkv_ref[:, 0:128] = dk_acc[:, 0:128].astype(BF16)
        dkv_ref[:, 128:256] = dv_acc[...].astype(BF16)

        @pl.when(h == 0)
        def _():
            dkr_ref[...] = dk_acc[:, 128:256]

        @pl.when(h > 0)
        def _():
            dkr_ref[...] += dk_acc[:, 128:256]

        @pl.when(h == N_HEADS - 1)
        def _():
            for r in range(nq):
                rows = slice(r * bq, (r + 1) * bq)
                dkr_ref[rows, :] = _rope(dkr_ref[rows, :], tab_ref[rows, :], -1)
            if carry:
                carry.finish(c_ins, c_outs, c_sems)

    W = N_HEADS * QK_PAD
    res = pl.pallas_call(
        body, name="attn_bwd", grid=(N_HEADS,),
        in_specs=[pl.BlockSpec((S, QK_PAD), lambda h: (0, h)),
                  pl.BlockSpec((S, 128), lambda h: (0, 2 * h)),
                  pl.BlockSpec((S, 128), lambda h: (0, 2 * h + 1)),
                  pl.BlockSpec((S, 128), lambda h: (0, 0)),
                  pl.BlockSpec((S, V_HEAD), lambda h: (0, h)),
                  pl.BlockSpec((S, V_HEAD), lambda h: (0, h)),
                  pl.BlockSpec((1, S, 1), lambda h: (h, 0, 0)),
                  pl.BlockSpec((S, 384), lambda h: (0, 0))] + [ANY] * n_ci,
        out_specs=[pl.BlockSpec((S, QK_PAD), lambda h: (0, h)),
                   pl.BlockSpec((S, QK_PAD), lambda h: (0, h)),
                   pl.BlockSpec((S, 128), lambda h: (0, 0))] + [ANY] * n_co,
        out_shape=[jax.ShapeDtypeStruct((S, W), BF16), jax.ShapeDtypeStruct((S, W), BF16),
                   jax.ShapeDtypeStruct((S, 128), F32)] + (carry.outs if carry else []),
        scratch_shapes=[pltpu.VMEM((S, QK_PAD), F32), pltpu.VMEM((S, QK_PAD), F32), pltpu.VMEM((S, V_HEAD), F32)]
        + (carry.sems if carry else []),
        compiler_params=_params(("arbitrary",)),
    )(q, kv, kv, kr, do, o, lse, tab, *(carry.ins if carry else []))
    return res[0], res[1], res[2], res[3:]


def _shift_down(cur, prev8, i, n):
    tm = cur.shape[0]
    prev8 = jnp.where(i == 0, jnp.zeros_like(prev8), prev8)
    full = jnp.concatenate([prev8, cur], axis=0)
    return pltpu.roll(full, n, 0)[8:8 + tm, :]


def _shift_up(cur, next8, i, last, n):
    tm = cur.shape[0]
    next8 = jnp.where(i == last, jnp.zeros_like(next8), next8)
    full = jnp.concatenate([cur, next8], axis=0)
    return pltpu.roll(full, tm + 8 - n, 0)[0:tm, :]


def _conv_fwd(pc, w_conv):
    S, D = pc.shape[0], pc.shape[1] // 3
    tm = _pick(S, (256, 128))

    def body(i, ins, outs, accs):
        b_ref, c_ref, x_ref, cp_ref, xp_ref, w_ref = ins
        z = c_ref[...] * x_ref[...]
        zp = cp_ref[...] * xp_ref[...]
        cz = w_ref[0:1, :] * _shift_down(z, zp, i, 2) + w_ref[1:2, :] * _shift_down(z, zp, i, 1) + w_ref[2:3, :] * z
        outs[0][...] = (b_ref[...] * cz).astype(BF16)

    return _rows(body, "conv_fwd", S, tm,
                 [("row", pc, 0, D), ("row", pc, 1, D), ("row", pc, 2, D), ("prev", pc, 1, D), ("prev", pc, 2, D),
                  ("full", w_conv)], [(D, BF16)])[0]


def _conv_bwd(dhb, pc, w_conv):
    S, D = dhb.shape
    tm = _pick(S, (256, 128))
    last = S // tm - 1

    def body(i, ins, outs, accs):
        g_ref, b_ref, c_ref, x_ref, cp_ref, xp_ref, gn_ref, bn_ref, w_ref = ins
        w0, w1, w2 = w_ref[0:1, :], w_ref[1:2, :], w_ref[2:3, :]
        c, x, g = c_ref[...], x_ref[...], g_ref[...]
        z = c * x
        zp = cp_ref[...] * xp_ref[...]
        z1, z2 = _shift_down(z, zp, i, 1), _shift_down(z, zp, i, 2)
        cz = w0 * z2 + w1 * z1 + w2 * z
        dcz = g * b_ref[...]
        dczn = gn_ref[...] * bn_ref[...]
        dz = w2 * dcz + w1 * _shift_up(dcz, dczn, i, last, 1) + w0 * _shift_up(dcz, dczn, i, last, 2)
        outs[0][:, 0:D] = (g * cz).astype(BF16)
        outs[0][:, D:2 * D] = (dz * x).astype(BF16)
        outs[0][:, 2 * D:3 * D] = (dz * c).astype(BF16)
        dw = jnp.concatenate([jnp.sum(dcz * z2, axis=0, keepdims=True), jnp.sum(dcz * z1, axis=0, keepdims=True),
                              jnp.sum(dcz * z, axis=0, keepdims=True)], axis=0)
        _acc_add(i, accs[0], dw)

    return _rows(body, "conv_bwd", S, tm,
                 [("row", dhb, 0, D), ("row", pc, 0, D), ("row", pc, 1, D), ("row", pc, 2, D),
                  ("prev", pc, 1, D), ("prev", pc, 2, D), ("next", dhb, 0, D), ("next", pc, 0, D), ("full", w_conv)],
                 [(3 * D, BF16)], [(3, D)])


def _merge_fwd(y_a, y_b, pg):
    S, D = y_a.shape

    def body(i, ins, outs, accs):
        ya, yb, ga, gb = ins
        outs[0][...] = (_sigmoid(ga[...]) * ya[...] + _sigmoid(gb[...]) * yb[...]).astype(BF16)

    return _rows(body, "merge_fwd", S, _pick(S, (256, 128)),
                 [("row", y_a, 0, D), ("row", y_b, 0, D), ("row", pg, 0, D), ("row", pg, 1, D)], [(D, BF16)])[0]


def _merge_bwd(dm, y_a, y_b, pg):
    S, D = dm.shape

    def body(i, ins, outs, accs):
        d, ya, yb = ins[0][...], ins[1][...], ins[2][...]
        sa, sb = _sigmoid(ins[3][...]), _sigmoid(ins[4][...])
        outs[0][...] = (d * sa).astype(BF16)
        outs[1][...] = (d * sb).astype(BF16)
        outs[2][:, 0:D] = (d * ya * (sa * (1.0 - sa))).astype(BF16)
        outs[2][:, D:2 * D] = (d * yb * (sb * (1.0 - sb))).astype(BF16)

    return _rows(body, "merge_bwd", S, _pick(S, (256, 128)),
                 [("row", dm, 0, D), ("row", y_a, 0, D), ("row", y_b, 0, D), ("row", pg, 0, D), ("row", pg, 1, D)],
                 [(D, BF16), (D, BF16), (2 * D, BF16)])


def _ln1_fwd(x, mix, gate1, g, b, scale2, shift2):
    S, D = x.shape

    def body(i, ins, outs, accs):
        x_ref, mix_ref, gate_ref, g_ref, b_ref, sc_ref, sh_ref = ins
        xh, _ = _ln_stats(ALPHA * x_ref[...] + gate_ref[...] * mix_ref[...])
        x1 = xh * g_ref[...] + b_ref[...]
        outs[0][...] = x1
        outs[1][...] = (x1 * (1.0 + sc_ref[...]) + sh_ref[...]).astype(BF16)

    return _rows(body, "ln1_fwd", S, _pick(S, (256, 128)),
                 [("row", x, 0, D), ("row", mix, 0, D), ("full", gate1), ("full", g), ("full", b),
                  ("full", scale2), ("full", shift2)], [(D, F32), (D, BF16)])


def _swiglu_fwd(hh):
    S, F = hh.shape[0], hh.shape[1] // 2

    def body(i, ins, outs, accs):
        hg = ins[0][...]
        outs[0][...] = (hg * _sigmoid(hg) * ins[1][...]).astype(BF16)

    return _rows(body, "swiglu_fwd", S, _pick(S, (128,)), [("row", hh, 0, F), ("row", hh, 1, F)], [(F, BF16)])[0]


def _swiglu_bwd(dact, hh):
    S, F = dact.shape

    def body(i, ins, outs, accs):
        d, hg, hu = ins[0][...], ins[1][...], ins[2][...]
        sg = _sigmoid(hg)
        outs[0][:, 0:F] = (d * hu * (sg * (1.0 + hg * (1.0 - sg)))).astype(BF16)
        outs[0][:, F:2 * F] = (d * (hg * sg)).astype(BF16)

    return _rows(body, "swiglu_bwd", S, _pick(S, (128,)),
                 [("row", dact, 0, F), ("row", hh, 0, F), ("row", hh, 1, F)], [(2 * F, BF16)])[0]


def _ln2_loss_bwd(x1, ffn, gate2, g, b, target):
    S, D = x1.shape

    def body(i, ins, outs, accs):
        x1_ref, f_ref, gate_ref, g_ref, b_ref, t_ref = ins
        f = f_ref[...]
        xh, rstd = _ln_stats(ALPHA * x1_ref[...] + gate_ref[...] * f)
        e = xh * g_ref[...] + b_ref[...] - t_ref[...]
        dy = e * (1.0 / D)
        dr = _ln_bwd(dy * g_ref[...], xh, rstd)
        outs[0][...] = (gate_ref[...] * dr).astype(BF16)
        outs[1][...] = ALPHA * dr
        _acc_add(i, accs[0], jnp.full((1, 128), (0.5 / D) * jnp.sum(e * e), F32))
        _acc_add(i, accs[1], jnp.sum(dy * xh, axis=0, keepdims=True))
        _acc_add(i, accs[2], jnp.sum(dy, axis=0, keepdims=True))
        _acc_add(i, accs[3], jnp.sum(dr * f, axis=0, keepdims=True))

    return _rows(body, "ln2_loss_bwd", S, _pick(S, (256, 128)),
                 [("row", x1, 0, D), ("row", ffn, 0, D), ("full", gate2), ("full", g), ("full", b), ("row", target, 0, D)],
                 [(D, BF16), (D, F32)], [(1, 128), (1, D), (1, D), (1, D)])


def _ln1_bwd(x, mix, dx1a, du2, gate1, g, b, scale2):
    S, D = x.shape

    def body(i, ins, outs, accs):
        x_ref, mix_ref, da_ref, du_ref, gate_ref, g_ref, b_ref, sc_ref = ins
        mix, du = mix_ref[...], du_ref[...]
        xh, rstd = _ln_stats(ALPHA * x_ref[...] + gate_ref[...] * mix)
        x1 = xh * g_ref[...] + b_ref[...]
        dx1 = da_ref[...] + du * (1.0 + sc_ref[...])
        dr = _ln_bwd(dx1 * g_ref[...], xh, rstd)
        outs[0][...] = (gate_ref[...] * dr).astype(BF16)
        outs[1][...] = ALPHA * dr
        _acc_add(i, accs[0], jnp.sum(du, axis=0, keepdims=True))
        _acc_add(i, accs[1], jnp.sum(du * x1, axis=0, keepdims=True))
        _acc_add(i, accs[2], jnp.sum(dx1 * xh, axis=0, keepdims=True))
        _acc_add(i, accs[3], jnp.sum(dx1, axis=0, keepdims=True))
        _acc_add(i, accs[4], jnp.sum(dr * mix, axis=0, keepdims=True))

    return _rows(body, "ln1_bwd", S, _pick(S, (256, 128)),
                 [("row", x, 0, D), ("row", mix, 0, D), ("row", dx1a, 0, D), ("row", du2, 0, D),
                  ("full", gate1), ("full", g), ("full", b), ("full", scale2)],
                 [(D, BF16), (D, F32)], [(1, D)] * 5)


def _rms_bwd(d_rq, d_rkv, pq, dkr, g_q, g_kv):
    S = pq.shape[0]

    def body(i, ins, outs, accs):
        dq_ref, dkv_ref, pq_ref, dkr_ref, gq_ref, gkv_ref = ins

        def rms_bwd(dy, x, g):
            r = lax.rsqrt(jnp.mean(x * x, axis=-1, keepdims=True) + RMS_EPS)
            dyg = dy * g
            dx = r * dyg - x * (r * r * r) * jnp.mean(dyg * x, axis=-1, keepdims=True)
            return dx, jnp.sum(dy * (x * r), axis=0, keepdims=True)

        dxq, dgq = rms_bwd(dq_ref[...], pq_ref[:, 0:Q_LORA], gq_ref[...])
        dxkv, dgkv = rms_bwd(dkv_ref[...], pq_ref[:, Q_LORA:Q_LORA + KV_LORA], gkv_ref[...])
        outs[0][:, 0:Q_LORA] = dxq.astype(BF16)
        outs[0][:, Q_LORA:Q_LORA + KV_LORA] = dxkv.astype(BF16)
        outs[0][:, Q_LORA + KV_LORA:QKV_A] = dkr_ref[...].astype(BF16)
        _acc_add(i, accs[0], dgq)
        _acc_add(i, accs[1], dgkv)

    return _rows(body, "rms_bwd", S, _pick(S, (256, 128)),
                 [("row", d_rq, 0, Q_LORA), ("row", d_rkv, 0, KV_LORA), ("row", pq, 0, QKV_A), ("row", dkr, 0, 128),
                  ("full", g_q), ("full", g_kv)], [(QKV_A, BF16)], [(1, Q_LORA), (1, KV_LORA)])


def _dx_final(dxa, du, x, scale1):
    S, D = x.shape

    def body(i, ins, outs, accs):
        du = ins[1][...]
        outs[0][...] = ins[0][...] + du * (1.0 + ins[3][...])
        _acc_add(i, accs[0], jnp.sum(du, axis=0, keepdims=True))
        _acc_add(i, accs[1], jnp.sum(du * ins[2][...], axis=0, keepdims=True))

    return _rows(body, "dx_final", S, _pick(S, (256, 128)),
                 [("row", dxa, 0, D), ("row", du, 0, D), ("row", x, 0, D), ("full", scale1)],
                 [(D, F32)], [(1, D), (1, D)])


def _ada_fwd(c_all, w, bias):
    B, D = c_all.shape
    NA = w.shape[1]
    tn = _pick(NA, (512, 256, 128))

    def body(c_ref, w_ref, b_ref, o_ref):
        cv = c_ref[...]
        ca = (cv * _sigmoid(cv)).astype(BF16)
        o_ref[...] = jnp.dot(ca, w_ref[...].astype(BF16), preferred_element_type=F32) + b_ref[...]

    return pl.pallas_call(
        body, name="ada_fwd", grid=(NA // tn,),
        in_specs=[pl.BlockSpec((B, D), lambda j: (0, 0)), pl.BlockSpec((D, tn), lambda j: (0, j)),
                  pl.BlockSpec((1, tn), lambda j: (0, j))],
        out_specs=pl.BlockSpec((B, tn), lambda j: (0, j)),
        out_shape=jax.ShapeDtypeStruct((B, NA), F32),
        compiler_params=_params(("arbitrary",)),
    )(c_all, w, bias)


def _ada_bwd(c_all, dmod):
    B, D = c_all.shape
    NA = dmod.shape[1]
    tn = _pick(NA, (512, 256, 128))

    def body(c_ref, d_ref, o_ref):
        cv = c_ref[...]
        ca = (cv * _sigmoid(cv)).astype(BF16)
        o_ref[...] = lax.dot_general(ca, d_ref[...].astype(BF16), TN, preferred_element_type=F32)

    return pl.pallas_call(
        body, name="ada_bwd", grid=(NA // tn,),
        in_specs=[pl.BlockSpec((B, D), lambda j: (0, 0)), pl.BlockSpec((B, tn), lambda j: (0, j))],
        out_specs=pl.BlockSpec((D, tn), lambda j: (0, j)),
        out_shape=jax.ShapeDtypeStruct((D, NA), F32),
        compiler_params=_params(("arbitrary",)),
    )(c_all, dmod)


def _sum8(parts):
    _, R, N = parts.shape

    def body(p_ref, o_ref):
        acc = p_ref[0]
        for d in range(1, 8):
            acc = acc + p_ref[d]
        o_ref[...] = acc

    return pl.pallas_call(body, name="sum8", out_shape=jax.ShapeDtypeStruct((R, N), F32),
                          compiler_params=_params())(parts)


def _adam_math(w, g, m, v):
    m = ADAM_B1 * m + (1.0 - ADAM_B1) * g
    v = ADAM_B2 * v + (1.0 - ADAM_B2) * (g * g)
    delta = -ADAM_LR * ((m / ADAM_C1) / (jnp.sqrt(v / ADAM_C2) + ADAM_EPS) + ADAM_WD * w)
    return delta, m, v


def _adam(name, w, m, v, g_parts):
    R, C = w.shape
    tm = _row_tile(R, C * 4, 1 << 20)
    n = len(g_parts)

    def body(*refs):
        w_ref, m_ref, v_ref = refs[0], refs[1], refs[2]
        g = refs[3][...]
        for r in refs[4:3 + n]:
            g = g + r[...]
        g_ref, d_ref, nm_ref, nv_ref = refs[3 + n:]
        delta, nm, nv = _adam_math(w_ref[...], g, m_ref[...], v_ref[...])
        g_ref[...] = g
        d_ref[...] = delta
        nm_ref[...] = nm
        nv_ref[...] = nv

    spec = pl.BlockSpec((tm, C), lambda i: (i, 0))
    return pl.pallas_call(
        body, name=name, grid=(R // tm,), in_specs=[spec] * (3 + n), out_specs=[spec] * 4,
        out_shape=[jax.ShapeDtypeStruct((R, C), F32)] * 4, compiler_params=_params(("parallel",)),
    )(w, m, v, *g_parts)


def _adam_halves(name, w, m, v, mine, other, core):
    R, C = w.shape
    Rh = R // 2
    tm = _row_tile(Rh, C * 4, 1 << 20)
    nh = Rh // tm

    def body(c_ref, w_ref, m_ref, v_ref, a_ref, b_ref, g_ref, d_ref, nm_ref, nv_ref):
        g = jnp.where(pl.program_id(0) // nh == c_ref[0], a_ref[...], b_ref[...])
        delta, nm, nv = _adam_math(w_ref[...], g, m_ref[...], v_ref[...])
        g_ref[...] = g
        d_ref[...] = delta
        nm_ref[...] = nm
        nv_ref[...] = nv

    spec = pl.BlockSpec((tm, C), lambda i, c_ref: (i, 0))
    a_spec = pl.BlockSpec((tm, C), lambda i, c_ref: (jnp.where(i // nh == c_ref[0], i % nh, 0), 0))
    b_spec = pl.BlockSpec((tm, C), lambda i, c_ref: (jnp.where(i // nh == c_ref[0], 0, i % nh), 0))
    return pl.pallas_call(
        body, name=name, out_shape=[jax.ShapeDtypeStruct((R, C), F32)] * 4,
        grid_spec=pltpu.PrefetchScalarGridSpec(num_scalar_prefetch=1, grid=(R // tm,),
                                               in_specs=[spec, spec, spec, a_spec, b_spec], out_specs=[spec] * 4),
        compiler_params=_params(("arbitrary",)),
    )(core, w, m, v, mine, other)


def _adam_small(name, w, m, v, g):
    def body(w_ref, m_ref, v_ref, g_ref, d_ref, nm_ref, nv_ref):
        delta, nm, nv = _adam_math(w_ref[...], g_ref[...], m_ref[...], v_ref[...])
        d_ref[...] = delta
        nm_ref[...] = nm
        nv_ref[...] = nv

    return pl.pallas_call(body, name=name, out_shape=[jax.ShapeDtypeStruct(w.shape, F32)] * 3,
                          compiler_params=_params())(w, m, v, g)


def _place():
    return lax.axis_index("x"), lax.axis_index("y"), lax.axis_index("c")


def _other_chips(x, y):
    return [(1 - x, y), (x, 1 - y), (1 - x, 1 - y)]


def _all_gather8(blk, name):
    R, N = blk.shape

    def body(x_ref, out_ref, send_sems, recv_sems, local_sem):
        x, y, c = _place()
        me = 4 * x + 2 * y + c
        mine = pltpu.make_async_copy(x_ref, out_ref.at[me], local_sem)
        mine.start()
        flips = [(j >> 2 & 1, j >> 1 & 1, j & 1) for j in range(1, 8)]
        peers = [((1 - x) if fx else x, (1 - y) if fy else y, (1 - c) if fc else c) for fx, fy, fc in flips]
        sends = []
        for j, peer in enumerate(peers):
            cp = pltpu.make_async_remote_copy(src_ref=x_ref, dst_ref=out_ref.at[me], send_sem=send_sems.at[j],
                                              recv_sem=recv_sems.at[j], device_id=peer, device_id_type=MESH)
            cp.start()
            sends.append(cp)
        for j, (px, py, pc) in enumerate(peers):
            pltpu.make_async_remote_copy(src_ref=x_ref, dst_ref=out_ref.at[4 * px + 2 * py + pc],
                                         send_sem=send_sems.at[j], recv_sem=recv_sems.at[j],
                                         device_id=(px, py, pc), device_id_type=MESH).wait_recv()
        for cp in sends:
            cp.wait_send()
        mine.wait()

    return pl.pallas_call(
        body, name=name, out_shape=jax.ShapeDtypeStruct((8, R, N), F32),
        in_specs=[pl.BlockSpec(memory_space=pltpu.VMEM)], out_specs=pl.BlockSpec(memory_space=pltpu.VMEM),
        scratch_shapes=[pltpu.SemaphoreType.DMA((7,)), pltpu.SemaphoreType.DMA((7,)), pltpu.SemaphoreType.DMA],
        compiler_params=_params(),
    )(blk)


def _scatter_plan(arrs):
    n = len(arrs)

    def copies(ins, outs, sems):
        send_sems, recv_sems = sems
        x, y, c = _place()
        chips = _other_chips(x, y)
        return [pltpu.make_async_remote_copy(src_ref=ins[k].at[2 * px + py], dst_ref=outs[k].at[j],
                                             send_sem=send_sems.at[3 * k + j], recv_sem=recv_sems.at[3 * k + j],
                                             device_id=(px, py, c), device_id_type=MESH)
                for k in range(n) for j, (px, py) in enumerate(chips)]

    def start(ins, outs, sems):
        for cp in copies(ins, outs, sems):
            cp.start()

    def finish(ins, outs, sems):
        for cp in copies(ins, outs, sems):
            cp.wait()

    return _Plan(arrs, [jax.ShapeDtypeStruct((3,) + a.shape[1:], a.dtype) for a in arrs],
                 [pltpu.SemaphoreType.DMA((3 * n,))] * 2, start, finish)


def _gather_plan(shards):
    n = len(shards)

    def parts(ins, outs, sems):
        s1, r1, s2, r2, loc = sems
        x, y, c = _place()
        me = 2 * x + y
        chips = _other_chips(x, y)
        sib = (x, y, 1 - c)

        def ici(k, j, slab, to):
            return pltpu.make_async_remote_copy(src_ref=ins[k].at[c], dst_ref=outs[k].at[slab, c], send_sem=s1.at[3 * k + j],
                                                recv_sem=r1.at[3 * k + j], device_id=to, device_id_type=MESH)

        def d2d(k, j, slab, half):
            return pltpu.make_async_remote_copy(src_ref=outs[k].at[slab, half], dst_ref=outs[k].at[slab, half],
                                                send_sem=s2.at[3 * k + j], recv_sem=r2.at[3 * k + j],
                                                device_id=sib, device_id_type=MESH)

        def own(k):
            return pltpu.make_async_remote_copy(src_ref=ins[k], dst_ref=outs[k].at[me], send_sem=loc.at[2 * k],
                                                recv_sem=loc.at[2 * k + 1], device_id=sib, device_id_type=MESH)

        return c, me, chips, ici, d2d, own

    def start(ins, outs, sems):
        c, me, chips, ici, d2d, own = parts(ins, outs, sems)
        for k in range(n):
            for j, (px, py) in enumerate(chips):
                ici(k, j, me, (px, py, c)).start()
        for k in range(n):
            own(k).start()

    def finish(ins, outs, sems):
        c, me, chips, ici, d2d, own = parts(ins, outs, sems)
        for k in range(n):
            for j, (px, py) in enumerate(chips):
                ici(k, j, 2 * px + py, (px, py, c)).wait_recv()
                d2d(k, j, 2 * px + py, c).start()
        for k in range(n):
            for j, (px, py) in enumerate(chips):
                d2d(k, j, 2 * px + py, 1 - c).wait_recv()
        for k in range(n):
            own(k).wait()
            for j, (px, py) in enumerate(chips):
                ici(k, j, me, (px, py, c)).wait_send()
                d2d(k, j, 2 * px + py, c).wait_send()

    return _Plan(shards, [jax.ShapeDtypeStruct((4,) + a.shape, a.dtype) for a in shards],
                 [pltpu.SemaphoreType.DMA((3 * n,))] * 4 + [pltpu.SemaphoreType.DMA((2 * n,))], start, finish)


def _pair_plan(parts):
    n = len(parts)

    def copies(ins, outs, sems):
        send_sems, recv_sems = sems
        x, y, c = _place()
        return [pltpu.make_async_remote_copy(src_ref=ins[k].at[p, 1 - c], dst_ref=outs[k].at[p],
                                             send_sem=send_sems.at[4 * k + p], recv_sem=recv_sems.at[4 * k + p],
                                             device_id=(x, y, 1 - c), device_id_type=MESH)
                for k in range(n) for p in range(4)]

    def start(ins, outs, sems):
        for cp in copies(ins, outs, sems):
            cp.start()

    def finish(ins, outs, sems):
        for cp in copies(ins, outs, sems):
            cp.wait()

    return _Plan(parts, [jax.ShapeDtypeStruct((4,) + a.shape[2:], a.dtype) for a in parts],
                 [pltpu.SemaphoreType.DMA((4 * n,))] * 2, start, finish)


def _sibling_plan(arrs):
    n = len(arrs)

    def copies(ins, outs, sems):
        send_sems, recv_sems = sems
        x, y, c = _place()
        return [pltpu.make_async_remote_copy(src_ref=ins[k], dst_ref=outs[k], send_sem=send_sems.at[k],
                                             recv_sem=recv_sems.at[k], device_id=(x, y, 1 - c), device_id_type=MESH)
                for k in range(n)]

    def start(ins, outs, sems):
        for cp in copies(ins, outs, sems):
            cp.start()

    def finish(ins, outs, sems):
        for cp in copies(ins, outs, sems):
            cp.wait()

    return _Plan(arrs, [jax.ShapeDtypeStruct(a.shape, a.dtype) for a in arrs],
                 [pltpu.SemaphoreType.DMA((n,))] * 2, start, finish)


def _join_plans(plans):
    def split(seq, counts):
        out, at = [], 0
        for cnt in counts:
            out.append(seq[at:at + cnt])
            at += cnt
        return out

    n_i, n_o, n_s = ([len(getattr(p, f)) for p in plans] for f in ("ins", "outs", "sems"))

    def start(ins, outs, sems):
        for p, i, o, s in zip(plans, split(ins, n_i), split(outs, n_o), split(sems, n_s)):
            p.start(i, o, s)

    def finish(ins, outs, sems):
        for p, i, o, s in zip(plans, split(ins, n_i), split(outs, n_o), split(sems, n_s)):
            p.finish(i, o, s)

    return _Plan(sum((p.ins for p in plans), []), sum((p.outs for p in plans), []), sum((p.sems for p in plans), []),
                 start, finish)


def _add_pair(parts, sib, core, name):
    P4, _, Rh, C = parts.shape
    tm = _row_tile(Rh, C * 4, 1 << 20, 16)

    def body(c_ref, a_ref, b_ref, o_ref):
        o_ref[...] = (a_ref[0].astype(F32) + b_ref[...].astype(F32)).astype(BF16)

    spec = pl.BlockSpec((1, tm, C), lambda p, i, c_ref: (p, i, 0))
    return pl.pallas_call(
        body, name=name, out_shape=jax.ShapeDtypeStruct((P4, Rh, C), BF16),
        grid_spec=pltpu.PrefetchScalarGridSpec(
            num_scalar_prefetch=1, grid=(P4, Rh // tm),
            in_specs=[pl.BlockSpec((1, 1, tm, C), lambda p, i, c_ref: (p, c_ref[0], i, 0)), spec], out_specs=spec),
        compiler_params=_params(("parallel", "parallel")),
    )(core, parts, sib)


def _sum_slabs(pre, recv, chip, name):
    _, Rh, C = pre.shape
    tm = _row_tile(Rh, C * 4, 1 << 20, 16)

    def body(me_ref, own_ref, r_ref, o_ref):
        acc = own_ref[0].astype(F32)
        for j in range(3):
            acc = acc + r_ref[j].astype(F32)
        o_ref[...] = acc

    return pl.pallas_call(
        body, name=name, out_shape=jax.ShapeDtypeStruct((Rh, C), F32),
        grid_spec=pltpu.PrefetchScalarGridSpec(
            num_scalar_prefetch=1, grid=(Rh // tm,),
            in_specs=[pl.BlockSpec((1, tm, C), lambda i, me_ref: (me_ref[0], i, 0)),
                      pl.BlockSpec((3, tm, C), lambda i, me_ref: (0, i, 0))],
            out_specs=pl.BlockSpec((tm, C), lambda i, me_ref: (i, 0))),
        compiler_params=_params(("parallel",)),
    )(chip, pre, recv)


def kernel(x, c, positions, w_ada, b_ada, w_in, g_q_a, w_q_b, g_kv_a, w_kv_b, w_o_a, w_conv, w_o_b, w_o, ln1_g, ln1_b, w_ffn_in, w_ffn_out, ln2_g, ln2_b, loss_target, m_w_ada, m_b_ada, m_w_in, m_g_q_a, m_w_q_b, m_g_kv_a, m_w_kv_b, m_w_o_a, m_w_conv, m_w_o_b, m_w_o, m_ln1_g, m_ln1_b, m_w_ffn_in, m_w_ffn_out, m_ln2_g, m_ln2_b, v_w_ada, v_b_ada, v_w_in, v_g_q_a, v_w_q_b, v_g_kv_a, v_w_kv_b, v_w_o_a, v_w_conv, v_w_o_b, v_w_o, v_ln1_g, v_ln1_b, v_w_ffn_in, v_w_ffn_out, v_ln2_g, v_ln2_b):
    S, D = x.shape[1], x.shape[2]
    F = w_ffn_out.shape[1] * 4
    ax, ay, ac = _place()
    chip = 2 * ax + ay
    dev = 4 * ax + 2 * ay + ac
    x2, tgt = x[0], loss_target[0]
    w_ada2, w_in2, w_q_b2, w_kv_b2 = w_ada[0], w_in[0], w_q_b[0], w_kv_b[0]
    w_o_a2, w_o_b2, w_o2, w_ffn_in2, w_ffn_out2 = w_o_a[0], w_o_b[0], w_o[0], w_ffn_in[0], w_ffn_out[0]
    NA = w_ada2.shape[1]
    CW = w_conv.shape[2]

    inv_freq = 1.0 / (ROPE_THETA ** (jnp.arange(0, QK_ROPE, 2, dtype=F32) / QK_ROPE))
    ang = positions[0].astype(F32)[:, None] * inv_freq
    cos, sin = jnp.cos(ang), jnp.sin(ang)
    z32, z64, z96 = jnp.zeros((S, 32), F32), jnp.zeros((S, 64), F32), jnp.zeros((S, 96), F32)
    tab = jnp.concatenate([cos, cos, z64, -sin, z96, z32, sin, z64], axis=1)

    def halves(a):
        return a.reshape(2, a.shape[0] // 2, a.shape[1])

    def whole(g):
        return g.reshape(4, 2 * g.shape[2], g.shape[3])

    def cols(g):
        return jnp.transpose(g, (1, 0, 2)).reshape(g.shape[1], 4 * g.shape[2])

    sh_in, sh_qb, sh_kvb, sh_oa, sh_ob, sh_o, sh_fi, sh_fo = (
        halves(w.astype(BF16)) for w in (w_in2, w_q_b2, w_kv_b2, w_o_a2, w_o_b2, w_o2, w_ffn_in2, w_ffn_out2))
    g_in, g_qb, g_kvb = (whole(g) for g in _run_plan(_gather_plan([sh_in, sh_qb, sh_kvb]), "gather_first"))
    W_in = cols(g_in)
    n_qkv = Q_LORA + KV_LORA + QK_ROPE
    W_qkv = jnp.pad(W_in[:, :n_qkv], ((0, 0), (0, QKV_A - n_qkv)))
    W_conv = W_in[:, n_qkv:n_qkv + 3 * D]
    W_gate = W_in[:, n_qkv + 3 * D:]
    W_qb = jnp.pad(cols(g_qb).reshape(Q_LORA, N_HEADS, QK_NOPE + QK_ROPE),
                   ((0, 0), (0, 0), (0, QK_PAD - QK_NOPE - QK_ROPE))).reshape(Q_LORA, N_HEADS * QK_PAD)
    W_kvb = cols(g_kvb)

    c_all = _all_gather8(c, "gather_c").reshape(8, D)
    wconv_all = _all_gather8(w_conv[0], "gather_wconv")
    w_conv_full = jnp.transpose(wconv_all[0::2], (1, 0, 2)).reshape(3, D)
    b_sh = lax.dynamic_slice(b_ada, (0, chip * NA), (1, NA))
    mod_sh = _ada_fwd(c_all, w_ada2, b_sh)
    mod_all = _all_gather8(mod_sh, "gather_mod")
    mod = lax.dynamic_slice(mod_all[0::2], (0, dev, 0), (4, 1, NA)).reshape(6, D)
    shift1, scale1, gate1, shift2, scale2, gate2 = (mod[k:k + 1] for k in range(6))

    u = _modulate(x2, scale1, shift1, "modulate1")
    pq = _matmul(u, W_qkv, "nn", F32, "proj_qkv")
    pc, (g_oa, g_ob) = _matmul(u, W_conv, "nn", F32, "proj_conv", carry=_gather_plan([sh_oa, sh_ob]))
    pg, (g_o,) = _matmul(u, W_gate, "nn", F32, "proj_gate", carry=_gather_plan([sh_o]))
    W_oa, W_ob, W_o = (g.reshape(-1, D) for g in (g_oa, g_ob, g_o))
    rq, rkv, kr = _rms_fwd(pq, tab, g_q_a, g_kv_a)
    q = _q_rope(_matmul(rq, W_qb, "nn", F32, "q_b"), tab)
    kv = _matmul(rkv, W_kvb, "nn", BF16, "kv_b")
    o, lse, (g_fi,) = _attn_fwd(q, kv, kr, carry=_gather_plan([sh_fi]))
    W_fi = cols(whole(g_fi))
    y_a = _matmul(o, W_oa, "nn", F32, "o_a")
    hb = _conv_fwd(pc, w_conv_full)
    y_b = _matmul(hb, W_ob, "nn", F32, "o_b")
    merged = _merge_fwd(y_a, y_b, pg)
    mix = _matmul(merged, W_o, "nn", F32, "w_o")
    x1, u2 = _ln1_fwd(x2, mix, gate1, ln1_g, ln1_b, scale2, shift2)
    hh, (g_fo,) = _matmul(u2, W_fi, "nn", F32, "ffn_in", carry=_gather_plan([sh_fo]))
    W_fo = g_fo.reshape(F, D)
    act = _swiglu_fwd(hh)
    ffn = _matmul(act, W_fo, "nn", F32, "ffn_out")

    core_i = ac.astype(jnp.int32).reshape(1)
    chip_i = chip.astype(jnp.int32).reshape(1)

    def uncols(g):
        return jnp.transpose(g.reshape(g.shape[0], 4, g.shape[1] // 4), (1, 0, 2))

    def slabs(p):
        return p.reshape(4, 2, p.shape[1] // 2, p.shape[2])

    def add_pairs(parts, sibs, nms):
        return [_add_pair(a, b, core_i, "add_pair_" + nm) for a, b, nm in zip(parts, sibs, nms)]

    def sum_all(pre, recv, nms):
        return [_sum_slabs(a, r, chip_i, "sum_slabs_" + nm) for a, r, nm in zip(pre, recv, nms)]

    dffn, dx1a, loss_acc, d_ln2_g, d_ln2_b, d_gate2 = _ln2_loss_bwd(x1, ffn, gate2, ln2_g, ln2_b, tgt)
    loss = lax.psum(loss_acc[0, 0], ("x", "y", "c"))
    dW_fo = _matmul(act, dffn, "tn", BF16, "d_w_ffn_out")
    p_fo = [slabs(dW_fo.reshape(4, -1, D))]
    dact, s_fo = _matmul(dffn, W_fo, "nt", F32, "d_act", carry=_pair_plan(p_fo))
    pre_fo = add_pairs(p_fo, s_fo, ["w_ffn_out"])
    dhh = _swiglu_bwd(dact, hh)
    dW_fi, r_fo = _matmul(u2, dhh, "tn", BF16, "d_w_ffn_in", carry=_scatter_plan(pre_fo))
    p_fi = [slabs(uncols(dW_fi))]
    du2, s_fi = _matmul(dhh, W_fi, "nt", F32, "d_u2", carry=_pair_plan(p_fi))
    pre_fi = add_pairs(p_fi, s_fi, ["w_ffn_in"])
    dmix, dxa, d_shift2, d_scale2, d_ln1_g, d_ln1_b, d_gate1 = _ln1_bwd(x2, mix, dx1a, du2, gate1, ln1_g, ln1_b, scale2)
    dW_o = _matmul(merged, dmix, "tn", BF16, "d_w_o")
    dmerged = _matmul(dmix, W_o, "nt", F32, "d_merged")
    dy_a, dy_b, dgate = _merge_bwd(dmerged, y_a, y_b, pg)
    dW_oa = _matmul(o, dy_a, "tn", BF16, "d_w_o_a")
    do = _matmul(dy_a, W_oa, "nt", BF16, "d_o")
    dW_ob = _matmul(hb, dy_b, "tn", BF16, "d_w_o_b")
    p_mid = [slabs(g.reshape(4, -1, D)) for g in (dW_oa, dW_ob, dW_o)]
    dhb, s_mid = _matmul(dy_b, W_ob, "nt", F32, "d_hb", carry=_pair_plan(p_mid))
    pre_mid = add_pairs(p_mid, s_mid, ["w_o_a", "w_o_b", "w_o"])
    dconv, d_wconv = _conv_bwd(dhb, pc, w_conv_full)
    dq, dkv, dkr, r_mid = _attn_bwd(q, kv, kr, do, o, lse, tab, carry=_scatter_plan(pre_fi + pre_mid))
    names_a = ["w_ffn_out", "w_ffn_in", "w_o_a", "w_o_b", "w_o"]
    fin_a = sum_all(pre_fo + pre_fi + pre_mid, list(r_fo) + list(r_mid), names_a)
    dW_qb = _matmul(rq, dq, "tn", BF16, "d_w_q_b")
    d_rq = _matmul(dq, W_qb, "nt", F32, "d_rq")
    dW_kvb = _matmul(rkv, dkv, "tn", BF16, "d_w_kv_b")
    d_rkv = _matmul(dkv, W_kvb, "nt", F32, "d_rkv")
    dqkv, d_g_q, d_g_kv = _rms_bwd(d_rq, d_rkv, pq, dkr, g_q_a, g_kv_a)
    dW_qkv = _matmul(u, dqkv, "tn", BF16, "d_w_qkv")
    dW_conv, fs_a = _matmul(u, dconv, "tn", BF16, "d_w_conv", carry=_sibling_plan(fin_a))
    dW_gate = _matmul(u, dgate, "tn", BF16, "d_w_gate")
    dW_in = jnp.concatenate([dW_qkv[:, :n_qkv], dW_conv, dW_gate], axis=1)
    dW_qb_u = dW_qb.reshape(Q_LORA, N_HEADS, QK_PAD)[:, :, :QK_NOPE + QK_ROPE].reshape(Q_LORA, -1)
    names_b = ["w_in", "w_q_b", "w_kv_b"]
    p_b = [slabs(uncols(g)) for g in (dW_in, dW_qb_u, dW_kvb)]
    du, s_b = _matmul(dqkv, W_qkv, "nt", F32, "d_u_qkv", carry=_pair_plan(p_b))
    pre_b = add_pairs(p_b, s_b, names_b)
    du = _matmul(dconv, W_conv, "nt", F32, "d_u_conv", add=du)
    du = _matmul(dgate, W_gate, "nt", F32, "d_u_gate", add=du)
    grad_x, d_shift1, d_scale1 = _dx_final(dxa, du, x2, scale1)
    r_b = _run_plan(_scatter_plan(pre_b), "scatter_last")
    fin_b = sum_all(pre_b, r_b, names_b)
    fs_b = _run_plan(_sibling_plan(fin_b), "sibling_last")
    names = names_a + names_b
    fin, fin_sib = fin_a + fin_b, list(fs_a) + list(fs_b)

    def pad_d(v):
        return jnp.pad(v, ((0, 0), (0, D - v.shape[1])))

    small = jnp.concatenate([d_ln1_g, d_ln1_b, d_ln2_g, d_ln2_b, pad_d(d_g_q), pad_d(d_g_kv), d_wconv,
                             d_shift1, d_scale1, d_gate1, d_shift2, d_scale2, d_gate2, jnp.zeros((1, D), F32)], axis=0)
    small_all = _all_gather8(small, "gather_small")
    small_sum = _sum8(small_all)
    g_ln1_g, g_ln1_b, g_ln2_g, g_ln2_b = (small_sum[k:k + 1] for k in range(4))
    g_g_q, g_g_kv = small_sum[4:5, :Q_LORA], small_sum[5:6, :KV_LORA]
    g_wconv = lax.dynamic_slice(small_sum[6:9], (0, chip * CW), (3, CW))
    g_b_ada = small_sum[9:15].reshape(1, 6 * D)
    dmod_all = small_all[:, 9:15, :].reshape(8, 6 * D)
    g_w_ada = _ada_bwd(c_all, lax.dynamic_slice(dmod_all, (0, chip * NA), (8, NA)))

    big = {}
    ws = dict(w_in=(w_in2, m_w_in[0], v_w_in[0]), w_q_b=(w_q_b2, m_w_q_b[0], v_w_q_b[0]),
              w_kv_b=(w_kv_b2, m_w_kv_b[0], v_w_kv_b[0]), w_o_a=(w_o_a2, m_w_o_a[0], v_w_o_a[0]),
              w_o_b=(w_o_b2, m_w_o_b[0], v_w_o_b[0]), w_o=(w_o2, m_w_o[0], v_w_o[0]),
              w_ffn_in=(w_ffn_in2, m_w_ffn_in[0], v_w_ffn_in[0]), w_ffn_out=(w_ffn_out2, m_w_ffn_out[0], v_w_ffn_out[0]))
    for nm, a, b in zip(names, fin, fin_sib):
        w_, m_, v_ = ws[nm]
        big[nm] = _adam_halves("adam_" + nm, w_, m_, v_, a, b, core_i)
    big["w_ada"] = _adam("adam_w_ada", w_ada2, m_w_ada[0], v_w_ada[0], [g_w_ada])
    sm = {}
    for nm, w_, m_, v_, g_ in [("b_ada", b_ada, m_b_ada, v_b_ada, g_b_ada), ("g_q_a", g_q_a, m_g_q_a, v_g_q_a, g_g_q),
                               ("g_kv_a", g_kv_a, m_g_kv_a, v_g_kv_a, g_g_kv),
                               ("w_conv", w_conv[0], m_w_conv[0], v_w_conv[0], g_wconv),
                               ("ln1_g", ln1_g, m_ln1_g, v_ln1_g, g_ln1_g), ("ln1_b", ln1_b, m_ln1_b, v_ln1_b, g_ln1_b),
                               ("ln2_g", ln2_g, m_ln2_g, v_ln2_g, g_ln2_g), ("ln2_b", ln2_b, m_ln2_b, v_ln2_b, g_ln2_b)]:
        sm[nm] = (g_,) + tuple(_adam_small("adam_" + nm, w_, m_, v_, g_))

    order = ["w_ada", "b_ada", "w_in", "g_q_a", "w_q_b", "g_kv_a", "w_kv_b", "w_o_a", "w_conv", "w_o_b", "w_o",
             "ln1_g", "ln1_b", "w_ffn_in", "w_ffn_out", "ln2_g", "ln2_b"]
    lead = {"b_ada", "g_q_a", "g_kv_a", "ln1_g", "ln1_b", "ln2_g", "ln2_b"}

    def leaf(nm, k):
        val = big[nm][k] if nm in big else sm[nm][k]
        return val if nm in lead else val[None]

    outs = [loss, grad_x[None]]
    for k in range(4):
        outs += [leaf(nm, k) for nm in order]
    return tuple(outs)
```

```python
import functools

import jax
import jax.numpy as jnp
from jax import lax
from jax.experimental import pallas as pl
from jax.experimental.pallas import tpu as pltpu

F32, BF16 = jnp.float32, jnp.bfloat16
N_HEADS, QK_NOPE, QK_ROPE, V_HEAD = 16, 128, 64, 128
Q_LORA, KV_LORA = 512, 512
QK_PAD = 256
QKV_A = 1152
CHUNK_SHIFT = 6
ATTN_SCALE = (QK_NOPE + QK_ROPE) ** -0.5
ROPE_THETA = 10000.0
ALPHA = 2.0 ** 0.25
LN_EPS, RMS_EPS = 1e-5, 1e-6
ADAM_LR, ADAM_B1, ADAM_B2, ADAM_EPS, ADAM_WD, ADAM_STEP = 0.001, 0.9, 0.999, 1e-08, 0.01, 10
ADAM_C1 = 1.0 - ADAM_B1 ** ADAM_STEP
ADAM_C2 = 1.0 - ADAM_B2 ** ADAM_STEP
VMEM_LIMIT = 56 * 1024 * 1024
MESH = pl.DeviceIdType.MESH
ANY = pl.BlockSpec(memory_space=pl.ANY)
NT = (((1,), (1,)), ((), ()))
TN = (((0,), (0,)), ((), ()))
NN = (((1,), (0,)), ((), ()))


def _params(sem=None):
    return pltpu.CompilerParams(dimension_semantics=sem, vmem_limit_bytes=VMEM_LIMIT)


def _pick(n, cands=(1024, 512, 384, 256, 128)):
    for t in cands:
        if n % t == 0:
            return t
    return n


def _row_tile(rows, row_bytes, budget, mult=8):
    best = mult
    for t in range(mult, rows + 1, mult):
        if rows % t == 0 and t * row_bytes <= budget:
            best = t
    return best


def _sigmoid(x):
    return jax.nn.sigmoid(x)


class _Plan:
    def __init__(self, ins, outs, sems, start, finish):
        self.ins, self.outs, self.sems, self.start, self.finish = list(ins), list(outs), list(sems), start, finish


def _run_plan(plan, name):
    n_in, n_out = len(plan.ins), len(plan.outs)

    def body(*refs):
        ins, outs, sems = refs[:n_in], refs[n_in:n_in + n_out], refs[n_in + n_out:]
        plan.start(ins, outs, sems)
        plan.finish(ins, outs, sems)

    return pl.pallas_call(body, name=name, out_shape=plan.outs, in_specs=[ANY] * n_in, out_specs=[ANY] * n_out,
                          scratch_shapes=plan.sems, compiler_params=_params())(*plan.ins)


def _matmul(a, b, mode, out_dtype, name, add=None, carry=None):
    if mode == "nn":
        (M, K), N, dims = a.shape, b.shape[1], NN
    elif mode == "nt":
        (M, K), N, dims = a.shape, b.shape[0], NT
    else:
        (K, M), N, dims = a.shape, b.shape[1], TN
    tm, tn, tk = _pick(M), _pick(N), _pick(K)
    nk = K // tk
    a_spec = (pl.BlockSpec((tk, tm), lambda i, j, k: (k, i)) if mode == "tn"
              else pl.BlockSpec((tm, tk), lambda i, j, k: (i, k)))
    b_spec = (pl.BlockSpec((tn, tk), lambda i, j, k: (j, k)) if mode == "nt"
              else pl.BlockSpec((tk, tn), lambda i, j, k: (k, j)))
    o_spec = pl.BlockSpec((tm, tn), lambda i, j, k: (i, j))
    has_add = add is not None
    n_ci = len(carry.ins) if carry else 0
    n_co = len(carry.outs) if carry else 0
    n_in = 2 + has_add
    grid = (M // tm, N // tn, nk)

    def body(*refs):
        a_ref, b_ref = refs[0], refs[1]
        add_ref = refs[2] if has_add else None
        o_ref = refs[n_in + n_ci]
        acc_ref = refs[n_in + n_ci + 1 + n_co]
        c_ins = refs[n_in:n_in + n_ci]
        c_outs = refs[n_in + n_ci + 1:n_in + n_ci + 1 + n_co]
        c_sems = refs[n_in + n_ci + 2 + n_co:]
        i, j, k = pl.program_id(0), pl.program_id(1), pl.program_id(2)

        if carry:
            @pl.when((i == 0) & (j == 0) & (k == 0))
            def _():
                carry.start(c_ins, c_outs, c_sems)

        @pl.when(k == 0)
        def _():
            acc_ref[...] = jnp.zeros_like(acc_ref)

        acc_ref[...] += lax.dot_general(a_ref[...], b_ref[...], dims, preferred_element_type=F32)

        @pl.when(k == nk - 1)
        def _():
            r = acc_ref[...]
            if has_add:
                r = r + add_ref[...]
            o_ref[...] = r.astype(o_ref.dtype)

        if carry:
            @pl.when((i == grid[0] - 1) & (j == grid[1] - 1) & (k == nk - 1))
            def _():
                carry.finish(c_ins, c_outs, c_sems)

    ins = [a, b] + ([add] if has_add else []) + (carry.ins if carry else [])
    in_specs = [a_spec, b_spec] + ([o_spec] if has_add else []) + [ANY] * n_ci
    res = pl.pallas_call(
        body, name=name, grid=grid,
        in_specs=in_specs, out_specs=[o_spec] + [ANY] * n_co,
        out_shape=[jax.ShapeDtypeStruct((M, N), out_dtype)] + (carry.outs if carry else []),
        scratch_shapes=[pltpu.VMEM((tm, tn), F32)] + (carry.sems if carry else []),
        compiler_params=_params(("arbitrary",) * 3 if carry else ("parallel", "parallel", "arbitrary")),
    )(*ins)
    return (res[0], res[1:]) if carry else res[0]


def _rows(body, name, n_rows, tm, ins, outs, accs=()):
    grid = (n_rows // tm,)
    per8 = tm // 8
    last8 = n_rows // 8 - 1
    arrays, in_specs = [], []
    for spec in ins:
        kind, arr = spec[0], spec[1]
        arrays.append(arr)
        if kind == "row":
            _, _, cb, w = spec
            in_specs.append(pl.BlockSpec((tm, w), lambda i, cb=cb: (i, cb)))
        elif kind == "full":
            in_specs.append(pl.BlockSpec(arr.shape, lambda i, nd=arr.ndim: (0,) * nd))
        elif kind == "prev":
            _, _, cb, w = spec
            in_specs.append(pl.BlockSpec((8, w), lambda i, cb=cb: (jnp.maximum(i * per8 - 1, 0), cb)))
        else:
            _, _, cb, w = spec
            in_specs.append(pl.BlockSpec((8, w), lambda i, cb=cb: (jnp.minimum((i + 1) * per8, last8), cb)))
    out_shape = [jax.ShapeDtypeStruct((n_rows, w), dt) for (w, dt) in outs]
    out_specs = [pl.BlockSpec((tm, w), lambda i: (i, 0)) for (w, _) in outs]
    out_shape += [jax.ShapeDtypeStruct(s, F32) for s in accs]
    out_specs += [pl.BlockSpec(s, lambda i, nd=len(s): (0,) * nd) for s in accs]
    n_in, n_out = len(ins), len(outs)

    def kernel_body(*refs):
        body(pl.program_id(0), refs[:n_in], refs[n_in:n_in + n_out], refs[n_in + n_out:])

    res = pl.pallas_call(
        kernel_body, name=name, grid=grid, in_specs=in_specs, out_specs=out_specs, out_shape=out_shape,
        compiler_params=_params(("arbitrary",)),
    )(*arrays)
    return res


def _acc_add(i, ref, val):
    @pl.when(i == 0)
    def _():
        ref[...] = val

    @pl.when(i > 0)
    def _():
        ref[...] += val


def _rope(t, tab, sign):
    c, sa, sb = tab[:, 0:128], tab[:, 128:256], tab[:, 256:384]
    rot = pltpu.roll(t, 96, 1) * sa + pltpu.roll(t, 32, 1) * sb
    return t * c + rot if sign > 0 else t * c - rot


def _ln_stats(r):
    mu = jnp.mean(r, axis=-1, keepdims=True)
    d = r - mu
    var = jnp.mean(d * d, axis=-1, keepdims=True)
    rstd = lax.rsqrt(var + LN_EPS)
    return d * rstd, rstd


def _ln_bwd(dxh, xh, rstd):
    m1 = jnp.mean(dxh, axis=-1, keepdims=True)
    m2 = jnp.mean(dxh * xh, axis=-1, keepdims=True)
    return rstd * (dxh - m1 - xh * m2)


def _modulate(x, scale, shift, name):
    S, D = x.shape

    def body(i, ins, outs, accs):
        outs[0][...] = (ins[0][...] * (1.0 + ins[1][...]) + ins[2][...]).astype(BF16)

    return _rows(body, name, S, _pick(S, (256, 128)), [("row", x, 0, D), ("full", scale), ("full", shift)], [(D, BF16)])[0]


def _rms_fwd(pq, tab, g_q, g_kv):
    S = pq.shape[0]

    def body(i, ins, outs, accs):
        pq_ref, tab_ref, gq_ref, gkv_ref = ins

        def rms(x, g):
            return x * lax.rsqrt(jnp.mean(x * x, axis=-1, keepdims=True) + RMS_EPS) * g

        outs[0][...] = rms(pq_ref[:, 0:Q_LORA], gq_ref[...]).astype(BF16)
        outs[1][...] = rms(pq_ref[:, Q_LORA:Q_LORA + KV_LORA], gkv_ref[...]).astype(BF16)
        outs[2][...] = _rope(pq_ref[:, Q_LORA + KV_LORA:QKV_A], tab_ref[...], 1).astype(BF16)

    return _rows(body, "rms_fwd", S, _pick(S, (256, 128)),
                 [("row", pq, 0, QKV_A), ("row", tab, 0, 384), ("full", g_q), ("full", g_kv)],
                 [(Q_LORA, BF16), (KV_LORA, BF16), (128, BF16)])


def _q_rope(q, tab):
    S, W = q.shape

    def body(i, ins, outs, accs):
        q_ref, tab_ref = ins
        t = tab_ref[...]
        for h in range(N_HEADS):
            lo = h * QK_PAD
            outs[0][:, lo:lo + 128] = q_ref[:, lo:lo + 128].astype(BF16)
            outs[0][:, lo + 128:lo + 256] = _rope(q_ref[:, lo + 128:lo + 256], t, 1).astype(BF16)

    return _rows(body, "q_rope", S, _pick(S, (256, 128)), [("row", q, 0, W), ("row", tab, 0, 384)], [(W, BF16)])[0]


def _allowed(q0, k0, bq):
    row = q0 + lax.broadcasted_iota(jnp.int32, (bq, bq), 0)
    col = k0 + lax.broadcasted_iota(jnp.int32, (bq, bq), 1)
    return (col >> CHUNK_SHIFT) <= (row >> CHUNK_SHIFT)


ATTN_BLOCK = 512


def _attn_fwd(q, kv, kr, carry=None):
    S = q.shape[0]
    bq = min(ATTN_BLOCK, S)
    nq = S // bq
    n_ci = len(carry.ins) if carry else 0
    n_co = len(carry.outs) if carry else 0

    def body(*refs):
        q_ref, kn_ref, v_ref, kr_ref = refs[:4]
        o_ref, lse_ref = refs[4 + n_ci:6 + n_ci]
        c_ins, c_outs = refs[4:4 + n_ci], refs[6 + n_ci:6 + n_ci + n_co]
        kcat = refs[6 + n_ci + n_co]
        c_sems = refs[7 + n_ci + n_co:]
        qi = pl.program_id(1)
        if carry:
            @pl.when((pl.program_id(0) == 0) & (qi == 0))
            def _():
                carry.start(c_ins, c_outs, c_sems)

        @pl.when(qi == 0)
        def _():
            kcat[:, 0:128] = kn_ref[...]
            kcat[:, 128:256] = kr_ref[...]

        qv = q_ref[...]

        def step(j, carry, masked):
            m, l, acc = carry
            off = pl.multiple_of(j * bq, bq)
            s = lax.dot_general(qv, kcat[pl.ds(off, bq), :], NT, preferred_element_type=F32) * ATTN_SCALE
            if masked:
                s = jnp.where(_allowed(qi * bq, off, bq), s, -1e30)
            m_new = jnp.maximum(m, jnp.max(s, axis=1, keepdims=True))
            a = jnp.exp(m - m_new)
            p = jnp.exp(s - m_new)
            l = a * l + jnp.sum(p, axis=1, keepdims=True)
            acc = a * acc + jnp.dot(p.astype(BF16), v_ref[pl.ds(off, bq), :], preferred_element_type=F32)
            return m_new, l, acc

        init = (jnp.full((bq, 1), -1e30, F32), jnp.zeros((bq, 1), F32), jnp.zeros((bq, V_HEAD), F32))
        below = lax.fori_loop(0, qi, lambda j, cr: step(j, cr, False), init)
        m, l, acc = step(qi, below, True)
        o_ref[...] = (acc / l).astype(BF16)
        lse_ref[0] = m + jnp.log(l)
        if carry:
            @pl.when((pl.program_id(0) == N_HEADS - 1) & (qi == nq - 1))
            def _():
                carry.finish(c_ins, c_outs, c_sems)

    res = pl.pallas_call(
        body, name="attn_fwd", grid=(N_HEADS, nq),
        in_specs=[pl.BlockSpec((bq, QK_PAD), lambda h, i: (i, h)),
                  pl.BlockSpec((S, 128), lambda h, i: (0, 2 * h)),
                  pl.BlockSpec((S, 128), lambda h, i: (0, 2 * h + 1)),
                  pl.BlockSpec((S, 128), lambda h, i: (0, 0))] + [ANY] * n_ci,
        out_specs=[pl.BlockSpec((bq, V_HEAD), lambda h, i: (i, h)),
                   pl.BlockSpec((1, bq, 1), lambda h, i: (h, i, 0))] + [ANY] * n_co,
        out_shape=[jax.ShapeDtypeStruct((S, N_HEADS * V_HEAD), BF16),
                   jax.ShapeDtypeStruct((N_HEADS, S, 1), F32)] + (carry.outs if carry else []),
        scratch_shapes=[pltpu.VMEM((S, QK_PAD), BF16)] + (carry.sems if carry else []),
        compiler_params=_params(("arbitrary", "arbitrary")),
    )(q, kv, kv, kr, *(carry.ins if carry else []))
    return res[0], res[1], res[2:]


def _attn_bwd(q, kv, kr, do, o, lse, tab, carry=None):
    S = q.shape[0]
    bq = min(ATTN_BLOCK, S)
    nq = S // bq

    n_ci = len(carry.ins) if carry else 0
    n_co = len(carry.outs) if carry else 0

    def body(*refs):
        q_ref, kn_ref, v_ref, kr_ref, do_ref, o_ref, lse_ref, tab_ref = refs[:8]
        dq_ref, dkv_ref, dkr_ref = refs[8 + n_ci:11 + n_ci]
        dq_acc, dk_acc, dv_acc, kcat, delta = refs[11 + n_ci + n_co:16 + n_ci + n_co]
        c_ins, c_outs, c_sems = refs[8:8 + n_ci], refs[11 + n_ci:11 + n_ci + n_co], refs[16 + n_ci + n_co:]
        h = pl.program_id(0)
        if carry:
            @pl.when(h == 0)
            def _():
                carry.start(c_ins, c_outs, c_sems)

        dq_acc[...] = jnp.zeros_like(dq_acc)
        dk_acc[...] = jnp.zeros_like(dk_acc)
        dv_acc[...] = jnp.zeros_like(dv_acc)
        kcat[:, 0:128] = kn_ref[...]
        kcat[:, 128:256] = kr_ref[...]
        for r in range(nq):
            rows = slice(r * bq, (r + 1) * bq)
            delta[rows, :] = jnp.sum(do_ref[rows, :].astype(F32) * o_ref[rows, :].astype(F32), axis=1, keepdims=True)

        def pair(i, j, masked):
            rows_i = pl.ds(pl.multiple_of(i * bq, bq), bq)
            rows_j = pl.ds(pl.multiple_of(j * bq, bq), bq)
            qv, dov, k = q_ref[rows_i, :], do_ref[rows_i, :], kcat[rows_j, :]
            s = lax.dot_general(qv, k, NT, preferred_element_type=F32) * ATTN_SCALE
            if masked:
                s = jnp.where(_allowed(i * bq, j * bq, bq), s, -1e30)
            p = jnp.exp(s - lse_ref[0, rows_i, :])
            dv_acc[rows_j, :] += lax.dot_general(p.astype(BF16), dov, TN, preferred_element_type=F32)
            dp = lax.dot_general(dov, v_ref[rows_j, :], NT, preferred_element_type=F32)
            ds = (p * (dp - delta[rows_i, :]) * ATTN_SCALE).astype(BF16)
            dk_acc[rows_j, :] += lax.dot_general(ds, qv, TN, preferred_element_type=F32)
            dq_acc[rows_i, :] += jnp.dot(ds, k, preferred_element_type=F32)

        def kv_step(j, _):
            pair(j, j, True)

            def q_step(i, _):
                pair(i, j, False)
                return 0

            lax.fori_loop(j + 1, nq, q_step, 0)
            return 0

        lax.fori_loop(0, nq, kv_step, 0)

        for r in range(nq):
            rows = slice(r * bq, (r + 1) * bq)
            dq_ref[rows, 0:128] = dq_acc[rows, 0:128].astype(BF16)
            dq_ref[rows, 128:256] = _rope(dq_acc[rows, 128:256], tab_ref[rows, :], -1).astype(BF16)
        dkv_ref[:, 0:128] = dk_acc[:, 0:128].astype(BF16)
        dkv_ref[:, 128:256] = dv_acc[...].astype(BF16)

        @pl.when(h == 0)
        def _():
            dkr_ref[...] = dk_acc[:, 128:256]

        @pl.when(h > 0)
        def _():
            dkr_ref[...] += dk_acc[:, 128:256]

        @pl.when(h == N_HEADS - 1)
        def _():
            for r in range(nq):
                rows = slice(r * bq, (r + 1) * bq)
                dkr_ref[rows, :] = _rope(dkr_ref[rows, :], tab_ref[rows, :], -1)
            if carry:
                carry.finish(c_ins, c_outs, c_sems)

    W = N_HEADS * QK_PAD
    res = pl.pallas_call(
        body, name="attn_bwd", grid=(N_HEADS,),
        in_specs=[pl.BlockSpec((S, QK_PAD), lambda h: (0, h)),
                  pl.BlockSpec((S, 128), lambda h: (0, 2 * h)),
                  pl.BlockSpec((S, 128), lambda h: (0, 2 * h + 1)),
                  pl.BlockSpec((S, 128), lambda h: (0, 0)),
                  pl.BlockSpec((S, V_HEAD), lambda h: (0, h)),
                  pl.BlockSpec((S, V_HEAD), lambda h: (0, h)),
                  pl.BlockSpec((1, S, 1), lambda h: (h, 0, 0)),
                  pl.BlockSpec((S, 384), lambda h: (0, 0))] + [ANY] * n_ci,
        out_specs=[pl.BlockSpec((S, QK_PAD), lambda h: (0, h)),
                   pl.BlockSpec((S, QK_PAD), lambda h: (0, h)),
                   pl.BlockSpec((S, 128), lambda h: (0, 0))] + [ANY] * n_co,
        out_shape=[jax.ShapeDtypeStruct((S, W), BF16), jax.ShapeDtypeStruct((S, W), BF16),
                   jax.ShapeDtypeStruct((S, 128), F32)] + (carry.outs if carry else []),
        scratch_shapes=[pltpu.VMEM((S, QK_PAD), F32), pltpu.VMEM((S, QK_PAD), F32), pltpu.VMEM((S, V_HEAD), F32),
                        pltpu.VMEM((S, QK_PAD), BF16), pltpu.VMEM((S, 1), F32)]
        + (carry.sems if carry else []),
        compiler_params=_params(("arbitrary",)),
    )(q, kv, kv, kr, do, o, lse, tab, *(carry.ins if carry else []))
    return res[0], res[1], res[2], res[3:]


def _shift_down(cur, prev8, i, n):
    tm = cur.shape[0]
    prev8 = jnp.where(i == 0, jnp.zeros_like(prev8), prev8)
    full = jnp.concatenate([prev8, cur], axis=0)
    return pltpu.roll(full, n, 0)[8:8 + tm, :]


def _shift_up(cur, next8, i, last, n):
    tm = cur.shape[0]
    next8 = jnp.where(i == last, jnp.zeros_like(next8), next8)
    full = jnp.concatenate([cur, next8], axis=0)
    return pltpu.roll(full, tm + 8 - n, 0)[0:tm, :]


def _conv_fwd(pc, w_conv):
    S, D = pc.shape[0], pc.shape[1] // 3
    tm = _pick(S, (256, 128))

    def body(i, ins, outs, accs):
        b_ref, c_ref, x_ref, cp_ref, xp_ref, w_ref = ins
        z = c_ref[...] * x_ref[...]
        zp = cp_ref[...] * xp_ref[...]
        cz = w_ref[0:1, :] * _shift_down(z, zp, i, 2) + w_ref[1:2, :] * _shift_down(z, zp, i, 1) + w_ref[2:3, :] * z
        outs[0][...] = (b_ref[...] * cz).astype(BF16)

    return _rows(body, "conv_fwd", S, tm,
                 [("row", pc, 0, D), ("row", pc, 1, D), ("row", pc, 2, D), ("prev", pc, 1, D), ("prev", pc, 2, D),
                  ("full", w_conv)], [(D, BF16)])[0]


def _conv_bwd(dhb, pc, w_conv):
    S, D = dhb.shape
    tm = _pick(S, (256, 128))
    last = S // tm - 1

    def body(i, ins, outs, accs):
        g_ref, b_ref, c_ref, x_ref, cp_ref, xp_ref, gn_ref, bn_ref, w_ref = ins
        w0, w1, w2 = w_ref[0:1, :], w_ref[1:2, :], w_ref[2:3, :]
        c, x, g = c_ref[...], x_ref[...], g_ref[...]
        z = c * x
        zp = cp_ref[...] * xp_ref[...]
        z1, z2 = _shift_down(z, zp, i, 1), _shift_down(z, zp, i, 2)
        cz = w0 * z2 + w1 * z1 + w2 * z
        dcz = g * b_ref[...]
        dczn = gn_ref[...] * bn_ref[...]
        dz = w2 * dcz + w1 * _shift_up(dcz, dczn, i, last, 1) + w0 * _shift_up(dcz, dczn, i, last, 2)
        outs[0][:, 0:D] = (g * cz).astype(BF16)
        outs[0][:, D:2 * D] = (dz * x).astype(BF16)
        outs[0][:, 2 * D:3 * D] = (dz * c).astype(BF16)
        dw = jnp.concatenate([jnp.sum(dcz * z2, axis=0, keepdims=True), jnp.sum(dcz * z1, axis=0, keepdims=True),
                              jnp.sum(dcz * z, axis=0, keepdims=True)], axis=0)
        _acc_add(i, accs[0], dw)

    return _rows(body, "conv_bwd", S, tm,
                 [("row", dhb, 0, D), ("row", pc, 0, D), ("row", pc, 1, D), ("row", pc, 2, D),
                  ("prev", pc, 1, D), ("prev", pc, 2, D), ("next", dhb, 0, D), ("next", pc, 0, D), ("full", w_conv)],
                 [(3 * D, BF16)], [(3, D)])


def _merge_fwd(y_a, y_b, pg):
    S, D = y_a.shape

    def body(i, ins, outs, accs):
        ya, yb, ga, gb = ins
        outs[0][...] = (_sigmoid(ga[...]) * ya[...] + _sigmoid(gb[...]) * yb[...]).astype(BF16)

    return _rows(body, "merge_fwd", S, _pick(S, (256, 128)),
                 [("row", y_a, 0, D), ("row", y_b, 0, D), ("row", pg, 0, D), ("row", pg, 1, D)], [(D, BF16)])[0]


def _merge_bwd(dm, y_a, y_b, pg):
    S, D = dm.shape

    def body(i, ins, outs, accs):
        d, ya, yb = ins[0][...], ins[1][...], ins[2][...]
        sa, sb = _sigmoid(ins[3][...]), _sigmoid(ins[4][...])
        outs[0][...] = (d * sa).astype(BF16)
        outs[1][...] = (d * sb).astype(BF16)
        outs[2][:, 0:D] = (d * ya * (sa * (1.0 - sa))).astype(BF16)
        outs[2][:, D:2 * D] = (d * yb * (sb * (1.0 - sb))).astype(BF16)

    return _rows(body, "merge_bwd", S, _pick(S, (256, 128)),
                 [("row", dm, 0, D), ("row", y_a, 0, D), ("row", y_b, 0, D), ("row", pg, 0, D), ("row", pg, 1, D)],
                 [(D, BF16), (D, BF16), (2 * D, BF16)])


def _ln1_fwd(x, mix, gate1, g, b, scale2, shift2):
    S, D = x.shape

    def body(i, ins, outs, accs):
        x_ref, mix_ref, gate_ref, g_ref, b_ref, sc_ref, sh_ref = ins
        xh, _ = _ln_stats(ALPHA * x_ref[...] + gate_ref[...] * mix_ref[...])
        x1 = xh * g_ref[...] + b_ref[...]
        outs[0][...] = x1
        outs[1][...] = (x1 * (1.0 + sc_ref[...]) + sh_ref[...]).astype(BF16)

    return _rows(body, "ln1_fwd", S, _pick(S, (256, 128)),
                 [("row", x, 0, D), ("row", mix, 0, D), ("full", gate1), ("full", g), ("full", b),
                  ("full", scale2), ("full", shift2)], [(D, F32), (D, BF16)])


def _swiglu_fwd(hh):
    S, F = hh.shape[0], hh.shape[1] // 2

    def body(i, ins, outs, accs):
        hg = ins[0][...]
        outs[0][...] = (hg * _sigmoid(hg) * ins[1][...]).astype(BF16)

    return _rows(body, "swiglu_fwd", S, _pick(S, (128,)), [("row", hh, 0, F), ("row", hh, 1, F)], [(F, BF16)])[0]


def _swiglu_bwd(dact, hh):
    S, F = dact.shape

    def body(i, ins, outs, accs):
        d, hg, hu = ins[0][...], ins[1][...], ins[2][...]
        sg = _sigmoid(hg)
        outs[0][:, 0:F] = (d * hu * (sg * (1.0 + hg * (1.0 - sg)))).astype(BF16)
        outs[0][:, F:2 * F] = (d * (hg * sg)).astype(BF16)

    return _rows(body, "swiglu_bwd", S, _pick(S, (128,)),
                 [("row", dact, 0, F), ("row", hh, 0, F), ("row", hh, 1, F)], [(2 * F, BF16)])[0]


def _ln2_loss_bwd(x1, ffn, gate2, g, b, target):
    S, D = x1.shape

    def body(i, ins, outs, accs):
        x1_ref, f_ref, gate_ref, g_ref, b_ref, t_ref = ins
        f = f_ref[...]
        xh, rstd = _ln_stats(ALPHA * x1_ref[...] + gate_ref[...] * f)
        e = xh * g_ref[...] + b_ref[...] - t_ref[...]
        dy = e * (1.0 / D)
        dr = _ln_bwd(dy * g_ref[...], xh, rstd)
        outs[0][...] = (gate_ref[...] * dr).astype(BF16)
        outs[1][...] = ALPHA * dr
        _acc_add(i, accs[0], jnp.full((1, 128), (0.5 / D) * jnp.sum(e * e), F32))
        _acc_add(i, accs[1], jnp.sum(dy * xh, axis=0, keepdims=True))
        _acc_add(i, accs[2], jnp.sum(dy, axis=0, keepdims=True))
        _acc_add(i, accs[3], jnp.sum(dr * f, axis=0, keepdims=True))

    return _rows(body, "ln2_loss_bwd", S, _pick(S, (256, 128)),
                 [("row", x1, 0, D), ("row", ffn, 0, D), ("full", gate2), ("full", g), ("full", b), ("row", target, 0, D)],
                 [(D, BF16), (D, F32)], [(1, 128), (1, D), (1, D), (1, D)])


def _ln1_bwd(x, mix, dx1a, du2, gate1, g, b, scale2):
    S, D = x.shape

    def body(i, ins, outs, accs):
        x_ref, mix_ref, da_ref, du_ref, gate_ref, g_ref, b_ref, sc_ref = ins
        mix, du = mix_ref[...], du_ref[...]
        xh, rstd = _ln_stats(ALPHA * x_ref[...] + gate_ref[...] * mix)
        x1 = xh * g_ref[...] + b_ref[...]
        dx1 = da_ref[...] + du * (1.0 + sc_ref[...])
        dr = _ln_bwd(dx1 * g_ref[...], xh, rstd)
        outs[0][...] = (gate_ref[...] * dr).astype(BF16)
        outs[1][...] = ALPHA * dr
        _acc_add(i, accs[0], jnp.sum(du, axis=0, keepdims=True))
        _acc_add(i, accs[1], jnp.sum(du * x1, axis=0, keepdims=True))
        _acc_add(i, accs[2], jnp.sum(dx1 * xh, axis=0, keepdims=True))
        _acc_add(i, accs[3], jnp.sum(dx1, axis=0, keepdims=True))
        _acc_add(i, accs[4], jnp.sum(dr * mix, axis=0, keepdims=True))

    return _rows(body, "ln1_bwd", S, _pick(S, (256, 128)),
                 [("row", x, 0, D), ("row", mix, 0, D), ("row", dx1a, 0, D), ("row", du2, 0, D),
                  ("full", gate1), ("full", g), ("full", b), ("full", scale2)],
                 [(D, BF16), (D, F32)], [(1, D)] * 5)


def _rms_bwd(d_rq, d_rkv, pq, dkr, g_q, g_kv):
    S = pq.shape[0]

    def body(i, ins, outs, accs):
        dq_ref, dkv_ref, pq_ref, dkr_ref, gq_ref, gkv_ref = ins

        def rms_bwd(dy, x, g):
            r = lax.rsqrt(jnp.mean(x * x, axis=-1, keepdims=True) + RMS_EPS)
            dyg = dy * g
            dx = r * dyg - x * (r * r * r) * jnp.mean(dyg * x, axis=-1, keepdims=True)
            return dx, jnp.sum(dy * (x * r), axis=0, keepdims=True)

        dxq, dgq = rms_bwd(dq_ref[...], pq_ref[:, 0:Q_LORA], gq_ref[...])
        dxkv, dgkv = rms_bwd(dkv_ref[...], pq_ref[:, Q_LORA:Q_LORA + KV_LORA], gkv_ref[...])
        outs[0][:, 0:Q_LORA] = dxq.astype(BF16)
        outs[0][:, Q_LORA:Q_LORA + KV_LORA] = dxkv.astype(BF16)
        outs[0][:, Q_LORA + KV_LORA:QKV_A] = dkr_ref[...].astype(BF16)
        _acc_add(i, accs[0], dgq)
        _acc_add(i, accs[1], dgkv)

    return _rows(body, "rms_bwd", S, _pick(S, (256, 128)),
                 [("row", d_rq, 0, Q_LORA), ("row", d_rkv, 0, KV_LORA), ("row", pq, 0, QKV_A), ("row", dkr, 0, 128),
                  ("full", g_q), ("full", g_kv)], [(QKV_A, BF16)], [(1, Q_LORA), (1, KV_LORA)])


def _dx_final(dxa, du, x, scale1):
    S, D = x.shape

    def body(i, ins, outs, accs):
        du = ins[1][...]
        outs[0][...] = ins[0][...] + du * (1.0 + ins[3][...])
        _acc_add(i, accs[0], jnp.sum(du, axis=0, keepdims=True))
        _acc_add(i, accs[1], jnp.sum(du * ins[2][...], axis=0, keepdims=True))

    return _rows(body, "dx_final", S, _pick(S, (256, 128)),
                 [("row", dxa, 0, D), ("row", du, 0, D), ("row", x, 0, D), ("full", scale1)],
                 [(D, F32)], [(1, D), (1, D)])


def _ada_fwd(c_all, w, bias):
    B, D = c_all.shape
    NA = w.shape[1]
    tn = _pick(NA, (512, 256, 128))

    def body(c_ref, w_ref, b_ref, o_ref):
        cv = c_ref[...]
        ca = (cv * _sigmoid(cv)).astype(BF16)
        o_ref[...] = jnp.dot(ca, w_ref[...].astype(BF16), preferred_element_type=F32) + b_ref[...]

    return pl.pallas_call(
        body, name="ada_fwd", grid=(NA // tn,),
        in_specs=[pl.BlockSpec((B, D), lambda j: (0, 0)), pl.BlockSpec((D, tn), lambda j: (0, j)),
                  pl.BlockSpec((1, tn), lambda j: (0, j))],
        out_specs=pl.BlockSpec((B, tn), lambda j: (0, j)),
        out_shape=jax.ShapeDtypeStruct((B, NA), F32),
        compiler_params=_params(("arbitrary",)),
    )(c_all, w, bias)


def _ada_bwd(c_all, dmod):
    B, D = c_all.shape
    NA = dmod.shape[1]
    tn = _pick(NA, (512, 256, 128))

    def body(c_ref, d_ref, o_ref):
        cv = c_ref[...]
        ca = (cv * _sigmoid(cv)).astype(BF16)
        o_ref[...] = lax.dot_general(ca, d_ref[...].astype(BF16), TN, preferred_element_type=F32)

    return pl.pallas_call(
        body, name="ada_bwd", grid=(NA // tn,),
        in_specs=[pl.BlockSpec((B, D), lambda j: (0, 0)), pl.BlockSpec((B, tn), lambda j: (0, j))],
        out_specs=pl.BlockSpec((D, tn), lambda j: (0, j)),
        out_shape=jax.ShapeDtypeStruct((D, NA), F32),
        compiler_params=_params(("arbitrary",)),
    )(c_all, dmod)


def _sum8(parts):
    _, R, N = parts.shape

    def body(p_ref, o_ref):
        acc = p_ref[0]
        for d in range(1, 8):
            acc = acc + p_ref[d]
        o_ref[...] = acc

    return pl.pallas_call(body, name="sum8", out_shape=jax.ShapeDtypeStruct((R, N), F32),
                          compiler_params=_params())(parts)


def _adam_math(w, g, m, v):
    m = ADAM_B1 * m + (1.0 - ADAM_B1) * g
    v = ADAM_B2 * v + (1.0 - ADAM_B2) * (g * g)
    delta = -ADAM_LR * ((m / ADAM_C1) / (jnp.sqrt(v / ADAM_C2) + ADAM_EPS) + ADAM_WD * w)
    return delta, m, v


def _adam(name, w, m, v, g_parts):
    R, C = w.shape
    tm = _row_tile(R, C * 4, 1 << 20)
    n = len(g_parts)

    def body(*refs):
        w_ref, m_ref, v_ref = refs[0], refs[1], refs[2]
        g = refs[3][...]
        for r in refs[4:3 + n]:
            g = g + r[...]
        g_ref, d_ref, nm_ref, nv_ref = refs[3 + n:]
        delta, nm, nv = _adam_math(w_ref[...], g, m_ref[...], v_ref[...])
        g_ref[...] = g
        d_ref[...] = delta
        nm_ref[...] = nm
        nv_ref[...] = nv

    spec = pl.BlockSpec((tm, C), lambda i: (i, 0))
    return pl.pallas_call(
        body, name=name, grid=(R // tm,), in_specs=[spec] * (3 + n), out_specs=[spec] * 4,
        out_shape=[jax.ShapeDtypeStruct((R, C), F32)] * 4, compiler_params=_params(("parallel",)),
    )(w, m, v, *g_parts)


def _adam_halves(name, w, m, v, mine, other, core):
    R, C = w.shape
    Rh = R // 2
    tm = _row_tile(Rh, C * 4, 1 << 20)
    nh = Rh // tm

    def body(c_ref, w_ref, m_ref, v_ref, a_ref, b_ref, g_ref, d_ref, nm_ref, nv_ref):
        g = jnp.where(pl.program_id(0) // nh == c_ref[0], a_ref[...], b_ref[...])
        delta, nm, nv = _adam_math(w_ref[...], g, m_ref[...], v_ref[...])
        g_ref[...] = g
        d_ref[...] = delta
        nm_ref[...] = nm
        nv_ref[...] = nv

    spec = pl.BlockSpec((tm, C), lambda i, c_ref: (i, 0))
    a_spec = pl.BlockSpec((tm, C), lambda i, c_ref: (jnp.where(i // nh == c_ref[0], i % nh, 0), 0))
    b_spec = pl.BlockSpec((tm, C), lambda i, c_ref: (jnp.where(i // nh == c_ref[0], 0, i % nh), 0))
    return pl.pallas_call(
        body, name=name, out_shape=[jax.ShapeDtypeStruct((R, C), F32)] * 4,
        grid_spec=pltpu.PrefetchScalarGridSpec(num_scalar_prefetch=1, grid=(R // tm,),
                                               in_specs=[spec, spec, spec, a_spec, b_spec], out_specs=[spec] * 4),
        compiler_params=_params(("arbitrary",)),
    )(core, w, m, v, mine, other)


def _adam_small(name, w, m, v, g):
    def body(w_ref, m_ref, v_ref, g_ref, d_ref, nm_ref, nv_ref):
        delta, nm, nv = _adam_math(w_ref[...], g_ref[...], m_ref[...], v_ref[...])
        d_ref[...] = delta
        nm_ref[...] = nm
        nv_ref[...] = nv

    return pl.pallas_call(body, name=name, out_shape=[jax.ShapeDtypeStruct(w.shape, F32)] * 3,
                          compiler_params=_params())(w, m, v, g)


def _place():
    return lax.axis_index("x"), lax.axis_index("y"), lax.axis_index("c")


def _other_chips(x, y):
    return [(1 - x, y), (x, 1 - y), (1 - x, 1 - y)]


def _all_gather8(blk, name):
    R, N = blk.shape

    def body(x_ref, out_ref, send_sems, recv_sems, local_sem):
        x, y, c = _place()
        me = 4 * x + 2 * y + c
        mine = pltpu.make_async_copy(x_ref, out_ref.at[me], local_sem)
        mine.start()
        flips = [(j >> 2 & 1, j >> 1 & 1, j & 1) for j in range(1, 8)]
        peers = [((1 - x) if fx else x, (1 - y) if fy else y, (1 - c) if fc else c) for fx, fy, fc in flips]
        sends = []
        for j, peer in enumerate(peers):
            cp = pltpu.make_async_remote_copy(src_ref=x_ref, dst_ref=out_ref.at[me], send_sem=send_sems.at[j],
                                              recv_sem=recv_sems.at[j], device_id=peer, device_id_type=MESH)
            cp.start()
            sends.append(cp)
        for j, (px, py, pc) in enumerate(peers):
            pltpu.make_async_remote_copy(src_ref=x_ref, dst_ref=out_ref.at[4 * px + 2 * py + pc],
                                         send_sem=send_sems.at[j], recv_sem=recv_sems.at[j],
                                         device_id=(px, py, pc), device_id_type=MESH).wait_recv()
        for cp in sends:
            cp.wait_send()
        mine.wait()

    return pl.pallas_call(
        body, name=name, out_shape=jax.ShapeDtypeStruct((8, R, N), F32),
        in_specs=[pl.BlockSpec(memory_space=pltpu.VMEM)], out_specs=pl.BlockSpec(memory_space=pltpu.VMEM),
        scratch_shapes=[pltpu.SemaphoreType.DMA((7,)), pltpu.SemaphoreType.DMA((7,)), pltpu.SemaphoreType.DMA],
        compiler_params=_params(),
    )(blk)


def _scatter_plan(arrs):
    n = len(arrs)

    def copies(ins, outs, sems):
        send_sems, recv_sems = sems
        x, y, c = _place()
        chips = _other_chips(x, y)
        return [pltpu.make_async_remote_copy(src_ref=ins[k].at[2 * px + py], dst_ref=outs[k].at[j],
                                             send_sem=send_sems.at[3 * k + j], recv_sem=recv_sems.at[3 * k + j],
                                             device_id=(px, py, c), device_id_type=MESH)
                for k in range(n) for j, (px, py) in enumerate(chips)]

    def start(ins, outs, sems):
        for cp in copies(ins, outs, sems):
            cp.start()

    def finish(ins, outs, sems):
        for cp in copies(ins, outs, sems):
            cp.wait()

    return _Plan(arrs, [jax.ShapeDtypeStruct((3,) + a.shape[1:], a.dtype) for a in arrs],
                 [pltpu.SemaphoreType.DMA((3 * n,))] * 2, start, finish)


def _gather_plan(shards):
    n = len(shards)

    def parts(ins, outs, sems):
        s1, r1, s2, r2, loc = sems
        x, y, c = _place()
        me = 2 * x + y
        chips = _other_chips(x, y)
        sib = (x, y, 1 - c)

        def ici(k, j, slab, to):
            return pltpu.make_async_remote_copy(src_ref=ins[k].at[c], dst_ref=outs[k].at[slab, c], send_sem=s1.at[3 * k + j],
                                                recv_sem=r1.at[3 * k + j], device_id=to, device_id_type=MESH)

        def d2d(k, j, slab, half):
            return pltpu.make_async_remote_copy(src_ref=outs[k].at[slab, half], dst_ref=outs[k].at[slab, half],
                                                send_sem=s2.at[3 * k + j], recv_sem=r2.at[3 * k + j],
                                                device_id=sib, device_id_type=MESH)

        def own(k):
            return pltpu.make_async_remote_copy(src_ref=ins[k], dst_ref=outs[k].at[me], send_sem=loc.at[2 * k],
                                                recv_sem=loc.at[2 * k + 1], device_id=sib, device_id_type=MESH)

        return c, me, chips, ici, d2d, own

    def start(ins, outs, sems):
        c, me, chips, ici, d2d, own = parts(ins, outs, sems)
        for k in range(n):
            for j, (px, py) in enumerate(chips):
                ici(k, j, me, (px, py, c)).start()
        for k in range(n):
            own(k).start()

    def finish(ins, outs, sems):
        c, me, chips, ici, d2d, own = parts(ins, outs, sems)
        for k in range(n):
            for j, (px, py) in enumerate(chips):
                ici(k, j, 2 * px + py, (px, py, c)).wait_recv()
                d2d(k, j, 2 * px + py, c).start()
        for k in range(n):
            for j, (px, py) in enumerate(chips):
                d2d(k, j, 2 * px + py, 1 - c).wait_recv()
        for k in range(n):
            own(k).wait()
            for j, (px, py) in enumerate(chips):
                ici(k, j, me, (px, py, c)).wait_send()
                d2d(k, j, 2 * px + py, c).wait_send()

    return _Plan(shards, [jax.ShapeDtypeStruct((4,) + a.shape, a.dtype) for a in shards],
                 [pltpu.SemaphoreType.DMA((3 * n,))] * 4 + [pltpu.SemaphoreType.DMA((2 * n,))], start, finish)


def _pair_plan(parts):
    n = len(parts)

    def copies(ins, outs, sems):
        send_sems, recv_sems = sems
        x, y, c = _place()
        return [pltpu.make_async_remote_copy(src_ref=ins[k].at[p, 1 - c], dst_ref=outs[k].at[p],
                                             send_sem=send_sems.at[4 * k + p], recv_sem=recv_sems.at[4 * k + p],
                                             device_id=(x, y, 1 - c), device_id_type=MESH)
                for k in range(n) for p in range(4)]

    def start(ins, outs, sems):
        for cp in copies(ins, outs, sems):
            cp.start()

    def finish(ins, outs, sems):
        for cp in copies(ins, outs, sems):
            cp.wait()

    return _Plan(parts, [jax.ShapeDtypeStruct((4,) + a.shape[2:], a.dtype) for a in parts],
                 [pltpu.SemaphoreType.DMA((4 * n,))] * 2, start, finish)


def _sibling_plan(arrs):
    n = len(arrs)

    def copies(ins, outs, sems):
        send_sems, recv_sems = sems
        x, y, c = _place()
        return [pltpu.make_async_remote_copy(src_ref=ins[k], dst_ref=outs[k], send_sem=send_sems.at[k],
                                             recv_sem=recv_sems.at[k], device_id=(x, y, 1 - c), device_id_type=MESH)
                for k in range(n)]

    def start(ins, outs, sems):
        for cp in copies(ins, outs, sems):
            cp.start()

    def finish(ins, outs, sems):
        for cp in copies(ins, outs, sems):
            cp.wait()

    return _Plan(arrs, [jax.ShapeDtypeStruct(a.shape, a.dtype) for a in arrs],
                 [pltpu.SemaphoreType.DMA((n,))] * 2, start, finish)


def _join_plans(plans):
    def split(seq, counts):
        out, at = [], 0
        for cnt in counts:
            out.append(seq[at:at + cnt])
            at += cnt
        return out

    n_i, n_o, n_s = ([len(getattr(p, f)) for p in plans] for f in ("ins", "outs", "sems"))

    def start(ins, outs, sems):
        for p, i, o, s in zip(plans, split(ins, n_i), split(outs, n_o), split(sems, n_s)):
            p.start(i, o, s)

    def finish(ins, outs, sems):
        for p, i, o, s in zip(plans, split(ins, n_i), split(outs, n_o), split(sems, n_s)):
            p.finish(i, o, s)

    return _Plan(sum((p.ins for p in plans), []), sum((p.outs for p in plans), []), sum((p.sems for p in plans), []),
                 start, finish)


def _add_pair(parts, sib, core, name):
    P4, _, Rh, C = parts.shape
    tm = _row_tile(Rh, C * 4, 1 << 20, 16)

    def body(c_ref, a_ref, b_ref, o_ref):
        o_ref[...] = (a_ref[0].astype(F32) + b_ref[...].astype(F32)).astype(BF16)

    spec = pl.BlockSpec((1, tm, C), lambda p, i, c_ref: (p, i, 0))
    return pl.pallas_call(
        body, name=name, out_shape=jax.ShapeDtypeStruct((P4, Rh, C), BF16),
        grid_spec=pltpu.PrefetchScalarGridSpec(
            num_scalar_prefetch=1, grid=(P4, Rh // tm),
            in_specs=[pl.BlockSpec((1, 1, tm, C), lambda p, i, c_ref: (p, c_ref[0], i, 0)), spec], out_specs=spec),
        compiler_params=_params(("parallel", "parallel")),
    )(core, parts, sib)


def _sum_slabs(pre, recv, chip, name):
    _, Rh, C = pre.shape
    tm = _row_tile(Rh, C * 4, 1 << 20, 16)

    def body(me_ref, own_ref, r_ref, o_ref):
        acc = own_ref[0].astype(F32)
        for j in range(3):
            acc = acc + r_ref[j].astype(F32)
        o_ref[...] = acc

    return pl.pallas_call(
        body, name=name, out_shape=jax.ShapeDtypeStruct((Rh, C), F32),
        grid_spec=pltpu.PrefetchScalarGridSpec(
            num_scalar_prefetch=1, grid=(Rh // tm,),
            in_specs=[pl.BlockSpec((1, tm, C), lambda i, me_ref: (me_ref[0], i, 0)),
                      pl.BlockSpec((3, tm, C), lambda i, me_ref: (0, i, 0))],
            out_specs=pl.BlockSpec((tm, C), lambda i, me_ref: (i, 0))),
        compiler_params=_params(("parallel",)),
    )(chip, pre, recv)


def kernel(x, c, positions, w_ada, b_ada, w_in, g_q_a, w_q_b, g_kv_a, w_kv_b, w_o_a, w_conv, w_o_b, w_o, ln1_g, ln1_b, w_ffn_in, w_ffn_out, ln2_g, ln2_b, loss_target, m_w_ada, m_b_ada, m_w_in, m_g_q_a, m_w_q_b, m_g_kv_a, m_w_kv_b, m_w_o_a, m_w_conv, m_w_o_b, m_w_o, m_ln1_g, m_ln1_b, m_w_ffn_in, m_w_ffn_out, m_ln2_g, m_ln2_b, v_w_ada, v_b_ada, v_w_in, v_g_q_a, v_w_q_b, v_g_kv_a, v_w_kv_b, v_w_o_a, v_w_conv, v_w_o_b, v_w_o, v_ln1_g, v_ln1_b, v_w_ffn_in, v_w_ffn_out, v_ln2_g, v_ln2_b):
    S, D = x.shape[1], x.shape[2]
    F = w_ffn_out.shape[1] * 4
    ax, ay, ac = _place()
    chip = 2 * ax + ay
    dev = 4 * ax + 2 * ay + ac
    x2, tgt = x[0], loss_target[0]
    w_ada2, w_in2, w_q_b2, w_kv_b2 = w_ada[0], w_in[0], w_q_b[0], w_kv_b[0]
    w_o_a2, w_o_b2, w_o2, w_ffn_in2, w_ffn_out2 = w_o_a[0], w_o_b[0], w_o[0], w_ffn_in[0], w_ffn_out[0]
    NA = w_ada2.shape[1]
    CW = w_conv.shape[2]

    inv_freq = 1.0 / (ROPE_THETA ** (jnp.arange(0, QK_ROPE, 2, dtype=F32) / QK_ROPE))
    ang = positions[0].astype(F32)[:, None] * inv_freq
    cos, sin = jnp.cos(ang), jnp.sin(ang)
    z32, z64, z96 = jnp.zeros((S, 32), F32), jnp.zeros((S, 64), F32), jnp.zeros((S, 96), F32)
    tab = jnp.concatenate([cos, cos, z64, -sin, z96, z32, sin, z64], axis=1)

    def halves(a):
        return a.reshape(2, a.shape[0] // 2, a.shape[1])

    def whole(g):
        return g.reshape(4, 2 * g.shape[2], g.shape[3])

    def cols(g):
        return jnp.transpose(g, (1, 0, 2)).reshape(g.shape[1], 4 * g.shape[2])

    sh_in, sh_qb, sh_kvb, sh_oa, sh_ob, sh_o, sh_fi, sh_fo = (
        halves(w.astype(BF16)) for w in (w_in2, w_q_b2, w_kv_b2, w_o_a2, w_o_b2, w_o2, w_ffn_in2, w_ffn_out2))
    g_in, g_qb, g_kvb = (whole(g) for g in _run_plan(_gather_plan([sh_in, sh_qb, sh_kvb]), "gather_first"))
    W_in = cols(g_in)
    n_qkv = Q_LORA + KV_LORA + QK_ROPE
    W_qkv = jnp.pad(W_in[:, :n_qkv], ((0, 0), (0, QKV_A - n_qkv)))
    W_conv = W_in[:, n_qkv:n_qkv + 3 * D]
    W_gate = W_in[:, n_qkv + 3 * D:]
    W_qb = jnp.pad(cols(g_qb).reshape(Q_LORA, N_HEADS, QK_NOPE + QK_ROPE),
                   ((0, 0), (0, 0), (0, QK_PAD - QK_NOPE - QK_ROPE))).reshape(Q_LORA, N_HEADS * QK_PAD)
    W_kvb = cols(g_kvb)

    c_all = _all_gather8(c, "gather_c").reshape(8, D)
    wconv_all = _all_gather8(w_conv[0], "gather_wconv")
    w_conv_full = jnp.transpose(wconv_all[0::2], (1, 0, 2)).reshape(3, D)
    b_sh = lax.dynamic_slice(b_ada, (0, chip * NA), (1, NA))
    mod_sh = _ada_fwd(c_all, w_ada2, b_sh)
    mod_all = _all_gather8(mod_sh, "gather_mod")
    mod = lax.dynamic_slice(mod_all[0::2], (0, dev, 0), (4, 1, NA)).reshape(6, D)
    shift1, scale1, gate1, shift2, scale2, gate2 = (mod[k:k + 1] for k in range(6))

    u = _modulate(x2, scale1, shift1, "modulate1")
    pq = _matmul(u, W_qkv, "nn", F32, "proj_qkv")
    pc, (g_oa, g_ob) = _matmul(u, W_conv, "nn", F32, "proj_conv", carry=_gather_plan([sh_oa, sh_ob]))
    pg, (g_o,) = _matmul(u, W_gate, "nn", F32, "proj_gate", carry=_gather_plan([sh_o]))
    W_oa, W_ob, W_o = (g.reshape(-1, D) for g in (g_oa, g_ob, g_o))
    rq, rkv, kr = _rms_fwd(pq, tab, g_q_a, g_kv_a)
    q = _q_rope(_matmul(rq, W_qb, "nn", F32, "q_b"), tab)
    kv = _matmul(rkv, W_kvb, "nn", BF16, "kv_b")
    o, lse, (g_fi,) = _attn_fwd(q, kv, kr, carry=_gather_plan([sh_fi]))
    W_fi = cols(whole(g_fi))
    y_a = _matmul(o, W_oa, "nn", F32, "o_a")
    hb = _conv_fwd(pc, w_conv_full)
    y_b = _matmul(hb, W_ob, "nn", F32, "o_b")
    merged = _merge_fwd(y_a, y_b, pg)
    mix = _matmul(merged, W_o, "nn", F32, "w_o")
    x1, u2 = _ln1_fwd(x2, mix, gate1, ln1_g, ln1_b, scale2, shift2)
    hh, (g_fo,) = _matmul(u2, W_fi, "nn", F32, "ffn_in", carry=_gather_plan([sh_fo]))
    W_fo = g_fo.reshape(F, D)
    act = _swiglu_fwd(hh)
    ffn = _matmul(act, W_fo, "nn", F32, "ffn_out")

    core_i = ac.astype(jnp.int32).reshape(1)
    chip_i = chip.astype(jnp.int32).reshape(1)

    def uncols(g):
        return jnp.transpose(g.reshape(g.shape[0], 4, g.shape[1] // 4), (1, 0, 2))

    def slabs(p):
        return p.reshape(4, 2, p.shape[1] // 2, p.shape[2])

    def add_pairs(parts, sibs, nms):
        return [_add_pair(a, b, core_i, "add_pair_" + nm) for a, b, nm in zip(parts, sibs, nms)]

    def sum_all(pre, recv, nms):
        return [_sum_slabs(a, r, chip_i, "sum_slabs_" + nm) for a, r, nm in zip(pre, recv, nms)]

    dffn, dx1a, loss_acc, d_ln2_g, d_ln2_b, d_gate2 = _ln2_loss_bwd(x1, ffn, gate2, ln2_g, ln2_b, tgt)
    loss = lax.psum(loss_acc[0, 0], ("x", "y", "c"))
    dW_fo = _matmul(act, dffn, "tn", BF16, "d_w_ffn_out")
    p_fo = [slabs(dW_fo.reshape(4, -1, D))]
    dact, s_fo = _matmul(dffn, W_fo, "nt", F32, "d_act", carry=_pair_plan(p_fo))
    pre_fo = add_pairs(p_fo, s_fo, ["w_ffn_out"])
    dhh = _swiglu_bwd(dact, hh)
    dW_fi, r_fo = _matmul(u2, dhh, "tn", BF16, "d_w_ffn_in", carry=_scatter_plan(pre_fo))
    p_fi = [slabs(uncols(dW_fi))]
    du2, s_fi = _matmul(dhh, W_fi, "nt", F32, "d_u2", carry=_pair_plan(p_fi))
    pre_fi = add_pairs(p_fi, s_fi, ["w_ffn_in"])
    dmix, dxa, d_shift2, d_scale2, d_ln1_g, d_ln1_b, d_gate1 = _ln1_bwd(x2, mix, dx1a, du2, gate1, ln1_g, ln1_b, scale2)
    dW_o = _matmul(merged, dmix, "tn", BF16, "d_w_o")
    dmerged = _matmul(dmix, W_o, "nt", F32, "d_merged")
    dy_a, dy_b, dgate = _merge_bwd(dmerged, y_a, y_b, pg)
    dW_oa = _matmul(o, dy_a, "tn", BF16, "d_w_o_a")
    do = _matmul(dy_a, W_oa, "nt", BF16, "d_o")
    dW_ob = _matmul(hb, dy_b, "tn", BF16, "d_w_o_b")
    p_mid = [slabs(g.reshape(4, -1, D)) for g in (dW_oa, dW_ob, dW_o)]
    dhb, s_mid = _matmul(dy_b, W_ob, "nt", F32, "d_hb", carry=_pair_plan(p_mid))
    pre_mid = add_pairs(p_mid, s_mid, ["w_o_a", "w_o_b", "w_o"])
    dconv, d_wconv = _conv_bwd(dhb, pc, w_conv_full)
    dq, dkv, dkr, r_mid = _attn_bwd(q, kv, kr, do, o, lse, tab, carry=_scatter_plan(pre_fi + pre_mid))
    names_a = ["w_ffn_out", "w_ffn_in", "w_o_a", "w_o_b", "w_o"]
    fin_a = sum_all(pre_fo + pre_fi + pre_mid, list(r_fo) + list(r_mid), names_a)
    dW_qb = _matmul(rq, dq, "tn", BF16, "d_w_q_b")
    d_rq = _matmul(dq, W_qb, "nt", F32, "d_rq")
    dW_kvb = _matmul(rkv, dkv, "tn", BF16, "d_w_kv_b")
    d_rkv = _matmul(dkv, W_kvb, "nt", F32, "d_rkv")
    dqkv, d_g_q, d_g_kv = _rms_bwd(d_rq, d_rkv, pq, dkr, g_q_a, g_kv_a)
    dW_qkv = _matmul(u, dqkv, "tn", BF16, "d_w_qkv")
    dW_conv, fs_a = _matmul(u, dconv, "tn", BF16, "d_w_conv", carry=_sibling_plan(fin_a))
    dW_gate = _matmul(u, dgate, "tn", BF16, "d_w_gate")
    dW_in = jnp.concatenate([dW_qkv[:, :n_qkv], dW_conv, dW_gate], axis=1)
    dW_qb_u = dW_qb.reshape(Q_LORA, N_HEADS, QK_PAD)[:, :, :QK_NOPE + QK_ROPE].reshape(Q_LORA, -1)
    names_b = ["w_in", "w_q_b", "w_kv_b"]
    p_b = [slabs(uncols(g)) for g in (dW_in, dW_qb_u, dW_kvb)]
    du, s_b = _matmul(dqkv, W_qkv, "nt", F32, "d_u_qkv", carry=_pair_plan(p_b))
    pre_b = add_pairs(p_b, s_b, names_b)
    du = _matmul(dconv, W_conv, "nt", F32, "d_u_conv", add=du)
    du = _matmul(dgate, W_gate, "nt", F32, "d_u_gate", add=du)
    grad_x, d_shift1, d_scale1 = _dx_final(dxa, du, x2, scale1)
    r_b = _run_plan(_scatter_plan(pre_b), "scatter_last")
    fin_b = sum_all(pre_b, r_b, names_b)
    fs_b = _run_plan(_sibling_plan(fin_b), "sibling_last")
    names = names_a + names_b
    fin, fin_sib = fin_a + fin_b, list(fs_a) + list(fs_b)

    def pad_d(v):
        return jnp.pad(v, ((0, 0), (0, D - v.shape[1])))

    small = jnp.concatenate([d_ln1_g, d_ln1_b, d_ln2_g, d_ln2_b, pad_d(d_g_q), pad_d(d_g_kv), d_wconv,
                             d_shift1, d_scale1, d_gate1, d_shift2, d_scale2, d_gate2, jnp.zeros((1, D), F32)], axis=0)
    small_all = _all_gather8(small, "gather_small")
    small_sum = _sum8(small_all)
    g_ln1_g, g_ln1_b, g_ln2_g, g_ln2_b = (small_sum[k:k + 1] for k in range(4))
    g_g_q, g_g_kv = small_sum[4:5, :Q_LORA], small_sum[5:6, :KV_LORA]
    g_wconv = lax.dynamic_slice(small_sum[6:9], (0, chip * CW), (3, CW))
    g_b_ada = small_sum[9:15].reshape(1, 6 * D)
    dmod_all = small_all[:, 9:15, :].reshape(8, 6 * D)
    g_w_ada = _ada_bwd(c_all, lax.dynamic_slice(dmod_all, (0, chip * NA), (8, NA)))

    big = {}
    ws = dict(w_in=(w_in2, m_w_in[0], v_w_in[0]), w_q_b=(w_q_b2, m_w_q_b[0], v_w_q_b[0]),
              w_kv_b=(w_kv_b2, m_w_kv_b[0], v_w_kv_b[0]), w_o_a=(w_o_a2, m_w_o_a[0], v_w_o_a[0]),
              w_o_b=(w_o_b2, m_w_o_b[0], v_w_o_b[0]), w_o=(w_o2, m_w_o[0], v_w_o[0]),
              w_ffn_in=(w_ffn_in2, m_w_ffn_in[0], v_w_ffn_in[0]), w_ffn_out=(w_ffn_out2, m_w_ffn_out[0], v_w_ffn_out[0]))
    for nm, a, b in zip(names, fin, fin_sib):
        w_, m_, v_ = ws[nm]
        big[nm] = _adam_halves("adam_" + nm, w_, m_, v_, a, b, core_i)
    big["w_ada"] = _adam("adam_w_ada", w_ada2, m_w_ada[0], v_w_ada[0], [g_w_ada])
    sm = {}
    for nm, w_, m_, v_, g_ in [("b_ada", b_ada, m_b_ada, v_b_ada, g_b_ada), ("g_q_a", g_q_a, m_g_q_a, v_g_q_a, g_g_q),
                               ("g_kv_a", g_kv_a, m_g_kv_a, v_g_kv_a, g_g_kv),
                               ("w_conv", w_conv[0], m_w_conv[0], v_w_conv[0], g_wconv),
                               ("ln1_g", ln1_g, m_ln1_g, v_ln1_g, g_ln1_g), ("ln1_b", ln1_b, m_ln1_b, v_ln1_b, g_ln1_b),
                               ("ln2_g", ln2_g, m_ln2_g, v_ln2_g, g_ln2_g), ("ln2_b", ln2_b, m_ln2_b, v_ln2_b, g_ln2_b)]:
        sm[nm] = (g_,) + tuple(_adam_small("adam_" + nm, w_, m_, v_, g_))

    order = ["w_ada", "b_ada", "w_in", "g_q_a", "w_q_b", "g_kv_a", "w_kv_b", "w_o_a", "w_conv", "w_o_b", "w_o",
             "ln1_g", "ln1_b", "w_ffn_in", "w_ffn_out", "ln2_g", "ln2_b"]
    lead = {"b_ada", "g_q_a", "g_kv_a", "ln1_g", "ln1_b", "ln2_g", "ln2_b"}

    def leaf(nm, k):
        val = big[nm][k] if nm in big else sm[nm][k]
        return val if nm in lead else val[None]

    outs = [loss, grad_x[None]]
    for k in range(4):
        outs += [leaf(nm, k) for nm in order]
    return tuple(outs)
```

```python
import functools

import jax
import jax.numpy as jnp
from jax import lax
from jax.experimental import pallas as pl
from jax.experimental.pallas import tpu as pltpu

F32, BF16 = jnp.float32, jnp.bfloat16
N_HEADS, QK_NOPE, QK_ROPE, V_HEAD = 16, 128, 64, 128
Q_LORA, KV_LORA = 512, 512
QK_PAD = 256
QKV_A = 1152
CHUNK_SHIFT = 6
ATTN_SCALE = (QK_NOPE + QK_ROPE) ** -0.5
ROPE_THETA = 10000.0
ALPHA = 2.0 ** 0.25
LN_EPS, RMS_EPS = 1e-5, 1e-6
ADAM_LR, ADAM_B1, ADAM_B2, ADAM_EPS, ADAM_WD, ADAM_STEP = 0.001, 0.9, 0.999, 1e-08, 0.01, 10
ADAM_C1 = 1.0 - ADAM_B1 ** ADAM_STEP
ADAM_C2 = 1.0 - ADAM_B2 ** ADAM_STEP
VMEM_LIMIT = 56 * 1024 * 1024
MESH = pl.DeviceIdType.MESH
ANY = pl.BlockSpec(memory_space=pl.ANY)
NT = (((1,), (1,)), ((), ()))
TN = (((0,), (0,)), ((), ()))
NN = (((1,), (0,)), ((), ()))


def _params(sem=None):
    return pltpu.CompilerParams(dimension_semantics=sem, vmem_limit_bytes=VMEM_LIMIT)


def _pick(n, cands=(1024, 512, 384, 256, 128)):
    for t in cands:
        if n % t == 0:
            return t
    return n


def _row_tile(rows, row_bytes, budget, mult=8):
    best = mult
    for t in range(mult, rows + 1, mult):
        if rows % t == 0 and t * row_bytes <= budget:
            best = t
    return best


def _sigmoid(x):
    return jax.nn.sigmoid(x)


class _Plan:
    def __init__(self, ins, outs, sems, start, finish, aliases=None):
        self.ins, self.outs, self.sems, self.start, self.finish = list(ins), list(outs), list(sems), start, finish
        self.aliases = dict(aliases or {})

    def io_aliases(self, first_in, first_out):
        return {first_in + i: first_out + o for i, o in self.aliases.items()}


def _run_plan(plan, name):
    n_in, n_out = len(plan.ins), len(plan.outs)

    def body(*refs):
        ins, outs, sems = refs[:n_in], refs[n_in:n_in + n_out], refs[n_in + n_out:]
        plan.start(ins, outs, sems)
        plan.finish(ins, outs, sems)

    return pl.pallas_call(body, name=name, out_shape=plan.outs, in_specs=[ANY] * n_in, out_specs=[ANY] * n_out,
                          scratch_shapes=plan.sems, input_output_aliases=plan.io_aliases(0, 0),
                          compiler_params=_params())(*plan.ins)


def _matmul(a, b, mode, out_dtype, name, add=None, carry=None):
    if mode == "nn":
        (M, K), N, dims = a.shape, b.shape[1], NN
    elif mode == "nt":
        (M, K), N, dims = a.shape, b.shape[0], NT
    else:
        (K, M), N, dims = a.shape, b.shape[1], TN
    tm, tn, tk = _pick(M), _pick(N), _pick(K)
    nk = K // tk
    a_spec = (pl.BlockSpec((tk, tm), lambda i, j, k: (k, i)) if mode == "tn"
              else pl.BlockSpec((tm, tk), lambda i, j, k: (i, k)))
    b_spec = (pl.BlockSpec((tn, tk), lambda i, j, k: (j, k)) if mode == "nt"
              else pl.BlockSpec((tk, tn), lambda i, j, k: (k, j)))
    o_spec = pl.BlockSpec((tm, tn), lambda i, j, k: (i, j))
    has_add = add is not None
    n_ci = len(carry.ins) if carry else 0
    n_co = len(carry.outs) if carry else 0
    n_in = 2 + has_add
    grid = (M // tm, N // tn, nk)

    def body(*refs):
        a_ref, b_ref = refs[0], refs[1]
        add_ref = refs[2] if has_add else None
        o_ref = refs[n_in + n_ci]
        acc_ref = refs[n_in + n_ci + 1 + n_co]
        c_ins = refs[n_in:n_in + n_ci]
        c_outs = refs[n_in + n_ci + 1:n_in + n_ci + 1 + n_co]
        c_sems = refs[n_in + n_ci + 2 + n_co:]
        i, j, k = pl.program_id(0), pl.program_id(1), pl.program_id(2)

        if carry:
            @pl.when((i == 0) & (j == 0) & (k == 0))
            def _():
                carry.start(c_ins, c_outs, c_sems)

        @pl.when(k == 0)
        def _():
            acc_ref[...] = jnp.zeros_like(acc_ref)

        acc_ref[...] += lax.dot_general(a_ref[...], b_ref[...], dims, preferred_element_type=F32)

        @pl.when(k == nk - 1)
        def _():
            r = acc_ref[...]
            if has_add:
                r = r + add_ref[...]
            o_ref[...] = r.astype(o_ref.dtype)

        if carry:
            @pl.when((i == grid[0] - 1) & (j == grid[1] - 1) & (k == nk - 1))
            def _():
                carry.finish(c_ins, c_outs, c_sems)

    ins = [a, b] + ([add] if has_add else []) + (carry.ins if carry else [])
    in_specs = [a_spec, b_spec] + ([o_spec] if has_add else []) + [ANY] * n_ci
    res = pl.pallas_call(
        body, name=name, grid=grid,
        in_specs=in_specs, out_specs=[o_spec] + [ANY] * n_co,
        out_shape=[jax.ShapeDtypeStruct((M, N), out_dtype)] + (carry.outs if carry else []),
        scratch_shapes=[pltpu.VMEM((tm, tn), F32)] + (carry.sems if carry else []),
        input_output_aliases=carry.io_aliases(n_in, 1) if carry else {},
        compiler_params=_params(("arbitrary",) * 3 if carry else ("parallel", "parallel", "arbitrary")),
    )(*ins)
    return (res[0], res[1:]) if carry else res[0]


def _rows(body, name, n_rows, tm, ins, outs, accs=()):
    grid = (n_rows // tm,)
    per8 = tm // 8
    last8 = n_rows // 8 - 1
    arrays, in_specs = [], []
    for spec in ins:
        kind, arr = spec[0], spec[1]
        arrays.append(arr)
        if kind == "row":
            _, _, cb, w = spec
            in_specs.append(pl.BlockSpec((tm, w), lambda i, cb=cb: (i, cb)))
        elif kind == "full":
            in_specs.append(pl.BlockSpec(arr.shape, lambda i, nd=arr.ndim: (0,) * nd))
        elif kind == "prev":
            _, _, cb, w = spec
            in_specs.append(pl.BlockSpec((8, w), lambda i, cb=cb: (jnp.maximum(i * per8 - 1, 0), cb)))
        else:
            _, _, cb, w = spec
            in_specs.append(pl.BlockSpec((8, w), lambda i, cb=cb: (jnp.minimum((i + 1) * per8, last8), cb)))
    out_shape = [jax.ShapeDtypeStruct((n_rows, w), dt) for (w, dt) in outs]
    out_specs = [pl.BlockSpec((tm, w), lambda i: (i, 0)) for (w, _) in outs]
    out_shape += [jax.ShapeDtypeStruct(s, F32) for s in accs]
    out_specs += [pl.BlockSpec(s, lambda i, nd=len(s): (0,) * nd) for s in accs]
    n_in, n_out = len(ins), len(outs)

    def kernel_body(*refs):
        body(pl.program_id(0), refs[:n_in], refs[n_in:n_in + n_out], refs[n_in + n_out:])

    res = pl.pallas_call(
        kernel_body, name=name, grid=grid, in_specs=in_specs, out_specs=out_specs, out_shape=out_shape,
        compiler_params=_params(("arbitrary",)),
    )(*arrays)
    return res


def _acc_add(i, ref, val):
    @pl.when(i == 0)
    def _():
        ref[...] = val

    @pl.when(i > 0)
    def _():
        ref[...] += val


def _rope(t, tab, sign):
    c, sa, sb = tab[:, 0:128], tab[:, 128:256], tab[:, 256:384]
    rot = pltpu.roll(t, 96, 1) * sa + pltpu.roll(t, 32, 1) * sb
    return t * c + rot if sign > 0 else t * c - rot


def _ln_stats(r):
    mu = jnp.mean(r, axis=-1, keepdims=True)
    d = r - mu
    var = jnp.mean(d * d, axis=-1, keepdims=True)
    rstd = lax.rsqrt(var + LN_EPS)
    return d * rstd, rstd


def _ln_bwd(dxh, xh, rstd):
    m1 = jnp.mean(dxh, axis=-1, keepdims=True)
    m2 = jnp.mean(dxh * xh, axis=-1, keepdims=True)
    return rstd * (dxh - m1 - xh * m2)


def _modulate(x, scale, shift, name):
    S, D = x.shape

    def body(i, ins, outs, accs):
        outs[0][...] = (ins[0][...] * (1.0 + ins[1][...]) + ins[2][...]).astype(BF16)

    return _rows(body, name, S, _pick(S, (256, 128)), [("row", x, 0, D), ("full", scale), ("full", shift)], [(D, BF16)])[0]


def _rms_fwd(pq, tab, g_q, g_kv):
    S = pq.shape[0]

    def body(i, ins, outs, accs):
        pq_ref, tab_ref, gq_ref, gkv_ref = ins

        def rms(x, g):
            return x * lax.rsqrt(jnp.mean(x * x, axis=-1, keepdims=True) + RMS_EPS) * g

        outs[0][...] = rms(pq_ref[:, 0:Q_LORA], gq_ref[...]).astype(BF16)
        outs[1][...] = rms(pq_ref[:, Q_LORA:Q_LORA + KV_LORA], gkv_ref[...]).astype(BF16)
        outs[2][...] = _rope(pq_ref[:, Q_LORA + KV_LORA:QKV_A], tab_ref[...], 1).astype(BF16)

    return _rows(body, "rms_fwd", S, _pick(S, (256, 128)),
                 [("row", pq, 0, QKV_A), ("row", tab, 0, 384), ("full", g_q), ("full", g_kv)],
                 [(Q_LORA, BF16), (KV_LORA, BF16), (128, BF16)])


def _q_rope(q, tab):
    S, W = q.shape

    def body(i, ins, outs, accs):
        q_ref, tab_ref = ins
        t = tab_ref[...]
        for h in range(N_HEADS):
            lo = h * QK_PAD
            outs[0][:, lo:lo + 128] = q_ref[:, lo:lo + 128].astype(BF16)
            outs[0][:, lo + 128:lo + 256] = _rope(q_ref[:, lo + 128:lo + 256], t, 1).astype(BF16)

    return _rows(body, "q_rope", S, _pick(S, (256, 128)), [("row", q, 0, W), ("row", tab, 0, 384)], [(W, BF16)])[0]


def _allowed(q0, k0, bq):
    row = q0 + lax.broadcasted_iota(jnp.int32, (bq, bq), 0)
    col = k0 + lax.broadcasted_iota(jnp.int32, (bq, bq), 1)
    return (col >> CHUNK_SHIFT) <= (row >> CHUNK_SHIFT)


ATTN_BLOCK = 512


def _attn_fwd(q, kv, kr, carry=None):
    S = q.shape[0]
    bq = min(ATTN_BLOCK, S)
    nq = S // bq
    n_ci = len(carry.ins) if carry else 0
    n_co = len(carry.outs) if carry else 0

    def body(*refs):
        q_ref, kn_ref, v_ref, kr_ref = refs[:4]
        o_ref, lse_ref = refs[4 + n_ci:6 + n_ci]
        c_ins, c_outs = refs[4:4 + n_ci], refs[6 + n_ci:6 + n_ci + n_co]
        kcat = refs[6 + n_ci + n_co]
        c_sems = refs[7 + n_ci + n_co:]
        qi = pl.program_id(1)
        if carry:
            @pl.when((pl.program_id(0) == 0) & (qi == 0))
            def _():
                carry.start(c_ins, c_outs, c_sems)

        @pl.when(qi == 0)
        def _():
            kcat[:, 0:128] = kn_ref[...]
            kcat[:, 128:256] = kr_ref[...]

        qv = q_ref[...]

        def step(j, carry, masked):
            m, l, acc = carry
            off = pl.multiple_of(j * bq, bq)
            s = lax.dot_general(qv, kcat[pl.ds(off, bq), :], NT, preferred_element_type=F32) * ATTN_SCALE
            if masked:
                s = jnp.where(_allowed(qi * bq, off, bq), s, -1e30)
            m_new = jnp.maximum(m, jnp.max(s, axis=1, keepdims=True))
            a = jnp.exp(m - m_new)
            p = jnp.exp(s - m_new)
            l = a * l + jnp.sum(p, axis=1, keepdims=True)
            acc = a * acc + jnp.dot(p.astype(BF16), v_ref[pl.ds(off, bq), :], preferred_element_type=F32)
            return m_new, l, acc

        init = (jnp.full((bq, 1), -1e30, F32), jnp.zeros((bq, 1), F32), jnp.zeros((bq, V_HEAD), F32))
        below = lax.fori_loop(0, qi, lambda j, cr: step(j, cr, False), init)
        m, l, acc = step(qi, below, True)
        o_ref[...] = (acc / l).astype(BF16)
        lse_ref[0] = m + jnp.log(l)
        if carry:
            @pl.when((pl.program_id(0) == N_HEADS - 1) & (qi == nq - 1))
            def _():
                carry.finish(c_ins, c_outs, c_sems)

    res = pl.pallas_call(
        body, name="attn_fwd", grid=(N_HEADS, nq),
        in_specs=[pl.BlockSpec((bq, QK_PAD), lambda h, i: (i, h)),
                  pl.BlockSpec((S, 128), lambda h, i: (0, 2 * h)),
                  pl.BlockSpec((S, 128), lambda h, i: (0, 2 * h + 1)),
                  pl.BlockSpec((S, 128), lambda h, i: (0, 0))] + [ANY] * n_ci,
        out_specs=[pl.BlockSpec((bq, V_HEAD), lambda h, i: (i, h)),
                   pl.BlockSpec((1, bq, 1), lambda h, i: (h, i, 0))] + [ANY] * n_co,
        out_shape=[jax.ShapeDtypeStruct((S, N_HEADS * V_HEAD), BF16),
                   jax.ShapeDtypeStruct((N_HEADS, S, 1), F32)] + (carry.outs if carry else []),
        scratch_shapes=[pltpu.VMEM((S, QK_PAD), BF16)] + (carry.sems if carry else []),
        input_output_aliases=carry.io_aliases(4, 2) if carry else {},
        compiler_params=_params(("arbitrary", "arbitrary")),
    )(q, kv, kv, kr, *(carry.ins if carry else []))
    return res[0], res[1], res[2:]


def _attn_bwd(q, kv, kr, do, o, lse, tab, carry=None):
    S = q.shape[0]
    bq = min(ATTN_BLOCK, S)
    nq = S // bq

    n_ci = len(carry.ins) if carry else 0
    n_co = len(carry.outs) if carry else 0

    def body(*refs):
        q_ref, kn_ref, v_ref, kr_ref, do_ref, o_ref, lse_ref, tab_ref = refs[:8]
        dq_ref, dkv_ref, dkr_ref = refs[8 + n_ci:11 + n_ci]
        dq_acc, dk_acc, dv_acc, kcat, delta = refs[11 + n_ci + n_co:16 + n_ci + n_co]
        c_ins, c_outs, c_sems = refs[8:8 + n_ci], refs[11 + n_ci:11 + n_ci + n_co], refs[16 + n_ci + n_co:]
        h = pl.program_id(0)
        if carry:
            @pl.when(h == 0)
            def _():
                carry.start(c_ins, c_outs, c_sems)

        dq_acc[...] = jnp.zeros_like(dq_acc)
        dk_acc[...] = jnp.zeros_like(dk_acc)
        dv_acc[...] = jnp.zeros_like(dv_acc)
        kcat[:, 0:128] = kn_ref[...]
        kcat[:, 128:256] = kr_ref[...]
        for r in range(nq):
            rows = slice(r * bq, (r + 1) * bq)
            delta[rows, :] = jnp.sum(do_ref[rows, :].astype(F32) * o_ref[rows, :].astype(F32), axis=1, keepdims=True)

        def pair(i, j, masked):
            rows_i = pl.ds(pl.multiple_of(i * bq, bq), bq)
            rows_j = pl.ds(pl.multiple_of(j * bq, bq), bq)
            qv, dov, k = q_ref[rows_i, :], do_ref[rows_i, :], kcat[rows_j, :]
            s = lax.dot_general(qv, k, NT, preferred_element_type=F32) * ATTN_SCALE
            if masked:
                s = jnp.where(_allowed(i * bq, j * bq, bq), s, -1e30)
            p = jnp.exp(s - lse_ref[0, rows_i, :])
            dv_acc[rows_j, :] += lax.dot_general(p.astype(BF16), dov, TN, preferred_element_type=F32)
            dp = lax.dot_general(dov, v_ref[rows_j, :], NT, preferred_element_type=F32)
            ds = (p * (dp - delta[rows_i, :]) * ATTN_SCALE).astype(BF16)
            dk_acc[rows_j, :] += lax.dot_general(ds, qv, TN, preferred_element_type=F32)
            dq_acc[rows_i, :] += jnp.dot(ds, k, preferred_element_type=F32)

        def kv_step(j, _):
            pair(j, j, True)

            def q_step(i, _):
                pair(i, j, False)
                return 0

            lax.fori_loop(j + 1, nq, q_step, 0)
            return 0

        lax.fori_loop(0, nq, kv_step, 0)

        for r in range(nq):
            rows = slice(r * bq, (r + 1) * bq)
            dq_ref[rows, 0:128] = dq_acc[rows, 0:128].astype(BF16)
            dq_ref[rows, 128:256] = _rope(dq_acc[rows, 128:256], tab_ref[rows, :], -1).astype(BF16)
        dkv_ref[:, 0:128] = dk_acc[:, 0:128].astype(BF16)
        dkv_ref[:, 128:256] = dv_acc[...].astype(BF16)

        @pl.when(h == 0)
        def _():
            dkr_ref[...] = dk_acc[:, 128:256]

        @pl.when(h > 0)
        def _():
            dkr_ref[...] += dk_acc[:, 128:256]

        @pl.when(h == N_HEADS - 1)
        def _():
            for r in range(nq):
                rows = slice(r * bq, (r + 1) * bq)
                dkr_ref[rows, :] = _rope(dkr_ref[rows, :], tab_ref[rows, :], -1)
            if carry:
                carry.finish(c_ins, c_outs, c_sems)

    W = N_HEADS * QK_PAD
    res = pl.pallas_call(
        body, name="attn_bwd", grid=(N_HEADS,),
        in_specs=[pl.BlockSpec((S, QK_PAD), lambda h: (0, h)),
                  pl.BlockSpec((S, 128), lambda h: (0, 2 * h)),
                  pl.BlockSpec((S, 128), lambda h: (0, 2 * h + 1)),
                  pl.BlockSpec((S, 128), lambda h: (0, 0)),
                  pl.BlockSpec((S, V_HEAD), lambda h: (0, h)),
                  pl.BlockSpec((S, V_HEAD), lambda h: (0, h)),
                  pl.BlockSpec((1, S, 1), lambda h: (h, 0, 0)),
                  pl.BlockSpec((S, 384), lambda h: (0, 0))] + [ANY] * n_ci,
        out_specs=[pl.BlockSpec((S, QK_PAD), lambda h: (0, h)),
                   pl.BlockSpec((S, QK_PAD), lambda h: (0, h)),
                   pl.BlockSpec((S, 128), lambda h: (0, 0))] + [ANY] * n_co,
        out_shape=[jax.ShapeDtypeStruct((S, W), BF16), jax.ShapeDtypeStruct((S, W), BF16),
                   jax.ShapeDtypeStruct((S, 128), F32)] + (carry.outs if carry else []),
        scratch_shapes=[pltpu.VMEM((S, QK_PAD), F32), pltpu.VMEM((S, QK_PAD), F32), pltpu.VMEM((S, V_HEAD), F32),
                        pltpu.VMEM((S, QK_PAD), BF16), pltpu.VMEM((S, 1), F32)]
        + (carry.sems if carry else []),
        input_output_aliases=carry.io_aliases(8, 3) if carry else {},
        compiler_params=_params(("arbitrary",)),
    )(q, kv, kv, kr, do, o, lse, tab, *(carry.ins if carry else []))
    return res[0], res[1], res[2], res[3:]


def _shift_down(cur, prev8, i, n):
    tm = cur.shape[0]
    prev8 = jnp.where(i == 0, jnp.zeros_like(prev8), prev8)
    full = jnp.concatenate([prev8, cur], axis=0)
    return pltpu.roll(full, n, 0)[8:8 + tm, :]


def _shift_up(cur, next8, i, last, n):
    tm = cur.shape[0]
    next8 = jnp.where(i == last, jnp.zeros_like(next8), next8)
    full = jnp.concatenate([cur, next8], axis=0)
    return pltpu.roll(full, tm + 8 - n, 0)[0:tm, :]


def _conv_fwd(pc, w_conv):
    S, D = pc.shape[0], pc.shape[1] // 3
    tm = _pick(S, (256, 128))

    def body(i, ins, outs, accs):
        b_ref, c_ref, x_ref, cp_ref, xp_ref, w_ref = ins
        z = c_ref[...] * x_ref[...]
        zp = cp_ref[...] * xp_ref[...]
        cz = w_ref[0:1, :] * _shift_down(z, zp, i, 2) + w_ref[1:2, :] * _shift_down(z, zp, i, 1) + w_ref[2:3, :] * z
        outs[0][...] = (b_ref[...] * cz).astype(BF16)

    return _rows(body, "conv_fwd", S, tm,
                 [("row", pc, 0, D), ("row", pc, 1, D), ("row", pc, 2, D), ("prev", pc, 1, D), ("prev", pc, 2, D),
                  ("full", w_conv)], [(D, BF16)])[0]


def _conv_bwd(dhb, pc, w_conv):
    S, D = dhb.shape
    tm = _pick(S, (256, 128))
    last = S // tm - 1

    def body(i, ins, outs, accs):
        g_ref, b_ref, c_ref, x_ref, cp_ref, xp_ref, gn_ref, bn_ref, w_ref = ins
        w0, w1, w2 = w_ref[0:1, :], w_ref[1:2, :], w_ref[2:3, :]
        c, x, g = c_ref[...], x_ref[...], g_ref[...]
        z = c * x
        zp = cp_ref[...] * xp_ref[...]
        z1, z2 = _shift_down(z, zp, i, 1), _shift_down(z, zp, i, 2)
        cz = w0 * z2 + w1 * z1 + w2 * z
        dcz = g * b_ref[...]
        dczn = gn_ref[...] * bn_ref[...]
        dz = w2 * dcz + w1 * _shift_up(dcz, dczn, i, last, 1) + w0 * _shift_up(dcz, dczn, i, last, 2)
        outs[0][:, 0:D] = (g * cz).astype(BF16)
        outs[0][:, D:2 * D] = (dz * x).astype(BF16)
        outs[0][:, 2 * D:3 * D] = (dz * c).astype(BF16)
        dw = jnp.concatenate([jnp.sum(dcz * z2, axis=0, keepdims=True), jnp.sum(dcz * z1, axis=0, keepdims=True),
                              jnp.sum(dcz * z, axis=0, keepdims=True)], axis=0)
        _acc_add(i, accs[0], dw)

    return _rows(body, "conv_bwd", S, tm,
                 [("row", dhb, 0, D), ("row", pc, 0, D), ("row", pc, 1, D), ("row", pc, 2, D),
                  ("prev", pc, 1, D), ("prev", pc, 2, D), ("next", dhb, 0, D), ("next", pc, 0, D), ("full", w_conv)],
                 [(3 * D, BF16)], [(3, D)])


def _merge_fwd(y_a, y_b, pg):
    S, D = y_a.shape

    def body(i, ins, outs, accs):
        ya, yb, ga, gb = ins
        outs[0][...] = (_sigmoid(ga[...]) * ya[...] + _sigmoid(gb[...]) * yb[...]).astype(BF16)

    return _rows(body, "merge_fwd", S, _pick(S, (256, 128)),
                 [("row", y_a, 0, D), ("row", y_b, 0, D), ("row", pg, 0, D), ("row", pg, 1, D)], [(D, BF16)])[0]


def _merge_bwd(dm, y_a, y_b, pg):
    S, D = dm.shape

    def body(i, ins, outs, accs):
        d, ya, yb = ins[0][...], ins[1][...], ins[2][...]
        sa, sb = _sigmoid(ins[3][...]), _sigmoid(ins[4][...])
        outs[0][...] = (d * sa).astype(BF16)
        outs[1][...] = (d * sb).astype(BF16)
        outs[2][:, 0:D] = (d * ya * (sa * (1.0 - sa))).astype(BF16)
        outs[2][:, D:2 * D] = (d * yb * (sb * (1.0 - sb))).astype(BF16)

    return _rows(body, "merge_bwd", S, _pick(S, (256, 128)),
                 [("row", dm, 0, D), ("row", y_a, 0, D), ("row", y_b, 0, D), ("row", pg, 0, D), ("row", pg, 1, D)],
                 [(D, BF16), (D, BF16), (2 * D, BF16)])


def _ln1_fwd(x, mix, gate1, g, b, scale2, shift2):
    S, D = x.shape

    def body(i, ins, outs, accs):
        x_ref, mix_ref, gate_ref, g_ref, b_ref, sc_ref, sh_ref = ins
        xh, _ = _ln_stats(ALPHA * x_ref[...] + gate_ref[...] * mix_ref[...])
        x1 = xh * g_ref[...] + b_ref[...]
        outs[0][...] = x1
        outs[1][...] = (x1 * (1.0 + sc_ref[...]) + sh_ref[...]).astype(BF16)

    return _rows(body, "ln1_fwd", S, _pick(S, (256, 128)),
                 [("row", x, 0, D), ("row", mix, 0, D), ("full", gate1), ("full", g), ("full", b),
                  ("full", scale2), ("full", shift2)], [(D, F32), (D, BF16)])


def _swiglu_fwd(hh):
    S, F = hh.shape[0], hh.shape[1] // 2

    def body(i, ins, outs, accs):
        hg = ins[0][...]
        outs[0][...] = (hg * _sigmoid(hg) * ins[1][...]).astype(BF16)

    return _rows(body, "swiglu_fwd", S, _pick(S, (128,)), [("row", hh, 0, F), ("row", hh, 1, F)], [(F, BF16)])[0]


def _swiglu_bwd(dact, hh):
    S, F = dact.shape

    def body(i, ins, outs, accs):
        d, hg, hu = ins[0][...], ins[1][...], ins[2][...]
        sg = _sigmoid(hg)
        outs[0][:, 0:F] = (d * hu * (sg * (1.0 + hg * (1.0 - sg)))).astype(BF16)
        outs[0][:, F:2 * F] = (d * (hg * sg)).astype(BF16)

    return _rows(body, "swiglu_bwd", S, _pick(S, (128,)),
                 [("row", dact, 0, F), ("row", hh, 0, F), ("row", hh, 1, F)], [(2 * F, BF16)])[0]


def _ln2_loss_bwd(x1, ffn, gate2, g, b, target):
    S, D = x1.shape

    def body(i, ins, outs, accs):
        x1_ref, f_ref, gate_ref, g_ref, b_ref, t_ref = ins
        f = f_ref[...]
        xh, rstd = _ln_stats(ALPHA * x1_ref[...] + gate_ref[...] * f)
        e = xh * g_ref[...] + b_ref[...] - t_ref[...]
        dy = e * (1.0 / D)
        dr = _ln_bwd(dy * g_ref[...], xh, rstd)
        outs[0][...] = (gate_ref[...] * dr).astype(BF16)
        outs[1][...] = ALPHA * dr
        _acc_add(i, accs[0], jnp.full((1, 128), (0.5 / D) * jnp.sum(e * e), F32))
        _acc_add(i, accs[1], jnp.sum(dy * xh, axis=0, keepdims=True))
        _acc_add(i, accs[2], jnp.sum(dy, axis=0, keepdims=True))
        _acc_add(i, accs[3], jnp.sum(dr * f, axis=0, keepdims=True))

    return _rows(body, "ln2_loss_bwd", S, _pick(S, (256, 128)),
                 [("row", x1, 0, D), ("row", ffn, 0, D), ("full", gate2), ("full", g), ("full", b), ("row", target, 0, D)],
                 [(D, BF16), (D, F32)], [(1, 128), (1, D), (1, D), (1, D)])


def _ln1_bwd(x, mix, dx1a, du2, gate1, g, b, scale2):
    S, D = x.shape

    def body(i, ins, outs, accs):
        x_ref, mix_ref, da_ref, du_ref, gate_ref, g_ref, b_ref, sc_ref = ins
        mix, du = mix_ref[...], du_ref[...]
        xh, rstd = _ln_stats(ALPHA * x_ref[...] + gate_ref[...] * mix)
        x1 = xh * g_ref[...] + b_ref[...]
        dx1 = da_ref[...] + du * (1.0 + sc_ref[...])
        dr = _ln_bwd(dx1 * g_ref[...], xh, rstd)
        outs[0][...] = (gate_ref[...] * dr).astype(BF16)
        outs[1][...] = ALPHA * dr
        _acc_add(i, accs[0], jnp.sum(du, axis=0, keepdims=True))
        _acc_add(i, accs[1], jnp.sum(du * x1, axis=0, keepdims=True))
        _acc_add(i, accs[2], jnp.sum(dx1 * xh, axis=0, keepdims=True))
        _acc_add(i, accs[3], jnp.sum(dx1, axis=0, keepdims=True))
        _acc_add(i, accs[4], jnp.sum(dr * mix, axis=0, keepdims=True))

    return _rows(body, "ln1_bwd", S, _pick(S, (256, 128)),
                 [("row", x, 0, D), ("row", mix, 0, D), ("row", dx1a, 0, D), ("row", du2, 0, D),
                  ("full", gate1), ("full", g), ("full", b), ("full", scale2)],
                 [(D, BF16), (D, F32)], [(1, D)] * 5)


def _rms_bwd(d_rq, d_rkv, pq, dkr, g_q, g_kv):
    S = pq.shape[0]

    def body(i, ins, outs, accs):
        dq_ref, dkv_ref, pq_ref, dkr_ref, gq_ref, gkv_ref = ins

        def rms_bwd(dy, x, g):
            r = lax.rsqrt(jnp.mean(x * x, axis=-1, keepdims=True) + RMS_EPS)
            dyg = dy * g
            dx = r * dyg - x * (r * r * r) * jnp.mean(dyg * x, axis=-1, keepdims=True)
            return dx, jnp.sum(dy * (x * r), axis=0, keepdims=True)

        dxq, dgq = rms_bwd(dq_ref[...], pq_ref[:, 0:Q_LORA], gq_ref[...])
        dxkv, dgkv = rms_bwd(dkv_ref[...], pq_ref[:, Q_LORA:Q_LORA + KV_LORA], gkv_ref[...])
        outs[0][:, 0:Q_LORA] = dxq.astype(BF16)
        outs[0][:, Q_LORA:Q_LORA + KV_LORA] = dxkv.astype(BF16)
        outs[0][:, Q_LORA + KV_LORA:QKV_A] = dkr_ref[...].astype(BF16)
        _acc_add(i, accs[0], dgq)
        _acc_add(i, accs[1], dgkv)

    return _rows(body, "rms_bwd", S, _pick(S, (256, 128)),
                 [("row", d_rq, 0, Q_LORA), ("row", d_rkv, 0, KV_LORA), ("row", pq, 0, QKV_A), ("row", dkr, 0, 128),
                  ("full", g_q), ("full", g_kv)], [(QKV_A, BF16)], [(1, Q_LORA), (1, KV_LORA)])


def _dx_final(dxa, du, x, scale1):
    S, D = x.shape

    def body(i, ins, outs, accs):
        du = ins[1][...]
        outs[0][...] = ins[0][...] + du * (1.0 + ins[3][...])
        _acc_add(i, accs[0], jnp.sum(du, axis=0, keepdims=True))
        _acc_add(i, accs[1], jnp.sum(du * ins[2][...], axis=0, keepdims=True))

    return _rows(body, "dx_final", S, _pick(S, (256, 128)),
                 [("row", dxa, 0, D), ("row", du, 0, D), ("row", x, 0, D), ("full", scale1)],
                 [(D, F32)], [(1, D), (1, D)])


def _ada_fwd(c_all, w, bias):
    B, D = c_all.shape
    NA = w.shape[1]
    tn = _pick(NA, (512, 256, 128))

    def body(c_ref, w_ref, b_ref, o_ref):
        cv = c_ref[...]
        ca = (cv * _sigmoid(cv)).astype(BF16)
        o_ref[...] = jnp.dot(ca, w_ref[...].astype(BF16), preferred_element_type=F32) + b_ref[...]

    return pl.pallas_call(
        body, name="ada_fwd", grid=(NA // tn,),
        in_specs=[pl.BlockSpec((B, D), lambda j: (0, 0)), pl.BlockSpec((D, tn), lambda j: (0, j)),
                  pl.BlockSpec((1, tn), lambda j: (0, j))],
        out_specs=pl.BlockSpec((B, tn), lambda j: (0, j)),
        out_shape=jax.ShapeDtypeStruct((B, NA), F32),
        compiler_params=_params(("arbitrary",)),
    )(c_all, w, bias)


def _ada_bwd(c_all, dmod):
    B, D = c_all.shape
    NA = dmod.shape[1]
    tn = _pick(NA, (512, 256, 128))

    def body(c_ref, d_ref, o_ref):
        cv = c_ref[...]
        ca = (cv * _sigmoid(cv)).astype(BF16)
        o_ref[...] = lax.dot_general(ca, d_ref[...].astype(BF16), TN, preferred_element_type=F32)

    return pl.pallas_call(
        body, name="ada_bwd", grid=(NA // tn,),
        in_specs=[pl.BlockSpec((B, D), lambda j: (0, 0)), pl.BlockSpec((B, tn), lambda j: (0, j))],
        out_specs=pl.BlockSpec((D, tn), lambda j: (0, j)),
        out_shape=jax.ShapeDtypeStruct((D, NA), F32),
        compiler_params=_params(("arbitrary",)),
    )(c_all, dmod)


def _sum8(parts):
    _, R, N = parts.shape

    def body(p_ref, o_ref):
        acc = p_ref[0]
        for d in range(1, 8):
            acc = acc + p_ref[d]
        o_ref[...] = acc

    return pl.pallas_call(body, name="sum8", out_shape=jax.ShapeDtypeStruct((R, N), F32),
                          compiler_params=_params())(parts)


def _adam_math(w, g, m, v):
    m = ADAM_B1 * m + (1.0 - ADAM_B1) * g
    v = ADAM_B2 * v + (1.0 - ADAM_B2) * (g * g)
    delta = -ADAM_LR * ((m / ADAM_C1) / (jnp.sqrt(v / ADAM_C2) + ADAM_EPS) + ADAM_WD * w)
    return delta, m, v


def _adam(name, w, m, v, g, carry=None):
    R, C = w.shape
    tm = _row_tile(R, C * 4, 1 << 20)
    steps = R // tm
    n_ci = len(carry.ins) if carry else 0
    n_co = len(carry.outs) if carry else 0

    def body(*refs):
        w_ref, m_ref, v_ref, g_ref = refs[:4]
        d_ref, nm_ref, nv_ref = refs[4 + n_ci:7 + n_ci]
        c_ins, c_outs, c_sems = refs[4:4 + n_ci], refs[7 + n_ci:7 + n_ci + n_co], refs[7 + n_ci + n_co:]
        if carry:
            @pl.when(pl.program_id(0) == 0)
            def _():
                carry.start(c_ins, c_outs, c_sems)

        delta, nm, nv = _adam_math(w_ref[...], g_ref[...], m_ref[...], v_ref[...])
        d_ref[...] = delta
        nm_ref[...] = nm
        nv_ref[...] = nv
        if carry:
            @pl.when(pl.program_id(0) == steps - 1)
            def _():
                carry.finish(c_ins, c_outs, c_sems)

    spec = pl.BlockSpec((tm, C), lambda i: (i, 0))
    res = pl.pallas_call(
        body, name=name, grid=(steps,), in_specs=[spec] * 4 + [ANY] * n_ci, out_specs=[spec] * 3 + [ANY] * n_co,
        out_shape=[jax.ShapeDtypeStruct((R, C), F32)] * 3 + (carry.outs if carry else []),
        scratch_shapes=carry.sems if carry else [],
        input_output_aliases=carry.io_aliases(4, 3) if carry else {},
        compiler_params=_params(("arbitrary",)),
    )(w, m, v, g, *(carry.ins if carry else []))
    return (res[:3], res[3:]) if carry else res


def _adam_halves(name, w, m, v, mine, other, core, carry=None):
    R, C = w.shape
    Rh = R // 2
    tm = _row_tile(Rh, C * 4, 1 << 20)
    nh = Rh // tm
    steps = R // tm
    n_ci = len(carry.ins) if carry else 0
    n_co = len(carry.outs) if carry else 0

    def body(*refs):
        c_ref, w_ref, m_ref, v_ref, a_ref, b_ref = refs[:6]
        g_ref, d_ref, nm_ref, nv_ref = refs[6 + n_ci:10 + n_ci]
        c_ins, c_outs, c_sems = refs[6:6 + n_ci], refs[10 + n_ci:10 + n_ci + n_co], refs[10 + n_ci + n_co:]
        if carry:
            @pl.when(pl.program_id(0) == 0)
            def _():
                carry.start(c_ins, c_outs, c_sems)

        g = jnp.where(pl.program_id(0) // nh == c_ref[0], a_ref[...], b_ref[...])
        delta, nm, nv = _adam_math(w_ref[...], g, m_ref[...], v_ref[...])
        g_ref[...] = g
        d_ref[...] = delta
        nm_ref[...] = nm
        nv_ref[...] = nv
        if carry:
            @pl.when(pl.program_id(0) == steps - 1)
            def _():
                carry.finish(c_ins, c_outs, c_sems)

    spec = pl.BlockSpec((tm, C), lambda i, c_ref: (i, 0))
    a_spec = pl.BlockSpec((tm, C), lambda i, c_ref: (jnp.where(i // nh == c_ref[0], i % nh, 0), 0))
    b_spec = pl.BlockSpec((tm, C), lambda i, c_ref: (jnp.where(i // nh == c_ref[0], 0, i % nh), 0))
    res = pl.pallas_call(
        body, name=name, out_shape=[jax.ShapeDtypeStruct((R, C), F32)] * 4 + (carry.outs if carry else []),
        grid_spec=pltpu.PrefetchScalarGridSpec(
            num_scalar_prefetch=1, grid=(steps,), in_specs=[spec, spec, spec, a_spec, b_spec] + [ANY] * n_ci,
            out_specs=[spec] * 4 + [ANY] * n_co, scratch_shapes=carry.sems if carry else []),
        input_output_aliases=carry.io_aliases(6, 4) if carry else {},
        compiler_params=_params(("arbitrary",)),
    )(core, w, m, v, mine, other, *(carry.ins if carry else []))
    return (res[:4], res[4:]) if carry else res


def _adam_small(name, w, m, v, g):
    def body(w_ref, m_ref, v_ref, g_ref, d_ref, nm_ref, nv_ref):
        delta, nm, nv = _adam_math(w_ref[...], g_ref[...], m_ref[...], v_ref[...])
        d_ref[...] = delta
        nm_ref[...] = nm
        nv_ref[...] = nv

    return pl.pallas_call(body, name=name, out_shape=[jax.ShapeDtypeStruct(w.shape, F32)] * 3,
                          compiler_params=_params())(w, m, v, g)


def _place():
    return lax.axis_index("x"), lax.axis_index("y"), lax.axis_index("c")


def _other_chips(x, y):
    return [(1 - x, y), (x, 1 - y), (1 - x, 1 - y)]


def _all_gather8(blk, name):
    R, N = blk.shape

    def body(x_ref, out_ref, send_sems, recv_sems, local_sem):
        x, y, c = _place()
        me = 4 * x + 2 * y + c
        mine = pltpu.make_async_copy(x_ref, out_ref.at[me], local_sem)
        mine.start()
        flips = [(j >> 2 & 1, j >> 1 & 1, j & 1) for j in range(1, 8)]
        peers = [((1 - x) if fx else x, (1 - y) if fy else y, (1 - c) if fc else c) for fx, fy, fc in flips]
        sends = []
        for j, peer in enumerate(peers):
            cp = pltpu.make_async_remote_copy(src_ref=x_ref, dst_ref=out_ref.at[me], send_sem=send_sems.at[j],
                                              recv_sem=recv_sems.at[j], device_id=peer, device_id_type=MESH)
            cp.start()
            sends.append(cp)
        for j, (px, py, pc) in enumerate(peers):
            pltpu.make_async_remote_copy(src_ref=x_ref, dst_ref=out_ref.at[4 * px + 2 * py + pc],
                                         send_sem=send_sems.at[j], recv_sem=recv_sems.at[j],
                                         device_id=(px, py, pc), device_id_type=MESH).wait_recv()
        for cp in sends:
            cp.wait_send()
        mine.wait()

    return pl.pallas_call(
        body, name=name, out_shape=jax.ShapeDtypeStruct((8, R, N), F32),
        in_specs=[pl.BlockSpec(memory_space=pltpu.VMEM)], out_specs=pl.BlockSpec(memory_space=pltpu.VMEM),
        scratch_shapes=[pltpu.SemaphoreType.DMA((7,)), pltpu.SemaphoreType.DMA((7,)), pltpu.SemaphoreType.DMA],
        compiler_params=_params(),
    )(blk)


def _piece(rows, piece):
    i, n = piece
    size = rows // n
    assert size * n == rows and size % 16 == 0, (rows, piece)
    return pl.ds(i * size, size)


def _scatter_plan(arrs, piece=(0, 1), into=None):
    n = len(arrs)

    def copies(ins, outs, sems):
        send_sems, recv_sems = sems
        x, y, c = _place()
        chips = _other_chips(x, y)
        cps = []
        for k in range(n):
            rows = _piece(arrs[k].shape[1], piece)
            for j, (px, py) in enumerate(chips):
                cps.append(pltpu.make_async_remote_copy(
                    src_ref=ins[k].at[2 * px + py, rows], dst_ref=outs[k].at[j, rows],
                    send_sem=send_sems.at[3 * k + j], recv_sem=recv_sems.at[3 * k + j],
                    device_id=(px, py, c), device_id_type=MESH))
        return cps

    def start(ins, outs, sems):
        for cp in copies(ins, outs, sems):
            cp.start()

    def finish(ins, outs, sems):
        for cp in copies(ins, outs, sems):
            cp.wait()

    return _Plan(list(arrs) + list(into or []), [jax.ShapeDtypeStruct((3,) + a.shape[1:], a.dtype) for a in arrs],
                 [pltpu.SemaphoreType.DMA((3 * n,))] * 2, start, finish,
                 aliases={n + k: k for k in range(n)} if into else None)


def _gather_plan(shards, piece=(0, 1), into=None):
    n = len(shards)

    def parts(ins, outs, sems):
        s1, r1, s2, r2, loc = sems
        x, y, c = _place()
        me = 2 * x + y
        chips = _other_chips(x, y)
        sib = (x, y, 1 - c)

        def rows(k):
            return _piece(shards[k].shape[1], piece)

        def ici(k, j, slab, to):
            return pltpu.make_async_remote_copy(src_ref=ins[k].at[c, rows(k)], dst_ref=outs[k].at[slab, c, rows(k)],
                                                send_sem=s1.at[3 * k + j], recv_sem=r1.at[3 * k + j],
                                                device_id=to, device_id_type=MESH)

        def d2d(k, j, slab, half):
            return pltpu.make_async_remote_copy(src_ref=outs[k].at[slab, half, rows(k)],
                                                dst_ref=outs[k].at[slab, half, rows(k)],
                                                send_sem=s2.at[3 * k + j], recv_sem=r2.at[3 * k + j],
                                                device_id=sib, device_id_type=MESH)

        def own(k):
            return pltpu.make_async_remote_copy(src_ref=ins[k].at[:, rows(k)], dst_ref=outs[k].at[me, :, rows(k)],
                                                send_sem=loc.at[2 * k], recv_sem=loc.at[2 * k + 1],
                                                device_id=sib, device_id_type=MESH)

        return c, me, chips, ici, d2d, own

    def start(ins, outs, sems):
        c, me, chips, ici, d2d, own = parts(ins, outs, sems)
        for k in range(n):
            for j, (px, py) in enumerate(chips):
                ici(k, j, me, (px, py, c)).start()
        for k in range(n):
            own(k).start()

    def finish(ins, outs, sems):
        c, me, chips, ici, d2d, own = parts(ins, outs, sems)
        for k in range(n):
            for j, (px, py) in enumerate(chips):
                ici(k, j, 2 * px + py, (px, py, c)).wait_recv()
                d2d(k, j, 2 * px + py, c).start()
        for k in range(n):
            for j, (px, py) in enumerate(chips):
                d2d(k, j, 2 * px + py, 1 - c).wait_recv()
        for k in range(n):
            own(k).wait()
            for j, (px, py) in enumerate(chips):
                ici(k, j, me, (px, py, c)).wait_send()
                d2d(k, j, 2 * px + py, c).wait_send()

    return _Plan(list(shards) + list(into or []), [jax.ShapeDtypeStruct((4,) + a.shape, a.dtype) for a in shards],
                 [pltpu.SemaphoreType.DMA((3 * n,))] * 4 + [pltpu.SemaphoreType.DMA((2 * n,))], start, finish,
                 aliases={n + k: k for k in range(n)} if into else None)


def _pair_plan(parts):
    n = len(parts)

    def copies(ins, outs, sems):
        send_sems, recv_sems = sems
        x, y, c = _place()
        return [pltpu.make_async_remote_copy(src_ref=ins[k].at[p, 1 - c], dst_ref=outs[k].at[p],
                                             send_sem=send_sems.at[4 * k + p], recv_sem=recv_sems.at[4 * k + p],
                                             device_id=(x, y, 1 - c), device_id_type=MESH)
                for k in range(n) for p in range(4)]

    def start(ins, outs, sems):
        for cp in copies(ins, outs, sems):
            cp.start()

    def finish(ins, outs, sems):
        for cp in copies(ins, outs, sems):
            cp.wait()

    return _Plan(parts, [jax.ShapeDtypeStruct((4,) + a.shape[2:], a.dtype) for a in parts],
                 [pltpu.SemaphoreType.DMA((4 * n,))] * 2, start, finish)


def _sibling_plan(arrs):
    n = len(arrs)

    def copies(ins, outs, sems):
        send_sems, recv_sems = sems
        x, y, c = _place()
        return [pltpu.make_async_remote_copy(src_ref=ins[k], dst_ref=outs[k], send_sem=send_sems.at[k],
                                             recv_sem=recv_sems.at[k], device_id=(x, y, 1 - c), device_id_type=MESH)
                for k in range(n)]

    def start(ins, outs, sems):
        for cp in copies(ins, outs, sems):
            cp.start()

    def finish(ins, outs, sems):
        for cp in copies(ins, outs, sems):
            cp.wait()

    return _Plan(arrs, [jax.ShapeDtypeStruct(a.shape, a.dtype) for a in arrs],
                 [pltpu.SemaphoreType.DMA((n,))] * 2, start, finish)


def _join_plans(plans):
    def split(seq, counts):
        out, at = [], 0
        for cnt in counts:
            out.append(seq[at:at + cnt])
            at += cnt
        return out

    n_i, n_o, n_s = ([len(getattr(p, f)) for p in plans] for f in ("ins", "outs", "sems"))

    def start(ins, outs, sems):
        for p, i, o, s in zip(plans, split(ins, n_i), split(outs, n_o), split(sems, n_s)):
            p.start(i, o, s)

    def finish(ins, outs, sems):
        for p, i, o, s in zip(plans, split(ins, n_i), split(outs, n_o), split(sems, n_s)):
            p.finish(i, o, s)

    aliases, at_i, at_o = {}, 0, 0
    for p in plans:
        aliases.update(p.io_aliases(at_i, at_o))
        at_i, at_o = at_i + len(p.ins), at_o + len(p.outs)
    return _Plan(sum((p.ins for p in plans), []), sum((p.outs for p in plans), []), sum((p.sems for p in plans), []),
                 start, finish, aliases)


def _add_pair(parts, sib, core, name):
    P4, _, Rh, C = parts.shape
    tm = _row_tile(Rh, C * 4, 1 << 20, 16)

    def body(c_ref, a_ref, b_ref, o_ref):
        o_ref[...] = (a_ref[0].astype(F32) + b_ref[...].astype(F32)).astype(BF16)

    spec = pl.BlockSpec((1, tm, C), lambda p, i, c_ref: (p, i, 0))
    return pl.pallas_call(
        body, name=name, out_shape=jax.ShapeDtypeStruct((P4, Rh, C), BF16),
        grid_spec=pltpu.PrefetchScalarGridSpec(
            num_scalar_prefetch=1, grid=(P4, Rh // tm),
            in_specs=[pl.BlockSpec((1, 1, tm, C), lambda p, i, c_ref: (p, c_ref[0], i, 0)), spec], out_specs=spec),
        compiler_params=_params(("parallel", "parallel")),
    )(core, parts, sib)


def _sum_slabs(pre, recv, chip, name):
    _, Rh, C = pre.shape
    tm = _row_tile(Rh, C * 4, 1 << 20, 16)

    def body(me_ref, own_ref, r_ref, o_ref):
        acc = own_ref[0].astype(F32)
        for j in range(3):
            acc = acc + r_ref[j].astype(F32)
        o_ref[...] = acc

    return pl.pallas_call(
        body, name=name, out_shape=jax.ShapeDtypeStruct((Rh, C), F32),
        grid_spec=pltpu.PrefetchScalarGridSpec(
            num_scalar_prefetch=1, grid=(Rh // tm,),
            in_specs=[pl.BlockSpec((1, tm, C), lambda i, me_ref: (me_ref[0], i, 0)),
                      pl.BlockSpec((3, tm, C), lambda i, me_ref: (0, i, 0))],
            out_specs=pl.BlockSpec((tm, C), lambda i, me_ref: (i, 0))),
        compiler_params=_params(("parallel",)),
    )(chip, pre, recv)


def kernel(x, c, positions, w_ada, b_ada, w_in, g_q_a, w_q_b, g_kv_a, w_kv_b, w_o_a, w_conv, w_o_b, w_o, ln1_g, ln1_b, w_ffn_in, w_ffn_out, ln2_g, ln2_b, loss_target, m_w_ada, m_b_ada, m_w_in, m_g_q_a, m_w_q_b, m_g_kv_a, m_w_kv_b, m_w_o_a, m_w_conv, m_w_o_b, m_w_o, m_ln1_g, m_ln1_b, m_w_ffn_in, m_w_ffn_out, m_ln2_g, m_ln2_b, v_w_ada, v_b_ada, v_w_in, v_g_q_a, v_w_q_b, v_g_kv_a, v_w_kv_b, v_w_o_a, v_w_conv, v_w_o_b, v_w_o, v_ln1_g, v_ln1_b, v_w_ffn_in, v_w_ffn_out, v_ln2_g, v_ln2_b):
    S, D = x.shape[1], x.shape[2]
    F = w_ffn_out.shape[1] * 4
    ax, ay, ac = _place()
    chip = 2 * ax + ay
    dev = 4 * ax + 2 * ay + ac
    x2, tgt = x[0], loss_target[0]
    w_ada2, w_in2, w_q_b2, w_kv_b2 = w_ada[0], w_in[0], w_q_b[0], w_kv_b[0]
    w_o_a2, w_o_b2, w_o2, w_ffn_in2, w_ffn_out2 = w_o_a[0], w_o_b[0], w_o[0], w_ffn_in[0], w_ffn_out[0]
    NA = w_ada2.shape[1]
    CW = w_conv.shape[2]

    inv_freq = 1.0 / (ROPE_THETA ** (jnp.arange(0, QK_ROPE, 2, dtype=F32) / QK_ROPE))
    ang = positions[0].astype(F32)[:, None] * inv_freq
    cos, sin = jnp.cos(ang), jnp.sin(ang)
    z32, z64, z96 = jnp.zeros((S, 32), F32), jnp.zeros((S, 64), F32), jnp.zeros((S, 96), F32)
    tab = jnp.concatenate([cos, cos, z64, -sin, z96, z32, sin, z64], axis=1)

    def halves(a):
        return a.reshape(2, a.shape[0] // 2, a.shape[1])

    def whole(g):
        return g.reshape(4, 2 * g.shape[2], g.shape[3])

    def cols(g):
        return jnp.transpose(g, (1, 0, 2)).reshape(g.shape[1], 4 * g.shape[2])

    sh_in, sh_qb, sh_kvb, sh_oa, sh_ob, sh_o, sh_fi, sh_fo = (
        halves(w.astype(BF16)) for w in (w_in2, w_q_b2, w_kv_b2, w_o_a2, w_o_b2, w_o2, w_ffn_in2, w_ffn_out2))
    g_in = whole(_run_plan(_gather_plan([sh_in]), "gather_first")[0])
    W_in = cols(g_in)
    n_qkv = Q_LORA + KV_LORA + QK_ROPE
    W_qkv = jnp.pad(W_in[:, :n_qkv], ((0, 0), (0, QKV_A - n_qkv)))
    W_conv = W_in[:, n_qkv:n_qkv + 3 * D]
    W_gate = W_in[:, n_qkv + 3 * D:]

    c_all = _all_gather8(c, "gather_c").reshape(8, D)
    wconv_all = _all_gather8(w_conv[0], "gather_wconv")
    w_conv_full = jnp.transpose(wconv_all[0::2], (1, 0, 2)).reshape(3, D)
    b_sh = lax.dynamic_slice(b_ada, (0, chip * NA), (1, NA))
    mod_sh = _ada_fwd(c_all, w_ada2, b_sh)
    mod_all = _all_gather8(mod_sh, "gather_mod")
    mod = lax.dynamic_slice(mod_all[0::2], (0, dev, 0), (4, 1, NA)).reshape(6, D)
    shift1, scale1, gate1, shift2, scale2, gate2 = (mod[k:k + 1] for k in range(6))

    u = _modulate(x2, scale1, shift1, "modulate1")
    pq, (g_qb, g_kvb) = _matmul(u, W_qkv, "nn", F32, "proj_qkv", carry=_gather_plan([sh_qb, sh_kvb]))
    W_qb = jnp.pad(cols(whole(g_qb)).reshape(Q_LORA, N_HEADS, QK_NOPE + QK_ROPE),
                   ((0, 0), (0, 0), (0, QK_PAD - QK_NOPE - QK_ROPE))).reshape(Q_LORA, N_HEADS * QK_PAD)
    W_kvb = cols(whole(g_kvb))
    pc, (g_oa, g_ob) = _matmul(u, W_conv, "nn", F32, "proj_conv", carry=_gather_plan([sh_oa, sh_ob]))
    pg, (g_o,) = _matmul(u, W_gate, "nn", F32, "proj_gate", carry=_gather_plan([sh_o]))
    W_oa, W_ob, W_o = (g.reshape(-1, D) for g in (g_oa, g_ob, g_o))
    rq, rkv, kr = _rms_fwd(pq, tab, g_q_a, g_kv_a)
    q = _q_rope(_matmul(rq, W_qb, "nn", F32, "q_b"), tab)
    kv = _matmul(rkv, W_kvb, "nn", BF16, "kv_b")
    o, lse, g_fi = _attn_fwd(q, kv, kr, carry=_gather_plan([sh_fi], (0, 2)))
    y_a, g_fi = _matmul(o, W_oa, "nn", F32, "o_a", carry=_gather_plan([sh_fi], (2, 4), g_fi))
    hb = _conv_fwd(pc, w_conv_full)
    y_b = _matmul(hb, W_ob, "nn", F32, "o_b")
    merged = _merge_fwd(y_a, y_b, pg)
    mix, g_fi = _matmul(merged, W_o, "nn", F32, "w_o", carry=_gather_plan([sh_fi], (3, 4), g_fi))
    W_fi = cols(whole(g_fi[0]))
    x1, u2 = _ln1_fwd(x2, mix, gate1, ln1_g, ln1_b, scale2, shift2)
    hh, (g_fo,) = _matmul(u2, W_fi, "nn", F32, "ffn_in", carry=_gather_plan([sh_fo]))
    W_fo = g_fo.reshape(F, D)
    act = _swiglu_fwd(hh)
    ffn = _matmul(act, W_fo, "nn", F32, "ffn_out")

    core_i = ac.astype(jnp.int32).reshape(1)
    chip_i = chip.astype(jnp.int32).reshape(1)

    def uncols(g):
        return jnp.transpose(g.reshape(g.shape[0], 4, g.shape[1] // 4), (1, 0, 2))

    def slabs(p):
        return p.reshape(4, 2, p.shape[1] // 2, p.shape[2])

    def add_pairs(parts, sibs, nms):
        return [_add_pair(a, b, core_i, "add_pair_" + nm) for a, b, nm in zip(parts, sibs, nms)]

    def sum_all(pre, recv, nms):
        return [_sum_slabs(a, r, chip_i, "sum_slabs_" + nm) for a, r, nm in zip(pre, recv, nms)]

    dffn, dx1a, loss_acc, d_ln2_g, d_ln2_b, d_gate2 = _ln2_loss_bwd(x1, ffn, gate2, ln2_g, ln2_b, tgt)
    loss = lax.psum(loss_acc[0, 0], ("x", "y", "c"))
    dW_fo = _matmul(act, dffn, "tn", BF16, "d_w_ffn_out")
    p_fo = [slabs(dW_fo.reshape(4, -1, D))]
    dact, s_fo = _matmul(dffn, W_fo, "nt", F32, "d_act", carry=_pair_plan(p_fo))
    pre_fo = add_pairs(p_fo, s_fo, ["w_ffn_out"])
    dhh = _swiglu_bwd(dact, hh)
    dW_fi, r_fo = _matmul(u2, dhh, "tn", BF16, "d_w_ffn_in", carry=_scatter_plan(pre_fo))
    p_fi = [slabs(uncols(dW_fi))]
    du2, s_fi = _matmul(dhh, W_fi, "nt", F32, "d_u2", carry=_pair_plan(p_fi))
    pre_fi = add_pairs(p_fi, s_fi, ["w_ffn_in"])
    dmix, dxa, d_shift2, d_scale2, d_ln1_g, d_ln1_b, d_gate1 = _ln1_bwd(x2, mix, dx1a, du2, gate1, ln1_g, ln1_b, scale2)
    dW_o = _matmul(merged, dmix, "tn", BF16, "d_w_o")
    dmerged = _matmul(dmix, W_o, "nt", F32, "d_merged")
    dy_a, dy_b, dgate = _merge_bwd(dmerged, y_a, y_b, pg)
    dW_oa = _matmul(o, dy_a, "tn", BF16, "d_w_o_a")
    do = _matmul(dy_a, W_oa, "nt", BF16, "d_o")
    dW_ob = _matmul(hb, dy_b, "tn", BF16, "d_w_o_b")
    p_mid = [slabs(g.reshape(4, -1, D)) for g in (dW_oa, dW_ob, dW_o)]
    dhb, s_mid = _matmul(dy_b, W_ob, "nt", F32, "d_hb", carry=_pair_plan(p_mid))
    pre_mid = add_pairs(p_mid, s_mid, ["w_o_a", "w_o_b", "w_o"])
    dconv, d_wconv = _conv_bwd(dhb, pc, w_conv_full)
    dq, dkv, dkr, r_fi = _attn_bwd(q, kv, kr, do, o, lse, tab, carry=_scatter_plan(pre_fi))
    names_a = ["w_ffn_out", "w_ffn_in", "w_o_a", "w_o_b", "w_o"]
    dW_qb = _matmul(rq, dq, "tn", BF16, "d_w_q_b")
    d_rq = _matmul(dq, W_qb, "nt", F32, "d_rq")
    dW_kvb = _matmul(rkv, dkv, "tn", BF16, "d_w_kv_b")
    d_rkv = _matmul(dkv, W_kvb, "nt", F32, "d_rkv")
    dqkv, d_g_q, d_g_kv = _rms_bwd(d_rq, d_rkv, pq, dkr, g_q_a, g_kv_a)
    dW_qkv = _matmul(u, dqkv, "tn", BF16, "d_w_qkv")
    dW_conv, r_mid = _matmul(u, dconv, "tn", BF16, "d_w_conv", carry=_scatter_plan(pre_mid, (0, 2)))
    dW_gate, r_mid = _matmul(u, dgate, "tn", BF16, "d_w_gate", carry=_scatter_plan(pre_mid, (1, 2), r_mid))
    fin_a = sum_all(pre_fo + pre_fi + pre_mid, list(r_fo) + list(r_fi) + list(r_mid), names_a)
    dW_in = jnp.concatenate([dW_qkv[:, :n_qkv], dW_conv, dW_gate], axis=1)
    dW_qb_u = dW_qb.reshape(Q_LORA, N_HEADS, QK_PAD)[:, :, :QK_NOPE + QK_ROPE].reshape(Q_LORA, -1)
    names_b = ["w_in", "w_q_b", "w_kv_b"]
    p_b = [slabs(uncols(g)) for g in (dW_in, dW_qb_u, dW_kvb)]
    du, s_b = _matmul(dqkv, W_qkv, "nt", F32, "d_u_qkv", carry=_pair_plan(p_b))
    pre_b = add_pairs(p_b, s_b, names_b)
    du, moved = _matmul(dconv, W_conv, "nt", F32, "d_u_conv", add=du,
                        carry=_join_plans([_scatter_plan(pre_b, (0, 4)), _sibling_plan(fin_a)]))
    r_b, fs_a = moved[:3], moved[3:]
    du, r_b = _matmul(dgate, W_gate, "nt", F32, "d_u_gate", add=du, carry=_scatter_plan(pre_b, (1, 4), r_b))
    grad_x, d_shift1, d_scale1 = _dx_final(dxa, du, x2, scale1)

    def pad_d(v):
        return jnp.pad(v, ((0, 0), (0, D - v.shape[1])))

    small = jnp.concatenate([d_ln1_g, d_ln1_b, d_ln2_g, d_ln2_b, pad_d(d_g_q), pad_d(d_g_kv), d_wconv,
                             d_shift1, d_scale1, d_gate1, d_shift2, d_scale2, d_gate2, jnp.zeros((1, D), F32)], axis=0)
    small_all = _all_gather8(small, "gather_small")
    small_sum = _sum8(small_all)
    g_ln1_g, g_ln1_b, g_ln2_g, g_ln2_b = (small_sum[k:k + 1] for k in range(4))
    g_g_q, g_g_kv = small_sum[4:5, :Q_LORA], small_sum[5:6, :KV_LORA]
    g_wconv = lax.dynamic_slice(small_sum[6:9], (0, chip * CW), (3, CW))
    g_b_ada = small_sum[9:15].reshape(1, 6 * D)
    dmod_all = small_all[:, 9:15, :].reshape(8, 6 * D)
    g_w_ada = _ada_bwd(c_all, lax.dynamic_slice(dmod_all, (0, chip * NA), (8, NA)))

    big = {}
    ws = dict(w_in=(w_in2, m_w_in[0], v_w_in[0]), w_q_b=(w_q_b2, m_w_q_b[0], v_w_q_b[0]),
              w_kv_b=(w_kv_b2, m_w_kv_b[0], v_w_kv_b[0]), w_o_a=(w_o_a2, m_w_o_a[0], v_w_o_a[0]),
              w_o_b=(w_o_b2, m_w_o_b[0], v_w_o_b[0]), w_o=(w_o2, m_w_o[0], v_w_o[0]),
              w_ffn_in=(w_ffn_in2, m_w_ffn_in[0], v_w_ffn_in[0]), w_ffn_out=(w_ffn_out2, m_w_ffn_out[0], v_w_ffn_out[0]))
    def adam_of(nm, a, b, carry=None):
        w_, m_, v_ = ws[nm]
        return _adam_halves("adam_" + nm, w_, m_, v_, a, b, core_i, carry)

    big["w_ffn_in"], r_b = adam_of("w_ffn_in", fin_a[1], fs_a[1], _scatter_plan(pre_b, (2, 4), r_b))
    upd, r_b = _adam("adam_w_ada", w_ada2, m_w_ada[0], v_w_ada[0], g_w_ada, _scatter_plan(pre_b, (3, 4), r_b))
    big["w_ada"] = [g_w_ada] + list(upd)
    fin_b = sum_all(pre_b, r_b, names_b)
    fs_b = _run_plan(_sibling_plan(fin_b), "sibling_last")
    for nm, a, b in zip(names_a + names_b, fin_a + fin_b, list(fs_a) + list(fs_b)):
        if nm != "w_ffn_in":
            big[nm] = adam_of(nm, a, b)
    sm = {}
    for nm, w_, m_, v_, g_ in [("b_ada", b_ada, m_b_ada, v_b_ada, g_b_ada), ("g_q_a", g_q_a, m_g_q_a, v_g_q_a, g_g_q),
                               ("g_kv_a", g_kv_a, m_g_kv_a, v_g_kv_a, g_g_kv),
                               ("w_conv", w_conv[0], m_w_conv[0], v_w_conv[0], g_wconv),
                               ("ln1_g", ln1_g, m_ln1_g, v_ln1_g, g_ln1_g), ("ln1_b", ln1_b, m_ln1_b, v_ln1_b, g_ln1_b),
                               ("ln2_g", ln2_g, m_ln2_g, v_ln2_g, g_ln2_g), ("ln2_b", ln2_b, m_ln2_b, v_ln2_b, g_ln2_b)]:
        sm[nm] = (g_,) + tuple(_adam_small("adam_" + nm, w_, m_, v_, g_))

    order = ["w_ada", "b_ada", "w_in", "g_q_a", "w_q_b", "g_kv_a", "w_kv_b", "w_o_a", "w_conv", "w_o_b", "w_o",
             "ln1_g", "ln1_b", "w_ffn_in", "w_ffn_out", "ln2_g", "ln2_b"]
    lead = {"b_ada", "g_q_a", "g_kv_a", "ln1_g", "ln1_b", "ln2_g", "ln2_b"}

    def leaf(nm, k):
        val = big[nm][k] if nm in big else sm[nm][k]
        return val if nm in lead else val[None]

    outs = [loss, grad_x[None]]
    for k in range(4):
        outs += [leaf(nm, k) for nm in order]
    return tuple(outs)
```

```python
import functools

import jax
import jax.numpy as jnp
from jax import lax
from jax.experimental import pallas as pl
from jax.experimental.pallas import tpu as pltpu

F32, BF16 = jnp.float32, jnp.bfloat16
N_HEADS, QK_NOPE, QK_ROPE, V_HEAD = 16, 128, 64, 128
Q_LORA, KV_LORA = 512, 512
QK_PAD = 256
QKV_A = 1152
CHUNK_SHIFT = 6
ATTN_SCALE = (QK_NOPE + QK_ROPE) ** -0.5
ROPE_THETA = 10000.0
ALPHA = 2.0 ** 0.25
LN_EPS, RMS_EPS = 1e-5, 1e-6
ADAM_LR, ADAM_B1, ADAM_B2, ADAM_EPS, ADAM_WD, ADAM_STEP = 0.001, 0.9, 0.999, 1e-08, 0.01, 10
ADAM_C1 = 1.0 - ADAM_B1 ** ADAM_STEP
ADAM_C2 = 1.0 - ADAM_B2 ** ADAM_STEP
VMEM_LIMIT = 56 * 1024 * 1024
MESH = pl.DeviceIdType.MESH
ANY = pl.BlockSpec(memory_space=pl.ANY)
NT = (((1,), (1,)), ((), ()))
TN = (((0,), (0,)), ((), ()))
NN = (((1,), (0,)), ((), ()))


def _params(sem=None):
    return pltpu.CompilerParams(dimension_semantics=sem, vmem_limit_bytes=VMEM_LIMIT)


def _pick(n, cands=(1408, 1024, 512, 384, 256, 128)):
    for t in cands:
        if n % t == 0:
            return t
    return n


def _row_tile(rows, row_bytes, budget, mult=8):
    best = mult
    for t in range(mult, rows + 1, mult):
        if rows % t == 0 and t * row_bytes <= budget:
            best = t
    return best


def _tile2(rows, cols, mult=8, budget=1 << 18):
    col_tiles = [t for t in range(128, cols + 1, 128) if cols % t == 0] or [cols]
    best = None
    for tc in col_tiles:
        for tr in range(mult, rows + 1, mult):
            if rows % tr == 0 and tr * tc <= budget and (best is None or tr * tc > best[0] * best[1]):
                best = (tr, tc)
    assert best is not None, (rows, cols)
    return best


def _sigmoid(x):
    return jax.nn.sigmoid(x)


class _Plan:
    def __init__(self, ins, outs, sems, start, finish, aliases=None):
        self.ins, self.outs, self.sems, self.start, self.finish = list(ins), list(outs), list(sems), start, finish
        self.aliases = dict(aliases or {})

    def io_aliases(self, first_in, first_out):
        return {first_in + i: first_out + o for i, o in self.aliases.items()}


def _run_plan(plan, name):
    n_in, n_out = len(plan.ins), len(plan.outs)

    def body(*refs):
        ins, outs, sems = refs[:n_in], refs[n_in:n_in + n_out], refs[n_in + n_out:]
        plan.start(ins, outs, sems)
        plan.finish(ins, outs, sems)

    return pl.pallas_call(body, name=name, out_shape=plan.outs, in_specs=[ANY] * n_in, out_specs=[ANY] * n_out,
                          scratch_shapes=plan.sems, input_output_aliases=plan.io_aliases(0, 0),
                          compiler_params=_params())(*plan.ins)


def _matmul(a, b, mode, out_dtype, name, add=None, carry=None, shards=None):
    if mode == "nn":
        (M, K), N, dims = a.shape, b.shape[-1] * (4 if shards else 1), NN
    elif mode == "nt":
        (M, K), N, dims = a.shape, b.shape[-2], NT
    else:
        (K, M), N, dims = a.shape, b.shape[1], TN
    split_n = shards and mode != "nt"
    tm = _pick(M)
    tn = _pick(N // 4) if split_n else _pick(N)
    if shards and mode == "nt":
        tk = _pick(K // 4)
    else:
        tk = K if K <= 2048 else _pick(K)
    nk = K // tk
    per = (N // 4 // tn) if split_n else (K // 4 // tk if shards else 1)
    a_spec = (pl.BlockSpec((tk, tm), lambda i, j, k: (k, i)) if mode == "tn"
              else pl.BlockSpec((tm, tk), lambda i, j, k: (i, k)))
    if shards == "b" and mode == "nn":
        b_spec = pl.BlockSpec((None, tk, tn), lambda i, j, k: (j // per, k, j % per))
    elif shards == "b":
        b_spec = pl.BlockSpec((None, tn, tk), lambda i, j, k: (k // per, j, k % per))
    else:
        b_spec = (pl.BlockSpec((tn, tk), lambda i, j, k: (j, k)) if mode == "nt"
                  else pl.BlockSpec((tk, tn), lambda i, j, k: (k, j)))
    o_spec = pl.BlockSpec((tm, tn), lambda i, j, k: (i, j))
    o_shape = (M, N)
    if shards == "o":
        o_spec, o_shape = pl.BlockSpec((None, tm, tn), lambda i, j, k: (j // per, i, j % per)), (4, M, N // 4)
    has_add = add is not None
    n_ci = len(carry.ins) if carry else 0
    n_co = len(carry.outs) if carry else 0
    n_in = 2 + has_add
    grid = (M // tm, N // tn, nk)

    def body(*refs):
        a_ref, b_ref = refs[0], refs[1]
        add_ref = refs[2] if has_add else None
        o_ref = refs[n_in + n_ci]
        acc_ref = refs[n_in + n_ci + 1 + n_co]
        c_ins = refs[n_in:n_in + n_ci]
        c_outs = refs[n_in + n_ci + 1:n_in + n_ci + 1 + n_co]
        c_sems = refs[n_in + n_ci + 2 + n_co:]
        i, j, k = pl.program_id(0), pl.program_id(1), pl.program_id(2)

        if carry:
            @pl.when((i == 0) & (j == 0) & (k == 0))
            def _():
                carry.start(c_ins, c_outs, c_sems)

        part = lax.dot_general(a_ref[...], b_ref[...], dims, preferred_element_type=F32)
        if nk == 1:
            o_ref[...] = (part + add_ref[...] if has_add else part).astype(o_ref.dtype)
        else:
            @pl.when(k == 0)
            def _():
                acc_ref[...] = part

            @pl.when((k > 0) & (k < nk - 1))
            def _():
                acc_ref[...] += part

            @pl.when(k == nk - 1)
            def _():
                r = acc_ref[...] + part
                if has_add:
                    r = r + add_ref[...]
                o_ref[...] = r.astype(o_ref.dtype)

        if carry:
            @pl.when((i == grid[0] - 1) & (j == grid[1] - 1) & (k == nk - 1))
            def _():
                carry.finish(c_ins, c_outs, c_sems)

    ins = [a, b] + ([add] if has_add else []) + (carry.ins if carry else [])
    in_specs = [a_spec, b_spec] + ([o_spec] if has_add else []) + [ANY] * n_ci
    res = pl.pallas_call(
        body, name=name, grid=grid,
        in_specs=in_specs, out_specs=[o_spec] + [ANY] * n_co,
        out_shape=[jax.ShapeDtypeStruct(o_shape, out_dtype)] + (carry.outs if carry else []),
        scratch_shapes=[pltpu.VMEM((tm, tn), F32)] + (carry.sems if carry else []),
        input_output_aliases=carry.io_aliases(n_in, 1) if carry else {},
        compiler_params=_params(("arbitrary",) * 3 if carry else ("parallel", "parallel", "arbitrary")),
    )(*ins)
    return (res[0], res[1:]) if carry else res[0]


def _rows(body, name, n_rows, tm, ins, outs, accs=()):
    grid = (n_rows // tm,)
    per8 = tm // 8
    last8 = n_rows // 8 - 1
    arrays, in_specs = [], []
    for spec in ins:
        kind, arr = spec[0], spec[1]
        arrays.append(arr)
        if kind == "row":
            _, _, cb, w = spec
            in_specs.append(pl.BlockSpec((tm, w), lambda i, cb=cb: (i, cb)))
        elif kind == "full":
            in_specs.append(pl.BlockSpec(arr.shape, lambda i, nd=arr.ndim: (0,) * nd))
        elif kind == "prev":
            _, _, cb, w = spec
            in_specs.append(pl.BlockSpec((8, w), lambda i, cb=cb: (jnp.maximum(i * per8 - 1, 0), cb)))
        else:
            _, _, cb, w = spec
            in_specs.append(pl.BlockSpec((8, w), lambda i, cb=cb: (jnp.minimum((i + 1) * per8, last8), cb)))
    out_shape = [jax.ShapeDtypeStruct((n_rows, w), dt) for (w, dt) in outs]
    out_specs = [pl.BlockSpec((tm, w), lambda i: (i, 0)) for (w, _) in outs]
    out_shape += [jax.ShapeDtypeStruct(s, F32) for s in accs]
    out_specs += [pl.BlockSpec(s, lambda i, nd=len(s): (0,) * nd) for s in accs]
    n_in, n_out = len(ins), len(outs)

    def kernel_body(*refs):
        body(pl.program_id(0), refs[:n_in], refs[n_in:n_in + n_out], refs[n_in + n_out:])

    res = pl.pallas_call(
        kernel_body, name=name, grid=grid, in_specs=in_specs, out_specs=out_specs, out_shape=out_shape,
        compiler_params=_params(("arbitrary",)),
    )(*arrays)
    return res


def _acc_add(i, ref, val):
    @pl.when(i == 0)
    def _():
        ref[...] = val

    @pl.when(i > 0)
    def _():
        ref[...] += val


def _rope(t, tab, sign):
    c, sa, sb = tab[:, 0:128], tab[:, 128:256], tab[:, 256:384]
    rot = pltpu.roll(t, 96, 1) * sa + pltpu.roll(t, 32, 1) * sb
    return t * c + rot if sign > 0 else t * c - rot


def _ln_stats(r):
    mu = jnp.mean(r, axis=-1, keepdims=True)
    d = r - mu
    var = jnp.mean(d * d, axis=-1, keepdims=True)
    rstd = lax.rsqrt(var + LN_EPS)
    return d * rstd, rstd


def _ln_bwd(dxh, xh, rstd):
    m1 = jnp.mean(dxh, axis=-1, keepdims=True)
    m2 = jnp.mean(dxh * xh, axis=-1, keepdims=True)
    return rstd * (dxh - m1 - xh * m2)


def _modulate(x, scale, shift, name):
    S, D = x.shape

    def body(i, ins, outs, accs):
        outs[0][...] = (ins[0][...] * (1.0 + ins[1][...]) + ins[2][...]).astype(BF16)

    return _rows(body, name, S, _pick(S, (256, 128)), [("row", x, 0, D), ("full", scale), ("full", shift)], [(D, BF16)])[0]


def _rms_fwd(pq, tab, g_q, g_kv):
    S = pq.shape[0]

    def body(i, ins, outs, accs):
        pq_ref, tab_ref, gq_ref, gkv_ref = ins

        def rms(x, g):
            return x * lax.rsqrt(jnp.mean(x * x, axis=-1, keepdims=True) + RMS_EPS) * g

        outs[0][...] = rms(pq_ref[:, 0:Q_LORA], gq_ref[...]).astype(BF16)
        outs[1][...] = rms(pq_ref[:, Q_LORA:Q_LORA + KV_LORA], gkv_ref[...]).astype(BF16)
        outs[2][...] = _rope(pq_ref[:, Q_LORA + KV_LORA:QKV_A], tab_ref[...], 1).astype(BF16)

    return _rows(body, "rms_fwd", S, _pick(S, (256, 128)),
                 [("row", pq, 0, QKV_A), ("row", tab, 0, 384), ("full", g_q), ("full", g_kv)],
                 [(Q_LORA, BF16), (KV_LORA, BF16), (128, BF16)])


def _q_rope(q, tab):
    S, W = q.shape

    def body(i, ins, outs, accs):
        q_ref, tab_ref = ins
        t = tab_ref[...]
        for h in range(N_HEADS):
            lo = h * QK_PAD
            outs[0][:, lo:lo + 128] = q_ref[:, lo:lo + 128].astype(BF16)
            outs[0][:, lo + 128:lo + 256] = _rope(q_ref[:, lo + 128:lo + 256], t, 1).astype(BF16)

    return _rows(body, "q_rope", S, _pick(S, (256, 128)), [("row", q, 0, W), ("row", tab, 0, 384)], [(W, BF16)])[0]


def _allowed(q0, k0, bq):
    row = q0 + lax.broadcasted_iota(jnp.int32, (bq, bq), 0)
    col = k0 + lax.broadcasted_iota(jnp.int32, (bq, bq), 1)
    return (col >> CHUNK_SHIFT) <= (row >> CHUNK_SHIFT)


ATTN_BLOCK = 512


def _attn_fwd(q, kv, kr, carry=None):
    S = q.shape[0]
    bq = min(ATTN_BLOCK, S)
    nq = S // bq
    n_ci = len(carry.ins) if carry else 0
    n_co = len(carry.outs) if carry else 0

    def body(*refs):
        q_ref, kn_ref, v_ref, kr_ref = refs[:4]
        o_ref, lse_ref = refs[4 + n_ci:6 + n_ci]
        c_ins, c_outs = refs[4:4 + n_ci], refs[6 + n_ci:6 + n_ci + n_co]
        kcat = refs[6 + n_ci + n_co]
        c_sems = refs[7 + n_ci + n_co:]
        qi = pl.program_id(1)
        if carry:
            @pl.when((pl.program_id(0) == 0) & (qi == 0))
            def _():
                carry.start(c_ins, c_outs, c_sems)

        @pl.when(qi == 0)
        def _():
            kcat[:, 0:128] = kn_ref[...]
            kcat[:, 128:256] = kr_ref[...]

        qv = q_ref[...]

        def step(j, carry, masked):
            m, l, acc = carry
            off = pl.multiple_of(j * bq, bq)
            s = lax.dot_general(qv, kcat[pl.ds(off, bq), :], NT, preferred_element_type=F32) * ATTN_SCALE
            if masked:
                s = jnp.where(_allowed(qi * bq, off, bq), s, -1e30)
            m_new = jnp.maximum(m, jnp.max(s, axis=1, keepdims=True))
            a = jnp.exp(m - m_new)
            p = jnp.exp(s - m_new)
            l = a * l + jnp.sum(p, axis=1, keepdims=True)
            acc = a * acc + jnp.dot(p.astype(BF16), v_ref[pl.ds(off, bq), :], preferred_element_type=F32)
            return m_new, l, acc

        init = (jnp.full((bq, 1), -1e30, F32), jnp.zeros((bq, 1), F32), jnp.zeros((bq, V_HEAD), F32))
        below = lax.fori_loop(0, qi, lambda j, cr: step(j, cr, False), init)
        m, l, acc = step(qi, below, True)
        o_ref[...] = (acc / l).astype(BF16)
        lse_ref[0] = m + jnp.log(l)
        if carry:
            @pl.when((pl.program_id(0) == N_HEADS - 1) & (qi == nq - 1))
            def _():
                carry.finish(c_ins, c_outs, c_sems)

    res = pl.pallas_call(
        body, name="attn_fwd", grid=(N_HEADS, nq),
        in_specs=[pl.BlockSpec((bq, QK_PAD), lambda h, i: (i, h)),
                  pl.BlockSpec((S, 128), lambda h, i: (0, 2 * h)),
                  pl.BlockSpec((S, 128), lambda h, i: (0, 2 * h + 1)),
                  pl.BlockSpec((S, 128), lambda h, i: (0, 0))] + [ANY] * n_ci,
        out_specs=[pl.BlockSpec((bq, V_HEAD), lambda h, i: (i, h)),
                   pl.BlockSpec((1, bq, 1), lambda h, i: (h, i, 0))] + [ANY] * n_co,
        out_shape=[jax.ShapeDtypeStruct((S, N_HEADS * V_HEAD), BF16),
                   jax.ShapeDtypeStruct((N_HEADS, S, 1), F32)] + (carry.outs if carry else []),
        scratch_shapes=[pltpu.VMEM((S, QK_PAD), BF16)] + (carry.sems if carry else []),
        input_output_aliases=carry.io_aliases(4, 2) if carry else {},
        compiler_params=_params(("arbitrary", "arbitrary")),
    )(q, kv, kv, kr, *(carry.ins if carry else []))
    return res[0], res[1], res[2:]


def _attn_bwd(q, kv, kr, do, o, lse, tab, carry=None):
    S = q.shape[0]
    bq = min(ATTN_BLOCK, S)
    nq = S // bq

    n_ci = len(carry.ins) if carry else 0
    n_co = len(carry.outs) if carry else 0

    def body(*refs):
        q_ref, kn_ref, v_ref, kr_ref, do_ref, o_ref, lse_ref, tab_ref = refs[:8]
        dq_ref, dkv_ref, dkr_ref = refs[8 + n_ci:11 + n_ci]
        dq_acc, dk_acc, dv_acc, kcat, delta = refs[11 + n_ci + n_co:16 + n_ci + n_co]
        c_ins, c_outs, c_sems = refs[8:8 + n_ci], refs[11 + n_ci:11 + n_ci + n_co], refs[16 + n_ci + n_co:]
        h = pl.program_id(0)
        if carry:
            @pl.when(h == 0)
            def _():
                carry.start(c_ins, c_outs, c_sems)

        dq_acc[...] = jnp.zeros_like(dq_acc)
        dk_acc[...] = jnp.zeros_like(dk_acc)
        dv_acc[...] = jnp.zeros_like(dv_acc)
        kcat[:, 0:128] = kn_ref[...]
        kcat[:, 128:256] = kr_ref[...]
        for r in range(nq):
            rows = slice(r * bq, (r + 1) * bq)
            delta[rows, :] = jnp.sum(do_ref[rows, :].astype(F32) * o_ref[rows, :].astype(F32), axis=1, keepdims=True)

        def pair(i, j, masked):
            rows_i = pl.ds(pl.multiple_of(i * bq, bq), bq)
            rows_j = pl.ds(pl.multiple_of(j * bq, bq), bq)
            qv, dov, k = q_ref[rows_i, :], do_ref[rows_i, :], kcat[rows_j, :]
            s = lax.dot_general(qv, k, NT, preferred_element_type=F32) * ATTN_SCALE
            if masked:
                s = jnp.where(_allowed(i * bq, j * bq, bq), s, -1e30)
            p = jnp.exp(s - lse_ref[0, rows_i, :])
            dv_acc[rows_j, :] += lax.dot_general(p.astype(BF16), dov, TN, preferred_element_type=F32)
            dp = lax.dot_general(dov, v_ref[rows_j, :], NT, preferred_element_type=F32)
            ds = (p * (dp - delta[rows_i, :]) * ATTN_SCALE).astype(BF16)
            dk_acc[rows_j, :] += lax.dot_general(ds, qv, TN, preferred_element_type=F32)
            dq_acc[rows_i, :] += jnp.dot(ds, k, preferred_element_type=F32)

        def kv_step(j, _):
            pair(j, j, True)

            def q_step(i, _):
                pair(i, j, False)
                return 0

            lax.fori_loop(j + 1, nq, q_step, 0)
            return 0

        lax.fori_loop(0, nq, kv_step, 0)

        for r in range(nq):
            rows = slice(r * bq, (r + 1) * bq)
            dq_ref[rows, 0:128] = dq_acc[rows, 0:128].astype(BF16)
            dq_ref[rows, 128:256] = _rope(dq_acc[rows, 128:256], tab_ref[rows, :], -1).astype(BF16)
        dkv_ref[:, 0:128] = dk_acc[:, 0:128].astype(BF16)
        dkv_ref[:, 128:256] = dv_acc[...].astype(BF16)

        @pl.when(h == 0)
        def _():
            dkr_ref[...] = dk_acc[:, 128:256]

        @pl.when(h > 0)
        def _():
            dkr_ref[...] += dk_acc[:, 128:256]

        @pl.when(h == N_HEADS - 1)
        def _():
            for r in range(nq):
                rows = slice(r * bq, (r + 1) * bq)
                dkr_ref[rows, :] = _rope(dkr_ref[rows, :], tab_ref[rows, :], -1)
            if carry:
                carry.finish(c_ins, c_outs, c_sems)

    W = N_HEADS * QK_PAD
    res = pl.pallas_call(
        body, name="attn_bwd", grid=(N_HEADS,),
        in_specs=[pl.BlockSpec((S, QK_PAD), lambda h: (0, h)),
                  pl.BlockSpec((S, 128), lambda h: (0, 2 * h)),
                  pl.BlockSpec((S, 128), lambda h: (0, 2 * h + 1)),
                  pl.BlockSpec((S, 128), lambda h: (0, 0)),
                  pl.BlockSpec((S, V_HEAD), lambda h: (0, h)),
                  pl.BlockSpec((S, V_HEAD), lambda h: (0, h)),
                  pl.BlockSpec((1, S, 1), lambda h: (h, 0, 0)),
                  pl.BlockSpec((S, 384), lambda h: (0, 0))] + [ANY] * n_ci,
        out_specs=[pl.BlockSpec((S, QK_PAD), lambda h: (0, h)),
                   pl.BlockSpec((S, QK_PAD), lambda h: (0, h)),
                   pl.BlockSpec((S, 128), lambda h: (0, 0))] + [ANY] * n_co,
        out_shape=[jax.ShapeDtypeStruct((S, W), BF16), jax.ShapeDtypeStruct((S, W), BF16),
                   jax.ShapeDtypeStruct((S, 128), F32)] + (carry.outs if carry else []),
        scratch_shapes=[pltpu.VMEM((S, QK_PAD), F32), pltpu.VMEM((S, QK_PAD), F32), pltpu.VMEM((S, V_HEAD), F32),
                        pltpu.VMEM((S, QK_PAD), BF16), pltpu.VMEM((S, 1), F32)]
        + (carry.sems if carry else []),
        input_output_aliases=carry.io_aliases(8, 3) if carry else {},
        compiler_params=_params(("arbitrary",)),
    )(q, kv, kv, kr, do, o, lse, tab, *(carry.ins if carry else []))
    return res[0], res[1], res[2], res[3:]


def _shift_down(cur, prev8, i, n):
    tm = cur.shape[0]
    prev8 = jnp.where(i == 0, jnp.zeros_like(prev8), prev8)
    full = jnp.concatenate([prev8, cur], axis=0)
    return pltpu.roll(full, n, 0)[8:8 + tm, :]


def _shift_up(cur, next8, i, last, n):
    tm = cur.shape[0]
    next8 = jnp.where(i == last, jnp.zeros_like(next8), next8)
    full = jnp.concatenate([cur, next8], axis=0)
    return pltpu.roll(full, tm + 8 - n, 0)[0:tm, :]


def _conv_fwd(pc, w_conv):
    S, D = pc.shape[0], pc.shape[1] // 3
    tm = _pick(S, (256, 128))

    def body(i, ins, outs, accs):
        b_ref, c_ref, x_ref, cp_ref, xp_ref, w_ref = ins
        z = c_ref[...] * x_ref[...]
        zp = cp_ref[...] * xp_ref[...]
        cz = w_ref[0:1, :] * _shift_down(z, zp, i, 2) + w_ref[1:2, :] * _shift_down(z, zp, i, 1) + w_ref[2:3, :] * z
        outs[0][...] = (b_ref[...] * cz).astype(BF16)

    return _rows(body, "conv_fwd", S, tm,
                 [("row", pc, 0, D), ("row", pc, 1, D), ("row", pc, 2, D), ("prev", pc, 1, D), ("prev", pc, 2, D),
                  ("full", w_conv)], [(D, BF16)])[0]


def _conv_bwd(dhb, pc, w_conv):
    S, D = dhb.shape
    tm = _pick(S, (256, 128))
    last = S // tm - 1

    def body(i, ins, outs, accs):
        g_ref, b_ref, c_ref, x_ref, cp_ref, xp_ref, gn_ref, bn_ref, w_ref = ins
        w0, w1, w2 = w_ref[0:1, :], w_ref[1:2, :], w_ref[2:3, :]
        c, x, g = c_ref[...], x_ref[...], g_ref[...]
        z = c * x
        zp = cp_ref[...] * xp_ref[...]
        z1, z2 = _shift_down(z, zp, i, 1), _shift_down(z, zp, i, 2)
        cz = w0 * z2 + w1 * z1 + w2 * z
        dcz = g * b_ref[...]
        dczn = gn_ref[...] * bn_ref[...]
        dz = w2 * dcz + w1 * _shift_up(dcz, dczn, i, last, 1) + w0 * _shift_up(dcz, dczn, i, last, 2)
        outs[0][:, 0:D] = (g * cz).astype(BF16)
        outs[0][:, D:2 * D] = (dz * x).astype(BF16)
        outs[0][:, 2 * D:3 * D] = (dz * c).astype(BF16)
        dw = jnp.concatenate([jnp.sum(dcz * z2, axis=0, keepdims=True), jnp.sum(dcz * z1, axis=0, keepdims=True),
                              jnp.sum(dcz * z, axis=0, keepdims=True)], axis=0)
        _acc_add(i, accs[0], dw)

    return _rows(body, "conv_bwd", S, tm,
                 [("row", dhb, 0, D), ("row", pc, 0, D), ("row", pc, 1, D), ("row", pc, 2, D),
                  ("prev", pc, 1, D), ("prev", pc, 2, D), ("next", dhb, 0, D), ("next", pc, 0, D), ("full", w_conv)],
                 [(3 * D, BF16)], [(3, D)])


def _merge_fwd(y_a, y_b, pg):
    S, D = y_a.shape

    def body(i, ins, outs, accs):
        ya, yb, ga, gb = ins
        outs[0][...] = (_sigmoid(ga[...]) * ya[...] + _sigmoid(gb[...]) * yb[...]).astype(BF16)

    return _rows(body, "merge_fwd", S, _pick(S, (256, 128)),
                 [("row", y_a, 0, D), ("row", y_b, 0, D), ("row", pg, 0, D), ("row", pg, 1, D)], [(D, BF16)])[0]


def _merge_bwd(dm, y_a, y_b, pg):
    S, D = dm.shape

    def body(i, ins, outs, accs):
        d, ya, yb = ins[0][...], ins[1][...], ins[2][...]
        sa, sb = _sigmoid(ins[3][...]), _sigmoid(ins[4][...])
        outs[0][...] = (d * sa).astype(BF16)
        outs[1][...] = (d * sb).astype(BF16)
        outs[2][:, 0:D] = (d * ya * (sa * (1.0 - sa))).astype(BF16)
        outs[2][:, D:2 * D] = (d * yb * (sb * (1.0 - sb))).astype(BF16)

    return _rows(body, "merge_bwd", S, _pick(S, (256, 128)),
                 [("row", dm, 0, D), ("row", y_a, 0, D), ("row", y_b, 0, D), ("row", pg, 0, D), ("row", pg, 1, D)],
                 [(D, BF16), (D, BF16), (2 * D, BF16)])


def _ln1_fwd(x, mix, gate1, g, b, scale2, shift2):
    S, D = x.shape

    def body(i, ins, outs, accs):
        x_ref, mix_ref, gate_ref, g_ref, b_ref, sc_ref, sh_ref = ins
        xh, _ = _ln_stats(ALPHA * x_ref[...] + gate_ref[...] * mix_ref[...])
        x1 = xh * g_ref[...] + b_ref[...]
        outs[0][...] = x1
        outs[1][...] = (x1 * (1.0 + sc_ref[...]) + sh_ref[...]).astype(BF16)

    return _rows(body, "ln1_fwd", S, _pick(S, (256, 128)),
                 [("row", x, 0, D), ("row", mix, 0, D), ("full", gate1), ("full", g), ("full", b),
                  ("full", scale2), ("full", shift2)], [(D, F32), (D, BF16)])


def _swiglu_fwd(hh):
    S, F = hh.shape[0], hh.shape[1] // 2

    def body(i, ins, outs, accs):
        hg = ins[0][...]
        outs[0][...] = (hg * _sigmoid(hg) * ins[1][...]).astype(BF16)

    return _rows(body, "swiglu_fwd", S, _pick(S, (128,)), [("row", hh, 0, F), ("row", hh, 1, F)], [(F, BF16)])[0]


def _swiglu_bwd(dact, hh):
    S, F = dact.shape

    def body(i, ins, outs, accs):
        d, hg, hu = ins[0][...], ins[1][...], ins[2][...]
        sg = _sigmoid(hg)
        outs[0][:, 0:F] = (d * hu * (sg * (1.0 + hg * (1.0 - sg)))).astype(BF16)
        outs[0][:, F:2 * F] = (d * (hg * sg)).astype(BF16)

    return _rows(body, "swiglu_bwd", S, _pick(S, (128,)),
                 [("row", dact, 0, F), ("row", hh, 0, F), ("row", hh, 1, F)], [(2 * F, BF16)])[0]


def _ln2_loss_bwd(x1, ffn, gate2, g, b, target):
    S, D = x1.shape

    def body(i, ins, outs, accs):
        x1_ref, f_ref, gate_ref, g_ref, b_ref, t_ref = ins
        f = f_ref[...]
        xh, rstd = _ln_stats(ALPHA * x1_ref[...] + gate_ref[...] * f)
        e = xh * g_ref[...] + b_ref[...] - t_ref[...]
        dy = e * (1.0 / D)
        dr = _ln_bwd(dy * g_ref[...], xh, rstd)
        outs[0][...] = (gate_ref[...] * dr).astype(BF16)
        outs[1][...] = ALPHA * dr
        _acc_add(i, accs[0], jnp.full((1, 128), (0.5 / D) * jnp.sum(e * e), F32))
        _acc_add(i, accs[1], jnp.sum(dy * xh, axis=0, keepdims=True))
        _acc_add(i, accs[2], jnp.sum(dy, axis=0, keepdims=True))
        _acc_add(i, accs[3], jnp.sum(dr * f, axis=0, keepdims=True))

    return _rows(body, "ln2_loss_bwd", S, _pick(S, (256, 128)),
                 [("row", x1, 0, D), ("row", ffn, 0, D), ("full", gate2), ("full", g), ("full", b), ("row", target, 0, D)],
                 [(D, BF16), (D, F32)], [(1, 128), (1, D), (1, D), (1, D)])


def _ln1_bwd(x, mix, dx1a, du2, gate1, g, b, scale2):
    S, D = x.shape

    def body(i, ins, outs, accs):
        x_ref, mix_ref, da_ref, du_ref, gate_ref, g_ref, b_ref, sc_ref = ins
        mix, du = mix_ref[...], du_ref[...]
        xh, rstd = _ln_stats(ALPHA * x_ref[...] + gate_ref[...] * mix)
        x1 = xh * g_ref[...] + b_ref[...]
        dx1 = da_ref[...] + du * (1.0 + sc_ref[...])
        dr = _ln_bwd(dx1 * g_ref[...], xh, rstd)
        outs[0][...] = (gate_ref[...] * dr).astype(BF16)
        outs[1][...] = ALPHA * dr
        _acc_add(i, accs[0], jnp.sum(du, axis=0, keepdims=True))
        _acc_add(i, accs[1], jnp.sum(du * x1, axis=0, keepdims=True))
        _acc_add(i, accs[2], jnp.sum(dx1 * xh, axis=0, keepdims=True))
        _acc_add(i, accs[3], jnp.sum(dx1, axis=0, keepdims=True))
        _acc_add(i, accs[4], jnp.sum(dr * mix, axis=0, keepdims=True))

    return _rows(body, "ln1_bwd", S, _pick(S, (256, 128)),
                 [("row", x, 0, D), ("row", mix, 0, D), ("row", dx1a, 0, D), ("row", du2, 0, D),
                  ("full", gate1), ("full", g), ("full", b), ("full", scale2)],
                 [(D, BF16), (D, F32)], [(1, D)] * 5)


def _rms_bwd(d_rq, d_rkv, pq, dkr, g_q, g_kv):
    S = pq.shape[0]

    def body(i, ins, outs, accs):
        dq_ref, dkv_ref, pq_ref, dkr_ref, gq_ref, gkv_ref = ins

        def rms_bwd(dy, x, g):
            r = lax.rsqrt(jnp.mean(x * x, axis=-1, keepdims=True) + RMS_EPS)
            dyg = dy * g
            dx = r * dyg - x * (r * r * r) * jnp.mean(dyg * x, axis=-1, keepdims=True)
            return dx, jnp.sum(dy * (x * r), axis=0, keepdims=True)

        dxq, dgq = rms_bwd(dq_ref[...], pq_ref[:, 0:Q_LORA], gq_ref[...])
        dxkv, dgkv = rms_bwd(dkv_ref[...], pq_ref[:, Q_LORA:Q_LORA + KV_LORA], gkv_ref[...])
        outs[0][:, 0:Q_LORA] = dxq.astype(BF16)
        outs[0][:, Q_LORA:Q_LORA + KV_LORA] = dxkv.astype(BF16)
        outs[0][:, Q_LORA + KV_LORA:QKV_A] = dkr_ref[...].astype(BF16)
        _acc_add(i, accs[0], dgq)
        _acc_add(i, accs[1], dgkv)

    return _rows(body, "rms_bwd", S, _pick(S, (256, 128)),
                 [("row", d_rq, 0, Q_LORA), ("row", d_rkv, 0, KV_LORA), ("row", pq, 0, QKV_A), ("row", dkr, 0, 128),
                  ("full", g_q), ("full", g_kv)], [(QKV_A, BF16)], [(1, Q_LORA), (1, KV_LORA)])


def _dx_final(dxa, du, x, scale1):
    S, D = x.shape

    def body(i, ins, outs, accs):
        du = ins[1][...]
        outs[0][...] = ins[0][...] + du * (1.0 + ins[3][...])
        _acc_add(i, accs[0], jnp.sum(du, axis=0, keepdims=True))
        _acc_add(i, accs[1], jnp.sum(du * ins[2][...], axis=0, keepdims=True))

    return _rows(body, "dx_final", S, _pick(S, (256, 128)),
                 [("row", dxa, 0, D), ("row", du, 0, D), ("row", x, 0, D), ("full", scale1)],
                 [(D, F32)], [(1, D), (1, D)])


def _ada_fwd(c_all, w, bias):
    B, D = c_all.shape
    NA = w.shape[1]
    tn = _pick(NA, (512, 256, 128))

    def body(c_ref, w_ref, b_ref, o_ref):
        cv = c_ref[...]
        ca = (cv * _sigmoid(cv)).astype(BF16)
        o_ref[...] = jnp.dot(ca, w_ref[...].astype(BF16), preferred_element_type=F32) + b_ref[...]

    return pl.pallas_call(
        body, name="ada_fwd", grid=(NA // tn,),
        in_specs=[pl.BlockSpec((B, D), lambda j: (0, 0)), pl.BlockSpec((D, tn), lambda j: (0, j)),
                  pl.BlockSpec((1, tn), lambda j: (0, j))],
        out_specs=pl.BlockSpec((B, tn), lambda j: (0, j)),
        out_shape=jax.ShapeDtypeStruct((B, NA), F32),
        compiler_params=_params(("arbitrary",)),
    )(c_all, w, bias)


def _ada_bwd(c_all, dmod):
    B, D = c_all.shape
    NA = dmod.shape[1]
    tn = _pick(NA, (512, 256, 128))

    def body(c_ref, d_ref, o_ref):
        cv = c_ref[...]
        ca = (cv * _sigmoid(cv)).astype(BF16)
        o_ref[...] = lax.dot_general(ca, d_ref[...].astype(BF16), TN, preferred_element_type=F32)

    return pl.pallas_call(
        body, name="ada_bwd", grid=(NA // tn,),
        in_specs=[pl.BlockSpec((B, D), lambda j: (0, 0)), pl.BlockSpec((B, tn), lambda j: (0, j))],
        out_specs=pl.BlockSpec((D, tn), lambda j: (0, j)),
        out_shape=jax.ShapeDtypeStruct((D, NA), F32),
        compiler_params=_params(("arbitrary",)),
    )(c_all, dmod)


def _sum8(parts):
    _, R, N = parts.shape

    def body(p_ref, o_ref):
        acc = p_ref[0]
        for d in range(1, 8):
            acc = acc + p_ref[d]
        o_ref[...] = acc

    return pl.pallas_call(body, name="sum8", out_shape=jax.ShapeDtypeStruct((R, N), F32),
                          compiler_params=_params())(parts)


def _adam_math(w, g, m, v):
    m = ADAM_B1 * m + (1.0 - ADAM_B1) * g
    v = ADAM_B2 * v + (1.0 - ADAM_B2) * (g * g)
    delta = -ADAM_LR * ((m / ADAM_C1) / (jnp.sqrt(v / ADAM_C2) + ADAM_EPS) + ADAM_WD * w)
    return delta, m, v


def _adam(name, w, m, v, g, carry=None):
    R, C = w.shape
    tm = _row_tile(R, C * 4, 1 << 20)
    steps = R // tm
    n_ci = len(carry.ins) if carry else 0
    n_co = len(carry.outs) if carry else 0

    def body(*refs):
        w_ref, m_ref, v_ref, g_ref = refs[:4]
        d_ref, nm_ref, nv_ref = refs[4 + n_ci:7 + n_ci]
        c_ins, c_outs, c_sems = refs[4:4 + n_ci], refs[7 + n_ci:7 + n_ci + n_co], refs[7 + n_ci + n_co:]
        if carry:
            @pl.when(pl.program_id(0) == 0)
            def _():
                carry.start(c_ins, c_outs, c_sems)

        delta, nm, nv = _adam_math(w_ref[...], g_ref[...], m_ref[...], v_ref[...])
        d_ref[...] = delta
        nm_ref[...] = nm
        nv_ref[...] = nv
        if carry:
            @pl.when(pl.program_id(0) == steps - 1)
            def _():
                carry.finish(c_ins, c_outs, c_sems)

    spec = pl.BlockSpec((tm, C), lambda i: (i, 0))
    res = pl.pallas_call(
        body, name=name, grid=(steps,), in_specs=[spec] * 4 + [ANY] * n_ci, out_specs=[spec] * 3 + [ANY] * n_co,
        out_shape=[jax.ShapeDtypeStruct((R, C), F32)] * 3 + (carry.outs if carry else []),
        scratch_shapes=carry.sems if carry else [],
        input_output_aliases=carry.io_aliases(4, 3) if carry else {},
        compiler_params=_params(("arbitrary",)),
    )(w, m, v, g, *(carry.ins if carry else []))
    return (res[:3], res[3:]) if carry else res


def _adam_halves(name, w, m, v, mine, other, core, carry=None):
    R, C = w.shape
    Rh = mine.shape[0]
    tc = max(t for t in range(128, C + 1, 128) if C % t == 0 and R * t <= (3 << 17))
    steps = C // tc
    n_ci = len(carry.ins) if carry else 0
    n_co = len(carry.outs) if carry else 0

    def body(*refs):
        c_ref, w_ref, m_ref, v_ref, a_ref, b_ref = refs[:6]
        g_ref, d_ref, nm_ref, nv_ref = refs[6 + n_ci:10 + n_ci]
        c_ins, c_outs, c_sems = refs[6:6 + n_ci], refs[10 + n_ci:10 + n_ci + n_co], refs[10 + n_ci + n_co:]
        if carry:
            @pl.when(pl.program_id(0) == 0)
            def _():
                carry.start(c_ins, c_outs, c_sems)

        first = c_ref[0] == 0
        g = jnp.concatenate([jnp.where(first, a_ref[...], b_ref[...]),
                             jnp.where(first, b_ref[0:R - Rh, :], a_ref[0:R - Rh, :])], axis=0)
        delta, nm, nv = _adam_math(w_ref[...], g, m_ref[...], v_ref[...])
        g_ref[...] = g
        d_ref[...] = delta
        nm_ref[...] = nm
        nv_ref[...] = nv
        if carry:
            @pl.when(pl.program_id(0) == steps - 1)
            def _():
                carry.finish(c_ins, c_outs, c_sems)

    spec = pl.BlockSpec((R, tc), lambda i, c_ref: (0, i))
    h_spec = pl.BlockSpec((Rh, tc), lambda i, c_ref: (0, i))
    res = pl.pallas_call(
        body, name=name, out_shape=[jax.ShapeDtypeStruct((R, C), F32)] * 4 + (carry.outs if carry else []),
        grid_spec=pltpu.PrefetchScalarGridSpec(
            num_scalar_prefetch=1, grid=(steps,), in_specs=[spec, spec, spec, h_spec, h_spec] + [ANY] * n_ci,
            out_specs=[spec] * 4 + [ANY] * n_co, scratch_shapes=carry.sems if carry else []),
        input_output_aliases=carry.io_aliases(6, 4) if carry else {},
        compiler_params=_params(("arbitrary",)),
    )(core, w, m, v, mine, other, *(carry.ins if carry else []))
    return (res[:4], res[4:]) if carry else res


def _adam_small(name, w, m, v, g):
    def body(w_ref, m_ref, v_ref, g_ref, d_ref, nm_ref, nv_ref):
        delta, nm, nv = _adam_math(w_ref[...], g_ref[...], m_ref[...], v_ref[...])
        d_ref[...] = delta
        nm_ref[...] = nm
        nv_ref[...] = nv

    return pl.pallas_call(body, name=name, out_shape=[jax.ShapeDtypeStruct(w.shape, F32)] * 3,
                          compiler_params=_params())(w, m, v, g)


def _place():
    return lax.axis_index("x"), lax.axis_index("y"), lax.axis_index("c")


def _other_chips(x, y):
    return [(1 - x, y), (x, 1 - y), (1 - x, 1 - y)]


def _all_gather8(blk, name):
    R, N = blk.shape

    def body(x_ref, out_ref, send_sems, recv_sems, local_sem):
        x, y, c = _place()
        me = 4 * x + 2 * y + c
        mine = pltpu.make_async_copy(x_ref, out_ref.at[me], local_sem)
        mine.start()
        flips = [(j >> 2 & 1, j >> 1 & 1, j & 1) for j in range(1, 8)]
        peers = [((1 - x) if fx else x, (1 - y) if fy else y, (1 - c) if fc else c) for fx, fy, fc in flips]
        sends = []
        for j, peer in enumerate(peers):
            cp = pltpu.make_async_remote_copy(src_ref=x_ref, dst_ref=out_ref.at[me], send_sem=send_sems.at[j],
                                              recv_sem=recv_sems.at[j], device_id=peer, device_id_type=MESH)
            cp.start()
            sends.append(cp)
        for j, (px, py, pc) in enumerate(peers):
            pltpu.make_async_remote_copy(src_ref=x_ref, dst_ref=out_ref.at[4 * px + 2 * py + pc],
                                         send_sem=send_sems.at[j], recv_sem=recv_sems.at[j],
                                         device_id=(px, py, pc), device_id_type=MESH).wait_recv()
        for cp in sends:
            cp.wait_send()
        mine.wait()

    return pl.pallas_call(
        body, name=name, out_shape=jax.ShapeDtypeStruct((8, R, N), F32),
        in_specs=[pl.BlockSpec(memory_space=pltpu.VMEM)], out_specs=pl.BlockSpec(memory_space=pltpu.VMEM),
        scratch_shapes=[pltpu.SemaphoreType.DMA((7,)), pltpu.SemaphoreType.DMA((7,)), pltpu.SemaphoreType.DMA],
        compiler_params=_params(),
    )(blk)


def _piece(rows, piece):
    i, n = piece
    assert rows % 16 == 0 and rows // 16 >= n, (rows, piece)
    lo, hi = (rows // 16 * i // n) * 16, (rows // 16 * (i + 1) // n) * 16
    return pl.ds(lo, hi - lo)


def _scatter_plan(arrs, piece=(0, 1), into=None):
    n = len(arrs)

    def copies(ins, outs, sems):
        send_sems, recv_sems = sems
        x, y, c = _place()
        chips = _other_chips(x, y)
        cps = []
        for k in range(n):
            rows = _piece(arrs[k].shape[1], piece)
            for j, (px, py) in enumerate(chips):
                cps.append(pltpu.make_async_remote_copy(
                    src_ref=ins[k].at[2 * px + py, rows], dst_ref=outs[k].at[j, rows],
                    send_sem=send_sems.at[3 * k + j], recv_sem=recv_sems.at[3 * k + j],
                    device_id=(px, py, c), device_id_type=MESH))
        return cps

    def start(ins, outs, sems):
        for cp in copies(ins, outs, sems):
            cp.start()

    def finish(ins, outs, sems):
        for cp in copies(ins, outs, sems):
            cp.wait()

    return _Plan(list(arrs) + list(into or []), [jax.ShapeDtypeStruct((3,) + a.shape[1:], a.dtype) for a in arrs],
                 [pltpu.SemaphoreType.DMA((3 * n,))] * 2, start, finish,
                 aliases={n + k: k for k in range(n)} if into else None)


def _gather_plan(shards, piece=(0, 1), into=None):
    n = len(shards)

    def parts(ins, outs, sems):
        s1, r1, s2, r2, loc = sems
        x, y, c = _place()
        me = 2 * x + y
        chips = _other_chips(x, y)
        sib = (x, y, 1 - c)

        def rows(k):
            return _piece(shards[k].shape[1], piece)

        def ici(k, j, slab, to):
            return pltpu.make_async_remote_copy(src_ref=ins[k].at[c, rows(k)], dst_ref=outs[k].at[slab, c, rows(k)],
                                                send_sem=s1.at[3 * k + j], recv_sem=r1.at[3 * k + j],
                                                device_id=to, device_id_type=MESH)

        def d2d(k, j, slab, half):
            return pltpu.make_async_remote_copy(src_ref=outs[k].at[slab, half, rows(k)],
                                                dst_ref=outs[k].at[slab, half, rows(k)],
                                                send_sem=s2.at[3 * k + j], recv_sem=r2.at[3 * k + j],
                                                device_id=sib, device_id_type=MESH)

        def own(k):
            return pltpu.make_async_remote_copy(src_ref=ins[k].at[:, rows(k)], dst_ref=outs[k].at[me, :, rows(k)],
                                                send_sem=loc.at[2 * k], recv_sem=loc.at[2 * k + 1],
                                                device_id=sib, device_id_type=MESH)

        return c, me, chips, ici, d2d, own

    def start(ins, outs, sems):
        c, me, chips, ici, d2d, own = parts(ins, outs, sems)
        for k in range(n):
            for j, (px, py) in enumerate(chips):
                ici(k, j, me, (px, py, c)).start()
        for k in range(n):
            own(k).start()

    def finish(ins, outs, sems):
        c, me, chips, ici, d2d, own = parts(ins, outs, sems)
        for k in range(n):
            for j, (px, py) in enumerate(chips):
                ici(k, j, 2 * px + py, (px, py, c)).wait_recv()
                d2d(k, j, 2 * px + py, c).start()
        for k in range(n):
            for j, (px, py) in enumerate(chips):
                d2d(k, j, 2 * px + py, 1 - c).wait_recv()
        for k in range(n):
            own(k).wait()
            for j, (px, py) in enumerate(chips):
                ici(k, j, me, (px, py, c)).wait_send()
                d2d(k, j, 2 * px + py, c).wait_send()

    return _Plan(list(shards) + list(into or []), [jax.ShapeDtypeStruct((4,) + a.shape, a.dtype) for a in shards],
                 [pltpu.SemaphoreType.DMA((3 * n,))] * 4 + [pltpu.SemaphoreType.DMA((2 * n,))], start, finish,
                 aliases={n + k: k for k in range(n)} if into else None)


def _pair_plan(parts):
    n = len(parts)

    def copies(ins, outs, sems):
        send_sems, recv_sems = sems
        x, y, c = _place()
        return [pltpu.make_async_remote_copy(src_ref=ins[k].at[p, 1 - c], dst_ref=outs[k].at[p],
                                             send_sem=send_sems.at[4 * k + p], recv_sem=recv_sems.at[4 * k + p],
                                             device_id=(x, y, 1 - c), device_id_type=MESH)
                for k in range(n) for p in range(4)]

    def start(ins, outs, sems):
        for cp in copies(ins, outs, sems):
            cp.start()

    def finish(ins, outs, sems):
        for cp in copies(ins, outs, sems):
            cp.wait()

    return _Plan(parts, [jax.ShapeDtypeStruct((4,) + a.shape[2:], a.dtype) for a in parts],
                 [pltpu.SemaphoreType.DMA((4 * n,))] * 2, start, finish)


def _sibling_plan(arrs):
    n = len(arrs)

    def copies(ins, outs, sems):
        send_sems, recv_sems = sems
        x, y, c = _place()
        return [pltpu.make_async_remote_copy(src_ref=ins[k], dst_ref=outs[k], send_sem=send_sems.at[k],
                                             recv_sem=recv_sems.at[k], device_id=(x, y, 1 - c), device_id_type=MESH)
                for k in range(n)]

    def start(ins, outs, sems):
        for cp in copies(ins, outs, sems):
            cp.start()

    def finish(ins, outs, sems):
        for cp in copies(ins, outs, sems):
            cp.wait()

    return _Plan(arrs, [jax.ShapeDtypeStruct(a.shape, a.dtype) for a in arrs],
                 [pltpu.SemaphoreType.DMA((n,))] * 2, start, finish)


def _join_plans(plans):
    def split(seq, counts):
        out, at = [], 0
        for cnt in counts:
            out.append(seq[at:at + cnt])
            at += cnt
        return out

    n_i, n_o, n_s = ([len(getattr(p, f)) for p in plans] for f in ("ins", "outs", "sems"))

    def start(ins, outs, sems):
        for p, i, o, s in zip(plans, split(ins, n_i), split(outs, n_o), split(sems, n_s)):
            p.start(i, o, s)

    def finish(ins, outs, sems):
        for p, i, o, s in zip(plans, split(ins, n_i), split(outs, n_o), split(sems, n_s)):
            p.finish(i, o, s)

    aliases, at_i, at_o = {}, 0, 0
    for p in plans:
        aliases.update(p.io_aliases(at_i, at_o))
        at_i, at_o = at_i + len(p.ins), at_o + len(p.outs)
    return _Plan(sum((p.ins for p in plans), []), sum((p.outs for p in plans), []), sum((p.sems for p in plans), []),
                 start, finish, aliases)


def _add_pair(parts, sib, core, name):
    P4, _, Rh, C = parts.shape
    tm, tc = _tile2(Rh, C, 16)

    def body(c_ref, a_ref, b_ref, o_ref):
        o_ref[...] = (a_ref[0].astype(F32) + b_ref[...].astype(F32)).astype(BF16)

    spec = pl.BlockSpec((1, tm, tc), lambda p, i, j, c_ref: (p, i, j))
    return pl.pallas_call(
        body, name=name, out_shape=jax.ShapeDtypeStruct((P4, Rh, C), BF16),
        grid_spec=pltpu.PrefetchScalarGridSpec(
            num_scalar_prefetch=1, grid=(P4, Rh // tm, C // tc),
            in_specs=[pl.BlockSpec((1, 1, tm, tc), lambda p, i, j, c_ref: (p, c_ref[0], i, j)), spec], out_specs=spec),
        compiler_params=_params(("parallel",) * 3),
    )(core, parts, sib)


def _sum_slabs(pre, recv, chip, name):
    _, Rh, C = pre.shape
    tm, tc = _tile2(Rh, C, 16)

    def body(me_ref, own_ref, r_ref, o_ref):
        acc = own_ref[0].astype(F32)
        for j in range(3):
            acc = acc + r_ref[j].astype(F32)
        o_ref[...] = acc

    return pl.pallas_call(
        body, name=name, out_shape=jax.ShapeDtypeStruct((Rh, C), F32),
        grid_spec=pltpu.PrefetchScalarGridSpec(
            num_scalar_prefetch=1, grid=(Rh // tm, C // tc),
            in_specs=[pl.BlockSpec((1, tm, tc), lambda i, j, me_ref: (me_ref[0], i, j)),
                      pl.BlockSpec((3, tm, tc), lambda i, j, me_ref: (0, i, j))],
            out_specs=pl.BlockSpec((tm, tc), lambda i, j, me_ref: (i, j))),
        compiler_params=_params(("parallel", "parallel")),
    )(chip, pre, recv)


def kernel(x, c, positions, w_ada, b_ada, w_in, g_q_a, w_q_b, g_kv_a, w_kv_b, w_o_a, w_conv, w_o_b, w_o, ln1_g, ln1_b, w_ffn_in, w_ffn_out, ln2_g, ln2_b, loss_target, m_w_ada, m_b_ada, m_w_in, m_g_q_a, m_w_q_b, m_g_kv_a, m_w_kv_b, m_w_o_a, m_w_conv, m_w_o_b, m_w_o, m_ln1_g, m_ln1_b, m_w_ffn_in, m_w_ffn_out, m_ln2_g, m_ln2_b, v_w_ada, v_b_ada, v_w_in, v_g_q_a, v_w_q_b, v_g_kv_a, v_w_kv_b, v_w_o_a, v_w_conv, v_w_o_b, v_w_o, v_ln1_g, v_ln1_b, v_w_ffn_in, v_w_ffn_out, v_ln2_g, v_ln2_b):
    S, D = x.shape[1], x.shape[2]
    F = w_ffn_out.shape[1] * 4
    ax, ay, ac = _place()
    chip = 2 * ax + ay
    dev = 4 * ax + 2 * ay + ac
    x2, tgt = x[0], loss_target[0]
    w_ada2, w_in2, w_q_b2, w_kv_b2 = w_ada[0], w_in[0], w_q_b[0], w_kv_b[0]
    w_o_a2, w_o_b2, w_o2, w_ffn_in2, w_ffn_out2 = w_o_a[0], w_o_b[0], w_o[0], w_ffn_in[0], w_ffn_out[0]
    NA = w_ada2.shape[1]
    CW = w_conv.shape[2]

    inv_freq = 1.0 / (ROPE_THETA ** (jnp.arange(0, QK_ROPE, 2, dtype=F32) / QK_ROPE))
    ang = positions[0].astype(F32)[:, None] * inv_freq
    cos, sin = jnp.cos(ang), jnp.sin(ang)
    z32, z64, z96 = jnp.zeros((S, 32), F32), jnp.zeros((S, 64), F32), jnp.zeros((S, 96), F32)
    tab = jnp.concatenate([cos, cos, z64, -sin, z96, z32, sin, z64], axis=1)

    def halves(a):
        return a.reshape(2, a.shape[0] // 2, a.shape[1])

    def whole(g):
        return g.reshape(4, 2 * g.shape[2], g.shape[3])

    def cols(g):
        return jnp.transpose(g, (1, 0, 2)).reshape(g.shape[1], 4 * g.shape[2])

    w_inT, m_w_inT, v_w_inT = w_in2.T, m_w_in[0].T, v_w_in[0].T
    CS = w_inT.shape[0]
    CSP = -(-CS // 32) * 32
    sh_in = halves(jnp.pad(w_inT.astype(BF16), ((0, CSP - CS), (0, 0))))
    sh_qb, sh_kvb, sh_oa, sh_ob, sh_o, sh_fi, sh_fo = (
        halves(w.astype(BF16)) for w in (w_q_b2, w_kv_b2, w_o_a2, w_o_b2, w_o2, w_ffn_in2, w_ffn_out2))
    g_in = whole(_run_plan(_gather_plan([sh_in]), "gather_first")[0])

    def in_rows(lo, hi):
        parts = [g_in[p, max(lo, p * CS) - p * CS:min(hi, (p + 1) * CS) - p * CS]
                 for p in range(4) if max(lo, p * CS) < min(hi, (p + 1) * CS)]
        return parts[0] if len(parts) == 1 else jnp.concatenate(parts, axis=0)

    n_qkv = Q_LORA + KV_LORA + QK_ROPE
    W_qkvT = jnp.pad(in_rows(0, n_qkv), ((0, QKV_A - n_qkv), (0, 0)))
    W_convT = in_rows(n_qkv, n_qkv + 3 * D)
    W_gateT = in_rows(n_qkv + 3 * D, n_qkv + 5 * D)

    c_all = _all_gather8(c, "gather_c").reshape(8, D)
    wconv_all = _all_gather8(w_conv[0], "gather_wconv")
    w_conv_full = jnp.transpose(wconv_all[0::2], (1, 0, 2)).reshape(3, D)
    b_sh = lax.dynamic_slice(b_ada, (0, chip * NA), (1, NA))
    mod_sh = _ada_fwd(c_all, w_ada2, b_sh)
    mod_all = _all_gather8(mod_sh, "gather_mod")
    mod = lax.dynamic_slice(mod_all[0::2], (0, dev, 0), (4, 1, NA)).reshape(6, D)
    shift1, scale1, gate1, shift2, scale2, gate2 = (mod[k:k + 1] for k in range(6))

    u = _modulate(x2, scale1, shift1, "modulate1")
    pq, (g_qb, g_kvb) = _matmul(u, W_qkvT, "nt", F32, "proj_qkv", carry=_gather_plan([sh_qb, sh_kvb]))
    W_qb = jnp.pad(cols(whole(g_qb)).reshape(Q_LORA, N_HEADS, QK_NOPE + QK_ROPE),
                   ((0, 0), (0, 0), (0, QK_PAD - QK_NOPE - QK_ROPE))).reshape(Q_LORA, N_HEADS * QK_PAD)
    W_kvb = cols(whole(g_kvb))
    pc, (g_oa, g_ob) = _matmul(u, W_convT, "nt", F32, "proj_conv", carry=_gather_plan([sh_oa, sh_ob]))
    pg, (g_o,) = _matmul(u, W_gateT, "nt", F32, "proj_gate", carry=_gather_plan([sh_o]))
    W_oa, W_ob, W_o = (g.reshape(-1, D) for g in (g_oa, g_ob, g_o))
    rq, rkv, kr = _rms_fwd(pq, tab, g_q_a, g_kv_a)
    q = _q_rope(_matmul(rq, W_qb, "nn", F32, "q_b"), tab)
    kv = _matmul(rkv, W_kvb, "nn", BF16, "kv_b")
    o, lse, g_fi = _attn_fwd(q, kv, kr, carry=_gather_plan([sh_fi], (0, 2)))
    y_a, g_fi = _matmul(o, W_oa, "nn", F32, "o_a", carry=_gather_plan([sh_fi], (2, 4), g_fi))
    hb = _conv_fwd(pc, w_conv_full)
    y_b = _matmul(hb, W_ob, "nn", F32, "o_b")
    merged = _merge_fwd(y_a, y_b, pg)
    mix, g_fi = _matmul(merged, W_o, "nn", F32, "w_o", carry=_gather_plan([sh_fi], (3, 4), g_fi))
    W_fi = whole(g_fi[0])
    x1, u2 = _ln1_fwd(x2, mix, gate1, ln1_g, ln1_b, scale2, shift2)
    hh, (g_fo,) = _matmul(u2, W_fi, "nn", F32, "ffn_in", carry=_gather_plan([sh_fo]), shards="b")
    W_fo = g_fo.reshape(F, D)
    act = _swiglu_fwd(hh)
    ffn = _matmul(act, W_fo, "nn", F32, "ffn_out")

    core_i = ac.astype(jnp.int32).reshape(1)
    chip_i = chip.astype(jnp.int32).reshape(1)

    def uncols(g):
        return jnp.transpose(g.reshape(g.shape[0], 4, g.shape[1] // 4), (1, 0, 2))

    def slabs(p):
        return p.reshape(4, 2, p.shape[1] // 2, p.shape[2])

    def add_pairs(parts, sibs, nms):
        return [_add_pair(a, b, core_i, "add_pair_" + nm) for a, b, nm in zip(parts, sibs, nms)]

    def sum_all(pre, recv, nms):
        return [_sum_slabs(a, r, chip_i, "sum_slabs_" + nm) for a, r, nm in zip(pre, recv, nms)]

    dffn, dx1a, loss_acc, d_ln2_g, d_ln2_b, d_gate2 = _ln2_loss_bwd(x1, ffn, gate2, ln2_g, ln2_b, tgt)
    loss = lax.psum(loss_acc[0, 0], ("x", "y", "c"))
    dW_fo = _matmul(act, dffn, "tn", BF16, "d_w_ffn_out")
    p_fo = [slabs(dW_fo.reshape(4, -1, D))]
    dact, s_fo = _matmul(dffn, W_fo, "nt", F32, "d_act", carry=_pair_plan(p_fo))
    pre_fo = add_pairs(p_fo, s_fo, ["w_ffn_out"])
    dhh = _swiglu_bwd(dact, hh)
    dW_fi, r_fo = _matmul(u2, dhh, "tn", BF16, "d_w_ffn_in", carry=_scatter_plan(pre_fo), shards="o")
    p_fi = [slabs(dW_fi)]
    du2, s_fi = _matmul(dhh, W_fi, "nt", F32, "d_u2", carry=_pair_plan(p_fi), shards="b")
    pre_fi = add_pairs(p_fi, s_fi, ["w_ffn_in"])
    dmix, dxa, d_shift2, d_scale2, d_ln1_g, d_ln1_b, d_gate1 = _ln1_bwd(x2, mix, dx1a, du2, gate1, ln1_g, ln1_b, scale2)
    dW_o = _matmul(merged, dmix, "tn", BF16, "d_w_o")
    dmerged = _matmul(dmix, W_o, "nt", F32, "d_merged")
    dy_a, dy_b, dgate = _merge_bwd(dmerged, y_a, y_b, pg)
    dW_oa = _matmul(o, dy_a, "tn", BF16, "d_w_o_a")
    do = _matmul(dy_a, W_oa, "nt", BF16, "d_o")
    dW_ob = _matmul(hb, dy_b, "tn", BF16, "d_w_o_b")
    p_mid = [slabs(g.reshape(4, -1, D)) for g in (dW_oa, dW_ob, dW_o)]
    dhb, s_mid = _matmul(dy_b, W_ob, "nt", F32, "d_hb", carry=_pair_plan(p_mid))
    pre_mid = add_pairs(p_mid, s_mid, ["w_o_a", "w_o_b", "w_o"])
    dconv, d_wconv = _conv_bwd(dhb, pc, w_conv_full)
    dq, dkv, dkr, r_fi = _attn_bwd(q, kv, kr, do, o, lse, tab, carry=_scatter_plan(pre_fi))
    names_a = ["w_ffn_out", "w_ffn_in", "w_o_a", "w_o_b", "w_o"]
    dW_qb = _matmul(rq, dq, "tn", BF16, "d_w_q_b")
    d_rq = _matmul(dq, W_qb, "nt", F32, "d_rq")
    dW_kvb = _matmul(rkv, dkv, "tn", BF16, "d_w_kv_b")
    d_rkv = _matmul(dkv, W_kvb, "nt", F32, "d_rkv")
    dqkv, d_g_q, d_g_kv = _rms_bwd(d_rq, d_rkv, pq, dkr, g_q_a, g_kv_a)
    dW_qkvT = _matmul(dqkv, u, "tn", BF16, "d_w_qkv")
    dW_convT, r_mid = _matmul(dconv, u, "tn", BF16, "d_w_conv", carry=_scatter_plan(pre_mid, (0, 2)))
    dW_gateT, r_mid = _matmul(dgate, u, "tn", BF16, "d_w_gate", carry=_scatter_plan(pre_mid, (1, 2), r_mid))
    fin_a = sum_all(pre_fo + pre_fi + pre_mid, list(r_fo) + list(r_fi) + list(r_mid), names_a)
    dW_inT = jnp.concatenate([dW_qkvT[:n_qkv], dW_convT, dW_gateT], axis=0).reshape(4, CS, D)
    dW_inT = jnp.pad(dW_inT, ((0, 0), (0, CSP - CS), (0, 0)))
    dW_qb_u = dW_qb.reshape(Q_LORA, N_HEADS, QK_PAD)[:, :, :QK_NOPE + QK_ROPE].reshape(Q_LORA, -1)
    names_b = ["w_in", "w_q_b", "w_kv_b"]
    p_b = [slabs(dW_inT), slabs(uncols(dW_qb_u)), slabs(uncols(dW_kvb))]
    du, s_b = _matmul(dqkv, W_qkvT, "nn", F32, "d_u_qkv", carry=_pair_plan(p_b))
    pre_b = add_pairs(p_b, s_b, names_b)
    du, moved = _matmul(dconv, W_convT, "nn", F32, "d_u_conv", add=du,
                        carry=_join_plans([_scatter_plan(pre_b, (0, 4)), _sibling_plan(fin_a)]))
    r_b, fs_a = moved[:3], moved[3:]
    du, r_b = _matmul(dgate, W_gateT, "nn", F32, "d_u_gate", add=du, carry=_scatter_plan(pre_b, (1, 4), r_b))
    grad_x, d_shift1, d_scale1 = _dx_final(dxa, du, x2, scale1)

    def pad_d(v):
        return jnp.pad(v, ((0, 0), (0, D - v.shape[1])))

    small = jnp.concatenate([d_ln1_g, d_ln1_b, d_ln2_g, d_ln2_b, pad_d(d_g_q), pad_d(d_g_kv), d_wconv,
                             d_shift1, d_scale1, d_gate1, d_shift2, d_scale2, d_gate2, jnp.zeros((1, D), F32)], axis=0)
    small_all = _all_gather8(small, "gather_small")
    small_sum = _sum8(small_all)
    g_ln1_g, g_ln1_b, g_ln2_g, g_ln2_b = (small_sum[k:k + 1] for k in range(4))
    g_g_q, g_g_kv = small_sum[4:5, :Q_LORA], small_sum[5:6, :KV_LORA]
    g_wconv = lax.dynamic_slice(small_sum[6:9], (0, chip * CW), (3, CW))
    g_b_ada = small_sum[9:15].reshape(1, 6 * D)
    dmod_all = small_all[:, 9:15, :].reshape(8, 6 * D)
    g_w_ada = _ada_bwd(c_all, lax.dynamic_slice(dmod_all, (0, chip * NA), (8, NA)))

    big = {}
    ws = dict(w_in=(w_inT, m_w_inT, v_w_inT), w_q_b=(w_q_b2, m_w_q_b[0], v_w_q_b[0]),
              w_kv_b=(w_kv_b2, m_w_kv_b[0], v_w_kv_b[0]), w_o_a=(w_o_a2, m_w_o_a[0], v_w_o_a[0]),
              w_o_b=(w_o_b2, m_w_o_b[0], v_w_o_b[0]), w_o=(w_o2, m_w_o[0], v_w_o[0]),
              w_ffn_in=(w_ffn_in2, m_w_ffn_in[0], v_w_ffn_in[0]), w_ffn_out=(w_ffn_out2, m_w_ffn_out[0], v_w_ffn_out[0]))
    def adam_of(nm, a, b, carry=None):
        w_, m_, v_ = ws[nm]
        return _adam_halves("adam_" + nm, w_, m_, v_, a, b, core_i, carry)

    big["w_ffn_in"], r_b = adam_of("w_ffn_in", fin_a[1], fs_a[1], _scatter_plan(pre_b, (2, 4), r_b))
    upd, r_b = _adam("adam_w_ada", w_ada2, m_w_ada[0], v_w_ada[0], g_w_ada, _scatter_plan(pre_b, (3, 4), r_b))
    big["w_ada"] = [g_w_ada] + list(upd)
    fin_b = sum_all(pre_b, r_b, names_b)
    fs_b = _run_plan(_sibling_plan(fin_b), "sibling_last")
    for nm, a, b in zip(names_a + names_b, fin_a + fin_b, list(fs_a) + list(fs_b)):
        if nm != "w_ffn_in":
            big[nm] = adam_of(nm, a, b)
    sm = {}
    for nm, w_, m_, v_, g_ in [("b_ada", b_ada, m_b_ada, v_b_ada, g_b_ada), ("g_q_a", g_q_a, m_g_q_a, v_g_q_a, g_g_q),
                               ("g_kv_a", g_kv_a, m_g_kv_a, v_g_kv_a, g_g_kv),
                               ("w_conv", w_conv[0], m_w_conv[0], v_w_conv[0], g_wconv),
                               ("ln1_g", ln1_g, m_ln1_g, v_ln1_g, g_ln1_g), ("ln1_b", ln1_b, m_ln1_b, v_ln1_b, g_ln1_b),
                               ("ln2_g", ln2_g, m_ln2_g, v_ln2_g, g_ln2_g), ("ln2_b", ln2_b, m_ln2_b, v_ln2_b, g_ln2_b)]:
        sm[nm] = (g_,) + tuple(_adam_small("adam_" + nm, w_, m_, v_, g_))

    order = ["w_ada", "b_ada", "w_in", "g_q_a", "w_q_b", "g_kv_a", "w_kv_b", "w_o_a", "w_conv", "w_o_b", "w_o",
             "ln1_g", "ln1_b", "w_ffn_in", "w_ffn_out", "ln2_g", "ln2_b"]
    lead = {"b_ada", "g_q_a", "g_kv_a", "ln1_g", "ln1_b", "ln2_g", "ln2_b"}

    def leaf(nm, k):
        val = big[nm][k] if nm in big else sm[nm][k]
        if nm == "w_in":
            val = val.T
        return val if nm in lead else val[None]

    outs = [loss, grad_x[None]]
    for k in range(4):
        outs += [leaf(nm, k) for nm in order]
    return tuple(outs)
```

```python
import functools

import jax
import jax.numpy as jnp
from jax import lax
from jax.experimental import pallas as pl
from jax.experimental.pallas import tpu as pltpu

F32, BF16 = jnp.float32, jnp.bfloat16
N_HEADS, QK_NOPE, QK_ROPE, V_HEAD = 16, 128, 64, 128
Q_LORA, KV_LORA = 512, 512
QK_PAD = 256
QKV_A = 1152
CHUNK_SHIFT = 6
ATTN_SCALE = (QK_NOPE + QK_ROPE) ** -0.5
ROPE_THETA = 10000.0
ALPHA = 2.0 ** 0.25
LN_EPS, RMS_EPS = 1e-5, 1e-6
ADAM_LR, ADAM_B1, ADAM_B2, ADAM_EPS, ADAM_WD, ADAM_STEP = 0.001, 0.9, 0.999, 1e-08, 0.01, 10
ADAM_C1 = 1.0 - ADAM_B1 ** ADAM_STEP
ADAM_C2 = 1.0 - ADAM_B2 ** ADAM_STEP
VMEM_LIMIT = 60 * 1024 * 1024
MESH = pl.DeviceIdType.MESH
ANY = pl.BlockSpec(memory_space=pl.ANY)
NT = (((1,), (1,)), ((), ()))
TN = (((0,), (0,)), ((), ()))
NN = (((1,), (0,)), ((), ()))


def _params(sem=None):
    return pltpu.CompilerParams(dimension_semantics=sem, vmem_limit_bytes=VMEM_LIMIT)


def _pick(n, cands=(1408, 1024, 512, 384, 256, 128)):
    for t in cands:
        if n % t == 0:
            return t
    return n


def _row_tile(rows, row_bytes, budget, mult=8):
    best = mult
    for t in range(mult, rows + 1, mult):
        if rows % t == 0 and t * row_bytes <= budget:
            best = t
    return best


def _tile2(rows, cols, mult=8, budget=1 << 18):
    col_tiles = [t for t in range(128, cols + 1, 128) if cols % t == 0] or [cols]
    best = None
    for tc in col_tiles:
        for tr in range(mult, rows + 1, mult):
            if rows % tr == 0 and tr * tc <= budget and (best is None or tr * tc > best[0] * best[1]):
                best = (tr, tc)
    assert best is not None, (rows, cols)
    return best


def _sigmoid(x):
    return jax.nn.sigmoid(x)


class _Plan:
    def __init__(self, ins, outs, sems, start, finish, aliases=None):
        self.ins, self.outs, self.sems, self.start, self.finish = list(ins), list(outs), list(sems), start, finish
        self.aliases = dict(aliases or {})

    def io_aliases(self, first_in, first_out):
        return {first_in + i: first_out + o for i, o in self.aliases.items()}


def _run_plan(plan, name):
    n_in, n_out = len(plan.ins), len(plan.outs)

    def body(*refs):
        ins, outs, sems = refs[:n_in], refs[n_in:n_in + n_out], refs[n_in + n_out:]
        plan.start(ins, outs, sems)
        plan.finish(ins, outs, sems)

    return pl.pallas_call(body, name=name, out_shape=plan.outs, in_specs=[ANY] * n_in, out_specs=[ANY] * n_out,
                          scratch_shapes=plan.sems, input_output_aliases=plan.io_aliases(0, 0),
                          compiler_params=_params())(*plan.ins)


def _matmul(a, b, mode, out_dtype, name, add=None, carry=None, shards=None):
    if mode == "nn":
        (M, K), N, dims = a.shape, b.shape[-1] * (4 if shards else 1), NN
    elif mode == "nt":
        (M, K), N, dims = a.shape, b.shape[-2], NT
    else:
        (K, M), N, dims = a.shape, b.shape[1], TN
    split_n = shards and mode != "nt"
    tm = _pick(M)
    tn = _pick(N // 4) if split_n else _pick(N)
    if shards and mode == "nt":
        tk = _pick(K // 4)
    else:
        tk = K if K <= 2048 else _pick(K)
    nk = K // tk
    if nk == 1 and M <= 2048 and M % 8 == 0 and (M * tk + tk * tn) * 4 + M * tn * (8 + 8 * (add is not None)) <= (50 << 20):
        tm = M
    per = (N // 4 // tn) if split_n else (K // 4 // tk if shards else 1)
    a_spec = (pl.BlockSpec((tk, tm), lambda i, j, k: (k, i)) if mode == "tn"
              else pl.BlockSpec((tm, tk), lambda i, j, k: (i, k)))
    if shards == "b" and mode == "nn":
        b_spec = pl.BlockSpec((None, tk, tn), lambda i, j, k: (j // per, k, j % per))
    elif shards == "b":
        b_spec = pl.BlockSpec((None, tn, tk), lambda i, j, k: (k // per, j, k % per))
    else:
        b_spec = (pl.BlockSpec((tn, tk), lambda i, j, k: (j, k)) if mode == "nt"
                  else pl.BlockSpec((tk, tn), lambda i, j, k: (k, j)))
    o_spec = pl.BlockSpec((tm, tn), lambda i, j, k: (i, j))
    o_shape = (M, N)
    if shards == "o":
        o_spec, o_shape = pl.BlockSpec((None, tm, tn), lambda i, j, k: (j // per, i, j % per)), (4, M, N // 4)
    has_add = add is not None
    n_ci = len(carry.ins) if carry else 0
    n_co = len(carry.outs) if carry else 0
    n_in = 2 + has_add
    grid = (M // tm, N // tn, nk)

    def body(*refs):
        a_ref, b_ref = refs[0], refs[1]
        add_ref = refs[2] if has_add else None
        o_ref = refs[n_in + n_ci]
        acc_ref = refs[n_in + n_ci + 1 + n_co] if nk > 1 else None
        c_ins = refs[n_in:n_in + n_ci]
        c_outs = refs[n_in + n_ci + 1:n_in + n_ci + 1 + n_co]
        c_sems = refs[n_in + n_ci + 1 + n_co + (nk > 1):]
        i, j, k = pl.program_id(0), pl.program_id(1), pl.program_id(2)

        if carry:
            @pl.when((i == 0) & (j == 0) & (k == 0))
            def _():
                carry.start(c_ins, c_outs, c_sems)

        part = lax.dot_general(a_ref[...], b_ref[...], dims, preferred_element_type=F32)
        if nk == 1:
            o_ref[...] = (part + add_ref[...] if has_add else part).astype(o_ref.dtype)
        else:
            @pl.when(k == 0)
            def _():
                acc_ref[...] = part

            @pl.when((k > 0) & (k < nk - 1))
            def _():
                acc_ref[...] += part

            @pl.when(k == nk - 1)
            def _():
                r = acc_ref[...] + part
                if has_add:
                    r = r + add_ref[...]
                o_ref[...] = r.astype(o_ref.dtype)

        if carry:
            @pl.when((i == grid[0] - 1) & (j == grid[1] - 1) & (k == nk - 1))
            def _():
                carry.finish(c_ins, c_outs, c_sems)

    ins = [a, b] + ([add] if has_add else []) + (carry.ins if carry else [])
    in_specs = [a_spec, b_spec] + ([o_spec] if has_add else []) + [ANY] * n_ci
    res = pl.pallas_call(
        body, name=name, grid=grid,
        in_specs=in_specs, out_specs=[o_spec] + [ANY] * n_co,
        out_shape=[jax.ShapeDtypeStruct(o_shape, out_dtype)] + (carry.outs if carry else []),
        scratch_shapes=([pltpu.VMEM((tm, tn), F32)] if nk > 1 else []) + (carry.sems if carry else []),
        input_output_aliases=carry.io_aliases(n_in, 1) if carry else {},
        compiler_params=_params(("arbitrary",) * 3 if carry else ("parallel", "parallel", "arbitrary")),
    )(*ins)
    return (res[0], res[1:]) if carry else res[0]


def _rows(body, name, n_rows, tm, ins, outs, accs=()):
    grid = (n_rows // tm,)
    per8 = tm // 8
    last8 = n_rows // 8 - 1
    arrays, in_specs = [], []
    for spec in ins:
        kind, arr = spec[0], spec[1]
        arrays.append(arr)
        if kind == "row":
            _, _, cb, w = spec
            in_specs.append(pl.BlockSpec((tm, w), lambda i, cb=cb: (i, cb)))
        elif kind == "full":
            in_specs.append(pl.BlockSpec(arr.shape, lambda i, nd=arr.ndim: (0,) * nd))
        elif kind == "prev":
            _, _, cb, w = spec
            in_specs.append(pl.BlockSpec((8, w), lambda i, cb=cb: (jnp.maximum(i * per8 - 1, 0), cb)))
        else:
            _, _, cb, w = spec
            in_specs.append(pl.BlockSpec((8, w), lambda i, cb=cb: (jnp.minimum((i + 1) * per8, last8), cb)))
    out_shape = [jax.ShapeDtypeStruct((n_rows, w), dt) for (w, dt) in outs]
    out_specs = [pl.BlockSpec((tm, w), lambda i: (i, 0)) for (w, _) in outs]
    out_shape += [jax.ShapeDtypeStruct(s, F32) for s in accs]
    out_specs += [pl.BlockSpec(s, lambda i, nd=len(s): (0,) * nd) for s in accs]
    n_in, n_out = len(ins), len(outs)

    def kernel_body(*refs):
        body(pl.program_id(0), refs[:n_in], refs[n_in:n_in + n_out], refs[n_in + n_out:])

    res = pl.pallas_call(
        kernel_body, name=name, grid=grid, in_specs=in_specs, out_specs=out_specs, out_shape=out_shape,
        compiler_params=_params(("arbitrary",)),
    )(*arrays)
    return res


def _acc_add(i, ref, val):
    @pl.when(i == 0)
    def _():
        ref[...] = val

    @pl.when(i > 0)
    def _():
        ref[...] += val


def _rope(t, tab, sign):
    c, sa, sb = tab[:, 0:128], tab[:, 128:256], tab[:, 256:384]
    rot = pltpu.roll(t, 96, 1) * sa + pltpu.roll(t, 32, 1) * sb
    return t * c + rot if sign > 0 else t * c - rot


def _ln_stats(r):
    mu = jnp.mean(r, axis=-1, keepdims=True)
    d = r - mu
    var = jnp.mean(d * d, axis=-1, keepdims=True)
    rstd = lax.rsqrt(var + LN_EPS)
    return d * rstd, rstd


def _ln_bwd(dxh, xh, rstd):
    m1 = jnp.mean(dxh, axis=-1, keepdims=True)
    m2 = jnp.mean(dxh * xh, axis=-1, keepdims=True)
    return rstd * (dxh - m1 - xh * m2)


def _modulate(x, scale, shift, name):
    S, D = x.shape

    def body(i, ins, outs, accs):
        outs[0][...] = (ins[0][...] * (1.0 + ins[1][...]) + ins[2][...]).astype(BF16)

    return _rows(body, name, S, _pick(S, (256, 128)), [("row", x, 0, D), ("full", scale), ("full", shift)], [(D, BF16)])[0]


def _rms_fwd(pq, tab, g_q, g_kv):
    S = pq.shape[0]

    def body(i, ins, outs, accs):
        pq_ref, tab_ref, gq_ref, gkv_ref = ins

        def rms(x, g):
            return x * lax.rsqrt(jnp.mean(x * x, axis=-1, keepdims=True) + RMS_EPS) * g

        outs[0][...] = rms(pq_ref[:, 0:Q_LORA], gq_ref[...]).astype(BF16)
        outs[1][...] = rms(pq_ref[:, Q_LORA:Q_LORA + KV_LORA], gkv_ref[...]).astype(BF16)
        outs[2][...] = _rope(pq_ref[:, Q_LORA + KV_LORA:QKV_A], tab_ref[...], 1).astype(BF16)

    return _rows(body, "rms_fwd", S, _pick(S, (256, 128)),
                 [("row", pq, 0, QKV_A), ("row", tab, 0, 384), ("full", g_q), ("full", g_kv)],
                 [(Q_LORA, BF16), (KV_LORA, BF16), (128, BF16)])


def _q_rope(q, tab):
    S, W = q.shape

    def body(i, ins, outs, accs):
        q_ref, tab_ref = ins
        t = tab_ref[...]
        for h in range(N_HEADS):
            lo = h * QK_PAD
            outs[0][:, lo:lo + 128] = q_ref[:, lo:lo + 128].astype(BF16)
            outs[0][:, lo + 128:lo + 256] = _rope(q_ref[:, lo + 128:lo + 256], t, 1).astype(BF16)

    return _rows(body, "q_rope", S, _pick(S, (256, 128)), [("row", q, 0, W), ("row", tab, 0, 384)], [(W, BF16)])[0]


def _allowed(q0, k0, bq):
    row = q0 + lax.broadcasted_iota(jnp.int32, (bq, bq), 0)
    col = k0 + lax.broadcasted_iota(jnp.int32, (bq, bq), 1)
    return (col >> CHUNK_SHIFT) <= (row >> CHUNK_SHIFT)


ATTN_BLOCK = 512


def _attn_fwd(q, kv, kr, carry=None):
    S = q.shape[0]
    bq = min(ATTN_BLOCK, S)
    nq = S // bq
    n_ci = len(carry.ins) if carry else 0
    n_co = len(carry.outs) if carry else 0

    def body(*refs):
        q_ref, kn_ref, v_ref, kr_ref = refs[:4]
        o_ref, lse_ref = refs[4 + n_ci:6 + n_ci]
        c_ins, c_outs = refs[4:4 + n_ci], refs[6 + n_ci:6 + n_ci + n_co]
        kcat = refs[6 + n_ci + n_co]
        c_sems = refs[7 + n_ci + n_co:]
        qi = pl.program_id(1)
        if carry:
            @pl.when((pl.program_id(0) == 0) & (qi == 0))
            def _():
                carry.start(c_ins, c_outs, c_sems)

        @pl.when(qi == 0)
        def _():
            kcat[:, 0:128] = kn_ref[...]
            kcat[:, 128:256] = kr_ref[...]

        qv = q_ref[...]

        def step(j, carry, masked):
            m, l, acc = carry
            off = pl.multiple_of(j * bq, bq)
            s = lax.dot_general(qv, kcat[pl.ds(off, bq), :], NT, preferred_element_type=F32) * ATTN_SCALE
            if masked:
                s = jnp.where(_allowed(qi * bq, off, bq), s, -1e30)
            m_new = jnp.maximum(m, jnp.max(s, axis=1, keepdims=True))
            a = jnp.exp(m - m_new)
            p = jnp.exp(s - m_new)
            l = a * l + jnp.sum(p, axis=1, keepdims=True)
            acc = a * acc + jnp.dot(p.astype(BF16), v_ref[pl.ds(off, bq), :], preferred_element_type=F32)
            return m_new, l, acc

        init = (jnp.full((bq, 1), -1e30, F32), jnp.zeros((bq, 1), F32), jnp.zeros((bq, V_HEAD), F32))
        below = lax.fori_loop(0, qi, lambda j, cr: step(j, cr, False), init)
        m, l, acc = step(qi, below, True)
        o_ref[...] = (acc / l).astype(BF16)
        lse_ref[0] = m + jnp.log(l)
        if carry:
            @pl.when((pl.program_id(0) == N_HEADS - 1) & (qi == nq - 1))
            def _():
                carry.finish(c_ins, c_outs, c_sems)

    res = pl.pallas_call(
        body, name="attn_fwd", grid=(N_HEADS, nq),
        in_specs=[pl.BlockSpec((bq, QK_PAD), lambda h, i: (i, h)),
                  pl.BlockSpec((S, 128), lambda h, i: (0, 2 * h)),
                  pl.BlockSpec((S, 128), lambda h, i: (0, 2 * h + 1)),
                  pl.BlockSpec((S, 128), lambda h, i: (0, 0))] + [ANY] * n_ci,
        out_specs=[pl.BlockSpec((bq, V_HEAD), lambda h, i: (i, h)),
                   pl.BlockSpec((1, bq, 1), lambda h, i: (h, i, 0))] + [ANY] * n_co,
        out_shape=[jax.ShapeDtypeStruct((S, N_HEADS * V_HEAD), BF16),
                   jax.ShapeDtypeStruct((N_HEADS, S, 1), F32)] + (carry.outs if carry else []),
        scratch_shapes=[pltpu.VMEM((S, QK_PAD), BF16)] + (carry.sems if carry else []),
        input_output_aliases=carry.io_aliases(4, 2) if carry else {},
        compiler_params=_params(("arbitrary", "arbitrary")),
    )(q, kv, kv, kr, *(carry.ins if carry else []))
    return res[0], res[1], res[2:]


def _attn_bwd(q, kv, kr, do, o, lse, tab, carry=None):
    S = q.shape[0]
    bq = min(ATTN_BLOCK, S)
    nq = S // bq

    n_ci = len(carry.ins) if carry else 0
    n_co = len(carry.outs) if carry else 0

    def body(*refs):
        q_ref, kn_ref, v_ref, kr_ref, do_ref, o_ref, lse_ref, tab_ref = refs[:8]
        dq_ref, dkv_ref, dkr_ref = refs[8 + n_ci:11 + n_ci]
        dq_acc, dk_acc, dv_acc, kcat, delta = refs[11 + n_ci + n_co:16 + n_ci + n_co]
        c_ins, c_outs, c_sems = refs[8:8 + n_ci], refs[11 + n_ci:11 + n_ci + n_co], refs[16 + n_ci + n_co:]
        h = pl.program_id(0)
        if carry:
            @pl.when(h == 0)
            def _():
                carry.start(c_ins, c_outs, c_sems)

        dq_acc[...] = jnp.zeros_like(dq_acc)
        dk_acc[...] = jnp.zeros_like(dk_acc)
        dv_acc[...] = jnp.zeros_like(dv_acc)
        kcat[:, 0:128] = kn_ref[...]
        kcat[:, 128:256] = kr_ref[...]
        for r in range(nq):
            rows = slice(r * bq, (r + 1) * bq)
            delta[rows, :] = jnp.sum(do_ref[rows, :].astype(F32) * o_ref[rows, :].astype(F32), axis=1, keepdims=True)

        def pair(i, j, masked):
            rows_i = pl.ds(pl.multiple_of(i * bq, bq), bq)
            rows_j = pl.ds(pl.multiple_of(j * bq, bq), bq)
            qv, dov, k = q_ref[rows_i, :], do_ref[rows_i, :], kcat[rows_j, :]
            s = lax.dot_general(qv, k, NT, preferred_element_type=F32) * ATTN_SCALE
            if masked:
                s = jnp.where(_allowed(i * bq, j * bq, bq), s, -1e30)
            p = jnp.exp(s - lse_ref[0, rows_i, :])
            dv_acc[rows_j, :] += lax.dot_general(p.astype(BF16), dov, TN, preferred_element_type=F32)
            dp = lax.dot_general(dov, v_ref[rows_j, :], NT, preferred_element_type=F32)
            ds = (p * (dp - delta[rows_i, :]) * ATTN_SCALE).astype(BF16)
            dk_acc[rows_j, :] += lax.dot_general(ds, qv, TN, preferred_element_type=F32)
            dq_acc[rows_i, :] += jnp.dot(ds, k, preferred_element_type=F32)

        def kv_step(j, _):
            pair(j, j, True)

            def q_step(i, _):
                pair(i, j, False)
                return 0

            lax.fori_loop(j + 1, nq, q_step, 0)
            return 0

        lax.fori_loop(0, nq, kv_step, 0)

        for r in range(nq):
            rows = slice(r * bq, (r + 1) * bq)
            dq_ref[rows, 0:128] = dq_acc[rows, 0:128].astype(BF16)
            dq_ref[rows, 128:256] = _rope(dq_acc[rows, 128:256], tab_ref[rows, :], -1).astype(BF16)
        dkv_ref[:, 0:128] = dk_acc[:, 0:128].astype(BF16)
        dkv_ref[:, 128:256] = dv_acc[...].astype(BF16)

        @pl.when(h == 0)
        def _():
            dkr_ref[...] = dk_acc[:, 128:256]

        @pl.when(h > 0)
        def _():
            dkr_ref[...] += dk_acc[:, 128:256]

        @pl.when(h == N_HEADS - 1)
        def _():
            for r in range(nq):
                rows = slice(r * bq, (r + 1) * bq)
                dkr_ref[rows, :] = _rope(dkr_ref[rows, :], tab_ref[rows, :], -1)
            if carry:
                carry.finish(c_ins, c_outs, c_sems)

    W = N_HEADS * QK_PAD
    res = pl.pallas_call(
        body, name="attn_bwd", grid=(N_HEADS,),
        in_specs=[pl.BlockSpec((S, QK_PAD), lambda h: (0, h)),
                  pl.BlockSpec((S, 128), lambda h: (0, 2 * h)),
                  pl.BlockSpec((S, 128), lambda h: (0, 2 * h + 1)),
                  pl.BlockSpec((S, 128), lambda h: (0, 0)),
                  pl.BlockSpec((S, V_HEAD), lambda h: (0, h)),
                  pl.BlockSpec((S, V_HEAD), lambda h: (0, h)),
                  pl.BlockSpec((1, S, 1), lambda h: (h, 0, 0)),
                  pl.BlockSpec((S, 384), lambda h: (0, 0))] + [ANY] * n_ci,
        out_specs=[pl.BlockSpec((S, QK_PAD), lambda h: (0, h)),
                   pl.BlockSpec((S, QK_PAD), lambda h: (0, h)),
                   pl.BlockSpec((S, 128), lambda h: (0, 0))] + [ANY] * n_co,
        out_shape=[jax.ShapeDtypeStruct((S, W), BF16), jax.ShapeDtypeStruct((S, W), BF16),
                   jax.ShapeDtypeStruct((S, 128), F32)] + (carry.outs if carry else []),
        scratch_shapes=[pltpu.VMEM((S, QK_PAD), F32), pltpu.VMEM((S, QK_PAD), F32), pltpu.VMEM((S, V_HEAD), F32),
                        pltpu.VMEM((S, QK_PAD), BF16), pltpu.VMEM((S, 1), F32)]
        + (carry.sems if carry else []),
        input_output_aliases=carry.io_aliases(8, 3) if carry else {},
        compiler_params=_params(("arbitrary",)),
    )(q, kv, kv, kr, do, o, lse, tab, *(carry.ins if carry else []))
    return res[0], res[1], res[2], res[3:]


def _shift_down(cur, prev8, i, n):
    tm = cur.shape[0]
    prev8 = jnp.where(i == 0, jnp.zeros_like(prev8), prev8)
    full = jnp.concatenate([prev8, cur], axis=0)
    return pltpu.roll(full, n, 0)[8:8 + tm, :]


def _shift_up(cur, next8, i, last, n):
    tm = cur.shape[0]
    next8 = jnp.where(i == last, jnp.zeros_like(next8), next8)
    full = jnp.concatenate([cur, next8], axis=0)
    return pltpu.roll(full, tm + 8 - n, 0)[0:tm, :]


def _conv_fwd(pc, w_conv):
    S, D = pc.shape[0], pc.shape[1] // 3
    tm = _pick(S, (256, 128))

    def body(i, ins, outs, accs):
        b_ref, c_ref, x_ref, cp_ref, xp_ref, w_ref = ins
        z = c_ref[...] * x_ref[...]
        zp = cp_ref[...] * xp_ref[...]
        cz = w_ref[0:1, :] * _shift_down(z, zp, i, 2) + w_ref[1:2, :] * _shift_down(z, zp, i, 1) + w_ref[2:3, :] * z
        outs[0][...] = (b_ref[...] * cz).astype(BF16)

    return _rows(body, "conv_fwd", S, tm,
                 [("row", pc, 0, D), ("row", pc, 1, D), ("row", pc, 2, D), ("prev", pc, 1, D), ("prev", pc, 2, D),
                  ("full", w_conv)], [(D, BF16)])[0]


def _conv_bwd(dhb, pc, w_conv):
    S, D = dhb.shape
    tm = _pick(S, (256, 128))
    last = S // tm - 1

    def body(i, ins, outs, accs):
        g_ref, b_ref, c_ref, x_ref, cp_ref, xp_ref, gn_ref, bn_ref, w_ref = ins
        w0, w1, w2 = w_ref[0:1, :], w_ref[1:2, :], w_ref[2:3, :]
        c, x, g = c_ref[...], x_ref[...], g_ref[...]
        z = c * x
        zp = cp_ref[...] * xp_ref[...]
        z1, z2 = _shift_down(z, zp, i, 1), _shift_down(z, zp, i, 2)
        cz = w0 * z2 + w1 * z1 + w2 * z
        dcz = g * b_ref[...]
        dczn = gn_ref[...] * bn_ref[...]
        dz = w2 * dcz + w1 * _shift_up(dcz, dczn, i, last, 1) + w0 * _shift_up(dcz, dczn, i, last, 2)
        outs[0][:, 0:D] = (g * cz).astype(BF16)
        outs[0][:, D:2 * D] = (dz * x).astype(BF16)
        outs[0][:, 2 * D:3 * D] = (dz * c).astype(BF16)
        dw = jnp.concatenate([jnp.sum(dcz * z2, axis=0, keepdims=True), jnp.sum(dcz * z1, axis=0, keepdims=True),
                              jnp.sum(dcz * z, axis=0, keepdims=True)], axis=0)
        _acc_add(i, accs[0], dw)

    return _rows(body, "conv_bwd", S, tm,
                 [("row", dhb, 0, D), ("row", pc, 0, D), ("row", pc, 1, D), ("row", pc, 2, D),
                  ("prev", pc, 1, D), ("prev", pc, 2, D), ("next", dhb, 0, D), ("next", pc, 0, D), ("full", w_conv)],
                 [(3 * D, BF16)], [(3, D)])


def _merge_fwd(y_a, y_b, pg):
    S, D = y_a.shape

    def body(i, ins, outs, accs):
        ya, yb, ga, gb = ins
        outs[0][...] = (_sigmoid(ga[...]) * ya[...] + _sigmoid(gb[...]) * yb[...]).astype(BF16)

    return _rows(body, "merge_fwd", S, _pick(S, (256, 128)),
                 [("row", y_a, 0, D), ("row", y_b, 0, D), ("row", pg, 0, D), ("row", pg, 1, D)], [(D, BF16)])[0]


def _merge_bwd(dm, y_a, y_b, pg):
    S, D = dm.shape

    def body(i, ins, outs, accs):
        d, ya, yb = ins[0][...], ins[1][...], ins[2][...]
        sa, sb = _sigmoid(ins[3][...]), _sigmoid(ins[4][...])
        outs[0][...] = (d * sa).astype(BF16)
        outs[1][...] = (d * sb).astype(BF16)
        outs[2][:, 0:D] = (d * ya * (sa * (1.0 - sa))).astype(BF16)
        outs[2][:, D:2 * D] = (d * yb * (sb * (1.0 - sb))).astype(BF16)

    return _rows(body, "merge_bwd", S, _pick(S, (256, 128)),
                 [("row", dm, 0, D), ("row", y_a, 0, D), ("row", y_b, 0, D), ("row", pg, 0, D), ("row", pg, 1, D)],
                 [(D, BF16), (D, BF16), (2 * D, BF16)])


def _ln1_fwd(x, mix, gate1, g, b, scale2, shift2):
    S, D = x.shape

    def body(i, ins, outs, accs):
        x_ref, mix_ref, gate_ref, g_ref, b_ref, sc_ref, sh_ref = ins
        xh, _ = _ln_stats(ALPHA * x_ref[...] + gate_ref[...] * mix_ref[...])
        x1 = xh * g_ref[...] + b_ref[...]
        outs[0][...] = x1
        outs[1][...] = (x1 * (1.0 + sc_ref[...]) + sh_ref[...]).astype(BF16)

    return _rows(body, "ln1_fwd", S, _pick(S, (256, 128)),
                 [("row", x, 0, D), ("row", mix, 0, D), ("full", gate1), ("full", g), ("full", b),
                  ("full", scale2), ("full", shift2)], [(D, F32), (D, BF16)])


def _swiglu_fwd(hh):
    S, F = hh.shape[0], hh.shape[1] // 2

    def body(i, ins, outs, accs):
        hg = ins[0][...]
        outs[0][...] = (hg * _sigmoid(hg) * ins[1][...]).astype(BF16)

    return _rows(body, "swiglu_fwd", S, _pick(S, (128,)), [("row", hh, 0, F), ("row", hh, 1, F)], [(F, BF16)])[0]


def _swiglu_bwd(dact, hh):
    S, F = dact.shape

    def body(i, ins, outs, accs):
        d, hg, hu = ins[0][...], ins[1][...], ins[2][...]
        sg = _sigmoid(hg)
        outs[0][:, 0:F] = (d * hu * (sg * (1.0 + hg * (1.0 - sg)))).astype(BF16)
        outs[0][:, F:2 * F] = (d * (hg * sg)).astype(BF16)

    return _rows(body, "swiglu_bwd", S, _pick(S, (128,)),
                 [("row", dact, 0, F), ("row", hh, 0, F), ("row", hh, 1, F)], [(2 * F, BF16)])[0]


def _ln2_loss_bwd(x1, ffn, gate2, g, b, target):
    S, D = x1.shape

    def body(i, ins, outs, accs):
        x1_ref, f_ref, gate_ref, g_ref, b_ref, t_ref = ins
        f = f_ref[...]
        xh, rstd = _ln_stats(ALPHA * x1_ref[...] + gate_ref[...] * f)
        e = xh * g_ref[...] + b_ref[...] - t_ref[...]
        dy = e * (1.0 / D)
        dr = _ln_bwd(dy * g_ref[...], xh, rstd)
        outs[0][...] = (gate_ref[...] * dr).astype(BF16)
        outs[1][...] = ALPHA * dr
        _acc_add(i, accs[0], jnp.full((1, 128), (0.5 / D) * jnp.sum(e * e), F32))
        _acc_add(i, accs[1], jnp.sum(dy * xh, axis=0, keepdims=True))
        _acc_add(i, accs[2], jnp.sum(dy, axis=0, keepdims=True))
        _acc_add(i, accs[3], jnp.sum(dr * f, axis=0, keepdims=True))

    return _rows(body, "ln2_loss_bwd", S, _pick(S, (256, 128)),
                 [("row", x1, 0, D), ("row", ffn, 0, D), ("full", gate2), ("full", g), ("full", b), ("row", target, 0, D)],
                 [(D, BF16), (D, F32)], [(1, 128), (1, D), (1, D), (1, D)])


def _ln1_bwd(x, mix, dx1a, du2, gate1, g, b, scale2):
    S, D = x.shape

    def body(i, ins, outs, accs):
        x_ref, mix_ref, da_ref, du_ref, gate_ref, g_ref, b_ref, sc_ref = ins
        mix, du = mix_ref[...], du_ref[...]
        xh, rstd = _ln_stats(ALPHA * x_ref[...] + gate_ref[...] * mix)
        x1 = xh * g_ref[...] + b_ref[...]
        dx1 = da_ref[...] + du * (1.0 + sc_ref[...])
        dr = _ln_bwd(dx1 * g_ref[...], xh, rstd)
        outs[0][...] = (gate_ref[...] * dr).astype(BF16)
        outs[1][...] = ALPHA * dr
        _acc_add(i, accs[0], jnp.sum(du, axis=0, keepdims=True))
        _acc_add(i, accs[1], jnp.sum(du * x1, axis=0, keepdims=True))
        _acc_add(i, accs[2], jnp.sum(dx1 * xh, axis=0, keepdims=True))
        _acc_add(i, accs[3], jnp.sum(dx1, axis=0, keepdims=True))
        _acc_add(i, accs[4], jnp.sum(dr * mix, axis=0, keepdims=True))

    return _rows(body, "ln1_bwd", S, _pick(S, (256, 128)),
                 [("row", x, 0, D), ("row", mix, 0, D), ("row", dx1a, 0, D), ("row", du2, 0, D),
                  ("full", gate1), ("full", g), ("full", b), ("full", scale2)],
                 [(D, BF16), (D, F32)], [(1, D)] * 5)


def _rms_bwd(d_rq, d_rkv, pq, dkr, g_q, g_kv):
    S = pq.shape[0]

    def body(i, ins, outs, accs):
        dq_ref, dkv_ref, pq_ref, dkr_ref, gq_ref, gkv_ref = ins

        def rms_bwd(dy, x, g):
            r = lax.rsqrt(jnp.mean(x * x, axis=-1, keepdims=True) + RMS_EPS)
            dyg = dy * g
            dx = r * dyg - x * (r * r * r) * jnp.mean(dyg * x, axis=-1, keepdims=True)
            return dx, jnp.sum(dy * (x * r), axis=0, keepdims=True)

        dxq, dgq = rms_bwd(dq_ref[...], pq_ref[:, 0:Q_LORA], gq_ref[...])
        dxkv, dgkv = rms_bwd(dkv_ref[...], pq_ref[:, Q_LORA:Q_LORA + KV_LORA], gkv_ref[...])
        outs[0][:, 0:Q_LORA] = dxq.astype(BF16)
        outs[0][:, Q_LORA:Q_LORA + KV_LORA] = dxkv.astype(BF16)
        outs[0][:, Q_LORA + KV_LORA:QKV_A] = dkr_ref[...].astype(BF16)
        _acc_add(i, accs[0], dgq)
        _acc_add(i, accs[1], dgkv)

    return _rows(body, "rms_bwd", S, _pick(S, (256, 128)),
                 [("row", d_rq, 0, Q_LORA), ("row", d_rkv, 0, KV_LORA), ("row", pq, 0, QKV_A), ("row", dkr, 0, 128),
                  ("full", g_q), ("full", g_kv)], [(QKV_A, BF16)], [(1, Q_LORA), (1, KV_LORA)])


def _dx_final(dxa, du, x, scale1):
    S, D = x.shape

    def body(i, ins, outs, accs):
        du = ins[1][...]
        outs[0][...] = ins[0][...] + du * (1.0 + ins[3][...])
        _acc_add(i, accs[0], jnp.sum(du, axis=0, keepdims=True))
        _acc_add(i, accs[1], jnp.sum(du * ins[2][...], axis=0, keepdims=True))

    return _rows(body, "dx_final", S, _pick(S, (256, 128)),
                 [("row", dxa, 0, D), ("row", du, 0, D), ("row", x, 0, D), ("full", scale1)],
                 [(D, F32)], [(1, D), (1, D)])


def _ada_fwd(c_all, w, bias):
    B, D = c_all.shape
    NA = w.shape[1]
    tn = _pick(NA, (512, 256, 128))

    def body(c_ref, w_ref, b_ref, o_ref):
        cv = c_ref[...]
        ca = (cv * _sigmoid(cv)).astype(BF16)
        o_ref[...] = jnp.dot(ca, w_ref[...].astype(BF16), preferred_element_type=F32) + b_ref[...]

    return pl.pallas_call(
        body, name="ada_fwd", grid=(NA // tn,),
        in_specs=[pl.BlockSpec((B, D), lambda j: (0, 0)), pl.BlockSpec((D, tn), lambda j: (0, j)),
                  pl.BlockSpec((1, tn), lambda j: (0, j))],
        out_specs=pl.BlockSpec((B, tn), lambda j: (0, j)),
        out_shape=jax.ShapeDtypeStruct((B, NA), F32),
        compiler_params=_params(("arbitrary",)),
    )(c_all, w, bias)


def _ada_bwd(c_all, dmod):
    B, D = c_all.shape
    NA = dmod.shape[1]
    tn = _pick(NA, (512, 256, 128))

    def body(c_ref, d_ref, o_ref):
        cv = c_ref[...]
        ca = (cv * _sigmoid(cv)).astype(BF16)
        o_ref[...] = lax.dot_general(ca, d_ref[...].astype(BF16), TN, preferred_element_type=F32)

    return pl.pallas_call(
        body, name="ada_bwd", grid=(NA // tn,),
        in_specs=[pl.BlockSpec((B, D), lambda j: (0, 0)), pl.BlockSpec((B, tn), lambda j: (0, j))],
        out_specs=pl.BlockSpec((D, tn), lambda j: (0, j)),
        out_shape=jax.ShapeDtypeStruct((D, NA), F32),
        compiler_params=_params(("arbitrary",)),
    )(c_all, dmod)


def _sum8(parts):
    _, R, N = parts.shape

    def body(p_ref, o_ref):
        acc = p_ref[0]
        for d in range(1, 8):
            acc = acc + p_ref[d]
        o_ref[...] = acc

    return pl.pallas_call(body, name="sum8", out_shape=jax.ShapeDtypeStruct((R, N), F32),
                          compiler_params=_params())(parts)


def _adam_math(w, g, m, v):
    m = ADAM_B1 * m + (1.0 - ADAM_B1) * g
    v = ADAM_B2 * v + (1.0 - ADAM_B2) * (g * g)
    delta = -ADAM_LR * ((m / ADAM_C1) / (jnp.sqrt(v / ADAM_C2) + ADAM_EPS) + ADAM_WD * w)
    return delta, m, v


def _adam(name, w, m, v, g, carry=None):
    R, C = w.shape
    tm = _row_tile(R, C * 4, 1 << 20)
    steps = R // tm
    n_ci = len(carry.ins) if carry else 0
    n_co = len(carry.outs) if carry else 0

    def body(*refs):
        w_ref, m_ref, v_ref, g_ref = refs[:4]
        d_ref, nm_ref, nv_ref = refs[4 + n_ci:7 + n_ci]
        c_ins, c_outs, c_sems = refs[4:4 + n_ci], refs[7 + n_ci:7 + n_ci + n_co], refs[7 + n_ci + n_co:]
        if carry:
            @pl.when(pl.program_id(0) == 0)
            def _():
                carry.start(c_ins, c_outs, c_sems)

        delta, nm, nv = _adam_math(w_ref[...], g_ref[...], m_ref[...], v_ref[...])
        d_ref[...] = delta
        nm_ref[...] = nm
        nv_ref[...] = nv
        if carry:
            @pl.when(pl.program_id(0) == steps - 1)
            def _():
                carry.finish(c_ins, c_outs, c_sems)

    spec = pl.BlockSpec((tm, C), lambda i: (i, 0))
    res = pl.pallas_call(
        body, name=name, grid=(steps,), in_specs=[spec] * 4 + [ANY] * n_ci, out_specs=[spec] * 3 + [ANY] * n_co,
        out_shape=[jax.ShapeDtypeStruct((R, C), F32)] * 3 + (carry.outs if carry else []),
        scratch_shapes=carry.sems if carry else [],
        input_output_aliases=carry.io_aliases(4, 3) if carry else {},
        compiler_params=_params(("arbitrary",)),
    )(w, m, v, g, *(carry.ins if carry else []))
    return (res[:3], res[3:]) if carry else res


def _adam_halves(name, w, m, v, mine, other, core, carry=None):
    R, C = w.shape
    Rh = mine.shape[0]
    tc = max(t for t in range(128, C + 1, 128) if C % t == 0 and R * t <= (3 << 17))
    steps = C // tc
    n_ci = len(carry.ins) if carry else 0
    n_co = len(carry.outs) if carry else 0

    def body(*refs):
        c_ref, w_ref, m_ref, v_ref, a_ref, b_ref = refs[:6]
        g_ref, d_ref, nm_ref, nv_ref = refs[6 + n_ci:10 + n_ci]
        c_ins, c_outs, c_sems = refs[6:6 + n_ci], refs[10 + n_ci:10 + n_ci + n_co], refs[10 + n_ci + n_co:]
        if carry:
            @pl.when(pl.program_id(0) == 0)
            def _():
                carry.start(c_ins, c_outs, c_sems)

        first = c_ref[0] == 0
        g = jnp.concatenate([jnp.where(first, a_ref[...], b_ref[...]),
                             jnp.where(first, b_ref[0:R - Rh, :], a_ref[0:R - Rh, :])], axis=0)
        delta, nm, nv = _adam_math(w_ref[...], g, m_ref[...], v_ref[...])
        g_ref[...] = g
        d_ref[...] = delta
        nm_ref[...] = nm
        nv_ref[...] = nv
        if carry:
            @pl.when(pl.program_id(0) == steps - 1)
            def _():
                carry.finish(c_ins, c_outs, c_sems)

    spec = pl.BlockSpec((R, tc), lambda i, c_ref: (0, i))
    h_spec = pl.BlockSpec((Rh, tc), lambda i, c_ref: (0, i))
    res = pl.pallas_call(
        body, name=name, out_shape=[jax.ShapeDtypeStruct((R, C), F32)] * 4 + (carry.outs if carry else []),
        grid_spec=pltpu.PrefetchScalarGridSpec(
            num_scalar_prefetch=1, grid=(steps,), in_specs=[spec, spec, spec, h_spec, h_spec] + [ANY] * n_ci,
            out_specs=[spec] * 4 + [ANY] * n_co, scratch_shapes=carry.sems if carry else []),
        input_output_aliases=carry.io_aliases(6, 4) if carry else {},
        compiler_params=_params(("arbitrary",)),
    )(core, w, m, v, mine, other, *(carry.ins if carry else []))
    return (res[:4], res[4:]) if carry else res


def _adam_small(name, w, m, v, g):
    def body(w_ref, m_ref, v_ref, g_ref, d_ref, nm_ref, nv_ref):
        delta, nm, nv = _adam_math(w_ref[...], g_ref[...], m_ref[...], v_ref[...])
        d_ref[...] = delta
        nm_ref[...] = nm
        nv_ref[...] = nv

    return pl.pallas_call(body, name=name, out_shape=[jax.ShapeDtypeStruct(w.shape, F32)] * 3,
                          compiler_params=_params())(w, m, v, g)


def _place():
    return lax.axis_index("x"), lax.axis_index("y"), lax.axis_index("c")


def _other_chips(x, y):
    return [(1 - x, y), (x, 1 - y), (1 - x, 1 - y)]


def _all_gather8(blk, name):
    R, N = blk.shape

    def body(x_ref, out_ref, send_sems, recv_sems, local_sem):
        x, y, c = _place()
        me = 4 * x + 2 * y + c
        mine = pltpu.make_async_copy(x_ref, out_ref.at[me], local_sem)
        mine.start()
        flips = [(j >> 2 & 1, j >> 1 & 1, j & 1) for j in range(1, 8)]
        peers = [((1 - x) if fx else x, (1 - y) if fy else y, (1 - c) if fc else c) for fx, fy, fc in flips]
        sends = []
        for j, peer in enumerate(peers):
            cp = pltpu.make_async_remote_copy(src_ref=x_ref, dst_ref=out_ref.at[me], send_sem=send_sems.at[j],
                                              recv_sem=recv_sems.at[j], device_id=peer, device_id_type=MESH)
            cp.start()
            sends.append(cp)
        for j, (px, py, pc) in enumerate(peers):
            pltpu.make_async_remote_copy(src_ref=x_ref, dst_ref=out_ref.at[4 * px + 2 * py + pc],
                                         send_sem=send_sems.at[j], recv_sem=recv_sems.at[j],
                                         device_id=(px, py, pc), device_id_type=MESH).wait_recv()
        for cp in sends:
            cp.wait_send()
        mine.wait()

    return pl.pallas_call(
        body, name=name, out_shape=jax.ShapeDtypeStruct((8, R, N), F32),
        in_specs=[pl.BlockSpec(memory_space=pltpu.VMEM)], out_specs=pl.BlockSpec(memory_space=pltpu.VMEM),
        scratch_shapes=[pltpu.SemaphoreType.DMA((7,)), pltpu.SemaphoreType.DMA((7,)), pltpu.SemaphoreType.DMA],
        compiler_params=_params(),
    )(blk)


def _piece(rows, piece):
    i, n = piece
    assert rows % 16 == 0 and rows // 16 >= n, (rows, piece)
    lo, hi = (rows // 16 * i // n) * 16, (rows // 16 * (i + 1) // n) * 16
    return pl.ds(lo, hi - lo)


def _scatter_plan(arrs, piece=(0, 1), into=None):
    n = len(arrs)

    def copies(ins, outs, sems):
        send_sems, recv_sems = sems
        x, y, c = _place()
        chips = _other_chips(x, y)
        cps = []
        for k in range(n):
            rows = _piece(arrs[k].shape[1], piece)
            for j, (px, py) in enumerate(chips):
                cps.append(pltpu.make_async_remote_copy(
                    src_ref=ins[k].at[2 * px + py, rows], dst_ref=outs[k].at[j, rows],
                    send_sem=send_sems.at[3 * k + j], recv_sem=recv_sems.at[3 * k + j],
                    device_id=(px, py, c), device_id_type=MESH))
        return cps

    def start(ins, outs, sems):
        for cp in copies(ins, outs, sems):
            cp.start()

    def finish(ins, outs, sems):
        for cp in copies(ins, outs, sems):
            cp.wait()

    return _Plan(list(arrs) + list(into or []), [jax.ShapeDtypeStruct((3,) + a.shape[1:], a.dtype) for a in arrs],
                 [pltpu.SemaphoreType.DMA((3 * n,))] * 2, start, finish,
                 aliases={n + k: k for k in range(n)} if into else None)


def _gather_plan(shards, piece=(0, 1), into=None):
    n = len(shards)

    def parts(ins, outs, sems):
        s1, r1, s2, r2, loc = sems
        x, y, c = _place()
        me = 2 * x + y
        chips = _other_chips(x, y)
        sib = (x, y, 1 - c)

        def rows(k):
            return _piece(shards[k].shape[1], piece)

        def ici(k, j, slab, to):
            return pltpu.make_async_remote_copy(src_ref=ins[k].at[c, rows(k)], dst_ref=outs[k].at[slab, c, rows(k)],
                                                send_sem=s1.at[3 * k + j], recv_sem=r1.at[3 * k + j],
                                                device_id=to, device_id_type=MESH)

        def d2d(k, j, slab, half):
            return pltpu.make_async_remote_copy(src_ref=outs[k].at[slab, half, rows(k)],
                                                dst_ref=outs[k].at[slab, half, rows(k)],
                                                send_sem=s2.at[3 * k + j], recv_sem=r2.at[3 * k + j],
                                                device_id=sib, device_id_type=MESH)

        def own(k):
            return pltpu.make_async_remote_copy(src_ref=ins[k].at[:, rows(k)], dst_ref=outs[k].at[me, :, rows(k)],
                                                send_sem=loc.at[2 * k], recv_sem=loc.at[2 * k + 1],
                                                device_id=sib, device_id_type=MESH)

        return c, me, chips, ici, d2d, own

    def start(ins, outs, sems):
        c, me, chips, ici, d2d, own = parts(ins, outs, sems)
        for k in range(n):
            for j, (px, py) in enumerate(chips):
                ici(k, j, me, (px, py, c)).start()
        for k in range(n):
            own(k).start()

    def finish(ins, outs, sems):
        c, me, chips, ici, d2d, own = parts(ins, outs, sems)
        for k in range(n):
            for j, (px, py) in enumerate(chips):
                ici(k, j, 2 * px + py, (px, py, c)).wait_recv()
                d2d(k, j, 2 * px + py, c).start()
        for k in range(n):
            for j, (px, py) in enumerate(chips):
                d2d(k, j, 2 * px + py, 1 - c).wait_recv()
        for k in range(n):
            own(k).wait()
            for j, (px, py) in enumerate(chips):
                ici(k, j, me, (px, py, c)).wait_send()
                d2d(k, j, 2 * px + py, c).wait_send()

    return _Plan(list(shards) + list(into or []), [jax.ShapeDtypeStruct((4,) + a.shape, a.dtype) for a in shards],
                 [pltpu.SemaphoreType.DMA((3 * n,))] * 4 + [pltpu.SemaphoreType.DMA((2 * n,))], start, finish,
                 aliases={n + k: k for k in range(n)} if into else None)


def _pair_plan(parts):
    n = len(parts)

    def copies(ins, outs, sems):
        send_sems, recv_sems = sems
        x, y, c = _place()
        return [pltpu.make_async_remote_copy(src_ref=ins[k].at[p, 1 - c], dst_ref=outs[k].at[p],
                                             send_sem=send_sems.at[4 * k + p], recv_sem=recv_sems.at[4 * k + p],
                                             device_id=(x, y, 1 - c), device_id_type=MESH)
                for k in range(n) for p in range(4)]

    def start(ins, outs, sems):
        for cp in copies(ins, outs, sems):
            cp.start()

    def finish(ins, outs, sems):
        for cp in copies(ins, outs, sems):
            cp.wait()

    return _Plan(parts, [jax.ShapeDtypeStruct((4,) + a.shape[2:], a.dtype) for a in parts],
                 [pltpu.SemaphoreType.DMA((4 * n,))] * 2, start, finish)


def _sibling_plan(arrs):
    n = len(arrs)

    def copies(ins, outs, sems):
        send_sems, recv_sems = sems
        x, y, c = _place()
        return [pltpu.make_async_remote_copy(src_ref=ins[k], dst_ref=outs[k], send_sem=send_sems.at[k],
                                             recv_sem=recv_sems.at[k], device_id=(x, y, 1 - c), device_id_type=MESH)
                for k in range(n)]

    def start(ins, outs, sems):
        for cp in copies(ins, outs, sems):
            cp.start()

    def finish(ins, outs, sems):
        for cp in copies(ins, outs, sems):
            cp.wait()

    return _Plan(arrs, [jax.ShapeDtypeStruct(a.shape, a.dtype) for a in arrs],
                 [pltpu.SemaphoreType.DMA((n,))] * 2, start, finish)


def _join_plans(plans):
    def split(seq, counts):
        out, at = [], 0
        for cnt in counts:
            out.append(seq[at:at + cnt])
            at += cnt
        return out

    n_i, n_o, n_s = ([len(getattr(p, f)) for p in plans] for f in ("ins", "outs", "sems"))

    def start(ins, outs, sems):
        for p, i, o, s in zip(plans, split(ins, n_i), split(outs, n_o), split(sems, n_s)):
            p.start(i, o, s)

    def finish(ins, outs, sems):
        for p, i, o, s in zip(plans, split(ins, n_i), split(outs, n_o), split(sems, n_s)):
            p.finish(i, o, s)

    aliases, at_i, at_o = {}, 0, 0
    for p in plans:
        aliases.update(p.io_aliases(at_i, at_o))
        at_i, at_o = at_i + len(p.ins), at_o + len(p.outs)
    return _Plan(sum((p.ins for p in plans), []), sum((p.outs for p in plans), []), sum((p.sems for p in plans), []),
                 start, finish, aliases)


def _add_pair(parts, sib, core, name):
    P4, _, Rh, C = parts.shape
    tm, tc = _tile2(Rh, C, 16)

    def body(c_ref, a_ref, b_ref, o_ref):
        o_ref[...] = (a_ref[0].astype(F32) + b_ref[...].astype(F32)).astype(BF16)

    spec = pl.BlockSpec((1, tm, tc), lambda p, i, j, c_ref: (p, i, j))
    return pl.pallas_call(
        body, name=name, out_shape=jax.ShapeDtypeStruct((P4, Rh, C), BF16),
        grid_spec=pltpu.PrefetchScalarGridSpec(
            num_scalar_prefetch=1, grid=(P4, Rh // tm, C // tc),
            in_specs=[pl.BlockSpec((1, 1, tm, tc), lambda p, i, j, c_ref: (p, c_ref[0], i, j)), spec], out_specs=spec),
        compiler_params=_params(("parallel",) * 3),
    )(core, parts, sib)


def _sum_slabs(pre, recv, chip, name):
    _, Rh, C = pre.shape
    tm, tc = _tile2(Rh, C, 16)

    def body(me_ref, own_ref, r_ref, o_ref):
        acc = own_ref[0].astype(F32)
        for j in range(3):
            acc = acc + r_ref[j].astype(F32)
        o_ref[...] = acc

    return pl.pallas_call(
        body, name=name, out_shape=jax.ShapeDtypeStruct((Rh, C), F32),
        grid_spec=pltpu.PrefetchScalarGridSpec(
            num_scalar_prefetch=1, grid=(Rh // tm, C // tc),
            in_specs=[pl.BlockSpec((1, tm, tc), lambda i, j, me_ref: (me_ref[0], i, j)),
                      pl.BlockSpec((3, tm, tc), lambda i, j, me_ref: (0, i, j))],
            out_specs=pl.BlockSpec((tm, tc), lambda i, j, me_ref: (i, j))),
        compiler_params=_params(("parallel", "parallel")),
    )(chip, pre, recv)


def kernel(x, c, positions, w_ada, b_ada, w_in, g_q_a, w_q_b, g_kv_a, w_kv_b, w_o_a, w_conv, w_o_b, w_o, ln1_g, ln1_b, w_ffn_in, w_ffn_out, ln2_g, ln2_b, loss_target, m_w_ada, m_b_ada, m_w_in, m_g_q_a, m_w_q_b, m_g_kv_a, m_w_kv_b, m_w_o_a, m_w_conv, m_w_o_b, m_w_o, m_ln1_g, m_ln1_b, m_w_ffn_in, m_w_ffn_out, m_ln2_g, m_ln2_b, v_w_ada, v_b_ada, v_w_in, v_g_q_a, v_w_q_b, v_g_kv_a, v_w_kv_b, v_w_o_a, v_w_conv, v_w_o_b, v_w_o, v_ln1_g, v_ln1_b, v_w_ffn_in, v_w_ffn_out, v_ln2_g, v_ln2_b):
    S, D = x.shape[1], x.shape[2]
    F = w_ffn_out.shape[1] * 4
    ax, ay, ac = _place()
    chip = 2 * ax + ay
    dev = 4 * ax + 2 * ay + ac
    x2, tgt = x[0], loss_target[0]
    w_ada2, w_in2, w_q_b2, w_kv_b2 = w_ada[0], w_in[0], w_q_b[0], w_kv_b[0]
    w_o_a2, w_o_b2, w_o2, w_ffn_in2, w_ffn_out2 = w_o_a[0], w_o_b[0], w_o[0], w_ffn_in[0], w_ffn_out[0]
    NA = w_ada2.shape[1]
    CW = w_conv.shape[2]

    inv_freq = 1.0 / (ROPE_THETA ** (jnp.arange(0, QK_ROPE, 2, dtype=F32) / QK_ROPE))
    ang = positions[0].astype(F32)[:, None] * inv_freq
    cos, sin = jnp.cos(ang), jnp.sin(ang)
    z32, z64, z96 = jnp.zeros((S, 32), F32), jnp.zeros((S, 64), F32), jnp.zeros((S, 96), F32)
    tab = jnp.concatenate([cos, cos, z64, -sin, z96, z32, sin, z64], axis=1)

    def halves(a):
        return a.reshape(2, a.shape[0] // 2, a.shape[1])

    def whole(g):
        return g.reshape(4, 2 * g.shape[2], g.shape[3])

    def cols(g):
        return jnp.transpose(g, (1, 0, 2)).reshape(g.shape[1], 4 * g.shape[2])

    w_inT, m_w_inT, v_w_inT = w_in2.T, m_w_in[0].T, v_w_in[0].T
    CS = w_inT.shape[0]
    CSP = -(-CS // 32) * 32
    sh_in = halves(jnp.pad(w_inT.astype(BF16), ((0, CSP - CS), (0, 0))))
    sh_qb, sh_kvb, sh_oa, sh_ob, sh_o, sh_fi, sh_fo = (
        halves(w.astype(BF16)) for w in (w_q_b2, w_kv_b2, w_o_a2, w_o_b2, w_o2, w_ffn_in2, w_ffn_out2))
    g_in = whole(_run_plan(_gather_plan([sh_in]), "gather_first")[0])

    def in_rows(lo, hi):
        parts = [g_in[p, max(lo, p * CS) - p * CS:min(hi, (p + 1) * CS) - p * CS]
                 for p in range(4) if max(lo, p * CS) < min(hi, (p + 1) * CS)]
        return parts[0] if len(parts) == 1 else jnp.concatenate(parts, axis=0)

    n_qkv = Q_LORA + KV_LORA + QK_ROPE
    W_qkvT = jnp.pad(in_rows(0, n_qkv), ((0, QKV_A - n_qkv), (0, 0)))
    W_convT = in_rows(n_qkv, n_qkv + 3 * D)
    W_gateT = in_rows(n_qkv + 3 * D, n_qkv + 5 * D)

    c_all = _all_gather8(c, "gather_c").reshape(8, D)
    wconv_all = _all_gather8(w_conv[0], "gather_wconv")
    w_conv_full = jnp.transpose(wconv_all[0::2], (1, 0, 2)).reshape(3, D)
    b_sh = lax.dynamic_slice(b_ada, (0, chip * NA), (1, NA))
    mod_sh = _ada_fwd(c_all, w_ada2, b_sh)
    mod_all = _all_gather8(mod_sh, "gather_mod")
    mod = lax.dynamic_slice(mod_all[0::2], (0, dev, 0), (4, 1, NA)).reshape(6, D)
    shift1, scale1, gate1, shift2, scale2, gate2 = (mod[k:k + 1] for k in range(6))

    u = _modulate(x2, scale1, shift1, "modulate1")
    pq, (g_qb, g_kvb) = _matmul(u, W_qkvT, "nt", F32, "proj_qkv", carry=_gather_plan([sh_qb, sh_kvb]))
    W_qb = jnp.pad(cols(whole(g_qb)).reshape(Q_LORA, N_HEADS, QK_NOPE + QK_ROPE),
                   ((0, 0), (0, 0), (0, QK_PAD - QK_NOPE - QK_ROPE))).reshape(Q_LORA, N_HEADS * QK_PAD)
    W_kvb = cols(whole(g_kvb))
    pc, (g_oa, g_ob) = _matmul(u, W_convT, "nt", F32, "proj_conv", carry=_gather_plan([sh_oa, sh_ob]))
    pg, (g_o,) = _matmul(u, W_gateT, "nt", F32, "proj_gate", carry=_gather_plan([sh_o]))
    W_oa, W_ob, W_o = (g.reshape(-1, D) for g in (g_oa, g_ob, g_o))
    rq, rkv, kr = _rms_fwd(pq, tab, g_q_a, g_kv_a)
    q = _q_rope(_matmul(rq, W_qb, "nn", F32, "q_b"), tab)
    kv = _matmul(rkv, W_kvb, "nn", BF16, "kv_b")
    o, lse, g_fi = _attn_fwd(q, kv, kr, carry=_gather_plan([sh_fi], (0, 2)))
    y_a, g_fi = _matmul(o, W_oa, "nn", F32, "o_a", carry=_gather_plan([sh_fi], (2, 4), g_fi))
    hb = _conv_fwd(pc, w_conv_full)
    y_b = _matmul(hb, W_ob, "nn", F32, "o_b")
    merged = _merge_fwd(y_a, y_b, pg)
    mix, g_fi = _matmul(merged, W_o, "nn", F32, "w_o", carry=_gather_plan([sh_fi], (3, 4), g_fi))
    W_fi = whole(g_fi[0])
    x1, u2 = _ln1_fwd(x2, mix, gate1, ln1_g, ln1_b, scale2, shift2)
    hh, (g_fo,) = _matmul(u2, W_fi, "nn", F32, "ffn_in", carry=_gather_plan([sh_fo]), shards="b")
    W_fo = g_fo.reshape(F, D)
    act = _swiglu_fwd(hh)
    ffn = _matmul(act, W_fo, "nn", F32, "ffn_out")

    core_i = ac.astype(jnp.int32).reshape(1)
    chip_i = chip.astype(jnp.int32).reshape(1)

    def uncols(g):
        return jnp.transpose(g.reshape(g.shape[0], 4, g.shape[1] // 4), (1, 0, 2))

    def slabs(p):
        return p.reshape(4, 2, p.shape[1] // 2, p.shape[2])

    def add_pairs(parts, sibs, nms):
        return [_add_pair(a, b, core_i, "add_pair_" + nm) for a, b, nm in zip(parts, sibs, nms)]

    def sum_all(pre, recv, nms):
        return [_sum_slabs(a, r, chip_i, "sum_slabs_" + nm) for a, r, nm in zip(pre, recv, nms)]

    dffn, dx1a, loss_acc, d_ln2_g, d_ln2_b, d_gate2 = _ln2_loss_bwd(x1, ffn, gate2, ln2_g, ln2_b, tgt)
    loss = lax.psum(loss_acc[0, 0], ("x", "y", "c"))
    dW_fo = _matmul(act, dffn, "tn", BF16, "d_w_ffn_out")
    p_fo = [slabs(dW_fo.reshape(4, -1, D))]
    dact, s_fo = _matmul(dffn, W_fo, "nt", F32, "d_act", carry=_pair_plan(p_fo))
    pre_fo = add_pairs(p_fo, s_fo, ["w_ffn_out"])
    dhh = _swiglu_bwd(dact, hh)
    dW_fi, r_fo = _matmul(u2, dhh, "tn", BF16, "d_w_ffn_in", carry=_scatter_plan(pre_fo), shards="o")
    p_fi = [slabs(dW_fi)]
    du2, s_fi = _matmul(dhh, W_fi, "nt", F32, "d_u2", carry=_pair_plan(p_fi), shards="b")
    pre_fi = add_pairs(p_fi, s_fi, ["w_ffn_in"])
    dmix, dxa, d_shift2, d_scale2, d_ln1_g, d_ln1_b, d_gate1 = _ln1_bwd(x2, mix, dx1a, du2, gate1, ln1_g, ln1_b, scale2)
    dW_o = _matmul(merged, dmix, "tn", BF16, "d_w_o")
    dmerged = _matmul(dmix, W_o, "nt", F32, "d_merged")
    dy_a, dy_b, dgate = _merge_bwd(dmerged, y_a, y_b, pg)
    dW_oa = _matmul(o, dy_a, "tn", BF16, "d_w_o_a")
    do = _matmul(dy_a, W_oa, "nt", BF16, "d_o")
    dW_ob = _matmul(hb, dy_b, "tn", BF16, "d_w_o_b")
    p_mid = [slabs(g.reshape(4, -1, D)) for g in (dW_oa, dW_ob, dW_o)]
    dhb, s_mid = _matmul(dy_b, W_ob, "nt", F32, "d_hb", carry=_pair_plan(p_mid))
    pre_mid = add_pairs(p_mid, s_mid, ["w_o_a", "w_o_b", "w_o"])
    dconv, d_wconv = _conv_bwd(dhb, pc, w_conv_full)
    dq, dkv, dkr, r_fi = _attn_bwd(q, kv, kr, do, o, lse, tab, carry=_scatter_plan(pre_fi))
    names_a = ["w_ffn_out", "w_ffn_in", "w_o_a", "w_o_b", "w_o"]
    dW_qb = _matmul(rq, dq, "tn", BF16, "d_w_q_b")
    d_rq = _matmul(dq, W_qb, "nt", F32, "d_rq")
    dW_kvb = _matmul(rkv, dkv, "tn", BF16, "d_w_kv_b")
    d_rkv = _matmul(dkv, W_kvb, "nt", F32, "d_rkv")
    dqkv, d_g_q, d_g_kv = _rms_bwd(d_rq, d_rkv, pq, dkr, g_q_a, g_kv_a)
    dW_qkvT = _matmul(dqkv, u, "tn", BF16, "d_w_qkv")
    dW_convT, r_mid = _matmul(dconv, u, "tn", BF16, "d_w_conv", carry=_scatter_plan(pre_mid, (0, 2)))
    dW_gateT, r_mid = _matmul(dgate, u, "tn", BF16, "d_w_gate", carry=_scatter_plan(pre_mid, (1, 2), r_mid))
    fin_a = sum_all(pre_fo + pre_fi + pre_mid, list(r_fo) + list(r_fi) + list(r_mid), names_a)
    dW_inT = jnp.concatenate([dW_qkvT[:n_qkv], dW_convT, dW_gateT], axis=0).reshape(4, CS, D)
    dW_inT = jnp.pad(dW_inT, ((0, 0), (0, CSP - CS), (0, 0)))
    dW_qb_u = dW_qb.reshape(Q_LORA, N_HEADS, QK_PAD)[:, :, :QK_NOPE + QK_ROPE].reshape(Q_LORA, -1)
    names_b = ["w_in", "w_q_b", "w_kv_b"]
    p_b = [slabs(dW_inT), slabs(uncols(dW_qb_u)), slabs(uncols(dW_kvb))]
    du, s_b = _matmul(dqkv, W_qkvT, "nn", F32, "d_u_qkv", carry=_pair_plan(p_b))
    pre_b = add_pairs(p_b, s_b, names_b)
    du, moved = _matmul(dconv, W_convT, "nn", F32, "d_u_conv", add=du,
                        carry=_join_plans([_scatter_plan(pre_b, (0, 4)), _sibling_plan(fin_a)]))
    r_b, fs_a = moved[:3], moved[3:]
    du, r_b = _matmul(dgate, W_gateT, "nn", F32, "d_u_gate", add=du, carry=_scatter_plan(pre_b, (1, 4), r_b))
    grad_x, d_shift1, d_scale1 = _dx_final(dxa, du, x2, scale1)

    def pad_d(v):
        return jnp.pad(v, ((0, 0), (0, D - v.shape[1])))

    small = jnp.concatenate([d_ln1_g, d_ln1_b, d_ln2_g, d_ln2_b, pad_d(d_g_q), pad_d(d_g_kv), d_wconv,
                             d_shift1, d_scale1, d_gate1, d_shift2, d_scale2, d_gate2, jnp.zeros((1, D), F32)], axis=0)
    small_all = _all_gather8(small, "gather_small")
    small_sum = _sum8(small_all)
    g_ln1_g, g_ln1_b, g_ln2_g, g_ln2_b = (small_sum[k:k + 1] for k in range(4))
    g_g_q, g_g_kv = small_sum[4:5, :Q_LORA], small_sum[5:6, :KV_LORA]
    g_wconv = lax.dynamic_slice(small_sum[6:9], (0, chip * CW), (3, CW))
    g_b_ada = small_sum[9:15].reshape(1, 6 * D)
    dmod_all = small_all[:, 9:15, :].reshape(8, 6 * D)
    g_w_ada = _ada_bwd(c_all, lax.dynamic_slice(dmod_all, (0, chip * NA), (8, NA)))

    big = {}
    ws = dict(w_in=(w_inT, m_w_inT, v_w_inT), w_q_b=(w_q_b2, m_w_q_b[0], v_w_q_b[0]),
              w_kv_b=(w_kv_b2, m_w_kv_b[0], v_w_kv_b[0]), w_o_a=(w_o_a2, m_w_o_a[0], v_w_o_a[0]),
              w_o_b=(w_o_b2, m_w_o_b[0], v_w_o_b[0]), w_o=(w_o2, m_w_o[0], v_w_o[0]),
              w_ffn_in=(w_ffn_in2, m_w_ffn_in[0], v_w_ffn_in[0]), w_ffn_out=(w_ffn_out2, m_w_ffn_out[0], v_w_ffn_out[0]))
    def adam_of(nm, a, b, carry=None):
        w_, m_, v_ = ws[nm]
        return _adam_halves("adam_" + nm, w_, m_, v_, a, b, core_i, carry)

    big["w_ffn_in"], r_b = adam_of("w_ffn_in", fin_a[1], fs_a[1], _scatter_plan(pre_b, (2, 4), r_b))
    upd, r_b = _adam("adam_w_ada", w_ada2, m_w_ada[0], v_w_ada[0], g_w_ada, _scatter_plan(pre_b, (3, 4), r_b))
    big["w_ada"] = [g_w_ada] + list(upd)
    fin_b = sum_all(pre_b, r_b, names_b)
    fs_b = _run_plan(_sibling_plan(fin_b), "sibling_last")
    for nm, a, b in zip(names_a + names_b, fin_a + fin_b, list(fs_a) + list(fs_b)):
        if nm != "w_ffn_in":
            big[nm] = adam_of(nm, a, b)
    sm = {}
    for nm, w_, m_, v_, g_ in [("b_ada", b_ada, m_b_ada, v_b_ada, g_b_ada), ("g_q_a", g_q_a, m_g_q_a, v_g_q_a, g_g_q),
                               ("g_kv_a", g_kv_a, m_g_kv_a, v_g_kv_a, g_g_kv),
                               ("w_conv", w_conv[0], m_w_conv[0], v_w_conv[0], g_wconv),
                               ("ln1_g", ln1_g, m_ln1_g, v_ln1_g, g_ln1_g), ("ln1_b", ln1_b, m_ln1_b, v_ln1_b, g_ln1_b),
                               ("ln2_g", ln2_g, m_ln2_g, v_ln2_g, g_ln2_g), ("ln2_b", ln2_b, m_ln2_b, v_ln2_b, g_ln2_b)]:
        sm[nm] = (g_,) + tuple(_adam_small("adam_" + nm, w_, m_, v_, g_))

    order = ["w_ada", "b_ada", "w_in", "g_q_a", "w_q_b", "g_kv_a", "w_kv_b", "w_o_a", "w_conv", "w_o_b", "w_o",
             "ln1_g", "ln1_b", "w_ffn_in", "w_ffn_out", "ln2_g", "ln2_b"]
    lead = {"b_ada", "g_q_a", "g_kv_a", "ln1_g", "ln1_b", "ln2_g", "ln2_b"}

    def leaf(nm, k):
        val = big[nm][k] if nm in big else sm[nm][k]
        if nm == "w_in":
            val = val.T
        return val if nm in lead else val[None]

    outs = [loss, grad_x[None]]
    for k in range(4):
        outs += [leaf(nm, k) for nm in order]
    return tuple(outs)
```

```python
import functools

import jax
import jax.numpy as jnp
from jax import lax
from jax.experimental import pallas as pl
from jax.experimental.pallas import tpu as pltpu

F32, BF16 = jnp.float32, jnp.bfloat16
N_HEADS, QK_NOPE, QK_ROPE, V_HEAD = 16, 128, 64, 128
Q_LORA, KV_LORA = 512, 512
QK_PAD = 256
QKV_A = 1152
CHUNK_SHIFT = 6
ATTN_SCALE = (QK_NOPE + QK_ROPE) ** -0.5
ROPE_THETA = 10000.0
ALPHA = 2.0 ** 0.25
LN_EPS, RMS_EPS = 1e-5, 1e-6
ADAM_LR, ADAM_B1, ADAM_B2, ADAM_EPS, ADAM_WD, ADAM_STEP = 0.001, 0.9, 0.999, 1e-08, 0.01, 10
ADAM_C1 = 1.0 - ADAM_B1 ** ADAM_STEP
ADAM_C2 = 1.0 - ADAM_B2 ** ADAM_STEP
VMEM_LIMIT = 56 * 1024 * 1024
MESH = pl.DeviceIdType.MESH
ANY = pl.BlockSpec(memory_space=pl.ANY)
NT = (((1,), (1,)), ((), ()))
TN = (((0,), (0,)), ((), ()))
NN = (((1,), (0,)), ((), ()))


def _params(sem=None):
    return pltpu.CompilerParams(dimension_semantics=sem, vmem_limit_bytes=VMEM_LIMIT)


def _pick(n, cands=(1408, 1024, 512, 384, 256, 128)):
    for t in cands:
        if n % t == 0:
            return t
    return n


def _row_tile(rows, row_bytes, budget, mult=8):
    best = mult
    for t in range(mult, rows + 1, mult):
        if rows % t == 0 and t * row_bytes <= budget:
            best = t
    return best


def _tile2(rows, cols, mult=8, budget=1 << 18):
    col_tiles = [t for t in range(128, cols + 1, 128) if cols % t == 0] or [cols]
    best = None
    for tc in col_tiles:
        for tr in range(mult, rows + 1, mult):
            if rows % tr == 0 and tr * tc <= budget and (best is None or tr * tc > best[0] * best[1]):
                best = (tr, tc)
    assert best is not None, (rows, cols)
    return best


def _sigmoid(x):
    return jax.nn.sigmoid(x)


class _Plan:
    def __init__(self, ins, outs, sems, start, finish, aliases=None):
        self.ins, self.outs, self.sems, self.start, self.finish = list(ins), list(outs), list(sems), start, finish
        self.aliases = dict(aliases or {})

    def io_aliases(self, first_in, first_out):
        return {first_in + i: first_out + o for i, o in self.aliases.items()}


def _run_plan(plan, name):
    n_in, n_out = len(plan.ins), len(plan.outs)

    def body(*refs):
        ins, outs, sems = refs[:n_in], refs[n_in:n_in + n_out], refs[n_in + n_out:]
        plan.start(ins, outs, sems)
        plan.finish(ins, outs, sems)

    return pl.pallas_call(body, name=name, out_shape=plan.outs, in_specs=[ANY] * n_in, out_specs=[ANY] * n_out,
                          scratch_shapes=plan.sems, input_output_aliases=plan.io_aliases(0, 0),
                          compiler_params=_params())(*plan.ins)


def _matmul(a, b, mode, out_dtype, name, add=None, carry=None, shards=None):
    if mode == "nn":
        (M, K), N, dims = a.shape, b.shape[-1] * (4 if shards else 1), NN
    elif mode == "nt":
        (M, K), N, dims = a.shape, b.shape[-2], NT
    else:
        (K, M), N, dims = a.shape, b.shape[1], TN
    split_n = shards and mode != "nt"
    tm = _pick(M)
    tn = _pick(N // 4) if split_n else _pick(N)
    if shards and mode == "nt":
        tk = _pick(K // 4)
    else:
        tk = K if K <= 2048 else _pick(K)
    nk = K // tk
    per = (N // 4 // tn) if split_n else (K // 4 // tk if shards else 1)
    a_spec = (pl.BlockSpec((tk, tm), lambda i, j, k: (k, i)) if mode == "tn"
              else pl.BlockSpec((tm, tk), lambda i, j, k: (i, k)))
    if shards == "b" and mode == "nn":
        b_spec = pl.BlockSpec((None, tk, tn), lambda i, j, k: (j // per, k, j % per))
    elif shards == "b":
        b_spec = pl.BlockSpec((None, tn, tk), lambda i, j, k: (k // per, j, k % per))
    else:
        b_spec = (pl.BlockSpec((tn, tk), lambda i, j, k: (j, k)) if mode == "nt"
                  else pl.BlockSpec((tk, tn), lambda i, j, k: (k, j)))
    o_spec = pl.BlockSpec((tm, tn), lambda i, j, k: (i, j))
    o_shape = (M, N)
    if shards == "o":
        o_spec, o_shape = pl.BlockSpec((None, tm, tn), lambda i, j, k: (j // per, i, j % per)), (4, M, N // 4)
    has_add = add is not None
    n_ci = len(carry.ins) if carry else 0
    n_co = len(carry.outs) if carry else 0
    n_in = 2 + has_add
    grid = (M // tm, N // tn, nk)

    def body(*refs):
        a_ref, b_ref = refs[0], refs[1]
        add_ref = refs[2] if has_add else None
        o_ref = refs[n_in + n_ci]
        acc_ref = refs[n_in + n_ci + 1 + n_co] if nk > 1 else None
        c_ins = refs[n_in:n_in + n_ci]
        c_outs = refs[n_in + n_ci + 1:n_in + n_ci + 1 + n_co]
        c_sems = refs[n_in + n_ci + 1 + n_co + (nk > 1):]
        i, j, k = pl.program_id(0), pl.program_id(1), pl.program_id(2)

        if carry:
            @pl.when((i == 0) & (j == 0) & (k == 0))
            def _():
                carry.start(c_ins, c_outs, c_sems)

        part = lax.dot_general(a_ref[...], b_ref[...], dims, preferred_element_type=F32)
        if nk == 1:
            o_ref[...] = (part + add_ref[...] if has_add else part).astype(o_ref.dtype)
        else:
            @pl.when(k == 0)
            def _():
                acc_ref[...] = part

            @pl.when((k > 0) & (k < nk - 1))
            def _():
                acc_ref[...] += part

            @pl.when(k == nk - 1)
            def _():
                r = acc_ref[...] + part
                if has_add:
                    r = r + add_ref[...]
                o_ref[...] = r.astype(o_ref.dtype)

        if carry:
            @pl.when((i == grid[0] - 1) & (j == grid[1] - 1) & (k == nk - 1))
            def _():
                carry.finish(c_ins, c_outs, c_sems)

    ins = [a, b] + ([add] if has_add else []) + (carry.ins if carry else [])
    in_specs = [a_spec, b_spec] + ([o_spec] if has_add else []) + [ANY] * n_ci
    res = pl.pallas_call(
        body, name=name, grid=grid,
        in_specs=in_specs, out_specs=[o_spec] + [ANY] * n_co,
        out_shape=[jax.ShapeDtypeStruct(o_shape, out_dtype)] + (carry.outs if carry else []),
        scratch_shapes=([pltpu.VMEM((tm, tn), F32)] if nk > 1 else []) + (carry.sems if carry else []),
        input_output_aliases=carry.io_aliases(n_in, 1) if carry else {},
        compiler_params=_params(("arbitrary",) * 3 if carry else ("parallel", "parallel", "arbitrary")),
    )(*ins)
    return (res[0], res[1:]) if carry else res[0]


def _rows(body, name, n_rows, tm, ins, outs, accs=()):
    grid = (n_rows // tm,)
    per8 = tm // 8
    last8 = n_rows // 8 - 1
    arrays, in_specs = [], []
    for spec in ins:
        kind, arr = spec[0], spec[1]
        arrays.append(arr)
        if kind == "row":
            _, _, cb, w = spec
            in_specs.append(pl.BlockSpec((tm, w), lambda i, cb=cb: (i, cb)))
        elif kind == "full":
            in_specs.append(pl.BlockSpec(arr.shape, lambda i, nd=arr.ndim: (0,) * nd))
        elif kind == "prev":
            _, _, cb, w = spec
            in_specs.append(pl.BlockSpec((8, w), lambda i, cb=cb: (jnp.maximum(i * per8 - 1, 0), cb)))
        else:
            _, _, cb, w = spec
            in_specs.append(pl.BlockSpec((8, w), lambda i, cb=cb: (jnp.minimum((i + 1) * per8, last8), cb)))
    out_shape = [jax.ShapeDtypeStruct((n_rows, w), dt) for (w, dt) in outs]
    out_specs = [pl.BlockSpec((tm, w), lambda i: (i, 0)) for (w, _) in outs]
    out_shape += [jax.ShapeDtypeStruct(s, F32) for s in accs]
    out_specs += [pl.BlockSpec(s, lambda i, nd=len(s): (0,) * nd) for s in accs]
    n_in, n_out = len(ins), len(outs)

    def kernel_body(*refs):
        body(pl.program_id(0), refs[:n_in], refs[n_in:n_in + n_out], refs[n_in + n_out:])

    res = pl.pallas_call(
        kernel_body, name=name, grid=grid, in_specs=in_specs, out_specs=out_specs, out_shape=out_shape,
        compiler_params=_params(("arbitrary",)),
    )(*arrays)
    return res


def _acc_add(i, ref, val):
    @pl.when(i == 0)
    def _():
        ref[...] = val

    @pl.when(i > 0)
    def _():
        ref[...] += val


def _rope(t, tab, sign):
    c, sa, sb = tab[:, 0:128], tab[:, 128:256], tab[:, 256:384]
    rot = pltpu.roll(t, 96, 1) * sa + pltpu.roll(t, 32, 1) * sb
    return t * c + rot if sign > 0 else t * c - rot


def _ln_stats(r):
    mu = jnp.mean(r, axis=-1, keepdims=True)
    d = r - mu
    var = jnp.mean(d * d, axis=-1, keepdims=True)
    rstd = lax.rsqrt(var + LN_EPS)
    return d * rstd, rstd


def _ln_bwd(dxh, xh, rstd):
    m1 = jnp.mean(dxh, axis=-1, keepdims=True)
    m2 = jnp.mean(dxh * xh, axis=-1, keepdims=True)
    return rstd * (dxh - m1 - xh * m2)


def _modulate(x, scale, shift, name):
    S, D = x.shape

    def body(i, ins, outs, accs):
        outs[0][...] = (ins[0][...] * (1.0 + ins[1][...]) + ins[2][...]).astype(BF16)

    return _rows(body, name, S, _pick(S, (256, 128)), [("row", x, 0, D), ("full", scale), ("full", shift)], [(D, BF16)])[0]


def _rms_fwd(pq, tab, g_q, g_kv):
    S = pq.shape[0]

    def body(i, ins, outs, accs):
        pq_ref, tab_ref, gq_ref, gkv_ref = ins

        def rms(x, g):
            return x * lax.rsqrt(jnp.mean(x * x, axis=-1, keepdims=True) + RMS_EPS) * g

        outs[0][...] = rms(pq_ref[:, 0:Q_LORA], gq_ref[...]).astype(BF16)
        outs[1][...] = rms(pq_ref[:, Q_LORA:Q_LORA + KV_LORA], gkv_ref[...]).astype(BF16)
        outs[2][...] = _rope(pq_ref[:, Q_LORA + KV_LORA:QKV_A], tab_ref[...], 1).astype(BF16)

    return _rows(body, "rms_fwd", S, _pick(S, (256, 128)),
                 [("row", pq, 0, QKV_A), ("row", tab, 0, 384), ("full", g_q), ("full", g_kv)],
                 [(Q_LORA, BF16), (KV_LORA, BF16), (128, BF16)])


def _q_rope(q, tab):
    S, W = q.shape

    def body(i, ins, outs, accs):
        q_ref, tab_ref = ins
        t = tab_ref[...]
        for h in range(N_HEADS):
            lo = h * QK_PAD
            outs[0][:, lo:lo + 128] = q_ref[:, lo:lo + 128].astype(BF16)
            outs[0][:, lo + 128:lo + 256] = _rope(q_ref[:, lo + 128:lo + 256], t, 1).astype(BF16)

    return _rows(body, "q_rope", S, _pick(S, (256, 128)), [("row", q, 0, W), ("row", tab, 0, 384)], [(W, BF16)])[0]


def _allowed(q0, k0, bq):
    row = q0 + lax.broadcasted_iota(jnp.int32, (bq, bq), 0)
    col = k0 + lax.broadcasted_iota(jnp.int32, (bq, bq), 1)
    return (col >> CHUNK_SHIFT) <= (row >> CHUNK_SHIFT)


ATTN_BLOCK = 512


def _attn_fwd(q, kv, kr, carry=None):
    S = q.shape[0]
    bq = min(ATTN_BLOCK, S)
    nq = S // bq
    n_ci = len(carry.ins) if carry else 0
    n_co = len(carry.outs) if carry else 0

    def body(*refs):
        q_ref, kn_ref, v_ref, kr_ref = refs[:4]
        o_ref, lse_ref = refs[4 + n_ci:6 + n_ci]
        c_ins, c_outs = refs[4:4 + n_ci], refs[6 + n_ci:6 + n_ci + n_co]
        kcat = refs[6 + n_ci + n_co]
        c_sems = refs[7 + n_ci + n_co:]
        qi = pl.program_id(1)
        if carry:
            @pl.when((pl.program_id(0) == 0) & (qi == 0))
            def _():
                carry.start(c_ins, c_outs, c_sems)

        @pl.when(qi == 0)
        def _():
            kcat[:, 0:128] = kn_ref[...]
            kcat[:, 128:256] = kr_ref[...]

        qv = q_ref[...]

        def step(j, carry, masked):
            m, l, acc = carry
            off = pl.multiple_of(j * bq, bq)
            s = lax.dot_general(qv, kcat[pl.ds(off, bq), :], NT, preferred_element_type=F32) * ATTN_SCALE
            if masked:
                s = jnp.where(_allowed(qi * bq, off, bq), s, -1e30)
            m_new = jnp.maximum(m, jnp.max(s, axis=1, keepdims=True))
            a = jnp.exp(m - m_new)
            p = jnp.exp(s - m_new)
            l = a * l + jnp.sum(p, axis=1, keepdims=True)
            acc = a * acc + jnp.dot(p.astype(BF16), v_ref[pl.ds(off, bq), :], preferred_element_type=F32)
            return m_new, l, acc

        init = (jnp.full((bq, 1), -1e30, F32), jnp.zeros((bq, 1), F32), jnp.zeros((bq, V_HEAD), F32))
        below = lax.fori_loop(0, qi, lambda j, cr: step(j, cr, False), init)
        m, l, acc = step(qi, below, True)
        o_ref[...] = (acc / l).astype(BF16)
        lse_ref[0] = m + jnp.log(l)
        if carry:
            @pl.when((pl.program_id(0) == N_HEADS - 1) & (qi == nq - 1))
            def _():
                carry.finish(c_ins, c_outs, c_sems)

    res = pl.pallas_call(
        body, name="attn_fwd", grid=(N_HEADS, nq),
        in_specs=[pl.BlockSpec((bq, QK_PAD), lambda h, i: (i, h)),
                  pl.BlockSpec((S, 128), lambda h, i: (0, 2 * h)),
                  pl.BlockSpec((S, 128), lambda h, i: (0, 2 * h + 1)),
                  pl.BlockSpec((S, 128), lambda h, i: (0, 0))] + [ANY] * n_ci,
        out_specs=[pl.BlockSpec((bq, V_HEAD), lambda h, i: (i, h)),
                   pl.BlockSpec((1, bq, 1), lambda h, i: (h, i, 0))] + [ANY] * n_co,
        out_shape=[jax.ShapeDtypeStruct((S, N_HEADS * V_HEAD), BF16),
                   jax.ShapeDtypeStruct((N_HEADS, S, 1), F32)] + (carry.outs if carry else []),
        scratch_shapes=[pltpu.VMEM((S, QK_PAD), BF16)] + (carry.sems if carry else []),
        input_output_aliases=carry.io_aliases(4, 2) if carry else {},
        compiler_params=_params(("arbitrary", "arbitrary")),
    )(q, kv, kv, kr, *(carry.ins if carry else []))
    return res[0], res[1], res[2:]


def _attn_bwd(q, kv, kr, do, o, lse, tab, carry=None):
    S = q.shape[0]
    bq = min(ATTN_BLOCK, S)
    nq = S // bq

    n_ci = len(carry.ins) if carry else 0
    n_co = len(carry.outs) if carry else 0

    def body(*refs):
        q_ref, kn_ref, v_ref, kr_ref, do_ref, o_ref, lse_ref, tab_ref = refs[:8]
        dq_ref, dkv_ref, dkr_ref = refs[8 + n_ci:11 + n_ci]
        dq_acc, dk_acc, dv_acc, kcat, delta = refs[11 + n_ci + n_co:16 + n_ci + n_co]
        c_ins, c_outs, c_sems = refs[8:8 + n_ci], refs[11 + n_ci:11 + n_ci + n_co], refs[16 + n_ci + n_co:]
        h = pl.program_id(0)
        if carry:
            @pl.when(h == 0)
            def _():
                carry.start(c_ins, c_outs, c_sems)

        dq_acc[...] = jnp.zeros_like(dq_acc)
        dk_acc[...] = jnp.zeros_like(dk_acc)
        dv_acc[...] = jnp.zeros_like(dv_acc)
        kcat[:, 0:128] = kn_ref[...]
        kcat[:, 128:256] = kr_ref[...]
        for r in range(nq):
            rows = slice(r * bq, (r + 1) * bq)
            delta[rows, :] = jnp.sum(do_ref[rows, :].astype(F32) * o_ref[rows, :].astype(F32), axis=1, keepdims=True)

        def pair(i, j, masked):
            rows_i = pl.ds(pl.multiple_of(i * bq, bq), bq)
            rows_j = pl.ds(pl.multiple_of(j * bq, bq), bq)
            qv, dov, k = q_ref[rows_i, :], do_ref[rows_i, :], kcat[rows_j, :]
            s = lax.dot_general(qv, k, NT, preferred_element_type=F32) * ATTN_SCALE
            if masked:
                s = jnp.where(_allowed(i * bq, j * bq, bq), s, -1e30)
            p = jnp.exp(s - lse_ref[0, rows_i, :])
            dv_acc[rows_j, :] += lax.dot_general(p.astype(BF16), dov, TN, preferred_element_type=F32)
            dp = lax.dot_general(dov, v_ref[rows_j, :], NT, preferred_element_type=F32)
            ds = (p * (dp - delta[rows_i, :]) * ATTN_SCALE).astype(BF16)
            dk_acc[rows_j, :] += lax.dot_general(ds, qv, TN, preferred_element_type=F32)
            dq_acc[rows_i, :] += jnp.dot(ds, k, preferred_element_type=F32)

        def kv_step(j, _):
            pair(j, j, True)

            def q_step(i, _):
                pair(i, j, False)
                return 0

            lax.fori_loop(j + 1, nq, q_step, 0)
            return 0

        lax.fori_loop(0, nq, kv_step, 0)

        for r in range(nq):
            rows = slice(r * bq, (r + 1) * bq)
            dq_ref[rows, 0:128] = dq_acc[rows, 0:128].astype(BF16)
            dq_ref[rows, 128:256] = _rope(dq_acc[rows, 128:256], tab_ref[rows, :], -1).astype(BF16)
        dkv_ref[:, 0:128] = dk_acc[:, 0:128].astype(BF16)
        dkv_ref[:, 128:256] = dv_acc[...].astype(BF16)

        @pl.when(h == 0)
        def _():
            dkr_ref[...] = dk_acc[:, 128:256]

        @pl.when(h > 0)
        def _():
            dkr_ref[...] += dk_acc[:, 128:256]

        @pl.when(h == N_HEADS - 1)
        def _():
            for r in range(nq):
                rows = slice(r * bq, (r + 1) * bq)
                dkr_ref[rows, :] = _rope(dkr_ref[rows, :], tab_ref[rows, :], -1)
            if carry:
                carry.finish(c_ins, c_outs, c_sems)

    W = N_HEADS * QK_PAD
    res = pl.pallas_call(
        body, name="attn_bwd", grid=(N_HEADS,),
        in_specs=[pl.BlockSpec((S, QK_PAD), lambda h: (0, h)),
                  pl.BlockSpec((S, 128), lambda h: (0, 2 * h)),
                  pl.BlockSpec((S, 128), lambda h: (0, 2 * h + 1)),
                  pl.BlockSpec((S, 128), lambda h: (0, 0)),
                  pl.BlockSpec((S, V_HEAD), lambda h: (0, h)),
                  pl.BlockSpec((S, V_HEAD), lambda h: (0, h)),
                  pl.BlockSpec((1, S, 1), lambda h: (h, 0, 0)),
                  pl.BlockSpec((S, 384), lambda h: (0, 0))] + [ANY] * n_ci,
        out_specs=[pl.BlockSpec((S, QK_PAD), lambda h: (0, h)),
                   pl.BlockSpec((S, QK_PAD), lambda h: (0, h)),
                   pl.BlockSpec((S, 128), lambda h: (0, 0))] + [ANY] * n_co,
        out_shape=[jax.ShapeDtypeStruct((S, W), BF16), jax.ShapeDtypeStruct((S, W), BF16),
                   jax.ShapeDtypeStruct((S, 128), F32)] + (carry.outs if carry else []),
        scratch_shapes=[pltpu.VMEM((S, QK_PAD), F32), pltpu.VMEM((S, QK_PAD), F32), pltpu.VMEM((S, V_HEAD), F32),
                        pltpu.VMEM((S, QK_PAD), BF16), pltpu.VMEM((S, 1), F32)]
        + (carry.sems if carry else []),
        input_output_aliases=carry.io_aliases(8, 3) if carry else {},
        compiler_params=_params(("arbitrary",)),
    )(q, kv, kv, kr, do, o, lse, tab, *(carry.ins if carry else []))
    return res[0], res[1], res[2], res[3:]


def _shift_down(cur, prev8, i, n):
    tm = cur.shape[0]
    prev8 = jnp.where(i == 0, jnp.zeros_like(prev8), prev8)
    full = jnp.concatenate([prev8, cur], axis=0)
    return pltpu.roll(full, n, 0)[8:8 + tm, :]


def _shift_up(cur, next8, i, last, n):
    tm = cur.shape[0]
    next8 = jnp.where(i == last, jnp.zeros_like(next8), next8)
    full = jnp.concatenate([cur, next8], axis=0)
    return pltpu.roll(full, tm + 8 - n, 0)[0:tm, :]


def _conv_fwd(pc, w_conv):
    S, D = pc.shape[0], pc.shape[1] // 3
    tm = _pick(S, (256, 128))

    def body(i, ins, outs, accs):
        b_ref, c_ref, x_ref, cp_ref, xp_ref, w_ref = ins
        z = c_ref[...] * x_ref[...]
        zp = cp_ref[...] * xp_ref[...]
        cz = w_ref[0:1, :] * _shift_down(z, zp, i, 2) + w_ref[1:2, :] * _shift_down(z, zp, i, 1) + w_ref[2:3, :] * z
        outs[0][...] = (b_ref[...] * cz).astype(BF16)

    return _rows(body, "conv_fwd", S, tm,
                 [("row", pc, 0, D), ("row", pc, 1, D), ("row", pc, 2, D), ("prev", pc, 1, D), ("prev", pc, 2, D),
                  ("full", w_conv)], [(D, BF16)])[0]


def _conv_bwd(dhb, pc, w_conv):
    S, D = dhb.shape
    tm = _pick(S, (256, 128))
    last = S // tm - 1

    def body(i, ins, outs, accs):
        g_ref, b_ref, c_ref, x_ref, cp_ref, xp_ref, gn_ref, bn_ref, w_ref = ins
        w0, w1, w2 = w_ref[0:1, :], w_ref[1:2, :], w_ref[2:3, :]
        c, x, g = c_ref[...], x_ref[...], g_ref[...]
        z = c * x
        zp = cp_ref[...] * xp_ref[...]
        z1, z2 = _shift_down(z, zp, i, 1), _shift_down(z, zp, i, 2)
        cz = w0 * z2 + w1 * z1 + w2 * z
        dcz = g * b_ref[...]
        dczn = gn_ref[...] * bn_ref[...]
        dz = w2 * dcz + w1 * _shift_up(dcz, dczn, i, last, 1) + w0 * _shift_up(dcz, dczn, i, last, 2)
        outs[0][:, 0:D] = (g * cz).astype(BF16)
        outs[0][:, D:2 * D] = (dz * x).astype(BF16)
        outs[0][:, 2 * D:3 * D] = (dz * c).astype(BF16)
        dw = jnp.concatenate([jnp.sum(dcz * z2, axis=0, keepdims=True), jnp.sum(dcz * z1, axis=0, keepdims=True),
                              jnp.sum(dcz * z, axis=0, keepdims=True)], axis=0)
        _acc_add(i, accs[0], dw)

    return _rows(body, "conv_bwd", S, tm,
                 [("row", dhb, 0, D), ("row", pc, 0, D), ("row", pc, 1, D), ("row", pc, 2, D),
                  ("prev", pc, 1, D), ("prev", pc, 2, D), ("next", dhb, 0, D), ("next", pc, 0, D), ("full", w_conv)],
                 [(3 * D, BF16)], [(3, D)])


def _merge_fwd(y_a, y_b, pg):
    S, D = y_a.shape

    def body(i, ins, outs, accs):
        ya, yb, ga, gb = ins
        outs[0][...] = (_sigmoid(ga[...].astype(F32)) * ya[...] + _sigmoid(gb[...].astype(F32)) * yb[...]).astype(BF16)

    return _rows(body, "merge_fwd", S, _pick(S, (256, 128)),
                 [("row", y_a, 0, D), ("row", y_b, 0, D), ("row", pg, 0, D), ("row", pg, 1, D)], [(D, BF16)])[0]


def _merge_bwd(dm, y_a, y_b, pg):
    S, D = dm.shape

    def body(i, ins, outs, accs):
        d, ya, yb = ins[0][...], ins[1][...], ins[2][...]
        sa, sb = _sigmoid(ins[3][...].astype(F32)), _sigmoid(ins[4][...].astype(F32))
        outs[0][...] = (d * sa).astype(BF16)
        outs[1][...] = (d * sb).astype(BF16)
        outs[2][:, 0:D] = (d * ya * (sa * (1.0 - sa))).astype(BF16)
        outs[2][:, D:2 * D] = (d * yb * (sb * (1.0 - sb))).astype(BF16)

    return _rows(body, "merge_bwd", S, _pick(S, (256, 128)),
                 [("row", dm, 0, D), ("row", y_a, 0, D), ("row", y_b, 0, D), ("row", pg, 0, D), ("row", pg, 1, D)],
                 [(D, BF16), (D, BF16), (2 * D, BF16)])


def _ln1_fwd(x, mix, gate1, g, b, scale2, shift2):
    S, D = x.shape

    def body(i, ins, outs, accs):
        x_ref, mix_ref, gate_ref, g_ref, b_ref, sc_ref, sh_ref = ins
        xh, _ = _ln_stats(ALPHA * x_ref[...] + gate_ref[...] * mix_ref[...])
        x1 = xh * g_ref[...] + b_ref[...]
        outs[0][...] = x1
        outs[1][...] = (x1 * (1.0 + sc_ref[...]) + sh_ref[...]).astype(BF16)

    return _rows(body, "ln1_fwd", S, _pick(S, (256, 128)),
                 [("row", x, 0, D), ("row", mix, 0, D), ("full", gate1), ("full", g), ("full", b),
                  ("full", scale2), ("full", shift2)], [(D, F32), (D, BF16)])


def _swiglu_fwd(hh):
    S, F = hh.shape[0], hh.shape[1] // 2

    def body(i, ins, outs, accs):
        hg = ins[0][...].astype(F32)
        outs[0][...] = (hg * _sigmoid(hg) * ins[1][...].astype(F32)).astype(BF16)

    return _rows(body, "swiglu_fwd", S, _pick(S, (128,)), [("row", hh, 0, F), ("row", hh, 1, F)], [(F, BF16)])[0]


def _swiglu_bwd(dact, hh):
    S, F = dact.shape

    def body(i, ins, outs, accs):
        d, hg, hu = ins[0][...].astype(F32), ins[1][...].astype(F32), ins[2][...].astype(F32)
        sg = _sigmoid(hg)
        outs[0][:, 0:F] = (d * hu * (sg * (1.0 + hg * (1.0 - sg)))).astype(BF16)
        outs[0][:, F:2 * F] = (d * (hg * sg)).astype(BF16)

    return _rows(body, "swiglu_bwd", S, _pick(S, (128,)),
                 [("row", dact, 0, F), ("row", hh, 0, F), ("row", hh, 1, F)], [(2 * F, BF16)])[0]


def _ln2_loss_bwd(x1, ffn, gate2, g, b, target):
    S, D = x1.shape

    def body(i, ins, outs, accs):
        x1_ref, f_ref, gate_ref, g_ref, b_ref, t_ref = ins
        f = f_ref[...]
        xh, rstd = _ln_stats(ALPHA * x1_ref[...] + gate_ref[...] * f)
        e = xh * g_ref[...] + b_ref[...] - t_ref[...]
        dy = e * (1.0 / D)
        dr = _ln_bwd(dy * g_ref[...], xh, rstd)
        outs[0][...] = (gate_ref[...] * dr).astype(BF16)
        outs[1][...] = ALPHA * dr
        _acc_add(i, accs[0], jnp.full((1, 128), (0.5 / D) * jnp.sum(e * e), F32))
        _acc_add(i, accs[1], jnp.sum(dy * xh, axis=0, keepdims=True))
        _acc_add(i, accs[2], jnp.sum(dy, axis=0, keepdims=True))
        _acc_add(i, accs[3], jnp.sum(dr * f, axis=0, keepdims=True))

    return _rows(body, "ln2_loss_bwd", S, _pick(S, (256, 128)),
                 [("row", x1, 0, D), ("row", ffn, 0, D), ("full", gate2), ("full", g), ("full", b), ("row", target, 0, D)],
                 [(D, BF16), (D, F32)], [(1, 128), (1, D), (1, D), (1, D)])


def _ln1_bwd(x, mix, dx1a, du2, gate1, g, b, scale2):
    S, D = x.shape

    def body(i, ins, outs, accs):
        x_ref, mix_ref, da_ref, du_ref, gate_ref, g_ref, b_ref, sc_ref = ins
        mix, du = mix_ref[...], du_ref[...]
        xh, rstd = _ln_stats(ALPHA * x_ref[...] + gate_ref[...] * mix)
        x1 = xh * g_ref[...] + b_ref[...]
        dx1 = da_ref[...] + du * (1.0 + sc_ref[...])
        dr = _ln_bwd(dx1 * g_ref[...], xh, rstd)
        outs[0][...] = (gate_ref[...] * dr).astype(BF16)
        outs[1][...] = ALPHA * dr
        _acc_add(i, accs[0], jnp.sum(du, axis=0, keepdims=True))
        _acc_add(i, accs[1], jnp.sum(du * x1, axis=0, keepdims=True))
        _acc_add(i, accs[2], jnp.sum(dx1 * xh, axis=0, keepdims=True))
        _acc_add(i, accs[3], jnp.sum(dx1, axis=0, keepdims=True))
        _acc_add(i, accs[4], jnp.sum(dr * mix, axis=0, keepdims=True))

    return _rows(body, "ln1_bwd", S, _pick(S, (256, 128)),
                 [("row", x, 0, D), ("row", mix, 0, D), ("row", dx1a, 0, D), ("row", du2, 0, D),
                  ("full", gate1), ("full", g), ("full", b), ("full", scale2)],
                 [(D, BF16), (D, F32)], [(1, D)] * 5)


def _rms_bwd(d_rq, d_rkv, pq, dkr, g_q, g_kv):
    S = pq.shape[0]

    def body(i, ins, outs, accs):
        dq_ref, dkv_ref, pq_ref, dkr_ref, gq_ref, gkv_ref = ins

        def rms_bwd(dy, x, g):
            r = lax.rsqrt(jnp.mean(x * x, axis=-1, keepdims=True) + RMS_EPS)
            dyg = dy * g
            dx = r * dyg - x * (r * r * r) * jnp.mean(dyg * x, axis=-1, keepdims=True)
            return dx, jnp.sum(dy * (x * r), axis=0, keepdims=True)

        dxq, dgq = rms_bwd(dq_ref[...], pq_ref[:, 0:Q_LORA], gq_ref[...])
        dxkv, dgkv = rms_bwd(dkv_ref[...], pq_ref[:, Q_LORA:Q_LORA + KV_LORA], gkv_ref[...])
        outs[0][:, 0:Q_LORA] = dxq.astype(BF16)
        outs[0][:, Q_LORA:Q_LORA + KV_LORA] = dxkv.astype(BF16)
        outs[0][:, Q_LORA + KV_LORA:QKV_A] = dkr_ref[...].astype(BF16)
        _acc_add(i, accs[0], dgq)
        _acc_add(i, accs[1], dgkv)

    return _rows(body, "rms_bwd", S, _pick(S, (256, 128)),
                 [("row", d_rq, 0, Q_LORA), ("row", d_rkv, 0, KV_LORA), ("row", pq, 0, QKV_A), ("row", dkr, 0, 128),
                  ("full", g_q), ("full", g_kv)], [(QKV_A, BF16)], [(1, Q_LORA), (1, KV_LORA)])


def _dx_final(dxa, du, x, scale1):
    S, D = x.shape

    def body(i, ins, outs, accs):
        du = ins[1][...]
        outs[0][...] = ins[0][...] + du * (1.0 + ins[3][...])
        _acc_add(i, accs[0], jnp.sum(du, axis=0, keepdims=True))
        _acc_add(i, accs[1], jnp.sum(du * ins[2][...], axis=0, keepdims=True))

    return _rows(body, "dx_final", S, _pick(S, (256, 128)),
                 [("row", dxa, 0, D), ("row", du, 0, D), ("row", x, 0, D), ("full", scale1)],
                 [(D, F32)], [(1, D), (1, D)])


def _ada_fwd(c_all, w, bias):
    B, D = c_all.shape
    NA = w.shape[1]
    tn = _pick(NA, (512, 256, 128))

    def body(c_ref, w_ref, b_ref, o_ref):
        cv = c_ref[...]
        ca = (cv * _sigmoid(cv)).astype(BF16)
        o_ref[...] = jnp.dot(ca, w_ref[...].astype(BF16), preferred_element_type=F32) + b_ref[...]

    return pl.pallas_call(
        body, name="ada_fwd", grid=(NA // tn,),
        in_specs=[pl.BlockSpec((B, D), lambda j: (0, 0)), pl.BlockSpec((D, tn), lambda j: (0, j)),
                  pl.BlockSpec((1, tn), lambda j: (0, j))],
        out_specs=pl.BlockSpec((B, tn), lambda j: (0, j)),
        out_shape=jax.ShapeDtypeStruct((B, NA), F32),
        compiler_params=_params(("arbitrary",)),
    )(c_all, w, bias)


def _ada_bwd(c_all, dmod):
    B, D = c_all.shape
    NA = dmod.shape[1]
    tn = _pick(NA, (512, 256, 128))

    def body(c_ref, d_ref, o_ref):
        cv = c_ref[...]
        ca = (cv * _sigmoid(cv)).astype(BF16)
        o_ref[...] = lax.dot_general(ca, d_ref[...].astype(BF16), TN, preferred_element_type=F32)

    return pl.pallas_call(
        body, name="ada_bwd", grid=(NA // tn,),
        in_specs=[pl.BlockSpec((B, D), lambda j: (0, 0)), pl.BlockSpec((B, tn), lambda j: (0, j))],
        out_specs=pl.BlockSpec((D, tn), lambda j: (0, j)),
        out_shape=jax.ShapeDtypeStruct((D, NA), F32),
        compiler_params=_params(("arbitrary",)),
    )(c_all, dmod)


def _sum8(parts):
    _, R, N = parts.shape

    def body(p_ref, o_ref):
        acc = p_ref[0]
        for d in range(1, 8):
            acc = acc + p_ref[d]
        o_ref[...] = acc

    return pl.pallas_call(body, name="sum8", out_shape=jax.ShapeDtypeStruct((R, N), F32),
                          compiler_params=_params())(parts)


def _adam_math(w, g, m, v):
    m = ADAM_B1 * m + (1.0 - ADAM_B1) * g
    v = ADAM_B2 * v + (1.0 - ADAM_B2) * (g * g)
    delta = -ADAM_LR * ((m / ADAM_C1) / (jnp.sqrt(v / ADAM_C2) + ADAM_EPS) + ADAM_WD * w)
    return delta, m, v


def _adam(name, w, m, v, g, carry=None):
    R, C = w.shape
    tm = _row_tile(R, C * 4, 1 << 20)
    steps = R // tm
    n_ci = len(carry.ins) if carry else 0
    n_co = len(carry.outs) if carry else 0

    def body(*refs):
        w_ref, m_ref, v_ref, g_ref = refs[:4]
        d_ref, nm_ref, nv_ref = refs[4 + n_ci:7 + n_ci]
        c_ins, c_outs, c_sems = refs[4:4 + n_ci], refs[7 + n_ci:7 + n_ci + n_co], refs[7 + n_ci + n_co:]
        if carry:
            @pl.when(pl.program_id(0) == 0)
            def _():
                carry.start(c_ins, c_outs, c_sems)

        delta, nm, nv = _adam_math(w_ref[...], g_ref[...], m_ref[...], v_ref[...])
        d_ref[...] = delta
        nm_ref[...] = nm
        nv_ref[...] = nv
        if carry:
            @pl.when(pl.program_id(0) == steps - 1)
            def _():
                carry.finish(c_ins, c_outs, c_sems)

    spec = pl.BlockSpec((tm, C), lambda i: (i, 0))
    res = pl.pallas_call(
        body, name=name, grid=(steps,), in_specs=[spec] * 4 + [ANY] * n_ci, out_specs=[spec] * 3 + [ANY] * n_co,
        out_shape=[jax.ShapeDtypeStruct((R, C), F32)] * 3 + (carry.outs if carry else []),
        scratch_shapes=carry.sems if carry else [],
        input_output_aliases=carry.io_aliases(4, 3) if carry else {},
        compiler_params=_params(("arbitrary",)),
    )(w, m, v, g, *(carry.ins if carry else []))
    return (res[:3], res[3:]) if carry else res


def _adam_halves(name, w, m, v, mine, other, core, carry=None):
    R, C = w.shape
    Rh = mine.shape[0]
    tc = max(t for t in range(128, C + 1, 128) if C % t == 0 and R * t <= (3 << 17))
    steps = C // tc
    n_ci = len(carry.ins) if carry else 0
    n_co = len(carry.outs) if carry else 0

    def body(*refs):
        c_ref, w_ref, m_ref, v_ref, a_ref, b_ref = refs[:6]
        g_ref, d_ref, nm_ref, nv_ref = refs[6 + n_ci:10 + n_ci]
        c_ins, c_outs, c_sems = refs[6:6 + n_ci], refs[10 + n_ci:10 + n_ci + n_co], refs[10 + n_ci + n_co:]
        if carry:
            @pl.when(pl.program_id(0) == 0)
            def _():
                carry.start(c_ins, c_outs, c_sems)

        first = c_ref[0] == 0
        g = jnp.concatenate([jnp.where(first, a_ref[...], b_ref[...]),
                             jnp.where(first, b_ref[0:R - Rh, :], a_ref[0:R - Rh, :])], axis=0)
        delta, nm, nv = _adam_math(w_ref[...], g, m_ref[...], v_ref[...])
        g_ref[...] = g
        d_ref[...] = delta
        nm_ref[...] = nm
        nv_ref[...] = nv
        if carry:
            @pl.when(pl.program_id(0) == steps - 1)
            def _():
                carry.finish(c_ins, c_outs, c_sems)

    spec = pl.BlockSpec((R, tc), lambda i, c_ref: (0, i))
    h_spec = pl.BlockSpec((Rh, tc), lambda i, c_ref: (0, i))
    res = pl.pallas_call(
        body, name=name, out_shape=[jax.ShapeDtypeStruct((R, C), F32)] * 4 + (carry.outs if carry else []),
        grid_spec=pltpu.PrefetchScalarGridSpec(
            num_scalar_prefetch=1, grid=(steps,), in_specs=[spec, spec, spec, h_spec, h_spec] + [ANY] * n_ci,
            out_specs=[spec] * 4 + [ANY] * n_co, scratch_shapes=carry.sems if carry else []),
        input_output_aliases=carry.io_aliases(6, 4) if carry else {},
        compiler_params=_params(("arbitrary",)),
    )(core, w, m, v, mine, other, *(carry.ins if carry else []))
    return (res[:4], res[4:]) if carry else res


def _adam_small(name, w, m, v, g):
    def body(w_ref, m_ref, v_ref, g_ref, d_ref, nm_ref, nv_ref):
        delta, nm, nv = _adam_math(w_ref[...], g_ref[...], m_ref[...], v_ref[...])
        d_ref[...] = delta
        nm_ref[...] = nm
        nv_ref[...] = nv

    return pl.pallas_call(body, name=name, out_shape=[jax.ShapeDtypeStruct(w.shape, F32)] * 3,
                          compiler_params=_params())(w, m, v, g)


def _place():
    return lax.axis_index("x"), lax.axis_index("y"), lax.axis_index("c")


def _other_chips(x, y):
    return [(1 - x, y), (x, 1 - y), (1 - x, 1 - y)]


def _all_gather8(blk, name):
    R, N = blk.shape

    def body(x_ref, out_ref, send_sems, recv_sems, local_sem):
        x, y, c = _place()
        me = 4 * x + 2 * y + c
        mine = pltpu.make_async_copy(x_ref, out_ref.at[me], local_sem)
        mine.start()
        flips = [(j >> 2 & 1, j >> 1 & 1, j & 1) for j in range(1, 8)]
        peers = [((1 - x) if fx else x, (1 - y) if fy else y, (1 - c) if fc else c) for fx, fy, fc in flips]
        sends = []
        for j, peer in enumerate(peers):
            cp = pltpu.make_async_remote_copy(src_ref=x_ref, dst_ref=out_ref.at[me], send_sem=send_sems.at[j],
                                              recv_sem=recv_sems.at[j], device_id=peer, device_id_type=MESH)
            cp.start()
            sends.append(cp)
        for j, (px, py, pc) in enumerate(peers):
            pltpu.make_async_remote_copy(src_ref=x_ref, dst_ref=out_ref.at[4 * px + 2 * py + pc],
                                         send_sem=send_sems.at[j], recv_sem=recv_sems.at[j],
                                         device_id=(px, py, pc), device_id_type=MESH).wait_recv()
        for cp in sends:
            cp.wait_send()
        mine.wait()

    return pl.pallas_call(
        body, name=name, out_shape=jax.ShapeDtypeStruct((8, R, N), F32),
        in_specs=[pl.BlockSpec(memory_space=pltpu.VMEM)], out_specs=pl.BlockSpec(memory_space=pltpu.VMEM),
        scratch_shapes=[pltpu.SemaphoreType.DMA((7,)), pltpu.SemaphoreType.DMA((7,)), pltpu.SemaphoreType.DMA],
        compiler_params=_params(),
    )(blk)


def _piece(rows, piece):
    i, n, k = piece if len(piece) == 3 else (piece[0], piece[1], 1)
    assert rows % 16 == 0 and rows // 16 >= n, (rows, piece)
    lo, hi = (rows // 16 * i // n) * 16, (rows // 16 * (i + k) // n) * 16
    return pl.ds(lo, hi - lo)


def _scatter_plan(arrs, piece=(0, 1), into=None):
    n = len(arrs)

    def copies(ins, outs, sems):
        send_sems, recv_sems = sems
        x, y, c = _place()
        chips = _other_chips(x, y)
        cps = []
        for k in range(n):
            rows = _piece(arrs[k].shape[1], piece)
            for j, (px, py) in enumerate(chips):
                cps.append(pltpu.make_async_remote_copy(
                    src_ref=ins[k].at[2 * px + py, rows], dst_ref=outs[k].at[j, rows],
                    send_sem=send_sems.at[3 * k + j], recv_sem=recv_sems.at[3 * k + j],
                    device_id=(px, py, c), device_id_type=MESH))
        return cps

    def start(ins, outs, sems):
        for cp in copies(ins, outs, sems):
            cp.start()

    def finish(ins, outs, sems):
        for cp in copies(ins, outs, sems):
            cp.wait()

    return _Plan(list(arrs) + list(into or []), [jax.ShapeDtypeStruct((3,) + a.shape[1:], a.dtype) for a in arrs],
                 [pltpu.SemaphoreType.DMA((3 * n,))] * 2, start, finish,
                 aliases={n + k: k for k in range(n)} if into else None)


def _gather_plan(shards, piece=(0, 1), into=None):
    n = len(shards)

    def parts(ins, outs, sems):
        s1, r1, s2, r2, loc = sems
        x, y, c = _place()
        me = 2 * x + y
        chips = _other_chips(x, y)
        sib = (x, y, 1 - c)

        def rows(k):
            return _piece(shards[k].shape[1], piece)

        def ici(k, j, slab, to):
            return pltpu.make_async_remote_copy(src_ref=ins[k].at[c, rows(k)], dst_ref=outs[k].at[slab, c, rows(k)],
                                                send_sem=s1.at[3 * k + j], recv_sem=r1.at[3 * k + j],
                                                device_id=to, device_id_type=MESH)

        def d2d(k, j, slab, half):
            return pltpu.make_async_remote_copy(src_ref=outs[k].at[slab, half, rows(k)],
                                                dst_ref=outs[k].at[slab, half, rows(k)],
                                                send_sem=s2.at[3 * k + j], recv_sem=r2.at[3 * k + j],
                                                device_id=sib, device_id_type=MESH)

        def own(k):
            return pltpu.make_async_remote_copy(src_ref=ins[k].at[:, rows(k)], dst_ref=outs[k].at[me, :, rows(k)],
                                                send_sem=loc.at[2 * k], recv_sem=loc.at[2 * k + 1],
                                                device_id=sib, device_id_type=MESH)

        return c, me, chips, ici, d2d, own

    def start(ins, outs, sems):
        c, me, chips, ici, d2d, own = parts(ins, outs, sems)
        for k in range(n):
            for j, (px, py) in enumerate(chips):
                ici(k, j, me, (px, py, c)).start()
        for k in range(n):
            own(k).start()

    def finish(ins, outs, sems):
        c, me, chips, ici, d2d, own = parts(ins, outs, sems)
        for k in range(n):
            for j, (px, py) in enumerate(chips):
                ici(k, j, 2 * px + py, (px, py, c)).wait_recv()
                d2d(k, j, 2 * px + py, c).start()
        for k in range(n):
            for j, (px, py) in enumerate(chips):
                d2d(k, j, 2 * px + py, 1 - c).wait_recv()
        for k in range(n):
            own(k).wait()
            for j, (px, py) in enumerate(chips):
                ici(k, j, me, (px, py, c)).wait_send()
                d2d(k, j, 2 * px + py, c).wait_send()

    return _Plan(list(shards) + list(into or []), [jax.ShapeDtypeStruct((4,) + a.shape, a.dtype) for a in shards],
                 [pltpu.SemaphoreType.DMA((3 * n,))] * 4 + [pltpu.SemaphoreType.DMA((2 * n,))], start, finish,
                 aliases={n + k: k for k in range(n)} if into else None)


def _pair_plan(parts):
    n = len(parts)

    def copies(ins, outs, sems):
        send_sems, recv_sems = sems
        x, y, c = _place()
        return [pltpu.make_async_remote_copy(src_ref=ins[k].at[p, 1 - c], dst_ref=outs[k].at[p],
                                             send_sem=send_sems.at[4 * k + p], recv_sem=recv_sems.at[4 * k + p],
                                             device_id=(x, y, 1 - c), device_id_type=MESH)
                for k in range(n) for p in range(4)]

    def start(ins, outs, sems):
        for cp in copies(ins, outs, sems):
            cp.start()

    def finish(ins, outs, sems):
        for cp in copies(ins, outs, sems):
            cp.wait()

    return _Plan(parts, [jax.ShapeDtypeStruct((4,) + a.shape[2:], a.dtype) for a in parts],
                 [pltpu.SemaphoreType.DMA((4 * n,))] * 2, start, finish)


def _sibling_plan(arrs):
    n = len(arrs)

    def copies(ins, outs, sems):
        send_sems, recv_sems = sems
        x, y, c = _place()
        return [pltpu.make_async_remote_copy(src_ref=ins[k], dst_ref=outs[k], send_sem=send_sems.at[k],
                                             recv_sem=recv_sems.at[k], device_id=(x, y, 1 - c), device_id_type=MESH)
                for k in range(n)]

    def start(ins, outs, sems):
        for cp in copies(ins, outs, sems):
            cp.start()

    def finish(ins, outs, sems):
        for cp in copies(ins, outs, sems):
            cp.wait()

    return _Plan(arrs, [jax.ShapeDtypeStruct(a.shape, a.dtype) for a in arrs],
                 [pltpu.SemaphoreType.DMA((n,))] * 2, start, finish)


def _join_plans(plans):
    def split(seq, counts):
        out, at = [], 0
        for cnt in counts:
            out.append(seq[at:at + cnt])
            at += cnt
        return out

    n_i, n_o, n_s = ([len(getattr(p, f)) for p in plans] for f in ("ins", "outs", "sems"))

    def start(ins, outs, sems):
        for p, i, o, s in zip(plans, split(ins, n_i), split(outs, n_o), split(sems, n_s)):
            p.start(i, o, s)

    def finish(ins, outs, sems):
        for p, i, o, s in zip(plans, split(ins, n_i), split(outs, n_o), split(sems, n_s)):
            p.finish(i, o, s)

    aliases, at_i, at_o = {}, 0, 0
    for p in plans:
        aliases.update(p.io_aliases(at_i, at_o))
        at_i, at_o = at_i + len(p.ins), at_o + len(p.outs)
    return _Plan(sum((p.ins for p in plans), []), sum((p.outs for p in plans), []), sum((p.sems for p in plans), []),
                 start, finish, aliases)


def _add_pair(parts, sib, core, name):
    P4, _, Rh, C = parts.shape
    tm, tc = _tile2(Rh, C, 16)

    def body(c_ref, a_ref, b_ref, o_ref):
        o_ref[...] = (a_ref[0].astype(F32) + b_ref[...].astype(F32)).astype(BF16)

    spec = pl.BlockSpec((1, tm, tc), lambda p, i, j, c_ref: (p, i, j))
    return pl.pallas_call(
        body, name=name, out_shape=jax.ShapeDtypeStruct((P4, Rh, C), BF16),
        grid_spec=pltpu.PrefetchScalarGridSpec(
            num_scalar_prefetch=1, grid=(P4, Rh // tm, C // tc),
            in_specs=[pl.BlockSpec((1, 1, tm, tc), lambda p, i, j, c_ref: (p, c_ref[0], i, j)), spec], out_specs=spec),
        compiler_params=_params(("parallel",) * 3),
    )(core, parts, sib)


def _sum_slabs(pre, recv, chip, name):
    _, Rh, C = pre.shape
    tm, tc = _tile2(Rh, C, 16)

    def body(me_ref, own_ref, r_ref, o_ref):
        acc = own_ref[0].astype(F32)
        for j in range(3):
            acc = acc + r_ref[j].astype(F32)
        o_ref[...] = acc

    return pl.pallas_call(
        body, name=name, out_shape=jax.ShapeDtypeStruct((Rh, C), F32),
        grid_spec=pltpu.PrefetchScalarGridSpec(
            num_scalar_prefetch=1, grid=(Rh // tm, C // tc),
            in_specs=[pl.BlockSpec((1, tm, tc), lambda i, j, me_ref: (me_ref[0], i, j)),
                      pl.BlockSpec((3, tm, tc), lambda i, j, me_ref: (0, i, j))],
            out_specs=pl.BlockSpec((tm, tc), lambda i, j, me_ref: (i, j))),
        compiler_params=_params(("parallel", "parallel")),
    )(chip, pre, recv)


def kernel(x, c, positions, w_ada, b_ada, w_in, g_q_a, w_q_b, g_kv_a, w_kv_b, w_o_a, w_conv, w_o_b, w_o, ln1_g, ln1_b, w_ffn_in, w_ffn_out, ln2_g, ln2_b, loss_target, m_w_ada, m_b_ada, m_w_in, m_g_q_a, m_w_q_b, m_g_kv_a, m_w_kv_b, m_w_o_a, m_w_conv, m_w_o_b, m_w_o, m_ln1_g, m_ln1_b, m_w_ffn_in, m_w_ffn_out, m_ln2_g, m_ln2_b, v_w_ada, v_b_ada, v_w_in, v_g_q_a, v_w_q_b, v_g_kv_a, v_w_kv_b, v_w_o_a, v_w_conv, v_w_o_b, v_w_o, v_ln1_g, v_ln1_b, v_w_ffn_in, v_w_ffn_out, v_ln2_g, v_ln2_b):
    S, D = x.shape[1], x.shape[2]
    F = w_ffn_out.shape[1] * 4
    ax, ay, ac = _place()
    chip = 2 * ax + ay
    dev = 4 * ax + 2 * ay + ac
    x2, tgt = x[0], loss_target[0]
    w_ada2, w_in2, w_q_b2, w_kv_b2 = w_ada[0], w_in[0], w_q_b[0], w_kv_b[0]
    w_o_a2, w_o_b2, w_o2, w_ffn_in2, w_ffn_out2 = w_o_a[0], w_o_b[0], w_o[0], w_ffn_in[0], w_ffn_out[0]
    NA = w_ada2.shape[1]
    CW = w_conv.shape[2]

    inv_freq = 1.0 / (ROPE_THETA ** (jnp.arange(0, QK_ROPE, 2, dtype=F32) / QK_ROPE))
    ang = positions[0].astype(F32)[:, None] * inv_freq
    cos, sin = jnp.cos(ang), jnp.sin(ang)
    z32, z64, z96 = jnp.zeros((S, 32), F32), jnp.zeros((S, 64), F32), jnp.zeros((S, 96), F32)
    tab = jnp.concatenate([cos, cos, z64, -sin, z96, z32, sin, z64], axis=1)

    def halves(a):
        return a.reshape(2, a.shape[0] // 2, a.shape[1])

    def whole(g):
        return g.reshape(4, 2 * g.shape[2], g.shape[3])

    def cols(g):
        return jnp.transpose(g, (1, 0, 2)).reshape(g.shape[1], 4 * g.shape[2])

    w_inT, m_w_inT, v_w_inT = w_in2.T, m_w_in[0].T, v_w_in[0].T
    CS = w_inT.shape[0]
    CSP = -(-CS // 32) * 32
    sh_in = halves(jnp.pad(w_inT.astype(BF16), ((0, CSP - CS), (0, 0))))
    sh_qb, sh_kvb, sh_oa, sh_ob, sh_o, sh_fi, sh_fo = (
        halves(w.astype(BF16)) for w in (w_q_b2, w_kv_b2, w_o_a2, w_o_b2, w_o2, w_ffn_in2, w_ffn_out2))
    g_in = whole(_run_plan(_gather_plan([sh_in]), "gather_first")[0])

    def in_rows(lo, hi):
        parts = [g_in[p, max(lo, p * CS) - p * CS:min(hi, (p + 1) * CS) - p * CS]
                 for p in range(4) if max(lo, p * CS) < min(hi, (p + 1) * CS)]
        return parts[0] if len(parts) == 1 else jnp.concatenate(parts, axis=0)

    n_qkv = Q_LORA + KV_LORA + QK_ROPE
    W_qkvT = jnp.pad(in_rows(0, n_qkv), ((0, QKV_A - n_qkv), (0, 0)))
    W_convT = in_rows(n_qkv, n_qkv + 3 * D)
    W_gateT = in_rows(n_qkv + 3 * D, n_qkv + 5 * D)

    c_all = _all_gather8(c, "gather_c").reshape(8, D)
    wconv_all = _all_gather8(w_conv[0], "gather_wconv")
    w_conv_full = jnp.transpose(wconv_all[0::2], (1, 0, 2)).reshape(3, D)
    b_sh = lax.dynamic_slice(b_ada, (0, chip * NA), (1, NA))
    mod_sh = _ada_fwd(c_all, w_ada2, b_sh)
    mod_all = _all_gather8(mod_sh, "gather_mod")
    mod = lax.dynamic_slice(mod_all[0::2], (0, dev, 0), (4, 1, NA)).reshape(6, D)
    shift1, scale1, gate1, shift2, scale2, gate2 = (mod[k:k + 1] for k in range(6))

    u = _modulate(x2, scale1, shift1, "modulate1")
    pq, (g_qb, g_kvb) = _matmul(u, W_qkvT, "nt", F32, "proj_qkv", carry=_gather_plan([sh_qb, sh_kvb]))
    W_qb = jnp.pad(cols(whole(g_qb)).reshape(Q_LORA, N_HEADS, QK_NOPE + QK_ROPE),
                   ((0, 0), (0, 0), (0, QK_PAD - QK_NOPE - QK_ROPE))).reshape(Q_LORA, N_HEADS * QK_PAD)
    W_kvb = cols(whole(g_kvb))
    pc, (g_oa, g_ob) = _matmul(u, W_convT, "nt", F32, "proj_conv", carry=_gather_plan([sh_oa, sh_ob]))
    pg, (g_o,) = _matmul(u, W_gateT, "nt", BF16, "proj_gate", carry=_gather_plan([sh_o]))
    W_oa, W_ob, W_o = (g.reshape(-1, D) for g in (g_oa, g_ob, g_o))
    rq, rkv, kr = _rms_fwd(pq, tab, g_q_a, g_kv_a)
    q = _q_rope(_matmul(rq, W_qb, "nn", F32, "q_b"), tab)
    kv = _matmul(rkv, W_kvb, "nn", BF16, "kv_b")
    o, lse, g_fi = _attn_fwd(q, kv, kr, carry=_gather_plan([sh_fi], (0, 8, 6)))
    y_a, g_fi = _matmul(o, W_oa, "nn", F32, "o_a", carry=_gather_plan([sh_fi], (6, 8), g_fi))
    hb = _conv_fwd(pc, w_conv_full)
    y_b = _matmul(hb, W_ob, "nn", F32, "o_b")
    merged = _merge_fwd(y_a, y_b, pg)
    mix, g_fi = _matmul(merged, W_o, "nn", F32, "w_o", carry=_gather_plan([sh_fi], (7, 8), g_fi))
    W_fi = whole(g_fi[0])
    x1, u2 = _ln1_fwd(x2, mix, gate1, ln1_g, ln1_b, scale2, shift2)
    hh, (g_fo,) = _matmul(u2, W_fi, "nn", BF16, "ffn_in", carry=_gather_plan([sh_fo]), shards="b")
    W_fo = g_fo.reshape(F, D)
    act = _swiglu_fwd(hh)
    ffn = _matmul(act, W_fo, "nn", F32, "ffn_out")

    core_i = ac.astype(jnp.int32).reshape(1)
    chip_i = chip.astype(jnp.int32).reshape(1)

    def uncols(g):
        return jnp.transpose(g.reshape(g.shape[0], 4, g.shape[1] // 4), (1, 0, 2))

    def slabs(p):
        return p.reshape(4, 2, p.shape[1] // 2, p.shape[2])

    def add_pairs(parts, sibs, nms):
        return [_add_pair(a, b, core_i, "add_pair_" + nm) for a, b, nm in zip(parts, sibs, nms)]

    def sum_all(pre, recv, nms):
        return [_sum_slabs(a, r, chip_i, "sum_slabs_" + nm) for a, r, nm in zip(pre, recv, nms)]

    dffn, dx1a, loss_acc, d_ln2_g, d_ln2_b, d_gate2 = _ln2_loss_bwd(x1, ffn, gate2, ln2_g, ln2_b, tgt)
    loss = lax.psum(loss_acc[0, 0], ("x", "y", "c"))
    dW_fo = _matmul(act, dffn, "tn", BF16, "d_w_ffn_out")
    p_fo = [slabs(dW_fo.reshape(4, -1, D))]
    dact, s_fo = _matmul(dffn, W_fo, "nt", BF16, "d_act", carry=_pair_plan(p_fo))
    pre_fo = add_pairs(p_fo, s_fo, ["w_ffn_out"])
    dhh = _swiglu_bwd(dact, hh)
    dW_fi, r_fo = _matmul(u2, dhh, "tn", BF16, "d_w_ffn_in", carry=_scatter_plan(pre_fo), shards="o")
    p_fi = [slabs(dW_fi)]
    du2, s_fi = _matmul(dhh, W_fi, "nt", F32, "d_u2", carry=_pair_plan(p_fi), shards="b")
    pre_fi = add_pairs(p_fi, s_fi, ["w_ffn_in"])
    dmix, dxa, d_shift2, d_scale2, d_ln1_g, d_ln1_b, d_gate1 = _ln1_bwd(x2, mix, dx1a, du2, gate1, ln1_g, ln1_b, scale2)
    dW_o = _matmul(merged, dmix, "tn", BF16, "d_w_o")
    dmerged = _matmul(dmix, W_o, "nt", F32, "d_merged")
    dy_a, dy_b, dgate = _merge_bwd(dmerged, y_a, y_b, pg)
    dW_oa = _matmul(o, dy_a, "tn", BF16, "d_w_o_a")
    do = _matmul(dy_a, W_oa, "nt", BF16, "d_o")
    dW_ob = _matmul(hb, dy_b, "tn", BF16, "d_w_o_b")
    p_mid = [slabs(g.reshape(4, -1, D)) for g in (dW_oa, dW_ob, dW_o)]
    dhb, s_mid = _matmul(dy_b, W_ob, "nt", F32, "d_hb", carry=_pair_plan(p_mid))
    pre_mid = add_pairs(p_mid, s_mid, ["w_o_a", "w_o_b", "w_o"])
    dconv, d_wconv = _conv_bwd(dhb, pc, w_conv_full)
    dq, dkv, dkr, r_fi = _attn_bwd(q, kv, kr, do, o, lse, tab, carry=_scatter_plan(pre_fi))
    names_a = ["w_ffn_out", "w_ffn_in", "w_o_a", "w_o_b", "w_o"]
    dW_qb = _matmul(rq, dq, "tn", BF16, "d_w_q_b")
    d_rq = _matmul(dq, W_qb, "nt", F32, "d_rq")
    dW_kvb = _matmul(rkv, dkv, "tn", BF16, "d_w_kv_b")
    d_rkv = _matmul(dkv, W_kvb, "nt", F32, "d_rkv")
    dqkv, d_g_q, d_g_kv = _rms_bwd(d_rq, d_rkv, pq, dkr, g_q_a, g_kv_a)
    dW_qkvT = _matmul(dqkv, u, "tn", BF16, "d_w_qkv")
    dW_convT, r_mid = _matmul(dconv, u, "tn", BF16, "d_w_conv", carry=_scatter_plan(pre_mid, (0, 2)))
    dW_gateT, r_mid = _matmul(dgate, u, "tn", BF16, "d_w_gate", carry=_scatter_plan(pre_mid, (1, 2), r_mid))
    fin_a = sum_all(pre_fo + pre_fi + pre_mid, list(r_fo) + list(r_fi) + list(r_mid), names_a)
    dW_inT = jnp.concatenate([dW_qkvT[:n_qkv], dW_convT, dW_gateT], axis=0).reshape(4, CS, D)
    dW_inT = jnp.pad(dW_inT, ((0, 0), (0, CSP - CS), (0, 0)))
    dW_qb_u = dW_qb.reshape(Q_LORA, N_HEADS, QK_PAD)[:, :, :QK_NOPE + QK_ROPE].reshape(Q_LORA, -1)
    names_b = ["w_in", "w_q_b", "w_kv_b"]
    p_b = [slabs(dW_inT), slabs(uncols(dW_qb_u)), slabs(uncols(dW_kvb))]
    du, s_b = _matmul(dqkv, W_qkvT, "nn", F32, "d_u_qkv", carry=_pair_plan(p_b))
    pre_b = add_pairs(p_b, s_b, names_b)
    du, moved = _matmul(dconv, W_convT, "nn", F32, "d_u_conv", add=du,
                        carry=_join_plans([_scatter_plan(pre_b, (0, 4)), _sibling_plan(fin_a)]))
    r_b, fs_a = moved[:3], moved[3:]
    du, r_b = _matmul(dgate, W_gateT, "nn", F32, "d_u_gate", add=du, carry=_scatter_plan(pre_b, (1, 4), r_b))
    grad_x, d_shift1, d_scale1 = _dx_final(dxa, du, x2, scale1)

    def pad_d(v):
        return jnp.pad(v, ((0, 0), (0, D - v.shape[1])))

    small = jnp.concatenate([d_ln1_g, d_ln1_b, d_ln2_g, d_ln2_b, pad_d(d_g_q), pad_d(d_g_kv), d_wconv,
                             d_shift1, d_scale1, d_gate1, d_shift2, d_scale2, d_gate2, jnp.zeros((1, D), F32)], axis=0)
    small_all = _all_gather8(small, "gather_small")
    small_sum = _sum8(small_all)
    g_ln1_g, g_ln1_b, g_ln2_g, g_ln2_b = (small_sum[k:k + 1] for k in range(4))
    g_g_q, g_g_kv = small_sum[4:5, :Q_LORA], small_sum[5:6, :KV_LORA]
    g_wconv = lax.dynamic_slice(small_sum[6:9], (0, chip * CW), (3, CW))
    g_b_ada = small_sum[9:15].reshape(1, 6 * D)
    dmod_all = small_all[:, 9:15, :].reshape(8, 6 * D)
    g_w_ada = _ada_bwd(c_all, lax.dynamic_slice(dmod_all, (0, chip * NA), (8, NA)))

    big = {}
    ws = dict(w_in=(w_inT, m_w_inT, v_w_inT), w_q_b=(w_q_b2, m_w_q_b[0], v_w_q_b[0]),
              w_kv_b=(w_kv_b2, m_w_kv_b[0], v_w_kv_b[0]), w_o_a=(w_o_a2, m_w_o_a[0], v_w_o_a[0]),
              w_o_b=(w_o_b2, m_w_o_b[0], v_w_o_b[0]), w_o=(w_o2, m_w_o[0], v_w_o[0]),
              w_ffn_in=(w_ffn_in2, m_w_ffn_in[0], v_w_ffn_in[0]), w_ffn_out=(w_ffn_out2, m_w_ffn_out[0], v_w_ffn_out[0]))
    def adam_of(nm, a, b, carry=None):
        w_, m_, v_ = ws[nm]
        return _adam_halves("adam_" + nm, w_, m_, v_, a, b, core_i, carry)

    big["w_ffn_in"], r_b = adam_of("w_ffn_in", fin_a[1], fs_a[1], _scatter_plan(pre_b, (2, 4), r_b))
    upd, r_b = _adam("adam_w_ada", w_ada2, m_w_ada[0], v_w_ada[0], g_w_ada, _scatter_plan(pre_b, (3, 4), r_b))
    big["w_ada"] = [g_w_ada] + list(upd)
    fin_b = sum_all(pre_b, r_b, names_b)
    fs_b = _run_plan(_sibling_plan(fin_b), "sibling_last")
    for nm, a, b in zip(names_a + names_b, fin_a + fin_b, list(fs_a) + list(fs_b)):
        if nm != "w_ffn_in":
            big[nm] = adam_of(nm, a, b)
    sm = {}
    for nm, w_, m_, v_, g_ in [("b_ada", b_ada, m_b_ada, v_b_ada, g_b_ada), ("g_q_a", g_q_a, m_g_q_a, v_g_q_a, g_g_q),
                               ("g_kv_a", g_kv_a, m_g_kv_a, v_g_kv_a, g_g_kv),
                               ("w_conv", w_conv[0], m_w_conv[0], v_w_conv[0], g_wconv),
                               ("ln1_g", ln1_g, m_ln1_g, v_ln1_g, g_ln1_g), ("ln1_b", ln1_b, m_ln1_b, v_ln1_b, g_ln1_b),
                               ("ln2_g", ln2_g, m_ln2_g, v_ln2_g, g_ln2_g), ("ln2_b", ln2_b, m_ln2_b, v_ln2_b, g_ln2_b)]:
        sm[nm] = (g_,) + tuple(_adam_small("adam_" + nm, w_, m_, v_, g_))

    order = ["w_ada", "b_ada", "w_in", "g_q_a", "w_q_b", "g_kv_a", "w_kv_b", "w_o_a", "w_conv", "w_o_b", "w_o",
             "ln1_g", "ln1_b", "w_ffn_in", "w_ffn_out", "ln2_g", "ln2_b"]
    lead = {"b_ada", "g_q_a", "g_kv_a", "ln1_g", "ln1_b", "ln2_g", "ln2_b"}

    def leaf(nm, k):
        val = big[nm][k] if nm in big else sm[nm][k]
        if nm == "w_in":
            val = val.T
        return val if nm in lead else val[None]

    outs = [loss, grad_x[None]]
    for k in range(4):
        outs += [leaf(nm, k) for nm in order]
    return tuple(outs)
```

```python
import functools

import jax
import jax.numpy as jnp
from jax import lax
from jax.experimental import pallas as pl
from jax.experimental.pallas import tpu as pltpu

F32, BF16 = jnp.float32, jnp.bfloat16
N_HEADS, QK_NOPE, QK_ROPE, V_HEAD = 16, 128, 64, 128
Q_LORA, KV_LORA = 512, 512
QK_PAD = 256
QKV_A = 1152
CHUNK_SHIFT = 6
ATTN_SCALE = (QK_NOPE + QK_ROPE) ** -0.5
ROPE_THETA = 10000.0
ALPHA = 2.0 ** 0.25
LN_EPS, RMS_EPS = 1e-5, 1e-6
ADAM_LR, ADAM_B1, ADAM_B2, ADAM_EPS, ADAM_WD, ADAM_STEP = 0.001, 0.9, 0.999, 1e-08, 0.01, 10
ADAM_C1 = 1.0 - ADAM_B1 ** ADAM_STEP
ADAM_C2 = 1.0 - ADAM_B2 ** ADAM_STEP
VMEM_LIMIT = 56 * 1024 * 1024
MESH = pl.DeviceIdType.MESH
ANY = pl.BlockSpec(memory_space=pl.ANY)
HBM_SPEC = pl.BlockSpec(memory_space=pltpu.HBM)
SEM_SPEC = pl.BlockSpec(memory_space=pltpu.SEMAPHORE)
NT = (((1,), (1,)), ((), ()))
TN = (((0,), (0,)), ((), ()))
NN = (((1,), (0,)), ((), ()))


def _params(sem=None):
    return pltpu.CompilerParams(dimension_semantics=sem, vmem_limit_bytes=VMEM_LIMIT)


def _pick(n, cands=(1408, 1024, 512, 384, 256, 128)):
    for t in cands:
        if n % t == 0:
            return t
    return n


def _row_tile(rows, row_bytes, budget, mult=8):
    best = mult
    for t in range(mult, rows + 1, mult):
        if rows % t == 0 and t * row_bytes <= budget:
            best = t
    return best


def _tile2(rows, cols, mult=8, budget=3 << 18):
    col_tiles = [t for t in range(128, cols + 1, 128) if cols % t == 0] or [cols]
    best = None
    for tc in col_tiles:
        for tr in range(mult, rows + 1, mult):
            if rows % tr == 0 and tr * tc <= budget and (best is None or (tr * tc, tc) > (best[0] * best[1], best[1])):
                best = (tr, tc)
    assert best is not None, (rows, cols)
    return best


def _sigmoid(x):
    return jax.nn.sigmoid(x)


class _Plan:
    def __init__(self, ins, outs, sems, start, finish, aliases=None):
        self.ins, self.outs, self.sems, self.start, self.finish = list(ins), list(outs), list(sems), start, finish
        self.aliases = dict(aliases or {})

    def io_aliases(self, first_in, first_out):
        return {first_in + i: first_out + o for i, o in self.aliases.items()}


def _run_plan(plan, name):
    n_in, n_out = len(plan.ins), len(plan.outs)

    def body(*refs):
        ins, outs, sems = refs[:n_in], refs[n_in:n_in + n_out], refs[n_in + n_out:]
        plan.start(ins, outs, sems)
        plan.finish(ins, outs, sems)

    return pl.pallas_call(body, name=name, out_shape=plan.outs, in_specs=[ANY] * n_in, out_specs=[ANY] * n_out,
                          scratch_shapes=plan.sems, input_output_aliases=plan.io_aliases(0, 0),
                          compiler_params=_params())(*plan.ins)


def _matmul(a, b, mode, out_dtype, name, add=None, carry=None, shards=None):
    if mode == "nn":
        (M, K), N, dims = a.shape, b.shape[-1] * (4 if shards else 1), NN
    elif mode == "nt":
        (M, K), N, dims = a.shape, b.shape[-2], NT
    else:
        (K, M), N, dims = a.shape, b.shape[1], TN
    split_n = shards and mode != "nt"
    tm = _pick(M)
    tn = _pick(N // 4) if split_n else _pick(N)
    if shards and mode == "nt":
        tk = _pick(K // 4)
    else:
        tk = K if K <= 2048 else _pick(K)
    nk = K // tk
    per = (N // 4 // tn) if split_n else (K // 4 // tk if shards else 1)
    a_spec = (pl.BlockSpec((tk, tm), lambda i, j, k: (k, i)) if mode == "tn"
              else pl.BlockSpec((tm, tk), lambda i, j, k: (i, k)))
    if shards == "b" and mode == "nn":
        b_spec = pl.BlockSpec((None, tk, tn), lambda i, j, k: (j // per, k, j % per))
    elif shards == "b":
        b_spec = pl.BlockSpec((None, tn, tk), lambda i, j, k: (k // per, j, k % per))
    else:
        b_spec = (pl.BlockSpec((tn, tk), lambda i, j, k: (j, k)) if mode == "nt"
                  else pl.BlockSpec((tk, tn), lambda i, j, k: (k, j)))
    o_spec = pl.BlockSpec((tm, tn), lambda i, j, k: (i, j))
    o_shape = (M, N)
    if shards == "o":
        o_spec, o_shape = pl.BlockSpec((None, tm, tn), lambda i, j, k: (j // per, i, j % per)), (4, M, N // 4)
    has_add = add is not None
    n_ci = len(carry.ins) if carry else 0
    n_co = len(carry.outs) if carry else 0
    n_in = 2 + has_add
    grid = (M // tm, N // tn, nk)

    def body(*refs):
        a_ref, b_ref = refs[0], refs[1]
        add_ref = refs[2] if has_add else None
        o_ref = refs[n_in + n_ci]
        acc_ref = refs[n_in + n_ci + 1 + n_co] if nk > 1 else None
        c_ins = refs[n_in:n_in + n_ci]
        c_outs = refs[n_in + n_ci + 1:n_in + n_ci + 1 + n_co]
        c_sems = refs[n_in + n_ci + 1 + n_co + (nk > 1):]
        i, j, k = pl.program_id(0), pl.program_id(1), pl.program_id(2)

        if carry:
            @pl.when((i == 0) & (j == 0) & (k == 0))
            def _():
                carry.start(c_ins, c_outs, c_sems)

        part = lax.dot_general(a_ref[...], b_ref[...], dims, preferred_element_type=F32)
        if nk == 1:
            o_ref[...] = (part + add_ref[...] if has_add else part).astype(o_ref.dtype)
        else:
            @pl.when(k == 0)
            def _():
                acc_ref[...] = part

            @pl.when((k > 0) & (k < nk - 1))
            def _():
                acc_ref[...] += part

            @pl.when(k == nk - 1)
            def _():
                r = acc_ref[...] + part
                if has_add:
                    r = r + add_ref[...]
                o_ref[...] = r.astype(o_ref.dtype)

        if carry:
            @pl.when((i == grid[0] - 1) & (j == grid[1] - 1) & (k == nk - 1))
            def _():
                carry.finish(c_ins, c_outs, c_sems)

    ins = [a, b] + ([add] if has_add else []) + (carry.ins if carry else [])
    in_specs = [a_spec, b_spec] + ([o_spec] if has_add else []) + [ANY] * n_ci
    res = pl.pallas_call(
        body, name=name, grid=grid,
        in_specs=in_specs, out_specs=[o_spec] + [ANY] * n_co,
        out_shape=[jax.ShapeDtypeStruct(o_shape, out_dtype)] + (carry.outs if carry else []),
        scratch_shapes=([pltpu.VMEM((tm, tn), F32)] if nk > 1 else []) + (carry.sems if carry else []),
        input_output_aliases=carry.io_aliases(n_in, 1) if carry else {},
        compiler_params=_params(("arbitrary",) * 3 if carry else ("parallel", "parallel", "arbitrary")),
    )(*ins)
    return (res[0], res[1:]) if carry else res[0]


def _rows(body, name, n_rows, tm, ins, outs, accs=()):
    grid = (n_rows // tm,)
    per8 = tm // 8
    last8 = n_rows // 8 - 1
    arrays, in_specs = [], []
    for spec in ins:
        kind, arr = spec[0], spec[1]
        arrays.append(arr)
        if kind == "row":
            _, _, cb, w = spec
            in_specs.append(pl.BlockSpec((tm, w), lambda i, cb=cb: (i, cb)))
        elif kind == "full":
            in_specs.append(pl.BlockSpec(arr.shape, lambda i, nd=arr.ndim: (0,) * nd))
        elif kind == "prev":
            _, _, cb, w = spec
            in_specs.append(pl.BlockSpec((8, w), lambda i, cb=cb: (jnp.maximum(i * per8 - 1, 0), cb)))
        else:
            _, _, cb, w = spec
            in_specs.append(pl.BlockSpec((8, w), lambda i, cb=cb: (jnp.minimum((i + 1) * per8, last8), cb)))
    out_shape = [jax.ShapeDtypeStruct((n_rows, w), dt) for (w, dt) in outs]
    out_specs = [pl.BlockSpec((tm, w), lambda i: (i, 0)) for (w, _) in outs]
    out_shape += [jax.ShapeDtypeStruct(s, F32) for s in accs]
    out_specs += [pl.BlockSpec(s, lambda i, nd=len(s): (0,) * nd) for s in accs]
    n_in, n_out = len(ins), len(outs)

    def kernel_body(*refs):
        body(pl.program_id(0), refs[:n_in], refs[n_in:n_in + n_out], refs[n_in + n_out:])

    res = pl.pallas_call(
        kernel_body, name=name, grid=grid, in_specs=in_specs, out_specs=out_specs, out_shape=out_shape,
        compiler_params=_params(("arbitrary",)),
    )(*arrays)
    return res


def _acc_add(i, ref, val):
    @pl.when(i == 0)
    def _():
        ref[...] = val

    @pl.when(i > 0)
    def _():
        ref[...] += val


def _rope(t, tab, sign):
    c, sa, sb = tab[:, 0:128], tab[:, 128:256], tab[:, 256:384]
    rot = pltpu.roll(t, 96, 1) * sa + pltpu.roll(t, 32, 1) * sb
    return t * c + rot if sign > 0 else t * c - rot


def _ln_stats(r):
    mu = jnp.mean(r, axis=-1, keepdims=True)
    d = r - mu
    var = jnp.mean(d * d, axis=-1, keepdims=True)
    rstd = lax.rsqrt(var + LN_EPS)
    return d * rstd, rstd


def _ln_bwd(dxh, xh, rstd):
    m1 = jnp.mean(dxh, axis=-1, keepdims=True)
    m2 = jnp.mean(dxh * xh, axis=-1, keepdims=True)
    return rstd * (dxh - m1 - xh * m2)


def _modulate(x, scale, shift, name):
    S, D = x.shape

    def body(i, ins, outs, accs):
        outs[0][...] = (ins[0][...] * (1.0 + ins[1][...]) + ins[2][...]).astype(BF16)

    return _rows(body, name, S, _pick(S, (256, 128)), [("row", x, 0, D), ("full", scale), ("full", shift)], [(D, BF16)])[0]


def _rms_fwd(pq, tab, g_q, g_kv):
    S = pq.shape[0]

    def body(i, ins, outs, accs):
        pq_ref, tab_ref, gq_ref, gkv_ref = ins

        def rms(x, g):
            return x * lax.rsqrt(jnp.mean(x * x, axis=-1, keepdims=True) + RMS_EPS) * g

        outs[0][...] = rms(pq_ref[:, 0:Q_LORA], gq_ref[...]).astype(BF16)
        outs[1][...] = rms(pq_ref[:, Q_LORA:Q_LORA + KV_LORA], gkv_ref[...]).astype(BF16)
        outs[2][...] = _rope(pq_ref[:, Q_LORA + KV_LORA:QKV_A], tab_ref[...], 1).astype(BF16)

    return _rows(body, "rms_fwd", S, _pick(S, (256, 128)),
                 [("row", pq, 0, QKV_A), ("row", tab, 0, 384), ("full", g_q), ("full", g_kv)],
                 [(Q_LORA, BF16), (KV_LORA, BF16), (128, BF16)])


def _q_rope(q, tab):
    S, W = q.shape

    def body(i, ins, outs, accs):
        q_ref, tab_ref = ins
        t = tab_ref[...]
        for h in range(N_HEADS):
            lo = h * QK_PAD
            outs[0][:, lo:lo + 128] = q_ref[:, lo:lo + 128].astype(BF16)
            outs[0][:, lo + 128:lo + 256] = _rope(q_ref[:, lo + 128:lo + 256], t, 1).astype(BF16)

    return _rows(body, "q_rope", S, _pick(S, (256, 128)), [("row", q, 0, W), ("row", tab, 0, 384)], [(W, BF16)])[0]


def _allowed(q0, k0, bq):
    row = q0 + lax.broadcasted_iota(jnp.int32, (bq, bq), 0)
    col = k0 + lax.broadcasted_iota(jnp.int32, (bq, bq), 1)
    return (col >> CHUNK_SHIFT) <= (row >> CHUNK_SHIFT)


ATTN_BLOCK = 512


def _attn_fwd(q, kv, kr, carry=None):
    S = q.shape[0]
    bq = min(ATTN_BLOCK, S)
    nq = S // bq
    n_ci = len(carry.ins) if carry else 0
    n_co = len(carry.outs) if carry else 0

    def body(*refs):
        q_ref, kn_ref, v_ref, kr_ref = refs[:4]
        o_ref, lse_ref = refs[4 + n_ci:6 + n_ci]
        c_ins, c_outs = refs[4:4 + n_ci], refs[6 + n_ci:6 + n_ci + n_co]
        kcat = refs[6 + n_ci + n_co]
        c_sems = refs[7 + n_ci + n_co:]
        qi = pl.program_id(1)
        if carry:
            @pl.when((pl.program_id(0) == 0) & (qi == 0))
            def _():
                carry.start(c_ins, c_outs, c_sems)

        @pl.when(qi == 0)
        def _():
            kcat[:, 0:128] = kn_ref[...]
            kcat[:, 128:256] = kr_ref[...]

        qv = q_ref[...]

        def step(j, carry, masked):
            m, l, acc = carry
            off = pl.multiple_of(j * bq, bq)
            s = lax.dot_general(qv, kcat[pl.ds(off, bq), :], NT, preferred_element_type=F32) * ATTN_SCALE
            if masked:
                s = jnp.where(_allowed(qi * bq, off, bq), s, -1e30)
            m_new = jnp.maximum(m, jnp.max(s, axis=1, keepdims=True))
            a = jnp.exp(m - m_new)
            p = jnp.exp(s - m_new)
            l = a * l + jnp.sum(p, axis=1, keepdims=True)
            acc = a * acc + jnp.dot(p.astype(BF16), v_ref[pl.ds(off, bq), :], preferred_element_type=F32)
            return m_new, l, acc

        init = (jnp.full((bq, 1), -1e30, F32), jnp.zeros((bq, 1), F32), jnp.zeros((bq, V_HEAD), F32))
        below = lax.fori_loop(0, qi, lambda j, cr: step(j, cr, False), init)
        m, l, acc = step(qi, below, True)
        o_ref[...] = (acc / l).astype(BF16)
        lse_ref[0] = m + jnp.log(l)
        if carry:
            @pl.when((pl.program_id(0) == N_HEADS - 1) & (qi == nq - 1))
            def _():
                carry.finish(c_ins, c_outs, c_sems)

    res = pl.pallas_call(
        body, name="attn_fwd", grid=(N_HEADS, nq),
        in_specs=[pl.BlockSpec((bq, QK_PAD), lambda h, i: (i, h)),
                  pl.BlockSpec((S, 128), lambda h, i: (0, 2 * h)),
                  pl.BlockSpec((S, 128), lambda h, i: (0, 2 * h + 1)),
                  pl.BlockSpec((S, 128), lambda h, i: (0, 0))] + [ANY] * n_ci,
        out_specs=[pl.BlockSpec((bq, V_HEAD), lambda h, i: (i, h)),
                   pl.BlockSpec((1, bq, 1), lambda h, i: (h, i, 0))] + [ANY] * n_co,
        out_shape=[jax.ShapeDtypeStruct((S, N_HEADS * V_HEAD), BF16),
                   jax.ShapeDtypeStruct((N_HEADS, S, 1), F32)] + (carry.outs if carry else []),
        scratch_shapes=[pltpu.VMEM((S, QK_PAD), BF16)] + (carry.sems if carry else []),
        input_output_aliases=carry.io_aliases(4, 2) if carry else {},
        compiler_params=_params(("arbitrary", "arbitrary")),
    )(q, kv, kv, kr, *(carry.ins if carry else []))
    return res[0], res[1], res[2:]


def _attn_bwd(q, kv, kr, do, o, lse, tab, carry=None):
    S = q.shape[0]
    bq = min(ATTN_BLOCK, S)
    nq = S // bq

    n_ci = len(carry.ins) if carry else 0
    n_co = len(carry.outs) if carry else 0

    def body(*refs):
        q_ref, kn_ref, v_ref, kr_ref, do_ref, o_ref, lse_ref, tab_ref = refs[:8]
        dq_ref, dkv_ref, dkr_ref = refs[8 + n_ci:11 + n_ci]
        dq_acc, dk_acc, dv_acc, kcat, delta = refs[11 + n_ci + n_co:16 + n_ci + n_co]
        c_ins, c_outs, c_sems = refs[8:8 + n_ci], refs[11 + n_ci:11 + n_ci + n_co], refs[16 + n_ci + n_co:]
        h = pl.program_id(0)
        if carry:
            @pl.when(h == 0)
            def _():
                carry.start(c_ins, c_outs, c_sems)

        dq_acc[...] = jnp.zeros_like(dq_acc)
        dk_acc[...] = jnp.zeros_like(dk_acc)
        dv_acc[...] = jnp.zeros_like(dv_acc)
        kcat[:, 0:128] = kn_ref[...]
        kcat[:, 128:256] = kr_ref[...]
        for r in range(nq):
            rows = slice(r * bq, (r + 1) * bq)
            delta[rows, :] = jnp.sum(do_ref[rows, :].astype(F32) * o_ref[rows, :].astype(F32), axis=1, keepdims=True)

        def pair(i, j, masked):
            rows_i = pl.ds(pl.multiple_of(i * bq, bq), bq)
            rows_j = pl.ds(pl.multiple_of(j * bq, bq), bq)
            qv, dov, k = q_ref[rows_i, :], do_ref[rows_i, :], kcat[rows_j, :]
            s = lax.dot_general(qv, k, NT, preferred_element_type=F32) * ATTN_SCALE
            if masked:
                s = jnp.where(_allowed(i * bq, j * bq, bq), s, -1e30)
            p = jnp.exp(s - lse_ref[0, rows_i, :])
            dv_acc[rows_j, :] += lax.dot_general(p.astype(BF16), dov, TN, preferred_element_type=F32)
            dp = lax.dot_general(dov, v_ref[rows_j, :], NT, preferred_element_type=F32)
            ds = (p * (dp - delta[rows_i, :]) * ATTN_SCALE).astype(BF16)
            dk_acc[rows_j, :] += lax.dot_general(ds, qv, TN, preferred_element_type=F32)
            dq_acc[rows_i, :] += jnp.dot(ds, k, preferred_element_type=F32)

        def kv_step(j, _):
            pair(j, j, True)

            def q_step(i, _):
                pair(i, j, False)
                return 0

            lax.fori_loop(j + 1, nq, q_step, 0)
            return 0

        lax.fori_loop(0, nq, kv_step, 0)

        for r in range(nq):
            rows = slice(r * bq, (r + 1) * bq)
            dq_ref[rows, 0:128] = dq_acc[rows, 0:128].astype(BF16)
            dq_ref[rows, 128:256] = _rope(dq_acc[rows, 128:256], tab_ref[rows, :], -1).astype(BF16)
        dkv_ref[:, 0:128] = dk_acc[:, 0:128].astype(BF16)
        dkv_ref[:, 128:256] = dv_acc[...].astype(BF16)

        @pl.when(h == 0)
        def _():
            dkr_ref[...] = dk_acc[:, 128:256]

        @pl.when(h > 0)
        def _():
            dkr_ref[...] += dk_acc[:, 128:256]

        @pl.when(h == N_HEADS - 1)
        def _():
            for r in range(nq):
                rows = slice(r * bq, (r + 1) * bq)
                dkr_ref[rows, :] = _rope(dkr_ref[rows, :], tab_ref[rows, :], -1)
            if carry:
                carry.finish(c_ins, c_outs, c_sems)

    W = N_HEADS * QK_PAD
    res = pl.pallas_call(
        body, name="attn_bwd", grid=(N_HEADS,),
        in_specs=[pl.BlockSpec((S, QK_PAD), lambda h: (0, h)),
                  pl.BlockSpec((S, 128), lambda h: (0, 2 * h)),
                  pl.BlockSpec((S, 128), lambda h: (0, 2 * h + 1)),
                  pl.BlockSpec((S, 128), lambda h: (0, 0)),
                  pl.BlockSpec((S, V_HEAD), lambda h: (0, h)),
                  pl.BlockSpec((S, V_HEAD), lambda h: (0, h)),
                  pl.BlockSpec((1, S, 1), lambda h: (h, 0, 0)),
                  pl.BlockSpec((S, 384), lambda h: (0, 0))] + [ANY] * n_ci,
        out_specs=[pl.BlockSpec((S, QK_PAD), lambda h: (0, h)),
                   pl.BlockSpec((S, QK_PAD), lambda h: (0, h)),
                   pl.BlockSpec((S, 128), lambda h: (0, 0))] + [ANY] * n_co,
        out_shape=[jax.ShapeDtypeStruct((S, W), BF16), jax.ShapeDtypeStruct((S, W), BF16),
                   jax.ShapeDtypeStruct((S, 128), F32)] + (carry.outs if carry else []),
        scratch_shapes=[pltpu.VMEM((S, QK_PAD), F32), pltpu.VMEM((S, QK_PAD), F32), pltpu.VMEM((S, V_HEAD), F32),
                        pltpu.VMEM((S, QK_PAD), BF16), pltpu.VMEM((S, 1), F32)]
        + (carry.sems if carry else []),
        input_output_aliases=carry.io_aliases(8, 3) if carry else {},
        compiler_params=_params(("arbitrary",)),
    )(q, kv, kv, kr, do, o, lse, tab, *(carry.ins if carry else []))
    return res[0], res[1], res[2], res[3:]


def _shift_down(cur, prev8, i, n):
    tm = cur.shape[0]
    prev8 = jnp.where(i == 0, jnp.zeros_like(prev8), prev8)
    full = jnp.concatenate([prev8, cur], axis=0)
    return pltpu.roll(full, n, 0)[8:8 + tm, :]


def _shift_up(cur, next8, i, last, n):
    tm = cur.shape[0]
    next8 = jnp.where(i == last, jnp.zeros_like(next8), next8)
    full = jnp.concatenate([cur, next8], axis=0)
    return pltpu.roll(full, tm + 8 - n, 0)[0:tm, :]


def _conv_fwd(pc, w_conv):
    S, D = pc.shape[0], pc.shape[1] // 3
    tm = _pick(S, (256, 128))

    def body(i, ins, outs, accs):
        b_ref, c_ref, x_ref, cp_ref, xp_ref, w_ref = ins
        z = c_ref[...] * x_ref[...]
        zp = cp_ref[...] * xp_ref[...]
        cz = w_ref[0:1, :] * _shift_down(z, zp, i, 2) + w_ref[1:2, :] * _shift_down(z, zp, i, 1) + w_ref[2:3, :] * z
        outs[0][...] = (b_ref[...] * cz).astype(BF16)

    return _rows(body, "conv_fwd", S, tm,
                 [("row", pc, 0, D), ("row", pc, 1, D), ("row", pc, 2, D), ("prev", pc, 1, D), ("prev", pc, 2, D),
                  ("full", w_conv)], [(D, BF16)])[0]


def _conv_bwd(dhb, pc, w_conv):
    S, D = dhb.shape
    tm = _pick(S, (256, 128))
    last = S // tm - 1

    def body(i, ins, outs, accs):
        g_ref, b_ref, c_ref, x_ref, cp_ref, xp_ref, gn_ref, bn_ref, w_ref = ins
        w0, w1, w2 = w_ref[0:1, :], w_ref[1:2, :], w_ref[2:3, :]
        c, x, g = c_ref[...], x_ref[...], g_ref[...]
        z = c * x
        zp = cp_ref[...] * xp_ref[...]
        z1, z2 = _shift_down(z, zp, i, 1), _shift_down(z, zp, i, 2)
        cz = w0 * z2 + w1 * z1 + w2 * z
        dcz = g * b_ref[...]
        dczn = gn_ref[...] * bn_ref[...]
        dz = w2 * dcz + w1 * _shift_up(dcz, dczn, i, last, 1) + w0 * _shift_up(dcz, dczn, i, last, 2)
        outs[0][:, 0:D] = (g * cz).astype(BF16)
        outs[0][:, D:2 * D] = (dz * x).astype(BF16)
        outs[0][:, 2 * D:3 * D] = (dz * c).astype(BF16)
        dw = jnp.concatenate([jnp.sum(dcz * z2, axis=0, keepdims=True), jnp.sum(dcz * z1, axis=0, keepdims=True),
                              jnp.sum(dcz * z, axis=0, keepdims=True)], axis=0)
        _acc_add(i, accs[0], dw)

    return _rows(body, "conv_bwd", S, tm,
                 [("row", dhb, 0, D), ("row", pc, 0, D), ("row", pc, 1, D), ("row", pc, 2, D),
                  ("prev", pc, 1, D), ("prev", pc, 2, D), ("next", dhb, 0, D), ("next", pc, 0, D), ("full", w_conv)],
                 [(3 * D, BF16)], [(3, D)])


def _merge_fwd(y_a, y_b, pg):
    S, D = y_a.shape

    def body(i, ins, outs, accs):
        ya, yb, ga, gb = ins
        outs[0][...] = (_sigmoid(ga[...].astype(F32)) * ya[...] + _sigmoid(gb[...].astype(F32)) * yb[...]).astype(BF16)

    return _rows(body, "merge_fwd", S, _pick(S, (256, 128)),
                 [("row", y_a, 0, D), ("row", y_b, 0, D), ("row", pg, 0, D), ("row", pg, 1, D)], [(D, BF16)])[0]


def _merge_bwd(dm, y_a, y_b, pg):
    S, D = dm.shape

    def body(i, ins, outs, accs):
        d, ya, yb = ins[0][...], ins[1][...], ins[2][...]
        sa, sb = _sigmoid(ins[3][...].astype(F32)), _sigmoid(ins[4][...].astype(F32))
        outs[0][...] = (d * sa).astype(BF16)
        outs[1][...] = (d * sb).astype(BF16)
        outs[2][:, 0:D] = (d * ya * (sa * (1.0 - sa))).astype(BF16)
        outs[2][:, D:2 * D] = (d * yb * (sb * (1.0 - sb))).astype(BF16)

    return _rows(body, "merge_bwd", S, _pick(S, (256, 128)),
                 [("row", dm, 0, D), ("row", y_a, 0, D), ("row", y_b, 0, D), ("row", pg, 0, D), ("row", pg, 1, D)],
                 [(D, BF16), (D, BF16), (2 * D, BF16)])


def _ln1_fwd(x, mix, gate1, g, b, scale2, shift2):
    S, D = x.shape

    def body(i, ins, outs, accs):
        x_ref, mix_ref, gate_ref, g_ref, b_ref, sc_ref, sh_ref = ins
        xh, _ = _ln_stats(ALPHA * x_ref[...] + gate_ref[...] * mix_ref[...])
        x1 = xh * g_ref[...] + b_ref[...]
        outs[0][...] = x1
        outs[1][...] = (x1 * (1.0 + sc_ref[...]) + sh_ref[...]).astype(BF16)

    return _rows(body, "ln1_fwd", S, _pick(S, (256, 128)),
                 [("row", x, 0, D), ("row", mix, 0, D), ("full", gate1), ("full", g), ("full", b),
                  ("full", scale2), ("full", shift2)], [(D, F32), (D, BF16)])


def _swiglu_fwd(hh):
    S, F = hh.shape[0], hh.shape[1] // 2

    def body(i, ins, outs, accs):
        hg = ins[0][...].astype(F32)
        outs[0][...] = (hg * _sigmoid(hg) * ins[1][...].astype(F32)).astype(BF16)

    return _rows(body, "swiglu_fwd", S, _pick(S, (128,)), [("row", hh, 0, F), ("row", hh, 1, F)], [(F, BF16)])[0]


def _swiglu_bwd(dact, hh):
    S, F = dact.shape

    def body(i, ins, outs, accs):
        d, hg, hu = ins[0][...].astype(F32), ins[1][...].astype(F32), ins[2][...].astype(F32)
        sg = _sigmoid(hg)
        outs[0][:, 0:F] = (d * hu * (sg * (1.0 + hg * (1.0 - sg)))).astype(BF16)
        outs[0][:, F:2 * F] = (d * (hg * sg)).astype(BF16)

    return _rows(body, "swiglu_bwd", S, _pick(S, (128,)),
                 [("row", dact, 0, F), ("row", hh, 0, F), ("row", hh, 1, F)], [(2 * F, BF16)])[0]


def _ln2_loss_bwd(x1, ffn, gate2, g, b, target):
    S, D = x1.shape

    def body(i, ins, outs, accs):
        x1_ref, f_ref, gate_ref, g_ref, b_ref, t_ref = ins
        f = f_ref[...]
        xh, rstd = _ln_stats(ALPHA * x1_ref[...] + gate_ref[...] * f)
        e = xh * g_ref[...] + b_ref[...] - t_ref[...]
        dy = e * (1.0 / D)
        dr = _ln_bwd(dy * g_ref[...], xh, rstd)
        outs[0][...] = (gate_ref[...] * dr).astype(BF16)
        outs[1][...] = ALPHA * dr
        _acc_add(i, accs[0], jnp.full((1, 128), (0.5 / D) * jnp.sum(e * e), F32))
        _acc_add(i, accs[1], jnp.sum(dy * xh, axis=0, keepdims=True))
        _acc_add(i, accs[2], jnp.sum(dy, axis=0, keepdims=True))
        _acc_add(i, accs[3], jnp.sum(dr * f, axis=0, keepdims=True))

    return _rows(body, "ln2_loss_bwd", S, _pick(S, (256, 128)),
                 [("row", x1, 0, D), ("row", ffn, 0, D), ("full", gate2), ("full", g), ("full", b), ("row", target, 0, D)],
                 [(D, BF16), (D, F32)], [(1, 128), (1, D), (1, D), (1, D)])


def _ln1_bwd(x, mix, dx1a, du2, gate1, g, b, scale2):
    S, D = x.shape

    def body(i, ins, outs, accs):
        x_ref, mix_ref, da_ref, du_ref, gate_ref, g_ref, b_ref, sc_ref = ins
        mix, du = mix_ref[...], du_ref[...]
        xh, rstd = _ln_stats(ALPHA * x_ref[...] + gate_ref[...] * mix)
        x1 = xh * g_ref[...] + b_ref[...]
        dx1 = da_ref[...] + du * (1.0 + sc_ref[...])
        dr = _ln_bwd(dx1 * g_ref[...], xh, rstd)
        outs[0][...] = (gate_ref[...] * dr).astype(BF16)
        outs[1][...] = ALPHA * dr
        _acc_add(i, accs[0], jnp.sum(du, axis=0, keepdims=True))
        _acc_add(i, accs[1], jnp.sum(du * x1, axis=0, keepdims=True))
        _acc_add(i, accs[2], jnp.sum(dx1 * xh, axis=0, keepdims=True))
        _acc_add(i, accs[3], jnp.sum(dx1, axis=0, keepdims=True))
        _acc_add(i, accs[4], jnp.sum(dr * mix, axis=0, keepdims=True))

    return _rows(body, "ln1_bwd", S, _pick(S, (256, 128)),
                 [("row", x, 0, D), ("row", mix, 0, D), ("row", dx1a, 0, D), ("row", du2, 0, D),
                  ("full", gate1), ("full", g), ("full", b), ("full", scale2)],
                 [(D, BF16), (D, F32)], [(1, D)] * 5)


def _rms_bwd(d_rq, d_rkv, pq, dkr, g_q, g_kv):
    S = pq.shape[0]

    def body(i, ins, outs, accs):
        dq_ref, dkv_ref, pq_ref, dkr_ref, gq_ref, gkv_ref = ins

        def rms_bwd(dy, x, g):
            r = lax.rsqrt(jnp.mean(x * x, axis=-1, keepdims=True) + RMS_EPS)
            dyg = dy * g
            dx = r * dyg - x * (r * r * r) * jnp.mean(dyg * x, axis=-1, keepdims=True)
            return dx, jnp.sum(dy * (x * r), axis=0, keepdims=True)

        dxq, dgq = rms_bwd(dq_ref[...], pq_ref[:, 0:Q_LORA], gq_ref[...])
        dxkv, dgkv = rms_bwd(dkv_ref[...], pq_ref[:, Q_LORA:Q_LORA + KV_LORA], gkv_ref[...])
        outs[0][:, 0:Q_LORA] = dxq.astype(BF16)
        outs[0][:, Q_LORA:Q_LORA + KV_LORA] = dxkv.astype(BF16)
        outs[0][:, Q_LORA + KV_LORA:QKV_A] = dkr_ref[...].astype(BF16)
        _acc_add(i, accs[0], dgq)
        _acc_add(i, accs[1], dgkv)

    return _rows(body, "rms_bwd", S, _pick(S, (256, 128)),
                 [("row", d_rq, 0, Q_LORA), ("row", d_rkv, 0, KV_LORA), ("row", pq, 0, QKV_A), ("row", dkr, 0, 128),
                  ("full", g_q), ("full", g_kv)], [(QKV_A, BF16)], [(1, Q_LORA), (1, KV_LORA)])


def _dx_final(dxa, du, x, scale1):
    S, D = x.shape

    def body(i, ins, outs, accs):
        du = ins[1][...]
        outs[0][...] = ins[0][...] + du * (1.0 + ins[3][...])
        _acc_add(i, accs[0], jnp.sum(du, axis=0, keepdims=True))
        _acc_add(i, accs[1], jnp.sum(du * ins[2][...], axis=0, keepdims=True))

    return _rows(body, "dx_final", S, _pick(S, (256, 128)),
                 [("row", dxa, 0, D), ("row", du, 0, D), ("row", x, 0, D), ("full", scale1)],
                 [(D, F32)], [(1, D), (1, D)])


def _ada_fwd(c_all, w, bias):
    B, D = c_all.shape
    NA = w.shape[1]
    tn = _pick(NA, (512, 256, 128))

    def body(c_ref, w_ref, b_ref, o_ref):
        cv = c_ref[...]
        ca = (cv * _sigmoid(cv)).astype(BF16)
        o_ref[...] = jnp.dot(ca, w_ref[...].astype(BF16), preferred_element_type=F32) + b_ref[...]

    return pl.pallas_call(
        body, name="ada_fwd", grid=(NA // tn,),
        in_specs=[pl.BlockSpec((B, D), lambda j: (0, 0)), pl.BlockSpec((D, tn), lambda j: (0, j)),
                  pl.BlockSpec((1, tn), lambda j: (0, j))],
        out_specs=pl.BlockSpec((B, tn), lambda j: (0, j)),
        out_shape=jax.ShapeDtypeStruct((B, NA), F32),
        compiler_params=_params(("arbitrary",)),
    )(c_all, w, bias)


def _ada_bwd(c_all, dmod):
    B, D = c_all.shape
    NA = dmod.shape[1]
    tn = _pick(NA, (512, 256, 128))

    def body(c_ref, d_ref, o_ref):
        cv = c_ref[...]
        ca = (cv * _sigmoid(cv)).astype(BF16)
        o_ref[...] = lax.dot_general(ca, d_ref[...].astype(BF16), TN, preferred_element_type=F32)

    return pl.pallas_call(
        body, name="ada_bwd", grid=(NA // tn,),
        in_specs=[pl.BlockSpec((B, D), lambda j: (0, 0)), pl.BlockSpec((B, tn), lambda j: (0, j))],
        out_specs=pl.BlockSpec((D, tn), lambda j: (0, j)),
        out_shape=jax.ShapeDtypeStruct((D, NA), F32),
        compiler_params=_params(("arbitrary",)),
    )(c_all, dmod)


def _pack_rows(parts, n_rows):
    N = parts[0].shape[1]

    def body(*refs):
        o_ref = refs[-1]
        o_ref[...] = jnp.zeros_like(o_ref)
        at = 0
        for r in refs[:-1]:
            o_ref[at:at + r.shape[0], :] = r[...]
            at += r.shape[0]

    return pl.pallas_call(body, name="pack_small", out_shape=jax.ShapeDtypeStruct((n_rows, N), F32),
                          compiler_params=_params())(*parts)


def _sum8(parts):
    _, R, N = parts.shape

    def body(p_ref, o_ref):
        acc = p_ref[0]
        for d in range(1, 8):
            acc = acc + p_ref[d]
        o_ref[...] = acc

    return pl.pallas_call(body, name="sum8", out_shape=jax.ShapeDtypeStruct((R, N), F32),
                          compiler_params=_params())(parts)


def _adam_math(w, g, m, v):
    m = ADAM_B1 * m + (1.0 - ADAM_B1) * g
    v = ADAM_B2 * v + (1.0 - ADAM_B2) * (g * g)
    delta = -ADAM_LR * ((m / ADAM_C1) / (jnp.sqrt(v / ADAM_C2) + ADAM_EPS) + ADAM_WD * w)
    return delta, m, v


def _adam(name, w, m, v, g, carry=None):
    R, C = w.shape
    tm = _row_tile(R, C * 4, 1 << 20)
    steps = R // tm
    n_ci = len(carry.ins) if carry else 0
    n_co = len(carry.outs) if carry else 0

    def body(*refs):
        w_ref, m_ref, v_ref, g_ref = refs[:4]
        d_ref, nm_ref, nv_ref = refs[4 + n_ci:7 + n_ci]
        c_ins, c_outs, c_sems = refs[4:4 + n_ci], refs[7 + n_ci:7 + n_ci + n_co], refs[7 + n_ci + n_co:]
        if carry:
            @pl.when(pl.program_id(0) == 0)
            def _():
                carry.start(c_ins, c_outs, c_sems)

        delta, nm, nv = _adam_math(w_ref[...], g_ref[...], m_ref[...], v_ref[...])
        d_ref[...] = delta
        nm_ref[...] = nm
        nv_ref[...] = nv
        if carry:
            @pl.when(pl.program_id(0) == steps - 1)
            def _():
                carry.finish(c_ins, c_outs, c_sems)

    spec = pl.BlockSpec((tm, C), lambda i: (i, 0))
    res = pl.pallas_call(
        body, name=name, grid=(steps,), in_specs=[spec] * 4 + [ANY] * n_ci, out_specs=[spec] * 3 + [ANY] * n_co,
        out_shape=[jax.ShapeDtypeStruct((R, C), F32)] * 3 + (carry.outs if carry else []),
        scratch_shapes=carry.sems if carry else [],
        input_output_aliases=carry.io_aliases(4, 3) if carry else {},
        compiler_params=_params(("arbitrary",)),
    )(w, m, v, g, *(carry.ins if carry else []))
    return (res[:3], res[3:]) if carry else res


def _adam_halves(name, w, m, v, mine, other, core, carry=None):
    R, C = w.shape
    Rh = mine.shape[0]
    tc = max(t for t in range(128, C + 1, 128) if C % t == 0 and R * t <= (3 << 17))
    steps = C // tc
    n_ci = len(carry.ins) if carry else 0
    n_co = len(carry.outs) if carry else 0

    def body(*refs):
        c_ref, w_ref, m_ref, v_ref, a_ref, b_ref = refs[:6]
        g_ref, d_ref, nm_ref, nv_ref = refs[6 + n_ci:10 + n_ci]
        c_ins, c_outs, c_sems = refs[6:6 + n_ci], refs[10 + n_ci:10 + n_ci + n_co], refs[10 + n_ci + n_co:]
        if carry:
            @pl.when(pl.program_id(0) == 0)
            def _():
                carry.start(c_ins, c_outs, c_sems)

        first = c_ref[0] == 0
        g = jnp.concatenate([jnp.where(first, a_ref[...], b_ref[...]),
                             jnp.where(first, b_ref[0:R - Rh, :], a_ref[0:R - Rh, :])], axis=0)
        delta, nm, nv = _adam_math(w_ref[...], g, m_ref[...], v_ref[...])
        g_ref[...] = g
        d_ref[...] = delta
        nm_ref[...] = nm
        nv_ref[...] = nv
        if carry:
            @pl.when(pl.program_id(0) == steps - 1)
            def _():
                carry.finish(c_ins, c_outs, c_sems)

    spec = pl.BlockSpec((R, tc), lambda i, c_ref: (0, i))
    h_spec = pl.BlockSpec((Rh, tc), lambda i, c_ref: (0, i))
    res = pl.pallas_call(
        body, name=name, out_shape=[jax.ShapeDtypeStruct((R, C), F32)] * 4 + (carry.outs if carry else []),
        grid_spec=pltpu.PrefetchScalarGridSpec(
            num_scalar_prefetch=1, grid=(steps,), in_specs=[spec, spec, spec, h_spec, h_spec] + [ANY] * n_ci,
            out_specs=[spec] * 4 + [ANY] * n_co, scratch_shapes=carry.sems if carry else []),
        input_output_aliases=carry.io_aliases(6, 4) if carry else {},
        compiler_params=_params(("arbitrary",)),
    )(core, w, m, v, mine, other, *(carry.ins if carry else []))
    return (res[:4], res[4:]) if carry else res


def _adam_small(name, w, m, v, g):
    def body(w_ref, m_ref, v_ref, g_ref, d_ref, nm_ref, nv_ref):
        delta, nm, nv = _adam_math(w_ref[...], g_ref[...], m_ref[...], v_ref[...])
        d_ref[...] = delta
        nm_ref[...] = nm
        nv_ref[...] = nv

    return pl.pallas_call(body, name=name, out_shape=[jax.ShapeDtypeStruct(w.shape, F32)] * 3,
                          compiler_params=_params())(w, m, v, g)


def _place():
    return lax.axis_index("x"), lax.axis_index("y"), lax.axis_index("c")


def _other_chips(x, y):
    return [(1 - x, y), (x, 1 - y), (1 - x, 1 - y)]


def _all_gather8(blk, name):
    R, N = blk.shape

    def body(x_ref, out_ref, send_sems, recv_sems, local_sem):
        x, y, c = _place()
        me = 4 * x + 2 * y + c
        mine = pltpu.make_async_copy(x_ref, out_ref.at[me], local_sem)
        mine.start()
        flips = [(j >> 2 & 1, j >> 1 & 1, j & 1) for j in range(1, 8)]
        peers = [((1 - x) if fx else x, (1 - y) if fy else y, (1 - c) if fc else c) for fx, fy, fc in flips]
        sends = []
        for j, peer in enumerate(peers):
            cp = pltpu.make_async_remote_copy(src_ref=x_ref, dst_ref=out_ref.at[me], send_sem=send_sems.at[j],
                                              recv_sem=recv_sems.at[j], device_id=peer, device_id_type=MESH)
            cp.start()
            sends.append(cp)
        for j, (px, py, pc) in enumerate(peers):
            pltpu.make_async_remote_copy(src_ref=x_ref, dst_ref=out_ref.at[4 * px + 2 * py + pc],
                                         send_sem=send_sems.at[j], recv_sem=recv_sems.at[j],
                                         device_id=(px, py, pc), device_id_type=MESH).wait_recv()
        for cp in sends:
            cp.wait_send()
        mine.wait()

    return pl.pallas_call(
        body, name=name, out_shape=jax.ShapeDtypeStruct((8, R, N), F32),
        in_specs=[pl.BlockSpec(memory_space=pltpu.VMEM)], out_specs=pl.BlockSpec(memory_space=pltpu.VMEM),
        scratch_shapes=[pltpu.SemaphoreType.DMA((7,)), pltpu.SemaphoreType.DMA((7,)), pltpu.SemaphoreType.DMA],
        compiler_params=_params(),
    )(blk)


def _piece(rows, piece):
    i, n, k = piece if len(piece) == 3 else (piece[0], piece[1], 1)
    assert rows % 16 == 0 and rows // 16 >= n, (rows, piece)
    lo, hi = (rows // 16 * i // n) * 16, (rows // 16 * (i + k) // n) * 16
    return pl.ds(lo, hi - lo)


def _scatter_plan(arrs, piece=(0, 1), into=None):
    n = len(arrs)

    def copies(ins, outs, sems):
        send_sems, recv_sems = sems
        x, y, c = _place()
        chips = _other_chips(x, y)
        cps = []
        for k in range(n):
            rows = _piece(arrs[k].shape[1], piece)
            for j, (px, py) in enumerate(chips):
                cps.append(pltpu.make_async_remote_copy(
                    src_ref=ins[k].at[2 * px + py, rows], dst_ref=outs[k].at[j, rows],
                    send_sem=send_sems.at[3 * k + j], recv_sem=recv_sems.at[3 * k + j],
                    device_id=(px, py, c), device_id_type=MESH))
        return cps

    def start(ins, outs, sems):
        for cp in copies(ins, outs, sems):
            cp.start()

    def finish(ins, outs, sems):
        for cp in copies(ins, outs, sems):
            cp.wait()

    return _Plan(list(arrs) + list(into or []), [jax.ShapeDtypeStruct((3,) + a.shape[1:], a.dtype) for a in arrs],
                 [pltpu.SemaphoreType.DMA((3 * n,))] * 2, start, finish,
                 aliases={n + k: k for k in range(n)} if into else None)


def _gather_plan(shards, piece=(0, 1), into=None):
    n = len(shards)

    def parts(ins, outs, sems):
        s1, r1, s2, r2, loc = sems
        x, y, c = _place()
        me = 2 * x + y
        chips = _other_chips(x, y)
        sib = (x, y, 1 - c)

        def rows(k):
            return _piece(shards[k].shape[1], piece)

        def ici(k, j, slab, to):
            return pltpu.make_async_remote_copy(src_ref=ins[k].at[c, rows(k)], dst_ref=outs[k].at[slab, c, rows(k)],
                                                send_sem=s1.at[3 * k + j], recv_sem=r1.at[3 * k + j],
                                                device_id=to, device_id_type=MESH)

        def d2d(k, j, slab, half):
            return pltpu.make_async_remote_copy(src_ref=outs[k].at[slab, half, rows(k)],
                                                dst_ref=outs[k].at[slab, half, rows(k)],
                                                send_sem=s2.at[3 * k + j], recv_sem=r2.at[3 * k + j],
                                                device_id=sib, device_id_type=MESH)

        def own(k):
            return pltpu.make_async_remote_copy(src_ref=ins[k].at[:, rows(k)], dst_ref=outs[k].at[me, :, rows(k)],
                                                send_sem=loc.at[2 * k], recv_sem=loc.at[2 * k + 1],
                                                device_id=sib, device_id_type=MESH)

        return c, me, chips, ici, d2d, own

    def start(ins, outs, sems):
        c, me, chips, ici, d2d, own = parts(ins, outs, sems)
        for k in range(n):
            for j, (px, py) in enumerate(chips):
                ici(k, j, me, (px, py, c)).start()
        for k in range(n):
            own(k).start()

    def finish(ins, outs, sems):
        c, me, chips, ici, d2d, own = parts(ins, outs, sems)
        for k in range(n):
            for j, (px, py) in enumerate(chips):
                ici(k, j, 2 * px + py, (px, py, c)).wait_recv()
                d2d(k, j, 2 * px + py, c).start()
        for k in range(n):
            for j, (px, py) in enumerate(chips):
                d2d(k, j, 2 * px + py, 1 - c).wait_recv()
        for k in range(n):
            own(k).wait()
            for j, (px, py) in enumerate(chips):
                ici(k, j, me, (px, py, c)).wait_send()
                d2d(k, j, 2 * px + py, c).wait_send()

    return _Plan(list(shards) + list(into or []), [jax.ShapeDtypeStruct((4,) + a.shape, a.dtype) for a in shards],
                 [pltpu.SemaphoreType.DMA((3 * n,))] * 4 + [pltpu.SemaphoreType.DMA((2 * n,))], start, finish,
                 aliases={n + k: k for k in range(n)} if into else None)


def _pair_plan(parts):
    n = len(parts)

    def copies(ins, outs, sems):
        send_sems, recv_sems = sems
        x, y, c = _place()
        return [pltpu.make_async_remote_copy(src_ref=ins[k].at[p, 1 - c], dst_ref=outs[k].at[p],
                                             send_sem=send_sems.at[4 * k + p], recv_sem=recv_sems.at[4 * k + p],
                                             device_id=(x, y, 1 - c), device_id_type=MESH)
                for k in range(n) for p in range(4)]

    def start(ins, outs, sems):
        for cp in copies(ins, outs, sems):
            cp.start()

    def finish(ins, outs, sems):
        for cp in copies(ins, outs, sems):
            cp.wait()

    return _Plan(parts, [jax.ShapeDtypeStruct((4,) + a.shape[2:], a.dtype) for a in parts],
                 [pltpu.SemaphoreType.DMA((4 * n,))] * 2, start, finish)


def _sibling_plan(arrs):
    n = len(arrs)

    def copies(ins, outs, sems):
        send_sems, recv_sems = sems
        x, y, c = _place()
        return [pltpu.make_async_remote_copy(src_ref=ins[k], dst_ref=outs[k], send_sem=send_sems.at[k],
                                             recv_sem=recv_sems.at[k], device_id=(x, y, 1 - c), device_id_type=MESH)
                for k in range(n)]

    def start(ins, outs, sems):
        for cp in copies(ins, outs, sems):
            cp.start()

    def finish(ins, outs, sems):
        for cp in copies(ins, outs, sems):
            cp.wait()

    return _Plan(arrs, [jax.ShapeDtypeStruct(a.shape, a.dtype) for a in arrs],
                 [pltpu.SemaphoreType.DMA((n,))] * 2, start, finish)


def _scatter_start(arrs, ride, name):
    n = len(arrs)
    lands = [lax.empty((3,) + a.shape[1:], a.dtype) for a in arrs]

    def body(*refs):
        ins, land = refs[:n], refs[n:2 * n]
        send_sems, recv_sems = refs[2 * n + 1], refs[2 * n + 2]
        x, y, c = _place()
        for k in range(n):
            for j, (px, py) in enumerate(_other_chips(x, y)):
                pltpu.make_async_remote_copy(src_ref=ins[k].at[2 * px + py], dst_ref=land[k].at[j],
                                             send_sem=send_sems.at[3 * k + j], recv_sem=recv_sems.at[3 * k + j],
                                             device_id=(px, py, c), device_id_type=MESH).start()

    hbm = [pltpu.with_memory_space_constraint(a, pltpu.HBM) for a in list(arrs) + lands + [ride]]
    res = pl.pallas_call(
        body, name=name,
        out_shape=[pltpu.SemaphoreType.DMA((3 * n,)), pltpu.SemaphoreType.DMA((3 * n,))]
        + [pltpu.HBM(a.shape, a.dtype) for a in hbm],
        in_specs=[HBM_SPEC] * (2 * n + 1), out_specs=[SEM_SPEC, SEM_SPEC] + [HBM_SPEC] * (2 * n + 1),
        input_output_aliases={i: 2 + i for i in range(2 * n + 1)},
        compiler_params=pltpu.CompilerParams(has_side_effects=pltpu.SideEffectType.DATAFLOW_SIDE_EFFECTING),
    )(*hbm)
    return res[0], res[1], res[2:2 + n], res[2 + n:2 + 2 * n], res[2 + 2 * n]


def _scatter_wait(send_sems, recv_sems, arrs, lands, after, name):
    n = len(arrs)

    def body(*refs):
        ins, land = refs[:n], refs[n:2 * n]
        ssem, rsem = refs[2 * n], refs[2 * n + 1]
        x, y, c = _place()
        for k in range(n):
            for j, (px, py) in enumerate(_other_chips(x, y)):
                cp = pltpu.make_async_remote_copy(src_ref=ins[k].at[2 * px + py], dst_ref=land[k].at[j],
                                                  send_sem=ssem.at[3 * k + j], recv_sem=rsem.at[3 * k + j],
                                                  device_id=(px, py, c), device_id_type=MESH)
                cp.wait_send()
                cp.wait_recv()

    res = pl.pallas_call(
        body, name=name, out_shape=[pltpu.HBM(a.shape, a.dtype) for a in list(arrs) + list(lands)],
        in_specs=[HBM_SPEC] * (2 * n) + [SEM_SPEC, SEM_SPEC] + [ANY] * len(after), out_specs=[HBM_SPEC] * (2 * n),
        input_output_aliases={i: i for i in range(2 * n)},
        compiler_params=pltpu.CompilerParams(has_side_effects=pltpu.SideEffectType.DATAFLOW_SIDE_EFFECTING),
    )(*arrs, *lands, send_sems, recv_sems, *after)
    return res[n:]


def _join_plans(plans):
    def split(seq, counts):
        out, at = [], 0
        for cnt in counts:
            out.append(seq[at:at + cnt])
            at += cnt
        return out

    n_i, n_o, n_s = ([len(getattr(p, f)) for p in plans] for f in ("ins", "outs", "sems"))

    def start(ins, outs, sems):
        for p, i, o, s in zip(plans, split(ins, n_i), split(outs, n_o), split(sems, n_s)):
            p.start(i, o, s)

    def finish(ins, outs, sems):
        for p, i, o, s in zip(plans, split(ins, n_i), split(outs, n_o), split(sems, n_s)):
            p.finish(i, o, s)

    aliases, at_i, at_o = {}, 0, 0
    for p in plans:
        aliases.update(p.io_aliases(at_i, at_o))
        at_i, at_o = at_i + len(p.ins), at_o + len(p.outs)
    return _Plan(sum((p.ins for p in plans), []), sum((p.outs for p in plans), []), sum((p.sems for p in plans), []),
                 start, finish, aliases)


def _add_pair(parts, sib, core, name):
    P4, _, Rh, C = parts.shape
    tm, tc = _tile2(Rh, C, 16)

    def body(c_ref, a_ref, b_ref, o_ref):
        o_ref[...] = (a_ref[0].astype(F32) + b_ref[...].astype(F32)).astype(BF16)

    spec = pl.BlockSpec((1, tm, tc), lambda p, i, j, c_ref: (p, i, j))
    return pl.pallas_call(
        body, name=name, out_shape=jax.ShapeDtypeStruct((P4, Rh, C), BF16),
        grid_spec=pltpu.PrefetchScalarGridSpec(
            num_scalar_prefetch=1, grid=(P4, Rh // tm, C // tc),
            in_specs=[pl.BlockSpec((1, 1, tm, tc), lambda p, i, j, c_ref: (p, c_ref[0], i, j)), spec], out_specs=spec),
        compiler_params=_params(("parallel",) * 3),
    )(core, parts, sib)


def _sum_slabs(pre, recv, chip, name):
    _, Rh, C = pre.shape
    tm, tc = _tile2(Rh, C, 16)

    def body(me_ref, own_ref, r_ref, o_ref):
        acc = own_ref[0].astype(F32)
        for j in range(3):
            acc = acc + r_ref[j].astype(F32)
        o_ref[...] = acc

    return pl.pallas_call(
        body, name=name, out_shape=jax.ShapeDtypeStruct((Rh, C), F32),
        grid_spec=pltpu.PrefetchScalarGridSpec(
            num_scalar_prefetch=1, grid=(Rh // tm, C // tc),
            in_specs=[pl.BlockSpec((1, tm, tc), lambda i, j, me_ref: (me_ref[0], i, j)),
                      pl.BlockSpec((3, tm, tc), lambda i, j, me_ref: (0, i, j))],
            out_specs=pl.BlockSpec((tm, tc), lambda i, j, me_ref: (i, j))),
        compiler_params=_params(("parallel", "parallel")),
    )(chip, pre, recv)


def kernel(x, c, positions, w_ada, b_ada, w_in, g_q_a, w_q_b, g_kv_a, w_kv_b, w_o_a, w_conv, w_o_b, w_o, ln1_g, ln1_b, w_ffn_in, w_ffn_out, ln2_g, ln2_b, loss_target, m_w_ada, m_b_ada, m_w_in, m_g_q_a, m_w_q_b, m_g_kv_a, m_w_kv_b, m_w_o_a, m_w_conv, m_w_o_b, m_w_o, m_ln1_g, m_ln1_b, m_w_ffn_in, m_w_ffn_out, m_ln2_g, m_ln2_b, v_w_ada, v_b_ada, v_w_in, v_g_q_a, v_w_q_b, v_g_kv_a, v_w_kv_b, v_w_o_a, v_w_conv, v_w_o_b, v_w_o, v_ln1_g, v_ln1_b, v_w_ffn_in, v_w_ffn_out, v_ln2_g, v_ln2_b):
    S, D = x.shape[1], x.shape[2]
    F = w_ffn_out.shape[1] * 4
    ax, ay, ac = _place()
    chip = 2 * ax + ay
    dev = 4 * ax + 2 * ay + ac
    x2, tgt = x[0], loss_target[0]
    w_ada2, w_in2, w_q_b2, w_kv_b2 = w_ada[0], w_in[0], w_q_b[0], w_kv_b[0]
    w_o_a2, w_o_b2, w_o2, w_ffn_in2, w_ffn_out2 = w_o_a[0], w_o_b[0], w_o[0], w_ffn_in[0], w_ffn_out[0]
    NA = w_ada2.shape[1]
    CW = w_conv.shape[2]

    inv_freq = 1.0 / (ROPE_THETA ** (jnp.arange(0, QK_ROPE, 2, dtype=F32) / QK_ROPE))
    ang = positions[0].astype(F32)[:, None] * inv_freq
    cos, sin = jnp.cos(ang), jnp.sin(ang)
    z32, z64, z96 = jnp.zeros((S, 32), F32), jnp.zeros((S, 64), F32), jnp.zeros((S, 96), F32)
    tab = jnp.concatenate([cos, cos, z64, -sin, z96, z32, sin, z64], axis=1)

    def halves(a):
        return a.reshape(2, a.shape[0] // 2, a.shape[1])

    def whole(g):
        return g.reshape(4, 2 * g.shape[2], g.shape[3])

    def cols(g):
        return jnp.transpose(g, (1, 0, 2)).reshape(g.shape[1], 4 * g.shape[2])

    w_inT, m_w_inT, v_w_inT = w_in2.T, m_w_in[0].T, v_w_in[0].T
    CS = w_inT.shape[0]
    CSP = -(-CS // 32) * 32
    sh_in = halves(jnp.pad(w_inT.astype(BF16), ((0, CSP - CS), (0, 0))))
    sh_qb, sh_kvb, sh_oa, sh_ob, sh_o, sh_fi, sh_fo = (
        halves(w.astype(BF16)) for w in (w_q_b2, w_kv_b2, w_o_a2, w_o_b2, w_o2, w_ffn_in2, w_ffn_out2))
    g_in = whole(_run_plan(_gather_plan([sh_in]), "gather_first")[0])

    def in_rows(lo, hi):
        parts = [g_in[p, max(lo, p * CS) - p * CS:min(hi, (p + 1) * CS) - p * CS]
                 for p in range(4) if max(lo, p * CS) < min(hi, (p + 1) * CS)]
        return parts[0] if len(parts) == 1 else jnp.concatenate(parts, axis=0)

    n_qkv = Q_LORA + KV_LORA + QK_ROPE
    W_qkvT = jnp.pad(in_rows(0, n_qkv), ((0, QKV_A - n_qkv), (0, 0)))
    W_convT = in_rows(n_qkv, n_qkv + 3 * D)
    W_gateT = in_rows(n_qkv + 3 * D, n_qkv + 5 * D)

    c_all = _all_gather8(c, "gather_c").reshape(8, D)
    wconv_all = _all_gather8(w_conv[0], "gather_wconv")
    w_conv_full = jnp.transpose(wconv_all[0::2], (1, 0, 2)).reshape(3, D)
    b_sh = lax.dynamic_slice(b_ada, (0, chip * NA), (1, NA))
    mod_sh = _ada_fwd(c_all, w_ada2, b_sh)
    mod_all = _all_gather8(mod_sh, "gather_mod")
    mod = lax.dynamic_slice(mod_all[0::2], (0, dev, 0), (4, 1, NA)).reshape(6, D)
    shift1, scale1, gate1, shift2, scale2, gate2 = (mod[k:k + 1] for k in range(6))

    u = _modulate(x2, scale1, shift1, "modulate1")
    pq, (g_qb, g_kvb) = _matmul(u, W_qkvT, "nt", F32, "proj_qkv", carry=_gather_plan([sh_qb, sh_kvb]))
    W_qb = jnp.pad(cols(whole(g_qb)).reshape(Q_LORA, N_HEADS, QK_NOPE + QK_ROPE),
                   ((0, 0), (0, 0), (0, QK_PAD - QK_NOPE - QK_ROPE))).reshape(Q_LORA, N_HEADS * QK_PAD)
    W_kvb = cols(whole(g_kvb))
    pc, (g_oa, g_ob) = _matmul(u, W_convT, "nt", F32, "proj_conv", carry=_gather_plan([sh_oa, sh_ob]))
    pg, (g_o,) = _matmul(u, W_gateT, "nt", BF16, "proj_gate", carry=_gather_plan([sh_o]))
    W_oa, W_ob, W_o = (g.reshape(-1, D) for g in (g_oa, g_ob, g_o))
    rq, rkv, kr = _rms_fwd(pq, tab, g_q_a, g_kv_a)
    q = _q_rope(_matmul(rq, W_qb, "nn", F32, "q_b"), tab)
    kv = _matmul(rkv, W_kvb, "nn", BF16, "kv_b")
    o, lse, g_fi = _attn_fwd(q, kv, kr, carry=_gather_plan([sh_fi], (0, 8, 6)))
    y_a, g_fi = _matmul(o, W_oa, "nn", F32, "o_a", carry=_gather_plan([sh_fi], (6, 8), g_fi))
    hb = _conv_fwd(pc, w_conv_full)
    y_b = _matmul(hb, W_ob, "nn", F32, "o_b")
    merged = _merge_fwd(y_a, y_b, pg)
    mix, g_fi = _matmul(merged, W_o, "nn", F32, "w_o", carry=_gather_plan([sh_fi], (7, 8), g_fi))
    W_fi = whole(g_fi[0])
    x1, u2 = _ln1_fwd(x2, mix, gate1, ln1_g, ln1_b, scale2, shift2)
    hh, (g_fo,) = _matmul(u2, W_fi, "nn", BF16, "ffn_in", carry=_gather_plan([sh_fo]), shards="b")
    W_fo = g_fo.reshape(F, D)
    act = _swiglu_fwd(hh)
    ffn = _matmul(act, W_fo, "nn", F32, "ffn_out")

    core_i = ac.astype(jnp.int32).reshape(1)
    chip_i = chip.astype(jnp.int32).reshape(1)

    def uncols(g):
        return jnp.transpose(g.reshape(g.shape[0], 4, g.shape[1] // 4), (1, 0, 2))

    def slabs(p):
        return p.reshape(4, 2, p.shape[1] // 2, p.shape[2])

    def add_pairs(parts, sibs, nms):
        return [_add_pair(a, b, core_i, "add_pair_" + nm) for a, b, nm in zip(parts, sibs, nms)]

    def sum_all(pre, recv, nms):
        return [_sum_slabs(a, r, chip_i, "sum_slabs_" + nm) for a, r, nm in zip(pre, recv, nms)]

    dffn, dx1a, loss_acc, d_ln2_g, d_ln2_b, d_gate2 = _ln2_loss_bwd(x1, ffn, gate2, ln2_g, ln2_b, tgt)
    loss = lax.psum(loss_acc[0, 0], ("x", "y", "c"))
    dW_fo = _matmul(act, dffn, "tn", BF16, "d_w_ffn_out")
    p_fo = [slabs(dW_fo.reshape(4, -1, D))]
    dact, s_fo = _matmul(dffn, W_fo, "nt", BF16, "d_act", carry=_pair_plan(p_fo))
    pre_fo = add_pairs(p_fo, s_fo, ["w_ffn_out"])
    dhh = _swiglu_bwd(dact, hh)
    dW_fi, r_fo = _matmul(u2, dhh, "tn", BF16, "d_w_ffn_in", carry=_scatter_plan(pre_fo), shards="o")
    p_fi = [slabs(dW_fi)]
    du2, s_fi = _matmul(dhh, W_fi, "nt", F32, "d_u2", carry=_pair_plan(p_fi), shards="b")
    pre_fi = add_pairs(p_fi, s_fi, ["w_ffn_in"])
    dmix, dxa, d_shift2, d_scale2, d_ln1_g, d_ln1_b, d_gate1 = _ln1_bwd(x2, mix, dx1a, du2, gate1, ln1_g, ln1_b, scale2)
    dW_o = _matmul(merged, dmix, "tn", BF16, "d_w_o")
    dmerged = _matmul(dmix, W_o, "nt", F32, "d_merged")
    dy_a, dy_b, dgate = _merge_bwd(dmerged, y_a, y_b, pg)
    dW_oa = _matmul(o, dy_a, "tn", BF16, "d_w_o_a")
    do = _matmul(dy_a, W_oa, "nt", BF16, "d_o")
    dW_ob = _matmul(hb, dy_b, "tn", BF16, "d_w_o_b")
    p_mid = [slabs(g.reshape(4, -1, D)) for g in (dW_oa, dW_ob, dW_o)]
    dhb, s_mid = _matmul(dy_b, W_ob, "nt", F32, "d_hb", carry=_pair_plan(p_mid))
    pre_mid = add_pairs(p_mid, s_mid, ["w_o_a", "w_o_b", "w_o"])
    dconv, d_wconv = _conv_bwd(dhb, pc, w_conv_full)
    dq, dkv, dkr, r_fi = _attn_bwd(q, kv, kr, do, o, lse, tab, carry=_scatter_plan(pre_fi))
    names_a = ["w_ffn_out", "w_ffn_in", "w_o_a", "w_o_b", "w_o"]
    dW_qb = _matmul(rq, dq, "tn", BF16, "d_w_q_b")
    d_rq = _matmul(dq, W_qb, "nt", F32, "d_rq")
    dW_kvb = _matmul(rkv, dkv, "tn", BF16, "d_w_kv_b")
    d_rkv = _matmul(dkv, W_kvb, "nt", F32, "d_rkv")
    dqkv, d_g_q, d_g_kv = _rms_bwd(d_rq, d_rkv, pq, dkr, g_q_a, g_kv_a)
    dW_qkvT = _matmul(dqkv, u, "tn", BF16, "d_w_qkv")
    dW_convT, r_mid = _matmul(dconv, u, "tn", BF16, "d_w_conv", carry=_scatter_plan(pre_mid, (0, 2)))
    dW_gateT, r_mid = _matmul(dgate, u, "tn", BF16, "d_w_gate", carry=_scatter_plan(pre_mid, (1, 2), r_mid))
    fin_a = sum_all(pre_fo + pre_fi + pre_mid, list(r_fo) + list(r_fi) + list(r_mid), names_a)
    dW_inT = jnp.concatenate([dW_qkvT[:n_qkv], dW_convT, dW_gateT], axis=0).reshape(4, CS, D)
    dW_inT = jnp.pad(dW_inT, ((0, 0), (0, CSP - CS), (0, 0)))
    dW_qb_u = dW_qb.reshape(Q_LORA, N_HEADS, QK_PAD)[:, :, :QK_NOPE + QK_ROPE].reshape(Q_LORA, -1)
    names_b = ["w_in", "w_q_b", "w_kv_b"]
    p_b = [slabs(dW_inT), slabs(uncols(dW_qb_u)), slabs(uncols(dW_kvb))]
    du, s_b = _matmul(dqkv, W_qkvT, "nn", F32, "d_u_qkv", carry=_pair_plan(p_b))
    pre_b = add_pairs(p_b, s_b, names_b)
    b_ssem, b_rsem, pre_b, lands_b, du = _scatter_start(pre_b, du, "scatter_last_start")
    du, fs_a = _matmul(dconv, W_convT, "nn", F32, "d_u_conv", add=du, carry=_sibling_plan(fin_a))
    du = _matmul(dgate, W_gateT, "nn", F32, "d_u_gate", add=du)
    grad_x, d_shift1, d_scale1 = _dx_final(dxa, du, x2, scale1)

    def pad_d(v):
        return jnp.pad(v, ((0, 0), (0, D - v.shape[1])))

    small = _pack_rows([d_ln1_g, d_ln1_b, d_ln2_g, d_ln2_b, pad_d(d_g_q), pad_d(d_g_kv), d_wconv,
                         d_shift1, d_scale1, d_gate1, d_shift2, d_scale2, d_gate2], 16)
    small_all = _all_gather8(small, "gather_small")
    small_sum = _sum8(small_all)
    g_ln1_g, g_ln1_b, g_ln2_g, g_ln2_b = (small_sum[k:k + 1] for k in range(4))
    g_g_q, g_g_kv = small_sum[4:5, :Q_LORA], small_sum[5:6, :KV_LORA]
    g_wconv = lax.dynamic_slice(small_sum[6:9], (0, chip * CW), (3, CW))
    g_b_ada = small_sum[9:15].reshape(1, 6 * D)
    dmod_all = small_all[:, 9:15, :].reshape(8, 6 * D)
    g_w_ada = _ada_bwd(c_all, lax.dynamic_slice(dmod_all, (0, chip * NA), (8, NA)))

    big = {}
    ws = dict(w_in=(w_inT, m_w_inT, v_w_inT), w_q_b=(w_q_b2, m_w_q_b[0], v_w_q_b[0]),
              w_kv_b=(w_kv_b2, m_w_kv_b[0], v_w_kv_b[0]), w_o_a=(w_o_a2, m_w_o_a[0], v_w_o_a[0]),
              w_o_b=(w_o_b2, m_w_o_b[0], v_w_o_b[0]), w_o=(w_o2, m_w_o[0], v_w_o[0]),
              w_ffn_in=(w_ffn_in2, m_w_ffn_in[0], v_w_ffn_in[0]), w_ffn_out=(w_ffn_out2, m_w_ffn_out[0], v_w_ffn_out[0]))
    def adam_of(nm, a, b, carry=None):
        w_, m_, v_ = ws[nm]
        return _adam_halves("adam_" + nm, w_, m_, v_, a, b, core_i, carry)

    for nm, a, b in zip(names_a, fin_a, fs_a):
        big[nm] = adam_of(nm, a, b)
    big["w_ada"] = [g_w_ada] + list(_adam("adam_w_ada", w_ada2, m_w_ada[0], v_w_ada[0], g_w_ada))
    done = [big[nm][1] for nm in names_a] + [big["w_ada"][1], grad_x]
    r_b = _scatter_wait(b_ssem, b_rsem, pre_b, lands_b, done, "scatter_last_wait")
    fin_b = sum_all(pre_b, r_b, names_b)
    fs_b = _run_plan(_sibling_plan(fin_b), "sibling_last")
    for nm, a, b in zip(names_b, fin_b, fs_b):
        big[nm] = adam_of(nm, a, b)
    sm = {}
    for nm, w_, m_, v_, g_ in [("b_ada", b_ada, m_b_ada, v_b_ada, g_b_ada), ("g_q_a", g_q_a, m_g_q_a, v_g_q_a, g_g_q),
                               ("g_kv_a", g_kv_a, m_g_kv_a, v_g_kv_a, g_g_kv),
                               ("w_conv", w_conv[0], m_w_conv[0], v_w_conv[0], g_wconv),
                               ("ln1_g", ln1_g, m_ln1_g, v_ln1_g, g_ln1_g), ("ln1_b", ln1_b, m_ln1_b, v_ln1_b, g_ln1_b),
                               ("ln2_g", ln2_g, m_ln2_g, v_ln2_g, g_ln2_g), ("ln2_b", ln2_b, m_ln2_b, v_ln2_b, g_ln2_b)]:
        sm[nm] = (g_,) + tuple(_adam_small("adam_" + nm, w_, m_, v_, g_))

    order = ["w_ada", "b_ada", "w_in", "g_q_a", "w_q_b", "g_kv_a", "w_kv_b", "w_o_a", "w_conv", "w_o_b", "w_o",
             "ln1_g", "ln1_b", "w_ffn_in", "w_ffn_out", "ln2_g", "ln2_b"]
    lead = {"b_ada", "g_q_a", "g_kv_a", "ln1_g", "ln1_b", "ln2_g", "ln2_b"}

    def leaf(nm, k):
        val = big[nm][k] if nm in big else sm[nm][k]
        if nm == "w_in":
            val = val.T
        return val if nm in lead else val[None]

    outs = [loss, grad_x[None]]
    for k in range(4):
        outs += [leaf(nm, k) for nm in order]
    return tuple(outs)
```

```python
import functools

import jax
import jax.numpy as jnp
from jax import lax
from jax.experimental import pallas as pl
from jax.experimental.pallas import tpu as pltpu

F32, BF16 = jnp.float32, jnp.bfloat16
N_HEADS, QK_NOPE, QK_ROPE, V_HEAD = 16, 128, 64, 128
Q_LORA, KV_LORA = 512, 512
QK_PAD = 256
QKV_A = 1152
CHUNK_SHIFT = 6
ATTN_SCALE = (QK_NOPE + QK_ROPE) ** -0.5
ROPE_THETA = 10000.0
ALPHA = 2.0 ** 0.25
LN_EPS, RMS_EPS = 1e-5, 1e-6
ADAM_LR, ADAM_B1, ADAM_B2, ADAM_EPS, ADAM_WD, ADAM_STEP = 0.001, 0.9, 0.999, 1e-08, 0.01, 10
ADAM_C1 = 1.0 - ADAM_B1 ** ADAM_STEP
ADAM_C2 = 1.0 - ADAM_B2 ** ADAM_STEP
VMEM_LIMIT = 56 * 1024 * 1024
MESH = pl.DeviceIdType.MESH
ANY = pl.BlockSpec(memory_space=pl.ANY)
HBM_SPEC = pl.BlockSpec(memory_space=pltpu.HBM)
SEM_SPEC = pl.BlockSpec(memory_space=pltpu.SEMAPHORE)
NT = (((1,), (1,)), ((), ()))
TN = (((0,), (0,)), ((), ()))
NN = (((1,), (0,)), ((), ()))


def _params(sem=None):
    return pltpu.CompilerParams(dimension_semantics=sem, vmem_limit_bytes=VMEM_LIMIT)


def _pick(n, cands=(1408, 1024, 512, 384, 256, 128)):
    for t in cands:
        if n % t == 0:
            return t
    return n


def _row_tile(rows, row_bytes, budget, mult=8):
    best = mult
    for t in range(mult, rows + 1, mult):
        if rows % t == 0 and t * row_bytes <= budget:
            best = t
    return best


def _tile2(rows, cols, mult=8, budget=3 << 18):
    col_tiles = [t for t in range(128, cols + 1, 128) if cols % t == 0] or [cols]
    best = None
    for tc in col_tiles:
        for tr in range(mult, rows + 1, mult):
            if rows % tr == 0 and tr * tc <= budget and (best is None or (tr * tc, tc) > (best[0] * best[1], best[1])):
                best = (tr, tc)
    assert best is not None, (rows, cols)
    return best


def _sigmoid(x):
    return jax.nn.sigmoid(x)


class _Plan:
    def __init__(self, ins, outs, sems, start, finish, aliases=None):
        self.ins, self.outs, self.sems, self.start, self.finish = list(ins), list(outs), list(sems), start, finish
        self.aliases = dict(aliases or {})

    def io_aliases(self, first_in, first_out):
        return {first_in + i: first_out + o for i, o in self.aliases.items()}


def _run_plan(plan, name, ride=None):
    n_in, n_out = len(plan.ins), len(plan.outs)
    extra = [] if ride is None else [ride]
    aliases = plan.io_aliases(0, 0)
    if extra:
        aliases[n_in] = n_out

    def body(*refs):
        ins, outs, sems = refs[:n_in], refs[n_in + len(extra):n_in + len(extra) + n_out], refs[n_in + 2 * len(extra) + n_out:]
        plan.start(ins, outs, sems)
        plan.finish(ins, outs, sems)

    return pl.pallas_call(body, name=name, out_shape=plan.outs + [jax.ShapeDtypeStruct(r.shape, r.dtype) for r in extra],
                          in_specs=[ANY] * (n_in + len(extra)), out_specs=[ANY] * (n_out + len(extra)),
                          scratch_shapes=plan.sems, input_output_aliases=aliases,
                          compiler_params=_params())(*plan.ins, *extra)


def _matmul(a, b, mode, out_dtype, name, add=None, carry=None, shards=None):
    if mode == "nn":
        (M, K), N, dims = a.shape, b.shape[-1] * (4 if shards else 1), NN
    elif mode == "nt":
        (M, K), N, dims = a.shape, b.shape[-2], NT
    else:
        (K, M), N, dims = a.shape, b.shape[1], TN
    split_n = shards and mode != "nt"
    tm = _pick(M)
    tn = _pick(N // 4) if split_n else _pick(N)
    if shards and mode == "nt":
        tk = _pick(K // 4)
    else:
        tk = K if K <= 2048 else _pick(K)
    nk = K // tk
    per = (N // 4 // tn) if split_n else (K // 4 // tk if shards else 1)
    a_spec = (pl.BlockSpec((tk, tm), lambda i, j, k: (k, i)) if mode == "tn"
              else pl.BlockSpec((tm, tk), lambda i, j, k: (i, k)))
    if shards == "b" and mode == "nn":
        b_spec = pl.BlockSpec((None, tk, tn), lambda i, j, k: (j // per, k, j % per))
    elif shards == "b":
        b_spec = pl.BlockSpec((None, tn, tk), lambda i, j, k: (k // per, j, k % per))
    else:
        b_spec = (pl.BlockSpec((tn, tk), lambda i, j, k: (j, k)) if mode == "nt"
                  else pl.BlockSpec((tk, tn), lambda i, j, k: (k, j)))
    o_spec = pl.BlockSpec((tm, tn), lambda i, j, k: (i, j))
    o_shape = (M, N)
    if shards == "o":
        o_spec, o_shape = pl.BlockSpec((None, tm, tn), lambda i, j, k: (j // per, i, j % per)), (4, M, N // 4)
    has_add = add is not None
    n_ci = len(carry.ins) if carry else 0
    n_co = len(carry.outs) if carry else 0
    n_in = 2 + has_add
    grid = (M // tm, N // tn, nk)

    def body(*refs):
        a_ref, b_ref = refs[0], refs[1]
        add_ref = refs[2] if has_add else None
        o_ref = refs[n_in + n_ci]
        acc_ref = refs[n_in + n_ci + 1 + n_co] if nk > 1 else None
        c_ins = refs[n_in:n_in + n_ci]
        c_outs = refs[n_in + n_ci + 1:n_in + n_ci + 1 + n_co]
        c_sems = refs[n_in + n_ci + 1 + n_co + (nk > 1):]
        i, j, k = pl.program_id(0), pl.program_id(1), pl.program_id(2)

        if carry:
            @pl.when((i == 0) & (j == 0) & (k == 0))
            def _():
                carry.start(c_ins, c_outs, c_sems)

        part = lax.dot_general(a_ref[...], b_ref[...], dims, preferred_element_type=F32)
        if nk == 1:
            o_ref[...] = (part + add_ref[...] if has_add else part).astype(o_ref.dtype)
        else:
            @pl.when(k == 0)
            def _():
                acc_ref[...] = part

            @pl.when((k > 0) & (k < nk - 1))
            def _():
                acc_ref[...] += part

            @pl.when(k == nk - 1)
            def _():
                r = acc_ref[...] + part
                if has_add:
                    r = r + add_ref[...]
                o_ref[...] = r.astype(o_ref.dtype)

        if carry:
            @pl.when((i == grid[0] - 1) & (j == grid[1] - 1) & (k == nk - 1))
            def _():
                carry.finish(c_ins, c_outs, c_sems)

    ins = [a, b] + ([add] if has_add else []) + (carry.ins if carry else [])
    in_specs = [a_spec, b_spec] + ([o_spec] if has_add else []) + [ANY] * n_ci
    res = pl.pallas_call(
        body, name=name, grid=grid,
        in_specs=in_specs, out_specs=[o_spec] + [ANY] * n_co,
        out_shape=[jax.ShapeDtypeStruct(o_shape, out_dtype)] + (carry.outs if carry else []),
        scratch_shapes=([pltpu.VMEM((tm, tn), F32)] if nk > 1 else []) + (carry.sems if carry else []),
        input_output_aliases=carry.io_aliases(n_in, 1) if carry else {},
        compiler_params=_params(("arbitrary",) * 3 if carry else ("parallel", "parallel", "arbitrary")),
    )(*ins)
    return (res[0], res[1:]) if carry else res[0]


def _rows(body, name, n_rows, tm, ins, outs, accs=()):
    grid = (n_rows // tm,)
    per8 = tm // 8
    last8 = n_rows // 8 - 1
    arrays, in_specs = [], []
    for spec in ins:
        kind, arr = spec[0], spec[1]
        arrays.append(arr)
        if kind == "row":
            _, _, cb, w = spec
            in_specs.append(pl.BlockSpec((tm, w), lambda i, cb=cb: (i, cb)))
        elif kind == "full":
            in_specs.append(pl.BlockSpec(arr.shape, lambda i, nd=arr.ndim: (0,) * nd))
        elif kind == "prev":
            _, _, cb, w = spec
            in_specs.append(pl.BlockSpec((8, w), lambda i, cb=cb: (jnp.maximum(i * per8 - 1, 0), cb)))
        else:
            _, _, cb, w = spec
            in_specs.append(pl.BlockSpec((8, w), lambda i, cb=cb: (jnp.minimum((i + 1) * per8, last8), cb)))
    out_shape = [jax.ShapeDtypeStruct((n_rows, w), dt) for (w, dt) in outs]
    out_specs = [pl.BlockSpec((tm, w), lambda i: (i, 0)) for (w, _) in outs]
    out_shape += [jax.ShapeDtypeStruct(s, F32) for s in accs]
    out_specs += [pl.BlockSpec(s, lambda i, nd=len(s): (0,) * nd) for s in accs]
    n_in, n_out = len(ins), len(outs)

    def kernel_body(*refs):
        body(pl.program_id(0), refs[:n_in], refs[n_in:n_in + n_out], refs[n_in + n_out:])

    res = pl.pallas_call(
        kernel_body, name=name, grid=grid, in_specs=in_specs, out_specs=out_specs, out_shape=out_shape,
        compiler_params=_params(("arbitrary",)),
    )(*arrays)
    return res


def _acc_add(i, ref, val):
    @pl.when(i == 0)
    def _():
        ref[...] = val

    @pl.when(i > 0)
    def _():
        ref[...] += val


def _rope(t, tab, sign):
    c, sa, sb = tab[:, 0:128], tab[:, 128:256], tab[:, 256:384]
    rot = pltpu.roll(t, 96, 1) * sa + pltpu.roll(t, 32, 1) * sb
    return t * c + rot if sign > 0 else t * c - rot


def _ln_stats(r):
    mu = jnp.mean(r, axis=-1, keepdims=True)
    d = r - mu
    var = jnp.mean(d * d, axis=-1, keepdims=True)
    rstd = lax.rsqrt(var + LN_EPS)
    return d * rstd, rstd


def _ln_bwd(dxh, xh, rstd):
    m1 = jnp.mean(dxh, axis=-1, keepdims=True)
    m2 = jnp.mean(dxh * xh, axis=-1, keepdims=True)
    return rstd * (dxh - m1 - xh * m2)


def _modulate(x, scale, shift, name):
    S, D = x.shape

    def body(i, ins, outs, accs):
        outs[0][...] = (ins[0][...] * (1.0 + ins[1][...]) + ins[2][...]).astype(BF16)

    return _rows(body, name, S, _pick(S, (256, 128)), [("row", x, 0, D), ("full", scale), ("full", shift)], [(D, BF16)])[0]


def _rms_fwd(pq, tab, g_q, g_kv):
    S = pq.shape[0]

    def body(i, ins, outs, accs):
        pq_ref, tab_ref, gq_ref, gkv_ref = ins

        def rms(x, g):
            return x * lax.rsqrt(jnp.mean(x * x, axis=-1, keepdims=True) + RMS_EPS) * g

        outs[0][...] = rms(pq_ref[:, 0:Q_LORA], gq_ref[...]).astype(BF16)
        outs[1][...] = rms(pq_ref[:, Q_LORA:Q_LORA + KV_LORA], gkv_ref[...]).astype(BF16)
        outs[2][...] = _rope(pq_ref[:, Q_LORA + KV_LORA:QKV_A], tab_ref[...], 1).astype(BF16)

    return _rows(body, "rms_fwd", S, _pick(S, (256, 128)),
                 [("row", pq, 0, QKV_A), ("row", tab, 0, 384), ("full", g_q), ("full", g_kv)],
                 [(Q_LORA, BF16), (KV_LORA, BF16), (128, BF16)])


def _q_rope(q, tab):
    S, W = q.shape

    def body(i, ins, outs, accs):
        q_ref, tab_ref = ins
        t = tab_ref[...]
        for h in range(N_HEADS):
            lo = h * QK_PAD
            outs[0][:, lo:lo + 128] = q_ref[:, lo:lo + 128].astype(BF16)
            outs[0][:, lo + 128:lo + 256] = _rope(q_ref[:, lo + 128:lo + 256], t, 1).astype(BF16)

    return _rows(body, "q_rope", S, _pick(S, (256, 128)), [("row", q, 0, W), ("row", tab, 0, 384)], [(W, BF16)])[0]


def _allowed(q0, k0, bq):
    row = q0 + lax.broadcasted_iota(jnp.int32, (bq, bq), 0)
    col = k0 + lax.broadcasted_iota(jnp.int32, (bq, bq), 1)
    return (col >> CHUNK_SHIFT) <= (row >> CHUNK_SHIFT)


ATTN_BLOCK = 512


def _attn_fwd(q, kv, kr, carry=None):
    S = q.shape[0]
    bq = min(ATTN_BLOCK, S)
    nq = S // bq
    n_ci = len(carry.ins) if carry else 0
    n_co = len(carry.outs) if carry else 0

    def body(*refs):
        q_ref, kn_ref, v_ref, kr_ref = refs[:4]
        o_ref, lse_ref = refs[4 + n_ci:6 + n_ci]
        c_ins, c_outs = refs[4:4 + n_ci], refs[6 + n_ci:6 + n_ci + n_co]
        kcat = refs[6 + n_ci + n_co]
        c_sems = refs[7 + n_ci + n_co:]
        qi = pl.program_id(1)
        if carry:
            @pl.when((pl.program_id(0) == 0) & (qi == 0))
            def _():
                carry.start(c_ins, c_outs, c_sems)

        @pl.when(qi == 0)
        def _():
            kcat[:, 0:128] = kn_ref[...]
            kcat[:, 128:256] = kr_ref[...]

        qv = q_ref[...]

        def step(j, carry, masked):
            m, l, acc = carry
            off = pl.multiple_of(j * bq, bq)
            s = lax.dot_general(qv, kcat[pl.ds(off, bq), :], NT, preferred_element_type=F32) * ATTN_SCALE
            if masked:
                s = jnp.where(_allowed(qi * bq, off, bq), s, -1e30)
            m_new = jnp.maximum(m, jnp.max(s, axis=1, keepdims=True))
            a = jnp.exp(m - m_new)
            p = jnp.exp(s - m_new)
            l = a * l + jnp.sum(p, axis=1, keepdims=True)
            acc = a * acc + jnp.dot(p.astype(BF16), v_ref[pl.ds(off, bq), :], preferred_element_type=F32)
            return m_new, l, acc

        init = (jnp.full((bq, 1), -1e30, F32), jnp.zeros((bq, 1), F32), jnp.zeros((bq, V_HEAD), F32))
        below = lax.fori_loop(0, qi, lambda j, cr: step(j, cr, False), init)
        m, l, acc = step(qi, below, True)
        o_ref[...] = (acc / l).astype(BF16)
        lse_ref[0] = m + jnp.log(l)
        if carry:
            @pl.when((pl.program_id(0) == N_HEADS - 1) & (qi == nq - 1))
            def _():
                carry.finish(c_ins, c_outs, c_sems)

    res = pl.pallas_call(
        body, name="attn_fwd", grid=(N_HEADS, nq),
        in_specs=[pl.BlockSpec((bq, QK_PAD), lambda h, i: (i, h)),
                  pl.BlockSpec((S, 128), lambda h, i: (0, 2 * h)),
                  pl.BlockSpec((S, 128), lambda h, i: (0, 2 * h + 1)),
                  pl.BlockSpec((S, 128), lambda h, i: (0, 0))] + [ANY] * n_ci,
        out_specs=[pl.BlockSpec((bq, V_HEAD), lambda h, i: (i, h)),
                   pl.BlockSpec((1, bq, 1), lambda h, i: (h, i, 0))] + [ANY] * n_co,
        out_shape=[jax.ShapeDtypeStruct((S, N_HEADS * V_HEAD), BF16),
                   jax.ShapeDtypeStruct((N_HEADS, S, 1), F32)] + (carry.outs if carry else []),
        scratch_shapes=[pltpu.VMEM((S, QK_PAD), BF16)] + (carry.sems if carry else []),
        input_output_aliases=carry.io_aliases(4, 2) if carry else {},
        compiler_params=_params(("arbitrary", "arbitrary")),
    )(q, kv, kv, kr, *(carry.ins if carry else []))
    return res[0], res[1], res[2:]


def _attn_bwd(q, kv, kr, do, o, lse, tab, carry=None):
    S = q.shape[0]
    bq = min(ATTN_BLOCK, S)
    nq = S // bq

    n_ci = len(carry.ins) if carry else 0
    n_co = len(carry.outs) if carry else 0

    def body(*refs):
        q_ref, kn_ref, v_ref, kr_ref, do_ref, o_ref, lse_ref, tab_ref = refs[:8]
        dq_ref, dkv_ref, dkr_ref = refs[8 + n_ci:11 + n_ci]
        dq_acc, dk_acc, dv_acc, kcat, delta = refs[11 + n_ci + n_co:16 + n_ci + n_co]
        c_ins, c_outs, c_sems = refs[8:8 + n_ci], refs[11 + n_ci:11 + n_ci + n_co], refs[16 + n_ci + n_co:]
        h = pl.program_id(0)
        if carry:
            @pl.when(h == 0)
            def _():
                carry.start(c_ins, c_outs, c_sems)

        dq_acc[...] = jnp.zeros_like(dq_acc)
        dk_acc[...] = jnp.zeros_like(dk_acc)
        dv_acc[...] = jnp.zeros_like(dv_acc)
        kcat[:, 0:128] = kn_ref[...]
        kcat[:, 128:256] = kr_ref[...]
        for r in range(nq):
            rows = slice(r * bq, (r + 1) * bq)
            delta[rows, :] = jnp.sum(do_ref[rows, :].astype(F32) * o_ref[rows, :].astype(F32), axis=1, keepdims=True)

        def pair(i, j, masked):
            rows_i = pl.ds(pl.multiple_of(i * bq, bq), bq)
            rows_j = pl.ds(pl.multiple_of(j * bq, bq), bq)
            qv, dov, k = q_ref[rows_i, :], do_ref[rows_i, :], kcat[rows_j, :]
            s = lax.dot_general(qv, k, NT, preferred_element_type=F32) * ATTN_SCALE
            if masked:
                s = jnp.where(_allowed(i * bq, j * bq, bq), s, -1e30)
            p = jnp.exp(s - lse_ref[0, rows_i, :])
            dv_acc[rows_j, :] += lax.dot_general(p.astype(BF16), dov, TN, preferred_element_type=F32)
            dp = lax.dot_general(dov, v_ref[rows_j, :], NT, preferred_element_type=F32)
            ds = (p * (dp - delta[rows_i, :]) * ATTN_SCALE).astype(BF16)
            dk_acc[rows_j, :] += lax.dot_general(ds, qv, TN, preferred_element_type=F32)
            dq_acc[rows_i, :] += jnp.dot(ds, k, preferred_element_type=F32)

        def kv_step(j, _):
            pair(j, j, True)

            def q_step(i, _):
                pair(i, j, False)
                return 0

            lax.fori_loop(j + 1, nq, q_step, 0)
            return 0

        lax.fori_loop(0, nq, kv_step, 0)

        for r in range(nq):
            rows = slice(r * bq, (r + 1) * bq)
            dq_ref[rows, 0:128] = dq_acc[rows, 0:128].astype(BF16)
            dq_ref[rows, 128:256] = _rope(dq_acc[rows, 128:256], tab_ref[rows, :], -1).astype(BF16)
        dkv_ref[:, 0:128] = dk_acc[:, 0:128].astype(BF16)
        dkv_ref[:, 128:256] = dv_acc[...].astype(BF16)

        @pl.when(h == 0)
        def _():
            dkr_ref[...] = dk_acc[:, 128:256]

        @pl.when(h > 0)
        def _():
            dkr_ref[...] += dk_acc[:, 128:256]

        @pl.when(h == N_HEADS - 1)
        def _():
            for r in range(nq):
                rows = slice(r * bq, (r + 1) * bq)
                dkr_ref[rows, :] = _rope(dkr_ref[rows, :], tab_ref[rows, :], -1)
            if carry:
                carry.finish(c_ins, c_outs, c_sems)

    W = N_HEADS * QK_PAD
    res = pl.pallas_call(
        body, name="attn_bwd", grid=(N_HEADS,),
        in_specs=[pl.BlockSpec((S, QK_PAD), lambda h: (0, h)),
                  pl.BlockSpec((S, 128), lambda h: (0, 2 * h)),
                  pl.BlockSpec((S, 128), lambda h: (0, 2 * h + 1)),
                  pl.BlockSpec((S, 128), lambda h: (0, 0)),
                  pl.BlockSpec((S, V_HEAD), lambda h: (0, h)),
                  pl.BlockSpec((S, V_HEAD), lambda h: (0, h)),
                  pl.BlockSpec((1, S, 1), lambda h: (h, 0, 0)),
                  pl.BlockSpec((S, 384), lambda h: (0, 0))] + [ANY] * n_ci,
        out_specs=[pl.BlockSpec((S, QK_PAD), lambda h: (0, h)),
                   pl.BlockSpec((S, QK_PAD), lambda h: (0, h)),
                   pl.BlockSpec((S, 128), lambda h: (0, 0))] + [ANY] * n_co,
        out_shape=[jax.ShapeDtypeStruct((S, W), BF16), jax.ShapeDtypeStruct((S, W), BF16),
                   jax.ShapeDtypeStruct((S, 128), F32)] + (carry.outs if carry else []),
        scratch_shapes=[pltpu.VMEM((S, QK_PAD), F32), pltpu.VMEM((S, QK_PAD), F32), pltpu.VMEM((S, V_HEAD), F32),
                        pltpu.VMEM((S, QK_PAD), BF16), pltpu.VMEM((S, 1), F32)]
        + (carry.sems if carry else []),
        input_output_aliases=carry.io_aliases(8, 3) if carry else {},
        compiler_params=_params(("arbitrary",)),
    )(q, kv, kv, kr, do, o, lse, tab, *(carry.ins if carry else []))
    return res[0], res[1], res[2], res[3:]


def _shift_down(cur, prev8, i, n):
    tm = cur.shape[0]
    prev8 = jnp.where(i == 0, jnp.zeros_like(prev8), prev8)
    full = jnp.concatenate([prev8, cur], axis=0)
    return pltpu.roll(full, n, 0)[8:8 + tm, :]


def _shift_up(cur, next8, i, last, n):
    tm = cur.shape[0]
    next8 = jnp.where(i == last, jnp.zeros_like(next8), next8)
    full = jnp.concatenate([cur, next8], axis=0)
    return pltpu.roll(full, tm + 8 - n, 0)[0:tm, :]


def _conv_fwd(pc, w_conv):
    S, D = pc.shape[0], pc.shape[1] // 3
    tm = _pick(S, (256, 128))

    def body(i, ins, outs, accs):
        b_ref, c_ref, x_ref, cp_ref, xp_ref, w_ref = ins
        z = c_ref[...] * x_ref[...]
        zp = cp_ref[...] * xp_ref[...]
        cz = w_ref[0:1, :] * _shift_down(z, zp, i, 2) + w_ref[1:2, :] * _shift_down(z, zp, i, 1) + w_ref[2:3, :] * z
        outs[0][...] = (b_ref[...] * cz).astype(BF16)

    return _rows(body, "conv_fwd", S, tm,
                 [("row", pc, 0, D), ("row", pc, 1, D), ("row", pc, 2, D), ("prev", pc, 1, D), ("prev", pc, 2, D),
                  ("full", w_conv)], [(D, BF16)])[0]


def _conv_bwd(dhb, pc, w_conv):
    S, D = dhb.shape
    tm = _pick(S, (256, 128))
    last = S // tm - 1

    def body(i, ins, outs, accs):
        g_ref, b_ref, c_ref, x_ref, cp_ref, xp_ref, gn_ref, bn_ref, w_ref = ins
        w0, w1, w2 = w_ref[0:1, :], w_ref[1:2, :], w_ref[2:3, :]
        c, x, g = c_ref[...], x_ref[...], g_ref[...]
        z = c * x
        zp = cp_ref[...] * xp_ref[...]
        z1, z2 = _shift_down(z, zp, i, 1), _shift_down(z, zp, i, 2)
        cz = w0 * z2 + w1 * z1 + w2 * z
        dcz = g * b_ref[...]
        dczn = gn_ref[...] * bn_ref[...]
        dz = w2 * dcz + w1 * _shift_up(dcz, dczn, i, last, 1) + w0 * _shift_up(dcz, dczn, i, last, 2)
        outs[0][:, 0:D] = (g * cz).astype(BF16)
        outs[0][:, D:2 * D] = (dz * x).astype(BF16)
        outs[0][:, 2 * D:3 * D] = (dz * c).astype(BF16)
        dw = jnp.concatenate([jnp.sum(dcz * z2, axis=0, keepdims=True), jnp.sum(dcz * z1, axis=0, keepdims=True),
                              jnp.sum(dcz * z, axis=0, keepdims=True)], axis=0)
        _acc_add(i, accs[0], dw)

    return _rows(body, "conv_bwd", S, tm,
                 [("row", dhb, 0, D), ("row", pc, 0, D), ("row", pc, 1, D), ("row", pc, 2, D),
                  ("prev", pc, 1, D), ("prev", pc, 2, D), ("next", dhb, 0, D), ("next", pc, 0, D), ("full", w_conv)],
                 [(3 * D, BF16)], [(3, D)])


def _merge_fwd(y_a, y_b, pg):
    S, D = y_a.shape

    def body(i, ins, outs, accs):
        ya, yb, ga, gb = ins
        outs[0][...] = (_sigmoid(ga[...].astype(F32)) * ya[...] + _sigmoid(gb[...].astype(F32)) * yb[...]).astype(BF16)

    return _rows(body, "merge_fwd", S, _pick(S, (256, 128)),
                 [("row", y_a, 0, D), ("row", y_b, 0, D), ("row", pg, 0, D), ("row", pg, 1, D)], [(D, BF16)])[0]


def _merge_bwd(dm, y_a, y_b, pg):
    S, D = dm.shape

    def body(i, ins, outs, accs):
        d, ya, yb = ins[0][...], ins[1][...], ins[2][...]
        sa, sb = _sigmoid(ins[3][...].astype(F32)), _sigmoid(ins[4][...].astype(F32))
        outs[0][...] = (d * sa).astype(BF16)
        outs[1][...] = (d * sb).astype(BF16)
        outs[2][:, 0:D] = (d * ya * (sa * (1.0 - sa))).astype(BF16)
        outs[2][:, D:2 * D] = (d * yb * (sb * (1.0 - sb))).astype(BF16)

    return _rows(body, "merge_bwd", S, _pick(S, (256, 128)),
                 [("row", dm, 0, D), ("row", y_a, 0, D), ("row", y_b, 0, D), ("row", pg, 0, D), ("row", pg, 1, D)],
                 [(D, BF16), (D, BF16), (2 * D, BF16)])


def _ln1_fwd(x, mix, gate1, g, b, scale2, shift2):
    S, D = x.shape

    def body(i, ins, outs, accs):
        x_ref, mix_ref, gate_ref, g_ref, b_ref, sc_ref, sh_ref = ins
        xh, _ = _ln_stats(ALPHA * x_ref[...] + gate_ref[...] * mix_ref[...])
        x1 = xh * g_ref[...] + b_ref[...]
        outs[0][...] = x1
        outs[1][...] = (x1 * (1.0 + sc_ref[...]) + sh_ref[...]).astype(BF16)

    return _rows(body, "ln1_fwd", S, _pick(S, (256, 128)),
                 [("row", x, 0, D), ("row", mix, 0, D), ("full", gate1), ("full", g), ("full", b),
                  ("full", scale2), ("full", shift2)], [(D, F32), (D, BF16)])


def _swiglu_fwd(hh):
    S, F = hh.shape[0], hh.shape[1] // 2

    def body(i, ins, outs, accs):
        hg = ins[0][...].astype(F32)
        outs[0][...] = (hg * _sigmoid(hg) * ins[1][...].astype(F32)).astype(BF16)

    return _rows(body, "swiglu_fwd", S, _pick(S, (128,)), [("row", hh, 0, F), ("row", hh, 1, F)], [(F, BF16)])[0]


def _swiglu_bwd(dact, hh):
    S, F = dact.shape

    def body(i, ins, outs, accs):
        d, hg, hu = ins[0][...].astype(F32), ins[1][...].astype(F32), ins[2][...].astype(F32)
        sg = _sigmoid(hg)
        outs[0][:, 0:F] = (d * hu * (sg * (1.0 + hg * (1.0 - sg)))).astype(BF16)
        outs[0][:, F:2 * F] = (d * (hg * sg)).astype(BF16)

    return _rows(body, "swiglu_bwd", S, _pick(S, (128,)),
                 [("row", dact, 0, F), ("row", hh, 0, F), ("row", hh, 1, F)], [(2 * F, BF16)])[0]


def _ln2_loss_bwd(x1, ffn, gate2, g, b, target):
    S, D = x1.shape

    def body(i, ins, outs, accs):
        x1_ref, f_ref, gate_ref, g_ref, b_ref, t_ref = ins
        f = f_ref[...]
        xh, rstd = _ln_stats(ALPHA * x1_ref[...] + gate_ref[...] * f)
        e = xh * g_ref[...] + b_ref[...] - t_ref[...]
        dy = e * (1.0 / D)
        dr = _ln_bwd(dy * g_ref[...], xh, rstd)
        outs[0][...] = (gate_ref[...] * dr).astype(BF16)
        outs[1][...] = ALPHA * dr
        _acc_add(i, accs[0], jnp.full((1, 128), (0.5 / D) * jnp.sum(e * e), F32))
        _acc_add(i, accs[1], jnp.sum(dy * xh, axis=0, keepdims=True))
        _acc_add(i, accs[2], jnp.sum(dy, axis=0, keepdims=True))
        _acc_add(i, accs[3], jnp.sum(dr * f, axis=0, keepdims=True))

    return _rows(body, "ln2_loss_bwd", S, _pick(S, (256, 128)),
                 [("row", x1, 0, D), ("row", ffn, 0, D), ("full", gate2), ("full", g), ("full", b), ("row", target, 0, D)],
                 [(D, BF16), (D, F32)], [(1, 128), (1, D), (1, D), (1, D)])


def _ln1_bwd(x, mix, dx1a, du2, gate1, g, b, scale2):
    S, D = x.shape

    def body(i, ins, outs, accs):
        x_ref, mix_ref, da_ref, du_ref, gate_ref, g_ref, b_ref, sc_ref = ins
        mix, du = mix_ref[...], du_ref[...]
        xh, rstd = _ln_stats(ALPHA * x_ref[...] + gate_ref[...] * mix)
        x1 = xh * g_ref[...] + b_ref[...]
        dx1 = da_ref[...] + du * (1.0 + sc_ref[...])
        dr = _ln_bwd(dx1 * g_ref[...], xh, rstd)
        outs[0][...] = (gate_ref[...] * dr).astype(BF16)
        outs[1][...] = ALPHA * dr
        _acc_add(i, accs[0], jnp.sum(du, axis=0, keepdims=True))
        _acc_add(i, accs[1], jnp.sum(du * x1, axis=0, keepdims=True))
        _acc_add(i, accs[2], jnp.sum(dx1 * xh, axis=0, keepdims=True))
        _acc_add(i, accs[3], jnp.sum(dx1, axis=0, keepdims=True))
        _acc_add(i, accs[4], jnp.sum(dr * mix, axis=0, keepdims=True))

    return _rows(body, "ln1_bwd", S, _pick(S, (256, 128)),
                 [("row", x, 0, D), ("row", mix, 0, D), ("row", dx1a, 0, D), ("row", du2, 0, D),
                  ("full", gate1), ("full", g), ("full", b), ("full", scale2)],
                 [(D, BF16), (D, F32)], [(1, D)] * 5)


def _rms_bwd(d_rq, d_rkv, pq, dkr, g_q, g_kv):
    S = pq.shape[0]

    def body(i, ins, outs, accs):
        dq_ref, dkv_ref, pq_ref, dkr_ref, gq_ref, gkv_ref = ins

        def rms_bwd(dy, x, g):
            r = lax.rsqrt(jnp.mean(x * x, axis=-1, keepdims=True) + RMS_EPS)
            dyg = dy * g
            dx = r * dyg - x * (r * r * r) * jnp.mean(dyg * x, axis=-1, keepdims=True)
            return dx, jnp.sum(dy * (x * r), axis=0, keepdims=True)

        dxq, dgq = rms_bwd(dq_ref[...], pq_ref[:, 0:Q_LORA], gq_ref[...])
        dxkv, dgkv = rms_bwd(dkv_ref[...], pq_ref[:, Q_LORA:Q_LORA + KV_LORA], gkv_ref[...])
        outs[0][:, 0:Q_LORA] = dxq.astype(BF16)
        outs[0][:, Q_LORA:Q_LORA + KV_LORA] = dxkv.astype(BF16)
        outs[0][:, Q_LORA + KV_LORA:QKV_A] = dkr_ref[...].astype(BF16)
        _acc_add(i, accs[0], dgq)
        _acc_add(i, accs[1], dgkv)

    return _rows(body, "rms_bwd", S, _pick(S, (256, 128)),
                 [("row", d_rq, 0, Q_LORA), ("row", d_rkv, 0, KV_LORA), ("row", pq, 0, QKV_A), ("row", dkr, 0, 128),
                  ("full", g_q), ("full", g_kv)], [(QKV_A, BF16)], [(1, Q_LORA), (1, KV_LORA)])


def _dx_final(dxa, du, x, scale1):
    S, D = x.shape

    def body(i, ins, outs, accs):
        du = ins[1][...]
        outs[0][...] = ins[0][...] + du * (1.0 + ins[3][...])
        _acc_add(i, accs[0], jnp.sum(du, axis=0, keepdims=True))
        _acc_add(i, accs[1], jnp.sum(du * ins[2][...], axis=0, keepdims=True))

    return _rows(body, "dx_final", S, _pick(S, (256, 128)),
                 [("row", dxa, 0, D), ("row", du, 0, D), ("row", x, 0, D), ("full", scale1)],
                 [(D, F32)], [(1, D), (1, D)])


def _ada_fwd(c_all, w, bias):
    B, D = c_all.shape
    NA = w.shape[1]
    tn = _pick(NA, (512, 256, 128))

    def body(c_ref, w_ref, b_ref, o_ref):
        cv = c_ref[...]
        ca = (cv * _sigmoid(cv)).astype(BF16)
        o_ref[...] = jnp.dot(ca, w_ref[...].astype(BF16), preferred_element_type=F32) + b_ref[...]

    return pl.pallas_call(
        body, name="ada_fwd", grid=(NA // tn,),
        in_specs=[pl.BlockSpec((B, D), lambda j: (0, 0)), pl.BlockSpec((D, tn), lambda j: (0, j)),
                  pl.BlockSpec((1, tn), lambda j: (0, j))],
        out_specs=pl.BlockSpec((B, tn), lambda j: (0, j)),
        out_shape=jax.ShapeDtypeStruct((B, NA), F32),
        compiler_params=_params(("arbitrary",)),
    )(c_all, w, bias)


def _ada_bwd(c_all, dmod):
    B, D = c_all.shape
    NA = dmod.shape[1]
    tn = _pick(NA, (512, 256, 128))

    def body(c_ref, d_ref, o_ref):
        cv = c_ref[...]
        ca = (cv * _sigmoid(cv)).astype(BF16)
        o_ref[...] = lax.dot_general(ca, d_ref[...].astype(BF16), TN, preferred_element_type=F32)

    return pl.pallas_call(
        body, name="ada_bwd", grid=(NA // tn,),
        in_specs=[pl.BlockSpec((B, D), lambda j: (0, 0)), pl.BlockSpec((B, tn), lambda j: (0, j))],
        out_specs=pl.BlockSpec((D, tn), lambda j: (0, j)),
        out_shape=jax.ShapeDtypeStruct((D, NA), F32),
        compiler_params=_params(("arbitrary",)),
    )(c_all, dmod)


def _pack_rows(parts, n_rows):
    N = parts[0].shape[1]

    def body(*refs):
        o_ref = refs[-1]
        o_ref[...] = jnp.zeros_like(o_ref)
        at = 0
        for r in refs[:-1]:
            o_ref[at:at + r.shape[0], :] = r[...]
            at += r.shape[0]

    return pl.pallas_call(body, name="pack_small", out_shape=jax.ShapeDtypeStruct((n_rows, N), F32),
                          compiler_params=_params())(*parts)


def _sum8(parts):
    _, R, N = parts.shape

    def body(p_ref, o_ref):
        acc = p_ref[0]
        for d in range(1, 8):
            acc = acc + p_ref[d]
        o_ref[...] = acc

    return pl.pallas_call(body, name="sum8", out_shape=jax.ShapeDtypeStruct((R, N), F32),
                          compiler_params=_params())(parts)


def _adam_math(w, g, m, v):
    m = ADAM_B1 * m + (1.0 - ADAM_B1) * g
    v = ADAM_B2 * v + (1.0 - ADAM_B2) * (g * g)
    delta = -ADAM_LR * ((m / ADAM_C1) / (jnp.sqrt(v / ADAM_C2) + ADAM_EPS) + ADAM_WD * w)
    return delta, m, v


def _adam(name, w, m, v, g, carry=None):
    R, C = w.shape
    tm = _row_tile(R, C * 4, 1 << 20)
    steps = R // tm
    n_ci = len(carry.ins) if carry else 0
    n_co = len(carry.outs) if carry else 0

    def body(*refs):
        w_ref, m_ref, v_ref, g_ref = refs[:4]
        d_ref, nm_ref, nv_ref = refs[4 + n_ci:7 + n_ci]
        c_ins, c_outs, c_sems = refs[4:4 + n_ci], refs[7 + n_ci:7 + n_ci + n_co], refs[7 + n_ci + n_co:]
        if carry:
            @pl.when(pl.program_id(0) == 0)
            def _():
                carry.start(c_ins, c_outs, c_sems)

        delta, nm, nv = _adam_math(w_ref[...], g_ref[...], m_ref[...], v_ref[...])
        d_ref[...] = delta
        nm_ref[...] = nm
        nv_ref[...] = nv
        if carry:
            @pl.when(pl.program_id(0) == steps - 1)
            def _():
                carry.finish(c_ins, c_outs, c_sems)

    spec = pl.BlockSpec((tm, C), lambda i: (i, 0))
    res = pl.pallas_call(
        body, name=name, grid=(steps,), in_specs=[spec] * 4 + [ANY] * n_ci, out_specs=[spec] * 3 + [ANY] * n_co,
        out_shape=[jax.ShapeDtypeStruct((R, C), F32)] * 3 + (carry.outs if carry else []),
        scratch_shapes=carry.sems if carry else [],
        input_output_aliases=carry.io_aliases(4, 3) if carry else {},
        compiler_params=_params(("arbitrary",)),
    )(w, m, v, g, *(carry.ins if carry else []))
    return (res[:3], res[3:]) if carry else res


def _adam_halves(name, w, m, v, mine, other, core, carry=None):
    R, C = w.shape
    Rh = mine.shape[0]
    tc = max(t for t in range(128, C + 1, 128) if C % t == 0 and R * t <= (3 << 17))
    steps = C // tc
    n_ci = len(carry.ins) if carry else 0
    n_co = len(carry.outs) if carry else 0

    def body(*refs):
        c_ref, w_ref, m_ref, v_ref, a_ref, b_ref = refs[:6]
        g_ref, d_ref, nm_ref, nv_ref = refs[6 + n_ci:10 + n_ci]
        c_ins, c_outs, c_sems = refs[6:6 + n_ci], refs[10 + n_ci:10 + n_ci + n_co], refs[10 + n_ci + n_co:]
        if carry:
            @pl.when(pl.program_id(0) == 0)
            def _():
                carry.start(c_ins, c_outs, c_sems)

        first = c_ref[0] == 0
        g = jnp.concatenate([jnp.where(first, a_ref[...], b_ref[...]),
                             jnp.where(first, b_ref[0:R - Rh, :], a_ref[0:R - Rh, :])], axis=0)
        delta, nm, nv = _adam_math(w_ref[...], g, m_ref[...], v_ref[...])
        g_ref[...] = g
        d_ref[...] = delta
        nm_ref[...] = nm
        nv_ref[...] = nv
        if carry:
            @pl.when(pl.program_id(0) == steps - 1)
            def _():
                carry.finish(c_ins, c_outs, c_sems)

    spec = pl.BlockSpec((R, tc), lambda i, c_ref: (0, i))
    h_spec = pl.BlockSpec((Rh, tc), lambda i, c_ref: (0, i))
    res = pl.pallas_call(
        body, name=name, out_shape=[jax.ShapeDtypeStruct((R, C), F32)] * 4 + (carry.outs if carry else []),
        grid_spec=pltpu.PrefetchScalarGridSpec(
            num_scalar_prefetch=1, grid=(steps,), in_specs=[spec, spec, spec, h_spec, h_spec] + [ANY] * n_ci,
            out_specs=[spec] * 4 + [ANY] * n_co, scratch_shapes=carry.sems if carry else []),
        input_output_aliases=carry.io_aliases(6, 4) if carry else {},
        compiler_params=_params(("arbitrary",)),
    )(core, w, m, v, mine, other, *(carry.ins if carry else []))
    return (res[:4], res[4:]) if carry else res


def _adam_small(name, w, m, v, g):
    def body(w_ref, m_ref, v_ref, g_ref, d_ref, nm_ref, nv_ref):
        delta, nm, nv = _adam_math(w_ref[...], g_ref[...], m_ref[...], v_ref[...])
        d_ref[...] = delta
        nm_ref[...] = nm
        nv_ref[...] = nv

    return pl.pallas_call(body, name=name, out_shape=[jax.ShapeDtypeStruct(w.shape, F32)] * 3,
                          compiler_params=_params())(w, m, v, g)


def _place():
    return lax.axis_index("x"), lax.axis_index("y"), lax.axis_index("c")


def _other_chips(x, y):
    return [(1 - x, y), (x, 1 - y), (1 - x, 1 - y)]


def _all_gather8(blk, name):
    R, N = blk.shape

    def body(x_ref, out_ref, send_sems, recv_sems, local_sem):
        x, y, c = _place()
        me = 4 * x + 2 * y + c
        mine = pltpu.make_async_copy(x_ref, out_ref.at[me], local_sem)
        mine.start()
        flips = [(j >> 2 & 1, j >> 1 & 1, j & 1) for j in range(1, 8)]
        peers = [((1 - x) if fx else x, (1 - y) if fy else y, (1 - c) if fc else c) for fx, fy, fc in flips]
        sends = []
        for j, peer in enumerate(peers):
            cp = pltpu.make_async_remote_copy(src_ref=x_ref, dst_ref=out_ref.at[me], send_sem=send_sems.at[j],
                                              recv_sem=recv_sems.at[j], device_id=peer, device_id_type=MESH)
            cp.start()
            sends.append(cp)
        for j, (px, py, pc) in enumerate(peers):
            pltpu.make_async_remote_copy(src_ref=x_ref, dst_ref=out_ref.at[4 * px + 2 * py + pc],
                                         send_sem=send_sems.at[j], recv_sem=recv_sems.at[j],
                                         device_id=(px, py, pc), device_id_type=MESH).wait_recv()
        for cp in sends:
            cp.wait_send()
        mine.wait()

    return pl.pallas_call(
        body, name=name, out_shape=jax.ShapeDtypeStruct((8, R, N), F32),
        in_specs=[pl.BlockSpec(memory_space=pltpu.VMEM)], out_specs=pl.BlockSpec(memory_space=pltpu.VMEM),
        scratch_shapes=[pltpu.SemaphoreType.DMA((7,)), pltpu.SemaphoreType.DMA((7,)), pltpu.SemaphoreType.DMA],
        compiler_params=_params(),
    )(blk)


def _piece(rows, piece):
    i, n, k = piece if len(piece) == 3 else (piece[0], piece[1], 1)
    assert rows % 16 == 0 and rows // 16 >= n, (rows, piece)
    lo, hi = (rows // 16 * i // n) * 16, (rows // 16 * (i + k) // n) * 16
    return pl.ds(lo, hi - lo)


def _scatter_plan(arrs, piece=(0, 1), into=None):
    n = len(arrs)

    def copies(ins, outs, sems):
        send_sems, recv_sems = sems
        x, y, c = _place()
        chips = _other_chips(x, y)
        cps = []
        for k in range(n):
            rows = _piece(arrs[k].shape[1], piece)
            for j, (px, py) in enumerate(chips):
                cps.append(pltpu.make_async_remote_copy(
                    src_ref=ins[k].at[2 * px + py, rows], dst_ref=outs[k].at[j, rows],
                    send_sem=send_sems.at[3 * k + j], recv_sem=recv_sems.at[3 * k + j],
                    device_id=(px, py, c), device_id_type=MESH))
        return cps

    def start(ins, outs, sems):
        for cp in copies(ins, outs, sems):
            cp.start()

    def finish(ins, outs, sems):
        for cp in copies(ins, outs, sems):
            cp.wait()

    return _Plan(list(arrs) + list(into or []), [jax.ShapeDtypeStruct((3,) + a.shape[1:], a.dtype) for a in arrs],
                 [pltpu.SemaphoreType.DMA((3 * n,))] * 2, start, finish,
                 aliases={n + k: k for k in range(n)} if into else None)


def _gather_plan(shards, piece=(0, 1), into=None, ici=True):
    n = len(shards)

    def parts(ins, outs, sems):
        s1, r1, s2, r2, loc = sems
        x, y, c = _place()
        me = 2 * x + y
        chips = _other_chips(x, y)
        sib = (x, y, 1 - c)

        def rows(k):
            return _piece(shards[k].shape[1], piece)

        def ici_copy(k, j, slab, to):
            return pltpu.make_async_remote_copy(src_ref=ins[k].at[c, rows(k)], dst_ref=outs[k].at[slab, c, rows(k)],
                                                send_sem=s1.at[3 * k + j], recv_sem=r1.at[3 * k + j],
                                                device_id=to, device_id_type=MESH)

        def d2d(k, j, slab, half):
            return pltpu.make_async_remote_copy(src_ref=outs[k].at[slab, half, rows(k)],
                                                dst_ref=outs[k].at[slab, half, rows(k)],
                                                send_sem=s2.at[3 * k + j], recv_sem=r2.at[3 * k + j],
                                                device_id=sib, device_id_type=MESH)

        def own(k):
            return pltpu.make_async_remote_copy(src_ref=ins[k].at[:, rows(k)], dst_ref=outs[k].at[me, :, rows(k)],
                                                send_sem=loc.at[2 * k], recv_sem=loc.at[2 * k + 1],
                                                device_id=sib, device_id_type=MESH)

        return c, me, chips, ici_copy, d2d, own

    def start(ins, outs, sems):
        c, me, chips, ici_copy, d2d, own = parts(ins, outs, sems)
        for k in range(n):
            for j, (px, py) in enumerate(chips):
                (ici_copy(k, j, me, (px, py, c)) if ici else d2d(k, j, 2 * px + py, c)).start()
        for k in range(n):
            own(k).start()

    def finish(ins, outs, sems):
        c, me, chips, ici_copy, d2d, own = parts(ins, outs, sems)
        if ici:
            for k in range(n):
                for j, (px, py) in enumerate(chips):
                    ici_copy(k, j, 2 * px + py, (px, py, c)).wait_recv()
                    d2d(k, j, 2 * px + py, c).start()
        for k in range(n):
            for j, (px, py) in enumerate(chips):
                d2d(k, j, 2 * px + py, 1 - c).wait_recv()
        for k in range(n):
            own(k).wait()
            for j, (px, py) in enumerate(chips):
                if ici:
                    ici_copy(k, j, me, (px, py, c)).wait_send()
                d2d(k, j, 2 * px + py, c).wait_send()

    return _Plan(list(shards) + list(into or []), [jax.ShapeDtypeStruct((4,) + a.shape, a.dtype) for a in shards],
                 [pltpu.SemaphoreType.DMA((3 * n,))] * 4 + [pltpu.SemaphoreType.DMA((2 * n,))], start, finish,
                 aliases={n + k: k for k in range(n)} if into else None)


def _pair_plan(parts):
    n = len(parts)

    def copies(ins, outs, sems):
        send_sems, recv_sems = sems
        x, y, c = _place()
        return [pltpu.make_async_remote_copy(src_ref=ins[k].at[p, 1 - c], dst_ref=outs[k].at[p],
                                             send_sem=send_sems.at[4 * k + p], recv_sem=recv_sems.at[4 * k + p],
                                             device_id=(x, y, 1 - c), device_id_type=MESH)
                for k in range(n) for p in range(4)]

    def start(ins, outs, sems):
        for cp in copies(ins, outs, sems):
            cp.start()

    def finish(ins, outs, sems):
        for cp in copies(ins, outs, sems):
            cp.wait()

    return _Plan(parts, [jax.ShapeDtypeStruct((4,) + a.shape[2:], a.dtype) for a in parts],
                 [pltpu.SemaphoreType.DMA((4 * n,))] * 2, start, finish)


def _sibling_plan(arrs):
    n = len(arrs)

    def copies(ins, outs, sems):
        send_sems, recv_sems = sems
        x, y, c = _place()
        return [pltpu.make_async_remote_copy(src_ref=ins[k], dst_ref=outs[k], send_sem=send_sems.at[k],
                                             recv_sem=recv_sems.at[k], device_id=(x, y, 1 - c), device_id_type=MESH)
                for k in range(n)]

    def start(ins, outs, sems):
        for cp in copies(ins, outs, sems):
            cp.start()

    def finish(ins, outs, sems):
        for cp in copies(ins, outs, sems):
            cp.wait()

    return _Plan(arrs, [jax.ShapeDtypeStruct(a.shape, a.dtype) for a in arrs],
                 [pltpu.SemaphoreType.DMA((n,))] * 2, start, finish)


def _scatter_copies(arrs):
    def copies(ins, land, send_sems, recv_sems):
        x, y, c = _place()
        return [pltpu.make_async_remote_copy(src_ref=ins[k].at[2 * px + py], dst_ref=land[k].at[j],
                                             send_sem=send_sems.at[3 * k + j], recv_sem=recv_sems.at[3 * k + j],
                                             device_id=(px, py, c), device_id_type=MESH)
                for k in range(len(arrs)) for j, (px, py) in enumerate(_other_chips(x, y))]

    return copies, [lax.empty((3,) + a.shape[1:], a.dtype) for a in arrs]


def _gather_copies(shards):
    def copies(ins, land, send_sems, recv_sems):
        x, y, c = _place()
        return [pltpu.make_async_remote_copy(src_ref=ins[k].at[c], dst_ref=land[k].at[2 * x + y, c],
                                             send_sem=send_sems.at[3 * k + j], recv_sem=recv_sems.at[3 * k + j],
                                             device_id=(px, py, c), device_id_type=MESH)
                for k in range(len(shards)) for j, (px, py) in enumerate(_other_chips(x, y))]

    return copies, [lax.empty((4,) + a.shape, a.dtype) for a in shards]


def _split_start(arrs, copies_lands, ride, name, after=()):
    copies, lands = copies_lands
    n = len(arrs)

    def body(*refs):
        first_out = 2 * n + 1 + len(after)
        for cp in copies(refs[:n], refs[n:2 * n], refs[first_out], refs[first_out + 1]):
            cp.start()

    hbm = [pltpu.with_memory_space_constraint(a, pltpu.HBM) for a in list(arrs) + lands + [ride]]
    res = pl.pallas_call(
        body, name=name,
        out_shape=[pltpu.SemaphoreType.DMA((3 * n,)), pltpu.SemaphoreType.DMA((3 * n,))]
        + [pltpu.HBM(a.shape, a.dtype) for a in hbm],
        in_specs=[HBM_SPEC] * (2 * n + 1) + [ANY] * len(after),
        out_specs=[SEM_SPEC, SEM_SPEC] + [HBM_SPEC] * (2 * n + 1),
        input_output_aliases={i: 2 + i for i in range(2 * n + 1)},
        compiler_params=pltpu.CompilerParams(has_side_effects=pltpu.SideEffectType.DATAFLOW_SIDE_EFFECTING),
    )(*hbm, *after)
    return res[0], res[1], res[2:2 + n], res[2 + n:2 + 2 * n], res[2 + 2 * n]


def _split_wait(started, copies_lands, after, name):
    send_sems, recv_sems, arrs, lands, _ = started
    copies = copies_lands[0]
    n = len(arrs)

    def body(*refs):
        for cp in copies(refs[:n], refs[n:2 * n], refs[2 * n], refs[2 * n + 1]):
            cp.wait_send()
            cp.wait_recv()

    res = pl.pallas_call(
        body, name=name, out_shape=[pltpu.HBM(a.shape, a.dtype) for a in list(arrs) + list(lands)],
        in_specs=[HBM_SPEC] * (2 * n) + [SEM_SPEC, SEM_SPEC] + [ANY] * len(after), out_specs=[HBM_SPEC] * (2 * n),
        input_output_aliases={i: i for i in range(2 * n)},
        compiler_params=pltpu.CompilerParams(has_side_effects=pltpu.SideEffectType.DATAFLOW_SIDE_EFFECTING),
    )(*arrs, *lands, send_sems, recv_sems, *after)
    return list(res[:n]), list(res[n:])


def _join_plans(plans):
    def split(seq, counts):
        out, at = [], 0
        for cnt in counts:
            out.append(seq[at:at + cnt])
            at += cnt
        return out

    n_i, n_o, n_s = ([len(getattr(p, f)) for p in plans] for f in ("ins", "outs", "sems"))

    def start(ins, outs, sems):
        for p, i, o, s in zip(plans, split(ins, n_i), split(outs, n_o), split(sems, n_s)):
            p.start(i, o, s)

    def finish(ins, outs, sems):
        for p, i, o, s in zip(plans, split(ins, n_i), split(outs, n_o), split(sems, n_s)):
            p.finish(i, o, s)

    aliases, at_i, at_o = {}, 0, 0
    for p in plans:
        aliases.update(p.io_aliases(at_i, at_o))
        at_i, at_o = at_i + len(p.ins), at_o + len(p.outs)
    return _Plan(sum((p.ins for p in plans), []), sum((p.outs for p in plans), []), sum((p.sems for p in plans), []),
                 start, finish, aliases)


def _add_pair(parts, sib, core, name):
    P4, _, Rh, C = parts.shape
    tm, tc = _tile2(Rh, C, 16)

    def body(c_ref, a_ref, b_ref, o_ref):
        o_ref[...] = (a_ref[0].astype(F32) + b_ref[...].astype(F32)).astype(BF16)

    spec = pl.BlockSpec((1, tm, tc), lambda p, i, j, c_ref: (p, i, j))
    return pl.pallas_call(
        body, name=name, out_shape=jax.ShapeDtypeStruct((P4, Rh, C), BF16),
        grid_spec=pltpu.PrefetchScalarGridSpec(
            num_scalar_prefetch=1, grid=(P4, Rh // tm, C // tc),
            in_specs=[pl.BlockSpec((1, 1, tm, tc), lambda p, i, j, c_ref: (p, c_ref[0], i, j)), spec], out_specs=spec),
        compiler_params=_params(("parallel",) * 3),
    )(core, parts, sib)


def _sum_slabs(pre, recv, chip, name):
    _, Rh, C = pre.shape
    tm, tc = _tile2(Rh, C, 16)

    def body(me_ref, own_ref, r_ref, o_ref):
        acc = own_ref[0].astype(F32)
        for j in range(3):
            acc = acc + r_ref[j].astype(F32)
        o_ref[...] = acc

    return pl.pallas_call(
        body, name=name, out_shape=jax.ShapeDtypeStruct((Rh, C), F32),
        grid_spec=pltpu.PrefetchScalarGridSpec(
            num_scalar_prefetch=1, grid=(Rh // tm, C // tc),
            in_specs=[pl.BlockSpec((1, tm, tc), lambda i, j, me_ref: (me_ref[0], i, j)),
                      pl.BlockSpec((3, tm, tc), lambda i, j, me_ref: (0, i, j))],
            out_specs=pl.BlockSpec((tm, tc), lambda i, j, me_ref: (i, j))),
        compiler_params=_params(("parallel", "parallel")),
    )(chip, pre, recv)


def kernel(x, c, positions, w_ada, b_ada, w_in, g_q_a, w_q_b, g_kv_a, w_kv_b, w_o_a, w_conv, w_o_b, w_o, ln1_g, ln1_b, w_ffn_in, w_ffn_out, ln2_g, ln2_b, loss_target, m_w_ada, m_b_ada, m_w_in, m_g_q_a, m_w_q_b, m_g_kv_a, m_w_kv_b, m_w_o_a, m_w_conv, m_w_o_b, m_w_o, m_ln1_g, m_ln1_b, m_w_ffn_in, m_w_ffn_out, m_ln2_g, m_ln2_b, v_w_ada, v_b_ada, v_w_in, v_g_q_a, v_w_q_b, v_g_kv_a, v_w_kv_b, v_w_o_a, v_w_conv, v_w_o_b, v_w_o, v_ln1_g, v_ln1_b, v_w_ffn_in, v_w_ffn_out, v_ln2_g, v_ln2_b):
    S, D = x.shape[1], x.shape[2]
    F = w_ffn_out.shape[1] * 4
    ax, ay, ac = _place()
    chip = 2 * ax + ay
    dev = 4 * ax + 2 * ay + ac
    x2, tgt = x[0], loss_target[0]
    w_ada2, w_in2, w_q_b2, w_kv_b2 = w_ada[0], w_in[0], w_q_b[0], w_kv_b[0]
    w_o_a2, w_o_b2, w_o2, w_ffn_in2, w_ffn_out2 = w_o_a[0], w_o_b[0], w_o[0], w_ffn_in[0], w_ffn_out[0]
    NA = w_ada2.shape[1]
    CW = w_conv.shape[2]

    inv_freq = 1.0 / (ROPE_THETA ** (jnp.arange(0, QK_ROPE, 2, dtype=F32) / QK_ROPE))
    ang = positions[0].astype(F32)[:, None] * inv_freq
    cos, sin = jnp.cos(ang), jnp.sin(ang)
    z32, z64, z96 = jnp.zeros((S, 32), F32), jnp.zeros((S, 64), F32), jnp.zeros((S, 96), F32)
    tab = jnp.concatenate([cos, cos, z64, -sin, z96, z32, sin, z64], axis=1)

    def halves(a):
        return a.reshape(2, a.shape[0] // 2, a.shape[1])

    def whole(g):
        return g.reshape(4, 2 * g.shape[2], g.shape[3])

    def cols(g):
        return jnp.transpose(g, (1, 0, 2)).reshape(g.shape[1], 4 * g.shape[2])

    w_inT, m_w_inT, v_w_inT = w_in2.T, m_w_in[0].T, v_w_in[0].T
    CS = w_inT.shape[0]
    CSP = -(-CS // 32) * 32
    sh_in = halves(jnp.pad(w_inT.astype(BF16), ((0, CSP - CS), (0, 0))))
    sh_qb, sh_kvb, sh_oa, sh_ob, sh_o, sh_fi, sh_fo = (
        halves(w.astype(BF16)) for w in (w_q_b2, w_kv_b2, w_o_a2, w_o_b2, w_o2, w_ffn_in2, w_ffn_out2))
    c_all = _all_gather8(c, "gather_c").reshape(8, D)
    wconv_all = _all_gather8(w_conv[0], "gather_wconv")
    w_conv_full = jnp.transpose(wconv_all[0::2], (1, 0, 2)).reshape(3, D)
    b_sh = lax.dynamic_slice(b_ada, (0, chip * NA), (1, NA))
    mod_sh = _ada_fwd(c_all, w_ada2, b_sh)
    mod_all = _all_gather8(mod_sh, "gather_mod")
    mod = lax.dynamic_slice(mod_all[0::2], (0, dev, 0), (4, 1, NA)).reshape(6, D)
    shift1, scale1, gate1, shift2, scale2, gate2 = (mod[k:k + 1] for k in range(6))

    g_in, shift1 = _run_plan(_gather_plan([sh_in]), "gather_first", ride=shift1)
    g_in = whole(g_in)
    sh_a1, sh_a2 = [sh_qb, sh_kvb], [sh_oa, sh_ob, sh_o]
    cl_a1, cl_a2, cl_fi, cl_fo = (_gather_copies(g) for g in (sh_a1, sh_a2, [sh_fi], [sh_fo]))
    st_a1 = _split_start(sh_a1, cl_a1, shift1, "gather_a1_start")
    st_a2 = _split_start(sh_a2, cl_a2, st_a1[4], "gather_a2_start")
    shift1 = st_a2[4]

    def in_rows(lo, hi):
        parts = [g_in[p, max(lo, p * CS) - p * CS:min(hi, (p + 1) * CS) - p * CS]
                 for p in range(4) if max(lo, p * CS) < min(hi, (p + 1) * CS)]
        return parts[0] if len(parts) == 1 else jnp.concatenate(parts, axis=0)

    n_qkv = Q_LORA + KV_LORA + QK_ROPE
    W_qkvT = jnp.pad(in_rows(0, n_qkv), ((0, QKV_A - n_qkv), (0, 0)))
    W_convT = in_rows(n_qkv, n_qkv + 3 * D)
    W_gateT = in_rows(n_qkv + 3 * D, n_qkv + 5 * D)

    u = _modulate(x2, scale1, shift1, "modulate1")
    pq = _matmul(u, W_qkvT, "nt", F32, "proj_qkv")
    pc = _matmul(u, W_convT, "nt", F32, "proj_conv")
    sh_a1, la1 = _split_wait(st_a1, cl_a1, [pc], "gather_a1_wait")
    pg, (g_qb, g_kvb) = _matmul(u, W_gateT, "nt", BF16, "proj_gate", carry=_gather_plan(sh_a1, into=la1, ici=False))
    st_fi = _split_start([sh_fi], cl_fi, g_q_a, "gather_fi_start", after=[pg])
    W_qb = jnp.pad(cols(whole(g_qb)).reshape(Q_LORA, N_HEADS, QK_NOPE + QK_ROPE),
                   ((0, 0), (0, 0), (0, QK_PAD - QK_NOPE - QK_ROPE))).reshape(Q_LORA, N_HEADS * QK_PAD)
    W_kvb = cols(whole(g_kvb))
    rq, rkv, kr = _rms_fwd(pq, tab, st_fi[4], g_kv_a)
    q = _q_rope(_matmul(rq, W_qb, "nn", F32, "q_b"), tab)
    kv = _matmul(rkv, W_kvb, "nn", BF16, "kv_b")
    sh_a2, la2 = _split_wait(st_a2, cl_a2, [kv], "gather_a2_wait")
    o, lse, (g_oa, g_ob, g_o) = _attn_fwd(q, kv, kr, carry=_gather_plan(sh_a2, into=la2, ici=False))
    W_oa, W_ob, W_o = (g.reshape(-1, D) for g in (g_oa, g_ob, g_o))
    y_a = _matmul(o, W_oa, "nn", F32, "o_a")
    hb = _conv_fwd(pc, w_conv_full)
    y_b = _matmul(hb, W_ob, "nn", F32, "o_b")
    st_fo = _split_start([sh_fo], cl_fo, ln1_g, "gather_fo_start", after=[y_b])
    merged = _merge_fwd(y_a, y_b, pg)
    sh_fi_t, lfi = _split_wait(st_fi, cl_fi, [merged], "gather_fi_wait")
    mix, (g_fi,) = _matmul(merged, W_o, "nn", F32, "w_o", carry=_gather_plan(sh_fi_t, into=lfi, ici=False))
    W_fi = whole(g_fi)
    x1, u2 = _ln1_fwd(x2, mix, gate1, st_fo[4], ln1_b, scale2, shift2)
    hh = _matmul(u2, W_fi, "nn", BF16, "ffn_in", shards="b")
    sh_fo_t, lfo = _split_wait(st_fo, cl_fo, [hh], "gather_fo_wait")
    W_fo = _run_plan(_gather_plan(sh_fo_t, into=lfo, ici=False), "forward_fo")[0].reshape(F, D)
    act = _swiglu_fwd(hh)
    ffn = _matmul(act, W_fo, "nn", F32, "ffn_out")

    core_i = ac.astype(jnp.int32).reshape(1)
    chip_i = chip.astype(jnp.int32).reshape(1)

    def uncols(g):
        return jnp.transpose(g.reshape(g.shape[0], 4, g.shape[1] // 4), (1, 0, 2))

    def slabs(p):
        return p.reshape(4, 2, p.shape[1] // 2, p.shape[2])

    def add_pairs(parts, sibs, nms):
        return [_add_pair(a, b, core_i, "add_pair_" + nm) for a, b, nm in zip(parts, sibs, nms)]

    def sum_all(pre, recv, nms):
        return [_sum_slabs(a, r, chip_i, "sum_slabs_" + nm) for a, r, nm in zip(pre, recv, nms)]

    dffn, dx1a, loss_acc, d_ln2_g, d_ln2_b, d_gate2 = _ln2_loss_bwd(x1, ffn, gate2, ln2_g, ln2_b, tgt)
    loss = lax.psum(loss_acc[0, 0], ("x", "y", "c"))
    dW_fo = _matmul(act, dffn, "tn", BF16, "d_w_ffn_out")
    p_fo = [slabs(dW_fo.reshape(4, -1, D))]
    dact, s_fo = _matmul(dffn, W_fo, "nt", BF16, "d_act", carry=_pair_plan(p_fo))
    pre_fo = add_pairs(p_fo, s_fo, ["w_ffn_out"])
    cs_fo = _scatter_copies(pre_fo)
    st_sfo = _split_start(pre_fo, cs_fo, scale2, "scatter_fo_start")
    dhh = _swiglu_bwd(dact, hh)
    dW_fi = _matmul(u2, dhh, "tn", BF16, "d_w_ffn_in", shards="o")
    p_fi = [slabs(dW_fi)]
    du2, s_fi = _matmul(dhh, W_fi, "nt", F32, "d_u2", carry=_pair_plan(p_fi), shards="b")
    pre_fi = add_pairs(p_fi, s_fi, ["w_ffn_in"])
    cs_fi = _scatter_copies(pre_fi)
    st_sfi = _split_start(pre_fi, cs_fi, st_sfo[4], "scatter_fi_start")
    dmix, dxa, d_shift2, d_scale2, d_ln1_g, d_ln1_b, d_gate1 = _ln1_bwd(x2, mix, dx1a, du2, gate1, ln1_g, ln1_b, st_sfi[4])
    dW_o = _matmul(merged, dmix, "tn", BF16, "d_w_o")
    dmerged = _matmul(dmix, W_o, "nt", F32, "d_merged")
    dy_a, dy_b, dgate = _merge_bwd(dmerged, y_a, y_b, pg)
    dW_oa = _matmul(o, dy_a, "tn", BF16, "d_w_o_a")
    do = _matmul(dy_a, W_oa, "nt", BF16, "d_o")
    dW_ob = _matmul(hb, dy_b, "tn", BF16, "d_w_o_b")
    p_mid = [slabs(g.reshape(4, -1, D)) for g in (dW_oa, dW_ob, dW_o)]
    dhb, s_mid = _matmul(dy_b, W_ob, "nt", F32, "d_hb", carry=_pair_plan(p_mid))
    pre_mid = add_pairs(p_mid, s_mid, ["w_o_a", "w_o_b", "w_o"])
    cs_mid = _scatter_copies(pre_mid)
    st_smid = _split_start(pre_mid, cs_mid, w_conv_full, "scatter_mid_start")
    dconv, d_wconv = _conv_bwd(dhb, pc, st_smid[4])
    dq, dkv, dkr, _ = _attn_bwd(q, kv, kr, do, o, lse, tab)
    names_a = ["w_ffn_out", "w_ffn_in", "w_o_a", "w_o_b", "w_o"]
    dW_qb = _matmul(rq, dq, "tn", BF16, "d_w_q_b")
    d_rq = _matmul(dq, W_qb, "nt", F32, "d_rq")
    dW_kvb = _matmul(rkv, dkv, "tn", BF16, "d_w_kv_b")
    d_rkv = _matmul(dkv, W_kvb, "nt", F32, "d_rkv")
    dqkv, d_g_q, d_g_kv = _rms_bwd(d_rq, d_rkv, pq, dkr, g_q_a, g_kv_a)
    dW_qkvT = _matmul(dqkv, u, "tn", BF16, "d_w_qkv")
    dW_convT = _matmul(dconv, u, "tn", BF16, "d_w_conv")
    dW_gateT = _matmul(dgate, u, "tn", BF16, "d_w_gate")
    pre_fo, r_fo = _split_wait(st_sfo, cs_fo, [dW_gateT], "scatter_fo_wait")
    pre_fi, r_fi = _split_wait(st_sfi, cs_fi, [dW_gateT], "scatter_fi_wait")
    pre_mid, r_mid = _split_wait(st_smid, cs_mid, [dW_gateT], "scatter_mid_wait")
    fin_a = sum_all(pre_fo + pre_fi + pre_mid, r_fo + r_fi + r_mid, names_a)
    dW_inT = jnp.concatenate([dW_qkvT[:n_qkv], dW_convT, dW_gateT], axis=0).reshape(4, CS, D)
    dW_inT = jnp.pad(dW_inT, ((0, 0), (0, CSP - CS), (0, 0)))
    dW_qb_u = dW_qb.reshape(Q_LORA, N_HEADS, QK_PAD)[:, :, :QK_NOPE + QK_ROPE].reshape(Q_LORA, -1)
    names_b = ["w_in", "w_q_b", "w_kv_b"]
    p_b = [slabs(dW_inT), slabs(uncols(dW_qb_u)), slabs(uncols(dW_kvb))]
    du, s_b = _matmul(dqkv, W_qkvT, "nn", F32, "d_u_qkv", carry=_pair_plan(p_b))
    pre_b = add_pairs(p_b, s_b, names_b)
    cs_b = _scatter_copies(pre_b)
    st_b = _split_start(pre_b, cs_b, scale1, "scatter_last_start")
    du, fs_a = _matmul(dconv, W_convT, "nn", F32, "d_u_conv", add=du, carry=_sibling_plan(fin_a))
    du = _matmul(dgate, W_gateT, "nn", F32, "d_u_gate", add=du)
    grad_x, d_shift1, d_scale1 = _dx_final(dxa, du, x2, st_b[4])

    def pad_d(v):
        return jnp.pad(v, ((0, 0), (0, D - v.shape[1])))

    small = _pack_rows([d_ln1_g, d_ln1_b, d_ln2_g, d_ln2_b, pad_d(d_g_q), pad_d(d_g_kv), d_wconv,
                         d_shift1, d_scale1, d_gate1, d_shift2, d_scale2, d_gate2], 16)
    small_all = _all_gather8(small, "gather_small")
    small_sum = _sum8(small_all)
    g_ln1_g, g_ln1_b, g_ln2_g, g_ln2_b = (small_sum[k:k + 1] for k in range(4))
    g_g_q, g_g_kv = small_sum[4:5, :Q_LORA], small_sum[5:6, :KV_LORA]
    g_wconv = lax.dynamic_slice(small_sum[6:9], (0, chip * CW), (3, CW))
    g_b_ada = small_sum[9:15].reshape(1, 6 * D)
    dmod_all = small_all[:, 9:15, :].reshape(8, 6 * D)
    g_w_ada = _ada_bwd(c_all, lax.dynamic_slice(dmod_all, (0, chip * NA), (8, NA)))

    big = {}
    ws = dict(w_in=(w_inT, m_w_inT, v_w_inT), w_q_b=(w_q_b2, m_w_q_b[0], v_w_q_b[0]),
              w_kv_b=(w_kv_b2, m_w_kv_b[0], v_w_kv_b[0]), w_o_a=(w_o_a2, m_w_o_a[0], v_w_o_a[0]),
              w_o_b=(w_o_b2, m_w_o_b[0], v_w_o_b[0]), w_o=(w_o2, m_w_o[0], v_w_o[0]),
              w_ffn_in=(w_ffn_in2, m_w_ffn_in[0], v_w_ffn_in[0]), w_ffn_out=(w_ffn_out2, m_w_ffn_out[0], v_w_ffn_out[0]))
    def adam_of(nm, a, b, carry=None):
        w_, m_, v_ = ws[nm]
        return _adam_halves("adam_" + nm, w_, m_, v_, a, b, core_i, carry)

    for nm, a, b in zip(names_a, fin_a, fs_a):
        big[nm] = adam_of(nm, a, b)
    big["w_ada"] = [g_w_ada] + list(_adam("adam_w_ada", w_ada2, m_w_ada[0], v_w_ada[0], g_w_ada))
    done = [big[nm][1] for nm in names_a] + [big["w_ada"][1], grad_x]
    pre_b, r_b = _split_wait(st_b, cs_b, done, "scatter_last_wait")
    fin_b = sum_all(pre_b, r_b, names_b)
    fs_b = _run_plan(_sibling_plan(fin_b), "sibling_last")
    for nm, a, b in zip(names_b, fin_b, fs_b):
        big[nm] = adam_of(nm, a, b)
    sm = {}
    for nm, w_, m_, v_, g_ in [("b_ada", b_ada, m_b_ada, v_b_ada, g_b_ada), ("g_q_a", g_q_a, m_g_q_a, v_g_q_a, g_g_q),
                               ("g_kv_a", g_kv_a, m_g_kv_a, v_g_kv_a, g_g_kv),
                               ("w_conv", w_conv[0], m_w_conv[0], v_w_conv[0], g_wconv),
                               ("ln1_g", ln1_g, m_ln1_g, v_ln1_g, g_ln1_g), ("ln1_b", ln1_b, m_ln1_b, v_ln1_b, g_ln1_b),
                               ("ln2_g", ln2_g, m_ln2_g, v_ln2_g, g_ln2_g), ("ln2_b", ln2_b, m_ln2_b, v_ln2_b, g_ln2_b)]:
        sm[nm] = (g_,) + tuple(_adam_small("adam_" + nm, w_, m_, v_, g_))

    order = ["w_ada", "b_ada", "w_in", "g_q_a", "w_q_b", "g_kv_a", "w_kv_b", "w_o_a", "w_conv", "w_o_b", "w_o",
             "ln1_g", "ln1_b", "w_ffn_in", "w_ffn_out", "ln2_g", "ln2_b"]
    lead = {"b_ada", "g_q_a", "g_kv_a", "ln1_g", "ln1_b", "ln2_g", "ln2_b"}

    def leaf(nm, k):
        val = big[nm][k] if nm in big else sm[nm][k]
        if nm == "w_in":
            val = val.T
        return val if nm in lead else val[None]

    outs = [loss, grad_x[None]]
    for k in range(4):
        outs += [leaf(nm, k) for nm in order]
    return tuple(outs)
```

```python
import functools

import jax
import jax.numpy as jnp
from jax import lax
from jax.experimental import pallas as pl
from jax.experimental.pallas import tpu as pltpu

F32, BF16 = jnp.float32, jnp.bfloat16
N_HEADS, QK_NOPE, QK_ROPE, V_HEAD = 16, 128, 64, 128
Q_LORA, KV_LORA = 512, 512
QK_PAD = 256
QKV_A = 1152
CHUNK_SHIFT = 6
ATTN_SCALE = (QK_NOPE + QK_ROPE) ** -0.5
ROPE_THETA = 10000.0
ALPHA = 2.0 ** 0.25
LN_EPS, RMS_EPS = 1e-5, 1e-6
ADAM_LR, ADAM_B1, ADAM_B2, ADAM_EPS, ADAM_WD, ADAM_STEP = 0.001, 0.9, 0.999, 1e-08, 0.01, 10
ADAM_C1 = 1.0 - ADAM_B1 ** ADAM_STEP
ADAM_C2 = 1.0 - ADAM_B2 ** ADAM_STEP
VMEM_LIMIT = 56 * 1024 * 1024
MESH = pl.DeviceIdType.MESH
ANY = pl.BlockSpec(memory_space=pl.ANY)
HBM_SPEC = pl.BlockSpec(memory_space=pltpu.HBM)
SEM_SPEC = pl.BlockSpec(memory_space=pltpu.SEMAPHORE)
NT = (((1,), (1,)), ((), ()))
TN = (((0,), (0,)), ((), ()))
NN = (((1,), (0,)), ((), ()))


def _params(sem=None):
    return pltpu.CompilerParams(dimension_semantics=sem, vmem_limit_bytes=VMEM_LIMIT)


def _pick(n, cands=(1408, 1024, 512, 384, 256, 128)):
    for t in cands:
        if n % t == 0:
            return t
    return n


def _row_tile(rows, row_bytes, budget, mult=8):
    best = mult
    for t in range(mult, rows + 1, mult):
        if rows % t == 0 and t * row_bytes <= budget:
            best = t
    return best


def _tile2(rows, cols, mult=8, budget=3 << 18):
    col_tiles = [t for t in range(128, cols + 1, 128) if cols % t == 0] or [cols]
    best = None
    for tc in col_tiles:
        for tr in range(mult, rows + 1, mult):
            if rows % tr == 0 and tr * tc <= budget and (best is None or (tr * tc, tc) > (best[0] * best[1], best[1])):
                best = (tr, tc)
    assert best is not None, (rows, cols)
    return best


def _sigmoid(x):
    return jax.nn.sigmoid(x)


class _Plan:
    def __init__(self, ins, outs, sems, start, finish, aliases=None):
        self.ins, self.outs, self.sems, self.start, self.finish = list(ins), list(outs), list(sems), start, finish
        self.aliases = dict(aliases or {})

    def io_aliases(self, first_in, first_out):
        return {first_in + i: first_out + o for i, o in self.aliases.items()}


def _run_plan(plan, name, ride=None):
    n_in, n_out = len(plan.ins), len(plan.outs)
    extra = [] if ride is None else [ride]
    aliases = plan.io_aliases(0, 0)
    if extra:
        aliases[n_in] = n_out

    def body(*refs):
        ins, outs, sems = refs[:n_in], refs[n_in + len(extra):n_in + len(extra) + n_out], refs[n_in + 2 * len(extra) + n_out:]
        plan.start(ins, outs, sems)
        plan.finish(ins, outs, sems)

    return pl.pallas_call(body, name=name, out_shape=plan.outs + [jax.ShapeDtypeStruct(r.shape, r.dtype) for r in extra],
                          in_specs=[ANY] * (n_in + len(extra)), out_specs=[ANY] * (n_out + len(extra)),
                          scratch_shapes=plan.sems, input_output_aliases=aliases,
                          compiler_params=_params())(*plan.ins, *extra)


def _matmul(a, b, mode, out_dtype, name, add=None, carry=None, shards=None):
    if mode == "nn":
        (M, K), N, dims = a.shape, b.shape[-1] * (4 if shards else 1), NN
    elif mode == "nt":
        (M, K), N, dims = a.shape, b.shape[-2], NT
    else:
        (K, M), N, dims = a.shape, b.shape[1], TN
    split_n = shards and mode != "nt"
    tm = _pick(M)
    tn = _pick(N // 4) if split_n else _pick(N)
    if shards and mode == "nt":
        tk = _pick(K // 4)
    else:
        tk = K if K <= 2048 else _pick(K)
    nk = K // tk
    per = (N // 4 // tn) if split_n else (K // 4 // tk if shards else 1)
    a_spec = (pl.BlockSpec((tk, tm), lambda i, j, k: (k, i)) if mode == "tn"
              else pl.BlockSpec((tm, tk), lambda i, j, k: (i, k)))
    if shards == "b" and mode == "nn":
        b_spec = pl.BlockSpec((None, tk, tn), lambda i, j, k: (j // per, k, j % per))
    elif shards == "b":
        b_spec = pl.BlockSpec((None, tn, tk), lambda i, j, k: (k // per, j, k % per))
    else:
        b_spec = (pl.BlockSpec((tn, tk), lambda i, j, k: (j, k)) if mode == "nt"
                  else pl.BlockSpec((tk, tn), lambda i, j, k: (k, j)))
    o_spec = pl.BlockSpec((tm, tn), lambda i, j, k: (i, j))
    o_shape = (M, N)
    if shards == "o":
        o_spec, o_shape = pl.BlockSpec((None, tm, tn), lambda i, j, k: (j // per, i, j % per)), (4, M, N // 4)
    has_add = add is not None
    n_ci = len(carry.ins) if carry else 0
    n_co = len(carry.outs) if carry else 0
    n_in = 2 + has_add
    grid = (M // tm, N // tn, nk)

    def body(*refs):
        a_ref, b_ref = refs[0], refs[1]
        add_ref = refs[2] if has_add else None
        o_ref = refs[n_in + n_ci]
        acc_ref = refs[n_in + n_ci + 1 + n_co] if nk > 1 else None
        c_ins = refs[n_in:n_in + n_ci]
        c_outs = refs[n_in + n_ci + 1:n_in + n_ci + 1 + n_co]
        c_sems = refs[n_in + n_ci + 1 + n_co + (nk > 1):]
        i, j, k = pl.program_id(0), pl.program_id(1), pl.program_id(2)

        if carry:
            @pl.when((i == 0) & (j == 0) & (k == 0))
            def _():
                carry.start(c_ins, c_outs, c_sems)

        part = lax.dot_general(a_ref[...], b_ref[...], dims, preferred_element_type=F32)
        if nk == 1:
            o_ref[...] = (part + add_ref[...] if has_add else part).astype(o_ref.dtype)
        else:
            @pl.when(k == 0)
            def _():
                acc_ref[...] = part

            @pl.when((k > 0) & (k < nk - 1))
            def _():
                acc_ref[...] += part

            @pl.when(k == nk - 1)
            def _():
                r = acc_ref[...] + part
                if has_add:
                    r = r + add_ref[...]
                o_ref[...] = r.astype(o_ref.dtype)

        if carry:
            @pl.when((i == grid[0] - 1) & (j == grid[1] - 1) & (k == nk - 1))
            def _():
                carry.finish(c_ins, c_outs, c_sems)

    ins = [a, b] + ([add] if has_add else []) + (carry.ins if carry else [])
    in_specs = [a_spec, b_spec] + ([o_spec] if has_add else []) + [ANY] * n_ci
    res = pl.pallas_call(
        body, name=name, grid=grid,
        in_specs=in_specs, out_specs=[o_spec] + [ANY] * n_co,
        out_shape=[jax.ShapeDtypeStruct(o_shape, out_dtype)] + (carry.outs if carry else []),
        scratch_shapes=([pltpu.VMEM((tm, tn), F32)] if nk > 1 else []) + (carry.sems if carry else []),
        input_output_aliases=carry.io_aliases(n_in, 1) if carry else {},
        compiler_params=_params(("arbitrary",) * 3 if carry else ("parallel", "parallel", "arbitrary")),
    )(*ins)
    return (res[0], res[1:]) if carry else res[0]


def _rows(body, name, n_rows, tm, ins, outs, accs=()):
    grid = (n_rows // tm,)
    per8 = tm // 8
    last8 = n_rows // 8 - 1
    arrays, in_specs = [], []
    for spec in ins:
        kind, arr = spec[0], spec[1]
        arrays.append(arr)
        if kind == "row":
            _, _, cb, w = spec
            in_specs.append(pl.BlockSpec((tm, w), lambda i, cb=cb: (i, cb)))
        elif kind == "full":
            in_specs.append(pl.BlockSpec(arr.shape, lambda i, nd=arr.ndim: (0,) * nd))
        elif kind == "prev":
            _, _, cb, w = spec
            in_specs.append(pl.BlockSpec((8, w), lambda i, cb=cb: (jnp.maximum(i * per8 - 1, 0), cb)))
        else:
            _, _, cb, w = spec
            in_specs.append(pl.BlockSpec((8, w), lambda i, cb=cb: (jnp.minimum((i + 1) * per8, last8), cb)))
    out_shape = [jax.ShapeDtypeStruct((n_rows, w), dt) for (w, dt) in outs]
    out_specs = [pl.BlockSpec((tm, w), lambda i: (i, 0)) for (w, _) in outs]
    out_shape += [jax.ShapeDtypeStruct(s, F32) for s in accs]
    out_specs += [pl.BlockSpec(s, lambda i, nd=len(s): (0,) * nd) for s in accs]
    n_in, n_out = len(ins), len(outs)

    def kernel_body(*refs):
        body(pl.program_id(0), refs[:n_in], refs[n_in:n_in + n_out], refs[n_in + n_out:])

    res = pl.pallas_call(
        kernel_body, name=name, grid=grid, in_specs=in_specs, out_specs=out_specs, out_shape=out_shape,
        compiler_params=_params(("arbitrary",)),
    )(*arrays)
    return res


def _acc_add(i, ref, val):
    @pl.when(i == 0)
    def _():
        ref[...] = val

    @pl.when(i > 0)
    def _():
        ref[...] += val


def _rope(t, tab, sign):
    c, sa, sb = tab[:, 0:128], tab[:, 128:256], tab[:, 256:384]
    rot = pltpu.roll(t, 96, 1) * sa + pltpu.roll(t, 32, 1) * sb
    return t * c + rot if sign > 0 else t * c - rot


def _ln_stats(r):
    mu = jnp.mean(r, axis=-1, keepdims=True)
    d = r - mu
    var = jnp.mean(d * d, axis=-1, keepdims=True)
    rstd = lax.rsqrt(var + LN_EPS)
    return d * rstd, rstd


def _ln_bwd(dxh, xh, rstd):
    m1 = jnp.mean(dxh, axis=-1, keepdims=True)
    m2 = jnp.mean(dxh * xh, axis=-1, keepdims=True)
    return rstd * (dxh - m1 - xh * m2)


def _modulate(x, scale, shift, name):
    S, D = x.shape

    def body(i, ins, outs, accs):
        outs[0][...] = (ins[0][...] * (1.0 + ins[1][...]) + ins[2][...]).astype(BF16)

    return _rows(body, name, S, _pick(S, (256, 128)), [("row", x, 0, D), ("full", scale), ("full", shift)], [(D, BF16)])[0]


def _rms_fwd(pq, tab, g_q, g_kv):
    S = pq.shape[0]

    def body(i, ins, outs, accs):
        pq_ref, tab_ref, gq_ref, gkv_ref = ins

        def rms(x, g):
            return x * lax.rsqrt(jnp.mean(x * x, axis=-1, keepdims=True) + RMS_EPS) * g

        outs[0][...] = rms(pq_ref[:, 0:Q_LORA], gq_ref[...]).astype(BF16)
        outs[1][...] = rms(pq_ref[:, Q_LORA:Q_LORA + KV_LORA], gkv_ref[...]).astype(BF16)
        outs[2][...] = _rope(pq_ref[:, Q_LORA + KV_LORA:QKV_A], tab_ref[...], 1).astype(BF16)

    return _rows(body, "rms_fwd", S, _pick(S, (256, 128)),
                 [("row", pq, 0, QKV_A), ("row", tab, 0, 384), ("full", g_q), ("full", g_kv)],
                 [(Q_LORA, BF16), (KV_LORA, BF16), (128, BF16)])


def _q_rope(q, tab):
    S, W = q.shape

    def body(i, ins, outs, accs):
        q_ref, tab_ref = ins
        t = tab_ref[...]
        for h in range(N_HEADS):
            lo = h * QK_PAD
            outs[0][:, lo:lo + 128] = q_ref[:, lo:lo + 128].astype(BF16)
            outs[0][:, lo + 128:lo + 256] = _rope(q_ref[:, lo + 128:lo + 256], t, 1).astype(BF16)

    return _rows(body, "q_rope", S, _pick(S, (256, 128)), [("row", q, 0, W), ("row", tab, 0, 384)], [(W, BF16)])[0]


def _allowed(q0, k0, bq):
    row = q0 + lax.broadcasted_iota(jnp.int32, (bq, bq), 0)
    col = k0 + lax.broadcasted_iota(jnp.int32, (bq, bq), 1)
    return (col >> CHUNK_SHIFT) <= (row >> CHUNK_SHIFT)


ATTN_BLOCK = 512


def _attn_fwd(q, kv, kr, carry=None):
    S = q.shape[0]
    bq = min(ATTN_BLOCK, S)
    nq = S // bq
    n_ci = len(carry.ins) if carry else 0
    n_co = len(carry.outs) if carry else 0

    def body(*refs):
        q_ref, kn_ref, v_ref, kr_ref = refs[:4]
        o_ref, lse_ref = refs[4 + n_ci:6 + n_ci]
        c_ins, c_outs = refs[4:4 + n_ci], refs[6 + n_ci:6 + n_ci + n_co]
        kcat = refs[6 + n_ci + n_co]
        c_sems = refs[7 + n_ci + n_co:]
        qi = pl.program_id(1)
        if carry:
            @pl.when((pl.program_id(0) == 0) & (qi == 0))
            def _():
                carry.start(c_ins, c_outs, c_sems)

        @pl.when(qi == 0)
        def _():
            kcat[:, 0:128] = kn_ref[...]
            kcat[:, 128:256] = kr_ref[...]

        qv = q_ref[...]

        def step(j, carry, masked):
            m, l, acc = carry
            off = pl.multiple_of(j * bq, bq)
            s = lax.dot_general(qv, kcat[pl.ds(off, bq), :], NT, preferred_element_type=F32) * ATTN_SCALE
            if masked:
                s = jnp.where(_allowed(qi * bq, off, bq), s, -1e30)
            m_new = jnp.maximum(m, jnp.max(s, axis=1, keepdims=True))
            a = jnp.exp(m - m_new)
            p = jnp.exp(s - m_new)
            l = a * l + jnp.sum(p, axis=1, keepdims=True)
            acc = a * acc + jnp.dot(p.astype(BF16), v_ref[pl.ds(off, bq), :], preferred_element_type=F32)
            return m_new, l, acc

        init = (jnp.full((bq, 1), -1e30, F32), jnp.zeros((bq, 1), F32), jnp.zeros((bq, V_HEAD), F32))
        below = lax.fori_loop(0, qi, lambda j, cr: step(j, cr, False), init)
        m, l, acc = step(qi, below, True)
        o_ref[...] = (acc / l).astype(BF16)
        lse_ref[0] = m + jnp.log(l)
        if carry:
            @pl.when((pl.program_id(0) == N_HEADS - 1) & (qi == nq - 1))
            def _():
                carry.finish(c_ins, c_outs, c_sems)

    res = pl.pallas_call(
        body, name="attn_fwd", grid=(N_HEADS, nq),
        in_specs=[pl.BlockSpec((bq, QK_PAD), lambda h, i: (i, h)),
                  pl.BlockSpec((S, 128), lambda h, i: (0, 2 * h)),
                  pl.BlockSpec((S, 128), lambda h, i: (0, 2 * h + 1)),
                  pl.BlockSpec((S, 128), lambda h, i: (0, 0))] + [ANY] * n_ci,
        out_specs=[pl.BlockSpec((bq, V_HEAD), lambda h, i: (i, h)),
                   pl.BlockSpec((1, bq, 1), lambda h, i: (h, i, 0))] + [ANY] * n_co,
        out_shape=[jax.ShapeDtypeStruct((S, N_HEADS * V_HEAD), BF16),
                   jax.ShapeDtypeStruct((N_HEADS, S, 1), F32)] + (carry.outs if carry else []),
        scratch_shapes=[pltpu.VMEM((S, QK_PAD), BF16)] + (carry.sems if carry else []),
        input_output_aliases=carry.io_aliases(4, 2) if carry else {},
        compiler_params=_params(("arbitrary", "arbitrary")),
    )(q, kv, kv, kr, *(carry.ins if carry else []))
    return res[0], res[1], res[2:]


def _attn_bwd(q, kv, kr, do, o, lse, tab, carry=None):
    S = q.shape[0]
    bq = min(ATTN_BLOCK, S)
    nq = S // bq

    n_ci = len(carry.ins) if carry else 0
    n_co = len(carry.outs) if carry else 0

    def body(*refs):
        q_ref, kn_ref, v_ref, kr_ref, do_ref, o_ref, lse_ref, tab_ref = refs[:8]
        dq_ref, dkv_ref, dkr_ref = refs[8 + n_ci:11 + n_ci]
        dq_acc, dk_acc, dv_acc, kcat, delta = refs[11 + n_ci + n_co:16 + n_ci + n_co]
        c_ins, c_outs, c_sems = refs[8:8 + n_ci], refs[11 + n_ci:11 + n_ci + n_co], refs[16 + n_ci + n_co:]
        h = pl.program_id(0)
        if carry:
            @pl.when(h == 0)
            def _():
                carry.start(c_ins, c_outs, c_sems)

        dq_acc[...] = jnp.zeros_like(dq_acc)
        dk_acc[...] = jnp.zeros_like(dk_acc)
        dv_acc[...] = jnp.zeros_like(dv_acc)
        kcat[:, 0:128] = kn_ref[...]
        kcat[:, 128:256] = kr_ref[...]
        for r in range(nq):
            rows = slice(r * bq, (r + 1) * bq)
            delta[rows, :] = jnp.sum(do_ref[rows, :].astype(F32) * o_ref[rows, :].astype(F32), axis=1, keepdims=True)

        def pair(i, j, masked):
            rows_i = pl.ds(pl.multiple_of(i * bq, bq), bq)
            rows_j = pl.ds(pl.multiple_of(j * bq, bq), bq)
            qv, dov, k = q_ref[rows_i, :], do_ref[rows_i, :], kcat[rows_j, :]
            s = lax.dot_general(qv, k, NT, preferred_element_type=F32) * ATTN_SCALE
            if masked:
                s = jnp.where(_allowed(i * bq, j * bq, bq), s, -1e30)
            p = jnp.exp(s - lse_ref[0, rows_i, :])
            dv_acc[rows_j, :] += lax.dot_general(p.astype(BF16), dov, TN, preferred_element_type=F32)
            dp = lax.dot_general(dov, v_ref[rows_j, :], NT, preferred_element_type=F32)
            ds = (p * (dp - delta[rows_i, :]) * ATTN_SCALE).astype(BF16)
            dk_acc[rows_j, :] += lax.dot_general(ds, qv, TN, preferred_element_type=F32)
            dq_acc[rows_i, :] += jnp.dot(ds, k, preferred_element_type=F32)

        def kv_step(j, _):
            pair(j, j, True)

            def q_step(i, _):
                pair(i, j, False)
                return 0

            lax.fori_loop(j + 1, nq, q_step, 0)
            return 0

        lax.fori_loop(0, nq, kv_step, 0)

        for r in range(nq):
            rows = slice(r * bq, (r + 1) * bq)
            dq_ref[rows, 0:128] = dq_acc[rows, 0:128].astype(BF16)
            dq_ref[rows, 128:256] = _rope(dq_acc[rows, 128:256], tab_ref[rows, :], -1).astype(BF16)
        dkv_ref[:, 0:128] = dk_acc[:, 0:128].astype(BF16)
        dkv_ref[:, 128:256] = dv_acc[...].astype(BF16)

        @pl.when(h == 0)
        def _():
            dkr_ref[...] = dk_acc[:, 128:256]

        @pl.when(h > 0)
        def _():
            dkr_ref[...] += dk_acc[:, 128:256]

        @pl.when(h == N_HEADS - 1)
        def _():
            for r in range(nq):
                rows = slice(r * bq, (r + 1) * bq)
                dkr_ref[rows, :] = _rope(dkr_ref[rows, :], tab_ref[rows, :], -1)
            if carry:
                carry.finish(c_ins, c_outs, c_sems)

    W = N_HEADS * QK_PAD
    res = pl.pallas_call(
        body, name="attn_bwd", grid=(N_HEADS,),
        in_specs=[pl.BlockSpec((S, QK_PAD), lambda h: (0, h)),
                  pl.BlockSpec((S, 128), lambda h: (0, 2 * h)),
                  pl.BlockSpec((S, 128), lambda h: (0, 2 * h + 1)),
                  pl.BlockSpec((S, 128), lambda h: (0, 0)),
                  pl.BlockSpec((S, V_HEAD), lambda h: (0, h)),
                  pl.BlockSpec((S, V_HEAD), lambda h: (0, h)),
                  pl.BlockSpec((1, S, 1), lambda h: (h, 0, 0)),
                  pl.BlockSpec((S, 384), lambda h: (0, 0))] + [ANY] * n_ci,
        out_specs=[pl.BlockSpec((S, QK_PAD), lambda h: (0, h)),
                   pl.BlockSpec((S, QK_PAD), lambda h: (0, h)),
                   pl.BlockSpec((S, 128), lambda h: (0, 0))] + [ANY] * n_co,
        out_shape=[jax.ShapeDtypeStruct((S, W), BF16), jax.ShapeDtypeStruct((S, W), BF16),
                   jax.ShapeDtypeStruct((S, 128), F32)] + (carry.outs if carry else []),
        scratch_shapes=[pltpu.VMEM((S, QK_PAD), F32), pltpu.VMEM((S, QK_PAD), F32), pltpu.VMEM((S, V_HEAD), F32),
                        pltpu.VMEM((S, QK_PAD), BF16), pltpu.VMEM((S, 1), F32)]
        + (carry.sems if carry else []),
        input_output_aliases=carry.io_aliases(8, 3) if carry else {},
        compiler_params=_params(("arbitrary",)),
    )(q, kv, kv, kr, do, o, lse, tab, *(carry.ins if carry else []))
    return res[0], res[1], res[2], res[3:]


def _shift_down(cur, prev8, i, n):
    tm = cur.shape[0]
    prev8 = jnp.where(i == 0, jnp.zeros_like(prev8), prev8)
    full = jnp.concatenate([prev8, cur], axis=0)
    return pltpu.roll(full, n, 0)[8:8 + tm, :]


def _shift_up(cur, next8, i, last, n):
    tm = cur.shape[0]
    next8 = jnp.where(i == last, jnp.zeros_like(next8), next8)
    full = jnp.concatenate([cur, next8], axis=0)
    return pltpu.roll(full, tm + 8 - n, 0)[0:tm, :]


def _conv_fwd(pc, w_conv):
    S, D = pc.shape[0], pc.shape[1] // 3
    tm = _pick(S, (256, 128))

    def body(i, ins, outs, accs):
        b_ref, c_ref, x_ref, cp_ref, xp_ref, w_ref = ins
        z = c_ref[...] * x_ref[...]
        zp = cp_ref[...] * xp_ref[...]
        cz = w_ref[0:1, :] * _shift_down(z, zp, i, 2) + w_ref[1:2, :] * _shift_down(z, zp, i, 1) + w_ref[2:3, :] * z
        outs[0][...] = (b_ref[...] * cz).astype(BF16)

    return _rows(body, "conv_fwd", S, tm,
                 [("row", pc, 0, D), ("row", pc, 1, D), ("row", pc, 2, D), ("prev", pc, 1, D), ("prev", pc, 2, D),
                  ("full", w_conv)], [(D, BF16)])[0]


def _conv_bwd(dhb, pc, w_conv):
    S, D = dhb.shape
    tm = _pick(S, (256, 128))
    last = S // tm - 1

    def body(i, ins, outs, accs):
        g_ref, b_ref, c_ref, x_ref, cp_ref, xp_ref, gn_ref, bn_ref, w_ref = ins
        w0, w1, w2 = w_ref[0:1, :], w_ref[1:2, :], w_ref[2:3, :]
        c, x, g = c_ref[...], x_ref[...], g_ref[...]
        z = c * x
        zp = cp_ref[...] * xp_ref[...]
        z1, z2 = _shift_down(z, zp, i, 1), _shift_down(z, zp, i, 2)
        cz = w0 * z2 + w1 * z1 + w2 * z
        dcz = g * b_ref[...]
        dczn = gn_ref[...] * bn_ref[...]
        dz = w2 * dcz + w1 * _shift_up(dcz, dczn, i, last, 1) + w0 * _shift_up(dcz, dczn, i, last, 2)
        outs[0][:, 0:D] = (g * cz).astype(BF16)
        outs[0][:, D:2 * D] = (dz * x).astype(BF16)
        outs[0][:, 2 * D:3 * D] = (dz * c).astype(BF16)
        dw = jnp.concatenate([jnp.sum(dcz * z2, axis=0, keepdims=True), jnp.sum(dcz * z1, axis=0, keepdims=True),
                              jnp.sum(dcz * z, axis=0, keepdims=True)], axis=0)
        _acc_add(i, accs[0], dw)

    return _rows(body, "conv_bwd", S, tm,
                 [("row", dhb, 0, D), ("row", pc, 0, D), ("row", pc, 1, D), ("row", pc, 2, D),
                  ("prev", pc, 1, D), ("prev", pc, 2, D), ("next", dhb, 0, D), ("next", pc, 0, D), ("full", w_conv)],
                 [(3 * D, BF16)], [(3, D)])


def _merge_fwd(y_a, y_b, pg):
    S, D = y_a.shape

    def body(i, ins, outs, accs):
        ya, yb, ga, gb = ins
        outs[0][...] = (_sigmoid(ga[...].astype(F32)) * ya[...] + _sigmoid(gb[...].astype(F32)) * yb[...]).astype(BF16)

    return _rows(body, "merge_fwd", S, _pick(S, (256, 128)),
                 [("row", y_a, 0, D), ("row", y_b, 0, D), ("row", pg, 0, D), ("row", pg, 1, D)], [(D, BF16)])[0]


def _merge_bwd(dm, y_a, y_b, pg):
    S, D = dm.shape

    def body(i, ins, outs, accs):
        d, ya, yb = ins[0][...], ins[1][...], ins[2][...]
        sa, sb = _sigmoid(ins[3][...].astype(F32)), _sigmoid(ins[4][...].astype(F32))
        outs[0][...] = (d * sa).astype(BF16)
        outs[1][...] = (d * sb).astype(BF16)
        outs[2][:, 0:D] = (d * ya * (sa * (1.0 - sa))).astype(BF16)
        outs[2][:, D:2 * D] = (d * yb * (sb * (1.0 - sb))).astype(BF16)

    return _rows(body, "merge_bwd", S, _pick(S, (256, 128)),
                 [("row", dm, 0, D), ("row", y_a, 0, D), ("row", y_b, 0, D), ("row", pg, 0, D), ("row", pg, 1, D)],
                 [(D, BF16), (D, BF16), (2 * D, BF16)])


def _ln1_fwd(x, mix, gate1, g, b, scale2, shift2):
    S, D = x.shape

    def body(i, ins, outs, accs):
        x_ref, mix_ref, gate_ref, g_ref, b_ref, sc_ref, sh_ref = ins
        xh, _ = _ln_stats(ALPHA * x_ref[...] + gate_ref[...] * mix_ref[...])
        x1 = xh * g_ref[...] + b_ref[...]
        outs[0][...] = x1
        outs[1][...] = (x1 * (1.0 + sc_ref[...]) + sh_ref[...]).astype(BF16)

    return _rows(body, "ln1_fwd", S, _pick(S, (256, 128)),
                 [("row", x, 0, D), ("row", mix, 0, D), ("full", gate1), ("full", g), ("full", b),
                  ("full", scale2), ("full", shift2)], [(D, F32), (D, BF16)])


def _swiglu_fwd(hh):
    S, F = hh.shape[0], hh.shape[1] // 2

    def body(i, ins, outs, accs):
        hg = ins[0][...].astype(F32)
        outs[0][...] = (hg * _sigmoid(hg) * ins[1][...].astype(F32)).astype(BF16)

    return _rows(body, "swiglu_fwd", S, _pick(S, (128,)), [("row", hh, 0, F), ("row", hh, 1, F)], [(F, BF16)])[0]


def _swiglu_bwd(dact, hh):
    S, F = dact.shape

    def body(i, ins, outs, accs):
        d, hg, hu = ins[0][...].astype(F32), ins[1][...].astype(F32), ins[2][...].astype(F32)
        sg = _sigmoid(hg)
        outs[0][:, 0:F] = (d * hu * (sg * (1.0 + hg * (1.0 - sg)))).astype(BF16)
        outs[0][:, F:2 * F] = (d * (hg * sg)).astype(BF16)

    return _rows(body, "swiglu_bwd", S, _pick(S, (128,)),
                 [("row", dact, 0, F), ("row", hh, 0, F), ("row", hh, 1, F)], [(2 * F, BF16)])[0]


def _ln2_loss_bwd(x1, ffn, gate2, g, b, target):
    S, D = x1.shape

    def body(i, ins, outs, accs):
        x1_ref, f_ref, gate_ref, g_ref, b_ref, t_ref = ins
        f = f_ref[...]
        xh, rstd = _ln_stats(ALPHA * x1_ref[...] + gate_ref[...] * f)
        e = xh * g_ref[...] + b_ref[...] - t_ref[...]
        dy = e * (1.0 / D)
        dr = _ln_bwd(dy * g_ref[...], xh, rstd)
        outs[0][...] = (gate_ref[...] * dr).astype(BF16)
        outs[1][...] = ALPHA * dr
        _acc_add(i, accs[0], jnp.full((1, 128), (0.5 / D) * jnp.sum(e * e), F32))
        _acc_add(i, accs[1], jnp.sum(dy * xh, axis=0, keepdims=True))
        _acc_add(i, accs[2], jnp.sum(dy, axis=0, keepdims=True))
        _acc_add(i, accs[3], jnp.sum(dr * f, axis=0, keepdims=True))

    return _rows(body, "ln2_loss_bwd", S, _pick(S, (256, 128)),
                 [("row", x1, 0, D), ("row", ffn, 0, D), ("full", gate2), ("full", g), ("full", b), ("row", target, 0, D)],
                 [(D, BF16), (D, F32)], [(1, 128), (1, D), (1, D), (1, D)])


def _ln1_bwd(x, mix, dx1a, du2, gate1, g, b, scale2):
    S, D = x.shape

    def body(i, ins, outs, accs):
        x_ref, mix_ref, da_ref, du_ref, gate_ref, g_ref, b_ref, sc_ref = ins
        mix, du = mix_ref[...], du_ref[...]
        xh, rstd = _ln_stats(ALPHA * x_ref[...] + gate_ref[...] * mix)
        x1 = xh * g_ref[...] + b_ref[...]
        dx1 = da_ref[...] + du * (1.0 + sc_ref[...])
        dr = _ln_bwd(dx1 * g_ref[...], xh, rstd)
        outs[0][...] = (gate_ref[...] * dr).astype(BF16)
        outs[1][...] = ALPHA * dr
        _acc_add(i, accs[0], jnp.sum(du, axis=0, keepdims=True))
        _acc_add(i, accs[1], jnp.sum(du * x1, axis=0, keepdims=True))
        _acc_add(i, accs[2], jnp.sum(dx1 * xh, axis=0, keepdims=True))
        _acc_add(i, accs[3], jnp.sum(dx1, axis=0, keepdims=True))
        _acc_add(i, accs[4], jnp.sum(dr * mix, axis=0, keepdims=True))

    return _rows(body, "ln1_bwd", S, _pick(S, (256, 128)),
                 [("row", x, 0, D), ("row", mix, 0, D), ("row", dx1a, 0, D), ("row", du2, 0, D),
                  ("full", gate1), ("full", g), ("full", b), ("full", scale2)],
                 [(D, BF16), (D, F32)], [(1, D)] * 5)


def _rms_bwd(d_rq, d_rkv, pq, dkr, g_q, g_kv):
    S = pq.shape[0]

    def body(i, ins, outs, accs):
        dq_ref, dkv_ref, pq_ref, dkr_ref, gq_ref, gkv_ref = ins

        def rms_bwd(dy, x, g):
            r = lax.rsqrt(jnp.mean(x * x, axis=-1, keepdims=True) + RMS_EPS)
            dyg = dy * g
            dx = r * dyg - x * (r * r * r) * jnp.mean(dyg * x, axis=-1, keepdims=True)
            return dx, jnp.sum(dy * (x * r), axis=0, keepdims=True)

        dxq, dgq = rms_bwd(dq_ref[...], pq_ref[:, 0:Q_LORA], gq_ref[...])
        dxkv, dgkv = rms_bwd(dkv_ref[...], pq_ref[:, Q_LORA:Q_LORA + KV_LORA], gkv_ref[...])
        outs[0][:, 0:Q_LORA] = dxq.astype(BF16)
        outs[0][:, Q_LORA:Q_LORA + KV_LORA] = dxkv.astype(BF16)
        outs[0][:, Q_LORA + KV_LORA:QKV_A] = dkr_ref[...].astype(BF16)
        _acc_add(i, accs[0], dgq)
        _acc_add(i, accs[1], dgkv)

    return _rows(body, "rms_bwd", S, _pick(S, (256, 128)),
                 [("row", d_rq, 0, Q_LORA), ("row", d_rkv, 0, KV_LORA), ("row", pq, 0, QKV_A), ("row", dkr, 0, 128),
                  ("full", g_q), ("full", g_kv)], [(QKV_A, BF16)], [(1, Q_LORA), (1, KV_LORA)])


def _dx_final(dxa, du, x, scale1):
    S, D = x.shape

    def body(i, ins, outs, accs):
        du = ins[1][...]
        outs[0][...] = ins[0][...] + du * (1.0 + ins[3][...])
        _acc_add(i, accs[0], jnp.sum(du, axis=0, keepdims=True))
        _acc_add(i, accs[1], jnp.sum(du * ins[2][...], axis=0, keepdims=True))

    return _rows(body, "dx_final", S, _pick(S, (256, 128)),
                 [("row", dxa, 0, D), ("row", du, 0, D), ("row", x, 0, D), ("full", scale1)],
                 [(D, F32)], [(1, D), (1, D)])


def _ada_fwd(c_all, w, bias):
    B, D = c_all.shape
    NA = w.shape[1]
    tn = _pick(NA, (512, 256, 128))

    def body(c_ref, w_ref, b_ref, o_ref):
        cv = c_ref[...]
        ca = (cv * _sigmoid(cv)).astype(BF16)
        o_ref[...] = jnp.dot(ca, w_ref[...].astype(BF16), preferred_element_type=F32) + b_ref[...]

    return pl.pallas_call(
        body, name="ada_fwd", grid=(NA // tn,),
        in_specs=[pl.BlockSpec((B, D), lambda j: (0, 0)), pl.BlockSpec((D, tn), lambda j: (0, j)),
                  pl.BlockSpec((1, tn), lambda j: (0, j))],
        out_specs=pl.BlockSpec((B, tn), lambda j: (0, j)),
        out_shape=jax.ShapeDtypeStruct((B, NA), F32),
        compiler_params=_params(("arbitrary",)),
    )(c_all, w, bias)


def _ada_bwd(c_all, dmod):
    B, D = c_all.shape
    NA = dmod.shape[1]
    tn = _pick(NA, (512, 256, 128))

    def body(c_ref, d_ref, o_ref):
        cv = c_ref[...]
        ca = (cv * _sigmoid(cv)).astype(BF16)
        o_ref[...] = lax.dot_general(ca, d_ref[...].astype(BF16), TN, preferred_element_type=F32)

    return pl.pallas_call(
        body, name="ada_bwd", grid=(NA // tn,),
        in_specs=[pl.BlockSpec((B, D), lambda j: (0, 0)), pl.BlockSpec((B, tn), lambda j: (0, j))],
        out_specs=pl.BlockSpec((D, tn), lambda j: (0, j)),
        out_shape=jax.ShapeDtypeStruct((D, NA), F32),
        compiler_params=_params(("arbitrary",)),
    )(c_all, dmod)


def _pack_rows(parts, n_rows, after=()):
    N = parts[0].shape[1]
    n = len(parts)

    def body(*refs):
        o_ref = refs[-1]
        o_ref[...] = jnp.zeros_like(o_ref)
        at = 0
        for r in refs[:n]:
            o_ref[at:at + r.shape[0], :] = r[...]
            at += r.shape[0]

    vmem = pl.BlockSpec(memory_space=pltpu.VMEM)
    return pl.pallas_call(body, name="pack_small", out_shape=jax.ShapeDtypeStruct((n_rows, N), F32),
                          in_specs=[vmem] * n + [ANY] * len(after), out_specs=vmem,
                          compiler_params=_params())(*parts, *after)


def _sum8(parts):
    _, R, N = parts.shape

    def body(p_ref, o_ref):
        acc = p_ref[0]
        for d in range(1, 8):
            acc = acc + p_ref[d]
        o_ref[...] = acc

    return pl.pallas_call(body, name="sum8", out_shape=jax.ShapeDtypeStruct((R, N), F32),
                          compiler_params=_params())(parts)


def _adam_math(w, g, m, v):
    m = ADAM_B1 * m + (1.0 - ADAM_B1) * g
    v = ADAM_B2 * v + (1.0 - ADAM_B2) * (g * g)
    delta = -ADAM_LR * ((m / ADAM_C1) / (jnp.sqrt(v / ADAM_C2) + ADAM_EPS) + ADAM_WD * w)
    return delta, m, v


def _adam(name, w, m, v, g, carry=None):
    R, C = w.shape
    tm = _row_tile(R, C * 4, 1 << 20)
    steps = R // tm
    n_ci = len(carry.ins) if carry else 0
    n_co = len(carry.outs) if carry else 0

    def body(*refs):
        w_ref, m_ref, v_ref, g_ref = refs[:4]
        d_ref, nm_ref, nv_ref = refs[4 + n_ci:7 + n_ci]
        c_ins, c_outs, c_sems = refs[4:4 + n_ci], refs[7 + n_ci:7 + n_ci + n_co], refs[7 + n_ci + n_co:]
        if carry:
            @pl.when(pl.program_id(0) == 0)
            def _():
                carry.start(c_ins, c_outs, c_sems)

        delta, nm, nv = _adam_math(w_ref[...], g_ref[...], m_ref[...], v_ref[...])
        d_ref[...] = delta
        nm_ref[...] = nm
        nv_ref[...] = nv
        if carry:
            @pl.when(pl.program_id(0) == steps - 1)
            def _():
                carry.finish(c_ins, c_outs, c_sems)

    spec = pl.BlockSpec((tm, C), lambda i: (i, 0))
    res = pl.pallas_call(
        body, name=name, grid=(steps,), in_specs=[spec] * 4 + [ANY] * n_ci, out_specs=[spec] * 3 + [ANY] * n_co,
        out_shape=[jax.ShapeDtypeStruct((R, C), F32)] * 3 + (carry.outs if carry else []),
        scratch_shapes=carry.sems if carry else [],
        input_output_aliases=carry.io_aliases(4, 3) if carry else {},
        compiler_params=_params(("arbitrary",)),
    )(w, m, v, g, *(carry.ins if carry else []))
    return (res[:3], res[3:]) if carry else res


def _adam_halves(name, w, m, v, mine, other, core, carry=None):
    R, C = w.shape
    Rh = mine.shape[0]
    tc = max(t for t in range(128, C + 1, 128) if C % t == 0 and R * t <= (3 << 17))
    steps = C // tc
    n_ci = len(carry.ins) if carry else 0
    n_co = len(carry.outs) if carry else 0

    def body(*refs):
        c_ref, w_ref, m_ref, v_ref, a_ref, b_ref = refs[:6]
        g_ref, d_ref, nm_ref, nv_ref = refs[6 + n_ci:10 + n_ci]
        c_ins, c_outs, c_sems = refs[6:6 + n_ci], refs[10 + n_ci:10 + n_ci + n_co], refs[10 + n_ci + n_co:]
        if carry:
            @pl.when(pl.program_id(0) == 0)
            def _():
                carry.start(c_ins, c_outs, c_sems)

        first = c_ref[0] == 0
        g = jnp.concatenate([jnp.where(first, a_ref[...], b_ref[...]),
                             jnp.where(first, b_ref[0:R - Rh, :], a_ref[0:R - Rh, :])], axis=0)
        delta, nm, nv = _adam_math(w_ref[...], g, m_ref[...], v_ref[...])
        g_ref[...] = g
        d_ref[...] = delta
        nm_ref[...] = nm
        nv_ref[...] = nv
        if carry:
            @pl.when(pl.program_id(0) == steps - 1)
            def _():
                carry.finish(c_ins, c_outs, c_sems)

    spec = pl.BlockSpec((R, tc), lambda i, c_ref: (0, i))
    h_spec = pl.BlockSpec((Rh, tc), lambda i, c_ref: (0, i))
    res = pl.pallas_call(
        body, name=name, out_shape=[jax.ShapeDtypeStruct((R, C), F32)] * 4 + (carry.outs if carry else []),
        grid_spec=pltpu.PrefetchScalarGridSpec(
            num_scalar_prefetch=1, grid=(steps,), in_specs=[spec, spec, spec, h_spec, h_spec] + [ANY] * n_ci,
            out_specs=[spec] * 4 + [ANY] * n_co, scratch_shapes=carry.sems if carry else []),
        input_output_aliases=carry.io_aliases(6, 4) if carry else {},
        compiler_params=_params(("arbitrary",)),
    )(core, w, m, v, mine, other, *(carry.ins if carry else []))
    return (res[:4], res[4:]) if carry else res


def _adam_small(name, w, m, v, g):
    def body(w_ref, m_ref, v_ref, g_ref, d_ref, nm_ref, nv_ref):
        delta, nm, nv = _adam_math(w_ref[...], g_ref[...], m_ref[...], v_ref[...])
        d_ref[...] = delta
        nm_ref[...] = nm
        nv_ref[...] = nv

    return pl.pallas_call(body, name=name, out_shape=[jax.ShapeDtypeStruct(w.shape, F32)] * 3,
                          compiler_params=_params())(w, m, v, g)


def _place():
    return lax.axis_index("x"), lax.axis_index("y"), lax.axis_index("c")


def _other_chips(x, y):
    return [(1 - x, y), (x, 1 - y), (1 - x, 1 - y)]


def _all_gather8(blk, name):
    R, N = blk.shape

    def body(x_ref, out_ref, send_sems, recv_sems, local_sem):
        x, y, c = _place()
        me = 4 * x + 2 * y + c
        mine = pltpu.make_async_copy(x_ref, out_ref.at[me], local_sem)
        mine.start()
        flips = [(j >> 2 & 1, j >> 1 & 1, j & 1) for j in range(1, 8)]
        peers = [((1 - x) if fx else x, (1 - y) if fy else y, (1 - c) if fc else c) for fx, fy, fc in flips]
        sends = []
        for j, peer in enumerate(peers):
            cp = pltpu.make_async_remote_copy(src_ref=x_ref, dst_ref=out_ref.at[me], send_sem=send_sems.at[j],
                                              recv_sem=recv_sems.at[j], device_id=peer, device_id_type=MESH)
            cp.start()
            sends.append(cp)
        for j, (px, py, pc) in enumerate(peers):
            pltpu.make_async_remote_copy(src_ref=x_ref, dst_ref=out_ref.at[4 * px + 2 * py + pc],
                                         send_sem=send_sems.at[j], recv_sem=recv_sems.at[j],
                                         device_id=(px, py, pc), device_id_type=MESH).wait_recv()
        for cp in sends:
            cp.wait_send()
        mine.wait()

    return pl.pallas_call(
        body, name=name, out_shape=jax.ShapeDtypeStruct((8, R, N), F32),
        in_specs=[pl.BlockSpec(memory_space=pltpu.VMEM)], out_specs=pl.BlockSpec(memory_space=pltpu.VMEM),
        scratch_shapes=[pltpu.SemaphoreType.DMA((7,)), pltpu.SemaphoreType.DMA((7,)), pltpu.SemaphoreType.DMA],
        compiler_params=_params(),
    )(blk)


def _piece(rows, piece):
    i, n, k = piece if len(piece) == 3 else (piece[0], piece[1], 1)
    assert rows % 16 == 0 and rows // 16 >= n, (rows, piece)
    lo, hi = (rows // 16 * i // n) * 16, (rows // 16 * (i + k) // n) * 16
    return pl.ds(lo, hi - lo)


def _scatter_plan(arrs, piece=(0, 1), into=None):
    n = len(arrs)

    def copies(ins, outs, sems):
        send_sems, recv_sems = sems
        x, y, c = _place()
        chips = _other_chips(x, y)
        cps = []
        for k in range(n):
            rows = _piece(arrs[k].shape[1], piece)
            for j, (px, py) in enumerate(chips):
                cps.append(pltpu.make_async_remote_copy(
                    src_ref=ins[k].at[2 * px + py, rows], dst_ref=outs[k].at[j, rows],
                    send_sem=send_sems.at[3 * k + j], recv_sem=recv_sems.at[3 * k + j],
                    device_id=(px, py, c), device_id_type=MESH))
        return cps

    def start(ins, outs, sems):
        for cp in copies(ins, outs, sems):
            cp.start()

    def finish(ins, outs, sems):
        for cp in copies(ins, outs, sems):
            cp.wait()

    return _Plan(list(arrs) + list(into or []), [jax.ShapeDtypeStruct((3,) + a.shape[1:], a.dtype) for a in arrs],
                 [pltpu.SemaphoreType.DMA((3 * n,))] * 2, start, finish,
                 aliases={n + k: k for k in range(n)} if into else None)


def _gather_plan(shards, piece=(0, 1), into=None, ici=True):
    n = len(shards)

    def parts(ins, outs, sems):
        s1, r1, s2, r2, loc = sems
        x, y, c = _place()
        me = 2 * x + y
        chips = _other_chips(x, y)
        sib = (x, y, 1 - c)

        def rows(k):
            return _piece(shards[k].shape[1], piece)

        def ici_copy(k, j, slab, to):
            return pltpu.make_async_remote_copy(src_ref=ins[k].at[c, rows(k)], dst_ref=outs[k].at[slab, c, rows(k)],
                                                send_sem=s1.at[3 * k + j], recv_sem=r1.at[3 * k + j],
                                                device_id=to, device_id_type=MESH)

        def d2d(k, j, slab, half):
            return pltpu.make_async_remote_copy(src_ref=outs[k].at[slab, half, rows(k)],
                                                dst_ref=outs[k].at[slab, half, rows(k)],
                                                send_sem=s2.at[3 * k + j], recv_sem=r2.at[3 * k + j],
                                                device_id=sib, device_id_type=MESH)

        def own(k):
            return pltpu.make_async_remote_copy(src_ref=ins[k].at[:, rows(k)], dst_ref=outs[k].at[me, :, rows(k)],
                                                send_sem=loc.at[2 * k], recv_sem=loc.at[2 * k + 1],
                                                device_id=sib, device_id_type=MESH)

        return c, me, chips, ici_copy, d2d, own

    def start(ins, outs, sems):
        c, me, chips, ici_copy, d2d, own = parts(ins, outs, sems)
        for k in range(n):
            for j, (px, py) in enumerate(chips):
                (ici_copy(k, j, me, (px, py, c)) if ici else d2d(k, j, 2 * px + py, c)).start()
        for k in range(n):
            own(k).start()

    def finish(ins, outs, sems):
        c, me, chips, ici_copy, d2d, own = parts(ins, outs, sems)
        if ici:
            for k in range(n):
                for j, (px, py) in enumerate(chips):
                    ici_copy(k, j, 2 * px + py, (px, py, c)).wait_recv()
                    d2d(k, j, 2 * px + py, c).start()
        for k in range(n):
            for j, (px, py) in enumerate(chips):
                d2d(k, j, 2 * px + py, 1 - c).wait_recv()
        for k in range(n):
            own(k).wait()
            for j, (px, py) in enumerate(chips):
                if ici:
                    ici_copy(k, j, me, (px, py, c)).wait_send()
                d2d(k, j, 2 * px + py, c).wait_send()

    return _Plan(list(shards) + list(into or []), [jax.ShapeDtypeStruct((4,) + a.shape, a.dtype) for a in shards],
                 [pltpu.SemaphoreType.DMA((3 * n,))] * 4 + [pltpu.SemaphoreType.DMA((2 * n,))], start, finish,
                 aliases={n + k: k for k in range(n)} if into else None)


def _pair_plan(parts):
    n = len(parts)

    def copies(ins, outs, sems):
        send_sems, recv_sems = sems
        x, y, c = _place()
        return [pltpu.make_async_remote_copy(src_ref=ins[k].at[p, 1 - c], dst_ref=outs[k].at[p],
                                             send_sem=send_sems.at[4 * k + p], recv_sem=recv_sems.at[4 * k + p],
                                             device_id=(x, y, 1 - c), device_id_type=MESH)
                for k in range(n) for p in range(4)]

    def start(ins, outs, sems):
        for cp in copies(ins, outs, sems):
            cp.start()

    def finish(ins, outs, sems):
        for cp in copies(ins, outs, sems):
            cp.wait()

    return _Plan(parts, [jax.ShapeDtypeStruct((4,) + a.shape[2:], a.dtype) for a in parts],
                 [pltpu.SemaphoreType.DMA((4 * n,))] * 2, start, finish)


def _sibling_plan(arrs):
    n = len(arrs)

    def copies(ins, outs, sems):
        send_sems, recv_sems = sems
        x, y, c = _place()
        return [pltpu.make_async_remote_copy(src_ref=ins[k], dst_ref=outs[k], send_sem=send_sems.at[k],
                                             recv_sem=recv_sems.at[k], device_id=(x, y, 1 - c), device_id_type=MESH)
                for k in range(n)]

    def start(ins, outs, sems):
        for cp in copies(ins, outs, sems):
            cp.start()

    def finish(ins, outs, sems):
        for cp in copies(ins, outs, sems):
            cp.wait()

    return _Plan(arrs, [jax.ShapeDtypeStruct(a.shape, a.dtype) for a in arrs],
                 [pltpu.SemaphoreType.DMA((n,))] * 2, start, finish)


def _scatter_copies(arrs):
    def copies(ins, land, send_sems, recv_sems):
        x, y, c = _place()
        return [pltpu.make_async_remote_copy(src_ref=ins[k].at[2 * px + py], dst_ref=land[k].at[j],
                                             send_sem=send_sems.at[3 * k + j], recv_sem=recv_sems.at[3 * k + j],
                                             device_id=(px, py, c), device_id_type=MESH)
                for k in range(len(arrs)) for j, (px, py) in enumerate(_other_chips(x, y))]

    return copies, [lax.empty((3,) + a.shape[1:], a.dtype) for a in arrs]


def _gather_copies(shards):
    def copies(ins, land, send_sems, recv_sems):
        x, y, c = _place()
        return [pltpu.make_async_remote_copy(src_ref=ins[k].at[c], dst_ref=land[k].at[2 * x + y, c],
                                             send_sem=send_sems.at[3 * k + j], recv_sem=recv_sems.at[3 * k + j],
                                             device_id=(px, py, c), device_id_type=MESH)
                for k in range(len(shards)) for j, (px, py) in enumerate(_other_chips(x, y))]

    return copies, [lax.empty((4,) + a.shape, a.dtype) for a in shards]


def _split_start(arrs, copies_lands, ride, name, after=()):
    copies, lands = copies_lands
    n = len(arrs)

    def body(*refs):
        first_out = 2 * n + 1 + len(after)
        for cp in copies(refs[:n], refs[n:2 * n], refs[first_out], refs[first_out + 1]):
            cp.start()

    hbm = [pltpu.with_memory_space_constraint(a, pltpu.HBM) for a in list(arrs) + lands + [ride]]
    res = pl.pallas_call(
        body, name=name,
        out_shape=[pltpu.SemaphoreType.DMA((3 * n,)), pltpu.SemaphoreType.DMA((3 * n,))]
        + [pltpu.HBM(a.shape, a.dtype) for a in hbm],
        in_specs=[HBM_SPEC] * (2 * n + 1) + [ANY] * len(after),
        out_specs=[SEM_SPEC, SEM_SPEC] + [HBM_SPEC] * (2 * n + 1),
        input_output_aliases={i: 2 + i for i in range(2 * n + 1)},
        compiler_params=pltpu.CompilerParams(has_side_effects=pltpu.SideEffectType.DATAFLOW_SIDE_EFFECTING),
    )(*hbm, *after)
    return res[0], res[1], res[2:2 + n], res[2 + n:2 + 2 * n], res[2 + 2 * n]


def _split_wait(started, copies_lands, after, name):
    send_sems, recv_sems, arrs, lands, _ = started
    copies = copies_lands[0]
    n = len(arrs)

    def body(*refs):
        for cp in copies(refs[:n], refs[n:2 * n], refs[2 * n], refs[2 * n + 1]):
            cp.wait_send()
            cp.wait_recv()

    res = pl.pallas_call(
        body, name=name, out_shape=[pltpu.HBM(a.shape, a.dtype) for a in list(arrs) + list(lands)],
        in_specs=[HBM_SPEC] * (2 * n) + [SEM_SPEC, SEM_SPEC] + [ANY] * len(after), out_specs=[HBM_SPEC] * (2 * n),
        input_output_aliases={i: i for i in range(2 * n)},
        compiler_params=pltpu.CompilerParams(has_side_effects=pltpu.SideEffectType.DATAFLOW_SIDE_EFFECTING),
    )(*arrs, *lands, send_sems, recv_sems, *after)
    return list(res[:n]), list(res[n:])


def _join_plans(plans):
    def split(seq, counts):
        out, at = [], 0
        for cnt in counts:
            out.append(seq[at:at + cnt])
            at += cnt
        return out

    n_i, n_o, n_s = ([len(getattr(p, f)) for p in plans] for f in ("ins", "outs", "sems"))

    def start(ins, outs, sems):
        for p, i, o, s in zip(plans, split(ins, n_i), split(outs, n_o), split(sems, n_s)):
            p.start(i, o, s)

    def finish(ins, outs, sems):
        for p, i, o, s in zip(plans, split(ins, n_i), split(outs, n_o), split(sems, n_s)):
            p.finish(i, o, s)

    aliases, at_i, at_o = {}, 0, 0
    for p in plans:
        aliases.update(p.io_aliases(at_i, at_o))
        at_i, at_o = at_i + len(p.ins), at_o + len(p.outs)
    return _Plan(sum((p.ins for p in plans), []), sum((p.outs for p in plans), []), sum((p.sems for p in plans), []),
                 start, finish, aliases)


def _add_pair(parts, sib, core, name):
    P4, _, Rh, C = parts.shape
    tm, tc = _tile2(Rh, C, 16)

    def body(c_ref, a_ref, b_ref, o_ref):
        o_ref[...] = (a_ref[0].astype(F32) + b_ref[...].astype(F32)).astype(BF16)

    spec = pl.BlockSpec((1, tm, tc), lambda p, i, j, c_ref: (p, i, j))
    return pl.pallas_call(
        body, name=name, out_shape=jax.ShapeDtypeStruct((P4, Rh, C), BF16),
        grid_spec=pltpu.PrefetchScalarGridSpec(
            num_scalar_prefetch=1, grid=(P4, Rh // tm, C // tc),
            in_specs=[pl.BlockSpec((1, 1, tm, tc), lambda p, i, j, c_ref: (p, c_ref[0], i, j)), spec], out_specs=spec),
        compiler_params=_params(("parallel",) * 3),
    )(core, parts, sib)


def _sum_slabs(pre, recv, chip, name):
    _, Rh, C = pre.shape
    tm, tc = _tile2(Rh, C, 16)

    def body(me_ref, own_ref, r_ref, o_ref):
        acc = own_ref[0].astype(F32)
        for j in range(3):
            acc = acc + r_ref[j].astype(F32)
        o_ref[...] = acc

    return pl.pallas_call(
        body, name=name, out_shape=jax.ShapeDtypeStruct((Rh, C), F32),
        grid_spec=pltpu.PrefetchScalarGridSpec(
            num_scalar_prefetch=1, grid=(Rh // tm, C // tc),
            in_specs=[pl.BlockSpec((1, tm, tc), lambda i, j, me_ref: (me_ref[0], i, j)),
                      pl.BlockSpec((3, tm, tc), lambda i, j, me_ref: (0, i, j))],
            out_specs=pl.BlockSpec((tm, tc), lambda i, j, me_ref: (i, j))),
        compiler_params=_params(("parallel", "parallel")),
    )(chip, pre, recv)


def kernel(x, c, positions, w_ada, b_ada, w_in, g_q_a, w_q_b, g_kv_a, w_kv_b, w_o_a, w_conv, w_o_b, w_o, ln1_g, ln1_b, w_ffn_in, w_ffn_out, ln2_g, ln2_b, loss_target, m_w_ada, m_b_ada, m_w_in, m_g_q_a, m_w_q_b, m_g_kv_a, m_w_kv_b, m_w_o_a, m_w_conv, m_w_o_b, m_w_o, m_ln1_g, m_ln1_b, m_w_ffn_in, m_w_ffn_out, m_ln2_g, m_ln2_b, v_w_ada, v_b_ada, v_w_in, v_g_q_a, v_w_q_b, v_g_kv_a, v_w_kv_b, v_w_o_a, v_w_conv, v_w_o_b, v_w_o, v_ln1_g, v_ln1_b, v_w_ffn_in, v_w_ffn_out, v_ln2_g, v_ln2_b):
    S, D = x.shape[1], x.shape[2]
    F = w_ffn_out.shape[1] * 4
    ax, ay, ac = _place()
    chip = 2 * ax + ay
    dev = 4 * ax + 2 * ay + ac
    x2, tgt = x[0], loss_target[0]
    w_ada2, w_in2, w_q_b2, w_kv_b2 = w_ada[0], w_in[0], w_q_b[0], w_kv_b[0]
    w_o_a2, w_o_b2, w_o2, w_ffn_in2, w_ffn_out2 = w_o_a[0], w_o_b[0], w_o[0], w_ffn_in[0], w_ffn_out[0]
    NA = w_ada2.shape[1]
    CW = w_conv.shape[2]

    inv_freq = 1.0 / (ROPE_THETA ** (jnp.arange(0, QK_ROPE, 2, dtype=F32) / QK_ROPE))
    ang = positions[0].astype(F32)[:, None] * inv_freq
    cos, sin = jnp.cos(ang), jnp.sin(ang)
    z32, z64, z96 = jnp.zeros((S, 32), F32), jnp.zeros((S, 64), F32), jnp.zeros((S, 96), F32)
    tab = jnp.concatenate([cos, cos, z64, -sin, z96, z32, sin, z64], axis=1)

    def halves(a):
        return a.reshape(2, a.shape[0] // 2, a.shape[1])

    def whole(g):
        return g.reshape(4, 2 * g.shape[2], g.shape[3])

    def cols(g):
        return jnp.transpose(g, (1, 0, 2)).reshape(g.shape[1], 4 * g.shape[2])

    w_inT, m_w_inT, v_w_inT = w_in2.T, m_w_in[0].T, v_w_in[0].T
    CS = w_inT.shape[0]
    CSP = -(-CS // 32) * 32
    sh_in = halves(jnp.pad(w_inT.astype(BF16), ((0, CSP - CS), (0, 0))))
    sh_qb, sh_kvb, sh_oa, sh_ob, sh_o, sh_fi, sh_fo = (
        halves(w.astype(BF16)) for w in (w_q_b2, w_kv_b2, w_o_a2, w_o_b2, w_o2, w_ffn_in2, w_ffn_out2))
    c_all = _all_gather8(c, "gather_c").reshape(8, D)
    wconv_all = _all_gather8(w_conv[0], "gather_wconv")
    w_conv_full = jnp.transpose(wconv_all[0::2], (1, 0, 2)).reshape(3, D)
    b_sh = lax.dynamic_slice(b_ada, (0, chip * NA), (1, NA))
    mod_sh = _ada_fwd(c_all, w_ada2, b_sh)
    mod_all = _all_gather8(mod_sh, "gather_mod")
    mod = lax.dynamic_slice(mod_all[0::2], (0, dev, 0), (4, 1, NA)).reshape(6, D)
    shift1, scale1, gate1, shift2, scale2, gate2 = (mod[k:k + 1] for k in range(6))

    g_in, shift1 = _run_plan(_gather_plan([sh_in]), "gather_first", ride=shift1)
    g_in = whole(g_in)
    sh_a1, sh_a2 = [sh_qb, sh_kvb], [sh_oa, sh_ob, sh_o]
    cl_a1, cl_a2, cl_fi, cl_fo = (_gather_copies(g) for g in (sh_a1, sh_a2, [sh_fi], [sh_fo]))
    st_a1 = _split_start(sh_a1, cl_a1, shift1, "gather_a1_start")
    st_a2 = _split_start(sh_a2, cl_a2, st_a1[4], "gather_a2_start")
    shift1 = st_a2[4]

    def in_rows(lo, hi):
        parts = [g_in[p, max(lo, p * CS) - p * CS:min(hi, (p + 1) * CS) - p * CS]
                 for p in range(4) if max(lo, p * CS) < min(hi, (p + 1) * CS)]
        return parts[0] if len(parts) == 1 else jnp.concatenate(parts, axis=0)

    n_qkv = Q_LORA + KV_LORA + QK_ROPE
    W_qkvT = jnp.pad(in_rows(0, n_qkv), ((0, QKV_A - n_qkv), (0, 0)))
    W_convT = in_rows(n_qkv, n_qkv + 3 * D)
    W_gateT = in_rows(n_qkv + 3 * D, n_qkv + 5 * D)

    u = _modulate(x2, scale1, shift1, "modulate1")
    pq = _matmul(u, W_qkvT, "nt", F32, "proj_qkv")
    pc = _matmul(u, W_convT, "nt", F32, "proj_conv")
    sh_a1, la1 = _split_wait(st_a1, cl_a1, [pc], "gather_a1_wait")
    pg, (g_qb, g_kvb) = _matmul(u, W_gateT, "nt", BF16, "proj_gate", carry=_gather_plan(sh_a1, into=la1, ici=False))
    st_fi = _split_start([sh_fi], cl_fi, g_q_a, "gather_fi_start", after=[pg])
    W_qb = jnp.pad(cols(whole(g_qb)).reshape(Q_LORA, N_HEADS, QK_NOPE + QK_ROPE),
                   ((0, 0), (0, 0), (0, QK_PAD - QK_NOPE - QK_ROPE))).reshape(Q_LORA, N_HEADS * QK_PAD)
    W_kvb = cols(whole(g_kvb))
    rq, rkv, kr = _rms_fwd(pq, tab, st_fi[4], g_kv_a)
    q = _q_rope(_matmul(rq, W_qb, "nn", F32, "q_b"), tab)
    kv = _matmul(rkv, W_kvb, "nn", BF16, "kv_b")
    sh_a2, la2 = _split_wait(st_a2, cl_a2, [kv], "gather_a2_wait")
    o, lse, (g_oa, g_ob, g_o) = _attn_fwd(q, kv, kr, carry=_gather_plan(sh_a2, into=la2, ici=False))
    W_oa, W_ob, W_o = (g.reshape(-1, D) for g in (g_oa, g_ob, g_o))
    y_a = _matmul(o, W_oa, "nn", F32, "o_a")
    hb = _conv_fwd(pc, w_conv_full)
    y_b = _matmul(hb, W_ob, "nn", F32, "o_b")
    st_fo = _split_start([sh_fo], cl_fo, ln1_g, "gather_fo_start", after=[y_b])
    merged = _merge_fwd(y_a, y_b, pg)
    sh_fi_t, lfi = _split_wait(st_fi, cl_fi, [merged], "gather_fi_wait")
    mix, (g_fi,) = _matmul(merged, W_o, "nn", F32, "w_o", carry=_gather_plan(sh_fi_t, into=lfi, ici=False))
    W_fi = whole(g_fi)
    x1, u2 = _ln1_fwd(x2, mix, gate1, st_fo[4], ln1_b, scale2, shift2)
    hh = _matmul(u2, W_fi, "nn", BF16, "ffn_in", shards="b")
    sh_fo_t, lfo = _split_wait(st_fo, cl_fo, [hh], "gather_fo_wait")
    W_fo = _run_plan(_gather_plan(sh_fo_t, into=lfo, ici=False), "forward_fo")[0].reshape(F, D)
    act = _swiglu_fwd(hh)
    ffn = _matmul(act, W_fo, "nn", F32, "ffn_out")

    core_i = ac.astype(jnp.int32).reshape(1)
    chip_i = chip.astype(jnp.int32).reshape(1)

    def uncols(g):
        return jnp.transpose(g.reshape(g.shape[0], 4, g.shape[1] // 4), (1, 0, 2))

    def slabs(p):
        return p.reshape(4, 2, p.shape[1] // 2, p.shape[2])

    def add_pairs(parts, sibs, nms):
        return [_add_pair(a, b, core_i, "add_pair_" + nm) for a, b, nm in zip(parts, sibs, nms)]

    def sum_all(pre, recv, nms):
        return [_sum_slabs(a, r, chip_i, "sum_slabs_" + nm) for a, r, nm in zip(pre, recv, nms)]

    dffn, dx1a, loss_acc, d_ln2_g, d_ln2_b, d_gate2 = _ln2_loss_bwd(x1, ffn, gate2, ln2_g, ln2_b, tgt)
    loss = lax.psum(loss_acc[0, 0], ("x", "y", "c"))
    dW_fo = _matmul(act, dffn, "tn", BF16, "d_w_ffn_out")
    p_fo = [slabs(dW_fo.reshape(4, -1, D))]
    dact, s_fo = _matmul(dffn, W_fo, "nt", BF16, "d_act", carry=_pair_plan(p_fo))
    pre_fo = add_pairs(p_fo, s_fo, ["w_ffn_out"])
    cs_fo = _scatter_copies(pre_fo)
    st_sfo = _split_start(pre_fo, cs_fo, scale2, "scatter_fo_start")
    dhh = _swiglu_bwd(dact, hh)
    dW_fi = _matmul(u2, dhh, "tn", BF16, "d_w_ffn_in", shards="o")
    p_fi = [slabs(dW_fi)]
    du2, s_fi = _matmul(dhh, W_fi, "nt", F32, "d_u2", carry=_pair_plan(p_fi), shards="b")
    pre_fi = add_pairs(p_fi, s_fi, ["w_ffn_in"])
    cs_fi = _scatter_copies(pre_fi)
    st_sfi = _split_start(pre_fi, cs_fi, st_sfo[4], "scatter_fi_start")
    dmix, dxa, d_shift2, d_scale2, d_ln1_g, d_ln1_b, d_gate1 = _ln1_bwd(x2, mix, dx1a, du2, gate1, ln1_g, ln1_b, st_sfi[4])
    dW_o = _matmul(merged, dmix, "tn", BF16, "d_w_o")
    dmerged = _matmul(dmix, W_o, "nt", F32, "d_merged")
    dy_a, dy_b, dgate = _merge_bwd(dmerged, y_a, y_b, pg)
    dW_oa = _matmul(o, dy_a, "tn", BF16, "d_w_o_a")
    do = _matmul(dy_a, W_oa, "nt", BF16, "d_o")
    dW_ob = _matmul(hb, dy_b, "tn", BF16, "d_w_o_b")
    p_mid = [slabs(g.reshape(4, -1, D)) for g in (dW_oa, dW_ob, dW_o)]
    dhb, s_mid = _matmul(dy_b, W_ob, "nt", F32, "d_hb", carry=_pair_plan(p_mid))
    pre_mid = add_pairs(p_mid, s_mid, ["w_o_a", "w_o_b", "w_o"])
    cs_mid = _scatter_copies(pre_mid)
    st_smid = _split_start(pre_mid, cs_mid, w_conv_full, "scatter_mid_start")
    dconv, d_wconv = _conv_bwd(dhb, pc, st_smid[4])
    dq, dkv, dkr, _ = _attn_bwd(q, kv, kr, do, o, lse, tab)
    names_a = ["w_ffn_out", "w_ffn_in", "w_o_a", "w_o_b", "w_o"]
    dW_qb = _matmul(rq, dq, "tn", BF16, "d_w_q_b")
    d_rq = _matmul(dq, W_qb, "nt", F32, "d_rq")
    dW_kvb = _matmul(rkv, dkv, "tn", BF16, "d_w_kv_b")
    d_rkv = _matmul(dkv, W_kvb, "nt", F32, "d_rkv")
    dqkv, d_g_q, d_g_kv = _rms_bwd(d_rq, d_rkv, pq, dkr, g_q_a, g_kv_a)
    dW_qkvT = _matmul(dqkv, u, "tn", BF16, "d_w_qkv")
    dW_convT = _matmul(dconv, u, "tn", BF16, "d_w_conv")
    dW_gateT = _matmul(dgate, u, "tn", BF16, "d_w_gate")
    pre_fo, r_fo = _split_wait(st_sfo, cs_fo, [dW_qkvT], "scatter_fo_wait")
    pre_fi, r_fi = _split_wait(st_sfi, cs_fi, [dW_qkvT], "scatter_fi_wait")
    pre_mid, r_mid = _split_wait(st_smid, cs_mid, [dW_qkvT], "scatter_mid_wait")
    fin_a = sum_all(pre_fo + pre_fi + pre_mid, r_fo + r_fi + r_mid, names_a)
    dW_inT = jnp.concatenate([dW_qkvT[:n_qkv], dW_convT, dW_gateT], axis=0).reshape(4, CS, D)
    dW_inT = jnp.pad(dW_inT, ((0, 0), (0, CSP - CS), (0, 0)))
    dW_qb_u = dW_qb.reshape(Q_LORA, N_HEADS, QK_PAD)[:, :, :QK_NOPE + QK_ROPE].reshape(Q_LORA, -1)
    names_b = ["w_in", "w_q_b", "w_kv_b"]
    p_b = [slabs(dW_inT), slabs(uncols(dW_qb_u)), slabs(uncols(dW_kvb))]
    du, s_b = _matmul(dqkv, W_qkvT, "nn", F32, "d_u_qkv", carry=_pair_plan(p_b))
    pre_b = add_pairs(p_b, s_b, names_b)
    cs_b = _scatter_copies(pre_b)
    st_b = _split_start(pre_b, cs_b, scale1, "scatter_last_start")
    du, fs_a = _matmul(dconv, W_convT, "nn", F32, "d_u_conv", add=du, carry=_sibling_plan(fin_a))
    du = _matmul(dgate, W_gateT, "nn", F32, "d_u_gate", add=du)
    grad_x, d_shift1, d_scale1 = _dx_final(dxa, du, x2, st_b[4])

    big = {}
    ws = dict(w_in=(w_inT, m_w_inT, v_w_inT), w_q_b=(w_q_b2, m_w_q_b[0], v_w_q_b[0]),
              w_kv_b=(w_kv_b2, m_w_kv_b[0], v_w_kv_b[0]), w_o_a=(w_o_a2, m_w_o_a[0], v_w_o_a[0]),
              w_o_b=(w_o_b2, m_w_o_b[0], v_w_o_b[0]), w_o=(w_o2, m_w_o[0], v_w_o[0]),
              w_ffn_in=(w_ffn_in2, m_w_ffn_in[0], v_w_ffn_in[0]), w_ffn_out=(w_ffn_out2, m_w_ffn_out[0], v_w_ffn_out[0]))

    def adam_of(nm, a, b, carry=None):
        w_, m_, v_ = ws[nm]
        return _adam_halves("adam_" + nm, w_, m_, v_, a, b, core_i, carry)

    for nm, a, b in zip(names_a, fin_a, fs_a):
        big[nm] = adam_of(nm, a, b)
    done = [big[nm][1] for nm in names_a] + [grad_x]
    pre_b, r_b = _split_wait(st_b, cs_b, done, "scatter_last_wait")
    fin_b = sum_all(pre_b, r_b, names_b)
    fs_b = _run_plan(_sibling_plan(fin_b), "sibling_last")
    for nm, a, b in zip(names_b, fin_b, fs_b):
        big[nm] = adam_of(nm, a, b)

    def pad_d(v):
        return jnp.pad(v, ((0, 0), (0, D - v.shape[1])))

    small = _pack_rows([d_ln1_g, d_ln1_b, d_ln2_g, d_ln2_b, pad_d(d_g_q), pad_d(d_g_kv), d_wconv,
                         d_shift1, d_scale1, d_gate1, d_shift2, d_scale2, d_gate2], 16, after=[pre_b[1]])
    small_all = _all_gather8(small, "gather_small")
    small_sum = _sum8(small_all)
    g_ln1_g, g_ln1_b, g_ln2_g, g_ln2_b = (small_sum[k:k + 1] for k in range(4))
    g_g_q, g_g_kv = small_sum[4:5, :Q_LORA], small_sum[5:6, :KV_LORA]
    g_wconv = lax.dynamic_slice(small_sum[6:9], (0, chip * CW), (3, CW))
    g_b_ada = small_sum[9:15].reshape(1, 6 * D)
    dmod_all = small_all[:, 9:15, :].reshape(8, 6 * D)
    g_w_ada = _ada_bwd(c_all, lax.dynamic_slice(dmod_all, (0, chip * NA), (8, NA)))
    big["w_ada"] = [g_w_ada] + list(_adam("adam_w_ada", w_ada2, m_w_ada[0], v_w_ada[0], g_w_ada))
    sm = {}
    for nm, w_, m_, v_, g_ in [("b_ada", b_ada, m_b_ada, v_b_ada, g_b_ada), ("g_q_a", g_q_a, m_g_q_a, v_g_q_a, g_g_q),
                               ("g_kv_a", g_kv_a, m_g_kv_a, v_g_kv_a, g_g_kv),
                               ("w_conv", w_conv[0], m_w_conv[0], v_w_conv[0], g_wconv),
                               ("ln1_g", ln1_g, m_ln1_g, v_ln1_g, g_ln1_g), ("ln1_b", ln1_b, m_ln1_b, v_ln1_b, g_ln1_b),
                               ("ln2_g", ln2_g, m_ln2_g, v_ln2_g, g_ln2_g), ("ln2_b", ln2_b, m_ln2_b, v_ln2_b, g_ln2_b)]:
        sm[nm] = (g_,) + tuple(_adam_small("adam_" + nm, w_, m_, v_, g_))

    order = ["w_ada", "b_ada", "w_in", "g_q_a", "w_q_b", "g_kv_a", "w_kv_b", "w_o_a", "w_conv", "w_o_b", "w_o",
             "ln1_g", "ln1_b", "w_ffn_in", "w_ffn_out", "ln2_g", "ln2_b"]
    lead = {"b_ada", "g_q_a", "g_kv_a", "ln1_g", "ln1_b", "ln2_g", "ln2_b"}

    def leaf(nm, k):
        val = big[nm][k] if nm in big else sm[nm][k]
        if nm == "w_in":
            val = val.T
        return val if nm in lead else val[None]

    outs = [loss, grad_x[None]]
    for k in range(4):
        outs += [leaf(nm, k) for nm in order]
    return tuple(outs)
```

```python
import functools

import jax
import jax.numpy as jnp
from jax import lax
from jax.experimental import pallas as pl
from jax.experimental.pallas import tpu as pltpu

F32, BF16 = jnp.float32, jnp.bfloat16
N_HEADS, QK_NOPE, QK_ROPE, V_HEAD = 16, 128, 64, 128
Q_LORA, KV_LORA = 512, 512
QK_PAD = 256
QKV_A = 1152
CHUNK_SHIFT = 6
ATTN_SCALE = (QK_NOPE + QK_ROPE) ** -0.5
ROPE_THETA = 10000.0
ALPHA = 2.0 ** 0.25
LN_EPS, RMS_EPS = 1e-5, 1e-6
ADAM_LR, ADAM_B1, ADAM_B2, ADAM_EPS, ADAM_WD, ADAM_STEP = 0.001, 0.9, 0.999, 1e-08, 0.01, 10
ADAM_C1 = 1.0 - ADAM_B1 ** ADAM_STEP
ADAM_C2 = 1.0 - ADAM_B2 ** ADAM_STEP
VMEM_LIMIT = 56 * 1024 * 1024
MESH = pl.DeviceIdType.MESH
ANY = pl.BlockSpec(memory_space=pl.ANY)
HBM_SPEC = pl.BlockSpec(memory_space=pltpu.HBM)
SEM_SPEC = pl.BlockSpec(memory_space=pltpu.SEMAPHORE)
NT = (((1,), (1,)), ((), ()))
TN = (((0,), (0,)), ((), ()))
NN = (((1,), (0,)), ((), ()))


def _params(sem=None):
    return pltpu.CompilerParams(dimension_semantics=sem, vmem_limit_bytes=VMEM_LIMIT)


def _pick(n, cands=(1408, 1024, 512, 384, 256, 128)):
    for t in cands:
        if n % t == 0:
            return t
    return n


def _row_tile(rows, row_bytes, budget, mult=8):
    best = mult
    for t in range(mult, rows + 1, mult):
        if rows % t == 0 and t * row_bytes <= budget:
            best = t
    return best


def _tile2(rows, cols, mult=8, budget=3 << 18):
    col_tiles = [t for t in range(128, cols + 1, 128) if cols % t == 0] or [cols]
    best = None
    for tc in col_tiles:
        for tr in range(mult, rows + 1, mult):
            if rows % tr == 0 and tr * tc <= budget and (best is None or (tr * tc, tc) > (best[0] * best[1], best[1])):
                best = (tr, tc)
    assert best is not None, (rows, cols)
    return best


def _sigmoid(x):
    return jax.nn.sigmoid(x)


class _Plan:
    def __init__(self, ins, outs, sems, start, finish, aliases=None):
        self.ins, self.outs, self.sems, self.start, self.finish = list(ins), list(outs), list(sems), start, finish
        self.aliases = dict(aliases or {})

    def io_aliases(self, first_in, first_out):
        return {first_in + i: first_out + o for i, o in self.aliases.items()}


def _token_plan(token):
    return _Plan([token], [], [], lambda *a: None, lambda *a: None)


def _run_plan(plan, name, ride=None):
    n_in, n_out = len(plan.ins), len(plan.outs)
    extra = [] if ride is None else [ride]
    aliases = plan.io_aliases(0, 0)
    if extra:
        aliases[n_in] = n_out

    def body(*refs):
        ins, outs, sems = refs[:n_in], refs[n_in + len(extra):n_in + len(extra) + n_out], refs[n_in + 2 * len(extra) + n_out:]
        plan.start(ins, outs, sems)
        plan.finish(ins, outs, sems)

    return pl.pallas_call(body, name=name, out_shape=plan.outs + [jax.ShapeDtypeStruct(r.shape, r.dtype) for r in extra],
                          in_specs=[ANY] * (n_in + len(extra)), out_specs=[ANY] * (n_out + len(extra)),
                          scratch_shapes=plan.sems, input_output_aliases=aliases,
                          compiler_params=_params())(*plan.ins, *extra)


def _matmul(a, b, mode, out_dtype, name, add=None, carry=None, shards=None):
    if mode == "nn":
        (M, K), N, dims = a.shape, b.shape[-1] * (4 if shards else 1), NN
    elif mode == "nt":
        (M, K), N, dims = a.shape, b.shape[-2], NT
    else:
        (K, M), N, dims = a.shape, b.shape[1], TN
    split_n = shards and mode != "nt"
    tm = _pick(M)
    tn = _pick(N // 4) if split_n else _pick(N)
    if shards and mode == "nt":
        tk = _pick(K // 4)
    else:
        tk = K if K <= 2048 else _pick(K)
    nk = K // tk
    per = (N // 4 // tn) if split_n else (K // 4 // tk if shards else 1)
    a_spec = (pl.BlockSpec((tk, tm), lambda i, j, k: (k, i)) if mode == "tn"
              else pl.BlockSpec((tm, tk), lambda i, j, k: (i, k)))
    if shards == "b" and mode == "nn":
        b_spec = pl.BlockSpec((None, tk, tn), lambda i, j, k: (j // per, k, j % per))
    elif shards == "b":
        b_spec = pl.BlockSpec((None, tn, tk), lambda i, j, k: (k // per, j, k % per))
    else:
        b_spec = (pl.BlockSpec((tn, tk), lambda i, j, k: (j, k)) if mode == "nt"
                  else pl.BlockSpec((tk, tn), lambda i, j, k: (k, j)))
    o_spec = pl.BlockSpec((tm, tn), lambda i, j, k: (i, j))
    o_shape = (M, N)
    if shards == "o":
        o_spec, o_shape = pl.BlockSpec((None, tm, tn), lambda i, j, k: (j // per, i, j % per)), (4, M, N // 4)
    has_add = add is not None
    n_ci = len(carry.ins) if carry else 0
    n_co = len(carry.outs) if carry else 0
    n_in = 2 + has_add
    grid = (M // tm, N // tn, nk)

    def body(*refs):
        a_ref, b_ref = refs[0], refs[1]
        add_ref = refs[2] if has_add else None
        o_ref = refs[n_in + n_ci]
        acc_ref = refs[n_in + n_ci + 1 + n_co] if nk > 1 else None
        c_ins = refs[n_in:n_in + n_ci]
        c_outs = refs[n_in + n_ci + 1:n_in + n_ci + 1 + n_co]
        c_sems = refs[n_in + n_ci + 1 + n_co + (nk > 1):]
        i, j, k = pl.program_id(0), pl.program_id(1), pl.program_id(2)

        if carry:
            @pl.when((i == 0) & (j == 0) & (k == 0))
            def _():
                carry.start(c_ins, c_outs, c_sems)

        part = lax.dot_general(a_ref[...], b_ref[...], dims, preferred_element_type=F32)
        if nk == 1:
            o_ref[...] = (part + add_ref[...] if has_add else part).astype(o_ref.dtype)
        else:
            @pl.when(k == 0)
            def _():
                acc_ref[...] = part

            @pl.when((k > 0) & (k < nk - 1))
            def _():
                acc_ref[...] += part

            @pl.when(k == nk - 1)
            def _():
                r = acc_ref[...] + part
                if has_add:
                    r = r + add_ref[...]
                o_ref[...] = r.astype(o_ref.dtype)

        if carry:
            @pl.when((i == grid[0] - 1) & (j == grid[1] - 1) & (k == nk - 1))
            def _():
                carry.finish(c_ins, c_outs, c_sems)

    ins = [a, b] + ([add] if has_add else []) + (carry.ins if carry else [])
    in_specs = [a_spec, b_spec] + ([o_spec] if has_add else []) + [ANY] * n_ci
    res = pl.pallas_call(
        body, name=name, grid=grid,
        in_specs=in_specs, out_specs=[o_spec] + [ANY] * n_co,
        out_shape=[jax.ShapeDtypeStruct(o_shape, out_dtype)] + (carry.outs if carry else []),
        scratch_shapes=([pltpu.VMEM((tm, tn), F32)] if nk > 1 else []) + (carry.sems if carry else []),
        input_output_aliases=carry.io_aliases(n_in, 1) if carry else {},
        compiler_params=_params(("arbitrary",) * 3 if carry else ("parallel", "parallel", "arbitrary")),
    )(*ins)
    return (res[0], res[1:]) if carry else res[0]


def _rows(body, name, n_rows, tm, ins, outs, accs=()):
    grid = (n_rows // tm,)
    per8 = tm // 8
    last8 = n_rows // 8 - 1
    arrays, in_specs = [], []
    for spec in ins:
        kind, arr = spec[0], spec[1]
        arrays.append(arr)
        if kind == "row":
            _, _, cb, w = spec
            in_specs.append(pl.BlockSpec((tm, w), lambda i, cb=cb: (i, cb)))
        elif kind == "full":
            in_specs.append(pl.BlockSpec(arr.shape, lambda i, nd=arr.ndim: (0,) * nd))
        elif kind == "prev":
            _, _, cb, w = spec
            in_specs.append(pl.BlockSpec((8, w), lambda i, cb=cb: (jnp.maximum(i * per8 - 1, 0), cb)))
        else:
            _, _, cb, w = spec
            in_specs.append(pl.BlockSpec((8, w), lambda i, cb=cb: (jnp.minimum((i + 1) * per8, last8), cb)))
    out_shape = [jax.ShapeDtypeStruct((n_rows, w), dt) for (w, dt) in outs]
    out_specs = [pl.BlockSpec((tm, w), lambda i: (i, 0)) for (w, _) in outs]
    out_shape += [jax.ShapeDtypeStruct(s, F32) for s in accs]
    out_specs += [pl.BlockSpec(s, lambda i, nd=len(s): (0,) * nd) for s in accs]
    n_in, n_out = len(ins), len(outs)

    def kernel_body(*refs):
        body(pl.program_id(0), refs[:n_in], refs[n_in:n_in + n_out], refs[n_in + n_out:])

    res = pl.pallas_call(
        kernel_body, name=name, grid=grid, in_specs=in_specs, out_specs=out_specs, out_shape=out_shape,
        compiler_params=_params(("arbitrary",)),
    )(*arrays)
    return res


def _acc_add(i, ref, val):
    @pl.when(i == 0)
    def _():
        ref[...] = val

    @pl.when(i > 0)
    def _():
        ref[...] += val


def _rope(t, tab, sign):
    c, sa, sb = tab[:, 0:128], tab[:, 128:256], tab[:, 256:384]
    rot = pltpu.roll(t, 96, 1) * sa + pltpu.roll(t, 32, 1) * sb
    return t * c + rot if sign > 0 else t * c - rot


def _ln_stats(r):
    mu = jnp.mean(r, axis=-1, keepdims=True)
    d = r - mu
    var = jnp.mean(d * d, axis=-1, keepdims=True)
    rstd = lax.rsqrt(var + LN_EPS)
    return d * rstd, rstd


def _ln_bwd(dxh, xh, rstd):
    m1 = jnp.mean(dxh, axis=-1, keepdims=True)
    m2 = jnp.mean(dxh * xh, axis=-1, keepdims=True)
    return rstd * (dxh - m1 - xh * m2)


def _modulate(x, scale, shift, name):
    S, D = x.shape

    def body(i, ins, outs, accs):
        outs[0][...] = (ins[0][...] * (1.0 + ins[1][...]) + ins[2][...]).astype(BF16)

    return _rows(body, name, S, _pick(S, (256, 128)), [("row", x, 0, D), ("full", scale), ("full", shift)], [(D, BF16)])[0]


def _rms_fwd(pq, tab, g_q, g_kv):
    S = pq.shape[0]

    def body(i, ins, outs, accs):
        pq_ref, tab_ref, gq_ref, gkv_ref = ins

        def rms(x, g):
            return x * lax.rsqrt(jnp.mean(x * x, axis=-1, keepdims=True) + RMS_EPS) * g

        outs[0][...] = rms(pq_ref[:, 0:Q_LORA], gq_ref[...]).astype(BF16)
        outs[1][...] = rms(pq_ref[:, Q_LORA:Q_LORA + KV_LORA], gkv_ref[...]).astype(BF16)
        outs[2][...] = _rope(pq_ref[:, Q_LORA + KV_LORA:QKV_A], tab_ref[...], 1).astype(BF16)

    return _rows(body, "rms_fwd", S, _pick(S, (256, 128)),
                 [("row", pq, 0, QKV_A), ("row", tab, 0, 384), ("full", g_q), ("full", g_kv)],
                 [(Q_LORA, BF16), (KV_LORA, BF16), (128, BF16)])


def _q_rope(q, tab):
    S, W = q.shape

    def body(i, ins, outs, accs):
        q_ref, tab_ref = ins
        t = tab_ref[...]
        for h in range(N_HEADS):
            lo = h * QK_PAD
            outs[0][:, lo:lo + 128] = q_ref[:, lo:lo + 128].astype(BF16)
            outs[0][:, lo + 128:lo + 256] = _rope(q_ref[:, lo + 128:lo + 256], t, 1).astype(BF16)

    return _rows(body, "q_rope", S, _pick(S, (256, 128)), [("row", q, 0, W), ("row", tab, 0, 384)], [(W, BF16)])[0]


def _allowed(q0, k0, bq):
    row = q0 + lax.broadcasted_iota(jnp.int32, (bq, bq), 0)
    col = k0 + lax.broadcasted_iota(jnp.int32, (bq, bq), 1)
    return (col >> CHUNK_SHIFT) <= (row >> CHUNK_SHIFT)


ATTN_BLOCK = 512


def _attn_fwd(q, kv, kr, carry=None):
    S = q.shape[0]
    bq = min(ATTN_BLOCK, S)
    nq = S // bq
    n_ci = len(carry.ins) if carry else 0
    n_co = len(carry.outs) if carry else 0

    def body(*refs):
        q_ref, kn_ref, v_ref, kr_ref = refs[:4]
        o_ref, lse_ref = refs[4 + n_ci:6 + n_ci]
        c_ins, c_outs = refs[4:4 + n_ci], refs[6 + n_ci:6 + n_ci + n_co]
        kcat = refs[6 + n_ci + n_co]
        c_sems = refs[7 + n_ci + n_co:]
        qi = pl.program_id(1)
        if carry:
            @pl.when((pl.program_id(0) == 0) & (qi == 0))
            def _():
                carry.start(c_ins, c_outs, c_sems)

        @pl.when(qi == 0)
        def _():
            kcat[:, 0:128] = kn_ref[...]
            kcat[:, 128:256] = kr_ref[...]

        qv = q_ref[...]

        def step(j, carry, masked):
            m, l, acc = carry
            off = pl.multiple_of(j * bq, bq)
            s = lax.dot_general(qv, kcat[pl.ds(off, bq), :], NT, preferred_element_type=F32) * ATTN_SCALE
            if masked:
                s = jnp.where(_allowed(qi * bq, off, bq), s, -1e30)
            m_new = jnp.maximum(m, jnp.max(s, axis=1, keepdims=True))
            a = jnp.exp(m - m_new)
            p = jnp.exp(s - m_new)
            l = a * l + jnp.sum(p, axis=1, keepdims=True)
            acc = a * acc + jnp.dot(p.astype(BF16), v_ref[pl.ds(off, bq), :], preferred_element_type=F32)
            return m_new, l, acc

        init = (jnp.full((bq, 1), -1e30, F32), jnp.zeros((bq, 1), F32), jnp.zeros((bq, V_HEAD), F32))
        below = lax.fori_loop(0, qi, lambda j, cr: step(j, cr, False), init)
        m, l, acc = step(qi, below, True)
        o_ref[...] = (acc / l).astype(BF16)
        lse_ref[0] = m + jnp.log(l)
        if carry:
            @pl.when((pl.program_id(0) == N_HEADS - 1) & (qi == nq - 1))
            def _():
                carry.finish(c_ins, c_outs, c_sems)

    res = pl.pallas_call(
        body, name="attn_fwd", grid=(N_HEADS, nq),
        in_specs=[pl.BlockSpec((bq, QK_PAD), lambda h, i: (i, h)),
                  pl.BlockSpec((S, 128), lambda h, i: (0, 2 * h)),
                  pl.BlockSpec((S, 128), lambda h, i: (0, 2 * h + 1)),
                  pl.BlockSpec((S, 128), lambda h, i: (0, 0))] + [ANY] * n_ci,
        out_specs=[pl.BlockSpec((bq, V_HEAD), lambda h, i: (i, h)),
                   pl.BlockSpec((1, bq, 1), lambda h, i: (h, i, 0))] + [ANY] * n_co,
        out_shape=[jax.ShapeDtypeStruct((S, N_HEADS * V_HEAD), BF16),
                   jax.ShapeDtypeStruct((N_HEADS, S, 1), F32)] + (carry.outs if carry else []),
        scratch_shapes=[pltpu.VMEM((S, QK_PAD), BF16)] + (carry.sems if carry else []),
        input_output_aliases=carry.io_aliases(4, 2) if carry else {},
        compiler_params=_params(("arbitrary", "arbitrary")),
    )(q, kv, kv, kr, *(carry.ins if carry else []))
    return res[0], res[1], res[2:]


def _attn_bwd(q, kv, kr, do, o, lse, tab, carry=None):
    S = q.shape[0]
    bq = min(ATTN_BLOCK, S)
    nq = S // bq

    n_ci = len(carry.ins) if carry else 0
    n_co = len(carry.outs) if carry else 0

    def body(*refs):
        q_ref, kn_ref, v_ref, kr_ref, do_ref, o_ref, lse_ref, tab_ref = refs[:8]
        dq_ref, dkv_ref, dkr_ref = refs[8 + n_ci:11 + n_ci]
        dq_acc, dk_acc, dv_acc, kcat, delta = refs[11 + n_ci + n_co:16 + n_ci + n_co]
        c_ins, c_outs, c_sems = refs[8:8 + n_ci], refs[11 + n_ci:11 + n_ci + n_co], refs[16 + n_ci + n_co:]
        h = pl.program_id(0)
        if carry:
            @pl.when(h == 0)
            def _():
                carry.start(c_ins, c_outs, c_sems)

        dq_acc[...] = jnp.zeros_like(dq_acc)
        dk_acc[...] = jnp.zeros_like(dk_acc)
        dv_acc[...] = jnp.zeros_like(dv_acc)
        kcat[:, 0:128] = kn_ref[...]
        kcat[:, 128:256] = kr_ref[...]
        for r in range(nq):
            rows = slice(r * bq, (r + 1) * bq)
            delta[rows, :] = jnp.sum(do_ref[rows, :].astype(F32) * o_ref[rows, :].astype(F32), axis=1, keepdims=True)

        def pair(i, j, masked):
            rows_i = pl.ds(pl.multiple_of(i * bq, bq), bq)
            rows_j = pl.ds(pl.multiple_of(j * bq, bq), bq)
            qv, dov, k = q_ref[rows_i, :], do_ref[rows_i, :], kcat[rows_j, :]
            s = lax.dot_general(qv, k, NT, preferred_element_type=F32) * ATTN_SCALE
            if masked:
                s = jnp.where(_allowed(i * bq, j * bq, bq), s, -1e30)
            p = jnp.exp(s - lse_ref[0, rows_i, :])
            dv_acc[rows_j, :] += lax.dot_general(p.astype(BF16), dov, TN, preferred_element_type=F32)
            dp = lax.dot_general(dov, v_ref[rows_j, :], NT, preferred_element_type=F32)
            ds = (p * (dp - delta[rows_i, :]) * ATTN_SCALE).astype(BF16)
            dk_acc[rows_j, :] += lax.dot_general(ds, qv, TN, preferred_element_type=F32)
            dq_acc[rows_i, :] += jnp.dot(ds, k, preferred_element_type=F32)

        def kv_step(j, _):
            pair(j, j, True)

            def q_step(i, _):
                pair(i, j, False)
                return 0

            lax.fori_loop(j + 1, nq, q_step, 0)
            return 0

        lax.fori_loop(0, nq, kv_step, 0)

        for r in range(nq):
            rows = slice(r * bq, (r + 1) * bq)
            dq_ref[rows, 0:128] = dq_acc[rows, 0:128].astype(BF16)
            dq_ref[rows, 128:256] = _rope(dq_acc[rows, 128:256], tab_ref[rows, :], -1).astype(BF16)
        dkv_ref[:, 0:128] = dk_acc[:, 0:128].astype(BF16)
        dkv_ref[:, 128:256] = dv_acc[...].astype(BF16)

        @pl.when(h == 0)
        def _():
            dkr_ref[...] = dk_acc[:, 128:256]

        @pl.when(h > 0)
        def _():
            dkr_ref[...] += dk_acc[:, 128:256]

        @pl.when(h == N_HEADS - 1)
        def _():
            for r in range(nq):
                rows = slice(r * bq, (r + 1) * bq)
                dkr_ref[rows, :] = _rope(dkr_ref[rows, :], tab_ref[rows, :], -1)
            if carry:
                carry.finish(c_ins, c_outs, c_sems)

    W = N_HEADS * QK_PAD
    res = pl.pallas_call(
        body, name="attn_bwd", grid=(N_HEADS,),
        in_specs=[pl.BlockSpec((S, QK_PAD), lambda h: (0, h)),
                  pl.BlockSpec((S, 128), lambda h: (0, 2 * h)),
                  pl.BlockSpec((S, 128), lambda h: (0, 2 * h + 1)),
                  pl.BlockSpec((S, 128), lambda h: (0, 0)),
                  pl.BlockSpec((S, V_HEAD), lambda h: (0, h)),
                  pl.BlockSpec((S, V_HEAD), lambda h: (0, h)),
                  pl.BlockSpec((1, S, 1), lambda h: (h, 0, 0)),
                  pl.BlockSpec((S, 384), lambda h: (0, 0))] + [ANY] * n_ci,
        out_specs=[pl.BlockSpec((S, QK_PAD), lambda h: (0, h)),
                   pl.BlockSpec((S, QK_PAD), lambda h: (0, h)),
                   pl.BlockSpec((S, 128), lambda h: (0, 0))] + [ANY] * n_co,
        out_shape=[jax.ShapeDtypeStruct((S, W), BF16), jax.ShapeDtypeStruct((S, W), BF16),
                   jax.ShapeDtypeStruct((S, 128), F32)] + (carry.outs if carry else []),
        scratch_shapes=[pltpu.VMEM((S, QK_PAD), F32), pltpu.VMEM((S, QK_PAD), F32), pltpu.VMEM((S, V_HEAD), F32),
                        pltpu.VMEM((S, QK_PAD), BF16), pltpu.VMEM((S, 1), F32)]
        + (carry.sems if carry else []),
        input_output_aliases=carry.io_aliases(8, 3) if carry else {},
        compiler_params=_params(("arbitrary",)),
    )(q, kv, kv, kr, do, o, lse, tab, *(carry.ins if carry else []))
    return res[0], res[1], res[2], res[3:]


def _shift_down(cur, prev8, i, n):
    tm = cur.shape[0]
    prev8 = jnp.where(i == 0, jnp.zeros_like(prev8), prev8)
    full = jnp.concatenate([prev8, cur], axis=0)
    return pltpu.roll(full, n, 0)[8:8 + tm, :]


def _shift_up(cur, next8, i, last, n):
    tm = cur.shape[0]
    next8 = jnp.where(i == last, jnp.zeros_like(next8), next8)
    full = jnp.concatenate([cur, next8], axis=0)
    return pltpu.roll(full, tm + 8 - n, 0)[0:tm, :]


def _conv_fwd(pc, w_conv):
    S, D = pc.shape[0], pc.shape[1] // 3
    tm = _pick(S, (256, 128))

    def body(i, ins, outs, accs):
        b_ref, c_ref, x_ref, cp_ref, xp_ref, w_ref = ins
        z = c_ref[...] * x_ref[...]
        zp = cp_ref[...] * xp_ref[...]
        cz = w_ref[0:1, :] * _shift_down(z, zp, i, 2) + w_ref[1:2, :] * _shift_down(z, zp, i, 1) + w_ref[2:3, :] * z
        outs[0][...] = (b_ref[...] * cz).astype(BF16)

    return _rows(body, "conv_fwd", S, tm,
                 [("row", pc, 0, D), ("row", pc, 1, D), ("row", pc, 2, D), ("prev", pc, 1, D), ("prev", pc, 2, D),
                  ("full", w_conv)], [(D, BF16)])[0]


def _conv_bwd(dhb, pc, w_conv):
    S, D = dhb.shape
    tm = _pick(S, (256, 128))
    last = S // tm - 1

    def body(i, ins, outs, accs):
        g_ref, b_ref, c_ref, x_ref, cp_ref, xp_ref, gn_ref, bn_ref, w_ref = ins
        w0, w1, w2 = w_ref[0:1, :], w_ref[1:2, :], w_ref[2:3, :]
        c, x, g = c_ref[...], x_ref[...], g_ref[...]
        z = c * x
        zp = cp_ref[...] * xp_ref[...]
        z1, z2 = _shift_down(z, zp, i, 1), _shift_down(z, zp, i, 2)
        cz = w0 * z2 + w1 * z1 + w2 * z
        dcz = g * b_ref[...]
        dczn = gn_ref[...] * bn_ref[...]
        dz = w2 * dcz + w1 * _shift_up(dcz, dczn, i, last, 1) + w0 * _shift_up(dcz, dczn, i, last, 2)
        outs[0][:, 0:D] = (g * cz).astype(BF16)
        outs[0][:, D:2 * D] = (dz * x).astype(BF16)
        outs[0][:, 2 * D:3 * D] = (dz * c).astype(BF16)
        dw = jnp.concatenate([jnp.sum(dcz * z2, axis=0, keepdims=True), jnp.sum(dcz * z1, axis=0, keepdims=True),
                              jnp.sum(dcz * z, axis=0, keepdims=True)], axis=0)
        _acc_add(i, accs[0], dw)

    return _rows(body, "conv_bwd", S, tm,
                 [("row", dhb, 0, D), ("row", pc, 0, D), ("row", pc, 1, D), ("row", pc, 2, D),
                  ("prev", pc, 1, D), ("prev", pc, 2, D), ("next", dhb, 0, D), ("next", pc, 0, D), ("full", w_conv)],
                 [(3 * D, BF16)], [(3, D)])


def _merge_fwd(y_a, y_b, pg):
    S, D = y_a.shape

    def body(i, ins, outs, accs):
        ya, yb, ga, gb = ins
        outs[0][...] = (_sigmoid(ga[...].astype(F32)) * ya[...] + _sigmoid(gb[...].astype(F32)) * yb[...]).astype(BF16)

    return _rows(body, "merge_fwd", S, _pick(S, (256, 128)),
                 [("row", y_a, 0, D), ("row", y_b, 0, D), ("row", pg, 0, D), ("row", pg, 1, D)], [(D, BF16)])[0]


def _merge_bwd(dm, y_a, y_b, pg):
    S, D = dm.shape

    def body(i, ins, outs, accs):
        d, ya, yb = ins[0][...], ins[1][...], ins[2][...]
        sa, sb = _sigmoid(ins[3][...].astype(F32)), _sigmoid(ins[4][...].astype(F32))
        outs[0][...] = (d * sa).astype(BF16)
        outs[1][...] = (d * sb).astype(BF16)
        outs[2][:, 0:D] = (d * ya * (sa * (1.0 - sa))).astype(BF16)
        outs[2][:, D:2 * D] = (d * yb * (sb * (1.0 - sb))).astype(BF16)

    return _rows(body, "merge_bwd", S, _pick(S, (256, 128)),
                 [("row", dm, 0, D), ("row", y_a, 0, D), ("row", y_b, 0, D), ("row", pg, 0, D), ("row", pg, 1, D)],
                 [(D, BF16), (D, BF16), (2 * D, BF16)])


def _ln1_fwd(x, mix, gate1, g, b, scale2, shift2):
    S, D = x.shape

    def body(i, ins, outs, accs):
        x_ref, mix_ref, gate_ref, g_ref, b_ref, sc_ref, sh_ref = ins
        xh, _ = _ln_stats(ALPHA * x_ref[...] + gate_ref[...] * mix_ref[...])
        x1 = xh * g_ref[...] + b_ref[...]
        outs[0][...] = x1
        outs[1][...] = (x1 * (1.0 + sc_ref[...]) + sh_ref[...]).astype(BF16)

    return _rows(body, "ln1_fwd", S, _pick(S, (256, 128)),
                 [("row", x, 0, D), ("row", mix, 0, D), ("full", gate1), ("full", g), ("full", b),
                  ("full", scale2), ("full", shift2)], [(D, F32), (D, BF16)])


def _swiglu_fwd(hh):
    S, F = hh.shape[0], hh.shape[1] // 2

    def body(i, ins, outs, accs):
        hg = ins[0][...].astype(F32)
        outs[0][...] = (hg * _sigmoid(hg) * ins[1][...].astype(F32)).astype(BF16)

    return _rows(body, "swiglu_fwd", S, _pick(S, (128,)), [("row", hh, 0, F), ("row", hh, 1, F)], [(F, BF16)])[0]


def _swiglu_bwd(dact, hh):
    S, F = dact.shape

    def body(i, ins, outs, accs):
        d, hg, hu = ins[0][...].astype(F32), ins[1][...].astype(F32), ins[2][...].astype(F32)
        sg = _sigmoid(hg)
        outs[0][:, 0:F] = (d * hu * (sg * (1.0 + hg * (1.0 - sg)))).astype(BF16)
        outs[0][:, F:2 * F] = (d * (hg * sg)).astype(BF16)

    return _rows(body, "swiglu_bwd", S, _pick(S, (128,)),
                 [("row", dact, 0, F), ("row", hh, 0, F), ("row", hh, 1, F)], [(2 * F, BF16)])[0]


def _ln2_loss_bwd(x1, ffn, gate2, g, b, target):
    S, D = x1.shape

    def body(i, ins, outs, accs):
        x1_ref, f_ref, gate_ref, g_ref, b_ref, t_ref = ins
        f = f_ref[...]
        xh, rstd = _ln_stats(ALPHA * x1_ref[...] + gate_ref[...] * f)
        e = xh * g_ref[...] + b_ref[...] - t_ref[...]
        dy = e * (1.0 / D)
        dr = _ln_bwd(dy * g_ref[...], xh, rstd)
        outs[0][...] = (gate_ref[...] * dr).astype(BF16)
        outs[1][...] = ALPHA * dr
        _acc_add(i, accs[0], jnp.full((1, 128), (0.5 / D) * jnp.sum(e * e), F32))
        _acc_add(i, accs[1], jnp.sum(dy * xh, axis=0, keepdims=True))
        _acc_add(i, accs[2], jnp.sum(dy, axis=0, keepdims=True))
        _acc_add(i, accs[3], jnp.sum(dr * f, axis=0, keepdims=True))

    return _rows(body, "ln2_loss_bwd", S, _pick(S, (256, 128)),
                 [("row", x1, 0, D), ("row", ffn, 0, D), ("full", gate2), ("full", g), ("full", b), ("row", target, 0, D)],
                 [(D, BF16), (D, F32)], [(1, 128), (1, D), (1, D), (1, D)])


def _ln1_bwd(x, mix, dx1a, du2, gate1, g, b, scale2):
    S, D = x.shape

    def body(i, ins, outs, accs):
        x_ref, mix_ref, da_ref, du_ref, gate_ref, g_ref, b_ref, sc_ref = ins
        mix, du = mix_ref[...], du_ref[...]
        xh, rstd = _ln_stats(ALPHA * x_ref[...] + gate_ref[...] * mix)
        x1 = xh * g_ref[...] + b_ref[...]
        dx1 = da_ref[...] + du * (1.0 + sc_ref[...])
        dr = _ln_bwd(dx1 * g_ref[...], xh, rstd)
        outs[0][...] = (gate_ref[...] * dr).astype(BF16)
        outs[1][...] = ALPHA * dr
        _acc_add(i, accs[0], jnp.sum(du, axis=0, keepdims=True))
        _acc_add(i, accs[1], jnp.sum(du * x1, axis=0, keepdims=True))
        _acc_add(i, accs[2], jnp.sum(dx1 * xh, axis=0, keepdims=True))
        _acc_add(i, accs[3], jnp.sum(dx1, axis=0, keepdims=True))
        _acc_add(i, accs[4], jnp.sum(dr * mix, axis=0, keepdims=True))

    return _rows(body, "ln1_bwd", S, _pick(S, (256, 128)),
                 [("row", x, 0, D), ("row", mix, 0, D), ("row", dx1a, 0, D), ("row", du2, 0, D),
                  ("full", gate1), ("full", g), ("full", b), ("full", scale2)],
                 [(D, BF16), (D, F32)], [(1, D)] * 5)


def _rms_bwd(d_rq, d_rkv, pq, dkr, g_q, g_kv):
    S = pq.shape[0]

    def body(i, ins, outs, accs):
        dq_ref, dkv_ref, pq_ref, dkr_ref, gq_ref, gkv_ref = ins

        def rms_bwd(dy, x, g):
            r = lax.rsqrt(jnp.mean(x * x, axis=-1, keepdims=True) + RMS_EPS)
            dyg = dy * g
            dx = r * dyg - x * (r * r * r) * jnp.mean(dyg * x, axis=-1, keepdims=True)
            return dx, jnp.sum(dy * (x * r), axis=0, keepdims=True)

        dxq, dgq = rms_bwd(dq_ref[...], pq_ref[:, 0:Q_LORA], gq_ref[...])
        dxkv, dgkv = rms_bwd(dkv_ref[...], pq_ref[:, Q_LORA:Q_LORA + KV_LORA], gkv_ref[...])
        outs[0][:, 0:Q_LORA] = dxq.astype(BF16)
        outs[0][:, Q_LORA:Q_LORA + KV_LORA] = dxkv.astype(BF16)
        outs[0][:, Q_LORA + KV_LORA:QKV_A] = dkr_ref[...].astype(BF16)
        _acc_add(i, accs[0], dgq)
        _acc_add(i, accs[1], dgkv)

    return _rows(body, "rms_bwd", S, _pick(S, (256, 128)),
                 [("row", d_rq, 0, Q_LORA), ("row", d_rkv, 0, KV_LORA), ("row", pq, 0, QKV_A), ("row", dkr, 0, 128),
                  ("full", g_q), ("full", g_kv)], [(QKV_A, BF16)], [(1, Q_LORA), (1, KV_LORA)])


def _dx_final(dxa, du, x, scale1):
    S, D = x.shape

    def body(i, ins, outs, accs):
        du = ins[1][...]
        outs[0][...] = ins[0][...] + du * (1.0 + ins[3][...])
        _acc_add(i, accs[0], jnp.sum(du, axis=0, keepdims=True))
        _acc_add(i, accs[1], jnp.sum(du * ins[2][...], axis=0, keepdims=True))

    return _rows(body, "dx_final", S, _pick(S, (256, 128)),
                 [("row", dxa, 0, D), ("row", du, 0, D), ("row", x, 0, D), ("full", scale1)],
                 [(D, F32)], [(1, D), (1, D)])


def _ada_fwd(c_all, w, bias):
    B, D = c_all.shape
    NA = w.shape[1]
    tn = _pick(NA, (512, 256, 128))

    def body(c_ref, w_ref, b_ref, o_ref):
        cv = c_ref[...]
        ca = (cv * _sigmoid(cv)).astype(BF16)
        o_ref[...] = jnp.dot(ca, w_ref[...].astype(BF16), preferred_element_type=F32) + b_ref[...]

    return pl.pallas_call(
        body, name="ada_fwd", grid=(NA // tn,),
        in_specs=[pl.BlockSpec((B, D), lambda j: (0, 0)), pl.BlockSpec((D, tn), lambda j: (0, j)),
                  pl.BlockSpec((1, tn), lambda j: (0, j))],
        out_specs=pl.BlockSpec((B, tn), lambda j: (0, j)),
        out_shape=jax.ShapeDtypeStruct((B, NA), F32),
        compiler_params=_params(("arbitrary",)),
    )(c_all, w, bias)


def _ada_bwd(c_all, dmod):
    B, D = c_all.shape
    NA = dmod.shape[1]
    tn = _pick(NA, (512, 256, 128))

    def body(c_ref, d_ref, o_ref):
        cv = c_ref[...]
        ca = (cv * _sigmoid(cv)).astype(BF16)
        o_ref[...] = lax.dot_general(ca, d_ref[...].astype(BF16), TN, preferred_element_type=F32)

    return pl.pallas_call(
        body, name="ada_bwd", grid=(NA // tn,),
        in_specs=[pl.BlockSpec((B, D), lambda j: (0, 0)), pl.BlockSpec((B, tn), lambda j: (0, j))],
        out_specs=pl.BlockSpec((D, tn), lambda j: (0, j)),
        out_shape=jax.ShapeDtypeStruct((D, NA), F32),
        compiler_params=_params(("arbitrary",)),
    )(c_all, dmod)


def _pack_rows(parts, n_rows, after=()):
    N = parts[0].shape[1]
    n = len(parts)

    def body(*refs):
        o_ref = refs[-1]
        o_ref[...] = jnp.zeros_like(o_ref)
        at = 0
        for r in refs[:n]:
            o_ref[at:at + r.shape[0], :] = r[...]
            at += r.shape[0]

    vmem = pl.BlockSpec(memory_space=pltpu.VMEM)
    return pl.pallas_call(body, name="pack_small", out_shape=jax.ShapeDtypeStruct((n_rows, N), F32),
                          in_specs=[vmem] * n + [ANY] * len(after), out_specs=vmem,
                          compiler_params=_params())(*parts, *after)


def _sum8(parts):
    _, R, N = parts.shape

    def body(p_ref, o_ref):
        acc = p_ref[0]
        for d in range(1, 8):
            acc = acc + p_ref[d]
        o_ref[...] = acc

    return pl.pallas_call(body, name="sum8", out_shape=jax.ShapeDtypeStruct((R, N), F32),
                          compiler_params=_params())(parts)


def _adam_math(w, g, m, v):
    m = ADAM_B1 * m + (1.0 - ADAM_B1) * g
    v = ADAM_B2 * v + (1.0 - ADAM_B2) * (g * g)
    delta = -ADAM_LR * ((m / ADAM_C1) / (jnp.sqrt(v / ADAM_C2) + ADAM_EPS) + ADAM_WD * w)
    return delta, m, v


def _adam(name, w, m, v, g, carry=None):
    R, C = w.shape
    tm = _row_tile(R, C * 4, 1 << 20)
    steps = R // tm
    n_ci = len(carry.ins) if carry else 0
    n_co = len(carry.outs) if carry else 0

    def body(*refs):
        w_ref, m_ref, v_ref, g_ref = refs[:4]
        d_ref, nm_ref, nv_ref = refs[4 + n_ci:7 + n_ci]
        c_ins, c_outs, c_sems = refs[4:4 + n_ci], refs[7 + n_ci:7 + n_ci + n_co], refs[7 + n_ci + n_co:]
        if carry:
            @pl.when(pl.program_id(0) == 0)
            def _():
                carry.start(c_ins, c_outs, c_sems)

        delta, nm, nv = _adam_math(w_ref[...], g_ref[...], m_ref[...], v_ref[...])
        d_ref[...] = delta
        nm_ref[...] = nm
        nv_ref[...] = nv
        if carry:
            @pl.when(pl.program_id(0) == steps - 1)
            def _():
                carry.finish(c_ins, c_outs, c_sems)

    spec = pl.BlockSpec((tm, C), lambda i: (i, 0))
    res = pl.pallas_call(
        body, name=name, grid=(steps,), in_specs=[spec] * 4 + [ANY] * n_ci, out_specs=[spec] * 3 + [ANY] * n_co,
        out_shape=[jax.ShapeDtypeStruct((R, C), F32)] * 3 + (carry.outs if carry else []),
        scratch_shapes=carry.sems if carry else [],
        input_output_aliases=carry.io_aliases(4, 3) if carry else {},
        compiler_params=_params(("arbitrary",)),
    )(w, m, v, g, *(carry.ins if carry else []))
    return (res[:3], res[3:]) if carry else res


def _adam_halves(name, w, m, v, mine, other, core, carry=None):
    R, C = w.shape
    Rh = mine.shape[0]
    tc = max(t for t in range(128, C + 1, 128) if C % t == 0 and R * t <= (3 << 17))
    steps = C // tc
    n_ci = len(carry.ins) if carry else 0
    n_co = len(carry.outs) if carry else 0

    def body(*refs):
        c_ref, w_ref, m_ref, v_ref, a_ref, b_ref = refs[:6]
        g_ref, d_ref, nm_ref, nv_ref = refs[6 + n_ci:10 + n_ci]
        c_ins, c_outs, c_sems = refs[6:6 + n_ci], refs[10 + n_ci:10 + n_ci + n_co], refs[10 + n_ci + n_co:]
        if carry:
            @pl.when(pl.program_id(0) == 0)
            def _():
                carry.start(c_ins, c_outs, c_sems)

        first = c_ref[0] == 0
        g = jnp.concatenate([jnp.where(first, a_ref[...], b_ref[...]),
                             jnp.where(first, b_ref[0:R - Rh, :], a_ref[0:R - Rh, :])], axis=0)
        delta, nm, nv = _adam_math(w_ref[...], g, m_ref[...], v_ref[...])
        g_ref[...] = g
        d_ref[...] = delta
        nm_ref[...] = nm
        nv_ref[...] = nv
        if carry:
            @pl.when(pl.program_id(0) == steps - 1)
            def _():
                carry.finish(c_ins, c_outs, c_sems)

    spec = pl.BlockSpec((R, tc), lambda i, c_ref: (0, i))
    h_spec = pl.BlockSpec((Rh, tc), lambda i, c_ref: (0, i))
    res = pl.pallas_call(
        body, name=name, out_shape=[jax.ShapeDtypeStruct((R, C), F32)] * 4 + (carry.outs if carry else []),
        grid_spec=pltpu.PrefetchScalarGridSpec(
            num_scalar_prefetch=1, grid=(steps,), in_specs=[spec, spec, spec, h_spec, h_spec] + [ANY] * n_ci,
            out_specs=[spec] * 4 + [ANY] * n_co, scratch_shapes=carry.sems if carry else []),
        input_output_aliases=carry.io_aliases(6, 4) if carry else {},
        compiler_params=_params(("arbitrary",)),
    )(core, w, m, v, mine, other, *(carry.ins if carry else []))
    return (res[:4], res[4:]) if carry else res


def _adam_small(name, w, m, v, g):
    def body(w_ref, m_ref, v_ref, g_ref, d_ref, nm_ref, nv_ref):
        delta, nm, nv = _adam_math(w_ref[...], g_ref[...], m_ref[...], v_ref[...])
        d_ref[...] = delta
        nm_ref[...] = nm
        nv_ref[...] = nv

    return pl.pallas_call(body, name=name, out_shape=[jax.ShapeDtypeStruct(w.shape, F32)] * 3,
                          compiler_params=_params())(w, m, v, g)


def _place():
    return lax.axis_index("x"), lax.axis_index("y"), lax.axis_index("c")


def _other_chips(x, y):
    return [(1 - x, y), (x, 1 - y), (1 - x, 1 - y)]


def _all_gather8(blk, name):
    R, N = blk.shape

    def body(x_ref, out_ref, send_sems, recv_sems, local_sem):
        x, y, c = _place()
        me = 4 * x + 2 * y + c
        mine = pltpu.make_async_copy(x_ref, out_ref.at[me], local_sem)
        mine.start()
        flips = [(j >> 2 & 1, j >> 1 & 1, j & 1) for j in range(1, 8)]
        peers = [((1 - x) if fx else x, (1 - y) if fy else y, (1 - c) if fc else c) for fx, fy, fc in flips]
        sends = []
        for j, peer in enumerate(peers):
            cp = pltpu.make_async_remote_copy(src_ref=x_ref, dst_ref=out_ref.at[me], send_sem=send_sems.at[j],
                                              recv_sem=recv_sems.at[j], device_id=peer, device_id_type=MESH)
            cp.start()
            sends.append(cp)
        for j, (px, py, pc) in enumerate(peers):
            pltpu.make_async_remote_copy(src_ref=x_ref, dst_ref=out_ref.at[4 * px + 2 * py + pc],
                                         send_sem=send_sems.at[j], recv_sem=recv_sems.at[j],
                                         device_id=(px, py, pc), device_id_type=MESH).wait_recv()
        for cp in sends:
            cp.wait_send()
        mine.wait()

    return pl.pallas_call(
        body, name=name, out_shape=jax.ShapeDtypeStruct((8, R, N), F32),
        in_specs=[pl.BlockSpec(memory_space=pltpu.VMEM)], out_specs=pl.BlockSpec(memory_space=pltpu.VMEM),
        scratch_shapes=[pltpu.SemaphoreType.DMA((7,)), pltpu.SemaphoreType.DMA((7,)), pltpu.SemaphoreType.DMA],
        compiler_params=_params(),
    )(blk)


def _piece(rows, piece):
    i, n, k = piece if len(piece) == 3 else (piece[0], piece[1], 1)
    assert rows % 16 == 0 and rows // 16 >= n, (rows, piece)
    lo, hi = (rows // 16 * i // n) * 16, (rows // 16 * (i + k) // n) * 16
    return pl.ds(lo, hi - lo)


def _scatter_plan(arrs, piece=(0, 1), into=None):
    n = len(arrs)

    def copies(ins, outs, sems):
        send_sems, recv_sems = sems
        x, y, c = _place()
        chips = _other_chips(x, y)
        cps = []
        for k in range(n):
            rows = _piece(arrs[k].shape[1], piece)
            for j, (px, py) in enumerate(chips):
                cps.append(pltpu.make_async_remote_copy(
                    src_ref=ins[k].at[2 * px + py, rows], dst_ref=outs[k].at[j, rows],
                    send_sem=send_sems.at[3 * k + j], recv_sem=recv_sems.at[3 * k + j],
                    device_id=(px, py, c), device_id_type=MESH))
        return cps

    def start(ins, outs, sems):
        for cp in copies(ins, outs, sems):
            cp.start()

    def finish(ins, outs, sems):
        for cp in copies(ins, outs, sems):
            cp.wait()

    return _Plan(list(arrs) + list(into or []), [jax.ShapeDtypeStruct((3,) + a.shape[1:], a.dtype) for a in arrs],
                 [pltpu.SemaphoreType.DMA((3 * n,))] * 2, start, finish,
                 aliases={n + k: k for k in range(n)} if into else None)


def _gather_plan(shards, piece=(0, 1), into=None, ici=True):
    n = len(shards)

    def parts(ins, outs, sems):
        s1, r1, s2, r2, loc = sems
        x, y, c = _place()
        me = 2 * x + y
        chips = _other_chips(x, y)
        sib = (x, y, 1 - c)

        def rows(k):
            return _piece(shards[k].shape[1], piece)

        def ici_copy(k, j, slab, to):
            return pltpu.make_async_remote_copy(src_ref=ins[k].at[c, rows(k)], dst_ref=outs[k].at[slab, c, rows(k)],
                                                send_sem=s1.at[3 * k + j], recv_sem=r1.at[3 * k + j],
                                                device_id=to, device_id_type=MESH)

        def d2d(k, j, slab, half):
            return pltpu.make_async_remote_copy(src_ref=outs[k].at[slab, half, rows(k)],
                                                dst_ref=outs[k].at[slab, half, rows(k)],
                                                send_sem=s2.at[3 * k + j], recv_sem=r2.at[3 * k + j],
                                                device_id=sib, device_id_type=MESH)

        def own(k):
            return pltpu.make_async_remote_copy(src_ref=ins[k].at[:, rows(k)], dst_ref=outs[k].at[me, :, rows(k)],
                                                send_sem=loc.at[2 * k], recv_sem=loc.at[2 * k + 1],
                                                device_id=sib, device_id_type=MESH)

        return c, me, chips, ici_copy, d2d, own

    def start(ins, outs, sems):
        c, me, chips, ici_copy, d2d, own = parts(ins, outs, sems)
        for k in range(n):
            for j, (px, py) in enumerate(chips):
                (ici_copy(k, j, me, (px, py, c)) if ici else d2d(k, j, 2 * px + py, c)).start()
        for k in range(n):
            own(k).start()

    def finish(ins, outs, sems):
        c, me, chips, ici_copy, d2d, own = parts(ins, outs, sems)
        if ici:
            for k in range(n):
                for j, (px, py) in enumerate(chips):
                    ici_copy(k, j, 2 * px + py, (px, py, c)).wait_recv()
                    d2d(k, j, 2 * px + py, c).start()
        for k in range(n):
            for j, (px, py) in enumerate(chips):
                d2d(k, j, 2 * px + py, 1 - c).wait_recv()
        for k in range(n):
            own(k).wait()
            for j, (px, py) in enumerate(chips):
                if ici:
                    ici_copy(k, j, me, (px, py, c)).wait_send()
                d2d(k, j, 2 * px + py, c).wait_send()

    return _Plan(list(shards) + list(into or []), [jax.ShapeDtypeStruct((4,) + a.shape, a.dtype) for a in shards],
                 [pltpu.SemaphoreType.DMA((3 * n,))] * 4 + [pltpu.SemaphoreType.DMA((2 * n,))], start, finish,
                 aliases={n + k: k for k in range(n)} if into else None)


def _pair_plan(parts):
    n = len(parts)

    def copies(ins, outs, sems):
        send_sems, recv_sems = sems
        x, y, c = _place()
        return [pltpu.make_async_remote_copy(src_ref=ins[k].at[p, 1 - c], dst_ref=outs[k].at[p],
                                             send_sem=send_sems.at[4 * k + p], recv_sem=recv_sems.at[4 * k + p],
                                             device_id=(x, y, 1 - c), device_id_type=MESH)
                for k in range(n) for p in range(4)]

    def start(ins, outs, sems):
        for cp in copies(ins, outs, sems):
            cp.start()

    def finish(ins, outs, sems):
        for cp in copies(ins, outs, sems):
            cp.wait()

    return _Plan(parts, [jax.ShapeDtypeStruct((4,) + a.shape[2:], a.dtype) for a in parts],
                 [pltpu.SemaphoreType.DMA((4 * n,))] * 2, start, finish)


def _sibling_plan(arrs):
    n = len(arrs)

    def copies(ins, outs, sems):
        send_sems, recv_sems = sems
        x, y, c = _place()
        return [pltpu.make_async_remote_copy(src_ref=ins[k], dst_ref=outs[k], send_sem=send_sems.at[k],
                                             recv_sem=recv_sems.at[k], device_id=(x, y, 1 - c), device_id_type=MESH)
                for k in range(n)]

    def start(ins, outs, sems):
        for cp in copies(ins, outs, sems):
            cp.start()

    def finish(ins, outs, sems):
        for cp in copies(ins, outs, sems):
            cp.wait()

    return _Plan(arrs, [jax.ShapeDtypeStruct(a.shape, a.dtype) for a in arrs],
                 [pltpu.SemaphoreType.DMA((n,))] * 2, start, finish)


def _scatter_copies(arrs):
    def copies(ins, land, send_sems, recv_sems):
        x, y, c = _place()
        return [pltpu.make_async_remote_copy(src_ref=ins[k].at[2 * px + py], dst_ref=land[k].at[j],
                                             send_sem=send_sems.at[3 * k + j], recv_sem=recv_sems.at[3 * k + j],
                                             device_id=(px, py, c), device_id_type=MESH)
                for k in range(len(arrs)) for j, (px, py) in enumerate(_other_chips(x, y))]

    return copies, [lax.empty((3,) + a.shape[1:], a.dtype) for a in arrs]


def _gather_copies(shards):
    def copies(ins, land, send_sems, recv_sems):
        x, y, c = _place()
        return [pltpu.make_async_remote_copy(src_ref=ins[k].at[c], dst_ref=land[k].at[2 * x + y, c],
                                             send_sem=send_sems.at[3 * k + j], recv_sem=recv_sems.at[3 * k + j],
                                             device_id=(px, py, c), device_id_type=MESH)
                for k in range(len(shards)) for j, (px, py) in enumerate(_other_chips(x, y))]

    return copies, [lax.empty((4,) + a.shape, a.dtype) for a in shards]


def _split_start(arrs, copies_lands, ride, name, after=()):
    copies, lands = copies_lands
    n = len(arrs)
    rides = list(ride) if isinstance(ride, (list, tuple)) else [ride]
    n_thru = 2 * n + len(rides)

    def body(*refs):
        first_out = n_thru + len(after)
        for cp in copies(refs[:n], refs[n:2 * n], refs[first_out], refs[first_out + 1]):
            cp.start()

    hbm = [pltpu.with_memory_space_constraint(a, pltpu.HBM) for a in list(arrs) + lands + rides]
    res = pl.pallas_call(
        body, name=name,
        out_shape=[pltpu.SemaphoreType.DMA((3 * n,)), pltpu.SemaphoreType.DMA((3 * n,))]
        + [pltpu.HBM(a.shape, a.dtype) for a in hbm],
        in_specs=[HBM_SPEC] * n_thru + [ANY] * len(after),
        out_specs=[SEM_SPEC, SEM_SPEC] + [HBM_SPEC] * n_thru,
        input_output_aliases={i: 2 + i for i in range(n_thru)},
        compiler_params=pltpu.CompilerParams(has_side_effects=pltpu.SideEffectType.DATAFLOW_SIDE_EFFECTING),
    )(*hbm, *after)
    return res[0], res[1], res[2:2 + n], res[2 + n:2 + 2 * n], list(res[2 + 2 * n:])


def _split_wait(started, copies_lands, after, name):
    send_sems, recv_sems, arrs, lands, _ = started
    copies = copies_lands[0]
    n = len(arrs)

    def body(*refs):
        for cp in copies(refs[:n], refs[n:2 * n], refs[2 * n], refs[2 * n + 1]):
            cp.wait_send()
            cp.wait_recv()

    res = pl.pallas_call(
        body, name=name, out_shape=[pltpu.HBM(a.shape, a.dtype) for a in list(arrs) + list(lands)],
        in_specs=[HBM_SPEC] * (2 * n) + [SEM_SPEC, SEM_SPEC] + [ANY] * len(after), out_specs=[HBM_SPEC] * (2 * n),
        input_output_aliases={i: i for i in range(2 * n)},
        compiler_params=pltpu.CompilerParams(has_side_effects=pltpu.SideEffectType.DATAFLOW_SIDE_EFFECTING),
    )(*arrs, *lands, send_sems, recv_sems, *after)
    return list(res[:n]), list(res[n:])


def _join_plans(plans):
    def split(seq, counts):
        out, at = [], 0
        for cnt in counts:
            out.append(seq[at:at + cnt])
            at += cnt
        return out

    n_i, n_o, n_s = ([len(getattr(p, f)) for p in plans] for f in ("ins", "outs", "sems"))

    def start(ins, outs, sems):
        for p, i, o, s in zip(plans, split(ins, n_i), split(outs, n_o), split(sems, n_s)):
            p.start(i, o, s)

    def finish(ins, outs, sems):
        for p, i, o, s in zip(plans, split(ins, n_i), split(outs, n_o), split(sems, n_s)):
            p.finish(i, o, s)

    aliases, at_i, at_o = {}, 0, 0
    for p in plans:
        aliases.update(p.io_aliases(at_i, at_o))
        at_i, at_o = at_i + len(p.ins), at_o + len(p.outs)
    return _Plan(sum((p.ins for p in plans), []), sum((p.outs for p in plans), []), sum((p.sems for p in plans), []),
                 start, finish, aliases)


def _add_pair(parts, sib, core, name):
    P4, _, Rh, C = parts.shape
    tm, tc = _tile2(Rh, C, 16)

    def body(c_ref, a_ref, b_ref, o_ref):
        o_ref[...] = (a_ref[0].astype(F32) + b_ref[...].astype(F32)).astype(BF16)

    spec = pl.BlockSpec((1, tm, tc), lambda p, i, j, c_ref: (p, i, j))
    return pl.pallas_call(
        body, name=name, out_shape=jax.ShapeDtypeStruct((P4, Rh, C), BF16),
        grid_spec=pltpu.PrefetchScalarGridSpec(
            num_scalar_prefetch=1, grid=(P4, Rh // tm, C // tc),
            in_specs=[pl.BlockSpec((1, 1, tm, tc), lambda p, i, j, c_ref: (p, c_ref[0], i, j)), spec], out_specs=spec),
        compiler_params=_params(("parallel",) * 3),
    )(core, parts, sib)


def _sum_slabs(pre, recv, chip, name):
    _, Rh, C = pre.shape
    tm, tc = _tile2(Rh, C, 16)

    def body(me_ref, own_ref, r_ref, o_ref):
        acc = own_ref[0].astype(F32)
        for j in range(3):
            acc = acc + r_ref[j].astype(F32)
        o_ref[...] = acc

    return pl.pallas_call(
        body, name=name, out_shape=jax.ShapeDtypeStruct((Rh, C), F32),
        grid_spec=pltpu.PrefetchScalarGridSpec(
            num_scalar_prefetch=1, grid=(Rh // tm, C // tc),
            in_specs=[pl.BlockSpec((1, tm, tc), lambda i, j, me_ref: (me_ref[0], i, j)),
                      pl.BlockSpec((3, tm, tc), lambda i, j, me_ref: (0, i, j))],
            out_specs=pl.BlockSpec((tm, tc), lambda i, j, me_ref: (i, j))),
        compiler_params=_params(("parallel", "parallel")),
    )(chip, pre, recv)


def kernel(x, c, positions, w_ada, b_ada, w_in, g_q_a, w_q_b, g_kv_a, w_kv_b, w_o_a, w_conv, w_o_b, w_o, ln1_g, ln1_b, w_ffn_in, w_ffn_out, ln2_g, ln2_b, loss_target, m_w_ada, m_b_ada, m_w_in, m_g_q_a, m_w_q_b, m_g_kv_a, m_w_kv_b, m_w_o_a, m_w_conv, m_w_o_b, m_w_o, m_ln1_g, m_ln1_b, m_w_ffn_in, m_w_ffn_out, m_ln2_g, m_ln2_b, v_w_ada, v_b_ada, v_w_in, v_g_q_a, v_w_q_b, v_g_kv_a, v_w_kv_b, v_w_o_a, v_w_conv, v_w_o_b, v_w_o, v_ln1_g, v_ln1_b, v_w_ffn_in, v_w_ffn_out, v_ln2_g, v_ln2_b):
    S, D = x.shape[1], x.shape[2]
    F = w_ffn_out.shape[1] * 4
    ax, ay, ac = _place()
    chip = 2 * ax + ay
    dev = 4 * ax + 2 * ay + ac
    x2, tgt = x[0], loss_target[0]
    w_ada2, w_in2, w_q_b2, w_kv_b2 = w_ada[0], w_in[0], w_q_b[0], w_kv_b[0]
    w_o_a2, w_o_b2, w_o2, w_ffn_in2, w_ffn_out2 = w_o_a[0], w_o_b[0], w_o[0], w_ffn_in[0], w_ffn_out[0]
    NA = w_ada2.shape[1]
    CW = w_conv.shape[2]

    inv_freq = 1.0 / (ROPE_THETA ** (jnp.arange(0, QK_ROPE, 2, dtype=F32) / QK_ROPE))
    ang = positions[0].astype(F32)[:, None] * inv_freq
    cos, sin = jnp.cos(ang), jnp.sin(ang)
    z32, z64, z96 = jnp.zeros((S, 32), F32), jnp.zeros((S, 64), F32), jnp.zeros((S, 96), F32)
    tab = jnp.concatenate([cos, cos, z64, -sin, z96, z32, sin, z64], axis=1)

    def halves(a):
        return a.reshape(2, a.shape[0] // 2, a.shape[1])

    def whole(g):
        return g.reshape(4, 2 * g.shape[2], g.shape[3])

    def cols(g):
        return jnp.transpose(g, (1, 0, 2)).reshape(g.shape[1], 4 * g.shape[2])

    w_inT, m_w_inT, v_w_inT = w_in2.T, m_w_in[0].T, v_w_in[0].T
    CS = w_inT.shape[0]
    CSP = -(-CS // 32) * 32
    sh_in = halves(jnp.pad(w_inT.astype(BF16), ((0, CSP - CS), (0, 0))))
    sh_qb, sh_kvb, sh_oa, sh_ob, sh_o, sh_fi, sh_fo = (
        halves(w.astype(BF16)) for w in (w_q_b2, w_kv_b2, w_o_a2, w_o_b2, w_o2, w_ffn_in2, w_ffn_out2))
    c_all = _all_gather8(c, "gather_c").reshape(8, D)
    wconv_all = _all_gather8(w_conv[0], "gather_wconv")
    w_conv_full = jnp.transpose(wconv_all[0::2], (1, 0, 2)).reshape(3, D)
    b_sh = lax.dynamic_slice(b_ada, (0, chip * NA), (1, NA))
    mod_sh = _ada_fwd(c_all, w_ada2, b_sh)
    mod_all = _all_gather8(mod_sh, "gather_mod")
    mod = lax.dynamic_slice(mod_all[0::2], (0, dev, 0), (4, 1, NA)).reshape(6, D)
    shift1, scale1, gate1, shift2, scale2, gate2 = (mod[k:k + 1] for k in range(6))

    g_in, shift1 = _run_plan(_gather_plan([sh_in]), "gather_first", ride=shift1)
    g_in = whole(g_in)
    sh_a1, sh_a2 = [sh_qb, sh_kvb], [sh_oa, sh_ob, sh_o]
    cl_a1, cl_a2, cl_fi, cl_fo = (_gather_copies(g) for g in (sh_a1, sh_a2, [sh_fi], [sh_fo]))
    st_a1 = _split_start(sh_a1, cl_a1, shift1, "gather_a1_start")
    st_a2 = _split_start(sh_a2, cl_a2, st_a1[4], "gather_a2_start")
    shift1 = st_a2[4][0]

    def in_rows(lo, hi):
        parts = [g_in[p, max(lo, p * CS) - p * CS:min(hi, (p + 1) * CS) - p * CS]
                 for p in range(4) if max(lo, p * CS) < min(hi, (p + 1) * CS)]
        return parts[0] if len(parts) == 1 else jnp.concatenate(parts, axis=0)

    n_qkv = Q_LORA + KV_LORA + QK_ROPE
    W_qkvT = jnp.pad(in_rows(0, n_qkv), ((0, QKV_A - n_qkv), (0, 0)))
    W_convT = in_rows(n_qkv, n_qkv + 3 * D)
    W_gateT = in_rows(n_qkv + 3 * D, n_qkv + 5 * D)

    u = _modulate(x2, scale1, shift1, "modulate1")
    pq = _matmul(u, W_qkvT, "nt", F32, "proj_qkv")
    pc = _matmul(u, W_convT, "nt", F32, "proj_conv")
    sh_a1, la1 = _split_wait(st_a1, cl_a1, [pc], "gather_a1_wait")
    pg, (g_qb, g_kvb) = _matmul(u, W_gateT, "nt", BF16, "proj_gate", carry=_gather_plan(sh_a1, into=la1, ici=False))
    st_fi = _split_start([sh_fi], cl_fi, g_q_a, "gather_fi_start", after=[pg])
    W_qb = jnp.pad(cols(whole(g_qb)).reshape(Q_LORA, N_HEADS, QK_NOPE + QK_ROPE),
                   ((0, 0), (0, 0), (0, QK_PAD - QK_NOPE - QK_ROPE))).reshape(Q_LORA, N_HEADS * QK_PAD)
    W_kvb = cols(whole(g_kvb))
    rq, rkv, kr = _rms_fwd(pq, tab, st_fi[4][0], g_kv_a)
    q = _q_rope(_matmul(rq, W_qb, "nn", F32, "q_b"), tab)
    kv = _matmul(rkv, W_kvb, "nn", BF16, "kv_b")
    sh_a2, la2 = _split_wait(st_a2, cl_a2, [kv], "gather_a2_wait")
    o, lse, (g_oa, g_ob, g_o) = _attn_fwd(q, kv, kr, carry=_gather_plan(sh_a2, into=la2, ici=False))
    W_oa, W_ob, W_o = (g.reshape(-1, D) for g in (g_oa, g_ob, g_o))
    y_a = _matmul(o, W_oa, "nn", F32, "o_a")
    hb = _conv_fwd(pc, w_conv_full)
    y_b = _matmul(hb, W_ob, "nn", F32, "o_b")
    st_fo = _split_start([sh_fo], cl_fo, ln1_g, "gather_fo_start", after=[y_b])
    merged = _merge_fwd(y_a, y_b, pg)
    sh_fi_t, lfi = _split_wait(st_fi, cl_fi, [merged], "gather_fi_wait")
    mix, (g_fi,) = _matmul(merged, W_o, "nn", F32, "w_o", carry=_gather_plan(sh_fi_t, into=lfi, ici=False))
    W_fi = whole(g_fi)
    x1, u2 = _ln1_fwd(x2, mix, gate1, st_fo[4][0], ln1_b, scale2, shift2)
    hh = _matmul(u2, W_fi, "nn", BF16, "ffn_in", shards="b")
    sh_fo_t, lfo = _split_wait(st_fo, cl_fo, [hh], "gather_fo_wait")
    W_fo = _run_plan(_gather_plan(sh_fo_t, into=lfo, ici=False), "forward_fo")[0].reshape(F, D)
    act = _swiglu_fwd(hh)
    ffn = _matmul(act, W_fo, "nn", F32, "ffn_out")

    core_i = ac.astype(jnp.int32).reshape(1)
    chip_i = chip.astype(jnp.int32).reshape(1)

    def uncols(g):
        return jnp.transpose(g.reshape(g.shape[0], 4, g.shape[1] // 4), (1, 0, 2))

    def slabs(p):
        return p.reshape(4, 2, p.shape[1] // 2, p.shape[2])

    def add_pairs(parts, sibs, nms):
        return [_add_pair(a, b, core_i, "add_pair_" + nm) for a, b, nm in zip(parts, sibs, nms)]

    def sum_all(pre, recv, nms):
        return [_sum_slabs(a, r, chip_i, "sum_slabs_" + nm) for a, r, nm in zip(pre, recv, nms)]

    dffn, dx1a, loss_acc, d_ln2_g, d_ln2_b, d_gate2 = _ln2_loss_bwd(x1, ffn, gate2, ln2_g, ln2_b, tgt)
    loss = lax.psum(loss_acc[0, 0], ("x", "y", "c"))
    dW_fo = _matmul(act, dffn, "tn", BF16, "d_w_ffn_out")
    p_fo = [slabs(dW_fo.reshape(4, -1, D))]
    dact, s_fo = _matmul(dffn, W_fo, "nt", BF16, "d_act", carry=_pair_plan(p_fo))
    pre_fo = add_pairs(p_fo, s_fo, ["w_ffn_out"])
    cs_fo = _scatter_copies(pre_fo)
    st_sfo = _split_start(pre_fo, cs_fo, scale2, "scatter_fo_start")
    dhh = _swiglu_bwd(dact, hh)
    dW_fi = _matmul(u2, dhh, "tn", BF16, "d_w_ffn_in", shards="o")
    p_fi = [slabs(dW_fi)]
    du2, s_fi = _matmul(dhh, W_fi, "nt", F32, "d_u2", carry=_pair_plan(p_fi), shards="b")
    pre_fi = add_pairs(p_fi, s_fi, ["w_ffn_in"])
    cs_fi = _scatter_copies(pre_fi)
    st_sfi = _split_start(pre_fi, cs_fi, st_sfo[4], "scatter_fi_start")
    dmix, dxa, d_shift2, d_scale2, d_ln1_g, d_ln1_b, d_gate1 = _ln1_bwd(x2, mix, dx1a, du2, gate1, ln1_g, ln1_b, st_sfi[4][0])
    dW_o = _matmul(merged, dmix, "tn", BF16, "d_w_o")
    dmerged = _matmul(dmix, W_o, "nt", F32, "d_merged")
    dy_a, dy_b, dgate = _merge_bwd(dmerged, y_a, y_b, pg)
    dW_oa = _matmul(o, dy_a, "tn", BF16, "d_w_o_a")
    do = _matmul(dy_a, W_oa, "nt", BF16, "d_o")
    dW_ob = _matmul(hb, dy_b, "tn", BF16, "d_w_o_b")
    p_mid = [slabs(g.reshape(4, -1, D)) for g in (dW_oa, dW_ob, dW_o)]
    dhb, s_mid = _matmul(dy_b, W_ob, "nt", F32, "d_hb", carry=_pair_plan(p_mid))
    pre_mid = add_pairs(p_mid, s_mid, ["w_o_a", "w_o_b", "w_o"])
    cs_mid = _scatter_copies(pre_mid)
    st_smid = _split_start(pre_mid, cs_mid, w_conv_full, "scatter_mid_start")
    dconv, d_wconv = _conv_bwd(dhb, pc, st_smid[4][0])
    dq, dkv, dkr, _ = _attn_bwd(q, kv, kr, do, o, lse, tab, carry=_token_plan(st_smid[4][0]))
    names_a = ["w_ffn_out", "w_ffn_in", "w_o_a", "w_o_b", "w_o"]
    dW_qb = _matmul(rq, dq, "tn", BF16, "d_w_q_b")
    d_rq = _matmul(dq, W_qb, "nt", F32, "d_rq")
    dW_kvb = _matmul(rkv, dkv, "tn", BF16, "d_w_kv_b")
    d_rkv = _matmul(dkv, W_kvb, "nt", F32, "d_rkv")
    dqkv, d_g_q, d_g_kv = _rms_bwd(d_rq, d_rkv, pq, dkr, g_q_a, g_kv_a)
    dW_qkvT = _matmul(dqkv, u, "tn", BF16, "d_w_qkv")
    dW_convT = _matmul(dconv, u, "tn", BF16, "d_w_conv")
    dW_gateT = _matmul(dgate, u, "tn", BF16, "d_w_gate")
    pre_fo, r_fo = _split_wait(st_sfo, cs_fo, [dW_qkvT], "scatter_fo_wait")
    pre_fi, r_fi = _split_wait(st_sfi, cs_fi, [dW_qkvT], "scatter_fi_wait")
    pre_mid, r_mid = _split_wait(st_smid, cs_mid, [dW_qkvT], "scatter_mid_wait")
    fin_a = sum_all(pre_fo + pre_fi + pre_mid, r_fo + r_fi + r_mid, names_a)
    dW_inT = jnp.concatenate([dW_qkvT[:n_qkv], dW_convT, dW_gateT], axis=0).reshape(4, CS, D)
    dW_inT = jnp.pad(dW_inT, ((0, 0), (0, CSP - CS), (0, 0)))
    dW_qb_u = dW_qb.reshape(Q_LORA, N_HEADS, QK_PAD)[:, :, :QK_NOPE + QK_ROPE].reshape(Q_LORA, -1)
    names_b = ["w_in", "w_q_b", "w_kv_b"]
    p_b = [slabs(dW_inT), slabs(uncols(dW_qb_u)), slabs(uncols(dW_kvb))]
    du, s_b = _matmul(dqkv, W_qkvT, "nn", F32, "d_u_qkv", carry=_pair_plan(p_b))
    pre_b = add_pairs(p_b, s_b, names_b)
    cs_b = _scatter_copies(pre_b)
    st_b = _split_start(pre_b, cs_b, scale1, "scatter_last_start")
    du, fs_a = _matmul(dconv, W_convT, "nn", F32, "d_u_conv", add=du, carry=_sibling_plan(fin_a))
    du = _matmul(dgate, W_gateT, "nn", F32, "d_u_gate", add=du)
    grad_x, d_shift1, d_scale1 = _dx_final(dxa, du, x2, st_b[4][0])

    big = {}
    ws = dict(w_in=(w_inT, m_w_inT, v_w_inT), w_q_b=(w_q_b2, m_w_q_b[0], v_w_q_b[0]),
              w_kv_b=(w_kv_b2, m_w_kv_b[0], v_w_kv_b[0]), w_o_a=(w_o_a2, m_w_o_a[0], v_w_o_a[0]),
              w_o_b=(w_o_b2, m_w_o_b[0], v_w_o_b[0]), w_o=(w_o2, m_w_o[0], v_w_o[0]),
              w_ffn_in=(w_ffn_in2, m_w_ffn_in[0], v_w_ffn_in[0]), w_ffn_out=(w_ffn_out2, m_w_ffn_out[0], v_w_ffn_out[0]))

    def adam_of(nm, a, b, carry=None):
        w_, m_, v_ = ws[nm]
        return _adam_halves("adam_" + nm, w_, m_, v_, a, b, core_i, carry)

    for nm, a, b in zip(names_a, fin_a, fs_a):
        big[nm] = adam_of(nm, a, b, _token_plan(st_b[4][0]))[0]
    done = [big[nm][1] for nm in names_a] + [grad_x]
    pre_b, r_b = _split_wait(st_b, cs_b, done, "scatter_last_wait")
    fin_b = sum_all(pre_b, r_b, names_b)
    fs_b = _run_plan(_sibling_plan(fin_b), "sibling_last")
    for nm, a, b in zip(names_b, fin_b, fs_b):
        big[nm] = adam_of(nm, a, b)

    def pad_d(v):
        return jnp.pad(v, ((0, 0), (0, D - v.shape[1])))

    small = _pack_rows([d_ln1_g, d_ln1_b, d_ln2_g, d_ln2_b, pad_d(d_g_q), pad_d(d_g_kv), d_wconv,
                         d_shift1, d_scale1, d_gate1, d_shift2, d_scale2, d_gate2], 16, after=[pre_b[1]])
    small_all = _all_gather8(small, "gather_small")
    small_sum = _sum8(small_all)
    g_ln1_g, g_ln1_b, g_ln2_g, g_ln2_b = (small_sum[k:k + 1] for k in range(4))
    g_g_q, g_g_kv = small_sum[4:5, :Q_LORA], small_sum[5:6, :KV_LORA]
    g_wconv = lax.dynamic_slice(small_sum[6:9], (0, chip * CW), (3, CW))
    g_b_ada = small_sum[9:15].reshape(1, 6 * D)
    dmod_all = small_all[:, 9:15, :].reshape(8, 6 * D)
    g_w_ada = _ada_bwd(c_all, lax.dynamic_slice(dmod_all, (0, chip * NA), (8, NA)))
    big["w_ada"] = [g_w_ada] + list(_adam("adam_w_ada", w_ada2, m_w_ada[0], v_w_ada[0], g_w_ada))
    sm = {}
    for nm, w_, m_, v_, g_ in [("b_ada", b_ada, m_b_ada, v_b_ada, g_b_ada), ("g_q_a", g_q_a, m_g_q_a, v_g_q_a, g_g_q),
                               ("g_kv_a", g_kv_a, m_g_kv_a, v_g_kv_a, g_g_kv),
                               ("w_conv", w_conv[0], m_w_conv[0], v_w_conv[0], g_wconv),
                               ("ln1_g", ln1_g, m_ln1_g, v_ln1_g, g_ln1_g), ("ln1_b", ln1_b, m_ln1_b, v_ln1_b, g_ln1_b),
                               ("ln2_g", ln2_g, m_ln2_g, v_ln2_g, g_ln2_g), ("ln2_b", ln2_b, m_ln2_b, v_ln2_b, g_ln2_b)]:
        sm[nm] = (g_,) + tuple(_adam_small("adam_" + nm, w_, m_, v_, g_))

    order = ["w_ada", "b_ada", "w_in", "g_q_a", "w_q_b", "g_kv_a", "w_kv_b", "w_o_a", "w_conv", "w_o_b", "w_o",
             "ln1_g", "ln1_b", "w_ffn_in", "w_ffn_out", "ln2_g", "ln2_b"]
    lead = {"b_ada", "g_q_a", "g_kv_a", "ln1_g", "ln1_b", "ln2_g", "ln2_b"}

    def leaf(nm, k):
        val = big[nm][k] if nm in big else sm[nm][k]
        if nm == "w_in":
            val = val.T
        return val if nm in lead else val[None]

    outs = [loss, grad_x[None]]
    for k in range(4):
        outs += [leaf(nm, k) for nm in order]
    return tuple(outs)
```

```python
import functools

import jax
import jax.numpy as jnp
from jax import lax
from jax.experimental import pallas as pl
from jax.experimental.pallas import tpu as pltpu

F32, BF16 = jnp.float32, jnp.bfloat16
N_HEADS, QK_NOPE, QK_ROPE, V_HEAD = 16, 128, 64, 128
Q_LORA, KV_LORA = 512, 512
QK_PAD = 256
QKV_A = 1152
CHUNK_SHIFT = 6
ATTN_SCALE = (QK_NOPE + QK_ROPE) ** -0.5
ROPE_THETA = 10000.0
ALPHA = 2.0 ** 0.25
LN_EPS, RMS_EPS = 1e-5, 1e-6
ADAM_LR, ADAM_B1, ADAM_B2, ADAM_EPS, ADAM_WD, ADAM_STEP = 0.001, 0.9, 0.999, 1e-08, 0.01, 10
ADAM_C1 = 1.0 - ADAM_B1 ** ADAM_STEP
ADAM_C2 = 1.0 - ADAM_B2 ** ADAM_STEP
VMEM_LIMIT = 56 * 1024 * 1024
MESH = pl.DeviceIdType.MESH
ANY = pl.BlockSpec(memory_space=pl.ANY)
HBM_SPEC = pl.BlockSpec(memory_space=pltpu.HBM)
SEM_SPEC = pl.BlockSpec(memory_space=pltpu.SEMAPHORE)
NT = (((1,), (1,)), ((), ()))
TN = (((0,), (0,)), ((), ()))
NN = (((1,), (0,)), ((), ()))


def _params(sem=None):
    return pltpu.CompilerParams(dimension_semantics=sem, vmem_limit_bytes=VMEM_LIMIT)


def _pick(n, cands=(1408, 1024, 512, 384, 256, 128)):
    for t in cands:
        if n % t == 0:
            return t
    return n


def _row_tile(rows, row_bytes, budget, mult=8):
    best = mult
    for t in range(mult, rows + 1, mult):
        if rows % t == 0 and t * row_bytes <= budget:
            best = t
    return best


def _tile2(rows, cols, mult=8, budget=3 << 18):
    col_tiles = [t for t in range(128, cols + 1, 128) if cols % t == 0] or [cols]
    best = None
    for tc in col_tiles:
        for tr in range(mult, rows + 1, mult):
            if rows % tr == 0 and tr * tc <= budget and (best is None or (tr * tc, tc) > (best[0] * best[1], best[1])):
                best = (tr, tc)
    assert best is not None, (rows, cols)
    return best


def _sigmoid(x):
    return jax.nn.sigmoid(x)


class _Plan:
    def __init__(self, ins, outs, sems, start, finish, aliases=None):
        self.ins, self.outs, self.sems, self.start, self.finish = list(ins), list(outs), list(sems), start, finish
        self.aliases = dict(aliases or {})

    def io_aliases(self, first_in, first_out):
        return {first_in + i: first_out + o for i, o in self.aliases.items()}


def _token_plan(token):
    return _Plan([token], [], [], lambda *a: None, lambda *a: None)


def _run_plan(plan, name, ride=None):
    n_in, n_out = len(plan.ins), len(plan.outs)
    extra = [] if ride is None else [ride]
    aliases = plan.io_aliases(0, 0)
    if extra:
        aliases[n_in] = n_out

    def body(*refs):
        ins, outs, sems = refs[:n_in], refs[n_in + len(extra):n_in + len(extra) + n_out], refs[n_in + 2 * len(extra) + n_out:]
        plan.start(ins, outs, sems)
        plan.finish(ins, outs, sems)

    return pl.pallas_call(body, name=name, out_shape=plan.outs + [jax.ShapeDtypeStruct(r.shape, r.dtype) for r in extra],
                          in_specs=[ANY] * (n_in + len(extra)), out_specs=[ANY] * (n_out + len(extra)),
                          scratch_shapes=plan.sems, input_output_aliases=aliases,
                          compiler_params=_params())(*plan.ins, *extra)


def _matmul(a, b, mode, out_dtype, name, add=None, carry=None, shards=None):
    if mode == "nn":
        (M, K), N, dims = a.shape, b.shape[-1] * (4 if shards else 1), NN
    elif mode == "nt":
        (M, K), N, dims = a.shape, b.shape[-2], NT
    else:
        (K, M), N, dims = a.shape, b.shape[1], TN
    split_n = shards and mode != "nt"
    tm = _pick(M)
    tn = _pick(N // 4) if split_n else _pick(N)
    if shards and mode == "nt":
        tk = _pick(K // 4)
    else:
        tk = K if K <= 2048 else _pick(K)
    nk = K // tk
    per = (N // 4 // tn) if split_n else (K // 4 // tk if shards else 1)
    a_spec = (pl.BlockSpec((tk, tm), lambda i, j, k: (k, i)) if mode == "tn"
              else pl.BlockSpec((tm, tk), lambda i, j, k: (i, k)))
    if shards == "b" and mode == "nn":
        b_spec = pl.BlockSpec((None, tk, tn), lambda i, j, k: (j // per, k, j % per))
    elif shards == "b":
        b_spec = pl.BlockSpec((None, tn, tk), lambda i, j, k: (k // per, j, k % per))
    else:
        b_spec = (pl.BlockSpec((tn, tk), lambda i, j, k: (j, k)) if mode == "nt"
                  else pl.BlockSpec((tk, tn), lambda i, j, k: (k, j)))
    o_spec = pl.BlockSpec((tm, tn), lambda i, j, k: (i, j))
    o_shape = (M, N)
    if shards == "o":
        o_spec, o_shape = pl.BlockSpec((None, tm, tn), lambda i, j, k: (j // per, i, j % per)), (4, M, N // 4)
    has_add = add is not None
    n_ci = len(carry.ins) if carry else 0
    n_co = len(carry.outs) if carry else 0
    n_in = 2 + has_add
    grid = (M // tm, N // tn, nk)

    def body(*refs):
        a_ref, b_ref = refs[0], refs[1]
        add_ref = refs[2] if has_add else None
        o_ref = refs[n_in + n_ci]
        acc_ref = refs[n_in + n_ci + 1 + n_co] if nk > 1 else None
        c_ins = refs[n_in:n_in + n_ci]
        c_outs = refs[n_in + n_ci + 1:n_in + n_ci + 1 + n_co]
        c_sems = refs[n_in + n_ci + 1 + n_co + (nk > 1):]
        i, j, k = pl.program_id(0), pl.program_id(1), pl.program_id(2)

        if carry:
            @pl.when((i == 0) & (j == 0) & (k == 0))
            def _():
                carry.start(c_ins, c_outs, c_sems)

        part = lax.dot_general(a_ref[...], b_ref[...], dims, preferred_element_type=F32)
        if nk == 1:
            o_ref[...] = (part + add_ref[...] if has_add else part).astype(o_ref.dtype)
        else:
            @pl.when(k == 0)
            def _():
                acc_ref[...] = part

            @pl.when((k > 0) & (k < nk - 1))
            def _():
                acc_ref[...] += part

            @pl.when(k == nk - 1)
            def _():
                r = acc_ref[...] + part
                if has_add:
                    r = r + add_ref[...]
                o_ref[...] = r.astype(o_ref.dtype)

        if carry:
            @pl.when((i == grid[0] - 1) & (j == grid[1] - 1) & (k == nk - 1))
            def _():
                carry.finish(c_ins, c_outs, c_sems)

    ins = [a, b] + ([add] if has_add else []) + (carry.ins if carry else [])
    in_specs = [a_spec, b_spec] + ([o_spec] if has_add else []) + [ANY] * n_ci
    res = pl.pallas_call(
        body, name=name, grid=grid,
        in_specs=in_specs, out_specs=[o_spec] + [ANY] * n_co,
        out_shape=[jax.ShapeDtypeStruct(o_shape, out_dtype)] + (carry.outs if carry else []),
        scratch_shapes=([pltpu.VMEM((tm, tn), F32)] if nk > 1 else []) + (carry.sems if carry else []),
        input_output_aliases=carry.io_aliases(n_in, 1) if carry else {},
        compiler_params=_params(("arbitrary",) * 3 if carry else ("parallel", "parallel", "arbitrary")),
    )(*ins)
    return (res[0], res[1:]) if carry else res[0]


def _rows(body, name, n_rows, tm, ins, outs, accs=(), carry=None):
    grid = (n_rows // tm,)
    per8 = tm // 8
    last8 = n_rows // 8 - 1
    arrays, in_specs = [], []
    for spec in ins:
        kind, arr = spec[0], spec[1]
        arrays.append(arr)
        if kind == "row":
            _, _, cb, w = spec
            in_specs.append(pl.BlockSpec((tm, w), lambda i, cb=cb: (i, cb)))
        elif kind == "full":
            in_specs.append(pl.BlockSpec(arr.shape, lambda i, nd=arr.ndim: (0,) * nd))
        elif kind == "prev":
            _, _, cb, w = spec
            in_specs.append(pl.BlockSpec((8, w), lambda i, cb=cb: (jnp.maximum(i * per8 - 1, 0), cb)))
        else:
            _, _, cb, w = spec
            in_specs.append(pl.BlockSpec((8, w), lambda i, cb=cb: (jnp.minimum((i + 1) * per8, last8), cb)))
    out_shape = [jax.ShapeDtypeStruct((n_rows, w), dt) for (w, dt) in outs]
    out_specs = [pl.BlockSpec((tm, w), lambda i: (i, 0)) for (w, _) in outs]
    out_shape += [jax.ShapeDtypeStruct(s, F32) for s in accs]
    out_specs += [pl.BlockSpec(s, lambda i, nd=len(s): (0,) * nd) for s in accs]
    n_in, n_out, n_acc = len(ins), len(outs), len(accs)
    n_ci = len(carry.ins) if carry else 0
    n_co = len(carry.outs) if carry else 0

    def kernel_body(*refs):
        first = n_in + n_ci
        c_ins, c_outs, c_sems = refs[n_in:first], refs[first + n_out + n_acc:first + n_out + n_acc + n_co], refs[first + n_out + n_acc + n_co:]
        if carry:
            @pl.when(pl.program_id(0) == 0)
            def _():
                carry.start(c_ins, c_outs, c_sems)

        body(pl.program_id(0), refs[:n_in], refs[first:first + n_out], refs[first + n_out:first + n_out + n_acc])
        if carry:
            @pl.when(pl.program_id(0) == grid[0] - 1)
            def _():
                carry.finish(c_ins, c_outs, c_sems)

    res = pl.pallas_call(
        kernel_body, name=name, grid=grid, in_specs=in_specs + [ANY] * n_ci, out_specs=out_specs + [ANY] * n_co,
        out_shape=out_shape + (carry.outs if carry else []), scratch_shapes=carry.sems if carry else [],
        input_output_aliases=carry.io_aliases(n_in, n_out + n_acc) if carry else {},
        compiler_params=_params(("arbitrary",)),
    )(*arrays, *(carry.ins if carry else []))
    return (res[:n_out + n_acc], res[n_out + n_acc:]) if carry else res


def _acc_add(i, ref, val):
    @pl.when(i == 0)
    def _():
        ref[...] = val

    @pl.when(i > 0)
    def _():
        ref[...] += val


def _rope(t, tab, sign):
    c, sa, sb = tab[:, 0:128], tab[:, 128:256], tab[:, 256:384]
    rot = pltpu.roll(t, 96, 1) * sa + pltpu.roll(t, 32, 1) * sb
    return t * c + rot if sign > 0 else t * c - rot


def _ln_stats(r):
    mu = jnp.mean(r, axis=-1, keepdims=True)
    d = r - mu
    var = jnp.mean(d * d, axis=-1, keepdims=True)
    rstd = lax.rsqrt(var + LN_EPS)
    return d * rstd, rstd


def _ln_bwd(dxh, xh, rstd):
    m1 = jnp.mean(dxh, axis=-1, keepdims=True)
    m2 = jnp.mean(dxh * xh, axis=-1, keepdims=True)
    return rstd * (dxh - m1 - xh * m2)


def _modulate(x, scale, shift, name):
    S, D = x.shape

    def body(i, ins, outs, accs):
        outs[0][...] = (ins[0][...] * (1.0 + ins[1][...]) + ins[2][...]).astype(BF16)

    return _rows(body, name, S, _pick(S, (256, 128)), [("row", x, 0, D), ("full", scale), ("full", shift)], [(D, BF16)])[0]


def _rms_fwd(pq, tab, g_q, g_kv):
    S = pq.shape[0]

    def body(i, ins, outs, accs):
        pq_ref, tab_ref, gq_ref, gkv_ref = ins

        def rms(x, g):
            return x * lax.rsqrt(jnp.mean(x * x, axis=-1, keepdims=True) + RMS_EPS) * g

        outs[0][...] = rms(pq_ref[:, 0:Q_LORA], gq_ref[...]).astype(BF16)
        outs[1][...] = rms(pq_ref[:, Q_LORA:Q_LORA + KV_LORA], gkv_ref[...]).astype(BF16)
        outs[2][...] = _rope(pq_ref[:, Q_LORA + KV_LORA:QKV_A], tab_ref[...], 1).astype(BF16)

    return _rows(body, "rms_fwd", S, _pick(S, (256, 128)),
                 [("row", pq, 0, QKV_A), ("row", tab, 0, 384), ("full", g_q), ("full", g_kv)],
                 [(Q_LORA, BF16), (KV_LORA, BF16), (128, BF16)])


def _q_rope(q, tab):
    S, W = q.shape

    def body(i, ins, outs, accs):
        q_ref, tab_ref = ins
        t = tab_ref[...]
        for h in range(N_HEADS):
            lo = h * QK_PAD
            outs[0][:, lo:lo + 128] = q_ref[:, lo:lo + 128].astype(BF16)
            outs[0][:, lo + 128:lo + 256] = _rope(q_ref[:, lo + 128:lo + 256], t, 1).astype(BF16)

    return _rows(body, "q_rope", S, _pick(S, (256, 128)), [("row", q, 0, W), ("row", tab, 0, 384)], [(W, BF16)])[0]


def _allowed(q0, k0, bq):
    row = q0 + lax.broadcasted_iota(jnp.int32, (bq, bq), 0)
    col = k0 + lax.broadcasted_iota(jnp.int32, (bq, bq), 1)
    return (col >> CHUNK_SHIFT) <= (row >> CHUNK_SHIFT)


ATTN_BLOCK = 512


def _attn_fwd(q, kv, kr, carry=None):
    S = q.shape[0]
    bq = min(ATTN_BLOCK, S)
    nq = S // bq
    n_ci = len(carry.ins) if carry else 0
    n_co = len(carry.outs) if carry else 0

    def body(*refs):
        q_ref, kn_ref, v_ref, kr_ref = refs[:4]
        o_ref, lse_ref = refs[4 + n_ci:6 + n_ci]
        c_ins, c_outs = refs[4:4 + n_ci], refs[6 + n_ci:6 + n_ci + n_co]
        kcat = refs[6 + n_ci + n_co]
        c_sems = refs[7 + n_ci + n_co:]
        qi = pl.program_id(1)
        if carry:
            @pl.when((pl.program_id(0) == 0) & (qi == 0))
            def _():
                carry.start(c_ins, c_outs, c_sems)

        @pl.when(qi == 0)
        def _():
            kcat[:, 0:128] = kn_ref[...]
            kcat[:, 128:256] = kr_ref[...]

        qv = q_ref[...]

        def step(j, carry, masked):
            m, l, acc = carry
            off = pl.multiple_of(j * bq, bq)
            s = lax.dot_general(qv, kcat[pl.ds(off, bq), :], NT, preferred_element_type=F32) * ATTN_SCALE
            if masked:
                s = jnp.where(_allowed(qi * bq, off, bq), s, -1e30)
            m_new = jnp.maximum(m, jnp.max(s, axis=1, keepdims=True))
            a = jnp.exp(m - m_new)
            p = jnp.exp(s - m_new)
            l = a * l + jnp.sum(p, axis=1, keepdims=True)
            acc = a * acc + jnp.dot(p.astype(BF16), v_ref[pl.ds(off, bq), :], preferred_element_type=F32)
            return m_new, l, acc

        init = (jnp.full((bq, 1), -1e30, F32), jnp.zeros((bq, 1), F32), jnp.zeros((bq, V_HEAD), F32))
        below = lax.fori_loop(0, qi, lambda j, cr: step(j, cr, False), init)
        m, l, acc = step(qi, below, True)
        o_ref[...] = (acc / l).astype(BF16)
        lse_ref[0] = m + jnp.log(l)
        if carry:
            @pl.when((pl.program_id(0) == N_HEADS - 1) & (qi == nq - 1))
            def _():
                carry.finish(c_ins, c_outs, c_sems)

    res = pl.pallas_call(
        body, name="attn_fwd", grid=(N_HEADS, nq),
        in_specs=[pl.BlockSpec((bq, QK_PAD), lambda h, i: (i, h)),
                  pl.BlockSpec((S, 128), lambda h, i: (0, 2 * h)),
                  pl.BlockSpec((S, 128), lambda h, i: (0, 2 * h + 1)),
                  pl.BlockSpec((S, 128), lambda h, i: (0, 0))] + [ANY] * n_ci,
        out_specs=[pl.BlockSpec((bq, V_HEAD), lambda h, i: (i, h)),
                   pl.BlockSpec((1, bq, 1), lambda h, i: (h, i, 0))] + [ANY] * n_co,
        out_shape=[jax.ShapeDtypeStruct((S, N_HEADS * V_HEAD), BF16),
                   jax.ShapeDtypeStruct((N_HEADS, S, 1), F32)] + (carry.outs if carry else []),
        scratch_shapes=[pltpu.VMEM((S, QK_PAD), BF16)] + (carry.sems if carry else []),
        input_output_aliases=carry.io_aliases(4, 2) if carry else {},
        compiler_params=_params(("arbitrary", "arbitrary")),
    )(q, kv, kv, kr, *(carry.ins if carry else []))
    return res[0], res[1], res[2:]


def _attn_bwd(q, kv, kr, do, o, lse, tab, carry=None):
    S = q.shape[0]
    bq = min(ATTN_BLOCK, S)
    nq = S // bq

    n_ci = len(carry.ins) if carry else 0
    n_co = len(carry.outs) if carry else 0

    def body(*refs):
        q_ref, kn_ref, v_ref, kr_ref, do_ref, o_ref, lse_ref, tab_ref = refs[:8]
        dq_ref, dkv_ref, dkr_ref = refs[8 + n_ci:11 + n_ci]
        dq_acc, dk_acc, dv_acc, kcat, delta = refs[11 + n_ci + n_co:16 + n_ci + n_co]
        c_ins, c_outs, c_sems = refs[8:8 + n_ci], refs[11 + n_ci:11 + n_ci + n_co], refs[16 + n_ci + n_co:]
        h = pl.program_id(0)
        if carry:
            @pl.when(h == 0)
            def _():
                carry.start(c_ins, c_outs, c_sems)

        dq_acc[...] = jnp.zeros_like(dq_acc)
        dk_acc[...] = jnp.zeros_like(dk_acc)
        dv_acc[...] = jnp.zeros_like(dv_acc)
        kcat[:, 0:128] = kn_ref[...]
        kcat[:, 128:256] = kr_ref[...]
        for r in range(nq):
            rows = slice(r * bq, (r + 1) * bq)
            delta[rows, :] = jnp.sum(do_ref[rows, :].astype(F32) * o_ref[rows, :].astype(F32), axis=1, keepdims=True)

        def pair(i, j, masked):
            rows_i = pl.ds(pl.multiple_of(i * bq, bq), bq)
            rows_j = pl.ds(pl.multiple_of(j * bq, bq), bq)
            qv, dov, k = q_ref[rows_i, :], do_ref[rows_i, :], kcat[rows_j, :]
            s = lax.dot_general(qv, k, NT, preferred_element_type=F32) * ATTN_SCALE
            if masked:
                s = jnp.where(_allowed(i * bq, j * bq, bq), s, -1e30)
            p = jnp.exp(s - lse_ref[0, rows_i, :])
            dv_acc[rows_j, :] += lax.dot_general(p.astype(BF16), dov, TN, preferred_element_type=F32)
            dp = lax.dot_general(dov, v_ref[rows_j, :], NT, preferred_element_type=F32)
            ds = (p * (dp - delta[rows_i, :]) * ATTN_SCALE).astype(BF16)
            dk_acc[rows_j, :] += lax.dot_general(ds, qv, TN, preferred_element_type=F32)
            dq_acc[rows_i, :] += jnp.dot(ds, k, preferred_element_type=F32)

        def kv_step(j, _):
            pair(j, j, True)

            def q_step(i, _):
                pair(i, j, False)
                return 0

            lax.fori_loop(j + 1, nq, q_step, 0)
            return 0

        lax.fori_loop(0, nq, kv_step, 0)

        for r in range(nq):
            rows = slice(r * bq, (r + 1) * bq)
            dq_ref[rows, 0:128] = dq_acc[rows, 0:128].astype(BF16)
            dq_ref[rows, 128:256] = _rope(dq_acc[rows, 128:256], tab_ref[rows, :], -1).astype(BF16)
        dkv_ref[:, 0:128] = dk_acc[:, 0:128].astype(BF16)
        dkv_ref[:, 128:256] = dv_acc[...].astype(BF16)

        @pl.when(h == 0)
        def _():
            dkr_ref[...] = dk_acc[:, 128:256]

        @pl.when(h > 0)
        def _():
            dkr_ref[...] += dk_acc[:, 128:256]

        @pl.when(h == N_HEADS - 1)
        def _():
            for r in range(nq):
                rows = slice(r * bq, (r + 1) * bq)
                dkr_ref[rows, :] = _rope(dkr_ref[rows, :], tab_ref[rows, :], -1)
            if carry:
                carry.finish(c_ins, c_outs, c_sems)

    W = N_HEADS * QK_PAD
    res = pl.pallas_call(
        body, name="attn_bwd", grid=(N_HEADS,),
        in_specs=[pl.BlockSpec((S, QK_PAD), lambda h: (0, h)),
                  pl.BlockSpec((S, 128), lambda h: (0, 2 * h)),
                  pl.BlockSpec((S, 128), lambda h: (0, 2 * h + 1)),
                  pl.BlockSpec((S, 128), lambda h: (0, 0)),
                  pl.BlockSpec((S, V_HEAD), lambda h: (0, h)),
                  pl.BlockSpec((S, V_HEAD), lambda h: (0, h)),
                  pl.BlockSpec((1, S, 1), lambda h: (h, 0, 0)),
                  pl.BlockSpec((S, 384), lambda h: (0, 0))] + [ANY] * n_ci,
        out_specs=[pl.BlockSpec((S, QK_PAD), lambda h: (0, h)),
                   pl.BlockSpec((S, QK_PAD), lambda h: (0, h)),
                   pl.BlockSpec((S, 128), lambda h: (0, 0))] + [ANY] * n_co,
        out_shape=[jax.ShapeDtypeStruct((S, W), BF16), jax.ShapeDtypeStruct((S, W), BF16),
                   jax.ShapeDtypeStruct((S, 128), F32)] + (carry.outs if carry else []),
        scratch_shapes=[pltpu.VMEM((S, QK_PAD), F32), pltpu.VMEM((S, QK_PAD), F32), pltpu.VMEM((S, V_HEAD), F32),
                        pltpu.VMEM((S, QK_PAD), BF16), pltpu.VMEM((S, 1), F32)]
        + (carry.sems if carry else []),
        input_output_aliases=carry.io_aliases(8, 3) if carry else {},
        compiler_params=_params(("arbitrary",)),
    )(q, kv, kv, kr, do, o, lse, tab, *(carry.ins if carry else []))
    return res[0], res[1], res[2], res[3:]


def _shift_down(cur, prev8, i, n):
    tm = cur.shape[0]
    prev8 = jnp.where(i == 0, jnp.zeros_like(prev8), prev8)
    full = jnp.concatenate([prev8, cur], axis=0)
    return pltpu.roll(full, n, 0)[8:8 + tm, :]


def _shift_up(cur, next8, i, last, n):
    tm = cur.shape[0]
    next8 = jnp.where(i == last, jnp.zeros_like(next8), next8)
    full = jnp.concatenate([cur, next8], axis=0)
    return pltpu.roll(full, tm + 8 - n, 0)[0:tm, :]


def _conv_fwd(pc, w_conv):
    S, D = pc.shape[0], pc.shape[1] // 3
    tm = _pick(S, (256, 128))

    def body(i, ins, outs, accs):
        b_ref, c_ref, x_ref, cp_ref, xp_ref, w_ref = ins
        z = c_ref[...] * x_ref[...]
        zp = cp_ref[...] * xp_ref[...]
        cz = w_ref[0:1, :] * _shift_down(z, zp, i, 2) + w_ref[1:2, :] * _shift_down(z, zp, i, 1) + w_ref[2:3, :] * z
        outs[0][...] = (b_ref[...] * cz).astype(BF16)

    return _rows(body, "conv_fwd", S, tm,
                 [("row", pc, 0, D), ("row", pc, 1, D), ("row", pc, 2, D), ("prev", pc, 1, D), ("prev", pc, 2, D),
                  ("full", w_conv)], [(D, BF16)])[0]


def _conv_bwd(dhb, pc, w_conv):
    S, D = dhb.shape
    tm = _pick(S, (256, 128))
    last = S // tm - 1

    def body(i, ins, outs, accs):
        g_ref, b_ref, c_ref, x_ref, cp_ref, xp_ref, gn_ref, bn_ref, w_ref = ins
        w0, w1, w2 = w_ref[0:1, :], w_ref[1:2, :], w_ref[2:3, :]
        c, x, g = c_ref[...], x_ref[...], g_ref[...]
        z = c * x
        zp = cp_ref[...] * xp_ref[...]
        z1, z2 = _shift_down(z, zp, i, 1), _shift_down(z, zp, i, 2)
        cz = w0 * z2 + w1 * z1 + w2 * z
        dcz = g * b_ref[...]
        dczn = gn_ref[...] * bn_ref[...]
        dz = w2 * dcz + w1 * _shift_up(dcz, dczn, i, last, 1) + w0 * _shift_up(dcz, dczn, i, last, 2)
        outs[0][:, 0:D] = (g * cz).astype(BF16)
        outs[0][:, D:2 * D] = (dz * x).astype(BF16)
        outs[0][:, 2 * D:3 * D] = (dz * c).astype(BF16)
        dw = jnp.concatenate([jnp.sum(dcz * z2, axis=0, keepdims=True), jnp.sum(dcz * z1, axis=0, keepdims=True),
                              jnp.sum(dcz * z, axis=0, keepdims=True)], axis=0)
        _acc_add(i, accs[0], dw)

    return _rows(body, "conv_bwd", S, tm,
                 [("row", dhb, 0, D), ("row", pc, 0, D), ("row", pc, 1, D), ("row", pc, 2, D),
                  ("prev", pc, 1, D), ("prev", pc, 2, D), ("next", dhb, 0, D), ("next", pc, 0, D), ("full", w_conv)],
                 [(3 * D, BF16)], [(3, D)])


def _merge_fwd(y_a, y_b, pg):
    S, D = y_a.shape

    def body(i, ins, outs, accs):
        ya, yb, ga, gb = ins
        outs[0][...] = (_sigmoid(ga[...].astype(F32)) * ya[...] + _sigmoid(gb[...].astype(F32)) * yb[...]).astype(BF16)

    return _rows(body, "merge_fwd", S, _pick(S, (256, 128)),
                 [("row", y_a, 0, D), ("row", y_b, 0, D), ("row", pg, 0, D), ("row", pg, 1, D)], [(D, BF16)])[0]


def _merge_bwd(dm, y_a, y_b, pg):
    S, D = dm.shape

    def body(i, ins, outs, accs):
        d, ya, yb = ins[0][...], ins[1][...], ins[2][...]
        sa, sb = _sigmoid(ins[3][...].astype(F32)), _sigmoid(ins[4][...].astype(F32))
        outs[0][...] = (d * sa).astype(BF16)
        outs[1][...] = (d * sb).astype(BF16)
        outs[2][:, 0:D] = (d * ya * (sa * (1.0 - sa))).astype(BF16)
        outs[2][:, D:2 * D] = (d * yb * (sb * (1.0 - sb))).astype(BF16)

    return _rows(body, "merge_bwd", S, _pick(S, (256, 128)),
                 [("row", dm, 0, D), ("row", y_a, 0, D), ("row", y_b, 0, D), ("row", pg, 0, D), ("row", pg, 1, D)],
                 [(D, BF16), (D, BF16), (2 * D, BF16)])


def _ln1_fwd(x, mix, gate1, g, b, scale2, shift2):
    S, D = x.shape

    def body(i, ins, outs, accs):
        x_ref, mix_ref, gate_ref, g_ref, b_ref, sc_ref, sh_ref = ins
        xh, _ = _ln_stats(ALPHA * x_ref[...] + gate_ref[...] * mix_ref[...])
        x1 = xh * g_ref[...] + b_ref[...]
        outs[0][...] = x1
        outs[1][...] = (x1 * (1.0 + sc_ref[...]) + sh_ref[...]).astype(BF16)

    return _rows(body, "ln1_fwd", S, _pick(S, (256, 128)),
                 [("row", x, 0, D), ("row", mix, 0, D), ("full", gate1), ("full", g), ("full", b),
                  ("full", scale2), ("full", shift2)], [(D, F32), (D, BF16)])


def _swiglu_fwd(hh, carry=None):
    S, F = hh.shape[0], hh.shape[1] // 2

    def body(i, ins, outs, accs):
        hg = ins[0][...].astype(F32)
        outs[0][...] = (hg * _sigmoid(hg) * ins[1][...].astype(F32)).astype(BF16)

    res = _rows(body, "swiglu_fwd", S, _pick(S, (128,)), [("row", hh, 0, F), ("row", hh, 1, F)], [(F, BF16)], carry=carry)
    return (res[0][0], res[1]) if carry else res[0]


def _swiglu_bwd(dact, hh):
    S, F = dact.shape

    def body(i, ins, outs, accs):
        d, hg, hu = ins[0][...].astype(F32), ins[1][...].astype(F32), ins[2][...].astype(F32)
        sg = _sigmoid(hg)
        outs[0][:, 0:F] = (d * hu * (sg * (1.0 + hg * (1.0 - sg)))).astype(BF16)
        outs[0][:, F:2 * F] = (d * (hg * sg)).astype(BF16)

    return _rows(body, "swiglu_bwd", S, _pick(S, (128,)),
                 [("row", dact, 0, F), ("row", hh, 0, F), ("row", hh, 1, F)], [(2 * F, BF16)])[0]


def _ln2_loss_bwd(x1, ffn, gate2, g, b, target):
    S, D = x1.shape

    def body(i, ins, outs, accs):
        x1_ref, f_ref, gate_ref, g_ref, b_ref, t_ref = ins
        f = f_ref[...]
        xh, rstd = _ln_stats(ALPHA * x1_ref[...] + gate_ref[...] * f)
        e = xh * g_ref[...] + b_ref[...] - t_ref[...]
        dy = e * (1.0 / D)
        dr = _ln_bwd(dy * g_ref[...], xh, rstd)
        outs[0][...] = (gate_ref[...] * dr).astype(BF16)
        outs[1][...] = ALPHA * dr
        _acc_add(i, accs[0], jnp.full((1, 128), (0.5 / D) * jnp.sum(e * e), F32))
        _acc_add(i, accs[1], jnp.sum(dy * xh, axis=0, keepdims=True))
        _acc_add(i, accs[2], jnp.sum(dy, axis=0, keepdims=True))
        _acc_add(i, accs[3], jnp.sum(dr * f, axis=0, keepdims=True))

    return _rows(body, "ln2_loss_bwd", S, _pick(S, (256, 128)),
                 [("row", x1, 0, D), ("row", ffn, 0, D), ("full", gate2), ("full", g), ("full", b), ("row", target, 0, D)],
                 [(D, BF16), (D, F32)], [(1, 128), (1, D), (1, D), (1, D)])


def _ln1_bwd(x, mix, dx1a, du2, gate1, g, b, scale2):
    S, D = x.shape

    def body(i, ins, outs, accs):
        x_ref, mix_ref, da_ref, du_ref, gate_ref, g_ref, b_ref, sc_ref = ins
        mix, du = mix_ref[...], du_ref[...]
        xh, rstd = _ln_stats(ALPHA * x_ref[...] + gate_ref[...] * mix)
        x1 = xh * g_ref[...] + b_ref[...]
        dx1 = da_ref[...] + du * (1.0 + sc_ref[...])
        dr = _ln_bwd(dx1 * g_ref[...], xh, rstd)
        outs[0][...] = (gate_ref[...] * dr).astype(BF16)
        outs[1][...] = ALPHA * dr
        _acc_add(i, accs[0], jnp.sum(du, axis=0, keepdims=True))
        _acc_add(i, accs[1], jnp.sum(du * x1, axis=0, keepdims=True))
        _acc_add(i, accs[2], jnp.sum(dx1 * xh, axis=0, keepdims=True))
        _acc_add(i, accs[3], jnp.sum(dx1, axis=0, keepdims=True))
        _acc_add(i, accs[4], jnp.sum(dr * mix, axis=0, keepdims=True))

    return _rows(body, "ln1_bwd", S, _pick(S, (256, 128)),
                 [("row", x, 0, D), ("row", mix, 0, D), ("row", dx1a, 0, D), ("row", du2, 0, D),
                  ("full", gate1), ("full", g), ("full", b), ("full", scale2)],
                 [(D, BF16), (D, F32)], [(1, D)] * 5)


def _rms_bwd(d_rq, d_rkv, pq, dkr, g_q, g_kv):
    S = pq.shape[0]

    def body(i, ins, outs, accs):
        dq_ref, dkv_ref, pq_ref, dkr_ref, gq_ref, gkv_ref = ins

        def rms_bwd(dy, x, g):
            r = lax.rsqrt(jnp.mean(x * x, axis=-1, keepdims=True) + RMS_EPS)
            dyg = dy * g
            dx = r * dyg - x * (r * r * r) * jnp.mean(dyg * x, axis=-1, keepdims=True)
            return dx, jnp.sum(dy * (x * r), axis=0, keepdims=True)

        dxq, dgq = rms_bwd(dq_ref[...], pq_ref[:, 0:Q_LORA], gq_ref[...])
        dxkv, dgkv = rms_bwd(dkv_ref[...], pq_ref[:, Q_LORA:Q_LORA + KV_LORA], gkv_ref[...])
        outs[0][:, 0:Q_LORA] = dxq.astype(BF16)
        outs[0][:, Q_LORA:Q_LORA + KV_LORA] = dxkv.astype(BF16)
        outs[0][:, Q_LORA + KV_LORA:QKV_A] = dkr_ref[...].astype(BF16)
        _acc_add(i, accs[0], dgq)
        _acc_add(i, accs[1], dgkv)

    return _rows(body, "rms_bwd", S, _pick(S, (256, 128)),
                 [("row", d_rq, 0, Q_LORA), ("row", d_rkv, 0, KV_LORA), ("row", pq, 0, QKV_A), ("row", dkr, 0, 128),
                  ("full", g_q), ("full", g_kv)], [(QKV_A, BF16)], [(1, Q_LORA), (1, KV_LORA)])


def _dx_final(dxa, du, x, scale1):
    S, D = x.shape

    def body(i, ins, outs, accs):
        du = ins[1][...]
        outs[0][...] = ins[0][...] + du * (1.0 + ins[3][...])
        _acc_add(i, accs[0], jnp.sum(du, axis=0, keepdims=True))
        _acc_add(i, accs[1], jnp.sum(du * ins[2][...], axis=0, keepdims=True))

    return _rows(body, "dx_final", S, _pick(S, (256, 128)),
                 [("row", dxa, 0, D), ("row", du, 0, D), ("row", x, 0, D), ("full", scale1)],
                 [(D, F32)], [(1, D), (1, D)])


def _ada_fwd(c_all, w, bias):
    B, D = c_all.shape
    NA = w.shape[1]
    tn = _pick(NA, (512, 256, 128))

    def body(c_ref, w_ref, b_ref, o_ref):
        cv = c_ref[...]
        ca = (cv * _sigmoid(cv)).astype(BF16)
        o_ref[...] = jnp.dot(ca, w_ref[...].astype(BF16), preferred_element_type=F32) + b_ref[...]

    return pl.pallas_call(
        body, name="ada_fwd", grid=(NA // tn,),
        in_specs=[pl.BlockSpec((B, D), lambda j: (0, 0)), pl.BlockSpec((D, tn), lambda j: (0, j)),
                  pl.BlockSpec((1, tn), lambda j: (0, j))],
        out_specs=pl.BlockSpec((B, tn), lambda j: (0, j)),
        out_shape=jax.ShapeDtypeStruct((B, NA), F32),
        compiler_params=_params(("arbitrary",)),
    )(c_all, w, bias)


def _ada_bwd(c_all, dmod):
    B, D = c_all.shape
    NA = dmod.shape[1]
    tn = _pick(NA, (512, 256, 128))

    def body(c_ref, d_ref, o_ref):
        cv = c_ref[...]
        ca = (cv * _sigmoid(cv)).astype(BF16)
        o_ref[...] = lax.dot_general(ca, d_ref[...].astype(BF16), TN, preferred_element_type=F32)

    return pl.pallas_call(
        body, name="ada_bwd", grid=(NA // tn,),
        in_specs=[pl.BlockSpec((B, D), lambda j: (0, 0)), pl.BlockSpec((B, tn), lambda j: (0, j))],
        out_specs=pl.BlockSpec((D, tn), lambda j: (0, j)),
        out_shape=jax.ShapeDtypeStruct((D, NA), F32),
        compiler_params=_params(("arbitrary",)),
    )(c_all, dmod)


def _pack_rows(parts, n_rows, after=()):
    N = parts[0].shape[1]
    n = len(parts)

    def body(*refs):
        o_ref = refs[-1]
        o_ref[...] = jnp.zeros_like(o_ref)
        at = 0
        for r in refs[:n]:
            o_ref[at:at + r.shape[0], :] = r[...]
            at += r.shape[0]

    vmem = pl.BlockSpec(memory_space=pltpu.VMEM)
    return pl.pallas_call(body, name="pack_small", out_shape=jax.ShapeDtypeStruct((n_rows, N), F32),
                          in_specs=[vmem] * n + [ANY] * len(after), out_specs=vmem,
                          compiler_params=_params())(*parts, *after)


def _sum8(parts):
    _, R, N = parts.shape

    def body(p_ref, o_ref):
        acc = p_ref[0]
        for d in range(1, 8):
            acc = acc + p_ref[d]
        o_ref[...] = acc

    return pl.pallas_call(body, name="sum8", out_shape=jax.ShapeDtypeStruct((R, N), F32),
                          compiler_params=_params())(parts)


def _adam_math(w, g, m, v):
    m = ADAM_B1 * m + (1.0 - ADAM_B1) * g
    v = ADAM_B2 * v + (1.0 - ADAM_B2) * (g * g)
    delta = -ADAM_LR * ((m / ADAM_C1) / (jnp.sqrt(v / ADAM_C2) + ADAM_EPS) + ADAM_WD * w)
    return delta, m, v


def _adam(name, w, m, v, g, carry=None):
    R, C = w.shape
    tm = _row_tile(R, C * 4, 1 << 20)
    steps = R // tm
    n_ci = len(carry.ins) if carry else 0
    n_co = len(carry.outs) if carry else 0

    def body(*refs):
        w_ref, m_ref, v_ref, g_ref = refs[:4]
        d_ref, nm_ref, nv_ref = refs[4 + n_ci:7 + n_ci]
        c_ins, c_outs, c_sems = refs[4:4 + n_ci], refs[7 + n_ci:7 + n_ci + n_co], refs[7 + n_ci + n_co:]
        if carry:
            @pl.when(pl.program_id(0) == 0)
            def _():
                carry.start(c_ins, c_outs, c_sems)

        delta, nm, nv = _adam_math(w_ref[...], g_ref[...], m_ref[...], v_ref[...])
        d_ref[...] = delta
        nm_ref[...] = nm
        nv_ref[...] = nv
        if carry:
            @pl.when(pl.program_id(0) == steps - 1)
            def _():
                carry.finish(c_ins, c_outs, c_sems)

    spec = pl.BlockSpec((tm, C), lambda i: (i, 0))
    res = pl.pallas_call(
        body, name=name, grid=(steps,), in_specs=[spec] * 4 + [ANY] * n_ci, out_specs=[spec] * 3 + [ANY] * n_co,
        out_shape=[jax.ShapeDtypeStruct((R, C), F32)] * 3 + (carry.outs if carry else []),
        scratch_shapes=carry.sems if carry else [],
        input_output_aliases=carry.io_aliases(4, 3) if carry else {},
        compiler_params=_params(("arbitrary",)),
    )(w, m, v, g, *(carry.ins if carry else []))
    return (res[:3], res[3:]) if carry else res


def _adam_halves(name, w, m, v, mine, other, core, carry=None):
    R, C = w.shape
    Rh = mine.shape[0]
    tc = max(t for t in range(128, C + 1, 128) if C % t == 0 and R * t <= (3 << 17))
    steps = C // tc
    n_ci = len(carry.ins) if carry else 0
    n_co = len(carry.outs) if carry else 0

    def body(*refs):
        c_ref, w_ref, m_ref, v_ref, a_ref, b_ref = refs[:6]
        g_ref, d_ref, nm_ref, nv_ref = refs[6 + n_ci:10 + n_ci]
        c_ins, c_outs, c_sems = refs[6:6 + n_ci], refs[10 + n_ci:10 + n_ci + n_co], refs[10 + n_ci + n_co:]
        if carry:
            @pl.when(pl.program_id(0) == 0)
            def _():
                carry.start(c_ins, c_outs, c_sems)

        first = c_ref[0] == 0
        g = jnp.concatenate([jnp.where(first, a_ref[...], b_ref[...]),
                             jnp.where(first, b_ref[0:R - Rh, :], a_ref[0:R - Rh, :])], axis=0)
        delta, nm, nv = _adam_math(w_ref[...], g, m_ref[...], v_ref[...])
        g_ref[...] = g
        d_ref[...] = delta
        nm_ref[...] = nm
        nv_ref[...] = nv
        if carry:
            @pl.when(pl.program_id(0) == steps - 1)
            def _():
                carry.finish(c_ins, c_outs, c_sems)

    spec = pl.BlockSpec((R, tc), lambda i, c_ref: (0, i))
    h_spec = pl.BlockSpec((Rh, tc), lambda i, c_ref: (0, i))
    res = pl.pallas_call(
        body, name=name, out_shape=[jax.ShapeDtypeStruct((R, C), F32)] * 4 + (carry.outs if carry else []),
        grid_spec=pltpu.PrefetchScalarGridSpec(
            num_scalar_prefetch=1, grid=(steps,), in_specs=[spec, spec, spec, h_spec, h_spec] + [ANY] * n_ci,
            out_specs=[spec] * 4 + [ANY] * n_co, scratch_shapes=carry.sems if carry else []),
        input_output_aliases=carry.io_aliases(6, 4) if carry else {},
        compiler_params=_params(("arbitrary",)),
    )(core, w, m, v, mine, other, *(carry.ins if carry else []))
    return (res[:4], res[4:]) if carry else res


def _adam_small(name, w, m, v, g):
    def body(w_ref, m_ref, v_ref, g_ref, d_ref, nm_ref, nv_ref):
        delta, nm, nv = _adam_math(w_ref[...], g_ref[...], m_ref[...], v_ref[...])
        d_ref[...] = delta
        nm_ref[...] = nm
        nv_ref[...] = nv

    return pl.pallas_call(body, name=name, out_shape=[jax.ShapeDtypeStruct(w.shape, F32)] * 3,
                          compiler_params=_params())(w, m, v, g)


def _place():
    return lax.axis_index("x"), lax.axis_index("y"), lax.axis_index("c")


def _other_chips(x, y):
    return [(1 - x, y), (x, 1 - y), (1 - x, 1 - y)]


def _all_gather8(blk, name):
    R, N = blk.shape

    def body(x_ref, out_ref, send_sems, recv_sems, local_sem):
        x, y, c = _place()
        me = 4 * x + 2 * y + c
        mine = pltpu.make_async_copy(x_ref, out_ref.at[me], local_sem)
        mine.start()
        flips = [(j >> 2 & 1, j >> 1 & 1, j & 1) for j in range(1, 8)]
        peers = [((1 - x) if fx else x, (1 - y) if fy else y, (1 - c) if fc else c) for fx, fy, fc in flips]
        sends = []
        for j, peer in enumerate(peers):
            cp = pltpu.make_async_remote_copy(src_ref=x_ref, dst_ref=out_ref.at[me], send_sem=send_sems.at[j],
                                              recv_sem=recv_sems.at[j], device_id=peer, device_id_type=MESH)
            cp.start()
            sends.append(cp)
        for j, (px, py, pc) in enumerate(peers):
            pltpu.make_async_remote_copy(src_ref=x_ref, dst_ref=out_ref.at[4 * px + 2 * py + pc],
                                         send_sem=send_sems.at[j], recv_sem=recv_sems.at[j],
                                         device_id=(px, py, pc), device_id_type=MESH).wait_recv()
        for cp in sends:
            cp.wait_send()
        mine.wait()

    return pl.pallas_call(
        body, name=name, out_shape=jax.ShapeDtypeStruct((8, R, N), F32),
        in_specs=[pl.BlockSpec(memory_space=pltpu.VMEM)], out_specs=pl.BlockSpec(memory_space=pltpu.VMEM),
        scratch_shapes=[pltpu.SemaphoreType.DMA((7,)), pltpu.SemaphoreType.DMA((7,)), pltpu.SemaphoreType.DMA],
        compiler_params=_params(),
    )(blk)


def _piece(rows, piece):
    i, n, k = piece if len(piece) == 3 else (piece[0], piece[1], 1)
    assert rows % 16 == 0 and rows // 16 >= n, (rows, piece)
    lo, hi = (rows // 16 * i // n) * 16, (rows // 16 * (i + k) // n) * 16
    return pl.ds(lo, hi - lo)


def _scatter_plan(arrs, piece=(0, 1), into=None):
    n = len(arrs)

    def copies(ins, outs, sems):
        send_sems, recv_sems = sems
        x, y, c = _place()
        chips = _other_chips(x, y)
        cps = []
        for k in range(n):
            rows = _piece(arrs[k].shape[1], piece)
            for j, (px, py) in enumerate(chips):
                cps.append(pltpu.make_async_remote_copy(
                    src_ref=ins[k].at[2 * px + py, rows], dst_ref=outs[k].at[j, rows],
                    send_sem=send_sems.at[3 * k + j], recv_sem=recv_sems.at[3 * k + j],
                    device_id=(px, py, c), device_id_type=MESH))
        return cps

    def start(ins, outs, sems):
        for cp in copies(ins, outs, sems):
            cp.start()

    def finish(ins, outs, sems):
        for cp in copies(ins, outs, sems):
            cp.wait()

    return _Plan(list(arrs) + list(into or []), [jax.ShapeDtypeStruct((3,) + a.shape[1:], a.dtype) for a in arrs],
                 [pltpu.SemaphoreType.DMA((3 * n,))] * 2, start, finish,
                 aliases={n + k: k for k in range(n)} if into else None)


def _gather_plan(shards, piece=(0, 1), into=None, ici=True):
    n = len(shards)

    def parts(ins, outs, sems):
        s1, r1, s2, r2, loc = sems
        x, y, c = _place()
        me = 2 * x + y
        chips = _other_chips(x, y)
        sib = (x, y, 1 - c)

        def rows(k):
            return _piece(shards[k].shape[1], piece)

        def ici_copy(k, j, slab, to):
            return pltpu.make_async_remote_copy(src_ref=ins[k].at[c, rows(k)], dst_ref=outs[k].at[slab, c, rows(k)],
                                                send_sem=s1.at[3 * k + j], recv_sem=r1.at[3 * k + j],
                                                device_id=to, device_id_type=MESH)

        def d2d(k, j, slab, half):
            return pltpu.make_async_remote_copy(src_ref=outs[k].at[slab, half, rows(k)],
                                                dst_ref=outs[k].at[slab, half, rows(k)],
                                                send_sem=s2.at[3 * k + j], recv_sem=r2.at[3 * k + j],
                                                device_id=sib, device_id_type=MESH)

        def own(k):
            return pltpu.make_async_remote_copy(src_ref=ins[k].at[:, rows(k)], dst_ref=outs[k].at[me, :, rows(k)],
                                                send_sem=loc.at[2 * k], recv_sem=loc.at[2 * k + 1],
                                                device_id=sib, device_id_type=MESH)

        return c, me, chips, ici_copy, d2d, own

    def start(ins, outs, sems):
        c, me, chips, ici_copy, d2d, own = parts(ins, outs, sems)
        for k in range(n):
            for j, (px, py) in enumerate(chips):
                (ici_copy(k, j, me, (px, py, c)) if ici else d2d(k, j, 2 * px + py, c)).start()
        for k in range(n):
            own(k).start()

    def finish(ins, outs, sems):
        c, me, chips, ici_copy, d2d, own = parts(ins, outs, sems)
        if ici:
            for k in range(n):
                for j, (px, py) in enumerate(chips):
                    ici_copy(k, j, 2 * px + py, (px, py, c)).wait_recv()
                    d2d(k, j, 2 * px + py, c).start()
        for k in range(n):
            for j, (px, py) in enumerate(chips):
                d2d(k, j, 2 * px + py, 1 - c).wait_recv()
        for k in range(n):
            own(k).wait()
            for j, (px, py) in enumerate(chips):
                if ici:
                    ici_copy(k, j, me, (px, py, c)).wait_send()
                d2d(k, j, 2 * px + py, c).wait_send()

    return _Plan(list(shards) + list(into or []), [jax.ShapeDtypeStruct((4,) + a.shape, a.dtype) for a in shards],
                 [pltpu.SemaphoreType.DMA((3 * n,))] * 4 + [pltpu.SemaphoreType.DMA((2 * n,))], start, finish,
                 aliases={n + k: k for k in range(n)} if into else None)


def _pair_plan(parts):
    n = len(parts)

    def copies(ins, outs, sems):
        send_sems, recv_sems = sems
        x, y, c = _place()
        return [pltpu.make_async_remote_copy(src_ref=ins[k].at[p, 1 - c], dst_ref=outs[k].at[p],
                                             send_sem=send_sems.at[4 * k + p], recv_sem=recv_sems.at[4 * k + p],
                                             device_id=(x, y, 1 - c), device_id_type=MESH)
                for k in range(n) for p in range(4)]

    def start(ins, outs, sems):
        for cp in copies(ins, outs, sems):
            cp.start()

    def finish(ins, outs, sems):
        for cp in copies(ins, outs, sems):
            cp.wait()

    return _Plan(parts, [jax.ShapeDtypeStruct((4,) + a.shape[2:], a.dtype) for a in parts],
                 [pltpu.SemaphoreType.DMA((4 * n,))] * 2, start, finish)


def _sibling_plan(arrs):
    n = len(arrs)

    def copies(ins, outs, sems):
        send_sems, recv_sems = sems
        x, y, c = _place()
        return [pltpu.make_async_remote_copy(src_ref=ins[k], dst_ref=outs[k], send_sem=send_sems.at[k],
                                             recv_sem=recv_sems.at[k], device_id=(x, y, 1 - c), device_id_type=MESH)
                for k in range(n)]

    def start(ins, outs, sems):
        for cp in copies(ins, outs, sems):
            cp.start()

    def finish(ins, outs, sems):
        for cp in copies(ins, outs, sems):
            cp.wait()

    return _Plan(arrs, [jax.ShapeDtypeStruct(a.shape, a.dtype) for a in arrs],
                 [pltpu.SemaphoreType.DMA((n,))] * 2, start, finish)


def _scatter_copies(arrs):
    def copies(ins, land, send_sems, recv_sems):
        x, y, c = _place()
        return [pltpu.make_async_remote_copy(src_ref=ins[k].at[2 * px + py], dst_ref=land[k].at[j],
                                             send_sem=send_sems.at[3 * k + j], recv_sem=recv_sems.at[3 * k + j],
                                             device_id=(px, py, c), device_id_type=MESH)
                for k in range(len(arrs)) for j, (px, py) in enumerate(_other_chips(x, y))]

    return copies, [lax.empty((3,) + a.shape[1:], a.dtype) for a in arrs]


def _gather_copies(shards):
    def copies(ins, land, send_sems, recv_sems):
        x, y, c = _place()
        return [pltpu.make_async_remote_copy(src_ref=ins[k].at[c], dst_ref=land[k].at[2 * x + y, c],
                                             send_sem=send_sems.at[3 * k + j], recv_sem=recv_sems.at[3 * k + j],
                                             device_id=(px, py, c), device_id_type=MESH)
                for k in range(len(shards)) for j, (px, py) in enumerate(_other_chips(x, y))]

    return copies, [lax.empty((4,) + a.shape, a.dtype) for a in shards]


def _split_start(arrs, copies_lands, ride, name, after=()):
    copies, lands = copies_lands
    n = len(arrs)
    rides = list(ride) if isinstance(ride, (list, tuple)) else [ride]
    n_thru = 2 * n + len(rides)

    def body(*refs):
        first_out = n_thru + len(after)
        for cp in copies(refs[:n], refs[n:2 * n], refs[first_out], refs[first_out + 1]):
            cp.start()

    hbm = [pltpu.with_memory_space_constraint(a, pltpu.HBM) for a in list(arrs) + lands + rides]
    res = pl.pallas_call(
        body, name=name,
        out_shape=[pltpu.SemaphoreType.DMA((3 * n,)), pltpu.SemaphoreType.DMA((3 * n,))]
        + [pltpu.HBM(a.shape, a.dtype) for a in hbm],
        in_specs=[HBM_SPEC] * n_thru + [ANY] * len(after),
        out_specs=[SEM_SPEC, SEM_SPEC] + [HBM_SPEC] * n_thru,
        input_output_aliases={i: 2 + i for i in range(n_thru)},
        compiler_params=pltpu.CompilerParams(has_side_effects=pltpu.SideEffectType.DATAFLOW_SIDE_EFFECTING),
    )(*hbm, *after)
    return res[0], res[1], res[2:2 + n], res[2 + n:2 + 2 * n], list(res[2 + 2 * n:])


def _split_wait(started, copies_lands, after, name):
    send_sems, recv_sems, arrs, lands, _ = started
    copies = copies_lands[0]
    n = len(arrs)

    def body(*refs):
        for cp in copies(refs[:n], refs[n:2 * n], refs[2 * n], refs[2 * n + 1]):
            cp.wait_send()
            cp.wait_recv()

    res = pl.pallas_call(
        body, name=name, out_shape=[pltpu.HBM(a.shape, a.dtype) for a in list(arrs) + list(lands)],
        in_specs=[HBM_SPEC] * (2 * n) + [SEM_SPEC, SEM_SPEC] + [ANY] * len(after), out_specs=[HBM_SPEC] * (2 * n),
        input_output_aliases={i: i for i in range(2 * n)},
        compiler_params=pltpu.CompilerParams(has_side_effects=pltpu.SideEffectType.DATAFLOW_SIDE_EFFECTING),
    )(*arrs, *lands, send_sems, recv_sems, *after)
    return list(res[:n]), list(res[n:])


def _join_plans(plans):
    def split(seq, counts):
        out, at = [], 0
        for cnt in counts:
            out.append(seq[at:at + cnt])
            at += cnt
        return out

    n_i, n_o, n_s = ([len(getattr(p, f)) for p in plans] for f in ("ins", "outs", "sems"))

    def start(ins, outs, sems):
        for p, i, o, s in zip(plans, split(ins, n_i), split(outs, n_o), split(sems, n_s)):
            p.start(i, o, s)

    def finish(ins, outs, sems):
        for p, i, o, s in zip(plans, split(ins, n_i), split(outs, n_o), split(sems, n_s)):
            p.finish(i, o, s)

    aliases, at_i, at_o = {}, 0, 0
    for p in plans:
        aliases.update(p.io_aliases(at_i, at_o))
        at_i, at_o = at_i + len(p.ins), at_o + len(p.outs)
    return _Plan(sum((p.ins for p in plans), []), sum((p.outs for p in plans), []), sum((p.sems for p in plans), []),
                 start, finish, aliases)


def _add_pair(parts, sib, core, name):
    P4, _, Rh, C = parts.shape
    tm, tc = _tile2(Rh, C, 16)

    def body(c_ref, a_ref, b_ref, o_ref):
        o_ref[...] = (a_ref[0].astype(F32) + b_ref[...].astype(F32)).astype(BF16)

    spec = pl.BlockSpec((1, tm, tc), lambda p, i, j, c_ref: (p, i, j))
    return pl.pallas_call(
        body, name=name, out_shape=jax.ShapeDtypeStruct((P4, Rh, C), BF16),
        grid_spec=pltpu.PrefetchScalarGridSpec(
            num_scalar_prefetch=1, grid=(P4, Rh // tm, C // tc),
            in_specs=[pl.BlockSpec((1, 1, tm, tc), lambda p, i, j, c_ref: (p, c_ref[0], i, j)), spec], out_specs=spec),
        compiler_params=_params(("parallel",) * 3),
    )(core, parts, sib)


def _sum_slabs(pre, recv, chip, name):
    _, Rh, C = pre.shape
    tm, tc = _tile2(Rh, C, 16)

    def body(me_ref, own_ref, r_ref, o_ref):
        acc = own_ref[0].astype(F32)
        for j in range(3):
            acc = acc + r_ref[j].astype(F32)
        o_ref[...] = acc

    return pl.pallas_call(
        body, name=name, out_shape=jax.ShapeDtypeStruct((Rh, C), F32),
        grid_spec=pltpu.PrefetchScalarGridSpec(
            num_scalar_prefetch=1, grid=(Rh // tm, C // tc),
            in_specs=[pl.BlockSpec((1, tm, tc), lambda i, j, me_ref: (me_ref[0], i, j)),
                      pl.BlockSpec((3, tm, tc), lambda i, j, me_ref: (0, i, j))],
            out_specs=pl.BlockSpec((tm, tc), lambda i, j, me_ref: (i, j))),
        compiler_params=_params(("parallel", "parallel")),
    )(chip, pre, recv)


def kernel(x, c, positions, w_ada, b_ada, w_in, g_q_a, w_q_b, g_kv_a, w_kv_b, w_o_a, w_conv, w_o_b, w_o, ln1_g, ln1_b, w_ffn_in, w_ffn_out, ln2_g, ln2_b, loss_target, m_w_ada, m_b_ada, m_w_in, m_g_q_a, m_w_q_b, m_g_kv_a, m_w_kv_b, m_w_o_a, m_w_conv, m_w_o_b, m_w_o, m_ln1_g, m_ln1_b, m_w_ffn_in, m_w_ffn_out, m_ln2_g, m_ln2_b, v_w_ada, v_b_ada, v_w_in, v_g_q_a, v_w_q_b, v_g_kv_a, v_w_kv_b, v_w_o_a, v_w_conv, v_w_o_b, v_w_o, v_ln1_g, v_ln1_b, v_w_ffn_in, v_w_ffn_out, v_ln2_g, v_ln2_b):
    S, D = x.shape[1], x.shape[2]
    F = w_ffn_out.shape[1] * 4
    ax, ay, ac = _place()
    chip = 2 * ax + ay
    dev = 4 * ax + 2 * ay + ac
    x2, tgt = x[0], loss_target[0]
    w_ada2, w_in2, w_q_b2, w_kv_b2 = w_ada[0], w_in[0], w_q_b[0], w_kv_b[0]
    w_o_a2, w_o_b2, w_o2, w_ffn_in2, w_ffn_out2 = w_o_a[0], w_o_b[0], w_o[0], w_ffn_in[0], w_ffn_out[0]
    NA = w_ada2.shape[1]
    CW = w_conv.shape[2]

    inv_freq = 1.0 / (ROPE_THETA ** (jnp.arange(0, QK_ROPE, 2, dtype=F32) / QK_ROPE))
    ang = positions[0].astype(F32)[:, None] * inv_freq
    cos, sin = jnp.cos(ang), jnp.sin(ang)
    z32, z64, z96 = jnp.zeros((S, 32), F32), jnp.zeros((S, 64), F32), jnp.zeros((S, 96), F32)
    tab = jnp.concatenate([cos, cos, z64, -sin, z96, z32, sin, z64], axis=1)

    def halves(a):
        return a.reshape(2, a.shape[0] // 2, a.shape[1])

    def whole(g):
        return g.reshape(4, 2 * g.shape[2], g.shape[3])

    def cols(g):
        return jnp.transpose(g, (1, 0, 2)).reshape(g.shape[1], 4 * g.shape[2])

    w_inT, m_w_inT, v_w_inT = w_in2.T, m_w_in[0].T, v_w_in[0].T
    CS = w_inT.shape[0]
    CSP = -(-CS // 32) * 32
    sh_in = halves(jnp.pad(w_inT.astype(BF16), ((0, CSP - CS), (0, 0))))
    sh_qb, sh_kvb, sh_oa, sh_ob, sh_o, sh_fi, sh_fo = (
        halves(w.astype(BF16)) for w in (w_q_b2, w_kv_b2, w_o_a2, w_o_b2, w_o2, w_ffn_in2, w_ffn_out2))
    c_all = _all_gather8(c, "gather_c").reshape(8, D)
    wconv_all = _all_gather8(w_conv[0], "gather_wconv")
    w_conv_full = jnp.transpose(wconv_all[0::2], (1, 0, 2)).reshape(3, D)
    b_sh = lax.dynamic_slice(b_ada, (0, chip * NA), (1, NA))
    mod_sh = _ada_fwd(c_all, w_ada2, b_sh)
    mod_all = _all_gather8(mod_sh, "gather_mod")
    mod = lax.dynamic_slice(mod_all[0::2], (0, dev, 0), (4, 1, NA)).reshape(6, D)
    shift1, scale1, gate1, shift2, scale2, gate2 = (mod[k:k + 1] for k in range(6))

    g_in, shift1 = _run_plan(_gather_plan([sh_in]), "gather_first", ride=shift1)
    g_in = whole(g_in)
    sh_a1, sh_a2 = [sh_qb, sh_kvb], [sh_oa, sh_ob, sh_o]
    cl_a1, cl_a2, cl_fi, cl_fo = (_gather_copies(g) for g in (sh_a1, sh_a2, [sh_fi], [sh_fo]))
    st_a1 = _split_start(sh_a1, cl_a1, shift1, "gather_a1_start")
    st_a2 = _split_start(sh_a2, cl_a2, st_a1[4], "gather_a2_start")
    shift1 = st_a2[4][0]

    def in_rows(lo, hi):
        parts = [g_in[p, max(lo, p * CS) - p * CS:min(hi, (p + 1) * CS) - p * CS]
                 for p in range(4) if max(lo, p * CS) < min(hi, (p + 1) * CS)]
        return parts[0] if len(parts) == 1 else jnp.concatenate(parts, axis=0)

    n_qkv = Q_LORA + KV_LORA + QK_ROPE
    W_qkvT = jnp.pad(in_rows(0, n_qkv), ((0, QKV_A - n_qkv), (0, 0)))
    W_convT = in_rows(n_qkv, n_qkv + 3 * D)
    W_gateT = in_rows(n_qkv + 3 * D, n_qkv + 5 * D)

    u = _modulate(x2, scale1, shift1, "modulate1")
    pq = _matmul(u, W_qkvT, "nt", F32, "proj_qkv")
    pc = _matmul(u, W_convT, "nt", F32, "proj_conv")
    sh_a1, la1 = _split_wait(st_a1, cl_a1, [pc], "gather_a1_wait")
    pg, (g_qb, g_kvb) = _matmul(u, W_gateT, "nt", BF16, "proj_gate", carry=_gather_plan(sh_a1, into=la1, ici=False))
    st_fi = _split_start([sh_fi], cl_fi, g_q_a, "gather_fi_start", after=[pg])
    W_qb = jnp.pad(cols(whole(g_qb)).reshape(Q_LORA, N_HEADS, QK_NOPE + QK_ROPE),
                   ((0, 0), (0, 0), (0, QK_PAD - QK_NOPE - QK_ROPE))).reshape(Q_LORA, N_HEADS * QK_PAD)
    W_kvb = cols(whole(g_kvb))
    rq, rkv, kr = _rms_fwd(pq, tab, st_fi[4][0], g_kv_a)
    kv = _matmul(rkv, W_kvb, "nn", BF16, "kv_b")
    sh_a2, la2 = _split_wait(st_a2, cl_a2, [kv], "gather_a2_wait")
    qf, (g_oa, g_ob, g_o) = _matmul(rq, W_qb, "nn", F32, "q_b", carry=_gather_plan(sh_a2, into=la2, ici=False))
    q = _q_rope(qf, tab)
    o, lse, _ = _attn_fwd(q, kv, kr)
    W_oa, W_ob, W_o = (g.reshape(-1, D) for g in (g_oa, g_ob, g_o))
    hb = _conv_fwd(pc, w_conv_full)
    sh_fi_t, lfi = _split_wait(st_fi, cl_fi, [hb], "gather_fi_wait")
    y_b, g_fi = _matmul(hb, W_ob, "nn", F32, "o_b", carry=_gather_plan(sh_fi_t, (0, 2), into=lfi, ici=False))
    y_a, (g_fi,) = _matmul(o, W_oa, "nn", F32, "o_a", carry=_gather_plan(sh_fi_t, (1, 2), into=g_fi, ici=False))
    st_fo = _split_start([sh_fo], cl_fo, ln1_g, "gather_fo_start", after=[y_b])
    merged = _merge_fwd(y_a, y_b, pg)
    mix = _matmul(merged, W_o, "nn", F32, "w_o")
    W_fi = whole(g_fi)
    x1, u2 = _ln1_fwd(x2, mix, gate1, st_fo[4][0], ln1_b, scale2, shift2)
    hh = _matmul(u2, W_fi, "nn", BF16, "ffn_in", shards="b")
    sh_fo_t, lfo = _split_wait(st_fo, cl_fo, [hh], "gather_fo_wait")
    act, (g_fo,) = _swiglu_fwd(hh, carry=_gather_plan(sh_fo_t, into=lfo, ici=False))
    W_fo = g_fo.reshape(F, D)
    ffn = _matmul(act, W_fo, "nn", F32, "ffn_out")

    core_i = ac.astype(jnp.int32).reshape(1)
    chip_i = chip.astype(jnp.int32).reshape(1)

    def uncols(g):
        return jnp.transpose(g.reshape(g.shape[0], 4, g.shape[1] // 4), (1, 0, 2))

    def slabs(p):
        return p.reshape(4, 2, p.shape[1] // 2, p.shape[2])

    def add_pairs(parts, sibs, nms):
        return [_add_pair(a, b, core_i, "add_pair_" + nm) for a, b, nm in zip(parts, sibs, nms)]

    def sum_all(pre, recv, nms):
        return [_sum_slabs(a, r, chip_i, "sum_slabs_" + nm) for a, r, nm in zip(pre, recv, nms)]

    dffn, dx1a, loss_acc, d_ln2_g, d_ln2_b, d_gate2 = _ln2_loss_bwd(x1, ffn, gate2, ln2_g, ln2_b, tgt)
    loss = lax.psum(loss_acc[0, 0], ("x", "y", "c"))
    dW_fo = _matmul(act, dffn, "tn", BF16, "d_w_ffn_out")
    p_fo = [slabs(dW_fo.reshape(4, -1, D))]
    dact, s_fo = _matmul(dffn, W_fo, "nt", BF16, "d_act", carry=_pair_plan(p_fo))
    pre_fo = add_pairs(p_fo, s_fo, ["w_ffn_out"])
    cs_fo = _scatter_copies(pre_fo)
    st_sfo = _split_start(pre_fo, cs_fo, scale2, "scatter_fo_start")
    dhh = _swiglu_bwd(dact, hh)
    dW_fi = _matmul(u2, dhh, "tn", BF16, "d_w_ffn_in", shards="o")
    p_fi = [slabs(dW_fi)]
    du2, s_fi = _matmul(dhh, W_fi, "nt", F32, "d_u2", carry=_pair_plan(p_fi), shards="b")
    pre_fi = add_pairs(p_fi, s_fi, ["w_ffn_in"])
    cs_fi = _scatter_copies(pre_fi)
    st_sfi = _split_start(pre_fi, cs_fi, st_sfo[4], "scatter_fi_start")
    dmix, dxa, d_shift2, d_scale2, d_ln1_g, d_ln1_b, d_gate1 = _ln1_bwd(x2, mix, dx1a, du2, gate1, ln1_g, ln1_b, st_sfi[4][0])
    dW_o = _matmul(merged, dmix, "tn", BF16, "d_w_o")
    dmerged = _matmul(dmix, W_o, "nt", F32, "d_merged")
    dy_a, dy_b, dgate = _merge_bwd(dmerged, y_a, y_b, pg)
    dW_oa = _matmul(o, dy_a, "tn", BF16, "d_w_o_a")
    do = _matmul(dy_a, W_oa, "nt", BF16, "d_o")
    dW_ob = _matmul(hb, dy_b, "tn", BF16, "d_w_o_b")
    p_mid = [slabs(g.reshape(4, -1, D)) for g in (dW_oa, dW_ob, dW_o)]
    dhb, s_mid = _matmul(dy_b, W_ob, "nt", F32, "d_hb", carry=_pair_plan(p_mid))
    pre_mid = add_pairs(p_mid, s_mid, ["w_o_a", "w_o_b", "w_o"])
    cs_mid = _scatter_copies(pre_mid)
    st_smid = _split_start(pre_mid, cs_mid, w_conv_full, "scatter_mid_start")
    dconv, d_wconv = _conv_bwd(dhb, pc, st_smid[4][0])
    dq, dkv, dkr, _ = _attn_bwd(q, kv, kr, do, o, lse, tab, carry=_token_plan(st_smid[4][0]))
    names_a = ["w_ffn_out", "w_ffn_in", "w_o_a", "w_o_b", "w_o"]
    dW_qb = _matmul(rq, dq, "tn", BF16, "d_w_q_b")
    d_rq = _matmul(dq, W_qb, "nt", F32, "d_rq")
    dW_kvb = _matmul(rkv, dkv, "tn", BF16, "d_w_kv_b")
    d_rkv = _matmul(dkv, W_kvb, "nt", F32, "d_rkv")
    dqkv, d_g_q, d_g_kv = _rms_bwd(d_rq, d_rkv, pq, dkr, g_q_a, g_kv_a)
    dW_qkvT = _matmul(dqkv, u, "tn", BF16, "d_w_qkv")
    dW_convT = _matmul(dconv, u, "tn", BF16, "d_w_conv")
    dW_gateT = _matmul(dgate, u, "tn", BF16, "d_w_gate")
    pre_fo, r_fo = _split_wait(st_sfo, cs_fo, [dW_qkvT], "scatter_fo_wait")
    pre_fi, r_fi = _split_wait(st_sfi, cs_fi, [dW_qkvT], "scatter_fi_wait")
    pre_mid, r_mid = _split_wait(st_smid, cs_mid, [dW_qkvT], "scatter_mid_wait")
    fin_a = sum_all(pre_fo + pre_fi + pre_mid, r_fo + r_fi + r_mid, names_a)
    dW_inT = jnp.concatenate([dW_qkvT[:n_qkv], dW_convT, dW_gateT], axis=0).reshape(4, CS, D)
    dW_inT = jnp.pad(dW_inT, ((0, 0), (0, CSP - CS), (0, 0)))
    dW_qb_u = dW_qb.reshape(Q_LORA, N_HEADS, QK_PAD)[:, :, :QK_NOPE + QK_ROPE].reshape(Q_LORA, -1)
    names_b = ["w_in", "w_q_b", "w_kv_b"]
    p_b = [slabs(dW_inT), slabs(uncols(dW_qb_u)), slabs(uncols(dW_kvb))]
    du, s_b = _matmul(dqkv, W_qkvT, "nn", F32, "d_u_qkv", carry=_pair_plan(p_b))
    pre_b = add_pairs(p_b, s_b, names_b)
    cs_b = _scatter_copies(pre_b)
    st_b = _split_start(pre_b, cs_b, scale1, "scatter_last_start")
    du, fs_a = _matmul(dconv, W_convT, "nn", F32, "d_u_conv", add=du, carry=_sibling_plan(fin_a))
    du = _matmul(dgate, W_gateT, "nn", F32, "d_u_gate", add=du)
    grad_x, d_shift1, d_scale1 = _dx_final(dxa, du, x2, st_b[4][0])

    big = {}
    ws = dict(w_in=(w_inT, m_w_inT, v_w_inT), w_q_b=(w_q_b2, m_w_q_b[0], v_w_q_b[0]),
              w_kv_b=(w_kv_b2, m_w_kv_b[0], v_w_kv_b[0]), w_o_a=(w_o_a2, m_w_o_a[0], v_w_o_a[0]),
              w_o_b=(w_o_b2, m_w_o_b[0], v_w_o_b[0]), w_o=(w_o2, m_w_o[0], v_w_o[0]),
              w_ffn_in=(w_ffn_in2, m_w_ffn_in[0], v_w_ffn_in[0]), w_ffn_out=(w_ffn_out2, m_w_ffn_out[0], v_w_ffn_out[0]))

    def adam_of(nm, a, b, carry=None):
        w_, m_, v_ = ws[nm]
        return _adam_halves("adam_" + nm, w_, m_, v_, a, b, core_i, carry)

    for nm, a, b in zip(names_a, fin_a, fs_a):
        big[nm] = adam_of(nm, a, b, _token_plan(st_b[4][0]))[0]
    done = [big[nm][1] for nm in names_a] + [grad_x]
    pre_b, r_b = _split_wait(st_b, cs_b, done, "scatter_last_wait")
    fin_b = sum_all(pre_b, r_b, names_b)
    fs_b = _run_plan(_sibling_plan(fin_b), "sibling_last")
    for nm, a, b in zip(names_b, fin_b, fs_b):
        big[nm] = adam_of(nm, a, b)

    def pad_d(v):
        return jnp.pad(v, ((0, 0), (0, D - v.shape[1])))

    small = _pack_rows([d_ln1_g, d_ln1_b, d_ln2_g, d_ln2_b, pad_d(d_g_q), pad_d(d_g_kv), d_wconv,
                         d_shift1, d_scale1, d_gate1, d_shift2, d_scale2, d_gate2], 16, after=[pre_b[1]])
    small_all = _all_gather8(small, "gather_small")
    small_sum = _sum8(small_all)
    g_ln1_g, g_ln1_b, g_ln2_g, g_ln2_b = (small_sum[k:k + 1] for k in range(4))
    g_g_q, g_g_kv = small_sum[4:5, :Q_LORA], small_sum[5:6, :KV_LORA]
    g_wconv = lax.dynamic_slice(small_sum[6:9], (0, chip * CW), (3, CW))
    g_b_ada = small_sum[9:15].reshape(1, 6 * D)
    dmod_all = small_all[:, 9:15, :].reshape(8, 6 * D)
    g_w_ada = _ada_bwd(c_all, lax.dynamic_slice(dmod_all, (0, chip * NA), (8, NA)))
    big["w_ada"] = [g_w_ada] + list(_adam("adam_w_ada", w_ada2, m_w_ada[0], v_w_ada[0], g_w_ada))
    sm = {}
    for nm, w_, m_, v_, g_ in [("b_ada", b_ada, m_b_ada, v_b_ada, g_b_ada), ("g_q_a", g_q_a, m_g_q_a, v_g_q_a, g_g_q),
                               ("g_kv_a", g_kv_a, m_g_kv_a, v_g_kv_a, g_g_kv),
                               ("w_conv", w_conv[0], m_w_conv[0], v_w_conv[0], g_wconv),
                               ("ln1_g", ln1_g, m_ln1_g, v_ln1_g, g_ln1_g), ("ln1_b", ln1_b, m_ln1_b, v_ln1_b, g_ln1_b),
                               ("ln2_g", ln2_g, m_ln2_g, v_ln2_g, g_ln2_g), ("ln2_b", ln2_b, m_ln2_b, v_ln2_b, g_ln2_b)]:
        sm[nm] = (g_,) + tuple(_adam_small("adam_" + nm, w_, m_, v_, g_))

    order = ["w_ada", "b_ada", "w_in", "g_q_a", "w_q_b", "g_kv_a", "w_kv_b", "w_o_a", "w_conv", "w_o_b", "w_o",
             "ln1_g", "ln1_b", "w_ffn_in", "w_ffn_out", "ln2_g", "ln2_b"]
    lead = {"b_ada", "g_q_a", "g_kv_a", "ln1_g", "ln1_b", "ln2_g", "ln2_b"}

    def leaf(nm, k):
        val = big[nm][k] if nm in big else sm[nm][k]
        if nm == "w_in":
            val = val.T
        return val if nm in lead else val[None]

    outs = [loss, grad_x[None]]
    for k in range(4):
        outs += [leaf(nm, k) for nm in order]
    return tuple(outs)
```

```python
import functools

import jax
import jax.numpy as jnp
from jax import lax
from jax.experimental import pallas as pl
from jax.experimental.pallas import tpu as pltpu

F32, BF16 = jnp.float32, jnp.bfloat16
N_HEADS, QK_NOPE, QK_ROPE, V_HEAD = 16, 128, 64, 128
Q_LORA, KV_LORA = 512, 512
QK_PAD = 256
QKV_A = 1152
CHUNK_SHIFT = 6
ATTN_SCALE = (QK_NOPE + QK_ROPE) ** -0.5
ROPE_THETA = 10000.0
ALPHA = 2.0 ** 0.25
LN_EPS, RMS_EPS = 1e-5, 1e-6
ADAM_LR, ADAM_B1, ADAM_B2, ADAM_EPS, ADAM_WD, ADAM_STEP = 0.001, 0.9, 0.999, 1e-08, 0.01, 10
ADAM_C1 = 1.0 - ADAM_B1 ** ADAM_STEP
ADAM_C2 = 1.0 - ADAM_B2 ** ADAM_STEP
VMEM_LIMIT = 56 * 1024 * 1024
MESH = pl.DeviceIdType.MESH
ANY = pl.BlockSpec(memory_space=pl.ANY)
HBM_SPEC = pl.BlockSpec(memory_space=pltpu.HBM)
SEM_SPEC = pl.BlockSpec(memory_space=pltpu.SEMAPHORE)
NT = (((1,), (1,)), ((), ()))
TN = (((0,), (0,)), ((), ()))
NN = (((1,), (0,)), ((), ()))


def _params(sem=None):
    return pltpu.CompilerParams(dimension_semantics=sem, vmem_limit_bytes=VMEM_LIMIT)


def _pick(n, cands=(1408, 1024, 512, 384, 256, 128)):
    for t in cands:
        if n % t == 0:
            return t
    return n


def _row_tile(rows, row_bytes, budget, mult=8):
    best = mult
    for t in range(mult, rows + 1, mult):
        if rows % t == 0 and t * row_bytes <= budget:
            best = t
    return best


def _tile2(rows, cols, mult=8, budget=3 << 18):
    col_tiles = [t for t in range(128, cols + 1, 128) if cols % t == 0] or [cols]
    best = None
    for tc in col_tiles:
        for tr in range(mult, rows + 1, mult):
            if rows % tr == 0 and tr * tc <= budget and (best is None or (tr * tc, tc) > (best[0] * best[1], best[1])):
                best = (tr, tc)
    assert best is not None, (rows, cols)
    return best


def _sigmoid(x):
    return jax.nn.sigmoid(x)


class _Plan:
    def __init__(self, ins, outs, sems, start, finish, aliases=None):
        self.ins, self.outs, self.sems, self.start, self.finish = list(ins), list(outs), list(sems), start, finish
        self.aliases = dict(aliases or {})

    def io_aliases(self, first_in, first_out):
        return {first_in + i: first_out + o for i, o in self.aliases.items()}


def _token_plan(token):
    return _Plan([token], [], [], lambda *a: None, lambda *a: None)


def _run_plan(plan, name, ride=None):
    n_in, n_out = len(plan.ins), len(plan.outs)
    extra = [] if ride is None else list(ride)
    aliases = plan.io_aliases(0, 0)
    for k in range(len(extra)):
        aliases[n_in + k] = n_out + k

    def body(*refs):
        ins, outs, sems = refs[:n_in], refs[n_in + len(extra):n_in + len(extra) + n_out], refs[n_in + 2 * len(extra) + n_out:]
        plan.start(ins, outs, sems)
        plan.finish(ins, outs, sems)

    return pl.pallas_call(body, name=name, out_shape=plan.outs + [jax.ShapeDtypeStruct(r.shape, r.dtype) for r in extra],
                          in_specs=[ANY] * (n_in + len(extra)), out_specs=[ANY] * (n_out + len(extra)),
                          scratch_shapes=plan.sems, input_output_aliases=aliases,
                          compiler_params=_params())(*plan.ins, *extra)


def _matmul(a, b, mode, out_dtype, name, add=None, carry=None, shards=None):
    if mode == "nn":
        (M, K), N, dims = a.shape, b.shape[-1] * (4 if shards else 1), NN
    elif mode == "nt":
        (M, K), N, dims = a.shape, b.shape[-2], NT
    else:
        (K, M), N, dims = a.shape, b.shape[1], TN
    split_n = shards and mode != "nt"
    tm = _pick(M)
    tn = _pick(N // 4) if split_n else _pick(N)
    if shards and mode == "nt":
        tk = _pick(K // 4)
    else:
        tk = K if K <= 2048 else _pick(K)
    nk = K // tk
    per = (N // 4 // tn) if split_n else (K // 4 // tk if shards else 1)
    a_spec = (pl.BlockSpec((tk, tm), lambda i, j, k: (k, i)) if mode == "tn"
              else pl.BlockSpec((tm, tk), lambda i, j, k: (i, k)))
    if shards == "b" and mode == "nn":
        b_spec = pl.BlockSpec((None, tk, tn), lambda i, j, k: (j // per, k, j % per))
    elif shards == "b":
        b_spec = pl.BlockSpec((None, tn, tk), lambda i, j, k: (k // per, j, k % per))
    else:
        b_spec = (pl.BlockSpec((tn, tk), lambda i, j, k: (j, k)) if mode == "nt"
                  else pl.BlockSpec((tk, tn), lambda i, j, k: (k, j)))
    o_spec = pl.BlockSpec((tm, tn), lambda i, j, k: (i, j))
    o_shape = (M, N)
    if shards == "o":
        o_spec, o_shape = pl.BlockSpec((None, tm, tn), lambda i, j, k: (j // per, i, j % per)), (4, M, N // 4)
    has_add = add is not None
    n_ci = len(carry.ins) if carry else 0
    n_co = len(carry.outs) if carry else 0
    n_in = 2 + has_add
    grid = (M // tm, N // tn, nk)

    def body(*refs):
        a_ref, b_ref = refs[0], refs[1]
        add_ref = refs[2] if has_add else None
        o_ref = refs[n_in + n_ci]
        acc_ref = refs[n_in + n_ci + 1 + n_co] if nk > 1 else None
        c_ins = refs[n_in:n_in + n_ci]
        c_outs = refs[n_in + n_ci + 1:n_in + n_ci + 1 + n_co]
        c_sems = refs[n_in + n_ci + 1 + n_co + (nk > 1):]
        i, j, k = pl.program_id(0), pl.program_id(1), pl.program_id(2)

        if carry:
            @pl.when((i == 0) & (j == 0) & (k == 0))
            def _():
                carry.start(c_ins, c_outs, c_sems)

        part = lax.dot_general(a_ref[...], b_ref[...], dims, preferred_element_type=F32)
        if nk == 1:
            o_ref[...] = (part + add_ref[...] if has_add else part).astype(o_ref.dtype)
        else:
            @pl.when(k == 0)
            def _():
                acc_ref[...] = part

            @pl.when((k > 0) & (k < nk - 1))
            def _():
                acc_ref[...] += part

            @pl.when(k == nk - 1)
            def _():
                r = acc_ref[...] + part
                if has_add:
                    r = r + add_ref[...]
                o_ref[...] = r.astype(o_ref.dtype)

        if carry:
            @pl.when((i == grid[0] - 1) & (j == grid[1] - 1) & (k == nk - 1))
            def _():
                carry.finish(c_ins, c_outs, c_sems)

    ins = [a, b] + ([add] if has_add else []) + (carry.ins if carry else [])
    in_specs = [a_spec, b_spec] + ([o_spec] if has_add else []) + [ANY] * n_ci
    res = pl.pallas_call(
        body, name=name, grid=grid,
        in_specs=in_specs, out_specs=[o_spec] + [ANY] * n_co,
        out_shape=[jax.ShapeDtypeStruct(o_shape, out_dtype)] + (carry.outs if carry else []),
        scratch_shapes=([pltpu.VMEM((tm, tn), F32)] if nk > 1 else []) + (carry.sems if carry else []),
        input_output_aliases=carry.io_aliases(n_in, 1) if carry else {},
        compiler_params=_params(("arbitrary",) * 3 if carry else ("parallel", "parallel", "arbitrary")),
    )(*ins)
    return (res[0], res[1:]) if carry else res[0]


def _rows(body, name, n_rows, tm, ins, outs, accs=(), carry=None):
    grid = (n_rows // tm,)
    per8 = tm // 8
    last8 = n_rows // 8 - 1
    arrays, in_specs = [], []
    for spec in ins:
        kind, arr = spec[0], spec[1]
        arrays.append(arr)
        if kind == "row":
            _, _, cb, w = spec
            in_specs.append(pl.BlockSpec((tm, w), lambda i, cb=cb: (i, cb)))
        elif kind == "full":
            in_specs.append(pl.BlockSpec(arr.shape, lambda i, nd=arr.ndim: (0,) * nd))
        elif kind == "prev":
            _, _, cb, w = spec
            in_specs.append(pl.BlockSpec((8, w), lambda i, cb=cb: (jnp.maximum(i * per8 - 1, 0), cb)))
        else:
            _, _, cb, w = spec
            in_specs.append(pl.BlockSpec((8, w), lambda i, cb=cb: (jnp.minimum((i + 1) * per8, last8), cb)))
    out_shape = [jax.ShapeDtypeStruct((n_rows, w), dt) for (w, dt) in outs]
    out_specs = [pl.BlockSpec((tm, w), lambda i: (i, 0)) for (w, _) in outs]
    out_shape += [jax.ShapeDtypeStruct(s, F32) for s in accs]
    out_specs += [pl.BlockSpec(s, lambda i, nd=len(s): (0,) * nd) for s in accs]
    n_in, n_out, n_acc = len(ins), len(outs), len(accs)
    n_ci = len(carry.ins) if carry else 0
    n_co = len(carry.outs) if carry else 0

    def kernel_body(*refs):
        first = n_in + n_ci
        c_ins, c_outs, c_sems = refs[n_in:first], refs[first + n_out + n_acc:first + n_out + n_acc + n_co], refs[first + n_out + n_acc + n_co:]
        if carry:
            @pl.when(pl.program_id(0) == 0)
            def _():
                carry.start(c_ins, c_outs, c_sems)

        body(pl.program_id(0), refs[:n_in], refs[first:first + n_out], refs[first + n_out:first + n_out + n_acc])
        if carry:
            @pl.when(pl.program_id(0) == grid[0] - 1)
            def _():
                carry.finish(c_ins, c_outs, c_sems)

    res = pl.pallas_call(
        kernel_body, name=name, grid=grid, in_specs=in_specs + [ANY] * n_ci, out_specs=out_specs + [ANY] * n_co,
        out_shape=out_shape + (carry.outs if carry else []), scratch_shapes=carry.sems if carry else [],
        input_output_aliases=carry.io_aliases(n_in, n_out + n_acc) if carry else {},
        compiler_params=_params(("arbitrary",)),
    )(*arrays, *(carry.ins if carry else []))
    return (res[:n_out + n_acc], res[n_out + n_acc:]) if carry else res


def _acc_add(i, ref, val):
    @pl.when(i == 0)
    def _():
        ref[...] = val

    @pl.when(i > 0)
    def _():
        ref[...] += val


def _rope(t, tab, sign):
    c, sa, sb = tab[:, 0:128], tab[:, 128:256], tab[:, 256:384]
    rot = pltpu.roll(t, 96, 1) * sa + pltpu.roll(t, 32, 1) * sb
    return t * c + rot if sign > 0 else t * c - rot


def _ln_stats(r):
    mu = jnp.mean(r, axis=-1, keepdims=True)
    d = r - mu
    var = jnp.mean(d * d, axis=-1, keepdims=True)
    rstd = lax.rsqrt(var + LN_EPS)
    return d * rstd, rstd


def _ln_bwd(dxh, xh, rstd):
    m1 = jnp.mean(dxh, axis=-1, keepdims=True)
    m2 = jnp.mean(dxh * xh, axis=-1, keepdims=True)
    return rstd * (dxh - m1 - xh * m2)


def _modulate(x, scale, shift, name):
    S, D = x.shape

    def body(i, ins, outs, accs):
        outs[0][...] = (ins[0][...] * (1.0 + ins[1][...]) + ins[2][...]).astype(BF16)

    return _rows(body, name, S, _pick(S, (256, 128)), [("row", x, 0, D), ("full", scale), ("full", shift)], [(D, BF16)])[0]


def _rms_fwd(pq, tab, g_q, g_kv):
    S = pq.shape[0]

    def body(i, ins, outs, accs):
        pq_ref, tab_ref, gq_ref, gkv_ref = ins

        def rms(x, g):
            return x * lax.rsqrt(jnp.mean(x * x, axis=-1, keepdims=True) + RMS_EPS) * g

        outs[0][...] = rms(pq_ref[:, 0:Q_LORA], gq_ref[...]).astype(BF16)
        outs[1][...] = rms(pq_ref[:, Q_LORA:Q_LORA + KV_LORA], gkv_ref[...]).astype(BF16)
        outs[2][...] = _rope(pq_ref[:, Q_LORA + KV_LORA:QKV_A], tab_ref[...], 1).astype(BF16)

    return _rows(body, "rms_fwd", S, _pick(S, (256, 128)),
                 [("row", pq, 0, QKV_A), ("row", tab, 0, 384), ("full", g_q), ("full", g_kv)],
                 [(Q_LORA, BF16), (KV_LORA, BF16), (128, BF16)])


def _q_rope(q, tab):
    S, W = q.shape

    def body(i, ins, outs, accs):
        q_ref, tab_ref = ins
        t = tab_ref[...]
        for h in range(N_HEADS):
            lo = h * QK_PAD
            outs[0][:, lo:lo + 128] = q_ref[:, lo:lo + 128].astype(BF16)
            outs[0][:, lo + 128:lo + 256] = _rope(q_ref[:, lo + 128:lo + 256], t, 1).astype(BF16)

    return _rows(body, "q_rope", S, _pick(S, (256, 128)), [("row", q, 0, W), ("row", tab, 0, 384)], [(W, BF16)])[0]


def _allowed(q0, k0, bq):
    row = q0 + lax.broadcasted_iota(jnp.int32, (bq, bq), 0)
    col = k0 + lax.broadcasted_iota(jnp.int32, (bq, bq), 1)
    return (col >> CHUNK_SHIFT) <= (row >> CHUNK_SHIFT)


ATTN_BLOCK = 512


def _attn_fwd(q, kv, kr, carry=None):
    S = q.shape[0]
    bq = min(ATTN_BLOCK, S)
    nq = S // bq
    n_ci = len(carry.ins) if carry else 0
    n_co = len(carry.outs) if carry else 0

    def body(*refs):
        q_ref, kn_ref, v_ref, kr_ref = refs[:4]
        o_ref, lse_ref = refs[4 + n_ci:6 + n_ci]
        c_ins, c_outs = refs[4:4 + n_ci], refs[6 + n_ci:6 + n_ci + n_co]
        kcat = refs[6 + n_ci + n_co]
        c_sems = refs[7 + n_ci + n_co:]
        qi = pl.program_id(1)
        if carry:
            @pl.when((pl.program_id(0) == 0) & (qi == 0))
            def _():
                carry.start(c_ins, c_outs, c_sems)

        @pl.when(qi == 0)
        def _():
            kcat[:, 0:128] = kn_ref[...]
            kcat[:, 128:256] = kr_ref[...]

        qv = q_ref[...]

        def step(j, carry, masked):
            m, l, acc = carry
            off = pl.multiple_of(j * bq, bq)
            s = lax.dot_general(qv, kcat[pl.ds(off, bq), :], NT, preferred_element_type=F32) * ATTN_SCALE
            if masked:
                s = jnp.where(_allowed(qi * bq, off, bq), s, -1e30)
            m_new = jnp.maximum(m, jnp.max(s, axis=1, keepdims=True))
            a = jnp.exp(m - m_new)
            p = jnp.exp(s - m_new)
            l = a * l + jnp.sum(p, axis=1, keepdims=True)
            acc = a * acc + jnp.dot(p.astype(BF16), v_ref[pl.ds(off, bq), :], preferred_element_type=F32)
            return m_new, l, acc

        init = (jnp.full((bq, 1), -1e30, F32), jnp.zeros((bq, 1), F32), jnp.zeros((bq, V_HEAD), F32))
        below = lax.fori_loop(0, qi, lambda j, cr: step(j, cr, False), init)
        m, l, acc = step(qi, below, True)
        o_ref[...] = (acc / l).astype(BF16)
        lse_ref[0] = m + jnp.log(l)
        if carry:
            @pl.when((pl.program_id(0) == N_HEADS - 1) & (qi == nq - 1))
            def _():
                carry.finish(c_ins, c_outs, c_sems)

    res = pl.pallas_call(
        body, name="attn_fwd", grid=(N_HEADS, nq),
        in_specs=[pl.BlockSpec((bq, QK_PAD), lambda h, i: (i, h)),
                  pl.BlockSpec((S, 128), lambda h, i: (0, 2 * h)),
                  pl.BlockSpec((S, 128), lambda h, i: (0, 2 * h + 1)),
                  pl.BlockSpec((S, 128), lambda h, i: (0, 0))] + [ANY] * n_ci,
        out_specs=[pl.BlockSpec((bq, V_HEAD), lambda h, i: (i, h)),
                   pl.BlockSpec((1, bq, 1), lambda h, i: (h, i, 0))] + [ANY] * n_co,
        out_shape=[jax.ShapeDtypeStruct((S, N_HEADS * V_HEAD), BF16),
                   jax.ShapeDtypeStruct((N_HEADS, S, 1), F32)] + (carry.outs if carry else []),
        scratch_shapes=[pltpu.VMEM((S, QK_PAD), BF16)] + (carry.sems if carry else []),
        input_output_aliases=carry.io_aliases(4, 2) if carry else {},
        compiler_params=_params(("arbitrary", "arbitrary")),
    )(q, kv, kv, kr, *(carry.ins if carry else []))
    return res[0], res[1], res[2:]


def _attn_bwd(q, kv, kr, do, o, lse, tab, carry=None):
    S = q.shape[0]
    bq = min(ATTN_BLOCK, S)
    nq = S // bq

    n_ci = len(carry.ins) if carry else 0
    n_co = len(carry.outs) if carry else 0

    def body(*refs):
        q_ref, kn_ref, v_ref, kr_ref, do_ref, o_ref, lse_ref, tab_ref = refs[:8]
        dq_ref, dkv_ref, dkr_ref = refs[8 + n_ci:11 + n_ci]
        dq_acc, dk_acc, dv_acc, kcat, delta = refs[11 + n_ci + n_co:16 + n_ci + n_co]
        c_ins, c_outs, c_sems = refs[8:8 + n_ci], refs[11 + n_ci:11 + n_ci + n_co], refs[16 + n_ci + n_co:]
        h = pl.program_id(0)
        if carry:
            @pl.when(h == 0)
            def _():
                carry.start(c_ins, c_outs, c_sems)

        dq_acc[...] = jnp.zeros_like(dq_acc)
        dk_acc[...] = jnp.zeros_like(dk_acc)
        dv_acc[...] = jnp.zeros_like(dv_acc)
        kcat[:, 0:128] = kn_ref[...]
        kcat[:, 128:256] = kr_ref[...]
        for r in range(nq):
            rows = slice(r * bq, (r + 1) * bq)
            delta[rows, :] = jnp.sum(do_ref[rows, :].astype(F32) * o_ref[rows, :].astype(F32), axis=1, keepdims=True)

        def pair(i, j, masked):
            rows_i = pl.ds(pl.multiple_of(i * bq, bq), bq)
            rows_j = pl.ds(pl.multiple_of(j * bq, bq), bq)
            qv, dov, k = q_ref[rows_i, :], do_ref[rows_i, :], kcat[rows_j, :]
            s = lax.dot_general(qv, k, NT, preferred_element_type=F32) * ATTN_SCALE
            if masked:
                s = jnp.where(_allowed(i * bq, j * bq, bq), s, -1e30)
            p = jnp.exp(s - lse_ref[0, rows_i, :])
            dv_acc[rows_j, :] += lax.dot_general(p.astype(BF16), dov, TN, preferred_element_type=F32)
            dp = lax.dot_general(dov, v_ref[rows_j, :], NT, preferred_element_type=F32)
            ds = (p * (dp - delta[rows_i, :]) * ATTN_SCALE).astype(BF16)
            dk_acc[rows_j, :] += lax.dot_general(ds, qv, TN, preferred_element_type=F32)
            dq_acc[rows_i, :] += jnp.dot(ds, k, preferred_element_type=F32)

        def kv_step(j, _):
            pair(j, j, True)

            def q_step(i, _):
                pair(i, j, False)
                return 0

            lax.fori_loop(j + 1, nq, q_step, 0)
            return 0

        lax.fori_loop(0, nq, kv_step, 0)

        for r in range(nq):
            rows = slice(r * bq, (r + 1) * bq)
            dq_ref[rows, 0:128] = dq_acc[rows, 0:128].astype(BF16)
            dq_ref[rows, 128:256] = _rope(dq_acc[rows, 128:256], tab_ref[rows, :], -1).astype(BF16)
        dkv_ref[:, 0:128] = dk_acc[:, 0:128].astype(BF16)
        dkv_ref[:, 128:256] = dv_acc[...].astype(BF16)

        @pl.when(h == 0)
        def _():
            dkr_ref[...] = dk_acc[:, 128:256]

        @pl.when(h > 0)
        def _():
            dkr_ref[...] += dk_acc[:, 128:256]

        @pl.when(h == N_HEADS - 1)
        def _():
            for r in range(nq):
                rows = slice(r * bq, (r + 1) * bq)
                dkr_ref[rows, :] = _rope(dkr_ref[rows, :], tab_ref[rows, :], -1)
            if carry:
                carry.finish(c_ins, c_outs, c_sems)

    W = N_HEADS * QK_PAD
    res = pl.pallas_call(
        body, name="attn_bwd", grid=(N_HEADS,),
        in_specs=[pl.BlockSpec((S, QK_PAD), lambda h: (0, h)),
                  pl.BlockSpec((S, 128), lambda h: (0, 2 * h)),
                  pl.BlockSpec((S, 128), lambda h: (0, 2 * h + 1)),
                  pl.BlockSpec((S, 128), lambda h: (0, 0)),
                  pl.BlockSpec((S, V_HEAD), lambda h: (0, h)),
                  pl.BlockSpec((S, V_HEAD), lambda h: (0, h)),
                  pl.BlockSpec((1, S, 1), lambda h: (h, 0, 0)),
                  pl.BlockSpec((S, 384), lambda h: (0, 0))] + [ANY] * n_ci,
        out_specs=[pl.BlockSpec((S, QK_PAD), lambda h: (0, h)),
                   pl.BlockSpec((S, QK_PAD), lambda h: (0, h)),
                   pl.BlockSpec((S, 128), lambda h: (0, 0))] + [ANY] * n_co,
        out_shape=[jax.ShapeDtypeStruct((S, W), BF16), jax.ShapeDtypeStruct((S, W), BF16),
                   jax.ShapeDtypeStruct((S, 128), F32)] + (carry.outs if carry else []),
        scratch_shapes=[pltpu.VMEM((S, QK_PAD), F32), pltpu.VMEM((S, QK_PAD), F32), pltpu.VMEM((S, V_HEAD), F32),
                        pltpu.VMEM((S, QK_PAD), BF16), pltpu.VMEM((S, 1), F32)]
        + (carry.sems if carry else []),
        input_output_aliases=carry.io_aliases(8, 3) if carry else {},
        compiler_params=_params(("arbitrary",)),
    )(q, kv, kv, kr, do, o, lse, tab, *(carry.ins if carry else []))
    return res[0], res[1], res[2], res[3:]


def _shift_down(cur, prev8, i, n):
    tm = cur.shape[0]
    prev8 = jnp.where(i == 0, jnp.zeros_like(prev8), prev8)
    full = jnp.concatenate([prev8, cur], axis=0)
    return pltpu.roll(full, n, 0)[8:8 + tm, :]


def _shift_up(cur, next8, i, last, n):
    tm = cur.shape[0]
    next8 = jnp.where(i == last, jnp.zeros_like(next8), next8)
    full = jnp.concatenate([cur, next8], axis=0)
    return pltpu.roll(full, tm + 8 - n, 0)[0:tm, :]


def _conv_fwd(pc, w_conv):
    S, D = pc.shape[0], pc.shape[1] // 3
    tm = _pick(S, (256, 128))

    def body(i, ins, outs, accs):
        b_ref, c_ref, x_ref, cp_ref, xp_ref, w_ref = ins
        z = c_ref[...] * x_ref[...]
        zp = cp_ref[...] * xp_ref[...]
        cz = w_ref[0:1, :] * _shift_down(z, zp, i, 2) + w_ref[1:2, :] * _shift_down(z, zp, i, 1) + w_ref[2:3, :] * z
        outs[0][...] = (b_ref[...] * cz).astype(BF16)

    return _rows(body, "conv_fwd", S, tm,
                 [("row", pc, 0, D), ("row", pc, 1, D), ("row", pc, 2, D), ("prev", pc, 1, D), ("prev", pc, 2, D),
                  ("full", w_conv)], [(D, BF16)])[0]


def _conv_bwd(dhb, pc, w_conv):
    S, D = dhb.shape
    tm = _pick(S, (256, 128))
    last = S // tm - 1

    def body(i, ins, outs, accs):
        g_ref, b_ref, c_ref, x_ref, cp_ref, xp_ref, gn_ref, bn_ref, w_ref = ins
        w0, w1, w2 = w_ref[0:1, :], w_ref[1:2, :], w_ref[2:3, :]
        c, x, g = c_ref[...], x_ref[...], g_ref[...]
        z = c * x
        zp = cp_ref[...] * xp_ref[...]
        z1, z2 = _shift_down(z, zp, i, 1), _shift_down(z, zp, i, 2)
        cz = w0 * z2 + w1 * z1 + w2 * z
        dcz = g * b_ref[...]
        dczn = gn_ref[...] * bn_ref[...]
        dz = w2 * dcz + w1 * _shift_up(dcz, dczn, i, last, 1) + w0 * _shift_up(dcz, dczn, i, last, 2)
        outs[0][:, 0:D] = (g * cz).astype(BF16)
        outs[0][:, D:2 * D] = (dz * x).astype(BF16)
        outs[0][:, 2 * D:3 * D] = (dz * c).astype(BF16)
        dw = jnp.concatenate([jnp.sum(dcz * z2, axis=0, keepdims=True), jnp.sum(dcz * z1, axis=0, keepdims=True),
                              jnp.sum(dcz * z, axis=0, keepdims=True)], axis=0)
        _acc_add(i, accs[0], dw)

    return _rows(body, "conv_bwd", S, tm,
                 [("row", dhb, 0, D), ("row", pc, 0, D), ("row", pc, 1, D), ("row", pc, 2, D),
                  ("prev", pc, 1, D), ("prev", pc, 2, D), ("next", dhb, 0, D), ("next", pc, 0, D), ("full", w_conv)],
                 [(3 * D, BF16)], [(3, D)])


def _merge_fwd(y_a, y_b, pg):
    S, D = y_a.shape

    def body(i, ins, outs, accs):
        ya, yb, ga, gb = ins
        outs[0][...] = (_sigmoid(ga[...].astype(F32)) * ya[...] + _sigmoid(gb[...].astype(F32)) * yb[...]).astype(BF16)

    return _rows(body, "merge_fwd", S, _pick(S, (256, 128)),
                 [("row", y_a, 0, D), ("row", y_b, 0, D), ("row", pg, 0, D), ("row", pg, 1, D)], [(D, BF16)])[0]


def _merge_bwd(dm, y_a, y_b, pg):
    S, D = dm.shape

    def body(i, ins, outs, accs):
        d, ya, yb = ins[0][...], ins[1][...], ins[2][...]
        sa, sb = _sigmoid(ins[3][...].astype(F32)), _sigmoid(ins[4][...].astype(F32))
        outs[0][...] = (d * sa).astype(BF16)
        outs[1][...] = (d * sb).astype(BF16)
        outs[2][:, 0:D] = (d * ya * (sa * (1.0 - sa))).astype(BF16)
        outs[2][:, D:2 * D] = (d * yb * (sb * (1.0 - sb))).astype(BF16)

    return _rows(body, "merge_bwd", S, _pick(S, (256, 128)),
                 [("row", dm, 0, D), ("row", y_a, 0, D), ("row", y_b, 0, D), ("row", pg, 0, D), ("row", pg, 1, D)],
                 [(D, BF16), (D, BF16), (2 * D, BF16)])


def _ln1_fwd(x, mix, gate1, g, b, scale2, shift2):
    S, D = x.shape

    def body(i, ins, outs, accs):
        x_ref, mix_ref, gate_ref, g_ref, b_ref, sc_ref, sh_ref = ins
        xh, _ = _ln_stats(ALPHA * x_ref[...] + gate_ref[...] * mix_ref[...])
        x1 = xh * g_ref[...] + b_ref[...]
        outs[0][...] = x1
        outs[1][...] = (x1 * (1.0 + sc_ref[...]) + sh_ref[...]).astype(BF16)

    return _rows(body, "ln1_fwd", S, _pick(S, (256, 128)),
                 [("row", x, 0, D), ("row", mix, 0, D), ("full", gate1), ("full", g), ("full", b),
                  ("full", scale2), ("full", shift2)], [(D, F32), (D, BF16)])


def _swiglu_fwd(hh, carry=None):
    S, F = hh.shape[0], hh.shape[1] // 2

    def body(i, ins, outs, accs):
        hg = ins[0][...].astype(F32)
        outs[0][...] = (hg * _sigmoid(hg) * ins[1][...].astype(F32)).astype(BF16)

    res = _rows(body, "swiglu_fwd", S, _pick(S, (128,)), [("row", hh, 0, F), ("row", hh, 1, F)], [(F, BF16)], carry=carry)
    return (res[0][0], res[1]) if carry else res[0]


def _swiglu_bwd(dact, hh):
    S, F = dact.shape

    def body(i, ins, outs, accs):
        d, hg, hu = ins[0][...].astype(F32), ins[1][...].astype(F32), ins[2][...].astype(F32)
        sg = _sigmoid(hg)
        outs[0][:, 0:F] = (d * hu * (sg * (1.0 + hg * (1.0 - sg)))).astype(BF16)
        outs[0][:, F:2 * F] = (d * (hg * sg)).astype(BF16)

    return _rows(body, "swiglu_bwd", S, _pick(S, (128,)),
                 [("row", dact, 0, F), ("row", hh, 0, F), ("row", hh, 1, F)], [(2 * F, BF16)])[0]


def _ln2_loss_bwd(x1, ffn, gate2, g, b, target):
    S, D = x1.shape

    def body(i, ins, outs, accs):
        x1_ref, f_ref, gate_ref, g_ref, b_ref, t_ref = ins
        f = f_ref[...]
        xh, rstd = _ln_stats(ALPHA * x1_ref[...] + gate_ref[...] * f)
        e = xh * g_ref[...] + b_ref[...] - t_ref[...]
        dy = e * (1.0 / D)
        dr = _ln_bwd(dy * g_ref[...], xh, rstd)
        outs[0][...] = (gate_ref[...] * dr).astype(BF16)
        outs[1][...] = ALPHA * dr
        _acc_add(i, accs[0], jnp.full((1, 128), (0.5 / D) * jnp.sum(e * e), F32))
        _acc_add(i, accs[1], jnp.sum(dy * xh, axis=0, keepdims=True))
        _acc_add(i, accs[2], jnp.sum(dy, axis=0, keepdims=True))
        _acc_add(i, accs[3], jnp.sum(dr * f, axis=0, keepdims=True))

    return _rows(body, "ln2_loss_bwd", S, _pick(S, (256, 128)),
                 [("row", x1, 0, D), ("row", ffn, 0, D), ("full", gate2), ("full", g), ("full", b), ("row", target, 0, D)],
                 [(D, BF16), (D, F32)], [(1, 128), (1, D), (1, D), (1, D)])


def _ln1_bwd(x, mix, dx1a, du2, gate1, g, b, scale2):
    S, D = x.shape

    def body(i, ins, outs, accs):
        x_ref, mix_ref, da_ref, du_ref, gate_ref, g_ref, b_ref, sc_ref = ins
        mix, du = mix_ref[...], du_ref[...]
        xh, rstd = _ln_stats(ALPHA * x_ref[...] + gate_ref[...] * mix)
        x1 = xh * g_ref[...] + b_ref[...]
        dx1 = da_ref[...] + du * (1.0 + sc_ref[...])
        dr = _ln_bwd(dx1 * g_ref[...], xh, rstd)
        outs[0][...] = (gate_ref[...] * dr).astype(BF16)
        outs[1][...] = ALPHA * dr
        _acc_add(i, accs[0], jnp.sum(du, axis=0, keepdims=True))
        _acc_add(i, accs[1], jnp.sum(du * x1, axis=0, keepdims=True))
        _acc_add(i, accs[2], jnp.sum(dx1 * xh, axis=0, keepdims=True))
        _acc_add(i, accs[3], jnp.sum(dx1, axis=0, keepdims=True))
        _acc_add(i, accs[4], jnp.sum(dr * mix, axis=0, keepdims=True))

    return _rows(body, "ln1_bwd", S, _pick(S, (256, 128)),
                 [("row", x, 0, D), ("row", mix, 0, D), ("row", dx1a, 0, D), ("row", du2, 0, D),
                  ("full", gate1), ("full", g), ("full", b), ("full", scale2)],
                 [(D, BF16), (D, F32)], [(1, D)] * 5)


def _rms_bwd(d_rq, d_rkv, pq, dkr, g_q, g_kv):
    S = pq.shape[0]

    def body(i, ins, outs, accs):
        dq_ref, dkv_ref, pq_ref, dkr_ref, gq_ref, gkv_ref = ins

        def rms_bwd(dy, x, g):
            r = lax.rsqrt(jnp.mean(x * x, axis=-1, keepdims=True) + RMS_EPS)
            dyg = dy * g
            dx = r * dyg - x * (r * r * r) * jnp.mean(dyg * x, axis=-1, keepdims=True)
            return dx, jnp.sum(dy * (x * r), axis=0, keepdims=True)

        dxq, dgq = rms_bwd(dq_ref[...], pq_ref[:, 0:Q_LORA], gq_ref[...])
        dxkv, dgkv = rms_bwd(dkv_ref[...], pq_ref[:, Q_LORA:Q_LORA + KV_LORA], gkv_ref[...])
        outs[0][:, 0:Q_LORA] = dxq.astype(BF16)
        outs[0][:, Q_LORA:Q_LORA + KV_LORA] = dxkv.astype(BF16)
        outs[0][:, Q_LORA + KV_LORA:QKV_A] = dkr_ref[...].astype(BF16)
        _acc_add(i, accs[0], dgq)
        _acc_add(i, accs[1], dgkv)

    return _rows(body, "rms_bwd", S, _pick(S, (256, 128)),
                 [("row", d_rq, 0, Q_LORA), ("row", d_rkv, 0, KV_LORA), ("row", pq, 0, QKV_A), ("row", dkr, 0, 128),
                  ("full", g_q), ("full", g_kv)], [(QKV_A, BF16)], [(1, Q_LORA), (1, KV_LORA)])


def _dx_final(dxa, du, x, scale1):
    S, D = x.shape

    def body(i, ins, outs, accs):
        du = ins[1][...]
        outs[0][...] = ins[0][...] + du * (1.0 + ins[3][...])
        _acc_add(i, accs[0], jnp.sum(du, axis=0, keepdims=True))
        _acc_add(i, accs[1], jnp.sum(du * ins[2][...], axis=0, keepdims=True))

    return _rows(body, "dx_final", S, _pick(S, (256, 128)),
                 [("row", dxa, 0, D), ("row", du, 0, D), ("row", x, 0, D), ("full", scale1)],
                 [(D, F32)], [(1, D), (1, D)])


def _ada_fwd(c_all, w, bias):
    B, D = c_all.shape
    NA = w.shape[1]
    tn = _pick(NA, (512, 256, 128))

    def body(c_ref, w_ref, b_ref, o_ref):
        cv = c_ref[...]
        ca = (cv * _sigmoid(cv)).astype(BF16)
        o_ref[...] = jnp.dot(ca, w_ref[...].astype(BF16), preferred_element_type=F32) + b_ref[...]

    return pl.pallas_call(
        body, name="ada_fwd", grid=(NA // tn,),
        in_specs=[pl.BlockSpec((B, D), lambda j: (0, 0)), pl.BlockSpec((D, tn), lambda j: (0, j)),
                  pl.BlockSpec((1, tn), lambda j: (0, j))],
        out_specs=pl.BlockSpec((B, tn), lambda j: (0, j)),
        out_shape=jax.ShapeDtypeStruct((B, NA), F32),
        compiler_params=_params(("arbitrary",)),
    )(c_all, w, bias)


def _ada_bwd(c_all, dmod):
    B, D = c_all.shape
    NA = dmod.shape[1]
    tn = _pick(NA, (512, 256, 128))

    def body(c_ref, d_ref, o_ref):
        cv = c_ref[...]
        ca = (cv * _sigmoid(cv)).astype(BF16)
        o_ref[...] = lax.dot_general(ca, d_ref[...].astype(BF16), TN, preferred_element_type=F32)

    return pl.pallas_call(
        body, name="ada_bwd", grid=(NA // tn,),
        in_specs=[pl.BlockSpec((B, D), lambda j: (0, 0)), pl.BlockSpec((B, tn), lambda j: (0, j))],
        out_specs=pl.BlockSpec((D, tn), lambda j: (0, j)),
        out_shape=jax.ShapeDtypeStruct((D, NA), F32),
        compiler_params=_params(("arbitrary",)),
    )(c_all, dmod)


def _pack_rows(parts, n_rows, after=()):
    N = parts[0].shape[1]
    n = len(parts)

    def body(*refs):
        o_ref = refs[-1]
        o_ref[...] = jnp.zeros_like(o_ref)
        at = 0
        for r in refs[:n]:
            o_ref[at:at + r.shape[0], :] = r[...]
            at += r.shape[0]

    vmem = pl.BlockSpec(memory_space=pltpu.VMEM)
    return pl.pallas_call(body, name="pack_small", out_shape=jax.ShapeDtypeStruct((n_rows, N), F32),
                          in_specs=[vmem] * n + [ANY] * len(after), out_specs=vmem,
                          compiler_params=_params())(*parts, *after)


def _sum8(parts):
    _, R, N = parts.shape

    def body(p_ref, o_ref):
        acc = p_ref[0]
        for d in range(1, 8):
            acc = acc + p_ref[d]
        o_ref[...] = acc

    return pl.pallas_call(body, name="sum8", out_shape=jax.ShapeDtypeStruct((R, N), F32),
                          compiler_params=_params())(parts)


def _adam_math(w, g, m, v):
    m = ADAM_B1 * m + (1.0 - ADAM_B1) * g
    v = ADAM_B2 * v + (1.0 - ADAM_B2) * (g * g)
    delta = -ADAM_LR * ((m / ADAM_C1) / (jnp.sqrt(v / ADAM_C2) + ADAM_EPS) + ADAM_WD * w)
    return delta, m, v


def _adam(name, w, m, v, g, carry=None):
    R, C = w.shape
    tm = _row_tile(R, C * 4, 1 << 20)
    steps = R // tm
    n_ci = len(carry.ins) if carry else 0
    n_co = len(carry.outs) if carry else 0

    def body(*refs):
        w_ref, m_ref, v_ref, g_ref = refs[:4]
        d_ref, nm_ref, nv_ref = refs[4 + n_ci:7 + n_ci]
        c_ins, c_outs, c_sems = refs[4:4 + n_ci], refs[7 + n_ci:7 + n_ci + n_co], refs[7 + n_ci + n_co:]
        if carry:
            @pl.when(pl.program_id(0) == 0)
            def _():
                carry.start(c_ins, c_outs, c_sems)

        delta, nm, nv = _adam_math(w_ref[...], g_ref[...], m_ref[...], v_ref[...])
        d_ref[...] = delta
        nm_ref[...] = nm
        nv_ref[...] = nv
        if carry:
            @pl.when(pl.program_id(0) == steps - 1)
            def _():
                carry.finish(c_ins, c_outs, c_sems)

    spec = pl.BlockSpec((tm, C), lambda i: (i, 0))
    res = pl.pallas_call(
        body, name=name, grid=(steps,), in_specs=[spec] * 4 + [ANY] * n_ci, out_specs=[spec] * 3 + [ANY] * n_co,
        out_shape=[jax.ShapeDtypeStruct((R, C), F32)] * 3 + (carry.outs if carry else []),
        scratch_shapes=carry.sems if carry else [],
        input_output_aliases=carry.io_aliases(4, 3) if carry else {},
        compiler_params=_params(("arbitrary",)),
    )(w, m, v, g, *(carry.ins if carry else []))
    return (res[:3], res[3:]) if carry else res


def _adam_halves(name, w, m, v, mine, other, core, carry=None):
    R, C = w.shape
    Rh = mine.shape[0]
    tc = max(t for t in range(128, C + 1, 128) if C % t == 0 and R * t <= (3 << 17))
    steps = C // tc
    n_ci = len(carry.ins) if carry else 0
    n_co = len(carry.outs) if carry else 0

    def body(*refs):
        c_ref, w_ref, m_ref, v_ref, a_ref, b_ref = refs[:6]
        g_ref, d_ref, nm_ref, nv_ref = refs[6 + n_ci:10 + n_ci]
        c_ins, c_outs, c_sems = refs[6:6 + n_ci], refs[10 + n_ci:10 + n_ci + n_co], refs[10 + n_ci + n_co:]
        if carry:
            @pl.when(pl.program_id(0) == 0)
            def _():
                carry.start(c_ins, c_outs, c_sems)

        first = c_ref[0] == 0
        g = jnp.concatenate([jnp.where(first, a_ref[...], b_ref[...]),
                             jnp.where(first, b_ref[0:R - Rh, :], a_ref[0:R - Rh, :])], axis=0)
        delta, nm, nv = _adam_math(w_ref[...], g, m_ref[...], v_ref[...])
        g_ref[...] = g
        d_ref[...] = delta
        nm_ref[...] = nm
        nv_ref[...] = nv
        if carry:
            @pl.when(pl.program_id(0) == steps - 1)
            def _():
                carry.finish(c_ins, c_outs, c_sems)

    spec = pl.BlockSpec((R, tc), lambda i, c_ref: (0, i))
    h_spec = pl.BlockSpec((Rh, tc), lambda i, c_ref: (0, i))
    res = pl.pallas_call(
        body, name=name, out_shape=[jax.ShapeDtypeStruct((R, C), F32)] * 4 + (carry.outs if carry else []),
        grid_spec=pltpu.PrefetchScalarGridSpec(
            num_scalar_prefetch=1, grid=(steps,), in_specs=[spec, spec, spec, h_spec, h_spec] + [ANY] * n_ci,
            out_specs=[spec] * 4 + [ANY] * n_co, scratch_shapes=carry.sems if carry else []),
        input_output_aliases=carry.io_aliases(6, 4) if carry else {},
        compiler_params=_params(("arbitrary",)),
    )(core, w, m, v, mine, other, *(carry.ins if carry else []))
    return (res[:4], res[4:]) if carry else res


def _adam_small(name, w, m, v, g):
    def body(w_ref, m_ref, v_ref, g_ref, d_ref, nm_ref, nv_ref):
        delta, nm, nv = _adam_math(w_ref[...], g_ref[...], m_ref[...], v_ref[...])
        d_ref[...] = delta
        nm_ref[...] = nm
        nv_ref[...] = nv

    return pl.pallas_call(body, name=name, out_shape=[jax.ShapeDtypeStruct(w.shape, F32)] * 3,
                          compiler_params=_params())(w, m, v, g)


def _place():
    return lax.axis_index("x"), lax.axis_index("y"), lax.axis_index("c")


def _other_chips(x, y):
    return [(1 - x, y), (x, 1 - y), (1 - x, 1 - y)]


def _all_gather8(blk, name):
    R, N = blk.shape

    def body(x_ref, out_ref, send_sems, recv_sems, local_sem):
        x, y, c = _place()
        me = 4 * x + 2 * y + c
        mine = pltpu.make_async_copy(x_ref, out_ref.at[me], local_sem)
        mine.start()
        flips = [(j >> 2 & 1, j >> 1 & 1, j & 1) for j in range(1, 8)]
        peers = [((1 - x) if fx else x, (1 - y) if fy else y, (1 - c) if fc else c) for fx, fy, fc in flips]
        sends = []
        for j, peer in enumerate(peers):
            cp = pltpu.make_async_remote_copy(src_ref=x_ref, dst_ref=out_ref.at[me], send_sem=send_sems.at[j],
                                              recv_sem=recv_sems.at[j], device_id=peer, device_id_type=MESH)
            cp.start()
            sends.append(cp)
        for j, (px, py, pc) in enumerate(peers):
            pltpu.make_async_remote_copy(src_ref=x_ref, dst_ref=out_ref.at[4 * px + 2 * py + pc],
                                         send_sem=send_sems.at[j], recv_sem=recv_sems.at[j],
                                         device_id=(px, py, pc), device_id_type=MESH).wait_recv()
        for cp in sends:
            cp.wait_send()
        mine.wait()

    return pl.pallas_call(
        body, name=name, out_shape=jax.ShapeDtypeStruct((8, R, N), F32),
        in_specs=[pl.BlockSpec(memory_space=pltpu.VMEM)], out_specs=pl.BlockSpec(memory_space=pltpu.VMEM),
        scratch_shapes=[pltpu.SemaphoreType.DMA((7,)), pltpu.SemaphoreType.DMA((7,)), pltpu.SemaphoreType.DMA],
        compiler_params=_params(),
    )(blk)


def _piece(rows, piece):
    i, n, k = piece if len(piece) == 3 else (piece[0], piece[1], 1)
    assert rows % 16 == 0 and rows // 16 >= n, (rows, piece)
    lo, hi = (rows // 16 * i // n) * 16, (rows // 16 * (i + k) // n) * 16
    return pl.ds(lo, hi - lo)


def _scatter_plan(arrs, piece=(0, 1), into=None):
    n = len(arrs)

    def copies(ins, outs, sems):
        send_sems, recv_sems = sems
        x, y, c = _place()
        chips = _other_chips(x, y)
        cps = []
        for k in range(n):
            rows = _piece(arrs[k].shape[1], piece)
            for j, (px, py) in enumerate(chips):
                cps.append(pltpu.make_async_remote_copy(
                    src_ref=ins[k].at[2 * px + py, rows], dst_ref=outs[k].at[j, rows],
                    send_sem=send_sems.at[3 * k + j], recv_sem=recv_sems.at[3 * k + j],
                    device_id=(px, py, c), device_id_type=MESH))
        return cps

    def start(ins, outs, sems):
        for cp in copies(ins, outs, sems):
            cp.start()

    def finish(ins, outs, sems):
        for cp in copies(ins, outs, sems):
            cp.wait()

    return _Plan(list(arrs) + list(into or []), [jax.ShapeDtypeStruct((3,) + a.shape[1:], a.dtype) for a in arrs],
                 [pltpu.SemaphoreType.DMA((3 * n,))] * 2, start, finish,
                 aliases={n + k: k for k in range(n)} if into else None)


def _gather_plan(shards, piece=(0, 1), into=None, ici=True):
    n = len(shards)

    def parts(ins, outs, sems):
        s1, r1, s2, r2, loc = sems
        x, y, c = _place()
        me = 2 * x + y
        chips = _other_chips(x, y)
        sib = (x, y, 1 - c)

        def rows(k):
            return _piece(shards[k].shape[1], piece)

        def ici_copy(k, j, slab, to):
            return pltpu.make_async_remote_copy(src_ref=ins[k].at[c, rows(k)], dst_ref=outs[k].at[slab, c, rows(k)],
                                                send_sem=s1.at[3 * k + j], recv_sem=r1.at[3 * k + j],
                                                device_id=to, device_id_type=MESH)

        def d2d(k, j, slab, half):
            return pltpu.make_async_remote_copy(src_ref=outs[k].at[slab, half, rows(k)],
                                                dst_ref=outs[k].at[slab, half, rows(k)],
                                                send_sem=s2.at[3 * k + j], recv_sem=r2.at[3 * k + j],
                                                device_id=sib, device_id_type=MESH)

        def own(k):
            return pltpu.make_async_remote_copy(src_ref=ins[k].at[:, rows(k)], dst_ref=outs[k].at[me, :, rows(k)],
                                                send_sem=loc.at[2 * k], recv_sem=loc.at[2 * k + 1],
                                                device_id=sib, device_id_type=MESH)

        return c, me, chips, ici_copy, d2d, own

    def start(ins, outs, sems):
        c, me, chips, ici_copy, d2d, own = parts(ins, outs, sems)
        for k in range(n):
            for j, (px, py) in enumerate(chips):
                (ici_copy(k, j, me, (px, py, c)) if ici else d2d(k, j, 2 * px + py, c)).start()
        for k in range(n):
            own(k).start()

    def finish(ins, outs, sems):
        c, me, chips, ici_copy, d2d, own = parts(ins, outs, sems)
        if ici:
            for k in range(n):
                for j, (px, py) in enumerate(chips):
                    ici_copy(k, j, 2 * px + py, (px, py, c)).wait_recv()
                    d2d(k, j, 2 * px + py, c).start()
        for k in range(n):
            for j, (px, py) in enumerate(chips):
                d2d(k, j, 2 * px + py, 1 - c).wait_recv()
        for k in range(n):
            own(k).wait()
            for j, (px, py) in enumerate(chips):
                if ici:
                    ici_copy(k, j, me, (px, py, c)).wait_send()
                d2d(k, j, 2 * px + py, c).wait_send()

    return _Plan(list(shards) + list(into or []), [jax.ShapeDtypeStruct((4,) + a.shape, a.dtype) for a in shards],
                 [pltpu.SemaphoreType.DMA((3 * n,))] * 4 + [pltpu.SemaphoreType.DMA((2 * n,))], start, finish,
                 aliases={n + k: k for k in range(n)} if into else None)


def _pair_plan(parts):
    n = len(parts)

    def copies(ins, outs, sems):
        send_sems, recv_sems = sems
        x, y, c = _place()
        return [pltpu.make_async_remote_copy(src_ref=ins[k].at[p, 1 - c], dst_ref=outs[k].at[p],
                                             send_sem=send_sems.at[4 * k + p], recv_sem=recv_sems.at[4 * k + p],
                                             device_id=(x, y, 1 - c), device_id_type=MESH)
                for k in range(n) for p in range(4)]

    def start(ins, outs, sems):
        for cp in copies(ins, outs, sems):
            cp.start()

    def finish(ins, outs, sems):
        for cp in copies(ins, outs, sems):
            cp.wait()

    return _Plan(parts, [jax.ShapeDtypeStruct((4,) + a.shape[2:], a.dtype) for a in parts],
                 [pltpu.SemaphoreType.DMA((4 * n,))] * 2, start, finish)


def _sibling_plan(arrs):
    n = len(arrs)

    def copies(ins, outs, sems):
        send_sems, recv_sems = sems
        x, y, c = _place()
        return [pltpu.make_async_remote_copy(src_ref=ins[k], dst_ref=outs[k], send_sem=send_sems.at[k],
                                             recv_sem=recv_sems.at[k], device_id=(x, y, 1 - c), device_id_type=MESH)
                for k in range(n)]

    def start(ins, outs, sems):
        for cp in copies(ins, outs, sems):
            cp.start()

    def finish(ins, outs, sems):
        for cp in copies(ins, outs, sems):
            cp.wait()

    return _Plan(arrs, [jax.ShapeDtypeStruct(a.shape, a.dtype) for a in arrs],
                 [pltpu.SemaphoreType.DMA((n,))] * 2, start, finish)


def _scatter_copies(arrs):
    def copies(ins, land, send_sems, recv_sems):
        x, y, c = _place()
        return [pltpu.make_async_remote_copy(src_ref=ins[k].at[2 * px + py], dst_ref=land[k].at[j],
                                             send_sem=send_sems.at[3 * k + j], recv_sem=recv_sems.at[3 * k + j],
                                             device_id=(px, py, c), device_id_type=MESH)
                for k in range(len(arrs)) for j, (px, py) in enumerate(_other_chips(x, y))]

    return copies, [lax.empty((3,) + a.shape[1:], a.dtype) for a in arrs]


def _gather_copies(shards):
    def copies(ins, land, send_sems, recv_sems):
        x, y, c = _place()
        return [pltpu.make_async_remote_copy(src_ref=ins[k].at[c], dst_ref=land[k].at[2 * x + y, c],
                                             send_sem=send_sems.at[3 * k + j], recv_sem=recv_sems.at[3 * k + j],
                                             device_id=(px, py, c), device_id_type=MESH)
                for k in range(len(shards)) for j, (px, py) in enumerate(_other_chips(x, y))]

    return copies, [lax.empty((4,) + a.shape, a.dtype) for a in shards]


def _split_start(arrs, copies_lands, ride, name, after=()):
    copies, lands = copies_lands
    n = len(arrs)
    rides = list(ride) if isinstance(ride, (list, tuple)) else [ride]
    n_thru = 2 * n + len(rides)

    def body(*refs):
        first_out = n_thru + len(after)
        for cp in copies(refs[:n], refs[n:2 * n], refs[first_out], refs[first_out + 1]):
            cp.start()

    hbm = [pltpu.with_memory_space_constraint(a, pltpu.HBM) for a in list(arrs) + lands + rides]
    res = pl.pallas_call(
        body, name=name,
        out_shape=[pltpu.SemaphoreType.DMA((3 * n,)), pltpu.SemaphoreType.DMA((3 * n,))]
        + [pltpu.HBM(a.shape, a.dtype) for a in hbm],
        in_specs=[HBM_SPEC] * n_thru + [ANY] * len(after),
        out_specs=[SEM_SPEC, SEM_SPEC] + [HBM_SPEC] * n_thru,
        input_output_aliases={i: 2 + i for i in range(n_thru)},
        compiler_params=pltpu.CompilerParams(has_side_effects=pltpu.SideEffectType.DATAFLOW_SIDE_EFFECTING),
    )(*hbm, *after)
    return res[0], res[1], res[2:2 + n], res[2 + n:2 + 2 * n], list(res[2 + 2 * n:])


def _split_wait(started, copies_lands, after, name):
    send_sems, recv_sems, arrs, lands, _ = started
    copies = copies_lands[0]
    n = len(arrs)

    def body(*refs):
        for cp in copies(refs[:n], refs[n:2 * n], refs[2 * n], refs[2 * n + 1]):
            cp.wait_send()
            cp.wait_recv()

    res = pl.pallas_call(
        body, name=name, out_shape=[pltpu.HBM(a.shape, a.dtype) for a in list(arrs) + list(lands)],
        in_specs=[HBM_SPEC] * (2 * n) + [SEM_SPEC, SEM_SPEC] + [ANY] * len(after), out_specs=[HBM_SPEC] * (2 * n),
        input_output_aliases={i: i for i in range(2 * n)},
        compiler_params=pltpu.CompilerParams(has_side_effects=pltpu.SideEffectType.DATAFLOW_SIDE_EFFECTING),
    )(*arrs, *lands, send_sems, recv_sems, *after)
    return list(res[:n]), list(res[n:])


def _join_plans(plans):
    def split(seq, counts):
        out, at = [], 0
        for cnt in counts:
            out.append(seq[at:at + cnt])
            at += cnt
        return out

    n_i, n_o, n_s = ([len(getattr(p, f)) for p in plans] for f in ("ins", "outs", "sems"))

    def start(ins, outs, sems):
        for p, i, o, s in zip(plans, split(ins, n_i), split(outs, n_o), split(sems, n_s)):
            p.start(i, o, s)

    def finish(ins, outs, sems):
        for p, i, o, s in zip(plans, split(ins, n_i), split(outs, n_o), split(sems, n_s)):
            p.finish(i, o, s)

    aliases, at_i, at_o = {}, 0, 0
    for p in plans:
        aliases.update(p.io_aliases(at_i, at_o))
        at_i, at_o = at_i + len(p.ins), at_o + len(p.outs)
    return _Plan(sum((p.ins for p in plans), []), sum((p.outs for p in plans), []), sum((p.sems for p in plans), []),
                 start, finish, aliases)


def _add_pair(parts, sib, core, name):
    P4, _, Rh, C = parts.shape
    tm, tc = _tile2(Rh, C, 16)

    def body(c_ref, a_ref, b_ref, o_ref):
        o_ref[...] = (a_ref[0].astype(F32) + b_ref[...].astype(F32)).astype(BF16)

    spec = pl.BlockSpec((1, tm, tc), lambda p, i, j, c_ref: (p, i, j))
    return pl.pallas_call(
        body, name=name, out_shape=jax.ShapeDtypeStruct((P4, Rh, C), BF16),
        grid_spec=pltpu.PrefetchScalarGridSpec(
            num_scalar_prefetch=1, grid=(P4, Rh // tm, C // tc),
            in_specs=[pl.BlockSpec((1, 1, tm, tc), lambda p, i, j, c_ref: (p, c_ref[0], i, j)), spec], out_specs=spec),
        compiler_params=_params(("parallel",) * 3),
    )(core, parts, sib)


def _sum_slabs(pre, recv, chip, name):
    _, Rh, C = pre.shape
    tm, tc = _tile2(Rh, C, 16)

    def body(me_ref, own_ref, r_ref, o_ref):
        acc = own_ref[0].astype(F32)
        for j in range(3):
            acc = acc + r_ref[j].astype(F32)
        o_ref[...] = acc

    return pl.pallas_call(
        body, name=name, out_shape=jax.ShapeDtypeStruct((Rh, C), F32),
        grid_spec=pltpu.PrefetchScalarGridSpec(
            num_scalar_prefetch=1, grid=(Rh // tm, C // tc),
            in_specs=[pl.BlockSpec((1, tm, tc), lambda i, j, me_ref: (me_ref[0], i, j)),
                      pl.BlockSpec((3, tm, tc), lambda i, j, me_ref: (0, i, j))],
            out_specs=pl.BlockSpec((tm, tc), lambda i, j, me_ref: (i, j))),
        compiler_params=_params(("parallel", "parallel")),
    )(chip, pre, recv)


def kernel(x, c, positions, w_ada, b_ada, w_in, g_q_a, w_q_b, g_kv_a, w_kv_b, w_o_a, w_conv, w_o_b, w_o, ln1_g, ln1_b, w_ffn_in, w_ffn_out, ln2_g, ln2_b, loss_target, m_w_ada, m_b_ada, m_w_in, m_g_q_a, m_w_q_b, m_g_kv_a, m_w_kv_b, m_w_o_a, m_w_conv, m_w_o_b, m_w_o, m_ln1_g, m_ln1_b, m_w_ffn_in, m_w_ffn_out, m_ln2_g, m_ln2_b, v_w_ada, v_b_ada, v_w_in, v_g_q_a, v_w_q_b, v_g_kv_a, v_w_kv_b, v_w_o_a, v_w_conv, v_w_o_b, v_w_o, v_ln1_g, v_ln1_b, v_w_ffn_in, v_w_ffn_out, v_ln2_g, v_ln2_b):
    S, D = x.shape[1], x.shape[2]
    F = w_ffn_out.shape[1] * 4
    ax, ay, ac = _place()
    chip = 2 * ax + ay
    dev = 4 * ax + 2 * ay + ac
    x2, tgt = x[0], loss_target[0]
    w_ada2, w_in2, w_q_b2, w_kv_b2 = w_ada[0], w_in[0], w_q_b[0], w_kv_b[0]
    w_o_a2, w_o_b2, w_o2, w_ffn_in2, w_ffn_out2 = w_o_a[0], w_o_b[0], w_o[0], w_ffn_in[0], w_ffn_out[0]
    NA = w_ada2.shape[1]
    CW = w_conv.shape[2]

    inv_freq = 1.0 / (ROPE_THETA ** (jnp.arange(0, QK_ROPE, 2, dtype=F32) / QK_ROPE))
    ang = positions[0].astype(F32)[:, None] * inv_freq
    cos, sin = jnp.cos(ang), jnp.sin(ang)
    z32, z64, z96 = jnp.zeros((S, 32), F32), jnp.zeros((S, 64), F32), jnp.zeros((S, 96), F32)
    tab = jnp.concatenate([cos, cos, z64, -sin, z96, z32, sin, z64], axis=1)

    def halves(a):
        return a.reshape(2, a.shape[0] // 2, a.shape[1])

    def whole(g):
        return g.reshape(4, 2 * g.shape[2], g.shape[3])

    def cols(g):
        return jnp.transpose(g, (1, 0, 2)).reshape(g.shape[1], 4 * g.shape[2])

    w_inT, m_w_inT, v_w_inT = w_in2.T, m_w_in[0].T, v_w_in[0].T
    CS = w_inT.shape[0]
    CSP = -(-CS // 32) * 32
    sh_in = halves(jnp.pad(w_inT.astype(BF16), ((0, CSP - CS), (0, 0))))
    sh_qb, sh_kvb, sh_oa, sh_ob, sh_o, sh_fi, sh_fo = (
        halves(w.astype(BF16)) for w in (w_q_b2, w_kv_b2, w_o_a2, w_o_b2, w_o2, w_ffn_in2, w_ffn_out2))
    c_all = _all_gather8(c, "gather_c").reshape(8, D)
    wconv_all = _all_gather8(w_conv[0], "gather_wconv")
    w_conv_full = jnp.transpose(wconv_all[0::2], (1, 0, 2)).reshape(3, D)
    b_sh = lax.dynamic_slice(b_ada, (0, chip * NA), (1, NA))
    mod_sh = _ada_fwd(c_all, w_ada2, b_sh)
    mod_all = _all_gather8(mod_sh, "gather_mod")
    mod = lax.dynamic_slice(mod_all[0::2], (0, dev, 0), (4, 1, NA)).reshape(6, D)
    shift1, scale1, gate1, shift2, scale2, gate2 = (mod[k:k + 1] for k in range(6))

    g_in, shift1, w_conv_full = _run_plan(_gather_plan([sh_in]), "gather_first", ride=[shift1, w_conv_full])
    g_in = whole(g_in)
    sh_a1, sh_a2 = [sh_qb, sh_kvb], [sh_oa, sh_ob, sh_o]
    cl_a1, cl_a2, cl_fi, cl_fo = (_gather_copies(g) for g in (sh_a1, sh_a2, [sh_fi], [sh_fo]))
    st_a1 = _split_start(sh_a1, cl_a1, shift1, "gather_a1_start")
    st_a2 = _split_start(sh_a2, cl_a2, st_a1[4], "gather_a2_start")
    shift1 = st_a2[4][0]

    def in_rows(lo, hi):
        parts = [g_in[p, max(lo, p * CS) - p * CS:min(hi, (p + 1) * CS) - p * CS]
                 for p in range(4) if max(lo, p * CS) < min(hi, (p + 1) * CS)]
        return parts[0] if len(parts) == 1 else jnp.concatenate(parts, axis=0)

    n_qkv = Q_LORA + KV_LORA + QK_ROPE
    W_qkvT = jnp.pad(in_rows(0, n_qkv), ((0, QKV_A - n_qkv), (0, 0)))
    W_convT = in_rows(n_qkv, n_qkv + 3 * D)
    W_gateT = in_rows(n_qkv + 3 * D, n_qkv + 5 * D)

    u = _modulate(x2, scale1, shift1, "modulate1")
    pq = _matmul(u, W_qkvT, "nt", F32, "proj_qkv")
    pc = _matmul(u, W_convT, "nt", F32, "proj_conv")
    sh_a1, la1 = _split_wait(st_a1, cl_a1, [pc], "gather_a1_wait")
    pg, (g_qb, g_kvb) = _matmul(u, W_gateT, "nt", BF16, "proj_gate", carry=_gather_plan(sh_a1, into=la1, ici=False))
    st_fi = _split_start([sh_fi], cl_fi, g_q_a, "gather_fi_start", after=[pg])
    W_qb = jnp.pad(cols(whole(g_qb)).reshape(Q_LORA, N_HEADS, QK_NOPE + QK_ROPE),
                   ((0, 0), (0, 0), (0, QK_PAD - QK_NOPE - QK_ROPE))).reshape(Q_LORA, N_HEADS * QK_PAD)
    W_kvb = cols(whole(g_kvb))
    rq, rkv, kr = _rms_fwd(pq, tab, st_fi[4][0], g_kv_a)
    kv = _matmul(rkv, W_kvb, "nn", BF16, "kv_b")
    sh_a2, la2 = _split_wait(st_a2, cl_a2, [kv], "gather_a2_wait")
    qf, (g_oa, g_ob, g_o) = _matmul(rq, W_qb, "nn", F32, "q_b", carry=_gather_plan(sh_a2, into=la2, ici=False))
    q = _q_rope(qf, tab)
    o, lse, _ = _attn_fwd(q, kv, kr)
    W_oa, W_ob, W_o = (g.reshape(-1, D) for g in (g_oa, g_ob, g_o))
    hb = _conv_fwd(pc, w_conv_full)
    sh_fi_t, lfi = _split_wait(st_fi, cl_fi, [hb], "gather_fi_wait")
    y_b, g_fi = _matmul(hb, W_ob, "nn", F32, "o_b", carry=_gather_plan(sh_fi_t, (0, 2), into=lfi, ici=False))
    y_a, (g_fi,) = _matmul(o, W_oa, "nn", F32, "o_a", carry=_gather_plan(sh_fi_t, (1, 2), into=g_fi, ici=False))
    st_fo = _split_start([sh_fo], cl_fo, ln1_g, "gather_fo_start", after=[y_b])
    merged = _merge_fwd(y_a, y_b, pg)
    mix = _matmul(merged, W_o, "nn", F32, "w_o")
    W_fi = whole(g_fi)
    x1, u2 = _ln1_fwd(x2, mix, gate1, st_fo[4][0], ln1_b, scale2, shift2)
    hh = _matmul(u2, W_fi, "nn", BF16, "ffn_in", shards="b")
    sh_fo_t, lfo = _split_wait(st_fo, cl_fo, [hh], "gather_fo_wait")
    act, (g_fo,) = _swiglu_fwd(hh, carry=_gather_plan(sh_fo_t, into=lfo, ici=False))
    W_fo = g_fo.reshape(F, D)
    ffn = _matmul(act, W_fo, "nn", F32, "ffn_out")

    core_i = ac.astype(jnp.int32).reshape(1)
    chip_i = chip.astype(jnp.int32).reshape(1)

    def uncols(g):
        return jnp.transpose(g.reshape(g.shape[0], 4, g.shape[1] // 4), (1, 0, 2))

    def slabs(p):
        return p.reshape(4, 2, p.shape[1] // 2, p.shape[2])

    def add_pairs(parts, sibs, nms):
        return [_add_pair(a, b, core_i, "add_pair_" + nm) for a, b, nm in zip(parts, sibs, nms)]

    def sum_all(pre, recv, nms):
        return [_sum_slabs(a, r, chip_i, "sum_slabs_" + nm) for a, r, nm in zip(pre, recv, nms)]

    dffn, dx1a, loss_acc, d_ln2_g, d_ln2_b, d_gate2 = _ln2_loss_bwd(x1, ffn, gate2, ln2_g, ln2_b, tgt)
    loss = lax.psum(loss_acc[0, 0], ("x", "y", "c"))
    dW_fo = _matmul(act, dffn, "tn", BF16, "d_w_ffn_out")
    p_fo = [slabs(dW_fo.reshape(4, -1, D))]
    dact, s_fo = _matmul(dffn, W_fo, "nt", BF16, "d_act", carry=_pair_plan(p_fo))
    pre_fo = add_pairs(p_fo, s_fo, ["w_ffn_out"])
    cs_fo = _scatter_copies(pre_fo)
    st_sfo = _split_start(pre_fo, cs_fo, scale2, "scatter_fo_start")
    dhh = _swiglu_bwd(dact, hh)
    dW_fi = _matmul(u2, dhh, "tn", BF16, "d_w_ffn_in", shards="o")
    p_fi = [slabs(dW_fi)]
    du2, s_fi = _matmul(dhh, W_fi, "nt", F32, "d_u2", carry=_pair_plan(p_fi), shards="b")
    pre_fi = add_pairs(p_fi, s_fi, ["w_ffn_in"])
    cs_fi = _scatter_copies(pre_fi)
    st_sfi = _split_start(pre_fi, cs_fi, st_sfo[4], "scatter_fi_start")
    dmix, dxa, d_shift2, d_scale2, d_ln1_g, d_ln1_b, d_gate1 = _ln1_bwd(x2, mix, dx1a, du2, gate1, ln1_g, ln1_b, st_sfi[4][0])
    dW_o = _matmul(merged, dmix, "tn", BF16, "d_w_o")
    dmerged = _matmul(dmix, W_o, "nt", F32, "d_merged")
    dy_a, dy_b, dgate = _merge_bwd(dmerged, y_a, y_b, pg)
    dW_oa = _matmul(o, dy_a, "tn", BF16, "d_w_o_a")
    do = _matmul(dy_a, W_oa, "nt", BF16, "d_o")
    dW_ob = _matmul(hb, dy_b, "tn", BF16, "d_w_o_b")
    p_mid = [slabs(g.reshape(4, -1, D)) for g in (dW_oa, dW_ob, dW_o)]
    dhb, s_mid = _matmul(dy_b, W_ob, "nt", F32, "d_hb", carry=_pair_plan(p_mid))
    pre_mid = add_pairs(p_mid, s_mid, ["w_o_a", "w_o_b", "w_o"])
    cs_mid = _scatter_copies(pre_mid)
    st_smid = _split_start(pre_mid, cs_mid, w_conv_full, "scatter_mid_start")
    dconv, d_wconv = _conv_bwd(dhb, pc, st_smid[4][0])
    dq, dkv, dkr, _ = _attn_bwd(q, kv, kr, do, o, lse, tab, carry=_token_plan(st_smid[4][0]))
    names_a = ["w_ffn_out", "w_ffn_in", "w_o_a", "w_o_b", "w_o"]
    dW_qb = _matmul(rq, dq, "tn", BF16, "d_w_q_b")
    d_rq = _matmul(dq, W_qb, "nt", F32, "d_rq")
    dW_kvb = _matmul(rkv, dkv, "tn", BF16, "d_w_kv_b")
    d_rkv = _matmul(dkv, W_kvb, "nt", F32, "d_rkv")
    dqkv, d_g_q, d_g_kv = _rms_bwd(d_rq, d_rkv, pq, dkr, g_q_a, g_kv_a)
    dW_qkvT = _matmul(dqkv, u, "tn", BF16, "d_w_qkv")
    dW_convT = _matmul(dconv, u, "tn", BF16, "d_w_conv")
    dW_gateT = _matmul(dgate, u, "tn", BF16, "d_w_gate")
    pre_fo, r_fo = _split_wait(st_sfo, cs_fo, [dW_qkvT], "scatter_fo_wait")
    pre_fi, r_fi = _split_wait(st_sfi, cs_fi, [dW_qkvT], "scatter_fi_wait")
    pre_mid, r_mid = _split_wait(st_smid, cs_mid, [dW_qkvT], "scatter_mid_wait")
    fin_a = sum_all(pre_fo + pre_fi + pre_mid, r_fo + r_fi + r_mid, names_a)
    dW_inT = jnp.concatenate([dW_qkvT[:n_qkv], dW_convT, dW_gateT], axis=0).reshape(4, CS, D)
    dW_inT = jnp.pad(dW_inT, ((0, 0), (0, CSP - CS), (0, 0)))
    dW_qb_u = dW_qb.reshape(Q_LORA, N_HEADS, QK_PAD)[:, :, :QK_NOPE + QK_ROPE].reshape(Q_LORA, -1)
    names_b = ["w_in", "w_q_b", "w_kv_b"]
    p_b = [slabs(dW_inT), slabs(uncols(dW_qb_u)), slabs(uncols(dW_kvb))]
    du, s_b = _matmul(dqkv, W_qkvT, "nn", F32, "d_u_qkv", carry=_pair_plan(p_b))
    pre_b = add_pairs(p_b, s_b, names_b)
    cs_b = _scatter_copies(pre_b)
    st_b = _split_start(pre_b, cs_b, scale1, "scatter_last_start")
    du, fs_a = _matmul(dconv, W_convT, "nn", F32, "d_u_conv", add=du, carry=_sibling_plan(fin_a))
    du = _matmul(dgate, W_gateT, "nn", F32, "d_u_gate", add=du)
    grad_x, d_shift1, d_scale1 = _dx_final(dxa, du, x2, st_b[4][0])

    big = {}
    ws = dict(w_in=(w_inT, m_w_inT, v_w_inT), w_q_b=(w_q_b2, m_w_q_b[0], v_w_q_b[0]),
              w_kv_b=(w_kv_b2, m_w_kv_b[0], v_w_kv_b[0]), w_o_a=(w_o_a2, m_w_o_a[0], v_w_o_a[0]),
              w_o_b=(w_o_b2, m_w_o_b[0], v_w_o_b[0]), w_o=(w_o2, m_w_o[0], v_w_o[0]),
              w_ffn_in=(w_ffn_in2, m_w_ffn_in[0], v_w_ffn_in[0]), w_ffn_out=(w_ffn_out2, m_w_ffn_out[0], v_w_ffn_out[0]))

    def adam_of(nm, a, b, carry=None):
        w_, m_, v_ = ws[nm]
        return _adam_halves("adam_" + nm, w_, m_, v_, a, b, core_i, carry)

    for nm, a, b in zip(names_a, fin_a, fs_a):
        big[nm] = adam_of(nm, a, b, _token_plan(st_b[4][0]))[0]
    done = [big[nm][1] for nm in names_a] + [grad_x]
    pre_b, r_b = _split_wait(st_b, cs_b, done, "scatter_last_wait")
    fin_b = sum_all(pre_b, r_b, names_b)
    fs_b = _run_plan(_sibling_plan(fin_b), "sibling_last")
    for nm, a, b in zip(names_b, fin_b, fs_b):
        big[nm] = adam_of(nm, a, b)

    def pad_d(v):
        return jnp.pad(v, ((0, 0), (0, D - v.shape[1])))

    small = _pack_rows([d_ln1_g, d_ln1_b, d_ln2_g, d_ln2_b, pad_d(d_g_q), pad_d(d_g_kv), d_wconv,
                         d_shift1, d_scale1, d_gate1, d_shift2, d_scale2, d_gate2], 16, after=[pre_b[1]])
    small_all = _all_gather8(small, "gather_small")
    small_sum = _sum8(small_all)
    g_ln1_g, g_ln1_b, g_ln2_g, g_ln2_b = (small_sum[k:k + 1] for k in range(4))
    g_g_q, g_g_kv = small_sum[4:5, :Q_LORA], small_sum[5:6, :KV_LORA]
    g_wconv = lax.dynamic_slice(small_sum[6:9], (0, chip * CW), (3, CW))
    g_b_ada = small_sum[9:15].reshape(1, 6 * D)
    dmod_all = small_all[:, 9:15, :].reshape(8, 6 * D)
    g_w_ada = _ada_bwd(c_all, lax.dynamic_slice(dmod_all, (0, chip * NA), (8, NA)))
    big["w_ada"] = [g_w_ada] + list(_adam("adam_w_ada", w_ada2, m_w_ada[0], v_w_ada[0], g_w_ada))
    sm = {}
    for nm, w_, m_, v_, g_ in [("b_ada", b_ada, m_b_ada, v_b_ada, g_b_ada), ("g_q_a", g_q_a, m_g_q_a, v_g_q_a, g_g_q),
                               ("g_kv_a", g_kv_a, m_g_kv_a, v_g_kv_a, g_g_kv),
                               ("w_conv", w_conv[0], m_w_conv[0], v_w_conv[0], g_wconv),
                               ("ln1_g", ln1_g, m_ln1_g, v_ln1_g, g_ln1_g), ("ln1_b", ln1_b, m_ln1_b, v_ln1_b, g_ln1_b),
                               ("ln2_g", ln2_g, m_ln2_g, v_ln2_g, g_ln2_g), ("ln2_b", ln2_b, m_ln2_b, v_ln2_b, g_ln2_b)]:
        sm[nm] = (g_,) + tuple(_adam_small("adam_" + nm, w_, m_, v_, g_))

    order = ["w_ada", "b_ada", "w_in", "g_q_a", "w_q_b", "g_kv_a", "w_kv_b", "w_o_a", "w_conv", "w_o_b", "w_o",
             "ln1_g", "ln1_b", "w_ffn_in", "w_ffn_out", "ln2_g", "ln2_b"]
    lead = {"b_ada", "g_q_a", "g_kv_a", "ln1_g", "ln1_b", "ln2_g", "ln2_b"}

    def leaf(nm, k):
        val = big[nm][k] if nm in big else sm[nm][k]
        if nm == "w_in":
            val = val.T
        return val if nm in lead else val[None]

    outs = [loss, grad_x[None]]
    for k in range(4):
        outs += [leaf(nm, k) for nm in order]
    return tuple(outs)
```

```python
import functools

import jax
import jax.numpy as jnp
from jax import lax
from jax.experimental import pallas as pl
from jax.experimental.pallas import tpu as pltpu

F32, BF16 = jnp.float32, jnp.bfloat16
N_HEADS, QK_NOPE, QK_ROPE, V_HEAD = 16, 128, 64, 128
Q_LORA, KV_LORA = 512, 512
QK_PAD = 256
QKV_A = 1152
CHUNK_SHIFT = 6
ATTN_SCALE = (QK_NOPE + QK_ROPE) ** -0.5
ROPE_THETA = 10000.0
ALPHA = 2.0 ** 0.25
LN_EPS, RMS_EPS = 1e-5, 1e-6
ADAM_LR, ADAM_B1, ADAM_B2, ADAM_EPS, ADAM_WD, ADAM_STEP = 0.001, 0.9, 0.999, 1e-08, 0.01, 10
ADAM_C1 = 1.0 - ADAM_B1 ** ADAM_STEP
ADAM_C2 = 1.0 - ADAM_B2 ** ADAM_STEP
VMEM_LIMIT = 56 * 1024 * 1024
MESH = pl.DeviceIdType.MESH
ANY = pl.BlockSpec(memory_space=pl.ANY)
HBM_SPEC = pl.BlockSpec(memory_space=pltpu.HBM)
SEM_SPEC = pl.BlockSpec(memory_space=pltpu.SEMAPHORE)
NT = (((1,), (1,)), ((), ()))
TN = (((0,), (0,)), ((), ()))
NN = (((1,), (0,)), ((), ()))


def _params(sem=None):
    return pltpu.CompilerParams(dimension_semantics=sem, vmem_limit_bytes=VMEM_LIMIT)


def _pick(n, cands=(1408, 1024, 512, 384, 256, 128)):
    for t in cands:
        if n % t == 0:
            return t
    return n


def _row_tile(rows, row_bytes, budget, mult=8):
    best = mult
    for t in range(mult, rows + 1, mult):
        if rows % t == 0 and t * row_bytes <= budget:
            best = t
    return best


def _tile2(rows, cols, mult=8, budget=3 << 18):
    col_tiles = [t for t in range(128, cols + 1, 128) if cols % t == 0] or [cols]
    best = None
    for tc in col_tiles:
        for tr in range(mult, rows + 1, mult):
            if rows % tr == 0 and tr * tc <= budget and (best is None or (tr * tc, tc) > (best[0] * best[1], best[1])):
                best = (tr, tc)
    assert best is not None, (rows, cols)
    return best


def _sigmoid(x):
    return jax.nn.sigmoid(x)


class _Plan:
    def __init__(self, ins, outs, sems, start, finish, aliases=None):
        self.ins, self.outs, self.sems, self.start, self.finish = list(ins), list(outs), list(sems), start, finish
        self.aliases = dict(aliases or {})

    def io_aliases(self, first_in, first_out):
        return {first_in + i: first_out + o for i, o in self.aliases.items()}


def _token_plan(token):
    return _Plan([token], [], [], lambda *a: None, lambda *a: None)


def _run_plan(plan, name, ride=None):
    n_in, n_out = len(plan.ins), len(plan.outs)
    extra = [] if ride is None else list(ride)
    aliases = plan.io_aliases(0, 0)
    for k in range(len(extra)):
        aliases[n_in + k] = n_out + k

    def body(*refs):
        ins, outs, sems = refs[:n_in], refs[n_in + len(extra):n_in + len(extra) + n_out], refs[n_in + 2 * len(extra) + n_out:]
        plan.start(ins, outs, sems)
        plan.finish(ins, outs, sems)

    return pl.pallas_call(body, name=name, out_shape=plan.outs + [jax.ShapeDtypeStruct(r.shape, r.dtype) for r in extra],
                          in_specs=[ANY] * (n_in + len(extra)), out_specs=[ANY] * (n_out + len(extra)),
                          scratch_shapes=plan.sems, input_output_aliases=aliases,
                          compiler_params=_params())(*plan.ins, *extra)


def _matmul(a, b, mode, out_dtype, name, add=None, carry=None, shards=None):
    if mode == "nn":
        (M, K), N, dims = a.shape, b.shape[-1] * (4 if shards else 1), NN
    elif mode == "nt":
        (M, K), N, dims = a.shape, b.shape[-2], NT
    else:
        (K, M), N, dims = a.shape, b.shape[1], TN
    split_n = shards and mode != "nt"
    tm = _pick(M)
    tn = _pick(N // 4) if split_n else _pick(N)
    if shards and mode == "nt":
        tk = _pick(K // 4)
    else:
        tk = K if K <= 2048 else _pick(K)
    nk = K // tk
    per = (N // 4 // tn) if split_n else (K // 4 // tk if shards else 1)
    a_spec = (pl.BlockSpec((tk, tm), lambda i, j, k: (k, i)) if mode == "tn"
              else pl.BlockSpec((tm, tk), lambda i, j, k: (i, k)))
    if shards == "b" and mode == "nn":
        b_spec = pl.BlockSpec((None, tk, tn), lambda i, j, k: (j // per, k, j % per))
    elif shards == "b":
        b_spec = pl.BlockSpec((None, tn, tk), lambda i, j, k: (k // per, j, k % per))
    else:
        b_spec = (pl.BlockSpec((tn, tk), lambda i, j, k: (j, k)) if mode == "nt"
                  else pl.BlockSpec((tk, tn), lambda i, j, k: (k, j)))
    o_spec = pl.BlockSpec((tm, tn), lambda i, j, k: (i, j))
    o_shape = (M, N)
    if shards == "o":
        o_spec, o_shape = pl.BlockSpec((None, tm, tn), lambda i, j, k: (j // per, i, j % per)), (4, M, N // 4)
    has_add = add is not None
    n_ci = len(carry.ins) if carry else 0
    n_co = len(carry.outs) if carry else 0
    n_in = 2 + has_add
    grid = (M // tm, N // tn, nk)

    def body(*refs):
        a_ref, b_ref = refs[0], refs[1]
        add_ref = refs[2] if has_add else None
        o_ref = refs[n_in + n_ci]
        acc_ref = refs[n_in + n_ci + 1 + n_co] if nk > 1 else None
        c_ins = refs[n_in:n_in + n_ci]
        c_outs = refs[n_in + n_ci + 1:n_in + n_ci + 1 + n_co]
        c_sems = refs[n_in + n_ci + 1 + n_co + (nk > 1):]
        i, j, k = pl.program_id(0), pl.program_id(1), pl.program_id(2)

        if carry:
            @pl.when((i == 0) & (j == 0) & (k == 0))
            def _():
                carry.start(c_ins, c_outs, c_sems)

        part = lax.dot_general(a_ref[...], b_ref[...], dims, preferred_element_type=F32)
        if nk == 1:
            o_ref[...] = (part + add_ref[...] if has_add else part).astype(o_ref.dtype)
        else:
            @pl.when(k == 0)
            def _():
                acc_ref[...] = part

            @pl.when((k > 0) & (k < nk - 1))
            def _():
                acc_ref[...] += part

            @pl.when(k == nk - 1)
            def _():
                r = acc_ref[...] + part
                if has_add:
                    r = r + add_ref[...]
                o_ref[...] = r.astype(o_ref.dtype)

        if carry:
            @pl.when((i == grid[0] - 1) & (j == grid[1] - 1) & (k == nk - 1))
            def _():
                carry.finish(c_ins, c_outs, c_sems)

    ins = [a, b] + ([add] if has_add else []) + (carry.ins if carry else [])
    in_specs = [a_spec, b_spec] + ([o_spec] if has_add else []) + [ANY] * n_ci
    res = pl.pallas_call(
        body, name=name, grid=grid,
        in_specs=in_specs, out_specs=[o_spec] + [ANY] * n_co,
        out_shape=[jax.ShapeDtypeStruct(o_shape, out_dtype)] + (carry.outs if carry else []),
        scratch_shapes=([pltpu.VMEM((tm, tn), F32)] if nk > 1 else []) + (carry.sems if carry else []),
        input_output_aliases=carry.io_aliases(n_in, 1) if carry else {},
        compiler_params=_params(("arbitrary",) * 3 if carry else ("parallel", "parallel", "arbitrary")),
    )(*ins)
    return (res[0], res[1:]) if carry else res[0]


def _rows(body, name, n_rows, tm, ins, outs, accs=(), carry=None):
    grid = (n_rows // tm,)
    per8 = tm // 8
    last8 = n_rows // 8 - 1
    arrays, in_specs = [], []
    for spec in ins:
        kind, arr = spec[0], spec[1]
        arrays.append(arr)
        if kind == "row":
            _, _, cb, w = spec
            in_specs.append(pl.BlockSpec((tm, w), lambda i, cb=cb: (i, cb)))
        elif kind == "full":
            in_specs.append(pl.BlockSpec(arr.shape, lambda i, nd=arr.ndim: (0,) * nd))
        elif kind == "prev":
            _, _, cb, w = spec
            in_specs.append(pl.BlockSpec((8, w), lambda i, cb=cb: (jnp.maximum(i * per8 - 1, 0), cb)))
        else:
            _, _, cb, w = spec
            in_specs.append(pl.BlockSpec((8, w), lambda i, cb=cb: (jnp.minimum((i + 1) * per8, last8), cb)))
    out_shape = [jax.ShapeDtypeStruct((n_rows, w), dt) for (w, dt) in outs]
    out_specs = [pl.BlockSpec((tm, w), lambda i: (i, 0)) for (w, _) in outs]
    out_shape += [jax.ShapeDtypeStruct(s, F32) for s in accs]
    out_specs += [pl.BlockSpec(s, lambda i, nd=len(s): (0,) * nd) for s in accs]
    n_in, n_out, n_acc = len(ins), len(outs), len(accs)
    n_ci = len(carry.ins) if carry else 0
    n_co = len(carry.outs) if carry else 0

    def kernel_body(*refs):
        first = n_in + n_ci
        c_ins, c_outs, c_sems = refs[n_in:first], refs[first + n_out + n_acc:first + n_out + n_acc + n_co], refs[first + n_out + n_acc + n_co:]
        if carry:
            @pl.when(pl.program_id(0) == 0)
            def _():
                carry.start(c_ins, c_outs, c_sems)

        body(pl.program_id(0), refs[:n_in], refs[first:first + n_out], refs[first + n_out:first + n_out + n_acc])
        if carry:
            @pl.when(pl.program_id(0) == grid[0] - 1)
            def _():
                carry.finish(c_ins, c_outs, c_sems)

    res = pl.pallas_call(
        kernel_body, name=name, grid=grid, in_specs=in_specs + [ANY] * n_ci, out_specs=out_specs + [ANY] * n_co,
        out_shape=out_shape + (carry.outs if carry else []), scratch_shapes=carry.sems if carry else [],
        input_output_aliases=carry.io_aliases(n_in, n_out + n_acc) if carry else {},
        compiler_params=_params(("arbitrary",)),
    )(*arrays, *(carry.ins if carry else []))
    return (res[:n_out + n_acc], res[n_out + n_acc:]) if carry else res


def _acc_add(i, ref, val):
    @pl.when(i == 0)
    def _():
        ref[...] = val

    @pl.when(i > 0)
    def _():
        ref[...] += val


def _rope(t, tab, sign):
    c, sa, sb = tab[:, 0:128], tab[:, 128:256], tab[:, 256:384]
    rot = pltpu.roll(t, 96, 1) * sa + pltpu.roll(t, 32, 1) * sb
    return t * c + rot if sign > 0 else t * c - rot


def _ln_stats(r):
    mu = jnp.mean(r, axis=-1, keepdims=True)
    d = r - mu
    var = jnp.mean(d * d, axis=-1, keepdims=True)
    rstd = lax.rsqrt(var + LN_EPS)
    return d * rstd, rstd


def _ln_bwd(dxh, xh, rstd):
    m1 = jnp.mean(dxh, axis=-1, keepdims=True)
    m2 = jnp.mean(dxh * xh, axis=-1, keepdims=True)
    return rstd * (dxh - m1 - xh * m2)


def _modulate(x, scale, shift, name):
    S, D = x.shape

    def body(i, ins, outs, accs):
        outs[0][...] = (ins[0][...] * (1.0 + ins[1][...]) + ins[2][...]).astype(BF16)

    return _rows(body, name, S, _pick(S, (256, 128)), [("row", x, 0, D), ("full", scale), ("full", shift)], [(D, BF16)])[0]


def _rms_fwd(pq, tab, g_q, g_kv):
    S = pq.shape[0]

    def body(i, ins, outs, accs):
        pq_ref, tab_ref, gq_ref, gkv_ref = ins

        def rms(x, g):
            return x * lax.rsqrt(jnp.mean(x * x, axis=-1, keepdims=True) + RMS_EPS) * g

        outs[0][...] = rms(pq_ref[:, 0:Q_LORA], gq_ref[...]).astype(BF16)
        outs[1][...] = rms(pq_ref[:, Q_LORA:Q_LORA + KV_LORA], gkv_ref[...]).astype(BF16)
        outs[2][...] = _rope(pq_ref[:, Q_LORA + KV_LORA:QKV_A], tab_ref[...], 1).astype(BF16)

    return _rows(body, "rms_fwd", S, _pick(S, (256, 128)),
                 [("row", pq, 0, QKV_A), ("row", tab, 0, 384), ("full", g_q), ("full", g_kv)],
                 [(Q_LORA, BF16), (KV_LORA, BF16), (128, BF16)])


def _q_rope(q, tab):
    S, W = q.shape

    def body(i, ins, outs, accs):
        q_ref, tab_ref = ins
        t = tab_ref[...]
        for h in range(N_HEADS):
            lo = h * QK_PAD
            outs[0][:, lo:lo + 128] = q_ref[:, lo:lo + 128].astype(BF16)
            outs[0][:, lo + 128:lo + 256] = _rope(q_ref[:, lo + 128:lo + 256], t, 1).astype(BF16)

    return _rows(body, "q_rope", S, _pick(S, (256, 128)), [("row", q, 0, W), ("row", tab, 0, 384)], [(W, BF16)])[0]


def _allowed(q0, k0, bq):
    row = q0 + lax.broadcasted_iota(jnp.int32, (bq, bq), 0)
    col = k0 + lax.broadcasted_iota(jnp.int32, (bq, bq), 1)
    return (col >> CHUNK_SHIFT) <= (row >> CHUNK_SHIFT)


ATTN_BLOCK = 512


def _attn_fwd(q, kv, kr, carry=None):
    S = q.shape[0]
    bq = min(ATTN_BLOCK, S)
    nq = S // bq
    n_ci = len(carry.ins) if carry else 0
    n_co = len(carry.outs) if carry else 0

    def body(*refs):
        q_ref, kn_ref, v_ref, kr_ref = refs[:4]
        o_ref, lse_ref = refs[4 + n_ci:6 + n_ci]
        c_ins, c_outs = refs[4:4 + n_ci], refs[6 + n_ci:6 + n_ci + n_co]
        kcat = refs[6 + n_ci + n_co]
        c_sems = refs[7 + n_ci + n_co:]
        qi = pl.program_id(1)
        if carry:
            @pl.when((pl.program_id(0) == 0) & (qi == 0))
            def _():
                carry.start(c_ins, c_outs, c_sems)

        @pl.when(qi == 0)
        def _():
            kcat[:, 0:128] = kn_ref[...]
            kcat[:, 128:256] = kr_ref[...]

        qv = q_ref[...]

        def step(j, carry, masked):
            m, l, acc = carry
            off = pl.multiple_of(j * bq, bq)
            s = lax.dot_general(qv, kcat[pl.ds(off, bq), :], NT, preferred_element_type=F32) * ATTN_SCALE
            if masked:
                s = jnp.where(_allowed(qi * bq, off, bq), s, -1e30)
            m_new = jnp.maximum(m, jnp.max(s, axis=1, keepdims=True))
            a = jnp.exp(m - m_new)
            p = jnp.exp(s - m_new)
            l = a * l + jnp.sum(p, axis=1, keepdims=True)
            acc = a * acc + jnp.dot(p.astype(BF16), v_ref[pl.ds(off, bq), :], preferred_element_type=F32)
            return m_new, l, acc

        init = (jnp.full((bq, 1), -1e30, F32), jnp.zeros((bq, 1), F32), jnp.zeros((bq, V_HEAD), F32))
        below = lax.fori_loop(0, qi, lambda j, cr: step(j, cr, False), init)
        m, l, acc = step(qi, below, True)
        o_ref[...] = (acc / l).astype(BF16)
        lse_ref[0] = m + jnp.log(l)
        if carry:
            @pl.when((pl.program_id(0) == N_HEADS - 1) & (qi == nq - 1))
            def _():
                carry.finish(c_ins, c_outs, c_sems)

    res = pl.pallas_call(
        body, name="attn_fwd", grid=(N_HEADS, nq),
        in_specs=[pl.BlockSpec((bq, QK_PAD), lambda h, i: (i, h)),
                  pl.BlockSpec((S, 128), lambda h, i: (0, 2 * h)),
                  pl.BlockSpec((S, 128), lambda h, i: (0, 2 * h + 1)),
                  pl.BlockSpec((S, 128), lambda h, i: (0, 0))] + [ANY] * n_ci,
        out_specs=[pl.BlockSpec((bq, V_HEAD), lambda h, i: (i, h)),
                   pl.BlockSpec((1, bq, 1), lambda h, i: (h, i, 0))] + [ANY] * n_co,
        out_shape=[jax.ShapeDtypeStruct((S, N_HEADS * V_HEAD), BF16),
                   jax.ShapeDtypeStruct((N_HEADS, S, 1), F32)] + (carry.outs if carry else []),
        scratch_shapes=[pltpu.VMEM((S, QK_PAD), BF16)] + (carry.sems if carry else []),
        input_output_aliases=carry.io_aliases(4, 2) if carry else {},
        compiler_params=_params(("arbitrary", "arbitrary")),
    )(q, kv, kv, kr, *(carry.ins if carry else []))
    return res[0], res[1], res[2:]


def _attn_bwd(q, kv, kr, do, o, lse, tab, carry=None):
    S = q.shape[0]
    bq = min(ATTN_BLOCK, S)
    nq = S // bq

    n_ci = len(carry.ins) if carry else 0
    n_co = len(carry.outs) if carry else 0

    def body(*refs):
        q_ref, kn_ref, v_ref, kr_ref, do_ref, o_ref, lse_ref, tab_ref = refs[:8]
        dq_ref, dkv_ref, dkr_ref = refs[8 + n_ci:11 + n_ci]
        dq_acc, dk_acc, dv_acc, kcat, delta = refs[11 + n_ci + n_co:16 + n_ci + n_co]
        c_ins, c_outs, c_sems = refs[8:8 + n_ci], refs[11 + n_ci:11 + n_ci + n_co], refs[16 + n_ci + n_co:]
        h = pl.program_id(0)
        if carry:
            @pl.when(h == 0)
            def _():
                carry.start(c_ins, c_outs, c_sems)

        dq_acc[...] = jnp.zeros_like(dq_acc)
        dk_acc[...] = jnp.zeros_like(dk_acc)
        dv_acc[...] = jnp.zeros_like(dv_acc)
        kcat[:, 0:128] = kn_ref[...]
        kcat[:, 128:256] = kr_ref[...]
        for r in range(nq):
            rows = slice(r * bq, (r + 1) * bq)
            delta[rows, :] = jnp.sum(do_ref[rows, :].astype(F32) * o_ref[rows, :].astype(F32), axis=1, keepdims=True)

        def pair(i, j, masked):
            rows_i = pl.ds(pl.multiple_of(i * bq, bq), bq)
            rows_j = pl.ds(pl.multiple_of(j * bq, bq), bq)
            qv, dov, k = q_ref[rows_i, :], do_ref[rows_i, :], kcat[rows_j, :]
            s = lax.dot_general(qv, k, NT, preferred_element_type=F32) * ATTN_SCALE
            if masked:
                s = jnp.where(_allowed(i * bq, j * bq, bq), s, -1e30)
            p = jnp.exp(s - lse_ref[0, rows_i, :])
            dv_acc[rows_j, :] += lax.dot_general(p.astype(BF16), dov, TN, preferred_element_type=F32)
            dp = lax.dot_general(dov, v_ref[rows_j, :], NT, preferred_element_type=F32)
            ds = (p * (dp - delta[rows_i, :]) * ATTN_SCALE).astype(BF16)
            dk_acc[rows_j, :] += lax.dot_general(ds, qv, TN, preferred_element_type=F32)
            dq_acc[rows_i, :] += jnp.dot(ds, k, preferred_element_type=F32)

        def kv_step(j, _):
            pair(j, j, True)

            def q_step(i, _):
                pair(i, j, False)
                return 0

            lax.fori_loop(j + 1, nq, q_step, 0)
            return 0

        lax.fori_loop(0, nq, kv_step, 0)

        for r in range(nq):
            rows = slice(r * bq, (r + 1) * bq)
            dq_ref[rows, 0:128] = dq_acc[rows, 0:128].astype(BF16)
            dq_ref[rows, 128:256] = _rope(dq_acc[rows, 128:256], tab_ref[rows, :], -1).astype(BF16)
        dkv_ref[:, 0:128] = dk_acc[:, 0:128].astype(BF16)
        dkv_ref[:, 128:256] = dv_acc[...].astype(BF16)

        @pl.when(h == 0)
        def _():
            dkr_ref[...] = dk_acc[:, 128:256]

        @pl.when(h > 0)
        def _():
            dkr_ref[...] += dk_acc[:, 128:256]

        @pl.when(h == N_HEADS - 1)
        def _():
            for r in range(nq):
                rows = slice(r * bq, (r + 1) * bq)
                dkr_ref[rows, :] = _rope(dkr_ref[rows, :], tab_ref[rows, :], -1)
            if carry:
                carry.finish(c_ins, c_outs, c_sems)

    W = N_HEADS * QK_PAD
    res = pl.pallas_call(
        body, name="attn_bwd", grid=(N_HEADS,),
        in_specs=[pl.BlockSpec((S, QK_PAD), lambda h: (0, h)),
                  pl.BlockSpec((S, 128), lambda h: (0, 2 * h)),
                  pl.BlockSpec((S, 128), lambda h: (0, 2 * h + 1)),
                  pl.BlockSpec((S, 128), lambda h: (0, 0)),
                  pl.BlockSpec((S, V_HEAD), lambda h: (0, h)),
                  pl.BlockSpec((S, V_HEAD), lambda h: (0, h)),
                  pl.BlockSpec((1, S, 1), lambda h: (h, 0, 0)),
                  pl.BlockSpec((S, 384), lambda h: (0, 0))] + [ANY] * n_ci,
        out_specs=[pl.BlockSpec((S, QK_PAD), lambda h: (0, h)),
                   pl.BlockSpec((S, QK_PAD), lambda h: (0, h)),
                   pl.BlockSpec((S, 128), lambda h: (0, 0))] + [ANY] * n_co,
        out_shape=[jax.ShapeDtypeStruct((S, W), BF16), jax.ShapeDtypeStruct((S, W), BF16),
                   jax.ShapeDtypeStruct((S, 128), F32)] + (carry.outs if carry else []),
        scratch_shapes=[pltpu.VMEM((S, QK_PAD), F32), pltpu.VMEM((S, QK_PAD), F32), pltpu.VMEM((S, V_HEAD), F32),
                        pltpu.VMEM((S, QK_PAD), BF16), pltpu.VMEM((S, 1), F32)]
        + (carry.sems if carry else []),
        input_output_aliases=carry.io_aliases(8, 3) if carry else {},
        compiler_params=_params(("arbitrary",)),
    )(q, kv, kv, kr, do, o, lse, tab, *(carry.ins if carry else []))
    return res[0], res[1], res[2], res[3:]


def _shift_down(cur, prev8, i, n):
    tm = cur.shape[0]
    prev8 = jnp.where(i == 0, jnp.zeros_like(prev8), prev8)
    full = jnp.concatenate([prev8, cur], axis=0)
    return pltpu.roll(full, n, 0)[8:8 + tm, :]


def _shift_up(cur, next8, i, last, n):
    tm = cur.shape[0]
    next8 = jnp.where(i == last, jnp.zeros_like(next8), next8)
    full = jnp.concatenate([cur, next8], axis=0)
    return pltpu.roll(full, tm + 8 - n, 0)[0:tm, :]


def _conv_fwd(pc, w_conv):
    S, D = pc.shape[0], pc.shape[1] // 3
    tm = _pick(S, (256, 128))

    def body(i, ins, outs, accs):
        b_ref, c_ref, x_ref, cp_ref, xp_ref, w_ref = ins
        z = c_ref[...] * x_ref[...]
        zp = cp_ref[...] * xp_ref[...]
        cz = w_ref[0:1, :] * _shift_down(z, zp, i, 2) + w_ref[1:2, :] * _shift_down(z, zp, i, 1) + w_ref[2:3, :] * z
        outs[0][...] = (b_ref[...] * cz).astype(BF16)

    return _rows(body, "conv_fwd", S, tm,
                 [("row", pc, 0, D), ("row", pc, 1, D), ("row", pc, 2, D), ("prev", pc, 1, D), ("prev", pc, 2, D),
                  ("full", w_conv)], [(D, BF16)])[0]


def _conv_bwd(dhb, pc, w_conv):
    S, D = dhb.shape
    tm = _pick(S, (256, 128))
    last = S // tm - 1

    def body(i, ins, outs, accs):
        g_ref, b_ref, c_ref, x_ref, cp_ref, xp_ref, gn_ref, bn_ref, w_ref = ins
        w0, w1, w2 = w_ref[0:1, :], w_ref[1:2, :], w_ref[2:3, :]
        c, x, g = c_ref[...], x_ref[...], g_ref[...]
        z = c * x
        zp = cp_ref[...] * xp_ref[...]
        z1, z2 = _shift_down(z, zp, i, 1), _shift_down(z, zp, i, 2)
        cz = w0 * z2 + w1 * z1 + w2 * z
        dcz = g * b_ref[...]
        dczn = gn_ref[...] * bn_ref[...]
        dz = w2 * dcz + w1 * _shift_up(dcz, dczn, i, last, 1) + w0 * _shift_up(dcz, dczn, i, last, 2)
        outs[0][:, 0:D] = (g * cz).astype(BF16)
        outs[0][:, D:2 * D] = (dz * x).astype(BF16)
        outs[0][:, 2 * D:3 * D] = (dz * c).astype(BF16)
        dw = jnp.concatenate([jnp.sum(dcz * z2, axis=0, keepdims=True), jnp.sum(dcz * z1, axis=0, keepdims=True),
                              jnp.sum(dcz * z, axis=0, keepdims=True)], axis=0)
        _acc_add(i, accs[0], dw)

    return _rows(body, "conv_bwd", S, tm,
                 [("row", dhb, 0, D), ("row", pc, 0, D), ("row", pc, 1, D), ("row", pc, 2, D),
                  ("prev", pc, 1, D), ("prev", pc, 2, D), ("next", dhb, 0, D), ("next", pc, 0, D), ("full", w_conv)],
                 [(3 * D, BF16)], [(3, D)])


def _merge_fwd(y_a, y_b, pg):
    S, D = y_a.shape

    def body(i, ins, outs, accs):
        ya, yb, ga, gb = ins
        outs[0][...] = (_sigmoid(ga[...].astype(F32)) * ya[...] + _sigmoid(gb[...].astype(F32)) * yb[...]).astype(BF16)

    return _rows(body, "merge_fwd", S, _pick(S, (256, 128)),
                 [("row", y_a, 0, D), ("row", y_b, 0, D), ("row", pg, 0, D), ("row", pg, 1, D)], [(D, BF16)])[0]


def _merge_bwd(dm, y_a, y_b, pg):
    S, D = dm.shape

    def body(i, ins, outs, accs):
        d, ya, yb = ins[0][...], ins[1][...], ins[2][...]
        sa, sb = _sigmoid(ins[3][...].astype(F32)), _sigmoid(ins[4][...].astype(F32))
        outs[0][...] = (d * sa).astype(BF16)
        outs[1][...] = (d * sb).astype(BF16)
        outs[2][:, 0:D] = (d * ya * (sa * (1.0 - sa))).astype(BF16)
        outs[2][:, D:2 * D] = (d * yb * (sb * (1.0 - sb))).astype(BF16)

    return _rows(body, "merge_bwd", S, _pick(S, (256, 128)),
                 [("row", dm, 0, D), ("row", y_a, 0, D), ("row", y_b, 0, D), ("row", pg, 0, D), ("row", pg, 1, D)],
                 [(D, BF16), (D, BF16), (2 * D, BF16)])


def _ln1_fwd(x, mix, gate1, g, b, scale2, shift2):
    S, D = x.shape

    def body(i, ins, outs, accs):
        x_ref, mix_ref, gate_ref, g_ref, b_ref, sc_ref, sh_ref = ins
        xh, _ = _ln_stats(ALPHA * x_ref[...] + gate_ref[...] * mix_ref[...])
        x1 = xh * g_ref[...] + b_ref[...]
        outs[0][...] = x1
        outs[1][...] = (x1 * (1.0 + sc_ref[...]) + sh_ref[...]).astype(BF16)

    return _rows(body, "ln1_fwd", S, _pick(S, (256, 128)),
                 [("row", x, 0, D), ("row", mix, 0, D), ("full", gate1), ("full", g), ("full", b),
                  ("full", scale2), ("full", shift2)], [(D, F32), (D, BF16)])


def _swiglu_fwd(hh, carry=None):
    S, F = hh.shape[0], hh.shape[1] // 2

    def body(i, ins, outs, accs):
        hg = ins[0][...].astype(F32)
        outs[0][...] = (hg * _sigmoid(hg) * ins[1][...].astype(F32)).astype(BF16)

    res = _rows(body, "swiglu_fwd", S, _pick(S, (128,)), [("row", hh, 0, F), ("row", hh, 1, F)], [(F, BF16)], carry=carry)
    return (res[0][0], res[1]) if carry else res[0]


def _swiglu_bwd(dact, hh):
    S, F = dact.shape

    def body(i, ins, outs, accs):
        d, hg, hu = ins[0][...].astype(F32), ins[1][...].astype(F32), ins[2][...].astype(F32)
        sg = _sigmoid(hg)
        outs[0][:, 0:F] = (d * hu * (sg * (1.0 + hg * (1.0 - sg)))).astype(BF16)
        outs[0][:, F:2 * F] = (d * (hg * sg)).astype(BF16)

    return _rows(body, "swiglu_bwd", S, _pick(S, (128,)),
                 [("row", dact, 0, F), ("row", hh, 0, F), ("row", hh, 1, F)], [(2 * F, BF16)])[0]


def _ln2_loss_bwd(x1, ffn, gate2, g, b, target):
    S, D = x1.shape

    def body(i, ins, outs, accs):
        x1_ref, f_ref, gate_ref, g_ref, b_ref, t_ref = ins
        f = f_ref[...]
        xh, rstd = _ln_stats(ALPHA * x1_ref[...] + gate_ref[...] * f)
        e = xh * g_ref[...] + b_ref[...] - t_ref[...]
        dy = e * (1.0 / D)
        dr = _ln_bwd(dy * g_ref[...], xh, rstd)
        outs[0][...] = (gate_ref[...] * dr).astype(BF16)
        outs[1][...] = ALPHA * dr
        _acc_add(i, accs[0], jnp.full((1, 128), (0.5 / D) * jnp.sum(e * e), F32))
        _acc_add(i, accs[1], jnp.sum(dy * xh, axis=0, keepdims=True))
        _acc_add(i, accs[2], jnp.sum(dy, axis=0, keepdims=True))
        _acc_add(i, accs[3], jnp.sum(dr * f, axis=0, keepdims=True))

    return _rows(body, "ln2_loss_bwd", S, _pick(S, (256, 128)),
                 [("row", x1, 0, D), ("row", ffn, 0, D), ("full", gate2), ("full", g), ("full", b), ("row", target, 0, D)],
                 [(D, BF16), (D, F32)], [(1, 128), (1, D), (1, D), (1, D)])


def _ln1_bwd(x, mix, dx1a, du2, gate1, g, b, scale2):
    S, D = x.shape

    def body(i, ins, outs, accs):
        x_ref, mix_ref, da_ref, du_ref, gate_ref, g_ref, b_ref, sc_ref = ins
        mix, du = mix_ref[...], du_ref[...]
        xh, rstd = _ln_stats(ALPHA * x_ref[...] + gate_ref[...] * mix)
        x1 = xh * g_ref[...] + b_ref[...]
        dx1 = da_ref[...] + du * (1.0 + sc_ref[...])
        dr = _ln_bwd(dx1 * g_ref[...], xh, rstd)
        outs[0][...] = (gate_ref[...] * dr).astype(BF16)
        outs[1][...] = ALPHA * dr
        _acc_add(i, accs[0], jnp.sum(du, axis=0, keepdims=True))
        _acc_add(i, accs[1], jnp.sum(du * x1, axis=0, keepdims=True))
        _acc_add(i, accs[2], jnp.sum(dx1 * xh, axis=0, keepdims=True))
        _acc_add(i, accs[3], jnp.sum(dx1, axis=0, keepdims=True))
        _acc_add(i, accs[4], jnp.sum(dr * mix, axis=0, keepdims=True))

    return _rows(body, "ln1_bwd", S, _pick(S, (256, 128)),
                 [("row", x, 0, D), ("row", mix, 0, D), ("row", dx1a, 0, D), ("row", du2, 0, D),
                  ("full", gate1), ("full", g), ("full", b), ("full", scale2)],
                 [(D, BF16), (D, F32)], [(1, D)] * 5)


def _rms_bwd(d_rq, d_rkv, pq, dkr, g_q, g_kv):
    S = pq.shape[0]

    def body(i, ins, outs, accs):
        dq_ref, dkv_ref, pq_ref, dkr_ref, gq_ref, gkv_ref = ins

        def rms_bwd(dy, x, g):
            r = lax.rsqrt(jnp.mean(x * x, axis=-1, keepdims=True) + RMS_EPS)
            dyg = dy * g
            dx = r * dyg - x * (r * r * r) * jnp.mean(dyg * x, axis=-1, keepdims=True)
            return dx, jnp.sum(dy * (x * r), axis=0, keepdims=True)

        dxq, dgq = rms_bwd(dq_ref[...], pq_ref[:, 0:Q_LORA], gq_ref[...])
        dxkv, dgkv = rms_bwd(dkv_ref[...], pq_ref[:, Q_LORA:Q_LORA + KV_LORA], gkv_ref[...])
        outs[0][:, 0:Q_LORA] = dxq.astype(BF16)
        outs[0][:, Q_LORA:Q_LORA + KV_LORA] = dxkv.astype(BF16)
        outs[0][:, Q_LORA + KV_LORA:QKV_A] = dkr_ref[...].astype(BF16)
        _acc_add(i, accs[0], dgq)
        _acc_add(i, accs[1], dgkv)

    return _rows(body, "rms_bwd", S, _pick(S, (256, 128)),
                 [("row", d_rq, 0, Q_LORA), ("row", d_rkv, 0, KV_LORA), ("row", pq, 0, QKV_A), ("row", dkr, 0, 128),
                  ("full", g_q), ("full", g_kv)], [(QKV_A, BF16)], [(1, Q_LORA), (1, KV_LORA)])


def _dx_final(dxa, du, x, scale1):
    S, D = x.shape

    def body(i, ins, outs, accs):
        du = ins[1][...]
        outs[0][...] = ins[0][...] + du * (1.0 + ins[3][...])
        _acc_add(i, accs[0], jnp.sum(du, axis=0, keepdims=True))
        _acc_add(i, accs[1], jnp.sum(du * ins[2][...], axis=0, keepdims=True))

    return _rows(body, "dx_final", S, _pick(S, (256, 128)),
                 [("row", dxa, 0, D), ("row", du, 0, D), ("row", x, 0, D), ("full", scale1)],
                 [(D, F32)], [(1, D), (1, D)])


def _ada_fwd(c_all, w, bias):
    B, D = c_all.shape
    NA = w.shape[1]
    tn = _pick(NA, (512, 256, 128))

    def body(c_ref, w_ref, b_ref, o_ref):
        cv = c_ref[...]
        ca = (cv * _sigmoid(cv)).astype(BF16)
        o_ref[...] = jnp.dot(ca, w_ref[...].astype(BF16), preferred_element_type=F32) + b_ref[...]

    return pl.pallas_call(
        body, name="ada_fwd", grid=(NA // tn,),
        in_specs=[pl.BlockSpec((B, D), lambda j: (0, 0)), pl.BlockSpec((D, tn), lambda j: (0, j)),
                  pl.BlockSpec((1, tn), lambda j: (0, j))],
        out_specs=pl.BlockSpec((B, tn), lambda j: (0, j)),
        out_shape=jax.ShapeDtypeStruct((B, NA), F32),
        compiler_params=_params(("arbitrary",)),
    )(c_all, w, bias)


def _ada_bwd(c_all, dmod):
    B, D = c_all.shape
    NA = dmod.shape[1]
    tn = _pick(NA, (512, 256, 128))

    def body(c_ref, d_ref, o_ref):
        cv = c_ref[...]
        ca = (cv * _sigmoid(cv)).astype(BF16)
        o_ref[...] = lax.dot_general(ca, d_ref[...].astype(BF16), TN, preferred_element_type=F32)

    return pl.pallas_call(
        body, name="ada_bwd", grid=(NA // tn,),
        in_specs=[pl.BlockSpec((B, D), lambda j: (0, 0)), pl.BlockSpec((B, tn), lambda j: (0, j))],
        out_specs=pl.BlockSpec((D, tn), lambda j: (0, j)),
        out_shape=jax.ShapeDtypeStruct((D, NA), F32),
        compiler_params=_params(("arbitrary",)),
    )(c_all, dmod)


def _pack_rows(parts, n_rows, after=()):
    N = parts[0].shape[1]
    n = len(parts)

    def body(*refs):
        o_ref = refs[-1]
        o_ref[...] = jnp.zeros_like(o_ref)
        at = 0
        for r in refs[:n]:
            o_ref[at:at + r.shape[0], :] = r[...]
            at += r.shape[0]

    vmem = pl.BlockSpec(memory_space=pltpu.VMEM)
    return pl.pallas_call(body, name="pack_small", out_shape=jax.ShapeDtypeStruct((n_rows, N), F32),
                          in_specs=[vmem] * n + [ANY] * len(after), out_specs=vmem,
                          compiler_params=_params())(*parts, *after)


def _sum8(parts):
    _, R, N = parts.shape

    def body(p_ref, o_ref):
        acc = p_ref[0]
        for d in range(1, 8):
            acc = acc + p_ref[d]
        o_ref[...] = acc

    return pl.pallas_call(body, name="sum8", out_shape=jax.ShapeDtypeStruct((R, N), F32),
                          compiler_params=_params())(parts)


def _adam_math(w, g, m, v):
    m = ADAM_B1 * m + (1.0 - ADAM_B1) * g
    v = ADAM_B2 * v + (1.0 - ADAM_B2) * (g * g)
    delta = -ADAM_LR * ((m / ADAM_C1) / (jnp.sqrt(v / ADAM_C2) + ADAM_EPS) + ADAM_WD * w)
    return delta, m, v


def _adam(name, w, m, v, g, carry=None):
    R, C = w.shape
    tm = _row_tile(R, C * 4, 1 << 20)
    steps = R // tm
    n_ci = len(carry.ins) if carry else 0
    n_co = len(carry.outs) if carry else 0

    def body(*refs):
        w_ref, m_ref, v_ref, g_ref = refs[:4]
        d_ref, nm_ref, nv_ref = refs[4 + n_ci:7 + n_ci]
        c_ins, c_outs, c_sems = refs[4:4 + n_ci], refs[7 + n_ci:7 + n_ci + n_co], refs[7 + n_ci + n_co:]
        if carry:
            @pl.when(pl.program_id(0) == 0)
            def _():
                carry.start(c_ins, c_outs, c_sems)

        delta, nm, nv = _adam_math(w_ref[...], g_ref[...], m_ref[...], v_ref[...])
        d_ref[...] = delta
        nm_ref[...] = nm
        nv_ref[...] = nv
        if carry:
            @pl.when(pl.program_id(0) == steps - 1)
            def _():
                carry.finish(c_ins, c_outs, c_sems)

    spec = pl.BlockSpec((tm, C), lambda i: (i, 0))
    res = pl.pallas_call(
        body, name=name, grid=(steps,), in_specs=[spec] * 4 + [ANY] * n_ci, out_specs=[spec] * 3 + [ANY] * n_co,
        out_shape=[jax.ShapeDtypeStruct((R, C), F32)] * 3 + (carry.outs if carry else []),
        scratch_shapes=carry.sems if carry else [],
        input_output_aliases=carry.io_aliases(4, 3) if carry else {},
        compiler_params=_params(("arbitrary",)),
    )(w, m, v, g, *(carry.ins if carry else []))
    return (res[:3], res[3:]) if carry else res


def _adam_halves(name, w, m, v, mine, other, core, carry=None):
    R, C = w.shape
    Rh = mine.shape[0]
    tc = max(t for t in range(128, C + 1, 128) if C % t == 0 and R * t <= (3 << 17))
    steps = C // tc
    n_ci = len(carry.ins) if carry else 0
    n_co = len(carry.outs) if carry else 0

    def body(*refs):
        c_ref, w_ref, m_ref, v_ref, a_ref, b_ref = refs[:6]
        g_ref, d_ref, nm_ref, nv_ref = refs[6 + n_ci:10 + n_ci]
        c_ins, c_outs, c_sems = refs[6:6 + n_ci], refs[10 + n_ci:10 + n_ci + n_co], refs[10 + n_ci + n_co:]
        if carry:
            @pl.when(pl.program_id(0) == 0)
            def _():
                carry.start(c_ins, c_outs, c_sems)

        first = c_ref[0] == 0
        g = jnp.concatenate([jnp.where(first, a_ref[...], b_ref[...]),
                             jnp.where(first, b_ref[0:R - Rh, :], a_ref[0:R - Rh, :])], axis=0)
        delta, nm, nv = _adam_math(w_ref[...], g, m_ref[...], v_ref[...])
        g_ref[...] = g
        d_ref[...] = delta
        nm_ref[...] = nm
        nv_ref[...] = nv
        if carry:
            @pl.when(pl.program_id(0) == steps - 1)
            def _():
                carry.finish(c_ins, c_outs, c_sems)

    spec = pl.BlockSpec((R, tc), lambda i, c_ref: (0, i))
    h_spec = pl.BlockSpec((Rh, tc), lambda i, c_ref: (0, i))
    res = pl.pallas_call(
        body, name=name, out_shape=[jax.ShapeDtypeStruct((R, C), F32)] * 4 + (carry.outs if carry else []),
        grid_spec=pltpu.PrefetchScalarGridSpec(
            num_scalar_prefetch=1, grid=(steps,), in_specs=[spec, spec, spec, h_spec, h_spec] + [ANY] * n_ci,
            out_specs=[spec] * 4 + [ANY] * n_co, scratch_shapes=carry.sems if carry else []),
        input_output_aliases=carry.io_aliases(6, 4) if carry else {},
        compiler_params=_params(("arbitrary",)),
    )(core, w, m, v, mine, other, *(carry.ins if carry else []))
    return (res[:4], res[4:]) if carry else res


def _adam_small(name, w, m, v, g):
    def body(w_ref, m_ref, v_ref, g_ref, d_ref, nm_ref, nv_ref):
        delta, nm, nv = _adam_math(w_ref[...], g_ref[...], m_ref[...], v_ref[...])
        d_ref[...] = delta
        nm_ref[...] = nm
        nv_ref[...] = nv

    return pl.pallas_call(body, name=name, out_shape=[jax.ShapeDtypeStruct(w.shape, F32)] * 3,
                          compiler_params=_params())(w, m, v, g)


def _place():
    return lax.axis_index("x"), lax.axis_index("y"), lax.axis_index("c")


def _other_chips(x, y):
    return [(1 - x, y), (x, 1 - y), (1 - x, 1 - y)]


def _all_gather8(blk, name):
    R, N = blk.shape

    def body(x_ref, out_ref, send_sems, recv_sems, local_sem):
        x, y, c = _place()
        me = 4 * x + 2 * y + c
        mine = pltpu.make_async_copy(x_ref, out_ref.at[me], local_sem)
        mine.start()
        flips = [(j >> 2 & 1, j >> 1 & 1, j & 1) for j in range(1, 8)]
        peers = [((1 - x) if fx else x, (1 - y) if fy else y, (1 - c) if fc else c) for fx, fy, fc in flips]
        sends = []
        for j, peer in enumerate(peers):
            cp = pltpu.make_async_remote_copy(src_ref=x_ref, dst_ref=out_ref.at[me], send_sem=send_sems.at[j],
                                              recv_sem=recv_sems.at[j], device_id=peer, device_id_type=MESH)
            cp.start()
            sends.append(cp)
        for j, (px, py, pc) in enumerate(peers):
            pltpu.make_async_remote_copy(src_ref=x_ref, dst_ref=out_ref.at[4 * px + 2 * py + pc],
                                         send_sem=send_sems.at[j], recv_sem=recv_sems.at[j],
                                         device_id=(px, py, pc), device_id_type=MESH).wait_recv()
        for cp in sends:
            cp.wait_send()
        mine.wait()

    return pl.pallas_call(
        body, name=name, out_shape=jax.ShapeDtypeStruct((8, R, N), F32),
        in_specs=[pl.BlockSpec(memory_space=pltpu.VMEM)], out_specs=pl.BlockSpec(memory_space=pltpu.VMEM),
        scratch_shapes=[pltpu.SemaphoreType.DMA((7,)), pltpu.SemaphoreType.DMA((7,)), pltpu.SemaphoreType.DMA],
        compiler_params=_params(),
    )(blk)


def _piece(rows, piece):
    i, n, k = piece if len(piece) == 3 else (piece[0], piece[1], 1)
    assert rows % 16 == 0 and rows // 16 >= n, (rows, piece)
    lo, hi = (rows // 16 * i // n) * 16, (rows // 16 * (i + k) // n) * 16
    return pl.ds(lo, hi - lo)


def _scatter_plan(arrs, piece=(0, 1), into=None):
    n = len(arrs)

    def copies(ins, outs, sems):
        send_sems, recv_sems = sems
        x, y, c = _place()
        chips = _other_chips(x, y)
        cps = []
        for k in range(n):
            rows = _piece(arrs[k].shape[1], piece)
            for j, (px, py) in enumerate(chips):
                cps.append(pltpu.make_async_remote_copy(
                    src_ref=ins[k].at[2 * px + py, rows], dst_ref=outs[k].at[j, rows],
                    send_sem=send_sems.at[3 * k + j], recv_sem=recv_sems.at[3 * k + j],
                    device_id=(px, py, c), device_id_type=MESH))
        return cps

    def start(ins, outs, sems):
        for cp in copies(ins, outs, sems):
            cp.start()

    def finish(ins, outs, sems):
        for cp in copies(ins, outs, sems):
            cp.wait()

    return _Plan(list(arrs) + list(into or []), [jax.ShapeDtypeStruct((3,) + a.shape[1:], a.dtype) for a in arrs],
                 [pltpu.SemaphoreType.DMA((3 * n,))] * 2, start, finish,
                 aliases={n + k: k for k in range(n)} if into else None)


def _gather_plan(shards, piece=(0, 1), into=None, ici=True):
    n = len(shards)

    def parts(ins, outs, sems):
        s1, r1, s2, r2, loc = sems
        x, y, c = _place()
        me = 2 * x + y
        chips = _other_chips(x, y)
        sib = (x, y, 1 - c)

        def rows(k):
            return _piece(shards[k].shape[1], piece)

        def ici_copy(k, j, slab, to):
            return pltpu.make_async_remote_copy(src_ref=ins[k].at[c, rows(k)], dst_ref=outs[k].at[slab, c, rows(k)],
                                                send_sem=s1.at[3 * k + j], recv_sem=r1.at[3 * k + j],
                                                device_id=to, device_id_type=MESH)

        def d2d(k, j, slab, half):
            return pltpu.make_async_remote_copy(src_ref=outs[k].at[slab, half, rows(k)],
                                                dst_ref=outs[k].at[slab, half, rows(k)],
                                                send_sem=s2.at[3 * k + j], recv_sem=r2.at[3 * k + j],
                                                device_id=sib, device_id_type=MESH)

        def own(k):
            return pltpu.make_async_remote_copy(src_ref=ins[k].at[:, rows(k)], dst_ref=outs[k].at[me, :, rows(k)],
                                                send_sem=loc.at[2 * k], recv_sem=loc.at[2 * k + 1],
                                                device_id=sib, device_id_type=MESH)

        return c, me, chips, ici_copy, d2d, own

    def start(ins, outs, sems):
        c, me, chips, ici_copy, d2d, own = parts(ins, outs, sems)
        for k in range(n):
            for j, (px, py) in enumerate(chips):
                (ici_copy(k, j, me, (px, py, c)) if ici else d2d(k, j, 2 * px + py, c)).start()
        for k in range(n):
            own(k).start()

    def finish(ins, outs, sems):
        c, me, chips, ici_copy, d2d, own = parts(ins, outs, sems)
        if ici:
            for k in range(n):
                for j, (px, py) in enumerate(chips):
                    ici_copy(k, j, 2 * px + py, (px, py, c)).wait_recv()
                    d2d(k, j, 2 * px + py, c).start()
        for k in range(n):
            for j, (px, py) in enumerate(chips):
                d2d(k, j, 2 * px + py, 1 - c).wait_recv()
        for k in range(n):
            own(k).wait()
            for j, (px, py) in enumerate(chips):
                if ici:
                    ici_copy(k, j, me, (px, py, c)).wait_send()
                d2d(k, j, 2 * px + py, c).wait_send()

    return _Plan(list(shards) + list(into or []), [jax.ShapeDtypeStruct((4,) + a.shape, a.dtype) for a in shards],
                 [pltpu.SemaphoreType.DMA((3 * n,))] * 4 + [pltpu.SemaphoreType.DMA((2 * n,))], start, finish,
                 aliases={n + k: k for k in range(n)} if into else None)


def _pair_plan(parts):
    n = len(parts)

    def copies(ins, outs, sems):
        send_sems, recv_sems = sems
        x, y, c = _place()
        return [pltpu.make_async_remote_copy(src_ref=ins[k].at[p, 1 - c], dst_ref=outs[k].at[p],
                                             send_sem=send_sems.at[4 * k + p], recv_sem=recv_sems.at[4 * k + p],
                                             device_id=(x, y, 1 - c), device_id_type=MESH)
                for k in range(n) for p in range(4)]

    def start(ins, outs, sems):
        for cp in copies(ins, outs, sems):
            cp.start()

    def finish(ins, outs, sems):
        for cp in copies(ins, outs, sems):
            cp.wait()

    return _Plan(parts, [jax.ShapeDtypeStruct((4,) + a.shape[2:], a.dtype) for a in parts],
                 [pltpu.SemaphoreType.DMA((4 * n,))] * 2, start, finish)


def _sibling_plan(arrs):
    n = len(arrs)

    def copies(ins, outs, sems):
        send_sems, recv_sems = sems
        x, y, c = _place()
        return [pltpu.make_async_remote_copy(src_ref=ins[k], dst_ref=outs[k], send_sem=send_sems.at[k],
                                             recv_sem=recv_sems.at[k], device_id=(x, y, 1 - c), device_id_type=MESH)
                for k in range(n)]

    def start(ins, outs, sems):
        for cp in copies(ins, outs, sems):
            cp.start()

    def finish(ins, outs, sems):
        for cp in copies(ins, outs, sems):
            cp.wait()

    return _Plan(arrs, [jax.ShapeDtypeStruct(a.shape, a.dtype) for a in arrs],
                 [pltpu.SemaphoreType.DMA((n,))] * 2, start, finish)


def _scatter_copies(arrs):
    def copies(ins, land, send_sems, recv_sems):
        x, y, c = _place()
        return [pltpu.make_async_remote_copy(src_ref=ins[k].at[2 * px + py], dst_ref=land[k].at[j],
                                             send_sem=send_sems.at[3 * k + j], recv_sem=recv_sems.at[3 * k + j],
                                             device_id=(px, py, c), device_id_type=MESH)
                for k in range(len(arrs)) for j, (px, py) in enumerate(_other_chips(x, y))]

    return copies, [lax.empty((3,) + a.shape[1:], a.dtype) for a in arrs]


def _gather_copies(shards):
    def copies(ins, land, send_sems, recv_sems):
        x, y, c = _place()
        return [pltpu.make_async_remote_copy(src_ref=ins[k].at[c], dst_ref=land[k].at[2 * x + y, c],
                                             send_sem=send_sems.at[3 * k + j], recv_sem=recv_sems.at[3 * k + j],
                                             device_id=(px, py, c), device_id_type=MESH)
                for k in range(len(shards)) for j, (px, py) in enumerate(_other_chips(x, y))]

    return copies, [lax.empty((4,) + a.shape, a.dtype) for a in shards]


def _split_start(arrs, copies_lands, ride, name, after=()):
    copies, lands = copies_lands
    n = len(arrs)
    rides = list(ride) if isinstance(ride, (list, tuple)) else [ride]
    n_thru = 2 * n + len(rides)

    def body(*refs):
        first_out = n_thru + len(after)
        for cp in copies(refs[:n], refs[n:2 * n], refs[first_out], refs[first_out + 1]):
            cp.start()

    hbm = [pltpu.with_memory_space_constraint(a, pltpu.HBM) for a in list(arrs) + lands + rides]
    res = pl.pallas_call(
        body, name=name,
        out_shape=[pltpu.SemaphoreType.DMA((3 * n,)), pltpu.SemaphoreType.DMA((3 * n,))]
        + [pltpu.HBM(a.shape, a.dtype) for a in hbm],
        in_specs=[HBM_SPEC] * n_thru + [ANY] * len(after),
        out_specs=[SEM_SPEC, SEM_SPEC] + [HBM_SPEC] * n_thru,
        input_output_aliases={i: 2 + i for i in range(n_thru)},
        compiler_params=pltpu.CompilerParams(has_side_effects=pltpu.SideEffectType.DATAFLOW_SIDE_EFFECTING),
    )(*hbm, *after)
    return res[0], res[1], res[2:2 + n], res[2 + n:2 + 2 * n], list(res[2 + 2 * n:])


def _split_wait(started, copies_lands, after, name):
    send_sems, recv_sems, arrs, lands, _ = started
    copies = copies_lands[0]
    n = len(arrs)

    def body(*refs):
        for cp in copies(refs[:n], refs[n:2 * n], refs[2 * n], refs[2 * n + 1]):
            cp.wait_send()
            cp.wait_recv()

    res = pl.pallas_call(
        body, name=name, out_shape=[pltpu.HBM(a.shape, a.dtype) for a in list(arrs) + list(lands)],
        in_specs=[HBM_SPEC] * (2 * n) + [SEM_SPEC, SEM_SPEC] + [ANY] * len(after), out_specs=[HBM_SPEC] * (2 * n),
        input_output_aliases={i: i for i in range(2 * n)},
        compiler_params=pltpu.CompilerParams(has_side_effects=pltpu.SideEffectType.DATAFLOW_SIDE_EFFECTING),
    )(*arrs, *lands, send_sems, recv_sems, *after)
    return list(res[:n]), list(res[n:])


def _join_plans(plans):
    def split(seq, counts):
        out, at = [], 0
        for cnt in counts:
            out.append(seq[at:at + cnt])
            at += cnt
        return out

    n_i, n_o, n_s = ([len(getattr(p, f)) for p in plans] for f in ("ins", "outs", "sems"))

    def start(ins, outs, sems):
        for p, i, o, s in zip(plans, split(ins, n_i), split(outs, n_o), split(sems, n_s)):
            p.start(i, o, s)

    def finish(ins, outs, sems):
        for p, i, o, s in zip(plans, split(ins, n_i), split(outs, n_o), split(sems, n_s)):
            p.finish(i, o, s)

    aliases, at_i, at_o = {}, 0, 0
    for p in plans:
        aliases.update(p.io_aliases(at_i, at_o))
        at_i, at_o = at_i + len(p.ins), at_o + len(p.outs)
    return _Plan(sum((p.ins for p in plans), []), sum((p.outs for p in plans), []), sum((p.sems for p in plans), []),
                 start, finish, aliases)


def _add_pair(parts, sib, core, name):
    P4, _, Rh, C = parts.shape
    tm, tc = _tile2(Rh, C, 16)

    def body(c_ref, a_ref, b_ref, o_ref):
        o_ref[...] = (a_ref[0].astype(F32) + b_ref[...].astype(F32)).astype(BF16)

    spec = pl.BlockSpec((1, tm, tc), lambda p, i, j, c_ref: (p, i, j))
    return pl.pallas_call(
        body, name=name, out_shape=jax.ShapeDtypeStruct((P4, Rh, C), BF16),
        grid_spec=pltpu.PrefetchScalarGridSpec(
            num_scalar_prefetch=1, grid=(P4, Rh // tm, C // tc),
            in_specs=[pl.BlockSpec((1, 1, tm, tc), lambda p, i, j, c_ref: (p, c_ref[0], i, j)), spec], out_specs=spec),
        compiler_params=_params(("parallel",) * 3),
    )(core, parts, sib)


def _sum_slabs(pre, recv, chip, name):
    _, Rh, C = pre.shape
    tm, tc = _tile2(Rh, C, 16)

    def body(me_ref, own_ref, r_ref, o_ref):
        acc = own_ref[0].astype(F32)
        for j in range(3):
            acc = acc + r_ref[j].astype(F32)
        o_ref[...] = acc

    return pl.pallas_call(
        body, name=name, out_shape=jax.ShapeDtypeStruct((Rh, C), F32),
        grid_spec=pltpu.PrefetchScalarGridSpec(
            num_scalar_prefetch=1, grid=(Rh // tm, C // tc),
            in_specs=[pl.BlockSpec((1, tm, tc), lambda i, j, me_ref: (me_ref[0], i, j)),
                      pl.BlockSpec((3, tm, tc), lambda i, j, me_ref: (0, i, j))],
            out_specs=pl.BlockSpec((tm, tc), lambda i, j, me_ref: (i, j))),
        compiler_params=_params(("parallel", "parallel")),
    )(chip, pre, recv)


def kernel(x, c, positions, w_ada, b_ada, w_in, g_q_a, w_q_b, g_kv_a, w_kv_b, w_o_a, w_conv, w_o_b, w_o, ln1_g, ln1_b, w_ffn_in, w_ffn_out, ln2_g, ln2_b, loss_target, m_w_ada, m_b_ada, m_w_in, m_g_q_a, m_w_q_b, m_g_kv_a, m_w_kv_b, m_w_o_a, m_w_conv, m_w_o_b, m_w_o, m_ln1_g, m_ln1_b, m_w_ffn_in, m_w_ffn_out, m_ln2_g, m_ln2_b, v_w_ada, v_b_ada, v_w_in, v_g_q_a, v_w_q_b, v_g_kv_a, v_w_kv_b, v_w_o_a, v_w_conv, v_w_o_b, v_w_o, v_ln1_g, v_ln1_b, v_w_ffn_in, v_w_ffn_out, v_ln2_g, v_ln2_b):
    S, D = x.shape[1], x.shape[2]
    F = w_ffn_out.shape[1] * 4
    ax, ay, ac = _place()
    chip = 2 * ax + ay
    dev = 4 * ax + 2 * ay + ac
    x2, tgt = x[0], loss_target[0]
    w_ada2, w_in2, w_q_b2, w_kv_b2 = w_ada[0], w_in[0], w_q_b[0], w_kv_b[0]
    w_o_a2, w_o_b2, w_o2, w_ffn_in2, w_ffn_out2 = w_o_a[0], w_o_b[0], w_o[0], w_ffn_in[0], w_ffn_out[0]
    NA = w_ada2.shape[1]
    CW = w_conv.shape[2]

    inv_freq = 1.0 / (ROPE_THETA ** (jnp.arange(0, QK_ROPE, 2, dtype=F32) / QK_ROPE))
    ang = positions[0].astype(F32)[:, None] * inv_freq
    cos, sin = jnp.cos(ang), jnp.sin(ang)
    z32, z64, z96 = jnp.zeros((S, 32), F32), jnp.zeros((S, 64), F32), jnp.zeros((S, 96), F32)
    tab = jnp.concatenate([cos, cos, z64, -sin, z96, z32, sin, z64], axis=1)

    def halves(a):
        return a.reshape(2, a.shape[0] // 2, a.shape[1])

    def whole(g):
        return g.reshape(4, 2 * g.shape[2], g.shape[3])

    def cols(g):
        return jnp.transpose(g, (1, 0, 2)).reshape(g.shape[1], 4 * g.shape[2])

    w_inT, m_w_inT, v_w_inT = w_in2.T, m_w_in[0].T, v_w_in[0].T
    CS = w_inT.shape[0]
    CSP = -(-CS // 32) * 32
    sh_in = halves(jnp.pad(w_inT.astype(BF16), ((0, CSP - CS), (0, 0))))
    sh_qb, sh_kvb, sh_oa, sh_ob, sh_o, sh_fi, sh_fo = (
        halves(w.astype(BF16)) for w in (w_q_b2, w_kv_b2, w_o_a2, w_o_b2, w_o2, w_ffn_in2, w_ffn_out2))
    c_all = _all_gather8(c, "gather_c").reshape(8, D)
    wconv_all = _all_gather8(w_conv[0], "gather_wconv")
    w_conv_full = jnp.transpose(wconv_all[0::2], (1, 0, 2)).reshape(3, D)
    b_sh = lax.dynamic_slice(b_ada, (0, chip * NA), (1, NA))
    mod_sh = _ada_fwd(c_all, w_ada2, b_sh)
    mod_all = _all_gather8(mod_sh, "gather_mod")
    mod = lax.dynamic_slice(mod_all[0::2], (0, dev, 0), (4, 1, NA)).reshape(6, D)
    shift1, scale1, gate1, shift2, scale2, gate2 = (mod[k:k + 1] for k in range(6))

    g_in, shift1, w_conv_full = _run_plan(_gather_plan([sh_in]), "gather_first", ride=[shift1, w_conv_full])
    g_in = whole(g_in)
    sh_a1, sh_a2 = [sh_qb, sh_kvb], [sh_oa, sh_ob, sh_o]
    cl_a1, cl_a2, cl_fi, cl_fo = (_gather_copies(g) for g in (sh_a1, sh_a2, [sh_fi], [sh_fo]))
    st_a1 = _split_start(sh_a1, cl_a1, shift1, "gather_a1_start")
    st_a2 = _split_start(sh_a2, cl_a2, st_a1[4], "gather_a2_start")
    shift1 = st_a2[4][0]

    def in_rows(lo, hi):
        parts = [g_in[p, max(lo, p * CS) - p * CS:min(hi, (p + 1) * CS) - p * CS]
                 for p in range(4) if max(lo, p * CS) < min(hi, (p + 1) * CS)]
        return parts[0] if len(parts) == 1 else jnp.concatenate(parts, axis=0)

    n_qkv = Q_LORA + KV_LORA + QK_ROPE
    W_qkvT = jnp.pad(in_rows(0, n_qkv), ((0, QKV_A - n_qkv), (0, 0)))
    W_convT = in_rows(n_qkv, n_qkv + 3 * D)
    W_gateT = in_rows(n_qkv + 3 * D, n_qkv + 5 * D)

    u = _modulate(x2, scale1, shift1, "modulate1")
    pq = _matmul(u, W_qkvT, "nt", F32, "proj_qkv")
    pc = _matmul(u, W_convT, "nt", F32, "proj_conv")
    sh_a1, la1 = _split_wait(st_a1, cl_a1, [pc], "gather_a1_wait")
    pg, (g_qb, g_kvb) = _matmul(u, W_gateT, "nt", BF16, "proj_gate", carry=_gather_plan(sh_a1, into=la1, ici=False))
    st_fi = _split_start([sh_fi], cl_fi, g_q_a, "gather_fi_start", after=[pg])
    W_qb = jnp.pad(cols(whole(g_qb)).reshape(Q_LORA, N_HEADS, QK_NOPE + QK_ROPE),
                   ((0, 0), (0, 0), (0, QK_PAD - QK_NOPE - QK_ROPE))).reshape(Q_LORA, N_HEADS * QK_PAD)
    W_kvb = cols(whole(g_kvb))
    rq, rkv, kr = _rms_fwd(pq, tab, st_fi[4][0], g_kv_a)
    kv = _matmul(rkv, W_kvb, "nn", BF16, "kv_b")
    sh_a2, la2 = _split_wait(st_a2, cl_a2, [kv], "gather_a2_wait")
    qf, (g_oa, g_ob, g_o) = _matmul(rq, W_qb, "nn", F32, "q_b", carry=_gather_plan(sh_a2, into=la2, ici=False))
    q = _q_rope(qf, tab)
    o, lse, _ = _attn_fwd(q, kv, kr)
    W_oa, W_ob, W_o = (g.reshape(-1, D) for g in (g_oa, g_ob, g_o))
    hb = _conv_fwd(pc, w_conv_full)
    sh_fi_t, lfi = _split_wait(st_fi, cl_fi, [o], "gather_fi_wait")
    y_b, g_fi = _matmul(hb, W_ob, "nn", F32, "o_b", carry=_gather_plan(sh_fi_t, (0, 2), into=lfi, ici=False))
    y_a, (g_fi,) = _matmul(o, W_oa, "nn", F32, "o_a", carry=_gather_plan(sh_fi_t, (1, 2), into=g_fi, ici=False))
    st_fo = _split_start([sh_fo], cl_fo, ln1_g, "gather_fo_start", after=[y_b])
    merged = _merge_fwd(y_a, y_b, pg)
    mix = _matmul(merged, W_o, "nn", F32, "w_o")
    W_fi = whole(g_fi)
    x1, u2 = _ln1_fwd(x2, mix, gate1, st_fo[4][0], ln1_b, scale2, shift2)
    hh = _matmul(u2, W_fi, "nn", BF16, "ffn_in", shards="b")
    sh_fo_t, lfo = _split_wait(st_fo, cl_fo, [hh], "gather_fo_wait")
    act, (g_fo,) = _swiglu_fwd(hh, carry=_gather_plan(sh_fo_t, into=lfo, ici=False))
    W_fo = g_fo.reshape(F, D)
    ffn = _matmul(act, W_fo, "nn", F32, "ffn_out")

    core_i = ac.astype(jnp.int32).reshape(1)
    chip_i = chip.astype(jnp.int32).reshape(1)

    def uncols(g):
        return jnp.transpose(g.reshape(g.shape[0], 4, g.shape[1] // 4), (1, 0, 2))

    def slabs(p):
        return p.reshape(4, 2, p.shape[1] // 2, p.shape[2])

    def add_pairs(parts, sibs, nms):
        return [_add_pair(a, b, core_i, "add_pair_" + nm) for a, b, nm in zip(parts, sibs, nms)]

    def sum_all(pre, recv, nms):
        return [_sum_slabs(a, r, chip_i, "sum_slabs_" + nm) for a, r, nm in zip(pre, recv, nms)]

    dffn, dx1a, loss_acc, d_ln2_g, d_ln2_b, d_gate2 = _ln2_loss_bwd(x1, ffn, gate2, ln2_g, ln2_b, tgt)
    loss = lax.psum(loss_acc[0, 0], ("x", "y", "c"))
    dW_fo = _matmul(act, dffn, "tn", BF16, "d_w_ffn_out")
    p_fo = [slabs(dW_fo.reshape(4, -1, D))]
    dact, s_fo = _matmul(dffn, W_fo, "nt", BF16, "d_act", carry=_pair_plan(p_fo))
    pre_fo = add_pairs(p_fo, s_fo, ["w_ffn_out"])
    cs_fo = _scatter_copies(pre_fo)
    st_sfo = _split_start(pre_fo, cs_fo, scale2, "scatter_fo_start")
    dhh = _swiglu_bwd(dact, hh)
    dW_fi = _matmul(u2, dhh, "tn", BF16, "d_w_ffn_in", shards="o")
    p_fi = [slabs(dW_fi)]
    du2, s_fi = _matmul(dhh, W_fi, "nt", F32, "d_u2", carry=_pair_plan(p_fi), shards="b")
    pre_fi = add_pairs(p_fi, s_fi, ["w_ffn_in"])
    cs_fi = _scatter_copies(pre_fi)
    st_sfi = _split_start(pre_fi, cs_fi, st_sfo[4], "scatter_fi_start")
    dmix, dxa, d_shift2, d_scale2, d_ln1_g, d_ln1_b, d_gate1 = _ln1_bwd(x2, mix, dx1a, du2, gate1, ln1_g, ln1_b, st_sfi[4][0])
    dW_o = _matmul(merged, dmix, "tn", BF16, "d_w_o")
    dmerged = _matmul(dmix, W_o, "nt", F32, "d_merged")
    dy_a, dy_b, dgate = _merge_bwd(dmerged, y_a, y_b, pg)
    dW_oa = _matmul(o, dy_a, "tn", BF16, "d_w_o_a")
    do = _matmul(dy_a, W_oa, "nt", BF16, "d_o")
    dW_ob = _matmul(hb, dy_b, "tn", BF16, "d_w_o_b")
    p_mid = [slabs(g.reshape(4, -1, D)) for g in (dW_oa, dW_ob, dW_o)]
    dhb, s_mid = _matmul(dy_b, W_ob, "nt", F32, "d_hb", carry=_pair_plan(p_mid))
    pre_mid = add_pairs(p_mid, s_mid, ["w_o_a", "w_o_b", "w_o"])
    cs_mid = _scatter_copies(pre_mid)
    st_smid = _split_start(pre_mid, cs_mid, w_conv_full, "scatter_mid_start")
    dconv, d_wconv = _conv_bwd(dhb, pc, st_smid[4][0])
    dq, dkv, dkr, _ = _attn_bwd(q, kv, kr, do, o, lse, tab, carry=_token_plan(st_smid[4][0]))
    names_a = ["w_ffn_out", "w_ffn_in", "w_o_a", "w_o_b", "w_o"]
    dW_qb = _matmul(rq, dq, "tn", BF16, "d_w_q_b")
    d_rq = _matmul(dq, W_qb, "nt", F32, "d_rq")
    dW_kvb = _matmul(rkv, dkv, "tn", BF16, "d_w_kv_b")
    d_rkv = _matmul(dkv, W_kvb, "nt", F32, "d_rkv")
    dqkv, d_g_q, d_g_kv = _rms_bwd(d_rq, d_rkv, pq, dkr, g_q_a, g_kv_a)
    dW_qkvT = _matmul(dqkv, u, "tn", BF16, "d_w_qkv")
    dW_convT = _matmul(dconv, u, "tn", BF16, "d_w_conv")
    dW_gateT = _matmul(dgate, u, "tn", BF16, "d_w_gate")
    pre_fo, r_fo = _split_wait(st_sfo, cs_fo, [dW_qkvT], "scatter_fo_wait")
    pre_fi, r_fi = _split_wait(st_sfi, cs_fi, [dW_qkvT], "scatter_fi_wait")
    pre_mid, r_mid = _split_wait(st_smid, cs_mid, [dW_qkvT], "scatter_mid_wait")
    fin_a = sum_all(pre_fo + pre_fi + pre_mid, r_fo + r_fi + r_mid, names_a)
    dW_inT = jnp.concatenate([dW_qkvT[:n_qkv], dW_convT, dW_gateT], axis=0).reshape(4, CS, D)
    dW_inT = jnp.pad(dW_inT, ((0, 0), (0, CSP - CS), (0, 0)))
    dW_qb_u = dW_qb.reshape(Q_LORA, N_HEADS, QK_PAD)[:, :, :QK_NOPE + QK_ROPE].reshape(Q_LORA, -1)
    names_b = ["w_in", "w_q_b", "w_kv_b"]
    p_b = [slabs(dW_inT), slabs(uncols(dW_qb_u)), slabs(uncols(dW_kvb))]
    du, s_b = _matmul(dqkv, W_qkvT, "nn", F32, "d_u_qkv", carry=_pair_plan(p_b))
    pre_b = add_pairs(p_b, s_b, names_b)
    cs_b = _scatter_copies(pre_b)
    st_b = _split_start(pre_b, cs_b, scale1, "scatter_last_start")
    du, fs_a = _matmul(dconv, W_convT, "nn", F32, "d_u_conv", add=du, carry=_sibling_plan(fin_a))
    du = _matmul(dgate, W_gateT, "nn", F32, "d_u_gate", add=du)
    grad_x, d_shift1, d_scale1 = _dx_final(dxa, du, x2, st_b[4][0])

    big = {}
    ws = dict(w_in=(w_inT, m_w_inT, v_w_inT), w_q_b=(w_q_b2, m_w_q_b[0], v_w_q_b[0]),
              w_kv_b=(w_kv_b2, m_w_kv_b[0], v_w_kv_b[0]), w_o_a=(w_o_a2, m_w_o_a[0], v_w_o_a[0]),
              w_o_b=(w_o_b2, m_w_o_b[0], v_w_o_b[0]), w_o=(w_o2, m_w_o[0], v_w_o[0]),
              w_ffn_in=(w_ffn_in2, m_w_ffn_in[0], v_w_ffn_in[0]), w_ffn_out=(w_ffn_out2, m_w_ffn_out[0], v_w_ffn_out[0]))

    def adam_of(nm, a, b, carry=None):
        w_, m_, v_ = ws[nm]
        return _adam_halves("adam_" + nm, w_, m_, v_, a, b, core_i, carry)

    for nm, a, b in zip(names_a, fin_a, fs_a):
        big[nm] = adam_of(nm, a, b, _token_plan(st_b[4][0]))[0]
    done = [big[nm][1] for nm in names_a] + [grad_x]
    pre_b, r_b = _split_wait(st_b, cs_b, done, "scatter_last_wait")
    fin_b = sum_all(pre_b, r_b, names_b)
    fs_b = _run_plan(_sibling_plan(fin_b), "sibling_last")
    for nm, a, b in zip(names_b, fin_b, fs_b):
        big[nm] = adam_of(nm, a, b)

    def pad_d(v):
        return jnp.pad(v, ((0, 0), (0, D - v.shape[1])))

    small = _pack_rows([d_ln1_g, d_ln1_b, d_ln2_g, d_ln2_b, pad_d(d_g_q), pad_d(d_g_kv), d_wconv,
                         d_shift1, d_scale1, d_gate1, d_shift2, d_scale2, d_gate2], 16, after=[pre_b[1]])
    small_all = _all_gather8(small, "gather_small")
    small_sum = _sum8(small_all)
    g_ln1_g, g_ln1_b, g_ln2_g, g_ln2_b = (small_sum[k:k + 1] for k in range(4))
    g_g_q, g_g_kv = small_sum[4:5, :Q_LORA], small_sum[5:6, :KV_LORA]
    g_wconv = lax.dynamic_slice(small_sum[6:9], (0, chip * CW), (3, CW))
    g_b_ada = small_sum[9:15].reshape(1, 6 * D)
    dmod_all = small_all[:, 9:15, :].reshape(8, 6 * D)
    g_w_ada = _ada_bwd(c_all, lax.dynamic_slice(dmod_all, (0, chip * NA), (8, NA)))
    big["w_ada"] = [g_w_ada] + list(_adam("adam_w_ada", w_ada2, m_w_ada[0], v_w_ada[0], g_w_ada))
    sm = {}
    for nm, w_, m_, v_, g_ in [("b_ada", b_ada, m_b_ada, v_b_ada, g_b_ada), ("g_q_a", g_q_a, m_g_q_a, v_g_q_a, g_g_q),
                               ("g_kv_a", g_kv_a, m_g_kv_a, v_g_kv_a, g_g_kv),
                               ("w_conv", w_conv[0], m_w_conv[0], v_w_conv[0], g_wconv),
                               ("ln1_g", ln1_g, m_ln1_g, v_ln1_g, g_ln1_g), ("ln1_b", ln1_b, m_ln1_b, v_ln1_b, g_ln1_b),
                               ("ln2_g", ln2_g, m_ln2_g, v_ln2_g, g_ln2_g), ("ln2_b", ln2_b, m_ln2_b, v_ln2_b, g_ln2_b)]:
        sm[nm] = (g_,) + tuple(_adam_small("adam_" + nm, w_, m_, v_, g_))

    order = ["w_ada", "b_ada", "w_in", "g_q_a", "w_q_b", "g_kv_a", "w_kv_b", "w_o_a", "w_conv", "w_o_b", "w_o",
             "ln1_g", "ln1_b", "w_ffn_in", "w_ffn_out", "ln2_g", "ln2_b"]
    lead = {"b_ada", "g_q_a", "g_kv_a", "ln1_g", "ln1_b", "ln2_g", "ln2_b"}

    def leaf(nm, k):
        val = big[nm][k] if nm in big else sm[nm][k]
        if nm == "w_in":
            val = val.T
        return val if nm in lead else val[None]

    outs = [loss, grad_x[None]]
    for k in range(4):
        outs += [leaf(nm, k) for nm in order]
    return tuple(outs)
```

```python
import functools

import jax
import jax.numpy as jnp
from jax import lax
from jax.experimental import pallas as pl
from jax.experimental.pallas import tpu as pltpu

F32, BF16 = jnp.float32, jnp.bfloat16
N_HEADS, QK_NOPE, QK_ROPE, V_HEAD = 16, 128, 64, 128
Q_LORA, KV_LORA = 512, 512
QK_PAD = 256
QKV_A = 1152
CHUNK_SHIFT = 6
ATTN_SCALE = (QK_NOPE + QK_ROPE) ** -0.5
ROPE_THETA = 10000.0
ALPHA = 2.0 ** 0.25
LN_EPS, RMS_EPS = 1e-5, 1e-6
ADAM_LR, ADAM_B1, ADAM_B2, ADAM_EPS, ADAM_WD, ADAM_STEP = 0.001, 0.9, 0.999, 1e-08, 0.01, 10
ADAM_C1 = 1.0 - ADAM_B1 ** ADAM_STEP
ADAM_C2 = 1.0 - ADAM_B2 ** ADAM_STEP
VMEM_LIMIT = 56 * 1024 * 1024
MESH = pl.DeviceIdType.MESH
ANY = pl.BlockSpec(memory_space=pl.ANY)
HBM_SPEC = pl.BlockSpec(memory_space=pltpu.HBM)
SEM_SPEC = pl.BlockSpec(memory_space=pltpu.SEMAPHORE)
NT = (((1,), (1,)), ((), ()))
TN = (((0,), (0,)), ((), ()))
NN = (((1,), (0,)), ((), ()))


def _params(sem=None):
    return pltpu.CompilerParams(dimension_semantics=sem, vmem_limit_bytes=VMEM_LIMIT)


def _pick(n, cands=(1408, 1024, 512, 384, 256, 128)):
    for t in cands:
        if n % t == 0:
            return t
    return n


def _row_tile(rows, row_bytes, budget, mult=8):
    best = mult
    for t in range(mult, rows + 1, mult):
        if rows % t == 0 and t * row_bytes <= budget:
            best = t
    return best


def _tile2(rows, cols, mult=8, budget=3 << 18):
    col_tiles = [t for t in range(128, cols + 1, 128) if cols % t == 0] or [cols]
    best = None
    for tc in col_tiles:
        for tr in range(mult, rows + 1, mult):
            if rows % tr == 0 and tr * tc <= budget and (best is None or (tr * tc, tc) > (best[0] * best[1], best[1])):
                best = (tr, tc)
    assert best is not None, (rows, cols)
    return best


def _sigmoid(x):
    return jax.nn.sigmoid(x)


class _Plan:
    def __init__(self, ins, outs, sems, start, finish, aliases=None):
        self.ins, self.outs, self.sems, self.start, self.finish = list(ins), list(outs), list(sems), start, finish
        self.aliases = dict(aliases or {})

    def io_aliases(self, first_in, first_out):
        return {first_in + i: first_out + o for i, o in self.aliases.items()}


def _token_plan(token):
    return _Plan([token], [], [], lambda *a: None, lambda *a: None)


def _run_plan(plan, name, ride=None):
    n_in, n_out = len(plan.ins), len(plan.outs)
    extra = [] if ride is None else list(ride)
    aliases = plan.io_aliases(0, 0)
    for k in range(len(extra)):
        aliases[n_in + k] = n_out + k

    def body(*refs):
        ins, outs, sems = refs[:n_in], refs[n_in + len(extra):n_in + len(extra) + n_out], refs[n_in + 2 * len(extra) + n_out:]
        plan.start(ins, outs, sems)
        plan.finish(ins, outs, sems)

    return pl.pallas_call(body, name=name, out_shape=plan.outs + [jax.ShapeDtypeStruct(r.shape, r.dtype) for r in extra],
                          in_specs=[ANY] * (n_in + len(extra)), out_specs=[ANY] * (n_out + len(extra)),
                          scratch_shapes=plan.sems, input_output_aliases=aliases,
                          compiler_params=_params())(*plan.ins, *extra)


def _matmul(a, b, mode, out_dtype, name, add=None, carry=None, shards=None):
    if mode == "nn":
        (M, K), N, dims = a.shape, b.shape[-1] * (4 if shards else 1), NN
    elif mode == "nt":
        (M, K), N, dims = a.shape, b.shape[-2], NT
    else:
        (K, M), N, dims = a.shape, b.shape[1], TN
    split_n = shards and mode != "nt"
    tm = _pick(M)
    tn = _pick(N // 4) if split_n else _pick(N)
    if shards and mode == "nt":
        tk = _pick(K // 4)
    else:
        tk = K if K <= 2048 else _pick(K)
    nk = K // tk
    per = (N // 4 // tn) if split_n else (K // 4 // tk if shards else 1)
    a_spec = (pl.BlockSpec((tk, tm), lambda i, j, k: (k, i)) if mode == "tn"
              else pl.BlockSpec((tm, tk), lambda i, j, k: (i, k)))
    if shards == "b" and mode == "nn":
        b_spec = pl.BlockSpec((None, tk, tn), lambda i, j, k: (j // per, k, j % per))
    elif shards == "b":
        b_spec = pl.BlockSpec((None, tn, tk), lambda i, j, k: (k // per, j, k % per))
    else:
        b_spec = (pl.BlockSpec((tn, tk), lambda i, j, k: (j, k)) if mode == "nt"
                  else pl.BlockSpec((tk, tn), lambda i, j, k: (k, j)))
    o_spec = pl.BlockSpec((tm, tn), lambda i, j, k: (i, j))
    o_shape = (M, N)
    if shards == "o":
        o_spec, o_shape = pl.BlockSpec((None, tm, tn), lambda i, j, k: (j // per, i, j % per)), (4, M, N // 4)
    has_add = add is not None
    n_ci = len(carry.ins) if carry else 0
    n_co = len(carry.outs) if carry else 0
    n_in = 2 + has_add
    grid = (M // tm, N // tn, nk)

    def body(*refs):
        a_ref, b_ref = refs[0], refs[1]
        add_ref = refs[2] if has_add else None
        o_ref = refs[n_in + n_ci]
        acc_ref = refs[n_in + n_ci + 1 + n_co] if nk > 1 else None
        c_ins = refs[n_in:n_in + n_ci]
        c_outs = refs[n_in + n_ci + 1:n_in + n_ci + 1 + n_co]
        c_sems = refs[n_in + n_ci + 1 + n_co + (nk > 1):]
        i, j, k = pl.program_id(0), pl.program_id(1), pl.program_id(2)

        if carry:
            @pl.when((i == 0) & (j == 0) & (k == 0))
            def _():
                carry.start(c_ins, c_outs, c_sems)

        part = lax.dot_general(a_ref[...], b_ref[...], dims, preferred_element_type=F32)
        if nk == 1:
            o_ref[...] = (part + add_ref[...] if has_add else part).astype(o_ref.dtype)
        else:
            @pl.when(k == 0)
            def _():
                acc_ref[...] = part

            @pl.when((k > 0) & (k < nk - 1))
            def _():
                acc_ref[...] += part

            @pl.when(k == nk - 1)
            def _():
                r = acc_ref[...] + part
                if has_add:
                    r = r + add_ref[...]
                o_ref[...] = r.astype(o_ref.dtype)

        if carry:
            @pl.when((i == grid[0] - 1) & (j == grid[1] - 1) & (k == nk - 1))
            def _():
                carry.finish(c_ins, c_outs, c_sems)

    ins = [a, b] + ([add] if has_add else []) + (carry.ins if carry else [])
    in_specs = [a_spec, b_spec] + ([o_spec] if has_add else []) + [ANY] * n_ci
    res = pl.pallas_call(
        body, name=name, grid=grid,
        in_specs=in_specs, out_specs=[o_spec] + [ANY] * n_co,
        out_shape=[jax.ShapeDtypeStruct(o_shape, out_dtype)] + (carry.outs if carry else []),
        scratch_shapes=([pltpu.VMEM((tm, tn), F32)] if nk > 1 else []) + (carry.sems if carry else []),
        input_output_aliases=carry.io_aliases(n_in, 1) if carry else {},
        compiler_params=_params(("arbitrary",) * 3 if carry else ("parallel", "parallel", "arbitrary")),
    )(*ins)
    return (res[0], res[1:]) if carry else res[0]


def _rows(body, name, n_rows, tm, ins, outs, accs=(), carry=None):
    grid = (n_rows // tm,)
    per8 = tm // 8
    last8 = n_rows // 8 - 1
    arrays, in_specs = [], []
    for spec in ins:
        kind, arr = spec[0], spec[1]
        arrays.append(arr)
        if kind == "row":
            _, _, cb, w = spec
            in_specs.append(pl.BlockSpec((tm, w), lambda i, cb=cb: (i, cb)))
        elif kind == "full":
            in_specs.append(pl.BlockSpec(arr.shape, lambda i, nd=arr.ndim: (0,) * nd))
        elif kind == "prev":
            _, _, cb, w = spec
            in_specs.append(pl.BlockSpec((8, w), lambda i, cb=cb: (jnp.maximum(i * per8 - 1, 0), cb)))
        else:
            _, _, cb, w = spec
            in_specs.append(pl.BlockSpec((8, w), lambda i, cb=cb: (jnp.minimum((i + 1) * per8, last8), cb)))
    out_shape = [jax.ShapeDtypeStruct((n_rows, w), dt) for (w, dt) in outs]
    out_specs = [pl.BlockSpec((tm, w), lambda i: (i, 0)) for (w, _) in outs]
    out_shape += [jax.ShapeDtypeStruct(s, F32) for s in accs]
    out_specs += [pl.BlockSpec(s, lambda i, nd=len(s): (0,) * nd) for s in accs]
    n_in, n_out, n_acc = len(ins), len(outs), len(accs)
    n_ci = len(carry.ins) if carry else 0
    n_co = len(carry.outs) if carry else 0

    def kernel_body(*refs):
        first = n_in + n_ci
        c_ins, c_outs, c_sems = refs[n_in:first], refs[first + n_out + n_acc:first + n_out + n_acc + n_co], refs[first + n_out + n_acc + n_co:]
        if carry:
            @pl.when(pl.program_id(0) == 0)
            def _():
                carry.start(c_ins, c_outs, c_sems)

        body(pl.program_id(0), refs[:n_in], refs[first:first + n_out], refs[first + n_out:first + n_out + n_acc])
        if carry:
            @pl.when(pl.program_id(0) == grid[0] - 1)
            def _():
                carry.finish(c_ins, c_outs, c_sems)

    res = pl.pallas_call(
        kernel_body, name=name, grid=grid, in_specs=in_specs + [ANY] * n_ci, out_specs=out_specs + [ANY] * n_co,
        out_shape=out_shape + (carry.outs if carry else []), scratch_shapes=carry.sems if carry else [],
        input_output_aliases=carry.io_aliases(n_in, n_out + n_acc) if carry else {},
        compiler_params=_params(("arbitrary",)),
    )(*arrays, *(carry.ins if carry else []))
    return (res[:n_out + n_acc], res[n_out + n_acc:]) if carry else res


def _acc_add(i, ref, val):
    @pl.when(i == 0)
    def _():
        ref[...] = val

    @pl.when(i > 0)
    def _():
        ref[...] += val


def _rope(t, tab, sign):
    c, sa, sb = tab[:, 0:128], tab[:, 128:256], tab[:, 256:384]
    rot = pltpu.roll(t, 96, 1) * sa + pltpu.roll(t, 32, 1) * sb
    return t * c + rot if sign > 0 else t * c - rot


def _ln_stats(r):
    mu = jnp.mean(r, axis=-1, keepdims=True)
    d = r - mu
    var = jnp.mean(d * d, axis=-1, keepdims=True)
    rstd = lax.rsqrt(var + LN_EPS)
    return d * rstd, rstd


def _ln_bwd(dxh, xh, rstd):
    m1 = jnp.mean(dxh, axis=-1, keepdims=True)
    m2 = jnp.mean(dxh * xh, axis=-1, keepdims=True)
    return rstd * (dxh - m1 - xh * m2)


def _modulate(x, scale, shift, name):
    S, D = x.shape

    def body(i, ins, outs, accs):
        outs[0][...] = (ins[0][...] * (1.0 + ins[1][...]) + ins[2][...]).astype(BF16)

    return _rows(body, name, S, _pick(S, (256, 128)), [("row", x, 0, D), ("full", scale), ("full", shift)], [(D, BF16)])[0]


def _rms_fwd(pq, tab, g_q, g_kv):
    S = pq.shape[0]

    def body(i, ins, outs, accs):
        pq_ref, tab_ref, gq_ref, gkv_ref = ins

        def rms(x, g):
            return x * lax.rsqrt(jnp.mean(x * x, axis=-1, keepdims=True) + RMS_EPS) * g

        outs[0][...] = rms(pq_ref[:, 0:Q_LORA], gq_ref[...]).astype(BF16)
        outs[1][...] = rms(pq_ref[:, Q_LORA:Q_LORA + KV_LORA], gkv_ref[...]).astype(BF16)
        outs[2][...] = _rope(pq_ref[:, Q_LORA + KV_LORA:QKV_A], tab_ref[...], 1).astype(BF16)

    return _rows(body, "rms_fwd", S, _pick(S, (256, 128)),
                 [("row", pq, 0, QKV_A), ("row", tab, 0, 384), ("full", g_q), ("full", g_kv)],
                 [(Q_LORA, BF16), (KV_LORA, BF16), (128, BF16)])


def _q_rope(q, tab):
    S, W = q.shape

    def body(i, ins, outs, accs):
        q_ref, tab_ref = ins
        t = tab_ref[...]
        for h in range(N_HEADS):
            lo = h * QK_PAD
            outs[0][:, lo:lo + 128] = q_ref[:, lo:lo + 128].astype(BF16)
            outs[0][:, lo + 128:lo + 256] = _rope(q_ref[:, lo + 128:lo + 256], t, 1).astype(BF16)

    return _rows(body, "q_rope", S, _pick(S, (256, 128)), [("row", q, 0, W), ("row", tab, 0, 384)], [(W, BF16)])[0]


def _allowed(q0, k0, bq):
    row = q0 + lax.broadcasted_iota(jnp.int32, (bq, bq), 0)
    col = k0 + lax.broadcasted_iota(jnp.int32, (bq, bq), 1)
    return (col >> CHUNK_SHIFT) <= (row >> CHUNK_SHIFT)


ATTN_BLOCK = 512


def _attn_fwd(q, kv, kr, carry=None):
    S = q.shape[0]
    bq = min(ATTN_BLOCK, S)
    nq = S // bq
    n_ci = len(carry.ins) if carry else 0
    n_co = len(carry.outs) if carry else 0

    def body(*refs):
        q_ref, kn_ref, v_ref, kr_ref = refs[:4]
        o_ref, lse_ref = refs[4 + n_ci:6 + n_ci]
        c_ins, c_outs = refs[4:4 + n_ci], refs[6 + n_ci:6 + n_ci + n_co]
        kcat = refs[6 + n_ci + n_co]
        c_sems = refs[7 + n_ci + n_co:]
        qi = pl.program_id(1)
        if carry:
            @pl.when((pl.program_id(0) == 0) & (qi == 0))
            def _():
                carry.start(c_ins, c_outs, c_sems)

        @pl.when(qi == 0)
        def _():
            kcat[:, 0:128] = kn_ref[...]
            kcat[:, 128:256] = kr_ref[...]

        qv = q_ref[...]

        def step(j, carry, masked):
            m, l, acc = carry
            off = pl.multiple_of(j * bq, bq)
            s = lax.dot_general(qv, kcat[pl.ds(off, bq), :], NT, preferred_element_type=F32) * ATTN_SCALE
            if masked:
                s = jnp.where(_allowed(qi * bq, off, bq), s, -1e30)
            m_new = jnp.maximum(m, jnp.max(s, axis=1, keepdims=True))
            a = jnp.exp(m - m_new)
            p = jnp.exp(s - m_new)
            l = a * l + jnp.sum(p, axis=1, keepdims=True)
            acc = a * acc + jnp.dot(p.astype(BF16), v_ref[pl.ds(off, bq), :], preferred_element_type=F32)
            return m_new, l, acc

        init = (jnp.full((bq, 1), -1e30, F32), jnp.zeros((bq, 1), F32), jnp.zeros((bq, V_HEAD), F32))
        below = lax.fori_loop(0, qi, lambda j, cr: step(j, cr, False), init)
        m, l, acc = step(qi, below, True)
        o_ref[...] = (acc / l).astype(BF16)
        lse_ref[0] = m + jnp.log(l)
        if carry:
            @pl.when((pl.program_id(0) == N_HEADS - 1) & (qi == nq - 1))
            def _():
                carry.finish(c_ins, c_outs, c_sems)

    res = pl.pallas_call(
        body, name="attn_fwd", grid=(N_HEADS, nq),
        in_specs=[pl.BlockSpec((bq, QK_PAD), lambda h, i: (i, h)),
                  pl.BlockSpec((S, 128), lambda h, i: (0, 2 * h)),
                  pl.BlockSpec((S, 128), lambda h, i: (0, 2 * h + 1)),
                  pl.BlockSpec((S, 128), lambda h, i: (0, 0))] + [ANY] * n_ci,
        out_specs=[pl.BlockSpec((bq, V_HEAD), lambda h, i: (i, h)),
                   pl.BlockSpec((1, bq, 1), lambda h, i: (h, i, 0))] + [ANY] * n_co,
        out_shape=[jax.ShapeDtypeStruct((S, N_HEADS * V_HEAD), BF16),
                   jax.ShapeDtypeStruct((N_HEADS, S, 1), F32)] + (carry.outs if carry else []),
        scratch_shapes=[pltpu.VMEM((S, QK_PAD), BF16)] + (carry.sems if carry else []),
        input_output_aliases=carry.io_aliases(4, 2) if carry else {},
        compiler_params=_params(("arbitrary", "arbitrary")),
    )(q, kv, kv, kr, *(carry.ins if carry else []))
    return res[0], res[1], res[2:]


def _attn_bwd(q, kv, kr, do, o, lse, tab, carry=None):
    S = q.shape[0]
    bq = min(ATTN_BLOCK, S)
    nq = S // bq

    n_ci = len(carry.ins) if carry else 0
    n_co = len(carry.outs) if carry else 0

    def body(*refs):
        q_ref, kn_ref, v_ref, kr_ref, do_ref, o_ref, lse_ref, tab_ref = refs[:8]
        dq_ref, dkv_ref, dkr_ref = refs[8 + n_ci:11 + n_ci]
        dq_acc, dk_acc, dv_acc, kcat, delta = refs[11 + n_ci + n_co:16 + n_ci + n_co]
        c_ins, c_outs, c_sems = refs[8:8 + n_ci], refs[11 + n_ci:11 + n_ci + n_co], refs[16 + n_ci + n_co:]
        h = pl.program_id(0)
        if carry:
            @pl.when(h == 0)
            def _():
                carry.start(c_ins, c_outs, c_sems)

        dq_acc[...] = jnp.zeros_like(dq_acc)
        dk_acc[...] = jnp.zeros_like(dk_acc)
        dv_acc[...] = jnp.zeros_like(dv_acc)
        kcat[:, 0:128] = kn_ref[...]
        kcat[:, 128:256] = kr_ref[...]
        for r in range(nq):
            rows = slice(r * bq, (r + 1) * bq)
            delta[rows, :] = jnp.sum(do_ref[rows, :].astype(F32) * o_ref[rows, :].astype(F32), axis=1, keepdims=True)

        def pair(i, j, masked):
            rows_i = pl.ds(pl.multiple_of(i * bq, bq), bq)
            rows_j = pl.ds(pl.multiple_of(j * bq, bq), bq)
            qv, dov, k = q_ref[rows_i, :], do_ref[rows_i, :], kcat[rows_j, :]
            s = lax.dot_general(qv, k, NT, preferred_element_type=F32) * ATTN_SCALE
            if masked:
                s = jnp.where(_allowed(i * bq, j * bq, bq), s, -1e30)
            p = jnp.exp(s - lse_ref[0, rows_i, :])
            dv_acc[rows_j, :] += lax.dot_general(p.astype(BF16), dov, TN, preferred_element_type=F32)
            dp = lax.dot_general(dov, v_ref[rows_j, :], NT, preferred_element_type=F32)
            ds = (p * (dp - delta[rows_i, :]) * ATTN_SCALE).astype(BF16)
            dk_acc[rows_j, :] += lax.dot_general(ds, qv, TN, preferred_element_type=F32)
            dq_acc[rows_i, :] += jnp.dot(ds, k, preferred_element_type=F32)

        def kv_step(j, _):
            pair(j, j, True)

            def q_step(i, _):
                pair(i, j, False)
                return 0

            lax.fori_loop(j + 1, nq, q_step, 0)
            return 0

        lax.fori_loop(0, nq, kv_step, 0)

        for r in range(nq):
            rows = slice(r * bq, (r + 1) * bq)
            dq_ref[rows, 0:128] = dq_acc[rows, 0:128].astype(BF16)
            dq_ref[rows, 128:256] = _rope(dq_acc[rows, 128:256], tab_ref[rows, :], -1).astype(BF16)
        dkv_ref[:, 0:128] = dk_acc[:, 0:128].astype(BF16)
        dkv_ref[:, 128:256] = dv_acc[...].astype(BF16)

        @pl.when(h == 0)
        def _():
            dkr_ref[...] = dk_acc[:, 128:256]

        @pl.when(h > 0)
        def _():
            dkr_ref[...] += dk_acc[:, 128:256]

        @pl.when(h == N_HEADS - 1)
        def _():
            for r in range(nq):
                rows = slice(r * bq, (r + 1) * bq)
                dkr_ref[rows, :] = _rope(dkr_ref[rows, :], tab_ref[rows, :], -1)
            if carry:
                carry.finish(c_ins, c_outs, c_sems)

    W = N_HEADS * QK_PAD
    res = pl.pallas_call(
        body, name="attn_bwd", grid=(N_HEADS,),
        in_specs=[pl.BlockSpec((S, QK_PAD), lambda h: (0, h)),
                  pl.BlockSpec((S, 128), lambda h: (0, 2 * h)),
                  pl.BlockSpec((S, 128), lambda h: (0, 2 * h + 1)),
                  pl.BlockSpec((S, 128), lambda h: (0, 0)),
                  pl.BlockSpec((S, V_HEAD), lambda h: (0, h)),
                  pl.BlockSpec((S, V_HEAD), lambda h: (0, h)),
                  pl.BlockSpec((1, S, 1), lambda h: (h, 0, 0)),
                  pl.BlockSpec((S, 384), lambda h: (0, 0))] + [ANY] * n_ci,
        out_specs=[pl.BlockSpec((S, QK_PAD), lambda h: (0, h)),
                   pl.BlockSpec((S, QK_PAD), lambda h: (0, h)),
                   pl.BlockSpec((S, 128), lambda h: (0, 0))] + [ANY] * n_co,
        out_shape=[jax.ShapeDtypeStruct((S, W), BF16), jax.ShapeDtypeStruct((S, W), BF16),
                   jax.ShapeDtypeStruct((S, 128), F32)] + (carry.outs if carry else []),
        scratch_shapes=[pltpu.VMEM((S, QK_PAD), F32), pltpu.VMEM((S, QK_PAD), F32), pltpu.VMEM((S, V_HEAD), F32),
                        pltpu.VMEM((S, QK_PAD), BF16), pltpu.VMEM((S, 1), F32)]
        + (carry.sems if carry else []),
        input_output_aliases=carry.io_aliases(8, 3) if carry else {},
        compiler_params=_params(("arbitrary",)),
    )(q, kv, kv, kr, do, o, lse, tab, *(carry.ins if carry else []))
    return res[0], res[1], res[2], res[3:]


def _shift_down(cur, prev8, i, n):
    tm = cur.shape[0]
    prev8 = jnp.where(i == 0, jnp.zeros_like(prev8), prev8)
    full = jnp.concatenate([prev8, cur], axis=0)
    return pltpu.roll(full, n, 0)[8:8 + tm, :]


def _shift_up(cur, next8, i, last, n):
    tm = cur.shape[0]
    next8 = jnp.where(i == last, jnp.zeros_like(next8), next8)
    full = jnp.concatenate([cur, next8], axis=0)
    return pltpu.roll(full, tm + 8 - n, 0)[0:tm, :]


def _conv_fwd(pc, w_conv):
    S, D = pc.shape[0], pc.shape[1] // 3
    tm = _pick(S, (256, 128))

    def body(i, ins, outs, accs):
        b_ref, c_ref, x_ref, cp_ref, xp_ref, w_ref = ins
        z = c_ref[...] * x_ref[...]
        zp = cp_ref[...] * xp_ref[...]
        cz = w_ref[0:1, :] * _shift_down(z, zp, i, 2) + w_ref[1:2, :] * _shift_down(z, zp, i, 1) + w_ref[2:3, :] * z
        outs[0][...] = (b_ref[...] * cz).astype(BF16)

    return _rows(body, "conv_fwd", S, tm,
                 [("row", pc, 0, D), ("row", pc, 1, D), ("row", pc, 2, D), ("prev", pc, 1, D), ("prev", pc, 2, D),
                  ("full", w_conv)], [(D, BF16)])[0]


def _conv_bwd(dhb, pc, w_conv):
    S, D = dhb.shape
    tm = _pick(S, (256, 128))
    last = S // tm - 1

    def body(i, ins, outs, accs):
        g_ref, b_ref, c_ref, x_ref, cp_ref, xp_ref, gn_ref, bn_ref, w_ref = ins
        w0, w1, w2 = w_ref[0:1, :], w_ref[1:2, :], w_ref[2:3, :]
        c, x, g = c_ref[...], x_ref[...], g_ref[...]
        z = c * x
        zp = cp_ref[...] * xp_ref[...]
        z1, z2 = _shift_down(z, zp, i, 1), _shift_down(z, zp, i, 2)
        cz = w0 * z2 + w1 * z1 + w2 * z
        dcz = g * b_ref[...]
        dczn = gn_ref[...] * bn_ref[...]
        dz = w2 * dcz + w1 * _shift_up(dcz, dczn, i, last, 1) + w0 * _shift_up(dcz, dczn, i, last, 2)
        outs[0][:, 0:D] = (g * cz).astype(BF16)
        outs[0][:, D:2 * D] = (dz * x).astype(BF16)
        outs[0][:, 2 * D:3 * D] = (dz * c).astype(BF16)
        dw = jnp.concatenate([jnp.sum(dcz * z2, axis=0, keepdims=True), jnp.sum(dcz * z1, axis=0, keepdims=True),
                              jnp.sum(dcz * z, axis=0, keepdims=True)], axis=0)
        _acc_add(i, accs[0], dw)

    return _rows(body, "conv_bwd", S, tm,
                 [("row", dhb, 0, D), ("row", pc, 0, D), ("row", pc, 1, D), ("row", pc, 2, D),
                  ("prev", pc, 1, D), ("prev", pc, 2, D), ("next", dhb, 0, D), ("next", pc, 0, D), ("full", w_conv)],
                 [(3 * D, BF16)], [(3, D)])


def _merge_fwd(y_a, y_b, pg):
    S, D = y_a.shape

    def body(i, ins, outs, accs):
        ya, yb, ga, gb = ins
        outs[0][...] = (_sigmoid(ga[...].astype(F32)) * ya[...].astype(F32)
                        + _sigmoid(gb[...].astype(F32)) * yb[...].astype(F32)).astype(BF16)

    return _rows(body, "merge_fwd", S, _pick(S, (256, 128)),
                 [("row", y_a, 0, D), ("row", y_b, 0, D), ("row", pg, 0, D), ("row", pg, 1, D)], [(D, BF16)])[0]


def _merge_bwd(dm, y_a, y_b, pg):
    S, D = dm.shape

    def body(i, ins, outs, accs):
        d, ya, yb = ins[0][...], ins[1][...].astype(F32), ins[2][...].astype(F32)
        sa, sb = _sigmoid(ins[3][...].astype(F32)), _sigmoid(ins[4][...].astype(F32))
        outs[0][...] = (d * sa).astype(BF16)
        outs[1][...] = (d * sb).astype(BF16)
        outs[2][:, 0:D] = (d * ya * (sa * (1.0 - sa))).astype(BF16)
        outs[2][:, D:2 * D] = (d * yb * (sb * (1.0 - sb))).astype(BF16)

    return _rows(body, "merge_bwd", S, _pick(S, (256, 128)),
                 [("row", dm, 0, D), ("row", y_a, 0, D), ("row", y_b, 0, D), ("row", pg, 0, D), ("row", pg, 1, D)],
                 [(D, BF16), (D, BF16), (2 * D, BF16)])


def _ln1_fwd(x, mix, gate1, g, b, scale2, shift2):
    S, D = x.shape

    def body(i, ins, outs, accs):
        x_ref, mix_ref, gate_ref, g_ref, b_ref, sc_ref, sh_ref = ins
        xh, _ = _ln_stats(ALPHA * x_ref[...] + gate_ref[...] * mix_ref[...])
        x1 = xh * g_ref[...] + b_ref[...]
        outs[0][...] = x1
        outs[1][...] = (x1 * (1.0 + sc_ref[...]) + sh_ref[...]).astype(BF16)

    return _rows(body, "ln1_fwd", S, _pick(S, (256, 128)),
                 [("row", x, 0, D), ("row", mix, 0, D), ("full", gate1), ("full", g), ("full", b),
                  ("full", scale2), ("full", shift2)], [(D, F32), (D, BF16)])


def _swiglu_fwd(hh, carry=None):
    S, F = hh.shape[0], hh.shape[1] // 2

    def body(i, ins, outs, accs):
        hg = ins[0][...].astype(F32)
        outs[0][...] = (hg * _sigmoid(hg) * ins[1][...].astype(F32)).astype(BF16)

    res = _rows(body, "swiglu_fwd", S, _pick(S, (128,)), [("row", hh, 0, F), ("row", hh, 1, F)], [(F, BF16)], carry=carry)
    return (res[0][0], res[1]) if carry else res[0]


def _swiglu_bwd(dact, hh):
    S, F = dact.shape

    def body(i, ins, outs, accs):
        d, hg, hu = ins[0][...].astype(F32), ins[1][...].astype(F32), ins[2][...].astype(F32)
        sg = _sigmoid(hg)
        outs[0][:, 0:F] = (d * hu * (sg * (1.0 + hg * (1.0 - sg)))).astype(BF16)
        outs[0][:, F:2 * F] = (d * (hg * sg)).astype(BF16)

    return _rows(body, "swiglu_bwd", S, _pick(S, (128,)),
                 [("row", dact, 0, F), ("row", hh, 0, F), ("row", hh, 1, F)], [(2 * F, BF16)])[0]


def _ln2_loss_bwd(x1, ffn, gate2, g, b, target):
    S, D = x1.shape

    def body(i, ins, outs, accs):
        x1_ref, f_ref, gate_ref, g_ref, b_ref, t_ref = ins
        f = f_ref[...]
        xh, rstd = _ln_stats(ALPHA * x1_ref[...] + gate_ref[...] * f)
        e = xh * g_ref[...] + b_ref[...] - t_ref[...]
        dy = e * (1.0 / D)
        dr = _ln_bwd(dy * g_ref[...], xh, rstd)
        outs[0][...] = (gate_ref[...] * dr).astype(BF16)
        outs[1][...] = ALPHA * dr
        _acc_add(i, accs[0], jnp.full((1, 128), (0.5 / D) * jnp.sum(e * e), F32))
        _acc_add(i, accs[1], jnp.sum(dy * xh, axis=0, keepdims=True))
        _acc_add(i, accs[2], jnp.sum(dy, axis=0, keepdims=True))
        _acc_add(i, accs[3], jnp.sum(dr * f, axis=0, keepdims=True))

    return _rows(body, "ln2_loss_bwd", S, _pick(S, (256, 128)),
                 [("row", x1, 0, D), ("row", ffn, 0, D), ("full", gate2), ("full", g), ("full", b), ("row", target, 0, D)],
                 [(D, BF16), (D, F32)], [(1, 128), (1, D), (1, D), (1, D)])


def _ln1_bwd(x, mix, dx1a, du2, gate1, g, b, scale2):
    S, D = x.shape

    def body(i, ins, outs, accs):
        x_ref, mix_ref, da_ref, du_ref, gate_ref, g_ref, b_ref, sc_ref = ins
        mix, du = mix_ref[...], du_ref[...]
        xh, rstd = _ln_stats(ALPHA * x_ref[...] + gate_ref[...] * mix)
        x1 = xh * g_ref[...] + b_ref[...]
        dx1 = da_ref[...] + du * (1.0 + sc_ref[...])
        dr = _ln_bwd(dx1 * g_ref[...], xh, rstd)
        outs[0][...] = (gate_ref[...] * dr).astype(BF16)
        outs[1][...] = ALPHA * dr
        _acc_add(i, accs[0], jnp.sum(du, axis=0, keepdims=True))
        _acc_add(i, accs[1], jnp.sum(du * x1, axis=0, keepdims=True))
        _acc_add(i, accs[2], jnp.sum(dx1 * xh, axis=0, keepdims=True))
        _acc_add(i, accs[3], jnp.sum(dx1, axis=0, keepdims=True))
        _acc_add(i, accs[4], jnp.sum(dr * mix, axis=0, keepdims=True))

    return _rows(body, "ln1_bwd", S, _pick(S, (256, 128)),
                 [("row", x, 0, D), ("row", mix, 0, D), ("row", dx1a, 0, D), ("row", du2, 0, D),
                  ("full", gate1), ("full", g), ("full", b), ("full", scale2)],
                 [(D, BF16), (D, F32)], [(1, D)] * 5)


def _rms_bwd(d_rq, d_rkv, pq, dkr, g_q, g_kv):
    S = pq.shape[0]

    def body(i, ins, outs, accs):
        dq_ref, dkv_ref, pq_ref, dkr_ref, gq_ref, gkv_ref = ins

        def rms_bwd(dy, x, g):
            r = lax.rsqrt(jnp.mean(x * x, axis=-1, keepdims=True) + RMS_EPS)
            dyg = dy * g
            dx = r * dyg - x * (r * r * r) * jnp.mean(dyg * x, axis=-1, keepdims=True)
            return dx, jnp.sum(dy * (x * r), axis=0, keepdims=True)

        dxq, dgq = rms_bwd(dq_ref[...], pq_ref[:, 0:Q_LORA], gq_ref[...])
        dxkv, dgkv = rms_bwd(dkv_ref[...], pq_ref[:, Q_LORA:Q_LORA + KV_LORA], gkv_ref[...])
        outs[0][:, 0:Q_LORA] = dxq.astype(BF16)
        outs[0][:, Q_LORA:Q_LORA + KV_LORA] = dxkv.astype(BF16)
        outs[0][:, Q_LORA + KV_LORA:QKV_A] = dkr_ref[...].astype(BF16)
        _acc_add(i, accs[0], dgq)
        _acc_add(i, accs[1], dgkv)

    return _rows(body, "rms_bwd", S, _pick(S, (256, 128)),
                 [("row", d_rq, 0, Q_LORA), ("row", d_rkv, 0, KV_LORA), ("row", pq, 0, QKV_A), ("row", dkr, 0, 128),
                  ("full", g_q), ("full", g_kv)], [(QKV_A, BF16)], [(1, Q_LORA), (1, KV_LORA)])


def _dx_final(dxa, du, x, scale1):
    S, D = x.shape

    def body(i, ins, outs, accs):
        du = ins[1][...]
        outs[0][...] = ins[0][...] + du * (1.0 + ins[3][...])
        _acc_add(i, accs[0], jnp.sum(du, axis=0, keepdims=True))
        _acc_add(i, accs[1], jnp.sum(du * ins[2][...], axis=0, keepdims=True))

    return _rows(body, "dx_final", S, _pick(S, (256, 128)),
                 [("row", dxa, 0, D), ("row", du, 0, D), ("row", x, 0, D), ("full", scale1)],
                 [(D, F32)], [(1, D), (1, D)])


def _ada_fwd(c_all, w, bias):
    B, D = c_all.shape
    NA = w.shape[1]
    tn = _pick(NA, (512, 256, 128))

    def body(c_ref, w_ref, b_ref, o_ref):
        cv = c_ref[...]
        ca = (cv * _sigmoid(cv)).astype(BF16)
        o_ref[...] = jnp.dot(ca, w_ref[...].astype(BF16), preferred_element_type=F32) + b_ref[...]

    return pl.pallas_call(
        body, name="ada_fwd", grid=(NA // tn,),
        in_specs=[pl.BlockSpec((B, D), lambda j: (0, 0)), pl.BlockSpec((D, tn), lambda j: (0, j)),
                  pl.BlockSpec((1, tn), lambda j: (0, j))],
        out_specs=pl.BlockSpec((B, tn), lambda j: (0, j)),
        out_shape=jax.ShapeDtypeStruct((B, NA), F32),
        compiler_params=_params(("arbitrary",)),
    )(c_all, w, bias)


def _ada_bwd(c_all, dmod):
    B, D = c_all.shape
    NA = dmod.shape[1]
    tn = _pick(NA, (512, 256, 128))

    def body(c_ref, d_ref, o_ref):
        cv = c_ref[...]
        ca = (cv * _sigmoid(cv)).astype(BF16)
        o_ref[...] = lax.dot_general(ca, d_ref[...].astype(BF16), TN, preferred_element_type=F32)

    return pl.pallas_call(
        body, name="ada_bwd", grid=(NA // tn,),
        in_specs=[pl.BlockSpec((B, D), lambda j: (0, 0)), pl.BlockSpec((B, tn), lambda j: (0, j))],
        out_specs=pl.BlockSpec((D, tn), lambda j: (0, j)),
        out_shape=jax.ShapeDtypeStruct((D, NA), F32),
        compiler_params=_params(("arbitrary",)),
    )(c_all, dmod)


def _pack_rows(parts, n_rows, after=()):
    N = parts[0].shape[1]
    n = len(parts)

    def body(*refs):
        o_ref = refs[-1]
        o_ref[...] = jnp.zeros_like(o_ref)
        at = 0
        for r in refs[:n]:
            o_ref[at:at + r.shape[0], :] = r[...]
            at += r.shape[0]

    vmem = pl.BlockSpec(memory_space=pltpu.VMEM)
    return pl.pallas_call(body, name="pack_small", out_shape=jax.ShapeDtypeStruct((n_rows, N), F32),
                          in_specs=[vmem] * n + [ANY] * len(after), out_specs=vmem,
                          compiler_params=_params())(*parts, *after)


def _sum8(parts):
    _, R, N = parts.shape

    def body(p_ref, o_ref):
        acc = p_ref[0]
        for d in range(1, 8):
            acc = acc + p_ref[d]
        o_ref[...] = acc

    return pl.pallas_call(body, name="sum8", out_shape=jax.ShapeDtypeStruct((R, N), F32),
                          compiler_params=_params())(parts)


def _adam_math(w, g, m, v):
    m = ADAM_B1 * m + (1.0 - ADAM_B1) * g
    v = ADAM_B2 * v + (1.0 - ADAM_B2) * (g * g)
    delta = -ADAM_LR * ((m / ADAM_C1) / (jnp.sqrt(v / ADAM_C2) + ADAM_EPS) + ADAM_WD * w)
    return delta, m, v


def _adam(name, w, m, v, g, carry=None):
    R, C = w.shape
    tm = _row_tile(R, C * 4, 1 << 20)
    steps = R // tm
    n_ci = len(carry.ins) if carry else 0
    n_co = len(carry.outs) if carry else 0

    def body(*refs):
        w_ref, m_ref, v_ref, g_ref = refs[:4]
        d_ref, nm_ref, nv_ref = refs[4 + n_ci:7 + n_ci]
        c_ins, c_outs, c_sems = refs[4:4 + n_ci], refs[7 + n_ci:7 + n_ci + n_co], refs[7 + n_ci + n_co:]
        if carry:
            @pl.when(pl.program_id(0) == 0)
            def _():
                carry.start(c_ins, c_outs, c_sems)

        delta, nm, nv = _adam_math(w_ref[...], g_ref[...], m_ref[...], v_ref[...])
        d_ref[...] = delta
        nm_ref[...] = nm
        nv_ref[...] = nv
        if carry:
            @pl.when(pl.program_id(0) == steps - 1)
            def _():
                carry.finish(c_ins, c_outs, c_sems)

    spec = pl.BlockSpec((tm, C), lambda i: (i, 0))
    res = pl.pallas_call(
        body, name=name, grid=(steps,), in_specs=[spec] * 4 + [ANY] * n_ci, out_specs=[spec] * 3 + [ANY] * n_co,
        out_shape=[jax.ShapeDtypeStruct((R, C), F32)] * 3 + (carry.outs if carry else []),
        scratch_shapes=carry.sems if carry else [],
        input_output_aliases=carry.io_aliases(4, 3) if carry else {},
        compiler_params=_params(("arbitrary",)),
    )(w, m, v, g, *(carry.ins if carry else []))
    return (res[:3], res[3:]) if carry else res


def _adam_halves(name, w, m, v, mine, other, core, carry=None):
    R, C = w.shape
    Rh = mine.shape[0]
    tc = max(t for t in range(128, C + 1, 128) if C % t == 0 and R * t <= (3 << 17))
    steps = C // tc
    n_ci = len(carry.ins) if carry else 0
    n_co = len(carry.outs) if carry else 0

    def body(*refs):
        c_ref, w_ref, m_ref, v_ref, a_ref, b_ref = refs[:6]
        g_ref, d_ref, nm_ref, nv_ref = refs[6 + n_ci:10 + n_ci]
        c_ins, c_outs, c_sems = refs[6:6 + n_ci], refs[10 + n_ci:10 + n_ci + n_co], refs[10 + n_ci + n_co:]
        if carry:
            @pl.when(pl.program_id(0) == 0)
            def _():
                carry.start(c_ins, c_outs, c_sems)

        first = c_ref[0] == 0
        g = jnp.concatenate([jnp.where(first, a_ref[...], b_ref[...]),
                             jnp.where(first, b_ref[0:R - Rh, :], a_ref[0:R - Rh, :])], axis=0)
        delta, nm, nv = _adam_math(w_ref[...], g, m_ref[...], v_ref[...])
        g_ref[...] = g
        d_ref[...] = delta
        nm_ref[...] = nm
        nv_ref[...] = nv
        if carry:
            @pl.when(pl.program_id(0) == steps - 1)
            def _():
                carry.finish(c_ins, c_outs, c_sems)

    spec = pl.BlockSpec((R, tc), lambda i, c_ref: (0, i))
    h_spec = pl.BlockSpec((Rh, tc), lambda i, c_ref: (0, i))
    res = pl.pallas_call(
        body, name=name, out_shape=[jax.ShapeDtypeStruct((R, C), F32)] * 4 + (carry.outs if carry else []),
        grid_spec=pltpu.PrefetchScalarGridSpec(
            num_scalar_prefetch=1, grid=(steps,), in_specs=[spec, spec, spec, h_spec, h_spec] + [ANY] * n_ci,
            out_specs=[spec] * 4 + [ANY] * n_co, scratch_shapes=carry.sems if carry else []),
        input_output_aliases=carry.io_aliases(6, 4) if carry else {},
        compiler_params=_params(("arbitrary",)),
    )(core, w, m, v, mine, other, *(carry.ins if carry else []))
    return (res[:4], res[4:]) if carry else res


def _adam_small(name, w, m, v, g):
    def body(w_ref, m_ref, v_ref, g_ref, d_ref, nm_ref, nv_ref):
        delta, nm, nv = _adam_math(w_ref[...], g_ref[...], m_ref[...], v_ref[...])
        d_ref[...] = delta
        nm_ref[...] = nm
        nv_ref[...] = nv

    return pl.pallas_call(body, name=name, out_shape=[jax.ShapeDtypeStruct(w.shape, F32)] * 3,
                          compiler_params=_params())(w, m, v, g)


def _place():
    return lax.axis_index("x"), lax.axis_index("y"), lax.axis_index("c")


def _other_chips(x, y):
    return [(1 - x, y), (x, 1 - y), (1 - x, 1 - y)]


def _all_gather8(blk, name):
    R, N = blk.shape

    def body(x_ref, out_ref, send_sems, recv_sems, local_sem):
        x, y, c = _place()
        me = 4 * x + 2 * y + c
        mine = pltpu.make_async_copy(x_ref, out_ref.at[me], local_sem)
        mine.start()
        flips = [(j >> 2 & 1, j >> 1 & 1, j & 1) for j in range(1, 8)]
        peers = [((1 - x) if fx else x, (1 - y) if fy else y, (1 - c) if fc else c) for fx, fy, fc in flips]
        sends = []
        for j, peer in enumerate(peers):
            cp = pltpu.make_async_remote_copy(src_ref=x_ref, dst_ref=out_ref.at[me], send_sem=send_sems.at[j],
                                              recv_sem=recv_sems.at[j], device_id=peer, device_id_type=MESH)
            cp.start()
            sends.append(cp)
        for j, (px, py, pc) in enumerate(peers):
            pltpu.make_async_remote_copy(src_ref=x_ref, dst_ref=out_ref.at[4 * px + 2 * py + pc],
                                         send_sem=send_sems.at[j], recv_sem=recv_sems.at[j],
                                         device_id=(px, py, pc), device_id_type=MESH).wait_recv()
        for cp in sends:
            cp.wait_send()
        mine.wait()

    return pl.pallas_call(
        body, name=name, out_shape=jax.ShapeDtypeStruct((8, R, N), F32),
        in_specs=[pl.BlockSpec(memory_space=pltpu.VMEM)], out_specs=pl.BlockSpec(memory_space=pltpu.VMEM),
        scratch_shapes=[pltpu.SemaphoreType.DMA((7,)), pltpu.SemaphoreType.DMA((7,)), pltpu.SemaphoreType.DMA],
        compiler_params=_params(),
    )(blk)


def _piece(rows, piece):
    i, n, k = piece if len(piece) == 3 else (piece[0], piece[1], 1)
    assert rows % 16 == 0 and rows // 16 >= n, (rows, piece)
    lo, hi = (rows // 16 * i // n) * 16, (rows // 16 * (i + k) // n) * 16
    return pl.ds(lo, hi - lo)


def _scatter_plan(arrs, piece=(0, 1), into=None):
    n = len(arrs)

    def copies(ins, outs, sems):
        send_sems, recv_sems = sems
        x, y, c = _place()
        chips = _other_chips(x, y)
        cps = []
        for k in range(n):
            rows = _piece(arrs[k].shape[1], piece)
            for j, (px, py) in enumerate(chips):
                cps.append(pltpu.make_async_remote_copy(
                    src_ref=ins[k].at[2 * px + py, rows], dst_ref=outs[k].at[j, rows],
                    send_sem=send_sems.at[3 * k + j], recv_sem=recv_sems.at[3 * k + j],
                    device_id=(px, py, c), device_id_type=MESH))
        return cps

    def start(ins, outs, sems):
        for cp in copies(ins, outs, sems):
            cp.start()

    def finish(ins, outs, sems):
        for cp in copies(ins, outs, sems):
            cp.wait()

    return _Plan(list(arrs) + list(into or []), [jax.ShapeDtypeStruct((3,) + a.shape[1:], a.dtype) for a in arrs],
                 [pltpu.SemaphoreType.DMA((3 * n,))] * 2, start, finish,
                 aliases={n + k: k for k in range(n)} if into else None)


def _gather_plan(shards, piece=(0, 1), into=None, ici=True):
    n = len(shards)

    def parts(ins, outs, sems):
        s1, r1, s2, r2, loc = sems
        x, y, c = _place()
        me = 2 * x + y
        chips = _other_chips(x, y)
        sib = (x, y, 1 - c)

        def rows(k):
            return _piece(shards[k].shape[1], piece)

        def ici_copy(k, j, slab, to):
            return pltpu.make_async_remote_copy(src_ref=ins[k].at[c, rows(k)], dst_ref=outs[k].at[slab, c, rows(k)],
                                                send_sem=s1.at[3 * k + j], recv_sem=r1.at[3 * k + j],
                                                device_id=to, device_id_type=MESH)

        def d2d(k, j, slab, half):
            return pltpu.make_async_remote_copy(src_ref=outs[k].at[slab, half, rows(k)],
                                                dst_ref=outs[k].at[slab, half, rows(k)],
                                                send_sem=s2.at[3 * k + j], recv_sem=r2.at[3 * k + j],
                                                device_id=sib, device_id_type=MESH)

        def own(k):
            return pltpu.make_async_remote_copy(src_ref=ins[k].at[:, rows(k)], dst_ref=outs[k].at[me, :, rows(k)],
                                                send_sem=loc.at[2 * k], recv_sem=loc.at[2 * k + 1],
                                                device_id=sib, device_id_type=MESH)

        return c, me, chips, ici_copy, d2d, own

    def start(ins, outs, sems):
        c, me, chips, ici_copy, d2d, own = parts(ins, outs, sems)
        for k in range(n):
            for j, (px, py) in enumerate(chips):
                (ici_copy(k, j, me, (px, py, c)) if ici else d2d(k, j, 2 * px + py, c)).start()
        for k in range(n):
            own(k).start()

    def finish(ins, outs, sems):
        c, me, chips, ici_copy, d2d, own = parts(ins, outs, sems)
        if ici:
            for k in range(n):
                for j, (px, py) in enumerate(chips):
                    ici_copy(k, j, 2 * px + py, (px, py, c)).wait_recv()
                    d2d(k, j, 2 * px + py, c).start()
        for k in range(n):
            for j, (px, py) in enumerate(chips):
                d2d(k, j, 2 * px + py, 1 - c).wait_recv()
        for k in range(n):
            own(k).wait()
            for j, (px, py) in enumerate(chips):
                if ici:
                    ici_copy(k, j, me, (px, py, c)).wait_send()
                d2d(k, j, 2 * px + py, c).wait_send()

    return _Plan(list(shards) + list(into or []), [jax.ShapeDtypeStruct((4,) + a.shape, a.dtype) for a in shards],
                 [pltpu.SemaphoreType.DMA((3 * n,))] * 4 + [pltpu.SemaphoreType.DMA((2 * n,))], start, finish,
                 aliases={n + k: k for k in range(n)} if into else None)


def _pair_plan(parts):
    n = len(parts)

    def copies(ins, outs, sems):
        send_sems, recv_sems = sems
        x, y, c = _place()
        return [pltpu.make_async_remote_copy(src_ref=ins[k].at[p, 1 - c], dst_ref=outs[k].at[p],
                                             send_sem=send_sems.at[4 * k + p], recv_sem=recv_sems.at[4 * k + p],
                                             device_id=(x, y, 1 - c), device_id_type=MESH)
                for k in range(n) for p in range(4)]

    def start(ins, outs, sems):
        for cp in copies(ins, outs, sems):
            cp.start()

    def finish(ins, outs, sems):
        for cp in copies(ins, outs, sems):
            cp.wait()

    return _Plan(parts, [jax.ShapeDtypeStruct((4,) + a.shape[2:], a.dtype) for a in parts],
                 [pltpu.SemaphoreType.DMA((4 * n,))] * 2, start, finish)


def _sibling_plan(arrs):
    n = len(arrs)

    def copies(ins, outs, sems):
        send_sems, recv_sems = sems
        x, y, c = _place()
        return [pltpu.make_async_remote_copy(src_ref=ins[k], dst_ref=outs[k], send_sem=send_sems.at[k],
                                             recv_sem=recv_sems.at[k], device_id=(x, y, 1 - c), device_id_type=MESH)
                for k in range(n)]

    def start(ins, outs, sems):
        for cp in copies(ins, outs, sems):
            cp.start()

    def finish(ins, outs, sems):
        for cp in copies(ins, outs, sems):
            cp.wait()

    return _Plan(arrs, [jax.ShapeDtypeStruct(a.shape, a.dtype) for a in arrs],
                 [pltpu.SemaphoreType.DMA((n,))] * 2, start, finish)


def _scatter_copies(arrs):
    def copies(ins, land, send_sems, recv_sems):
        x, y, c = _place()
        return [pltpu.make_async_remote_copy(src_ref=ins[k].at[2 * px + py], dst_ref=land[k].at[j],
                                             send_sem=send_sems.at[3 * k + j], recv_sem=recv_sems.at[3 * k + j],
                                             device_id=(px, py, c), device_id_type=MESH)
                for k in range(len(arrs)) for j, (px, py) in enumerate(_other_chips(x, y))]

    return copies, [lax.empty((3,) + a.shape[1:], a.dtype) for a in arrs]


def _gather_copies(shards):
    def copies(ins, land, send_sems, recv_sems):
        x, y, c = _place()
        return [pltpu.make_async_remote_copy(src_ref=ins[k].at[c], dst_ref=land[k].at[2 * x + y, c],
                                             send_sem=send_sems.at[3 * k + j], recv_sem=recv_sems.at[3 * k + j],
                                             device_id=(px, py, c), device_id_type=MESH)
                for k in range(len(shards)) for j, (px, py) in enumerate(_other_chips(x, y))]

    return copies, [lax.empty((4,) + a.shape, a.dtype) for a in shards]


def _split_start(arrs, copies_lands, ride, name, after=()):
    copies, lands = copies_lands
    n = len(arrs)
    rides = list(ride) if isinstance(ride, (list, tuple)) else [ride]
    n_thru = 2 * n + len(rides)

    def body(*refs):
        first_out = n_thru + len(after)
        for cp in copies(refs[:n], refs[n:2 * n], refs[first_out], refs[first_out + 1]):
            cp.start()

    hbm = [pltpu.with_memory_space_constraint(a, pltpu.HBM) for a in list(arrs) + lands + rides]
    res = pl.pallas_call(
        body, name=name,
        out_shape=[pltpu.SemaphoreType.DMA((3 * n,)), pltpu.SemaphoreType.DMA((3 * n,))]
        + [pltpu.HBM(a.shape, a.dtype) for a in hbm],
        in_specs=[HBM_SPEC] * n_thru + [ANY] * len(after),
        out_specs=[SEM_SPEC, SEM_SPEC] + [HBM_SPEC] * n_thru,
        input_output_aliases={i: 2 + i for i in range(n_thru)},
        compiler_params=pltpu.CompilerParams(has_side_effects=pltpu.SideEffectType.DATAFLOW_SIDE_EFFECTING),
    )(*hbm, *after)
    return res[0], res[1], res[2:2 + n], res[2 + n:2 + 2 * n], list(res[2 + 2 * n:])


def _split_wait(started, copies_lands, after, name):
    send_sems, recv_sems, arrs, lands, _ = started
    copies = copies_lands[0]
    n = len(arrs)

    def body(*refs):
        for cp in copies(refs[:n], refs[n:2 * n], refs[2 * n], refs[2 * n + 1]):
            cp.wait_send()
            cp.wait_recv()

    res = pl.pallas_call(
        body, name=name, out_shape=[pltpu.HBM(a.shape, a.dtype) for a in list(arrs) + list(lands)],
        in_specs=[HBM_SPEC] * (2 * n) + [SEM_SPEC, SEM_SPEC] + [ANY] * len(after), out_specs=[HBM_SPEC] * (2 * n),
        input_output_aliases={i: i for i in range(2 * n)},
        compiler_params=pltpu.CompilerParams(has_side_effects=pltpu.SideEffectType.DATAFLOW_SIDE_EFFECTING),
    )(*arrs, *lands, send_sems, recv_sems, *after)
    return list(res[:n]), list(res[n:])


def _join_plans(plans):
    def split(seq, counts):
        out, at = [], 0
        for cnt in counts:
            out.append(seq[at:at + cnt])
            at += cnt
        return out

    n_i, n_o, n_s = ([len(getattr(p, f)) for p in plans] for f in ("ins", "outs", "sems"))

    def start(ins, outs, sems):
        for p, i, o, s in zip(plans, split(ins, n_i), split(outs, n_o), split(sems, n_s)):
            p.start(i, o, s)

    def finish(ins, outs, sems):
        for p, i, o, s in zip(plans, split(ins, n_i), split(outs, n_o), split(sems, n_s)):
            p.finish(i, o, s)

    aliases, at_i, at_o = {}, 0, 0
    for p in plans:
        aliases.update(p.io_aliases(at_i, at_o))
        at_i, at_o = at_i + len(p.ins), at_o + len(p.outs)
    return _Plan(sum((p.ins for p in plans), []), sum((p.outs for p in plans), []), sum((p.sems for p in plans), []),
                 start, finish, aliases)


def _add_pair(parts, sib, core, name):
    P4, _, Rh, C = parts.shape
    tm, tc = _tile2(Rh, C, 16)

    def body(c_ref, a_ref, b_ref, o_ref):
        o_ref[...] = (a_ref[0].astype(F32) + b_ref[...].astype(F32)).astype(BF16)

    spec = pl.BlockSpec((1, tm, tc), lambda p, i, j, c_ref: (p, i, j))
    return pl.pallas_call(
        body, name=name, out_shape=jax.ShapeDtypeStruct((P4, Rh, C), BF16),
        grid_spec=pltpu.PrefetchScalarGridSpec(
            num_scalar_prefetch=1, grid=(P4, Rh // tm, C // tc),
            in_specs=[pl.BlockSpec((1, 1, tm, tc), lambda p, i, j, c_ref: (p, c_ref[0], i, j)), spec], out_specs=spec),
        compiler_params=_params(("parallel",) * 3),
    )(core, parts, sib)


def _sum_slabs(pre, recv, chip, name):
    _, Rh, C = pre.shape
    tm, tc = _tile2(Rh, C, 16)

    def body(me_ref, own_ref, r_ref, o_ref):
        acc = own_ref[0].astype(F32)
        for j in range(3):
            acc = acc + r_ref[j].astype(F32)
        o_ref[...] = acc

    return pl.pallas_call(
        body, name=name, out_shape=jax.ShapeDtypeStruct((Rh, C), F32),
        grid_spec=pltpu.PrefetchScalarGridSpec(
            num_scalar_prefetch=1, grid=(Rh // tm, C // tc),
            in_specs=[pl.BlockSpec((1, tm, tc), lambda i, j, me_ref: (me_ref[0], i, j)),
                      pl.BlockSpec((3, tm, tc), lambda i, j, me_ref: (0, i, j))],
            out_specs=pl.BlockSpec((tm, tc), lambda i, j, me_ref: (i, j))),
        compiler_params=_params(("parallel", "parallel")),
    )(chip, pre, recv)


def kernel(x, c, positions, w_ada, b_ada, w_in, g_q_a, w_q_b, g_kv_a, w_kv_b, w_o_a, w_conv, w_o_b, w_o, ln1_g, ln1_b, w_ffn_in, w_ffn_out, ln2_g, ln2_b, loss_target, m_w_ada, m_b_ada, m_w_in, m_g_q_a, m_w_q_b, m_g_kv_a, m_w_kv_b, m_w_o_a, m_w_conv, m_w_o_b, m_w_o, m_ln1_g, m_ln1_b, m_w_ffn_in, m_w_ffn_out, m_ln2_g, m_ln2_b, v_w_ada, v_b_ada, v_w_in, v_g_q_a, v_w_q_b, v_g_kv_a, v_w_kv_b, v_w_o_a, v_w_conv, v_w_o_b, v_w_o, v_ln1_g, v_ln1_b, v_w_ffn_in, v_w_ffn_out, v_ln2_g, v_ln2_b):
    S, D = x.shape[1], x.shape[2]
    F = w_ffn_out.shape[1] * 4
    ax, ay, ac = _place()
    chip = 2 * ax + ay
    dev = 4 * ax + 2 * ay + ac
    x2, tgt = x[0], loss_target[0]
    w_ada2, w_in2, w_q_b2, w_kv_b2 = w_ada[0], w_in[0], w_q_b[0], w_kv_b[0]
    w_o_a2, w_o_b2, w_o2, w_ffn_in2, w_ffn_out2 = w_o_a[0], w_o_b[0], w_o[0], w_ffn_in[0], w_ffn_out[0]
    NA = w_ada2.shape[1]
    CW = w_conv.shape[2]

    inv_freq = 1.0 / (ROPE_THETA ** (jnp.arange(0, QK_ROPE, 2, dtype=F32) / QK_ROPE))
    ang = positions[0].astype(F32)[:, None] * inv_freq
    cos, sin = jnp.cos(ang), jnp.sin(ang)
    z32, z64, z96 = jnp.zeros((S, 32), F32), jnp.zeros((S, 64), F32), jnp.zeros((S, 96), F32)
    tab = jnp.concatenate([cos, cos, z64, -sin, z96, z32, sin, z64], axis=1)

    def halves(a):
        return a.reshape(2, a.shape[0] // 2, a.shape[1])

    def whole(g):
        return g.reshape(4, 2 * g.shape[2], g.shape[3])

    def cols(g):
        return jnp.transpose(g, (1, 0, 2)).reshape(g.shape[1], 4 * g.shape[2])

    w_inT, m_w_inT, v_w_inT = w_in2.T, m_w_in[0].T, v_w_in[0].T
    CS = w_inT.shape[0]
    CSP = -(-CS // 32) * 32
    sh_in = halves(jnp.pad(w_inT.astype(BF16), ((0, CSP - CS), (0, 0))))
    sh_qb, sh_kvb, sh_oa, sh_ob, sh_o, sh_fi, sh_fo = (
        halves(w.astype(BF16)) for w in (w_q_b2, w_kv_b2, w_o_a2, w_o_b2, w_o2, w_ffn_in2, w_ffn_out2))
    c_all = _all_gather8(c, "gather_c").reshape(8, D)
    wconv_all = _all_gather8(w_conv[0], "gather_wconv")
    w_conv_full = jnp.transpose(wconv_all[0::2], (1, 0, 2)).reshape(3, D)
    b_sh = lax.dynamic_slice(b_ada, (0, chip * NA), (1, NA))
    mod_sh = _ada_fwd(c_all, w_ada2, b_sh)
    mod_all = _all_gather8(mod_sh, "gather_mod")
    mod = lax.dynamic_slice(mod_all[0::2], (0, dev, 0), (4, 1, NA)).reshape(6, D)
    shift1, scale1, gate1, shift2, scale2, gate2 = (mod[k:k + 1] for k in range(6))

    g_in, shift1, w_conv_full = _run_plan(_gather_plan([sh_in]), "gather_first", ride=[shift1, w_conv_full])
    g_in = whole(g_in)
    sh_a1, sh_a2 = [sh_qb, sh_kvb], [sh_oa, sh_ob, sh_o]
    cl_a1, cl_a2, cl_fi, cl_fo = (_gather_copies(g) for g in (sh_a1, sh_a2, [sh_fi], [sh_fo]))
    st_a1 = _split_start(sh_a1, cl_a1, shift1, "gather_a1_start")
    st_a2 = _split_start(sh_a2, cl_a2, st_a1[4], "gather_a2_start")
    shift1 = st_a2[4][0]

    def in_rows(lo, hi):
        parts = [g_in[p, max(lo, p * CS) - p * CS:min(hi, (p + 1) * CS) - p * CS]
                 for p in range(4) if max(lo, p * CS) < min(hi, (p + 1) * CS)]
        return parts[0] if len(parts) == 1 else jnp.concatenate(parts, axis=0)

    n_qkv = Q_LORA + KV_LORA + QK_ROPE
    W_qkvT = jnp.pad(in_rows(0, n_qkv), ((0, QKV_A - n_qkv), (0, 0)))
    W_convT = in_rows(n_qkv, n_qkv + 3 * D)
    W_gateT = in_rows(n_qkv + 3 * D, n_qkv + 5 * D)

    u = _modulate(x2, scale1, shift1, "modulate1")
    pq = _matmul(u, W_qkvT, "nt", F32, "proj_qkv")
    pc = _matmul(u, W_convT, "nt", F32, "proj_conv")
    sh_a1, la1 = _split_wait(st_a1, cl_a1, [pc], "gather_a1_wait")
    pg, (g_qb, g_kvb) = _matmul(u, W_gateT, "nt", BF16, "proj_gate", carry=_gather_plan(sh_a1, into=la1, ici=False))
    st_fi = _split_start([sh_fi], cl_fi, g_q_a, "gather_fi_start", after=[pg])
    W_qb = jnp.pad(cols(whole(g_qb)).reshape(Q_LORA, N_HEADS, QK_NOPE + QK_ROPE),
                   ((0, 0), (0, 0), (0, QK_PAD - QK_NOPE - QK_ROPE))).reshape(Q_LORA, N_HEADS * QK_PAD)
    W_kvb = cols(whole(g_kvb))
    rq, rkv, kr = _rms_fwd(pq, tab, st_fi[4][0], g_kv_a)
    kv = _matmul(rkv, W_kvb, "nn", BF16, "kv_b")
    sh_a2, la2 = _split_wait(st_a2, cl_a2, [kv], "gather_a2_wait")
    qf, (g_oa, g_ob, g_o) = _matmul(rq, W_qb, "nn", F32, "q_b", carry=_gather_plan(sh_a2, into=la2, ici=False))
    q = _q_rope(qf, tab)
    o, lse, _ = _attn_fwd(q, kv, kr)
    W_oa, W_ob, W_o = (g.reshape(-1, D) for g in (g_oa, g_ob, g_o))
    hb = _conv_fwd(pc, w_conv_full)
    sh_fi_t, lfi = _split_wait(st_fi, cl_fi, [o], "gather_fi_wait")
    y_b, g_fi = _matmul(hb, W_ob, "nn", BF16, "o_b", carry=_gather_plan(sh_fi_t, (0, 2), into=lfi, ici=False))
    y_a, (g_fi,) = _matmul(o, W_oa, "nn", BF16, "o_a", carry=_gather_plan(sh_fi_t, (1, 2), into=g_fi, ici=False))
    st_fo = _split_start([sh_fo], cl_fo, ln1_g, "gather_fo_start", after=[y_b])
    merged = _merge_fwd(y_a, y_b, pg)
    mix = _matmul(merged, W_o, "nn", F32, "w_o")
    W_fi = whole(g_fi)
    x1, u2 = _ln1_fwd(x2, mix, gate1, st_fo[4][0], ln1_b, scale2, shift2)
    hh = _matmul(u2, W_fi, "nn", BF16, "ffn_in", shards="b")
    sh_fo_t, lfo = _split_wait(st_fo, cl_fo, [hh], "gather_fo_wait")
    act, (g_fo,) = _swiglu_fwd(hh, carry=_gather_plan(sh_fo_t, into=lfo, ici=False))
    W_fo = g_fo.reshape(F, D)
    ffn = _matmul(act, W_fo, "nn", F32, "ffn_out")

    core_i = ac.astype(jnp.int32).reshape(1)
    chip_i = chip.astype(jnp.int32).reshape(1)

    def uncols(g):
        return jnp.transpose(g.reshape(g.shape[0], 4, g.shape[1] // 4), (1, 0, 2))

    def slabs(p):
        return p.reshape(4, 2, p.shape[1] // 2, p.shape[2])

    def add_pairs(parts, sibs, nms):
        return [_add_pair(a, b, core_i, "add_pair_" + nm) for a, b, nm in zip(parts, sibs, nms)]

    def sum_all(pre, recv, nms):
        return [_sum_slabs(a, r, chip_i, "sum_slabs_" + nm) for a, r, nm in zip(pre, recv, nms)]

    dffn, dx1a, loss_acc, d_ln2_g, d_ln2_b, d_gate2 = _ln2_loss_bwd(x1, ffn, gate2, ln2_g, ln2_b, tgt)
    loss = lax.psum(loss_acc[0, 0], ("x", "y", "c"))
    dW_fo = _matmul(act, dffn, "tn", BF16, "d_w_ffn_out")
    p_fo = [slabs(dW_fo.reshape(4, -1, D))]
    dact, s_fo = _matmul(dffn, W_fo, "nt", BF16, "d_act", carry=_pair_plan(p_fo))
    pre_fo = add_pairs(p_fo, s_fo, ["w_ffn_out"])
    cs_fo = _scatter_copies(pre_fo)
    st_sfo = _split_start(pre_fo, cs_fo, scale2, "scatter_fo_start")
    dhh = _swiglu_bwd(dact, hh)
    dW_fi = _matmul(u2, dhh, "tn", BF16, "d_w_ffn_in", shards="o")
    p_fi = [slabs(dW_fi)]
    du2, s_fi = _matmul(dhh, W_fi, "nt", F32, "d_u2", carry=_pair_plan(p_fi), shards="b")
    pre_fi = add_pairs(p_fi, s_fi, ["w_ffn_in"])
    cs_fi = _scatter_copies(pre_fi)
    st_sfi = _split_start(pre_fi, cs_fi, st_sfo[4], "scatter_fi_start")
    dmix, dxa, d_shift2, d_scale2, d_ln1_g, d_ln1_b, d_gate1 = _ln1_bwd(x2, mix, dx1a, du2, gate1, ln1_g, ln1_b, st_sfi[4][0])
    dW_o = _matmul(merged, dmix, "tn", BF16, "d_w_o")
    dmerged = _matmul(dmix, W_o, "nt", F32, "d_merged")
    dy_a, dy_b, dgate = _merge_bwd(dmerged, y_a, y_b, pg)
    dW_oa = _matmul(o, dy_a, "tn", BF16, "d_w_o_a")
    do = _matmul(dy_a, W_oa, "nt", BF16, "d_o")
    dW_ob = _matmul(hb, dy_b, "tn", BF16, "d_w_o_b")
    p_mid = [slabs(g.reshape(4, -1, D)) for g in (dW_oa, dW_ob, dW_o)]
    dhb, s_mid = _matmul(dy_b, W_ob, "nt", F32, "d_hb", carry=_pair_plan(p_mid))
    pre_mid = add_pairs(p_mid, s_mid, ["w_o_a", "w_o_b", "w_o"])
    cs_mid = _scatter_copies(pre_mid)
    st_smid = _split_start(pre_mid, cs_mid, w_conv_full, "scatter_mid_start")
    dconv, d_wconv = _conv_bwd(dhb, pc, st_smid[4][0])
    dq, dkv, dkr, _ = _attn_bwd(q, kv, kr, do, o, lse, tab, carry=_token_plan(st_smid[4][0]))
    names_a = ["w_ffn_out", "w_ffn_in", "w_o_a", "w_o_b", "w_o"]
    dW_qb = _matmul(rq, dq, "tn", BF16, "d_w_q_b")
    d_rq = _matmul(dq, W_qb, "nt", F32, "d_rq")
    dW_kvb = _matmul(rkv, dkv, "tn", BF16, "d_w_kv_b")
    d_rkv = _matmul(dkv, W_kvb, "nt", F32, "d_rkv")
    dqkv, d_g_q, d_g_kv = _rms_bwd(d_rq, d_rkv, pq, dkr, g_q_a, g_kv_a)
    dW_qkvT = _matmul(dqkv, u, "tn", BF16, "d_w_qkv")
    dW_convT = _matmul(dconv, u, "tn", BF16, "d_w_conv")
    dW_gateT = _matmul(dgate, u, "tn", BF16, "d_w_gate")
    pre_fo, r_fo = _split_wait(st_sfo, cs_fo, [dW_qkvT], "scatter_fo_wait")
    pre_fi, r_fi = _split_wait(st_sfi, cs_fi, [dW_qkvT], "scatter_fi_wait")
    pre_mid, r_mid = _split_wait(st_smid, cs_mid, [dW_qkvT], "scatter_mid_wait")
    fin_a = sum_all(pre_fo + pre_fi + pre_mid, r_fo + r_fi + r_mid, names_a)
    srcs = [(0, dW_qkvT[:n_qkv]), (n_qkv, dW_convT), (n_qkv + 3 * D, dW_gateT)]
    rows_of = []
    for p in range(4):
        for lo, src in srcs:
            a, b = max(lo, p * CS), min(lo + src.shape[0], (p + 1) * CS)
            if a < b:
                rows_of.append(src[a - lo:b - lo])
        rows_of.append(jnp.zeros((CSP - CS, D), BF16))
    dW_inT = jnp.concatenate(rows_of, axis=0).reshape(4, CSP, D)
    dW_qb_u = dW_qb.reshape(Q_LORA, N_HEADS, QK_PAD)[:, :, :QK_NOPE + QK_ROPE].reshape(Q_LORA, -1)
    names_b = ["w_in", "w_q_b", "w_kv_b"]
    p_b = [slabs(dW_inT), slabs(uncols(dW_qb_u)), slabs(uncols(dW_kvb))]
    du, s_b = _matmul(dqkv, W_qkvT, "nn", F32, "d_u_qkv", carry=_pair_plan(p_b))
    pre_b = add_pairs(p_b, s_b, names_b)
    cs_b = _scatter_copies(pre_b)
    st_b = _split_start(pre_b, cs_b, scale1, "scatter_last_start")
    du, fs_a = _matmul(dconv, W_convT, "nn", F32, "d_u_conv", add=du, carry=_sibling_plan(fin_a))
    du = _matmul(dgate, W_gateT, "nn", F32, "d_u_gate", add=du)
    grad_x, d_shift1, d_scale1 = _dx_final(dxa, du, x2, st_b[4][0])

    big = {}
    ws = dict(w_in=(w_inT, m_w_inT, v_w_inT), w_q_b=(w_q_b2, m_w_q_b[0], v_w_q_b[0]),
              w_kv_b=(w_kv_b2, m_w_kv_b[0], v_w_kv_b[0]), w_o_a=(w_o_a2, m_w_o_a[0], v_w_o_a[0]),
              w_o_b=(w_o_b2, m_w_o_b[0], v_w_o_b[0]), w_o=(w_o2, m_w_o[0], v_w_o[0]),
              w_ffn_in=(w_ffn_in2, m_w_ffn_in[0], v_w_ffn_in[0]), w_ffn_out=(w_ffn_out2, m_w_ffn_out[0], v_w_ffn_out[0]))

    def adam_of(nm, a, b, carry=None):
        w_, m_, v_ = ws[nm]
        return _adam_halves("adam_" + nm, w_, m_, v_, a, b, core_i, carry)

    for nm, a, b in zip(names_a, fin_a, fs_a):
        big[nm] = adam_of(nm, a, b, _token_plan(st_b[4][0]))[0]
    done = [big[nm][1] for nm in names_a] + [grad_x]
    pre_b, r_b = _split_wait(st_b, cs_b, done, "scatter_last_wait")
    fin_b = sum_all(pre_b, r_b, names_b)
    fs_b = _run_plan(_sibling_plan(fin_b), "sibling_last")
    for nm, a, b in zip(names_b, fin_b, fs_b):
        big[nm] = adam_of(nm, a, b)

    def pad_d(v):
        return jnp.pad(v, ((0, 0), (0, D - v.shape[1])))

    small = _pack_rows([d_ln1_g, d_ln1_b, d_ln2_g, d_ln2_b, pad_d(d_g_q), pad_d(d_g_kv), d_wconv,
                         d_shift1, d_scale1, d_gate1, d_shift2, d_scale2, d_gate2], 16, after=[pre_b[1]])
    small_all = _all_gather8(small, "gather_small")
    small_sum = _sum8(small_all)
    g_ln1_g, g_ln1_b, g_ln2_g, g_ln2_b = (small_sum[k:k + 1] for k in range(4))
    g_g_q, g_g_kv = small_sum[4:5, :Q_LORA], small_sum[5:6, :KV_LORA]
    g_wconv = lax.dynamic_slice(small_sum[6:9], (0, chip * CW), (3, CW))
    g_b_ada = small_sum[9:15].reshape(1, 6 * D)
    dmod_all = small_all[:, 9:15, :].reshape(8, 6 * D)
    g_w_ada = _ada_bwd(c_all, lax.dynamic_slice(dmod_all, (0, chip * NA), (8, NA)))
    big["w_ada"] = [g_w_ada] + list(_adam("adam_w_ada", w_ada2, m_w_ada[0], v_w_ada[0], g_w_ada))
    sm = {}
    for nm, w_, m_, v_, g_ in [("b_ada", b_ada, m_b_ada, v_b_ada, g_b_ada), ("g_q_a", g_q_a, m_g_q_a, v_g_q_a, g_g_q),
                               ("g_kv_a", g_kv_a, m_g_kv_a, v_g_kv_a, g_g_kv),
                               ("w_conv", w_conv[0], m_w_conv[0], v_w_conv[0], g_wconv),
                               ("ln1_g", ln1_g, m_ln1_g, v_ln1_g, g_ln1_g), ("ln1_b", ln1_b, m_ln1_b, v_ln1_b, g_ln1_b),
                               ("ln2_g", ln2_g, m_ln2_g, v_ln2_g, g_ln2_g), ("ln2_b", ln2_b, m_ln2_b, v_ln2_b, g_ln2_b)]:
        sm[nm] = (g_,) + tuple(_adam_small("adam_" + nm, w_, m_, v_, g_))

    order = ["w_ada", "b_ada", "w_in", "g_q_a", "w_q_b", "g_kv_a", "w_kv_b", "w_o_a", "w_conv", "w_o_b", "w_o",
             "ln1_g", "ln1_b", "w_ffn_in", "w_ffn_out", "ln2_g", "ln2_b"]
    lead = {"b_ada", "g_q_a", "g_kv_a", "ln1_g", "ln1_b", "ln2_g", "ln2_b"}

    def leaf(nm, k):
        val = big[nm][k] if nm in big else sm[nm][k]
        if nm == "w_in":
            val = val.T
        return val if nm in lead else val[None]

    outs = [loss, grad_x[None]]
    for k in range(4):
        outs += [leaf(nm, k) for nm in order]
    return tuple(outs)
```

```python
import functools

import jax
import jax.numpy as jnp
from jax import lax
from jax.experimental import pallas as pl
from jax.experimental.pallas import tpu as pltpu

F32, BF16 = jnp.float32, jnp.bfloat16
N_HEADS, QK_NOPE, QK_ROPE, V_HEAD = 16, 128, 64, 128
Q_LORA, KV_LORA = 512, 512
QK_PAD = 256
QKV_A = 1152
CHUNK_SHIFT = 6
ATTN_SCALE = (QK_NOPE + QK_ROPE) ** -0.5
ROPE_THETA = 10000.0
ALPHA = 2.0 ** 0.25
LN_EPS, RMS_EPS = 1e-5, 1e-6
ADAM_LR, ADAM_B1, ADAM_B2, ADAM_EPS, ADAM_WD, ADAM_STEP = 0.001, 0.9, 0.999, 1e-08, 0.01, 10
ADAM_C1 = 1.0 - ADAM_B1 ** ADAM_STEP
ADAM_C2 = 1.0 - ADAM_B2 ** ADAM_STEP
VMEM_LIMIT = 56 * 1024 * 1024
MESH = pl.DeviceIdType.MESH
ANY = pl.BlockSpec(memory_space=pl.ANY)
HBM_SPEC = pl.BlockSpec(memory_space=pltpu.HBM)
SEM_SPEC = pl.BlockSpec(memory_space=pltpu.SEMAPHORE)
NT = (((1,), (1,)), ((), ()))
TN = (((0,), (0,)), ((), ()))
NN = (((1,), (0,)), ((), ()))


def _params(sem=None):
    return pltpu.CompilerParams(dimension_semantics=sem, vmem_limit_bytes=VMEM_LIMIT)


def _pick(n, cands=(1408, 1024, 512, 384, 256, 128)):
    for t in cands:
        if n % t == 0:
            return t
    return n


def _row_tile(rows, row_bytes, budget, mult=8):
    best = mult
    for t in range(mult, rows + 1, mult):
        if rows % t == 0 and t * row_bytes <= budget:
            best = t
    return best


def _tile2(rows, cols, mult=8, budget=3 << 18):
    col_tiles = [t for t in range(128, cols + 1, 128) if cols % t == 0] or [cols]
    best = None
    for tc in col_tiles:
        for tr in range(mult, rows + 1, mult):
            if rows % tr == 0 and tr * tc <= budget and (best is None or (tr * tc, tc) > (best[0] * best[1], best[1])):
                best = (tr, tc)
    assert best is not None, (rows, cols)
    return best


def _sigmoid(x):
    return jax.nn.sigmoid(x)


class _Plan:
    def __init__(self, ins, outs, sems, start, finish, aliases=None):
        self.ins, self.outs, self.sems, self.start, self.finish = list(ins), list(outs), list(sems), start, finish
        self.aliases = dict(aliases or {})

    def io_aliases(self, first_in, first_out):
        return {first_in + i: first_out + o for i, o in self.aliases.items()}


def _token_plan(token):
    return _Plan([token], [], [], lambda *a: None, lambda *a: None)


def _run_plan(plan, name, ride=None):
    n_in, n_out = len(plan.ins), len(plan.outs)
    extra = [] if ride is None else list(ride)
    aliases = plan.io_aliases(0, 0)
    for k in range(len(extra)):
        aliases[n_in + k] = n_out + k

    def body(*refs):
        ins, outs, sems = refs[:n_in], refs[n_in + len(extra):n_in + len(extra) + n_out], refs[n_in + 2 * len(extra) + n_out:]
        plan.start(ins, outs, sems)
        plan.finish(ins, outs, sems)

    return pl.pallas_call(body, name=name, out_shape=plan.outs + [jax.ShapeDtypeStruct(r.shape, r.dtype) for r in extra],
                          in_specs=[ANY] * (n_in + len(extra)), out_specs=[ANY] * (n_out + len(extra)),
                          scratch_shapes=plan.sems, input_output_aliases=aliases,
                          compiler_params=_params())(*plan.ins, *extra)


def _matmul(a, b, mode, out_dtype, name, add=None, carry=None, shards=None):
    if mode == "nn":
        (M, K), N, dims = a.shape, b.shape[-1] * (4 if shards else 1), NN
    elif mode == "nt":
        (M, K), N, dims = a.shape, b.shape[-2], NT
    else:
        (K, M), N, dims = a.shape, b.shape[1], TN
    split_n = shards and mode != "nt"
    tm = _pick(M)
    tn = _pick(N // 4) if split_n else _pick(N)
    if shards and mode == "nt":
        tk = _pick(K // 4)
    else:
        tk = K if K <= 2048 else _pick(K)
    nk = K // tk
    per = (N // 4 // tn) if split_n else (K // 4 // tk if shards else 1)
    a_spec = (pl.BlockSpec((tk, tm), lambda i, j, k: (k, i)) if mode == "tn"
              else pl.BlockSpec((tm, tk), lambda i, j, k: (i, k)))
    if shards == "b" and mode == "nn":
        b_spec = pl.BlockSpec((None, tk, tn), lambda i, j, k: (j // per, k, j % per))
    elif shards == "b":
        b_spec = pl.BlockSpec((None, tn, tk), lambda i, j, k: (k // per, j, k % per))
    else:
        b_spec = (pl.BlockSpec((tn, tk), lambda i, j, k: (j, k)) if mode == "nt"
                  else pl.BlockSpec((tk, tn), lambda i, j, k: (k, j)))
    o_spec = pl.BlockSpec((tm, tn), lambda i, j, k: (i, j))
    o_shape = (M, N)
    if shards == "o":
        o_spec, o_shape = pl.BlockSpec((None, tm, tn), lambda i, j, k: (j // per, i, j % per)), (4, M, N // 4)
    has_add = add is not None
    n_ci = len(carry.ins) if carry else 0
    n_co = len(carry.outs) if carry else 0
    n_in = 2 + has_add
    grid = (M // tm, N // tn, nk)

    def body(*refs):
        a_ref, b_ref = refs[0], refs[1]
        add_ref = refs[2] if has_add else None
        o_ref = refs[n_in + n_ci]
        acc_ref = refs[n_in + n_ci + 1 + n_co] if nk > 1 else None
        c_ins = refs[n_in:n_in + n_ci]
        c_outs = refs[n_in + n_ci + 1:n_in + n_ci + 1 + n_co]
        c_sems = refs[n_in + n_ci + 1 + n_co + (nk > 1):]
        i, j, k = pl.program_id(0), pl.program_id(1), pl.program_id(2)

        if carry:
            @pl.when((i == 0) & (j == 0) & (k == 0))
            def _():
                carry.start(c_ins, c_outs, c_sems)

        part = lax.dot_general(a_ref[...], b_ref[...], dims, preferred_element_type=F32)
        if nk == 1:
            o_ref[...] = (part + add_ref[...] if has_add else part).astype(o_ref.dtype)
        else:
            @pl.when(k == 0)
            def _():
                acc_ref[...] = part

            @pl.when((k > 0) & (k < nk - 1))
            def _():
                acc_ref[...] += part

            @pl.when(k == nk - 1)
            def _():
                r = acc_ref[...] + part
                if has_add:
                    r = r + add_ref[...]
                o_ref[...] = r.astype(o_ref.dtype)

        if carry:
            @pl.when((i == grid[0] - 1) & (j == grid[1] - 1) & (k == nk - 1))
            def _():
                carry.finish(c_ins, c_outs, c_sems)

    ins = [a, b] + ([add] if has_add else []) + (carry.ins if carry else [])
    in_specs = [a_spec, b_spec] + ([o_spec] if has_add else []) + [ANY] * n_ci
    res = pl.pallas_call(
        body, name=name, grid=grid,
        in_specs=in_specs, out_specs=[o_spec] + [ANY] * n_co,
        out_shape=[jax.ShapeDtypeStruct(o_shape, out_dtype)] + (carry.outs if carry else []),
        scratch_shapes=([pltpu.VMEM((tm, tn), F32)] if nk > 1 else []) + (carry.sems if carry else []),
        input_output_aliases=carry.io_aliases(n_in, 1) if carry else {},
        compiler_params=_params(("arbitrary",) * 3 if carry else ("parallel", "parallel", "arbitrary")),
    )(*ins)
    return (res[0], res[1:]) if carry else res[0]


def _rows(body, name, n_rows, tm, ins, outs, accs=(), carry=None):
    grid = (n_rows // tm,)

    def halo(arr):
        return 16 if arr.dtype == BF16 else 8

    arrays, in_specs = [], []
    for spec in ins:
        kind, arr = spec[0], spec[1]
        arrays.append(arr)
        if kind == "row":
            _, _, cb, w = spec
            in_specs.append(pl.BlockSpec((tm, w), lambda i, cb=cb: (i, cb)))
        elif kind == "full":
            in_specs.append(pl.BlockSpec(arr.shape, lambda i, nd=arr.ndim: (0,) * nd))
        elif kind == "prev":
            _, _, cb, w = spec
            h = halo(arr)
            in_specs.append(pl.BlockSpec((h, w), lambda i, cb=cb, per=tm // h: (jnp.maximum(i * per - 1, 0), cb)))
        else:
            _, _, cb, w = spec
            h = halo(arr)
            in_specs.append(pl.BlockSpec((h, w), lambda i, cb=cb, per=tm // h, last=n_rows // h - 1:
                                         (jnp.minimum((i + 1) * per, last), cb)))
    out_shape = [jax.ShapeDtypeStruct((n_rows, w), dt) for (w, dt) in outs]
    out_specs = [pl.BlockSpec((tm, w), lambda i: (i, 0)) for (w, _) in outs]
    out_shape += [jax.ShapeDtypeStruct(s, F32) for s in accs]
    out_specs += [pl.BlockSpec(s, lambda i, nd=len(s): (0,) * nd) for s in accs]
    n_in, n_out, n_acc = len(ins), len(outs), len(accs)
    n_ci = len(carry.ins) if carry else 0
    n_co = len(carry.outs) if carry else 0

    def kernel_body(*refs):
        first = n_in + n_ci
        c_ins, c_outs, c_sems = refs[n_in:first], refs[first + n_out + n_acc:first + n_out + n_acc + n_co], refs[first + n_out + n_acc + n_co:]
        if carry:
            @pl.when(pl.program_id(0) == 0)
            def _():
                carry.start(c_ins, c_outs, c_sems)

        body(pl.program_id(0), refs[:n_in], refs[first:first + n_out], refs[first + n_out:first + n_out + n_acc])
        if carry:
            @pl.when(pl.program_id(0) == grid[0] - 1)
            def _():
                carry.finish(c_ins, c_outs, c_sems)

    res = pl.pallas_call(
        kernel_body, name=name, grid=grid, in_specs=in_specs + [ANY] * n_ci, out_specs=out_specs + [ANY] * n_co,
        out_shape=out_shape + (carry.outs if carry else []), scratch_shapes=carry.sems if carry else [],
        input_output_aliases=carry.io_aliases(n_in, n_out + n_acc) if carry else {},
        compiler_params=_params(("arbitrary",)),
    )(*arrays, *(carry.ins if carry else []))
    return (res[:n_out + n_acc], res[n_out + n_acc:]) if carry else res


def _acc_add(i, ref, val):
    @pl.when(i == 0)
    def _():
        ref[...] = val

    @pl.when(i > 0)
    def _():
        ref[...] += val


def _rope(t, tab, sign):
    c, sa, sb = tab[:, 0:128], tab[:, 128:256], tab[:, 256:384]
    rot = pltpu.roll(t, 96, 1) * sa + pltpu.roll(t, 32, 1) * sb
    return t * c + rot if sign > 0 else t * c - rot


def _ln_stats(r):
    mu = jnp.mean(r, axis=-1, keepdims=True)
    d = r - mu
    var = jnp.mean(d * d, axis=-1, keepdims=True)
    rstd = lax.rsqrt(var + LN_EPS)
    return d * rstd, rstd


def _ln_bwd(dxh, xh, rstd):
    m1 = jnp.mean(dxh, axis=-1, keepdims=True)
    m2 = jnp.mean(dxh * xh, axis=-1, keepdims=True)
    return rstd * (dxh - m1 - xh * m2)


def _modulate(x, scale, shift, name):
    S, D = x.shape

    def body(i, ins, outs, accs):
        outs[0][...] = (ins[0][...] * (1.0 + ins[1][...]) + ins[2][...]).astype(BF16)

    return _rows(body, name, S, _pick(S, (256, 128)), [("row", x, 0, D), ("full", scale), ("full", shift)], [(D, BF16)])[0]


def _rms_fwd(pq, tab, g_q, g_kv):
    S = pq.shape[0]

    def body(i, ins, outs, accs):
        pq_ref, tab_ref, gq_ref, gkv_ref = ins

        def rms(x, g):
            return x * lax.rsqrt(jnp.mean(x * x, axis=-1, keepdims=True) + RMS_EPS) * g

        outs[0][...] = rms(pq_ref[:, 0:Q_LORA], gq_ref[...]).astype(BF16)
        outs[1][...] = rms(pq_ref[:, Q_LORA:Q_LORA + KV_LORA], gkv_ref[...]).astype(BF16)
        outs[2][...] = _rope(pq_ref[:, Q_LORA + KV_LORA:QKV_A], tab_ref[...], 1).astype(BF16)

    return _rows(body, "rms_fwd", S, _pick(S, (256, 128)),
                 [("row", pq, 0, QKV_A), ("row", tab, 0, 384), ("full", g_q), ("full", g_kv)],
                 [(Q_LORA, BF16), (KV_LORA, BF16), (128, BF16)])


def _q_rope(q, tab):
    S, W = q.shape

    def body(i, ins, outs, accs):
        q_ref, tab_ref = ins
        t = tab_ref[...]
        for h in range(N_HEADS):
            lo = h * QK_PAD
            outs[0][:, lo:lo + 128] = q_ref[:, lo:lo + 128].astype(BF16)
            outs[0][:, lo + 128:lo + 256] = _rope(q_ref[:, lo + 128:lo + 256], t, 1).astype(BF16)

    return _rows(body, "q_rope", S, _pick(S, (256, 128)), [("row", q, 0, W), ("row", tab, 0, 384)], [(W, BF16)])[0]


def _allowed(q0, k0, bq):
    row = q0 + lax.broadcasted_iota(jnp.int32, (bq, bq), 0)
    col = k0 + lax.broadcasted_iota(jnp.int32, (bq, bq), 1)
    return (col >> CHUNK_SHIFT) <= (row >> CHUNK_SHIFT)


ATTN_BLOCK = 512


def _attn_fwd(q, kv, kr, carry=None):
    S = q.shape[0]
    bq = min(ATTN_BLOCK, S)
    nq = S // bq
    n_ci = len(carry.ins) if carry else 0
    n_co = len(carry.outs) if carry else 0

    def body(*refs):
        q_ref, kn_ref, v_ref, kr_ref = refs[:4]
        o_ref, lse_ref = refs[4 + n_ci:6 + n_ci]
        c_ins, c_outs = refs[4:4 + n_ci], refs[6 + n_ci:6 + n_ci + n_co]
        kcat = refs[6 + n_ci + n_co]
        c_sems = refs[7 + n_ci + n_co:]
        qi = pl.program_id(1)
        if carry:
            @pl.when((pl.program_id(0) == 0) & (qi == 0))
            def _():
                carry.start(c_ins, c_outs, c_sems)

        @pl.when(qi == 0)
        def _():
            kcat[:, 0:128] = kn_ref[...]
            kcat[:, 128:256] = kr_ref[...]

        qv = q_ref[...]

        def step(j, carry, masked):
            m, l, acc = carry
            off = pl.multiple_of(j * bq, bq)
            s = lax.dot_general(qv, kcat[pl.ds(off, bq), :], NT, preferred_element_type=F32) * ATTN_SCALE
            if masked:
                s = jnp.where(_allowed(qi * bq, off, bq), s, -1e30)
            m_new = jnp.maximum(m, jnp.max(s, axis=1, keepdims=True))
            a = jnp.exp(m - m_new)
            p = jnp.exp(s - m_new)
            l = a * l + jnp.sum(p, axis=1, keepdims=True)
            acc = a * acc + jnp.dot(p.astype(BF16), v_ref[pl.ds(off, bq), :], preferred_element_type=F32)
            return m_new, l, acc

        init = (jnp.full((bq, 1), -1e30, F32), jnp.zeros((bq, 1), F32), jnp.zeros((bq, V_HEAD), F32))
        below = lax.fori_loop(0, qi, lambda j, cr: step(j, cr, False), init)
        m, l, acc = step(qi, below, True)
        o_ref[...] = (acc / l).astype(BF16)
        lse_ref[0] = m + jnp.log(l)
        if carry:
            @pl.when((pl.program_id(0) == N_HEADS - 1) & (qi == nq - 1))
            def _():
                carry.finish(c_ins, c_outs, c_sems)

    res = pl.pallas_call(
        body, name="attn_fwd", grid=(N_HEADS, nq),
        in_specs=[pl.BlockSpec((bq, QK_PAD), lambda h, i: (i, h)),
                  pl.BlockSpec((S, 128), lambda h, i: (0, 2 * h)),
                  pl.BlockSpec((S, 128), lambda h, i: (0, 2 * h + 1)),
                  pl.BlockSpec((S, 128), lambda h, i: (0, 0))] + [ANY] * n_ci,
        out_specs=[pl.BlockSpec((bq, V_HEAD), lambda h, i: (i, h)),
                   pl.BlockSpec((1, bq, 1), lambda h, i: (h, i, 0))] + [ANY] * n_co,
        out_shape=[jax.ShapeDtypeStruct((S, N_HEADS * V_HEAD), BF16),
                   jax.ShapeDtypeStruct((N_HEADS, S, 1), F32)] + (carry.outs if carry else []),
        scratch_shapes=[pltpu.VMEM((S, QK_PAD), BF16)] + (carry.sems if carry else []),
        input_output_aliases=carry.io_aliases(4, 2) if carry else {},
        compiler_params=_params(("arbitrary", "arbitrary")),
    )(q, kv, kv, kr, *(carry.ins if carry else []))
    return res[0], res[1], res[2:]


def _attn_bwd(q, kv, kr, do, o, lse, tab, carry=None):
    S = q.shape[0]
    bq = min(ATTN_BLOCK, S)
    nq = S // bq

    n_ci = len(carry.ins) if carry else 0
    n_co = len(carry.outs) if carry else 0

    def body(*refs):
        q_ref, kn_ref, v_ref, kr_ref, do_ref, o_ref, lse_ref, tab_ref = refs[:8]
        dq_ref, dkv_ref, dkr_ref = refs[8 + n_ci:11 + n_ci]
        dq_acc, dk_acc, dv_acc, kcat, delta = refs[11 + n_ci + n_co:16 + n_ci + n_co]
        c_ins, c_outs, c_sems = refs[8:8 + n_ci], refs[11 + n_ci:11 + n_ci + n_co], refs[16 + n_ci + n_co:]
        h = pl.program_id(0)
        if carry:
            @pl.when(h == 0)
            def _():
                carry.start(c_ins, c_outs, c_sems)

        dq_acc[...] = jnp.zeros_like(dq_acc)
        dk_acc[...] = jnp.zeros_like(dk_acc)
        dv_acc[...] = jnp.zeros_like(dv_acc)
        kcat[:, 0:128] = kn_ref[...]
        kcat[:, 128:256] = kr_ref[...]
        for r in range(nq):
            rows = slice(r * bq, (r + 1) * bq)
            delta[rows, :] = jnp.sum(do_ref[rows, :].astype(F32) * o_ref[rows, :].astype(F32), axis=1, keepdims=True)

        def pair(i, j, masked):
            rows_i = pl.ds(pl.multiple_of(i * bq, bq), bq)
            rows_j = pl.ds(pl.multiple_of(j * bq, bq), bq)
            qv, dov, k = q_ref[rows_i, :], do_ref[rows_i, :], kcat[rows_j, :]
            s = lax.dot_general(qv, k, NT, preferred_element_type=F32) * ATTN_SCALE
            if masked:
                s = jnp.where(_allowed(i * bq, j * bq, bq), s, -1e30)
            p = jnp.exp(s - lse_ref[0, rows_i, :])
            dv_acc[rows_j, :] += lax.dot_general(p.astype(BF16), dov, TN, preferred_element_type=F32)
            dp = lax.dot_general(dov, v_ref[rows_j, :], NT, preferred_element_type=F32)
            ds = (p * (dp - delta[rows_i, :]) * ATTN_SCALE).astype(BF16)
            dk_acc[rows_j, :] += lax.dot_general(ds, qv, TN, preferred_element_type=F32)
            dq_acc[rows_i, :] += jnp.dot(ds, k, preferred_element_type=F32)

        def kv_step(j, _):
            pair(j, j, True)

            def q_step(i, _):
                pair(i, j, False)
                return 0

            lax.fori_loop(j + 1, nq, q_step, 0)
            return 0

        lax.fori_loop(0, nq, kv_step, 0)

        for r in range(nq):
            rows = slice(r * bq, (r + 1) * bq)
            dq_ref[rows, 0:128] = dq_acc[rows, 0:128].astype(BF16)
            dq_ref[rows, 128:256] = _rope(dq_acc[rows, 128:256], tab_ref[rows, :], -1).astype(BF16)
        dkv_ref[:, 0:128] = dk_acc[:, 0:128].astype(BF16)
        dkv_ref[:, 128:256] = dv_acc[...].astype(BF16)

        @pl.when(h == 0)
        def _():
            dkr_ref[...] = dk_acc[:, 128:256]

        @pl.when(h > 0)
        def _():
            dkr_ref[...] += dk_acc[:, 128:256]

        @pl.when(h == N_HEADS - 1)
        def _():
            for r in range(nq):
                rows = slice(r * bq, (r + 1) * bq)
                dkr_ref[rows, :] = _rope(dkr_ref[rows, :], tab_ref[rows, :], -1)
            if carry:
                carry.finish(c_ins, c_outs, c_sems)

    W = N_HEADS * QK_PAD
    res = pl.pallas_call(
        body, name="attn_bwd", grid=(N_HEADS,),
        in_specs=[pl.BlockSpec((S, QK_PAD), lambda h: (0, h)),
                  pl.BlockSpec((S, 128), lambda h: (0, 2 * h)),
                  pl.BlockSpec((S, 128), lambda h: (0, 2 * h + 1)),
                  pl.BlockSpec((S, 128), lambda h: (0, 0)),
                  pl.BlockSpec((S, V_HEAD), lambda h: (0, h)),
                  pl.BlockSpec((S, V_HEAD), lambda h: (0, h)),
                  pl.BlockSpec((1, S, 1), lambda h: (h, 0, 0)),
                  pl.BlockSpec((S, 384), lambda h: (0, 0))] + [ANY] * n_ci,
        out_specs=[pl.BlockSpec((S, QK_PAD), lambda h: (0, h)),
                   pl.BlockSpec((S, QK_PAD), lambda h: (0, h)),
                   pl.BlockSpec((S, 128), lambda h: (0, 0))] + [ANY] * n_co,
        out_shape=[jax.ShapeDtypeStruct((S, W), BF16), jax.ShapeDtypeStruct((S, W), BF16),
                   jax.ShapeDtypeStruct((S, 128), F32)] + (carry.outs if carry else []),
        scratch_shapes=[pltpu.VMEM((S, QK_PAD), F32), pltpu.VMEM((S, QK_PAD), F32), pltpu.VMEM((S, V_HEAD), F32),
                        pltpu.VMEM((S, QK_PAD), BF16), pltpu.VMEM((S, 1), F32)]
        + (carry.sems if carry else []),
        input_output_aliases=carry.io_aliases(8, 3) if carry else {},
        compiler_params=_params(("arbitrary",)),
    )(q, kv, kv, kr, do, o, lse, tab, *(carry.ins if carry else []))
    return res[0], res[1], res[2], res[3:]


def _shift_down(cur, prev, i, n):
    tm, h = cur.shape[0], prev.shape[0]
    prev = jnp.where(i == 0, jnp.zeros_like(prev), prev)
    full = jnp.concatenate([prev, cur], axis=0)
    return pltpu.roll(full, n, 0)[h:h + tm, :]


def _shift_up(cur, nxt, i, last, n):
    tm, h = cur.shape[0], nxt.shape[0]
    nxt = jnp.where(i == last, jnp.zeros_like(nxt), nxt)
    full = jnp.concatenate([cur, nxt], axis=0)
    return pltpu.roll(full, tm + h - n, 0)[0:tm, :]


def _conv_fwd(pc, w_conv):
    S, D = pc.shape[0], pc.shape[1] // 3
    tm = _pick(S, (256, 128))

    def body(i, ins, outs, accs):
        b_ref, c_ref, x_ref, cp_ref, xp_ref, w_ref = ins
        z = c_ref[...].astype(F32) * x_ref[...].astype(F32)
        zp = cp_ref[...].astype(F32) * xp_ref[...].astype(F32)
        cz = w_ref[0:1, :] * _shift_down(z, zp, i, 2) + w_ref[1:2, :] * _shift_down(z, zp, i, 1) + w_ref[2:3, :] * z
        outs[0][...] = (b_ref[...].astype(F32) * cz).astype(BF16)

    return _rows(body, "conv_fwd", S, tm,
                 [("row", pc, 0, D), ("row", pc, 1, D), ("row", pc, 2, D), ("prev", pc, 1, D), ("prev", pc, 2, D),
                  ("full", w_conv)], [(D, BF16)])[0]


def _conv_bwd(dhb, pc, w_conv):
    S, D = dhb.shape
    tm = _pick(S, (256, 128))
    last = S // tm - 1

    def body(i, ins, outs, accs):
        g_ref, b_ref, c_ref, x_ref, cp_ref, xp_ref, gn_ref, bn_ref, w_ref = ins
        w0, w1, w2 = w_ref[0:1, :], w_ref[1:2, :], w_ref[2:3, :]
        c, x, g = c_ref[...].astype(F32), x_ref[...].astype(F32), g_ref[...].astype(F32)
        z = c * x
        zp = cp_ref[...].astype(F32) * xp_ref[...].astype(F32)
        z1, z2 = _shift_down(z, zp, i, 1), _shift_down(z, zp, i, 2)
        cz = w0 * z2 + w1 * z1 + w2 * z
        dcz = g * b_ref[...].astype(F32)
        dczn = gn_ref[...].astype(F32) * bn_ref[...].astype(F32)
        dz = w2 * dcz + w1 * _shift_up(dcz, dczn, i, last, 1) + w0 * _shift_up(dcz, dczn, i, last, 2)
        outs[0][:, 0:D] = (g * cz).astype(BF16)
        outs[0][:, D:2 * D] = (dz * x).astype(BF16)
        outs[0][:, 2 * D:3 * D] = (dz * c).astype(BF16)
        dw = jnp.concatenate([jnp.sum(dcz * z2, axis=0, keepdims=True), jnp.sum(dcz * z1, axis=0, keepdims=True),
                              jnp.sum(dcz * z, axis=0, keepdims=True)], axis=0)
        _acc_add(i, accs[0], dw)

    return _rows(body, "conv_bwd", S, tm,
                 [("row", dhb, 0, D), ("row", pc, 0, D), ("row", pc, 1, D), ("row", pc, 2, D),
                  ("prev", pc, 1, D), ("prev", pc, 2, D), ("next", dhb, 0, D), ("next", pc, 0, D), ("full", w_conv)],
                 [(3 * D, BF16)], [(3, D)])


def _merge_fwd(y_a, y_b, pg):
    S, D = y_a.shape

    def body(i, ins, outs, accs):
        ya, yb, ga, gb = ins
        outs[0][...] = (_sigmoid(ga[...].astype(F32)) * ya[...].astype(F32)
                        + _sigmoid(gb[...].astype(F32)) * yb[...].astype(F32)).astype(BF16)

    return _rows(body, "merge_fwd", S, _pick(S, (256, 128)),
                 [("row", y_a, 0, D), ("row", y_b, 0, D), ("row", pg, 0, D), ("row", pg, 1, D)], [(D, BF16)])[0]


def _merge_bwd(dm, y_a, y_b, pg):
    S, D = dm.shape

    def body(i, ins, outs, accs):
        d, ya, yb = ins[0][...].astype(F32), ins[1][...].astype(F32), ins[2][...].astype(F32)
        sa, sb = _sigmoid(ins[3][...].astype(F32)), _sigmoid(ins[4][...].astype(F32))
        outs[0][...] = (d * sa).astype(BF16)
        outs[1][...] = (d * sb).astype(BF16)
        outs[2][:, 0:D] = (d * ya * (sa * (1.0 - sa))).astype(BF16)
        outs[2][:, D:2 * D] = (d * yb * (sb * (1.0 - sb))).astype(BF16)

    return _rows(body, "merge_bwd", S, _pick(S, (256, 128)),
                 [("row", dm, 0, D), ("row", y_a, 0, D), ("row", y_b, 0, D), ("row", pg, 0, D), ("row", pg, 1, D)],
                 [(D, BF16), (D, BF16), (2 * D, BF16)])


def _ln1_fwd(x, mix, gate1, g, b, scale2, shift2):
    S, D = x.shape

    def body(i, ins, outs, accs):
        x_ref, mix_ref, gate_ref, g_ref, b_ref, sc_ref, sh_ref = ins
        xh, _ = _ln_stats(ALPHA * x_ref[...] + gate_ref[...] * mix_ref[...])
        x1 = xh * g_ref[...] + b_ref[...]
        outs[0][...] = x1
        outs[1][...] = (x1 * (1.0 + sc_ref[...]) + sh_ref[...]).astype(BF16)

    return _rows(body, "ln1_fwd", S, _pick(S, (256, 128)),
                 [("row", x, 0, D), ("row", mix, 0, D), ("full", gate1), ("full", g), ("full", b),
                  ("full", scale2), ("full", shift2)], [(D, F32), (D, BF16)])


def _swiglu_fwd(hh, carry=None):
    S, F = hh.shape[0], hh.shape[1] // 2

    def body(i, ins, outs, accs):
        hg = ins[0][...].astype(F32)
        outs[0][...] = (hg * _sigmoid(hg) * ins[1][...].astype(F32)).astype(BF16)

    res = _rows(body, "swiglu_fwd", S, _pick(S, (128,)), [("row", hh, 0, F), ("row", hh, 1, F)], [(F, BF16)], carry=carry)
    return (res[0][0], res[1]) if carry else res[0]


def _swiglu_bwd(dact, hh):
    S, F = dact.shape

    def body(i, ins, outs, accs):
        d, hg, hu = ins[0][...].astype(F32), ins[1][...].astype(F32), ins[2][...].astype(F32)
        sg = _sigmoid(hg)
        outs[0][:, 0:F] = (d * hu * (sg * (1.0 + hg * (1.0 - sg)))).astype(BF16)
        outs[0][:, F:2 * F] = (d * (hg * sg)).astype(BF16)

    return _rows(body, "swiglu_bwd", S, _pick(S, (128,)),
                 [("row", dact, 0, F), ("row", hh, 0, F), ("row", hh, 1, F)], [(2 * F, BF16)])[0]


def _ln2_loss_bwd(x1, ffn, gate2, g, b, target):
    S, D = x1.shape

    def body(i, ins, outs, accs):
        x1_ref, f_ref, gate_ref, g_ref, b_ref, t_ref = ins
        f = f_ref[...]
        xh, rstd = _ln_stats(ALPHA * x1_ref[...] + gate_ref[...] * f)
        e = xh * g_ref[...] + b_ref[...] - t_ref[...]
        dy = e * (1.0 / D)
        dr = _ln_bwd(dy * g_ref[...], xh, rstd)
        outs[0][...] = (gate_ref[...] * dr).astype(BF16)
        outs[1][...] = ALPHA * dr
        _acc_add(i, accs[0], jnp.full((1, 128), (0.5 / D) * jnp.sum(e * e), F32))
        _acc_add(i, accs[1], jnp.sum(dy * xh, axis=0, keepdims=True))
        _acc_add(i, accs[2], jnp.sum(dy, axis=0, keepdims=True))
        _acc_add(i, accs[3], jnp.sum(dr * f, axis=0, keepdims=True))

    return _rows(body, "ln2_loss_bwd", S, _pick(S, (256, 128)),
                 [("row", x1, 0, D), ("row", ffn, 0, D), ("full", gate2), ("full", g), ("full", b), ("row", target, 0, D)],
                 [(D, BF16), (D, F32)], [(1, 128), (1, D), (1, D), (1, D)])


def _ln1_bwd(x, mix, dx1a, du2, gate1, g, b, scale2):
    S, D = x.shape

    def body(i, ins, outs, accs):
        x_ref, mix_ref, da_ref, du_ref, gate_ref, g_ref, b_ref, sc_ref = ins
        mix, du = mix_ref[...], du_ref[...]
        xh, rstd = _ln_stats(ALPHA * x_ref[...] + gate_ref[...] * mix)
        x1 = xh * g_ref[...] + b_ref[...]
        dx1 = da_ref[...] + du * (1.0 + sc_ref[...])
        dr = _ln_bwd(dx1 * g_ref[...], xh, rstd)
        outs[0][...] = (gate_ref[...] * dr).astype(BF16)
        outs[1][...] = ALPHA * dr
        _acc_add(i, accs[0], jnp.sum(du, axis=0, keepdims=True))
        _acc_add(i, accs[1], jnp.sum(du * x1, axis=0, keepdims=True))
        _acc_add(i, accs[2], jnp.sum(dx1 * xh, axis=0, keepdims=True))
        _acc_add(i, accs[3], jnp.sum(dx1, axis=0, keepdims=True))
        _acc_add(i, accs[4], jnp.sum(dr * mix, axis=0, keepdims=True))

    return _rows(body, "ln1_bwd", S, _pick(S, (256, 128)),
                 [("row", x, 0, D), ("row", mix, 0, D), ("row", dx1a, 0, D), ("row", du2, 0, D),
                  ("full", gate1), ("full", g), ("full", b), ("full", scale2)],
                 [(D, BF16), (D, F32)], [(1, D)] * 5)


def _rms_bwd(d_rq, d_rkv, pq, dkr, g_q, g_kv):
    S = pq.shape[0]

    def body(i, ins, outs, accs):
        dq_ref, dkv_ref, pq_ref, dkr_ref, gq_ref, gkv_ref = ins

        def rms_bwd(dy, x, g):
            r = lax.rsqrt(jnp.mean(x * x, axis=-1, keepdims=True) + RMS_EPS)
            dyg = dy * g
            dx = r * dyg - x * (r * r * r) * jnp.mean(dyg * x, axis=-1, keepdims=True)
            return dx, jnp.sum(dy * (x * r), axis=0, keepdims=True)

        dxq, dgq = rms_bwd(dq_ref[...], pq_ref[:, 0:Q_LORA], gq_ref[...])
        dxkv, dgkv = rms_bwd(dkv_ref[...], pq_ref[:, Q_LORA:Q_LORA + KV_LORA], gkv_ref[...])
        outs[0][:, 0:Q_LORA] = dxq.astype(BF16)
        outs[0][:, Q_LORA:Q_LORA + KV_LORA] = dxkv.astype(BF16)
        outs[0][:, Q_LORA + KV_LORA:QKV_A] = dkr_ref[...].astype(BF16)
        _acc_add(i, accs[0], dgq)
        _acc_add(i, accs[1], dgkv)

    return _rows(body, "rms_bwd", S, _pick(S, (256, 128)),
                 [("row", d_rq, 0, Q_LORA), ("row", d_rkv, 0, KV_LORA), ("row", pq, 0, QKV_A), ("row", dkr, 0, 128),
                  ("full", g_q), ("full", g_kv)], [(QKV_A, BF16)], [(1, Q_LORA), (1, KV_LORA)])


def _dx_final(dxa, du, x, scale1):
    S, D = x.shape

    def body(i, ins, outs, accs):
        du = ins[1][...]
        outs[0][...] = ins[0][...] + du * (1.0 + ins[3][...])
        _acc_add(i, accs[0], jnp.sum(du, axis=0, keepdims=True))
        _acc_add(i, accs[1], jnp.sum(du * ins[2][...], axis=0, keepdims=True))

    return _rows(body, "dx_final", S, _pick(S, (256, 128)),
                 [("row", dxa, 0, D), ("row", du, 0, D), ("row", x, 0, D), ("full", scale1)],
                 [(D, F32)], [(1, D), (1, D)])


def _ada_fwd(c_all, w, bias):
    B, D = c_all.shape
    NA = w.shape[1]
    tn = _pick(NA, (512, 256, 128))

    def body(c_ref, w_ref, b_ref, o_ref):
        cv = c_ref[...]
        ca = (cv * _sigmoid(cv)).astype(BF16)
        o_ref[...] = jnp.dot(ca, w_ref[...].astype(BF16), preferred_element_type=F32) + b_ref[...]

    return pl.pallas_call(
        body, name="ada_fwd", grid=(NA // tn,),
        in_specs=[pl.BlockSpec((B, D), lambda j: (0, 0)), pl.BlockSpec((D, tn), lambda j: (0, j)),
                  pl.BlockSpec((1, tn), lambda j: (0, j))],
        out_specs=pl.BlockSpec((B, tn), lambda j: (0, j)),
        out_shape=jax.ShapeDtypeStruct((B, NA), F32),
        compiler_params=_params(("arbitrary",)),
    )(c_all, w, bias)


def _ada_bwd(c_all, dmod):
    B, D = c_all.shape
    NA = dmod.shape[1]
    tn = _pick(NA, (512, 256, 128))

    def body(c_ref, d_ref, o_ref):
        cv = c_ref[...]
        ca = (cv * _sigmoid(cv)).astype(BF16)
        o_ref[...] = lax.dot_general(ca, d_ref[...].astype(BF16), TN, preferred_element_type=F32)

    return pl.pallas_call(
        body, name="ada_bwd", grid=(NA // tn,),
        in_specs=[pl.BlockSpec((B, D), lambda j: (0, 0)), pl.BlockSpec((B, tn), lambda j: (0, j))],
        out_specs=pl.BlockSpec((D, tn), lambda j: (0, j)),
        out_shape=jax.ShapeDtypeStruct((D, NA), F32),
        compiler_params=_params(("arbitrary",)),
    )(c_all, dmod)


def _pack_rows(parts, n_rows, after=()):
    N = parts[0].shape[1]
    n = len(parts)

    def body(*refs):
        o_ref = refs[-1]
        o_ref[...] = jnp.zeros_like(o_ref)
        at = 0
        for r in refs[:n]:
            o_ref[at:at + r.shape[0], :] = r[...]
            at += r.shape[0]

    vmem = pl.BlockSpec(memory_space=pltpu.VMEM)
    return pl.pallas_call(body, name="pack_small", out_shape=jax.ShapeDtypeStruct((n_rows, N), F32),
                          in_specs=[vmem] * n + [ANY] * len(after), out_specs=vmem,
                          compiler_params=_params())(*parts, *after)


def _sum8(parts):
    _, R, N = parts.shape

    def body(p_ref, o_ref):
        acc = p_ref[0]
        for d in range(1, 8):
            acc = acc + p_ref[d]
        o_ref[...] = acc

    return pl.pallas_call(body, name="sum8", out_shape=jax.ShapeDtypeStruct((R, N), F32),
                          compiler_params=_params())(parts)


def _adam_math(w, g, m, v):
    m = ADAM_B1 * m + (1.0 - ADAM_B1) * g
    v = ADAM_B2 * v + (1.0 - ADAM_B2) * (g * g)
    delta = -ADAM_LR * ((m / ADAM_C1) / (jnp.sqrt(v / ADAM_C2) + ADAM_EPS) + ADAM_WD * w)
    return delta, m, v


def _adam(name, w, m, v, g, carry=None):
    R, C = w.shape
    tm = _row_tile(R, C * 4, 1 << 20)
    steps = R // tm
    n_ci = len(carry.ins) if carry else 0
    n_co = len(carry.outs) if carry else 0

    def body(*refs):
        w_ref, m_ref, v_ref, g_ref = refs[:4]
        d_ref, nm_ref, nv_ref = refs[4 + n_ci:7 + n_ci]
        c_ins, c_outs, c_sems = refs[4:4 + n_ci], refs[7 + n_ci:7 + n_ci + n_co], refs[7 + n_ci + n_co:]
        if carry:
            @pl.when(pl.program_id(0) == 0)
            def _():
                carry.start(c_ins, c_outs, c_sems)

        delta, nm, nv = _adam_math(w_ref[...], g_ref[...], m_ref[...], v_ref[...])
        d_ref[...] = delta
        nm_ref[...] = nm
        nv_ref[...] = nv
        if carry:
            @pl.when(pl.program_id(0) == steps - 1)
            def _():
                carry.finish(c_ins, c_outs, c_sems)

    spec = pl.BlockSpec((tm, C), lambda i: (i, 0))
    res = pl.pallas_call(
        body, name=name, grid=(steps,), in_specs=[spec] * 4 + [ANY] * n_ci, out_specs=[spec] * 3 + [ANY] * n_co,
        out_shape=[jax.ShapeDtypeStruct((R, C), F32)] * 3 + (carry.outs if carry else []),
        scratch_shapes=carry.sems if carry else [],
        input_output_aliases=carry.io_aliases(4, 3) if carry else {},
        compiler_params=_params(("arbitrary",)),
    )(w, m, v, g, *(carry.ins if carry else []))
    return (res[:3], res[3:]) if carry else res


def _adam_halves(name, w, m, v, mine, other, core, carry=None):
    R, C = w.shape
    Rh = mine.shape[0]
    tc = max(t for t in range(128, C + 1, 128) if C % t == 0 and R * t <= (3 << 17))
    steps = C // tc
    n_ci = len(carry.ins) if carry else 0
    n_co = len(carry.outs) if carry else 0

    def body(*refs):
        c_ref, w_ref, m_ref, v_ref, a_ref, b_ref = refs[:6]
        g_ref, d_ref, nm_ref, nv_ref = refs[6 + n_ci:10 + n_ci]
        c_ins, c_outs, c_sems = refs[6:6 + n_ci], refs[10 + n_ci:10 + n_ci + n_co], refs[10 + n_ci + n_co:]
        if carry:
            @pl.when(pl.program_id(0) == 0)
            def _():
                carry.start(c_ins, c_outs, c_sems)

        first = c_ref[0] == 0
        g = jnp.concatenate([jnp.where(first, a_ref[...], b_ref[...]),
                             jnp.where(first, b_ref[0:R - Rh, :], a_ref[0:R - Rh, :])], axis=0)
        delta, nm, nv = _adam_math(w_ref[...], g, m_ref[...], v_ref[...])
        g_ref[...] = g
        d_ref[...] = delta
        nm_ref[...] = nm
        nv_ref[...] = nv
        if carry:
            @pl.when(pl.program_id(0) == steps - 1)
            def _():
                carry.finish(c_ins, c_outs, c_sems)

    spec = pl.BlockSpec((R, tc), lambda i, c_ref: (0, i))
    h_spec = pl.BlockSpec((Rh, tc), lambda i, c_ref: (0, i))
    res = pl.pallas_call(
        body, name=name, out_shape=[jax.ShapeDtypeStruct((R, C), F32)] * 4 + (carry.outs if carry else []),
        grid_spec=pltpu.PrefetchScalarGridSpec(
            num_scalar_prefetch=1, grid=(steps,), in_specs=[spec, spec, spec, h_spec, h_spec] + [ANY] * n_ci,
            out_specs=[spec] * 4 + [ANY] * n_co, scratch_shapes=carry.sems if carry else []),
        input_output_aliases=carry.io_aliases(6, 4) if carry else {},
        compiler_params=_params(("arbitrary",)),
    )(core, w, m, v, mine, other, *(carry.ins if carry else []))
    return (res[:4], res[4:]) if carry else res


def _adam_small(name, w, m, v, g):
    def body(w_ref, m_ref, v_ref, g_ref, d_ref, nm_ref, nv_ref):
        delta, nm, nv = _adam_math(w_ref[...], g_ref[...], m_ref[...], v_ref[...])
        d_ref[...] = delta
        nm_ref[...] = nm
        nv_ref[...] = nv

    return pl.pallas_call(body, name=name, out_shape=[jax.ShapeDtypeStruct(w.shape, F32)] * 3,
                          compiler_params=_params())(w, m, v, g)


def _place():
    return lax.axis_index("x"), lax.axis_index("y"), lax.axis_index("c")


def _other_chips(x, y):
    return [(1 - x, y), (x, 1 - y), (1 - x, 1 - y)]


def _all_gather8(blk, name):
    R, N = blk.shape

    def body(x_ref, out_ref, send_sems, recv_sems, local_sem):
        x, y, c = _place()
        me = 4 * x + 2 * y + c
        mine = pltpu.make_async_copy(x_ref, out_ref.at[me], local_sem)
        mine.start()
        flips = [(j >> 2 & 1, j >> 1 & 1, j & 1) for j in range(1, 8)]
        peers = [((1 - x) if fx else x, (1 - y) if fy else y, (1 - c) if fc else c) for fx, fy, fc in flips]
        sends = []
        for j, peer in enumerate(peers):
            cp = pltpu.make_async_remote_copy(src_ref=x_ref, dst_ref=out_ref.at[me], send_sem=send_sems.at[j],
                                              recv_sem=recv_sems.at[j], device_id=peer, device_id_type=MESH)
            cp.start()
            sends.append(cp)
        for j, (px, py, pc) in enumerate(peers):
            pltpu.make_async_remote_copy(src_ref=x_ref, dst_ref=out_ref.at[4 * px + 2 * py + pc],
                                         send_sem=send_sems.at[j], recv_sem=recv_sems.at[j],
                                         device_id=(px, py, pc), device_id_type=MESH).wait_recv()
        for cp in sends:
            cp.wait_send()
        mine.wait()

    return pl.pallas_call(
        body, name=name, out_shape=jax.ShapeDtypeStruct((8, R, N), F32),
        in_specs=[pl.BlockSpec(memory_space=pltpu.VMEM)], out_specs=pl.BlockSpec(memory_space=pltpu.VMEM),
        scratch_shapes=[pltpu.SemaphoreType.DMA((7,)), pltpu.SemaphoreType.DMA((7,)), pltpu.SemaphoreType.DMA],
        compiler_params=_params(),
    )(blk)


def _piece(rows, piece):
    i, n, k = piece if len(piece) == 3 else (piece[0], piece[1], 1)
    assert rows % 16 == 0 and rows // 16 >= n, (rows, piece)
    lo, hi = (rows // 16 * i // n) * 16, (rows // 16 * (i + k) // n) * 16
    return pl.ds(lo, hi - lo)


def _scatter_plan(arrs, piece=(0, 1), into=None):
    n = len(arrs)

    def copies(ins, outs, sems):
        send_sems, recv_sems = sems
        x, y, c = _place()
        chips = _other_chips(x, y)
        cps = []
        for k in range(n):
            rows = _piece(arrs[k].shape[1], piece)
            for j, (px, py) in enumerate(chips):
                cps.append(pltpu.make_async_remote_copy(
                    src_ref=ins[k].at[2 * px + py, rows], dst_ref=outs[k].at[j, rows],
                    send_sem=send_sems.at[3 * k + j], recv_sem=recv_sems.at[3 * k + j],
                    device_id=(px, py, c), device_id_type=MESH))
        return cps

    def start(ins, outs, sems):
        for cp in copies(ins, outs, sems):
            cp.start()

    def finish(ins, outs, sems):
        for cp in copies(ins, outs, sems):
            cp.wait()

    return _Plan(list(arrs) + list(into or []), [jax.ShapeDtypeStruct((3,) + a.shape[1:], a.dtype) for a in arrs],
                 [pltpu.SemaphoreType.DMA((3 * n,))] * 2, start, finish,
                 aliases={n + k: k for k in range(n)} if into else None)


def _gather_plan(shards, piece=(0, 1), into=None, ici=True):
    n = len(shards)

    def parts(ins, outs, sems):
        s1, r1, s2, r2, loc = sems
        x, y, c = _place()
        me = 2 * x + y
        chips = _other_chips(x, y)
        sib = (x, y, 1 - c)

        def rows(k):
            return _piece(shards[k].shape[1], piece)

        def ici_copy(k, j, slab, to):
            return pltpu.make_async_remote_copy(src_ref=ins[k].at[c, rows(k)], dst_ref=outs[k].at[slab, c, rows(k)],
                                                send_sem=s1.at[3 * k + j], recv_sem=r1.at[3 * k + j],
                                                device_id=to, device_id_type=MESH)

        def d2d(k, j, slab, half):
            return pltpu.make_async_remote_copy(src_ref=outs[k].at[slab, half, rows(k)],
                                                dst_ref=outs[k].at[slab, half, rows(k)],
                                                send_sem=s2.at[3 * k + j], recv_sem=r2.at[3 * k + j],
                                                device_id=sib, device_id_type=MESH)

        def own(k):
            return pltpu.make_async_remote_copy(src_ref=ins[k].at[:, rows(k)], dst_ref=outs[k].at[me, :, rows(k)],
                                                send_sem=loc.at[2 * k], recv_sem=loc.at[2 * k + 1],
                                                device_id=sib, device_id_type=MESH)

        return c, me, chips, ici_copy, d2d, own

    def start(ins, outs, sems):
        c, me, chips, ici_copy, d2d, own = parts(ins, outs, sems)
        for k in range(n):
            for j, (px, py) in enumerate(chips):
                (ici_copy(k, j, me, (px, py, c)) if ici else d2d(k, j, 2 * px + py, c)).start()
        for k in range(n):
            own(k).start()

    def finish(ins, outs, sems):
        c, me, chips, ici_copy, d2d, own = parts(ins, outs, sems)
        if ici:
            for k in range(n):
                for j, (px, py) in enumerate(chips):
                    ici_copy(k, j, 2 * px + py, (px, py, c)).wait_recv()
                    d2d(k, j, 2 * px + py, c).start()
        for k in range(n):
            for j, (px, py) in enumerate(chips):
                d2d(k, j, 2 * px + py, 1 - c).wait_recv()
        for k in range(n):
            own(k).wait()
            for j, (px, py) in enumerate(chips):
                if ici:
                    ici_copy(k, j, me, (px, py, c)).wait_send()
                d2d(k, j, 2 * px + py, c).wait_send()

    return _Plan(list(shards) + list(into or []), [jax.ShapeDtypeStruct((4,) + a.shape, a.dtype) for a in shards],
                 [pltpu.SemaphoreType.DMA((3 * n,))] * 4 + [pltpu.SemaphoreType.DMA((2 * n,))], start, finish,
                 aliases={n + k: k for k in range(n)} if into else None)


def _pair_plan(parts):
    n = len(parts)

    def copies(ins, outs, sems):
        send_sems, recv_sems = sems
        x, y, c = _place()
        return [pltpu.make_async_remote_copy(src_ref=ins[k].at[p, 1 - c], dst_ref=outs[k].at[p],
                                             send_sem=send_sems.at[4 * k + p], recv_sem=recv_sems.at[4 * k + p],
                                             device_id=(x, y, 1 - c), device_id_type=MESH)
                for k in range(n) for p in range(4)]

    def start(ins, outs, sems):
        for cp in copies(ins, outs, sems):
            cp.start()

    def finish(ins, outs, sems):
        for cp in copies(ins, outs, sems):
            cp.wait()

    return _Plan(parts, [jax.ShapeDtypeStruct((4,) + a.shape[2:], a.dtype) for a in parts],
                 [pltpu.SemaphoreType.DMA((4 * n,))] * 2, start, finish)


def _sibling_plan(arrs):
    n = len(arrs)

    def copies(ins, outs, sems):
        send_sems, recv_sems = sems
        x, y, c = _place()
        return [pltpu.make_async_remote_copy(src_ref=ins[k], dst_ref=outs[k], send_sem=send_sems.at[k],
                                             recv_sem=recv_sems.at[k], device_id=(x, y, 1 - c), device_id_type=MESH)
                for k in range(n)]

    def start(ins, outs, sems):
        for cp in copies(ins, outs, sems):
            cp.start()

    def finish(ins, outs, sems):
        for cp in copies(ins, outs, sems):
            cp.wait()

    return _Plan(arrs, [jax.ShapeDtypeStruct(a.shape, a.dtype) for a in arrs],
                 [pltpu.SemaphoreType.DMA((n,))] * 2, start, finish)


def _scatter_copies(arrs):
    def copies(ins, land, send_sems, recv_sems):
        x, y, c = _place()
        return [pltpu.make_async_remote_copy(src_ref=ins[k].at[2 * px + py], dst_ref=land[k].at[j],
                                             send_sem=send_sems.at[3 * k + j], recv_sem=recv_sems.at[3 * k + j],
                                             device_id=(px, py, c), device_id_type=MESH)
                for k in range(len(arrs)) for j, (px, py) in enumerate(_other_chips(x, y))]

    return copies, [lax.empty((3,) + a.shape[1:], a.dtype) for a in arrs]


def _gather_copies(shards):
    def copies(ins, land, send_sems, recv_sems):
        x, y, c = _place()
        return [pltpu.make_async_remote_copy(src_ref=ins[k].at[c], dst_ref=land[k].at[2 * x + y, c],
                                             send_sem=send_sems.at[3 * k + j], recv_sem=recv_sems.at[3 * k + j],
                                             device_id=(px, py, c), device_id_type=MESH)
                for k in range(len(shards)) for j, (px, py) in enumerate(_other_chips(x, y))]

    return copies, [lax.empty((4,) + a.shape, a.dtype) for a in shards]


def _split_start(arrs, copies_lands, ride, name, after=()):
    copies, lands = copies_lands
    n = len(arrs)
    rides = list(ride) if isinstance(ride, (list, tuple)) else [ride]
    n_thru = 2 * n + len(rides)

    def body(*refs):
        first_out = n_thru + len(after)
        for cp in copies(refs[:n], refs[n:2 * n], refs[first_out], refs[first_out + 1]):
            cp.start()

    hbm = [pltpu.with_memory_space_constraint(a, pltpu.HBM) for a in list(arrs) + lands + rides]
    res = pl.pallas_call(
        body, name=name,
        out_shape=[pltpu.SemaphoreType.DMA((3 * n,)), pltpu.SemaphoreType.DMA((3 * n,))]
        + [pltpu.HBM(a.shape, a.dtype) for a in hbm],
        in_specs=[HBM_SPEC] * n_thru + [ANY] * len(after),
        out_specs=[SEM_SPEC, SEM_SPEC] + [HBM_SPEC] * n_thru,
        input_output_aliases={i: 2 + i for i in range(n_thru)},
        compiler_params=pltpu.CompilerParams(has_side_effects=pltpu.SideEffectType.DATAFLOW_SIDE_EFFECTING),
    )(*hbm, *after)
    return res[0], res[1], res[2:2 + n], res[2 + n:2 + 2 * n], list(res[2 + 2 * n:])


def _split_wait(started, copies_lands, after, name):
    send_sems, recv_sems, arrs, lands, _ = started
    copies = copies_lands[0]
    n = len(arrs)

    def body(*refs):
        for cp in copies(refs[:n], refs[n:2 * n], refs[2 * n], refs[2 * n + 1]):
            cp.wait_send()
            cp.wait_recv()

    res = pl.pallas_call(
        body, name=name, out_shape=[pltpu.HBM(a.shape, a.dtype) for a in list(arrs) + list(lands)],
        in_specs=[HBM_SPEC] * (2 * n) + [SEM_SPEC, SEM_SPEC] + [ANY] * len(after), out_specs=[HBM_SPEC] * (2 * n),
        input_output_aliases={i: i for i in range(2 * n)},
        compiler_params=pltpu.CompilerParams(has_side_effects=pltpu.SideEffectType.DATAFLOW_SIDE_EFFECTING),
    )(*arrs, *lands, send_sems, recv_sems, *after)
    return list(res[:n]), list(res[n:])


def _join_plans(plans):
    def split(seq, counts):
        out, at = [], 0
        for cnt in counts:
            out.append(seq[at:at + cnt])
            at += cnt
        return out

    n_i, n_o, n_s = ([len(getattr(p, f)) for p in plans] for f in ("ins", "outs", "sems"))

    def start(ins, outs, sems):
        for p, i, o, s in zip(plans, split(ins, n_i), split(outs, n_o), split(sems, n_s)):
            p.start(i, o, s)

    def finish(ins, outs, sems):
        for p, i, o, s in zip(plans, split(ins, n_i), split(outs, n_o), split(sems, n_s)):
            p.finish(i, o, s)

    aliases, at_i, at_o = {}, 0, 0
    for p in plans:
        aliases.update(p.io_aliases(at_i, at_o))
        at_i, at_o = at_i + len(p.ins), at_o + len(p.outs)
    return _Plan(sum((p.ins for p in plans), []), sum((p.outs for p in plans), []), sum((p.sems for p in plans), []),
                 start, finish, aliases)


def _add_pair(parts, sib, core, name):
    P4, _, Rh, C = parts.shape
    tm, tc = _tile2(Rh, C, 16)

    def body(c_ref, a_ref, b_ref, o_ref):
        o_ref[...] = (a_ref[0].astype(F32) + b_ref[...].astype(F32)).astype(BF16)

    spec = pl.BlockSpec((1, tm, tc), lambda p, i, j, c_ref: (p, i, j))
    return pl.pallas_call(
        body, name=name, out_shape=jax.ShapeDtypeStruct((P4, Rh, C), BF16),
        grid_spec=pltpu.PrefetchScalarGridSpec(
            num_scalar_prefetch=1, grid=(P4, Rh // tm, C // tc),
            in_specs=[pl.BlockSpec((1, 1, tm, tc), lambda p, i, j, c_ref: (p, c_ref[0], i, j)), spec], out_specs=spec),
        compiler_params=_params(("parallel",) * 3),
    )(core, parts, sib)


def _sum_slabs(pre, recv, chip, name):
    _, Rh, C = pre.shape
    tm, tc = _tile2(Rh, C, 16)

    def body(me_ref, own_ref, r_ref, o_ref):
        acc = own_ref[0].astype(F32)
        for j in range(3):
            acc = acc + r_ref[j].astype(F32)
        o_ref[...] = acc

    return pl.pallas_call(
        body, name=name, out_shape=jax.ShapeDtypeStruct((Rh, C), F32),
        grid_spec=pltpu.PrefetchScalarGridSpec(
            num_scalar_prefetch=1, grid=(Rh // tm, C // tc),
            in_specs=[pl.BlockSpec((1, tm, tc), lambda i, j, me_ref: (me_ref[0], i, j)),
                      pl.BlockSpec((3, tm, tc), lambda i, j, me_ref: (0, i, j))],
            out_specs=pl.BlockSpec((tm, tc), lambda i, j, me_ref: (i, j))),
        compiler_params=_params(("parallel", "parallel")),
    )(chip, pre, recv)


def kernel(x, c, positions, w_ada, b_ada, w_in, g_q_a, w_q_b, g_kv_a, w_kv_b, w_o_a, w_conv, w_o_b, w_o, ln1_g, ln1_b, w_ffn_in, w_ffn_out, ln2_g, ln2_b, loss_target, m_w_ada, m_b_ada, m_w_in, m_g_q_a, m_w_q_b, m_g_kv_a, m_w_kv_b, m_w_o_a, m_w_conv, m_w_o_b, m_w_o, m_ln1_g, m_ln1_b, m_w_ffn_in, m_w_ffn_out, m_ln2_g, m_ln2_b, v_w_ada, v_b_ada, v_w_in, v_g_q_a, v_w_q_b, v_g_kv_a, v_w_kv_b, v_w_o_a, v_w_conv, v_w_o_b, v_w_o, v_ln1_g, v_ln1_b, v_w_ffn_in, v_w_ffn_out, v_ln2_g, v_ln2_b):
    S, D = x.shape[1], x.shape[2]
    F = w_ffn_out.shape[1] * 4
    ax, ay, ac = _place()
    chip = 2 * ax + ay
    dev = 4 * ax + 2 * ay + ac
    x2, tgt = x[0], loss_target[0]
    w_ada2, w_in2, w_q_b2, w_kv_b2 = w_ada[0], w_in[0], w_q_b[0], w_kv_b[0]
    w_o_a2, w_o_b2, w_o2, w_ffn_in2, w_ffn_out2 = w_o_a[0], w_o_b[0], w_o[0], w_ffn_in[0], w_ffn_out[0]
    NA = w_ada2.shape[1]
    CW = w_conv.shape[2]

    inv_freq = 1.0 / (ROPE_THETA ** (jnp.arange(0, QK_ROPE, 2, dtype=F32) / QK_ROPE))
    ang = positions[0].astype(F32)[:, None] * inv_freq
    cos, sin = jnp.cos(ang), jnp.sin(ang)
    z32, z64, z96 = jnp.zeros((S, 32), F32), jnp.zeros((S, 64), F32), jnp.zeros((S, 96), F32)
    tab = jnp.concatenate([cos, cos, z64, -sin, z96, z32, sin, z64], axis=1)

    def halves(a):
        return a.reshape(2, a.shape[0] // 2, a.shape[1])

    def whole(g):
        return g.reshape(4, 2 * g.shape[2], g.shape[3])

    def cols(g):
        return jnp.transpose(g, (1, 0, 2)).reshape(g.shape[1], 4 * g.shape[2])

    w_inT, m_w_inT, v_w_inT = w_in2.T, m_w_in[0].T, v_w_in[0].T
    CS = w_inT.shape[0]
    CSP = -(-CS // 32) * 32
    sh_in = halves(jnp.pad(w_inT.astype(BF16), ((0, CSP - CS), (0, 0))))
    sh_qb, sh_kvb, sh_oa, sh_ob, sh_o, sh_fi, sh_fo = (
        halves(w.astype(BF16)) for w in (w_q_b2, w_kv_b2, w_o_a2, w_o_b2, w_o2, w_ffn_in2, w_ffn_out2))
    c_all = _all_gather8(c, "gather_c").reshape(8, D)
    wconv_all = _all_gather8(w_conv[0], "gather_wconv")
    w_conv_full = jnp.transpose(wconv_all[0::2], (1, 0, 2)).reshape(3, D)
    b_sh = lax.dynamic_slice(b_ada, (0, chip * NA), (1, NA))
    mod_sh = _ada_fwd(c_all, w_ada2, b_sh)
    mod_all = _all_gather8(mod_sh, "gather_mod")
    mod = lax.dynamic_slice(mod_all[0::2], (0, dev, 0), (4, 1, NA)).reshape(6, D)
    shift1, scale1, gate1, shift2, scale2, gate2 = (mod[k:k + 1] for k in range(6))

    g_in, shift1, w_conv_full = _run_plan(_gather_plan([sh_in]), "gather_first", ride=[shift1, w_conv_full])
    g_in = whole(g_in)
    sh_a1, sh_a2 = [sh_qb, sh_kvb], [sh_oa, sh_ob, sh_o]
    cl_a1, cl_a2, cl_fi, cl_fo = (_gather_copies(g) for g in (sh_a1, sh_a2, [sh_fi], [sh_fo]))
    st_a1 = _split_start(sh_a1, cl_a1, shift1, "gather_a1_start")
    st_a2 = _split_start(sh_a2, cl_a2, st_a1[4], "gather_a2_start")
    shift1 = st_a2[4][0]

    def in_rows(lo, hi):
        parts = [g_in[p, max(lo, p * CS) - p * CS:min(hi, (p + 1) * CS) - p * CS]
                 for p in range(4) if max(lo, p * CS) < min(hi, (p + 1) * CS)]
        return parts[0] if len(parts) == 1 else jnp.concatenate(parts, axis=0)

    n_qkv = Q_LORA + KV_LORA + QK_ROPE
    W_qkvT = jnp.pad(in_rows(0, n_qkv), ((0, QKV_A - n_qkv), (0, 0)))
    W_convT = in_rows(n_qkv, n_qkv + 3 * D)
    W_gateT = in_rows(n_qkv + 3 * D, n_qkv + 5 * D)

    u = _modulate(x2, scale1, shift1, "modulate1")
    pq = _matmul(u, W_qkvT, "nt", F32, "proj_qkv")
    pc = _matmul(u, W_convT, "nt", BF16, "proj_conv")
    sh_a1, la1 = _split_wait(st_a1, cl_a1, [pc], "gather_a1_wait")
    pg, (g_qb, g_kvb) = _matmul(u, W_gateT, "nt", BF16, "proj_gate", carry=_gather_plan(sh_a1, into=la1, ici=False))
    st_fi = _split_start([sh_fi], cl_fi, g_q_a, "gather_fi_start", after=[pg])
    W_qb = jnp.pad(cols(whole(g_qb)).reshape(Q_LORA, N_HEADS, QK_NOPE + QK_ROPE),
                   ((0, 0), (0, 0), (0, QK_PAD - QK_NOPE - QK_ROPE))).reshape(Q_LORA, N_HEADS * QK_PAD)
    W_kvb = cols(whole(g_kvb))
    rq, rkv, kr = _rms_fwd(pq, tab, st_fi[4][0], g_kv_a)
    kv = _matmul(rkv, W_kvb, "nn", BF16, "kv_b")
    sh_a2, la2 = _split_wait(st_a2, cl_a2, [kv], "gather_a2_wait")
    qf, (g_oa, g_ob, g_o) = _matmul(rq, W_qb, "nn", F32, "q_b", carry=_gather_plan(sh_a2, into=la2, ici=False))
    q = _q_rope(qf, tab)
    o, lse, _ = _attn_fwd(q, kv, kr)
    W_oa, W_ob, W_o = (g.reshape(-1, D) for g in (g_oa, g_ob, g_o))
    hb = _conv_fwd(pc, w_conv_full)
    sh_fi_t, lfi = _split_wait(st_fi, cl_fi, [o], "gather_fi_wait")
    y_b, g_fi = _matmul(hb, W_ob, "nn", BF16, "o_b", carry=_gather_plan(sh_fi_t, (0, 2), into=lfi, ici=False))
    y_a, (g_fi,) = _matmul(o, W_oa, "nn", BF16, "o_a", carry=_gather_plan(sh_fi_t, (1, 2), into=g_fi, ici=False))
    st_fo = _split_start([sh_fo], cl_fo, ln1_g, "gather_fo_start", after=[y_b])
    merged = _merge_fwd(y_a, y_b, pg)
    mix = _matmul(merged, W_o, "nn", F32, "w_o")
    W_fi = whole(g_fi)
    x1, u2 = _ln1_fwd(x2, mix, gate1, st_fo[4][0], ln1_b, scale2, shift2)
    hh = _matmul(u2, W_fi, "nn", BF16, "ffn_in", shards="b")
    sh_fo_t, lfo = _split_wait(st_fo, cl_fo, [hh], "gather_fo_wait")
    act, (g_fo,) = _swiglu_fwd(hh, carry=_gather_plan(sh_fo_t, into=lfo, ici=False))
    W_fo = g_fo.reshape(F, D)
    ffn = _matmul(act, W_fo, "nn", F32, "ffn_out")

    core_i = ac.astype(jnp.int32).reshape(1)
    chip_i = chip.astype(jnp.int32).reshape(1)

    def uncols(g):
        return jnp.transpose(g.reshape(g.shape[0], 4, g.shape[1] // 4), (1, 0, 2))

    def slabs(p):
        return p.reshape(4, 2, p.shape[1] // 2, p.shape[2])

    def add_pairs(parts, sibs, nms):
        return [_add_pair(a, b, core_i, "add_pair_" + nm) for a, b, nm in zip(parts, sibs, nms)]

    def sum_all(pre, recv, nms):
        return [_sum_slabs(a, r, chip_i, "sum_slabs_" + nm) for a, r, nm in zip(pre, recv, nms)]

    dffn, dx1a, loss_acc, d_ln2_g, d_ln2_b, d_gate2 = _ln2_loss_bwd(x1, ffn, gate2, ln2_g, ln2_b, tgt)
    loss = lax.psum(loss_acc[0, 0], ("x", "y", "c"))
    dW_fo = _matmul(act, dffn, "tn", BF16, "d_w_ffn_out")
    p_fo = [slabs(dW_fo.reshape(4, -1, D))]
    dact, s_fo = _matmul(dffn, W_fo, "nt", BF16, "d_act", carry=_pair_plan(p_fo))
    pre_fo = add_pairs(p_fo, s_fo, ["w_ffn_out"])
    cs_fo = _scatter_copies(pre_fo)
    st_sfo = _split_start(pre_fo, cs_fo, scale2, "scatter_fo_start")
    dhh = _swiglu_bwd(dact, hh)
    dW_fi = _matmul(u2, dhh, "tn", BF16, "d_w_ffn_in", shards="o")
    p_fi = [slabs(dW_fi)]
    du2, s_fi = _matmul(dhh, W_fi, "nt", F32, "d_u2", carry=_pair_plan(p_fi), shards="b")
    pre_fi = add_pairs(p_fi, s_fi, ["w_ffn_in"])
    cs_fi = _scatter_copies(pre_fi)
    st_sfi = _split_start(pre_fi, cs_fi, st_sfo[4], "scatter_fi_start")
    dmix, dxa, d_shift2, d_scale2, d_ln1_g, d_ln1_b, d_gate1 = _ln1_bwd(x2, mix, dx1a, du2, gate1, ln1_g, ln1_b, st_sfi[4][0])
    dW_o = _matmul(merged, dmix, "tn", BF16, "d_w_o")
    dmerged = _matmul(dmix, W_o, "nt", BF16, "d_merged")
    dy_a, dy_b, dgate = _merge_bwd(dmerged, y_a, y_b, pg)
    dW_oa = _matmul(o, dy_a, "tn", BF16, "d_w_o_a")
    do = _matmul(dy_a, W_oa, "nt", BF16, "d_o")
    dW_ob = _matmul(hb, dy_b, "tn", BF16, "d_w_o_b")
    p_mid = [slabs(g.reshape(4, -1, D)) for g in (dW_oa, dW_ob, dW_o)]
    dhb, s_mid = _matmul(dy_b, W_ob, "nt", BF16, "d_hb", carry=_pair_plan(p_mid))
    pre_mid = add_pairs(p_mid, s_mid, ["w_o_a", "w_o_b", "w_o"])
    cs_mid = _scatter_copies(pre_mid)
    st_smid = _split_start(pre_mid, cs_mid, w_conv_full, "scatter_mid_start")
    dconv, d_wconv = _conv_bwd(dhb, pc, st_smid[4][0])
    dq, dkv, dkr, _ = _attn_bwd(q, kv, kr, do, o, lse, tab, carry=_token_plan(st_smid[4][0]))
    names_a = ["w_ffn_out", "w_ffn_in", "w_o_a", "w_o_b", "w_o"]
    dW_qb = _matmul(rq, dq, "tn", BF16, "d_w_q_b")
    d_rq = _matmul(dq, W_qb, "nt", F32, "d_rq")
    dW_kvb = _matmul(rkv, dkv, "tn", BF16, "d_w_kv_b")
    d_rkv = _matmul(dkv, W_kvb, "nt", F32, "d_rkv")
    dqkv, d_g_q, d_g_kv = _rms_bwd(d_rq, d_rkv, pq, dkr, g_q_a, g_kv_a)
    dW_qkvT = _matmul(dqkv, u, "tn", BF16, "d_w_qkv")
    dW_convT = _matmul(dconv, u, "tn", BF16, "d_w_conv")
    dW_gateT = _matmul(dgate, u, "tn", BF16, "d_w_gate")
    pre_fo, r_fo = _split_wait(st_sfo, cs_fo, [dW_qkvT], "scatter_fo_wait")
    pre_fi, r_fi = _split_wait(st_sfi, cs_fi, [dW_qkvT], "scatter_fi_wait")
    pre_mid, r_mid = _split_wait(st_smid, cs_mid, [dW_qkvT], "scatter_mid_wait")
    fin_a = sum_all(pre_fo + pre_fi + pre_mid, r_fo + r_fi + r_mid, names_a)
    srcs = [(0, dW_qkvT[:n_qkv]), (n_qkv, dW_convT), (n_qkv + 3 * D, dW_gateT)]
    rows_of = []
    for p in range(4):
        for lo, src in srcs:
            a, b = max(lo, p * CS), min(lo + src.shape[0], (p + 1) * CS)
            if a < b:
                rows_of.append(src[a - lo:b - lo])
        rows_of.append(jnp.zeros((CSP - CS, D), BF16))
    dW_inT = jnp.concatenate(rows_of, axis=0).reshape(4, CSP, D)
    dW_qb_u = dW_qb.reshape(Q_LORA, N_HEADS, QK_PAD)[:, :, :QK_NOPE + QK_ROPE].reshape(Q_LORA, -1)
    names_b = ["w_in", "w_q_b", "w_kv_b"]
    p_b = [slabs(dW_inT), slabs(uncols(dW_qb_u)), slabs(uncols(dW_kvb))]
    du, s_b = _matmul(dqkv, W_qkvT, "nn", F32, "d_u_qkv", carry=_pair_plan(p_b))
    pre_b = add_pairs(p_b, s_b, names_b)
    cs_b = _scatter_copies(pre_b)
    st_b = _split_start(pre_b, cs_b, scale1, "scatter_last_start")
    du, fs_a = _matmul(dconv, W_convT, "nn", F32, "d_u_conv", add=du, carry=_sibling_plan(fin_a))
    du = _matmul(dgate, W_gateT, "nn", F32, "d_u_gate", add=du)
    grad_x, d_shift1, d_scale1 = _dx_final(dxa, du, x2, st_b[4][0])

    big = {}
    ws = dict(w_in=(w_inT, m_w_inT, v_w_inT), w_q_b=(w_q_b2, m_w_q_b[0], v_w_q_b[0]),
              w_kv_b=(w_kv_b2, m_w_kv_b[0], v_w_kv_b[0]), w_o_a=(w_o_a2, m_w_o_a[0], v_w_o_a[0]),
              w_o_b=(w_o_b2, m_w_o_b[0], v_w_o_b[0]), w_o=(w_o2, m_w_o[0], v_w_o[0]),
              w_ffn_in=(w_ffn_in2, m_w_ffn_in[0], v_w_ffn_in[0]), w_ffn_out=(w_ffn_out2, m_w_ffn_out[0], v_w_ffn_out[0]))

    def adam_of(nm, a, b, carry=None):
        w_, m_, v_ = ws[nm]
        return _adam_halves("adam_" + nm, w_, m_, v_, a, b, core_i, carry)

    for nm, a, b in zip(names_a, fin_a, fs_a):
        big[nm] = adam_of(nm, a, b, _token_plan(st_b[4][0]))[0]
    done = [big[nm][1] for nm in names_a] + [grad_x]
    pre_b, r_b = _split_wait(st_b, cs_b, done, "scatter_last_wait")
    fin_b = sum_all(pre_b, r_b, names_b)
    fs_b = _run_plan(_sibling_plan(fin_b), "sibling_last")
    for nm, a, b in zip(names_b, fin_b, fs_b):
        big[nm] = adam_of(nm, a, b)

    def pad_d(v):
        return jnp.pad(v, ((0, 0), (0, D - v.shape[1])))

    small = _pack_rows([d_ln1_g, d_ln1_b, d_ln2_g, d_ln2_b, pad_d(d_g_q), pad_d(d_g_kv), d_wconv,
                         d_shift1, d_scale1, d_gate1, d_shift2, d_scale2, d_gate2], 16, after=[pre_b[1]])
    small_all = _all_gather8(small, "gather_small")
    small_sum = _sum8(small_all)
    g_ln1_g, g_ln1_b, g_ln2_g, g_ln2_b = (small_sum[k:k + 1] for k in range(4))
    g_g_q, g_g_kv = small_sum[4:5, :Q_LORA], small_sum[5:6, :KV_LORA]
    g_wconv = lax.dynamic_slice(small_sum[6:9], (0, chip * CW), (3, CW))
    g_b_ada = small_sum[9:15].reshape(1, 6 * D)
    dmod_all = small_all[:, 9:15, :].reshape(8, 6 * D)
    g_w_ada = _ada_bwd(c_all, lax.dynamic_slice(dmod_all, (0, chip * NA), (8, NA)))
    big["w_ada"] = [g_w_ada] + list(_adam("adam_w_ada", w_ada2, m_w_ada[0], v_w_ada[0], g_w_ada))
    sm = {}
    for nm, w_, m_, v_, g_ in [("b_ada", b_ada, m_b_ada, v_b_ada, g_b_ada), ("g_q_a", g_q_a, m_g_q_a, v_g_q_a, g_g_q),
                               ("g_kv_a", g_kv_a, m_g_kv_a, v_g_kv_a, g_g_kv),
                               ("w_conv", w_conv[0], m_w_conv[0], v_w_conv[0], g_wconv),
                               ("ln1_g", ln1_g, m_ln1_g, v_ln1_g, g_ln1_g), ("ln1_b", ln1_b, m_ln1_b, v_ln1_b, g_ln1_b),
                               ("ln2_g", ln2_g, m_ln2_g, v_ln2_g, g_ln2_g), ("ln2_b", ln2_b, m_ln2_b, v_ln2_b, g_ln2_b)]:
        sm[nm] = (g_,) + tuple(_adam_small("adam_" + nm, w_, m_, v_, g_))

    order = ["w_ada", "b_ada", "w_in", "g_q_a", "w_q_b", "g_kv_a", "w_kv_b", "w_o_a", "w_conv", "w_o_b", "w_o",
             "ln1_g", "ln1_b", "w_ffn_in", "w_ffn_out", "ln2_g", "ln2_b"]
    lead = {"b_ada", "g_q_a", "g_kv_a", "ln1_g", "ln1_b", "ln2_g", "ln2_b"}

    def leaf(nm, k):
        val = big[nm][k] if nm in big else sm[nm][k]
        if nm == "w_in":
            val = val.T
        return val if nm in lead else val[None]

    outs = [loss, grad_x[None]]
    for k in range(4):
        outs += [leaf(nm, k) for nm in order]
    return tuple(outs)
```

```python
import functools

import jax
import jax.numpy as jnp
from jax import lax
from jax.experimental import pallas as pl
from jax.experimental.pallas import tpu as pltpu

F32, BF16 = jnp.float32, jnp.bfloat16
N_HEADS, QK_NOPE, QK_ROPE, V_HEAD = 16, 128, 64, 128
Q_LORA, KV_LORA = 512, 512
QK_PAD = 256
QKV_A = 1152
CHUNK_SHIFT = 6
ATTN_SCALE = (QK_NOPE + QK_ROPE) ** -0.5
ROPE_THETA = 10000.0
ALPHA = 2.0 ** 0.25
LN_EPS, RMS_EPS = 1e-5, 1e-6
ADAM_LR, ADAM_B1, ADAM_B2, ADAM_EPS, ADAM_WD, ADAM_STEP = 0.001, 0.9, 0.999, 1e-08, 0.01, 10
ADAM_C1 = 1.0 - ADAM_B1 ** ADAM_STEP
ADAM_C2 = 1.0 - ADAM_B2 ** ADAM_STEP
VMEM_LIMIT = 56 * 1024 * 1024
MESH = pl.DeviceIdType.MESH
ANY = pl.BlockSpec(memory_space=pl.ANY)
HBM_SPEC = pl.BlockSpec(memory_space=pltpu.HBM)
SEM_SPEC = pl.BlockSpec(memory_space=pltpu.SEMAPHORE)
NT = (((1,), (1,)), ((), ()))
TN = (((0,), (0,)), ((), ()))
NN = (((1,), (0,)), ((), ()))


def _params(sem=None):
    return pltpu.CompilerParams(dimension_semantics=sem, vmem_limit_bytes=VMEM_LIMIT)


def _pick(n, cands=(1408, 1024, 512, 384, 256, 128)):
    for t in cands:
        if n % t == 0:
            return t
    return n


def _row_tile(rows, row_bytes, budget, mult=8):
    best = mult
    for t in range(mult, rows + 1, mult):
        if rows % t == 0 and t * row_bytes <= budget:
            best = t
    return best


def _tile2(rows, cols, mult=8, budget=3 << 18):
    col_tiles = [t for t in range(128, cols + 1, 128) if cols % t == 0] or [cols]
    best = None
    for tc in col_tiles:
        for tr in range(mult, rows + 1, mult):
            if rows % tr == 0 and tr * tc <= budget and (best is None or (tr * tc, tc) > (best[0] * best[1], best[1])):
                best = (tr, tc)
    assert best is not None, (rows, cols)
    return best


def _sigmoid(x):
    return jax.nn.sigmoid(x)


class _Plan:
    def __init__(self, ins, outs, sems, start, finish, aliases=None):
        self.ins, self.outs, self.sems, self.start, self.finish = list(ins), list(outs), list(sems), start, finish
        self.aliases = dict(aliases or {})

    def io_aliases(self, first_in, first_out):
        return {first_in + i: first_out + o for i, o in self.aliases.items()}


def _token_plan(token):
    return _Plan([token], [], [], lambda *a: None, lambda *a: None)


def _run_plan(plan, name, ride=None):
    n_in, n_out = len(plan.ins), len(plan.outs)
    extra = [] if ride is None else list(ride)
    aliases = plan.io_aliases(0, 0)
    for k in range(len(extra)):
        aliases[n_in + k] = n_out + k

    def body(*refs):
        ins, outs, sems = refs[:n_in], refs[n_in + len(extra):n_in + len(extra) + n_out], refs[n_in + 2 * len(extra) + n_out:]
        plan.start(ins, outs, sems)
        plan.finish(ins, outs, sems)

    return pl.pallas_call(body, name=name, out_shape=plan.outs + [jax.ShapeDtypeStruct(r.shape, r.dtype) for r in extra],
                          in_specs=[ANY] * (n_in + len(extra)), out_specs=[ANY] * (n_out + len(extra)),
                          scratch_shapes=plan.sems, input_output_aliases=aliases,
                          compiler_params=_params())(*plan.ins, *extra)


def _matmul(a, b, mode, out_dtype, name, add=None, carry=None, shards=None):
    if mode == "nn":
        (M, K), N, dims = a.shape, b.shape[-1] * (4 if shards else 1), NN
    elif mode == "nt":
        (M, K), N, dims = a.shape, b.shape[-2], NT
    else:
        (K, M), N, dims = a.shape, b.shape[1], TN
    split_n = shards and mode != "nt"
    tm = _pick(M)
    tn = _pick(N // 4) if split_n else _pick(N)
    deep = (2816, 2048, 1408, 1024, 512, 384, 256, 128)
    if shards and mode == "nt":
        tk = _pick(K // 4, deep)
    else:
        tk = K if K <= 2048 else _pick(K, deep)
    nk = K // tk
    per = (N // 4 // tn) if split_n else (K // 4 // tk if shards else 1)
    a_spec = (pl.BlockSpec((tk, tm), lambda i, j, k: (k, i)) if mode == "tn"
              else pl.BlockSpec((tm, tk), lambda i, j, k: (i, k)))
    if shards == "b" and mode == "nn":
        b_spec = pl.BlockSpec((None, tk, tn), lambda i, j, k: (j // per, k, j % per))
    elif shards == "b":
        b_spec = pl.BlockSpec((None, tn, tk), lambda i, j, k: (k // per, j, k % per))
    else:
        b_spec = (pl.BlockSpec((tn, tk), lambda i, j, k: (j, k)) if mode == "nt"
                  else pl.BlockSpec((tk, tn), lambda i, j, k: (k, j)))
    o_spec = pl.BlockSpec((tm, tn), lambda i, j, k: (i, j))
    o_shape = (M, N)
    if shards == "o":
        o_spec, o_shape = pl.BlockSpec((None, tm, tn), lambda i, j, k: (j // per, i, j % per)), (4, M, N // 4)
    has_add = add is not None
    n_ci = len(carry.ins) if carry else 0
    n_co = len(carry.outs) if carry else 0
    n_in = 2 + has_add
    grid = (M // tm, N // tn, nk)

    def body(*refs):
        a_ref, b_ref = refs[0], refs[1]
        add_ref = refs[2] if has_add else None
        o_ref = refs[n_in + n_ci]
        acc_ref = refs[n_in + n_ci + 1 + n_co] if nk > 1 else None
        c_ins = refs[n_in:n_in + n_ci]
        c_outs = refs[n_in + n_ci + 1:n_in + n_ci + 1 + n_co]
        c_sems = refs[n_in + n_ci + 1 + n_co + (nk > 1):]
        i, j, k = pl.program_id(0), pl.program_id(1), pl.program_id(2)

        if carry:
            @pl.when((i == 0) & (j == 0) & (k == 0))
            def _():
                carry.start(c_ins, c_outs, c_sems)

        part = lax.dot_general(a_ref[...], b_ref[...], dims, preferred_element_type=F32)
        if nk == 1:
            o_ref[...] = (part + add_ref[...] if has_add else part).astype(o_ref.dtype)
        else:
            @pl.when(k == 0)
            def _():
                acc_ref[...] = part

            @pl.when((k > 0) & (k < nk - 1))
            def _():
                acc_ref[...] += part

            @pl.when(k == nk - 1)
            def _():
                r = acc_ref[...] + part
                if has_add:
                    r = r + add_ref[...]
                o_ref[...] = r.astype(o_ref.dtype)

        if carry:
            @pl.when((i == grid[0] - 1) & (j == grid[1] - 1) & (k == nk - 1))
            def _():
                carry.finish(c_ins, c_outs, c_sems)

    ins = [a, b] + ([add] if has_add else []) + (carry.ins if carry else [])
    in_specs = [a_spec, b_spec] + ([o_spec] if has_add else []) + [ANY] * n_ci
    res = pl.pallas_call(
        body, name=name, grid=grid,
        in_specs=in_specs, out_specs=[o_spec] + [ANY] * n_co,
        out_shape=[jax.ShapeDtypeStruct(o_shape, out_dtype)] + (carry.outs if carry else []),
        scratch_shapes=([pltpu.VMEM((tm, tn), F32)] if nk > 1 else []) + (carry.sems if carry else []),
        input_output_aliases=carry.io_aliases(n_in, 1) if carry else {},
        compiler_params=_params(("arbitrary",) * 3 if carry else ("parallel", "parallel", "arbitrary")),
    )(*ins)
    return (res[0], res[1:]) if carry else res[0]


def _rows(body, name, n_rows, tm, ins, outs, accs=(), carry=None):
    grid = (n_rows // tm,)

    def halo(arr):
        return 16 if arr.dtype == BF16 else 8

    arrays, in_specs = [], []
    for spec in ins:
        kind, arr = spec[0], spec[1]
        arrays.append(arr)
        if kind == "row":
            _, _, cb, w = spec
            in_specs.append(pl.BlockSpec((tm, w), lambda i, cb=cb: (i, cb)))
        elif kind == "full":
            in_specs.append(pl.BlockSpec(arr.shape, lambda i, nd=arr.ndim: (0,) * nd))
        elif kind == "prev":
            _, _, cb, w = spec
            h = halo(arr)
            in_specs.append(pl.BlockSpec((h, w), lambda i, cb=cb, per=tm // h: (jnp.maximum(i * per - 1, 0), cb)))
        else:
            _, _, cb, w = spec
            h = halo(arr)
            in_specs.append(pl.BlockSpec((h, w), lambda i, cb=cb, per=tm // h, last=n_rows // h - 1:
                                         (jnp.minimum((i + 1) * per, last), cb)))
    out_shape = [jax.ShapeDtypeStruct((n_rows, w), dt) for (w, dt) in outs]
    out_specs = [pl.BlockSpec((tm, w), lambda i: (i, 0)) for (w, _) in outs]
    out_shape += [jax.ShapeDtypeStruct(s, F32) for s in accs]
    out_specs += [pl.BlockSpec(s, lambda i, nd=len(s): (0,) * nd) for s in accs]
    n_in, n_out, n_acc = len(ins), len(outs), len(accs)
    n_ci = len(carry.ins) if carry else 0
    n_co = len(carry.outs) if carry else 0

    def kernel_body(*refs):
        first = n_in + n_ci
        c_ins, c_outs, c_sems = refs[n_in:first], refs[first + n_out + n_acc:first + n_out + n_acc + n_co], refs[first + n_out + n_acc + n_co:]
        if carry:
            @pl.when(pl.program_id(0) == 0)
            def _():
                carry.start(c_ins, c_outs, c_sems)

        body(pl.program_id(0), refs[:n_in], refs[first:first + n_out], refs[first + n_out:first + n_out + n_acc])
        if carry:
            @pl.when(pl.program_id(0) == grid[0] - 1)
            def _():
                carry.finish(c_ins, c_outs, c_sems)

    res = pl.pallas_call(
        kernel_body, name=name, grid=grid, in_specs=in_specs + [ANY] * n_ci, out_specs=out_specs + [ANY] * n_co,
        out_shape=out_shape + (carry.outs if carry else []), scratch_shapes=carry.sems if carry else [],
        input_output_aliases=carry.io_aliases(n_in, n_out + n_acc) if carry else {},
        compiler_params=_params(("arbitrary",)),
    )(*arrays, *(carry.ins if carry else []))
    return (res[:n_out + n_acc], res[n_out + n_acc:]) if carry else res


def _acc_add(i, ref, val):
    @pl.when(i == 0)
    def _():
        ref[...] = val

    @pl.when(i > 0)
    def _():
        ref[...] += val


def _rope(t, tab, sign):
    c, sa, sb = tab[:, 0:128], tab[:, 128:256], tab[:, 256:384]
    rot = pltpu.roll(t, 96, 1) * sa + pltpu.roll(t, 32, 1) * sb
    return t * c + rot if sign > 0 else t * c - rot


def _ln_stats(r):
    mu = jnp.mean(r, axis=-1, keepdims=True)
    d = r - mu
    var = jnp.mean(d * d, axis=-1, keepdims=True)
    rstd = lax.rsqrt(var + LN_EPS)
    return d * rstd, rstd


def _ln_bwd(dxh, xh, rstd):
    m1 = jnp.mean(dxh, axis=-1, keepdims=True)
    m2 = jnp.mean(dxh * xh, axis=-1, keepdims=True)
    return rstd * (dxh - m1 - xh * m2)


def _modulate(x, scale, shift, name):
    S, D = x.shape

    def body(i, ins, outs, accs):
        outs[0][...] = (ins[0][...] * (1.0 + ins[1][...]) + ins[2][...]).astype(BF16)

    return _rows(body, name, S, _pick(S, (256, 128)), [("row", x, 0, D), ("full", scale), ("full", shift)], [(D, BF16)])[0]


def _rms_fwd(pq, tab, g_q, g_kv):
    S = pq.shape[0]

    def body(i, ins, outs, accs):
        pq_ref, tab_ref, gq_ref, gkv_ref = ins

        def rms(x, g):
            return x * lax.rsqrt(jnp.mean(x * x, axis=-1, keepdims=True) + RMS_EPS) * g

        outs[0][...] = rms(pq_ref[:, 0:Q_LORA], gq_ref[...]).astype(BF16)
        outs[1][...] = rms(pq_ref[:, Q_LORA:Q_LORA + KV_LORA], gkv_ref[...]).astype(BF16)
        outs[2][...] = _rope(pq_ref[:, Q_LORA + KV_LORA:QKV_A], tab_ref[...], 1).astype(BF16)

    return _rows(body, "rms_fwd", S, _pick(S, (256, 128)),
                 [("row", pq, 0, QKV_A), ("row", tab, 0, 384), ("full", g_q), ("full", g_kv)],
                 [(Q_LORA, BF16), (KV_LORA, BF16), (128, BF16)])


def _q_rope(q, tab):
    S, W = q.shape

    def body(i, ins, outs, accs):
        q_ref, tab_ref = ins
        t = tab_ref[...]
        for h in range(N_HEADS):
            lo = h * QK_PAD
            outs[0][:, lo:lo + 128] = q_ref[:, lo:lo + 128].astype(BF16)
            outs[0][:, lo + 128:lo + 256] = _rope(q_ref[:, lo + 128:lo + 256], t, 1).astype(BF16)

    return _rows(body, "q_rope", S, _pick(S, (256, 128)), [("row", q, 0, W), ("row", tab, 0, 384)], [(W, BF16)])[0]


def _allowed(q0, k0, bq):
    row = q0 + lax.broadcasted_iota(jnp.int32, (bq, bq), 0)
    col = k0 + lax.broadcasted_iota(jnp.int32, (bq, bq), 1)
    return (col >> CHUNK_SHIFT) <= (row >> CHUNK_SHIFT)


ATTN_BLOCK = 512


def _attn_fwd(q, kv, kr, carry=None):
    S = q.shape[0]
    bq = min(ATTN_BLOCK, S)
    nq = S // bq
    n_ci = len(carry.ins) if carry else 0
    n_co = len(carry.outs) if carry else 0

    def body(*refs):
        q_ref, kn_ref, v_ref, kr_ref = refs[:4]
        o_ref, lse_ref = refs[4 + n_ci:6 + n_ci]
        c_ins, c_outs = refs[4:4 + n_ci], refs[6 + n_ci:6 + n_ci + n_co]
        kcat = refs[6 + n_ci + n_co]
        c_sems = refs[7 + n_ci + n_co:]
        qi = pl.program_id(1)
        if carry:
            @pl.when((pl.program_id(0) == 0) & (qi == 0))
            def _():
                carry.start(c_ins, c_outs, c_sems)

        @pl.when(qi == 0)
        def _():
            kcat[:, 0:128] = kn_ref[...]
            kcat[:, 128:256] = kr_ref[...]

        qv = q_ref[...]

        def step(j, carry, masked):
            m, l, acc = carry
            off = pl.multiple_of(j * bq, bq)
            s = lax.dot_general(qv, kcat[pl.ds(off, bq), :], NT, preferred_element_type=F32) * ATTN_SCALE
            if masked:
                s = jnp.where(_allowed(qi * bq, off, bq), s, -1e30)
            m_new = jnp.maximum(m, jnp.max(s, axis=1, keepdims=True))
            a = jnp.exp(m - m_new)
            p = jnp.exp(s - m_new)
            l = a * l + jnp.sum(p, axis=1, keepdims=True)
            acc = a * acc + jnp.dot(p.astype(BF16), v_ref[pl.ds(off, bq), :], preferred_element_type=F32)
            return m_new, l, acc

        init = (jnp.full((bq, 1), -1e30, F32), jnp.zeros((bq, 1), F32), jnp.zeros((bq, V_HEAD), F32))
        below = lax.fori_loop(0, qi, lambda j, cr: step(j, cr, False), init)
        m, l, acc = step(qi, below, True)
        o_ref[...] = (acc / l).astype(BF16)
        lse_ref[0] = m + jnp.log(l)
        if carry:
            @pl.when((pl.program_id(0) == N_HEADS - 1) & (qi == nq - 1))
            def _():
                carry.finish(c_ins, c_outs, c_sems)

    res = pl.pallas_call(
        body, name="attn_fwd", grid=(N_HEADS, nq),
        in_specs=[pl.BlockSpec((bq, QK_PAD), lambda h, i: (i, h)),
                  pl.BlockSpec((S, 128), lambda h, i: (0, 2 * h)),
                  pl.BlockSpec((S, 128), lambda h, i: (0, 2 * h + 1)),
                  pl.BlockSpec((S, 128), lambda h, i: (0, 0))] + [ANY] * n_ci,
        out_specs=[pl.BlockSpec((bq, V_HEAD), lambda h, i: (i, h)),
                   pl.BlockSpec((1, bq, 1), lambda h, i: (h, i, 0))] + [ANY] * n_co,
        out_shape=[jax.ShapeDtypeStruct((S, N_HEADS * V_HEAD), BF16),
                   jax.ShapeDtypeStruct((N_HEADS, S, 1), F32)] + (carry.outs if carry else []),
        scratch_shapes=[pltpu.VMEM((S, QK_PAD), BF16)] + (carry.sems if carry else []),
        input_output_aliases=carry.io_aliases(4, 2) if carry else {},
        compiler_params=_params(("arbitrary", "arbitrary")),
    )(q, kv, kv, kr, *(carry.ins if carry else []))
    return res[0], res[1], res[2:]


def _attn_bwd(q, kv, kr, do, o, lse, tab, carry=None):
    S = q.shape[0]
    bq = min(ATTN_BLOCK, S)
    nq = S // bq

    n_ci = len(carry.ins) if carry else 0
    n_co = len(carry.outs) if carry else 0

    def body(*refs):
        q_ref, kn_ref, v_ref, kr_ref, do_ref, o_ref, lse_ref, tab_ref = refs[:8]
        dq_ref, dkv_ref, dkr_ref = refs[8 + n_ci:11 + n_ci]
        dq_acc, dk_acc, dv_acc, kcat, delta = refs[11 + n_ci + n_co:16 + n_ci + n_co]
        c_ins, c_outs, c_sems = refs[8:8 + n_ci], refs[11 + n_ci:11 + n_ci + n_co], refs[16 + n_ci + n_co:]
        h = pl.program_id(0)
        if carry:
            @pl.when(h == 0)
            def _():
                carry.start(c_ins, c_outs, c_sems)

        dq_acc[...] = jnp.zeros_like(dq_acc)
        dk_acc[...] = jnp.zeros_like(dk_acc)
        dv_acc[...] = jnp.zeros_like(dv_acc)
        kcat[:, 0:128] = kn_ref[...]
        kcat[:, 128:256] = kr_ref[...]
        for r in range(nq):
            rows = slice(r * bq, (r + 1) * bq)
            delta[rows, :] = jnp.sum(do_ref[rows, :].astype(F32) * o_ref[rows, :].astype(F32), axis=1, keepdims=True)

        def pair(i, j, masked):
            rows_i = pl.ds(pl.multiple_of(i * bq, bq), bq)
            rows_j = pl.ds(pl.multiple_of(j * bq, bq), bq)
            qv, dov, k = q_ref[rows_i, :], do_ref[rows_i, :], kcat[rows_j, :]
            s = lax.dot_general(qv, k, NT, preferred_element_type=F32) * ATTN_SCALE
            if masked:
                s = jnp.where(_allowed(i * bq, j * bq, bq), s, -1e30)
            p = jnp.exp(s - lse_ref[0, rows_i, :])
            dv_acc[rows_j, :] += lax.dot_general(p.astype(BF16), dov, TN, preferred_element_type=F32)
            dp = lax.dot_general(dov, v_ref[rows_j, :], NT, preferred_element_type=F32)
            ds = (p * (dp - delta[rows_i, :]) * ATTN_SCALE).astype(BF16)
            dk_acc[rows_j, :] += lax.dot_general(ds, qv, TN, preferred_element_type=F32)
            dq_acc[rows_i, :] += jnp.dot(ds, k, preferred_element_type=F32)

        def kv_step(j, _):
            pair(j, j, True)

            def q_step(i, _):
                pair(i, j, False)
                return 0

            lax.fori_loop(j + 1, nq, q_step, 0)
            return 0

        lax.fori_loop(0, nq, kv_step, 0)

        for r in range(nq):
            rows = slice(r * bq, (r + 1) * bq)
            dq_ref[rows, 0:128] = dq_acc[rows, 0:128].astype(BF16)
            dq_ref[rows, 128:256] = _rope(dq_acc[rows, 128:256], tab_ref[rows, :], -1).astype(BF16)
        dkv_ref[:, 0:128] = dk_acc[:, 0:128].astype(BF16)
        dkv_ref[:, 128:256] = dv_acc[...].astype(BF16)

        @pl.when(h == 0)
        def _():
            dkr_ref[...] = dk_acc[:, 128:256]

        @pl.when(h > 0)
        def _():
            dkr_ref[...] += dk_acc[:, 128:256]

        @pl.when(h == N_HEADS - 1)
        def _():
            for r in range(nq):
                rows = slice(r * bq, (r + 1) * bq)
                dkr_ref[rows, :] = _rope(dkr_ref[rows, :], tab_ref[rows, :], -1)
            if carry:
                carry.finish(c_ins, c_outs, c_sems)

    W = N_HEADS * QK_PAD
    res = pl.pallas_call(
        body, name="attn_bwd", grid=(N_HEADS,),
        in_specs=[pl.BlockSpec((S, QK_PAD), lambda h: (0, h)),
                  pl.BlockSpec((S, 128), lambda h: (0, 2 * h)),
                  pl.BlockSpec((S, 128), lambda h: (0, 2 * h + 1)),
                  pl.BlockSpec((S, 128), lambda h: (0, 0)),
                  pl.BlockSpec((S, V_HEAD), lambda h: (0, h)),
                  pl.BlockSpec((S, V_HEAD), lambda h: (0, h)),
                  pl.BlockSpec((1, S, 1), lambda h: (h, 0, 0)),
                  pl.BlockSpec((S, 384), lambda h: (0, 0))] + [ANY] * n_ci,
        out_specs=[pl.BlockSpec((S, QK_PAD), lambda h: (0, h)),
                   pl.BlockSpec((S, QK_PAD), lambda h: (0, h)),
                   pl.BlockSpec((S, 128), lambda h: (0, 0))] + [ANY] * n_co,
        out_shape=[jax.ShapeDtypeStruct((S, W), BF16), jax.ShapeDtypeStruct((S, W), BF16),
                   jax.ShapeDtypeStruct((S, 128), F32)] + (carry.outs if carry else []),
        scratch_shapes=[pltpu.VMEM((S, QK_PAD), F32), pltpu.VMEM((S, QK_PAD), F32), pltpu.VMEM((S, V_HEAD), F32),
                        pltpu.VMEM((S, QK_PAD), BF16), pltpu.VMEM((S, 1), F32)]
        + (carry.sems if carry else []),
        input_output_aliases=carry.io_aliases(8, 3) if carry else {},
        compiler_params=_params(("arbitrary",)),
    )(q, kv, kv, kr, do, o, lse, tab, *(carry.ins if carry else []))
    return res[0], res[1], res[2], res[3:]


def _shift_down(cur, prev, i, n):
    tm, h = cur.shape[0], prev.shape[0]
    prev = jnp.where(i == 0, jnp.zeros_like(prev), prev)
    full = jnp.concatenate([prev, cur], axis=0)
    return pltpu.roll(full, n, 0)[h:h + tm, :]


def _shift_up(cur, nxt, i, last, n):
    tm, h = cur.shape[0], nxt.shape[0]
    nxt = jnp.where(i == last, jnp.zeros_like(nxt), nxt)
    full = jnp.concatenate([cur, nxt], axis=0)
    return pltpu.roll(full, tm + h - n, 0)[0:tm, :]


def _conv_fwd(pc, w_conv):
    S, D = pc.shape[0], pc.shape[1] // 3
    tm = _pick(S, (256, 128))

    def body(i, ins, outs, accs):
        b_ref, c_ref, x_ref, cp_ref, xp_ref, w_ref = ins
        z = c_ref[...].astype(F32) * x_ref[...].astype(F32)
        zp = cp_ref[...].astype(F32) * xp_ref[...].astype(F32)
        cz = w_ref[0:1, :] * _shift_down(z, zp, i, 2) + w_ref[1:2, :] * _shift_down(z, zp, i, 1) + w_ref[2:3, :] * z
        outs[0][...] = (b_ref[...].astype(F32) * cz).astype(BF16)

    return _rows(body, "conv_fwd", S, tm,
                 [("row", pc, 0, D), ("row", pc, 1, D), ("row", pc, 2, D), ("prev", pc, 1, D), ("prev", pc, 2, D),
                  ("full", w_conv)], [(D, BF16)])[0]


def _conv_bwd(dhb, pc, w_conv):
    S, D = dhb.shape
    tm = _pick(S, (256, 128))
    last = S // tm - 1

    def body(i, ins, outs, accs):
        g_ref, b_ref, c_ref, x_ref, cp_ref, xp_ref, gn_ref, bn_ref, w_ref = ins
        w0, w1, w2 = w_ref[0:1, :], w_ref[1:2, :], w_ref[2:3, :]
        c, x, g = c_ref[...].astype(F32), x_ref[...].astype(F32), g_ref[...].astype(F32)
        z = c * x
        zp = cp_ref[...].astype(F32) * xp_ref[...].astype(F32)
        z1, z2 = _shift_down(z, zp, i, 1), _shift_down(z, zp, i, 2)
        cz = w0 * z2 + w1 * z1 + w2 * z
        dcz = g * b_ref[...].astype(F32)
        dczn = gn_ref[...].astype(F32) * bn_ref[...].astype(F32)
        dz = w2 * dcz + w1 * _shift_up(dcz, dczn, i, last, 1) + w0 * _shift_up(dcz, dczn, i, last, 2)
        outs[0][:, 0:D] = (g * cz).astype(BF16)
        outs[0][:, D:2 * D] = (dz * x).astype(BF16)
        outs[0][:, 2 * D:3 * D] = (dz * c).astype(BF16)
        dw = jnp.concatenate([jnp.sum(dcz * z2, axis=0, keepdims=True), jnp.sum(dcz * z1, axis=0, keepdims=True),
                              jnp.sum(dcz * z, axis=0, keepdims=True)], axis=0)
        _acc_add(i, accs[0], dw)

    return _rows(body, "conv_bwd", S, tm,
                 [("row", dhb, 0, D), ("row", pc, 0, D), ("row", pc, 1, D), ("row", pc, 2, D),
                  ("prev", pc, 1, D), ("prev", pc, 2, D), ("next", dhb, 0, D), ("next", pc, 0, D), ("full", w_conv)],
                 [(3 * D, BF16)], [(3, D)])


def _merge_fwd(y_a, y_b, pg):
    S, D = y_a.shape

    def body(i, ins, outs, accs):
        ya, yb, ga, gb = ins
        outs[0][...] = (_sigmoid(ga[...].astype(F32)) * ya[...].astype(F32)
                        + _sigmoid(gb[...].astype(F32)) * yb[...].astype(F32)).astype(BF16)

    return _rows(body, "merge_fwd", S, _pick(S, (256, 128)),
                 [("row", y_a, 0, D), ("row", y_b, 0, D), ("row", pg, 0, D), ("row", pg, 1, D)], [(D, BF16)])[0]


def _merge_bwd(dm, y_a, y_b, pg):
    S, D = dm.shape

    def body(i, ins, outs, accs):
        d, ya, yb = ins[0][...].astype(F32), ins[1][...].astype(F32), ins[2][...].astype(F32)
        sa, sb = _sigmoid(ins[3][...].astype(F32)), _sigmoid(ins[4][...].astype(F32))
        outs[0][...] = (d * sa).astype(BF16)
        outs[1][...] = (d * sb).astype(BF16)
        outs[2][:, 0:D] = (d * ya * (sa * (1.0 - sa))).astype(BF16)
        outs[2][:, D:2 * D] = (d * yb * (sb * (1.0 - sb))).astype(BF16)

    return _rows(body, "merge_bwd", S, _pick(S, (256, 128)),
                 [("row", dm, 0, D), ("row", y_a, 0, D), ("row", y_b, 0, D), ("row", pg, 0, D), ("row", pg, 1, D)],
                 [(D, BF16), (D, BF16), (2 * D, BF16)])


def _ln1_fwd(x, mix, gate1, g, b, scale2, shift2):
    S, D = x.shape

    def body(i, ins, outs, accs):
        x_ref, mix_ref, gate_ref, g_ref, b_ref, sc_ref, sh_ref = ins
        xh, _ = _ln_stats(ALPHA * x_ref[...] + gate_ref[...] * mix_ref[...])
        x1 = xh * g_ref[...] + b_ref[...]
        outs[0][...] = x1
        outs[1][...] = (x1 * (1.0 + sc_ref[...]) + sh_ref[...]).astype(BF16)

    return _rows(body, "ln1_fwd", S, _pick(S, (256, 128)),
                 [("row", x, 0, D), ("row", mix, 0, D), ("full", gate1), ("full", g), ("full", b),
                  ("full", scale2), ("full", shift2)], [(D, F32), (D, BF16)])


def _swiglu_fwd(hh, carry=None):
    S, F = hh.shape[0], hh.shape[1] // 2

    def body(i, ins, outs, accs):
        hg = ins[0][...].astype(F32)
        outs[0][...] = (hg * _sigmoid(hg) * ins[1][...].astype(F32)).astype(BF16)

    res = _rows(body, "swiglu_fwd", S, _pick(S, (128,)), [("row", hh, 0, F), ("row", hh, 1, F)], [(F, BF16)], carry=carry)
    return (res[0][0], res[1]) if carry else res[0]


def _swiglu_bwd(dact, hh):
    S, F = dact.shape

    def body(i, ins, outs, accs):
        d, hg, hu = ins[0][...].astype(F32), ins[1][...].astype(F32), ins[2][...].astype(F32)
        sg = _sigmoid(hg)
        outs[0][:, 0:F] = (d * hu * (sg * (1.0 + hg * (1.0 - sg)))).astype(BF16)
        outs[0][:, F:2 * F] = (d * (hg * sg)).astype(BF16)

    return _rows(body, "swiglu_bwd", S, _pick(S, (128,)),
                 [("row", dact, 0, F), ("row", hh, 0, F), ("row", hh, 1, F)], [(2 * F, BF16)])[0]


def _ln2_loss_bwd(x1, ffn, gate2, g, b, target):
    S, D = x1.shape

    def body(i, ins, outs, accs):
        x1_ref, f_ref, gate_ref, g_ref, b_ref, t_ref = ins
        f = f_ref[...]
        xh, rstd = _ln_stats(ALPHA * x1_ref[...] + gate_ref[...] * f)
        e = xh * g_ref[...] + b_ref[...] - t_ref[...]
        dy = e * (1.0 / D)
        dr = _ln_bwd(dy * g_ref[...], xh, rstd)
        outs[0][...] = (gate_ref[...] * dr).astype(BF16)
        outs[1][...] = ALPHA * dr
        _acc_add(i, accs[0], jnp.full((1, 128), (0.5 / D) * jnp.sum(e * e), F32))
        _acc_add(i, accs[1], jnp.sum(dy * xh, axis=0, keepdims=True))
        _acc_add(i, accs[2], jnp.sum(dy, axis=0, keepdims=True))
        _acc_add(i, accs[3], jnp.sum(dr * f, axis=0, keepdims=True))

    return _rows(body, "ln2_loss_bwd", S, _pick(S, (256, 128)),
                 [("row", x1, 0, D), ("row", ffn, 0, D), ("full", gate2), ("full", g), ("full", b), ("row", target, 0, D)],
                 [(D, BF16), (D, F32)], [(1, 128), (1, D), (1, D), (1, D)])


def _ln1_bwd(x, mix, dx1a, du2, gate1, g, b, scale2):
    S, D = x.shape

    def body(i, ins, outs, accs):
        x_ref, mix_ref, da_ref, du_ref, gate_ref, g_ref, b_ref, sc_ref = ins
        mix, du = mix_ref[...], du_ref[...]
        xh, rstd = _ln_stats(ALPHA * x_ref[...] + gate_ref[...] * mix)
        x1 = xh * g_ref[...] + b_ref[...]
        dx1 = da_ref[...] + du * (1.0 + sc_ref[...])
        dr = _ln_bwd(dx1 * g_ref[...], xh, rstd)
        outs[0][...] = (gate_ref[...] * dr).astype(BF16)
        outs[1][...] = ALPHA * dr
        _acc_add(i, accs[0], jnp.sum(du, axis=0, keepdims=True))
        _acc_add(i, accs[1], jnp.sum(du * x1, axis=0, keepdims=True))
        _acc_add(i, accs[2], jnp.sum(dx1 * xh, axis=0, keepdims=True))
        _acc_add(i, accs[3], jnp.sum(dx1, axis=0, keepdims=True))
        _acc_add(i, accs[4], jnp.sum(dr * mix, axis=0, keepdims=True))

    return _rows(body, "ln1_bwd", S, _pick(S, (256, 128)),
                 [("row", x, 0, D), ("row", mix, 0, D), ("row", dx1a, 0, D), ("row", du2, 0, D),
                  ("full", gate1), ("full", g), ("full", b), ("full", scale2)],
                 [(D, BF16), (D, F32)], [(1, D)] * 5)


def _rms_bwd(d_rq, d_rkv, pq, dkr, g_q, g_kv):
    S = pq.shape[0]

    def body(i, ins, outs, accs):
        dq_ref, dkv_ref, pq_ref, dkr_ref, gq_ref, gkv_ref = ins

        def rms_bwd(dy, x, g):
            r = lax.rsqrt(jnp.mean(x * x, axis=-1, keepdims=True) + RMS_EPS)
            dyg = dy * g
            dx = r * dyg - x * (r * r * r) * jnp.mean(dyg * x, axis=-1, keepdims=True)
            return dx, jnp.sum(dy * (x * r), axis=0, keepdims=True)

        dxq, dgq = rms_bwd(dq_ref[...], pq_ref[:, 0:Q_LORA], gq_ref[...])
        dxkv, dgkv = rms_bwd(dkv_ref[...], pq_ref[:, Q_LORA:Q_LORA + KV_LORA], gkv_ref[...])
        outs[0][:, 0:Q_LORA] = dxq.astype(BF16)
        outs[0][:, Q_LORA:Q_LORA + KV_LORA] = dxkv.astype(BF16)
        outs[0][:, Q_LORA + KV_LORA:QKV_A] = dkr_ref[...].astype(BF16)
        _acc_add(i, accs[0], dgq)
        _acc_add(i, accs[1], dgkv)

    return _rows(body, "rms_bwd", S, _pick(S, (256, 128)),
                 [("row", d_rq, 0, Q_LORA), ("row", d_rkv, 0, KV_LORA), ("row", pq, 0, QKV_A), ("row", dkr, 0, 128),
                  ("full", g_q), ("full", g_kv)], [(QKV_A, BF16)], [(1, Q_LORA), (1, KV_LORA)])


def _dx_final(dxa, du, x, scale1):
    S, D = x.shape

    def body(i, ins, outs, accs):
        du = ins[1][...]
        outs[0][...] = ins[0][...] + du * (1.0 + ins[3][...])
        _acc_add(i, accs[0], jnp.sum(du, axis=0, keepdims=True))
        _acc_add(i, accs[1], jnp.sum(du * ins[2][...], axis=0, keepdims=True))

    return _rows(body, "dx_final", S, _pick(S, (256, 128)),
                 [("row", dxa, 0, D), ("row", du, 0, D), ("row", x, 0, D), ("full", scale1)],
                 [(D, F32)], [(1, D), (1, D)])


def _ada_fwd(c_all, w, bias):
    B, D = c_all.shape
    NA = w.shape[1]
    tn = _pick(NA, (512, 256, 128))

    def body(c_ref, w_ref, b_ref, o_ref):
        cv = c_ref[...]
        ca = (cv * _sigmoid(cv)).astype(BF16)
        o_ref[...] = jnp.dot(ca, w_ref[...].astype(BF16), preferred_element_type=F32) + b_ref[...]

    return pl.pallas_call(
        body, name="ada_fwd", grid=(NA // tn,),
        in_specs=[pl.BlockSpec((B, D), lambda j: (0, 0)), pl.BlockSpec((D, tn), lambda j: (0, j)),
                  pl.BlockSpec((1, tn), lambda j: (0, j))],
        out_specs=pl.BlockSpec((B, tn), lambda j: (0, j)),
        out_shape=jax.ShapeDtypeStruct((B, NA), F32),
        compiler_params=_params(("arbitrary",)),
    )(c_all, w, bias)


def _ada_bwd(c_all, dmod):
    B, D = c_all.shape
    NA = dmod.shape[1]
    tn = _pick(NA, (512, 256, 128))

    def body(c_ref, d_ref, o_ref):
        cv = c_ref[...]
        ca = (cv * _sigmoid(cv)).astype(BF16)
        o_ref[...] = lax.dot_general(ca, d_ref[...].astype(BF16), TN, preferred_element_type=F32)

    return pl.pallas_call(
        body, name="ada_bwd", grid=(NA // tn,),
        in_specs=[pl.BlockSpec((B, D), lambda j: (0, 0)), pl.BlockSpec((B, tn), lambda j: (0, j))],
        out_specs=pl.BlockSpec((D, tn), lambda j: (0, j)),
        out_shape=jax.ShapeDtypeStruct((D, NA), F32),
        compiler_params=_params(("arbitrary",)),
    )(c_all, dmod)


def _pack_rows(parts, n_rows, after=()):
    N = parts[0].shape[1]
    n = len(parts)

    def body(*refs):
        o_ref = refs[-1]
        o_ref[...] = jnp.zeros_like(o_ref)
        at = 0
        for r in refs[:n]:
            o_ref[at:at + r.shape[0], :] = r[...]
            at += r.shape[0]

    vmem = pl.BlockSpec(memory_space=pltpu.VMEM)
    return pl.pallas_call(body, name="pack_small", out_shape=jax.ShapeDtypeStruct((n_rows, N), F32),
                          in_specs=[vmem] * n + [ANY] * len(after), out_specs=vmem,
                          compiler_params=_params())(*parts, *after)


def _sum8(parts):
    _, R, N = parts.shape

    def body(p_ref, o_ref):
        acc = p_ref[0]
        for d in range(1, 8):
            acc = acc + p_ref[d]
        o_ref[...] = acc

    return pl.pallas_call(body, name="sum8", out_shape=jax.ShapeDtypeStruct((R, N), F32),
                          compiler_params=_params())(parts)


def _adam_math(w, g, m, v):
    m = ADAM_B1 * m + (1.0 - ADAM_B1) * g
    v = ADAM_B2 * v + (1.0 - ADAM_B2) * (g * g)
    delta = -ADAM_LR * ((m / ADAM_C1) / (jnp.sqrt(v / ADAM_C2) + ADAM_EPS) + ADAM_WD * w)
    return delta, m, v


def _adam(name, w, m, v, g, carry=None):
    R, C = w.shape
    tm = _row_tile(R, C * 4, 1 << 20)
    steps = R // tm
    n_ci = len(carry.ins) if carry else 0
    n_co = len(carry.outs) if carry else 0

    def body(*refs):
        w_ref, m_ref, v_ref, g_ref = refs[:4]
        d_ref, nm_ref, nv_ref = refs[4 + n_ci:7 + n_ci]
        c_ins, c_outs, c_sems = refs[4:4 + n_ci], refs[7 + n_ci:7 + n_ci + n_co], refs[7 + n_ci + n_co:]
        if carry:
            @pl.when(pl.program_id(0) == 0)
            def _():
                carry.start(c_ins, c_outs, c_sems)

        delta, nm, nv = _adam_math(w_ref[...], g_ref[...], m_ref[...], v_ref[...])
        d_ref[...] = delta
        nm_ref[...] = nm
        nv_ref[...] = nv
        if carry:
            @pl.when(pl.program_id(0) == steps - 1)
            def _():
                carry.finish(c_ins, c_outs, c_sems)

    spec = pl.BlockSpec((tm, C), lambda i: (i, 0))
    res = pl.pallas_call(
        body, name=name, grid=(steps,), in_specs=[spec] * 4 + [ANY] * n_ci, out_specs=[spec] * 3 + [ANY] * n_co,
        out_shape=[jax.ShapeDtypeStruct((R, C), F32)] * 3 + (carry.outs if carry else []),
        scratch_shapes=carry.sems if carry else [],
        input_output_aliases=carry.io_aliases(4, 3) if carry else {},
        compiler_params=_params(("arbitrary",)),
    )(w, m, v, g, *(carry.ins if carry else []))
    return (res[:3], res[3:]) if carry else res


def _adam_halves(name, w, m, v, mine, other, core, carry=None):
    R, C = w.shape
    Rh = mine.shape[0]
    tc = max(t for t in range(128, C + 1, 128) if C % t == 0 and R * t <= (3 << 17))
    steps = C // tc
    n_ci = len(carry.ins) if carry else 0
    n_co = len(carry.outs) if carry else 0

    def body(*refs):
        c_ref, w_ref, m_ref, v_ref, a_ref, b_ref = refs[:6]
        g_ref, d_ref, nm_ref, nv_ref = refs[6 + n_ci:10 + n_ci]
        c_ins, c_outs, c_sems = refs[6:6 + n_ci], refs[10 + n_ci:10 + n_ci + n_co], refs[10 + n_ci + n_co:]
        if carry:
            @pl.when(pl.program_id(0) == 0)
            def _():
                carry.start(c_ins, c_outs, c_sems)

        first = c_ref[0] == 0
        g = jnp.concatenate([jnp.where(first, a_ref[...], b_ref[...]),
                             jnp.where(first, b_ref[0:R - Rh, :], a_ref[0:R - Rh, :])], axis=0)
        delta, nm, nv = _adam_math(w_ref[...], g, m_ref[...], v_ref[...])
        g_ref[...] = g
        d_ref[...] = delta
        nm_ref[...] = nm
        nv_ref[...] = nv
        if carry:
            @pl.when(pl.program_id(0) == steps - 1)
            def _():
                carry.finish(c_ins, c_outs, c_sems)

    spec = pl.BlockSpec((R, tc), lambda i, c_ref: (0, i))
    h_spec = pl.BlockSpec((Rh, tc), lambda i, c_ref: (0, i))
    res = pl.pallas_call(
        body, name=name, out_shape=[jax.ShapeDtypeStruct((R, C), F32)] * 4 + (carry.outs if carry else []),
        grid_spec=pltpu.PrefetchScalarGridSpec(
            num_scalar_prefetch=1, grid=(steps,), in_specs=[spec, spec, spec, h_spec, h_spec] + [ANY] * n_ci,
            out_specs=[spec] * 4 + [ANY] * n_co, scratch_shapes=carry.sems if carry else []),
        input_output_aliases=carry.io_aliases(6, 4) if carry else {},
        compiler_params=_params(("arbitrary",)),
    )(core, w, m, v, mine, other, *(carry.ins if carry else []))
    return (res[:4], res[4:]) if carry else res


def _adam_small(name, w, m, v, g):
    def body(w_ref, m_ref, v_ref, g_ref, d_ref, nm_ref, nv_ref):
        delta, nm, nv = _adam_math(w_ref[...], g_ref[...], m_ref[...], v_ref[...])
        d_ref[...] = delta
        nm_ref[...] = nm
        nv_ref[...] = nv

    return pl.pallas_call(body, name=name, out_shape=[jax.ShapeDtypeStruct(w.shape, F32)] * 3,
                          compiler_params=_params())(w, m, v, g)


def _place():
    return lax.axis_index("x"), lax.axis_index("y"), lax.axis_index("c")


def _other_chips(x, y):
    return [(1 - x, y), (x, 1 - y), (1 - x, 1 - y)]


def _all_gather8(blk, name):
    R, N = blk.shape

    def body(x_ref, out_ref, send_sems, recv_sems, local_sem):
        x, y, c = _place()
        me = 4 * x + 2 * y + c
        mine = pltpu.make_async_copy(x_ref, out_ref.at[me], local_sem)
        mine.start()
        flips = [(j >> 2 & 1, j >> 1 & 1, j & 1) for j in range(1, 8)]
        peers = [((1 - x) if fx else x, (1 - y) if fy else y, (1 - c) if fc else c) for fx, fy, fc in flips]
        sends = []
        for j, peer in enumerate(peers):
            cp = pltpu.make_async_remote_copy(src_ref=x_ref, dst_ref=out_ref.at[me], send_sem=send_sems.at[j],
                                              recv_sem=recv_sems.at[j], device_id=peer, device_id_type=MESH)
            cp.start()
            sends.append(cp)
        for j, (px, py, pc) in enumerate(peers):
            pltpu.make_async_remote_copy(src_ref=x_ref, dst_ref=out_ref.at[4 * px + 2 * py + pc],
                                         send_sem=send_sems.at[j], recv_sem=recv_sems.at[j],
                                         device_id=(px, py, pc), device_id_type=MESH).wait_recv()
        for cp in sends:
            cp.wait_send()
        mine.wait()

    return pl.pallas_call(
        body, name=name, out_shape=jax.ShapeDtypeStruct((8, R, N), F32),
        in_specs=[pl.BlockSpec(memory_space=pltpu.VMEM)], out_specs=pl.BlockSpec(memory_space=pltpu.VMEM),
        scratch_shapes=[pltpu.SemaphoreType.DMA((7,)), pltpu.SemaphoreType.DMA((7,)), pltpu.SemaphoreType.DMA],
        compiler_params=_params(),
    )(blk)


def _piece(rows, piece):
    i, n, k = piece if len(piece) == 3 else (piece[0], piece[1], 1)
    assert rows % 16 == 0 and rows // 16 >= n, (rows, piece)
    lo, hi = (rows // 16 * i // n) * 16, (rows // 16 * (i + k) // n) * 16
    return pl.ds(lo, hi - lo)


def _scatter_plan(arrs, piece=(0, 1), into=None):
    n = len(arrs)

    def copies(ins, outs, sems):
        send_sems, recv_sems = sems
        x, y, c = _place()
        chips = _other_chips(x, y)
        cps = []
        for k in range(n):
            rows = _piece(arrs[k].shape[1], piece)
            for j, (px, py) in enumerate(chips):
                cps.append(pltpu.make_async_remote_copy(
                    src_ref=ins[k].at[2 * px + py, rows], dst_ref=outs[k].at[j, rows],
                    send_sem=send_sems.at[3 * k + j], recv_sem=recv_sems.at[3 * k + j],
                    device_id=(px, py, c), device_id_type=MESH))
        return cps

    def start(ins, outs, sems):
        for cp in copies(ins, outs, sems):
            cp.start()

    def finish(ins, outs, sems):
        for cp in copies(ins, outs, sems):
            cp.wait()

    return _Plan(list(arrs) + list(into or []), [jax.ShapeDtypeStruct((3,) + a.shape[1:], a.dtype) for a in arrs],
                 [pltpu.SemaphoreType.DMA((3 * n,))] * 2, start, finish,
                 aliases={n + k: k for k in range(n)} if into else None)


def _gather_plan(shards, piece=(0, 1), into=None, ici=True):
    n = len(shards)

    def parts(ins, outs, sems):
        s1, r1, s2, r2, loc = sems
        x, y, c = _place()
        me = 2 * x + y
        chips = _other_chips(x, y)
        sib = (x, y, 1 - c)

        def rows(k):
            return _piece(shards[k].shape[1], piece)

        def ici_copy(k, j, slab, to):
            return pltpu.make_async_remote_copy(src_ref=ins[k].at[c, rows(k)], dst_ref=outs[k].at[slab, c, rows(k)],
                                                send_sem=s1.at[3 * k + j], recv_sem=r1.at[3 * k + j],
                                                device_id=to, device_id_type=MESH)

        def d2d(k, j, slab, half):
            return pltpu.make_async_remote_copy(src_ref=outs[k].at[slab, half, rows(k)],
                                                dst_ref=outs[k].at[slab, half, rows(k)],
                                                send_sem=s2.at[3 * k + j], recv_sem=r2.at[3 * k + j],
                                                device_id=sib, device_id_type=MESH)

        def own(k):
            return pltpu.make_async_remote_copy(src_ref=ins[k].at[:, rows(k)], dst_ref=outs[k].at[me, :, rows(k)],
                                                send_sem=loc.at[2 * k], recv_sem=loc.at[2 * k + 1],
                                                device_id=sib, device_id_type=MESH)

        return c, me, chips, ici_copy, d2d, own

    def start(ins, outs, sems):
        c, me, chips, ici_copy, d2d, own = parts(ins, outs, sems)
        for k in range(n):
            for j, (px, py) in enumerate(chips):
                (ici_copy(k, j, me, (px, py, c)) if ici else d2d(k, j, 2 * px + py, c)).start()
        for k in range(n):
            own(k).start()

    def finish(ins, outs, sems):
        c, me, chips, ici_copy, d2d, own = parts(ins, outs, sems)
        if ici:
            for k in range(n):
                for j, (px, py) in enumerate(chips):
                    ici_copy(k, j, 2 * px + py, (px, py, c)).wait_recv()
                    d2d(k, j, 2 * px + py, c).start()
        for k in range(n):
            for j, (px, py) in enumerate(chips):
                d2d(k, j, 2 * px + py, 1 - c).wait_recv()
        for k in range(n):
            own(k).wait()
            for j, (px, py) in enumerate(chips):
                if ici:
                    ici_copy(k, j, me, (px, py, c)).wait_send()
                d2d(k, j, 2 * px + py, c).wait_send()

    return _Plan(list(shards) + list(into or []), [jax.ShapeDtypeStruct((4,) + a.shape, a.dtype) for a in shards],
                 [pltpu.SemaphoreType.DMA((3 * n,))] * 4 + [pltpu.SemaphoreType.DMA((2 * n,))], start, finish,
                 aliases={n + k: k for k in range(n)} if into else None)


def _pair_plan(parts):
    n = len(parts)

    def copies(ins, outs, sems):
        send_sems, recv_sems = sems
        x, y, c = _place()
        return [pltpu.make_async_remote_copy(src_ref=ins[k].at[p, 1 - c], dst_ref=outs[k].at[p],
                                             send_sem=send_sems.at[4 * k + p], recv_sem=recv_sems.at[4 * k + p],
                                             device_id=(x, y, 1 - c), device_id_type=MESH)
                for k in range(n) for p in range(4)]

    def start(ins, outs, sems):
        for cp in copies(ins, outs, sems):
            cp.start()

    def finish(ins, outs, sems):
        for cp in copies(ins, outs, sems):
            cp.wait()

    return _Plan(parts, [jax.ShapeDtypeStruct((4,) + a.shape[2:], a.dtype) for a in parts],
                 [pltpu.SemaphoreType.DMA((4 * n,))] * 2, start, finish)


def _sibling_plan(arrs):
    n = len(arrs)

    def copies(ins, outs, sems):
        send_sems, recv_sems = sems
        x, y, c = _place()
        return [pltpu.make_async_remote_copy(src_ref=ins[k], dst_ref=outs[k], send_sem=send_sems.at[k],
                                             recv_sem=recv_sems.at[k], device_id=(x, y, 1 - c), device_id_type=MESH)
                for k in range(n)]

    def start(ins, outs, sems):
        for cp in copies(ins, outs, sems):
            cp.start()

    def finish(ins, outs, sems):
        for cp in copies(ins, outs, sems):
            cp.wait()

    return _Plan(arrs, [jax.ShapeDtypeStruct(a.shape, a.dtype) for a in arrs],
                 [pltpu.SemaphoreType.DMA((n,))] * 2, start, finish)


def _scatter_copies(arrs):
    def copies(ins, land, send_sems, recv_sems):
        x, y, c = _place()
        return [pltpu.make_async_remote_copy(src_ref=ins[k].at[2 * px + py], dst_ref=land[k].at[j],
                                             send_sem=send_sems.at[3 * k + j], recv_sem=recv_sems.at[3 * k + j],
                                             device_id=(px, py, c), device_id_type=MESH)
                for k in range(len(arrs)) for j, (px, py) in enumerate(_other_chips(x, y))]

    return copies, [lax.empty((3,) + a.shape[1:], a.dtype) for a in arrs]


def _gather_copies(shards):
    def copies(ins, land, send_sems, recv_sems):
        x, y, c = _place()
        return [pltpu.make_async_remote_copy(src_ref=ins[k].at[c], dst_ref=land[k].at[2 * x + y, c],
                                             send_sem=send_sems.at[3 * k + j], recv_sem=recv_sems.at[3 * k + j],
                                             device_id=(px, py, c), device_id_type=MESH)
                for k in range(len(shards)) for j, (px, py) in enumerate(_other_chips(x, y))]

    return copies, [lax.empty((4,) + a.shape, a.dtype) for a in shards]


def _split_start(arrs, copies_lands, ride, name, after=()):
    copies, lands = copies_lands
    n = len(arrs)
    rides = list(ride) if isinstance(ride, (list, tuple)) else [ride]
    n_thru = 2 * n + len(rides)

    def body(*refs):
        first_out = n_thru + len(after)
        for cp in copies(refs[:n], refs[n:2 * n], refs[first_out], refs[first_out + 1]):
            cp.start()

    hbm = [pltpu.with_memory_space_constraint(a, pltpu.HBM) for a in list(arrs) + lands + rides]
    res = pl.pallas_call(
        body, name=name,
        out_shape=[pltpu.SemaphoreType.DMA((3 * n,)), pltpu.SemaphoreType.DMA((3 * n,))]
        + [pltpu.HBM(a.shape, a.dtype) for a in hbm],
        in_specs=[HBM_SPEC] * n_thru + [ANY] * len(after),
        out_specs=[SEM_SPEC, SEM_SPEC] + [HBM_SPEC] * n_thru,
        input_output_aliases={i: 2 + i for i in range(n_thru)},
        compiler_params=pltpu.CompilerParams(has_side_effects=pltpu.SideEffectType.DATAFLOW_SIDE_EFFECTING),
    )(*hbm, *after)
    return res[0], res[1], res[2:2 + n], res[2 + n:2 + 2 * n], list(res[2 + 2 * n:])


def _split_wait(started, copies_lands, after, name):
    send_sems, recv_sems, arrs, lands, _ = started
    copies = copies_lands[0]
    n = len(arrs)

    def body(*refs):
        for cp in copies(refs[:n], refs[n:2 * n], refs[2 * n], refs[2 * n + 1]):
            cp.wait_send()
            cp.wait_recv()

    res = pl.pallas_call(
        body, name=name, out_shape=[pltpu.HBM(a.shape, a.dtype) for a in list(arrs) + list(lands)],
        in_specs=[HBM_SPEC] * (2 * n) + [SEM_SPEC, SEM_SPEC] + [ANY] * len(after), out_specs=[HBM_SPEC] * (2 * n),
        input_output_aliases={i: i for i in range(2 * n)},
        compiler_params=pltpu.CompilerParams(has_side_effects=pltpu.SideEffectType.DATAFLOW_SIDE_EFFECTING),
    )(*arrs, *lands, send_sems, recv_sems, *after)
    return list(res[:n]), list(res[n:])


def _join_plans(plans):
    def split(seq, counts):
        out, at = [], 0
        for cnt in counts:
            out.append(seq[at:at + cnt])
            at += cnt
        return out

    n_i, n_o, n_s = ([len(getattr(p, f)) for p in plans] for f in ("ins", "outs", "sems"))

    def start(ins, outs, sems):
        for p, i, o, s in zip(plans, split(ins, n_i), split(outs, n_o), split(sems, n_s)):
            p.start(i, o, s)

    def finish(ins, outs, sems):
        for p, i, o, s in zip(plans, split(ins, n_i), split(outs, n_o), split(sems, n_s)):
            p.finish(i, o, s)

    aliases, at_i, at_o = {}, 0, 0
    for p in plans:
        aliases.update(p.io_aliases(at_i, at_o))
        at_i, at_o = at_i + len(p.ins), at_o + len(p.outs)
    return _Plan(sum((p.ins for p in plans), []), sum((p.outs for p in plans), []), sum((p.sems for p in plans), []),
                 start, finish, aliases)


def _add_pair(parts, sib, core, name):
    P4, _, Rh, C = parts.shape
    tm, tc = _tile2(Rh, C, 16)

    def body(c_ref, a_ref, b_ref, o_ref):
        o_ref[...] = (a_ref[0].astype(F32) + b_ref[...].astype(F32)).astype(BF16)

    spec = pl.BlockSpec((1, tm, tc), lambda p, i, j, c_ref: (p, i, j))
    return pl.pallas_call(
        body, name=name, out_shape=jax.ShapeDtypeStruct((P4, Rh, C), BF16),
        grid_spec=pltpu.PrefetchScalarGridSpec(
            num_scalar_prefetch=1, grid=(P4, Rh // tm, C // tc),
            in_specs=[pl.BlockSpec((1, 1, tm, tc), lambda p, i, j, c_ref: (p, c_ref[0], i, j)), spec], out_specs=spec),
        compiler_params=_params(("parallel",) * 3),
    )(core, parts, sib)


def _sum_slabs(pre, recv, chip, name):
    _, Rh, C = pre.shape
    tm, tc = _tile2(Rh, C, 16)

    def body(me_ref, own_ref, r_ref, o_ref):
        acc = own_ref[0].astype(F32)
        for j in range(3):
            acc = acc + r_ref[j].astype(F32)
        o_ref[...] = acc

    return pl.pallas_call(
        body, name=name, out_shape=jax.ShapeDtypeStruct((Rh, C), F32),
        grid_spec=pltpu.PrefetchScalarGridSpec(
            num_scalar_prefetch=1, grid=(Rh // tm, C // tc),
            in_specs=[pl.BlockSpec((1, tm, tc), lambda i, j, me_ref: (me_ref[0], i, j)),
                      pl.BlockSpec((3, tm, tc), lambda i, j, me_ref: (0, i, j))],
            out_specs=pl.BlockSpec((tm, tc), lambda i, j, me_ref: (i, j))),
        compiler_params=_params(("parallel", "parallel")),
    )(chip, pre, recv)


def kernel(x, c, positions, w_ada, b_ada, w_in, g_q_a, w_q_b, g_kv_a, w_kv_b, w_o_a, w_conv, w_o_b, w_o, ln1_g, ln1_b, w_ffn_in, w_ffn_out, ln2_g, ln2_b, loss_target, m_w_ada, m_b_ada, m_w_in, m_g_q_a, m_w_q_b, m_g_kv_a, m_w_kv_b, m_w_o_a, m_w_conv, m_w_o_b, m_w_o, m_ln1_g, m_ln1_b, m_w_ffn_in, m_w_ffn_out, m_ln2_g, m_ln2_b, v_w_ada, v_b_ada, v_w_in, v_g_q_a, v_w_q_b, v_g_kv_a, v_w_kv_b, v_w_o_a, v_w_conv, v_w_o_b, v_w_o, v_ln1_g, v_ln1_b, v_w_ffn_in, v_w_ffn_out, v_ln2_g, v_ln2_b):
    S, D = x.shape[1], x.shape[2]
    F = w_ffn_out.shape[1] * 4
    ax, ay, ac = _place()
    chip = 2 * ax + ay
    dev = 4 * ax + 2 * ay + ac
    x2, tgt = x[0], loss_target[0]
    w_ada2, w_in2, w_q_b2, w_kv_b2 = w_ada[0], w_in[0], w_q_b[0], w_kv_b[0]
    w_o_a2, w_o_b2, w_o2, w_ffn_in2, w_ffn_out2 = w_o_a[0], w_o_b[0], w_o[0], w_ffn_in[0], w_ffn_out[0]
    NA = w_ada2.shape[1]
    CW = w_conv.shape[2]

    inv_freq = 1.0 / (ROPE_THETA ** (jnp.arange(0, QK_ROPE, 2, dtype=F32) / QK_ROPE))
    ang = positions[0].astype(F32)[:, None] * inv_freq
    cos, sin = jnp.cos(ang), jnp.sin(ang)
    z32, z64, z96 = jnp.zeros((S, 32), F32), jnp.zeros((S, 64), F32), jnp.zeros((S, 96), F32)
    tab = jnp.concatenate([cos, cos, z64, -sin, z96, z32, sin, z64], axis=1)

    def halves(a):
        return a.reshape(2, a.shape[0] // 2, a.shape[1])

    def whole(g):
        return g.reshape(4, 2 * g.shape[2], g.shape[3])

    def cols(g):
        return jnp.transpose(g, (1, 0, 2)).reshape(g.shape[1], 4 * g.shape[2])

    w_inT, m_w_inT, v_w_inT = w_in2.T, m_w_in[0].T, v_w_in[0].T
    CS = w_inT.shape[0]
    CSP = -(-CS // 32) * 32
    sh_in = halves(jnp.pad(w_inT.astype(BF16), ((0, CSP - CS), (0, 0))))
    sh_qb, sh_kvb, sh_oa, sh_ob, sh_o, sh_fi, sh_fo = (
        halves(w.astype(BF16)) for w in (w_q_b2, w_kv_b2, w_o_a2, w_o_b2, w_o2, w_ffn_in2, w_ffn_out2))
    c_all = _all_gather8(c, "gather_c").reshape(8, D)
    wconv_all = _all_gather8(w_conv[0], "gather_wconv")
    w_conv_full = jnp.transpose(wconv_all[0::2], (1, 0, 2)).reshape(3, D)
    b_sh = lax.dynamic_slice(b_ada, (0, chip * NA), (1, NA))
    mod_sh = _ada_fwd(c_all, w_ada2, b_sh)
    mod_all = _all_gather8(mod_sh, "gather_mod")
    mod = lax.dynamic_slice(mod_all[0::2], (0, dev, 0), (4, 1, NA)).reshape(6, D)
    shift1, scale1, gate1, shift2, scale2, gate2 = (mod[k:k + 1] for k in range(6))

    g_in, shift1, w_conv_full = _run_plan(_gather_plan([sh_in]), "gather_first", ride=[shift1, w_conv_full])
    g_in = whole(g_in)
    sh_a1, sh_a2 = [sh_qb, sh_kvb], [sh_oa, sh_ob, sh_o]
    cl_a1, cl_a2, cl_fi, cl_fo = (_gather_copies(g) for g in (sh_a1, sh_a2, [sh_fi], [sh_fo]))
    st_a1 = _split_start(sh_a1, cl_a1, shift1, "gather_a1_start")
    st_a2 = _split_start(sh_a2, cl_a2, st_a1[4], "gather_a2_start")
    shift1 = st_a2[4][0]

    def in_rows(lo, hi):
        parts = [g_in[p, max(lo, p * CS) - p * CS:min(hi, (p + 1) * CS) - p * CS]
                 for p in range(4) if max(lo, p * CS) < min(hi, (p + 1) * CS)]
        return parts[0] if len(parts) == 1 else jnp.concatenate(parts, axis=0)

    n_qkv = Q_LORA + KV_LORA + QK_ROPE
    W_qkvT = jnp.pad(in_rows(0, n_qkv), ((0, QKV_A - n_qkv), (0, 0)))
    W_convT = in_rows(n_qkv, n_qkv + 3 * D)
    W_gateT = in_rows(n_qkv + 3 * D, n_qkv + 5 * D)

    u = _modulate(x2, scale1, shift1, "modulate1")
    pq = _matmul(u, W_qkvT, "nt", F32, "proj_qkv")
    pc = _matmul(u, W_convT, "nt", BF16, "proj_conv")
    sh_a1, la1 = _split_wait(st_a1, cl_a1, [pc], "gather_a1_wait")
    pg, (g_qb, g_kvb) = _matmul(u, W_gateT, "nt", BF16, "proj_gate", carry=_gather_plan(sh_a1, into=la1, ici=False))
    st_fi = _split_start([sh_fi], cl_fi, g_q_a, "gather_fi_start", after=[pg])
    W_qb = jnp.pad(cols(whole(g_qb)).reshape(Q_LORA, N_HEADS, QK_NOPE + QK_ROPE),
                   ((0, 0), (0, 0), (0, QK_PAD - QK_NOPE - QK_ROPE))).reshape(Q_LORA, N_HEADS * QK_PAD)
    W_kvb = cols(whole(g_kvb))
    rq, rkv, kr = _rms_fwd(pq, tab, st_fi[4][0], g_kv_a)
    kv = _matmul(rkv, W_kvb, "nn", BF16, "kv_b")
    sh_a2, la2 = _split_wait(st_a2, cl_a2, [kv], "gather_a2_wait")
    qf, (g_oa, g_ob, g_o) = _matmul(rq, W_qb, "nn", F32, "q_b", carry=_gather_plan(sh_a2, into=la2, ici=False))
    q = _q_rope(qf, tab)
    o, lse, _ = _attn_fwd(q, kv, kr)
    W_oa, W_ob, W_o = (g.reshape(-1, D) for g in (g_oa, g_ob, g_o))
    hb = _conv_fwd(pc, w_conv_full)
    sh_fi_t, lfi = _split_wait(st_fi, cl_fi, [o], "gather_fi_wait")
    y_b, g_fi = _matmul(hb, W_ob, "nn", BF16, "o_b", carry=_gather_plan(sh_fi_t, (0, 2), into=lfi, ici=False))
    y_a, (g_fi,) = _matmul(o, W_oa, "nn", BF16, "o_a", carry=_gather_plan(sh_fi_t, (1, 2), into=g_fi, ici=False))
    st_fo = _split_start([sh_fo], cl_fo, ln1_g, "gather_fo_start", after=[y_b])
    merged = _merge_fwd(y_a, y_b, pg)
    mix = _matmul(merged, W_o, "nn", F32, "w_o")
    W_fi = whole(g_fi)
    x1, u2 = _ln1_fwd(x2, mix, gate1, st_fo[4][0], ln1_b, scale2, shift2)
    hh = _matmul(u2, W_fi, "nn", BF16, "ffn_in", shards="b")
    sh_fo_t, lfo = _split_wait(st_fo, cl_fo, [hh], "gather_fo_wait")
    act, (g_fo,) = _swiglu_fwd(hh, carry=_gather_plan(sh_fo_t, into=lfo, ici=False))
    W_fo = g_fo.reshape(F, D)
    ffn = _matmul(act, W_fo, "nn", F32, "ffn_out")

    core_i = ac.astype(jnp.int32).reshape(1)
    chip_i = chip.astype(jnp.int32).reshape(1)

    def uncols(g):
        return jnp.transpose(g.reshape(g.shape[0], 4, g.shape[1] // 4), (1, 0, 2))

    def slabs(p):
        return p.reshape(4, 2, p.shape[1] // 2, p.shape[2])

    def add_pairs(parts, sibs, nms):
        return [_add_pair(a, b, core_i, "add_pair_" + nm) for a, b, nm in zip(parts, sibs, nms)]

    def sum_all(pre, recv, nms):
        return [_sum_slabs(a, r, chip_i, "sum_slabs_" + nm) for a, r, nm in zip(pre, recv, nms)]

    dffn, dx1a, loss_acc, d_ln2_g, d_ln2_b, d_gate2 = _ln2_loss_bwd(x1, ffn, gate2, ln2_g, ln2_b, tgt)
    loss = lax.psum(loss_acc[0, 0], ("x", "y", "c"))
    dW_fo = _matmul(act, dffn, "tn", BF16, "d_w_ffn_out")
    p_fo = [slabs(dW_fo.reshape(4, -1, D))]
    dact, s_fo = _matmul(dffn, W_fo, "nt", BF16, "d_act", carry=_pair_plan(p_fo))
    pre_fo = add_pairs(p_fo, s_fo, ["w_ffn_out"])
    cs_fo = _scatter_copies(pre_fo)
    st_sfo = _split_start(pre_fo, cs_fo, scale2, "scatter_fo_start")
    dhh = _swiglu_bwd(dact, hh)
    dW_fi = _matmul(u2, dhh, "tn", BF16, "d_w_ffn_in", shards="o")
    p_fi = [slabs(dW_fi)]
    du2, s_fi = _matmul(dhh, W_fi, "nt", F32, "d_u2", carry=_pair_plan(p_fi), shards="b")
    pre_fi = add_pairs(p_fi, s_fi, ["w_ffn_in"])
    cs_fi = _scatter_copies(pre_fi)
    st_sfi = _split_start(pre_fi, cs_fi, st_sfo[4], "scatter_fi_start")
    dmix, dxa, d_shift2, d_scale2, d_ln1_g, d_ln1_b, d_gate1 = _ln1_bwd(x2, mix, dx1a, du2, gate1, ln1_g, ln1_b, st_sfi[4][0])
    dW_o = _matmul(merged, dmix, "tn", BF16, "d_w_o")
    dmerged = _matmul(dmix, W_o, "nt", BF16, "d_merged")
    dy_a, dy_b, dgate = _merge_bwd(dmerged, y_a, y_b, pg)
    dW_oa = _matmul(o, dy_a, "tn", BF16, "d_w_o_a")
    do = _matmul(dy_a, W_oa, "nt", BF16, "d_o")
    dW_ob = _matmul(hb, dy_b, "tn", BF16, "d_w_o_b")
    p_mid = [slabs(g.reshape(4, -1, D)) for g in (dW_oa, dW_ob, dW_o)]
    dhb, s_mid = _matmul(dy_b, W_ob, "nt", BF16, "d_hb", carry=_pair_plan(p_mid))
    pre_mid = add_pairs(p_mid, s_mid, ["w_o_a", "w_o_b", "w_o"])
    cs_mid = _scatter_copies(pre_mid)
    st_smid = _split_start(pre_mid, cs_mid, w_conv_full, "scatter_mid_start")
    dconv, d_wconv = _conv_bwd(dhb, pc, st_smid[4][0])
    dq, dkv, dkr, _ = _attn_bwd(q, kv, kr, do, o, lse, tab, carry=_token_plan(st_smid[4][0]))
    names_a = ["w_ffn_out", "w_ffn_in", "w_o_a", "w_o_b", "w_o"]
    dW_qb = _matmul(rq, dq, "tn", BF16, "d_w_q_b")
    d_rq = _matmul(dq, W_qb, "nt", F32, "d_rq")
    dW_kvb = _matmul(rkv, dkv, "tn", BF16, "d_w_kv_b")
    d_rkv = _matmul(dkv, W_kvb, "nt", F32, "d_rkv")
    dqkv, d_g_q, d_g_kv = _rms_bwd(d_rq, d_rkv, pq, dkr, g_q_a, g_kv_a)
    dW_qkvT = _matmul(dqkv, u, "tn", BF16, "d_w_qkv")
    dW_convT = _matmul(dconv, u, "tn", BF16, "d_w_conv")
    dW_gateT = _matmul(dgate, u, "tn", BF16, "d_w_gate")
    pre_fo, r_fo = _split_wait(st_sfo, cs_fo, [dW_qkvT], "scatter_fo_wait")
    pre_fi, r_fi = _split_wait(st_sfi, cs_fi, [dW_qkvT], "scatter_fi_wait")
    pre_mid, r_mid = _split_wait(st_smid, cs_mid, [dW_qkvT], "scatter_mid_wait")
    fin_a = sum_all(pre_fo + pre_fi + pre_mid, r_fo + r_fi + r_mid, names_a)
    srcs = [(0, dW_qkvT[:n_qkv]), (n_qkv, dW_convT), (n_qkv + 3 * D, dW_gateT)]
    rows_of = []
    for p in range(4):
        for lo, src in srcs:
            a, b = max(lo, p * CS), min(lo + src.shape[0], (p + 1) * CS)
            if a < b:
                rows_of.append(src[a - lo:b - lo])
        rows_of.append(jnp.zeros((CSP - CS, D), BF16))
    dW_inT = jnp.concatenate(rows_of, axis=0).reshape(4, CSP, D)
    dW_qb_u = dW_qb.reshape(Q_LORA, N_HEADS, QK_PAD)[:, :, :QK_NOPE + QK_ROPE].reshape(Q_LORA, -1)
    names_b = ["w_in", "w_q_b", "w_kv_b"]
    p_b = [slabs(dW_inT), slabs(uncols(dW_qb_u)), slabs(uncols(dW_kvb))]
    du, s_b = _matmul(dqkv, W_qkvT, "nn", F32, "d_u_qkv", carry=_pair_plan(p_b))
    pre_b = add_pairs(p_b, s_b, names_b)
    cs_b = _scatter_copies(pre_b)
    st_b = _split_start(pre_b, cs_b, scale1, "scatter_last_start")
    du, fs_a = _matmul(dconv, W_convT, "nn", F32, "d_u_conv", add=du, carry=_sibling_plan(fin_a))
    du = _matmul(dgate, W_gateT, "nn", F32, "d_u_gate", add=du)
    grad_x, d_shift1, d_scale1 = _dx_final(dxa, du, x2, st_b[4][0])

    big = {}
    ws = dict(w_in=(w_inT, m_w_inT, v_w_inT), w_q_b=(w_q_b2, m_w_q_b[0], v_w_q_b[0]),
              w_kv_b=(w_kv_b2, m_w_kv_b[0], v_w_kv_b[0]), w_o_a=(w_o_a2, m_w_o_a[0], v_w_o_a[0]),
              w_o_b=(w_o_b2, m_w_o_b[0], v_w_o_b[0]), w_o=(w_o2, m_w_o[0], v_w_o[0]),
              w_ffn_in=(w_ffn_in2, m_w_ffn_in[0], v_w_ffn_in[0]), w_ffn_out=(w_ffn_out2, m_w_ffn_out[0], v_w_ffn_out[0]))

    def adam_of(nm, a, b, carry=None):
        w_, m_, v_ = ws[nm]
        return _adam_halves("adam_" + nm, w_, m_, v_, a, b, core_i, carry)

    for nm, a, b in zip(names_a, fin_a, fs_a):
        big[nm] = adam_of(nm, a, b, _token_plan(st_b[4][0]))[0]
    done = [big[nm][1] for nm in names_a] + [grad_x]
    pre_b, r_b = _split_wait(st_b, cs_b, done, "scatter_last_wait")
    fin_b = sum_all(pre_b, r_b, names_b)
    fs_b = _run_plan(_sibling_plan(fin_b), "sibling_last")
    for nm, a, b in zip(names_b, fin_b, fs_b):
        big[nm] = adam_of(nm, a, b)

    def pad_d(v):
        return jnp.pad(v, ((0, 0), (0, D - v.shape[1])))

    small = _pack_rows([d_ln1_g, d_ln1_b, d_ln2_g, d_ln2_b, pad_d(d_g_q), pad_d(d_g_kv), d_wconv,
                         d_shift1, d_scale1, d_gate1, d_shift2, d_scale2, d_gate2], 16, after=[pre_b[1]])
    small_all = _all_gather8(small, "gather_small")
    small_sum = _sum8(small_all)
    g_ln1_g, g_ln1_b, g_ln2_g, g_ln2_b = (small_sum[k:k + 1] for k in range(4))
    g_g_q, g_g_kv = small_sum[4:5, :Q_LORA], small_sum[5:6, :KV_LORA]
    g_wconv = lax.dynamic_slice(small_sum[6:9], (0, chip * CW), (3, CW))
    g_b_ada = small_sum[9:15].reshape(1, 6 * D)
    dmod_all = small_all[:, 9:15, :].reshape(8, 6 * D)
    g_w_ada = _ada_bwd(c_all, lax.dynamic_slice(dmod_all, (0, chip * NA), (8, NA)))
    big["w_ada"] = [g_w_ada] + list(_adam("adam_w_ada", w_ada2, m_w_ada[0], v_w_ada[0], g_w_ada))
    sm = {}
    for nm, w_, m_, v_, g_ in [("b_ada", b_ada, m_b_ada, v_b_ada, g_b_ada), ("g_q_a", g_q_a, m_g_q_a, v_g_q_a, g_g_q),
                               ("g_kv_a", g_kv_a, m_g_kv_a, v_g_kv_a, g_g_kv),
                               ("w_conv", w_conv[0], m_w_conv[0], v_w_conv[0], g_wconv),
                               ("ln1_g", ln1_g, m_ln1_g, v_ln1_g, g_ln1_g), ("ln1_b", ln1_b, m_ln1_b, v_ln1_b, g_ln1_b),
                               ("ln2_g", ln2_g, m_ln2_g, v_ln2_g, g_ln2_g), ("ln2_b", ln2_b, m_ln2_b, v_ln2_b, g_ln2_b)]:
        sm[nm] = (g_,) + tuple(_adam_small("adam_" + nm, w_, m_, v_, g_))

    order = ["w_ada", "b_ada", "w_in", "g_q_a", "w_q_b", "g_kv_a", "w_kv_b", "w_o_a", "w_conv", "w_o_b", "w_o",
             "ln1_g", "ln1_b", "w_ffn_in", "w_ffn_out", "ln2_g", "ln2_b"]
    lead = {"b_ada", "g_q_a", "g_kv_a", "ln1_g", "ln1_b", "ln2_g", "ln2_b"}

    def leaf(nm, k):
        val = big[nm][k] if nm in big else sm[nm][k]
        if nm == "w_in":
            val = val.T
        return val if nm in lead else val[None]

    outs = [loss, grad_x[None]]
    for k in range(4):
        outs += [leaf(nm, k) for nm in order]
    return tuple(outs)
```

```python
import functools

import jax
import jax.numpy as jnp
from jax import lax
from jax.experimental import pallas as pl
from jax.experimental.pallas import tpu as pltpu

F32, BF16 = jnp.float32, jnp.bfloat16
N_HEADS, QK_NOPE, QK_ROPE, V_HEAD = 16, 128, 64, 128
Q_LORA, KV_LORA = 512, 512
QK_PAD = 256
QKV_A = 1152
CHUNK_SHIFT = 6
ATTN_SCALE = (QK_NOPE + QK_ROPE) ** -0.5
ROPE_THETA = 10000.0
ALPHA = 2.0 ** 0.25
LN_EPS, RMS_EPS = 1e-5, 1e-6
ADAM_LR, ADAM_B1, ADAM_B2, ADAM_EPS, ADAM_WD, ADAM_STEP = 0.001, 0.9, 0.999, 1e-08, 0.01, 10
ADAM_C1 = 1.0 - ADAM_B1 ** ADAM_STEP
ADAM_C2 = 1.0 - ADAM_B2 ** ADAM_STEP
VMEM_LIMIT = 56 * 1024 * 1024
MESH = pl.DeviceIdType.MESH
ANY = pl.BlockSpec(memory_space=pl.ANY)
HBM_SPEC = pl.BlockSpec(memory_space=pltpu.HBM)
SEM_SPEC = pl.BlockSpec(memory_space=pltpu.SEMAPHORE)
NT = (((1,), (1,)), ((), ()))
TN = (((0,), (0,)), ((), ()))
NN = (((1,), (0,)), ((), ()))


def _params(sem=None):
    return pltpu.CompilerParams(dimension_semantics=sem, vmem_limit_bytes=VMEM_LIMIT)


def _pick(n, cands=(1408, 1024, 512, 384, 256, 128)):
    for t in cands:
        if n % t == 0:
            return t
    return n


def _row_tile(rows, row_bytes, budget, mult=8):
    best = mult
    for t in range(mult, rows + 1, mult):
        if rows % t == 0 and t * row_bytes <= budget:
            best = t
    return best


def _tile2(rows, cols, mult=8, budget=3 << 18):
    col_tiles = [t for t in range(128, cols + 1, 128) if cols % t == 0] or [cols]
    best = None
    for tc in col_tiles:
        for tr in range(mult, rows + 1, mult):
            if rows % tr == 0 and tr * tc <= budget and (best is None or (tr * tc, tc) > (best[0] * best[1], best[1])):
                best = (tr, tc)
    assert best is not None, (rows, cols)
    return best


def _sigmoid(x):
    return jax.nn.sigmoid(x)


class _Plan:
    def __init__(self, ins, outs, sems, start, finish, aliases=None):
        self.ins, self.outs, self.sems, self.start, self.finish = list(ins), list(outs), list(sems), start, finish
        self.aliases = dict(aliases or {})

    def io_aliases(self, first_in, first_out):
        return {first_in + i: first_out + o for i, o in self.aliases.items()}


def _token_plan(token):
    return _Plan([token], [], [], lambda *a: None, lambda *a: None)


def _run_plan(plan, name, ride=None):
    n_in, n_out = len(plan.ins), len(plan.outs)
    extra = [] if ride is None else list(ride)
    aliases = plan.io_aliases(0, 0)
    for k in range(len(extra)):
        aliases[n_in + k] = n_out + k

    def body(*refs):
        ins, outs, sems = refs[:n_in], refs[n_in + len(extra):n_in + len(extra) + n_out], refs[n_in + 2 * len(extra) + n_out:]
        plan.start(ins, outs, sems)
        plan.finish(ins, outs, sems)

    return pl.pallas_call(body, name=name, out_shape=plan.outs + [jax.ShapeDtypeStruct(r.shape, r.dtype) for r in extra],
                          in_specs=[ANY] * (n_in + len(extra)), out_specs=[ANY] * (n_out + len(extra)),
                          scratch_shapes=plan.sems, input_output_aliases=aliases,
                          compiler_params=_params())(*plan.ins, *extra)


def _matmul(a, b, mode, out_dtype, name, add=None, carry=None, shards=None, finish=None):
    if mode == "nn":
        (M, K), N, dims = a.shape, b.shape[-1] * (4 if shards else 1), NN
    elif mode == "nt":
        (M, K), N, dims = a.shape, b.shape[-2], NT
    else:
        (K, M), N, dims = a.shape, b.shape[1], TN
    split_n = shards and mode != "nt"
    tm = _pick(M)
    tn = _pick(N // 4) if split_n else _pick(N)
    deep = (2816, 2048, 1408, 1024, 512, 384, 256, 128)
    if shards and mode == "nt":
        tk = _pick(K // 4, deep)
    else:
        tk = K if K <= 2048 else _pick(K, deep)
    nk = K // tk
    per = (N // 4 // tn) if split_n else (K // 4 // tk if shards else 1)
    a_spec = (pl.BlockSpec((tk, tm), lambda i, j, k: (k, i)) if mode == "tn"
              else pl.BlockSpec((tm, tk), lambda i, j, k: (i, k)))
    if shards == "b" and mode == "nn":
        b_spec = pl.BlockSpec((None, tk, tn), lambda i, j, k: (j // per, k, j % per))
    elif shards == "b":
        b_spec = pl.BlockSpec((None, tn, tk), lambda i, j, k: (k // per, j, k % per))
    else:
        b_spec = (pl.BlockSpec((tn, tk), lambda i, j, k: (j, k)) if mode == "nt"
                  else pl.BlockSpec((tk, tn), lambda i, j, k: (k, j)))
    o_spec = pl.BlockSpec((tm, tn), lambda i, j, k: (i, j))
    o_shape = (M, N)
    if shards == "o":
        o_spec, o_shape = pl.BlockSpec((None, tm, tn), lambda i, j, k: (j // per, i, j % per)), (4, M, N // 4)
    has_add = add is not None
    has_fin = finish is not None
    n_ci = len(carry.ins) if carry else 0
    n_co = len(carry.outs) if carry else 0
    n_in = 2 + has_add + has_fin
    grid = (M // tm, N // tn, nk)

    def body(*refs):
        a_ref, b_ref = refs[0], refs[1]
        add_ref = refs[2] if has_add else None
        fin_ref = refs[2 + has_add] if has_fin else None

        def store(r):
            if has_add:
                r = r + add_ref[...]
            if has_fin:
                r = finish[0](r, fin_ref[...])
            o_ref[...] = r.astype(o_ref.dtype)

        o_ref = refs[n_in + n_ci]
        acc_ref = refs[n_in + n_ci + 1 + n_co] if nk > 1 else None
        c_ins = refs[n_in:n_in + n_ci]
        c_outs = refs[n_in + n_ci + 1:n_in + n_ci + 1 + n_co]
        c_sems = refs[n_in + n_ci + 1 + n_co + (nk > 1):]
        i, j, k = pl.program_id(0), pl.program_id(1), pl.program_id(2)

        if carry:
            @pl.when((i == 0) & (j == 0) & (k == 0))
            def _():
                carry.start(c_ins, c_outs, c_sems)

        part = lax.dot_general(a_ref[...], b_ref[...], dims, preferred_element_type=F32)
        if nk == 1:
            store(part)
        else:
            @pl.when(k == 0)
            def _():
                acc_ref[...] = part

            @pl.when((k > 0) & (k < nk - 1))
            def _():
                acc_ref[...] += part

            @pl.when(k == nk - 1)
            def _():
                store(acc_ref[...] + part)

        if carry:
            @pl.when((i == grid[0] - 1) & (j == grid[1] - 1) & (k == nk - 1))
            def _():
                carry.finish(c_ins, c_outs, c_sems)

    ins = [a, b] + ([add] if has_add else []) + ([finish[1]] if has_fin else []) + (carry.ins if carry else [])
    in_specs = ([a_spec, b_spec] + ([o_spec] if has_add else [])
                + ([pl.BlockSpec((tm, finish[1].shape[1]), lambda i, j, k: (i, 0))] if has_fin else []) + [ANY] * n_ci)
    res = pl.pallas_call(
        body, name=name, grid=grid,
        in_specs=in_specs, out_specs=[o_spec] + [ANY] * n_co,
        out_shape=[jax.ShapeDtypeStruct(o_shape, out_dtype)] + (carry.outs if carry else []),
        scratch_shapes=([pltpu.VMEM((tm, tn), F32)] if nk > 1 else []) + (carry.sems if carry else []),
        input_output_aliases=carry.io_aliases(n_in, 1) if carry else {},
        compiler_params=_params(("arbitrary",) * 3 if carry else ("parallel", "parallel", "arbitrary")),
    )(*ins)
    return (res[0], res[1:]) if carry else res[0]


def _rows(body, name, n_rows, tm, ins, outs, accs=(), carry=None):
    grid = (n_rows // tm,)

    def halo(arr):
        return 16 if arr.dtype == BF16 else 8

    arrays, in_specs = [], []
    for spec in ins:
        kind, arr = spec[0], spec[1]
        arrays.append(arr)
        if kind == "row":
            _, _, cb, w = spec
            in_specs.append(pl.BlockSpec((tm, w), lambda i, cb=cb: (i, cb)))
        elif kind == "full":
            in_specs.append(pl.BlockSpec(arr.shape, lambda i, nd=arr.ndim: (0,) * nd))
        elif kind == "prev":
            _, _, cb, w = spec
            h = halo(arr)
            in_specs.append(pl.BlockSpec((h, w), lambda i, cb=cb, per=tm // h: (jnp.maximum(i * per - 1, 0), cb)))
        else:
            _, _, cb, w = spec
            h = halo(arr)
            in_specs.append(pl.BlockSpec((h, w), lambda i, cb=cb, per=tm // h, last=n_rows // h - 1:
                                         (jnp.minimum((i + 1) * per, last), cb)))
    out_shape = [jax.ShapeDtypeStruct((n_rows, w), dt) for (w, dt) in outs]
    out_specs = [pl.BlockSpec((tm, w), lambda i: (i, 0)) for (w, _) in outs]
    out_shape += [jax.ShapeDtypeStruct(s, F32) for s in accs]
    out_specs += [pl.BlockSpec(s, lambda i, nd=len(s): (0,) * nd) for s in accs]
    n_in, n_out, n_acc = len(ins), len(outs), len(accs)
    n_ci = len(carry.ins) if carry else 0
    n_co = len(carry.outs) if carry else 0

    def kernel_body(*refs):
        first = n_in + n_ci
        c_ins, c_outs, c_sems = refs[n_in:first], refs[first + n_out + n_acc:first + n_out + n_acc + n_co], refs[first + n_out + n_acc + n_co:]
        if carry:
            @pl.when(pl.program_id(0) == 0)
            def _():
                carry.start(c_ins, c_outs, c_sems)

        body(pl.program_id(0), refs[:n_in], refs[first:first + n_out], refs[first + n_out:first + n_out + n_acc])
        if carry:
            @pl.when(pl.program_id(0) == grid[0] - 1)
            def _():
                carry.finish(c_ins, c_outs, c_sems)

    res = pl.pallas_call(
        kernel_body, name=name, grid=grid, in_specs=in_specs + [ANY] * n_ci, out_specs=out_specs + [ANY] * n_co,
        out_shape=out_shape + (carry.outs if carry else []), scratch_shapes=carry.sems if carry else [],
        input_output_aliases=carry.io_aliases(n_in, n_out + n_acc) if carry else {},
        compiler_params=_params(("arbitrary",)),
    )(*arrays, *(carry.ins if carry else []))
    return (res[:n_out + n_acc], res[n_out + n_acc:]) if carry else res


def _acc_add(i, ref, val):
    @pl.when(i == 0)
    def _():
        ref[...] = val

    @pl.when(i > 0)
    def _():
        ref[...] += val


def _rope(t, tab, sign):
    c, sa, sb = tab[:, 0:128], tab[:, 128:256], tab[:, 256:384]
    rot = pltpu.roll(t, 96, 1) * sa + pltpu.roll(t, 32, 1) * sb
    return t * c + rot if sign > 0 else t * c - rot


def _ln_stats(r):
    mu = jnp.mean(r, axis=-1, keepdims=True)
    d = r - mu
    var = jnp.mean(d * d, axis=-1, keepdims=True)
    rstd = lax.rsqrt(var + LN_EPS)
    return d * rstd, rstd


def _ln_bwd(dxh, xh, rstd):
    m1 = jnp.mean(dxh, axis=-1, keepdims=True)
    m2 = jnp.mean(dxh * xh, axis=-1, keepdims=True)
    return rstd * (dxh - m1 - xh * m2)


def _modulate(x, scale, shift, name):
    S, D = x.shape

    def body(i, ins, outs, accs):
        outs[0][...] = (ins[0][...] * (1.0 + ins[1][...]) + ins[2][...]).astype(BF16)

    return _rows(body, name, S, _pick(S, (256, 128)), [("row", x, 0, D), ("full", scale), ("full", shift)], [(D, BF16)])[0]


def _rms_fwd(pq, tab, g_q, g_kv):
    S = pq.shape[0]

    def body(i, ins, outs, accs):
        pq_ref, tab_ref, gq_ref, gkv_ref = ins

        def rms(x, g):
            return x * lax.rsqrt(jnp.mean(x * x, axis=-1, keepdims=True) + RMS_EPS) * g

        outs[0][...] = rms(pq_ref[:, 0:Q_LORA], gq_ref[...]).astype(BF16)
        outs[1][...] = rms(pq_ref[:, Q_LORA:Q_LORA + KV_LORA], gkv_ref[...]).astype(BF16)
        outs[2][...] = _rope(pq_ref[:, Q_LORA + KV_LORA:QKV_A], tab_ref[...], 1).astype(BF16)

    return _rows(body, "rms_fwd", S, _pick(S, (256, 128)),
                 [("row", pq, 0, QKV_A), ("row", tab, 0, 384), ("full", g_q), ("full", g_kv)],
                 [(Q_LORA, BF16), (KV_LORA, BF16), (128, BF16)])


def _allowed(q0, k0, bq):
    row = q0 + lax.broadcasted_iota(jnp.int32, (bq, bq), 0)
    col = k0 + lax.broadcasted_iota(jnp.int32, (bq, bq), 1)
    return (col >> CHUNK_SHIFT) <= (row >> CHUNK_SHIFT)


ATTN_BLOCK = 512


def _attn_fwd(q, kv, kr, carry=None):
    S = q.shape[0]
    bq = min(ATTN_BLOCK, S)
    nq = S // bq
    n_ci = len(carry.ins) if carry else 0
    n_co = len(carry.outs) if carry else 0

    def body(*refs):
        q_ref, kn_ref, v_ref, kr_ref = refs[:4]
        o_ref, lse_ref = refs[4 + n_ci:6 + n_ci]
        c_ins, c_outs = refs[4:4 + n_ci], refs[6 + n_ci:6 + n_ci + n_co]
        kcat = refs[6 + n_ci + n_co]
        c_sems = refs[7 + n_ci + n_co:]
        qi = pl.program_id(1)
        if carry:
            @pl.when((pl.program_id(0) == 0) & (qi == 0))
            def _():
                carry.start(c_ins, c_outs, c_sems)

        @pl.when(qi == 0)
        def _():
            kcat[:, 0:128] = kn_ref[...]
            kcat[:, 128:256] = kr_ref[...]

        qv = q_ref[...]

        def step(j, carry, masked):
            m, l, acc = carry
            off = pl.multiple_of(j * bq, bq)
            s = lax.dot_general(qv, kcat[pl.ds(off, bq), :], NT, preferred_element_type=F32) * ATTN_SCALE
            if masked:
                s = jnp.where(_allowed(qi * bq, off, bq), s, -1e30)
            m_new = jnp.maximum(m, jnp.max(s, axis=1, keepdims=True))
            a = jnp.exp(m - m_new)
            p = jnp.exp(s - m_new)
            l = a * l + jnp.sum(p, axis=1, keepdims=True)
            acc = a * acc + jnp.dot(p.astype(BF16), v_ref[pl.ds(off, bq), :], preferred_element_type=F32)
            return m_new, l, acc

        init = (jnp.full((bq, 1), -1e30, F32), jnp.zeros((bq, 1), F32), jnp.zeros((bq, V_HEAD), F32))
        below = lax.fori_loop(0, qi, lambda j, cr: step(j, cr, False), init)
        m, l, acc = step(qi, below, True)
        o_ref[...] = (acc / l).astype(BF16)
        lse_ref[0] = m + jnp.log(l)
        if carry:
            @pl.when((pl.program_id(0) == N_HEADS - 1) & (qi == nq - 1))
            def _():
                carry.finish(c_ins, c_outs, c_sems)

    res = pl.pallas_call(
        body, name="attn_fwd", grid=(N_HEADS, nq),
        in_specs=[pl.BlockSpec((bq, QK_PAD), lambda h, i: (i, h)),
                  pl.BlockSpec((S, 128), lambda h, i: (0, 2 * h)),
                  pl.BlockSpec((S, 128), lambda h, i: (0, 2 * h + 1)),
                  pl.BlockSpec((S, 128), lambda h, i: (0, 0))] + [ANY] * n_ci,
        out_specs=[pl.BlockSpec((bq, V_HEAD), lambda h, i: (i, h)),
                   pl.BlockSpec((1, bq, 1), lambda h, i: (h, i, 0))] + [ANY] * n_co,
        out_shape=[jax.ShapeDtypeStruct((S, N_HEADS * V_HEAD), BF16),
                   jax.ShapeDtypeStruct((N_HEADS, S, 1), F32)] + (carry.outs if carry else []),
        scratch_shapes=[pltpu.VMEM((S, QK_PAD), BF16)] + (carry.sems if carry else []),
        input_output_aliases=carry.io_aliases(4, 2) if carry else {},
        compiler_params=_params(("arbitrary", "arbitrary")),
    )(q, kv, kv, kr, *(carry.ins if carry else []))
    return res[0], res[1], res[2:]


def _attn_bwd(q, kv, kr, do, o, lse, tab, carry=None):
    S = q.shape[0]
    bq = min(ATTN_BLOCK, S)
    nq = S // bq

    n_ci = len(carry.ins) if carry else 0
    n_co = len(carry.outs) if carry else 0

    def body(*refs):
        q_ref, kn_ref, v_ref, kr_ref, do_ref, o_ref, lse_ref, tab_ref = refs[:8]
        dq_ref, dkv_ref, dkr_ref = refs[8 + n_ci:11 + n_ci]
        dq_acc, dk_acc, dv_acc, kcat, delta = refs[11 + n_ci + n_co:16 + n_ci + n_co]
        c_ins, c_outs, c_sems = refs[8:8 + n_ci], refs[11 + n_ci:11 + n_ci + n_co], refs[16 + n_ci + n_co:]
        h = pl.program_id(0)
        if carry:
            @pl.when(h == 0)
            def _():
                carry.start(c_ins, c_outs, c_sems)

        dq_acc[...] = jnp.zeros_like(dq_acc)
        dk_acc[...] = jnp.zeros_like(dk_acc)
        dv_acc[...] = jnp.zeros_like(dv_acc)
        kcat[:, 0:128] = kn_ref[...]
        kcat[:, 128:256] = kr_ref[...]
        for r in range(nq):
            rows = slice(r * bq, (r + 1) * bq)
            delta[rows, :] = jnp.sum(do_ref[rows, :].astype(F32) * o_ref[rows, :].astype(F32), axis=1, keepdims=True)

        def pair(i, j, masked):
            rows_i = pl.ds(pl.multiple_of(i * bq, bq), bq)
            rows_j = pl.ds(pl.multiple_of(j * bq, bq), bq)
            qv, dov, k = q_ref[rows_i, :], do_ref[rows_i, :], kcat[rows_j, :]
            s = lax.dot_general(qv, k, NT, preferred_element_type=F32) * ATTN_SCALE
            if masked:
                s = jnp.where(_allowed(i * bq, j * bq, bq), s, -1e30)
            p = jnp.exp(s - lse_ref[0, rows_i, :])
            dv_acc[rows_j, :] += lax.dot_general(p.astype(BF16), dov, TN, preferred_element_type=F32)
            dp = lax.dot_general(dov, v_ref[rows_j, :], NT, preferred_element_type=F32)
            ds = (p * (dp - delta[rows_i, :]) * ATTN_SCALE).astype(BF16)
            dk_acc[rows_j, :] += lax.dot_general(ds, qv, TN, preferred_element_type=F32)
            dq_acc[rows_i, :] += jnp.dot(ds, k, preferred_element_type=F32)

        def kv_step(j, _):
            pair(j, j, True)

            def q_step(i, _):
                pair(i, j, False)
                return 0

            lax.fori_loop(j + 1, nq, q_step, 0)
            return 0

        lax.fori_loop(0, nq, kv_step, 0)

        for r in range(nq):
            rows = slice(r * bq, (r + 1) * bq)
            dq_ref[rows, 0:128] = dq_acc[rows, 0:128].astype(BF16)
            dq_ref[rows, 128:256] = _rope(dq_acc[rows, 128:256], tab_ref[rows, :], -1).astype(BF16)
        dkv_ref[:, 0:128] = dk_acc[:, 0:128].astype(BF16)
        dkv_ref[:, 128:256] = dv_acc[...].astype(BF16)

        @pl.when(h == 0)
        def _():
            dkr_ref[...] = dk_acc[:, 128:256]

        @pl.when(h > 0)
        def _():
            dkr_ref[...] += dk_acc[:, 128:256]

        @pl.when(h == N_HEADS - 1)
        def _():
            for r in range(nq):
                rows = slice(r * bq, (r + 1) * bq)
                dkr_ref[rows, :] = _rope(dkr_ref[rows, :], tab_ref[rows, :], -1)
            if carry:
                carry.finish(c_ins, c_outs, c_sems)

    W = N_HEADS * QK_PAD
    res = pl.pallas_call(
        body, name="attn_bwd", grid=(N_HEADS,),
        in_specs=[pl.BlockSpec((S, QK_PAD), lambda h: (0, h)),
                  pl.BlockSpec((S, 128), lambda h: (0, 2 * h)),
                  pl.BlockSpec((S, 128), lambda h: (0, 2 * h + 1)),
                  pl.BlockSpec((S, 128), lambda h: (0, 0)),
                  pl.BlockSpec((S, V_HEAD), lambda h: (0, h)),
                  pl.BlockSpec((S, V_HEAD), lambda h: (0, h)),
                  pl.BlockSpec((1, S, 1), lambda h: (h, 0, 0)),
                  pl.BlockSpec((S, 384), lambda h: (0, 0))] + [ANY] * n_ci,
        out_specs=[pl.BlockSpec((S, QK_PAD), lambda h: (0, h)),
                   pl.BlockSpec((S, QK_PAD), lambda h: (0, h)),
                   pl.BlockSpec((S, 128), lambda h: (0, 0))] + [ANY] * n_co,
        out_shape=[jax.ShapeDtypeStruct((S, W), BF16), jax.ShapeDtypeStruct((S, W), BF16),
                   jax.ShapeDtypeStruct((S, 128), F32)] + (carry.outs if carry else []),
        scratch_shapes=[pltpu.VMEM((S, QK_PAD), F32), pltpu.VMEM((S, QK_PAD), F32), pltpu.VMEM((S, V_HEAD), F32),
                        pltpu.VMEM((S, QK_PAD), BF16), pltpu.VMEM((S, 1), F32)]
        + (carry.sems if carry else []),
        input_output_aliases=carry.io_aliases(8, 3) if carry else {},
        compiler_params=_params(("arbitrary",)),
    )(q, kv, kv, kr, do, o, lse, tab, *(carry.ins if carry else []))
    return res[0], res[1], res[2], res[3:]


def _shift_down(cur, prev, i, n):
    tm, h = cur.shape[0], prev.shape[0]
    prev = jnp.where(i == 0, jnp.zeros_like(prev), prev)
    full = jnp.concatenate([prev, cur], axis=0)
    return pltpu.roll(full, n, 0)[h:h + tm, :]


def _shift_up(cur, nxt, i, last, n):
    tm, h = cur.shape[0], nxt.shape[0]
    nxt = jnp.where(i == last, jnp.zeros_like(nxt), nxt)
    full = jnp.concatenate([cur, nxt], axis=0)
    return pltpu.roll(full, tm + h - n, 0)[0:tm, :]


def _conv_fwd(pc, w_conv):
    S, D = pc.shape[0], pc.shape[1] // 3
    tm = _pick(S, (256, 128))

    def body(i, ins, outs, accs):
        b_ref, c_ref, x_ref, cp_ref, xp_ref, w_ref = ins
        z = c_ref[...].astype(F32) * x_ref[...].astype(F32)
        zp = cp_ref[...].astype(F32) * xp_ref[...].astype(F32)
        cz = w_ref[0:1, :] * _shift_down(z, zp, i, 2) + w_ref[1:2, :] * _shift_down(z, zp, i, 1) + w_ref[2:3, :] * z
        outs[0][...] = (b_ref[...].astype(F32) * cz).astype(BF16)

    return _rows(body, "conv_fwd", S, tm,
                 [("row", pc, 0, D), ("row", pc, 1, D), ("row", pc, 2, D), ("prev", pc, 1, D), ("prev", pc, 2, D),
                  ("full", w_conv)], [(D, BF16)])[0]


def _conv_bwd(dhb, pc, w_conv):
    S, D = dhb.shape
    tm = _pick(S, (256, 128))
    last = S // tm - 1

    def body(i, ins, outs, accs):
        g_ref, b_ref, c_ref, x_ref, cp_ref, xp_ref, gn_ref, bn_ref, w_ref = ins
        w0, w1, w2 = w_ref[0:1, :], w_ref[1:2, :], w_ref[2:3, :]
        c, x, g = c_ref[...].astype(F32), x_ref[...].astype(F32), g_ref[...].astype(F32)
        z = c * x
        zp = cp_ref[...].astype(F32) * xp_ref[...].astype(F32)
        z1, z2 = _shift_down(z, zp, i, 1), _shift_down(z, zp, i, 2)
        cz = w0 * z2 + w1 * z1 + w2 * z
        dcz = g * b_ref[...].astype(F32)
        dczn = gn_ref[...].astype(F32) * bn_ref[...].astype(F32)
        dz = w2 * dcz + w1 * _shift_up(dcz, dczn, i, last, 1) + w0 * _shift_up(dcz, dczn, i, last, 2)
        outs[0][:, 0:D] = (g * cz).astype(BF16)
        outs[0][:, D:2 * D] = (dz * x).astype(BF16)
        outs[0][:, 2 * D:3 * D] = (dz * c).astype(BF16)
        dw = jnp.concatenate([jnp.sum(dcz * z2, axis=0, keepdims=True), jnp.sum(dcz * z1, axis=0, keepdims=True),
                              jnp.sum(dcz * z, axis=0, keepdims=True)], axis=0)
        _acc_add(i, accs[0], dw)

    return _rows(body, "conv_bwd", S, tm,
                 [("row", dhb, 0, D), ("row", pc, 0, D), ("row", pc, 1, D), ("row", pc, 2, D),
                  ("prev", pc, 1, D), ("prev", pc, 2, D), ("next", dhb, 0, D), ("next", pc, 0, D), ("full", w_conv)],
                 [(3 * D, BF16)], [(3, D)])


def _merge_fwd(y_a, y_b, pg):
    S, D = y_a.shape

    def body(i, ins, outs, accs):
        ya, yb, ga, gb = ins
        outs[0][...] = (_sigmoid(ga[...].astype(F32)) * ya[...].astype(F32)
                        + _sigmoid(gb[...].astype(F32)) * yb[...].astype(F32)).astype(BF16)

    return _rows(body, "merge_fwd", S, _pick(S, (256, 128)),
                 [("row", y_a, 0, D), ("row", y_b, 0, D), ("row", pg, 0, D), ("row", pg, 1, D)], [(D, BF16)])[0]


def _merge_bwd(dm, y_a, y_b, pg):
    S, D = dm.shape

    def body(i, ins, outs, accs):
        d, ya, yb = ins[0][...].astype(F32), ins[1][...].astype(F32), ins[2][...].astype(F32)
        sa, sb = _sigmoid(ins[3][...].astype(F32)), _sigmoid(ins[4][...].astype(F32))
        outs[0][...] = (d * sa).astype(BF16)
        outs[1][...] = (d * sb).astype(BF16)
        outs[2][:, 0:D] = (d * ya * (sa * (1.0 - sa))).astype(BF16)
        outs[2][:, D:2 * D] = (d * yb * (sb * (1.0 - sb))).astype(BF16)

    return _rows(body, "merge_bwd", S, _pick(S, (256, 128)),
                 [("row", dm, 0, D), ("row", y_a, 0, D), ("row", y_b, 0, D), ("row", pg, 0, D), ("row", pg, 1, D)],
                 [(D, BF16), (D, BF16), (2 * D, BF16)])


def _ln1_fwd(x, mix, gate1, g, b, scale2, shift2):
    S, D = x.shape

    def body(i, ins, outs, accs):
        x_ref, mix_ref, gate_ref, g_ref, b_ref, sc_ref, sh_ref = ins
        xh, _ = _ln_stats(ALPHA * x_ref[...] + gate_ref[...] * mix_ref[...])
        x1 = xh * g_ref[...] + b_ref[...]
        outs[0][...] = x1
        outs[1][...] = (x1 * (1.0 + sc_ref[...]) + sh_ref[...]).astype(BF16)

    return _rows(body, "ln1_fwd", S, _pick(S, (256, 128)),
                 [("row", x, 0, D), ("row", mix, 0, D), ("full", gate1), ("full", g), ("full", b),
                  ("full", scale2), ("full", shift2)], [(D, F32), (D, BF16)])


def _swiglu_fwd(hh, carry=None):
    S, F = hh.shape[0], hh.shape[1] // 2

    def body(i, ins, outs, accs):
        hg = ins[0][...].astype(F32)
        outs[0][...] = (hg * _sigmoid(hg) * ins[1][...].astype(F32)).astype(BF16)

    res = _rows(body, "swiglu_fwd", S, _pick(S, (128,)), [("row", hh, 0, F), ("row", hh, 1, F)], [(F, BF16)], carry=carry)
    return (res[0][0], res[1]) if carry else res[0]


def _swiglu_bwd(dact, hh):
    S, F = dact.shape

    def body(i, ins, outs, accs):
        d, hg, hu = ins[0][...].astype(F32), ins[1][...].astype(F32), ins[2][...].astype(F32)
        sg = _sigmoid(hg)
        outs[0][:, 0:F] = (d * hu * (sg * (1.0 + hg * (1.0 - sg)))).astype(BF16)
        outs[0][:, F:2 * F] = (d * (hg * sg)).astype(BF16)

    return _rows(body, "swiglu_bwd", S, _pick(S, (128,)),
                 [("row", dact, 0, F), ("row", hh, 0, F), ("row", hh, 1, F)], [(2 * F, BF16)])[0]


def _ln2_loss_bwd(x1, ffn, gate2, g, b, target):
    S, D = x1.shape

    def body(i, ins, outs, accs):
        x1_ref, f_ref, gate_ref, g_ref, b_ref, t_ref = ins
        f = f_ref[...]
        xh, rstd = _ln_stats(ALPHA * x1_ref[...] + gate_ref[...] * f)
        e = xh * g_ref[...] + b_ref[...] - t_ref[...]
        dy = e * (1.0 / D)
        dr = _ln_bwd(dy * g_ref[...], xh, rstd)
        outs[0][...] = (gate_ref[...] * dr).astype(BF16)
        outs[1][...] = ALPHA * dr
        _acc_add(i, accs[0], jnp.full((1, 128), (0.5 / D) * jnp.sum(e * e), F32))
        _acc_add(i, accs[1], jnp.sum(dy * xh, axis=0, keepdims=True))
        _acc_add(i, accs[2], jnp.sum(dy, axis=0, keepdims=True))
        _acc_add(i, accs[3], jnp.sum(dr * f, axis=0, keepdims=True))

    return _rows(body, "ln2_loss_bwd", S, _pick(S, (256, 128)),
                 [("row", x1, 0, D), ("row", ffn, 0, D), ("full", gate2), ("full", g), ("full", b), ("row", target, 0, D)],
                 [(D, BF16), (D, F32)], [(1, 128), (1, D), (1, D), (1, D)])


def _ln1_bwd(x, mix, dx1a, du2, gate1, g, b, scale2):
    S, D = x.shape

    def body(i, ins, outs, accs):
        x_ref, mix_ref, da_ref, du_ref, gate_ref, g_ref, b_ref, sc_ref = ins
        mix, du = mix_ref[...], du_ref[...]
        xh, rstd = _ln_stats(ALPHA * x_ref[...] + gate_ref[...] * mix)
        x1 = xh * g_ref[...] + b_ref[...]
        dx1 = da_ref[...] + du * (1.0 + sc_ref[...])
        dr = _ln_bwd(dx1 * g_ref[...], xh, rstd)
        outs[0][...] = (gate_ref[...] * dr).astype(BF16)
        outs[1][...] = ALPHA * dr
        _acc_add(i, accs[0], jnp.sum(du, axis=0, keepdims=True))
        _acc_add(i, accs[1], jnp.sum(du * x1, axis=0, keepdims=True))
        _acc_add(i, accs[2], jnp.sum(dx1 * xh, axis=0, keepdims=True))
        _acc_add(i, accs[3], jnp.sum(dx1, axis=0, keepdims=True))
        _acc_add(i, accs[4], jnp.sum(dr * mix, axis=0, keepdims=True))

    return _rows(body, "ln1_bwd", S, _pick(S, (256, 128)),
                 [("row", x, 0, D), ("row", mix, 0, D), ("row", dx1a, 0, D), ("row", du2, 0, D),
                  ("full", gate1), ("full", g), ("full", b), ("full", scale2)],
                 [(D, BF16), (D, F32)], [(1, D)] * 5)


def _rms_bwd(d_rq, d_rkv, pq, dkr, g_q, g_kv):
    S = pq.shape[0]

    def body(i, ins, outs, accs):
        dq_ref, dkv_ref, pq_ref, dkr_ref, gq_ref, gkv_ref = ins

        def rms_bwd(dy, x, g):
            r = lax.rsqrt(jnp.mean(x * x, axis=-1, keepdims=True) + RMS_EPS)
            dyg = dy * g
            dx = r * dyg - x * (r * r * r) * jnp.mean(dyg * x, axis=-1, keepdims=True)
            return dx, jnp.sum(dy * (x * r), axis=0, keepdims=True)

        dxq, dgq = rms_bwd(dq_ref[...], pq_ref[:, 0:Q_LORA], gq_ref[...])
        dxkv, dgkv = rms_bwd(dkv_ref[...], pq_ref[:, Q_LORA:Q_LORA + KV_LORA], gkv_ref[...])
        outs[0][:, 0:Q_LORA] = dxq.astype(BF16)
        outs[0][:, Q_LORA:Q_LORA + KV_LORA] = dxkv.astype(BF16)
        outs[0][:, Q_LORA + KV_LORA:QKV_A] = dkr_ref[...].astype(BF16)
        _acc_add(i, accs[0], dgq)
        _acc_add(i, accs[1], dgkv)

    return _rows(body, "rms_bwd", S, _pick(S, (256, 128)),
                 [("row", d_rq, 0, Q_LORA), ("row", d_rkv, 0, KV_LORA), ("row", pq, 0, QKV_A), ("row", dkr, 0, 128),
                  ("full", g_q), ("full", g_kv)], [(QKV_A, BF16)], [(1, Q_LORA), (1, KV_LORA)])


def _dx_final(dxa, du, x, scale1):
    S, D = x.shape

    def body(i, ins, outs, accs):
        du = ins[1][...]
        outs[0][...] = ins[0][...] + du * (1.0 + ins[3][...])
        _acc_add(i, accs[0], jnp.sum(du, axis=0, keepdims=True))
        _acc_add(i, accs[1], jnp.sum(du * ins[2][...], axis=0, keepdims=True))

    return _rows(body, "dx_final", S, _pick(S, (256, 128)),
                 [("row", dxa, 0, D), ("row", du, 0, D), ("row", x, 0, D), ("full", scale1)],
                 [(D, F32)], [(1, D), (1, D)])


def _ada_fwd(c_all, w, bias):
    B, D = c_all.shape
    NA = w.shape[1]
    tn = _pick(NA, (512, 256, 128))

    def body(c_ref, w_ref, b_ref, o_ref):
        cv = c_ref[...]
        ca = (cv * _sigmoid(cv)).astype(BF16)
        o_ref[...] = jnp.dot(ca, w_ref[...].astype(BF16), preferred_element_type=F32) + b_ref[...]

    return pl.pallas_call(
        body, name="ada_fwd", grid=(NA // tn,),
        in_specs=[pl.BlockSpec((B, D), lambda j: (0, 0)), pl.BlockSpec((D, tn), lambda j: (0, j)),
                  pl.BlockSpec((1, tn), lambda j: (0, j))],
        out_specs=pl.BlockSpec((B, tn), lambda j: (0, j)),
        out_shape=jax.ShapeDtypeStruct((B, NA), F32),
        compiler_params=_params(("arbitrary",)),
    )(c_all, w, bias)


def _ada_bwd(c_all, dmod):
    B, D = c_all.shape
    NA = dmod.shape[1]
    tn = _pick(NA, (512, 256, 128))

    def body(c_ref, d_ref, o_ref):
        cv = c_ref[...]
        ca = (cv * _sigmoid(cv)).astype(BF16)
        o_ref[...] = lax.dot_general(ca, d_ref[...].astype(BF16), TN, preferred_element_type=F32)

    return pl.pallas_call(
        body, name="ada_bwd", grid=(NA // tn,),
        in_specs=[pl.BlockSpec((B, D), lambda j: (0, 0)), pl.BlockSpec((B, tn), lambda j: (0, j))],
        out_specs=pl.BlockSpec((D, tn), lambda j: (0, j)),
        out_shape=jax.ShapeDtypeStruct((D, NA), F32),
        compiler_params=_params(("arbitrary",)),
    )(c_all, dmod)


def _pack_rows(parts, n_rows, after=()):
    N = parts[0].shape[1]
    n = len(parts)

    def body(*refs):
        o_ref = refs[-1]
        o_ref[...] = jnp.zeros_like(o_ref)
        at = 0
        for r in refs[:n]:
            o_ref[at:at + r.shape[0], :] = r[...]
            at += r.shape[0]

    vmem = pl.BlockSpec(memory_space=pltpu.VMEM)
    return pl.pallas_call(body, name="pack_small", out_shape=jax.ShapeDtypeStruct((n_rows, N), F32),
                          in_specs=[vmem] * n + [ANY] * len(after), out_specs=vmem,
                          compiler_params=_params())(*parts, *after)


def _sum8(parts):
    _, R, N = parts.shape

    def body(p_ref, o_ref):
        acc = p_ref[0]
        for d in range(1, 8):
            acc = acc + p_ref[d]
        o_ref[...] = acc

    return pl.pallas_call(body, name="sum8", out_shape=jax.ShapeDtypeStruct((R, N), F32),
                          compiler_params=_params())(parts)


def _adam_math(w, g, m, v):
    m = ADAM_B1 * m + (1.0 - ADAM_B1) * g
    v = ADAM_B2 * v + (1.0 - ADAM_B2) * (g * g)
    delta = -ADAM_LR * ((m / ADAM_C1) / (jnp.sqrt(v / ADAM_C2) + ADAM_EPS) + ADAM_WD * w)
    return delta, m, v


def _adam(name, w, m, v, g, carry=None):
    R, C = w.shape
    tm = _row_tile(R, C * 4, 1 << 20)
    steps = R // tm
    n_ci = len(carry.ins) if carry else 0
    n_co = len(carry.outs) if carry else 0

    def body(*refs):
        w_ref, m_ref, v_ref, g_ref = refs[:4]
        d_ref, nm_ref, nv_ref = refs[4 + n_ci:7 + n_ci]
        c_ins, c_outs, c_sems = refs[4:4 + n_ci], refs[7 + n_ci:7 + n_ci + n_co], refs[7 + n_ci + n_co:]
        if carry:
            @pl.when(pl.program_id(0) == 0)
            def _():
                carry.start(c_ins, c_outs, c_sems)

        delta, nm, nv = _adam_math(w_ref[...], g_ref[...], m_ref[...], v_ref[...])
        d_ref[...] = delta
        nm_ref[...] = nm
        nv_ref[...] = nv
        if carry:
            @pl.when(pl.program_id(0) == steps - 1)
            def _():
                carry.finish(c_ins, c_outs, c_sems)

    spec = pl.BlockSpec((tm, C), lambda i: (i, 0))
    res = pl.pallas_call(
        body, name=name, grid=(steps,), in_specs=[spec] * 4 + [ANY] * n_ci, out_specs=[spec] * 3 + [ANY] * n_co,
        out_shape=[jax.ShapeDtypeStruct((R, C), F32)] * 3 + (carry.outs if carry else []),
        scratch_shapes=carry.sems if carry else [],
        input_output_aliases=carry.io_aliases(4, 3) if carry else {},
        compiler_params=_params(("arbitrary",)),
    )(w, m, v, g, *(carry.ins if carry else []))
    return (res[:3], res[3:]) if carry else res


def _adam_halves(name, w, m, v, mine, other, core, carry=None):
    R, C = w.shape
    Rh = mine.shape[0]
    tc = max(t for t in range(128, C + 1, 128) if C % t == 0 and R * t <= (3 << 17))
    steps = C // tc
    n_ci = len(carry.ins) if carry else 0
    n_co = len(carry.outs) if carry else 0

    def body(*refs):
        c_ref, w_ref, m_ref, v_ref, a_ref, b_ref = refs[:6]
        g_ref, d_ref, nm_ref, nv_ref = refs[6 + n_ci:10 + n_ci]
        c_ins, c_outs, c_sems = refs[6:6 + n_ci], refs[10 + n_ci:10 + n_ci + n_co], refs[10 + n_ci + n_co:]
        if carry:
            @pl.when(pl.program_id(0) == 0)
            def _():
                carry.start(c_ins, c_outs, c_sems)

        first = c_ref[0] == 0
        g = jnp.concatenate([jnp.where(first, a_ref[...], b_ref[...]),
                             jnp.where(first, b_ref[0:R - Rh, :], a_ref[0:R - Rh, :])], axis=0)
        delta, nm, nv = _adam_math(w_ref[...], g, m_ref[...], v_ref[...])
        g_ref[...] = g
        d_ref[...] = delta
        nm_ref[...] = nm
        nv_ref[...] = nv
        if carry:
            @pl.when(pl.program_id(0) == steps - 1)
            def _():
                carry.finish(c_ins, c_outs, c_sems)

    spec = pl.BlockSpec((R, tc), lambda i, c_ref: (0, i))
    h_spec = pl.BlockSpec((Rh, tc), lambda i, c_ref: (0, i))
    res = pl.pallas_call(
        body, name=name, out_shape=[jax.ShapeDtypeStruct((R, C), F32)] * 4 + (carry.outs if carry else []),
        grid_spec=pltpu.PrefetchScalarGridSpec(
            num_scalar_prefetch=1, grid=(steps,), in_specs=[spec, spec, spec, h_spec, h_spec] + [ANY] * n_ci,
            out_specs=[spec] * 4 + [ANY] * n_co, scratch_shapes=carry.sems if carry else []),
        input_output_aliases=carry.io_aliases(6, 4) if carry else {},
        compiler_params=_params(("arbitrary",)),
    )(core, w, m, v, mine, other, *(carry.ins if carry else []))
    return (res[:4], res[4:]) if carry else res


def _adam_small(name, w, m, v, g):
    def body(w_ref, m_ref, v_ref, g_ref, d_ref, nm_ref, nv_ref):
        delta, nm, nv = _adam_math(w_ref[...], g_ref[...], m_ref[...], v_ref[...])
        d_ref[...] = delta
        nm_ref[...] = nm
        nv_ref[...] = nv

    return pl.pallas_call(body, name=name, out_shape=[jax.ShapeDtypeStruct(w.shape, F32)] * 3,
                          compiler_params=_params())(w, m, v, g)


def _place():
    return lax.axis_index("x"), lax.axis_index("y"), lax.axis_index("c")


def _other_chips(x, y):
    return [(1 - x, y), (x, 1 - y), (1 - x, 1 - y)]


def _all_gather8(blk, name):
    R, N = blk.shape

    def body(x_ref, out_ref, send_sems, recv_sems, local_sem):
        x, y, c = _place()
        me = 4 * x + 2 * y + c
        mine = pltpu.make_async_copy(x_ref, out_ref.at[me], local_sem)
        mine.start()
        flips = [(j >> 2 & 1, j >> 1 & 1, j & 1) for j in range(1, 8)]
        peers = [((1 - x) if fx else x, (1 - y) if fy else y, (1 - c) if fc else c) for fx, fy, fc in flips]
        sends = []
        for j, peer in enumerate(peers):
            cp = pltpu.make_async_remote_copy(src_ref=x_ref, dst_ref=out_ref.at[me], send_sem=send_sems.at[j],
                                              recv_sem=recv_sems.at[j], device_id=peer, device_id_type=MESH)
            cp.start()
            sends.append(cp)
        for j, (px, py, pc) in enumerate(peers):
            pltpu.make_async_remote_copy(src_ref=x_ref, dst_ref=out_ref.at[4 * px + 2 * py + pc],
                                         send_sem=send_sems.at[j], recv_sem=recv_sems.at[j],
                                         device_id=(px, py, pc), device_id_type=MESH).wait_recv()
        for cp in sends:
            cp.wait_send()
        mine.wait()

    return pl.pallas_call(
        body, name=name, out_shape=jax.ShapeDtypeStruct((8, R, N), F32),
        in_specs=[pl.BlockSpec(memory_space=pltpu.VMEM)], out_specs=pl.BlockSpec(memory_space=pltpu.VMEM),
        scratch_shapes=[pltpu.SemaphoreType.DMA((7,)), pltpu.SemaphoreType.DMA((7,)), pltpu.SemaphoreType.DMA],
        compiler_params=_params(),
    )(blk)


def _piece(rows, piece):
    i, n, k = piece if len(piece) == 3 else (piece[0], piece[1], 1)
    assert rows % 16 == 0 and rows // 16 >= n, (rows, piece)
    lo, hi = (rows // 16 * i // n) * 16, (rows // 16 * (i + k) // n) * 16
    return pl.ds(lo, hi - lo)


def _scatter_plan(arrs, piece=(0, 1), into=None):
    n = len(arrs)

    def copies(ins, outs, sems):
        send_sems, recv_sems = sems
        x, y, c = _place()
        chips = _other_chips(x, y)
        cps = []
        for k in range(n):
            rows = _piece(arrs[k].shape[1], piece)
            for j, (px, py) in enumerate(chips):
                cps.append(pltpu.make_async_remote_copy(
                    src_ref=ins[k].at[2 * px + py, rows], dst_ref=outs[k].at[j, rows],
                    send_sem=send_sems.at[3 * k + j], recv_sem=recv_sems.at[3 * k + j],
                    device_id=(px, py, c), device_id_type=MESH))
        return cps

    def start(ins, outs, sems):
        for cp in copies(ins, outs, sems):
            cp.start()

    def finish(ins, outs, sems):
        for cp in copies(ins, outs, sems):
            cp.wait()

    return _Plan(list(arrs) + list(into or []), [jax.ShapeDtypeStruct((3,) + a.shape[1:], a.dtype) for a in arrs],
                 [pltpu.SemaphoreType.DMA((3 * n,))] * 2, start, finish,
                 aliases={n + k: k for k in range(n)} if into else None)


def _gather_plan(shards, piece=(0, 1), into=None, ici=True):
    n = len(shards)

    def parts(ins, outs, sems):
        s1, r1, s2, r2, loc = sems
        x, y, c = _place()
        me = 2 * x + y
        chips = _other_chips(x, y)
        sib = (x, y, 1 - c)

        def rows(k):
            return _piece(shards[k].shape[1], piece)

        def ici_copy(k, j, slab, to):
            return pltpu.make_async_remote_copy(src_ref=ins[k].at[c, rows(k)], dst_ref=outs[k].at[slab, c, rows(k)],
                                                send_sem=s1.at[3 * k + j], recv_sem=r1.at[3 * k + j],
                                                device_id=to, device_id_type=MESH)

        def d2d(k, j, slab, half):
            return pltpu.make_async_remote_copy(src_ref=outs[k].at[slab, half, rows(k)],
                                                dst_ref=outs[k].at[slab, half, rows(k)],
                                                send_sem=s2.at[3 * k + j], recv_sem=r2.at[3 * k + j],
                                                device_id=sib, device_id_type=MESH)

        def own(k):
            return pltpu.make_async_remote_copy(src_ref=ins[k].at[:, rows(k)], dst_ref=outs[k].at[me, :, rows(k)],
                                                send_sem=loc.at[2 * k], recv_sem=loc.at[2 * k + 1],
                                                device_id=sib, device_id_type=MESH)

        return c, me, chips, ici_copy, d2d, own

    def start(ins, outs, sems):
        c, me, chips, ici_copy, d2d, own = parts(ins, outs, sems)
        for k in range(n):
            for j, (px, py) in enumerate(chips):
                (ici_copy(k, j, me, (px, py, c)) if ici else d2d(k, j, 2 * px + py, c)).start()
        for k in range(n):
            own(k).start()

    def finish(ins, outs, sems):
        c, me, chips, ici_copy, d2d, own = parts(ins, outs, sems)
        if ici:
            for k in range(n):
                for j, (px, py) in enumerate(chips):
                    ici_copy(k, j, 2 * px + py, (px, py, c)).wait_recv()
                    d2d(k, j, 2 * px + py, c).start()
        for k in range(n):
            for j, (px, py) in enumerate(chips):
                d2d(k, j, 2 * px + py, 1 - c).wait_recv()
        for k in range(n):
            own(k).wait()
            for j, (px, py) in enumerate(chips):
                if ici:
                    ici_copy(k, j, me, (px, py, c)).wait_send()
                d2d(k, j, 2 * px + py, c).wait_send()

    return _Plan(list(shards) + list(into or []), [jax.ShapeDtypeStruct((4,) + a.shape, a.dtype) for a in shards],
                 [pltpu.SemaphoreType.DMA((3 * n,))] * 4 + [pltpu.SemaphoreType.DMA((2 * n,))], start, finish,
                 aliases={n + k: k for k in range(n)} if into else None)


def _pair_plan(parts):
    n = len(parts)

    def copies(ins, outs, sems):
        send_sems, recv_sems = sems
        x, y, c = _place()
        return [pltpu.make_async_remote_copy(src_ref=ins[k].at[p, 1 - c], dst_ref=outs[k].at[p],
                                             send_sem=send_sems.at[4 * k + p], recv_sem=recv_sems.at[4 * k + p],
                                             device_id=(x, y, 1 - c), device_id_type=MESH)
                for k in range(n) for p in range(4)]

    def start(ins, outs, sems):
        for cp in copies(ins, outs, sems):
            cp.start()

    def finish(ins, outs, sems):
        for cp in copies(ins, outs, sems):
            cp.wait()

    return _Plan(parts, [jax.ShapeDtypeStruct((4,) + a.shape[2:], a.dtype) for a in parts],
                 [pltpu.SemaphoreType.DMA((4 * n,))] * 2, start, finish)


def _sibling_plan(arrs):
    n = len(arrs)

    def copies(ins, outs, sems):
        send_sems, recv_sems = sems
        x, y, c = _place()
        return [pltpu.make_async_remote_copy(src_ref=ins[k], dst_ref=outs[k], send_sem=send_sems.at[k],
                                             recv_sem=recv_sems.at[k], device_id=(x, y, 1 - c), device_id_type=MESH)
                for k in range(n)]

    def start(ins, outs, sems):
        for cp in copies(ins, outs, sems):
            cp.start()

    def finish(ins, outs, sems):
        for cp in copies(ins, outs, sems):
            cp.wait()

    return _Plan(arrs, [jax.ShapeDtypeStruct(a.shape, a.dtype) for a in arrs],
                 [pltpu.SemaphoreType.DMA((n,))] * 2, start, finish)


def _scatter_copies(arrs):
    def copies(ins, land, send_sems, recv_sems):
        x, y, c = _place()
        return [pltpu.make_async_remote_copy(src_ref=ins[k].at[2 * px + py], dst_ref=land[k].at[j],
                                             send_sem=send_sems.at[3 * k + j], recv_sem=recv_sems.at[3 * k + j],
                                             device_id=(px, py, c), device_id_type=MESH)
                for k in range(len(arrs)) for j, (px, py) in enumerate(_other_chips(x, y))]

    return copies, [lax.empty((3,) + a.shape[1:], a.dtype) for a in arrs]


def _gather_copies(shards):
    def copies(ins, land, send_sems, recv_sems):
        x, y, c = _place()
        return [pltpu.make_async_remote_copy(src_ref=ins[k].at[c], dst_ref=land[k].at[2 * x + y, c],
                                             send_sem=send_sems.at[3 * k + j], recv_sem=recv_sems.at[3 * k + j],
                                             device_id=(px, py, c), device_id_type=MESH)
                for k in range(len(shards)) for j, (px, py) in enumerate(_other_chips(x, y))]

    return copies, [lax.empty((4,) + a.shape, a.dtype) for a in shards]


def _split_start(arrs, copies_lands, ride, name, after=()):
    copies, lands = copies_lands
    n = len(arrs)
    rides = list(ride) if isinstance(ride, (list, tuple)) else [ride]
    n_thru = 2 * n + len(rides)

    def body(*refs):
        first_out = n_thru + len(after)
        for cp in copies(refs[:n], refs[n:2 * n], refs[first_out], refs[first_out + 1]):
            cp.start()

    hbm = [pltpu.with_memory_space_constraint(a, pltpu.HBM) for a in list(arrs) + lands + rides]
    res = pl.pallas_call(
        body, name=name,
        out_shape=[pltpu.SemaphoreType.DMA((3 * n,)), pltpu.SemaphoreType.DMA((3 * n,))]
        + [pltpu.HBM(a.shape, a.dtype) for a in hbm],
        in_specs=[HBM_SPEC] * n_thru + [ANY] * len(after),
        out_specs=[SEM_SPEC, SEM_SPEC] + [HBM_SPEC] * n_thru,
        input_output_aliases={i: 2 + i for i in range(n_thru)},
        compiler_params=pltpu.CompilerParams(has_side_effects=pltpu.SideEffectType.DATAFLOW_SIDE_EFFECTING),
    )(*hbm, *after)
    return res[0], res[1], res[2:2 + n], res[2 + n:2 + 2 * n], list(res[2 + 2 * n:])


def _split_wait(started, copies_lands, after, name):
    send_sems, recv_sems, arrs, lands, _ = started
    copies = copies_lands[0]
    n = len(arrs)

    def body(*refs):
        for cp in copies(refs[:n], refs[n:2 * n], refs[2 * n], refs[2 * n + 1]):
            cp.wait_send()
            cp.wait_recv()

    res = pl.pallas_call(
        body, name=name, out_shape=[pltpu.HBM(a.shape, a.dtype) for a in list(arrs) + list(lands)],
        in_specs=[HBM_SPEC] * (2 * n) + [SEM_SPEC, SEM_SPEC] + [ANY] * len(after), out_specs=[HBM_SPEC] * (2 * n),
        input_output_aliases={i: i for i in range(2 * n)},
        compiler_params=pltpu.CompilerParams(has_side_effects=pltpu.SideEffectType.DATAFLOW_SIDE_EFFECTING),
    )(*arrs, *lands, send_sems, recv_sems, *after)
    return list(res[:n]), list(res[n:])


def _join_plans(plans):
    def split(seq, counts):
        out, at = [], 0
        for cnt in counts:
            out.append(seq[at:at + cnt])
            at += cnt
        return out

    n_i, n_o, n_s = ([len(getattr(p, f)) for p in plans] for f in ("ins", "outs", "sems"))

    def start(ins, outs, sems):
        for p, i, o, s in zip(plans, split(ins, n_i), split(outs, n_o), split(sems, n_s)):
            p.start(i, o, s)

    def finish(ins, outs, sems):
        for p, i, o, s in zip(plans, split(ins, n_i), split(outs, n_o), split(sems, n_s)):
            p.finish(i, o, s)

    aliases, at_i, at_o = {}, 0, 0
    for p in plans:
        aliases.update(p.io_aliases(at_i, at_o))
        at_i, at_o = at_i + len(p.ins), at_o + len(p.outs)
    return _Plan(sum((p.ins for p in plans), []), sum((p.outs for p in plans), []), sum((p.sems for p in plans), []),
                 start, finish, aliases)


def _add_pair(parts, sib, core, name):
    P4, _, Rh, C = parts.shape
    tm, tc = _tile2(Rh, C, 16)

    def body(c_ref, a_ref, b_ref, o_ref):
        o_ref[...] = (a_ref[0].astype(F32) + b_ref[...].astype(F32)).astype(BF16)

    spec = pl.BlockSpec((1, tm, tc), lambda p, i, j, c_ref: (p, i, j))
    return pl.pallas_call(
        body, name=name, out_shape=jax.ShapeDtypeStruct((P4, Rh, C), BF16),
        grid_spec=pltpu.PrefetchScalarGridSpec(
            num_scalar_prefetch=1, grid=(P4, Rh // tm, C // tc),
            in_specs=[pl.BlockSpec((1, 1, tm, tc), lambda p, i, j, c_ref: (p, c_ref[0], i, j)), spec], out_specs=spec),
        compiler_params=_params(("parallel",) * 3),
    )(core, parts, sib)


def _sum_slabs(pre, recv, chip, name):
    _, Rh, C = pre.shape
    tm, tc = _tile2(Rh, C, 16)

    def body(me_ref, own_ref, r_ref, o_ref):
        acc = own_ref[0].astype(F32)
        for j in range(3):
            acc = acc + r_ref[j].astype(F32)
        o_ref[...] = acc

    return pl.pallas_call(
        body, name=name, out_shape=jax.ShapeDtypeStruct((Rh, C), F32),
        grid_spec=pltpu.PrefetchScalarGridSpec(
            num_scalar_prefetch=1, grid=(Rh // tm, C // tc),
            in_specs=[pl.BlockSpec((1, tm, tc), lambda i, j, me_ref: (me_ref[0], i, j)),
                      pl.BlockSpec((3, tm, tc), lambda i, j, me_ref: (0, i, j))],
            out_specs=pl.BlockSpec((tm, tc), lambda i, j, me_ref: (i, j))),
        compiler_params=_params(("parallel", "parallel")),
    )(chip, pre, recv)


def kernel(x, c, positions, w_ada, b_ada, w_in, g_q_a, w_q_b, g_kv_a, w_kv_b, w_o_a, w_conv, w_o_b, w_o, ln1_g, ln1_b, w_ffn_in, w_ffn_out, ln2_g, ln2_b, loss_target, m_w_ada, m_b_ada, m_w_in, m_g_q_a, m_w_q_b, m_g_kv_a, m_w_kv_b, m_w_o_a, m_w_conv, m_w_o_b, m_w_o, m_ln1_g, m_ln1_b, m_w_ffn_in, m_w_ffn_out, m_ln2_g, m_ln2_b, v_w_ada, v_b_ada, v_w_in, v_g_q_a, v_w_q_b, v_g_kv_a, v_w_kv_b, v_w_o_a, v_w_conv, v_w_o_b, v_w_o, v_ln1_g, v_ln1_b, v_w_ffn_in, v_w_ffn_out, v_ln2_g, v_ln2_b):
    S, D = x.shape[1], x.shape[2]
    F = w_ffn_out.shape[1] * 4
    ax, ay, ac = _place()
    chip = 2 * ax + ay
    dev = 4 * ax + 2 * ay + ac
    x2, tgt = x[0], loss_target[0]
    w_ada2, w_in2, w_q_b2, w_kv_b2 = w_ada[0], w_in[0], w_q_b[0], w_kv_b[0]
    w_o_a2, w_o_b2, w_o2, w_ffn_in2, w_ffn_out2 = w_o_a[0], w_o_b[0], w_o[0], w_ffn_in[0], w_ffn_out[0]
    NA = w_ada2.shape[1]
    CW = w_conv.shape[2]

    inv_freq = 1.0 / (ROPE_THETA ** (jnp.arange(0, QK_ROPE, 2, dtype=F32) / QK_ROPE))
    ang = positions[0].astype(F32)[:, None] * inv_freq
    cos, sin = jnp.cos(ang), jnp.sin(ang)
    z32, z64, z96 = jnp.zeros((S, 32), F32), jnp.zeros((S, 64), F32), jnp.zeros((S, 96), F32)
    tab = jnp.concatenate([cos, cos, z64, -sin, z96, z32, sin, z64], axis=1)

    def halves(a):
        return a.reshape(2, a.shape[0] // 2, a.shape[1])

    def whole(g):
        return g.reshape(4, 2 * g.shape[2], g.shape[3])

    def cols(g):
        return jnp.transpose(g, (1, 0, 2)).reshape(g.shape[1], 4 * g.shape[2])

    w_inT, m_w_inT, v_w_inT = w_in2.T, m_w_in[0].T, v_w_in[0].T
    CS = w_inT.shape[0]
    CSP = -(-CS // 32) * 32
    sh_in = halves(jnp.pad(w_inT.astype(BF16), ((0, CSP - CS), (0, 0))))
    sh_qb, sh_kvb, sh_oa, sh_ob, sh_o, sh_fi, sh_fo = (
        halves(w.astype(BF16)) for w in (w_q_b2, w_kv_b2, w_o_a2, w_o_b2, w_o2, w_ffn_in2, w_ffn_out2))
    c_all = _all_gather8(c, "gather_c").reshape(8, D)
    wconv_all = _all_gather8(w_conv[0], "gather_wconv")
    w_conv_full = jnp.transpose(wconv_all[0::2], (1, 0, 2)).reshape(3, D)
    b_sh = lax.dynamic_slice(b_ada, (0, chip * NA), (1, NA))
    mod_sh = _ada_fwd(c_all, w_ada2, b_sh)
    mod_all = _all_gather8(mod_sh, "gather_mod")
    mod = lax.dynamic_slice(mod_all[0::2], (0, dev, 0), (4, 1, NA)).reshape(6, D)
    shift1, scale1, gate1, shift2, scale2, gate2 = (mod[k:k + 1] for k in range(6))

    g_in, shift1, w_conv_full = _run_plan(_gather_plan([sh_in]), "gather_first", ride=[shift1, w_conv_full])
    g_in = whole(g_in)
    sh_a1, sh_a2 = [sh_qb, sh_kvb], [sh_oa, sh_ob, sh_o]
    cl_a1, cl_a2, cl_fi, cl_fo = (_gather_copies(g) for g in (sh_a1, sh_a2, [sh_fi], [sh_fo]))
    st_a1 = _split_start(sh_a1, cl_a1, shift1, "gather_a1_start")
    st_a2 = _split_start(sh_a2, cl_a2, st_a1[4], "gather_a2_start")
    shift1 = st_a2[4][0]

    def in_rows(lo, hi):
        parts = [g_in[p, max(lo, p * CS) - p * CS:min(hi, (p + 1) * CS) - p * CS]
                 for p in range(4) if max(lo, p * CS) < min(hi, (p + 1) * CS)]
        return parts[0] if len(parts) == 1 else jnp.concatenate(parts, axis=0)

    n_qkv = Q_LORA + KV_LORA + QK_ROPE
    W_qkvT = jnp.pad(in_rows(0, n_qkv), ((0, QKV_A - n_qkv), (0, 0)))
    W_convT = in_rows(n_qkv, n_qkv + 3 * D)
    W_gateT = in_rows(n_qkv + 3 * D, n_qkv + 5 * D)

    u = _modulate(x2, scale1, shift1, "modulate1")
    pq = _matmul(u, W_qkvT, "nt", F32, "proj_qkv")
    pc = _matmul(u, W_convT, "nt", BF16, "proj_conv")
    sh_a1, la1 = _split_wait(st_a1, cl_a1, [pc], "gather_a1_wait")
    pg, (g_qb, g_kvb) = _matmul(u, W_gateT, "nt", BF16, "proj_gate", carry=_gather_plan(sh_a1, into=la1, ici=False))
    st_fi = _split_start([sh_fi], cl_fi, g_q_a, "gather_fi_start", after=[pg])
    W_qb = jnp.pad(cols(whole(g_qb)).reshape(Q_LORA, N_HEADS, QK_NOPE + QK_ROPE),
                   ((0, 0), (0, 0), (0, QK_PAD - QK_NOPE - QK_ROPE))).reshape(Q_LORA, N_HEADS * QK_PAD)
    W_kvb = cols(whole(g_kvb))
    rq, rkv, kr = _rms_fwd(pq, tab, st_fi[4][0], g_kv_a)
    kv = _matmul(rkv, W_kvb, "nn", BF16, "kv_b")
    sh_a2, la2 = _split_wait(st_a2, cl_a2, [kv], "gather_a2_wait")
    def rope_heads(r, t):
        return jnp.concatenate([r[:, lo:lo + 128] if lo % QK_PAD == 0 else _rope(r[:, lo:lo + 128], t, 1)
                                for lo in range(0, r.shape[1], 128)], axis=1)

    q, (g_oa, g_ob, g_o) = _matmul(rq, W_qb, "nn", BF16, "q_b", carry=_gather_plan(sh_a2, into=la2, ici=False),
                                   finish=(rope_heads, tab))
    o, lse, _ = _attn_fwd(q, kv, kr)
    W_oa, W_ob, W_o = (g.reshape(-1, D) for g in (g_oa, g_ob, g_o))
    hb = _conv_fwd(pc, w_conv_full)
    sh_fi_t, lfi = _split_wait(st_fi, cl_fi, [o], "gather_fi_wait")
    y_b, g_fi = _matmul(hb, W_ob, "nn", BF16, "o_b", carry=_gather_plan(sh_fi_t, (0, 2), into=lfi, ici=False))
    y_a, (g_fi,) = _matmul(o, W_oa, "nn", BF16, "o_a", carry=_gather_plan(sh_fi_t, (1, 2), into=g_fi, ici=False))
    st_fo = _split_start([sh_fo], cl_fo, ln1_g, "gather_fo_start", after=[y_b])
    merged = _merge_fwd(y_a, y_b, pg)
    mix = _matmul(merged, W_o, "nn", F32, "w_o")
    W_fi = whole(g_fi)
    x1, u2 = _ln1_fwd(x2, mix, gate1, st_fo[4][0], ln1_b, scale2, shift2)
    hh = _matmul(u2, W_fi, "nn", BF16, "ffn_in", shards="b")
    sh_fo_t, lfo = _split_wait(st_fo, cl_fo, [hh], "gather_fo_wait")
    act, (g_fo,) = _swiglu_fwd(hh, carry=_gather_plan(sh_fo_t, into=lfo, ici=False))
    W_fo = g_fo.reshape(F, D)
    ffn = _matmul(act, W_fo, "nn", F32, "ffn_out")

    core_i = ac.astype(jnp.int32).reshape(1)
    chip_i = chip.astype(jnp.int32).reshape(1)

    def uncols(g):
        return jnp.transpose(g.reshape(g.shape[0], 4, g.shape[1] // 4), (1, 0, 2))

    def slabs(p):
        return p.reshape(4, 2, p.shape[1] // 2, p.shape[2])

    def add_pairs(parts, sibs, nms):
        return [_add_pair(a, b, core_i, "add_pair_" + nm) for a, b, nm in zip(parts, sibs, nms)]

    def sum_all(pre, recv, nms):
        return [_sum_slabs(a, r, chip_i, "sum_slabs_" + nm) for a, r, nm in zip(pre, recv, nms)]

    dffn, dx1a, loss_acc, d_ln2_g, d_ln2_b, d_gate2 = _ln2_loss_bwd(x1, ffn, gate2, ln2_g, ln2_b, tgt)
    loss = lax.psum(loss_acc[0, 0], ("x", "y", "c"))
    dW_fo = _matmul(act, dffn, "tn", BF16, "d_w_ffn_out")
    p_fo = [slabs(dW_fo.reshape(4, -1, D))]
    dact, s_fo = _matmul(dffn, W_fo, "nt", BF16, "d_act", carry=_pair_plan(p_fo))
    pre_fo = add_pairs(p_fo, s_fo, ["w_ffn_out"])
    cs_fo = _scatter_copies(pre_fo)
    st_sfo = _split_start(pre_fo, cs_fo, scale2, "scatter_fo_start")
    dhh = _swiglu_bwd(dact, hh)
    dW_fi = _matmul(u2, dhh, "tn", BF16, "d_w_ffn_in", shards="o")
    p_fi = [slabs(dW_fi)]
    du2, s_fi = _matmul(dhh, W_fi, "nt", F32, "d_u2", carry=_pair_plan(p_fi), shards="b")
    pre_fi = add_pairs(p_fi, s_fi, ["w_ffn_in"])
    cs_fi = _scatter_copies(pre_fi)
    st_sfi = _split_start(pre_fi, cs_fi, st_sfo[4], "scatter_fi_start")
    dmix, dxa, d_shift2, d_scale2, d_ln1_g, d_ln1_b, d_gate1 = _ln1_bwd(x2, mix, dx1a, du2, gate1, ln1_g, ln1_b, st_sfi[4][0])
    dW_o = _matmul(merged, dmix, "tn", BF16, "d_w_o")
    dmerged = _matmul(dmix, W_o, "nt", BF16, "d_merged")
    dy_a, dy_b, dgate = _merge_bwd(dmerged, y_a, y_b, pg)
    dW_oa = _matmul(o, dy_a, "tn", BF16, "d_w_o_a")
    do = _matmul(dy_a, W_oa, "nt", BF16, "d_o")
    dW_ob = _matmul(hb, dy_b, "tn", BF16, "d_w_o_b")
    p_mid = [slabs(g.reshape(4, -1, D)) for g in (dW_oa, dW_ob, dW_o)]
    dhb, s_mid = _matmul(dy_b, W_ob, "nt", BF16, "d_hb", carry=_pair_plan(p_mid))
    pre_mid = add_pairs(p_mid, s_mid, ["w_o_a", "w_o_b", "w_o"])
    cs_mid = _scatter_copies(pre_mid)
    st_smid = _split_start(pre_mid, cs_mid, w_conv_full, "scatter_mid_start")
    dconv, d_wconv = _conv_bwd(dhb, pc, st_smid[4][0])
    dq, dkv, dkr, _ = _attn_bwd(q, kv, kr, do, o, lse, tab, carry=_token_plan(st_smid[4][0]))
    names_a = ["w_ffn_out", "w_ffn_in", "w_o_a", "w_o_b", "w_o"]
    dW_qb = _matmul(rq, dq, "tn", BF16, "d_w_q_b")
    d_rq = _matmul(dq, W_qb, "nt", F32, "d_rq")
    dW_kvb = _matmul(rkv, dkv, "tn", BF16, "d_w_kv_b")
    d_rkv = _matmul(dkv, W_kvb, "nt", F32, "d_rkv")
    dqkv, d_g_q, d_g_kv = _rms_bwd(d_rq, d_rkv, pq, dkr, g_q_a, g_kv_a)
    dW_qkvT = _matmul(dqkv, u, "tn", BF16, "d_w_qkv")
    dW_convT = _matmul(dconv, u, "tn", BF16, "d_w_conv")
    dW_gateT = _matmul(dgate, u, "tn", BF16, "d_w_gate")
    pre_fo, r_fo = _split_wait(st_sfo, cs_fo, [dW_qkvT], "scatter_fo_wait")
    pre_fi, r_fi = _split_wait(st_sfi, cs_fi, [dW_qkvT], "scatter_fi_wait")
    pre_mid, r_mid = _split_wait(st_smid, cs_mid, [dW_qkvT], "scatter_mid_wait")
    fin_a = sum_all(pre_fo + pre_fi + pre_mid, r_fo + r_fi + r_mid, names_a)
    srcs = [(0, dW_qkvT[:n_qkv]), (n_qkv, dW_convT), (n_qkv + 3 * D, dW_gateT)]
    rows_of = []
    for p in range(4):
        for lo, src in srcs:
            a, b = max(lo, p * CS), min(lo + src.shape[0], (p + 1) * CS)
            if a < b:
                rows_of.append(src[a - lo:b - lo])
        rows_of.append(jnp.zeros((CSP - CS, D), BF16))
    dW_inT = jnp.concatenate(rows_of, axis=0).reshape(4, CSP, D)
    dW_qb_u = dW_qb.reshape(Q_LORA, N_HEADS, QK_PAD)[:, :, :QK_NOPE + QK_ROPE].reshape(Q_LORA, -1)
    names_b = ["w_in", "w_q_b", "w_kv_b"]
    p_b = [slabs(dW_inT), slabs(uncols(dW_qb_u)), slabs(uncols(dW_kvb))]
    du, s_b = _matmul(dqkv, W_qkvT, "nn", F32, "d_u_qkv", carry=_pair_plan(p_b))
    pre_b = add_pairs(p_b, s_b, names_b)
    cs_b = _scatter_copies(pre_b)
    st_b = _split_start(pre_b, cs_b, scale1, "scatter_last_start")
    du, fs_a = _matmul(dconv, W_convT, "nn", F32, "d_u_conv", add=du, carry=_sibling_plan(fin_a))
    du = _matmul(dgate, W_gateT, "nn", F32, "d_u_gate", add=du)
    grad_x, d_shift1, d_scale1 = _dx_final(dxa, du, x2, st_b[4][0])

    big = {}
    ws = dict(w_in=(w_inT, m_w_inT, v_w_inT), w_q_b=(w_q_b2, m_w_q_b[0], v_w_q_b[0]),
              w_kv_b=(w_kv_b2, m_w_kv_b[0], v_w_kv_b[0]), w_o_a=(w_o_a2, m_w_o_a[0], v_w_o_a[0]),
              w_o_b=(w_o_b2, m_w_o_b[0], v_w_o_b[0]), w_o=(w_o2, m_w_o[0], v_w_o[0]),
              w_ffn_in=(w_ffn_in2, m_w_ffn_in[0], v_w_ffn_in[0]), w_ffn_out=(w_ffn_out2, m_w_ffn_out[0], v_w_ffn_out[0]))

    def adam_of(nm, a, b, carry=None):
        w_, m_, v_ = ws[nm]
        return _adam_halves("adam_" + nm, w_, m_, v_, a, b, core_i, carry)

    for nm, a, b in zip(names_a, fin_a, fs_a):
        big[nm] = adam_of(nm, a, b, _token_plan(st_b[4][0]))[0]
    done = [big[nm][1] for nm in names_a] + [grad_x]
    pre_b, r_b = _split_wait(st_b, cs_b, done, "scatter_last_wait")
    fin_b = sum_all(pre_b, r_b, names_b)
    fs_b = _run_plan(_sibling_plan(fin_b), "sibling_last")
    for nm, a, b in zip(names_b, fin_b, fs_b):
        big[nm] = adam_of(nm, a, b)

    def pad_d(v):
        return jnp.pad(v, ((0, 0), (0, D - v.shape[1])))

    small = _pack_rows([d_ln1_g, d_ln1_b, d_ln2_g, d_ln2_b, pad_d(d_g_q), pad_d(d_g_kv), d_wconv,
                         d_shift1, d_scale1, d_gate1, d_shift2, d_scale2, d_gate2], 16, after=[pre_b[1]])
    small_all = _all_gather8(small, "gather_small")
    small_sum = _sum8(small_all)
    g_ln1_g, g_ln1_b, g_ln2_g, g_ln2_b = (small_sum[k:k + 1] for k in range(4))
    g_g_q, g_g_kv = small_sum[4:5, :Q_LORA], small_sum[5:6, :KV_LORA]
    g_wconv = lax.dynamic_slice(small_sum[6:9], (0, chip * CW), (3, CW))
    g_b_ada = small_sum[9:15].reshape(1, 6 * D)
    dmod_all = small_all[:, 9:15, :].reshape(8, 6 * D)
    g_w_ada = _ada_bwd(c_all, lax.dynamic_slice(dmod_all, (0, chip * NA), (8, NA)))
    big["w_ada"] = [g_w_ada] + list(_adam("adam_w_ada", w_ada2, m_w_ada[0], v_w_ada[0], g_w_ada))
    sm = {}
    for nm, w_, m_, v_, g_ in [("b_ada", b_ada, m_b_ada, v_b_ada, g_b_ada), ("g_q_a", g_q_a, m_g_q_a, v_g_q_a, g_g_q),
                               ("g_kv_a", g_kv_a, m_g_kv_a, v_g_kv_a, g_g_kv),
                               ("w_conv", w_conv[0], m_w_conv[0], v_w_conv[0], g_wconv),
                               ("ln1_g", ln1_g, m_ln1_g, v_ln1_g, g_ln1_g), ("ln1_b", ln1_b, m_ln1_b, v_ln1_b, g_ln1_b),
                               ("ln2_g", ln2_g, m_ln2_g, v_ln2_g, g_ln2_g), ("ln2_b", ln2_b, m_ln2_b, v_ln2_b, g_ln2_b)]:
        sm[nm] = (g_,) + tuple(_adam_small("adam_" + nm, w_, m_, v_, g_))

    order = ["w_ada", "b_ada", "w_in", "g_q_a", "w_q_b", "g_kv_a", "w_kv_b", "w_o_a", "w_conv", "w_o_b", "w_o",
             "ln1_g", "ln1_b", "w_ffn_in", "w_ffn_out", "ln2_g", "ln2_b"]
    lead = {"b_ada", "g_q_a", "g_kv_a", "ln1_g", "ln1_b", "ln2_g", "ln2_b"}

    def leaf(nm, k):
        val = big[nm][k] if nm in big else sm[nm][k]
        if nm == "w_in":
            val = val.T
        return val if nm in lead else val[None]

    outs = [loss, grad_x[None]]
    for k in range(4):
        outs += [leaf(nm, k) for nm in order]
    return tuple(outs)
```

```python
import jax
import jax.numpy as jnp
from jax import lax
from jax.experimental import pallas as pl
from jax.experimental.pallas import tpu as pltpu

F32, BF16 = jnp.float32, jnp.bfloat16
N_HEADS, QK_NOPE, QK_ROPE, V_HEAD = 16, 128, 64, 128
Q_LORA, KV_LORA = 512, 512
QK_PAD = 256
QKV_A = 1152
CHUNK_SHIFT = 6
ATTN_SCALE = (QK_NOPE + QK_ROPE) ** -0.5
LOG2E = 1.4426950408889634
SCALE2 = ATTN_SCALE * LOG2E
ROPE_THETA = 10000.0
ALPHA = 2.0 ** 0.25
LN_EPS, RMS_EPS = 1e-5, 1e-6
ADAM_LR, ADAM_B1, ADAM_B2, ADAM_EPS, ADAM_WD, ADAM_STEP = 0.001, 0.9, 0.999, 1e-08, 0.01, 10
ADAM_C1 = 1.0 - ADAM_B1 ** ADAM_STEP
ADAM_C2 = 1.0 - ADAM_B2 ** ADAM_STEP
VMEM_LIMIT = 56 * 1024 * 1024
MESH = pl.DeviceIdType.MESH
ANY = pl.BlockSpec(memory_space=pl.ANY)
HBM_SPEC = pl.BlockSpec(memory_space=pltpu.HBM)
SEM_SPEC = pl.BlockSpec(memory_space=pltpu.SEMAPHORE)
NT = (((1,), (1,)), ((), ()))
TN = (((0,), (0,)), ((), ()))
NN = (((1,), (0,)), ((), ()))


def _params(sem=None):
    return pltpu.CompilerParams(dimension_semantics=sem, vmem_limit_bytes=VMEM_LIMIT)


def _pick(n, cands=(1408, 1024, 512, 384, 256, 128)):
    for t in cands:
        if n % t == 0:
            return t
    return n


def _row_tile(rows, row_bytes, budget, mult=8):
    best = mult
    for t in range(mult, rows + 1, mult):
        if rows % t == 0 and t * row_bytes <= budget:
            best = t
    return best


def _tile2(rows, cols, mult=8, budget=3 << 18):
    col_tiles = [t for t in range(128, cols + 1, 128) if cols % t == 0] or [cols]
    best = None
    for tc in col_tiles:
        for tr in range(mult, rows + 1, mult):
            if rows % tr == 0 and tr * tc <= budget and (best is None or (tr * tc, tc) > (best[0] * best[1], best[1])):
                best = (tr, tc)
    assert best is not None, (rows, cols)
    return best


def _sigmoid(x):
    return jax.nn.sigmoid(x)


class _Plan:
    def __init__(self, ins, outs, sems, start, finish, aliases=None):
        self.ins, self.outs, self.sems, self.start, self.finish = list(ins), list(outs), list(sems), start, finish
        self.aliases = dict(aliases or {})

    def io_aliases(self, first_in, first_out):
        return {first_in + i: first_out + o for i, o in self.aliases.items()}


def _token_plan(token):
    return _Plan([token], [], [], lambda *a: None, lambda *a: None)


def _run_plan(plan, name, ride=None):
    n_in, n_out = len(plan.ins), len(plan.outs)
    extra = [] if ride is None else list(ride)
    aliases = plan.io_aliases(0, 0)
    for k in range(len(extra)):
        aliases[n_in + k] = n_out + k

    def body(*refs):
        ins, outs, sems = refs[:n_in], refs[n_in + len(extra):n_in + len(extra) + n_out], refs[n_in + 2 * len(extra) + n_out:]
        plan.start(ins, outs, sems)
        plan.finish(ins, outs, sems)

    return pl.pallas_call(body, name=name, out_shape=plan.outs + [jax.ShapeDtypeStruct(r.shape, r.dtype) for r in extra],
                          in_specs=[ANY] * (n_in + len(extra)), out_specs=[ANY] * (n_out + len(extra)),
                          scratch_shapes=plan.sems, input_output_aliases=aliases,
                          compiler_params=_params())(*plan.ins, *extra)


def _matmul(a, b, mode, out_dtype, name, add=None, carry=None, shards=None, finish=None):
    if mode == "nn":
        (M, K), N, dims = a.shape, b.shape[-1] * (4 if shards else 1), NN
    elif mode == "nt":
        (M, K), N, dims = a.shape, b.shape[-2], NT
    else:
        (K, M), N, dims = a.shape, b.shape[1], TN
    split_n = shards and mode != "nt"
    tm = _pick(M)
    tn = _pick(N // 4) if split_n else _pick(N)
    deep = (2816, 2048, 1408, 1024, 512, 384, 256, 128)
    if shards and mode == "nt":
        tk = _pick(K // 4, deep)
    else:
        tk = K if K <= 2048 else _pick(K, deep)
    nk = K // tk
    per = (N // 4 // tn) if split_n else (K // 4 // tk if shards else 1)
    a_spec = (pl.BlockSpec((tk, tm), lambda i, j, k: (k, i)) if mode == "tn"
              else pl.BlockSpec((tm, tk), lambda i, j, k: (i, k)))
    if shards == "b" and mode == "nn":
        b_spec = pl.BlockSpec((None, tk, tn), lambda i, j, k: (j // per, k, j % per))
    elif shards == "b":
        b_spec = pl.BlockSpec((None, tn, tk), lambda i, j, k: (k // per, j, k % per))
    else:
        b_spec = (pl.BlockSpec((tn, tk), lambda i, j, k: (j, k)) if mode == "nt"
                  else pl.BlockSpec((tk, tn), lambda i, j, k: (k, j)))
    o_spec = pl.BlockSpec((tm, tn), lambda i, j, k: (i, j))
    o_shape = (M, N)
    if shards == "o":
        o_spec, o_shape = pl.BlockSpec((None, tm, tn), lambda i, j, k: (j // per, i, j % per)), (4, M, N // 4)
    has_add = add is not None
    has_fin = finish is not None
    n_ci = len(carry.ins) if carry else 0
    n_co = len(carry.outs) if carry else 0
    n_in = 2 + has_add + has_fin
    grid = (M // tm, N // tn, nk)

    def body(*refs):
        a_ref, b_ref = refs[0], refs[1]
        add_ref = refs[2] if has_add else None
        fin_ref = refs[2 + has_add] if has_fin else None

        def store(r):
            if has_add:
                r = r + add_ref[...]
            if has_fin:
                r = finish[0](r, fin_ref[...])
            o_ref[...] = r.astype(o_ref.dtype)

        o_ref = refs[n_in + n_ci]
        acc_ref = refs[n_in + n_ci + 1 + n_co] if nk > 1 else None
        c_ins = refs[n_in:n_in + n_ci]
        c_outs = refs[n_in + n_ci + 1:n_in + n_ci + 1 + n_co]
        c_sems = refs[n_in + n_ci + 1 + n_co + (nk > 1):]
        i, j, k = pl.program_id(0), pl.program_id(1), pl.program_id(2)

        if carry:
            @pl.when((i == 0) & (j == 0) & (k == 0))
            def _():
                carry.start(c_ins, c_outs, c_sems)

        part = lax.dot_general(a_ref[...], b_ref[...], dims, preferred_element_type=F32)
        if nk == 1:
            store(part)
        else:
            @pl.when(k == 0)
            def _():
                acc_ref[...] = part

            @pl.when((k > 0) & (k < nk - 1))
            def _():
                acc_ref[...] += part

            @pl.when(k == nk - 1)
            def _():
                store(acc_ref[...] + part)

        if carry:
            @pl.when((i == grid[0] - 1) & (j == grid[1] - 1) & (k == nk - 1))
            def _():
                carry.finish(c_ins, c_outs, c_sems)

    ins = [a, b] + ([add] if has_add else []) + ([finish[1]] if has_fin else []) + (carry.ins if carry else [])
    in_specs = ([a_spec, b_spec] + ([o_spec] if has_add else [])
                + ([pl.BlockSpec((tm, finish[1].shape[1]), lambda i, j, k: (i, 0))] if has_fin else []) + [ANY] * n_ci)
    res = pl.pallas_call(
        body, name=name, grid=grid,
        in_specs=in_specs, out_specs=[o_spec] + [ANY] * n_co,
        out_shape=[jax.ShapeDtypeStruct(o_shape, out_dtype)] + (carry.outs if carry else []),
        scratch_shapes=([pltpu.VMEM((tm, tn), F32)] if nk > 1 else []) + (carry.sems if carry else []),
        input_output_aliases=carry.io_aliases(n_in, 1) if carry else {},
        compiler_params=_params(("arbitrary",) * 3 if carry else ("parallel", "parallel", "arbitrary")),
    )(*ins)
    return (res[0], res[1:]) if carry else res[0]


def _rows(body, name, n_rows, tm, ins, outs, accs=(), carry=None):
    grid = (n_rows // tm,)

    def halo(arr):
        return 16 if arr.dtype == BF16 else 8

    arrays, in_specs = [], []
    for spec in ins:
        kind, arr = spec[0], spec[1]
        arrays.append(arr)
        if kind == "row":
            _, _, cb, w = spec
            in_specs.append(pl.BlockSpec((tm, w), lambda i, cb=cb: (i, cb)))
        elif kind == "full":
            in_specs.append(pl.BlockSpec(arr.shape, lambda i, nd=arr.ndim: (0,) * nd))
        elif kind == "prev":
            _, _, cb, w = spec
            h = halo(arr)
            in_specs.append(pl.BlockSpec((h, w), lambda i, cb=cb, per=tm // h: (jnp.maximum(i * per - 1, 0), cb)))
        else:
            _, _, cb, w = spec
            h = halo(arr)
            in_specs.append(pl.BlockSpec((h, w), lambda i, cb=cb, per=tm // h, last=n_rows // h - 1:
                                         (jnp.minimum((i + 1) * per, last), cb)))
    out_shape = [jax.ShapeDtypeStruct((n_rows, w), dt) for (w, dt) in outs]
    out_specs = [pl.BlockSpec((tm, w), lambda i: (i, 0)) for (w, _) in outs]
    out_shape += [jax.ShapeDtypeStruct(s, F32) for s in accs]
    out_specs += [pl.BlockSpec(s, lambda i, nd=len(s): (0,) * nd) for s in accs]
    n_in, n_out, n_acc = len(ins), len(outs), len(accs)
    n_ci = len(carry.ins) if carry else 0
    n_co = len(carry.outs) if carry else 0

    def kernel_body(*refs):
        first = n_in + n_ci
        c_ins, c_outs, c_sems = refs[n_in:first], refs[first + n_out + n_acc:first + n_out + n_acc + n_co], refs[first + n_out + n_acc + n_co:]
        if carry:
            @pl.when(pl.program_id(0) == 0)
            def _():
                carry.start(c_ins, c_outs, c_sems)

        body(pl.program_id(0), refs[:n_in], refs[first:first + n_out], refs[first + n_out:first + n_out + n_acc])
        if carry:
            @pl.when(pl.program_id(0) == grid[0] - 1)
            def _():
                carry.finish(c_ins, c_outs, c_sems)

    res = pl.pallas_call(
        kernel_body, name=name, grid=grid, in_specs=in_specs + [ANY] * n_ci, out_specs=out_specs + [ANY] * n_co,
        out_shape=out_shape + (carry.outs if carry else []), scratch_shapes=carry.sems if carry else [],
        input_output_aliases=carry.io_aliases(n_in, n_out + n_acc) if carry else {},
        compiler_params=_params(("arbitrary",)),
    )(*arrays, *(carry.ins if carry else []))
    return (res[:n_out + n_acc], res[n_out + n_acc:]) if carry else res


def _acc_add(i, ref, val):
    @pl.when(i == 0)
    def _():
        ref[...] = val

    @pl.when(i > 0)
    def _():
        ref[...] += val


def _rope(t, tab, sign):
    c, sa, sb = tab[:, 0:128], tab[:, 128:256], tab[:, 256:384]
    rot = pltpu.roll(t, 96, 1) * sa + pltpu.roll(t, 32, 1) * sb
    return t * c + rot if sign > 0 else t * c - rot


def _ln_stats(r):
    mu = jnp.mean(r, axis=-1, keepdims=True)
    d = r - mu
    var = jnp.mean(d * d, axis=-1, keepdims=True)
    rstd = lax.rsqrt(var + LN_EPS)
    return d * rstd, rstd


def _ln_bwd(dxh, xh, rstd):
    m1 = jnp.mean(dxh, axis=-1, keepdims=True)
    m2 = jnp.mean(dxh * xh, axis=-1, keepdims=True)
    return rstd * (dxh - m1 - xh * m2)


def _modulate(x, scale, shift, name):
    S, D = x.shape

    def body(i, ins, outs, accs):
        outs[0][...] = (ins[0][...] * (1.0 + ins[1][...]) + ins[2][...]).astype(BF16)

    return _rows(body, name, S, _pick(S, (256, 128)), [("row", x, 0, D), ("full", scale), ("full", shift)], [(D, BF16)])[0]


def _rms_fwd(pq, tab, g_q, g_kv):
    S = pq.shape[0]

    def body(i, ins, outs, accs):
        pq_ref, tab_ref, gq_ref, gkv_ref = ins

        def rms(x, g):
            return x * lax.rsqrt(jnp.mean(x * x, axis=-1, keepdims=True) + RMS_EPS) * g

        outs[0][...] = rms(pq_ref[:, 0:Q_LORA], gq_ref[...]).astype(BF16)
        outs[1][...] = rms(pq_ref[:, Q_LORA:Q_LORA + KV_LORA], gkv_ref[...]).astype(BF16)
        outs[2][...] = _rope(pq_ref[:, Q_LORA + KV_LORA:QKV_A], tab_ref[...], 1).astype(BF16)

    return _rows(body, "rms_fwd", S, _pick(S, (256, 128)),
                 [("row", pq, 0, QKV_A), ("row", tab, 0, 384), ("full", g_q), ("full", g_kv)],
                 [(Q_LORA, BF16), (KV_LORA, BF16), (128, BF16)])


def _allowed(q0, k0, bq):
    row = q0 + lax.broadcasted_iota(jnp.int32, (bq, bq), 0)
    col = k0 + lax.broadcasted_iota(jnp.int32, (bq, bq), 1)
    return (col >> CHUNK_SHIFT) <= (row >> CHUNK_SHIFT)


ATTN_BLOCK = 512


def _attn_fwd(q, kv, kr, carry=None):
    S = q.shape[0]
    bq = min(ATTN_BLOCK, S)
    nq = S // bq
    n_ci = len(carry.ins) if carry else 0
    n_co = len(carry.outs) if carry else 0

    def body(*refs):
        q_ref, kn_ref, v_ref, kr_ref = refs[:4]
        o_ref, lse_ref = refs[4 + n_ci:6 + n_ci]
        c_ins, c_outs = refs[4:4 + n_ci], refs[6 + n_ci:6 + n_ci + n_co]
        kcat = refs[6 + n_ci + n_co]
        c_sems = refs[7 + n_ci + n_co:]
        qi = pl.program_id(1)
        if carry:
            @pl.when((pl.program_id(0) == 0) & (qi == 0))
            def _():
                carry.start(c_ins, c_outs, c_sems)

        @pl.when(qi == 0)
        def _():
            kcat[:, 0:128] = kn_ref[...]
            kcat[:, 128:256] = kr_ref[...]

        qv = q_ref[...]

        def step(j, carry, masked):
            m, l, acc = carry
            off = pl.multiple_of(j * bq, bq)
            s = lax.dot_general(qv, kcat[pl.ds(off, bq), :], NT, preferred_element_type=F32) * SCALE2
            if masked:
                s = jnp.where(_allowed(qi * bq, off, bq), s, -1e30)
            m_new = jnp.maximum(m, jnp.max(s, axis=1, keepdims=True))
            a = jnp.exp2(m - m_new)
            p = jnp.exp2(s - m_new)
            l = a * l + jnp.sum(p, axis=1, keepdims=True)
            acc = a * acc + jnp.dot(p.astype(BF16), v_ref[pl.ds(off, bq), :], preferred_element_type=F32)
            return m_new, l, acc

        init = (jnp.full((bq, 1), -1e30, F32), jnp.zeros((bq, 1), F32), jnp.zeros((bq, V_HEAD), F32))
        below = lax.fori_loop(0, qi, lambda j, cr: step(j, cr, False), init)
        m, l, acc = step(qi, below, True)
        o_ref[...] = (acc / l).astype(BF16)
        lse_ref[0] = m + jnp.log2(l)
        if carry:
            @pl.when((pl.program_id(0) == N_HEADS - 1) & (qi == nq - 1))
            def _():
                carry.finish(c_ins, c_outs, c_sems)

    res = pl.pallas_call(
        body, name="attn_fwd", grid=(N_HEADS, nq),
        in_specs=[pl.BlockSpec((bq, QK_PAD), lambda h, i: (i, h)),
                  pl.BlockSpec((S, 128), lambda h, i: (0, 2 * h)),
                  pl.BlockSpec((S, 128), lambda h, i: (0, 2 * h + 1)),
                  pl.BlockSpec((S, 128), lambda h, i: (0, 0))] + [ANY] * n_ci,
        out_specs=[pl.BlockSpec((bq, V_HEAD), lambda h, i: (i, h)),
                   pl.BlockSpec((1, bq, 1), lambda h, i: (h, i, 0))] + [ANY] * n_co,
        out_shape=[jax.ShapeDtypeStruct((S, N_HEADS * V_HEAD), BF16),
                   jax.ShapeDtypeStruct((N_HEADS, S, 1), F32)] + (carry.outs if carry else []),
        scratch_shapes=[pltpu.VMEM((S, QK_PAD), BF16)] + (carry.sems if carry else []),
        input_output_aliases=carry.io_aliases(4, 2) if carry else {},
        compiler_params=_params(("arbitrary", "arbitrary")),
    )(q, kv, kv, kr, *(carry.ins if carry else []))
    return res[0], res[1], res[2:]


def _attn_bwd(q, kv, kr, do, o, lse, tab, carry=None):
    S = q.shape[0]
    bq = min(ATTN_BLOCK, S)
    nq = S // bq

    n_ci = len(carry.ins) if carry else 0
    n_co = len(carry.outs) if carry else 0

    def body(*refs):
        q_ref, kn_ref, v_ref, kr_ref, do_ref, o_ref, lse_ref, tab_ref = refs[:8]
        dq_ref, dkv_ref, dkr_ref = refs[8 + n_ci:11 + n_ci]
        dq_acc, dk_acc, dv_acc, kcat, delta = refs[11 + n_ci + n_co:16 + n_ci + n_co]
        c_ins, c_outs, c_sems = refs[8:8 + n_ci], refs[11 + n_ci:11 + n_ci + n_co], refs[16 + n_ci + n_co:]
        h = pl.program_id(0)
        if carry:
            @pl.when(h == 0)
            def _():
                carry.start(c_ins, c_outs, c_sems)

        dq_acc[...] = jnp.zeros_like(dq_acc)
        dk_acc[...] = jnp.zeros_like(dk_acc)
        dv_acc[...] = jnp.zeros_like(dv_acc)
        kcat[:, 0:128] = kn_ref[...]
        kcat[:, 128:256] = kr_ref[...]
        for r in range(nq):
            rows = slice(r * bq, (r + 1) * bq)
            delta[rows, :] = jnp.sum(do_ref[rows, :].astype(F32) * o_ref[rows, :].astype(F32), axis=1, keepdims=True)

        def pair(i, j, masked):
            rows_i = pl.ds(pl.multiple_of(i * bq, bq), bq)
            rows_j = pl.ds(pl.multiple_of(j * bq, bq), bq)
            qv, dov, k = q_ref[rows_i, :], do_ref[rows_i, :], kcat[rows_j, :]
            s = lax.dot_general(qv, k, NT, preferred_element_type=F32) * SCALE2
            if masked:
                s = jnp.where(_allowed(i * bq, j * bq, bq), s, -1e30)
            p = jnp.exp2(s - lse_ref[0, rows_i, :])
            dv_acc[rows_j, :] += lax.dot_general(p.astype(BF16), dov, TN, preferred_element_type=F32)
            dp = lax.dot_general(dov, v_ref[rows_j, :], NT, preferred_element_type=F32)
            ds = (p * (dp - delta[rows_i, :]) * ATTN_SCALE).astype(BF16)
            dk_acc[rows_j, :] += lax.dot_general(ds, qv, TN, preferred_element_type=F32)
            dq_acc[rows_i, :] += jnp.dot(ds, k, preferred_element_type=F32)

        def kv_step(j, _):
            pair(j, j, True)

            def q_step(i, _):
                pair(i, j, False)
                return 0

            lax.fori_loop(j + 1, nq, q_step, 0)
            return 0

        lax.fori_loop(0, nq, kv_step, 0)

        for r in range(nq):
            rows = slice(r * bq, (r + 1) * bq)
            dq_ref[rows, 0:128] = dq_acc[rows, 0:128].astype(BF16)
            dq_ref[rows, 128:256] = _rope(dq_acc[rows, 128:256], tab_ref[rows, :], -1).astype(BF16)
        dkv_ref[:, 0:128] = dk_acc[:, 0:128].astype(BF16)
        dkv_ref[:, 128:256] = dv_acc[...].astype(BF16)

        @pl.when(h == 0)
        def _():
            dkr_ref[...] = dk_acc[:, 128:256]

        @pl.when(h > 0)
        def _():
            dkr_ref[...] += dk_acc[:, 128:256]

        @pl.when(h == N_HEADS - 1)
        def _():
            for r in range(nq):
                rows = slice(r * bq, (r + 1) * bq)
                dkr_ref[rows, :] = _rope(dkr_ref[rows, :], tab_ref[rows, :], -1)
            if carry:
                carry.finish(c_ins, c_outs, c_sems)

    W = N_HEADS * QK_PAD
    res = pl.pallas_call(
        body, name="attn_bwd", grid=(N_HEADS,),
        in_specs=[pl.BlockSpec((S, QK_PAD), lambda h: (0, h)),
                  pl.BlockSpec((S, 128), lambda h: (0, 2 * h)),
                  pl.BlockSpec((S, 128), lambda h: (0, 2 * h + 1)),
                  pl.BlockSpec((S, 128), lambda h: (0, 0)),
                  pl.BlockSpec((S, V_HEAD), lambda h: (0, h)),
                  pl.BlockSpec((S, V_HEAD), lambda h: (0, h)),
                  pl.BlockSpec((1, S, 1), lambda h: (h, 0, 0)),
                  pl.BlockSpec((S, 384), lambda h: (0, 0))] + [ANY] * n_ci,
        out_specs=[pl.BlockSpec((S, QK_PAD), lambda h: (0, h)),
                   pl.BlockSpec((S, QK_PAD), lambda h: (0, h)),
                   pl.BlockSpec((S, 128), lambda h: (0, 0))] + [ANY] * n_co,
        out_shape=[jax.ShapeDtypeStruct((S, W), BF16), jax.ShapeDtypeStruct((S, W), BF16),
                   jax.ShapeDtypeStruct((S, 128), F32)] + (carry.outs if carry else []),
        scratch_shapes=[pltpu.VMEM((S, QK_PAD), F32), pltpu.VMEM((S, QK_PAD), F32), pltpu.VMEM((S, V_HEAD), F32),
                        pltpu.VMEM((S, QK_PAD), BF16), pltpu.VMEM((S, 1), F32)]
        + (carry.sems if carry else []),
        input_output_aliases=carry.io_aliases(8, 3) if carry else {},
        compiler_params=_params(("arbitrary",)),
    )(q, kv, kv, kr, do, o, lse, tab, *(carry.ins if carry else []))
    return res[0], res[1], res[2], res[3:]


def _shift_down(cur, prev, i, n):
    tm, h = cur.shape[0], prev.shape[0]
    prev = jnp.where(i == 0, jnp.zeros_like(prev), prev)
    full = jnp.concatenate([prev, cur], axis=0)
    return pltpu.roll(full, n, 0)[h:h + tm, :]


def _shift_up(cur, nxt, i, last, n):
    tm, h = cur.shape[0], nxt.shape[0]
    nxt = jnp.where(i == last, jnp.zeros_like(nxt), nxt)
    full = jnp.concatenate([cur, nxt], axis=0)
    return pltpu.roll(full, tm + h - n, 0)[0:tm, :]


def _conv_fwd(pc, w_conv):
    S, D = pc.shape[0], pc.shape[1] // 3
    tm = _pick(S, (256, 128))

    def body(i, ins, outs, accs):
        b_ref, c_ref, x_ref, cp_ref, xp_ref, w_ref = ins
        z = c_ref[...].astype(F32) * x_ref[...].astype(F32)
        zp = cp_ref[...].astype(F32) * xp_ref[...].astype(F32)
        cz = w_ref[0:1, :] * _shift_down(z, zp, i, 2) + w_ref[1:2, :] * _shift_down(z, zp, i, 1) + w_ref[2:3, :] * z
        outs[0][...] = (b_ref[...].astype(F32) * cz).astype(BF16)

    return _rows(body, "conv_fwd", S, tm,
                 [("row", pc, 0, D), ("row", pc, 1, D), ("row", pc, 2, D), ("prev", pc, 1, D), ("prev", pc, 2, D),
                  ("full", w_conv)], [(D, BF16)])[0]


def _conv_bwd(dhb, pc, w_conv):
    S, D = dhb.shape
    tm = _pick(S, (256, 128))
    last = S // tm - 1

    def body(i, ins, outs, accs):
        g_ref, b_ref, c_ref, x_ref, cp_ref, xp_ref, gn_ref, bn_ref, w_ref = ins
        w0, w1, w2 = w_ref[0:1, :], w_ref[1:2, :], w_ref[2:3, :]
        c, x, g = c_ref[...].astype(F32), x_ref[...].astype(F32), g_ref[...].astype(F32)
        z = c * x
        zp = cp_ref[...].astype(F32) * xp_ref[...].astype(F32)
        z1, z2 = _shift_down(z, zp, i, 1), _shift_down(z, zp, i, 2)
        cz = w0 * z2 + w1 * z1 + w2 * z
        dcz = g * b_ref[...].astype(F32)
        dczn = gn_ref[...].astype(F32) * bn_ref[...].astype(F32)
        dz = w2 * dcz + w1 * _shift_up(dcz, dczn, i, last, 1) + w0 * _shift_up(dcz, dczn, i, last, 2)
        outs[0][:, 0:D] = (g * cz).astype(BF16)
        outs[0][:, D:2 * D] = (dz * x).astype(BF16)
        outs[0][:, 2 * D:3 * D] = (dz * c).astype(BF16)
        dw = jnp.concatenate([jnp.sum(dcz * z2, axis=0, keepdims=True), jnp.sum(dcz * z1, axis=0, keepdims=True),
                              jnp.sum(dcz * z, axis=0, keepdims=True)], axis=0)
        _acc_add(i, accs[0], dw)

    return _rows(body, "conv_bwd", S, tm,
                 [("row", dhb, 0, D), ("row", pc, 0, D), ("row", pc, 1, D), ("row", pc, 2, D),
                  ("prev", pc, 1, D), ("prev", pc, 2, D), ("next", dhb, 0, D), ("next", pc, 0, D), ("full", w_conv)],
                 [(3 * D, BF16)], [(3, D)])


def _merge_fwd(y_a, y_b, pg):
    S, D = y_a.shape

    def body(i, ins, outs, accs):
        ya, yb, ga, gb = ins
        outs[0][...] = (_sigmoid(ga[...].astype(F32)) * ya[...].astype(F32)
                        + _sigmoid(gb[...].astype(F32)) * yb[...].astype(F32)).astype(BF16)

    return _rows(body, "merge_fwd", S, _pick(S, (256, 128)),
                 [("row", y_a, 0, D), ("row", y_b, 0, D), ("row", pg, 0, D), ("row", pg, 1, D)], [(D, BF16)])[0]


def _merge_bwd(dm, y_a, y_b, pg):
    S, D = dm.shape

    def body(i, ins, outs, accs):
        d, ya, yb = ins[0][...].astype(F32), ins[1][...].astype(F32), ins[2][...].astype(F32)
        sa, sb = _sigmoid(ins[3][...].astype(F32)), _sigmoid(ins[4][...].astype(F32))
        outs[0][...] = (d * sa).astype(BF16)
        outs[1][...] = (d * sb).astype(BF16)
        outs[2][:, 0:D] = (d * ya * (sa * (1.0 - sa))).astype(BF16)
        outs[2][:, D:2 * D] = (d * yb * (sb * (1.0 - sb))).astype(BF16)

    return _rows(body, "merge_bwd", S, _pick(S, (256, 128)),
                 [("row", dm, 0, D), ("row", y_a, 0, D), ("row", y_b, 0, D), ("row", pg, 0, D), ("row", pg, 1, D)],
                 [(D, BF16), (D, BF16), (2 * D, BF16)])


def _ln1_fwd(x, mix, gate1, g, b, scale2, shift2):
    S, D = x.shape

    def body(i, ins, outs, accs):
        x_ref, mix_ref, gate_ref, g_ref, b_ref, sc_ref, sh_ref = ins
        xh, _ = _ln_stats(ALPHA * x_ref[...] + gate_ref[...] * mix_ref[...])
        x1 = xh * g_ref[...] + b_ref[...]
        outs[0][...] = x1
        outs[1][...] = (x1 * (1.0 + sc_ref[...]) + sh_ref[...]).astype(BF16)

    return _rows(body, "ln1_fwd", S, _pick(S, (256, 128)),
                 [("row", x, 0, D), ("row", mix, 0, D), ("full", gate1), ("full", g), ("full", b),
                  ("full", scale2), ("full", shift2)], [(D, F32), (D, BF16)])


def _swiglu_fwd(hh, carry=None):
    S, F = hh.shape[0], hh.shape[1] // 2

    def body(i, ins, outs, accs):
        hg = ins[0][...].astype(F32)
        outs[0][...] = (hg * _sigmoid(hg) * ins[1][...].astype(F32)).astype(BF16)

    res = _rows(body, "swiglu_fwd", S, _pick(S, (128,)), [("row", hh, 0, F), ("row", hh, 1, F)], [(F, BF16)], carry=carry)
    return (res[0][0], res[1]) if carry else res[0]


def _swiglu_bwd(dact, hh):
    S, F = dact.shape

    def body(i, ins, outs, accs):
        d, hg, hu = ins[0][...].astype(F32), ins[1][...].astype(F32), ins[2][...].astype(F32)
        sg = _sigmoid(hg)
        outs[0][:, 0:F] = (d * hu * (sg * (1.0 + hg * (1.0 - sg)))).astype(BF16)
        outs[0][:, F:2 * F] = (d * (hg * sg)).astype(BF16)

    return _rows(body, "swiglu_bwd", S, _pick(S, (128,)),
                 [("row", dact, 0, F), ("row", hh, 0, F), ("row", hh, 1, F)], [(2 * F, BF16)])[0]


def _ln2_loss_bwd(x1, ffn, gate2, g, b, target):
    S, D = x1.shape

    def body(i, ins, outs, accs):
        x1_ref, f_ref, gate_ref, g_ref, b_ref, t_ref = ins
        f = f_ref[...]
        xh, rstd = _ln_stats(ALPHA * x1_ref[...] + gate_ref[...] * f)
        e = xh * g_ref[...] + b_ref[...] - t_ref[...]
        dy = e * (1.0 / D)
        dr = _ln_bwd(dy * g_ref[...], xh, rstd)
        outs[0][...] = (gate_ref[...] * dr).astype(BF16)
        outs[1][...] = ALPHA * dr
        _acc_add(i, accs[0], jnp.full((1, 128), (0.5 / D) * jnp.sum(e * e), F32))
        _acc_add(i, accs[1], jnp.sum(dy * xh, axis=0, keepdims=True))
        _acc_add(i, accs[2], jnp.sum(dy, axis=0, keepdims=True))
        _acc_add(i, accs[3], jnp.sum(dr * f, axis=0, keepdims=True))

    return _rows(body, "ln2_loss_bwd", S, _pick(S, (256, 128)),
                 [("row", x1, 0, D), ("row", ffn, 0, D), ("full", gate2), ("full", g), ("full", b), ("row", target, 0, D)],
                 [(D, BF16), (D, F32)], [(1, 128), (1, D), (1, D), (1, D)])


def _ln1_bwd(x, mix, dx1a, du2, gate1, g, b, scale2):
    S, D = x.shape

    def body(i, ins, outs, accs):
        x_ref, mix_ref, da_ref, du_ref, gate_ref, g_ref, b_ref, sc_ref = ins
        mix, du = mix_ref[...], du_ref[...]
        xh, rstd = _ln_stats(ALPHA * x_ref[...] + gate_ref[...] * mix)
        x1 = xh * g_ref[...] + b_ref[...]
        dx1 = da_ref[...] + du * (1.0 + sc_ref[...])
        dr = _ln_bwd(dx1 * g_ref[...], xh, rstd)
        outs[0][...] = (gate_ref[...] * dr).astype(BF16)
        outs[1][...] = ALPHA * dr
        _acc_add(i, accs[0], jnp.sum(du, axis=0, keepdims=True))
        _acc_add(i, accs[1], jnp.sum(du * x1, axis=0, keepdims=True))
        _acc_add(i, accs[2], jnp.sum(dx1 * xh, axis=0, keepdims=True))
        _acc_add(i, accs[3], jnp.sum(dx1, axis=0, keepdims=True))
        _acc_add(i, accs[4], jnp.sum(dr * mix, axis=0, keepdims=True))

    return _rows(body, "ln1_bwd", S, _pick(S, (256, 128)),
                 [("row", x, 0, D), ("row", mix, 0, D), ("row", dx1a, 0, D), ("row", du2, 0, D),
                  ("full", gate1), ("full", g), ("full", b), ("full", scale2)],
                 [(D, BF16), (D, F32)], [(1, D)] * 5)


def _rms_bwd(d_rq, d_rkv, pq, dkr, g_q, g_kv):
    S = pq.shape[0]

    def body(i, ins, outs, accs):
        dq_ref, dkv_ref, pq_ref, dkr_ref, gq_ref, gkv_ref = ins

        def rms_bwd(dy, x, g):
            r = lax.rsqrt(jnp.mean(x * x, axis=-1, keepdims=True) + RMS_EPS)
            dyg = dy * g
            dx = r * dyg - x * (r * r * r) * jnp.mean(dyg * x, axis=-1, keepdims=True)
            return dx, jnp.sum(dy * (x * r), axis=0, keepdims=True)

        dxq, dgq = rms_bwd(dq_ref[...], pq_ref[:, 0:Q_LORA], gq_ref[...])
        dxkv, dgkv = rms_bwd(dkv_ref[...], pq_ref[:, Q_LORA:Q_LORA + KV_LORA], gkv_ref[...])
        outs[0][:, 0:Q_LORA] = dxq.astype(BF16)
        outs[0][:, Q_LORA:Q_LORA + KV_LORA] = dxkv.astype(BF16)
        outs[0][:, Q_LORA + KV_LORA:QKV_A] = dkr_ref[...].astype(BF16)
        _acc_add(i, accs[0], dgq)
        _acc_add(i, accs[1], dgkv)

    return _rows(body, "rms_bwd", S, _pick(S, (256, 128)),
                 [("row", d_rq, 0, Q_LORA), ("row", d_rkv, 0, KV_LORA), ("row", pq, 0, QKV_A), ("row", dkr, 0, 128),
                  ("full", g_q), ("full", g_kv)], [(QKV_A, BF16)], [(1, Q_LORA), (1, KV_LORA)])


def _dx_final(dxa, du, x, scale1):
    S, D = x.shape

    def body(i, ins, outs, accs):
        du = ins[1][...]
        outs[0][...] = ins[0][...] + du * (1.0 + ins[3][...])
        _acc_add(i, accs[0], jnp.sum(du, axis=0, keepdims=True))
        _acc_add(i, accs[1], jnp.sum(du * ins[2][...], axis=0, keepdims=True))

    return _rows(body, "dx_final", S, _pick(S, (256, 128)),
                 [("row", dxa, 0, D), ("row", du, 0, D), ("row", x, 0, D), ("full", scale1)],
                 [(D, F32)], [(1, D), (1, D)])


def _ada_fwd(c_all, w, bias):
    B, D = c_all.shape
    NA = w.shape[1]
    tn = _pick(NA, (512, 256, 128))

    def body(c_ref, w_ref, b_ref, o_ref):
        cv = c_ref[...]
        ca = (cv * _sigmoid(cv)).astype(BF16)
        o_ref[...] = jnp.dot(ca, w_ref[...].astype(BF16), preferred_element_type=F32) + b_ref[...]

    return pl.pallas_call(
        body, name="ada_fwd", grid=(NA // tn,),
        in_specs=[pl.BlockSpec((B, D), lambda j: (0, 0)), pl.BlockSpec((D, tn), lambda j: (0, j)),
                  pl.BlockSpec((1, tn), lambda j: (0, j))],
        out_specs=pl.BlockSpec((B, tn), lambda j: (0, j)),
        out_shape=jax.ShapeDtypeStruct((B, NA), F32),
        compiler_params=_params(("arbitrary",)),
    )(c_all, w, bias)


def _ada_bwd(c_all, dmod):
    B, D = c_all.shape
    NA = dmod.shape[1]
    tn = _pick(NA, (512, 256, 128))

    def body(c_ref, d_ref, o_ref):
        cv = c_ref[...]
        ca = (cv * _sigmoid(cv)).astype(BF16)
        o_ref[...] = lax.dot_general(ca, d_ref[...].astype(BF16), TN, preferred_element_type=F32)

    return pl.pallas_call(
        body, name="ada_bwd", grid=(NA // tn,),
        in_specs=[pl.BlockSpec((B, D), lambda j: (0, 0)), pl.BlockSpec((B, tn), lambda j: (0, j))],
        out_specs=pl.BlockSpec((D, tn), lambda j: (0, j)),
        out_shape=jax.ShapeDtypeStruct((D, NA), F32),
        compiler_params=_params(("arbitrary",)),
    )(c_all, dmod)


def _pack_rows(parts, n_rows, after=()):
    N = parts[0].shape[1]
    n = len(parts)

    def body(*refs):
        o_ref = refs[-1]
        o_ref[...] = jnp.zeros_like(o_ref)
        at = 0
        for r in refs[:n]:
            o_ref[at:at + r.shape[0], :] = r[...]
            at += r.shape[0]

    vmem = pl.BlockSpec(memory_space=pltpu.VMEM)
    return pl.pallas_call(body, name="pack_small", out_shape=jax.ShapeDtypeStruct((n_rows, N), F32),
                          in_specs=[vmem] * n + [ANY] * len(after), out_specs=vmem,
                          compiler_params=_params())(*parts, *after)


def _sum8(parts):
    _, R, N = parts.shape

    def body(p_ref, o_ref):
        acc = p_ref[0]
        for d in range(1, 8):
            acc = acc + p_ref[d]
        o_ref[...] = acc

    return pl.pallas_call(body, name="sum8", out_shape=jax.ShapeDtypeStruct((R, N), F32),
                          compiler_params=_params())(parts)


def _adam_math(w, g, m, v):
    m = ADAM_B1 * m + (1.0 - ADAM_B1) * g
    v = ADAM_B2 * v + (1.0 - ADAM_B2) * (g * g)
    delta = -ADAM_LR * ((m / ADAM_C1) / (jnp.sqrt(v / ADAM_C2) + ADAM_EPS) + ADAM_WD * w)
    return delta, m, v


def _adam(name, w, m, v, g, carry=None):
    R, C = w.shape
    tm = _row_tile(R, C * 4, 1 << 20)
    steps = R // tm
    n_ci = len(carry.ins) if carry else 0
    n_co = len(carry.outs) if carry else 0

    def body(*refs):
        w_ref, m_ref, v_ref, g_ref = refs[:4]
        d_ref, nm_ref, nv_ref = refs[4 + n_ci:7 + n_ci]
        c_ins, c_outs, c_sems = refs[4:4 + n_ci], refs[7 + n_ci:7 + n_ci + n_co], refs[7 + n_ci + n_co:]
        if carry:
            @pl.when(pl.program_id(0) == 0)
            def _():
                carry.start(c_ins, c_outs, c_sems)

        delta, nm, nv = _adam_math(w_ref[...], g_ref[...], m_ref[...], v_ref[...])
        d_ref[...] = delta
        nm_ref[...] = nm
        nv_ref[...] = nv
        if carry:
            @pl.when(pl.program_id(0) == steps - 1)
            def _():
                carry.finish(c_ins, c_outs, c_sems)

    spec = pl.BlockSpec((tm, C), lambda i: (i, 0))
    res = pl.pallas_call(
        body, name=name, grid=(steps,), in_specs=[spec] * 4 + [ANY] * n_ci, out_specs=[spec] * 3 + [ANY] * n_co,
        out_shape=[jax.ShapeDtypeStruct((R, C), F32)] * 3 + (carry.outs if carry else []),
        scratch_shapes=carry.sems if carry else [],
        input_output_aliases=carry.io_aliases(4, 3) if carry else {},
        compiler_params=_params(("arbitrary",)),
    )(w, m, v, g, *(carry.ins if carry else []))
    return (res[:3], res[3:]) if carry else res


def _adam_halves(name, w, m, v, mine, other, core, carry=None):
    R, C = w.shape
    Rh = mine.shape[0]
    tc = max(t for t in range(128, C + 1, 128) if C % t == 0 and R * t <= (3 << 17))
    steps = C // tc
    n_ci = len(carry.ins) if carry else 0
    n_co = len(carry.outs) if carry else 0

    def body(*refs):
        c_ref, w_ref, m_ref, v_ref, a_ref, b_ref = refs[:6]
        g_ref, d_ref, nm_ref, nv_ref = refs[6 + n_ci:10 + n_ci]
        c_ins, c_outs, c_sems = refs[6:6 + n_ci], refs[10 + n_ci:10 + n_ci + n_co], refs[10 + n_ci + n_co:]
        if carry:
            @pl.when(pl.program_id(0) == 0)
            def _():
                carry.start(c_ins, c_outs, c_sems)

        first = c_ref[0] == 0
        g = jnp.concatenate([jnp.where(first, a_ref[...], b_ref[...]),
                             jnp.where(first, b_ref[0:R - Rh, :], a_ref[0:R - Rh, :])], axis=0)
        delta, nm, nv = _adam_math(w_ref[...], g, m_ref[...], v_ref[...])
        g_ref[...] = g
        d_ref[...] = delta
        nm_ref[...] = nm
        nv_ref[...] = nv
        if carry:
            @pl.when(pl.program_id(0) == steps - 1)
            def _():
                carry.finish(c_ins, c_outs, c_sems)

    spec = pl.BlockSpec((R, tc), lambda i, c_ref: (0, i))
    h_spec = pl.BlockSpec((Rh, tc), lambda i, c_ref: (0, i))
    res = pl.pallas_call(
        body, name=name, out_shape=[jax.ShapeDtypeStruct((R, C), F32)] * 4 + (carry.outs if carry else []),
        grid_spec=pltpu.PrefetchScalarGridSpec(
            num_scalar_prefetch=1, grid=(steps,), in_specs=[spec, spec, spec, h_spec, h_spec] + [ANY] * n_ci,
            out_specs=[spec] * 4 + [ANY] * n_co, scratch_shapes=carry.sems if carry else []),
        input_output_aliases=carry.io_aliases(6, 4) if carry else {},
        compiler_params=_params(("arbitrary",)),
    )(core, w, m, v, mine, other, *(carry.ins if carry else []))
    return (res[:4], res[4:]) if carry else res


def _adam_small(name, w, m, v, g):
    def body(w_ref, m_ref, v_ref, g_ref, d_ref, nm_ref, nv_ref):
        delta, nm, nv = _adam_math(w_ref[...], g_ref[...], m_ref[...], v_ref[...])
        d_ref[...] = delta
        nm_ref[...] = nm
        nv_ref[...] = nv

    return pl.pallas_call(body, name=name, out_shape=[jax.ShapeDtypeStruct(w.shape, F32)] * 3,
                          compiler_params=_params())(w, m, v, g)


def _place():
    return lax.axis_index("x"), lax.axis_index("y"), lax.axis_index("c")


def _other_chips(x, y):
    return [(1 - x, y), (x, 1 - y), (1 - x, 1 - y)]


def _all_gather8(blk, name):
    R, N = blk.shape

    def body(x_ref, out_ref, send_sems, recv_sems, local_sem):
        x, y, c = _place()
        me = 4 * x + 2 * y + c
        mine = pltpu.make_async_copy(x_ref, out_ref.at[me], local_sem)
        mine.start()
        flips = [(j >> 2 & 1, j >> 1 & 1, j & 1) for j in range(1, 8)]
        peers = [((1 - x) if fx else x, (1 - y) if fy else y, (1 - c) if fc else c) for fx, fy, fc in flips]
        sends = []
        for j, peer in enumerate(peers):
            cp = pltpu.make_async_remote_copy(src_ref=x_ref, dst_ref=out_ref.at[me], send_sem=send_sems.at[j],
                                              recv_sem=recv_sems.at[j], device_id=peer, device_id_type=MESH)
            cp.start()
            sends.append(cp)
        for j, (px, py, pc) in enumerate(peers):
            pltpu.make_async_remote_copy(src_ref=x_ref, dst_ref=out_ref.at[4 * px + 2 * py + pc],
                                         send_sem=send_sems.at[j], recv_sem=recv_sems.at[j],
                                         device_id=(px, py, pc), device_id_type=MESH).wait_recv()
        for cp in sends:
            cp.wait_send()
        mine.wait()

    return pl.pallas_call(
        body, name=name, out_shape=jax.ShapeDtypeStruct((8, R, N), F32),
        in_specs=[pl.BlockSpec(memory_space=pltpu.VMEM)], out_specs=pl.BlockSpec(memory_space=pltpu.VMEM),
        scratch_shapes=[pltpu.SemaphoreType.DMA((7,)), pltpu.SemaphoreType.DMA((7,)), pltpu.SemaphoreType.DMA],
        compiler_params=_params(),
    )(blk)


def _piece(rows, piece):
    i, n, k = piece if len(piece) == 3 else (piece[0], piece[1], 1)
    assert rows % 16 == 0 and rows // 16 >= n, (rows, piece)
    lo, hi = (rows // 16 * i // n) * 16, (rows // 16 * (i + k) // n) * 16
    return pl.ds(lo, hi - lo)


def _gather_plan(shards, piece=(0, 1), into=None, ici=True):
    n = len(shards)

    def parts(ins, outs, sems):
        s1, r1, s2, r2, loc = sems
        x, y, c = _place()
        me = 2 * x + y
        chips = _other_chips(x, y)
        sib = (x, y, 1 - c)

        def rows(k):
            return _piece(shards[k].shape[1], piece)

        def ici_copy(k, j, slab, to):
            return pltpu.make_async_remote_copy(src_ref=ins[k].at[c, rows(k)], dst_ref=outs[k].at[slab, c, rows(k)],
                                                send_sem=s1.at[3 * k + j], recv_sem=r1.at[3 * k + j],
                                                device_id=to, device_id_type=MESH)

        def d2d(k, j, slab, half):
            return pltpu.make_async_remote_copy(src_ref=outs[k].at[slab, half, rows(k)],
                                                dst_ref=outs[k].at[slab, half, rows(k)],
                                                send_sem=s2.at[3 * k + j], recv_sem=r2.at[3 * k + j],
                                                device_id=sib, device_id_type=MESH)

        def own(k):
            return pltpu.make_async_remote_copy(src_ref=ins[k].at[:, rows(k)], dst_ref=outs[k].at[me, :, rows(k)],
                                                send_sem=loc.at[2 * k], recv_sem=loc.at[2 * k + 1],
                                                device_id=sib, device_id_type=MESH)

        return c, me, chips, ici_copy, d2d, own

    def start(ins, outs, sems):
        c, me, chips, ici_copy, d2d, own = parts(ins, outs, sems)
        for k in range(n):
            for j, (px, py) in enumerate(chips):
                (ici_copy(k, j, me, (px, py, c)) if ici else d2d(k, j, 2 * px + py, c)).start()
        for k in range(n):
            own(k).start()

    def finish(ins, outs, sems):
        c, me, chips, ici_copy, d2d, own = parts(ins, outs, sems)
        if ici:
            for k in range(n):
                for j, (px, py) in enumerate(chips):
                    ici_copy(k, j, 2 * px + py, (px, py, c)).wait_recv()
                    d2d(k, j, 2 * px + py, c).start()
        for k in range(n):
            for j, (px, py) in enumerate(chips):
                d2d(k, j, 2 * px + py, 1 - c).wait_recv()
        for k in range(n):
            own(k).wait()
            for j, (px, py) in enumerate(chips):
                if ici:
                    ici_copy(k, j, me, (px, py, c)).wait_send()
                d2d(k, j, 2 * px + py, c).wait_send()

    return _Plan(list(shards) + list(into or []), [jax.ShapeDtypeStruct((4,) + a.shape, a.dtype) for a in shards],
                 [pltpu.SemaphoreType.DMA((3 * n,))] * 4 + [pltpu.SemaphoreType.DMA((2 * n,))], start, finish,
                 aliases={n + k: k for k in range(n)} if into else None)


def _pair_plan(parts):
    n = len(parts)

    def copies(ins, outs, sems):
        send_sems, recv_sems = sems
        x, y, c = _place()
        return [pltpu.make_async_remote_copy(src_ref=ins[k].at[p, 1 - c], dst_ref=outs[k].at[p],
                                             send_sem=send_sems.at[4 * k + p], recv_sem=recv_sems.at[4 * k + p],
                                             device_id=(x, y, 1 - c), device_id_type=MESH)
                for k in range(n) for p in range(4)]

    def start(ins, outs, sems):
        for cp in copies(ins, outs, sems):
            cp.start()

    def finish(ins, outs, sems):
        for cp in copies(ins, outs, sems):
            cp.wait()

    return _Plan(parts, [jax.ShapeDtypeStruct((4,) + a.shape[2:], a.dtype) for a in parts],
                 [pltpu.SemaphoreType.DMA((4 * n,))] * 2, start, finish)


def _sibling_plan(arrs):
    n = len(arrs)

    def copies(ins, outs, sems):
        send_sems, recv_sems = sems
        x, y, c = _place()
        return [pltpu.make_async_remote_copy(src_ref=ins[k], dst_ref=outs[k], send_sem=send_sems.at[k],
                                             recv_sem=recv_sems.at[k], device_id=(x, y, 1 - c), device_id_type=MESH)
                for k in range(n)]

    def start(ins, outs, sems):
        for cp in copies(ins, outs, sems):
            cp.start()

    def finish(ins, outs, sems):
        for cp in copies(ins, outs, sems):
            cp.wait()

    return _Plan(arrs, [jax.ShapeDtypeStruct(a.shape, a.dtype) for a in arrs],
                 [pltpu.SemaphoreType.DMA((n,))] * 2, start, finish)


def _scatter_copies(arrs):
    def copies(ins, land, send_sems, recv_sems):
        x, y, c = _place()
        return [pltpu.make_async_remote_copy(src_ref=ins[k].at[2 * px + py], dst_ref=land[k].at[j],
                                             send_sem=send_sems.at[3 * k + j], recv_sem=recv_sems.at[3 * k + j],
                                             device_id=(px, py, c), device_id_type=MESH)
                for k in range(len(arrs)) for j, (px, py) in enumerate(_other_chips(x, y))]

    return copies, [lax.empty((3,) + a.shape[1:], a.dtype) for a in arrs]


def _gather_copies(shards):
    def copies(ins, land, send_sems, recv_sems):
        x, y, c = _place()
        return [pltpu.make_async_remote_copy(src_ref=ins[k].at[c], dst_ref=land[k].at[2 * x + y, c],
                                             send_sem=send_sems.at[3 * k + j], recv_sem=recv_sems.at[3 * k + j],
                                             device_id=(px, py, c), device_id_type=MESH)
                for k in range(len(shards)) for j, (px, py) in enumerate(_other_chips(x, y))]

    return copies, [lax.empty((4,) + a.shape, a.dtype) for a in shards]


def _split_start(arrs, copies_lands, ride, name, after=()):
    copies, lands = copies_lands
    n = len(arrs)
    rides = list(ride) if isinstance(ride, (list, tuple)) else [ride]
    n_thru = 2 * n + len(rides)

    def body(*refs):
        first_out = n_thru + len(after)
        for cp in copies(refs[:n], refs[n:2 * n], refs[first_out], refs[first_out + 1]):
            cp.start()

    hbm = [pltpu.with_memory_space_constraint(a, pltpu.HBM) for a in list(arrs) + lands + rides]
    res = pl.pallas_call(
        body, name=name,
        out_shape=[pltpu.SemaphoreType.DMA((3 * n,)), pltpu.SemaphoreType.DMA((3 * n,))]
        + [pltpu.HBM(a.shape, a.dtype) for a in hbm],
        in_specs=[HBM_SPEC] * n_thru + [ANY] * len(after),
        out_specs=[SEM_SPEC, SEM_SPEC] + [HBM_SPEC] * n_thru,
        input_output_aliases={i: 2 + i for i in range(n_thru)},
        compiler_params=pltpu.CompilerParams(has_side_effects=pltpu.SideEffectType.DATAFLOW_SIDE_EFFECTING),
    )(*hbm, *after)
    return res[0], res[1], res[2:2 + n], res[2 + n:2 + 2 * n], list(res[2 + 2 * n:])


def _split_wait(started, copies_lands, after, name):
    send_sems, recv_sems, arrs, lands, _ = started
    copies = copies_lands[0]
    n = len(arrs)

    def body(*refs):
        for cp in copies(refs[:n], refs[n:2 * n], refs[2 * n], refs[2 * n + 1]):
            cp.wait_send()
            cp.wait_recv()

    res = pl.pallas_call(
        body, name=name, out_shape=[pltpu.HBM(a.shape, a.dtype) for a in list(arrs) + list(lands)],
        in_specs=[HBM_SPEC] * (2 * n) + [SEM_SPEC, SEM_SPEC] + [ANY] * len(after), out_specs=[HBM_SPEC] * (2 * n),
        input_output_aliases={i: i for i in range(2 * n)},
        compiler_params=pltpu.CompilerParams(has_side_effects=pltpu.SideEffectType.DATAFLOW_SIDE_EFFECTING),
    )(*arrs, *lands, send_sems, recv_sems, *after)
    return list(res[:n]), list(res[n:])


def _add_pair(parts, sib, core, name):
    P4, _, Rh, C = parts.shape
    tm, tc = _tile2(Rh, C, 16)

    def body(c_ref, a_ref, b_ref, o_ref):
        o_ref[...] = (a_ref[0].astype(F32) + b_ref[...].astype(F32)).astype(BF16)

    spec = pl.BlockSpec((1, tm, tc), lambda p, i, j, c_ref: (p, i, j))
    return pl.pallas_call(
        body, name=name, out_shape=jax.ShapeDtypeStruct((P4, Rh, C), BF16),
        grid_spec=pltpu.PrefetchScalarGridSpec(
            num_scalar_prefetch=1, grid=(P4, Rh // tm, C // tc),
            in_specs=[pl.BlockSpec((1, 1, tm, tc), lambda p, i, j, c_ref: (p, c_ref[0], i, j)), spec], out_specs=spec),
        compiler_params=_params(("parallel",) * 3),
    )(core, parts, sib)


def _sum_slabs(pre, recv, chip, name):
    _, Rh, C = pre.shape
    tm, tc = _tile2(Rh, C, 16)

    def body(me_ref, own_ref, r_ref, o_ref):
        acc = own_ref[0].astype(F32)
        for j in range(3):
            acc = acc + r_ref[j].astype(F32)
        o_ref[...] = acc

    return pl.pallas_call(
        body, name=name, out_shape=jax.ShapeDtypeStruct((Rh, C), F32),
        grid_spec=pltpu.PrefetchScalarGridSpec(
            num_scalar_prefetch=1, grid=(Rh // tm, C // tc),
            in_specs=[pl.BlockSpec((1, tm, tc), lambda i, j, me_ref: (me_ref[0], i, j)),
                      pl.BlockSpec((3, tm, tc), lambda i, j, me_ref: (0, i, j))],
            out_specs=pl.BlockSpec((tm, tc), lambda i, j, me_ref: (i, j))),
        compiler_params=_params(("parallel", "parallel")),
    )(chip, pre, recv)


def kernel(x, c, positions, w_ada, b_ada, w_in, g_q_a, w_q_b, g_kv_a, w_kv_b, w_o_a, w_conv, w_o_b, w_o, ln1_g, ln1_b, w_ffn_in, w_ffn_out, ln2_g, ln2_b, loss_target, m_w_ada, m_b_ada, m_w_in, m_g_q_a, m_w_q_b, m_g_kv_a, m_w_kv_b, m_w_o_a, m_w_conv, m_w_o_b, m_w_o, m_ln1_g, m_ln1_b, m_w_ffn_in, m_w_ffn_out, m_ln2_g, m_ln2_b, v_w_ada, v_b_ada, v_w_in, v_g_q_a, v_w_q_b, v_g_kv_a, v_w_kv_b, v_w_o_a, v_w_conv, v_w_o_b, v_w_o, v_ln1_g, v_ln1_b, v_w_ffn_in, v_w_ffn_out, v_ln2_g, v_ln2_b):
    S, D = x.shape[1], x.shape[2]
    F = w_ffn_out.shape[1] * 4
    ax, ay, ac = _place()
    chip = 2 * ax + ay
    dev = 4 * ax + 2 * ay + ac
    x2, tgt = x[0], loss_target[0]
    w_ada2, w_in2, w_q_b2, w_kv_b2 = w_ada[0], w_in[0], w_q_b[0], w_kv_b[0]
    w_o_a2, w_o_b2, w_o2, w_ffn_in2, w_ffn_out2 = w_o_a[0], w_o_b[0], w_o[0], w_ffn_in[0], w_ffn_out[0]
    NA = w_ada2.shape[1]
    CW = w_conv.shape[2]

    inv_freq = 1.0 / (ROPE_THETA ** (jnp.arange(0, QK_ROPE, 2, dtype=F32) / QK_ROPE))
    ang = positions[0].astype(F32)[:, None] * inv_freq
    cos, sin = jnp.cos(ang), jnp.sin(ang)
    z32, z64, z96 = jnp.zeros((S, 32), F32), jnp.zeros((S, 64), F32), jnp.zeros((S, 96), F32)
    tab = jnp.concatenate([cos, cos, z64, -sin, z96, z32, sin, z64], axis=1)

    def halves(a):
        return a.reshape(2, a.shape[0] // 2, a.shape[1])

    def whole(g):
        return g.reshape(4, 2 * g.shape[2], g.shape[3])

    def cols(g):
        return jnp.transpose(g, (1, 0, 2)).reshape(g.shape[1], 4 * g.shape[2])

    w_inT, m_w_inT, v_w_inT = w_in2.T, m_w_in[0].T, v_w_in[0].T
    CS = w_inT.shape[0]
    CSP = -(-CS // 32) * 32
    sh_in = halves(jnp.pad(w_inT.astype(BF16), ((0, CSP - CS), (0, 0))))
    sh_qb, sh_kvb, sh_oa, sh_ob, sh_o, sh_fi, sh_fo = (
        halves(w.astype(BF16)) for w in (w_q_b2, w_kv_b2, w_o_a2, w_o_b2, w_o2, w_ffn_in2, w_ffn_out2))
    c_all = _all_gather8(c, "gather_c").reshape(8, D)
    wconv_all = _all_gather8(w_conv[0], "gather_wconv")
    w_conv_full = jnp.transpose(wconv_all[0::2], (1, 0, 2)).reshape(3, D)
    b_sh = lax.dynamic_slice(b_ada, (0, chip * NA), (1, NA))
    mod_sh = _ada_fwd(c_all, w_ada2, b_sh)
    mod_all = _all_gather8(mod_sh, "gather_mod")
    mod = lax.dynamic_slice(mod_all[0::2], (0, dev, 0), (4, 1, NA)).reshape(6, D)
    shift1, scale1, gate1, shift2, scale2, gate2 = (mod[k:k + 1] for k in range(6))

    g_in, shift1, w_conv_full = _run_plan(_gather_plan([sh_in]), "gather_first", ride=[shift1, w_conv_full])
    g_in = whole(g_in)
    sh_a1, sh_a2 = [sh_qb, sh_kvb], [sh_oa, sh_ob, sh_o]
    cl_a1, cl_a2, cl_fi, cl_fo = (_gather_copies(g) for g in (sh_a1, sh_a2, [sh_fi], [sh_fo]))
    st_a1 = _split_start(sh_a1, cl_a1, shift1, "gather_a1_start")
    st_a2 = _split_start(sh_a2, cl_a2, st_a1[4], "gather_a2_start")
    shift1 = st_a2[4][0]

    def in_rows(lo, hi):
        parts = [g_in[p, max(lo, p * CS) - p * CS:min(hi, (p + 1) * CS) - p * CS]
                 for p in range(4) if max(lo, p * CS) < min(hi, (p + 1) * CS)]
        return parts[0] if len(parts) == 1 else jnp.concatenate(parts, axis=0)

    n_qkv = Q_LORA + KV_LORA + QK_ROPE
    W_qkvT = jnp.pad(in_rows(0, n_qkv), ((0, QKV_A - n_qkv), (0, 0)))
    W_convT = in_rows(n_qkv, n_qkv + 3 * D)
    W_gateT = in_rows(n_qkv + 3 * D, n_qkv + 5 * D)

    u = _modulate(x2, scale1, shift1, "modulate1")
    pq = _matmul(u, W_qkvT, "nt", F32, "proj_qkv")
    pc = _matmul(u, W_convT, "nt", BF16, "proj_conv")
    sh_a1, la1 = _split_wait(st_a1, cl_a1, [pc], "gather_a1_wait")
    pg, (g_qb, g_kvb) = _matmul(u, W_gateT, "nt", BF16, "proj_gate", carry=_gather_plan(sh_a1, into=la1, ici=False))
    st_fi = _split_start([sh_fi], cl_fi, g_q_a, "gather_fi_start", after=[pg])
    W_qb = jnp.pad(cols(whole(g_qb)).reshape(Q_LORA, N_HEADS, QK_NOPE + QK_ROPE),
                   ((0, 0), (0, 0), (0, QK_PAD - QK_NOPE - QK_ROPE))).reshape(Q_LORA, N_HEADS * QK_PAD)
    W_kvb = cols(whole(g_kvb))
    rq, rkv, kr = _rms_fwd(pq, tab, st_fi[4][0], g_kv_a)
    kv = _matmul(rkv, W_kvb, "nn", BF16, "kv_b")
    sh_a2, la2 = _split_wait(st_a2, cl_a2, [kv], "gather_a2_wait")
    def rope_heads(r, t):
        return jnp.concatenate([r[:, lo:lo + 128] if lo % QK_PAD == 0 else _rope(r[:, lo:lo + 128], t, 1)
                                for lo in range(0, r.shape[1], 128)], axis=1)

    q, (g_oa, g_ob, g_o) = _matmul(rq, W_qb, "nn", BF16, "q_b", carry=_gather_plan(sh_a2, into=la2, ici=False),
                                   finish=(rope_heads, tab))
    o, lse, _ = _attn_fwd(q, kv, kr)
    W_oa, W_ob, W_o = (g.reshape(-1, D) for g in (g_oa, g_ob, g_o))
    hb = _conv_fwd(pc, w_conv_full)
    sh_fi_t, lfi = _split_wait(st_fi, cl_fi, [o], "gather_fi_wait")
    y_b, g_fi = _matmul(hb, W_ob, "nn", BF16, "o_b", carry=_gather_plan(sh_fi_t, (0, 2), into=lfi, ici=False))
    y_a, (g_fi,) = _matmul(o, W_oa, "nn", BF16, "o_a", carry=_gather_plan(sh_fi_t, (1, 2), into=g_fi, ici=False))
    st_fo = _split_start([sh_fo], cl_fo, ln1_g, "gather_fo_start", after=[y_b])
    merged = _merge_fwd(y_a, y_b, pg)
    mix = _matmul(merged, W_o, "nn", F32, "w_o")
    W_fi = whole(g_fi)
    x1, u2 = _ln1_fwd(x2, mix, gate1, st_fo[4][0], ln1_b, scale2, shift2)
    hh = _matmul(u2, W_fi, "nn", BF16, "ffn_in", shards="b")
    sh_fo_t, lfo = _split_wait(st_fo, cl_fo, [hh], "gather_fo_wait")
    act, (g_fo,) = _swiglu_fwd(hh, carry=_gather_plan(sh_fo_t, into=lfo, ici=False))
    W_fo = g_fo.reshape(F, D)
    ffn = _matmul(act, W_fo, "nn", F32, "ffn_out")

    core_i = ac.astype(jnp.int32).reshape(1)
    chip_i = chip.astype(jnp.int32).reshape(1)

    def uncols(g):
        return jnp.transpose(g.reshape(g.shape[0], 4, g.shape[1] // 4), (1, 0, 2))

    def slabs(p):
        return p.reshape(4, 2, p.shape[1] // 2, p.shape[2])

    def add_pairs(parts, sibs, nms):
        return [_add_pair(a, b, core_i, "add_pair_" + nm) for a, b, nm in zip(parts, sibs, nms)]

    def sum_all(pre, recv, nms):
        return [_sum_slabs(a, r, chip_i, "sum_slabs_" + nm) for a, r, nm in zip(pre, recv, nms)]

    dffn, dx1a, loss_acc, d_ln2_g, d_ln2_b, d_gate2 = _ln2_loss_bwd(x1, ffn, gate2, ln2_g, ln2_b, tgt)
    dW_fo = _matmul(act, dffn, "tn", BF16, "d_w_ffn_out")
    p_fo = [slabs(dW_fo.reshape(4, -1, D))]
    dact, s_fo = _matmul(dffn, W_fo, "nt", BF16, "d_act", carry=_pair_plan(p_fo))
    pre_fo = add_pairs(p_fo, s_fo, ["w_ffn_out"])
    cs_fo = _scatter_copies(pre_fo)
    st_sfo = _split_start(pre_fo, cs_fo, scale2, "scatter_fo_start")
    dhh = _swiglu_bwd(dact, hh)
    dW_fi = _matmul(u2, dhh, "tn", BF16, "d_w_ffn_in", shards="o")
    p_fi = [slabs(dW_fi)]
    du2, s_fi = _matmul(dhh, W_fi, "nt", F32, "d_u2", carry=_pair_plan(p_fi), shards="b")
    pre_fi = add_pairs(p_fi, s_fi, ["w_ffn_in"])
    cs_fi = _scatter_copies(pre_fi)
    st_sfi = _split_start(pre_fi, cs_fi, st_sfo[4], "scatter_fi_start")
    dmix, dxa, d_shift2, d_scale2, d_ln1_g, d_ln1_b, d_gate1 = _ln1_bwd(x2, mix, dx1a, du2, gate1, ln1_g, ln1_b, st_sfi[4][0])
    dW_o = _matmul(merged, dmix, "tn", BF16, "d_w_o")
    dmerged = _matmul(dmix, W_o, "nt", BF16, "d_merged")
    dy_a, dy_b, dgate = _merge_bwd(dmerged, y_a, y_b, pg)
    dW_oa = _matmul(o, dy_a, "tn", BF16, "d_w_o_a")
    do = _matmul(dy_a, W_oa, "nt", BF16, "d_o")
    dW_ob = _matmul(hb, dy_b, "tn", BF16, "d_w_o_b")
    p_mid = [slabs(g.reshape(4, -1, D)) for g in (dW_oa, dW_ob, dW_o)]
    dhb, s_mid = _matmul(dy_b, W_ob, "nt", BF16, "d_hb", carry=_pair_plan(p_mid))
    pre_mid = add_pairs(p_mid, s_mid, ["w_o_a", "w_o_b", "w_o"])
    cs_mid = _scatter_copies(pre_mid)
    st_smid = _split_start(pre_mid, cs_mid, w_conv_full, "scatter_mid_start")
    dconv, d_wconv = _conv_bwd(dhb, pc, st_smid[4][0])
    dq, dkv, dkr, _ = _attn_bwd(q, kv, kr, do, o, lse, tab, carry=_token_plan(st_smid[4][0]))
    names_a = ["w_ffn_out", "w_ffn_in", "w_o_a", "w_o_b", "w_o"]
    dW_qb = _matmul(rq, dq, "tn", BF16, "d_w_q_b")
    d_rq = _matmul(dq, W_qb, "nt", F32, "d_rq")
    dW_kvb = _matmul(rkv, dkv, "tn", BF16, "d_w_kv_b")
    d_rkv = _matmul(dkv, W_kvb, "nt", F32, "d_rkv")
    dqkv, d_g_q, d_g_kv = _rms_bwd(d_rq, d_rkv, pq, dkr, g_q_a, g_kv_a)
    dW_qkvT = _matmul(dqkv, u, "tn", BF16, "d_w_qkv")
    dW_convT = _matmul(dconv, u, "tn", BF16, "d_w_conv")
    dW_gateT = _matmul(dgate, u, "tn", BF16, "d_w_gate")
    pre_fo, r_fo = _split_wait(st_sfo, cs_fo, [dW_qkvT], "scatter_fo_wait")
    pre_fi, r_fi = _split_wait(st_sfi, cs_fi, [dW_qkvT], "scatter_fi_wait")
    pre_mid, r_mid = _split_wait(st_smid, cs_mid, [dW_qkvT], "scatter_mid_wait")
    fin_a = sum_all(pre_fo + pre_fi + pre_mid, r_fo + r_fi + r_mid, names_a)
    srcs = [(0, dW_qkvT[:n_qkv]), (n_qkv, dW_convT), (n_qkv + 3 * D, dW_gateT)]
    rows_of = []
    for p in range(4):
        for lo, src in srcs:
            a, b = max(lo, p * CS), min(lo + src.shape[0], (p + 1) * CS)
            if a < b:
                rows_of.append(src[a - lo:b - lo])
        rows_of.append(jnp.zeros((CSP - CS, D), BF16))
    dW_inT = jnp.concatenate(rows_of, axis=0).reshape(4, CSP, D)
    dW_qb_u = dW_qb.reshape(Q_LORA, N_HEADS, QK_PAD)[:, :, :QK_NOPE + QK_ROPE].reshape(Q_LORA, -1)
    names_b = ["w_in", "w_q_b", "w_kv_b"]
    p_b = [slabs(dW_inT), slabs(uncols(dW_qb_u)), slabs(uncols(dW_kvb))]
    du, s_b = _matmul(dqkv, W_qkvT, "nn", F32, "d_u_qkv", carry=_pair_plan(p_b))
    pre_b = add_pairs(p_b, s_b, names_b)
    cs_b = _scatter_copies(pre_b)
    st_b = _split_start(pre_b, cs_b, scale1, "scatter_last_start")
    du, fs_a = _matmul(dconv, W_convT, "nn", F32, "d_u_conv", add=du, carry=_sibling_plan(fin_a))
    du = _matmul(dgate, W_gateT, "nn", F32, "d_u_gate", add=du)
    grad_x, d_shift1, d_scale1 = _dx_final(dxa, du, x2, st_b[4][0])

    big = {}
    ws = dict(w_in=(w_inT, m_w_inT, v_w_inT), w_q_b=(w_q_b2, m_w_q_b[0], v_w_q_b[0]),
              w_kv_b=(w_kv_b2, m_w_kv_b[0], v_w_kv_b[0]), w_o_a=(w_o_a2, m_w_o_a[0], v_w_o_a[0]),
              w_o_b=(w_o_b2, m_w_o_b[0], v_w_o_b[0]), w_o=(w_o2, m_w_o[0], v_w_o[0]),
              w_ffn_in=(w_ffn_in2, m_w_ffn_in[0], v_w_ffn_in[0]), w_ffn_out=(w_ffn_out2, m_w_ffn_out[0], v_w_ffn_out[0]))

    def adam_of(nm, a, b, carry=None):
        w_, m_, v_ = ws[nm]
        return _adam_halves("adam_" + nm, w_, m_, v_, a, b, core_i, carry)

    for nm, a, b in zip(names_a, fin_a, fs_a):
        big[nm] = adam_of(nm, a, b, _token_plan(st_b[4][0]))[0]
    done = [big[nm][1] for nm in names_a] + [grad_x]
    pre_b, r_b = _split_wait(st_b, cs_b, done, "scatter_last_wait")
    fin_b = sum_all(pre_b, r_b, names_b)
    fs_b = _run_plan(_sibling_plan(fin_b), "sibling_last")
    for nm, a, b in zip(names_b, fin_b, fs_b):
        big[nm] = adam_of(nm, a, b)

    def pad_d(v):
        return jnp.pad(v, ((0, 0), (0, D - v.shape[1])))

    small = _pack_rows([d_ln1_g, d_ln1_b, d_ln2_g, d_ln2_b, pad_d(d_g_q), pad_d(d_g_kv), d_wconv,
                         d_shift1, d_scale1, d_gate1, d_shift2, d_scale2, d_gate2, pad_d(loss_acc)], 16, after=[pre_b[1]])
    small_all = _all_gather8(small, "gather_small")
    small_sum = _sum8(small_all)
    loss = small_sum[15, 0]
    g_ln1_g, g_ln1_b, g_ln2_g, g_ln2_b = (small_sum[k:k + 1] for k in range(4))
    g_g_q, g_g_kv = small_sum[4:5, :Q_LORA], small_sum[5:6, :KV_LORA]
    g_wconv = lax.dynamic_slice(small_sum[6:9], (0, chip * CW), (3, CW))
    g_b_ada = small_sum[9:15].reshape(1, 6 * D)
    dmod_all = small_all[:, 9:15, :].reshape(8, 6 * D)
    g_w_ada = _ada_bwd(c_all, lax.dynamic_slice(dmod_all, (0, chip * NA), (8, NA)))
    big["w_ada"] = [g_w_ada] + list(_adam("adam_w_ada", w_ada2, m_w_ada[0], v_w_ada[0], g_w_ada))
    sm = {}
    for nm, w_, m_, v_, g_ in [("b_ada", b_ada, m_b_ada, v_b_ada, g_b_ada), ("g_q_a", g_q_a, m_g_q_a, v_g_q_a, g_g_q),
                               ("g_kv_a", g_kv_a, m_g_kv_a, v_g_kv_a, g_g_kv),
                               ("w_conv", w_conv[0], m_w_conv[0], v_w_conv[0], g_wconv),
                               ("ln1_g", ln1_g, m_ln1_g, v_ln1_g, g_ln1_g), ("ln1_b", ln1_b, m_ln1_b, v_ln1_b, g_ln1_b),
                               ("ln2_g", ln2_g, m_ln2_g, v_ln2_g, g_ln2_g), ("ln2_b", ln2_b, m_ln2_b, v_ln2_b, g_ln2_b)]:
        sm[nm] = (g_,) + tuple(_adam_small("adam_" + nm, w_, m_, v_, g_))

    order = ["w_ada", "b_ada", "w_in", "g_q_a", "w_q_b", "g_kv_a", "w_kv_b", "w_o_a", "w_conv", "w_o_b", "w_o",
             "ln1_g", "ln1_b", "w_ffn_in", "w_ffn_out", "ln2_g", "ln2_b"]
    lead = {"b_ada", "g_q_a", "g_kv_a", "ln1_g", "ln1_b", "ln2_g", "ln2_b"}

    def leaf(nm, k):
        val = big[nm][k] if nm in big else sm[nm][k]
        if nm == "w_in":
            val = val.T
        return val if nm in lead else val[None]

    outs = [loss, grad_x[None]]
    for k in range(4):
        outs += [leaf(nm, k) for nm in order]
    return tuple(outs)
```

```python
import jax
import jax.numpy as jnp
from jax import lax
from jax.experimental import pallas as pl
from jax.experimental.pallas import tpu as pltpu

F32, BF16 = jnp.float32, jnp.bfloat16
N_HEADS, QK_NOPE, QK_ROPE, V_HEAD = 16, 128, 64, 128
Q_LORA, KV_LORA = 512, 512
QK_PAD = 256
QKV_A = 1152
CHUNK_SHIFT = 6
ATTN_SCALE = (QK_NOPE + QK_ROPE) ** -0.5
LOG2E = 1.4426950408889634
SCALE2 = ATTN_SCALE * LOG2E
ROPE_THETA = 10000.0
ALPHA = 2.0 ** 0.25
LN_EPS, RMS_EPS = 1e-5, 1e-6
ADAM_LR, ADAM_B1, ADAM_B2, ADAM_EPS, ADAM_WD, ADAM_STEP = 0.001, 0.9, 0.999, 1e-08, 0.01, 10
ADAM_C1 = 1.0 - ADAM_B1 ** ADAM_STEP
ADAM_C2 = 1.0 - ADAM_B2 ** ADAM_STEP
VMEM_LIMIT = 56 * 1024 * 1024
MESH = pl.DeviceIdType.MESH
ANY = pl.BlockSpec(memory_space=pl.ANY)
HBM_SPEC = pl.BlockSpec(memory_space=pltpu.HBM)
SEM_SPEC = pl.BlockSpec(memory_space=pltpu.SEMAPHORE)
NT = (((1,), (1,)), ((), ()))
TN = (((0,), (0,)), ((), ()))
NN = (((1,), (0,)), ((), ()))


def _params(sem=None):
    return pltpu.CompilerParams(dimension_semantics=sem, vmem_limit_bytes=VMEM_LIMIT)


def _pick(n, cands=(1408, 1024, 512, 384, 256, 128)):
    for t in cands:
        if n % t == 0:
            return t
    return n


def _row_tile(rows, row_bytes, budget, mult=8):
    best = mult
    for t in range(mult, rows + 1, mult):
        if rows % t == 0 and t * row_bytes <= budget:
            best = t
    return best


def _tile2(rows, cols, mult=8, budget=3 << 18):
    col_tiles = [t for t in range(128, cols + 1, 128) if cols % t == 0] or [cols]
    best = None
    for tc in col_tiles:
        for tr in range(mult, rows + 1, mult):
            if rows % tr == 0 and tr * tc <= budget and (best is None or (tr * tc, tc) > (best[0] * best[1], best[1])):
                best = (tr, tc)
    assert best is not None, (rows, cols)
    return best


def _sigmoid(x):
    return jax.nn.sigmoid(x)


class _Plan:
    def __init__(self, ins, outs, sems, start, finish, aliases=None):
        self.ins, self.outs, self.sems, self.start, self.finish = list(ins), list(outs), list(sems), start, finish
        self.aliases = dict(aliases or {})

    def io_aliases(self, first_in, first_out):
        return {first_in + i: first_out + o for i, o in self.aliases.items()}


def _token_plan(token):
    return _Plan([token], [], [], lambda *a: None, lambda *a: None)


def _run_plan(plan, name, ride=None):
    n_in, n_out = len(plan.ins), len(plan.outs)
    extra = [] if ride is None else list(ride)
    aliases = plan.io_aliases(0, 0)
    for k in range(len(extra)):
        aliases[n_in + k] = n_out + k

    def body(*refs):
        ins, outs, sems = refs[:n_in], refs[n_in + len(extra):n_in + len(extra) + n_out], refs[n_in + 2 * len(extra) + n_out:]
        plan.start(ins, outs, sems)
        plan.finish(ins, outs, sems)

    return pl.pallas_call(body, name=name, out_shape=plan.outs + [jax.ShapeDtypeStruct(r.shape, r.dtype) for r in extra],
                          in_specs=[ANY] * (n_in + len(extra)), out_specs=[ANY] * (n_out + len(extra)),
                          scratch_shapes=plan.sems, input_output_aliases=aliases,
                          compiler_params=_params())(*plan.ins, *extra)


def _matmul(a, b, mode, out_dtype, name, add=None, carry=None, shards=None, finish=None):
    if mode == "nn":
        (M, K), N, dims = a.shape, b.shape[-1] * (4 if shards else 1), NN
    elif mode == "nt":
        (M, K), N, dims = a.shape, b.shape[-2], NT
    else:
        (K, M), N, dims = a.shape, b.shape[1], TN
    split_n = shards and mode != "nt"
    tm = _pick(M)
    tn = _pick(N // 4) if split_n else _pick(N)
    deep = (2816, 2048, 1408, 1024, 512, 384, 256, 128)
    if shards and mode == "nt":
        tk = _pick(K // 4, deep)
    else:
        tk = K if K <= 2048 else _pick(K, deep)
    nk = K // tk
    per = (N // 4 // tn) if split_n else (K // 4 // tk if shards else 1)
    a_spec = (pl.BlockSpec((tk, tm), lambda i, j, k: (k, i)) if mode == "tn"
              else pl.BlockSpec((tm, tk), lambda i, j, k: (i, k)))
    if shards == "b" and mode == "nn":
        b_spec = pl.BlockSpec((None, tk, tn), lambda i, j, k: (j // per, k, j % per))
    elif shards == "b":
        b_spec = pl.BlockSpec((None, tn, tk), lambda i, j, k: (k // per, j, k % per))
    else:
        b_spec = (pl.BlockSpec((tn, tk), lambda i, j, k: (j, k)) if mode == "nt"
                  else pl.BlockSpec((tk, tn), lambda i, j, k: (k, j)))
    o_spec = pl.BlockSpec((tm, tn), lambda i, j, k: (i, j))
    o_shape = (M, N)
    if shards == "o":
        o_spec, o_shape = pl.BlockSpec((None, tm, tn), lambda i, j, k: (j // per, i, j % per)), (4, M, N // 4)
    has_add = add is not None
    has_fin = finish is not None
    n_ci = len(carry.ins) if carry else 0
    n_co = len(carry.outs) if carry else 0
    n_in = 2 + has_add + has_fin
    grid = (M // tm, N // tn, nk)

    def body(*refs):
        a_ref, b_ref = refs[0], refs[1]
        add_ref = refs[2] if has_add else None
        fin_ref = refs[2 + has_add] if has_fin else None

        def store(r):
            if has_add:
                r = r + add_ref[...]
            if has_fin:
                r = finish[0](r, fin_ref[...])
            o_ref[...] = r.astype(o_ref.dtype)

        o_ref = refs[n_in + n_ci]
        acc_ref = refs[n_in + n_ci + 1 + n_co] if nk > 1 else None
        c_ins = refs[n_in:n_in + n_ci]
        c_outs = refs[n_in + n_ci + 1:n_in + n_ci + 1 + n_co]
        c_sems = refs[n_in + n_ci + 1 + n_co + (nk > 1):]
        i, j, k = pl.program_id(0), pl.program_id(1), pl.program_id(2)

        if carry:
            @pl.when((i == 0) & (j == 0) & (k == 0))
            def _():
                carry.start(c_ins, c_outs, c_sems)

        part = lax.dot_general(a_ref[...], b_ref[...], dims, preferred_element_type=F32)
        if nk == 1:
            store(part)
        else:
            @pl.when(k == 0)
            def _():
                acc_ref[...] = part

            @pl.when((k > 0) & (k < nk - 1))
            def _():
                acc_ref[...] += part

            @pl.when(k == nk - 1)
            def _():
                store(acc_ref[...] + part)

        if carry:
            @pl.when((i == grid[0] - 1) & (j == grid[1] - 1) & (k == nk - 1))
            def _():
                carry.finish(c_ins, c_outs, c_sems)

    ins = [a, b] + ([add] if has_add else []) + ([finish[1]] if has_fin else []) + (carry.ins if carry else [])
    in_specs = ([a_spec, b_spec] + ([o_spec] if has_add else [])
                + ([pl.BlockSpec((tm, finish[1].shape[1]), lambda i, j, k: (i, 0))] if has_fin else []) + [ANY] * n_ci)
    res = pl.pallas_call(
        body, name=name, grid=grid,
        in_specs=in_specs, out_specs=[o_spec] + [ANY] * n_co,
        out_shape=[jax.ShapeDtypeStruct(o_shape, out_dtype)] + (carry.outs if carry else []),
        scratch_shapes=([pltpu.VMEM((tm, tn), F32)] if nk > 1 else []) + (carry.sems if carry else []),
        input_output_aliases=carry.io_aliases(n_in, 1) if carry else {},
        compiler_params=_params(("arbitrary",) * 3 if carry else ("parallel", "parallel", "arbitrary")),
    )(*ins)
    return (res[0], res[1:]) if carry else res[0]


def _rows(body, name, n_rows, tm, ins, outs, accs=(), carry=None):
    grid = (n_rows // tm,)

    def halo(arr):
        return 16 if arr.dtype == BF16 else 8

    arrays, in_specs = [], []
    for spec in ins:
        kind, arr = spec[0], spec[1]
        arrays.append(arr)
        if kind == "row":
            _, _, cb, w = spec
            in_specs.append(pl.BlockSpec((tm, w), lambda i, cb=cb: (i, cb)))
        elif kind == "full":
            in_specs.append(pl.BlockSpec(arr.shape, lambda i, nd=arr.ndim: (0,) * nd))
        elif kind == "prev":
            _, _, cb, w = spec
            h = halo(arr)
            in_specs.append(pl.BlockSpec((h, w), lambda i, cb=cb, per=tm // h: (jnp.maximum(i * per - 1, 0), cb)))
        else:
            _, _, cb, w = spec
            h = halo(arr)
            in_specs.append(pl.BlockSpec((h, w), lambda i, cb=cb, per=tm // h, last=n_rows // h - 1:
                                         (jnp.minimum((i + 1) * per, last), cb)))
    out_shape = [jax.ShapeDtypeStruct((n_rows, w), dt) for (w, dt) in outs]
    out_specs = [pl.BlockSpec((tm, w), lambda i: (i, 0)) for (w, _) in outs]
    out_shape += [jax.ShapeDtypeStruct(s, F32) for s in accs]
    out_specs += [pl.BlockSpec(s, lambda i, nd=len(s): (0,) * nd) for s in accs]
    n_in, n_out, n_acc = len(ins), len(outs), len(accs)
    n_ci = len(carry.ins) if carry else 0
    n_co = len(carry.outs) if carry else 0

    def kernel_body(*refs):
        first = n_in + n_ci
        c_ins, c_outs, c_sems = refs[n_in:first], refs[first + n_out + n_acc:first + n_out + n_acc + n_co], refs[first + n_out + n_acc + n_co:]
        if carry:
            @pl.when(pl.program_id(0) == 0)
            def _():
                carry.start(c_ins, c_outs, c_sems)

        body(pl.program_id(0), refs[:n_in], refs[first:first + n_out], refs[first + n_out:first + n_out + n_acc])
        if carry:
            @pl.when(pl.program_id(0) == grid[0] - 1)
            def _():
                carry.finish(c_ins, c_outs, c_sems)

    res = pl.pallas_call(
        kernel_body, name=name, grid=grid, in_specs=in_specs + [ANY] * n_ci, out_specs=out_specs + [ANY] * n_co,
        out_shape=out_shape + (carry.outs if carry else []), scratch_shapes=carry.sems if carry else [],
        input_output_aliases=carry.io_aliases(n_in, n_out + n_acc) if carry else {},
        compiler_params=_params(("arbitrary",)),
    )(*arrays, *(carry.ins if carry else []))
    return (res[:n_out + n_acc], res[n_out + n_acc:]) if carry else res


def _acc_add(i, ref, val):
    @pl.when(i == 0)
    def _():
        ref[...] = val

    @pl.when(i > 0)
    def _():
        ref[...] += val


def _rope(t, tab, sign):
    c, sa, sb = tab[:, 0:128], tab[:, 128:256], tab[:, 256:384]
    rot = pltpu.roll(t, 96, 1) * sa + pltpu.roll(t, 32, 1) * sb
    return t * c + rot if sign > 0 else t * c - rot


def _ln_stats(r):
    mu = jnp.mean(r, axis=-1, keepdims=True)
    d = r - mu
    var = jnp.mean(d * d, axis=-1, keepdims=True)
    rstd = lax.rsqrt(var + LN_EPS)
    return d * rstd, rstd


def _ln_bwd(dxh, xh, rstd):
    m1 = jnp.mean(dxh, axis=-1, keepdims=True)
    m2 = jnp.mean(dxh * xh, axis=-1, keepdims=True)
    return rstd * (dxh - m1 - xh * m2)


def _modulate(x, scale, shift, name):
    S, D = x.shape

    def body(i, ins, outs, accs):
        outs[0][...] = (ins[0][...] * (1.0 + ins[1][...]) + ins[2][...]).astype(BF16)

    return _rows(body, name, S, _pick(S, (256, 128)), [("row", x, 0, D), ("full", scale), ("full", shift)], [(D, BF16)])[0]


def _rms_fwd(pq, tab, g_q, g_kv):
    S = pq.shape[0]

    def body(i, ins, outs, accs):
        pq_ref, tab_ref, gq_ref, gkv_ref = ins

        def rms(x, g):
            return x * lax.rsqrt(jnp.mean(x * x, axis=-1, keepdims=True) + RMS_EPS) * g

        outs[0][...] = rms(pq_ref[:, 0:Q_LORA], gq_ref[...]).astype(BF16)
        outs[1][...] = rms(pq_ref[:, Q_LORA:Q_LORA + KV_LORA], gkv_ref[...]).astype(BF16)
        outs[2][...] = _rope(pq_ref[:, Q_LORA + KV_LORA:QKV_A], tab_ref[...], 1).astype(BF16)

    return _rows(body, "rms_fwd", S, _pick(S, (256, 128)),
                 [("row", pq, 0, QKV_A), ("row", tab, 0, 384), ("full", g_q), ("full", g_kv)],
                 [(Q_LORA, BF16), (KV_LORA, BF16), (128, BF16)])


def _allowed(q0, k0, bq):
    row = q0 + lax.broadcasted_iota(jnp.int32, (bq, bq), 0)
    col = k0 + lax.broadcasted_iota(jnp.int32, (bq, bq), 1)
    return (col >> CHUNK_SHIFT) <= (row >> CHUNK_SHIFT)


ATTN_BLOCK = 512


HEADS_PER_STEP = 2


def _attn_fwd(q, kv, kr):
    S = q.shape[0]
    bq = min(ATTN_BLOCK, S)
    nq = S // bq
    G = HEADS_PER_STEP

    def body(q_ref, kv_ref, kr_ref, o_ref, lse_ref, kcat):
        qi = pl.program_id(1)

        @pl.when(qi == 0)
        def _():
            for g in range(G):
                kcat[g, :, 0:128] = kv_ref[:, g * 256:g * 256 + 128]
                kcat[g, :, 128:256] = kr_ref[...]

        qs = [q_ref[:, g * QK_PAD:(g + 1) * QK_PAD] for g in range(G)]

        def step(j, carry, masked):
            off = pl.multiple_of(j * bq, bq)
            rows = pl.ds(off, bq)
            mask = _allowed(qi * bq, off, bq) if masked else None
            out = []
            for g in range(G):
                m, l, acc = carry[g]
                s = lax.dot_general(qs[g], kcat[g, rows, :], NT, preferred_element_type=F32) * SCALE2
                if masked:
                    s = jnp.where(mask, s, -1e30)
                m_new = jnp.maximum(m, jnp.max(s, axis=1, keepdims=True))
                a = jnp.exp2(m - m_new)
                p = jnp.exp2(s - m_new)
                l = a * l + jnp.sum(p, axis=1, keepdims=True)
                acc = a * acc + jnp.dot(p.astype(BF16), kv_ref[rows, g * 256 + 128:(g + 1) * 256],
                                        preferred_element_type=F32)
                out.append((m_new, l, acc))
            return tuple(out)

        init = tuple((jnp.full((bq, 1), -1e30, F32), jnp.zeros((bq, 1), F32), jnp.zeros((bq, V_HEAD), F32))
                     for _ in range(G))
        below = lax.fori_loop(0, qi, lambda j, cr: step(j, cr, False), init)
        for g, (m, l, acc) in enumerate(step(qi, below, True)):
            o_ref[:, g * V_HEAD:(g + 1) * V_HEAD] = (acc / l).astype(BF16)
            lse_ref[g] = m + jnp.log2(l)

    return pl.pallas_call(
        body, name="attn_fwd", grid=(N_HEADS // G, nq),
        in_specs=[pl.BlockSpec((bq, G * QK_PAD), lambda h, i: (i, h)),
                  pl.BlockSpec((S, G * 256), lambda h, i: (0, h)),
                  pl.BlockSpec((S, 128), lambda h, i: (0, 0))],
        out_specs=[pl.BlockSpec((bq, G * V_HEAD), lambda h, i: (i, h)),
                   pl.BlockSpec((G, bq, 1), lambda h, i: (h, i, 0))],
        out_shape=[jax.ShapeDtypeStruct((S, N_HEADS * V_HEAD), BF16),
                   jax.ShapeDtypeStruct((N_HEADS, S, 1), F32)],
        scratch_shapes=[pltpu.VMEM((G, S, QK_PAD), BF16)],
        compiler_params=_params(("arbitrary", "arbitrary")),
    )(q, kv, kr)


def _attn_bwd(q, kv, kr, do, o, lse, tab, carry=None):
    S = q.shape[0]
    bq = min(ATTN_BLOCK, S)
    nq = S // bq

    n_ci = len(carry.ins) if carry else 0
    n_co = len(carry.outs) if carry else 0

    def body(*refs):
        q_ref, kn_ref, v_ref, kr_ref, do_ref, o_ref, lse_ref, tab_ref = refs[:8]
        dq_ref, dkv_ref, dkr_ref = refs[8 + n_ci:11 + n_ci]
        dq_acc, dk_acc, dv_acc, kcat, delta = refs[11 + n_ci + n_co:16 + n_ci + n_co]
        c_ins, c_outs, c_sems = refs[8:8 + n_ci], refs[11 + n_ci:11 + n_ci + n_co], refs[16 + n_ci + n_co:]
        h = pl.program_id(0)
        if carry:
            @pl.when(h == 0)
            def _():
                carry.start(c_ins, c_outs, c_sems)

        dq_acc[...] = jnp.zeros_like(dq_acc)
        dk_acc[...] = jnp.zeros_like(dk_acc)
        dv_acc[...] = jnp.zeros_like(dv_acc)
        kcat[:, 0:128] = kn_ref[...]
        kcat[:, 128:256] = kr_ref[...]
        for r in range(nq):
            rows = slice(r * bq, (r + 1) * bq)
            delta[rows, :] = jnp.sum(do_ref[rows, :].astype(F32) * o_ref[rows, :].astype(F32), axis=1, keepdims=True)

        def pair(i, j, masked):
            rows_i = pl.ds(pl.multiple_of(i * bq, bq), bq)
            rows_j = pl.ds(pl.multiple_of(j * bq, bq), bq)
            qv, dov, k = q_ref[rows_i, :], do_ref[rows_i, :], kcat[rows_j, :]
            s = lax.dot_general(qv, k, NT, preferred_element_type=F32) * SCALE2
            if masked:
                s = jnp.where(_allowed(i * bq, j * bq, bq), s, -1e30)
            p = jnp.exp2(s - lse_ref[0, rows_i, :])
            dv_acc[rows_j, :] += lax.dot_general(p.astype(BF16), dov, TN, preferred_element_type=F32)
            dp = lax.dot_general(dov, v_ref[rows_j, :], NT, preferred_element_type=F32)
            ds = (p * (dp - delta[rows_i, :]) * ATTN_SCALE).astype(BF16)
            dk_acc[rows_j, :] += lax.dot_general(ds, qv, TN, preferred_element_type=F32)
            dq_acc[rows_i, :] += jnp.dot(ds, k, preferred_element_type=F32)

        def kv_step(j, _):
            pair(j, j, True)

            def q_step(i, _):
                pair(i, j, False)
                return 0

            lax.fori_loop(j + 1, nq, q_step, 0)
            return 0

        lax.fori_loop(0, nq, kv_step, 0)

        for r in range(nq):
            rows = slice(r * bq, (r + 1) * bq)
            dq_ref[rows, 0:128] = dq_acc[rows, 0:128].astype(BF16)
            dq_ref[rows, 128:256] = _rope(dq_acc[rows, 128:256], tab_ref[rows, :], -1).astype(BF16)
        dkv_ref[:, 0:128] = dk_acc[:, 0:128].astype(BF16)
        dkv_ref[:, 128:256] = dv_acc[...].astype(BF16)

        @pl.when(h == 0)
        def _():
            dkr_ref[...] = dk_acc[:, 128:256]

        @pl.when(h > 0)
        def _():
            dkr_ref[...] += dk_acc[:, 128:256]

        @pl.when(h == N_HEADS - 1)
        def _():
            for r in range(nq):
                rows = slice(r * bq, (r + 1) * bq)
                dkr_ref[rows, :] = _rope(dkr_ref[rows, :], tab_ref[rows, :], -1)
            if carry:
                carry.finish(c_ins, c_outs, c_sems)

    W = N_HEADS * QK_PAD
    res = pl.pallas_call(
        body, name="attn_bwd", grid=(N_HEADS,),
        in_specs=[pl.BlockSpec((S, QK_PAD), lambda h: (0, h)),
                  pl.BlockSpec((S, 128), lambda h: (0, 2 * h)),
                  pl.BlockSpec((S, 128), lambda h: (0, 2 * h + 1)),
                  pl.BlockSpec((S, 128), lambda h: (0, 0)),
                  pl.BlockSpec((S, V_HEAD), lambda h: (0, h)),
                  pl.BlockSpec((S, V_HEAD), lambda h: (0, h)),
                  pl.BlockSpec((1, S, 1), lambda h: (h, 0, 0)),
                  pl.BlockSpec((S, 384), lambda h: (0, 0))] + [ANY] * n_ci,
        out_specs=[pl.BlockSpec((S, QK_PAD), lambda h: (0, h)),
                   pl.BlockSpec((S, QK_PAD), lambda h: (0, h)),
                   pl.BlockSpec((S, 128), lambda h: (0, 0))] + [ANY] * n_co,
        out_shape=[jax.ShapeDtypeStruct((S, W), BF16), jax.ShapeDtypeStruct((S, W), BF16),
                   jax.ShapeDtypeStruct((S, 128), F32)] + (carry.outs if carry else []),
        scratch_shapes=[pltpu.VMEM((S, QK_PAD), F32), pltpu.VMEM((S, QK_PAD), F32), pltpu.VMEM((S, V_HEAD), F32),
                        pltpu.VMEM((S, QK_PAD), BF16), pltpu.VMEM((S, 1), F32)]
        + (carry.sems if carry else []),
        input_output_aliases=carry.io_aliases(8, 3) if carry else {},
        compiler_params=_params(("arbitrary",)),
    )(q, kv, kv, kr, do, o, lse, tab, *(carry.ins if carry else []))
    return res[0], res[1], res[2], res[3:]


def _shift_down(cur, prev, i, n):
    tm, h = cur.shape[0], prev.shape[0]
    prev = jnp.where(i == 0, jnp.zeros_like(prev), prev)
    full = jnp.concatenate([prev, cur], axis=0)
    return pltpu.roll(full, n, 0)[h:h + tm, :]


def _shift_up(cur, nxt, i, last, n):
    tm, h = cur.shape[0], nxt.shape[0]
    nxt = jnp.where(i == last, jnp.zeros_like(nxt), nxt)
    full = jnp.concatenate([cur, nxt], axis=0)
    return pltpu.roll(full, tm + h - n, 0)[0:tm, :]


def _conv_fwd(pc, w_conv):
    S, D = pc.shape[0], pc.shape[1] // 3
    tm = _pick(S, (256, 128))

    def body(i, ins, outs, accs):
        b_ref, c_ref, x_ref, cp_ref, xp_ref, w_ref = ins
        z = c_ref[...].astype(F32) * x_ref[...].astype(F32)
        zp = cp_ref[...].astype(F32) * xp_ref[...].astype(F32)
        cz = w_ref[0:1, :] * _shift_down(z, zp, i, 2) + w_ref[1:2, :] * _shift_down(z, zp, i, 1) + w_ref[2:3, :] * z
        outs[0][...] = (b_ref[...].astype(F32) * cz).astype(BF16)

    return _rows(body, "conv_fwd", S, tm,
                 [("row", pc, 0, D), ("row", pc, 1, D), ("row", pc, 2, D), ("prev", pc, 1, D), ("prev", pc, 2, D),
                  ("full", w_conv)], [(D, BF16)])[0]


def _conv_bwd(dhb, pc, w_conv):
    S, D = dhb.shape
    tm = _pick(S, (256, 128))
    last = S // tm - 1

    def body(i, ins, outs, accs):
        g_ref, b_ref, c_ref, x_ref, cp_ref, xp_ref, gn_ref, bn_ref, w_ref = ins
        w0, w1, w2 = w_ref[0:1, :], w_ref[1:2, :], w_ref[2:3, :]
        c, x, g = c_ref[...].astype(F32), x_ref[...].astype(F32), g_ref[...].astype(F32)
        z = c * x
        zp = cp_ref[...].astype(F32) * xp_ref[...].astype(F32)
        z1, z2 = _shift_down(z, zp, i, 1), _shift_down(z, zp, i, 2)
        cz = w0 * z2 + w1 * z1 + w2 * z
        dcz = g * b_ref[...].astype(F32)
        dczn = gn_ref[...].astype(F32) * bn_ref[...].astype(F32)
        dz = w2 * dcz + w1 * _shift_up(dcz, dczn, i, last, 1) + w0 * _shift_up(dcz, dczn, i, last, 2)
        outs[0][:, 0:D] = (g * cz).astype(BF16)
        outs[0][:, D:2 * D] = (dz * x).astype(BF16)
        outs[0][:, 2 * D:3 * D] = (dz * c).astype(BF16)
        dw = jnp.concatenate([jnp.sum(dcz * z2, axis=0, keepdims=True), jnp.sum(dcz * z1, axis=0, keepdims=True),
                              jnp.sum(dcz * z, axis=0, keepdims=True)], axis=0)
        _acc_add(i, accs[0], dw)

    return _rows(body, "conv_bwd", S, tm,
                 [("row", dhb, 0, D), ("row", pc, 0, D), ("row", pc, 1, D), ("row", pc, 2, D),
                  ("prev", pc, 1, D), ("prev", pc, 2, D), ("next", dhb, 0, D), ("next", pc, 0, D), ("full", w_conv)],
                 [(3 * D, BF16)], [(3, D)])


def _merge_fwd(y_a, y_b, pg):
    S, D = y_a.shape

    def body(i, ins, outs, accs):
        ya, yb, ga, gb = ins
        outs[0][...] = (_sigmoid(ga[...].astype(F32)) * ya[...].astype(F32)
                        + _sigmoid(gb[...].astype(F32)) * yb[...].astype(F32)).astype(BF16)

    return _rows(body, "merge_fwd", S, _pick(S, (256, 128)),
                 [("row", y_a, 0, D), ("row", y_b, 0, D), ("row", pg, 0, D), ("row", pg, 1, D)], [(D, BF16)])[0]


def _merge_bwd(dm, y_a, y_b, pg):
    S, D = dm.shape

    def body(i, ins, outs, accs):
        d, ya, yb = ins[0][...].astype(F32), ins[1][...].astype(F32), ins[2][...].astype(F32)
        sa, sb = _sigmoid(ins[3][...].astype(F32)), _sigmoid(ins[4][...].astype(F32))
        outs[0][...] = (d * sa).astype(BF16)
        outs[1][...] = (d * sb).astype(BF16)
        outs[2][:, 0:D] = (d * ya * (sa * (1.0 - sa))).astype(BF16)
        outs[2][:, D:2 * D] = (d * yb * (sb * (1.0 - sb))).astype(BF16)

    return _rows(body, "merge_bwd", S, _pick(S, (256, 128)),
                 [("row", dm, 0, D), ("row", y_a, 0, D), ("row", y_b, 0, D), ("row", pg, 0, D), ("row", pg, 1, D)],
                 [(D, BF16), (D, BF16), (2 * D, BF16)])


def _ln1_fwd(x, mix, gate1, g, b, scale2, shift2):
    S, D = x.shape

    def body(i, ins, outs, accs):
        x_ref, mix_ref, gate_ref, g_ref, b_ref, sc_ref, sh_ref = ins
        xh, _ = _ln_stats(ALPHA * x_ref[...] + gate_ref[...] * mix_ref[...])
        x1 = xh * g_ref[...] + b_ref[...]
        outs[0][...] = x1
        outs[1][...] = (x1 * (1.0 + sc_ref[...]) + sh_ref[...]).astype(BF16)

    return _rows(body, "ln1_fwd", S, _pick(S, (256, 128)),
                 [("row", x, 0, D), ("row", mix, 0, D), ("full", gate1), ("full", g), ("full", b),
                  ("full", scale2), ("full", shift2)], [(D, F32), (D, BF16)])


def _swiglu_fwd(hh, carry=None):
    S, F = hh.shape[0], hh.shape[1] // 2

    def body(i, ins, outs, accs):
        hg = ins[0][...].astype(F32)
        outs[0][...] = (hg * _sigmoid(hg) * ins[1][...].astype(F32)).astype(BF16)

    res = _rows(body, "swiglu_fwd", S, _pick(S, (128,)), [("row", hh, 0, F), ("row", hh, 1, F)], [(F, BF16)], carry=carry)
    return (res[0][0], res[1]) if carry else res[0]


def _swiglu_bwd(dact, hh):
    S, F = dact.shape

    def body(i, ins, outs, accs):
        d, hg, hu = ins[0][...].astype(F32), ins[1][...].astype(F32), ins[2][...].astype(F32)
        sg = _sigmoid(hg)
        outs[0][:, 0:F] = (d * hu * (sg * (1.0 + hg * (1.0 - sg)))).astype(BF16)
        outs[0][:, F:2 * F] = (d * (hg * sg)).astype(BF16)

    return _rows(body, "swiglu_bwd", S, _pick(S, (128,)),
                 [("row", dact, 0, F), ("row", hh, 0, F), ("row", hh, 1, F)], [(2 * F, BF16)])[0]


def _ln2_loss_bwd(x1, ffn, gate2, g, b, target):
    S, D = x1.shape

    def body(i, ins, outs, accs):
        x1_ref, f_ref, gate_ref, g_ref, b_ref, t_ref = ins
        f = f_ref[...]
        xh, rstd = _ln_stats(ALPHA * x1_ref[...] + gate_ref[...] * f)
        e = xh * g_ref[...] + b_ref[...] - t_ref[...]
        dy = e * (1.0 / D)
        dr = _ln_bwd(dy * g_ref[...], xh, rstd)
        outs[0][...] = (gate_ref[...] * dr).astype(BF16)
        outs[1][...] = ALPHA * dr
        _acc_add(i, accs[0], jnp.full((1, 128), (0.5 / D) * jnp.sum(e * e), F32))
        _acc_add(i, accs[1], jnp.sum(dy * xh, axis=0, keepdims=True))
        _acc_add(i, accs[2], jnp.sum(dy, axis=0, keepdims=True))
        _acc_add(i, accs[3], jnp.sum(dr * f, axis=0, keepdims=True))

    return _rows(body, "ln2_loss_bwd", S, _pick(S, (256, 128)),
                 [("row", x1, 0, D), ("row", ffn, 0, D), ("full", gate2), ("full", g), ("full", b), ("row", target, 0, D)],
                 [(D, BF16), (D, F32)], [(1, 128), (1, D), (1, D), (1, D)])


def _ln1_bwd(x, mix, dx1a, du2, gate1, g, b, scale2):
    S, D = x.shape

    def body(i, ins, outs, accs):
        x_ref, mix_ref, da_ref, du_ref, gate_ref, g_ref, b_ref, sc_ref = ins
        mix, du = mix_ref[...], du_ref[...]
        xh, rstd = _ln_stats(ALPHA * x_ref[...] + gate_ref[...] * mix)
        x1 = xh * g_ref[...] + b_ref[...]
        dx1 = da_ref[...] + du * (1.0 + sc_ref[...])
        dr = _ln_bwd(dx1 * g_ref[...], xh, rstd)
        outs[0][...] = (gate_ref[...] * dr).astype(BF16)
        outs[1][...] = ALPHA * dr
        _acc_add(i, accs[0], jnp.sum(du, axis=0, keepdims=True))
        _acc_add(i, accs[1], jnp.sum(du * x1, axis=0, keepdims=True))
        _acc_add(i, accs[2], jnp.sum(dx1 * xh, axis=0, keepdims=True))
        _acc_add(i, accs[3], jnp.sum(dx1, axis=0, keepdims=True))
        _acc_add(i, accs[4], jnp.sum(dr * mix, axis=0, keepdims=True))

    return _rows(body, "ln1_bwd", S, _pick(S, (256, 128)),
                 [("row", x, 0, D), ("row", mix, 0, D), ("row", dx1a, 0, D), ("row", du2, 0, D),
                  ("full", gate1), ("full", g), ("full", b), ("full", scale2)],
                 [(D, BF16), (D, F32)], [(1, D)] * 5)


def _rms_bwd(d_rq, d_rkv, pq, dkr, g_q, g_kv):
    S = pq.shape[0]

    def body(i, ins, outs, accs):
        dq_ref, dkv_ref, pq_ref, dkr_ref, gq_ref, gkv_ref = ins

        def rms_bwd(dy, x, g):
            r = lax.rsqrt(jnp.mean(x * x, axis=-1, keepdims=True) + RMS_EPS)
            dyg = dy * g
            dx = r * dyg - x * (r * r * r) * jnp.mean(dyg * x, axis=-1, keepdims=True)
            return dx, jnp.sum(dy * (x * r), axis=0, keepdims=True)

        dxq, dgq = rms_bwd(dq_ref[...], pq_ref[:, 0:Q_LORA], gq_ref[...])
        dxkv, dgkv = rms_bwd(dkv_ref[...], pq_ref[:, Q_LORA:Q_LORA + KV_LORA], gkv_ref[...])
        outs[0][:, 0:Q_LORA] = dxq.astype(BF16)
        outs[0][:, Q_LORA:Q_LORA + KV_LORA] = dxkv.astype(BF16)
        outs[0][:, Q_LORA + KV_LORA:QKV_A] = dkr_ref[...].astype(BF16)
        _acc_add(i, accs[0], dgq)
        _acc_add(i, accs[1], dgkv)

    return _rows(body, "rms_bwd", S, _pick(S, (256, 128)),
                 [("row", d_rq, 0, Q_LORA), ("row", d_rkv, 0, KV_LORA), ("row", pq, 0, QKV_A), ("row", dkr, 0, 128),
                  ("full", g_q), ("full", g_kv)], [(QKV_A, BF16)], [(1, Q_LORA), (1, KV_LORA)])


def _dx_final(dxa, du, x, scale1):
    S, D = x.shape

    def body(i, ins, outs, accs):
        du = ins[1][...]
        outs[0][...] = ins[0][...] + du * (1.0 + ins[3][...])
        _acc_add(i, accs[0], jnp.sum(du, axis=0, keepdims=True))
        _acc_add(i, accs[1], jnp.sum(du * ins[2][...], axis=0, keepdims=True))

    return _rows(body, "dx_final", S, _pick(S, (256, 128)),
                 [("row", dxa, 0, D), ("row", du, 0, D), ("row", x, 0, D), ("full", scale1)],
                 [(D, F32)], [(1, D), (1, D)])


def _ada_fwd(c_all, w, bias):
    B, D = c_all.shape
    NA = w.shape[1]
    tn = _pick(NA, (512, 256, 128))

    def body(c_ref, w_ref, b_ref, o_ref):
        cv = c_ref[...]
        ca = (cv * _sigmoid(cv)).astype(BF16)
        o_ref[...] = jnp.dot(ca, w_ref[...].astype(BF16), preferred_element_type=F32) + b_ref[...]

    return pl.pallas_call(
        body, name="ada_fwd", grid=(NA // tn,),
        in_specs=[pl.BlockSpec((B, D), lambda j: (0, 0)), pl.BlockSpec((D, tn), lambda j: (0, j)),
                  pl.BlockSpec((1, tn), lambda j: (0, j))],
        out_specs=pl.BlockSpec((B, tn), lambda j: (0, j)),
        out_shape=jax.ShapeDtypeStruct((B, NA), F32),
        compiler_params=_params(("arbitrary",)),
    )(c_all, w, bias)


def _ada_bwd(c_all, dmod):
    B, D = c_all.shape
    NA = dmod.shape[1]
    tn = _pick(NA, (512, 256, 128))

    def body(c_ref, d_ref, o_ref):
        cv = c_ref[...]
        ca = (cv * _sigmoid(cv)).astype(BF16)
        o_ref[...] = lax.dot_general(ca, d_ref[...].astype(BF16), TN, preferred_element_type=F32)

    return pl.pallas_call(
        body, name="ada_bwd", grid=(NA // tn,),
        in_specs=[pl.BlockSpec((B, D), lambda j: (0, 0)), pl.BlockSpec((B, tn), lambda j: (0, j))],
        out_specs=pl.BlockSpec((D, tn), lambda j: (0, j)),
        out_shape=jax.ShapeDtypeStruct((D, NA), F32),
        compiler_params=_params(("arbitrary",)),
    )(c_all, dmod)


def _pack_rows(parts, n_rows, after=()):
    N = parts[0].shape[1]
    n = len(parts)

    def body(*refs):
        o_ref = refs[-1]
        o_ref[...] = jnp.zeros_like(o_ref)
        at = 0
        for r in refs[:n]:
            o_ref[at:at + r.shape[0], :] = r[...]
            at += r.shape[0]

    vmem = pl.BlockSpec(memory_space=pltpu.VMEM)
    return pl.pallas_call(body, name="pack_small", out_shape=jax.ShapeDtypeStruct((n_rows, N), F32),
                          in_specs=[vmem] * n + [ANY] * len(after), out_specs=vmem,
                          compiler_params=_params())(*parts, *after)


def _sum8(parts):
    _, R, N = parts.shape

    def body(p_ref, o_ref):
        acc = p_ref[0]
        for d in range(1, 8):
            acc = acc + p_ref[d]
        o_ref[...] = acc

    return pl.pallas_call(body, name="sum8", out_shape=jax.ShapeDtypeStruct((R, N), F32),
                          compiler_params=_params())(parts)


def _adam_math(w, g, m, v):
    m = ADAM_B1 * m + (1.0 - ADAM_B1) * g
    v = ADAM_B2 * v + (1.0 - ADAM_B2) * (g * g)
    delta = -ADAM_LR * ((m / ADAM_C1) / (jnp.sqrt(v / ADAM_C2) + ADAM_EPS) + ADAM_WD * w)
    return delta, m, v


def _adam(name, w, m, v, g, carry=None):
    R, C = w.shape
    tm = _row_tile(R, C * 4, 1 << 20)
    steps = R // tm
    n_ci = len(carry.ins) if carry else 0
    n_co = len(carry.outs) if carry else 0

    def body(*refs):
        w_ref, m_ref, v_ref, g_ref = refs[:4]
        d_ref, nm_ref, nv_ref = refs[4 + n_ci:7 + n_ci]
        c_ins, c_outs, c_sems = refs[4:4 + n_ci], refs[7 + n_ci:7 + n_ci + n_co], refs[7 + n_ci + n_co:]
        if carry:
            @pl.when(pl.program_id(0) == 0)
            def _():
                carry.start(c_ins, c_outs, c_sems)

        delta, nm, nv = _adam_math(w_ref[...], g_ref[...], m_ref[...], v_ref[...])
        d_ref[...] = delta
        nm_ref[...] = nm
        nv_ref[...] = nv
        if carry:
            @pl.when(pl.program_id(0) == steps - 1)
            def _():
                carry.finish(c_ins, c_outs, c_sems)

    spec = pl.BlockSpec((tm, C), lambda i: (i, 0))
    res = pl.pallas_call(
        body, name=name, grid=(steps,), in_specs=[spec] * 4 + [ANY] * n_ci, out_specs=[spec] * 3 + [ANY] * n_co,
        out_shape=[jax.ShapeDtypeStruct((R, C), F32)] * 3 + (carry.outs if carry else []),
        scratch_shapes=carry.sems if carry else [],
        input_output_aliases=carry.io_aliases(4, 3) if carry else {},
        compiler_params=_params(("arbitrary",)),
    )(w, m, v, g, *(carry.ins if carry else []))
    return (res[:3], res[3:]) if carry else res


def _adam_halves(name, w, m, v, mine, other, core, carry=None):
    R, C = w.shape
    Rh = mine.shape[0]
    tc = max(t for t in range(128, C + 1, 128) if C % t == 0 and R * t <= (3 << 17))
    steps = C // tc
    n_ci = len(carry.ins) if carry else 0
    n_co = len(carry.outs) if carry else 0

    def body(*refs):
        c_ref, w_ref, m_ref, v_ref, a_ref, b_ref = refs[:6]
        g_ref, d_ref, nm_ref, nv_ref = refs[6 + n_ci:10 + n_ci]
        c_ins, c_outs, c_sems = refs[6:6 + n_ci], refs[10 + n_ci:10 + n_ci + n_co], refs[10 + n_ci + n_co:]
        if carry:
            @pl.when(pl.program_id(0) == 0)
            def _():
                carry.start(c_ins, c_outs, c_sems)

        first = c_ref[0] == 0
        g = jnp.concatenate([jnp.where(first, a_ref[...], b_ref[...]),
                             jnp.where(first, b_ref[0:R - Rh, :], a_ref[0:R - Rh, :])], axis=0)
        delta, nm, nv = _adam_math(w_ref[...], g, m_ref[...], v_ref[...])
        g_ref[...] = g
        d_ref[...] = delta
        nm_ref[...] = nm
        nv_ref[...] = nv
        if carry:
            @pl.when(pl.program_id(0) == steps - 1)
            def _():
                carry.finish(c_ins, c_outs, c_sems)

    spec = pl.BlockSpec((R, tc), lambda i, c_ref: (0, i))
    h_spec = pl.BlockSpec((Rh, tc), lambda i, c_ref: (0, i))
    res = pl.pallas_call(
        body, name=name, out_shape=[jax.ShapeDtypeStruct((R, C), F32)] * 4 + (carry.outs if carry else []),
        grid_spec=pltpu.PrefetchScalarGridSpec(
            num_scalar_prefetch=1, grid=(steps,), in_specs=[spec, spec, spec, h_spec, h_spec] + [ANY] * n_ci,
            out_specs=[spec] * 4 + [ANY] * n_co, scratch_shapes=carry.sems if carry else []),
        input_output_aliases=carry.io_aliases(6, 4) if carry else {},
        compiler_params=_params(("arbitrary",)),
    )(core, w, m, v, mine, other, *(carry.ins if carry else []))
    return (res[:4], res[4:]) if carry else res


def _adam_small(name, w, m, v, g):
    def body(w_ref, m_ref, v_ref, g_ref, d_ref, nm_ref, nv_ref):
        delta, nm, nv = _adam_math(w_ref[...], g_ref[...], m_ref[...], v_ref[...])
        d_ref[...] = delta
        nm_ref[...] = nm
        nv_ref[...] = nv

    return pl.pallas_call(body, name=name, out_shape=[jax.ShapeDtypeStruct(w.shape, F32)] * 3,
                          compiler_params=_params())(w, m, v, g)


def _place():
    return lax.axis_index("x"), lax.axis_index("y"), lax.axis_index("c")


def _other_chips(x, y):
    return [(1 - x, y), (x, 1 - y), (1 - x, 1 - y)]


def _all_gather8(blk, name):
    R, N = blk.shape

    def body(x_ref, out_ref, send_sems, recv_sems, local_sem):
        x, y, c = _place()
        me = 4 * x + 2 * y + c
        mine = pltpu.make_async_copy(x_ref, out_ref.at[me], local_sem)
        mine.start()
        flips = [(j >> 2 & 1, j >> 1 & 1, j & 1) for j in range(1, 8)]
        peers = [((1 - x) if fx else x, (1 - y) if fy else y, (1 - c) if fc else c) for fx, fy, fc in flips]
        sends = []
        for j, peer in enumerate(peers):
            cp = pltpu.make_async_remote_copy(src_ref=x_ref, dst_ref=out_ref.at[me], send_sem=send_sems.at[j],
                                              recv_sem=recv_sems.at[j], device_id=peer, device_id_type=MESH)
            cp.start()
            sends.append(cp)
        for j, (px, py, pc) in enumerate(peers):
            pltpu.make_async_remote_copy(src_ref=x_ref, dst_ref=out_ref.at[4 * px + 2 * py + pc],
                                         send_sem=send_sems.at[j], recv_sem=recv_sems.at[j],
                                         device_id=(px, py, pc), device_id_type=MESH).wait_recv()
        for cp in sends:
            cp.wait_send()
        mine.wait()

    return pl.pallas_call(
        body, name=name, out_shape=jax.ShapeDtypeStruct((8, R, N), F32),
        in_specs=[pl.BlockSpec(memory_space=pltpu.VMEM)], out_specs=pl.BlockSpec(memory_space=pltpu.VMEM),
        scratch_shapes=[pltpu.SemaphoreType.DMA((7,)), pltpu.SemaphoreType.DMA((7,)), pltpu.SemaphoreType.DMA],
        compiler_params=_params(),
    )(blk)


def _piece(rows, piece):
    i, n, k = piece if len(piece) == 3 else (piece[0], piece[1], 1)
    assert rows % 16 == 0 and rows // 16 >= n, (rows, piece)
    lo, hi = (rows // 16 * i // n) * 16, (rows // 16 * (i + k) // n) * 16
    return pl.ds(lo, hi - lo)


def _gather_plan(shards, piece=(0, 1), into=None, ici=True):
    n = len(shards)

    def parts(ins, outs, sems):
        s1, r1, s2, r2, loc = sems
        x, y, c = _place()
        me = 2 * x + y
        chips = _other_chips(x, y)
        sib = (x, y, 1 - c)

        def rows(k):
            return _piece(shards[k].shape[1], piece)

        def ici_copy(k, j, slab, to):
            return pltpu.make_async_remote_copy(src_ref=ins[k].at[c, rows(k)], dst_ref=outs[k].at[slab, c, rows(k)],
                                                send_sem=s1.at[3 * k + j], recv_sem=r1.at[3 * k + j],
                                                device_id=to, device_id_type=MESH)

        def d2d(k, j, slab, half):
            return pltpu.make_async_remote_copy(src_ref=outs[k].at[slab, half, rows(k)],
                                                dst_ref=outs[k].at[slab, half, rows(k)],
                                                send_sem=s2.at[3 * k + j], recv_sem=r2.at[3 * k + j],
                                                device_id=sib, device_id_type=MESH)

        def own(k):
            return pltpu.make_async_remote_copy(src_ref=ins[k].at[:, rows(k)], dst_ref=outs[k].at[me, :, rows(k)],
                                                send_sem=loc.at[2 * k], recv_sem=loc.at[2 * k + 1],
                                                device_id=sib, device_id_type=MESH)

        return c, me, chips, ici_copy, d2d, own

    def start(ins, outs, sems):
        c, me, chips, ici_copy, d2d, own = parts(ins, outs, sems)
        for k in range(n):
            for j, (px, py) in enumerate(chips):
                (ici_copy(k, j, me, (px, py, c)) if ici else d2d(k, j, 2 * px + py, c)).start()
        for k in range(n):
            own(k).start()

    def finish(ins, outs, sems):
        c, me, chips, ici_copy, d2d, own = parts(ins, outs, sems)
        if ici:
            for k in range(n):
                for j, (px, py) in enumerate(chips):
                    ici_copy(k, j, 2 * px + py, (px, py, c)).wait_recv()
                    d2d(k, j, 2 * px + py, c).start()
        for k in range(n):
            for j, (px, py) in enumerate(chips):
                d2d(k, j, 2 * px + py, 1 - c).wait_recv()
        for k in range(n):
            own(k).wait()
            for j, (px, py) in enumerate(chips):
                if ici:
                    ici_copy(k, j, me, (px, py, c)).wait_send()
                d2d(k, j, 2 * px + py, c).wait_send()

    return _Plan(list(shards) + list(into or []), [jax.ShapeDtypeStruct((4,) + a.shape, a.dtype) for a in shards],
                 [pltpu.SemaphoreType.DMA((3 * n,))] * 4 + [pltpu.SemaphoreType.DMA((2 * n,))], start, finish,
                 aliases={n + k: k for k in range(n)} if into else None)


def _pair_plan(parts):
    n = len(parts)

    def copies(ins, outs, sems):
        send_sems, recv_sems = sems
        x, y, c = _place()
        return [pltpu.make_async_remote_copy(src_ref=ins[k].at[p, 1 - c], dst_ref=outs[k].at[p],
                                             send_sem=send_sems.at[4 * k + p], recv_sem=recv_sems.at[4 * k + p],
                                             device_id=(x, y, 1 - c), device_id_type=MESH)
                for k in range(n) for p in range(4)]

    def start(ins, outs, sems):
        for cp in copies(ins, outs, sems):
            cp.start()

    def finish(ins, outs, sems):
        for cp in copies(ins, outs, sems):
            cp.wait()

    return _Plan(parts, [jax.ShapeDtypeStruct((4,) + a.shape[2:], a.dtype) for a in parts],
                 [pltpu.SemaphoreType.DMA((4 * n,))] * 2, start, finish)


def _sibling_plan(arrs):
    n = len(arrs)

    def copies(ins, outs, sems):
        send_sems, recv_sems = sems
        x, y, c = _place()
        return [pltpu.make_async_remote_copy(src_ref=ins[k], dst_ref=outs[k], send_sem=send_sems.at[k],
                                             recv_sem=recv_sems.at[k], device_id=(x, y, 1 - c), device_id_type=MESH)
                for k in range(n)]

    def start(ins, outs, sems):
        for cp in copies(ins, outs, sems):
            cp.start()

    def finish(ins, outs, sems):
        for cp in copies(ins, outs, sems):
            cp.wait()

    return _Plan(arrs, [jax.ShapeDtypeStruct(a.shape, a.dtype) for a in arrs],
                 [pltpu.SemaphoreType.DMA((n,))] * 2, start, finish)


def _scatter_copies(arrs):
    def copies(ins, land, send_sems, recv_sems):
        x, y, c = _place()
        return [pltpu.make_async_remote_copy(src_ref=ins[k].at[2 * px + py], dst_ref=land[k].at[j],
                                             send_sem=send_sems.at[3 * k + j], recv_sem=recv_sems.at[3 * k + j],
                                             device_id=(px, py, c), device_id_type=MESH)
                for k in range(len(arrs)) for j, (px, py) in enumerate(_other_chips(x, y))]

    return copies, [lax.empty((3,) + a.shape[1:], a.dtype) for a in arrs]


def _gather_copies(shards):
    def copies(ins, land, send_sems, recv_sems):
        x, y, c = _place()
        return [pltpu.make_async_remote_copy(src_ref=ins[k].at[c], dst_ref=land[k].at[2 * x + y, c],
                                             send_sem=send_sems.at[3 * k + j], recv_sem=recv_sems.at[3 * k + j],
                                             device_id=(px, py, c), device_id_type=MESH)
                for k in range(len(shards)) for j, (px, py) in enumerate(_other_chips(x, y))]

    return copies, [lax.empty((4,) + a.shape, a.dtype) for a in shards]


def _split_start(arrs, copies_lands, ride, name, after=()):
    copies, lands = copies_lands
    n = len(arrs)
    rides = list(ride) if isinstance(ride, (list, tuple)) else [ride]
    n_thru = 2 * n + len(rides)

    def body(*refs):
        first_out = n_thru + len(after)
        for cp in copies(refs[:n], refs[n:2 * n], refs[first_out], refs[first_out + 1]):
            cp.start()

    hbm = [pltpu.with_memory_space_constraint(a, pltpu.HBM) for a in list(arrs) + lands + rides]
    res = pl.pallas_call(
        body, name=name,
        out_shape=[pltpu.SemaphoreType.DMA((3 * n,)), pltpu.SemaphoreType.DMA((3 * n,))]
        + [pltpu.HBM(a.shape, a.dtype) for a in hbm],
        in_specs=[HBM_SPEC] * n_thru + [ANY] * len(after),
        out_specs=[SEM_SPEC, SEM_SPEC] + [HBM_SPEC] * n_thru,
        input_output_aliases={i: 2 + i for i in range(n_thru)},
        compiler_params=pltpu.CompilerParams(has_side_effects=pltpu.SideEffectType.DATAFLOW_SIDE_EFFECTING),
    )(*hbm, *after)
    return res[0], res[1], res[2:2 + n], res[2 + n:2 + 2 * n], list(res[2 + 2 * n:])


def _split_wait(started, copies_lands, after, name):
    send_sems, recv_sems, arrs, lands, _ = started
    copies = copies_lands[0]
    n = len(arrs)

    def body(*refs):
        for cp in copies(refs[:n], refs[n:2 * n], refs[2 * n], refs[2 * n + 1]):
            cp.wait_send()
            cp.wait_recv()

    res = pl.pallas_call(
        body, name=name, out_shape=[pltpu.HBM(a.shape, a.dtype) for a in list(arrs) + list(lands)],
        in_specs=[HBM_SPEC] * (2 * n) + [SEM_SPEC, SEM_SPEC] + [ANY] * len(after), out_specs=[HBM_SPEC] * (2 * n),
        input_output_aliases={i: i for i in range(2 * n)},
        compiler_params=pltpu.CompilerParams(has_side_effects=pltpu.SideEffectType.DATAFLOW_SIDE_EFFECTING),
    )(*arrs, *lands, send_sems, recv_sems, *after)
    return list(res[:n]), list(res[n:])


def _add_pair(parts, sib, core, name):
    P4, _, Rh, C = parts.shape
    tm, tc = _tile2(Rh, C, 16)

    def body(c_ref, a_ref, b_ref, o_ref):
        o_ref[...] = (a_ref[0].astype(F32) + b_ref[...].astype(F32)).astype(BF16)

    spec = pl.BlockSpec((1, tm, tc), lambda p, i, j, c_ref: (p, i, j))
    return pl.pallas_call(
        body, name=name, out_shape=jax.ShapeDtypeStruct((P4, Rh, C), BF16),
        grid_spec=pltpu.PrefetchScalarGridSpec(
            num_scalar_prefetch=1, grid=(P4, Rh // tm, C // tc),
            in_specs=[pl.BlockSpec((1, 1, tm, tc), lambda p, i, j, c_ref: (p, c_ref[0], i, j)), spec], out_specs=spec),
        compiler_params=_params(("parallel",) * 3),
    )(core, parts, sib)


def _sum_slabs(pre, recv, chip, name):
    _, Rh, C = pre.shape
    tm, tc = _tile2(Rh, C, 16)

    def body(me_ref, own_ref, r_ref, o_ref):
        acc = own_ref[0].astype(F32)
        for j in range(3):
            acc = acc + r_ref[j].astype(F32)
        o_ref[...] = acc

    return pl.pallas_call(
        body, name=name, out_shape=jax.ShapeDtypeStruct((Rh, C), F32),
        grid_spec=pltpu.PrefetchScalarGridSpec(
            num_scalar_prefetch=1, grid=(Rh // tm, C // tc),
            in_specs=[pl.BlockSpec((1, tm, tc), lambda i, j, me_ref: (me_ref[0], i, j)),
                      pl.BlockSpec((3, tm, tc), lambda i, j, me_ref: (0, i, j))],
            out_specs=pl.BlockSpec((tm, tc), lambda i, j, me_ref: (i, j))),
        compiler_params=_params(("parallel", "parallel")),
    )(chip, pre, recv)


def kernel(x, c, positions, w_ada, b_ada, w_in, g_q_a, w_q_b, g_kv_a, w_kv_b, w_o_a, w_conv, w_o_b, w_o, ln1_g, ln1_b, w_ffn_in, w_ffn_out, ln2_g, ln2_b, loss_target, m_w_ada, m_b_ada, m_w_in, m_g_q_a, m_w_q_b, m_g_kv_a, m_w_kv_b, m_w_o_a, m_w_conv, m_w_o_b, m_w_o, m_ln1_g, m_ln1_b, m_w_ffn_in, m_w_ffn_out, m_ln2_g, m_ln2_b, v_w_ada, v_b_ada, v_w_in, v_g_q_a, v_w_q_b, v_g_kv_a, v_w_kv_b, v_w_o_a, v_w_conv, v_w_o_b, v_w_o, v_ln1_g, v_ln1_b, v_w_ffn_in, v_w_ffn_out, v_ln2_g, v_ln2_b):
    S, D = x.shape[1], x.shape[2]
    F = w_ffn_out.shape[1] * 4
    ax, ay, ac = _place()
    chip = 2 * ax + ay
    dev = 4 * ax + 2 * ay + ac
    x2, tgt = x[0], loss_target[0]
    w_ada2, w_in2, w_q_b2, w_kv_b2 = w_ada[0], w_in[0], w_q_b[0], w_kv_b[0]
    w_o_a2, w_o_b2, w_o2, w_ffn_in2, w_ffn_out2 = w_o_a[0], w_o_b[0], w_o[0], w_ffn_in[0], w_ffn_out[0]
    NA = w_ada2.shape[1]
    CW = w_conv.shape[2]

    inv_freq = 1.0 / (ROPE_THETA ** (jnp.arange(0, QK_ROPE, 2, dtype=F32) / QK_ROPE))
    ang = positions[0].astype(F32)[:, None] * inv_freq
    cos, sin = jnp.cos(ang), jnp.sin(ang)
    z32, z64, z96 = jnp.zeros((S, 32), F32), jnp.zeros((S, 64), F32), jnp.zeros((S, 96), F32)
    tab = jnp.concatenate([cos, cos, z64, -sin, z96, z32, sin, z64], axis=1)

    def halves(a):
        return a.reshape(2, a.shape[0] // 2, a.shape[1])

    def whole(g):
        return g.reshape(4, 2 * g.shape[2], g.shape[3])

    def cols(g):
        return jnp.transpose(g, (1, 0, 2)).reshape(g.shape[1], 4 * g.shape[2])

    w_inT, m_w_inT, v_w_inT = w_in2.T, m_w_in[0].T, v_w_in[0].T
    CS = w_inT.shape[0]
    CSP = -(-CS // 32) * 32
    sh_in = halves(jnp.pad(w_inT.astype(BF16), ((0, CSP - CS), (0, 0))))
    sh_qb, sh_kvb, sh_oa, sh_ob, sh_o, sh_fi, sh_fo = (
        halves(w.astype(BF16)) for w in (w_q_b2, w_kv_b2, w_o_a2, w_o_b2, w_o2, w_ffn_in2, w_ffn_out2))
    c_all = _all_gather8(c, "gather_c").reshape(8, D)
    wconv_all = _all_gather8(w_conv[0], "gather_wconv")
    w_conv_full = jnp.transpose(wconv_all[0::2], (1, 0, 2)).reshape(3, D)
    b_sh = lax.dynamic_slice(b_ada, (0, chip * NA), (1, NA))
    mod_sh = _ada_fwd(c_all, w_ada2, b_sh)
    mod_all = _all_gather8(mod_sh, "gather_mod")
    mod = lax.dynamic_slice(mod_all[0::2], (0, dev, 0), (4, 1, NA)).reshape(6, D)
    shift1, scale1, gate1, shift2, scale2, gate2 = (mod[k:k + 1] for k in range(6))

    g_in, shift1, w_conv_full = _run_plan(_gather_plan([sh_in]), "gather_first", ride=[shift1, w_conv_full])
    g_in = whole(g_in)
    sh_a1, sh_a2 = [sh_qb, sh_kvb], [sh_oa, sh_ob, sh_o]
    cl_a1, cl_a2, cl_fi, cl_fo = (_gather_copies(g) for g in (sh_a1, sh_a2, [sh_fi], [sh_fo]))
    st_a1 = _split_start(sh_a1, cl_a1, shift1, "gather_a1_start")
    st_a2 = _split_start(sh_a2, cl_a2, st_a1[4], "gather_a2_start")
    shift1 = st_a2[4][0]

    def in_rows(lo, hi):
        parts = [g_in[p, max(lo, p * CS) - p * CS:min(hi, (p + 1) * CS) - p * CS]
                 for p in range(4) if max(lo, p * CS) < min(hi, (p + 1) * CS)]
        return parts[0] if len(parts) == 1 else jnp.concatenate(parts, axis=0)

    n_qkv = Q_LORA + KV_LORA + QK_ROPE
    W_qkvT = jnp.pad(in_rows(0, n_qkv), ((0, QKV_A - n_qkv), (0, 0)))
    W_convT = in_rows(n_qkv, n_qkv + 3 * D)
    W_gateT = in_rows(n_qkv + 3 * D, n_qkv + 5 * D)

    u = _modulate(x2, scale1, shift1, "modulate1")
    pq = _matmul(u, W_qkvT, "nt", F32, "proj_qkv")
    pc = _matmul(u, W_convT, "nt", BF16, "proj_conv")
    sh_a1, la1 = _split_wait(st_a1, cl_a1, [pc], "gather_a1_wait")
    pg, (g_qb, g_kvb) = _matmul(u, W_gateT, "nt", BF16, "proj_gate", carry=_gather_plan(sh_a1, into=la1, ici=False))
    st_fi = _split_start([sh_fi], cl_fi, g_q_a, "gather_fi_start", after=[pg])
    W_qb = jnp.pad(cols(whole(g_qb)).reshape(Q_LORA, N_HEADS, QK_NOPE + QK_ROPE),
                   ((0, 0), (0, 0), (0, QK_PAD - QK_NOPE - QK_ROPE))).reshape(Q_LORA, N_HEADS * QK_PAD)
    W_kvb = cols(whole(g_kvb))
    rq, rkv, kr = _rms_fwd(pq, tab, st_fi[4][0], g_kv_a)
    kv = _matmul(rkv, W_kvb, "nn", BF16, "kv_b")
    sh_a2, la2 = _split_wait(st_a2, cl_a2, [kv], "gather_a2_wait")
    def rope_heads(r, t):
        return jnp.concatenate([r[:, lo:lo + 128] if lo % QK_PAD == 0 else _rope(r[:, lo:lo + 128], t, 1)
                                for lo in range(0, r.shape[1], 128)], axis=1)

    q, (g_oa, g_ob, g_o) = _matmul(rq, W_qb, "nn", BF16, "q_b", carry=_gather_plan(sh_a2, into=la2, ici=False),
                                   finish=(rope_heads, tab))
    o, lse = _attn_fwd(q, kv, kr)
    W_oa, W_ob, W_o = (g.reshape(-1, D) for g in (g_oa, g_ob, g_o))
    hb = _conv_fwd(pc, w_conv_full)
    sh_fi_t, lfi = _split_wait(st_fi, cl_fi, [o], "gather_fi_wait")
    y_b, g_fi = _matmul(hb, W_ob, "nn", BF16, "o_b", carry=_gather_plan(sh_fi_t, (0, 2), into=lfi, ici=False))
    y_a, (g_fi,) = _matmul(o, W_oa, "nn", BF16, "o_a", carry=_gather_plan(sh_fi_t, (1, 2), into=g_fi, ici=False))
    st_fo = _split_start([sh_fo], cl_fo, ln1_g, "gather_fo_start", after=[y_b])
    merged = _merge_fwd(y_a, y_b, pg)
    mix = _matmul(merged, W_o, "nn", F32, "w_o")
    W_fi = whole(g_fi)
    x1, u2 = _ln1_fwd(x2, mix, gate1, st_fo[4][0], ln1_b, scale2, shift2)
    hh = _matmul(u2, W_fi, "nn", BF16, "ffn_in", shards="b")
    sh_fo_t, lfo = _split_wait(st_fo, cl_fo, [hh], "gather_fo_wait")
    act, (g_fo,) = _swiglu_fwd(hh, carry=_gather_plan(sh_fo_t, into=lfo, ici=False))
    W_fo = g_fo.reshape(F, D)
    ffn = _matmul(act, W_fo, "nn", F32, "ffn_out")

    core_i = ac.astype(jnp.int32).reshape(1)
    chip_i = chip.astype(jnp.int32).reshape(1)

    def uncols(g):
        return jnp.transpose(g.reshape(g.shape[0], 4, g.shape[1] // 4), (1, 0, 2))

    def slabs(p):
        return p.reshape(4, 2, p.shape[1] // 2, p.shape[2])

    def add_pairs(parts, sibs, nms):
        return [_add_pair(a, b, core_i, "add_pair_" + nm) for a, b, nm in zip(parts, sibs, nms)]

    def sum_all(pre, recv, nms):
        return [_sum_slabs(a, r, chip_i, "sum_slabs_" + nm) for a, r, nm in zip(pre, recv, nms)]

    dffn, dx1a, loss_acc, d_ln2_g, d_ln2_b, d_gate2 = _ln2_loss_bwd(x1, ffn, gate2, ln2_g, ln2_b, tgt)
    dW_fo = _matmul(act, dffn, "tn", BF16, "d_w_ffn_out")
    p_fo = [slabs(dW_fo.reshape(4, -1, D))]
    dact, s_fo = _matmul(dffn, W_fo, "nt", BF16, "d_act", carry=_pair_plan(p_fo))
    pre_fo = add_pairs(p_fo, s_fo, ["w_ffn_out"])
    cs_fo = _scatter_copies(pre_fo)
    st_sfo = _split_start(pre_fo, cs_fo, scale2, "scatter_fo_start")
    dhh = _swiglu_bwd(dact, hh)
    dW_fi = _matmul(u2, dhh, "tn", BF16, "d_w_ffn_in", shards="o")
    p_fi = [slabs(dW_fi)]
    du2, s_fi = _matmul(dhh, W_fi, "nt", F32, "d_u2", carry=_pair_plan(p_fi), shards="b")
    pre_fi = add_pairs(p_fi, s_fi, ["w_ffn_in"])
    cs_fi = _scatter_copies(pre_fi)
    st_sfi = _split_start(pre_fi, cs_fi, st_sfo[4], "scatter_fi_start")
    dmix, dxa, d_shift2, d_scale2, d_ln1_g, d_ln1_b, d_gate1 = _ln1_bwd(x2, mix, dx1a, du2, gate1, ln1_g, ln1_b, st_sfi[4][0])
    dW_o = _matmul(merged, dmix, "tn", BF16, "d_w_o")
    dmerged = _matmul(dmix, W_o, "nt", BF16, "d_merged")
    dy_a, dy_b, dgate = _merge_bwd(dmerged, y_a, y_b, pg)
    dW_oa = _matmul(o, dy_a, "tn", BF16, "d_w_o_a")
    do = _matmul(dy_a, W_oa, "nt", BF16, "d_o")
    dW_ob = _matmul(hb, dy_b, "tn", BF16, "d_w_o_b")
    p_mid = [slabs(g.reshape(4, -1, D)) for g in (dW_oa, dW_ob, dW_o)]
    dhb, s_mid = _matmul(dy_b, W_ob, "nt", BF16, "d_hb", carry=_pair_plan(p_mid))
    pre_mid = add_pairs(p_mid, s_mid, ["w_o_a", "w_o_b", "w_o"])
    cs_mid = _scatter_copies(pre_mid)
    st_smid = _split_start(pre_mid, cs_mid, w_conv_full, "scatter_mid_start")
    dconv, d_wconv = _conv_bwd(dhb, pc, st_smid[4][0])
    dq, dkv, dkr, _ = _attn_bwd(q, kv, kr, do, o, lse, tab, carry=_token_plan(st_smid[4][0]))
    names_a = ["w_ffn_out", "w_ffn_in", "w_o_a", "w_o_b", "w_o"]
    dW_qb = _matmul(rq, dq, "tn", BF16, "d_w_q_b")
    d_rq = _matmul(dq, W_qb, "nt", F32, "d_rq")
    dW_kvb = _matmul(rkv, dkv, "tn", BF16, "d_w_kv_b")
    d_rkv = _matmul(dkv, W_kvb, "nt", F32, "d_rkv")
    dqkv, d_g_q, d_g_kv = _rms_bwd(d_rq, d_rkv, pq, dkr, g_q_a, g_kv_a)
    dW_qkvT = _matmul(dqkv, u, "tn", BF16, "d_w_qkv")
    dW_convT = _matmul(dconv, u, "tn", BF16, "d_w_conv")
    dW_gateT = _matmul(dgate, u, "tn", BF16, "d_w_gate")
    pre_fo, r_fo = _split_wait(st_sfo, cs_fo, [dW_qkvT], "scatter_fo_wait")
    pre_fi, r_fi = _split_wait(st_sfi, cs_fi, [dW_qkvT], "scatter_fi_wait")
    pre_mid, r_mid = _split_wait(st_smid, cs_mid, [dW_qkvT], "scatter_mid_wait")
    fin_a = sum_all(pre_fo + pre_fi + pre_mid, r_fo + r_fi + r_mid, names_a)
    srcs = [(0, dW_qkvT[:n_qkv]), (n_qkv, dW_convT), (n_qkv + 3 * D, dW_gateT)]
    rows_of = []
    for p in range(4):
        for lo, src in srcs:
            a, b = max(lo, p * CS), min(lo + src.shape[0], (p + 1) * CS)
            if a < b:
                rows_of.append(src[a - lo:b - lo])
        rows_of.append(jnp.zeros((CSP - CS, D), BF16))
    dW_inT = jnp.concatenate(rows_of, axis=0).reshape(4, CSP, D)
    dW_qb_u = dW_qb.reshape(Q_LORA, N_HEADS, QK_PAD)[:, :, :QK_NOPE + QK_ROPE].reshape(Q_LORA, -1)
    names_b = ["w_in", "w_q_b", "w_kv_b"]
    p_b = [slabs(dW_inT), slabs(uncols(dW_qb_u)), slabs(uncols(dW_kvb))]
    du, s_b = _matmul(dqkv, W_qkvT, "nn", F32, "d_u_qkv", carry=_pair_plan(p_b))
    pre_b = add_pairs(p_b, s_b, names_b)
    cs_b = _scatter_copies(pre_b)
    st_b = _split_start(pre_b, cs_b, scale1, "scatter_last_start")
    du, fs_a = _matmul(dconv, W_convT, "nn", F32, "d_u_conv", add=du, carry=_sibling_plan(fin_a))
    du = _matmul(dgate, W_gateT, "nn", F32, "d_u_gate", add=du)
    grad_x, d_shift1, d_scale1 = _dx_final(dxa, du, x2, st_b[4][0])

    big = {}
    ws = dict(w_in=(w_inT, m_w_inT, v_w_inT), w_q_b=(w_q_b2, m_w_q_b[0], v_w_q_b[0]),
              w_kv_b=(w_kv_b2, m_w_kv_b[0], v_w_kv_b[0]), w_o_a=(w_o_a2, m_w_o_a[0], v_w_o_a[0]),
              w_o_b=(w_o_b2, m_w_o_b[0], v_w_o_b[0]), w_o=(w_o2, m_w_o[0], v_w_o[0]),
              w_ffn_in=(w_ffn_in2, m_w_ffn_in[0], v_w_ffn_in[0]), w_ffn_out=(w_ffn_out2, m_w_ffn_out[0], v_w_ffn_out[0]))

    def adam_of(nm, a, b, carry=None):
        w_, m_, v_ = ws[nm]
        return _adam_halves("adam_" + nm, w_, m_, v_, a, b, core_i, carry)

    for nm, a, b in zip(names_a, fin_a, fs_a):
        big[nm] = adam_of(nm, a, b, _token_plan(st_b[4][0]))[0]
    done = [big[nm][1] for nm in names_a] + [grad_x]
    pre_b, r_b = _split_wait(st_b, cs_b, done, "scatter_last_wait")
    fin_b = sum_all(pre_b, r_b, names_b)
    fs_b = _run_plan(_sibling_plan(fin_b), "sibling_last")
    for nm, a, b in zip(names_b, fin_b, fs_b):
        big[nm] = adam_of(nm, a, b)

    def pad_d(v):
        return jnp.pad(v, ((0, 0), (0, D - v.shape[1])))

    small = _pack_rows([d_ln1_g, d_ln1_b, d_ln2_g, d_ln2_b, pad_d(d_g_q), pad_d(d_g_kv), d_wconv,
                         d_shift1, d_scale1, d_gate1, d_shift2, d_scale2, d_gate2, pad_d(loss_acc)], 16, after=[pre_b[1]])
    small_all = _all_gather8(small, "gather_small")
    small_sum = _sum8(small_all)
    loss = small_sum[15, 0]
    g_ln1_g, g_ln1_b, g_ln2_g, g_ln2_b = (small_sum[k:k + 1] for k in range(4))
    g_g_q, g_g_kv = small_sum[4:5, :Q_LORA], small_sum[5:6, :KV_LORA]
    g_wconv = lax.dynamic_slice(small_sum[6:9], (0, chip * CW), (3, CW))
    g_b_ada = small_sum[9:15].reshape(1, 6 * D)
    dmod_all = small_all[:, 9:15, :].reshape(8, 6 * D)
    g_w_ada = _ada_bwd(c_all, lax.dynamic_slice(dmod_all, (0, chip * NA), (8, NA)))
    big["w_ada"] = [g_w_ada] + list(_adam("adam_w_ada", w_ada2, m_w_ada[0], v_w_ada[0], g_w_ada))
    sm = {}
    for nm, w_, m_, v_, g_ in [("b_ada", b_ada, m_b_ada, v_b_ada, g_b_ada), ("g_q_a", g_q_a, m_g_q_a, v_g_q_a, g_g_q),
                               ("g_kv_a", g_kv_a, m_g_kv_a, v_g_kv_a, g_g_kv),
                               ("w_conv", w_conv[0], m_w_conv[0], v_w_conv[0], g_wconv),
                               ("ln1_g", ln1_g, m_ln1_g, v_ln1_g, g_ln1_g), ("ln1_b", ln1_b, m_ln1_b, v_ln1_b, g_ln1_b),
                               ("ln2_g", ln2_g, m_ln2_g, v_ln2_g, g_ln2_g), ("ln2_b", ln2_b, m_ln2_b, v_ln2_b, g_ln2_b)]:
        sm[nm] = (g_,) + tuple(_adam_small("adam_" + nm, w_, m_, v_, g_))

    order = ["w_ada", "b_ada", "w_in", "g_q_a", "w_q_b", "g_kv_a", "w_kv_b", "w_o_a", "w_conv", "w_o_b", "w_o",
             "ln1_g", "ln1_b", "w_ffn_in", "w_ffn_out", "ln2_g", "ln2_b"]
    lead = {"b_ada", "g_q_a", "g_kv_a", "ln1_g", "ln1_b", "ln2_g", "ln2_b"}

    def leaf(nm, k):
        val = big[nm][k] if nm in big else sm[nm][k]
        if nm == "w_in":
            val = val.T
        return val if nm in lead else val[None]

    outs = [loss, grad_x[None]]
    for k in range(4):
        outs += [leaf(nm, k) for nm in order]
    return tuple(outs)
```

```python
import jax
import jax.numpy as jnp
from jax import lax
from jax.experimental import pallas as pl
from jax.experimental.pallas import tpu as pltpu

F32, BF16 = jnp.float32, jnp.bfloat16
N_HEADS, QK_NOPE, QK_ROPE, V_HEAD = 16, 128, 64, 128
Q_LORA, KV_LORA = 512, 512
QK_PAD = 256
QKV_A = 1152
CHUNK_SHIFT = 6
ATTN_SCALE = (QK_NOPE + QK_ROPE) ** -0.5
LOG2E = 1.4426950408889634
SCALE2 = ATTN_SCALE * LOG2E
ROPE_THETA = 10000.0
ALPHA = 2.0 ** 0.25
LN_EPS, RMS_EPS = 1e-5, 1e-6
ADAM_LR, ADAM_B1, ADAM_B2, ADAM_EPS, ADAM_WD, ADAM_STEP = 0.001, 0.9, 0.999, 1e-08, 0.01, 10
ADAM_C1 = 1.0 - ADAM_B1 ** ADAM_STEP
ADAM_C2 = 1.0 - ADAM_B2 ** ADAM_STEP
VMEM_LIMIT = 56 * 1024 * 1024
MESH = pl.DeviceIdType.MESH
ANY = pl.BlockSpec(memory_space=pl.ANY)
HBM_SPEC = pl.BlockSpec(memory_space=pltpu.HBM)
SEM_SPEC = pl.BlockSpec(memory_space=pltpu.SEMAPHORE)
NT = (((1,), (1,)), ((), ()))
TN = (((0,), (0,)), ((), ()))
NN = (((1,), (0,)), ((), ()))


def _params(sem=None):
    return pltpu.CompilerParams(dimension_semantics=sem, vmem_limit_bytes=VMEM_LIMIT)


def _pick(n, cands=(1408, 1024, 512, 384, 256, 128)):
    for t in cands:
        if n % t == 0:
            return t
    return n


def _row_tile(rows, row_bytes, budget, mult=8):
    best = mult
    for t in range(mult, rows + 1, mult):
        if rows % t == 0 and t * row_bytes <= budget:
            best = t
    return best


def _tile2(rows, cols, mult=8, budget=3 << 18):
    col_tiles = [t for t in range(128, cols + 1, 128) if cols % t == 0] or [cols]
    best = None
    for tc in col_tiles:
        for tr in range(mult, rows + 1, mult):
            if rows % tr == 0 and tr * tc <= budget and (best is None or (tr * tc, tc) > (best[0] * best[1], best[1])):
                best = (tr, tc)
    assert best is not None, (rows, cols)
    return best


def _sigmoid(x):
    return jax.nn.sigmoid(x)


class _Plan:
    def __init__(self, ins, outs, sems, start, finish, aliases=None):
        self.ins, self.outs, self.sems, self.start, self.finish = list(ins), list(outs), list(sems), start, finish
        self.aliases = dict(aliases or {})

    def io_aliases(self, first_in, first_out):
        return {first_in + i: first_out + o for i, o in self.aliases.items()}


def _token_plan(token):
    return _Plan([token], [], [], lambda *a: None, lambda *a: None)


def _run_plan(plan, name, ride=None):
    n_in, n_out = len(plan.ins), len(plan.outs)
    extra = [] if ride is None else list(ride)
    aliases = plan.io_aliases(0, 0)
    for k in range(len(extra)):
        aliases[n_in + k] = n_out + k

    def body(*refs):
        ins, outs, sems = refs[:n_in], refs[n_in + len(extra):n_in + len(extra) + n_out], refs[n_in + 2 * len(extra) + n_out:]
        plan.start(ins, outs, sems)
        plan.finish(ins, outs, sems)

    return pl.pallas_call(body, name=name, out_shape=plan.outs + [jax.ShapeDtypeStruct(r.shape, r.dtype) for r in extra],
                          in_specs=[ANY] * (n_in + len(extra)), out_specs=[ANY] * (n_out + len(extra)),
                          scratch_shapes=plan.sems, input_output_aliases=aliases,
                          compiler_params=_params())(*plan.ins, *extra)


def _matmul(a, b, mode, out_dtype, name, add=None, carry=None, shards=None, finish=None):
    if mode == "nn":
        (M, K), N, dims = a.shape, b.shape[-1] * (4 if shards else 1), NN
    elif mode == "nt":
        (M, K), N, dims = a.shape, b.shape[-2], NT
    else:
        (K, M), N, dims = a.shape, b.shape[1], TN
    split_n = shards and mode != "nt"
    tm = _pick(M)
    tn = _pick(N // 4) if split_n else _pick(N)
    deep = (2816, 2048, 1408, 1024, 512, 384, 256, 128)
    if shards and mode == "nt":
        tk = _pick(K // 4, deep)
    else:
        tk = K if K <= 2048 else _pick(K, deep)
    nk = K // tk
    per = (N // 4 // tn) if split_n else (K // 4 // tk if shards else 1)
    a_spec = (pl.BlockSpec((tk, tm), lambda i, j, k: (k, i)) if mode == "tn"
              else pl.BlockSpec((tm, tk), lambda i, j, k: (i, k)))
    if shards == "b" and mode == "nn":
        b_spec = pl.BlockSpec((None, tk, tn), lambda i, j, k: (j // per, k, j % per))
    elif shards == "b":
        b_spec = pl.BlockSpec((None, tn, tk), lambda i, j, k: (k // per, j, k % per))
    else:
        b_spec = (pl.BlockSpec((tn, tk), lambda i, j, k: (j, k)) if mode == "nt"
                  else pl.BlockSpec((tk, tn), lambda i, j, k: (k, j)))
    o_spec = pl.BlockSpec((tm, tn), lambda i, j, k: (i, j))
    o_shape = (M, N)
    if shards == "o":
        o_spec, o_shape = pl.BlockSpec((None, tm, tn), lambda i, j, k: (j // per, i, j % per)), (4, M, N // 4)
    has_add = add is not None
    has_fin = finish is not None
    n_ci = len(carry.ins) if carry else 0
    n_co = len(carry.outs) if carry else 0
    n_in = 2 + has_add + has_fin
    grid = (M // tm, N // tn, nk)

    def body(*refs):
        a_ref, b_ref = refs[0], refs[1]
        add_ref = refs[2] if has_add else None
        fin_ref = refs[2 + has_add] if has_fin else None

        def store(r):
            if has_add:
                r = r + add_ref[...]
            if has_fin:
                r = finish[0](r, fin_ref[...])
            o_ref[...] = r.astype(o_ref.dtype)

        o_ref = refs[n_in + n_ci]
        acc_ref = refs[n_in + n_ci + 1 + n_co] if nk > 1 else None
        c_ins = refs[n_in:n_in + n_ci]
        c_outs = refs[n_in + n_ci + 1:n_in + n_ci + 1 + n_co]
        c_sems = refs[n_in + n_ci + 1 + n_co + (nk > 1):]
        i, j, k = pl.program_id(0), pl.program_id(1), pl.program_id(2)

        if carry:
            @pl.when((i == 0) & (j == 0) & (k == 0))
            def _():
                carry.start(c_ins, c_outs, c_sems)

        part = lax.dot_general(a_ref[...], b_ref[...], dims, preferred_element_type=F32)
        if nk == 1:
            store(part)
        else:
            @pl.when(k == 0)
            def _():
                acc_ref[...] = part

            @pl.when((k > 0) & (k < nk - 1))
            def _():
                acc_ref[...] += part

            @pl.when(k == nk - 1)
            def _():
                store(acc_ref[...] + part)

        if carry:
            @pl.when((i == grid[0] - 1) & (j == grid[1] - 1) & (k == nk - 1))
            def _():
                carry.finish(c_ins, c_outs, c_sems)

    ins = [a, b] + ([add] if has_add else []) + ([finish[1]] if has_fin else []) + (carry.ins if carry else [])
    in_specs = ([a_spec, b_spec] + ([o_spec] if has_add else [])
                + ([pl.BlockSpec((tm, finish[1].shape[1]), lambda i, j, k: (i, 0))] if has_fin else []) + [ANY] * n_ci)
    res = pl.pallas_call(
        body, name=name, grid=grid,
        in_specs=in_specs, out_specs=[o_spec] + [ANY] * n_co,
        out_shape=[jax.ShapeDtypeStruct(o_shape, out_dtype)] + (carry.outs if carry else []),
        scratch_shapes=([pltpu.VMEM((tm, tn), F32)] if nk > 1 else []) + (carry.sems if carry else []),
        input_output_aliases=carry.io_aliases(n_in, 1) if carry else {},
        compiler_params=_params(("arbitrary",) * 3 if carry else ("parallel", "parallel", "arbitrary")),
    )(*ins)
    return (res[0], res[1:]) if carry else res[0]


def _rows(body, name, n_rows, tm, ins, outs, accs=(), carry=None):
    grid = (n_rows // tm,)

    def halo(arr):
        return 16 if arr.dtype == BF16 else 8

    arrays, in_specs = [], []
    for spec in ins:
        kind, arr = spec[0], spec[1]
        arrays.append(arr)
        if kind == "row":
            _, _, cb, w = spec
            in_specs.append(pl.BlockSpec((tm, w), lambda i, cb=cb: (i, cb)))
        elif kind == "full":
            in_specs.append(pl.BlockSpec(arr.shape, lambda i, nd=arr.ndim: (0,) * nd))
        elif kind == "prev":
            _, _, cb, w = spec
            h = halo(arr)
            in_specs.append(pl.BlockSpec((h, w), lambda i, cb=cb, per=tm // h: (jnp.maximum(i * per - 1, 0), cb)))
        else:
            _, _, cb, w = spec
            h = halo(arr)
            in_specs.append(pl.BlockSpec((h, w), lambda i, cb=cb, per=tm // h, last=n_rows // h - 1:
                                         (jnp.minimum((i + 1) * per, last), cb)))
    out_shape = [jax.ShapeDtypeStruct((n_rows, w), dt) for (w, dt) in outs]
    out_specs = [pl.BlockSpec((tm, w), lambda i: (i, 0)) for (w, _) in outs]
    out_shape += [jax.ShapeDtypeStruct(s, F32) for s in accs]
    out_specs += [pl.BlockSpec(s, lambda i, nd=len(s): (0,) * nd) for s in accs]
    n_in, n_out, n_acc = len(ins), len(outs), len(accs)
    n_ci = len(carry.ins) if carry else 0
    n_co = len(carry.outs) if carry else 0

    def kernel_body(*refs):
        first = n_in + n_ci
        c_ins, c_outs, c_sems = refs[n_in:first], refs[first + n_out + n_acc:first + n_out + n_acc + n_co], refs[first + n_out + n_acc + n_co:]
        if carry:
            @pl.when(pl.program_id(0) == 0)
            def _():
                carry.start(c_ins, c_outs, c_sems)

        body(pl.program_id(0), refs[:n_in], refs[first:first + n_out], refs[first + n_out:first + n_out + n_acc])
        if carry:
            @pl.when(pl.program_id(0) == grid[0] - 1)
            def _():
                carry.finish(c_ins, c_outs, c_sems)

    res = pl.pallas_call(
        kernel_body, name=name, grid=grid, in_specs=in_specs + [ANY] * n_ci, out_specs=out_specs + [ANY] * n_co,
        out_shape=out_shape + (carry.outs if carry else []), scratch_shapes=carry.sems if carry else [],
        input_output_aliases=carry.io_aliases(n_in, n_out + n_acc) if carry else {},
        compiler_params=_params(("arbitrary",)),
    )(*arrays, *(carry.ins if carry else []))
    return (res[:n_out + n_acc], res[n_out + n_acc:]) if carry else res


def _acc_add(i, ref, val):
    @pl.when(i == 0)
    def _():
        ref[...] = val

    @pl.when(i > 0)
    def _():
        ref[...] += val


def _rope(t, tab, sign):
    c, sa, sb = tab[:, 0:128], tab[:, 128:256], tab[:, 256:384]
    rot = pltpu.roll(t, 96, 1) * sa + pltpu.roll(t, 32, 1) * sb
    return t * c + rot if sign > 0 else t * c - rot


def _ln_stats(r):
    mu = jnp.mean(r, axis=-1, keepdims=True)
    d = r - mu
    var = jnp.mean(d * d, axis=-1, keepdims=True)
    rstd = lax.rsqrt(var + LN_EPS)
    return d * rstd, rstd


def _ln_bwd(dxh, xh, rstd):
    m1 = jnp.mean(dxh, axis=-1, keepdims=True)
    m2 = jnp.mean(dxh * xh, axis=-1, keepdims=True)
    return rstd * (dxh - m1 - xh * m2)


def _modulate(x, scale, shift, name):
    S, D = x.shape

    def body(i, ins, outs, accs):
        outs[0][...] = (ins[0][...] * (1.0 + ins[1][...]) + ins[2][...]).astype(BF16)

    return _rows(body, name, S, _pick(S, (256, 128)), [("row", x, 0, D), ("full", scale), ("full", shift)], [(D, BF16)])[0]


def _rms_fwd(pq, tab, g_q, g_kv):
    S = pq.shape[0]

    def body(i, ins, outs, accs):
        pq_ref, tab_ref, gq_ref, gkv_ref = ins

        def rms(x, g):
            return x * lax.rsqrt(jnp.mean(x * x, axis=-1, keepdims=True) + RMS_EPS) * g

        outs[0][...] = rms(pq_ref[:, 0:Q_LORA], gq_ref[...]).astype(BF16)
        outs[1][...] = rms(pq_ref[:, Q_LORA:Q_LORA + KV_LORA], gkv_ref[...]).astype(BF16)
        outs[2][...] = _rope(pq_ref[:, Q_LORA + KV_LORA:QKV_A], tab_ref[...], 1).astype(BF16)

    return _rows(body, "rms_fwd", S, _pick(S, (256, 128)),
                 [("row", pq, 0, QKV_A), ("row", tab, 0, 384), ("full", g_q), ("full", g_kv)],
                 [(Q_LORA, BF16), (KV_LORA, BF16), (128, BF16)])


def _allowed(q0, k0, bq):
    row = q0 + lax.broadcasted_iota(jnp.int32, (bq, bq), 0)
    col = k0 + lax.broadcasted_iota(jnp.int32, (bq, bq), 1)
    return (col >> CHUNK_SHIFT) <= (row >> CHUNK_SHIFT)


ATTN_BLOCK = 512


HEADS_PER_STEP = 2


def _attn_fwd(q, kv, kr):
    S = q.shape[0]
    bq = min(ATTN_BLOCK, S)
    nq = S // bq
    G = HEADS_PER_STEP

    def body(q_ref, kv_ref, kr_ref, o_ref, lse_ref, kcat):
        qi = pl.program_id(1)

        @pl.when(qi == 0)
        def _():
            for g in range(G):
                kcat[g, :, 0:128] = kv_ref[:, g * 256:g * 256 + 128]
                kcat[g, :, 128:256] = kr_ref[...]

        qs = [q_ref[:, g * QK_PAD:(g + 1) * QK_PAD] for g in range(G)]

        def step(j, carry, masked):
            off = pl.multiple_of(j * bq, bq)
            rows = pl.ds(off, bq)
            mask = _allowed(qi * bq, off, bq) if masked else None
            out = []
            for g in range(G):
                m, l, acc = carry[g]
                s = lax.dot_general(qs[g], kcat[g, rows, :], NT, preferred_element_type=F32) * SCALE2
                if masked:
                    s = jnp.where(mask, s, -1e30)
                m_new = jnp.maximum(m, jnp.max(s, axis=1, keepdims=True))
                a = jnp.exp2(m - m_new)
                p = jnp.exp2(s - m_new)
                l = a * l + jnp.sum(p, axis=1, keepdims=True)
                acc = a * acc + jnp.dot(p.astype(BF16), kv_ref[rows, g * 256 + 128:(g + 1) * 256],
                                        preferred_element_type=F32)
                out.append((m_new, l, acc))
            return tuple(out)

        init = tuple((jnp.full((bq, 1), -1e30, F32), jnp.zeros((bq, 1), F32), jnp.zeros((bq, V_HEAD), F32))
                     for _ in range(G))
        below = lax.fori_loop(0, qi, lambda j, cr: step(j, cr, False), init)
        for g, (m, l, acc) in enumerate(step(qi, below, True)):
            o_ref[:, g * V_HEAD:(g + 1) * V_HEAD] = (acc / l).astype(BF16)
            lse_ref[g] = m + jnp.log2(l)

    return pl.pallas_call(
        body, name="attn_fwd", grid=(N_HEADS // G, nq),
        in_specs=[pl.BlockSpec((bq, G * QK_PAD), lambda h, i: (i, h)),
                  pl.BlockSpec((S, G * 256), lambda h, i: (0, h)),
                  pl.BlockSpec((S, 128), lambda h, i: (0, 0))],
        out_specs=[pl.BlockSpec((bq, G * V_HEAD), lambda h, i: (i, h)),
                   pl.BlockSpec((G, bq, 1), lambda h, i: (h, i, 0))],
        out_shape=[jax.ShapeDtypeStruct((S, N_HEADS * V_HEAD), BF16),
                   jax.ShapeDtypeStruct((N_HEADS, S, 1), F32)],
        scratch_shapes=[pltpu.VMEM((G, S, QK_PAD), BF16)],
        compiler_params=_params(("arbitrary", "arbitrary")),
    )(q, kv, kr)


def _attn_bwd(q, kv, kr, do, o, lse, tab, carry=None):
    S = q.shape[0]
    bq = min(ATTN_BLOCK, S)
    nq = S // bq

    n_ci = len(carry.ins) if carry else 0
    n_co = len(carry.outs) if carry else 0

    def body(*refs):
        q_ref, kn_ref, v_ref, kr_ref, do_ref, o_ref, lse_ref, tab_ref = refs[:8]
        dq_ref, dkv_ref, dkr_ref = refs[8 + n_ci:11 + n_ci]
        dq_acc, dk_acc, dv_acc, kcat, delta = refs[11 + n_ci + n_co:16 + n_ci + n_co]
        c_ins, c_outs, c_sems = refs[8:8 + n_ci], refs[11 + n_ci:11 + n_ci + n_co], refs[16 + n_ci + n_co:]
        h = pl.program_id(0)
        if carry:
            @pl.when(h == 0)
            def _():
                carry.start(c_ins, c_outs, c_sems)

        dq_acc[...] = jnp.zeros_like(dq_acc)
        dk_acc[...] = jnp.zeros_like(dk_acc)
        dv_acc[...] = jnp.zeros_like(dv_acc)
        kcat[:, 0:128] = kn_ref[...]
        kcat[:, 128:256] = kr_ref[...]
        for r in range(nq):
            rows = slice(r * bq, (r + 1) * bq)
            delta[rows, :] = jnp.sum(do_ref[rows, :].astype(F32) * o_ref[rows, :].astype(F32), axis=1, keepdims=True)

        def pair(i, j, masked):
            rows_i = pl.ds(pl.multiple_of(i * bq, bq), bq)
            rows_j = pl.ds(pl.multiple_of(j * bq, bq), bq)
            qv, dov, k = q_ref[rows_i, :], do_ref[rows_i, :], kcat[rows_j, :]
            s = lax.dot_general(qv, k, NT, preferred_element_type=F32) * SCALE2
            if masked:
                s = jnp.where(_allowed(i * bq, j * bq, bq), s, -1e30)
            p = jnp.exp2(s - lse_ref[0, rows_i, :])
            dv_acc[rows_j, :] += lax.dot_general(p.astype(BF16), dov, TN, preferred_element_type=F32)
            dp = lax.dot_general(dov, v_ref[rows_j, :], NT, preferred_element_type=F32)
            ds = (p * (dp - delta[rows_i, :]) * ATTN_SCALE).astype(BF16)
            dk_acc[rows_j, :] += lax.dot_general(ds, qv, TN, preferred_element_type=F32)
            dq_acc[rows_i, :] += jnp.dot(ds, k, preferred_element_type=F32)

        def kv_step(j, _):
            pair(j, j, True)

            def q_step(i, _):
                pair(i, j, False)
                return 0

            lax.fori_loop(j + 1, nq, q_step, 0)
            return 0

        lax.fori_loop(0, nq, kv_step, 0)

        for r in range(nq):
            rows = slice(r * bq, (r + 1) * bq)
            dq_ref[rows, 0:128] = dq_acc[rows, 0:128].astype(BF16)
            dq_ref[rows, 128:256] = _rope(dq_acc[rows, 128:256], tab_ref[rows, :], -1).astype(BF16)
        dkv_ref[:, 0:128] = dk_acc[:, 0:128].astype(BF16)
        dkv_ref[:, 128:256] = dv_acc[...].astype(BF16)

        @pl.when(h == 0)
        def _():
            dkr_ref[...] = dk_acc[:, 128:256]

        @pl.when(h > 0)
        def _():
            dkr_ref[...] += dk_acc[:, 128:256]

        @pl.when(h == N_HEADS - 1)
        def _():
            for r in range(nq):
                rows = slice(r * bq, (r + 1) * bq)
                dkr_ref[rows, :] = _rope(dkr_ref[rows, :], tab_ref[rows, :], -1)
            if carry:
                carry.finish(c_ins, c_outs, c_sems)

    W = N_HEADS * QK_PAD
    res = pl.pallas_call(
        body, name="attn_bwd", grid=(N_HEADS,),
        in_specs=[pl.BlockSpec((S, QK_PAD), lambda h: (0, h)),
                  pl.BlockSpec((S, 128), lambda h: (0, 2 * h)),
                  pl.BlockSpec((S, 128), lambda h: (0, 2 * h + 1)),
                  pl.BlockSpec((S, 128), lambda h: (0, 0)),
                  pl.BlockSpec((S, V_HEAD), lambda h: (0, h)),
                  pl.BlockSpec((S, V_HEAD), lambda h: (0, h)),
                  pl.BlockSpec((1, S, 1), lambda h: (h, 0, 0)),
                  pl.BlockSpec((S, 384), lambda h: (0, 0))] + [ANY] * n_ci,
        out_specs=[pl.BlockSpec((S, QK_PAD), lambda h: (0, h)),
                   pl.BlockSpec((S, QK_PAD), lambda h: (0, h)),
                   pl.BlockSpec((S, 128), lambda h: (0, 0))] + [ANY] * n_co,
        out_shape=[jax.ShapeDtypeStruct((S, W), BF16), jax.ShapeDtypeStruct((S, W), BF16),
                   jax.ShapeDtypeStruct((S, 128), F32)] + (carry.outs if carry else []),
        scratch_shapes=[pltpu.VMEM((S, QK_PAD), F32), pltpu.VMEM((S, QK_PAD), F32), pltpu.VMEM((S, V_HEAD), F32),
                        pltpu.VMEM((S, QK_PAD), BF16), pltpu.VMEM((S, 1), F32)]
        + (carry.sems if carry else []),
        input_output_aliases=carry.io_aliases(8, 3) if carry else {},
        compiler_params=_params(("arbitrary",)),
    )(q, kv, kv, kr, do, o, lse, tab, *(carry.ins if carry else []))
    return res[0], res[1], res[2], res[3:]


def _shift_down(cur, prev, i, n):
    tm, h = cur.shape[0], prev.shape[0]
    prev = jnp.where(i == 0, jnp.zeros_like(prev), prev)
    full = jnp.concatenate([prev, cur], axis=0)
    return pltpu.roll(full, n, 0)[h:h + tm, :]


def _shift_up(cur, nxt, i, last, n):
    tm, h = cur.shape[0], nxt.shape[0]
    nxt = jnp.where(i == last, jnp.zeros_like(nxt), nxt)
    full = jnp.concatenate([cur, nxt], axis=0)
    return pltpu.roll(full, tm + h - n, 0)[0:tm, :]


def _conv_fwd(pc, w_conv):
    S, D = pc.shape[0], pc.shape[1] // 3
    tm = _pick(S, (256, 128))

    def body(i, ins, outs, accs):
        b_ref, c_ref, x_ref, cp_ref, xp_ref, w_ref = ins
        z = c_ref[...].astype(F32) * x_ref[...].astype(F32)
        zp = cp_ref[...].astype(F32) * xp_ref[...].astype(F32)
        cz = w_ref[0:1, :] * _shift_down(z, zp, i, 2) + w_ref[1:2, :] * _shift_down(z, zp, i, 1) + w_ref[2:3, :] * z
        outs[0][...] = (b_ref[...].astype(F32) * cz).astype(BF16)

    return _rows(body, "conv_fwd", S, tm,
                 [("row", pc, 0, D), ("row", pc, 1, D), ("row", pc, 2, D), ("prev", pc, 1, D), ("prev", pc, 2, D),
                  ("full", w_conv)], [(D, BF16)])[0]


def _conv_bwd(dhb, pc, w_conv):
    S, D = dhb.shape
    tm = _pick(S, (256, 128))
    last = S // tm - 1

    def body(i, ins, outs, accs):
        g_ref, b_ref, c_ref, x_ref, cp_ref, xp_ref, gn_ref, bn_ref, w_ref = ins
        w0, w1, w2 = w_ref[0:1, :], w_ref[1:2, :], w_ref[2:3, :]
        c, x, g = c_ref[...].astype(F32), x_ref[...].astype(F32), g_ref[...].astype(F32)
        z = c * x
        zp = cp_ref[...].astype(F32) * xp_ref[...].astype(F32)
        z1, z2 = _shift_down(z, zp, i, 1), _shift_down(z, zp, i, 2)
        cz = w0 * z2 + w1 * z1 + w2 * z
        dcz = g * b_ref[...].astype(F32)
        dczn = gn_ref[...].astype(F32) * bn_ref[...].astype(F32)
        dz = w2 * dcz + w1 * _shift_up(dcz, dczn, i, last, 1) + w0 * _shift_up(dcz, dczn, i, last, 2)
        outs[0][:, 0:D] = (g * cz).astype(BF16)
        outs[0][:, D:2 * D] = (dz * x).astype(BF16)
        outs[0][:, 2 * D:3 * D] = (dz * c).astype(BF16)
        dw = jnp.concatenate([jnp.sum(dcz * z2, axis=0, keepdims=True), jnp.sum(dcz * z1, axis=0, keepdims=True),
                              jnp.sum(dcz * z, axis=0, keepdims=True)], axis=0)
        _acc_add(i, accs[0], dw)

    return _rows(body, "conv_bwd", S, tm,
                 [("row", dhb, 0, D), ("row", pc, 0, D), ("row", pc, 1, D), ("row", pc, 2, D),
                  ("prev", pc, 1, D), ("prev", pc, 2, D), ("next", dhb, 0, D), ("next", pc, 0, D), ("full", w_conv)],
                 [(3 * D, BF16)], [(3, D)])


def _merge_fwd(y_a, y_b, pg):
    S, D = y_a.shape

    def body(i, ins, outs, accs):
        ya, yb, ga, gb = ins
        outs[0][...] = (_sigmoid(ga[...].astype(F32)) * ya[...].astype(F32)
                        + _sigmoid(gb[...].astype(F32)) * yb[...].astype(F32)).astype(BF16)

    return _rows(body, "merge_fwd", S, _pick(S, (256, 128)),
                 [("row", y_a, 0, D), ("row", y_b, 0, D), ("row", pg, 0, D), ("row", pg, 1, D)], [(D, BF16)])[0]


def _merge_bwd(dm, y_a, y_b, pg):
    S, D = dm.shape

    def body(i, ins, outs, accs):
        d, ya, yb = ins[0][...].astype(F32), ins[1][...].astype(F32), ins[2][...].astype(F32)
        sa, sb = _sigmoid(ins[3][...].astype(F32)), _sigmoid(ins[4][...].astype(F32))
        outs[0][...] = (d * sa).astype(BF16)
        outs[1][...] = (d * sb).astype(BF16)
        outs[2][:, 0:D] = (d * ya * (sa * (1.0 - sa))).astype(BF16)
        outs[2][:, D:2 * D] = (d * yb * (sb * (1.0 - sb))).astype(BF16)

    return _rows(body, "merge_bwd", S, _pick(S, (256, 128)),
                 [("row", dm, 0, D), ("row", y_a, 0, D), ("row", y_b, 0, D), ("row", pg, 0, D), ("row", pg, 1, D)],
                 [(D, BF16), (D, BF16), (2 * D, BF16)])


def _ln1_fwd(x, mix, gate1, g, b, scale2, shift2, carry=None):
    S, D = x.shape

    def body(i, ins, outs, accs):
        x_ref, mix_ref, gate_ref, g_ref, b_ref, sc_ref, sh_ref = ins
        xh, _ = _ln_stats(ALPHA * x_ref[...] + gate_ref[...] * mix_ref[...])
        x1 = xh * g_ref[...] + b_ref[...]
        outs[0][...] = x1
        outs[1][...] = (x1 * (1.0 + sc_ref[...]) + sh_ref[...]).astype(BF16)

    return _rows(body, "ln1_fwd", S, _pick(S, (256, 128)),
                 [("row", x, 0, D), ("row", mix, 0, D), ("full", gate1), ("full", g), ("full", b),
                  ("full", scale2), ("full", shift2)], [(D, F32), (D, BF16)], carry=carry)


def _swiglu_fwd(hh, carry=None):
    S, F = hh.shape[0], hh.shape[1] // 2

    def body(i, ins, outs, accs):
        hg = ins[0][...].astype(F32)
        outs[0][...] = (hg * _sigmoid(hg) * ins[1][...].astype(F32)).astype(BF16)

    res = _rows(body, "swiglu_fwd", S, _pick(S, (128,)), [("row", hh, 0, F), ("row", hh, 1, F)], [(F, BF16)], carry=carry)
    return (res[0][0], res[1]) if carry else res[0]


def _swiglu_bwd(dact, hh):
    S, F = dact.shape

    def body(i, ins, outs, accs):
        d, hg, hu = ins[0][...].astype(F32), ins[1][...].astype(F32), ins[2][...].astype(F32)
        sg = _sigmoid(hg)
        outs[0][:, 0:F] = (d * hu * (sg * (1.0 + hg * (1.0 - sg)))).astype(BF16)
        outs[0][:, F:2 * F] = (d * (hg * sg)).astype(BF16)

    return _rows(body, "swiglu_bwd", S, _pick(S, (128,)),
                 [("row", dact, 0, F), ("row", hh, 0, F), ("row", hh, 1, F)], [(2 * F, BF16)])[0]


def _ln2_loss_bwd(x1, ffn, gate2, g, b, target):
    S, D = x1.shape

    def body(i, ins, outs, accs):
        x1_ref, f_ref, gate_ref, g_ref, b_ref, t_ref = ins
        f = f_ref[...]
        xh, rstd = _ln_stats(ALPHA * x1_ref[...] + gate_ref[...] * f)
        e = xh * g_ref[...] + b_ref[...] - t_ref[...]
        dy = e * (1.0 / D)
        dr = _ln_bwd(dy * g_ref[...], xh, rstd)
        outs[0][...] = (gate_ref[...] * dr).astype(BF16)
        outs[1][...] = ALPHA * dr
        _acc_add(i, accs[0], jnp.full((1, 128), (0.5 / D) * jnp.sum(e * e), F32))
        _acc_add(i, accs[1], jnp.sum(dy * xh, axis=0, keepdims=True))
        _acc_add(i, accs[2], jnp.sum(dy, axis=0, keepdims=True))
        _acc_add(i, accs[3], jnp.sum(dr * f, axis=0, keepdims=True))

    return _rows(body, "ln2_loss_bwd", S, _pick(S, (256, 128)),
                 [("row", x1, 0, D), ("row", ffn, 0, D), ("full", gate2), ("full", g), ("full", b), ("row", target, 0, D)],
                 [(D, BF16), (D, F32)], [(1, 128), (1, D), (1, D), (1, D)])


def _ln1_bwd(x, mix, dx1a, du2, gate1, g, b, scale2):
    S, D = x.shape

    def body(i, ins, outs, accs):
        x_ref, mix_ref, da_ref, du_ref, gate_ref, g_ref, b_ref, sc_ref = ins
        mix, du = mix_ref[...], du_ref[...]
        xh, rstd = _ln_stats(ALPHA * x_ref[...] + gate_ref[...] * mix)
        x1 = xh * g_ref[...] + b_ref[...]
        dx1 = da_ref[...] + du * (1.0 + sc_ref[...])
        dr = _ln_bwd(dx1 * g_ref[...], xh, rstd)
        outs[0][...] = (gate_ref[...] * dr).astype(BF16)
        outs[1][...] = ALPHA * dr
        _acc_add(i, accs[0], jnp.sum(du, axis=0, keepdims=True))
        _acc_add(i, accs[1], jnp.sum(du * x1, axis=0, keepdims=True))
        _acc_add(i, accs[2], jnp.sum(dx1 * xh, axis=0, keepdims=True))
        _acc_add(i, accs[3], jnp.sum(dx1, axis=0, keepdims=True))
        _acc_add(i, accs[4], jnp.sum(dr * mix, axis=0, keepdims=True))

    return _rows(body, "ln1_bwd", S, _pick(S, (256, 128)),
                 [("row", x, 0, D), ("row", mix, 0, D), ("row", dx1a, 0, D), ("row", du2, 0, D),
                  ("full", gate1), ("full", g), ("full", b), ("full", scale2)],
                 [(D, BF16), (D, F32)], [(1, D)] * 5)


def _rms_bwd(d_rq, d_rkv, pq, dkr, g_q, g_kv):
    S = pq.shape[0]

    def body(i, ins, outs, accs):
        dq_ref, dkv_ref, pq_ref, dkr_ref, gq_ref, gkv_ref = ins

        def rms_bwd(dy, x, g):
            r = lax.rsqrt(jnp.mean(x * x, axis=-1, keepdims=True) + RMS_EPS)
            dyg = dy * g
            dx = r * dyg - x * (r * r * r) * jnp.mean(dyg * x, axis=-1, keepdims=True)
            return dx, jnp.sum(dy * (x * r), axis=0, keepdims=True)

        dxq, dgq = rms_bwd(dq_ref[...], pq_ref[:, 0:Q_LORA], gq_ref[...])
        dxkv, dgkv = rms_bwd(dkv_ref[...], pq_ref[:, Q_LORA:Q_LORA + KV_LORA], gkv_ref[...])
        outs[0][:, 0:Q_LORA] = dxq.astype(BF16)
        outs[0][:, Q_LORA:Q_LORA + KV_LORA] = dxkv.astype(BF16)
        outs[0][:, Q_LORA + KV_LORA:QKV_A] = dkr_ref[...].astype(BF16)
        _acc_add(i, accs[0], dgq)
        _acc_add(i, accs[1], dgkv)

    return _rows(body, "rms_bwd", S, _pick(S, (256, 128)),
                 [("row", d_rq, 0, Q_LORA), ("row", d_rkv, 0, KV_LORA), ("row", pq, 0, QKV_A), ("row", dkr, 0, 128),
                  ("full", g_q), ("full", g_kv)], [(QKV_A, BF16)], [(1, Q_LORA), (1, KV_LORA)])


def _dx_final(dxa, du, x, scale1):
    S, D = x.shape

    def body(i, ins, outs, accs):
        du = ins[1][...]
        outs[0][...] = ins[0][...] + du * (1.0 + ins[3][...])
        _acc_add(i, accs[0], jnp.sum(du, axis=0, keepdims=True))
        _acc_add(i, accs[1], jnp.sum(du * ins[2][...], axis=0, keepdims=True))

    return _rows(body, "dx_final", S, _pick(S, (256, 128)),
                 [("row", dxa, 0, D), ("row", du, 0, D), ("row", x, 0, D), ("full", scale1)],
                 [(D, F32)], [(1, D), (1, D)])


def _ada_fwd(c_all, w, bias):
    B, D = c_all.shape
    NA = w.shape[1]
    tn = _pick(NA, (512, 256, 128))

    def body(c_ref, w_ref, b_ref, o_ref):
        cv = c_ref[...]
        ca = (cv * _sigmoid(cv)).astype(BF16)
        o_ref[...] = jnp.dot(ca, w_ref[...].astype(BF16), preferred_element_type=F32) + b_ref[...]

    return pl.pallas_call(
        body, name="ada_fwd", grid=(NA // tn,),
        in_specs=[pl.BlockSpec((B, D), lambda j: (0, 0)), pl.BlockSpec((D, tn), lambda j: (0, j)),
                  pl.BlockSpec((1, tn), lambda j: (0, j))],
        out_specs=pl.BlockSpec((B, tn), lambda j: (0, j)),
        out_shape=jax.ShapeDtypeStruct((B, NA), F32),
        compiler_params=_params(("arbitrary",)),
    )(c_all, w, bias)


def _ada_bwd(c_all, dmod):
    B, D = c_all.shape
    NA = dmod.shape[1]
    tn = _pick(NA, (512, 256, 128))

    def body(c_ref, d_ref, o_ref):
        cv = c_ref[...]
        ca = (cv * _sigmoid(cv)).astype(BF16)
        o_ref[...] = lax.dot_general(ca, d_ref[...].astype(BF16), TN, preferred_element_type=F32)

    return pl.pallas_call(
        body, name="ada_bwd", grid=(NA // tn,),
        in_specs=[pl.BlockSpec((B, D), lambda j: (0, 0)), pl.BlockSpec((B, tn), lambda j: (0, j))],
        out_specs=pl.BlockSpec((D, tn), lambda j: (0, j)),
        out_shape=jax.ShapeDtypeStruct((D, NA), F32),
        compiler_params=_params(("arbitrary",)),
    )(c_all, dmod)


def _pack_rows(parts, n_rows, after=()):
    N = parts[0].shape[1]
    n = len(parts)

    def body(*refs):
        o_ref = refs[-1]
        o_ref[...] = jnp.zeros_like(o_ref)
        at = 0
        for r in refs[:n]:
            o_ref[at:at + r.shape[0], :] = r[...]
            at += r.shape[0]

    vmem = pl.BlockSpec(memory_space=pltpu.VMEM)
    return pl.pallas_call(body, name="pack_small", out_shape=jax.ShapeDtypeStruct((n_rows, N), F32),
                          in_specs=[vmem] * n + [ANY] * len(after), out_specs=vmem,
                          compiler_params=_params())(*parts, *after)


def _sum8(parts):
    _, R, N = parts.shape

    def body(p_ref, o_ref):
        acc = p_ref[0]
        for d in range(1, 8):
            acc = acc + p_ref[d]
        o_ref[...] = acc

    return pl.pallas_call(body, name="sum8", out_shape=jax.ShapeDtypeStruct((R, N), F32),
                          compiler_params=_params())(parts)


def _adam_math(w, g, m, v):
    m = ADAM_B1 * m + (1.0 - ADAM_B1) * g
    v = ADAM_B2 * v + (1.0 - ADAM_B2) * (g * g)
    delta = -ADAM_LR * ((m / ADAM_C1) / (jnp.sqrt(v / ADAM_C2) + ADAM_EPS) + ADAM_WD * w)
    return delta, m, v


def _adam(name, w, m, v, g, carry=None):
    R, C = w.shape
    tm = _row_tile(R, C * 4, 1 << 20)
    steps = R // tm
    n_ci = len(carry.ins) if carry else 0
    n_co = len(carry.outs) if carry else 0

    def body(*refs):
        w_ref, m_ref, v_ref, g_ref = refs[:4]
        d_ref, nm_ref, nv_ref = refs[4 + n_ci:7 + n_ci]
        c_ins, c_outs, c_sems = refs[4:4 + n_ci], refs[7 + n_ci:7 + n_ci + n_co], refs[7 + n_ci + n_co:]
        if carry:
            @pl.when(pl.program_id(0) == 0)
            def _():
                carry.start(c_ins, c_outs, c_sems)

        delta, nm, nv = _adam_math(w_ref[...], g_ref[...], m_ref[...], v_ref[...])
        d_ref[...] = delta
        nm_ref[...] = nm
        nv_ref[...] = nv
        if carry:
            @pl.when(pl.program_id(0) == steps - 1)
            def _():
                carry.finish(c_ins, c_outs, c_sems)

    spec = pl.BlockSpec((tm, C), lambda i: (i, 0))
    res = pl.pallas_call(
        body, name=name, grid=(steps,), in_specs=[spec] * 4 + [ANY] * n_ci, out_specs=[spec] * 3 + [ANY] * n_co,
        out_shape=[jax.ShapeDtypeStruct((R, C), F32)] * 3 + (carry.outs if carry else []),
        scratch_shapes=carry.sems if carry else [],
        input_output_aliases=carry.io_aliases(4, 3) if carry else {},
        compiler_params=_params(("arbitrary",)),
    )(w, m, v, g, *(carry.ins if carry else []))
    return (res[:3], res[3:]) if carry else res


def _adam_halves(name, w, m, v, mine, other, core, carry=None):
    R, C = w.shape
    Rh = mine.shape[0]
    tc = max(t for t in range(128, C + 1, 128) if C % t == 0 and R * t <= (3 << 17))
    steps = C // tc
    n_ci = len(carry.ins) if carry else 0
    n_co = len(carry.outs) if carry else 0

    def body(*refs):
        c_ref, w_ref, m_ref, v_ref, a_ref, b_ref = refs[:6]
        g_ref, d_ref, nm_ref, nv_ref = refs[6 + n_ci:10 + n_ci]
        c_ins, c_outs, c_sems = refs[6:6 + n_ci], refs[10 + n_ci:10 + n_ci + n_co], refs[10 + n_ci + n_co:]
        if carry:
            @pl.when(pl.program_id(0) == 0)
            def _():
                carry.start(c_ins, c_outs, c_sems)

        first = c_ref[0] == 0
        g = jnp.concatenate([jnp.where(first, a_ref[...], b_ref[...]),
                             jnp.where(first, b_ref[0:R - Rh, :], a_ref[0:R - Rh, :])], axis=0)
        delta, nm, nv = _adam_math(w_ref[...], g, m_ref[...], v_ref[...])
        g_ref[...] = g
        d_ref[...] = delta
        nm_ref[...] = nm
        nv_ref[...] = nv
        if carry:
            @pl.when(pl.program_id(0) == steps - 1)
            def _():
                carry.finish(c_ins, c_outs, c_sems)

    spec = pl.BlockSpec((R, tc), lambda i, c_ref: (0, i))
    h_spec = pl.BlockSpec((Rh, tc), lambda i, c_ref: (0, i))
    res = pl.pallas_call(
        body, name=name, out_shape=[jax.ShapeDtypeStruct((R, C), F32)] * 4 + (carry.outs if carry else []),
        grid_spec=pltpu.PrefetchScalarGridSpec(
            num_scalar_prefetch=1, grid=(steps,), in_specs=[spec, spec, spec, h_spec, h_spec] + [ANY] * n_ci,
            out_specs=[spec] * 4 + [ANY] * n_co, scratch_shapes=carry.sems if carry else []),
        input_output_aliases=carry.io_aliases(6, 4) if carry else {},
        compiler_params=_params(("arbitrary",)),
    )(core, w, m, v, mine, other, *(carry.ins if carry else []))
    return (res[:4], res[4:]) if carry else res


def _adam_small(name, w, m, v, g):
    def body(w_ref, m_ref, v_ref, g_ref, d_ref, nm_ref, nv_ref):
        delta, nm, nv = _adam_math(w_ref[...], g_ref[...], m_ref[...], v_ref[...])
        d_ref[...] = delta
        nm_ref[...] = nm
        nv_ref[...] = nv

    return pl.pallas_call(body, name=name, out_shape=[jax.ShapeDtypeStruct(w.shape, F32)] * 3,
                          compiler_params=_params())(w, m, v, g)


def _place():
    return lax.axis_index("x"), lax.axis_index("y"), lax.axis_index("c")


def _other_chips(x, y):
    return [(1 - x, y), (x, 1 - y), (1 - x, 1 - y)]


def _all_gather8(blk, name):
    R, N = blk.shape

    def body(x_ref, out_ref, send_sems, recv_sems, local_sem):
        x, y, c = _place()
        me = 4 * x + 2 * y + c
        mine = pltpu.make_async_copy(x_ref, out_ref.at[me], local_sem)
        mine.start()
        flips = [(j >> 2 & 1, j >> 1 & 1, j & 1) for j in range(1, 8)]
        peers = [((1 - x) if fx else x, (1 - y) if fy else y, (1 - c) if fc else c) for fx, fy, fc in flips]
        sends = []
        for j, peer in enumerate(peers):
            cp = pltpu.make_async_remote_copy(src_ref=x_ref, dst_ref=out_ref.at[me], send_sem=send_sems.at[j],
                                              recv_sem=recv_sems.at[j], device_id=peer, device_id_type=MESH)
            cp.start()
            sends.append(cp)
        for j, (px, py, pc) in enumerate(peers):
            pltpu.make_async_remote_copy(src_ref=x_ref, dst_ref=out_ref.at[4 * px + 2 * py + pc],
                                         send_sem=send_sems.at[j], recv_sem=recv_sems.at[j],
                                         device_id=(px, py, pc), device_id_type=MESH).wait_recv()
        for cp in sends:
            cp.wait_send()
        mine.wait()

    return pl.pallas_call(
        body, name=name, out_shape=jax.ShapeDtypeStruct((8, R, N), F32),
        in_specs=[pl.BlockSpec(memory_space=pltpu.VMEM)], out_specs=pl.BlockSpec(memory_space=pltpu.VMEM),
        scratch_shapes=[pltpu.SemaphoreType.DMA((7,)), pltpu.SemaphoreType.DMA((7,)), pltpu.SemaphoreType.DMA],
        compiler_params=_params(),
    )(blk)


def _piece(rows, piece):
    i, n, k = piece if len(piece) == 3 else (piece[0], piece[1], 1)
    assert rows % 16 == 0 and rows // 16 >= n, (rows, piece)
    lo, hi = (rows // 16 * i // n) * 16, (rows // 16 * (i + k) // n) * 16
    return pl.ds(lo, hi - lo)


def _gather_plan(shards, piece=(0, 1), into=None, ici=True):
    n = len(shards)

    def parts(ins, outs, sems):
        s1, r1, s2, r2, loc = sems
        x, y, c = _place()
        me = 2 * x + y
        chips = _other_chips(x, y)
        sib = (x, y, 1 - c)

        def rows(k):
            return _piece(shards[k].shape[1], piece)

        def ici_copy(k, j, slab, to):
            return pltpu.make_async_remote_copy(src_ref=ins[k].at[c, rows(k)], dst_ref=outs[k].at[slab, c, rows(k)],
                                                send_sem=s1.at[3 * k + j], recv_sem=r1.at[3 * k + j],
                                                device_id=to, device_id_type=MESH)

        def d2d(k, j, slab, half):
            return pltpu.make_async_remote_copy(src_ref=outs[k].at[slab, half, rows(k)],
                                                dst_ref=outs[k].at[slab, half, rows(k)],
                                                send_sem=s2.at[3 * k + j], recv_sem=r2.at[3 * k + j],
                                                device_id=sib, device_id_type=MESH)

        def own(k):
            return pltpu.make_async_remote_copy(src_ref=ins[k].at[:, rows(k)], dst_ref=outs[k].at[me, :, rows(k)],
                                                send_sem=loc.at[2 * k], recv_sem=loc.at[2 * k + 1],
                                                device_id=sib, device_id_type=MESH)

        return c, me, chips, ici_copy, d2d, own

    def start(ins, outs, sems):
        c, me, chips, ici_copy, d2d, own = parts(ins, outs, sems)
        for k in range(n):
            for j, (px, py) in enumerate(chips):
                (ici_copy(k, j, me, (px, py, c)) if ici else d2d(k, j, 2 * px + py, c)).start()
        for k in range(n):
            own(k).start()

    def finish(ins, outs, sems):
        c, me, chips, ici_copy, d2d, own = parts(ins, outs, sems)
        if ici:
            for k in range(n):
                for j, (px, py) in enumerate(chips):
                    ici_copy(k, j, 2 * px + py, (px, py, c)).wait_recv()
                    d2d(k, j, 2 * px + py, c).start()
        for k in range(n):
            for j, (px, py) in enumerate(chips):
                d2d(k, j, 2 * px + py, 1 - c).wait_recv()
        for k in range(n):
            own(k).wait()
            for j, (px, py) in enumerate(chips):
                if ici:
                    ici_copy(k, j, me, (px, py, c)).wait_send()
                d2d(k, j, 2 * px + py, c).wait_send()

    return _Plan(list(shards) + list(into or []), [jax.ShapeDtypeStruct((4,) + a.shape, a.dtype) for a in shards],
                 [pltpu.SemaphoreType.DMA((3 * n,))] * 4 + [pltpu.SemaphoreType.DMA((2 * n,))], start, finish,
                 aliases={n + k: k for k in range(n)} if into else None)


def _pair_plan(parts):
    n = len(parts)

    def copies(ins, outs, sems):
        send_sems, recv_sems = sems
        x, y, c = _place()
        return [pltpu.make_async_remote_copy(src_ref=ins[k].at[p, 1 - c], dst_ref=outs[k].at[p],
                                             send_sem=send_sems.at[4 * k + p], recv_sem=recv_sems.at[4 * k + p],
                                             device_id=(x, y, 1 - c), device_id_type=MESH)
                for k in range(n) for p in range(4)]

    def start(ins, outs, sems):
        for cp in copies(ins, outs, sems):
            cp.start()

    def finish(ins, outs, sems):
        for cp in copies(ins, outs, sems):
            cp.wait()

    return _Plan(parts, [jax.ShapeDtypeStruct((4,) + a.shape[2:], a.dtype) for a in parts],
                 [pltpu.SemaphoreType.DMA((4 * n,))] * 2, start, finish)


def _sibling_plan(arrs):
    n = len(arrs)

    def copies(ins, outs, sems):
        send_sems, recv_sems = sems
        x, y, c = _place()
        return [pltpu.make_async_remote_copy(src_ref=ins[k], dst_ref=outs[k], send_sem=send_sems.at[k],
                                             recv_sem=recv_sems.at[k], device_id=(x, y, 1 - c), device_id_type=MESH)
                for k in range(n)]

    def start(ins, outs, sems):
        for cp in copies(ins, outs, sems):
            cp.start()

    def finish(ins, outs, sems):
        for cp in copies(ins, outs, sems):
            cp.wait()

    return _Plan(arrs, [jax.ShapeDtypeStruct(a.shape, a.dtype) for a in arrs],
                 [pltpu.SemaphoreType.DMA((n,))] * 2, start, finish)


def _scatter_copies(arrs):
    def copies(ins, land, send_sems, recv_sems):
        x, y, c = _place()
        return [pltpu.make_async_remote_copy(src_ref=ins[k].at[2 * px + py], dst_ref=land[k].at[j],
                                             send_sem=send_sems.at[3 * k + j], recv_sem=recv_sems.at[3 * k + j],
                                             device_id=(px, py, c), device_id_type=MESH)
                for k in range(len(arrs)) for j, (px, py) in enumerate(_other_chips(x, y))]

    return copies, [lax.empty((3,) + a.shape[1:], a.dtype) for a in arrs]


def _gather_copies(shards):
    def copies(ins, land, send_sems, recv_sems):
        x, y, c = _place()
        return [pltpu.make_async_remote_copy(src_ref=ins[k].at[c], dst_ref=land[k].at[2 * x + y, c],
                                             send_sem=send_sems.at[3 * k + j], recv_sem=recv_sems.at[3 * k + j],
                                             device_id=(px, py, c), device_id_type=MESH)
                for k in range(len(shards)) for j, (px, py) in enumerate(_other_chips(x, y))]

    return copies, [lax.empty((4,) + a.shape, a.dtype) for a in shards]


def _split_start(arrs, copies_lands, ride, name, after=()):
    copies, lands = copies_lands
    n = len(arrs)
    rides = list(ride) if isinstance(ride, (list, tuple)) else [ride]
    n_thru = 2 * n + len(rides)

    def body(*refs):
        first_out = n_thru + len(after)
        for cp in copies(refs[:n], refs[n:2 * n], refs[first_out], refs[first_out + 1]):
            cp.start()

    hbm = [pltpu.with_memory_space_constraint(a, pltpu.HBM) for a in list(arrs) + lands + rides]
    res = pl.pallas_call(
        body, name=name,
        out_shape=[pltpu.SemaphoreType.DMA((3 * n,)), pltpu.SemaphoreType.DMA((3 * n,))]
        + [pltpu.HBM(a.shape, a.dtype) for a in hbm],
        in_specs=[HBM_SPEC] * n_thru + [ANY] * len(after),
        out_specs=[SEM_SPEC, SEM_SPEC] + [HBM_SPEC] * n_thru,
        input_output_aliases={i: 2 + i for i in range(n_thru)},
        compiler_params=pltpu.CompilerParams(has_side_effects=pltpu.SideEffectType.DATAFLOW_SIDE_EFFECTING),
    )(*hbm, *after)
    return res[0], res[1], res[2:2 + n], res[2 + n:2 + 2 * n], list(res[2 + 2 * n:])


def _split_wait(started, copies_lands, after, name):
    send_sems, recv_sems, arrs, lands, _ = started
    copies = copies_lands[0]
    n = len(arrs)

    def body(*refs):
        for cp in copies(refs[:n], refs[n:2 * n], refs[2 * n], refs[2 * n + 1]):
            cp.wait_send()
            cp.wait_recv()

    res = pl.pallas_call(
        body, name=name, out_shape=[pltpu.HBM(a.shape, a.dtype) for a in list(arrs) + list(lands)],
        in_specs=[HBM_SPEC] * (2 * n) + [SEM_SPEC, SEM_SPEC] + [ANY] * len(after), out_specs=[HBM_SPEC] * (2 * n),
        input_output_aliases={i: i for i in range(2 * n)},
        compiler_params=pltpu.CompilerParams(has_side_effects=pltpu.SideEffectType.DATAFLOW_SIDE_EFFECTING),
    )(*arrs, *lands, send_sems, recv_sems, *after)
    return list(res[:n]), list(res[n:])


def _add_pair(parts, sib, core, name):
    P4, _, Rh, C = parts.shape
    tm, tc = _tile2(Rh, C, 16)

    def body(c_ref, a_ref, b_ref, o_ref):
        o_ref[...] = (a_ref[0].astype(F32) + b_ref[...].astype(F32)).astype(BF16)

    spec = pl.BlockSpec((1, tm, tc), lambda p, i, j, c_ref: (p, i, j))
    return pl.pallas_call(
        body, name=name, out_shape=jax.ShapeDtypeStruct((P4, Rh, C), BF16),
        grid_spec=pltpu.PrefetchScalarGridSpec(
            num_scalar_prefetch=1, grid=(P4, Rh // tm, C // tc),
            in_specs=[pl.BlockSpec((1, 1, tm, tc), lambda p, i, j, c_ref: (p, c_ref[0], i, j)), spec], out_specs=spec),
        compiler_params=_params(("parallel",) * 3),
    )(core, parts, sib)


def _sum_slabs(pre, recv, chip, name):
    _, Rh, C = pre.shape
    tm, tc = _tile2(Rh, C, 16)

    def body(me_ref, own_ref, r_ref, o_ref):
        acc = own_ref[0].astype(F32)
        for j in range(3):
            acc = acc + r_ref[j].astype(F32)
        o_ref[...] = acc

    return pl.pallas_call(
        body, name=name, out_shape=jax.ShapeDtypeStruct((Rh, C), F32),
        grid_spec=pltpu.PrefetchScalarGridSpec(
            num_scalar_prefetch=1, grid=(Rh // tm, C // tc),
            in_specs=[pl.BlockSpec((1, tm, tc), lambda i, j, me_ref: (me_ref[0], i, j)),
                      pl.BlockSpec((3, tm, tc), lambda i, j, me_ref: (0, i, j))],
            out_specs=pl.BlockSpec((tm, tc), lambda i, j, me_ref: (i, j))),
        compiler_params=_params(("parallel", "parallel")),
    )(chip, pre, recv)


def kernel(x, c, positions, w_ada, b_ada, w_in, g_q_a, w_q_b, g_kv_a, w_kv_b, w_o_a, w_conv, w_o_b, w_o, ln1_g, ln1_b, w_ffn_in, w_ffn_out, ln2_g, ln2_b, loss_target, m_w_ada, m_b_ada, m_w_in, m_g_q_a, m_w_q_b, m_g_kv_a, m_w_kv_b, m_w_o_a, m_w_conv, m_w_o_b, m_w_o, m_ln1_g, m_ln1_b, m_w_ffn_in, m_w_ffn_out, m_ln2_g, m_ln2_b, v_w_ada, v_b_ada, v_w_in, v_g_q_a, v_w_q_b, v_g_kv_a, v_w_kv_b, v_w_o_a, v_w_conv, v_w_o_b, v_w_o, v_ln1_g, v_ln1_b, v_w_ffn_in, v_w_ffn_out, v_ln2_g, v_ln2_b):
    S, D = x.shape[1], x.shape[2]
    F = w_ffn_out.shape[1] * 4
    ax, ay, ac = _place()
    chip = 2 * ax + ay
    dev = 4 * ax + 2 * ay + ac
    x2, tgt = x[0], loss_target[0]
    w_ada2, w_in2, w_q_b2, w_kv_b2 = w_ada[0], w_in[0], w_q_b[0], w_kv_b[0]
    w_o_a2, w_o_b2, w_o2, w_ffn_in2, w_ffn_out2 = w_o_a[0], w_o_b[0], w_o[0], w_ffn_in[0], w_ffn_out[0]
    NA = w_ada2.shape[1]
    CW = w_conv.shape[2]

    inv_freq = 1.0 / (ROPE_THETA ** (jnp.arange(0, QK_ROPE, 2, dtype=F32) / QK_ROPE))
    ang = positions[0].astype(F32)[:, None] * inv_freq
    cos, sin = jnp.cos(ang), jnp.sin(ang)
    z32, z64, z96 = jnp.zeros((S, 32), F32), jnp.zeros((S, 64), F32), jnp.zeros((S, 96), F32)
    tab = jnp.concatenate([cos, cos, z64, -sin, z96, z32, sin, z64], axis=1)

    def halves(a):
        return a.reshape(2, a.shape[0] // 2, a.shape[1])

    def whole(g):
        return g.reshape(4, 2 * g.shape[2], g.shape[3])

    def cols(g):
        return jnp.transpose(g, (1, 0, 2)).reshape(g.shape[1], 4 * g.shape[2])

    w_inT, m_w_inT, v_w_inT = w_in2.T, m_w_in[0].T, v_w_in[0].T
    CS = w_inT.shape[0]
    CSP = -(-CS // 32) * 32
    sh_in = halves(jnp.pad(w_inT.astype(BF16), ((0, CSP - CS), (0, 0))))
    sh_qb, sh_kvb, sh_oa, sh_ob, sh_o = (
        halves(w.astype(BF16)) for w in (w_q_b2, w_kv_b2, w_o_a2, w_o_b2, w_o2))
    c_all = _all_gather8(c, "gather_c").reshape(8, D)
    wconv_all = _all_gather8(w_conv[0], "gather_wconv")
    w_conv_full = jnp.transpose(wconv_all[0::2], (1, 0, 2)).reshape(3, D)
    b_sh = lax.dynamic_slice(b_ada, (0, chip * NA), (1, NA))
    mod_sh = _ada_fwd(c_all, w_ada2, b_sh)
    mod_all = _all_gather8(mod_sh, "gather_mod")
    mod = lax.dynamic_slice(mod_all[0::2], (0, dev, 0), (4, 1, NA)).reshape(6, D)
    shift1, scale1, gate1, shift2, scale2, gate2 = (mod[k:k + 1] for k in range(6))

    g_in, shift1, w_conv_full = _run_plan(_gather_plan([sh_in]), "gather_first", ride=[shift1, w_conv_full])
    sh_a1, sh_a2 = [sh_qb, sh_kvb], [sh_oa, sh_ob, sh_o]
    cl_a1, cl_a2 = _gather_copies(sh_a1), _gather_copies(sh_a2)
    st_a1 = _split_start(sh_a1, cl_a1, shift1, "gather_a1_start")
    st_a2 = _split_start(sh_a2, cl_a2, st_a1[4], "gather_a2_start")
    g_in, w_fi_f32, w_fo_f32, shift1 = lax.optimization_barrier((g_in, w_ffn_in2, w_ffn_out2, st_a2[4][0]))
    g_in = whole(g_in)
    sh_fi, sh_fo = halves(w_fi_f32.astype(BF16)), halves(w_fo_f32.astype(BF16))
    cl_fi, cl_fo = _gather_copies([sh_fi]), _gather_copies([sh_fo])

    def in_rows(lo, hi):
        parts = [g_in[p, max(lo, p * CS) - p * CS:min(hi, (p + 1) * CS) - p * CS]
                 for p in range(4) if max(lo, p * CS) < min(hi, (p + 1) * CS)]
        return parts[0] if len(parts) == 1 else jnp.concatenate(parts, axis=0)

    n_qkv = Q_LORA + KV_LORA + QK_ROPE
    W_qkvT = jnp.pad(in_rows(0, n_qkv), ((0, QKV_A - n_qkv), (0, 0)))
    W_convT = in_rows(n_qkv, n_qkv + 3 * D)
    W_gateT = in_rows(n_qkv + 3 * D, n_qkv + 5 * D)

    u = _modulate(x2, scale1, shift1, "modulate1")
    pq = _matmul(u, W_qkvT, "nt", F32, "proj_qkv")
    pc = _matmul(u, W_convT, "nt", BF16, "proj_conv")
    sh_a1, la1 = _split_wait(st_a1, cl_a1, [pc], "gather_a1_wait")
    pg, (g_qb, g_kvb) = _matmul(u, W_gateT, "nt", BF16, "proj_gate", carry=_gather_plan(sh_a1, into=la1, ici=False))
    st_fi = _split_start([sh_fi], cl_fi, g_q_a, "gather_fi_start", after=[pg])
    W_qb = jnp.pad(cols(whole(g_qb)).reshape(Q_LORA, N_HEADS, QK_NOPE + QK_ROPE),
                   ((0, 0), (0, 0), (0, QK_PAD - QK_NOPE - QK_ROPE))).reshape(Q_LORA, N_HEADS * QK_PAD)
    W_kvb = cols(whole(g_kvb))
    rq, rkv, kr = _rms_fwd(pq, tab, st_fi[4][0], g_kv_a)
    kv = _matmul(rkv, W_kvb, "nn", BF16, "kv_b")
    sh_a2, la2 = _split_wait(st_a2, cl_a2, [kv], "gather_a2_wait")
    def rope_heads(r, t):
        return jnp.concatenate([r[:, lo:lo + 128] if lo % QK_PAD == 0 else _rope(r[:, lo:lo + 128], t, 1)
                                for lo in range(0, r.shape[1], 128)], axis=1)

    q, (g_oa, g_ob, g_o) = _matmul(rq, W_qb, "nn", BF16, "q_b", carry=_gather_plan(sh_a2, into=la2, ici=False),
                                   finish=(rope_heads, tab))
    o, lse = _attn_fwd(q, kv, kr)
    W_oa, W_ob, W_o = (g.reshape(-1, D) for g in (g_oa, g_ob, g_o))
    hb = _conv_fwd(pc, w_conv_full)
    st_fo = _split_start([sh_fo], cl_fo, ln1_g, "gather_fo_start", after=[o])
    y_b = _matmul(hb, W_ob, "nn", BF16, "o_b")
    y_a = _matmul(o, W_oa, "nn", BF16, "o_a")
    merged = _merge_fwd(y_a, y_b, pg)
    sh_fi_t, lfi = _split_wait(st_fi, cl_fi, [merged], "gather_fi_wait")
    mix, g_fi = _matmul(merged, W_o, "nn", F32, "w_o", carry=_gather_plan(sh_fi_t, (0, 2), into=lfi, ici=False))
    (x1, u2), (g_fi,) = _ln1_fwd(x2, mix, gate1, st_fo[4][0], ln1_b, scale2, shift2,
                                 carry=_gather_plan(sh_fi_t, (1, 2), into=g_fi, ici=False))
    W_fi = whole(g_fi)
    hh = _matmul(u2, W_fi, "nn", BF16, "ffn_in", shards="b")
    sh_fo_t, lfo = _split_wait(st_fo, cl_fo, [hh], "gather_fo_wait")
    act, (g_fo,) = _swiglu_fwd(hh, carry=_gather_plan(sh_fo_t, into=lfo, ici=False))
    W_fo = g_fo.reshape(F, D)
    ffn = _matmul(act, W_fo, "nn", F32, "ffn_out")

    core_i = ac.astype(jnp.int32).reshape(1)
    chip_i = chip.astype(jnp.int32).reshape(1)

    def uncols(g):
        return jnp.transpose(g.reshape(g.shape[0], 4, g.shape[1] // 4), (1, 0, 2))

    def slabs(p):
        return p.reshape(4, 2, p.shape[1] // 2, p.shape[2])

    def add_pairs(parts, sibs, nms):
        return [_add_pair(a, b, core_i, "add_pair_" + nm) for a, b, nm in zip(parts, sibs, nms)]

    def sum_all(pre, recv, nms):
        return [_sum_slabs(a, r, chip_i, "sum_slabs_" + nm) for a, r, nm in zip(pre, recv, nms)]

    dffn, dx1a, loss_acc, d_ln2_g, d_ln2_b, d_gate2 = _ln2_loss_bwd(x1, ffn, gate2, ln2_g, ln2_b, tgt)
    dW_fo = _matmul(act, dffn, "tn", BF16, "d_w_ffn_out")
    p_fo = [slabs(dW_fo.reshape(4, -1, D))]
    dact, s_fo = _matmul(dffn, W_fo, "nt", BF16, "d_act", carry=_pair_plan(p_fo))
    pre_fo = add_pairs(p_fo, s_fo, ["w_ffn_out"])
    cs_fo = _scatter_copies(pre_fo)
    st_sfo = _split_start(pre_fo, cs_fo, scale2, "scatter_fo_start")
    dhh = _swiglu_bwd(dact, hh)
    dW_fi = _matmul(u2, dhh, "tn", BF16, "d_w_ffn_in", shards="o")
    p_fi = [slabs(dW_fi)]
    du2, s_fi = _matmul(dhh, W_fi, "nt", F32, "d_u2", carry=_pair_plan(p_fi), shards="b")
    pre_fi = add_pairs(p_fi, s_fi, ["w_ffn_in"])
    cs_fi = _scatter_copies(pre_fi)
    st_sfi = _split_start(pre_fi, cs_fi, st_sfo[4], "scatter_fi_start")
    dmix, dxa, d_shift2, d_scale2, d_ln1_g, d_ln1_b, d_gate1 = _ln1_bwd(x2, mix, dx1a, du2, gate1, ln1_g, ln1_b, st_sfi[4][0])
    dW_o = _matmul(merged, dmix, "tn", BF16, "d_w_o")
    dmerged = _matmul(dmix, W_o, "nt", BF16, "d_merged")
    dy_a, dy_b, dgate = _merge_bwd(dmerged, y_a, y_b, pg)
    dW_oa = _matmul(o, dy_a, "tn", BF16, "d_w_o_a")
    do = _matmul(dy_a, W_oa, "nt", BF16, "d_o")
    dW_ob = _matmul(hb, dy_b, "tn", BF16, "d_w_o_b")
    p_mid = [slabs(g.reshape(4, -1, D)) for g in (dW_oa, dW_ob, dW_o)]
    dhb, s_mid = _matmul(dy_b, W_ob, "nt", BF16, "d_hb", carry=_pair_plan(p_mid))
    pre_mid = add_pairs(p_mid, s_mid, ["w_o_a", "w_o_b", "w_o"])
    cs_mid = _scatter_copies(pre_mid)
    st_smid = _split_start(pre_mid, cs_mid, w_conv_full, "scatter_mid_start")
    dconv, d_wconv = _conv_bwd(dhb, pc, st_smid[4][0])
    dq, dkv, dkr, _ = _attn_bwd(q, kv, kr, do, o, lse, tab, carry=_token_plan(st_smid[4][0]))
    names_a = ["w_ffn_out", "w_ffn_in", "w_o_a", "w_o_b", "w_o"]
    dW_qb = _matmul(rq, dq, "tn", BF16, "d_w_q_b")
    d_rq = _matmul(dq, W_qb, "nt", F32, "d_rq")
    dW_kvb = _matmul(rkv, dkv, "tn", BF16, "d_w_kv_b")
    d_rkv = _matmul(dkv, W_kvb, "nt", F32, "d_rkv")
    dqkv, d_g_q, d_g_kv = _rms_bwd(d_rq, d_rkv, pq, dkr, g_q_a, g_kv_a)
    dW_qkvT = _matmul(dqkv, u, "tn", BF16, "d_w_qkv")
    dW_convT = _matmul(dconv, u, "tn", BF16, "d_w_conv")
    dW_gateT = _matmul(dgate, u, "tn", BF16, "d_w_gate")
    pre_fo, r_fo = _split_wait(st_sfo, cs_fo, [dW_qkvT], "scatter_fo_wait")
    pre_fi, r_fi = _split_wait(st_sfi, cs_fi, [dW_qkvT], "scatter_fi_wait")
    pre_mid, r_mid = _split_wait(st_smid, cs_mid, [dW_qkvT], "scatter_mid_wait")
    fin_a = sum_all(pre_fo + pre_fi + pre_mid, r_fo + r_fi + r_mid, names_a)
    srcs = [(0, dW_qkvT[:n_qkv]), (n_qkv, dW_convT), (n_qkv + 3 * D, dW_gateT)]
    rows_of = []
    for p in range(4):
        for lo, src in srcs:
            a, b = max(lo, p * CS), min(lo + src.shape[0], (p + 1) * CS)
            if a < b:
                rows_of.append(src[a - lo:b - lo])
        rows_of.append(jnp.zeros((CSP - CS, D), BF16))
    dW_inT = jnp.concatenate(rows_of, axis=0).reshape(4, CSP, D)
    dW_qb_u = dW_qb.reshape(Q_LORA, N_HEADS, QK_PAD)[:, :, :QK_NOPE + QK_ROPE].reshape(Q_LORA, -1)
    names_b = ["w_in", "w_q_b", "w_kv_b"]
    p_b = [slabs(dW_inT), slabs(uncols(dW_qb_u)), slabs(uncols(dW_kvb))]
    du, s_b = _matmul(dqkv, W_qkvT, "nn", F32, "d_u_qkv", carry=_pair_plan(p_b))
    pre_b = add_pairs(p_b, s_b, names_b)
    cs_b = _scatter_copies(pre_b)
    st_b = _split_start(pre_b, cs_b, scale1, "scatter_last_start")
    du, fs_a = _matmul(dconv, W_convT, "nn", F32, "d_u_conv", add=du, carry=_sibling_plan(fin_a))
    du = _matmul(dgate, W_gateT, "nn", F32, "d_u_gate", add=du)
    grad_x, d_shift1, d_scale1 = _dx_final(dxa, du, x2, st_b[4][0])

    big = {}
    ws = dict(w_in=(w_inT, m_w_inT, v_w_inT), w_q_b=(w_q_b2, m_w_q_b[0], v_w_q_b[0]),
              w_kv_b=(w_kv_b2, m_w_kv_b[0], v_w_kv_b[0]), w_o_a=(w_o_a2, m_w_o_a[0], v_w_o_a[0]),
              w_o_b=(w_o_b2, m_w_o_b[0], v_w_o_b[0]), w_o=(w_o2, m_w_o[0], v_w_o[0]),
              w_ffn_in=(w_ffn_in2, m_w_ffn_in[0], v_w_ffn_in[0]), w_ffn_out=(w_ffn_out2, m_w_ffn_out[0], v_w_ffn_out[0]))

    def adam_of(nm, a, b, carry=None):
        w_, m_, v_ = ws[nm]
        return _adam_halves("adam_" + nm, w_, m_, v_, a, b, core_i, carry)

    for nm, a, b in zip(names_a, fin_a, fs_a):
        big[nm] = adam_of(nm, a, b, _token_plan(st_b[4][0]))[0]
    done = [big[nm][1] for nm in names_a] + [grad_x]
    pre_b, r_b = _split_wait(st_b, cs_b, done, "scatter_last_wait")
    fin_b = sum_all(pre_b, r_b, names_b)
    fs_b = _run_plan(_sibling_plan(fin_b), "sibling_last")
    for nm, a, b in zip(names_b, fin_b, fs_b):
        big[nm] = adam_of(nm, a, b)

    def pad_d(v):
        return jnp.pad(v, ((0, 0), (0, D - v.shape[1])))

    small = _pack_rows([d_ln1_g, d_ln1_b, d_ln2_g, d_ln2_b, pad_d(d_g_q), pad_d(d_g_kv), d_wconv,
                         d_shift1, d_scale1, d_gate1, d_shift2, d_scale2, d_gate2, pad_d(loss_acc)], 16, after=[pre_b[1]])
    small_all = _all_gather8(small, "gather_small")
    small_sum = _sum8(small_all)
    loss = small_sum[15, 0]
    g_ln1_g, g_ln1_b, g_ln2_g, g_ln2_b = (small_sum[k:k + 1] for k in range(4))
    g_g_q, g_g_kv = small_sum[4:5, :Q_LORA], small_sum[5:6, :KV_LORA]
    g_wconv = lax.dynamic_slice(small_sum[6:9], (0, chip * CW), (3, CW))
    g_b_ada = small_sum[9:15].reshape(1, 6 * D)
    dmod_all = small_all[:, 9:15, :].reshape(8, 6 * D)
    g_w_ada = _ada_bwd(c_all, lax.dynamic_slice(dmod_all, (0, chip * NA), (8, NA)))
    big["w_ada"] = [g_w_ada] + list(_adam("adam_w_ada", w_ada2, m_w_ada[0], v_w_ada[0], g_w_ada))
    sm = {}
    for nm, w_, m_, v_, g_ in [("b_ada", b_ada, m_b_ada, v_b_ada, g_b_ada), ("g_q_a", g_q_a, m_g_q_a, v_g_q_a, g_g_q),
                               ("g_kv_a", g_kv_a, m_g_kv_a, v_g_kv_a, g_g_kv),
                               ("w_conv", w_conv[0], m_w_conv[0], v_w_conv[0], g_wconv),
                               ("ln1_g", ln1_g, m_ln1_g, v_ln1_g, g_ln1_g), ("ln1_b", ln1_b, m_ln1_b, v_ln1_b, g_ln1_b),
                               ("ln2_g", ln2_g, m_ln2_g, v_ln2_g, g_ln2_g), ("ln2_b", ln2_b, m_ln2_b, v_ln2_b, g_ln2_b)]:
        sm[nm] = (g_,) + tuple(_adam_small("adam_" + nm, w_, m_, v_, g_))

    order = ["w_ada", "b_ada", "w_in", "g_q_a", "w_q_b", "g_kv_a", "w_kv_b", "w_o_a", "w_conv", "w_o_b", "w_o",
             "ln1_g", "ln1_b", "w_ffn_in", "w_ffn_out", "ln2_g", "ln2_b"]
    lead = {"b_ada", "g_q_a", "g_kv_a", "ln1_g", "ln1_b", "ln2_g", "ln2_b"}

    def leaf(nm, k):
        val = big[nm][k] if nm in big else sm[nm][k]
        if nm == "w_in":
            val = val.T
        return val if nm in lead else val[None]

    outs = [loss, grad_x[None]]
    for k in range(4):
        outs += [leaf(nm, k) for nm in order]
    return tuple(outs)
```

```python
import jax
import jax.numpy as jnp
from jax import lax
from jax.experimental import pallas as pl
from jax.experimental.pallas import tpu as pltpu

F32, BF16 = jnp.float32, jnp.bfloat16
N_HEADS, QK_NOPE, QK_ROPE, V_HEAD = 16, 128, 64, 128
Q_LORA, KV_LORA = 512, 512
QK_PAD = 256
QKV_A = 1152
CHUNK_SHIFT = 6
ATTN_SCALE = (QK_NOPE + QK_ROPE) ** -0.5
LOG2E = 1.4426950408889634
SCALE2 = ATTN_SCALE * LOG2E
ROPE_THETA = 10000.0
ALPHA = 2.0 ** 0.25
LN_EPS, RMS_EPS = 1e-5, 1e-6
ADAM_LR, ADAM_B1, ADAM_B2, ADAM_EPS, ADAM_WD, ADAM_STEP = 0.001, 0.9, 0.999, 1e-08, 0.01, 10
ADAM_C1 = 1.0 - ADAM_B1 ** ADAM_STEP
ADAM_C2 = 1.0 - ADAM_B2 ** ADAM_STEP
VMEM_LIMIT = 56 * 1024 * 1024
MESH = pl.DeviceIdType.MESH
ANY = pl.BlockSpec(memory_space=pl.ANY)
HBM_SPEC = pl.BlockSpec(memory_space=pltpu.HBM)
SEM_SPEC = pl.BlockSpec(memory_space=pltpu.SEMAPHORE)
NT = (((1,), (1,)), ((), ()))
TN = (((0,), (0,)), ((), ()))
NN = (((1,), (0,)), ((), ()))


def _params(sem=None):
    return pltpu.CompilerParams(dimension_semantics=sem, vmem_limit_bytes=VMEM_LIMIT)


def _pick(n, cands=(1408, 1024, 512, 384, 256, 128)):
    for t in cands:
        if n % t == 0:
            return t
    return n


def _row_tile(rows, row_bytes, budget, mult=8):
    best = mult
    for t in range(mult, rows + 1, mult):
        if rows % t == 0 and t * row_bytes <= budget:
            best = t
    return best


def _tile2(rows, cols, mult=8, budget=3 << 18):
    col_tiles = [t for t in range(128, cols + 1, 128) if cols % t == 0] or [cols]
    best = None
    for tc in col_tiles:
        for tr in range(mult, rows + 1, mult):
            if rows % tr == 0 and tr * tc <= budget and (best is None or (tr * tc, tc) > (best[0] * best[1], best[1])):
                best = (tr, tc)
    assert best is not None, (rows, cols)
    return best


def _sigmoid(x):
    return jax.nn.sigmoid(x)


class _Plan:
    def __init__(self, ins, outs, sems, start, finish, aliases=None):
        self.ins, self.outs, self.sems, self.start, self.finish = list(ins), list(outs), list(sems), start, finish
        self.aliases = dict(aliases or {})

    def io_aliases(self, first_in, first_out):
        return {first_in + i: first_out + o for i, o in self.aliases.items()}


def _token_plan(token):
    return _Plan([token], [], [], lambda *a: None, lambda *a: None)


def _run_plan(plan, name, ride=None):
    n_in, n_out = len(plan.ins), len(plan.outs)
    extra = [] if ride is None else list(ride)
    aliases = plan.io_aliases(0, 0)
    for k in range(len(extra)):
        aliases[n_in + k] = n_out + k

    def body(*refs):
        ins, outs, sems = refs[:n_in], refs[n_in + len(extra):n_in + len(extra) + n_out], refs[n_in + 2 * len(extra) + n_out:]
        plan.start(ins, outs, sems)
        plan.finish(ins, outs, sems)

    return pl.pallas_call(body, name=name, out_shape=plan.outs + [jax.ShapeDtypeStruct(r.shape, r.dtype) for r in extra],
                          in_specs=[ANY] * (n_in + len(extra)), out_specs=[ANY] * (n_out + len(extra)),
                          scratch_shapes=plan.sems, input_output_aliases=aliases,
                          compiler_params=_params())(*plan.ins, *extra)


def _matmul(a, b, mode, out_dtype, name, add=None, carry=None, shards=None, finish=None):
    if mode == "nn":
        (M, K), N, dims = a.shape, b.shape[-1] * (4 if shards else 1), NN
    elif mode == "nt":
        (M, K), N, dims = a.shape, b.shape[-2], NT
    else:
        (K, M), N, dims = a.shape, b.shape[1], TN
    split_n = shards and mode != "nt"
    tm = _pick(M)
    tn = _pick(N // 4) if split_n else _pick(N)
    deep = (2816, 2048, 1408, 1024, 512, 384, 256, 128)
    if shards and mode == "nt":
        tk = _pick(K // 4, deep)
    else:
        tk = K if K <= 2048 else _pick(K, deep)
    nk = K // tk
    per = (N // 4 // tn) if split_n else (K // 4 // tk if shards else 1)
    a_spec = (pl.BlockSpec((tk, tm), lambda i, j, k: (k, i)) if mode == "tn"
              else pl.BlockSpec((tm, tk), lambda i, j, k: (i, k)))
    if shards == "b" and mode == "nn":
        b_spec = pl.BlockSpec((None, tk, tn), lambda i, j, k: (j // per, k, j % per))
    elif shards == "b":
        b_spec = pl.BlockSpec((None, tn, tk), lambda i, j, k: (k // per, j, k % per))
    else:
        b_spec = (pl.BlockSpec((tn, tk), lambda i, j, k: (j, k)) if mode == "nt"
                  else pl.BlockSpec((tk, tn), lambda i, j, k: (k, j)))
    o_spec = pl.BlockSpec((tm, tn), lambda i, j, k: (i, j))
    o_shape = (M, N)
    if shards == "o":
        o_spec, o_shape = pl.BlockSpec((None, tm, tn), lambda i, j, k: (j // per, i, j % per)), (4, M, N // 4)
    has_add = add is not None
    has_fin = finish is not None
    n_ci = len(carry.ins) if carry else 0
    n_co = len(carry.outs) if carry else 0
    n_in = 2 + has_add + has_fin
    grid = (M // tm, N // tn, nk)

    def body(*refs):
        a_ref, b_ref = refs[0], refs[1]
        add_ref = refs[2] if has_add else None
        fin_ref = refs[2 + has_add] if has_fin else None

        def store(r):
            if has_add:
                r = r + add_ref[...]
            if has_fin:
                r = finish[0](r, fin_ref[...])
            o_ref[...] = r.astype(o_ref.dtype)

        o_ref = refs[n_in + n_ci]
        acc_ref = refs[n_in + n_ci + 1 + n_co] if nk > 1 else None
        c_ins = refs[n_in:n_in + n_ci]
        c_outs = refs[n_in + n_ci + 1:n_in + n_ci + 1 + n_co]
        c_sems = refs[n_in + n_ci + 1 + n_co + (nk > 1):]
        i, j, k = pl.program_id(0), pl.program_id(1), pl.program_id(2)

        if carry:
            @pl.when((i == 0) & (j == 0) & (k == 0))
            def _():
                carry.start(c_ins, c_outs, c_sems)

        part = lax.dot_general(a_ref[...], b_ref[...], dims, preferred_element_type=F32)
        if nk == 1:
            store(part)
        else:
            @pl.when(k == 0)
            def _():
                acc_ref[...] = part

            @pl.when((k > 0) & (k < nk - 1))
            def _():
                acc_ref[...] += part

            @pl.when(k == nk - 1)
            def _():
                store(acc_ref[...] + part)

        if carry:
            @pl.when((i == grid[0] - 1) & (j == grid[1] - 1) & (k == nk - 1))
            def _():
                carry.finish(c_ins, c_outs, c_sems)

    ins = [a, b] + ([add] if has_add else []) + ([finish[1]] if has_fin else []) + (carry.ins if carry else [])
    in_specs = ([a_spec, b_spec] + ([o_spec] if has_add else [])
                + ([pl.BlockSpec((tm, finish[1].shape[1]), lambda i, j, k: (i, 0))] if has_fin else []) + [ANY] * n_ci)
    res = pl.pallas_call(
        body, name=name, grid=grid,
        in_specs=in_specs, out_specs=[o_spec] + [ANY] * n_co,
        out_shape=[jax.ShapeDtypeStruct(o_shape, out_dtype)] + (carry.outs if carry else []),
        scratch_shapes=([pltpu.VMEM((tm, tn), F32)] if nk > 1 else []) + (carry.sems if carry else []),
        input_output_aliases=carry.io_aliases(n_in, 1) if carry else {},
        compiler_params=_params(("arbitrary",) * 3 if carry else ("parallel", "parallel", "arbitrary")),
    )(*ins)
    return (res[0], res[1:]) if carry else res[0]


def _rows(body, name, n_rows, tm, ins, outs, accs=(), carry=None):
    grid = (n_rows // tm,)

    def halo(arr):
        return 16 if arr.dtype == BF16 else 8

    arrays, in_specs = [], []
    for spec in ins:
        kind, arr = spec[0], spec[1]
        arrays.append(arr)
        if kind == "row":
            _, _, cb, w = spec
            in_specs.append(pl.BlockSpec((tm, w), lambda i, cb=cb: (i, cb)))
        elif kind == "full":
            in_specs.append(pl.BlockSpec(arr.shape, lambda i, nd=arr.ndim: (0,) * nd))
        elif kind == "prev":
            _, _, cb, w = spec
            h = halo(arr)
            in_specs.append(pl.BlockSpec((h, w), lambda i, cb=cb, per=tm // h: (jnp.maximum(i * per - 1, 0), cb)))
        else:
            _, _, cb, w = spec
            h = halo(arr)
            in_specs.append(pl.BlockSpec((h, w), lambda i, cb=cb, per=tm // h, last=n_rows // h - 1:
                                         (jnp.minimum((i + 1) * per, last), cb)))
    out_shape = [jax.ShapeDtypeStruct((n_rows, w), dt) for (w, dt) in outs]
    out_specs = [pl.BlockSpec((tm, w), lambda i: (i, 0)) for (w, _) in outs]
    out_shape += [jax.ShapeDtypeStruct(s, F32) for s in accs]
    out_specs += [pl.BlockSpec(s, lambda i, nd=len(s): (0,) * nd) for s in accs]
    n_in, n_out, n_acc = len(ins), len(outs), len(accs)
    n_ci = len(carry.ins) if carry else 0
    n_co = len(carry.outs) if carry else 0

    def kernel_body(*refs):
        first = n_in + n_ci
        c_ins, c_outs, c_sems = refs[n_in:first], refs[first + n_out + n_acc:first + n_out + n_acc + n_co], refs[first + n_out + n_acc + n_co:]
        if carry:
            @pl.when(pl.program_id(0) == 0)
            def _():
                carry.start(c_ins, c_outs, c_sems)

        body(pl.program_id(0), refs[:n_in], refs[first:first + n_out], refs[first + n_out:first + n_out + n_acc])
        if carry:
            @pl.when(pl.program_id(0) == grid[0] - 1)
            def _():
                carry.finish(c_ins, c_outs, c_sems)

    res = pl.pallas_call(
        kernel_body, name=name, grid=grid, in_specs=in_specs + [ANY] * n_ci, out_specs=out_specs + [ANY] * n_co,
        out_shape=out_shape + (carry.outs if carry else []), scratch_shapes=carry.sems if carry else [],
        input_output_aliases=carry.io_aliases(n_in, n_out + n_acc) if carry else {},
        compiler_params=_params(("arbitrary",)),
    )(*arrays, *(carry.ins if carry else []))
    return (res[:n_out + n_acc], res[n_out + n_acc:]) if carry else res


def _acc_add(i, ref, val):
    @pl.when(i == 0)
    def _():
        ref[...] = val

    @pl.when(i > 0)
    def _():
        ref[...] += val


def _rope(t, tab, sign):
    c, sa, sb = tab[:, 0:128], tab[:, 128:256], tab[:, 256:384]
    rot = pltpu.roll(t, 96, 1) * sa + pltpu.roll(t, 32, 1) * sb
    return t * c + rot if sign > 0 else t * c - rot


def _ln_stats(r):
    mu = jnp.mean(r, axis=-1, keepdims=True)
    d = r - mu
    var = jnp.mean(d * d, axis=-1, keepdims=True)
    rstd = lax.rsqrt(var + LN_EPS)
    return d * rstd, rstd


def _ln_bwd(dxh, xh, rstd):
    m1 = jnp.mean(dxh, axis=-1, keepdims=True)
    m2 = jnp.mean(dxh * xh, axis=-1, keepdims=True)
    return rstd * (dxh - m1 - xh * m2)


def _modulate(x, scale, shift, name, carry=None):
    S, D = x.shape

    def body(i, ins, outs, accs):
        outs[0][...] = (ins[0][...] * (1.0 + ins[1][...]) + ins[2][...]).astype(BF16)

    res = _rows(body, name, S, _pick(S, (256, 128)), [("row", x, 0, D), ("full", scale), ("full", shift)], [(D, BF16)],
                carry=carry)
    return (res[0][0], res[1]) if carry else res[0]


def _rms_fwd(pq, tab, g_q, g_kv):
    S = pq.shape[0]

    def body(i, ins, outs, accs):
        pq_ref, tab_ref, gq_ref, gkv_ref = ins

        def rms(x, g):
            return x * lax.rsqrt(jnp.mean(x * x, axis=-1, keepdims=True) + RMS_EPS) * g

        outs[0][...] = rms(pq_ref[:, 0:Q_LORA], gq_ref[...]).astype(BF16)
        outs[1][...] = rms(pq_ref[:, Q_LORA:Q_LORA + KV_LORA], gkv_ref[...]).astype(BF16)
        outs[2][...] = _rope(pq_ref[:, Q_LORA + KV_LORA:QKV_A], tab_ref[...], 1).astype(BF16)

    return _rows(body, "rms_fwd", S, _pick(S, (256, 128)),
                 [("row", pq, 0, QKV_A), ("row", tab, 0, 384), ("full", g_q), ("full", g_kv)],
                 [(Q_LORA, BF16), (KV_LORA, BF16), (128, BF16)])


def _allowed(q0, k0, bq):
    row = q0 + lax.broadcasted_iota(jnp.int32, (bq, bq), 0)
    col = k0 + lax.broadcasted_iota(jnp.int32, (bq, bq), 1)
    return (col >> CHUNK_SHIFT) <= (row >> CHUNK_SHIFT)


ATTN_BLOCK = 512


HEADS_PER_STEP = 2


def _attn_fwd(q, kv, kr):
    S = q.shape[0]
    bq = min(ATTN_BLOCK, S)
    nq = S // bq
    G = HEADS_PER_STEP

    def body(q_ref, kv_ref, kr_ref, o_ref, lse_ref, kcat):
        qi = pl.program_id(1)

        @pl.when(qi == 0)
        def _():
            for g in range(G):
                kcat[g, :, 0:128] = kv_ref[:, g * 256:g * 256 + 128]
                kcat[g, :, 128:256] = kr_ref[...]

        qs = [q_ref[:, g * QK_PAD:(g + 1) * QK_PAD] for g in range(G)]

        def step(j, carry, masked):
            off = pl.multiple_of(j * bq, bq)
            rows = pl.ds(off, bq)
            mask = _allowed(qi * bq, off, bq) if masked else None
            out = []
            for g in range(G):
                m, l, acc = carry[g]
                s = lax.dot_general(qs[g], kcat[g, rows, :], NT, preferred_element_type=F32) * SCALE2
                if masked:
                    s = jnp.where(mask, s, -1e30)
                m_new = jnp.maximum(m, jnp.max(s, axis=1, keepdims=True))
                a = jnp.exp2(m - m_new)
                p = jnp.exp2(s - m_new)
                l = a * l + jnp.sum(p, axis=1, keepdims=True)
                acc = a * acc + jnp.dot(p.astype(BF16), kv_ref[rows, g * 256 + 128:(g + 1) * 256],
                                        preferred_element_type=F32)
                out.append((m_new, l, acc))
            return tuple(out)

        init = tuple((jnp.full((bq, 1), -1e30, F32), jnp.zeros((bq, 1), F32), jnp.zeros((bq, V_HEAD), F32))
                     for _ in range(G))
        below = lax.fori_loop(0, qi, lambda j, cr: step(j, cr, False), init)
        for g, (m, l, acc) in enumerate(step(qi, below, True)):
            o_ref[:, g * V_HEAD:(g + 1) * V_HEAD] = (acc / l).astype(BF16)
            lse_ref[g] = m + jnp.log2(l)

    return pl.pallas_call(
        body, name="attn_fwd", grid=(N_HEADS // G, nq),
        in_specs=[pl.BlockSpec((bq, G * QK_PAD), lambda h, i: (i, h)),
                  pl.BlockSpec((S, G * 256), lambda h, i: (0, h)),
                  pl.BlockSpec((S, 128), lambda h, i: (0, 0))],
        out_specs=[pl.BlockSpec((bq, G * V_HEAD), lambda h, i: (i, h)),
                   pl.BlockSpec((G, bq, 1), lambda h, i: (h, i, 0))],
        out_shape=[jax.ShapeDtypeStruct((S, N_HEADS * V_HEAD), BF16),
                   jax.ShapeDtypeStruct((N_HEADS, S, 1), F32)],
        scratch_shapes=[pltpu.VMEM((G, S, QK_PAD), BF16)],
        compiler_params=_params(("arbitrary", "arbitrary")),
    )(q, kv, kr)


def _attn_bwd(q, kv, kr, do, o, lse, tab, carry=None):
    S = q.shape[0]
    bq = min(ATTN_BLOCK, S)
    nq = S // bq

    n_ci = len(carry.ins) if carry else 0
    n_co = len(carry.outs) if carry else 0

    def body(*refs):
        q_ref, kn_ref, v_ref, kr_ref, do_ref, o_ref, lse_ref, tab_ref = refs[:8]
        dq_ref, dkv_ref, dkr_ref = refs[8 + n_ci:11 + n_ci]
        dq_acc, dk_acc, dv_acc, kcat, delta = refs[11 + n_ci + n_co:16 + n_ci + n_co]
        c_ins, c_outs, c_sems = refs[8:8 + n_ci], refs[11 + n_ci:11 + n_ci + n_co], refs[16 + n_ci + n_co:]
        h = pl.program_id(0)
        if carry:
            @pl.when(h == 0)
            def _():
                carry.start(c_ins, c_outs, c_sems)

        dq_acc[...] = jnp.zeros_like(dq_acc)
        dk_acc[...] = jnp.zeros_like(dk_acc)
        dv_acc[...] = jnp.zeros_like(dv_acc)
        kcat[:, 0:128] = kn_ref[...]
        kcat[:, 128:256] = kr_ref[...]
        for r in range(nq):
            rows = slice(r * bq, (r + 1) * bq)
            delta[rows, :] = jnp.sum(do_ref[rows, :].astype(F32) * o_ref[rows, :].astype(F32), axis=1, keepdims=True)

        def pair(i, j, masked):
            rows_i = pl.ds(pl.multiple_of(i * bq, bq), bq)
            rows_j = pl.ds(pl.multiple_of(j * bq, bq), bq)
            qv, dov, k = q_ref[rows_i, :], do_ref[rows_i, :], kcat[rows_j, :]
            s = lax.dot_general(qv, k, NT, preferred_element_type=F32) * SCALE2
            if masked:
                s = jnp.where(_allowed(i * bq, j * bq, bq), s, -1e30)
            p = jnp.exp2(s - lse_ref[0, rows_i, :])
            dv_acc[rows_j, :] += lax.dot_general(p.astype(BF16), dov, TN, preferred_element_type=F32)
            dp = lax.dot_general(dov, v_ref[rows_j, :], NT, preferred_element_type=F32)
            ds = (p * (dp - delta[rows_i, :]) * ATTN_SCALE).astype(BF16)
            dk_acc[rows_j, :] += lax.dot_general(ds, qv, TN, preferred_element_type=F32)
            dq_acc[rows_i, :] += jnp.dot(ds, k, preferred_element_type=F32)

        def kv_step(j, _):
            pair(j, j, True)

            def q_step(i, _):
                pair(i, j, False)
                return 0

            lax.fori_loop(j + 1, nq, q_step, 0)
            return 0

        lax.fori_loop(0, nq, kv_step, 0)

        for r in range(nq):
            rows = slice(r * bq, (r + 1) * bq)
            dq_ref[rows, 0:128] = dq_acc[rows, 0:128].astype(BF16)
            dq_ref[rows, 128:256] = _rope(dq_acc[rows, 128:256], tab_ref[rows, :], -1).astype(BF16)
        dkv_ref[:, 0:128] = dk_acc[:, 0:128].astype(BF16)
        dkv_ref[:, 128:256] = dv_acc[...].astype(BF16)

        @pl.when(h == 0)
        def _():
            dkr_ref[...] = dk_acc[:, 128:256]

        @pl.when(h > 0)
        def _():
            dkr_ref[...] += dk_acc[:, 128:256]

        @pl.when(h == N_HEADS - 1)
        def _():
            for r in range(nq):
                rows = slice(r * bq, (r + 1) * bq)
                dkr_ref[rows, :] = _rope(dkr_ref[rows, :], tab_ref[rows, :], -1)
            if carry:
                carry.finish(c_ins, c_outs, c_sems)

    W = N_HEADS * QK_PAD
    res = pl.pallas_call(
        body, name="attn_bwd", grid=(N_HEADS,),
        in_specs=[pl.BlockSpec((S, QK_PAD), lambda h: (0, h)),
                  pl.BlockSpec((S, 128), lambda h: (0, 2 * h)),
                  pl.BlockSpec((S, 128), lambda h: (0, 2 * h + 1)),
                  pl.BlockSpec((S, 128), lambda h: (0, 0)),
                  pl.BlockSpec((S, V_HEAD), lambda h: (0, h)),
                  pl.BlockSpec((S, V_HEAD), lambda h: (0, h)),
                  pl.BlockSpec((1, S, 1), lambda h: (h, 0, 0)),
                  pl.BlockSpec((S, 384), lambda h: (0, 0))] + [ANY] * n_ci,
        out_specs=[pl.BlockSpec((S, QK_PAD), lambda h: (0, h)),
                   pl.BlockSpec((S, QK_PAD), lambda h: (0, h)),
                   pl.BlockSpec((S, 128), lambda h: (0, 0))] + [ANY] * n_co,
        out_shape=[jax.ShapeDtypeStruct((S, W), BF16), jax.ShapeDtypeStruct((S, W), BF16),
                   jax.ShapeDtypeStruct((S, 128), F32)] + (carry.outs if carry else []),
        scratch_shapes=[pltpu.VMEM((S, QK_PAD), F32), pltpu.VMEM((S, QK_PAD), F32), pltpu.VMEM((S, V_HEAD), F32),
                        pltpu.VMEM((S, QK_PAD), BF16), pltpu.VMEM((S, 1), F32)]
        + (carry.sems if carry else []),
        input_output_aliases=carry.io_aliases(8, 3) if carry else {},
        compiler_params=_params(("arbitrary",)),
    )(q, kv, kv, kr, do, o, lse, tab, *(carry.ins if carry else []))
    return res[0], res[1], res[2], res[3:]


def _shift_down(cur, prev, i, n):
    tm, h = cur.shape[0], prev.shape[0]
    prev = jnp.where(i == 0, jnp.zeros_like(prev), prev)
    full = jnp.concatenate([prev, cur], axis=0)
    return pltpu.roll(full, n, 0)[h:h + tm, :]


def _shift_up(cur, nxt, i, last, n):
    tm, h = cur.shape[0], nxt.shape[0]
    nxt = jnp.where(i == last, jnp.zeros_like(nxt), nxt)
    full = jnp.concatenate([cur, nxt], axis=0)
    return pltpu.roll(full, tm + h - n, 0)[0:tm, :]


def _conv_fwd(pc, w_conv):
    S, D = pc.shape[0], pc.shape[1] // 3
    tm = _pick(S, (256, 128))

    def body(i, ins, outs, accs):
        b_ref, c_ref, x_ref, cp_ref, xp_ref, w_ref = ins
        z = c_ref[...].astype(F32) * x_ref[...].astype(F32)
        zp = cp_ref[...].astype(F32) * xp_ref[...].astype(F32)
        cz = w_ref[0:1, :] * _shift_down(z, zp, i, 2) + w_ref[1:2, :] * _shift_down(z, zp, i, 1) + w_ref[2:3, :] * z
        outs[0][...] = (b_ref[...].astype(F32) * cz).astype(BF16)

    return _rows(body, "conv_fwd", S, tm,
                 [("row", pc, 0, D), ("row", pc, 1, D), ("row", pc, 2, D), ("prev", pc, 1, D), ("prev", pc, 2, D),
                  ("full", w_conv)], [(D, BF16)])[0]


def _conv_bwd(dhb, pc, w_conv):
    S, D = dhb.shape
    tm = _pick(S, (256, 128))
    last = S // tm - 1

    def body(i, ins, outs, accs):
        g_ref, b_ref, c_ref, x_ref, cp_ref, xp_ref, gn_ref, bn_ref, w_ref = ins
        w0, w1, w2 = w_ref[0:1, :], w_ref[1:2, :], w_ref[2:3, :]
        c, x, g = c_ref[...].astype(F32), x_ref[...].astype(F32), g_ref[...].astype(F32)
        z = c * x
        zp = cp_ref[...].astype(F32) * xp_ref[...].astype(F32)
        z1, z2 = _shift_down(z, zp, i, 1), _shift_down(z, zp, i, 2)
        cz = w0 * z2 + w1 * z1 + w2 * z
        dcz = g * b_ref[...].astype(F32)
        dczn = gn_ref[...].astype(F32) * bn_ref[...].astype(F32)
        dz = w2 * dcz + w1 * _shift_up(dcz, dczn, i, last, 1) + w0 * _shift_up(dcz, dczn, i, last, 2)
        outs[0][:, 0:D] = (g * cz).astype(BF16)
        outs[0][:, D:2 * D] = (dz * x).astype(BF16)
        outs[0][:, 2 * D:3 * D] = (dz * c).astype(BF16)
        dw = jnp.concatenate([jnp.sum(dcz * z2, axis=0, keepdims=True), jnp.sum(dcz * z1, axis=0, keepdims=True),
                              jnp.sum(dcz * z, axis=0, keepdims=True)], axis=0)
        _acc_add(i, accs[0], dw)

    return _rows(body, "conv_bwd", S, tm,
                 [("row", dhb, 0, D), ("row", pc, 0, D), ("row", pc, 1, D), ("row", pc, 2, D),
                  ("prev", pc, 1, D), ("prev", pc, 2, D), ("next", dhb, 0, D), ("next", pc, 0, D), ("full", w_conv)],
                 [(3 * D, BF16)], [(3, D)])


def _merge_fwd(y_a, y_b, pg):
    S, D = y_a.shape

    def body(i, ins, outs, accs):
        ya, yb, ga, gb = ins
        outs[0][...] = (_sigmoid(ga[...].astype(F32)) * ya[...].astype(F32)
                        + _sigmoid(gb[...].astype(F32)) * yb[...].astype(F32)).astype(BF16)

    return _rows(body, "merge_fwd", S, _pick(S, (256, 128)),
                 [("row", y_a, 0, D), ("row", y_b, 0, D), ("row", pg, 0, D), ("row", pg, 1, D)], [(D, BF16)])[0]


def _merge_bwd(dm, y_a, y_b, pg):
    S, D = dm.shape

    def body(i, ins, outs, accs):
        d, ya, yb = ins[0][...].astype(F32), ins[1][...].astype(F32), ins[2][...].astype(F32)
        sa, sb = _sigmoid(ins[3][...].astype(F32)), _sigmoid(ins[4][...].astype(F32))
        outs[0][...] = (d * sa).astype(BF16)
        outs[1][...] = (d * sb).astype(BF16)
        outs[2][:, 0:D] = (d * ya * (sa * (1.0 - sa))).astype(BF16)
        outs[2][:, D:2 * D] = (d * yb * (sb * (1.0 - sb))).astype(BF16)

    return _rows(body, "merge_bwd", S, _pick(S, (256, 128)),
                 [("row", dm, 0, D), ("row", y_a, 0, D), ("row", y_b, 0, D), ("row", pg, 0, D), ("row", pg, 1, D)],
                 [(D, BF16), (D, BF16), (2 * D, BF16)])


def _ln1_fwd(x, mix, gate1, g, b, scale2, shift2, carry=None):
    S, D = x.shape

    def body(i, ins, outs, accs):
        x_ref, mix_ref, gate_ref, g_ref, b_ref, sc_ref, sh_ref = ins
        xh, _ = _ln_stats(ALPHA * x_ref[...] + gate_ref[...] * mix_ref[...])
        x1 = xh * g_ref[...] + b_ref[...]
        outs[0][...] = x1
        outs[1][...] = (x1 * (1.0 + sc_ref[...]) + sh_ref[...]).astype(BF16)

    return _rows(body, "ln1_fwd", S, _pick(S, (256, 128)),
                 [("row", x, 0, D), ("row", mix, 0, D), ("full", gate1), ("full", g), ("full", b),
                  ("full", scale2), ("full", shift2)], [(D, F32), (D, BF16)], carry=carry)


def _swiglu_fwd(hh, carry=None):
    S, F = hh.shape[0], hh.shape[1] // 2

    def body(i, ins, outs, accs):
        hg = ins[0][...].astype(F32)
        outs[0][...] = (hg * _sigmoid(hg) * ins[1][...].astype(F32)).astype(BF16)

    res = _rows(body, "swiglu_fwd", S, _pick(S, (128,)), [("row", hh, 0, F), ("row", hh, 1, F)], [(F, BF16)], carry=carry)
    return (res[0][0], res[1]) if carry else res[0]


def _swiglu_bwd(dact, hh):
    S, F = dact.shape

    def body(i, ins, outs, accs):
        d, hg, hu = ins[0][...].astype(F32), ins[1][...].astype(F32), ins[2][...].astype(F32)
        sg = _sigmoid(hg)
        outs[0][:, 0:F] = (d * hu * (sg * (1.0 + hg * (1.0 - sg)))).astype(BF16)
        outs[0][:, F:2 * F] = (d * (hg * sg)).astype(BF16)

    return _rows(body, "swiglu_bwd", S, _pick(S, (128,)),
                 [("row", dact, 0, F), ("row", hh, 0, F), ("row", hh, 1, F)], [(2 * F, BF16)])[0]


def _ln2_loss_bwd(x1, ffn, gate2, g, b, target):
    S, D = x1.shape

    def body(i, ins, outs, accs):
        x1_ref, f_ref, gate_ref, g_ref, b_ref, t_ref = ins
        f = f_ref[...]
        xh, rstd = _ln_stats(ALPHA * x1_ref[...] + gate_ref[...] * f)
        e = xh * g_ref[...] + b_ref[...] - t_ref[...]
        dy = e * (1.0 / D)
        dr = _ln_bwd(dy * g_ref[...], xh, rstd)
        outs[0][...] = (gate_ref[...] * dr).astype(BF16)
        outs[1][...] = ALPHA * dr
        _acc_add(i, accs[0], jnp.full((1, 128), (0.5 / D) * jnp.sum(e * e), F32))
        _acc_add(i, accs[1], jnp.sum(dy * xh, axis=0, keepdims=True))
        _acc_add(i, accs[2], jnp.sum(dy, axis=0, keepdims=True))
        _acc_add(i, accs[3], jnp.sum(dr * f, axis=0, keepdims=True))

    return _rows(body, "ln2_loss_bwd", S, _pick(S, (256, 128)),
                 [("row", x1, 0, D), ("row", ffn, 0, D), ("full", gate2), ("full", g), ("full", b), ("row", target, 0, D)],
                 [(D, BF16), (D, F32)], [(1, 128), (1, D), (1, D), (1, D)])


def _ln1_bwd(x, mix, dx1a, du2, gate1, g, b, scale2):
    S, D = x.shape

    def body(i, ins, outs, accs):
        x_ref, mix_ref, da_ref, du_ref, gate_ref, g_ref, b_ref, sc_ref = ins
        mix, du = mix_ref[...], du_ref[...]
        xh, rstd = _ln_stats(ALPHA * x_ref[...] + gate_ref[...] * mix)
        x1 = xh * g_ref[...] + b_ref[...]
        dx1 = da_ref[...] + du * (1.0 + sc_ref[...])
        dr = _ln_bwd(dx1 * g_ref[...], xh, rstd)
        outs[0][...] = (gate_ref[...] * dr).astype(BF16)
        outs[1][...] = ALPHA * dr
        _acc_add(i, accs[0], jnp.sum(du, axis=0, keepdims=True))
        _acc_add(i, accs[1], jnp.sum(du * x1, axis=0, keepdims=True))
        _acc_add(i, accs[2], jnp.sum(dx1 * xh, axis=0, keepdims=True))
        _acc_add(i, accs[3], jnp.sum(dx1, axis=0, keepdims=True))
        _acc_add(i, accs[4], jnp.sum(dr * mix, axis=0, keepdims=True))

    return _rows(body, "ln1_bwd", S, _pick(S, (256, 128)),
                 [("row", x, 0, D), ("row", mix, 0, D), ("row", dx1a, 0, D), ("row", du2, 0, D),
                  ("full", gate1), ("full", g), ("full", b), ("full", scale2)],
                 [(D, BF16), (D, F32)], [(1, D)] * 5)


def _rms_bwd(d_rq, d_rkv, pq, dkr, g_q, g_kv):
    S = pq.shape[0]

    def body(i, ins, outs, accs):
        dq_ref, dkv_ref, pq_ref, dkr_ref, gq_ref, gkv_ref = ins

        def rms_bwd(dy, x, g):
            r = lax.rsqrt(jnp.mean(x * x, axis=-1, keepdims=True) + RMS_EPS)
            dyg = dy * g
            dx = r * dyg - x * (r * r * r) * jnp.mean(dyg * x, axis=-1, keepdims=True)
            return dx, jnp.sum(dy * (x * r), axis=0, keepdims=True)

        dxq, dgq = rms_bwd(dq_ref[...], pq_ref[:, 0:Q_LORA], gq_ref[...])
        dxkv, dgkv = rms_bwd(dkv_ref[...], pq_ref[:, Q_LORA:Q_LORA + KV_LORA], gkv_ref[...])
        outs[0][:, 0:Q_LORA] = dxq.astype(BF16)
        outs[0][:, Q_LORA:Q_LORA + KV_LORA] = dxkv.astype(BF16)
        outs[0][:, Q_LORA + KV_LORA:QKV_A] = dkr_ref[...].astype(BF16)
        _acc_add(i, accs[0], dgq)
        _acc_add(i, accs[1], dgkv)

    return _rows(body, "rms_bwd", S, _pick(S, (256, 128)),
                 [("row", d_rq, 0, Q_LORA), ("row", d_rkv, 0, KV_LORA), ("row", pq, 0, QKV_A), ("row", dkr, 0, 128),
                  ("full", g_q), ("full", g_kv)], [(QKV_A, BF16)], [(1, Q_LORA), (1, KV_LORA)])


def _dx_final(dxa, du, x, scale1):
    S, D = x.shape

    def body(i, ins, outs, accs):
        du = ins[1][...]
        outs[0][...] = ins[0][...] + du * (1.0 + ins[3][...])
        _acc_add(i, accs[0], jnp.sum(du, axis=0, keepdims=True))
        _acc_add(i, accs[1], jnp.sum(du * ins[2][...], axis=0, keepdims=True))

    return _rows(body, "dx_final", S, _pick(S, (256, 128)),
                 [("row", dxa, 0, D), ("row", du, 0, D), ("row", x, 0, D), ("full", scale1)],
                 [(D, F32)], [(1, D), (1, D)])


def _ada_fwd(c_all, w, bias):
    B, D = c_all.shape
    NA = w.shape[1]
    tn = _pick(NA, (512, 256, 128))

    def body(c_ref, w_ref, b_ref, o_ref):
        cv = c_ref[...]
        ca = (cv * _sigmoid(cv)).astype(BF16)
        o_ref[...] = jnp.dot(ca, w_ref[...].astype(BF16), preferred_element_type=F32) + b_ref[...]

    return pl.pallas_call(
        body, name="ada_fwd", grid=(NA // tn,),
        in_specs=[pl.BlockSpec((B, D), lambda j: (0, 0)), pl.BlockSpec((D, tn), lambda j: (0, j)),
                  pl.BlockSpec((1, tn), lambda j: (0, j))],
        out_specs=pl.BlockSpec((B, tn), lambda j: (0, j)),
        out_shape=jax.ShapeDtypeStruct((B, NA), F32),
        compiler_params=_params(("arbitrary",)),
    )(c_all, w, bias)


def _ada_bwd(c_all, dmod):
    B, D = c_all.shape
    NA = dmod.shape[1]
    tn = _pick(NA, (512, 256, 128))

    def body(c_ref, d_ref, o_ref):
        cv = c_ref[...]
        ca = (cv * _sigmoid(cv)).astype(BF16)
        o_ref[...] = lax.dot_general(ca, d_ref[...].astype(BF16), TN, preferred_element_type=F32)

    return pl.pallas_call(
        body, name="ada_bwd", grid=(NA // tn,),
        in_specs=[pl.BlockSpec((B, D), lambda j: (0, 0)), pl.BlockSpec((B, tn), lambda j: (0, j))],
        out_specs=pl.BlockSpec((D, tn), lambda j: (0, j)),
        out_shape=jax.ShapeDtypeStruct((D, NA), F32),
        compiler_params=_params(("arbitrary",)),
    )(c_all, dmod)


def _pack_rows(parts, n_rows, after=()):
    N = parts[0].shape[1]
    n = len(parts)

    def body(*refs):
        o_ref = refs[-1]
        o_ref[...] = jnp.zeros_like(o_ref)
        at = 0
        for r in refs[:n]:
            o_ref[at:at + r.shape[0], :] = r[...]
            at += r.shape[0]

    vmem = pl.BlockSpec(memory_space=pltpu.VMEM)
    return pl.pallas_call(body, name="pack_small", out_shape=jax.ShapeDtypeStruct((n_rows, N), F32),
                          in_specs=[vmem] * n + [ANY] * len(after), out_specs=vmem,
                          compiler_params=_params())(*parts, *after)


def _sum8(parts):
    _, R, N = parts.shape

    def body(p_ref, o_ref):
        acc = p_ref[0]
        for d in range(1, 8):
            acc = acc + p_ref[d]
        o_ref[...] = acc

    return pl.pallas_call(body, name="sum8", out_shape=jax.ShapeDtypeStruct((R, N), F32),
                          compiler_params=_params())(parts)


def _adam_math(w, g, m, v):
    m = ADAM_B1 * m + (1.0 - ADAM_B1) * g
    v = ADAM_B2 * v + (1.0 - ADAM_B2) * (g * g)
    delta = -ADAM_LR * ((m / ADAM_C1) / (jnp.sqrt(v / ADAM_C2) + ADAM_EPS) + ADAM_WD * w)
    return delta, m, v


def _adam(name, w, m, v, g, carry=None):
    R, C = w.shape
    tm = _row_tile(R, C * 4, 1 << 20)
    steps = R // tm
    n_ci = len(carry.ins) if carry else 0
    n_co = len(carry.outs) if carry else 0

    def body(*refs):
        w_ref, m_ref, v_ref, g_ref = refs[:4]
        d_ref, nm_ref, nv_ref = refs[4 + n_ci:7 + n_ci]
        c_ins, c_outs, c_sems = refs[4:4 + n_ci], refs[7 + n_ci:7 + n_ci + n_co], refs[7 + n_ci + n_co:]
        if carry:
            @pl.when(pl.program_id(0) == 0)
            def _():
                carry.start(c_ins, c_outs, c_sems)

        delta, nm, nv = _adam_math(w_ref[...], g_ref[...], m_ref[...], v_ref[...])
        d_ref[...] = delta
        nm_ref[...] = nm
        nv_ref[...] = nv
        if carry:
            @pl.when(pl.program_id(0) == steps - 1)
            def _():
                carry.finish(c_ins, c_outs, c_sems)

    spec = pl.BlockSpec((tm, C), lambda i: (i, 0))
    res = pl.pallas_call(
        body, name=name, grid=(steps,), in_specs=[spec] * 4 + [ANY] * n_ci, out_specs=[spec] * 3 + [ANY] * n_co,
        out_shape=[jax.ShapeDtypeStruct((R, C), F32)] * 3 + (carry.outs if carry else []),
        scratch_shapes=carry.sems if carry else [],
        input_output_aliases=carry.io_aliases(4, 3) if carry else {},
        compiler_params=_params(("arbitrary",)),
    )(w, m, v, g, *(carry.ins if carry else []))
    return (res[:3], res[3:]) if carry else res


def _adam_halves(name, w, m, v, mine, other, core, carry=None):
    R, C = w.shape
    Rh = mine.shape[0]
    tc = max(t for t in range(128, C + 1, 128) if C % t == 0 and R * t <= (3 << 17))
    steps = C // tc
    n_ci = len(carry.ins) if carry else 0
    n_co = len(carry.outs) if carry else 0

    def body(*refs):
        c_ref, w_ref, m_ref, v_ref, a_ref, b_ref = refs[:6]
        g_ref, d_ref, nm_ref, nv_ref = refs[6 + n_ci:10 + n_ci]
        c_ins, c_outs, c_sems = refs[6:6 + n_ci], refs[10 + n_ci:10 + n_ci + n_co], refs[10 + n_ci + n_co:]
        if carry:
            @pl.when(pl.program_id(0) == 0)
            def _():
                carry.start(c_ins, c_outs, c_sems)

        first = c_ref[0] == 0
        g = jnp.concatenate([jnp.where(first, a_ref[...], b_ref[...]),
                             jnp.where(first, b_ref[0:R - Rh, :], a_ref[0:R - Rh, :])], axis=0)
        delta, nm, nv = _adam_math(w_ref[...], g, m_ref[...], v_ref[...])
        g_ref[...] = g
        d_ref[...] = delta
        nm_ref[...] = nm
        nv_ref[...] = nv
        if carry:
            @pl.when(pl.program_id(0) == steps - 1)
            def _():
                carry.finish(c_ins, c_outs, c_sems)

    spec = pl.BlockSpec((R, tc), lambda i, c_ref: (0, i))
    h_spec = pl.BlockSpec((Rh, tc), lambda i, c_ref: (0, i))
    res = pl.pallas_call(
        body, name=name, out_shape=[jax.ShapeDtypeStruct((R, C), F32)] * 4 + (carry.outs if carry else []),
        grid_spec=pltpu.PrefetchScalarGridSpec(
            num_scalar_prefetch=1, grid=(steps,), in_specs=[spec, spec, spec, h_spec, h_spec] + [ANY] * n_ci,
            out_specs=[spec] * 4 + [ANY] * n_co, scratch_shapes=carry.sems if carry else []),
        input_output_aliases=carry.io_aliases(6, 4) if carry else {},
        compiler_params=_params(("arbitrary",)),
    )(core, w, m, v, mine, other, *(carry.ins if carry else []))
    return (res[:4], res[4:]) if carry else res


def _adam_small(name, w, m, v, g):
    def body(w_ref, m_ref, v_ref, g_ref, d_ref, nm_ref, nv_ref):
        delta, nm, nv = _adam_math(w_ref[...], g_ref[...], m_ref[...], v_ref[...])
        d_ref[...] = delta
        nm_ref[...] = nm
        nv_ref[...] = nv

    return pl.pallas_call(body, name=name, out_shape=[jax.ShapeDtypeStruct(w.shape, F32)] * 3,
                          compiler_params=_params())(w, m, v, g)


def _place():
    return lax.axis_index("x"), lax.axis_index("y"), lax.axis_index("c")


def _other_chips(x, y):
    return [(1 - x, y), (x, 1 - y), (1 - x, 1 - y)]


def _all_gather8(blk, name):
    R, N = blk.shape

    def body(x_ref, out_ref, send_sems, recv_sems, local_sem):
        x, y, c = _place()
        me = 4 * x + 2 * y + c
        mine = pltpu.make_async_copy(x_ref, out_ref.at[me], local_sem)
        mine.start()
        flips = [(j >> 2 & 1, j >> 1 & 1, j & 1) for j in range(1, 8)]
        peers = [((1 - x) if fx else x, (1 - y) if fy else y, (1 - c) if fc else c) for fx, fy, fc in flips]
        sends = []
        for j, peer in enumerate(peers):
            cp = pltpu.make_async_remote_copy(src_ref=x_ref, dst_ref=out_ref.at[me], send_sem=send_sems.at[j],
                                              recv_sem=recv_sems.at[j], device_id=peer, device_id_type=MESH)
            cp.start()
            sends.append(cp)
        for j, (px, py, pc) in enumerate(peers):
            pltpu.make_async_remote_copy(src_ref=x_ref, dst_ref=out_ref.at[4 * px + 2 * py + pc],
                                         send_sem=send_sems.at[j], recv_sem=recv_sems.at[j],
                                         device_id=(px, py, pc), device_id_type=MESH).wait_recv()
        for cp in sends:
            cp.wait_send()
        mine.wait()

    return pl.pallas_call(
        body, name=name, out_shape=jax.ShapeDtypeStruct((8, R, N), F32),
        in_specs=[pl.BlockSpec(memory_space=pltpu.VMEM)], out_specs=pl.BlockSpec(memory_space=pltpu.VMEM),
        scratch_shapes=[pltpu.SemaphoreType.DMA((7,)), pltpu.SemaphoreType.DMA((7,)), pltpu.SemaphoreType.DMA],
        compiler_params=_params(),
    )(blk)


def _piece(rows, piece):
    i, n, k = piece if len(piece) == 3 else (piece[0], piece[1], 1)
    assert rows % 16 == 0 and rows // 16 >= n, (rows, piece)
    lo, hi = (rows // 16 * i // n) * 16, (rows // 16 * (i + k) // n) * 16
    return pl.ds(lo, hi - lo)


def _gather_plan(shards, piece=(0, 1), into=None, ici=True):
    n = len(shards)

    def parts(ins, outs, sems):
        s1, r1, s2, r2, loc = sems
        x, y, c = _place()
        me = 2 * x + y
        chips = _other_chips(x, y)
        sib = (x, y, 1 - c)

        def rows(k):
            return _piece(shards[k].shape[1], piece)

        def ici_copy(k, j, slab, to):
            return pltpu.make_async_remote_copy(src_ref=ins[k].at[c, rows(k)], dst_ref=outs[k].at[slab, c, rows(k)],
                                                send_sem=s1.at[3 * k + j], recv_sem=r1.at[3 * k + j],
                                                device_id=to, device_id_type=MESH)

        def d2d(k, j, slab, half):
            return pltpu.make_async_remote_copy(src_ref=outs[k].at[slab, half, rows(k)],
                                                dst_ref=outs[k].at[slab, half, rows(k)],
                                                send_sem=s2.at[3 * k + j], recv_sem=r2.at[3 * k + j],
                                                device_id=sib, device_id_type=MESH)

        def own(k):
            return pltpu.make_async_remote_copy(src_ref=ins[k].at[:, rows(k)], dst_ref=outs[k].at[me, :, rows(k)],
                                                send_sem=loc.at[2 * k], recv_sem=loc.at[2 * k + 1],
                                                device_id=sib, device_id_type=MESH)

        return c, me, chips, ici_copy, d2d, own

    def start(ins, outs, sems):
        c, me, chips, ici_copy, d2d, own = parts(ins, outs, sems)
        for k in range(n):
            for j, (px, py) in enumerate(chips):
                (ici_copy(k, j, me, (px, py, c)) if ici else d2d(k, j, 2 * px + py, c)).start()
        for k in range(n):
            own(k).start()

    def finish(ins, outs, sems):
        c, me, chips, ici_copy, d2d, own = parts(ins, outs, sems)
        if ici:
            for k in range(n):
                for j, (px, py) in enumerate(chips):
                    ici_copy(k, j, 2 * px + py, (px, py, c)).wait_recv()
                    d2d(k, j, 2 * px + py, c).start()
        for k in range(n):
            for j, (px, py) in enumerate(chips):
                d2d(k, j, 2 * px + py, 1 - c).wait_recv()
        for k in range(n):
            own(k).wait()
            for j, (px, py) in enumerate(chips):
                if ici:
                    ici_copy(k, j, me, (px, py, c)).wait_send()
                d2d(k, j, 2 * px + py, c).wait_send()

    return _Plan(list(shards) + list(into or []), [jax.ShapeDtypeStruct((4,) + a.shape, a.dtype) for a in shards],
                 [pltpu.SemaphoreType.DMA((3 * n,))] * 4 + [pltpu.SemaphoreType.DMA((2 * n,))], start, finish,
                 aliases={n + k: k for k in range(n)} if into else None)


def _pair_plan(parts):
    n = len(parts)

    def copies(ins, outs, sems):
        send_sems, recv_sems = sems
        x, y, c = _place()
        return [pltpu.make_async_remote_copy(src_ref=ins[k].at[p, 1 - c], dst_ref=outs[k].at[p],
                                             send_sem=send_sems.at[4 * k + p], recv_sem=recv_sems.at[4 * k + p],
                                             device_id=(x, y, 1 - c), device_id_type=MESH)
                for k in range(n) for p in range(4)]

    def start(ins, outs, sems):
        for cp in copies(ins, outs, sems):
            cp.start()

    def finish(ins, outs, sems):
        for cp in copies(ins, outs, sems):
            cp.wait()

    return _Plan(parts, [jax.ShapeDtypeStruct((4,) + a.shape[2:], a.dtype) for a in parts],
                 [pltpu.SemaphoreType.DMA((4 * n,))] * 2, start, finish)


def _sibling_plan(arrs):
    n = len(arrs)

    def copies(ins, outs, sems):
        send_sems, recv_sems = sems
        x, y, c = _place()
        return [pltpu.make_async_remote_copy(src_ref=ins[k], dst_ref=outs[k], send_sem=send_sems.at[k],
                                             recv_sem=recv_sems.at[k], device_id=(x, y, 1 - c), device_id_type=MESH)
                for k in range(n)]

    def start(ins, outs, sems):
        for cp in copies(ins, outs, sems):
            cp.start()

    def finish(ins, outs, sems):
        for cp in copies(ins, outs, sems):
            cp.wait()

    return _Plan(arrs, [jax.ShapeDtypeStruct(a.shape, a.dtype) for a in arrs],
                 [pltpu.SemaphoreType.DMA((n,))] * 2, start, finish)


def _scatter_copies(arrs):
    def copies(ins, land, send_sems, recv_sems):
        x, y, c = _place()
        return [pltpu.make_async_remote_copy(src_ref=ins[k].at[2 * px + py], dst_ref=land[k].at[j],
                                             send_sem=send_sems.at[3 * k + j], recv_sem=recv_sems.at[3 * k + j],
                                             device_id=(px, py, c), device_id_type=MESH)
                for k in range(len(arrs)) for j, (px, py) in enumerate(_other_chips(x, y))]

    return copies, [lax.empty((3,) + a.shape[1:], a.dtype) for a in arrs]


def _gather_copies(shards):
    def copies(ins, land, send_sems, recv_sems):
        x, y, c = _place()
        return [pltpu.make_async_remote_copy(src_ref=ins[k].at[c], dst_ref=land[k].at[2 * x + y, c],
                                             send_sem=send_sems.at[3 * k + j], recv_sem=recv_sems.at[3 * k + j],
                                             device_id=(px, py, c), device_id_type=MESH)
                for k in range(len(shards)) for j, (px, py) in enumerate(_other_chips(x, y))]

    return copies, [lax.empty((4,) + a.shape, a.dtype) for a in shards]


def _split_start(arrs, copies_lands, ride, name, after=()):
    copies, lands = copies_lands
    n = len(arrs)
    rides = list(ride) if isinstance(ride, (list, tuple)) else [ride]
    n_thru = 2 * n + len(rides)

    def body(*refs):
        first_out = n_thru + len(after)
        for cp in copies(refs[:n], refs[n:2 * n], refs[first_out], refs[first_out + 1]):
            cp.start()

    hbm = [pltpu.with_memory_space_constraint(a, pltpu.HBM) for a in list(arrs) + lands + rides]
    res = pl.pallas_call(
        body, name=name,
        out_shape=[pltpu.SemaphoreType.DMA((3 * n,)), pltpu.SemaphoreType.DMA((3 * n,))]
        + [pltpu.HBM(a.shape, a.dtype) for a in hbm],
        in_specs=[HBM_SPEC] * n_thru + [ANY] * len(after),
        out_specs=[SEM_SPEC, SEM_SPEC] + [HBM_SPEC] * n_thru,
        input_output_aliases={i: 2 + i for i in range(n_thru)},
        compiler_params=pltpu.CompilerParams(has_side_effects=pltpu.SideEffectType.DATAFLOW_SIDE_EFFECTING),
    )(*hbm, *after)
    return res[0], res[1], res[2:2 + n], res[2 + n:2 + 2 * n], list(res[2 + 2 * n:])


def _split_wait(started, copies_lands, after, name):
    send_sems, recv_sems, arrs, lands, _ = started
    copies = copies_lands[0]
    n = len(arrs)

    def body(*refs):
        for cp in copies(refs[:n], refs[n:2 * n], refs[2 * n], refs[2 * n + 1]):
            cp.wait_send()
            cp.wait_recv()

    res = pl.pallas_call(
        body, name=name, out_shape=[pltpu.HBM(a.shape, a.dtype) for a in list(arrs) + list(lands)],
        in_specs=[HBM_SPEC] * (2 * n) + [SEM_SPEC, SEM_SPEC] + [ANY] * len(after), out_specs=[HBM_SPEC] * (2 * n),
        input_output_aliases={i: i for i in range(2 * n)},
        compiler_params=pltpu.CompilerParams(has_side_effects=pltpu.SideEffectType.DATAFLOW_SIDE_EFFECTING),
    )(*arrs, *lands, send_sems, recv_sems, *after)
    return list(res[:n]), list(res[n:])


def _add_pair(parts, sib, core, name):
    P4, _, Rh, C = parts.shape
    tm, tc = _tile2(Rh, C, 16)

    def body(c_ref, a_ref, b_ref, o_ref):
        o_ref[...] = (a_ref[0].astype(F32) + b_ref[...].astype(F32)).astype(BF16)

    spec = pl.BlockSpec((1, tm, tc), lambda p, i, j, c_ref: (p, i, j))
    return pl.pallas_call(
        body, name=name, out_shape=jax.ShapeDtypeStruct((P4, Rh, C), BF16),
        grid_spec=pltpu.PrefetchScalarGridSpec(
            num_scalar_prefetch=1, grid=(P4, Rh // tm, C // tc),
            in_specs=[pl.BlockSpec((1, 1, tm, tc), lambda p, i, j, c_ref: (p, c_ref[0], i, j)), spec], out_specs=spec),
        compiler_params=_params(("parallel",) * 3),
    )(core, parts, sib)


def _sum_slabs(pre, recv, chip, name):
    _, Rh, C = pre.shape
    tm, tc = _tile2(Rh, C, 16)

    def body(me_ref, own_ref, r_ref, o_ref):
        acc = own_ref[0].astype(F32)
        for j in range(3):
            acc = acc + r_ref[j].astype(F32)
        o_ref[...] = acc

    return pl.pallas_call(
        body, name=name, out_shape=jax.ShapeDtypeStruct((Rh, C), F32),
        grid_spec=pltpu.PrefetchScalarGridSpec(
            num_scalar_prefetch=1, grid=(Rh // tm, C // tc),
            in_specs=[pl.BlockSpec((1, tm, tc), lambda i, j, me_ref: (me_ref[0], i, j)),
                      pl.BlockSpec((3, tm, tc), lambda i, j, me_ref: (0, i, j))],
            out_specs=pl.BlockSpec((tm, tc), lambda i, j, me_ref: (i, j))),
        compiler_params=_params(("parallel", "parallel")),
    )(chip, pre, recv)


def kernel(x, c, positions, w_ada, b_ada, w_in, g_q_a, w_q_b, g_kv_a, w_kv_b, w_o_a, w_conv, w_o_b, w_o, ln1_g, ln1_b, w_ffn_in, w_ffn_out, ln2_g, ln2_b, loss_target, m_w_ada, m_b_ada, m_w_in, m_g_q_a, m_w_q_b, m_g_kv_a, m_w_kv_b, m_w_o_a, m_w_conv, m_w_o_b, m_w_o, m_ln1_g, m_ln1_b, m_w_ffn_in, m_w_ffn_out, m_ln2_g, m_ln2_b, v_w_ada, v_b_ada, v_w_in, v_g_q_a, v_w_q_b, v_g_kv_a, v_w_kv_b, v_w_o_a, v_w_conv, v_w_o_b, v_w_o, v_ln1_g, v_ln1_b, v_w_ffn_in, v_w_ffn_out, v_ln2_g, v_ln2_b):
    S, D = x.shape[1], x.shape[2]
    F = w_ffn_out.shape[1] * 4
    ax, ay, ac = _place()
    chip = 2 * ax + ay
    dev = 4 * ax + 2 * ay + ac
    x2, tgt = x[0], loss_target[0]
    w_ada2, w_in2, w_q_b2, w_kv_b2 = w_ada[0], w_in[0], w_q_b[0], w_kv_b[0]
    w_o_a2, w_o_b2, w_o2, w_ffn_in2, w_ffn_out2 = w_o_a[0], w_o_b[0], w_o[0], w_ffn_in[0], w_ffn_out[0]
    NA = w_ada2.shape[1]
    CW = w_conv.shape[2]

    inv_freq = 1.0 / (ROPE_THETA ** (jnp.arange(0, QK_ROPE, 2, dtype=F32) / QK_ROPE))
    ang = positions[0].astype(F32)[:, None] * inv_freq
    cos, sin = jnp.cos(ang), jnp.sin(ang)
    z32, z64, z96 = jnp.zeros((S, 32), F32), jnp.zeros((S, 64), F32), jnp.zeros((S, 96), F32)
    tab = jnp.concatenate([cos, cos, z64, -sin, z96, z32, sin, z64], axis=1)

    def halves(a):
        return a.reshape(2, a.shape[0] // 2, a.shape[1])

    def whole(g):
        return g.reshape(4, 2 * g.shape[2], g.shape[3])

    def cols(g):
        return jnp.transpose(g, (1, 0, 2)).reshape(g.shape[1], 4 * g.shape[2])

    w_inT, m_w_inT, v_w_inT = w_in2.T, m_w_in[0].T, v_w_in[0].T
    CS = w_inT.shape[0]
    CSP = -(-CS // 32) * 32
    sh_in = halves(jnp.pad(w_inT.astype(BF16), ((0, CSP - CS), (0, 0))))
    c_all = _all_gather8(c, "gather_c").reshape(8, D)
    wconv_all = _all_gather8(w_conv[0], "gather_wconv")
    w_conv_full = jnp.transpose(wconv_all[0::2], (1, 0, 2)).reshape(3, D)
    b_sh = lax.dynamic_slice(b_ada, (0, chip * NA), (1, NA))
    mod_sh = _ada_fwd(c_all, w_ada2, b_sh)
    mod_all = _all_gather8(mod_sh, "gather_mod")
    mod = lax.dynamic_slice(mod_all[0::2], (0, dev, 0), (4, 1, NA)).reshape(6, D)
    shift1, scale1, gate1, shift2, scale2, gate2 = (mod[k:k + 1] for k in range(6))

    cl_in = _gather_copies([sh_in])
    st_in = _split_start([sh_in], cl_in, [shift1, w_conv_full], "gather_in_start")
    shift1, w_conv_full = st_in[4]
    others = lax.optimization_barrier((w_q_b2, w_kv_b2, w_o_a2, w_o_b2, w_o2, w_ffn_in2, w_ffn_out2, shift1))
    sh_qb, sh_kvb, sh_oa, sh_ob, sh_o, sh_fi, sh_fo = (halves(w.astype(BF16)) for w in others[:7])
    shift1 = others[7]
    sh_in_t, l_in = _split_wait(st_in, cl_in, [sh_qb, sh_kvb, sh_oa, sh_ob, sh_o, sh_fi, sh_fo], "gather_in_wait")
    sh_a1, sh_a2 = [sh_qb, sh_kvb], [sh_oa, sh_ob, sh_o]
    cl_a1, cl_a2, cl_fi, cl_fo = (_gather_copies(g) for g in (sh_a1, sh_a2, [sh_fi], [sh_fo]))
    st_a1 = _split_start(sh_a1, cl_a1, shift1, "gather_a1_start", after=[l_in[0]])
    st_a2 = _split_start(sh_a2, cl_a2, st_a1[4], "gather_a2_start")
    u, (g_in,) = _modulate(x2, scale1, st_a2[4][0], "modulate1", carry=_gather_plan(sh_in_t, into=l_in, ici=False))
    g_in = whole(g_in)

    def in_rows(lo, hi):
        parts = [g_in[p, max(lo, p * CS) - p * CS:min(hi, (p + 1) * CS) - p * CS]
                 for p in range(4) if max(lo, p * CS) < min(hi, (p + 1) * CS)]
        return parts[0] if len(parts) == 1 else jnp.concatenate(parts, axis=0)

    n_qkv = Q_LORA + KV_LORA + QK_ROPE
    W_qkvT = jnp.pad(in_rows(0, n_qkv), ((0, QKV_A - n_qkv), (0, 0)))
    W_convT = in_rows(n_qkv, n_qkv + 3 * D)
    W_gateT = in_rows(n_qkv + 3 * D, n_qkv + 5 * D)

    pq = _matmul(u, W_qkvT, "nt", F32, "proj_qkv")
    pc = _matmul(u, W_convT, "nt", BF16, "proj_conv")
    sh_a1, la1 = _split_wait(st_a1, cl_a1, [pc], "gather_a1_wait")
    pg, (g_qb, g_kvb) = _matmul(u, W_gateT, "nt", BF16, "proj_gate", carry=_gather_plan(sh_a1, into=la1, ici=False))
    st_fi = _split_start([sh_fi], cl_fi, g_q_a, "gather_fi_start", after=[pg])
    W_qb = jnp.pad(cols(whole(g_qb)).reshape(Q_LORA, N_HEADS, QK_NOPE + QK_ROPE),
                   ((0, 0), (0, 0), (0, QK_PAD - QK_NOPE - QK_ROPE))).reshape(Q_LORA, N_HEADS * QK_PAD)
    W_kvb = cols(whole(g_kvb))
    rq, rkv, kr = _rms_fwd(pq, tab, st_fi[4][0], g_kv_a)
    kv = _matmul(rkv, W_kvb, "nn", BF16, "kv_b")
    sh_a2, la2 = _split_wait(st_a2, cl_a2, [kv], "gather_a2_wait")
    def rope_heads(r, t):
        return jnp.concatenate([r[:, lo:lo + 128] if lo % QK_PAD == 0 else _rope(r[:, lo:lo + 128], t, 1)
                                for lo in range(0, r.shape[1], 128)], axis=1)

    q, (g_oa, g_ob, g_o) = _matmul(rq, W_qb, "nn", BF16, "q_b", carry=_gather_plan(sh_a2, into=la2, ici=False),
                                   finish=(rope_heads, tab))
    o, lse = _attn_fwd(q, kv, kr)
    W_oa, W_ob, W_o = (g.reshape(-1, D) for g in (g_oa, g_ob, g_o))
    hb = _conv_fwd(pc, w_conv_full)
    st_fo = _split_start([sh_fo], cl_fo, ln1_g, "gather_fo_start", after=[o])
    y_b = _matmul(hb, W_ob, "nn", BF16, "o_b")
    y_a = _matmul(o, W_oa, "nn", BF16, "o_a")
    merged = _merge_fwd(y_a, y_b, pg)
    sh_fi_t, lfi = _split_wait(st_fi, cl_fi, [merged], "gather_fi_wait")
    mix, g_fi = _matmul(merged, W_o, "nn", F32, "w_o", carry=_gather_plan(sh_fi_t, (0, 2), into=lfi, ici=False))
    (x1, u2), (g_fi,) = _ln1_fwd(x2, mix, gate1, st_fo[4][0], ln1_b, scale2, shift2,
                                 carry=_gather_plan(sh_fi_t, (1, 2), into=g_fi, ici=False))
    W_fi = whole(g_fi)
    hh = _matmul(u2, W_fi, "nn", BF16, "ffn_in", shards="b")
    sh_fo_t, lfo = _split_wait(st_fo, cl_fo, [hh], "gather_fo_wait")
    act, (g_fo,) = _swiglu_fwd(hh, carry=_gather_plan(sh_fo_t, into=lfo, ici=False))
    W_fo = g_fo.reshape(F, D)
    ffn = _matmul(act, W_fo, "nn", F32, "ffn_out")

    core_i = ac.astype(jnp.int32).reshape(1)
    chip_i = chip.astype(jnp.int32).reshape(1)

    def uncols(g):
        return jnp.transpose(g.reshape(g.shape[0], 4, g.shape[1] // 4), (1, 0, 2))

    def slabs(p):
        return p.reshape(4, 2, p.shape[1] // 2, p.shape[2])

    def add_pairs(parts, sibs, nms):
        return [_add_pair(a, b, core_i, "add_pair_" + nm) for a, b, nm in zip(parts, sibs, nms)]

    def sum_all(pre, recv, nms):
        return [_sum_slabs(a, r, chip_i, "sum_slabs_" + nm) for a, r, nm in zip(pre, recv, nms)]

    dffn, dx1a, loss_acc, d_ln2_g, d_ln2_b, d_gate2 = _ln2_loss_bwd(x1, ffn, gate2, ln2_g, ln2_b, tgt)
    dW_fo = _matmul(act, dffn, "tn", BF16, "d_w_ffn_out")
    p_fo = [slabs(dW_fo.reshape(4, -1, D))]
    dact, s_fo = _matmul(dffn, W_fo, "nt", BF16, "d_act", carry=_pair_plan(p_fo))
    pre_fo = add_pairs(p_fo, s_fo, ["w_ffn_out"])
    cs_fo = _scatter_copies(pre_fo)
    st_sfo = _split_start(pre_fo, cs_fo, scale2, "scatter_fo_start")
    dhh = _swiglu_bwd(dact, hh)
    dW_fi = _matmul(u2, dhh, "tn", BF16, "d_w_ffn_in", shards="o")
    p_fi = [slabs(dW_fi)]
    du2, s_fi = _matmul(dhh, W_fi, "nt", F32, "d_u2", carry=_pair_plan(p_fi), shards="b")
    pre_fi = add_pairs(p_fi, s_fi, ["w_ffn_in"])
    cs_fi = _scatter_copies(pre_fi)
    st_sfi = _split_start(pre_fi, cs_fi, st_sfo[4], "scatter_fi_start")
    dmix, dxa, d_shift2, d_scale2, d_ln1_g, d_ln1_b, d_gate1 = _ln1_bwd(x2, mix, dx1a, du2, gate1, ln1_g, ln1_b, st_sfi[4][0])
    dW_o = _matmul(merged, dmix, "tn", BF16, "d_w_o")
    dmerged = _matmul(dmix, W_o, "nt", BF16, "d_merged")
    dy_a, dy_b, dgate = _merge_bwd(dmerged, y_a, y_b, pg)
    dW_oa = _matmul(o, dy_a, "tn", BF16, "d_w_o_a")
    do = _matmul(dy_a, W_oa, "nt", BF16, "d_o")
    dW_ob = _matmul(hb, dy_b, "tn", BF16, "d_w_o_b")
    p_mid = [slabs(g.reshape(4, -1, D)) for g in (dW_oa, dW_ob, dW_o)]
    dhb, s_mid = _matmul(dy_b, W_ob, "nt", BF16, "d_hb", carry=_pair_plan(p_mid))
    pre_mid = add_pairs(p_mid, s_mid, ["w_o_a", "w_o_b", "w_o"])
    cs_mid = _scatter_copies(pre_mid)
    st_smid = _split_start(pre_mid, cs_mid, w_conv_full, "scatter_mid_start")
    dconv, d_wconv = _conv_bwd(dhb, pc, st_smid[4][0])
    dq, dkv, dkr, _ = _attn_bwd(q, kv, kr, do, o, lse, tab, carry=_token_plan(st_smid[4][0]))
    names_a = ["w_ffn_out", "w_ffn_in", "w_o_a", "w_o_b", "w_o"]
    dW_qb = _matmul(rq, dq, "tn", BF16, "d_w_q_b")
    d_rq = _matmul(dq, W_qb, "nt", F32, "d_rq")
    dW_kvb = _matmul(rkv, dkv, "tn", BF16, "d_w_kv_b")
    d_rkv = _matmul(dkv, W_kvb, "nt", F32, "d_rkv")
    dqkv, d_g_q, d_g_kv = _rms_bwd(d_rq, d_rkv, pq, dkr, g_q_a, g_kv_a)
    dW_qkvT = _matmul(dqkv, u, "tn", BF16, "d_w_qkv")
    dW_convT = _matmul(dconv, u, "tn", BF16, "d_w_conv")
    dW_gateT = _matmul(dgate, u, "tn", BF16, "d_w_gate")
    pre_fo, r_fo = _split_wait(st_sfo, cs_fo, [dW_qkvT], "scatter_fo_wait")
    pre_fi, r_fi = _split_wait(st_sfi, cs_fi, [dW_qkvT], "scatter_fi_wait")
    pre_mid, r_mid = _split_wait(st_smid, cs_mid, [dW_qkvT], "scatter_mid_wait")
    fin_a = sum_all(pre_fo + pre_fi + pre_mid, r_fo + r_fi + r_mid, names_a)
    srcs = [(0, dW_qkvT[:n_qkv]), (n_qkv, dW_convT), (n_qkv + 3 * D, dW_gateT)]
    rows_of = []
    for p in range(4):
        for lo, src in srcs:
            a, b = max(lo, p * CS), min(lo + src.shape[0], (p + 1) * CS)
            if a < b:
                rows_of.append(src[a - lo:b - lo])
        rows_of.append(jnp.zeros((CSP - CS, D), BF16))
    dW_inT = jnp.concatenate(rows_of, axis=0).reshape(4, CSP, D)
    dW_qb_u = dW_qb.reshape(Q_LORA, N_HEADS, QK_PAD)[:, :, :QK_NOPE + QK_ROPE].reshape(Q_LORA, -1)
    names_b = ["w_in", "w_q_b", "w_kv_b"]
    p_b = [slabs(dW_inT), slabs(uncols(dW_qb_u)), slabs(uncols(dW_kvb))]
    du, s_b = _matmul(dqkv, W_qkvT, "nn", F32, "d_u_qkv", carry=_pair_plan(p_b))
    pre_b = add_pairs(p_b, s_b, names_b)
    cs_b = _scatter_copies(pre_b)
    st_b = _split_start(pre_b, cs_b, scale1, "scatter_last_start")
    du, fs_a = _matmul(dconv, W_convT, "nn", F32, "d_u_conv", add=du, carry=_sibling_plan(fin_a))
    du = _matmul(dgate, W_gateT, "nn", F32, "d_u_gate", add=du)
    grad_x, d_shift1, d_scale1 = _dx_final(dxa, du, x2, st_b[4][0])

    big = {}
    ws = dict(w_in=(w_inT, m_w_inT, v_w_inT), w_q_b=(w_q_b2, m_w_q_b[0], v_w_q_b[0]),
              w_kv_b=(w_kv_b2, m_w_kv_b[0], v_w_kv_b[0]), w_o_a=(w_o_a2, m_w_o_a[0], v_w_o_a[0]),
              w_o_b=(w_o_b2, m_w_o_b[0], v_w_o_b[0]), w_o=(w_o2, m_w_o[0], v_w_o[0]),
              w_ffn_in=(w_ffn_in2, m_w_ffn_in[0], v_w_ffn_in[0]), w_ffn_out=(w_ffn_out2, m_w_ffn_out[0], v_w_ffn_out[0]))

    def adam_of(nm, a, b, carry=None):
        w_, m_, v_ = ws[nm]
        return _adam_halves("adam_" + nm, w_, m_, v_, a, b, core_i, carry)

    for nm, a, b in zip(names_a, fin_a, fs_a):
        big[nm] = adam_of(nm, a, b, _token_plan(st_b[4][0]))[0]
    done = [big[nm][1] for nm in names_a] + [grad_x]
    pre_b, r_b = _split_wait(st_b, cs_b, done, "scatter_last_wait")
    fin_b = sum_all(pre_b, r_b, names_b)
    fs_b = _run_plan(_sibling_plan(fin_b), "sibling_last")
    for nm, a, b in zip(names_b, fin_b, fs_b):
        big[nm] = adam_of(nm, a, b)

    def pad_d(v):
        return jnp.pad(v, ((0, 0), (0, D - v.shape[1])))

    small = _pack_rows([d_ln1_g, d_ln1_b, d_ln2_g, d_ln2_b, pad_d(d_g_q), pad_d(d_g_kv), d_wconv,
                         d_shift1, d_scale1, d_gate1, d_shift2, d_scale2, d_gate2, pad_d(loss_acc)], 16, after=[pre_b[1]])
    small_all = _all_gather8(small, "gather_small")
    small_sum = _sum8(small_all)
    loss = small_sum[15, 0]
    g_ln1_g, g_ln1_b, g_ln2_g, g_ln2_b = (small_sum[k:k + 1] for k in range(4))
    g_g_q, g_g_kv = small_sum[4:5, :Q_LORA], small_sum[5:6, :KV_LORA]
    g_wconv = lax.dynamic_slice(small_sum[6:9], (0, chip * CW), (3, CW))
    g_b_ada = small_sum[9:15].reshape(1, 6 * D)
    dmod_all = small_all[:, 9:15, :].reshape(8, 6 * D)
    g_w_ada = _ada_bwd(c_all, lax.dynamic_slice(dmod_all, (0, chip * NA), (8, NA)))
    big["w_ada"] = [g_w_ada] + list(_adam("adam_w_ada", w_ada2, m_w_ada[0], v_w_ada[0], g_w_ada))
    sm = {}
    for nm, w_, m_, v_, g_ in [("b_ada", b_ada, m_b_ada, v_b_ada, g_b_ada), ("g_q_a", g_q_a, m_g_q_a, v_g_q_a, g_g_q),
                               ("g_kv_a", g_kv_a, m_g_kv_a, v_g_kv_a, g_g_kv),
                               ("w_conv", w_conv[0], m_w_conv[0], v_w_conv[0], g_wconv),
                               ("ln1_g", ln1_g, m_ln1_g, v_ln1_g, g_ln1_g), ("ln1_b", ln1_b, m_ln1_b, v_ln1_b, g_ln1_b),
                               ("ln2_g", ln2_g, m_ln2_g, v_ln2_g, g_ln2_g), ("ln2_b", ln2_b, m_ln2_b, v_ln2_b, g_ln2_b)]:
        sm[nm] = (g_,) + tuple(_adam_small("adam_" + nm, w_, m_, v_, g_))

    order = ["w_ada", "b_ada", "w_in", "g_q_a", "w_q_b", "g_kv_a", "w_kv_b", "w_o_a", "w_conv", "w_o_b", "w_o",
             "ln1_g", "ln1_b", "w_ffn_in", "w_ffn_out", "ln2_g", "ln2_b"]
    lead = {"b_ada", "g_q_a", "g_kv_a", "ln1_g", "ln1_b", "ln2_g", "ln2_b"}

    def leaf(nm, k):
        val = big[nm][k] if nm in big else sm[nm][k]
        if nm == "w_in":
            val = val.T
        return val if nm in lead else val[None]

    outs = [loss, grad_x[None]]
    for k in range(4):
        outs += [leaf(nm, k) for nm in order]
    return tuple(outs)
```

```python
import jax
import jax.numpy as jnp
from jax import lax
from jax.experimental import pallas as pl
from jax.experimental.pallas import tpu as pltpu

F32, BF16 = jnp.float32, jnp.bfloat16
N_HEADS, QK_NOPE, QK_ROPE, V_HEAD = 16, 128, 64, 128
Q_LORA, KV_LORA = 512, 512
QK_PAD = 256
QKV_A = 1152
CHUNK_SHIFT = 6
ATTN_SCALE = (QK_NOPE + QK_ROPE) ** -0.5
LOG2E = 1.4426950408889634
SCALE2 = ATTN_SCALE * LOG2E
ROPE_THETA = 10000.0
ALPHA = 2.0 ** 0.25
LN_EPS, RMS_EPS = 1e-5, 1e-6
ADAM_LR, ADAM_B1, ADAM_B2, ADAM_EPS, ADAM_WD, ADAM_STEP = 0.001, 0.9, 0.999, 1e-08, 0.01, 10
ADAM_C1 = 1.0 - ADAM_B1 ** ADAM_STEP
ADAM_C2 = 1.0 - ADAM_B2 ** ADAM_STEP
VMEM_LIMIT = 56 * 1024 * 1024
MESH = pl.DeviceIdType.MESH
ANY = pl.BlockSpec(memory_space=pl.ANY)
HBM_SPEC = pl.BlockSpec(memory_space=pltpu.HBM)
SEM_SPEC = pl.BlockSpec(memory_space=pltpu.SEMAPHORE)
NT = (((1,), (1,)), ((), ()))
TN = (((0,), (0,)), ((), ()))
NN = (((1,), (0,)), ((), ()))


def _params(sem=None):
    return pltpu.CompilerParams(dimension_semantics=sem, vmem_limit_bytes=VMEM_LIMIT)


def _pick(n, cands=(1408, 1024, 512, 384, 256, 128)):
    for t in cands:
        if n % t == 0:
            return t
    return n


def _row_tile(rows, row_bytes, budget, mult=8):
    best = mult
    for t in range(mult, rows + 1, mult):
        if rows % t == 0 and t * row_bytes <= budget:
            best = t
    return best


def _tile2(rows, cols, mult=8, budget=3 << 18):
    col_tiles = [t for t in range(128, cols + 1, 128) if cols % t == 0] or [cols]
    best = None
    for tc in col_tiles:
        for tr in range(mult, rows + 1, mult):
            if rows % tr == 0 and tr * tc <= budget and (best is None or (tr * tc, tc) > (best[0] * best[1], best[1])):
                best = (tr, tc)
    assert best is not None, (rows, cols)
    return best


def _sigmoid(x):
    return jax.nn.sigmoid(x)


class _Plan:
    def __init__(self, ins, outs, sems, start, finish, aliases=None):
        self.ins, self.outs, self.sems, self.start, self.finish = list(ins), list(outs), list(sems), start, finish
        self.aliases = dict(aliases or {})

    def io_aliases(self, first_in, first_out):
        return {first_in + i: first_out + o for i, o in self.aliases.items()}


def _token_plan(token):
    return _Plan([token], [], [], lambda *a: None, lambda *a: None)


def _run_plan(plan, name, ride=None):
    n_in, n_out = len(plan.ins), len(plan.outs)
    extra = [] if ride is None else list(ride)
    aliases = plan.io_aliases(0, 0)
    for k in range(len(extra)):
        aliases[n_in + k] = n_out + k

    def body(*refs):
        ins, outs, sems = refs[:n_in], refs[n_in + len(extra):n_in + len(extra) + n_out], refs[n_in + 2 * len(extra) + n_out:]
        plan.start(ins, outs, sems)
        plan.finish(ins, outs, sems)

    return pl.pallas_call(body, name=name, out_shape=plan.outs + [jax.ShapeDtypeStruct(r.shape, r.dtype) for r in extra],
                          in_specs=[ANY] * (n_in + len(extra)), out_specs=[ANY] * (n_out + len(extra)),
                          scratch_shapes=plan.sems, input_output_aliases=aliases,
                          compiler_params=_params())(*plan.ins, *extra)


def _matmul(a, b, mode, out_dtype, name, add=None, carry=None, shards=None, finish=None):
    if mode == "nn":
        (M, K), N, dims = a.shape, b.shape[-1] * (4 if shards else 1), NN
    elif mode == "nt":
        (M, K), N, dims = a.shape, b.shape[-2], NT
    else:
        (K, M), N, dims = a.shape, b.shape[1], TN
    split_n = shards and mode != "nt"
    tm = _pick(M)
    tn = _pick(N // 4) if split_n else _pick(N)
    deep = (2816, 2048, 1408, 1024, 512, 384, 256, 128)
    if shards and mode == "nt":
        tk = _pick(K // 4, deep)
    else:
        tk = K if K <= 2048 else _pick(K, deep)
    nk = K // tk
    per = (N // 4 // tn) if split_n else (K // 4 // tk if shards else 1)
    a_spec = (pl.BlockSpec((tk, tm), lambda i, j, k: (k, i)) if mode == "tn"
              else pl.BlockSpec((tm, tk), lambda i, j, k: (i, k)))
    if shards == "b" and mode == "nn":
        b_spec = pl.BlockSpec((None, tk, tn), lambda i, j, k: (j // per, k, j % per))
    elif shards == "b":
        b_spec = pl.BlockSpec((None, tn, tk), lambda i, j, k: (k // per, j, k % per))
    else:
        b_spec = (pl.BlockSpec((tn, tk), lambda i, j, k: (j, k)) if mode == "nt"
                  else pl.BlockSpec((tk, tn), lambda i, j, k: (k, j)))
    o_spec = pl.BlockSpec((tm, tn), lambda i, j, k: (i, j))
    o_shape = (M, N)
    if shards == "o":
        o_spec, o_shape = pl.BlockSpec((None, tm, tn), lambda i, j, k: (j // per, i, j % per)), (4, M, N // 4)
    has_add = add is not None
    has_fin = finish is not None
    n_ci = len(carry.ins) if carry else 0
    n_co = len(carry.outs) if carry else 0
    n_in = 2 + has_add + has_fin
    grid = (M // tm, N // tn, nk)

    def body(*refs):
        a_ref, b_ref = refs[0], refs[1]
        add_ref = refs[2] if has_add else None
        fin_ref = refs[2 + has_add] if has_fin else None

        def store(r):
            if has_add:
                r = r + add_ref[...]
            if has_fin:
                r = finish[0](r, fin_ref[...])
            o_ref[...] = r.astype(o_ref.dtype)

        o_ref = refs[n_in + n_ci]
        acc_ref = refs[n_in + n_ci + 1 + n_co] if nk > 1 else None
        c_ins = refs[n_in:n_in + n_ci]
        c_outs = refs[n_in + n_ci + 1:n_in + n_ci + 1 + n_co]
        c_sems = refs[n_in + n_ci + 1 + n_co + (nk > 1):]
        i, j, k = pl.program_id(0), pl.program_id(1), pl.program_id(2)

        if carry:
            @pl.when((i == 0) & (j == 0) & (k == 0))
            def _():
                carry.start(c_ins, c_outs, c_sems)

        part = lax.dot_general(a_ref[...], b_ref[...], dims, preferred_element_type=F32)
        if nk == 1:
            store(part)
        else:
            @pl.when(k == 0)
            def _():
                acc_ref[...] = part

            @pl.when((k > 0) & (k < nk - 1))
            def _():
                acc_ref[...] += part

            @pl.when(k == nk - 1)
            def _():
                store(acc_ref[...] + part)

        if carry:
            @pl.when((i == grid[0] - 1) & (j == grid[1] - 1) & (k == nk - 1))
            def _():
                carry.finish(c_ins, c_outs, c_sems)

    ins = [a, b] + ([add] if has_add else []) + ([finish[1]] if has_fin else []) + (carry.ins if carry else [])
    in_specs = ([a_spec, b_spec] + ([o_spec] if has_add else [])
                + ([pl.BlockSpec((tm, finish[1].shape[1]), lambda i, j, k: (i, 0))] if has_fin else []) + [ANY] * n_ci)
    res = pl.pallas_call(
        body, name=name, grid=grid,
        in_specs=in_specs, out_specs=[o_spec] + [ANY] * n_co,
        out_shape=[jax.ShapeDtypeStruct(o_shape, out_dtype)] + (carry.outs if carry else []),
        scratch_shapes=([pltpu.VMEM((tm, tn), F32)] if nk > 1 else []) + (carry.sems if carry else []),
        input_output_aliases=carry.io_aliases(n_in, 1) if carry else {},
        compiler_params=_params(("arbitrary",) * 3 if carry else ("parallel", "parallel", "arbitrary")),
    )(*ins)
    return (res[0], res[1:]) if carry else res[0]


def _rows(body, name, n_rows, tm, ins, outs, accs=(), carry=None):
    grid = (n_rows // tm,)

    def halo(arr):
        return 16 if arr.dtype == BF16 else 8

    arrays, in_specs = [], []
    for spec in ins:
        kind, arr = spec[0], spec[1]
        arrays.append(arr)
        if kind == "row":
            _, _, cb, w = spec
            in_specs.append(pl.BlockSpec((tm, w), lambda i, cb=cb: (i, cb)))
        elif kind == "full":
            in_specs.append(pl.BlockSpec(arr.shape, lambda i, nd=arr.ndim: (0,) * nd))
        elif kind == "prev":
            _, _, cb, w = spec
            h = halo(arr)
            in_specs.append(pl.BlockSpec((h, w), lambda i, cb=cb, per=tm // h: (jnp.maximum(i * per - 1, 0), cb)))
        else:
            _, _, cb, w = spec
            h = halo(arr)
            in_specs.append(pl.BlockSpec((h, w), lambda i, cb=cb, per=tm // h, last=n_rows // h - 1:
                                         (jnp.minimum((i + 1) * per, last), cb)))
    out_shape = [jax.ShapeDtypeStruct((n_rows, w), dt) for (w, dt) in outs]
    out_specs = [pl.BlockSpec((tm, w), lambda i: (i, 0)) for (w, _) in outs]
    out_shape += [jax.ShapeDtypeStruct(s, F32) for s in accs]
    out_specs += [pl.BlockSpec(s, lambda i, nd=len(s): (0,) * nd) for s in accs]
    n_in, n_out, n_acc = len(ins), len(outs), len(accs)
    n_ci = len(carry.ins) if carry else 0
    n_co = len(carry.outs) if carry else 0

    def kernel_body(*refs):
        first = n_in + n_ci
        c_ins, c_outs, c_sems = refs[n_in:first], refs[first + n_out + n_acc:first + n_out + n_acc + n_co], refs[first + n_out + n_acc + n_co:]
        if carry:
            @pl.when(pl.program_id(0) == 0)
            def _():
                carry.start(c_ins, c_outs, c_sems)

        body(pl.program_id(0), refs[:n_in], refs[first:first + n_out], refs[first + n_out:first + n_out + n_acc])
        if carry:
            @pl.when(pl.program_id(0) == grid[0] - 1)
            def _():
                carry.finish(c_ins, c_outs, c_sems)

    res = pl.pallas_call(
        kernel_body, name=name, grid=grid, in_specs=in_specs + [ANY] * n_ci, out_specs=out_specs + [ANY] * n_co,
        out_shape=out_shape + (carry.outs if carry else []), scratch_shapes=carry.sems if carry else [],
        input_output_aliases=carry.io_aliases(n_in, n_out + n_acc) if carry else {},
        compiler_params=_params(("arbitrary",)),
    )(*arrays, *(carry.ins if carry else []))
    return (res[:n_out + n_acc], res[n_out + n_acc:]) if carry else res


def _acc_add(i, ref, val):
    @pl.when(i == 0)
    def _():
        ref[...] = val

    @pl.when(i > 0)
    def _():
        ref[...] += val


def _rope(t, tab, sign):
    c, sa, sb = tab[:, 0:128], tab[:, 128:256], tab[:, 256:384]
    rot = pltpu.roll(t, 96, 1) * sa + pltpu.roll(t, 32, 1) * sb
    return t * c + rot if sign > 0 else t * c - rot


def _ln_stats(r):
    mu = jnp.mean(r, axis=-1, keepdims=True)
    d = r - mu
    var = jnp.mean(d * d, axis=-1, keepdims=True)
    rstd = lax.rsqrt(var + LN_EPS)
    return d * rstd, rstd


def _ln_bwd(dxh, xh, rstd):
    m1 = jnp.mean(dxh, axis=-1, keepdims=True)
    m2 = jnp.mean(dxh * xh, axis=-1, keepdims=True)
    return rstd * (dxh - m1 - xh * m2)


def _modulate(x, scale, shift, name, carry=None):
    S, D = x.shape

    def body(i, ins, outs, accs):
        outs[0][...] = (ins[0][...] * (1.0 + ins[1][...]) + ins[2][...]).astype(BF16)

    res = _rows(body, name, S, _pick(S, (256, 128)), [("row", x, 0, D), ("full", scale), ("full", shift)], [(D, BF16)],
                carry=carry)
    return (res[0][0], res[1]) if carry else res[0]


def _rms_fwd(pq, tab, g_q, g_kv):
    S = pq.shape[0]

    def body(i, ins, outs, accs):
        pq_ref, tab_ref, gq_ref, gkv_ref = ins

        def rms(x, g):
            return x * lax.rsqrt(jnp.mean(x * x, axis=-1, keepdims=True) + RMS_EPS) * g

        outs[0][...] = rms(pq_ref[:, 0:Q_LORA], gq_ref[...]).astype(BF16)
        outs[1][...] = rms(pq_ref[:, Q_LORA:Q_LORA + KV_LORA], gkv_ref[...]).astype(BF16)
        outs[2][...] = _rope(pq_ref[:, Q_LORA + KV_LORA:QKV_A], tab_ref[...], 1).astype(BF16)

    return _rows(body, "rms_fwd", S, _pick(S, (256, 128)),
                 [("row", pq, 0, QKV_A), ("row", tab, 0, 384), ("full", g_q), ("full", g_kv)],
                 [(Q_LORA, BF16), (KV_LORA, BF16), (128, BF16)])


def _allowed(q0, k0, bq):
    row = q0 + lax.broadcasted_iota(jnp.int32, (bq, bq), 0)
    col = k0 + lax.broadcasted_iota(jnp.int32, (bq, bq), 1)
    return (col >> CHUNK_SHIFT) <= (row >> CHUNK_SHIFT)


ATTN_BLOCK = 512


HEADS_PER_STEP = 2


def _attn_fwd(q, kv, kr):
    S = q.shape[0]
    bq = min(ATTN_BLOCK, S)
    nq = S // bq
    G = HEADS_PER_STEP

    def body(q_ref, kv_ref, kr_ref, o_ref, lse_ref, kcat):
        qi = pl.program_id(1)

        @pl.when(qi == 0)
        def _():
            for g in range(G):
                kcat[g, :, 0:128] = kv_ref[:, g * 256:g * 256 + 128]
                kcat[g, :, 128:256] = kr_ref[...]

        qs = [q_ref[:, g * QK_PAD:(g + 1) * QK_PAD] for g in range(G)]

        def step(j, carry, masked):
            off = pl.multiple_of(j * bq, bq)
            rows = pl.ds(off, bq)
            mask = _allowed(qi * bq, off, bq) if masked else None
            out = []
            for g in range(G):
                m, l, acc = carry[g]
                s = lax.dot_general(qs[g], kcat[g, rows, :], NT, preferred_element_type=F32) * SCALE2
                if masked:
                    s = jnp.where(mask, s, -1e30)
                m_new = jnp.maximum(m, jnp.max(s, axis=1, keepdims=True))
                a = jnp.exp2(m - m_new)
                p = jnp.exp2(s - m_new)
                l = a * l + jnp.sum(p, axis=1, keepdims=True)
                acc = a * acc + jnp.dot(p.astype(BF16), kv_ref[rows, g * 256 + 128:(g + 1) * 256],
                                        preferred_element_type=F32)
                out.append((m_new, l, acc))
            return tuple(out)

        init = tuple((jnp.full((bq, 1), -1e30, F32), jnp.zeros((bq, 1), F32), jnp.zeros((bq, V_HEAD), F32))
                     for _ in range(G))
        below = lax.fori_loop(0, qi, lambda j, cr: step(j, cr, False), init)
        for g, (m, l, acc) in enumerate(step(qi, below, True)):
            o_ref[:, g * V_HEAD:(g + 1) * V_HEAD] = (acc / l).astype(BF16)
            lse_ref[g] = m + jnp.log2(l)

    return pl.pallas_call(
        body, name="attn_fwd", grid=(N_HEADS // G, nq),
        in_specs=[pl.BlockSpec((bq, G * QK_PAD), lambda h, i: (i, h)),
                  pl.BlockSpec((S, G * 256), lambda h, i: (0, h)),
                  pl.BlockSpec((S, 128), lambda h, i: (0, 0))],
        out_specs=[pl.BlockSpec((bq, G * V_HEAD), lambda h, i: (i, h)),
                   pl.BlockSpec((G, bq, 1), lambda h, i: (h, i, 0))],
        out_shape=[jax.ShapeDtypeStruct((S, N_HEADS * V_HEAD), BF16),
                   jax.ShapeDtypeStruct((N_HEADS, S, 1), F32)],
        scratch_shapes=[pltpu.VMEM((G, S, QK_PAD), BF16)],
        compiler_params=_params(("arbitrary", "arbitrary")),
    )(q, kv, kr)


def _attn_bwd(q, kv, kr, do, o, lse, tab, carry=None):
    S = q.shape[0]
    bq = min(ATTN_BLOCK, S)
    nq = S // bq

    n_ci = len(carry.ins) if carry else 0
    n_co = len(carry.outs) if carry else 0

    def body(*refs):
        q_ref, kn_ref, v_ref, kr_ref, do_ref, o_ref, lse_ref, tab_ref = refs[:8]
        dq_ref, dkv_ref, dkr_ref = refs[8 + n_ci:11 + n_ci]
        dq_acc, dk_acc, dv_acc, kcat, delta = refs[11 + n_ci + n_co:16 + n_ci + n_co]
        c_ins, c_outs, c_sems = refs[8:8 + n_ci], refs[11 + n_ci:11 + n_ci + n_co], refs[16 + n_ci + n_co:]
        h = pl.program_id(0)
        if carry:
            @pl.when(h == 0)
            def _():
                carry.start(c_ins, c_outs, c_sems)

        dq_acc[...] = jnp.zeros_like(dq_acc)
        dk_acc[...] = jnp.zeros_like(dk_acc)
        dv_acc[...] = jnp.zeros_like(dv_acc)
        kcat[:, 0:128] = kn_ref[...]
        kcat[:, 128:256] = kr_ref[...]
        for r in range(nq):
            rows = slice(r * bq, (r + 1) * bq)
            delta[rows, :] = jnp.sum(do_ref[rows, :].astype(F32) * o_ref[rows, :].astype(F32), axis=1, keepdims=True)

        def pair(i, j, masked):
            rows_i = pl.ds(pl.multiple_of(i * bq, bq), bq)
            rows_j = pl.ds(pl.multiple_of(j * bq, bq), bq)
            qv, dov, k = q_ref[rows_i, :], do_ref[rows_i, :], kcat[rows_j, :]
            s = lax.dot_general(qv, k, NT, preferred_element_type=F32) * SCALE2
            if masked:
                s = jnp.where(_allowed(i * bq, j * bq, bq), s, -1e30)
            p = jnp.exp2(s - lse_ref[0, rows_i, :])
            dv_acc[rows_j, :] += lax.dot_general(p.astype(BF16), dov, TN, preferred_element_type=F32)
            dp = lax.dot_general(dov, v_ref[rows_j, :], NT, preferred_element_type=F32)
            ds = (p * (dp - delta[rows_i, :]) * ATTN_SCALE).astype(BF16)
            dk_acc[rows_j, :] += lax.dot_general(ds, qv, TN, preferred_element_type=F32)
            dq_acc[rows_i, :] += jnp.dot(ds, k, preferred_element_type=F32)

        def kv_step(j, _):
            pair(j, j, True)

            def q_step(i, _):
                pair(i, j, False)
                return 0

            lax.fori_loop(j + 1, nq, q_step, 0)
            return 0

        lax.fori_loop(0, nq, kv_step, 0)

        for r in range(nq):
            rows = slice(r * bq, (r + 1) * bq)
            dq_ref[rows, 0:128] = dq_acc[rows, 0:128].astype(BF16)
            dq_ref[rows, 128:256] = _rope(dq_acc[rows, 128:256], tab_ref[rows, :], -1).astype(BF16)
        dkv_ref[:, 0:128] = dk_acc[:, 0:128].astype(BF16)
        dkv_ref[:, 128:256] = dv_acc[...].astype(BF16)

        @pl.when(h == 0)
        def _():
            dkr_ref[...] = dk_acc[:, 128:256]

        @pl.when(h > 0)
        def _():
            dkr_ref[...] += dk_acc[:, 128:256]

        @pl.when(h == N_HEADS - 1)
        def _():
            for r in range(nq):
                rows = slice(r * bq, (r + 1) * bq)
                dkr_ref[rows, :] = _rope(dkr_ref[rows, :], tab_ref[rows, :], -1)
            if carry:
                carry.finish(c_ins, c_outs, c_sems)

    W = N_HEADS * QK_PAD
    res = pl.pallas_call(
        body, name="attn_bwd", grid=(N_HEADS,),
        in_specs=[pl.BlockSpec((S, QK_PAD), lambda h: (0, h)),
                  pl.BlockSpec((S, 128), lambda h: (0, 2 * h)),
                  pl.BlockSpec((S, 128), lambda h: (0, 2 * h + 1)),
                  pl.BlockSpec((S, 128), lambda h: (0, 0)),
                  pl.BlockSpec((S, V_HEAD), lambda h: (0, h)),
                  pl.BlockSpec((S, V_HEAD), lambda h: (0, h)),
                  pl.BlockSpec((1, S, 1), lambda h: (h, 0, 0)),
                  pl.BlockSpec((S, 384), lambda h: (0, 0))] + [ANY] * n_ci,
        out_specs=[pl.BlockSpec((S, QK_PAD), lambda h: (0, h)),
                   pl.BlockSpec((S, QK_PAD), lambda h: (0, h)),
                   pl.BlockSpec((S, 128), lambda h: (0, 0))] + [ANY] * n_co,
        out_shape=[jax.ShapeDtypeStruct((S, W), BF16), jax.ShapeDtypeStruct((S, W), BF16),
                   jax.ShapeDtypeStruct((S, 128), F32)] + (carry.outs if carry else []),
        scratch_shapes=[pltpu.VMEM((S, QK_PAD), F32), pltpu.VMEM((S, QK_PAD), F32), pltpu.VMEM((S, V_HEAD), F32),
                        pltpu.VMEM((S, QK_PAD), BF16), pltpu.VMEM((S, 1), F32)]
        + (carry.sems if carry else []),
        input_output_aliases=carry.io_aliases(8, 3) if carry else {},
        compiler_params=_params(("arbitrary",)),
    )(q, kv, kv, kr, do, o, lse, tab, *(carry.ins if carry else []))
    return res[0], res[1], res[2], res[3:]


def _shift_down(cur, prev, i, n):
    tm, h = cur.shape[0], prev.shape[0]
    prev = jnp.where(i == 0, jnp.zeros_like(prev), prev)
    full = jnp.concatenate([prev, cur], axis=0)
    return pltpu.roll(full, n, 0)[h:h + tm, :]


def _shift_up(cur, nxt, i, last, n):
    tm, h = cur.shape[0], nxt.shape[0]
    nxt = jnp.where(i == last, jnp.zeros_like(nxt), nxt)
    full = jnp.concatenate([cur, nxt], axis=0)
    return pltpu.roll(full, tm + h - n, 0)[0:tm, :]


def _conv_fwd(pc, w_conv):
    S, D = pc.shape[0], pc.shape[1] // 3
    tm = _pick(S, (256, 128))

    def body(i, ins, outs, accs):
        b_ref, c_ref, x_ref, cp_ref, xp_ref, w_ref = ins
        z = c_ref[...].astype(F32) * x_ref[...].astype(F32)
        zp = cp_ref[...].astype(F32) * xp_ref[...].astype(F32)
        cz = w_ref[0:1, :] * _shift_down(z, zp, i, 2) + w_ref[1:2, :] * _shift_down(z, zp, i, 1) + w_ref[2:3, :] * z
        outs[0][...] = (b_ref[...].astype(F32) * cz).astype(BF16)

    return _rows(body, "conv_fwd", S, tm,
                 [("row", pc, 0, D), ("row", pc, 1, D), ("row", pc, 2, D), ("prev", pc, 1, D), ("prev", pc, 2, D),
                  ("full", w_conv)], [(D, BF16)])[0]


def _conv_bwd(dhb, pc, w_conv):
    S, D = dhb.shape
    tm = _pick(S, (256, 128))
    last = S // tm - 1

    def body(i, ins, outs, accs):
        g_ref, b_ref, c_ref, x_ref, cp_ref, xp_ref, gn_ref, bn_ref, w_ref = ins
        w0, w1, w2 = w_ref[0:1, :], w_ref[1:2, :], w_ref[2:3, :]
        c, x, g = c_ref[...].astype(F32), x_ref[...].astype(F32), g_ref[...].astype(F32)
        z = c * x
        zp = cp_ref[...].astype(F32) * xp_ref[...].astype(F32)
        z1, z2 = _shift_down(z, zp, i, 1), _shift_down(z, zp, i, 2)
        cz = w0 * z2 + w1 * z1 + w2 * z
        dcz = g * b_ref[...].astype(F32)
        dczn = gn_ref[...].astype(F32) * bn_ref[...].astype(F32)
        dz = w2 * dcz + w1 * _shift_up(dcz, dczn, i, last, 1) + w0 * _shift_up(dcz, dczn, i, last, 2)
        outs[0][:, 0:D] = (g * cz).astype(BF16)
        outs[0][:, D:2 * D] = (dz * x).astype(BF16)
        outs[0][:, 2 * D:3 * D] = (dz * c).astype(BF16)
        dw = jnp.concatenate([jnp.sum(dcz * z2, axis=0, keepdims=True), jnp.sum(dcz * z1, axis=0, keepdims=True),
                              jnp.sum(dcz * z, axis=0, keepdims=True)], axis=0)
        _acc_add(i, accs[0], dw)

    return _rows(body, "conv_bwd", S, tm,
                 [("row", dhb, 0, D), ("row", pc, 0, D), ("row", pc, 1, D), ("row", pc, 2, D),
                  ("prev", pc, 1, D), ("prev", pc, 2, D), ("next", dhb, 0, D), ("next", pc, 0, D), ("full", w_conv)],
                 [(3 * D, BF16)], [(3, D)])


def _merge_fwd(y_a, y_b, pg):
    S, D = y_a.shape

    def body(i, ins, outs, accs):
        ya, yb, ga, gb = ins
        outs[0][...] = (_sigmoid(ga[...].astype(F32)) * ya[...].astype(F32)
                        + _sigmoid(gb[...].astype(F32)) * yb[...].astype(F32)).astype(BF16)

    return _rows(body, "merge_fwd", S, _pick(S, (256, 128)),
                 [("row", y_a, 0, D), ("row", y_b, 0, D), ("row", pg, 0, D), ("row", pg, 1, D)], [(D, BF16)])[0]


def _merge_bwd(dm, y_a, y_b, pg):
    S, D = dm.shape

    def body(i, ins, outs, accs):
        d, ya, yb = ins[0][...].astype(F32), ins[1][...].astype(F32), ins[2][...].astype(F32)
        sa, sb = _sigmoid(ins[3][...].astype(F32)), _sigmoid(ins[4][...].astype(F32))
        outs[0][...] = (d * sa).astype(BF16)
        outs[1][...] = (d * sb).astype(BF16)
        outs[2][:, 0:D] = (d * ya * (sa * (1.0 - sa))).astype(BF16)
        outs[2][:, D:2 * D] = (d * yb * (sb * (1.0 - sb))).astype(BF16)

    return _rows(body, "merge_bwd", S, _pick(S, (256, 128)),
                 [("row", dm, 0, D), ("row", y_a, 0, D), ("row", y_b, 0, D), ("row", pg, 0, D), ("row", pg, 1, D)],
                 [(D, BF16), (D, BF16), (2 * D, BF16)])


def _ln1_fwd(x, mix, gate1, g, b, scale2, shift2, carry=None):
    S, D = x.shape

    def body(i, ins, outs, accs):
        x_ref, mix_ref, gate_ref, g_ref, b_ref, sc_ref, sh_ref = ins
        xh, _ = _ln_stats(ALPHA * x_ref[...] + gate_ref[...] * mix_ref[...])
        x1 = xh * g_ref[...] + b_ref[...]
        outs[0][...] = x1
        outs[1][...] = (x1 * (1.0 + sc_ref[...]) + sh_ref[...]).astype(BF16)

    return _rows(body, "ln1_fwd", S, _pick(S, (256, 128)),
                 [("row", x, 0, D), ("row", mix, 0, D), ("full", gate1), ("full", g), ("full", b),
                  ("full", scale2), ("full", shift2)], [(D, F32), (D, BF16)], carry=carry)


def _swiglu_fwd(hh, carry=None):
    S, F = hh.shape[0], hh.shape[1] // 2

    def body(i, ins, outs, accs):
        hg = ins[0][...].astype(F32)
        outs[0][...] = (hg * _sigmoid(hg) * ins[1][...].astype(F32)).astype(BF16)

    res = _rows(body, "swiglu_fwd", S, _pick(S, (128,)), [("row", hh, 0, F), ("row", hh, 1, F)], [(F, BF16)], carry=carry)
    return (res[0][0], res[1]) if carry else res[0]


def _swiglu_bwd(dact, hh):
    S, F = dact.shape

    def body(i, ins, outs, accs):
        d, hg, hu = ins[0][...].astype(F32), ins[1][...].astype(F32), ins[2][...].astype(F32)
        sg = _sigmoid(hg)
        outs[0][:, 0:F] = (d * hu * (sg * (1.0 + hg * (1.0 - sg)))).astype(BF16)
        outs[0][:, F:2 * F] = (d * (hg * sg)).astype(BF16)

    return _rows(body, "swiglu_bwd", S, _pick(S, (128,)),
                 [("row", dact, 0, F), ("row", hh, 0, F), ("row", hh, 1, F)], [(2 * F, BF16)])[0]


def _ln2_loss_bwd(x1, ffn, gate2, g, b, target):
    S, D = x1.shape

    def body(i, ins, outs, accs):
        x1_ref, f_ref, gate_ref, g_ref, b_ref, t_ref = ins
        f = f_ref[...]
        xh, rstd = _ln_stats(ALPHA * x1_ref[...] + gate_ref[...] * f)
        e = xh * g_ref[...] + b_ref[...] - t_ref[...]
        dy = e * (1.0 / D)
        dr = _ln_bwd(dy * g_ref[...], xh, rstd)
        outs[0][...] = (gate_ref[...] * dr).astype(BF16)
        outs[1][...] = ALPHA * dr
        _acc_add(i, accs[0], jnp.full((1, 128), (0.5 / D) * jnp.sum(e * e), F32))
        _acc_add(i, accs[1], jnp.sum(dy * xh, axis=0, keepdims=True))
        _acc_add(i, accs[2], jnp.sum(dy, axis=0, keepdims=True))
        _acc_add(i, accs[3], jnp.sum(dr * f, axis=0, keepdims=True))

    return _rows(body, "ln2_loss_bwd", S, _pick(S, (256, 128)),
                 [("row", x1, 0, D), ("row", ffn, 0, D), ("full", gate2), ("full", g), ("full", b), ("row", target, 0, D)],
                 [(D, BF16), (D, F32)], [(1, 128), (1, D), (1, D), (1, D)])


def _ln1_bwd(x, mix, dx1a, du2, gate1, g, b, scale2):
    S, D = x.shape

    def body(i, ins, outs, accs):
        x_ref, mix_ref, da_ref, du_ref, gate_ref, g_ref, b_ref, sc_ref = ins
        mix, du = mix_ref[...], du_ref[...]
        xh, rstd = _ln_stats(ALPHA * x_ref[...] + gate_ref[...] * mix)
        x1 = xh * g_ref[...] + b_ref[...]
        dx1 = da_ref[...] + du * (1.0 + sc_ref[...])
        dr = _ln_bwd(dx1 * g_ref[...], xh, rstd)
        outs[0][...] = (gate_ref[...] * dr).astype(BF16)
        outs[1][...] = ALPHA * dr
        _acc_add(i, accs[0], jnp.sum(du, axis=0, keepdims=True))
        _acc_add(i, accs[1], jnp.sum(du * x1, axis=0, keepdims=True))
        _acc_add(i, accs[2], jnp.sum(dx1 * xh, axis=0, keepdims=True))
        _acc_add(i, accs[3], jnp.sum(dx1, axis=0, keepdims=True))
        _acc_add(i, accs[4], jnp.sum(dr * mix, axis=0, keepdims=True))

    return _rows(body, "ln1_bwd", S, _pick(S, (256, 128)),
                 [("row", x, 0, D), ("row", mix, 0, D), ("row", dx1a, 0, D), ("row", du2, 0, D),
                  ("full", gate1), ("full", g), ("full", b), ("full", scale2)],
                 [(D, BF16), (D, F32)], [(1, D)] * 5)


def _rms_bwd(d_rq, d_rkv, pq, dkr, g_q, g_kv):
    S = pq.shape[0]

    def body(i, ins, outs, accs):
        dq_ref, dkv_ref, pq_ref, dkr_ref, gq_ref, gkv_ref = ins

        def rms_bwd(dy, x, g):
            r = lax.rsqrt(jnp.mean(x * x, axis=-1, keepdims=True) + RMS_EPS)
            dyg = dy * g
            dx = r * dyg - x * (r * r * r) * jnp.mean(dyg * x, axis=-1, keepdims=True)
            return dx, jnp.sum(dy * (x * r), axis=0, keepdims=True)

        dxq, dgq = rms_bwd(dq_ref[...], pq_ref[:, 0:Q_LORA], gq_ref[...])
        dxkv, dgkv = rms_bwd(dkv_ref[...], pq_ref[:, Q_LORA:Q_LORA + KV_LORA], gkv_ref[...])
        outs[0][:, 0:Q_LORA] = dxq.astype(BF16)
        outs[0][:, Q_LORA:Q_LORA + KV_LORA] = dxkv.astype(BF16)
        outs[0][:, Q_LORA + KV_LORA:QKV_A] = dkr_ref[...].astype(BF16)
        _acc_add(i, accs[0], dgq)
        _acc_add(i, accs[1], dgkv)

    return _rows(body, "rms_bwd", S, _pick(S, (256, 128)),
                 [("row", d_rq, 0, Q_LORA), ("row", d_rkv, 0, KV_LORA), ("row", pq, 0, QKV_A), ("row", dkr, 0, 128),
                  ("full", g_q), ("full", g_kv)], [(QKV_A, BF16)], [(1, Q_LORA), (1, KV_LORA)])


def _dx_final(dxa, du, x, scale1):
    S, D = x.shape

    def body(i, ins, outs, accs):
        du = ins[1][...]
        outs[0][...] = ins[0][...] + du * (1.0 + ins[3][...])
        _acc_add(i, accs[0], jnp.sum(du, axis=0, keepdims=True))
        _acc_add(i, accs[1], jnp.sum(du * ins[2][...], axis=0, keepdims=True))

    return _rows(body, "dx_final", S, _pick(S, (256, 128)),
                 [("row", dxa, 0, D), ("row", du, 0, D), ("row", x, 0, D), ("full", scale1)],
                 [(D, F32)], [(1, D), (1, D)])


def _ada_fwd(c_all, w, bias):
    B, D = c_all.shape
    NA = w.shape[1]
    tn = _pick(NA, (512, 256, 128))

    def body(c_ref, w_ref, b_ref, o_ref):
        cv = c_ref[...]
        ca = (cv * _sigmoid(cv)).astype(BF16)
        o_ref[...] = jnp.dot(ca, w_ref[...].astype(BF16), preferred_element_type=F32) + b_ref[...]

    return pl.pallas_call(
        body, name="ada_fwd", grid=(NA // tn,),
        in_specs=[pl.BlockSpec((B, D), lambda j: (0, 0)), pl.BlockSpec((D, tn), lambda j: (0, j)),
                  pl.BlockSpec((1, tn), lambda j: (0, j))],
        out_specs=pl.BlockSpec((B, tn), lambda j: (0, j)),
        out_shape=jax.ShapeDtypeStruct((B, NA), F32),
        compiler_params=_params(("arbitrary",)),
    )(c_all, w, bias)


def _ada_bwd(c_all, dmod):
    B, D = c_all.shape
    NA = dmod.shape[1]
    tn = _pick(NA, (512, 256, 128))

    def body(c_ref, d_ref, o_ref):
        cv = c_ref[...]
        ca = (cv * _sigmoid(cv)).astype(BF16)
        o_ref[...] = lax.dot_general(ca, d_ref[...].astype(BF16), TN, preferred_element_type=F32)

    return pl.pallas_call(
        body, name="ada_bwd", grid=(NA // tn,),
        in_specs=[pl.BlockSpec((B, D), lambda j: (0, 0)), pl.BlockSpec((B, tn), lambda j: (0, j))],
        out_specs=pl.BlockSpec((D, tn), lambda j: (0, j)),
        out_shape=jax.ShapeDtypeStruct((D, NA), F32),
        compiler_params=_params(("arbitrary",)),
    )(c_all, dmod)


def _pack_rows(parts, n_rows, after=()):
    N = parts[0].shape[1]
    n = len(parts)

    def body(*refs):
        o_ref = refs[-1]
        o_ref[...] = jnp.zeros_like(o_ref)
        at = 0
        for r in refs[:n]:
            o_ref[at:at + r.shape[0], :] = r[...]
            at += r.shape[0]

    vmem = pl.BlockSpec(memory_space=pltpu.VMEM)
    return pl.pallas_call(body, name="pack_small", out_shape=jax.ShapeDtypeStruct((n_rows, N), F32),
                          in_specs=[vmem] * n + [ANY] * len(after), out_specs=vmem,
                          compiler_params=_params())(*parts, *after)


def _sum8(parts):
    _, R, N = parts.shape

    def body(p_ref, o_ref):
        acc = p_ref[0]
        for d in range(1, 8):
            acc = acc + p_ref[d]
        o_ref[...] = acc

    return pl.pallas_call(body, name="sum8", out_shape=jax.ShapeDtypeStruct((R, N), F32),
                          compiler_params=_params())(parts)


def _adam_math(w, g, m, v):
    m = ADAM_B1 * m + (1.0 - ADAM_B1) * g
    v = ADAM_B2 * v + (1.0 - ADAM_B2) * (g * g)
    delta = -ADAM_LR * ((m / ADAM_C1) / (jnp.sqrt(v / ADAM_C2) + ADAM_EPS) + ADAM_WD * w)
    return delta, m, v


def _adam(name, w, m, v, g, carry=None):
    R, C = w.shape
    tm = _row_tile(R, C * 4, 1 << 20)
    steps = R // tm
    n_ci = len(carry.ins) if carry else 0
    n_co = len(carry.outs) if carry else 0

    def body(*refs):
        w_ref, m_ref, v_ref, g_ref = refs[:4]
        d_ref, nm_ref, nv_ref = refs[4 + n_ci:7 + n_ci]
        c_ins, c_outs, c_sems = refs[4:4 + n_ci], refs[7 + n_ci:7 + n_ci + n_co], refs[7 + n_ci + n_co:]
        if carry:
            @pl.when(pl.program_id(0) == 0)
            def _():
                carry.start(c_ins, c_outs, c_sems)

        delta, nm, nv = _adam_math(w_ref[...], g_ref[...], m_ref[...], v_ref[...])
        d_ref[...] = delta
        nm_ref[...] = nm
        nv_ref[...] = nv
        if carry:
            @pl.when(pl.program_id(0) == steps - 1)
            def _():
                carry.finish(c_ins, c_outs, c_sems)

    spec = pl.BlockSpec((tm, C), lambda i: (i, 0))
    res = pl.pallas_call(
        body, name=name, grid=(steps,), in_specs=[spec] * 4 + [ANY] * n_ci, out_specs=[spec] * 3 + [ANY] * n_co,
        out_shape=[jax.ShapeDtypeStruct((R, C), F32)] * 3 + (carry.outs if carry else []),
        scratch_shapes=carry.sems if carry else [],
        input_output_aliases=carry.io_aliases(4, 3) if carry else {},
        compiler_params=_params(("arbitrary",)),
    )(w, m, v, g, *(carry.ins if carry else []))
    return (res[:3], res[3:]) if carry else res


def _adam_halves(name, w, m, v, mine, other, core, carry=None):
    R, C = w.shape
    Rh = mine.shape[0]
    tc = max(t for t in range(128, C + 1, 128) if C % t == 0 and R * t <= (3 << 17))
    steps = C // tc
    n_ci = len(carry.ins) if carry else 0
    n_co = len(carry.outs) if carry else 0

    def body(*refs):
        c_ref, w_ref, m_ref, v_ref, a_ref, b_ref = refs[:6]
        g_ref, d_ref, nm_ref, nv_ref = refs[6 + n_ci:10 + n_ci]
        c_ins, c_outs, c_sems = refs[6:6 + n_ci], refs[10 + n_ci:10 + n_ci + n_co], refs[10 + n_ci + n_co:]
        if carry:
            @pl.when(pl.program_id(0) == 0)
            def _():
                carry.start(c_ins, c_outs, c_sems)

        first = c_ref[0] == 0
        g = jnp.concatenate([jnp.where(first, a_ref[...], b_ref[...]),
                             jnp.where(first, b_ref[0:R - Rh, :], a_ref[0:R - Rh, :])], axis=0)
        delta, nm, nv = _adam_math(w_ref[...], g, m_ref[...], v_ref[...])
        g_ref[...] = g
        d_ref[...] = delta
        nm_ref[...] = nm
        nv_ref[...] = nv
        if carry:
            @pl.when(pl.program_id(0) == steps - 1)
            def _():
                carry.finish(c_ins, c_outs, c_sems)

    spec = pl.BlockSpec((R, tc), lambda i, c_ref: (0, i))
    h_spec = pl.BlockSpec((Rh, tc), lambda i, c_ref: (0, i))
    res = pl.pallas_call(
        body, name=name, out_shape=[jax.ShapeDtypeStruct((R, C), F32)] * 4 + (carry.outs if carry else []),
        grid_spec=pltpu.PrefetchScalarGridSpec(
            num_scalar_prefetch=1, grid=(steps,), in_specs=[spec, spec, spec, h_spec, h_spec] + [ANY] * n_ci,
            out_specs=[spec] * 4 + [ANY] * n_co, scratch_shapes=carry.sems if carry else []),
        input_output_aliases=carry.io_aliases(6, 4) if carry else {},
        compiler_params=_params(("arbitrary",)),
    )(core, w, m, v, mine, other, *(carry.ins if carry else []))
    return (res[:4], res[4:]) if carry else res


def _adam_small(name, w, m, v, g):
    def body(w_ref, m_ref, v_ref, g_ref, d_ref, nm_ref, nv_ref):
        delta, nm, nv = _adam_math(w_ref[...], g_ref[...], m_ref[...], v_ref[...])
        d_ref[...] = delta
        nm_ref[...] = nm
        nv_ref[...] = nv

    return pl.pallas_call(body, name=name, out_shape=[jax.ShapeDtypeStruct(w.shape, F32)] * 3,
                          compiler_params=_params())(w, m, v, g)


def _place():
    return lax.axis_index("x"), lax.axis_index("y"), lax.axis_index("c")


def _other_chips(x, y):
    return [(1 - x, y), (x, 1 - y), (1 - x, 1 - y)]


def _all_gather8(blk, name):
    R, N = blk.shape

    def body(x_ref, out_ref, send_sems, recv_sems, local_sem):
        x, y, c = _place()
        me = 4 * x + 2 * y + c
        mine = pltpu.make_async_copy(x_ref, out_ref.at[me], local_sem)
        mine.start()
        flips = [(j >> 2 & 1, j >> 1 & 1, j & 1) for j in range(1, 8)]
        peers = [((1 - x) if fx else x, (1 - y) if fy else y, (1 - c) if fc else c) for fx, fy, fc in flips]
        sends = []
        for j, peer in enumerate(peers):
            cp = pltpu.make_async_remote_copy(src_ref=x_ref, dst_ref=out_ref.at[me], send_sem=send_sems.at[j],
                                              recv_sem=recv_sems.at[j], device_id=peer, device_id_type=MESH)
            cp.start()
            sends.append(cp)
        for j, (px, py, pc) in enumerate(peers):
            pltpu.make_async_remote_copy(src_ref=x_ref, dst_ref=out_ref.at[4 * px + 2 * py + pc],
                                         send_sem=send_sems.at[j], recv_sem=recv_sems.at[j],
                                         device_id=(px, py, pc), device_id_type=MESH).wait_recv()
        for cp in sends:
            cp.wait_send()
        mine.wait()

    return pl.pallas_call(
        body, name=name, out_shape=jax.ShapeDtypeStruct((8, R, N), F32),
        in_specs=[pl.BlockSpec(memory_space=pltpu.VMEM)], out_specs=pl.BlockSpec(memory_space=pltpu.VMEM),
        scratch_shapes=[pltpu.SemaphoreType.DMA((7,)), pltpu.SemaphoreType.DMA((7,)), pltpu.SemaphoreType.DMA],
        compiler_params=_params(),
    )(blk)


def _piece(rows, piece):
    i, n, k = piece if len(piece) == 3 else (piece[0], piece[1], 1)
    assert rows % 16 == 0 and rows // 16 >= n, (rows, piece)
    lo, hi = (rows // 16 * i // n) * 16, (rows // 16 * (i + k) // n) * 16
    return pl.ds(lo, hi - lo)


def _gather_plan(shards, piece=(0, 1), into=None, ici=True):
    n = len(shards)

    def parts(ins, outs, sems):
        s1, r1, s2, r2, loc = sems
        x, y, c = _place()
        me = 2 * x + y
        chips = _other_chips(x, y)
        sib = (x, y, 1 - c)

        def rows(k):
            return _piece(shards[k].shape[1], piece)

        def ici_copy(k, j, slab, to):
            return pltpu.make_async_remote_copy(src_ref=ins[k].at[c, rows(k)], dst_ref=outs[k].at[slab, c, rows(k)],
                                                send_sem=s1.at[3 * k + j], recv_sem=r1.at[3 * k + j],
                                                device_id=to, device_id_type=MESH)

        def d2d(k, j, slab, half):
            return pltpu.make_async_remote_copy(src_ref=outs[k].at[slab, half, rows(k)],
                                                dst_ref=outs[k].at[slab, half, rows(k)],
                                                send_sem=s2.at[3 * k + j], recv_sem=r2.at[3 * k + j],
                                                device_id=sib, device_id_type=MESH)

        def own(k):
            return pltpu.make_async_remote_copy(src_ref=ins[k].at[:, rows(k)], dst_ref=outs[k].at[me, :, rows(k)],
                                                send_sem=loc.at[2 * k], recv_sem=loc.at[2 * k + 1],
                                                device_id=sib, device_id_type=MESH)

        return c, me, chips, ici_copy, d2d, own

    def start(ins, outs, sems):
        c, me, chips, ici_copy, d2d, own = parts(ins, outs, sems)
        for k in range(n):
            for j, (px, py) in enumerate(chips):
                (ici_copy(k, j, me, (px, py, c)) if ici else d2d(k, j, 2 * px + py, c)).start()
        for k in range(n):
            own(k).start()

    def finish(ins, outs, sems):
        c, me, chips, ici_copy, d2d, own = parts(ins, outs, sems)
        if ici:
            for k in range(n):
                for j, (px, py) in enumerate(chips):
                    ici_copy(k, j, 2 * px + py, (px, py, c)).wait_recv()
                    d2d(k, j, 2 * px + py, c).start()
        for k in range(n):
            for j, (px, py) in enumerate(chips):
                d2d(k, j, 2 * px + py, 1 - c).wait_recv()
        for k in range(n):
            own(k).wait()
            for j, (px, py) in enumerate(chips):
                if ici:
                    ici_copy(k, j, me, (px, py, c)).wait_send()
                d2d(k, j, 2 * px + py, c).wait_send()

    return _Plan(list(shards) + list(into or []), [jax.ShapeDtypeStruct((4,) + a.shape, a.dtype) for a in shards],
                 [pltpu.SemaphoreType.DMA((3 * n,))] * 4 + [pltpu.SemaphoreType.DMA((2 * n,))], start, finish,
                 aliases={n + k: k for k in range(n)} if into else None)


def _pair_plan(parts):
    n = len(parts)

    def copies(ins, outs, sems):
        send_sems, recv_sems = sems
        x, y, c = _place()
        return [pltpu.make_async_remote_copy(src_ref=ins[k].at[p, 1 - c], dst_ref=outs[k].at[p],
                                             send_sem=send_sems.at[4 * k + p], recv_sem=recv_sems.at[4 * k + p],
                                             device_id=(x, y, 1 - c), device_id_type=MESH)
                for k in range(n) for p in range(4)]

    def start(ins, outs, sems):
        for cp in copies(ins, outs, sems):
            cp.start()

    def finish(ins, outs, sems):
        for cp in copies(ins, outs, sems):
            cp.wait()

    return _Plan(parts, [jax.ShapeDtypeStruct((4,) + a.shape[2:], a.dtype) for a in parts],
                 [pltpu.SemaphoreType.DMA((4 * n,))] * 2, start, finish)


def _sibling_plan(arrs):
    n = len(arrs)

    def copies(ins, outs, sems):
        send_sems, recv_sems = sems
        x, y, c = _place()
        return [pltpu.make_async_remote_copy(src_ref=ins[k], dst_ref=outs[k], send_sem=send_sems.at[k],
                                             recv_sem=recv_sems.at[k], device_id=(x, y, 1 - c), device_id_type=MESH)
                for k in range(n)]

    def start(ins, outs, sems):
        for cp in copies(ins, outs, sems):
            cp.start()

    def finish(ins, outs, sems):
        for cp in copies(ins, outs, sems):
            cp.wait()

    return _Plan(arrs, [jax.ShapeDtypeStruct(a.shape, a.dtype) for a in arrs],
                 [pltpu.SemaphoreType.DMA((n,))] * 2, start, finish)


def _scatter_copies(arrs):
    def copies(ins, land, send_sems, recv_sems):
        x, y, c = _place()
        return [pltpu.make_async_remote_copy(src_ref=ins[k].at[2 * px + py], dst_ref=land[k].at[j],
                                             send_sem=send_sems.at[3 * k + j], recv_sem=recv_sems.at[3 * k + j],
                                             device_id=(px, py, c), device_id_type=MESH)
                for k in range(len(arrs)) for j, (px, py) in enumerate(_other_chips(x, y))]

    return copies, [lax.empty((3,) + a.shape[1:], a.dtype) for a in arrs]


def _gather_copies(shards, piece=(0, 1), lands=None):
    def copies(ins, land, send_sems, recv_sems):
        x, y, c = _place()
        return [pltpu.make_async_remote_copy(
                    src_ref=ins[k].at[c, _piece(shards[k].shape[1], piece)],
                    dst_ref=land[k].at[2 * x + y, c, _piece(shards[k].shape[1], piece)],
                    send_sem=send_sems.at[3 * k + j], recv_sem=recv_sems.at[3 * k + j],
                    device_id=(px, py, c), device_id_type=MESH)
                for k in range(len(shards)) for j, (px, py) in enumerate(_other_chips(x, y))]

    return copies, list(lands) if lands is not None else [lax.empty((4,) + a.shape, a.dtype) for a in shards]


def _split_start(arrs, copies_lands, ride, name, after=()):
    copies, lands = copies_lands
    n = len(arrs)
    rides = list(ride) if isinstance(ride, (list, tuple)) else [ride]
    n_thru = 2 * n + len(rides)

    def body(*refs):
        first_out = n_thru + len(after)
        for cp in copies(refs[:n], refs[n:2 * n], refs[first_out], refs[first_out + 1]):
            cp.start()

    hbm = [pltpu.with_memory_space_constraint(a, pltpu.HBM) for a in list(arrs) + lands + rides]
    res = pl.pallas_call(
        body, name=name,
        out_shape=[pltpu.SemaphoreType.DMA((3 * n,)), pltpu.SemaphoreType.DMA((3 * n,))]
        + [pltpu.HBM(a.shape, a.dtype) for a in hbm],
        in_specs=[HBM_SPEC] * n_thru + [ANY] * len(after),
        out_specs=[SEM_SPEC, SEM_SPEC] + [HBM_SPEC] * n_thru,
        input_output_aliases={i: 2 + i for i in range(n_thru)},
        compiler_params=pltpu.CompilerParams(has_side_effects=pltpu.SideEffectType.DATAFLOW_SIDE_EFFECTING),
    )(*hbm, *after)
    return res[0], res[1], res[2:2 + n], res[2 + n:2 + 2 * n], list(res[2 + 2 * n:])


def _split_wait(started, copies_lands, after, name):
    send_sems, recv_sems, arrs, lands, _ = started
    copies = copies_lands[0]
    n = len(arrs)

    def body(*refs):
        for cp in copies(refs[:n], refs[n:2 * n], refs[2 * n], refs[2 * n + 1]):
            cp.wait_send()
            cp.wait_recv()

    res = pl.pallas_call(
        body, name=name, out_shape=[pltpu.HBM(a.shape, a.dtype) for a in list(arrs) + list(lands)],
        in_specs=[HBM_SPEC] * (2 * n) + [SEM_SPEC, SEM_SPEC] + [ANY] * len(after), out_specs=[HBM_SPEC] * (2 * n),
        input_output_aliases={i: i for i in range(2 * n)},
        compiler_params=pltpu.CompilerParams(has_side_effects=pltpu.SideEffectType.DATAFLOW_SIDE_EFFECTING),
    )(*arrs, *lands, send_sems, recv_sems, *after)
    return list(res[:n]), list(res[n:])


def _add_pair(parts, sib, core, name):
    P4, _, Rh, C = parts.shape
    tm, tc = _tile2(Rh, C, 16)

    def body(c_ref, a_ref, b_ref, o_ref):
        o_ref[...] = (a_ref[0].astype(F32) + b_ref[...].astype(F32)).astype(BF16)

    spec = pl.BlockSpec((1, tm, tc), lambda p, i, j, c_ref: (p, i, j))
    return pl.pallas_call(
        body, name=name, out_shape=jax.ShapeDtypeStruct((P4, Rh, C), BF16),
        grid_spec=pltpu.PrefetchScalarGridSpec(
            num_scalar_prefetch=1, grid=(P4, Rh // tm, C // tc),
            in_specs=[pl.BlockSpec((1, 1, tm, tc), lambda p, i, j, c_ref: (p, c_ref[0], i, j)), spec], out_specs=spec),
        compiler_params=_params(("parallel",) * 3),
    )(core, parts, sib)


def _sum_slabs(pre, recv, chip, name):
    _, Rh, C = pre.shape
    tm, tc = _tile2(Rh, C, 16)

    def body(me_ref, own_ref, r_ref, o_ref):
        acc = own_ref[0].astype(F32)
        for j in range(3):
            acc = acc + r_ref[j].astype(F32)
        o_ref[...] = acc

    return pl.pallas_call(
        body, name=name, out_shape=jax.ShapeDtypeStruct((Rh, C), F32),
        grid_spec=pltpu.PrefetchScalarGridSpec(
            num_scalar_prefetch=1, grid=(Rh // tm, C // tc),
            in_specs=[pl.BlockSpec((1, tm, tc), lambda i, j, me_ref: (me_ref[0], i, j)),
                      pl.BlockSpec((3, tm, tc), lambda i, j, me_ref: (0, i, j))],
            out_specs=pl.BlockSpec((tm, tc), lambda i, j, me_ref: (i, j))),
        compiler_params=_params(("parallel", "parallel")),
    )(chip, pre, recv)


def kernel(x, c, positions, w_ada, b_ada, w_in, g_q_a, w_q_b, g_kv_a, w_kv_b, w_o_a, w_conv, w_o_b, w_o, ln1_g, ln1_b, w_ffn_in, w_ffn_out, ln2_g, ln2_b, loss_target, m_w_ada, m_b_ada, m_w_in, m_g_q_a, m_w_q_b, m_g_kv_a, m_w_kv_b, m_w_o_a, m_w_conv, m_w_o_b, m_w_o, m_ln1_g, m_ln1_b, m_w_ffn_in, m_w_ffn_out, m_ln2_g, m_ln2_b, v_w_ada, v_b_ada, v_w_in, v_g_q_a, v_w_q_b, v_g_kv_a, v_w_kv_b, v_w_o_a, v_w_conv, v_w_o_b, v_w_o, v_ln1_g, v_ln1_b, v_w_ffn_in, v_w_ffn_out, v_ln2_g, v_ln2_b):
    S, D = x.shape[1], x.shape[2]
    F = w_ffn_out.shape[1] * 4
    ax, ay, ac = _place()
    chip = 2 * ax + ay
    dev = 4 * ax + 2 * ay + ac
    x2, tgt = x[0], loss_target[0]
    w_ada2, w_in2, w_q_b2, w_kv_b2 = w_ada[0], w_in[0], w_q_b[0], w_kv_b[0]
    w_o_a2, w_o_b2, w_o2, w_ffn_in2, w_ffn_out2 = w_o_a[0], w_o_b[0], w_o[0], w_ffn_in[0], w_ffn_out[0]
    NA = w_ada2.shape[1]
    CW = w_conv.shape[2]

    inv_freq = 1.0 / (ROPE_THETA ** (jnp.arange(0, QK_ROPE, 2, dtype=F32) / QK_ROPE))
    ang = positions[0].astype(F32)[:, None] * inv_freq
    cos, sin = jnp.cos(ang), jnp.sin(ang)
    z32, z64, z96 = jnp.zeros((S, 32), F32), jnp.zeros((S, 64), F32), jnp.zeros((S, 96), F32)
    tab = jnp.concatenate([cos, cos, z64, -sin, z96, z32, sin, z64], axis=1)

    def halves(a):
        return a.reshape(2, a.shape[0] // 2, a.shape[1])

    def whole(g):
        return g.reshape(4, 2 * g.shape[2], g.shape[3])

    def cols(g):
        return jnp.transpose(g, (1, 0, 2)).reshape(g.shape[1], 4 * g.shape[2])

    w_inT, m_w_inT, v_w_inT = w_in2.T, m_w_in[0].T, v_w_in[0].T
    CS = w_inT.shape[0]
    CSP = -(-CS // 32) * 32
    sh_in = halves(jnp.pad(w_inT.astype(BF16), ((0, CSP - CS), (0, 0))))
    c_all = _all_gather8(c, "gather_c").reshape(8, D)
    wconv_all = _all_gather8(w_conv[0], "gather_wconv")
    w_conv_full = jnp.transpose(wconv_all[0::2], (1, 0, 2)).reshape(3, D)
    b_sh = lax.dynamic_slice(b_ada, (0, chip * NA), (1, NA))
    mod_sh = _ada_fwd(c_all, w_ada2, b_sh)
    mod_all = _all_gather8(mod_sh, "gather_mod")
    mod = lax.dynamic_slice(mod_all[0::2], (0, dev, 0), (4, 1, NA)).reshape(6, D)
    shift1, scale1, gate1, shift2, scale2, gate2 = (mod[k:k + 1] for k in range(6))

    cl_in0 = _gather_copies([sh_in], (0, 2))
    st_in0 = _split_start([sh_in], cl_in0, [shift1, w_conv_full], "gather_in0_start")
    cl_in1 = _gather_copies(st_in0[2], (1, 2), st_in0[3])
    st_in1 = _split_start(st_in0[2], cl_in1, st_in0[4], "gather_in1_start")
    st_in0 = st_in0[:2] + (st_in1[2], st_in1[3], st_in0[4])
    shift1, w_conv_full = st_in1[4]
    others = lax.optimization_barrier((w_q_b2, w_kv_b2, w_o_a2, w_o_b2, w_o2, w_ffn_in2, w_ffn_out2, shift1))
    sh_qb, sh_kvb, sh_oa, sh_ob, sh_o, sh_fi, sh_fo = (halves(w.astype(BF16)) for w in others[:7])
    shift1 = others[7]
    sh_in_t, l_in = _split_wait(st_in0, cl_in0, [sh_qb, sh_kvb, sh_oa, sh_ob, sh_o, sh_fi, sh_fo], "gather_in0_wait")
    l_in = _run_plan(_gather_plan(sh_in_t, (0, 2), into=l_in, ici=False), "handon_in0")
    sh_in_t, l_in = _split_wait(st_in1[:2] + (sh_in_t, l_in, None), cl_in1, [], "gather_in1_wait")
    sh_a1, sh_a2 = [sh_qb, sh_kvb], [sh_oa, sh_ob, sh_o]
    cl_a1, cl_a2, cl_fi, cl_fo = (_gather_copies(g) for g in (sh_a1, sh_a2, [sh_fi], [sh_fo]))
    st_a1 = _split_start(sh_a1, cl_a1, shift1, "gather_a1_start", after=[l_in[0]])
    st_a2 = _split_start(sh_a2, cl_a2, st_a1[4], "gather_a2_start")
    u, (g_in,) = _modulate(x2, scale1, st_a2[4][0], "modulate1",
                           carry=_gather_plan(sh_in_t, (1, 2), into=l_in, ici=False))
    g_in = whole(g_in)

    def in_rows(lo, hi):
        parts = [g_in[p, max(lo, p * CS) - p * CS:min(hi, (p + 1) * CS) - p * CS]
                 for p in range(4) if max(lo, p * CS) < min(hi, (p + 1) * CS)]
        return parts[0] if len(parts) == 1 else jnp.concatenate(parts, axis=0)

    n_qkv = Q_LORA + KV_LORA + QK_ROPE
    W_qkvT = jnp.pad(in_rows(0, n_qkv), ((0, QKV_A - n_qkv), (0, 0)))
    W_convT = in_rows(n_qkv, n_qkv + 3 * D)
    W_gateT = in_rows(n_qkv + 3 * D, n_qkv + 5 * D)

    pq = _matmul(u, W_qkvT, "nt", F32, "proj_qkv")
    pc = _matmul(u, W_convT, "nt", BF16, "proj_conv")
    sh_a1, la1 = _split_wait(st_a1, cl_a1, [pc], "gather_a1_wait")
    pg, (g_qb, g_kvb) = _matmul(u, W_gateT, "nt", BF16, "proj_gate", carry=_gather_plan(sh_a1, into=la1, ici=False))
    st_fi = _split_start([sh_fi], cl_fi, g_q_a, "gather_fi_start", after=[pg])
    W_qb = jnp.pad(cols(whole(g_qb)).reshape(Q_LORA, N_HEADS, QK_NOPE + QK_ROPE),
                   ((0, 0), (0, 0), (0, QK_PAD - QK_NOPE - QK_ROPE))).reshape(Q_LORA, N_HEADS * QK_PAD)
    W_kvb = cols(whole(g_kvb))
    rq, rkv, kr = _rms_fwd(pq, tab, st_fi[4][0], g_kv_a)
    kv = _matmul(rkv, W_kvb, "nn", BF16, "kv_b")
    sh_a2, la2 = _split_wait(st_a2, cl_a2, [kv], "gather_a2_wait")
    def rope_heads(r, t):
        return jnp.concatenate([r[:, lo:lo + 128] if lo % QK_PAD == 0 else _rope(r[:, lo:lo + 128], t, 1)
                                for lo in range(0, r.shape[1], 128)], axis=1)

    q, (g_oa, g_ob, g_o) = _matmul(rq, W_qb, "nn", BF16, "q_b", carry=_gather_plan(sh_a2, into=la2, ici=False),
                                   finish=(rope_heads, tab))
    o, lse = _attn_fwd(q, kv, kr)
    W_oa, W_ob, W_o = (g.reshape(-1, D) for g in (g_oa, g_ob, g_o))
    hb = _conv_fwd(pc, w_conv_full)
    st_fo = _split_start([sh_fo], cl_fo, ln1_g, "gather_fo_start", after=[o])
    y_b = _matmul(hb, W_ob, "nn", BF16, "o_b")
    y_a = _matmul(o, W_oa, "nn", BF16, "o_a")
    merged = _merge_fwd(y_a, y_b, pg)
    sh_fi_t, lfi = _split_wait(st_fi, cl_fi, [merged], "gather_fi_wait")
    mix, g_fi = _matmul(merged, W_o, "nn", F32, "w_o", carry=_gather_plan(sh_fi_t, (0, 2), into=lfi, ici=False))
    (x1, u2), (g_fi,) = _ln1_fwd(x2, mix, gate1, st_fo[4][0], ln1_b, scale2, shift2,
                                 carry=_gather_plan(sh_fi_t, (1, 2), into=g_fi, ici=False))
    W_fi = whole(g_fi)
    hh = _matmul(u2, W_fi, "nn", BF16, "ffn_in", shards="b")
    sh_fo_t, lfo = _split_wait(st_fo, cl_fo, [hh], "gather_fo_wait")
    act, (g_fo,) = _swiglu_fwd(hh, carry=_gather_plan(sh_fo_t, into=lfo, ici=False))
    W_fo = g_fo.reshape(F, D)
    ffn = _matmul(act, W_fo, "nn", F32, "ffn_out")

    core_i = ac.astype(jnp.int32).reshape(1)
    chip_i = chip.astype(jnp.int32).reshape(1)

    def uncols(g):
        return jnp.transpose(g.reshape(g.shape[0], 4, g.shape[1] // 4), (1, 0, 2))

    def slabs(p):
        return p.reshape(4, 2, p.shape[1] // 2, p.shape[2])

    def add_pairs(parts, sibs, nms):
        return [_add_pair(a, b, core_i, "add_pair_" + nm) for a, b, nm in zip(parts, sibs, nms)]

    def sum_all(pre, recv, nms):
        return [_sum_slabs(a, r, chip_i, "sum_slabs_" + nm) for a, r, nm in zip(pre, recv, nms)]

    dffn, dx1a, loss_acc, d_ln2_g, d_ln2_b, d_gate2 = _ln2_loss_bwd(x1, ffn, gate2, ln2_g, ln2_b, tgt)
    dW_fo = _matmul(act, dffn, "tn", BF16, "d_w_ffn_out")
    p_fo = [slabs(dW_fo.reshape(4, -1, D))]
    dact, s_fo = _matmul(dffn, W_fo, "nt", BF16, "d_act", carry=_pair_plan(p_fo))
    pre_fo = add_pairs(p_fo, s_fo, ["w_ffn_out"])
    cs_fo = _scatter_copies(pre_fo)
    st_sfo = _split_start(pre_fo, cs_fo, scale2, "scatter_fo_start")
    dhh = _swiglu_bwd(dact, hh)
    dW_fi = _matmul(u2, dhh, "tn", BF16, "d_w_ffn_in", shards="o")
    p_fi = [slabs(dW_fi)]
    du2, s_fi = _matmul(dhh, W_fi, "nt", F32, "d_u2", carry=_pair_plan(p_fi), shards="b")
    pre_fi = add_pairs(p_fi, s_fi, ["w_ffn_in"])
    cs_fi = _scatter_copies(pre_fi)
    st_sfi = _split_start(pre_fi, cs_fi, st_sfo[4], "scatter_fi_start")
    dmix, dxa, d_shift2, d_scale2, d_ln1_g, d_ln1_b, d_gate1 = _ln1_bwd(x2, mix, dx1a, du2, gate1, ln1_g, ln1_b, st_sfi[4][0])
    dW_o = _matmul(merged, dmix, "tn", BF16, "d_w_o")
    dmerged = _matmul(dmix, W_o, "nt", BF16, "d_merged")
    dy_a, dy_b, dgate = _merge_bwd(dmerged, y_a, y_b, pg)
    dW_oa = _matmul(o, dy_a, "tn", BF16, "d_w_o_a")
    do = _matmul(dy_a, W_oa, "nt", BF16, "d_o")
    dW_ob = _matmul(hb, dy_b, "tn", BF16, "d_w_o_b")
    p_mid = [slabs(g.reshape(4, -1, D)) for g in (dW_oa, dW_ob, dW_o)]
    dhb, s_mid = _matmul(dy_b, W_ob, "nt", BF16, "d_hb", carry=_pair_plan(p_mid))
    pre_mid = add_pairs(p_mid, s_mid, ["w_o_a", "w_o_b", "w_o"])
    cs_mid = _scatter_copies(pre_mid)
    st_smid = _split_start(pre_mid, cs_mid, w_conv_full, "scatter_mid_start")
    dconv, d_wconv = _conv_bwd(dhb, pc, st_smid[4][0])
    dq, dkv, dkr, _ = _attn_bwd(q, kv, kr, do, o, lse, tab, carry=_token_plan(st_smid[4][0]))
    names_a = ["w_ffn_out", "w_ffn_in", "w_o_a", "w_o_b", "w_o"]
    dW_qb = _matmul(rq, dq, "tn", BF16, "d_w_q_b")
    d_rq = _matmul(dq, W_qb, "nt", F32, "d_rq")
    dW_kvb = _matmul(rkv, dkv, "tn", BF16, "d_w_kv_b")
    d_rkv = _matmul(dkv, W_kvb, "nt", F32, "d_rkv")
    dqkv, d_g_q, d_g_kv = _rms_bwd(d_rq, d_rkv, pq, dkr, g_q_a, g_kv_a)
    dW_qkvT = _matmul(dqkv, u, "tn", BF16, "d_w_qkv")
    dW_convT = _matmul(dconv, u, "tn", BF16, "d_w_conv")
    dW_gateT = _matmul(dgate, u, "tn", BF16, "d_w_gate")
    pre_fo, r_fo = _split_wait(st_sfo, cs_fo, [dW_qkvT], "scatter_fo_wait")
    pre_fi, r_fi = _split_wait(st_sfi, cs_fi, [dW_qkvT], "scatter_fi_wait")
    pre_mid, r_mid = _split_wait(st_smid, cs_mid, [dW_qkvT], "scatter_mid_wait")
    fin_a = sum_all(pre_fo + pre_fi + pre_mid, r_fo + r_fi + r_mid, names_a)
    srcs = [(0, dW_qkvT[:n_qkv]), (n_qkv, dW_convT), (n_qkv + 3 * D, dW_gateT)]
    rows_of = []
    for p in range(4):
        for lo, src in srcs:
            a, b = max(lo, p * CS), min(lo + src.shape[0], (p + 1) * CS)
            if a < b:
                rows_of.append(src[a - lo:b - lo])
        rows_of.append(jnp.zeros((CSP - CS, D), BF16))
    dW_inT = jnp.concatenate(rows_of, axis=0).reshape(4, CSP, D)
    dW_qb_u = dW_qb.reshape(Q_LORA, N_HEADS, QK_PAD)[:, :, :QK_NOPE + QK_ROPE].reshape(Q_LORA, -1)
    names_b = ["w_in", "w_q_b", "w_kv_b"]
    p_b = [slabs(dW_inT), slabs(uncols(dW_qb_u)), slabs(uncols(dW_kvb))]
    du, s_b = _matmul(dqkv, W_qkvT, "nn", F32, "d_u_qkv", carry=_pair_plan(p_b))
    pre_b = add_pairs(p_b, s_b, names_b)
    cs_b = _scatter_copies(pre_b)
    st_b = _split_start(pre_b, cs_b, scale1, "scatter_last_start")
    du, fs_a = _matmul(dconv, W_convT, "nn", F32, "d_u_conv", add=du, carry=_sibling_plan(fin_a))
    du = _matmul(dgate, W_gateT, "nn", F32, "d_u_gate", add=du)
    grad_x, d_shift1, d_scale1 = _dx_final(dxa, du, x2, st_b[4][0])

    big = {}
    ws = dict(w_in=(w_inT, m_w_inT, v_w_inT), w_q_b=(w_q_b2, m_w_q_b[0], v_w_q_b[0]),
              w_kv_b=(w_kv_b2, m_w_kv_b[0], v_w_kv_b[0]), w_o_a=(w_o_a2, m_w_o_a[0], v_w_o_a[0]),
              w_o_b=(w_o_b2, m_w_o_b[0], v_w_o_b[0]), w_o=(w_o2, m_w_o[0], v_w_o[0]),
              w_ffn_in=(w_ffn_in2, m_w_ffn_in[0], v_w_ffn_in[0]), w_ffn_out=(w_ffn_out2, m_w_ffn_out[0], v_w_ffn_out[0]))

    def adam_of(nm, a, b, carry=None):
        w_, m_, v_ = ws[nm]
        return _adam_halves("adam_" + nm, w_, m_, v_, a, b, core_i, carry)

    for nm, a, b in zip(names_a, fin_a, fs_a):
        big[nm] = adam_of(nm, a, b, _token_plan(st_b[4][0]))[0]
    done = [big[nm][1] for nm in names_a] + [grad_x]
    pre_b, r_b = _split_wait(st_b, cs_b, done, "scatter_last_wait")
    fin_b = sum_all(pre_b, r_b, names_b)
    fs_b = _run_plan(_sibling_plan(fin_b), "sibling_last")
    for nm, a, b in zip(names_b, fin_b, fs_b):
        big[nm] = adam_of(nm, a, b)

    def pad_d(v):
        return jnp.pad(v, ((0, 0), (0, D - v.shape[1])))

    small = _pack_rows([d_ln1_g, d_ln1_b, d_ln2_g, d_ln2_b, pad_d(d_g_q), pad_d(d_g_kv), d_wconv,
                         d_shift1, d_scale1, d_gate1, d_shift2, d_scale2, d_gate2, pad_d(loss_acc)], 16, after=[pre_b[1]])
    small_all = _all_gather8(small, "gather_small")
    small_sum = _sum8(small_all)
    loss = small_sum[15, 0]
    g_ln1_g, g_ln1_b, g_ln2_g, g_ln2_b = (small_sum[k:k + 1] for k in range(4))
    g_g_q, g_g_kv = small_sum[4:5, :Q_LORA], small_sum[5:6, :KV_LORA]
    g_wconv = lax.dynamic_slice(small_sum[6:9], (0, chip * CW), (3, CW))
    g_b_ada = small_sum[9:15].reshape(1, 6 * D)
    dmod_all = small_all[:, 9:15, :].reshape(8, 6 * D)
    g_w_ada = _ada_bwd(c_all, lax.dynamic_slice(dmod_all, (0, chip * NA), (8, NA)))
    big["w_ada"] = [g_w_ada] + list(_adam("adam_w_ada", w_ada2, m_w_ada[0], v_w_ada[0], g_w_ada))
    sm = {}
    for nm, w_, m_, v_, g_ in [("b_ada", b_ada, m_b_ada, v_b_ada, g_b_ada), ("g_q_a", g_q_a, m_g_q_a, v_g_q_a, g_g_q),
                               ("g_kv_a", g_kv_a, m_g_kv_a, v_g_kv_a, g_g_kv),
                               ("w_conv", w_conv[0], m_w_conv[0], v_w_conv[0], g_wconv),
                               ("ln1_g", ln1_g, m_ln1_g, v_ln1_g, g_ln1_g), ("ln1_b", ln1_b, m_ln1_b, v_ln1_b, g_ln1_b),
                               ("ln2_g", ln2_g, m_ln2_g, v_ln2_g, g_ln2_g), ("ln2_b", ln2_b, m_ln2_b, v_ln2_b, g_ln2_b)]:
        sm[nm] = (g_,) + tuple(_adam_small("adam_" + nm, w_, m_, v_, g_))

    order = ["w_ada", "b_ada", "w_in", "g_q_a", "w_q_b", "g_kv_a", "w_kv_b", "w_o_a", "w_conv", "w_o_b", "w_o",
             "ln1_g", "ln1_b", "w_ffn_in", "w_ffn_out", "ln2_g", "ln2_b"]
    lead = {"b_ada", "g_q_a", "g_kv_a", "ln1_g", "ln1_b", "ln2_g", "ln2_b"}

    def leaf(nm, k):
        val = big[nm][k] if nm in big else sm[nm][k]
        if nm == "w_in":
            val = val.T
        return val if nm in lead else val[None]

    outs = [loss, grad_x[None]]
    for k in range(4):
        outs += [leaf(nm, k) for nm in order]
    return tuple(outs)
```

```python
import jax
import jax.numpy as jnp
from jax import lax
from jax.experimental import pallas as pl
from jax.experimental.pallas import tpu as pltpu

F32, BF16 = jnp.float32, jnp.bfloat16
N_HEADS, QK_NOPE, QK_ROPE, V_HEAD = 16, 128, 64, 128
Q_LORA, KV_LORA = 512, 512
QK_PAD = 256
QKV_A = 1152
CHUNK_SHIFT = 6
ATTN_SCALE = (QK_NOPE + QK_ROPE) ** -0.5
LOG2E = 1.4426950408889634
SCALE2 = ATTN_SCALE * LOG2E
ROPE_THETA = 10000.0
ALPHA = 2.0 ** 0.25
LN_EPS, RMS_EPS = 1e-5, 1e-6
ADAM_LR, ADAM_B1, ADAM_B2, ADAM_EPS, ADAM_WD, ADAM_STEP = 0.001, 0.9, 0.999, 1e-08, 0.01, 10
ADAM_C1 = 1.0 - ADAM_B1 ** ADAM_STEP
ADAM_C2 = 1.0 - ADAM_B2 ** ADAM_STEP
VMEM_LIMIT = 56 * 1024 * 1024
MESH = pl.DeviceIdType.MESH
ANY = pl.BlockSpec(memory_space=pl.ANY)
HBM_SPEC = pl.BlockSpec(memory_space=pltpu.HBM)
SEM_SPEC = pl.BlockSpec(memory_space=pltpu.SEMAPHORE)
NT = (((1,), (1,)), ((), ()))
TN = (((0,), (0,)), ((), ()))
NN = (((1,), (0,)), ((), ()))


def _params(sem=None):
    return pltpu.CompilerParams(dimension_semantics=sem, vmem_limit_bytes=VMEM_LIMIT)


def _pick(n, cands=(1408, 1024, 512, 384, 256, 128)):
    for t in cands:
        if n % t == 0:
            return t
    return n


def _row_tile(rows, row_bytes, budget, mult=8):
    best = mult
    for t in range(mult, rows + 1, mult):
        if rows % t == 0 and t * row_bytes <= budget:
            best = t
    return best


def _tile2(rows, cols, mult=8, budget=3 << 18):
    col_tiles = [t for t in range(128, cols + 1, 128) if cols % t == 0] or [cols]
    best = None
    for tc in col_tiles:
        for tr in range(mult, rows + 1, mult):
            if rows % tr == 0 and tr * tc <= budget and (best is None or (tr * tc, tc) > (best[0] * best[1], best[1])):
                best = (tr, tc)
    assert best is not None, (rows, cols)
    return best


def _sigmoid(x):
    return jax.nn.sigmoid(x)


class _Plan:
    def __init__(self, ins, outs, sems, start, finish, aliases=None):
        self.ins, self.outs, self.sems, self.start, self.finish = list(ins), list(outs), list(sems), start, finish
        self.aliases = dict(aliases or {})

    def io_aliases(self, first_in, first_out):
        return {first_in + i: first_out + o for i, o in self.aliases.items()}


def _token_plan(token):
    return _Plan([token], [], [], lambda *a: None, lambda *a: None)


def _run_plan(plan, name, ride=None):
    n_in, n_out = len(plan.ins), len(plan.outs)
    extra = [] if ride is None else list(ride)
    aliases = plan.io_aliases(0, 0)
    for k in range(len(extra)):
        aliases[n_in + k] = n_out + k

    def body(*refs):
        ins, outs, sems = refs[:n_in], refs[n_in + len(extra):n_in + len(extra) + n_out], refs[n_in + 2 * len(extra) + n_out:]
        plan.start(ins, outs, sems)
        plan.finish(ins, outs, sems)

    return pl.pallas_call(body, name=name, out_shape=plan.outs + [jax.ShapeDtypeStruct(r.shape, r.dtype) for r in extra],
                          in_specs=[ANY] * (n_in + len(extra)), out_specs=[ANY] * (n_out + len(extra)),
                          scratch_shapes=plan.sems, input_output_aliases=aliases,
                          compiler_params=_params())(*plan.ins, *extra)


def _matmul(a, b, mode, out_dtype, name, add=None, carry=None, shards=None, finish=None):
    if mode == "nn":
        (M, K), N, dims = a.shape, b.shape[-1] * (4 if shards else 1), NN
    elif mode == "nt":
        (M, K), N, dims = a.shape, b.shape[-2], NT
    else:
        (K, M), N, dims = a.shape, b.shape[1], TN
    split_n = shards and mode != "nt"
    tm = _pick(M)
    tn = _pick(N // 4) if split_n else _pick(N)
    deep = (2816, 2048, 1408, 1024, 512, 384, 256, 128)
    if shards and mode == "nt":
        tk = _pick(K // 4, deep)
    else:
        tk = K if K <= 2048 else _pick(K, deep)
    nk = K // tk
    per = (N // 4 // tn) if split_n else (K // 4 // tk if shards else 1)
    a_spec = (pl.BlockSpec((tk, tm), lambda i, j, k: (k, i)) if mode == "tn"
              else pl.BlockSpec((tm, tk), lambda i, j, k: (i, k)))
    if shards == "b" and mode == "nn":
        b_spec = pl.BlockSpec((None, tk, tn), lambda i, j, k: (j // per, k, j % per))
    elif shards == "b":
        b_spec = pl.BlockSpec((None, tn, tk), lambda i, j, k: (k // per, j, k % per))
    else:
        b_spec = (pl.BlockSpec((tn, tk), lambda i, j, k: (j, k)) if mode == "nt"
                  else pl.BlockSpec((tk, tn), lambda i, j, k: (k, j)))
    o_spec = pl.BlockSpec((tm, tn), lambda i, j, k: (i, j))
    o_shape = (M, N)
    if shards == "o":
        o_spec, o_shape = pl.BlockSpec((None, tm, tn), lambda i, j, k: (j // per, i, j % per)), (4, M, N // 4)
    has_add = add is not None
    has_fin = finish is not None
    n_ci = len(carry.ins) if carry else 0
    n_co = len(carry.outs) if carry else 0
    n_in = 2 + has_add + has_fin
    grid = (M // tm, N // tn, nk)

    def body(*refs):
        a_ref, b_ref = refs[0], refs[1]
        add_ref = refs[2] if has_add else None
        fin_ref = refs[2 + has_add] if has_fin else None

        def store(r):
            if has_add:
                r = r + add_ref[...]
            if has_fin:
                r = finish[0](r, fin_ref[...])
            o_ref[...] = r.astype(o_ref.dtype)

        o_ref = refs[n_in + n_ci]
        acc_ref = refs[n_in + n_ci + 1 + n_co] if nk > 1 else None
        c_ins = refs[n_in:n_in + n_ci]
        c_outs = refs[n_in + n_ci + 1:n_in + n_ci + 1 + n_co]
        c_sems = refs[n_in + n_ci + 1 + n_co + (nk > 1):]
        i, j, k = pl.program_id(0), pl.program_id(1), pl.program_id(2)

        if carry:
            @pl.when((i == 0) & (j == 0) & (k == 0))
            def _():
                carry.start(c_ins, c_outs, c_sems)

        part = lax.dot_general(a_ref[...], b_ref[...], dims, preferred_element_type=F32)
        if nk == 1:
            store(part)
        else:
            @pl.when(k == 0)
            def _():
                acc_ref[...] = part

            @pl.when((k > 0) & (k < nk - 1))
            def _():
                acc_ref[...] += part

            @pl.when(k == nk - 1)
            def _():
                store(acc_ref[...] + part)

        if carry:
            @pl.when((i == grid[0] - 1) & (j == grid[1] - 1) & (k == nk - 1))
            def _():
                carry.finish(c_ins, c_outs, c_sems)

    ins = [a, b] + ([add] if has_add else []) + ([finish[1]] if has_fin else []) + (carry.ins if carry else [])
    in_specs = ([a_spec, b_spec] + ([o_spec] if has_add else [])
                + ([pl.BlockSpec((tm, finish[1].shape[1]), lambda i, j, k: (i, 0))] if has_fin else []) + [ANY] * n_ci)
    res = pl.pallas_call(
        body, name=name, grid=grid,
        in_specs=in_specs, out_specs=[o_spec] + [ANY] * n_co,
        out_shape=[jax.ShapeDtypeStruct(o_shape, out_dtype)] + (carry.outs if carry else []),
        scratch_shapes=([pltpu.VMEM((tm, tn), F32)] if nk > 1 else []) + (carry.sems if carry else []),
        input_output_aliases=carry.io_aliases(n_in, 1) if carry else {},
        compiler_params=_params(("arbitrary",) * 3 if carry else ("parallel", "parallel", "arbitrary")),
    )(*ins)
    return (res[0], res[1:]) if carry else res[0]


def _rows(body, name, n_rows, tm, ins, outs, accs=(), carry=None):
    grid = (n_rows // tm,)

    def halo(arr):
        return 16 if arr.dtype == BF16 else 8

    arrays, in_specs = [], []
    for spec in ins:
        kind, arr = spec[0], spec[1]
        arrays.append(arr)
        if kind == "row":
            _, _, cb, w = spec
            in_specs.append(pl.BlockSpec((tm, w), lambda i, cb=cb: (i, cb)))
        elif kind == "full":
            in_specs.append(pl.BlockSpec(arr.shape, lambda i, nd=arr.ndim: (0,) * nd))
        elif kind == "prev":
            _, _, cb, w = spec
            h = halo(arr)
            in_specs.append(pl.BlockSpec((h, w), lambda i, cb=cb, per=tm // h: (jnp.maximum(i * per - 1, 0), cb)))
        else:
            _, _, cb, w = spec
            h = halo(arr)
            in_specs.append(pl.BlockSpec((h, w), lambda i, cb=cb, per=tm // h, last=n_rows // h - 1:
                                         (jnp.minimum((i + 1) * per, last), cb)))
    out_shape = [jax.ShapeDtypeStruct((n_rows, w), dt) for (w, dt) in outs]
    out_specs = [pl.BlockSpec((tm, w), lambda i: (i, 0)) for (w, _) in outs]
    out_shape += [jax.ShapeDtypeStruct(s, F32) for s in accs]
    out_specs += [pl.BlockSpec(s, lambda i, nd=len(s): (0,) * nd) for s in accs]
    n_in, n_out, n_acc = len(ins), len(outs), len(accs)
    n_ci = len(carry.ins) if carry else 0
    n_co = len(carry.outs) if carry else 0

    def kernel_body(*refs):
        first = n_in + n_ci
        c_ins, c_outs, c_sems = refs[n_in:first], refs[first + n_out + n_acc:first + n_out + n_acc + n_co], refs[first + n_out + n_acc + n_co:]
        if carry:
            @pl.when(pl.program_id(0) == 0)
            def _():
                carry.start(c_ins, c_outs, c_sems)

        body(pl.program_id(0), refs[:n_in], refs[first:first + n_out], refs[first + n_out:first + n_out + n_acc])
        if carry:
            @pl.when(pl.program_id(0) == grid[0] - 1)
            def _():
                carry.finish(c_ins, c_outs, c_sems)

    res = pl.pallas_call(
        kernel_body, name=name, grid=grid, in_specs=in_specs + [ANY] * n_ci, out_specs=out_specs + [ANY] * n_co,
        out_shape=out_shape + (carry.outs if carry else []), scratch_shapes=carry.sems if carry else [],
        input_output_aliases=carry.io_aliases(n_in, n_out + n_acc) if carry else {},
        compiler_params=_params(("arbitrary",)),
    )(*arrays, *(carry.ins if carry else []))
    return (res[:n_out + n_acc], res[n_out + n_acc:]) if carry else res


def _acc_add(i, ref, val):
    @pl.when(i == 0)
    def _():
        ref[...] = val

    @pl.when(i > 0)
    def _():
        ref[...] += val


def _rope(t, tab, sign):
    c, sa, sb = tab[:, 0:128], tab[:, 128:256], tab[:, 256:384]
    rot = pltpu.roll(t, 96, 1) * sa + pltpu.roll(t, 32, 1) * sb
    return t * c + rot if sign > 0 else t * c - rot


def _ln_stats(r):
    mu = jnp.mean(r, axis=-1, keepdims=True)
    d = r - mu
    var = jnp.mean(d * d, axis=-1, keepdims=True)
    rstd = lax.rsqrt(var + LN_EPS)
    return d * rstd, rstd


def _ln_bwd(dxh, xh, rstd):
    m1 = jnp.mean(dxh, axis=-1, keepdims=True)
    m2 = jnp.mean(dxh * xh, axis=-1, keepdims=True)
    return rstd * (dxh - m1 - xh * m2)


def _modulate(x, scale, shift, name, carry=None):
    S, D = x.shape

    def body(i, ins, outs, accs):
        outs[0][...] = (ins[0][...] * (1.0 + ins[1][...]) + ins[2][...]).astype(BF16)

    res = _rows(body, name, S, _pick(S, (256, 128)), [("row", x, 0, D), ("full", scale), ("full", shift)], [(D, BF16)],
                carry=carry)
    return (res[0][0], res[1]) if carry else res[0]


def _rms_fwd(pq, tab, g_q, g_kv):
    S = pq.shape[0]

    def body(i, ins, outs, accs):
        pq_ref, tab_ref, gq_ref, gkv_ref = ins

        def rms(x, g):
            return x * lax.rsqrt(jnp.mean(x * x, axis=-1, keepdims=True) + RMS_EPS) * g

        outs[0][...] = rms(pq_ref[:, 0:Q_LORA], gq_ref[...]).astype(BF16)
        outs[1][...] = rms(pq_ref[:, Q_LORA:Q_LORA + KV_LORA], gkv_ref[...]).astype(BF16)
        outs[2][...] = _rope(pq_ref[:, Q_LORA + KV_LORA:QKV_A], tab_ref[...], 1).astype(BF16)

    return _rows(body, "rms_fwd", S, _pick(S, (256, 128)),
                 [("row", pq, 0, QKV_A), ("row", tab, 0, 384), ("full", g_q), ("full", g_kv)],
                 [(Q_LORA, BF16), (KV_LORA, BF16), (128, BF16)])


def _allowed(q0, k0, bq):
    row = q0 + lax.broadcasted_iota(jnp.int32, (bq, bq), 0)
    col = k0 + lax.broadcasted_iota(jnp.int32, (bq, bq), 1)
    return (col >> CHUNK_SHIFT) <= (row >> CHUNK_SHIFT)


ATTN_BLOCK = 512


HEADS_PER_STEP = 2


def _attn_fwd(q, kv, kr):
    S = q.shape[0]
    bq = min(ATTN_BLOCK, S)
    nq = S // bq
    G = HEADS_PER_STEP

    def body(q_ref, kv_ref, kr_ref, o_ref, lse_ref, kcat):
        qi = pl.program_id(1)

        @pl.when(qi == 0)
        def _():
            for g in range(G):
                kcat[g, :, 0:128] = kv_ref[:, g * 256:g * 256 + 128]
                kcat[g, :, 128:256] = kr_ref[...]

        qs = [q_ref[:, g * QK_PAD:(g + 1) * QK_PAD] for g in range(G)]

        def step(j, carry, masked):
            off = pl.multiple_of(j * bq, bq)
            rows = pl.ds(off, bq)
            mask = _allowed(qi * bq, off, bq) if masked else None
            out = []
            for g in range(G):
                m, l, acc = carry[g]
                s = lax.dot_general(qs[g], kcat[g, rows, :], NT, preferred_element_type=F32) * SCALE2
                if masked:
                    s = jnp.where(mask, s, -1e30)
                m_new = jnp.maximum(m, jnp.max(s, axis=1, keepdims=True))
                a = jnp.exp2(m - m_new)
                p = jnp.exp2(s - m_new)
                l = a * l + jnp.sum(p, axis=1, keepdims=True)
                acc = a * acc + jnp.dot(p.astype(BF16), kv_ref[rows, g * 256 + 128:(g + 1) * 256],
                                        preferred_element_type=F32)
                out.append((m_new, l, acc))
            return tuple(out)

        init = tuple((jnp.full((bq, 1), -1e30, F32), jnp.zeros((bq, 1), F32), jnp.zeros((bq, V_HEAD), F32))
                     for _ in range(G))
        below = lax.fori_loop(0, qi, lambda j, cr: step(j, cr, False), init)
        for g, (m, l, acc) in enumerate(step(qi, below, True)):
            o_ref[:, g * V_HEAD:(g + 1) * V_HEAD] = (acc / l).astype(BF16)
            lse_ref[g] = m + jnp.log2(l)

    return pl.pallas_call(
        body, name="attn_fwd", grid=(N_HEADS // G, nq),
        in_specs=[pl.BlockSpec((bq, G * QK_PAD), lambda h, i: (i, h)),
                  pl.BlockSpec((S, G * 256), lambda h, i: (0, h)),
                  pl.BlockSpec((S, 128), lambda h, i: (0, 0))],
        out_specs=[pl.BlockSpec((bq, G * V_HEAD), lambda h, i: (i, h)),
                   pl.BlockSpec((G, bq, 1), lambda h, i: (h, i, 0))],
        out_shape=[jax.ShapeDtypeStruct((S, N_HEADS * V_HEAD), BF16),
                   jax.ShapeDtypeStruct((N_HEADS, S, 1), F32)],
        scratch_shapes=[pltpu.VMEM((G, S, QK_PAD), BF16)],
        compiler_params=_params(("arbitrary", "arbitrary")),
    )(q, kv, kr)


def _attn_bwd(q, kv, kr, do, o, lse, tab, carry=None):
    S = q.shape[0]
    bq = min(ATTN_BLOCK, S)
    nq = S // bq

    n_ci = len(carry.ins) if carry else 0
    n_co = len(carry.outs) if carry else 0

    def body(*refs):
        q_ref, kn_ref, v_ref, kr_ref, do_ref, o_ref, lse_ref, tab_ref = refs[:8]
        dq_ref, dkv_ref, dkr_ref = refs[8 + n_ci:11 + n_ci]
        dq_acc, dk_acc, dv_acc, kcat, delta = refs[11 + n_ci + n_co:16 + n_ci + n_co]
        c_ins, c_outs, c_sems = refs[8:8 + n_ci], refs[11 + n_ci:11 + n_ci + n_co], refs[16 + n_ci + n_co:]
        h = pl.program_id(0)
        if carry:
            @pl.when(h == 0)
            def _():
                carry.start(c_ins, c_outs, c_sems)

        dq_acc[...] = jnp.zeros_like(dq_acc)
        dk_acc[...] = jnp.zeros_like(dk_acc)
        dv_acc[...] = jnp.zeros_like(dv_acc)
        kcat[:, 0:128] = kn_ref[...]
        kcat[:, 128:256] = kr_ref[...]
        for r in range(nq):
            rows = slice(r * bq, (r + 1) * bq)
            delta[rows, :] = jnp.sum(do_ref[rows, :].astype(F32) * o_ref[rows, :].astype(F32), axis=1, keepdims=True)

        def pair(i, j, masked):
            rows_i = pl.ds(pl.multiple_of(i * bq, bq), bq)
            rows_j = pl.ds(pl.multiple_of(j * bq, bq), bq)
            qv, dov, k = q_ref[rows_i, :], do_ref[rows_i, :], kcat[rows_j, :]
            s = lax.dot_general(qv, k, NT, preferred_element_type=F32) * SCALE2
            if masked:
                s = jnp.where(_allowed(i * bq, j * bq, bq), s, -1e30)
            p = jnp.exp2(s - lse_ref[0, rows_i, :])
            dv_acc[rows_j, :] += lax.dot_general(p.astype(BF16), dov, TN, preferred_element_type=F32)
            dp = lax.dot_general(dov, v_ref[rows_j, :], NT, preferred_element_type=F32)
            ds = (p * (dp - delta[rows_i, :]) * ATTN_SCALE).astype(BF16)
            dk_acc[rows_j, :] += lax.dot_general(ds, qv, TN, preferred_element_type=F32)
            dq_acc[rows_i, :] += jnp.dot(ds, k, preferred_element_type=F32)

        def kv_step(j, _):
            pair(j, j, True)

            def q_step(i, _):
                pair(i, j, False)
                return 0

            lax.fori_loop(j + 1, nq, q_step, 0)
            return 0

        lax.fori_loop(0, nq, kv_step, 0)

        for r in range(nq):
            rows = slice(r * bq, (r + 1) * bq)
            dq_ref[rows, 0:128] = dq_acc[rows, 0:128].astype(BF16)
            dq_ref[rows, 128:256] = _rope(dq_acc[rows, 128:256], tab_ref[rows, :], -1).astype(BF16)
        dkv_ref[:, 0:128] = dk_acc[:, 0:128].astype(BF16)
        dkv_ref[:, 128:256] = dv_acc[...].astype(BF16)

        @pl.when(h == 0)
        def _():
            dkr_ref[...] = dk_acc[:, 128:256]

        @pl.when(h > 0)
        def _():
            dkr_ref[...] += dk_acc[:, 128:256]

        @pl.when(h == N_HEADS - 1)
        def _():
            for r in range(nq):
                rows = slice(r * bq, (r + 1) * bq)
                dkr_ref[rows, :] = _rope(dkr_ref[rows, :], tab_ref[rows, :], -1)
            if carry:
                carry.finish(c_ins, c_outs, c_sems)

    W = N_HEADS * QK_PAD
    res = pl.pallas_call(
        body, name="attn_bwd", grid=(N_HEADS,),
        in_specs=[pl.BlockSpec((S, QK_PAD), lambda h: (0, h)),
                  pl.BlockSpec((S, 128), lambda h: (0, 2 * h)),
                  pl.BlockSpec((S, 128), lambda h: (0, 2 * h + 1)),
                  pl.BlockSpec((S, 128), lambda h: (0, 0)),
                  pl.BlockSpec((S, V_HEAD), lambda h: (0, h)),
                  pl.BlockSpec((S, V_HEAD), lambda h: (0, h)),
                  pl.BlockSpec((1, S, 1), lambda h: (h, 0, 0)),
                  pl.BlockSpec((S, 384), lambda h: (0, 0))] + [ANY] * n_ci,
        out_specs=[pl.BlockSpec((S, QK_PAD), lambda h: (0, h)),
                   pl.BlockSpec((S, QK_PAD), lambda h: (0, h)),
                   pl.BlockSpec((S, 128), lambda h: (0, 0))] + [ANY] * n_co,
        out_shape=[jax.ShapeDtypeStruct((S, W), BF16), jax.ShapeDtypeStruct((S, W), BF16),
                   jax.ShapeDtypeStruct((S, 128), F32)] + (carry.outs if carry else []),
        scratch_shapes=[pltpu.VMEM((S, QK_PAD), F32), pltpu.VMEM((S, QK_PAD), F32), pltpu.VMEM((S, V_HEAD), F32),
                        pltpu.VMEM((S, QK_PAD), BF16), pltpu.VMEM((S, 1), F32)]
        + (carry.sems if carry else []),
        input_output_aliases=carry.io_aliases(8, 3) if carry else {},
        compiler_params=_params(("arbitrary",)),
    )(q, kv, kv, kr, do, o, lse, tab, *(carry.ins if carry else []))
    return res[0], res[1], res[2], res[3:]


def _shift_down(cur, prev, i, n):
    tm, h = cur.shape[0], prev.shape[0]
    prev = jnp.where(i == 0, jnp.zeros_like(prev), prev)
    full = jnp.concatenate([prev, cur], axis=0)
    return pltpu.roll(full, n, 0)[h:h + tm, :]


def _shift_up(cur, nxt, i, last, n):
    tm, h = cur.shape[0], nxt.shape[0]
    nxt = jnp.where(i == last, jnp.zeros_like(nxt), nxt)
    full = jnp.concatenate([cur, nxt], axis=0)
    return pltpu.roll(full, tm + h - n, 0)[0:tm, :]


def _conv_fwd(pc, w_conv):
    S, D = pc.shape[0], pc.shape[1] // 3
    tm = _pick(S, (256, 128))

    def body(i, ins, outs, accs):
        b_ref, c_ref, x_ref, cp_ref, xp_ref, w_ref = ins
        z = c_ref[...].astype(F32) * x_ref[...].astype(F32)
        zp = cp_ref[...].astype(F32) * xp_ref[...].astype(F32)
        cz = w_ref[0:1, :] * _shift_down(z, zp, i, 2) + w_ref[1:2, :] * _shift_down(z, zp, i, 1) + w_ref[2:3, :] * z
        outs[0][...] = (b_ref[...].astype(F32) * cz).astype(BF16)

    return _rows(body, "conv_fwd", S, tm,
                 [("row", pc, 0, D), ("row", pc, 1, D), ("row", pc, 2, D), ("prev", pc, 1, D), ("prev", pc, 2, D),
                  ("full", w_conv)], [(D, BF16)])[0]


def _conv_bwd(dhb, pc, w_conv):
    S, D = dhb.shape
    tm = _pick(S, (256, 128))
    last = S // tm - 1

    def body(i, ins, outs, accs):
        g_ref, b_ref, c_ref, x_ref, cp_ref, xp_ref, gn_ref, bn_ref, w_ref = ins
        w0, w1, w2 = w_ref[0:1, :], w_ref[1:2, :], w_ref[2:3, :]
        c, x, g = c_ref[...].astype(F32), x_ref[...].astype(F32), g_ref[...].astype(F32)
        z = c * x
        zp = cp_ref[...].astype(F32) * xp_ref[...].astype(F32)
        z1, z2 = _shift_down(z, zp, i, 1), _shift_down(z, zp, i, 2)
        cz = w0 * z2 + w1 * z1 + w2 * z
        dcz = g * b_ref[...].astype(F32)
        dczn = gn_ref[...].astype(F32) * bn_ref[...].astype(F32)
        dz = w2 * dcz + w1 * _shift_up(dcz, dczn, i, last, 1) + w0 * _shift_up(dcz, dczn, i, last, 2)
        outs[0][:, 0:D] = (g * cz).astype(BF16)
        outs[0][:, D:2 * D] = (dz * x).astype(BF16)
        outs[0][:, 2 * D:3 * D] = (dz * c).astype(BF16)
        dw = jnp.concatenate([jnp.sum(dcz * z2, axis=0, keepdims=True), jnp.sum(dcz * z1, axis=0, keepdims=True),
                              jnp.sum(dcz * z, axis=0, keepdims=True)], axis=0)
        _acc_add(i, accs[0], dw)

    return _rows(body, "conv_bwd", S, tm,
                 [("row", dhb, 0, D), ("row", pc, 0, D), ("row", pc, 1, D), ("row", pc, 2, D),
                  ("prev", pc, 1, D), ("prev", pc, 2, D), ("next", dhb, 0, D), ("next", pc, 0, D), ("full", w_conv)],
                 [(3 * D, BF16)], [(3, D)])


def _merge_fwd(y_a, y_b, pg):
    S, D = y_a.shape

    def body(i, ins, outs, accs):
        ya, yb, ga, gb = ins
        outs[0][...] = (_sigmoid(ga[...].astype(F32)) * ya[...].astype(F32)
                        + _sigmoid(gb[...].astype(F32)) * yb[...].astype(F32)).astype(BF16)

    return _rows(body, "merge_fwd", S, _pick(S, (256, 128)),
                 [("row", y_a, 0, D), ("row", y_b, 0, D), ("row", pg, 0, D), ("row", pg, 1, D)], [(D, BF16)])[0]


def _merge_bwd(dm, y_a, y_b, pg):
    S, D = dm.shape

    def body(i, ins, outs, accs):
        d, ya, yb = ins[0][...].astype(F32), ins[1][...].astype(F32), ins[2][...].astype(F32)
        sa, sb = _sigmoid(ins[3][...].astype(F32)), _sigmoid(ins[4][...].astype(F32))
        outs[0][...] = (d * sa).astype(BF16)
        outs[1][...] = (d * sb).astype(BF16)
        outs[2][:, 0:D] = (d * ya * (sa * (1.0 - sa))).astype(BF16)
        outs[2][:, D:2 * D] = (d * yb * (sb * (1.0 - sb))).astype(BF16)

    return _rows(body, "merge_bwd", S, _pick(S, (256, 128)),
                 [("row", dm, 0, D), ("row", y_a, 0, D), ("row", y_b, 0, D), ("row", pg, 0, D), ("row", pg, 1, D)],
                 [(D, BF16), (D, BF16), (2 * D, BF16)])


def _ln1_fwd(x, mix, gate1, g, b, scale2, shift2, carry=None):
    S, D = x.shape

    def body(i, ins, outs, accs):
        x_ref, mix_ref, gate_ref, g_ref, b_ref, sc_ref, sh_ref = ins
        xh, _ = _ln_stats(ALPHA * x_ref[...] + gate_ref[...] * mix_ref[...])
        x1 = xh * g_ref[...] + b_ref[...]
        outs[0][...] = x1
        outs[1][...] = (x1 * (1.0 + sc_ref[...]) + sh_ref[...]).astype(BF16)

    return _rows(body, "ln1_fwd", S, _pick(S, (256, 128)),
                 [("row", x, 0, D), ("row", mix, 0, D), ("full", gate1), ("full", g), ("full", b),
                  ("full", scale2), ("full", shift2)], [(D, F32), (D, BF16)], carry=carry)


def _swiglu_fwd(hh, carry=None):
    S, F = hh.shape[0], hh.shape[1] // 2

    def body(i, ins, outs, accs):
        hg = ins[0][...].astype(F32)
        outs[0][...] = (hg * _sigmoid(hg) * ins[1][...].astype(F32)).astype(BF16)

    res = _rows(body, "swiglu_fwd", S, _pick(S, (128,)), [("row", hh, 0, F), ("row", hh, 1, F)], [(F, BF16)], carry=carry)
    return (res[0][0], res[1]) if carry else res[0]


def _swiglu_bwd(dact, hh):
    S, F = dact.shape

    def body(i, ins, outs, accs):
        d, hg, hu = ins[0][...].astype(F32), ins[1][...].astype(F32), ins[2][...].astype(F32)
        sg = _sigmoid(hg)
        outs[0][:, 0:F] = (d * hu * (sg * (1.0 + hg * (1.0 - sg)))).astype(BF16)
        outs[0][:, F:2 * F] = (d * (hg * sg)).astype(BF16)

    return _rows(body, "swiglu_bwd", S, _pick(S, (128,)),
                 [("row", dact, 0, F), ("row", hh, 0, F), ("row", hh, 1, F)], [(2 * F, BF16)])[0]


def _ln2_loss_bwd(x1, ffn, gate2, g, b, target):
    S, D = x1.shape

    def body(i, ins, outs, accs):
        x1_ref, f_ref, gate_ref, g_ref, b_ref, t_ref = ins
        f = f_ref[...]
        xh, rstd = _ln_stats(ALPHA * x1_ref[...] + gate_ref[...] * f)
        e = xh * g_ref[...] + b_ref[...] - t_ref[...]
        dy = e * (1.0 / D)
        dr = _ln_bwd(dy * g_ref[...], xh, rstd)
        outs[0][...] = (gate_ref[...] * dr).astype(BF16)
        outs[1][...] = ALPHA * dr
        _acc_add(i, accs[0], jnp.full((1, 128), (0.5 / D) * jnp.sum(e * e), F32))
        _acc_add(i, accs[1], jnp.sum(dy * xh, axis=0, keepdims=True))
        _acc_add(i, accs[2], jnp.sum(dy, axis=0, keepdims=True))
        _acc_add(i, accs[3], jnp.sum(dr * f, axis=0, keepdims=True))

    return _rows(body, "ln2_loss_bwd", S, _pick(S, (256, 128)),
                 [("row", x1, 0, D), ("row", ffn, 0, D), ("full", gate2), ("full", g), ("full", b), ("row", target, 0, D)],
                 [(D, BF16), (D, F32)], [(1, 128), (1, D), (1, D), (1, D)])


def _ln1_bwd(x, mix, dx1a, du2, gate1, g, b, scale2):
    S, D = x.shape

    def body(i, ins, outs, accs):
        x_ref, mix_ref, da_ref, du_ref, gate_ref, g_ref, b_ref, sc_ref = ins
        mix, du = mix_ref[...], du_ref[...]
        xh, rstd = _ln_stats(ALPHA * x_ref[...] + gate_ref[...] * mix)
        x1 = xh * g_ref[...] + b_ref[...]
        dx1 = da_ref[...] + du * (1.0 + sc_ref[...])
        dr = _ln_bwd(dx1 * g_ref[...], xh, rstd)
        outs[0][...] = (gate_ref[...] * dr).astype(BF16)
        outs[1][...] = ALPHA * dr
        _acc_add(i, accs[0], jnp.sum(du, axis=0, keepdims=True))
        _acc_add(i, accs[1], jnp.sum(du * x1, axis=0, keepdims=True))
        _acc_add(i, accs[2], jnp.sum(dx1 * xh, axis=0, keepdims=True))
        _acc_add(i, accs[3], jnp.sum(dx1, axis=0, keepdims=True))
        _acc_add(i, accs[4], jnp.sum(dr * mix, axis=0, keepdims=True))

    return _rows(body, "ln1_bwd", S, _pick(S, (256, 128)),
                 [("row", x, 0, D), ("row", mix, 0, D), ("row", dx1a, 0, D), ("row", du2, 0, D),
                  ("full", gate1), ("full", g), ("full", b), ("full", scale2)],
                 [(D, BF16), (D, F32)], [(1, D)] * 5)


def _rms_bwd(d_rq, d_rkv, pq, dkr, g_q, g_kv):
    S = pq.shape[0]

    def body(i, ins, outs, accs):
        dq_ref, dkv_ref, pq_ref, dkr_ref, gq_ref, gkv_ref = ins

        def rms_bwd(dy, x, g):
            r = lax.rsqrt(jnp.mean(x * x, axis=-1, keepdims=True) + RMS_EPS)
            dyg = dy * g
            dx = r * dyg - x * (r * r * r) * jnp.mean(dyg * x, axis=-1, keepdims=True)
            return dx, jnp.sum(dy * (x * r), axis=0, keepdims=True)

        dxq, dgq = rms_bwd(dq_ref[...], pq_ref[:, 0:Q_LORA], gq_ref[...])
        dxkv, dgkv = rms_bwd(dkv_ref[...], pq_ref[:, Q_LORA:Q_LORA + KV_LORA], gkv_ref[...])
        outs[0][:, 0:Q_LORA] = dxq.astype(BF16)
        outs[0][:, Q_LORA:Q_LORA + KV_LORA] = dxkv.astype(BF16)
        outs[0][:, Q_LORA + KV_LORA:QKV_A] = dkr_ref[...].astype(BF16)
        _acc_add(i, accs[0], dgq)
        _acc_add(i, accs[1], dgkv)

    return _rows(body, "rms_bwd", S, _pick(S, (256, 128)),
                 [("row", d_rq, 0, Q_LORA), ("row", d_rkv, 0, KV_LORA), ("row", pq, 0, QKV_A), ("row", dkr, 0, 128),
                  ("full", g_q), ("full", g_kv)], [(QKV_A, BF16)], [(1, Q_LORA), (1, KV_LORA)])


def _dx_final(dxa, du, x, scale1):
    S, D = x.shape

    def body(i, ins, outs, accs):
        du = ins[1][...]
        outs[0][...] = ins[0][...] + du * (1.0 + ins[3][...])
        _acc_add(i, accs[0], jnp.sum(du, axis=0, keepdims=True))
        _acc_add(i, accs[1], jnp.sum(du * ins[2][...], axis=0, keepdims=True))

    return _rows(body, "dx_final", S, _pick(S, (256, 128)),
                 [("row", dxa, 0, D), ("row", du, 0, D), ("row", x, 0, D), ("full", scale1)],
                 [(D, F32)], [(1, D), (1, D)])


def _ada_fwd(c_all, w, bias):
    B, D = c_all.shape
    NA = w.shape[1]
    tn = _pick(NA, (512, 256, 128))

    def body(c_ref, w_ref, b_ref, o_ref):
        cv = c_ref[...]
        ca = (cv * _sigmoid(cv)).astype(BF16)
        o_ref[...] = jnp.dot(ca, w_ref[...].astype(BF16), preferred_element_type=F32) + b_ref[...]

    return pl.pallas_call(
        body, name="ada_fwd", grid=(NA // tn,),
        in_specs=[pl.BlockSpec((B, D), lambda j: (0, 0)), pl.BlockSpec((D, tn), lambda j: (0, j)),
                  pl.BlockSpec((1, tn), lambda j: (0, j))],
        out_specs=pl.BlockSpec((B, tn), lambda j: (0, j)),
        out_shape=jax.ShapeDtypeStruct((B, NA), F32),
        compiler_params=_params(("arbitrary",)),
    )(c_all, w, bias)


def _ada_bwd(c_all, dmod):
    B, D = c_all.shape
    NA = dmod.shape[1]
    tn = _pick(NA, (512, 256, 128))

    def body(c_ref, d_ref, o_ref):
        cv = c_ref[...]
        ca = (cv * _sigmoid(cv)).astype(BF16)
        o_ref[...] = lax.dot_general(ca, d_ref[...].astype(BF16), TN, preferred_element_type=F32)

    return pl.pallas_call(
        body, name="ada_bwd", grid=(NA // tn,),
        in_specs=[pl.BlockSpec((B, D), lambda j: (0, 0)), pl.BlockSpec((B, tn), lambda j: (0, j))],
        out_specs=pl.BlockSpec((D, tn), lambda j: (0, j)),
        out_shape=jax.ShapeDtypeStruct((D, NA), F32),
        compiler_params=_params(("arbitrary",)),
    )(c_all, dmod)


def _pack_rows(parts, n_rows, after=()):
    N = parts[0].shape[1]
    n = len(parts)

    def body(*refs):
        o_ref = refs[-1]
        o_ref[...] = jnp.zeros_like(o_ref)
        at = 0
        for r in refs[:n]:
            o_ref[at:at + r.shape[0], :] = r[...]
            at += r.shape[0]

    vmem = pl.BlockSpec(memory_space=pltpu.VMEM)
    return pl.pallas_call(body, name="pack_small", out_shape=jax.ShapeDtypeStruct((n_rows, N), F32),
                          in_specs=[vmem] * n + [ANY] * len(after), out_specs=vmem,
                          compiler_params=_params())(*parts, *after)


def _sum8(parts):
    _, R, N = parts.shape

    def body(p_ref, o_ref):
        acc = p_ref[0]
        for d in range(1, 8):
            acc = acc + p_ref[d]
        o_ref[...] = acc

    return pl.pallas_call(body, name="sum8", out_shape=jax.ShapeDtypeStruct((R, N), F32),
                          compiler_params=_params())(parts)


def _adam_math(w, g, m, v):
    m = ADAM_B1 * m + (1.0 - ADAM_B1) * g
    v = ADAM_B2 * v + (1.0 - ADAM_B2) * (g * g)
    delta = -ADAM_LR * ((m / ADAM_C1) / (jnp.sqrt(v / ADAM_C2) + ADAM_EPS) + ADAM_WD * w)
    return delta, m, v


def _adam(name, w, m, v, g, carry=None):
    R, C = w.shape
    tm = _row_tile(R, C * 4, 1 << 20)
    steps = R // tm
    n_ci = len(carry.ins) if carry else 0
    n_co = len(carry.outs) if carry else 0

    def body(*refs):
        w_ref, m_ref, v_ref, g_ref = refs[:4]
        d_ref, nm_ref, nv_ref = refs[4 + n_ci:7 + n_ci]
        c_ins, c_outs, c_sems = refs[4:4 + n_ci], refs[7 + n_ci:7 + n_ci + n_co], refs[7 + n_ci + n_co:]
        if carry:
            @pl.when(pl.program_id(0) == 0)
            def _():
                carry.start(c_ins, c_outs, c_sems)

        delta, nm, nv = _adam_math(w_ref[...], g_ref[...], m_ref[...], v_ref[...])
        d_ref[...] = delta
        nm_ref[...] = nm
        nv_ref[...] = nv
        if carry:
            @pl.when(pl.program_id(0) == steps - 1)
            def _():
                carry.finish(c_ins, c_outs, c_sems)

    spec = pl.BlockSpec((tm, C), lambda i: (i, 0))
    res = pl.pallas_call(
        body, name=name, grid=(steps,), in_specs=[spec] * 4 + [ANY] * n_ci, out_specs=[spec] * 3 + [ANY] * n_co,
        out_shape=[jax.ShapeDtypeStruct((R, C), F32)] * 3 + (carry.outs if carry else []),
        scratch_shapes=carry.sems if carry else [],
        input_output_aliases=carry.io_aliases(4, 3) if carry else {},
        compiler_params=_params(("arbitrary",)),
    )(w, m, v, g, *(carry.ins if carry else []))
    return (res[:3], res[3:]) if carry else res


def _adam_halves(name, w, m, v, mine, other, core, carry=None):
    R, C = w.shape
    Rh = mine.shape[0]
    tc = max(t for t in range(128, C + 1, 128) if C % t == 0 and R * t <= (3 << 17))
    steps = C // tc
    n_ci = len(carry.ins) if carry else 0
    n_co = len(carry.outs) if carry else 0

    def body(*refs):
        c_ref, w_ref, m_ref, v_ref, a_ref, b_ref = refs[:6]
        g_ref, d_ref, nm_ref, nv_ref = refs[6 + n_ci:10 + n_ci]
        c_ins, c_outs, c_sems = refs[6:6 + n_ci], refs[10 + n_ci:10 + n_ci + n_co], refs[10 + n_ci + n_co:]
        if carry:
            @pl.when(pl.program_id(0) == 0)
            def _():
                carry.start(c_ins, c_outs, c_sems)

        first = c_ref[0] == 0
        g = jnp.concatenate([jnp.where(first, a_ref[...], b_ref[...]),
                             jnp.where(first, b_ref[0:R - Rh, :], a_ref[0:R - Rh, :])], axis=0)
        delta, nm, nv = _adam_math(w_ref[...], g, m_ref[...], v_ref[...])
        g_ref[...] = g
        d_ref[...] = delta
        nm_ref[...] = nm
        nv_ref[...] = nv
        if carry:
            @pl.when(pl.program_id(0) == steps - 1)
            def _():
                carry.finish(c_ins, c_outs, c_sems)

    spec = pl.BlockSpec((R, tc), lambda i, c_ref: (0, i))
    h_spec = pl.BlockSpec((Rh, tc), lambda i, c_ref: (0, i))
    res = pl.pallas_call(
        body, name=name, out_shape=[jax.ShapeDtypeStruct((R, C), F32)] * 4 + (carry.outs if carry else []),
        grid_spec=pltpu.PrefetchScalarGridSpec(
            num_scalar_prefetch=1, grid=(steps,), in_specs=[spec, spec, spec, h_spec, h_spec] + [ANY] * n_ci,
            out_specs=[spec] * 4 + [ANY] * n_co, scratch_shapes=carry.sems if carry else []),
        input_output_aliases=carry.io_aliases(6, 4) if carry else {},
        compiler_params=_params(("arbitrary",)),
    )(core, w, m, v, mine, other, *(carry.ins if carry else []))
    return (res[:4], res[4:]) if carry else res


def _adam_small(name, w, m, v, g):
    def body(w_ref, m_ref, v_ref, g_ref, d_ref, nm_ref, nv_ref):
        delta, nm, nv = _adam_math(w_ref[...], g_ref[...], m_ref[...], v_ref[...])
        d_ref[...] = delta
        nm_ref[...] = nm
        nv_ref[...] = nv

    return pl.pallas_call(body, name=name, out_shape=[jax.ShapeDtypeStruct(w.shape, F32)] * 3,
                          compiler_params=_params())(w, m, v, g)


def _place():
    return lax.axis_index("x"), lax.axis_index("y"), lax.axis_index("c")


def _other_chips(x, y):
    return [(1 - x, y), (x, 1 - y), (1 - x, 1 - y)]


def _all_gather8(blk, name):
    R, N = blk.shape

    def body(x_ref, out_ref, send_sems, recv_sems, local_sem):
        x, y, c = _place()
        me = 4 * x + 2 * y + c
        mine = pltpu.make_async_copy(x_ref, out_ref.at[me], local_sem)
        mine.start()
        flips = [(j >> 2 & 1, j >> 1 & 1, j & 1) for j in range(1, 8)]
        peers = [((1 - x) if fx else x, (1 - y) if fy else y, (1 - c) if fc else c) for fx, fy, fc in flips]
        sends = []
        for j, peer in enumerate(peers):
            cp = pltpu.make_async_remote_copy(src_ref=x_ref, dst_ref=out_ref.at[me], send_sem=send_sems.at[j],
                                              recv_sem=recv_sems.at[j], device_id=peer, device_id_type=MESH)
            cp.start()
            sends.append(cp)
        for j, (px, py, pc) in enumerate(peers):
            pltpu.make_async_remote_copy(src_ref=x_ref, dst_ref=out_ref.at[4 * px + 2 * py + pc],
                                         send_sem=send_sems.at[j], recv_sem=recv_sems.at[j],
                                         device_id=(px, py, pc), device_id_type=MESH).wait_recv()
        for cp in sends:
            cp.wait_send()
        mine.wait()

    return pl.pallas_call(
        body, name=name, out_shape=jax.ShapeDtypeStruct((8, R, N), F32),
        in_specs=[pl.BlockSpec(memory_space=pltpu.VMEM)], out_specs=pl.BlockSpec(memory_space=pltpu.VMEM),
        scratch_shapes=[pltpu.SemaphoreType.DMA((7,)), pltpu.SemaphoreType.DMA((7,)), pltpu.SemaphoreType.DMA],
        compiler_params=_params(),
    )(blk)


def _piece(rows, piece):
    i, n, k = piece if len(piece) == 3 else (piece[0], piece[1], 1)
    assert rows % 16 == 0 and rows // 16 >= n, (rows, piece)
    lo, hi = (rows // 16 * i // n) * 16, (rows // 16 * (i + k) // n) * 16
    return pl.ds(lo, hi - lo)


def _gather_plan(shards, piece=(0, 1), into=None, ici=True):
    n = len(shards)

    def parts(ins, outs, sems):
        s1, r1, s2, r2, loc = sems
        x, y, c = _place()
        me = 2 * x + y
        chips = _other_chips(x, y)
        sib = (x, y, 1 - c)

        def rows(k):
            return _piece(shards[k].shape[1], piece)

        def ici_copy(k, j, slab, to):
            return pltpu.make_async_remote_copy(src_ref=ins[k].at[c, rows(k)], dst_ref=outs[k].at[slab, c, rows(k)],
                                                send_sem=s1.at[3 * k + j], recv_sem=r1.at[3 * k + j],
                                                device_id=to, device_id_type=MESH)

        def d2d(k, j, slab, half):
            return pltpu.make_async_remote_copy(src_ref=outs[k].at[slab, half, rows(k)],
                                                dst_ref=outs[k].at[slab, half, rows(k)],
                                                send_sem=s2.at[3 * k + j], recv_sem=r2.at[3 * k + j],
                                                device_id=sib, device_id_type=MESH)

        def own(k):
            return pltpu.make_async_remote_copy(src_ref=ins[k].at[:, rows(k)], dst_ref=outs[k].at[me, :, rows(k)],
                                                send_sem=loc.at[2 * k], recv_sem=loc.at[2 * k + 1],
                                                device_id=sib, device_id_type=MESH)

        return c, me, chips, ici_copy, d2d, own

    def start(ins, outs, sems):
        c, me, chips, ici_copy, d2d, own = parts(ins, outs, sems)
        for k in range(n):
            for j, (px, py) in enumerate(chips):
                (ici_copy(k, j, me, (px, py, c)) if ici else d2d(k, j, 2 * px + py, c)).start()
        for k in range(n):
            own(k).start()

    def finish(ins, outs, sems):
        c, me, chips, ici_copy, d2d, own = parts(ins, outs, sems)
        if ici:
            for k in range(n):
                for j, (px, py) in enumerate(chips):
                    ici_copy(k, j, 2 * px + py, (px, py, c)).wait_recv()
                    d2d(k, j, 2 * px + py, c).start()
        for k in range(n):
            for j, (px, py) in enumerate(chips):
                d2d(k, j, 2 * px + py, 1 - c).wait_recv()
        for k in range(n):
            own(k).wait()
            for j, (px, py) in enumerate(chips):
                if ici:
                    ici_copy(k, j, me, (px, py, c)).wait_send()
                d2d(k, j, 2 * px + py, c).wait_send()

    return _Plan(list(shards) + list(into or []), [jax.ShapeDtypeStruct((4,) + a.shape, a.dtype) for a in shards],
                 [pltpu.SemaphoreType.DMA((3 * n,))] * 4 + [pltpu.SemaphoreType.DMA((2 * n,))], start, finish,
                 aliases={n + k: k for k in range(n)} if into else None)


def _pair_plan(parts):
    n = len(parts)

    def copies(ins, outs, sems):
        send_sems, recv_sems = sems
        x, y, c = _place()
        return [pltpu.make_async_remote_copy(src_ref=ins[k].at[p, 1 - c], dst_ref=outs[k].at[p],
                                             send_sem=send_sems.at[4 * k + p], recv_sem=recv_sems.at[4 * k + p],
                                             device_id=(x, y, 1 - c), device_id_type=MESH)
                for k in range(n) for p in range(4)]

    def start(ins, outs, sems):
        for cp in copies(ins, outs, sems):
            cp.start()

    def finish(ins, outs, sems):
        for cp in copies(ins, outs, sems):
            cp.wait()

    return _Plan(parts, [jax.ShapeDtypeStruct((4,) + a.shape[2:], a.dtype) for a in parts],
                 [pltpu.SemaphoreType.DMA((4 * n,))] * 2, start, finish)


def _sibling_plan(arrs):
    n = len(arrs)

    def copies(ins, outs, sems):
        send_sems, recv_sems = sems
        x, y, c = _place()
        return [pltpu.make_async_remote_copy(src_ref=ins[k], dst_ref=outs[k], send_sem=send_sems.at[k],
                                             recv_sem=recv_sems.at[k], device_id=(x, y, 1 - c), device_id_type=MESH)
                for k in range(n)]

    def start(ins, outs, sems):
        for cp in copies(ins, outs, sems):
            cp.start()

    def finish(ins, outs, sems):
        for cp in copies(ins, outs, sems):
            cp.wait()

    return _Plan(arrs, [jax.ShapeDtypeStruct(a.shape, a.dtype) for a in arrs],
                 [pltpu.SemaphoreType.DMA((n,))] * 2, start, finish)


def _scatter_copies(arrs):
    def copies(ins, land, send_sems, recv_sems):
        x, y, c = _place()
        return [pltpu.make_async_remote_copy(src_ref=ins[k].at[2 * px + py], dst_ref=land[k].at[j],
                                             send_sem=send_sems.at[3 * k + j], recv_sem=recv_sems.at[3 * k + j],
                                             device_id=(px, py, c), device_id_type=MESH)
                for k in range(len(arrs)) for j, (px, py) in enumerate(_other_chips(x, y))]

    return copies, [lax.empty((3,) + a.shape[1:], a.dtype) for a in arrs]


def _gather_copies(shards, piece=(0, 1), lands=None):
    def copies(ins, land, send_sems, recv_sems):
        x, y, c = _place()
        return [pltpu.make_async_remote_copy(
                    src_ref=ins[k].at[c, _piece(shards[k].shape[1], piece)],
                    dst_ref=land[k].at[2 * x + y, c, _piece(shards[k].shape[1], piece)],
                    send_sem=send_sems.at[3 * k + j], recv_sem=recv_sems.at[3 * k + j],
                    device_id=(px, py, c), device_id_type=MESH)
                for k in range(len(shards)) for j, (px, py) in enumerate(_other_chips(x, y))]

    return copies, list(lands) if lands is not None else [lax.empty((4,) + a.shape, a.dtype) for a in shards]


def _split_start(arrs, copies_lands, ride, name, after=()):
    copies, lands = copies_lands
    n = len(arrs)
    rides = list(ride) if isinstance(ride, (list, tuple)) else [ride]
    n_thru = 2 * n + len(rides)

    def body(*refs):
        first_out = n_thru + len(after)
        for cp in copies(refs[:n], refs[n:2 * n], refs[first_out], refs[first_out + 1]):
            cp.start()

    hbm = [pltpu.with_memory_space_constraint(a, pltpu.HBM) for a in list(arrs) + lands + rides]
    res = pl.pallas_call(
        body, name=name,
        out_shape=[pltpu.SemaphoreType.DMA((3 * n,)), pltpu.SemaphoreType.DMA((3 * n,))]
        + [pltpu.HBM(a.shape, a.dtype) for a in hbm],
        in_specs=[HBM_SPEC] * n_thru + [ANY] * len(after),
        out_specs=[SEM_SPEC, SEM_SPEC] + [HBM_SPEC] * n_thru,
        input_output_aliases={i: 2 + i for i in range(n_thru)},
        compiler_params=pltpu.CompilerParams(has_side_effects=pltpu.SideEffectType.DATAFLOW_SIDE_EFFECTING),
    )(*hbm, *after)
    return res[0], res[1], res[2:2 + n], res[2 + n:2 + 2 * n], list(res[2 + 2 * n:])


def _split_wait(started, copies_lands, after, name):
    send_sems, recv_sems, arrs, lands, _ = started
    copies = copies_lands[0]
    n = len(arrs)

    def body(*refs):
        for cp in copies(refs[:n], refs[n:2 * n], refs[2 * n], refs[2 * n + 1]):
            cp.wait_send()
            cp.wait_recv()

    res = pl.pallas_call(
        body, name=name, out_shape=[pltpu.HBM(a.shape, a.dtype) for a in list(arrs) + list(lands)],
        in_specs=[HBM_SPEC] * (2 * n) + [SEM_SPEC, SEM_SPEC] + [ANY] * len(after), out_specs=[HBM_SPEC] * (2 * n),
        input_output_aliases={i: i for i in range(2 * n)},
        compiler_params=pltpu.CompilerParams(has_side_effects=pltpu.SideEffectType.DATAFLOW_SIDE_EFFECTING),
    )(*arrs, *lands, send_sems, recv_sems, *after)
    return list(res[:n]), list(res[n:])


def _add_pair(parts, sib, core, name):
    P4, _, Rh, C = parts.shape
    tm, tc = _tile2(Rh, C, 16)

    def body(c_ref, a_ref, b_ref, o_ref):
        o_ref[...] = (a_ref[0].astype(F32) + b_ref[...].astype(F32)).astype(BF16)

    spec = pl.BlockSpec((1, tm, tc), lambda p, i, j, c_ref: (p, i, j))
    return pl.pallas_call(
        body, name=name, out_shape=jax.ShapeDtypeStruct((P4, Rh, C), BF16),
        grid_spec=pltpu.PrefetchScalarGridSpec(
            num_scalar_prefetch=1, grid=(P4, Rh // tm, C // tc),
            in_specs=[pl.BlockSpec((1, 1, tm, tc), lambda p, i, j, c_ref: (p, c_ref[0], i, j)), spec], out_specs=spec),
        compiler_params=_params(("parallel",) * 3),
    )(core, parts, sib)


def _sum_slabs(pre, recv, chip, name):
    _, Rh, C = pre.shape
    tm, tc = _tile2(Rh, C, 16)

    def body(me_ref, own_ref, r_ref, o_ref):
        acc = own_ref[0].astype(F32)
        for j in range(3):
            acc = acc + r_ref[j].astype(F32)
        o_ref[...] = acc

    return pl.pallas_call(
        body, name=name, out_shape=jax.ShapeDtypeStruct((Rh, C), F32),
        grid_spec=pltpu.PrefetchScalarGridSpec(
            num_scalar_prefetch=1, grid=(Rh // tm, C // tc),
            in_specs=[pl.BlockSpec((1, tm, tc), lambda i, j, me_ref: (me_ref[0], i, j)),
                      pl.BlockSpec((3, tm, tc), lambda i, j, me_ref: (0, i, j))],
            out_specs=pl.BlockSpec((tm, tc), lambda i, j, me_ref: (i, j))),
        compiler_params=_params(("parallel", "parallel")),
    )(chip, pre, recv)


def kernel(x, c, positions, w_ada, b_ada, w_in, g_q_a, w_q_b, g_kv_a, w_kv_b, w_o_a, w_conv, w_o_b, w_o, ln1_g, ln1_b, w_ffn_in, w_ffn_out, ln2_g, ln2_b, loss_target, m_w_ada, m_b_ada, m_w_in, m_g_q_a, m_w_q_b, m_g_kv_a, m_w_kv_b, m_w_o_a, m_w_conv, m_w_o_b, m_w_o, m_ln1_g, m_ln1_b, m_w_ffn_in, m_w_ffn_out, m_ln2_g, m_ln2_b, v_w_ada, v_b_ada, v_w_in, v_g_q_a, v_w_q_b, v_g_kv_a, v_w_kv_b, v_w_o_a, v_w_conv, v_w_o_b, v_w_o, v_ln1_g, v_ln1_b, v_w_ffn_in, v_w_ffn_out, v_ln2_g, v_ln2_b):
    S, D = x.shape[1], x.shape[2]
    F = w_ffn_out.shape[1] * 4
    ax, ay, ac = _place()
    chip = 2 * ax + ay
    dev = 4 * ax + 2 * ay + ac
    x2, tgt = x[0], loss_target[0]
    w_ada2, w_in2, w_q_b2, w_kv_b2 = w_ada[0], w_in[0], w_q_b[0], w_kv_b[0]
    w_o_a2, w_o_b2, w_o2, w_ffn_in2, w_ffn_out2 = w_o_a[0], w_o_b[0], w_o[0], w_ffn_in[0], w_ffn_out[0]
    NA = w_ada2.shape[1]
    CW = w_conv.shape[2]

    inv_freq = 1.0 / (ROPE_THETA ** (jnp.arange(0, QK_ROPE, 2, dtype=F32) / QK_ROPE))
    ang = positions[0].astype(F32)[:, None] * inv_freq
    cos, sin = jnp.cos(ang), jnp.sin(ang)
    z32, z64, z96 = jnp.zeros((S, 32), F32), jnp.zeros((S, 64), F32), jnp.zeros((S, 96), F32)
    tab = jnp.concatenate([cos, cos, z64, -sin, z96, z32, sin, z64], axis=1)

    def halves(a):
        return a.reshape(2, a.shape[0] // 2, a.shape[1])

    def whole(g):
        return g.reshape(4, 2 * g.shape[2], g.shape[3])

    def cols(g):
        return jnp.transpose(g, (1, 0, 2)).reshape(g.shape[1], 4 * g.shape[2])

    w_inT, m_w_inT, v_w_inT = w_in2.T, m_w_in[0].T, v_w_in[0].T
    CS = w_inT.shape[0]
    CSP = -(-CS // 32) * 32
    sh_in = halves(jnp.pad(w_inT.astype(BF16), ((0, CSP - CS), (0, 0))))
    c_all = _all_gather8(c, "gather_c").reshape(8, D)
    wconv_all = _all_gather8(w_conv[0], "gather_wconv")
    w_conv_full = jnp.transpose(wconv_all[0::2], (1, 0, 2)).reshape(3, D)
    b_sh = lax.dynamic_slice(b_ada, (0, chip * NA), (1, NA))
    mod_sh = _ada_fwd(c_all, w_ada2, b_sh)
    mod_all = _all_gather8(mod_sh, "gather_mod")
    mod = lax.dynamic_slice(mod_all[0::2], (0, dev, 0), (4, 1, NA)).reshape(6, D)
    shift1, scale1, gate1, shift2, scale2, gate2 = (mod[k:k + 1] for k in range(6))

    n_pc = 4
    st_in, cl_in, sh_in_t, l_in, rides = [], [], [sh_in], None, [shift1, w_conv_full]
    for i in range(n_pc):
        cl_in.append(_gather_copies(sh_in_t, (i, n_pc), l_in))
        st_in.append(_split_start(sh_in_t, cl_in[i], rides, "gather_in%d_start" % i))
        sh_in_t, l_in, rides = st_in[i][2], st_in[i][3], st_in[i][4]
    shift1, w_conv_full = rides
    others = lax.optimization_barrier((w_q_b2, w_kv_b2, w_o_a2, w_o_b2, w_o2, w_ffn_in2, w_ffn_out2, shift1))
    sh_qb, sh_kvb, sh_oa, sh_ob, sh_o, sh_fi, sh_fo = (halves(w.astype(BF16)) for w in others[:7])
    shift1 = others[7]
    for i in range(n_pc):
        casts = [sh_qb, sh_kvb, sh_oa, sh_ob, sh_o, sh_fi, sh_fo] if i == 0 else []
        sh_in_t, l_in = _split_wait(st_in[i][:2] + (sh_in_t, l_in, None), cl_in[i], casts, "gather_in%d_wait" % i)
        if i < n_pc - 1:
            l_in = _run_plan(_gather_plan(sh_in_t, (i, n_pc), into=l_in, ici=False), "handon_in%d" % i)
    sh_a1, sh_a2 = [sh_qb, sh_kvb], [sh_oa, sh_ob, sh_o]
    cl_a1, cl_a2, cl_fi, cl_fo = (_gather_copies(g) for g in (sh_a1, sh_a2, [sh_fi], [sh_fo]))
    st_a1 = _split_start(sh_a1, cl_a1, shift1, "gather_a1_start", after=[l_in[0]])
    st_a2 = _split_start(sh_a2, cl_a2, st_a1[4], "gather_a2_start")
    u, (g_in,) = _modulate(x2, scale1, st_a2[4][0], "modulate1",
                           carry=_gather_plan(sh_in_t, (n_pc - 1, n_pc), into=l_in, ici=False))
    g_in = whole(g_in)

    def in_rows(lo, hi):
        parts = [g_in[p, max(lo, p * CS) - p * CS:min(hi, (p + 1) * CS) - p * CS]
                 for p in range(4) if max(lo, p * CS) < min(hi, (p + 1) * CS)]
        return parts[0] if len(parts) == 1 else jnp.concatenate(parts, axis=0)

    n_qkv = Q_LORA + KV_LORA + QK_ROPE
    W_qkvT = jnp.pad(in_rows(0, n_qkv), ((0, QKV_A - n_qkv), (0, 0)))
    W_convT = in_rows(n_qkv, n_qkv + 3 * D)
    W_gateT = in_rows(n_qkv + 3 * D, n_qkv + 5 * D)

    pq = _matmul(u, W_qkvT, "nt", F32, "proj_qkv")
    pc = _matmul(u, W_convT, "nt", BF16, "proj_conv")
    sh_a1, la1 = _split_wait(st_a1, cl_a1, [pc], "gather_a1_wait")
    pg, (g_qb, g_kvb) = _matmul(u, W_gateT, "nt", BF16, "proj_gate", carry=_gather_plan(sh_a1, into=la1, ici=False))
    st_fi = _split_start([sh_fi], cl_fi, g_q_a, "gather_fi_start", after=[pg])
    W_qb = jnp.pad(cols(whole(g_qb)).reshape(Q_LORA, N_HEADS, QK_NOPE + QK_ROPE),
                   ((0, 0), (0, 0), (0, QK_PAD - QK_NOPE - QK_ROPE))).reshape(Q_LORA, N_HEADS * QK_PAD)
    W_kvb = cols(whole(g_kvb))
    rq, rkv, kr = _rms_fwd(pq, tab, st_fi[4][0], g_kv_a)
    kv = _matmul(rkv, W_kvb, "nn", BF16, "kv_b")
    sh_a2, la2 = _split_wait(st_a2, cl_a2, [kv], "gather_a2_wait")
    def rope_heads(r, t):
        return jnp.concatenate([r[:, lo:lo + 128] if lo % QK_PAD == 0 else _rope(r[:, lo:lo + 128], t, 1)
                                for lo in range(0, r.shape[1], 128)], axis=1)

    q, (g_oa, g_ob, g_o) = _matmul(rq, W_qb, "nn", BF16, "q_b", carry=_gather_plan(sh_a2, into=la2, ici=False),
                                   finish=(rope_heads, tab))
    o, lse = _attn_fwd(q, kv, kr)
    W_oa, W_ob, W_o = (g.reshape(-1, D) for g in (g_oa, g_ob, g_o))
    hb = _conv_fwd(pc, w_conv_full)
    st_fo = _split_start([sh_fo], cl_fo, ln1_g, "gather_fo_start", after=[o])
    y_b = _matmul(hb, W_ob, "nn", BF16, "o_b")
    y_a = _matmul(o, W_oa, "nn", BF16, "o_a")
    merged = _merge_fwd(y_a, y_b, pg)
    sh_fi_t, lfi = _split_wait(st_fi, cl_fi, [merged], "gather_fi_wait")
    mix, g_fi = _matmul(merged, W_o, "nn", F32, "w_o", carry=_gather_plan(sh_fi_t, (0, 2), into=lfi, ici=False))
    (x1, u2), (g_fi,) = _ln1_fwd(x2, mix, gate1, st_fo[4][0], ln1_b, scale2, shift2,
                                 carry=_gather_plan(sh_fi_t, (1, 2), into=g_fi, ici=False))
    W_fi = whole(g_fi)
    hh = _matmul(u2, W_fi, "nn", BF16, "ffn_in", shards="b")
    sh_fo_t, lfo = _split_wait(st_fo, cl_fo, [hh], "gather_fo_wait")
    act, (g_fo,) = _swiglu_fwd(hh, carry=_gather_plan(sh_fo_t, into=lfo, ici=False))
    W_fo = g_fo.reshape(F, D)
    ffn = _matmul(act, W_fo, "nn", F32, "ffn_out")

    core_i = ac.astype(jnp.int32).reshape(1)
    chip_i = chip.astype(jnp.int32).reshape(1)

    def uncols(g):
        return jnp.transpose(g.reshape(g.shape[0], 4, g.shape[1] // 4), (1, 0, 2))

    def slabs(p):
        return p.reshape(4, 2, p.shape[1] // 2, p.shape[2])

    def add_pairs(parts, sibs, nms):
        return [_add_pair(a, b, core_i, "add_pair_" + nm) for a, b, nm in zip(parts, sibs, nms)]

    def sum_all(pre, recv, nms):
        return [_sum_slabs(a, r, chip_i, "sum_slabs_" + nm) for a, r, nm in zip(pre, recv, nms)]

    dffn, dx1a, loss_acc, d_ln2_g, d_ln2_b, d_gate2 = _ln2_loss_bwd(x1, ffn, gate2, ln2_g, ln2_b, tgt)
    dW_fo = _matmul(act, dffn, "tn", BF16, "d_w_ffn_out")
    p_fo = [slabs(dW_fo.reshape(4, -1, D))]
    dact, s_fo = _matmul(dffn, W_fo, "nt", BF16, "d_act", carry=_pair_plan(p_fo))
    pre_fo = add_pairs(p_fo, s_fo, ["w_ffn_out"])
    cs_fo = _scatter_copies(pre_fo)
    st_sfo = _split_start(pre_fo, cs_fo, scale2, "scatter_fo_start")
    dhh = _swiglu_bwd(dact, hh)
    dW_fi = _matmul(u2, dhh, "tn", BF16, "d_w_ffn_in", shards="o")
    p_fi = [slabs(dW_fi)]
    du2, s_fi = _matmul(dhh, W_fi, "nt", F32, "d_u2", carry=_pair_plan(p_fi), shards="b")
    pre_fi = add_pairs(p_fi, s_fi, ["w_ffn_in"])
    cs_fi = _scatter_copies(pre_fi)
    st_sfi = _split_start(pre_fi, cs_fi, st_sfo[4], "scatter_fi_start")
    dmix, dxa, d_shift2, d_scale2, d_ln1_g, d_ln1_b, d_gate1 = _ln1_bwd(x2, mix, dx1a, du2, gate1, ln1_g, ln1_b, st_sfi[4][0])
    dW_o = _matmul(merged, dmix, "tn", BF16, "d_w_o")
    dmerged = _matmul(dmix, W_o, "nt", BF16, "d_merged")
    dy_a, dy_b, dgate = _merge_bwd(dmerged, y_a, y_b, pg)
    dW_oa = _matmul(o, dy_a, "tn", BF16, "d_w_o_a")
    do = _matmul(dy_a, W_oa, "nt", BF16, "d_o")
    dW_ob = _matmul(hb, dy_b, "tn", BF16, "d_w_o_b")
    p_mid = [slabs(g.reshape(4, -1, D)) for g in (dW_oa, dW_ob, dW_o)]
    dhb, s_mid = _matmul(dy_b, W_ob, "nt", BF16, "d_hb", carry=_pair_plan(p_mid))
    pre_mid = add_pairs(p_mid, s_mid, ["w_o_a", "w_o_b", "w_o"])
    cs_mid = _scatter_copies(pre_mid)
    st_smid = _split_start(pre_mid, cs_mid, w_conv_full, "scatter_mid_start")
    dconv, d_wconv = _conv_bwd(dhb, pc, st_smid[4][0])
    dq, dkv, dkr, _ = _attn_bwd(q, kv, kr, do, o, lse, tab, carry=_token_plan(st_smid[4][0]))
    names_a = ["w_ffn_out", "w_ffn_in", "w_o_a", "w_o_b", "w_o"]
    dW_qb = _matmul(rq, dq, "tn", BF16, "d_w_q_b")
    d_rq = _matmul(dq, W_qb, "nt", F32, "d_rq")
    dW_kvb = _matmul(rkv, dkv, "tn", BF16, "d_w_kv_b")
    d_rkv = _matmul(dkv, W_kvb, "nt", F32, "d_rkv")
    dqkv, d_g_q, d_g_kv = _rms_bwd(d_rq, d_rkv, pq, dkr, g_q_a, g_kv_a)
    dW_qkvT = _matmul(dqkv, u, "tn", BF16, "d_w_qkv")
    dW_convT = _matmul(dconv, u, "tn", BF16, "d_w_conv")
    dW_gateT = _matmul(dgate, u, "tn", BF16, "d_w_gate")
    pre_fo, r_fo = _split_wait(st_sfo, cs_fo, [dW_qkvT], "scatter_fo_wait")
    pre_fi, r_fi = _split_wait(st_sfi, cs_fi, [dW_qkvT], "scatter_fi_wait")
    pre_mid, r_mid = _split_wait(st_smid, cs_mid, [dW_qkvT], "scatter_mid_wait")
    fin_a = sum_all(pre_fo + pre_fi + pre_mid, r_fo + r_fi + r_mid, names_a)
    srcs = [(0, dW_qkvT[:n_qkv]), (n_qkv, dW_convT), (n_qkv + 3 * D, dW_gateT)]
    rows_of = []
    for p in range(4):
        for lo, src in srcs:
            a, b = max(lo, p * CS), min(lo + src.shape[0], (p + 1) * CS)
            if a < b:
                rows_of.append(src[a - lo:b - lo])
        rows_of.append(jnp.zeros((CSP - CS, D), BF16))
    dW_inT = jnp.concatenate(rows_of, axis=0).reshape(4, CSP, D)
    dW_qb_u = dW_qb.reshape(Q_LORA, N_HEADS, QK_PAD)[:, :, :QK_NOPE + QK_ROPE].reshape(Q_LORA, -1)
    names_b = ["w_in", "w_q_b", "w_kv_b"]
    p_b = [slabs(dW_inT), slabs(uncols(dW_qb_u)), slabs(uncols(dW_kvb))]
    du, s_b = _matmul(dqkv, W_qkvT, "nn", F32, "d_u_qkv", carry=_pair_plan(p_b))
    pre_b = add_pairs(p_b, s_b, names_b)
    cs_b = _scatter_copies(pre_b)
    st_b = _split_start(pre_b, cs_b, scale1, "scatter_last_start")
    du, fs_a = _matmul(dconv, W_convT, "nn", F32, "d_u_conv", add=du, carry=_sibling_plan(fin_a))
    du = _matmul(dgate, W_gateT, "nn", F32, "d_u_gate", add=du)
    grad_x, d_shift1, d_scale1 = _dx_final(dxa, du, x2, st_b[4][0])

    big = {}
    ws = dict(w_in=(w_inT, m_w_inT, v_w_inT), w_q_b=(w_q_b2, m_w_q_b[0], v_w_q_b[0]),
              w_kv_b=(w_kv_b2, m_w_kv_b[0], v_w_kv_b[0]), w_o_a=(w_o_a2, m_w_o_a[0], v_w_o_a[0]),
              w_o_b=(w_o_b2, m_w_o_b[0], v_w_o_b[0]), w_o=(w_o2, m_w_o[0], v_w_o[0]),
              w_ffn_in=(w_ffn_in2, m_w_ffn_in[0], v_w_ffn_in[0]), w_ffn_out=(w_ffn_out2, m_w_ffn_out[0], v_w_ffn_out[0]))

    def adam_of(nm, a, b, carry=None):
        w_, m_, v_ = ws[nm]
        return _adam_halves("adam_" + nm, w_, m_, v_, a, b, core_i, carry)

    for nm, a, b in zip(names_a, fin_a, fs_a):
        big[nm] = adam_of(nm, a, b, _token_plan(st_b[4][0]))[0]
    done = [big[nm][1] for nm in names_a] + [grad_x]
    pre_b, r_b = _split_wait(st_b, cs_b, done, "scatter_last_wait")
    fin_b = sum_all(pre_b, r_b, names_b)
    fs_b = _run_plan(_sibling_plan(fin_b), "sibling_last")
    for nm, a, b in zip(names_b, fin_b, fs_b):
        big[nm] = adam_of(nm, a, b)

    def pad_d(v):
        return jnp.pad(v, ((0, 0), (0, D - v.shape[1])))

    small = _pack_rows([d_ln1_g, d_ln1_b, d_ln2_g, d_ln2_b, pad_d(d_g_q), pad_d(d_g_kv), d_wconv,
                         d_shift1, d_scale1, d_gate1, d_shift2, d_scale2, d_gate2, pad_d(loss_acc)], 16, after=[pre_b[1]])
    small_all = _all_gather8(small, "gather_small")
    small_sum = _sum8(small_all)
    loss = small_sum[15, 0]
    g_ln1_g, g_ln1_b, g_ln2_g, g_ln2_b = (small_sum[k:k + 1] for k in range(4))
    g_g_q, g_g_kv = small_sum[4:5, :Q_LORA], small_sum[5:6, :KV_LORA]
    g_wconv = lax.dynamic_slice(small_sum[6:9], (0, chip * CW), (3, CW))
    g_b_ada = small_sum[9:15].reshape(1, 6 * D)
    dmod_all = small_all[:, 9:15, :].reshape(8, 6 * D)
    g_w_ada = _ada_bwd(c_all, lax.dynamic_slice(dmod_all, (0, chip * NA), (8, NA)))
    big["w_ada"] = [g_w_ada] + list(_adam("adam_w_ada", w_ada2, m_w_ada[0], v_w_ada[0], g_w_ada))
    sm = {}
    for nm, w_, m_, v_, g_ in [("b_ada", b_ada, m_b_ada, v_b_ada, g_b_ada), ("g_q_a", g_q_a, m_g_q_a, v_g_q_a, g_g_q),
                               ("g_kv_a", g_kv_a, m_g_kv_a, v_g_kv_a, g_g_kv),
                               ("w_conv", w_conv[0], m_w_conv[0], v_w_conv[0], g_wconv),
                               ("ln1_g", ln1_g, m_ln1_g, v_ln1_g, g_ln1_g), ("ln1_b", ln1_b, m_ln1_b, v_ln1_b, g_ln1_b),
                               ("ln2_g", ln2_g, m_ln2_g, v_ln2_g, g_ln2_g), ("ln2_b", ln2_b, m_ln2_b, v_ln2_b, g_ln2_b)]:
        sm[nm] = (g_,) + tuple(_adam_small("adam_" + nm, w_, m_, v_, g_))

    order = ["w_ada", "b_ada", "w_in", "g_q_a", "w_q_b", "g_kv_a", "w_kv_b", "w_o_a", "w_conv", "w_o_b", "w_o",
             "ln1_g", "ln1_b", "w_ffn_in", "w_ffn_out", "ln2_g", "ln2_b"]
    lead = {"b_ada", "g_q_a", "g_kv_a", "ln1_g", "ln1_b", "ln2_g", "ln2_b"}

    def leaf(nm, k):
        val = big[nm][k] if nm in big else sm[nm][k]
        if nm == "w_in":
            val = val.T
        return val if nm in lead else val[None]

    outs = [loss, grad_x[None]]
    for k in range(4):
        outs += [leaf(nm, k) for nm in order]
    return tuple(outs)
```

```python
import jax
import jax.numpy as jnp
from jax import lax
from jax.experimental import pallas as pl
from jax.experimental.pallas import tpu as pltpu

F32, BF16 = jnp.float32, jnp.bfloat16
N_HEADS, QK_NOPE, QK_ROPE, V_HEAD = 16, 128, 64, 128
Q_LORA, KV_LORA = 512, 512
QK_PAD = 256
QKV_A = 1152
CHUNK_SHIFT = 6
ATTN_SCALE = (QK_NOPE + QK_ROPE) ** -0.5
LOG2E = 1.4426950408889634
SCALE2 = ATTN_SCALE * LOG2E
ROPE_THETA = 10000.0
ALPHA = 2.0 ** 0.25
LN_EPS, RMS_EPS = 1e-5, 1e-6
ADAM_LR, ADAM_B1, ADAM_B2, ADAM_EPS, ADAM_WD, ADAM_STEP = 0.001, 0.9, 0.999, 1e-08, 0.01, 10
ADAM_C1 = 1.0 - ADAM_B1 ** ADAM_STEP
ADAM_C2 = 1.0 - ADAM_B2 ** ADAM_STEP
VMEM_LIMIT = 56 * 1024 * 1024
MESH = pl.DeviceIdType.MESH
ANY = pl.BlockSpec(memory_space=pl.ANY)
HBM_SPEC = pl.BlockSpec(memory_space=pltpu.HBM)
SEM_SPEC = pl.BlockSpec(memory_space=pltpu.SEMAPHORE)
NT = (((1,), (1,)), ((), ()))
TN = (((0,), (0,)), ((), ()))
NN = (((1,), (0,)), ((), ()))


def _params(sem=None):
    return pltpu.CompilerParams(dimension_semantics=sem, vmem_limit_bytes=VMEM_LIMIT)


def _pick(n, cands=(1408, 1024, 512, 384, 256, 128)):
    for t in cands:
        if n % t == 0:
            return t
    return n


def _row_tile(rows, row_bytes, budget, mult=8):
    best = mult
    for t in range(mult, rows + 1, mult):
        if rows % t == 0 and t * row_bytes <= budget:
            best = t
    return best


def _tile2(rows, cols, mult=8, budget=3 << 18):
    col_tiles = [t for t in range(128, cols + 1, 128) if cols % t == 0] or [cols]
    best = None
    for tc in col_tiles:
        for tr in range(mult, rows + 1, mult):
            if rows % tr == 0 and tr * tc <= budget and (best is None or (tr * tc, tc) > (best[0] * best[1], best[1])):
                best = (tr, tc)
    assert best is not None, (rows, cols)
    return best


def _sigmoid(x):
    return jax.nn.sigmoid(x)


class _Plan:
    def __init__(self, ins, outs, sems, start, finish, aliases=None):
        self.ins, self.outs, self.sems, self.start, self.finish = list(ins), list(outs), list(sems), start, finish
        self.aliases = dict(aliases or {})

    def io_aliases(self, first_in, first_out):
        return {first_in + i: first_out + o for i, o in self.aliases.items()}


def _token_plan(token):
    return _Plan([token], [], [], lambda *a: None, lambda *a: None)


def _run_plan(plan, name, ride=None):
    n_in, n_out = len(plan.ins), len(plan.outs)
    extra = [] if ride is None else list(ride)
    aliases = plan.io_aliases(0, 0)
    for k in range(len(extra)):
        aliases[n_in + k] = n_out + k

    def body(*refs):
        ins, outs, sems = refs[:n_in], refs[n_in + len(extra):n_in + len(extra) + n_out], refs[n_in + 2 * len(extra) + n_out:]
        plan.start(ins, outs, sems)
        plan.finish(ins, outs, sems)

    return pl.pallas_call(body, name=name, out_shape=plan.outs + [jax.ShapeDtypeStruct(r.shape, r.dtype) for r in extra],
                          in_specs=[ANY] * (n_in + len(extra)), out_specs=[ANY] * (n_out + len(extra)),
                          scratch_shapes=plan.sems, input_output_aliases=aliases,
                          compiler_params=_params())(*plan.ins, *extra)


def _matmul(a, b, mode, out_dtype, name, add=None, carry=None, shards=None, finish=None):
    if mode == "nn":
        (M, K), N, dims = a.shape, b.shape[-1] * (4 if shards else 1), NN
    elif mode == "nt":
        (M, K), N, dims = a.shape, b.shape[-2], NT
    else:
        (K, M), N, dims = a.shape, b.shape[1], TN
    split_n = shards and mode != "nt"
    tm = _pick(M)
    tn = _pick(N // 4) if split_n else _pick(N)
    deep = (2816, 2048, 1408, 1024, 512, 384, 256, 128)
    if shards and mode == "nt":
        tk = _pick(K // 4, deep)
    else:
        tk = K if K <= 2048 else _pick(K, deep)
    nk = K // tk
    per = (N // 4 // tn) if split_n else (K // 4 // tk if shards else 1)
    a_spec = (pl.BlockSpec((tk, tm), lambda i, j, k: (k, i)) if mode == "tn"
              else pl.BlockSpec((tm, tk), lambda i, j, k: (i, k)))
    if shards == "b" and mode == "nn":
        b_spec = pl.BlockSpec((None, tk, tn), lambda i, j, k: (j // per, k, j % per))
    elif shards == "b":
        b_spec = pl.BlockSpec((None, tn, tk), lambda i, j, k: (k // per, j, k % per))
    else:
        b_spec = (pl.BlockSpec((tn, tk), lambda i, j, k: (j, k)) if mode == "nt"
                  else pl.BlockSpec((tk, tn), lambda i, j, k: (k, j)))
    o_spec = pl.BlockSpec((tm, tn), lambda i, j, k: (i, j))
    o_shape = (M, N)
    if shards == "o":
        o_spec, o_shape = pl.BlockSpec((None, tm, tn), lambda i, j, k: (j // per, i, j % per)), (4, M, N // 4)
    has_add = add is not None
    has_fin = finish is not None
    n_ci = len(carry.ins) if carry else 0
    n_co = len(carry.outs) if carry else 0
    n_in = 2 + has_add + has_fin
    grid = (M // tm, N // tn, nk)

    def body(*refs):
        a_ref, b_ref = refs[0], refs[1]
        add_ref = refs[2] if has_add else None
        fin_ref = refs[2 + has_add] if has_fin else None

        def store(r):
            if has_add:
                r = r + add_ref[...]
            if has_fin:
                r = finish[0](r, fin_ref[...])
            o_ref[...] = r.astype(o_ref.dtype)

        o_ref = refs[n_in + n_ci]
        acc_ref = refs[n_in + n_ci + 1 + n_co] if nk > 1 else None
        c_ins = refs[n_in:n_in + n_ci]
        c_outs = refs[n_in + n_ci + 1:n_in + n_ci + 1 + n_co]
        c_sems = refs[n_in + n_ci + 1 + n_co + (nk > 1):]
        i, j, k = pl.program_id(0), pl.program_id(1), pl.program_id(2)

        if carry:
            @pl.when((i == 0) & (j == 0) & (k == 0))
            def _():
                carry.start(c_ins, c_outs, c_sems)

        part = lax.dot_general(a_ref[...], b_ref[...], dims, preferred_element_type=F32)
        if nk == 1:
            store(part)
        else:
            @pl.when(k == 0)
            def _():
                acc_ref[...] = part

            @pl.when((k > 0) & (k < nk - 1))
            def _():
                acc_ref[...] += part

            @pl.when(k == nk - 1)
            def _():
                store(acc_ref[...] + part)

        if carry:
            @pl.when((i == grid[0] - 1) & (j == grid[1] - 1) & (k == nk - 1))
            def _():
                carry.finish(c_ins, c_outs, c_sems)

    ins = [a, b] + ([add] if has_add else []) + ([finish[1]] if has_fin else []) + (carry.ins if carry else [])
    in_specs = ([a_spec, b_spec] + ([o_spec] if has_add else [])
                + ([pl.BlockSpec((tm, finish[1].shape[1]), lambda i, j, k: (i, 0))] if has_fin else []) + [ANY] * n_ci)
    res = pl.pallas_call(
        body, name=name, grid=grid,
        in_specs=in_specs, out_specs=[o_spec] + [ANY] * n_co,
        out_shape=[jax.ShapeDtypeStruct(o_shape, out_dtype)] + (carry.outs if carry else []),
        scratch_shapes=([pltpu.VMEM((tm, tn), F32)] if nk > 1 else []) + (carry.sems if carry else []),
        input_output_aliases=carry.io_aliases(n_in, 1) if carry else {},
        compiler_params=_params(("arbitrary",) * 3 if carry else ("parallel", "parallel", "arbitrary")),
    )(*ins)
    return (res[0], res[1:]) if carry else res[0]


def _rows(body, name, n_rows, tm, ins, outs, accs=(), carry=None):
    grid = (n_rows // tm,)

    def halo(arr):
        return 16 if arr.dtype == BF16 else 8

    arrays, in_specs = [], []
    for spec in ins:
        kind, arr = spec[0], spec[1]
        arrays.append(arr)
        if kind == "row":
            _, _, cb, w = spec
            in_specs.append(pl.BlockSpec((tm, w), lambda i, cb=cb: (i, cb)))
        elif kind == "full":
            in_specs.append(pl.BlockSpec(arr.shape, lambda i, nd=arr.ndim: (0,) * nd))
        elif kind == "prev":
            _, _, cb, w = spec
            h = halo(arr)
            in_specs.append(pl.BlockSpec((h, w), lambda i, cb=cb, per=tm // h: (jnp.maximum(i * per - 1, 0), cb)))
        else:
            _, _, cb, w = spec
            h = halo(arr)
            in_specs.append(pl.BlockSpec((h, w), lambda i, cb=cb, per=tm // h, last=n_rows // h - 1:
                                         (jnp.minimum((i + 1) * per, last), cb)))
    out_shape = [jax.ShapeDtypeStruct((n_rows, w), dt) for (w, dt) in outs]
    out_specs = [pl.BlockSpec((tm, w), lambda i: (i, 0)) for (w, _) in outs]
    out_shape += [jax.ShapeDtypeStruct(s, F32) for s in accs]
    out_specs += [pl.BlockSpec(s, lambda i, nd=len(s): (0,) * nd) for s in accs]
    n_in, n_out, n_acc = len(ins), len(outs), len(accs)
    n_ci = len(carry.ins) if carry else 0
    n_co = len(carry.outs) if carry else 0

    def kernel_body(*refs):
        first = n_in + n_ci
        c_ins, c_outs, c_sems = refs[n_in:first], refs[first + n_out + n_acc:first + n_out + n_acc + n_co], refs[first + n_out + n_acc + n_co:]
        if carry:
            @pl.when(pl.program_id(0) == 0)
            def _():
                carry.start(c_ins, c_outs, c_sems)

        body(pl.program_id(0), refs[:n_in], refs[first:first + n_out], refs[first + n_out:first + n_out + n_acc])
        if carry:
            @pl.when(pl.program_id(0) == grid[0] - 1)
            def _():
                carry.finish(c_ins, c_outs, c_sems)

    res = pl.pallas_call(
        kernel_body, name=name, grid=grid, in_specs=in_specs + [ANY] * n_ci, out_specs=out_specs + [ANY] * n_co,
        out_shape=out_shape + (carry.outs if carry else []), scratch_shapes=carry.sems if carry else [],
        input_output_aliases=carry.io_aliases(n_in, n_out + n_acc) if carry else {},
        compiler_params=_params(("arbitrary",)),
    )(*arrays, *(carry.ins if carry else []))
    return (res[:n_out + n_acc], res[n_out + n_acc:]) if carry else res


def _acc_add(i, ref, val):
    @pl.when(i == 0)
    def _():
        ref[...] = val

    @pl.when(i > 0)
    def _():
        ref[...] += val


def _rope(t, tab, sign):
    c, sa, sb = tab[:, 0:128], tab[:, 128:256], tab[:, 256:384]
    rot = pltpu.roll(t, 96, 1) * sa + pltpu.roll(t, 32, 1) * sb
    return t * c + rot if sign > 0 else t * c - rot


def _ln_stats(r):
    mu = jnp.mean(r, axis=-1, keepdims=True)
    d = r - mu
    var = jnp.mean(d * d, axis=-1, keepdims=True)
    rstd = lax.rsqrt(var + LN_EPS)
    return d * rstd, rstd


def _ln_bwd(dxh, xh, rstd):
    m1 = jnp.mean(dxh, axis=-1, keepdims=True)
    m2 = jnp.mean(dxh * xh, axis=-1, keepdims=True)
    return rstd * (dxh - m1 - xh * m2)


def _modulate(x, scale, shift, name, carry=None):
    S, D = x.shape

    def body(i, ins, outs, accs):
        outs[0][...] = (ins[0][...] * (1.0 + ins[1][...]) + ins[2][...]).astype(BF16)

    res = _rows(body, name, S, _pick(S, (256, 128)), [("row", x, 0, D), ("full", scale), ("full", shift)], [(D, BF16)],
                carry=carry)
    return (res[0][0], res[1]) if carry else res[0]


def _rms_fwd(pq, tab, g_q, g_kv):
    S = pq.shape[0]

    def body(i, ins, outs, accs):
        pq_ref, tab_ref, gq_ref, gkv_ref = ins

        def rms(x, g):
            return x * lax.rsqrt(jnp.mean(x * x, axis=-1, keepdims=True) + RMS_EPS) * g

        outs[0][...] = rms(pq_ref[:, 0:Q_LORA], gq_ref[...]).astype(BF16)
        outs[1][...] = rms(pq_ref[:, Q_LORA:Q_LORA + KV_LORA], gkv_ref[...]).astype(BF16)
        outs[2][...] = _rope(pq_ref[:, Q_LORA + KV_LORA:QKV_A], tab_ref[...], 1).astype(BF16)

    return _rows(body, "rms_fwd", S, _pick(S, (256, 128)),
                 [("row", pq, 0, QKV_A), ("row", tab, 0, 384), ("full", g_q), ("full", g_kv)],
                 [(Q_LORA, BF16), (KV_LORA, BF16), (128, BF16)])


def _allowed(q0, k0, bq):
    row = q0 + lax.broadcasted_iota(jnp.int32, (bq, bq), 0)
    col = k0 + lax.broadcasted_iota(jnp.int32, (bq, bq), 1)
    return (col >> CHUNK_SHIFT) <= (row >> CHUNK_SHIFT)


ATTN_BLOCK = 512


HEADS_PER_STEP = 2


def _attn_fwd(q, kv, kr):
    S = q.shape[0]
    bq = min(ATTN_BLOCK, S)
    nq = S // bq
    G = HEADS_PER_STEP

    def body(q_ref, kv_ref, kr_ref, o_ref, lse_ref, kcat):
        qi = pl.program_id(1)

        @pl.when(qi == 0)
        def _():
            for g in range(G):
                kcat[g, :, 0:128] = kv_ref[:, g * 256:g * 256 + 128]
                kcat[g, :, 128:256] = kr_ref[...]

        qs = [q_ref[:, g * QK_PAD:(g + 1) * QK_PAD] for g in range(G)]

        def step(j, carry, masked):
            off = pl.multiple_of(j * bq, bq)
            rows = pl.ds(off, bq)
            mask = _allowed(qi * bq, off, bq) if masked else None
            out = []
            for g in range(G):
                m, l, acc = carry[g]
                s = lax.dot_general(qs[g], kcat[g, rows, :], NT, preferred_element_type=F32) * SCALE2
                if masked:
                    s = jnp.where(mask, s, -1e30)
                m_new = jnp.maximum(m, jnp.max(s, axis=1, keepdims=True))
                a = jnp.exp2(m - m_new)
                p = jnp.exp2(s - m_new)
                l = a * l + jnp.sum(p, axis=1, keepdims=True)
                acc = a * acc + jnp.dot(p.astype(BF16), kv_ref[rows, g * 256 + 128:(g + 1) * 256],
                                        preferred_element_type=F32)
                out.append((m_new, l, acc))
            return tuple(out)

        init = tuple((jnp.full((bq, 1), -1e30, F32), jnp.zeros((bq, 1), F32), jnp.zeros((bq, V_HEAD), F32))
                     for _ in range(G))
        below = lax.fori_loop(0, qi, lambda j, cr: step(j, cr, False), init)
        for g, (m, l, acc) in enumerate(step(qi, below, True)):
            o_ref[:, g * V_HEAD:(g + 1) * V_HEAD] = (acc / l).astype(BF16)
            lse_ref[g] = m + jnp.log2(l)

    return pl.pallas_call(
        body, name="attn_fwd", grid=(N_HEADS // G, nq),
        in_specs=[pl.BlockSpec((bq, G * QK_PAD), lambda h, i: (i, h)),
                  pl.BlockSpec((S, G * 256), lambda h, i: (0, h)),
                  pl.BlockSpec((S, 128), lambda h, i: (0, 0))],
        out_specs=[pl.BlockSpec((bq, G * V_HEAD), lambda h, i: (i, h)),
                   pl.BlockSpec((G, bq, 1), lambda h, i: (h, i, 0))],
        out_shape=[jax.ShapeDtypeStruct((S, N_HEADS * V_HEAD), BF16),
                   jax.ShapeDtypeStruct((N_HEADS, S, 1), F32)],
        scratch_shapes=[pltpu.VMEM((G, S, QK_PAD), BF16)],
        compiler_params=_params(("arbitrary", "arbitrary")),
    )(q, kv, kr)


def _attn_bwd(q, kv, kr, do, o, lse, tab, carry=None):
    S = q.shape[0]
    bq = min(ATTN_BLOCK, S)
    nq = S // bq

    n_ci = len(carry.ins) if carry else 0
    n_co = len(carry.outs) if carry else 0

    def body(*refs):
        q_ref, kn_ref, v_ref, kr_ref, do_ref, o_ref, lse_ref, tab_ref = refs[:8]
        dq_ref, dkv_ref, dkr_ref = refs[8 + n_ci:11 + n_ci]
        dq_acc, dk_acc, dv_acc, kcat, delta = refs[11 + n_ci + n_co:16 + n_ci + n_co]
        c_ins, c_outs, c_sems = refs[8:8 + n_ci], refs[11 + n_ci:11 + n_ci + n_co], refs[16 + n_ci + n_co:]
        h = pl.program_id(0)
        if carry:
            @pl.when(h == 0)
            def _():
                carry.start(c_ins, c_outs, c_sems)

        dq_acc[...] = jnp.zeros_like(dq_acc)
        dk_acc[...] = jnp.zeros_like(dk_acc)
        dv_acc[...] = jnp.zeros_like(dv_acc)
        kcat[:, 0:128] = kn_ref[...]
        kcat[:, 128:256] = kr_ref[...]
        for r in range(nq):
            rows = slice(r * bq, (r + 1) * bq)
            delta[rows, :] = jnp.sum(do_ref[rows, :].astype(F32) * o_ref[rows, :].astype(F32), axis=1, keepdims=True)

        def pair(i, j, masked):
            rows_i = pl.ds(pl.multiple_of(i * bq, bq), bq)
            rows_j = pl.ds(pl.multiple_of(j * bq, bq), bq)
            qv, dov, k = q_ref[rows_i, :], do_ref[rows_i, :], kcat[rows_j, :]
            s = lax.dot_general(qv, k, NT, preferred_element_type=F32) * SCALE2
            if masked:
                s = jnp.where(_allowed(i * bq, j * bq, bq), s, -1e30)
            p = jnp.exp2(s - lse_ref[0, rows_i, :])
            dv_acc[rows_j, :] += lax.dot_general(p.astype(BF16), dov, TN, preferred_element_type=F32)
            dp = lax.dot_general(dov, v_ref[rows_j, :], NT, preferred_element_type=F32)
            ds = (p * (dp - delta[rows_i, :]) * ATTN_SCALE).astype(BF16)
            dk_acc[rows_j, :] += lax.dot_general(ds, qv, TN, preferred_element_type=F32)
            dq_acc[rows_i, :] += jnp.dot(ds, k, preferred_element_type=F32)

        def kv_step(j, _):
            pair(j, j, True)

            def q_step(i, _):
                pair(i, j, False)
                return 0

            lax.fori_loop(j + 1, nq, q_step, 0)
            return 0

        lax.fori_loop(0, nq, kv_step, 0)

        for r in range(nq):
            rows = slice(r * bq, (r + 1) * bq)
            dq_ref[rows, 0:128] = dq_acc[rows, 0:128].astype(BF16)
            dq_ref[rows, 128:256] = _rope(dq_acc[rows, 128:256], tab_ref[rows, :], -1).astype(BF16)
        dkv_ref[:, 0:128] = dk_acc[:, 0:128].astype(BF16)
        dkv_ref[:, 128:256] = dv_acc[...].astype(BF16)

        @pl.when(h == 0)
        def _():
            dkr_ref[...] = dk_acc[:, 128:256]

        @pl.when(h > 0)
        def _():
            dkr_ref[...] += dk_acc[:, 128:256]

        @pl.when(h == N_HEADS - 1)
        def _():
            for r in range(nq):
                rows = slice(r * bq, (r + 1) * bq)
                dkr_ref[rows, :] = _rope(dkr_ref[rows, :], tab_ref[rows, :], -1)
            if carry:
                carry.finish(c_ins, c_outs, c_sems)

    W = N_HEADS * QK_PAD
    res = pl.pallas_call(
        body, name="attn_bwd", grid=(N_HEADS,),
        in_specs=[pl.BlockSpec((S, QK_PAD), lambda h: (0, h)),
                  pl.BlockSpec((S, 128), lambda h: (0, 2 * h)),
                  pl.BlockSpec((S, 128), lambda h: (0, 2 * h + 1)),
                  pl.BlockSpec((S, 128), lambda h: (0, 0)),
                  pl.BlockSpec((S, V_HEAD), lambda h: (0, h)),
                  pl.BlockSpec((S, V_HEAD), lambda h: (0, h)),
                  pl.BlockSpec((1, S, 1), lambda h: (h, 0, 0)),
                  pl.BlockSpec((S, 384), lambda h: (0, 0))] + [ANY] * n_ci,
        out_specs=[pl.BlockSpec((S, QK_PAD), lambda h: (0, h)),
                   pl.BlockSpec((S, QK_PAD), lambda h: (0, h)),
                   pl.BlockSpec((S, 128), lambda h: (0, 0))] + [ANY] * n_co,
        out_shape=[jax.ShapeDtypeStruct((S, W), BF16), jax.ShapeDtypeStruct((S, W), BF16),
                   jax.ShapeDtypeStruct((S, 128), F32)] + (carry.outs if carry else []),
        scratch_shapes=[pltpu.VMEM((S, QK_PAD), F32), pltpu.VMEM((S, QK_PAD), F32), pltpu.VMEM((S, V_HEAD), F32),
                        pltpu.VMEM((S, QK_PAD), BF16), pltpu.VMEM((S, 1), F32)]
        + (carry.sems if carry else []),
        input_output_aliases=carry.io_aliases(8, 3) if carry else {},
        compiler_params=_params(("arbitrary",)),
    )(q, kv, kv, kr, do, o, lse, tab, *(carry.ins if carry else []))
    return res[0], res[1], res[2], res[3:]


def _shift_down(cur, prev, i, n):
    tm, h = cur.shape[0], prev.shape[0]
    prev = jnp.where(i == 0, jnp.zeros_like(prev), prev)
    full = jnp.concatenate([prev, cur], axis=0)
    return pltpu.roll(full, n, 0)[h:h + tm, :]


def _shift_up(cur, nxt, i, last, n):
    tm, h = cur.shape[0], nxt.shape[0]
    nxt = jnp.where(i == last, jnp.zeros_like(nxt), nxt)
    full = jnp.concatenate([cur, nxt], axis=0)
    return pltpu.roll(full, tm + h - n, 0)[0:tm, :]


def _conv_fwd(pc, w_conv):
    S, D = pc.shape[0], pc.shape[1] // 3
    tm = _pick(S, (256, 128))

    def body(i, ins, outs, accs):
        b_ref, c_ref, x_ref, cp_ref, xp_ref, w_ref = ins
        z = c_ref[...].astype(F32) * x_ref[...].astype(F32)
        zp = cp_ref[...].astype(F32) * xp_ref[...].astype(F32)
        cz = w_ref[0:1, :] * _shift_down(z, zp, i, 2) + w_ref[1:2, :] * _shift_down(z, zp, i, 1) + w_ref[2:3, :] * z
        outs[0][...] = (b_ref[...].astype(F32) * cz).astype(BF16)

    return _rows(body, "conv_fwd", S, tm,
                 [("row", pc, 0, D), ("row", pc, 1, D), ("row", pc, 2, D), ("prev", pc, 1, D), ("prev", pc, 2, D),
                  ("full", w_conv)], [(D, BF16)])[0]


def _conv_bwd(dhb, pc, w_conv):
    S, D = dhb.shape
    tm = _pick(S, (256, 128))
    last = S // tm - 1

    def body(i, ins, outs, accs):
        g_ref, b_ref, c_ref, x_ref, cp_ref, xp_ref, gn_ref, bn_ref, w_ref = ins
        w0, w1, w2 = w_ref[0:1, :], w_ref[1:2, :], w_ref[2:3, :]
        c, x, g = c_ref[...].astype(F32), x_ref[...].astype(F32), g_ref[...].astype(F32)
        z = c * x
        zp = cp_ref[...].astype(F32) * xp_ref[...].astype(F32)
        z1, z2 = _shift_down(z, zp, i, 1), _shift_down(z, zp, i, 2)
        cz = w0 * z2 + w1 * z1 + w2 * z
        dcz = g * b_ref[...].astype(F32)
        dczn = gn_ref[...].astype(F32) * bn_ref[...].astype(F32)
        dz = w2 * dcz + w1 * _shift_up(dcz, dczn, i, last, 1) + w0 * _shift_up(dcz, dczn, i, last, 2)
        outs[0][:, 0:D] = (g * cz).astype(BF16)
        outs[0][:, D:2 * D] = (dz * x).astype(BF16)
        outs[0][:, 2 * D:3 * D] = (dz * c).astype(BF16)
        dw = jnp.concatenate([jnp.sum(dcz * z2, axis=0, keepdims=True), jnp.sum(dcz * z1, axis=0, keepdims=True),
                              jnp.sum(dcz * z, axis=0, keepdims=True)], axis=0)
        _acc_add(i, accs[0], dw)

    return _rows(body, "conv_bwd", S, tm,
                 [("row", dhb, 0, D), ("row", pc, 0, D), ("row", pc, 1, D), ("row", pc, 2, D),
                  ("prev", pc, 1, D), ("prev", pc, 2, D), ("next", dhb, 0, D), ("next", pc, 0, D), ("full", w_conv)],
                 [(3 * D, BF16)], [(3, D)])


def _merge_fwd(y_a, y_b, pg):
    S, D = y_a.shape

    def body(i, ins, outs, accs):
        ya, yb, ga, gb = ins
        outs[0][...] = (_sigmoid(ga[...].astype(F32)) * ya[...].astype(F32)
                        + _sigmoid(gb[...].astype(F32)) * yb[...].astype(F32)).astype(BF16)

    return _rows(body, "merge_fwd", S, _pick(S, (256, 128)),
                 [("row", y_a, 0, D), ("row", y_b, 0, D), ("row", pg, 0, D), ("row", pg, 1, D)], [(D, BF16)])[0]


def _merge_bwd(dm, y_a, y_b, pg):
    S, D = dm.shape

    def body(i, ins, outs, accs):
        d, ya, yb = ins[0][...].astype(F32), ins[1][...].astype(F32), ins[2][...].astype(F32)
        sa, sb = _sigmoid(ins[3][...].astype(F32)), _sigmoid(ins[4][...].astype(F32))
        outs[0][...] = (d * sa).astype(BF16)
        outs[1][...] = (d * sb).astype(BF16)
        outs[2][:, 0:D] = (d * ya * (sa * (1.0 - sa))).astype(BF16)
        outs[2][:, D:2 * D] = (d * yb * (sb * (1.0 - sb))).astype(BF16)

    return _rows(body, "merge_bwd", S, _pick(S, (256, 128)),
                 [("row", dm, 0, D), ("row", y_a, 0, D), ("row", y_b, 0, D), ("row", pg, 0, D), ("row", pg, 1, D)],
                 [(D, BF16), (D, BF16), (2 * D, BF16)])


def _ln1_fwd(x, mix, gate1, g, b, scale2, shift2, carry=None):
    S, D = x.shape

    def body(i, ins, outs, accs):
        x_ref, mix_ref, gate_ref, g_ref, b_ref, sc_ref, sh_ref = ins
        xh, _ = _ln_stats(ALPHA * x_ref[...] + gate_ref[...] * mix_ref[...])
        x1 = xh * g_ref[...] + b_ref[...]
        outs[0][...] = x1
        outs[1][...] = (x1 * (1.0 + sc_ref[...]) + sh_ref[...]).astype(BF16)

    return _rows(body, "ln1_fwd", S, _pick(S, (256, 128)),
                 [("row", x, 0, D), ("row", mix, 0, D), ("full", gate1), ("full", g), ("full", b),
                  ("full", scale2), ("full", shift2)], [(D, F32), (D, BF16)], carry=carry)


def _swiglu_fwd(hh, carry=None):
    S, F = hh.shape[0], hh.shape[1] // 2

    def body(i, ins, outs, accs):
        hg = ins[0][...].astype(F32)
        outs[0][...] = (hg * _sigmoid(hg) * ins[1][...].astype(F32)).astype(BF16)

    res = _rows(body, "swiglu_fwd", S, _pick(S, (128,)), [("row", hh, 0, F), ("row", hh, 1, F)], [(F, BF16)], carry=carry)
    return (res[0][0], res[1]) if carry else res[0]


def _swiglu_bwd(dact, hh):
    S, F = dact.shape

    def body(i, ins, outs, accs):
        d, hg, hu = ins[0][...].astype(F32), ins[1][...].astype(F32), ins[2][...].astype(F32)
        sg = _sigmoid(hg)
        outs[0][:, 0:F] = (d * hu * (sg * (1.0 + hg * (1.0 - sg)))).astype(BF16)
        outs[0][:, F:2 * F] = (d * (hg * sg)).astype(BF16)

    return _rows(body, "swiglu_bwd", S, _pick(S, (128,)),
                 [("row", dact, 0, F), ("row", hh, 0, F), ("row", hh, 1, F)], [(2 * F, BF16)])[0]


def _ln2_loss_bwd(x1, ffn, gate2, g, b, target):
    S, D = x1.shape

    def body(i, ins, outs, accs):
        x1_ref, f_ref, gate_ref, g_ref, b_ref, t_ref = ins
        f = f_ref[...]
        xh, rstd = _ln_stats(ALPHA * x1_ref[...] + gate_ref[...] * f)
        e = xh * g_ref[...] + b_ref[...] - t_ref[...]
        dy = e * (1.0 / D)
        dr = _ln_bwd(dy * g_ref[...], xh, rstd)
        outs[0][...] = (gate_ref[...] * dr).astype(BF16)
        outs[1][...] = ALPHA * dr
        _acc_add(i, accs[0], jnp.full((1, 128), (0.5 / D) * jnp.sum(e * e), F32))
        _acc_add(i, accs[1], jnp.sum(dy * xh, axis=0, keepdims=True))
        _acc_add(i, accs[2], jnp.sum(dy, axis=0, keepdims=True))
        _acc_add(i, accs[3], jnp.sum(dr * f, axis=0, keepdims=True))

    return _rows(body, "ln2_loss_bwd", S, _pick(S, (256, 128)),
                 [("row", x1, 0, D), ("row", ffn, 0, D), ("full", gate2), ("full", g), ("full", b), ("row", target, 0, D)],
                 [(D, BF16), (D, F32)], [(1, 128), (1, D), (1, D), (1, D)])


def _ln1_bwd(x, mix, dx1a, du2, gate1, g, b, scale2):
    S, D = x.shape

    def body(i, ins, outs, accs):
        x_ref, mix_ref, da_ref, du_ref, gate_ref, g_ref, b_ref, sc_ref = ins
        mix, du = mix_ref[...], du_ref[...]
        xh, rstd = _ln_stats(ALPHA * x_ref[...] + gate_ref[...] * mix)
        x1 = xh * g_ref[...] + b_ref[...]
        dx1 = da_ref[...] + du * (1.0 + sc_ref[...])
        dr = _ln_bwd(dx1 * g_ref[...], xh, rstd)
        outs[0][...] = (gate_ref[...] * dr).astype(BF16)
        outs[1][...] = ALPHA * dr
        _acc_add(i, accs[0], jnp.sum(du, axis=0, keepdims=True))
        _acc_add(i, accs[1], jnp.sum(du * x1, axis=0, keepdims=True))
        _acc_add(i, accs[2], jnp.sum(dx1 * xh, axis=0, keepdims=True))
        _acc_add(i, accs[3], jnp.sum(dx1, axis=0, keepdims=True))
        _acc_add(i, accs[4], jnp.sum(dr * mix, axis=0, keepdims=True))

    return _rows(body, "ln1_bwd", S, _pick(S, (256, 128)),
                 [("row", x, 0, D), ("row", mix, 0, D), ("row", dx1a, 0, D), ("row", du2, 0, D),
                  ("full", gate1), ("full", g), ("full", b), ("full", scale2)],
                 [(D, BF16), (D, F32)], [(1, D)] * 5)


def _rms_bwd(d_rq, d_rkv, pq, dkr, g_q, g_kv):
    S = pq.shape[0]

    def body(i, ins, outs, accs):
        dq_ref, dkv_ref, pq_ref, dkr_ref, gq_ref, gkv_ref = ins

        def rms_bwd(dy, x, g):
            r = lax.rsqrt(jnp.mean(x * x, axis=-1, keepdims=True) + RMS_EPS)
            dyg = dy * g
            dx = r * dyg - x * (r * r * r) * jnp.mean(dyg * x, axis=-1, keepdims=True)
            return dx, jnp.sum(dy * (x * r), axis=0, keepdims=True)

        dxq, dgq = rms_bwd(dq_ref[...], pq_ref[:, 0:Q_LORA], gq_ref[...])
        dxkv, dgkv = rms_bwd(dkv_ref[...], pq_ref[:, Q_LORA:Q_LORA + KV_LORA], gkv_ref[...])
        outs[0][:, 0:Q_LORA] = dxq.astype(BF16)
        outs[0][:, Q_LORA:Q_LORA + KV_LORA] = dxkv.astype(BF16)
        outs[0][:, Q_LORA + KV_LORA:QKV_A] = dkr_ref[...].astype(BF16)
        _acc_add(i, accs[0], dgq)
        _acc_add(i, accs[1], dgkv)

    return _rows(body, "rms_bwd", S, _pick(S, (256, 128)),
                 [("row", d_rq, 0, Q_LORA), ("row", d_rkv, 0, KV_LORA), ("row", pq, 0, QKV_A), ("row", dkr, 0, 128),
                  ("full", g_q), ("full", g_kv)], [(QKV_A, BF16)], [(1, Q_LORA), (1, KV_LORA)])


def _dx_final(dxa, du, x, scale1):
    S, D = x.shape

    def body(i, ins, outs, accs):
        du = ins[1][...]
        outs[0][...] = ins[0][...] + du * (1.0 + ins[3][...])
        _acc_add(i, accs[0], jnp.sum(du, axis=0, keepdims=True))
        _acc_add(i, accs[1], jnp.sum(du * ins[2][...], axis=0, keepdims=True))

    return _rows(body, "dx_final", S, _pick(S, (256, 128)),
                 [("row", dxa, 0, D), ("row", du, 0, D), ("row", x, 0, D), ("full", scale1)],
                 [(D, F32)], [(1, D), (1, D)])


def _ada_fwd(c_all, w, bias):
    B, D = c_all.shape
    NA = w.shape[1]
    tn = _pick(NA, (512, 256, 128))

    def body(c_ref, w_ref, b_ref, o_ref):
        cv = c_ref[...]
        ca = (cv * _sigmoid(cv)).astype(BF16)
        o_ref[...] = jnp.dot(ca, w_ref[...].astype(BF16), preferred_element_type=F32) + b_ref[...]

    return pl.pallas_call(
        body, name="ada_fwd", grid=(NA // tn,),
        in_specs=[pl.BlockSpec((B, D), lambda j: (0, 0)), pl.BlockSpec((D, tn), lambda j: (0, j)),
                  pl.BlockSpec((1, tn), lambda j: (0, j))],
        out_specs=pl.BlockSpec((B, tn), lambda j: (0, j)),
        out_shape=jax.ShapeDtypeStruct((B, NA), F32),
        compiler_params=_params(("arbitrary",)),
    )(c_all, w, bias)


def _ada_bwd(c_all, dmod):
    B, D = c_all.shape
    NA = dmod.shape[1]
    tn = _pick(NA, (512, 256, 128))

    def body(c_ref, d_ref, o_ref):
        cv = c_ref[...]
        ca = (cv * _sigmoid(cv)).astype(BF16)
        o_ref[...] = lax.dot_general(ca, d_ref[...].astype(BF16), TN, preferred_element_type=F32)

    return pl.pallas_call(
        body, name="ada_bwd", grid=(NA // tn,),
        in_specs=[pl.BlockSpec((B, D), lambda j: (0, 0)), pl.BlockSpec((B, tn), lambda j: (0, j))],
        out_specs=pl.BlockSpec((D, tn), lambda j: (0, j)),
        out_shape=jax.ShapeDtypeStruct((D, NA), F32),
        compiler_params=_params(("arbitrary",)),
    )(c_all, dmod)


def _pack_rows(parts, n_rows, after=()):
    N = parts[0].shape[1]
    n = len(parts)

    def body(*refs):
        o_ref = refs[-1]
        o_ref[...] = jnp.zeros_like(o_ref)
        at = 0
        for r in refs[:n]:
            o_ref[at:at + r.shape[0], :] = r[...]
            at += r.shape[0]

    vmem = pl.BlockSpec(memory_space=pltpu.VMEM)
    return pl.pallas_call(body, name="pack_small", out_shape=jax.ShapeDtypeStruct((n_rows, N), F32),
                          in_specs=[vmem] * n + [ANY] * len(after), out_specs=vmem,
                          compiler_params=_params())(*parts, *after)


def _sum8(parts):
    _, R, N = parts.shape

    def body(p_ref, o_ref):
        acc = p_ref[0]
        for d in range(1, 8):
            acc = acc + p_ref[d]
        o_ref[...] = acc

    return pl.pallas_call(body, name="sum8", out_shape=jax.ShapeDtypeStruct((R, N), F32),
                          compiler_params=_params())(parts)


def _adam_math(w, g, m, v):
    m = ADAM_B1 * m + (1.0 - ADAM_B1) * g
    v = ADAM_B2 * v + (1.0 - ADAM_B2) * (g * g)
    delta = -ADAM_LR * ((m / ADAM_C1) / (jnp.sqrt(v / ADAM_C2) + ADAM_EPS) + ADAM_WD * w)
    return delta, m, v


def _adam(name, w, m, v, g, carry=None):
    R, C = w.shape
    tm = _row_tile(R, C * 4, 1 << 20)
    steps = R // tm
    n_ci = len(carry.ins) if carry else 0
    n_co = len(carry.outs) if carry else 0

    def body(*refs):
        w_ref, m_ref, v_ref, g_ref = refs[:4]
        d_ref, nm_ref, nv_ref = refs[4 + n_ci:7 + n_ci]
        c_ins, c_outs, c_sems = refs[4:4 + n_ci], refs[7 + n_ci:7 + n_ci + n_co], refs[7 + n_ci + n_co:]
        if carry:
            @pl.when(pl.program_id(0) == 0)
            def _():
                carry.start(c_ins, c_outs, c_sems)

        delta, nm, nv = _adam_math(w_ref[...], g_ref[...], m_ref[...], v_ref[...])
        d_ref[...] = delta
        nm_ref[...] = nm
        nv_ref[...] = nv
        if carry:
            @pl.when(pl.program_id(0) == steps - 1)
            def _():
                carry.finish(c_ins, c_outs, c_sems)

    spec = pl.BlockSpec((tm, C), lambda i: (i, 0))
    res = pl.pallas_call(
        body, name=name, grid=(steps,), in_specs=[spec] * 4 + [ANY] * n_ci, out_specs=[spec] * 3 + [ANY] * n_co,
        out_shape=[jax.ShapeDtypeStruct((R, C), F32)] * 3 + (carry.outs if carry else []),
        scratch_shapes=carry.sems if carry else [],
        input_output_aliases=carry.io_aliases(4, 3) if carry else {},
        compiler_params=_params(("arbitrary",)),
    )(w, m, v, g, *(carry.ins if carry else []))
    return (res[:3], res[3:]) if carry else res


def _adam_halves(name, w, m, v, mine, other, core, carry=None):
    R, C = w.shape
    Rh = mine.shape[0]
    tc = max(t for t in range(128, C + 1, 128) if C % t == 0 and R * t <= (3 << 17))
    steps = C // tc
    n_ci = len(carry.ins) if carry else 0
    n_co = len(carry.outs) if carry else 0

    def body(*refs):
        c_ref, w_ref, m_ref, v_ref, a_ref, b_ref = refs[:6]
        g_ref, d_ref, nm_ref, nv_ref = refs[6 + n_ci:10 + n_ci]
        c_ins, c_outs, c_sems = refs[6:6 + n_ci], refs[10 + n_ci:10 + n_ci + n_co], refs[10 + n_ci + n_co:]
        if carry:
            @pl.when(pl.program_id(0) == 0)
            def _():
                carry.start(c_ins, c_outs, c_sems)

        first = c_ref[0] == 0
        g = jnp.concatenate([jnp.where(first, a_ref[...], b_ref[...]),
                             jnp.where(first, b_ref[0:R - Rh, :], a_ref[0:R - Rh, :])], axis=0)
        delta, nm, nv = _adam_math(w_ref[...], g, m_ref[...], v_ref[...])
        g_ref[...] = g
        d_ref[...] = delta
        nm_ref[...] = nm
        nv_ref[...] = nv
        if carry:
            @pl.when(pl.program_id(0) == steps - 1)
            def _():
                carry.finish(c_ins, c_outs, c_sems)

    spec = pl.BlockSpec((R, tc), lambda i, c_ref: (0, i))
    h_spec = pl.BlockSpec((Rh, tc), lambda i, c_ref: (0, i))
    res = pl.pallas_call(
        body, name=name, out_shape=[jax.ShapeDtypeStruct((R, C), F32)] * 4 + (carry.outs if carry else []),
        grid_spec=pltpu.PrefetchScalarGridSpec(
            num_scalar_prefetch=1, grid=(steps,), in_specs=[spec, spec, spec, h_spec, h_spec] + [ANY] * n_ci,
            out_specs=[spec] * 4 + [ANY] * n_co, scratch_shapes=carry.sems if carry else []),
        input_output_aliases=carry.io_aliases(6, 4) if carry else {},
        compiler_params=_params(("arbitrary",)),
    )(core, w, m, v, mine, other, *(carry.ins if carry else []))
    return (res[:4], res[4:]) if carry else res


def _adam_small(name, w, m, v, g):
    def body(w_ref, m_ref, v_ref, g_ref, d_ref, nm_ref, nv_ref):
        delta, nm, nv = _adam_math(w_ref[...], g_ref[...], m_ref[...], v_ref[...])
        d_ref[...] = delta
        nm_ref[...] = nm
        nv_ref[...] = nv

    return pl.pallas_call(body, name=name, out_shape=[jax.ShapeDtypeStruct(w.shape, F32)] * 3,
                          compiler_params=_params())(w, m, v, g)


def _place():
    return lax.axis_index("x"), lax.axis_index("y"), lax.axis_index("c")


def _other_chips(x, y):
    return [(1 - x, y), (x, 1 - y), (1 - x, 1 - y)]


def _all_gather8(blk, name):
    R, N = blk.shape

    def body(x_ref, out_ref, send_sems, recv_sems, local_sem):
        x, y, c = _place()
        me = 4 * x + 2 * y + c
        mine = pltpu.make_async_copy(x_ref, out_ref.at[me], local_sem)
        mine.start()
        flips = [(j >> 2 & 1, j >> 1 & 1, j & 1) for j in range(1, 8)]
        peers = [((1 - x) if fx else x, (1 - y) if fy else y, (1 - c) if fc else c) for fx, fy, fc in flips]
        sends = []
        for j, peer in enumerate(peers):
            cp = pltpu.make_async_remote_copy(src_ref=x_ref, dst_ref=out_ref.at[me], send_sem=send_sems.at[j],
                                              recv_sem=recv_sems.at[j], device_id=peer, device_id_type=MESH)
            cp.start()
            sends.append(cp)
        for j, (px, py, pc) in enumerate(peers):
            pltpu.make_async_remote_copy(src_ref=x_ref, dst_ref=out_ref.at[4 * px + 2 * py + pc],
                                         send_sem=send_sems.at[j], recv_sem=recv_sems.at[j],
                                         device_id=(px, py, pc), device_id_type=MESH).wait_recv()
        for cp in sends:
            cp.wait_send()
        mine.wait()

    return pl.pallas_call(
        body, name=name, out_shape=jax.ShapeDtypeStruct((8, R, N), F32),
        in_specs=[pl.BlockSpec(memory_space=pltpu.VMEM)], out_specs=pl.BlockSpec(memory_space=pltpu.VMEM),
        scratch_shapes=[pltpu.SemaphoreType.DMA((7,)), pltpu.SemaphoreType.DMA((7,)), pltpu.SemaphoreType.DMA],
        compiler_params=_params(),
    )(blk)


def _piece(rows, piece):
    i, n, k = piece if len(piece) == 3 else (piece[0], piece[1], 1)
    assert rows % 16 == 0 and rows // 16 >= n, (rows, piece)
    lo, hi = (rows // 16 * i // n) * 16, (rows // 16 * (i + k) // n) * 16
    return pl.ds(lo, hi - lo)


def _gather_plan(shards, piece=(0, 1), into=None, ici=True):
    n = len(shards)

    def parts(ins, outs, sems):
        s1, r1, s2, r2, loc = sems
        x, y, c = _place()
        me = 2 * x + y
        chips = _other_chips(x, y)
        sib = (x, y, 1 - c)

        def rows(k):
            return _piece(shards[k].shape[1], piece)

        def ici_copy(k, j, slab, to):
            return pltpu.make_async_remote_copy(src_ref=ins[k].at[c, rows(k)], dst_ref=outs[k].at[slab, c, rows(k)],
                                                send_sem=s1.at[3 * k + j], recv_sem=r1.at[3 * k + j],
                                                device_id=to, device_id_type=MESH)

        def d2d(k, j, slab, half):
            return pltpu.make_async_remote_copy(src_ref=outs[k].at[slab, half, rows(k)],
                                                dst_ref=outs[k].at[slab, half, rows(k)],
                                                send_sem=s2.at[3 * k + j], recv_sem=r2.at[3 * k + j],
                                                device_id=sib, device_id_type=MESH)

        def own(k):
            return pltpu.make_async_remote_copy(src_ref=ins[k].at[:, rows(k)], dst_ref=outs[k].at[me, :, rows(k)],
                                                send_sem=loc.at[2 * k], recv_sem=loc.at[2 * k + 1],
                                                device_id=sib, device_id_type=MESH)

        return c, me, chips, ici_copy, d2d, own

    def start(ins, outs, sems):
        c, me, chips, ici_copy, d2d, own = parts(ins, outs, sems)
        for k in range(n):
            for j, (px, py) in enumerate(chips):
                (ici_copy(k, j, me, (px, py, c)) if ici else d2d(k, j, 2 * px + py, c)).start()
        for k in range(n):
            own(k).start()

    def finish(ins, outs, sems):
        c, me, chips, ici_copy, d2d, own = parts(ins, outs, sems)
        if ici:
            for k in range(n):
                for j, (px, py) in enumerate(chips):
                    ici_copy(k, j, 2 * px + py, (px, py, c)).wait_recv()
                    d2d(k, j, 2 * px + py, c).start()
        for k in range(n):
            for j, (px, py) in enumerate(chips):
                d2d(k, j, 2 * px + py, 1 - c).wait_recv()
        for k in range(n):
            own(k).wait()
            for j, (px, py) in enumerate(chips):
                if ici:
                    ici_copy(k, j, me, (px, py, c)).wait_send()
                d2d(k, j, 2 * px + py, c).wait_send()

    return _Plan(list(shards) + list(into or []), [jax.ShapeDtypeStruct((4,) + a.shape, a.dtype) for a in shards],
                 [pltpu.SemaphoreType.DMA((3 * n,))] * 4 + [pltpu.SemaphoreType.DMA((2 * n,))], start, finish,
                 aliases={n + k: k for k in range(n)} if into else None)


def _pair_plan(parts):
    n = len(parts)

    def copies(ins, outs, sems):
        send_sems, recv_sems = sems
        x, y, c = _place()
        return [pltpu.make_async_remote_copy(src_ref=ins[k].at[p, 1 - c], dst_ref=outs[k].at[p],
                                             send_sem=send_sems.at[4 * k + p], recv_sem=recv_sems.at[4 * k + p],
                                             device_id=(x, y, 1 - c), device_id_type=MESH)
                for k in range(n) for p in range(4)]

    def start(ins, outs, sems):
        for cp in copies(ins, outs, sems):
            cp.start()

    def finish(ins, outs, sems):
        for cp in copies(ins, outs, sems):
            cp.wait()

    return _Plan(parts, [jax.ShapeDtypeStruct((4,) + a.shape[2:], a.dtype) for a in parts],
                 [pltpu.SemaphoreType.DMA((4 * n,))] * 2, start, finish)


def _sibling_plan(arrs):
    n = len(arrs)

    def copies(ins, outs, sems):
        send_sems, recv_sems = sems
        x, y, c = _place()
        return [pltpu.make_async_remote_copy(src_ref=ins[k], dst_ref=outs[k], send_sem=send_sems.at[k],
                                             recv_sem=recv_sems.at[k], device_id=(x, y, 1 - c), device_id_type=MESH)
                for k in range(n)]

    def start(ins, outs, sems):
        for cp in copies(ins, outs, sems):
            cp.start()

    def finish(ins, outs, sems):
        for cp in copies(ins, outs, sems):
            cp.wait()

    return _Plan(arrs, [jax.ShapeDtypeStruct(a.shape, a.dtype) for a in arrs],
                 [pltpu.SemaphoreType.DMA((n,))] * 2, start, finish)


def _scatter_copies(arrs):
    def copies(ins, land, send_sems, recv_sems):
        x, y, c = _place()
        return [pltpu.make_async_remote_copy(src_ref=ins[k].at[2 * px + py], dst_ref=land[k].at[j],
                                             send_sem=send_sems.at[3 * k + j], recv_sem=recv_sems.at[3 * k + j],
                                             device_id=(px, py, c), device_id_type=MESH)
                for k in range(len(arrs)) for j, (px, py) in enumerate(_other_chips(x, y))]

    return copies, [lax.empty((3,) + a.shape[1:], a.dtype) for a in arrs]


def _gather_copies(shards, piece=(0, 1), lands=None):
    def copies(ins, land, send_sems, recv_sems):
        x, y, c = _place()
        return [pltpu.make_async_remote_copy(
                    src_ref=ins[k].at[c, _piece(shards[k].shape[1], piece)],
                    dst_ref=land[k].at[2 * x + y, c, _piece(shards[k].shape[1], piece)],
                    send_sem=send_sems.at[3 * k + j], recv_sem=recv_sems.at[3 * k + j],
                    device_id=(px, py, c), device_id_type=MESH)
                for k in range(len(shards)) for j, (px, py) in enumerate(_other_chips(x, y))]

    return copies, list(lands) if lands is not None else [lax.empty((4,) + a.shape, a.dtype) for a in shards]


def _split_start(arrs, copies_lands, ride, name, after=()):
    copies, lands = copies_lands
    n = len(arrs)
    rides = list(ride) if isinstance(ride, (list, tuple)) else [ride]
    n_thru = 2 * n + len(rides)

    def body(*refs):
        first_out = n_thru + len(after)
        for cp in copies(refs[:n], refs[n:2 * n], refs[first_out], refs[first_out + 1]):
            cp.start()

    hbm = [pltpu.with_memory_space_constraint(a, pltpu.HBM) for a in list(arrs) + lands + rides]
    res = pl.pallas_call(
        body, name=name,
        out_shape=[pltpu.SemaphoreType.DMA((3 * n,)), pltpu.SemaphoreType.DMA((3 * n,))]
        + [pltpu.HBM(a.shape, a.dtype) for a in hbm],
        in_specs=[HBM_SPEC] * n_thru + [ANY] * len(after),
        out_specs=[SEM_SPEC, SEM_SPEC] + [HBM_SPEC] * n_thru,
        input_output_aliases={i: 2 + i for i in range(n_thru)},
        compiler_params=pltpu.CompilerParams(has_side_effects=pltpu.SideEffectType.DATAFLOW_SIDE_EFFECTING),
    )(*hbm, *after)
    return res[0], res[1], res[2:2 + n], res[2 + n:2 + 2 * n], list(res[2 + 2 * n:])


def _split_wait(started, copies_lands, after, name):
    send_sems, recv_sems, arrs, lands, _ = started
    copies = copies_lands[0]
    n = len(arrs)

    def body(*refs):
        for cp in copies(refs[:n], refs[n:2 * n], refs[2 * n], refs[2 * n + 1]):
            cp.wait_send()
            cp.wait_recv()

    res = pl.pallas_call(
        body, name=name, out_shape=[pltpu.HBM(a.shape, a.dtype) for a in list(arrs) + list(lands)],
        in_specs=[HBM_SPEC] * (2 * n) + [SEM_SPEC, SEM_SPEC] + [ANY] * len(after), out_specs=[HBM_SPEC] * (2 * n),
        input_output_aliases={i: i for i in range(2 * n)},
        compiler_params=pltpu.CompilerParams(has_side_effects=pltpu.SideEffectType.DATAFLOW_SIDE_EFFECTING),
    )(*arrs, *lands, send_sems, recv_sems, *after)
    return list(res[:n]), list(res[n:])


def _add_pair(parts, sib, core, name):
    P4, _, Rh, C = parts.shape
    tm, tc = _tile2(Rh, C, 16)

    def body(c_ref, a_ref, b_ref, o_ref):
        o_ref[...] = (a_ref[0].astype(F32) + b_ref[...].astype(F32)).astype(BF16)

    spec = pl.BlockSpec((1, tm, tc), lambda p, i, j, c_ref: (p, i, j))
    return pl.pallas_call(
        body, name=name, out_shape=jax.ShapeDtypeStruct((P4, Rh, C), BF16),
        grid_spec=pltpu.PrefetchScalarGridSpec(
            num_scalar_prefetch=1, grid=(P4, Rh // tm, C // tc),
            in_specs=[pl.BlockSpec((1, 1, tm, tc), lambda p, i, j, c_ref: (p, c_ref[0], i, j)), spec], out_specs=spec),
        compiler_params=_params(("parallel",) * 3),
    )(core, parts, sib)


def _sum_slabs(pre, recv, chip, name):
    _, Rh, C = pre.shape
    tm, tc = _tile2(Rh, C, 16)

    def body(me_ref, own_ref, r_ref, o_ref):
        acc = own_ref[0].astype(F32)
        for j in range(3):
            acc = acc + r_ref[j].astype(F32)
        o_ref[...] = acc

    return pl.pallas_call(
        body, name=name, out_shape=jax.ShapeDtypeStruct((Rh, C), F32),
        grid_spec=pltpu.PrefetchScalarGridSpec(
            num_scalar_prefetch=1, grid=(Rh // tm, C // tc),
            in_specs=[pl.BlockSpec((1, tm, tc), lambda i, j, me_ref: (me_ref[0], i, j)),
                      pl.BlockSpec((3, tm, tc), lambda i, j, me_ref: (0, i, j))],
            out_specs=pl.BlockSpec((tm, tc), lambda i, j, me_ref: (i, j))),
        compiler_params=_params(("parallel", "parallel")),
    )(chip, pre, recv)


def kernel(x, c, positions, w_ada, b_ada, w_in, g_q_a, w_q_b, g_kv_a, w_kv_b, w_o_a, w_conv, w_o_b, w_o, ln1_g, ln1_b, w_ffn_in, w_ffn_out, ln2_g, ln2_b, loss_target, m_w_ada, m_b_ada, m_w_in, m_g_q_a, m_w_q_b, m_g_kv_a, m_w_kv_b, m_w_o_a, m_w_conv, m_w_o_b, m_w_o, m_ln1_g, m_ln1_b, m_w_ffn_in, m_w_ffn_out, m_ln2_g, m_ln2_b, v_w_ada, v_b_ada, v_w_in, v_g_q_a, v_w_q_b, v_g_kv_a, v_w_kv_b, v_w_o_a, v_w_conv, v_w_o_b, v_w_o, v_ln1_g, v_ln1_b, v_w_ffn_in, v_w_ffn_out, v_ln2_g, v_ln2_b):
    S, D = x.shape[1], x.shape[2]
    F = w_ffn_out.shape[1] * 4
    ax, ay, ac = _place()
    chip = 2 * ax + ay
    dev = 4 * ax + 2 * ay + ac
    x2, tgt = x[0], loss_target[0]
    w_ada2, w_in2, w_q_b2, w_kv_b2 = w_ada[0], w_in[0], w_q_b[0], w_kv_b[0]
    w_o_a2, w_o_b2, w_o2, w_ffn_in2, w_ffn_out2 = w_o_a[0], w_o_b[0], w_o[0], w_ffn_in[0], w_ffn_out[0]
    NA = w_ada2.shape[1]
    CW = w_conv.shape[2]

    inv_freq = 1.0 / (ROPE_THETA ** (jnp.arange(0, QK_ROPE, 2, dtype=F32) / QK_ROPE))
    ang = positions[0].astype(F32)[:, None] * inv_freq
    cos, sin = jnp.cos(ang), jnp.sin(ang)
    z32, z64, z96 = jnp.zeros((S, 32), F32), jnp.zeros((S, 64), F32), jnp.zeros((S, 96), F32)
    tab = jnp.concatenate([cos, cos, z64, -sin, z96, z32, sin, z64], axis=1)

    def halves(a):
        return a.reshape(2, a.shape[0] // 2, a.shape[1])

    def whole(g):
        return g.reshape(4, 2 * g.shape[2], g.shape[3])

    def cols(g):
        return jnp.transpose(g, (1, 0, 2)).reshape(g.shape[1], 4 * g.shape[2])

    w_inT, m_w_inT, v_w_inT = w_in2.T, m_w_in[0].T, v_w_in[0].T
    CS = w_inT.shape[0]
    CSP = -(-CS // 32) * 32
    sh_in = halves(jnp.pad(w_inT.astype(BF16), ((0, CSP - CS), (0, 0))))
    c_all = _all_gather8(c, "gather_c").reshape(8, D)
    wconv_all = _all_gather8(w_conv[0], "gather_wconv")
    w_conv_full = jnp.transpose(wconv_all[0::2], (1, 0, 2)).reshape(3, D)
    b_sh = lax.dynamic_slice(b_ada, (0, chip * NA), (1, NA))
    mod_sh = _ada_fwd(c_all, w_ada2, b_sh)
    mod_all = _all_gather8(mod_sh, "gather_mod")
    mod = lax.dynamic_slice(mod_all[0::2], (0, dev, 0), (4, 1, NA)).reshape(6, D)
    shift1, scale1, gate1, shift2, scale2, gate2 = (mod[k:k + 1] for k in range(6))

    n_pc = 8
    st_in, cl_in, sh_in_t, l_in, rides = [], [], [sh_in], None, [shift1, w_conv_full]
    for i in range(n_pc):
        cl_in.append(_gather_copies(sh_in_t, (i, n_pc), l_in))
        st_in.append(_split_start(sh_in_t, cl_in[i], rides, "gather_in%d_start" % i))
        sh_in_t, l_in, rides = st_in[i][2], st_in[i][3], st_in[i][4]
    shift1, w_conv_full = rides
    others = lax.optimization_barrier((w_q_b2, w_kv_b2, w_o_a2, w_o_b2, w_o2, w_ffn_in2, w_ffn_out2, shift1))
    sh_qb, sh_kvb, sh_oa, sh_ob, sh_o, sh_fi, sh_fo = (halves(w.astype(BF16)) for w in others[:7])
    shift1 = others[7]
    for i in range(n_pc):
        casts = [sh_qb, sh_kvb, sh_oa, sh_ob, sh_o, sh_fi, sh_fo] if i == 0 else []
        sh_in_t, l_in = _split_wait(st_in[i][:2] + (sh_in_t, l_in, None), cl_in[i], casts, "gather_in%d_wait" % i)
        if i < n_pc - 1:
            l_in = _run_plan(_gather_plan(sh_in_t, (i, n_pc), into=l_in, ici=False), "handon_in%d" % i)
    sh_a1, sh_a2 = [sh_qb, sh_kvb], [sh_oa, sh_ob, sh_o]
    cl_a1, cl_a2, cl_fi, cl_fo = (_gather_copies(g) for g in (sh_a1, sh_a2, [sh_fi], [sh_fo]))
    st_a1 = _split_start(sh_a1, cl_a1, shift1, "gather_a1_start", after=[l_in[0]])
    st_a2 = _split_start(sh_a2, cl_a2, st_a1[4], "gather_a2_start")
    u, (g_in,) = _modulate(x2, scale1, st_a2[4][0], "modulate1",
                           carry=_gather_plan(sh_in_t, (n_pc - 1, n_pc), into=l_in, ici=False))
    g_in = whole(g_in)

    def in_rows(lo, hi):
        parts = [g_in[p, max(lo, p * CS) - p * CS:min(hi, (p + 1) * CS) - p * CS]
                 for p in range(4) if max(lo, p * CS) < min(hi, (p + 1) * CS)]
        return parts[0] if len(parts) == 1 else jnp.concatenate(parts, axis=0)

    n_qkv = Q_LORA + KV_LORA + QK_ROPE
    W_qkvT = jnp.pad(in_rows(0, n_qkv), ((0, QKV_A - n_qkv), (0, 0)))
    W_convT = in_rows(n_qkv, n_qkv + 3 * D)
    W_gateT = in_rows(n_qkv + 3 * D, n_qkv + 5 * D)

    pq = _matmul(u, W_qkvT, "nt", F32, "proj_qkv")
    pc = _matmul(u, W_convT, "nt", BF16, "proj_conv")
    sh_a1, la1 = _split_wait(st_a1, cl_a1, [pc], "gather_a1_wait")
    pg, (g_qb, g_kvb) = _matmul(u, W_gateT, "nt", BF16, "proj_gate", carry=_gather_plan(sh_a1, into=la1, ici=False))
    st_fi = _split_start([sh_fi], cl_fi, g_q_a, "gather_fi_start", after=[pg])
    W_qb = jnp.pad(cols(whole(g_qb)).reshape(Q_LORA, N_HEADS, QK_NOPE + QK_ROPE),
                   ((0, 0), (0, 0), (0, QK_PAD - QK_NOPE - QK_ROPE))).reshape(Q_LORA, N_HEADS * QK_PAD)
    W_kvb = cols(whole(g_kvb))
    rq, rkv, kr = _rms_fwd(pq, tab, st_fi[4][0], g_kv_a)
    kv = _matmul(rkv, W_kvb, "nn", BF16, "kv_b")
    sh_a2, la2 = _split_wait(st_a2, cl_a2, [kv], "gather_a2_wait")
    def rope_heads(r, t):
        return jnp.concatenate([r[:, lo:lo + 128] if lo % QK_PAD == 0 else _rope(r[:, lo:lo + 128], t, 1)
                                for lo in range(0, r.shape[1], 128)], axis=1)

    q, (g_oa, g_ob, g_o) = _matmul(rq, W_qb, "nn", BF16, "q_b", carry=_gather_plan(sh_a2, into=la2, ici=False),
                                   finish=(rope_heads, tab))
    o, lse = _attn_fwd(q, kv, kr)
    W_oa, W_ob, W_o = (g.reshape(-1, D) for g in (g_oa, g_ob, g_o))
    hb = _conv_fwd(pc, w_conv_full)
    st_fo = _split_start([sh_fo], cl_fo, ln1_g, "gather_fo_start", after=[o])
    y_b = _matmul(hb, W_ob, "nn", BF16, "o_b")
    y_a = _matmul(o, W_oa, "nn", BF16, "o_a")
    merged = _merge_fwd(y_a, y_b, pg)
    sh_fi_t, lfi = _split_wait(st_fi, cl_fi, [merged], "gather_fi_wait")
    mix, g_fi = _matmul(merged, W_o, "nn", F32, "w_o", carry=_gather_plan(sh_fi_t, (0, 2), into=lfi, ici=False))
    (x1, u2), (g_fi,) = _ln1_fwd(x2, mix, gate1, st_fo[4][0], ln1_b, scale2, shift2,
                                 carry=_gather_plan(sh_fi_t, (1, 2), into=g_fi, ici=False))
    W_fi = whole(g_fi)
    hh = _matmul(u2, W_fi, "nn", BF16, "ffn_in", shards="b")
    sh_fo_t, lfo = _split_wait(st_fo, cl_fo, [hh], "gather_fo_wait")
    act, (g_fo,) = _swiglu_fwd(hh, carry=_gather_plan(sh_fo_t, into=lfo, ici=False))
    W_fo = g_fo.reshape(F, D)
    ffn = _matmul(act, W_fo, "nn", F32, "ffn_out")

    core_i = ac.astype(jnp.int32).reshape(1)
    chip_i = chip.astype(jnp.int32).reshape(1)

    def uncols(g):
        return jnp.transpose(g.reshape(g.shape[0], 4, g.shape[1] // 4), (1, 0, 2))

    def slabs(p):
        return p.reshape(4, 2, p.shape[1] // 2, p.shape[2])

    def add_pairs(parts, sibs, nms):
        return [_add_pair(a, b, core_i, "add_pair_" + nm) for a, b, nm in zip(parts, sibs, nms)]

    def sum_all(pre, recv, nms):
        return [_sum_slabs(a, r, chip_i, "sum_slabs_" + nm) for a, r, nm in zip(pre, recv, nms)]

    dffn, dx1a, loss_acc, d_ln2_g, d_ln2_b, d_gate2 = _ln2_loss_bwd(x1, ffn, gate2, ln2_g, ln2_b, tgt)
    dW_fo = _matmul(act, dffn, "tn", BF16, "d_w_ffn_out")
    p_fo = [slabs(dW_fo.reshape(4, -1, D))]
    dact, s_fo = _matmul(dffn, W_fo, "nt", BF16, "d_act", carry=_pair_plan(p_fo))
    pre_fo = add_pairs(p_fo, s_fo, ["w_ffn_out"])
    cs_fo = _scatter_copies(pre_fo)
    st_sfo = _split_start(pre_fo, cs_fo, scale2, "scatter_fo_start")
    dhh = _swiglu_bwd(dact, hh)
    dW_fi = _matmul(u2, dhh, "tn", BF16, "d_w_ffn_in", shards="o")
    p_fi = [slabs(dW_fi)]
    du2, s_fi = _matmul(dhh, W_fi, "nt", F32, "d_u2", carry=_pair_plan(p_fi), shards="b")
    pre_fi = add_pairs(p_fi, s_fi, ["w_ffn_in"])
    cs_fi = _scatter_copies(pre_fi)
    st_sfi = _split_start(pre_fi, cs_fi, st_sfo[4], "scatter_fi_start")
    dmix, dxa, d_shift2, d_scale2, d_ln1_g, d_ln1_b, d_gate1 = _ln1_bwd(x2, mix, dx1a, du2, gate1, ln1_g, ln1_b, st_sfi[4][0])
    dW_o = _matmul(merged, dmix, "tn", BF16, "d_w_o")
    dmerged = _matmul(dmix, W_o, "nt", BF16, "d_merged")
    dy_a, dy_b, dgate = _merge_bwd(dmerged, y_a, y_b, pg)
    dW_oa = _matmul(o, dy_a, "tn", BF16, "d_w_o_a")
    do = _matmul(dy_a, W_oa, "nt", BF16, "d_o")
    dW_ob = _matmul(hb, dy_b, "tn", BF16, "d_w_o_b")
    p_mid = [slabs(g.reshape(4, -1, D)) for g in (dW_oa, dW_ob, dW_o)]
    dhb, s_mid = _matmul(dy_b, W_ob, "nt", BF16, "d_hb", carry=_pair_plan(p_mid))
    pre_mid = add_pairs(p_mid, s_mid, ["w_o_a", "w_o_b", "w_o"])
    cs_mid = _scatter_copies(pre_mid)
    st_smid = _split_start(pre_mid, cs_mid, w_conv_full, "scatter_mid_start")
    dconv, d_wconv = _conv_bwd(dhb, pc, st_smid[4][0])
    dq, dkv, dkr, _ = _attn_bwd(q, kv, kr, do, o, lse, tab, carry=_token_plan(st_smid[4][0]))
    names_a = ["w_ffn_out", "w_ffn_in", "w_o_a", "w_o_b", "w_o"]
    dW_qb = _matmul(rq, dq, "tn", BF16, "d_w_q_b")
    d_rq = _matmul(dq, W_qb, "nt", F32, "d_rq")
    dW_kvb = _matmul(rkv, dkv, "tn", BF16, "d_w_kv_b")
    d_rkv = _matmul(dkv, W_kvb, "nt", F32, "d_rkv")
    dqkv, d_g_q, d_g_kv = _rms_bwd(d_rq, d_rkv, pq, dkr, g_q_a, g_kv_a)
    dW_qkvT = _matmul(dqkv, u, "tn", BF16, "d_w_qkv")
    dW_convT = _matmul(dconv, u, "tn", BF16, "d_w_conv")
    dW_gateT = _matmul(dgate, u, "tn", BF16, "d_w_gate")
    pre_fo, r_fo = _split_wait(st_sfo, cs_fo, [dW_qkvT], "scatter_fo_wait")
    pre_fi, r_fi = _split_wait(st_sfi, cs_fi, [dW_qkvT], "scatter_fi_wait")
    pre_mid, r_mid = _split_wait(st_smid, cs_mid, [dW_qkvT], "scatter_mid_wait")
    fin_a = sum_all(pre_fo + pre_fi + pre_mid, r_fo + r_fi + r_mid, names_a)
    srcs = [(0, dW_qkvT[:n_qkv]), (n_qkv, dW_convT), (n_qkv + 3 * D, dW_gateT)]
    rows_of = []
    for p in range(4):
        for lo, src in srcs:
            a, b = max(lo, p * CS), min(lo + src.shape[0], (p + 1) * CS)
            if a < b:
                rows_of.append(src[a - lo:b - lo])
        rows_of.append(jnp.zeros((CSP - CS, D), BF16))
    dW_inT = jnp.concatenate(rows_of, axis=0).reshape(4, CSP, D)
    dW_qb_u = dW_qb.reshape(Q_LORA, N_HEADS, QK_PAD)[:, :, :QK_NOPE + QK_ROPE].reshape(Q_LORA, -1)
    names_b = ["w_in", "w_q_b", "w_kv_b"]
    p_b = [slabs(dW_inT), slabs(uncols(dW_qb_u)), slabs(uncols(dW_kvb))]
    du, s_b = _matmul(dqkv, W_qkvT, "nn", F32, "d_u_qkv", carry=_pair_plan(p_b))
    pre_b = add_pairs(p_b, s_b, names_b)
    cs_b = _scatter_copies(pre_b)
    st_b = _split_start(pre_b, cs_b, scale1, "scatter_last_start")
    du, fs_a = _matmul(dconv, W_convT, "nn", F32, "d_u_conv", add=du, carry=_sibling_plan(fin_a))
    du = _matmul(dgate, W_gateT, "nn", F32, "d_u_gate", add=du)
    grad_x, d_shift1, d_scale1 = _dx_final(dxa, du, x2, st_b[4][0])

    big = {}
    ws = dict(w_in=(w_inT, m_w_inT, v_w_inT), w_q_b=(w_q_b2, m_w_q_b[0], v_w_q_b[0]),
              w_kv_b=(w_kv_b2, m_w_kv_b[0], v_w_kv_b[0]), w_o_a=(w_o_a2, m_w_o_a[0], v_w_o_a[0]),
              w_o_b=(w_o_b2, m_w_o_b[0], v_w_o_b[0]), w_o=(w_o2, m_w_o[0], v_w_o[0]),
              w_ffn_in=(w_ffn_in2, m_w_ffn_in[0], v_w_ffn_in[0]), w_ffn_out=(w_ffn_out2, m_w_ffn_out[0], v_w_ffn_out[0]))

    def adam_of(nm, a, b, carry=None):
        w_, m_, v_ = ws[nm]
        return _adam_halves("adam_" + nm, w_, m_, v_, a, b, core_i, carry)

    for nm, a, b in zip(names_a, fin_a, fs_a):
        big[nm] = adam_of(nm, a, b, _token_plan(st_b[4][0]))[0]
    done = [big[nm][1] for nm in names_a] + [grad_x]
    pre_b, r_b = _split_wait(st_b, cs_b, done, "scatter_last_wait")
    fin_b = sum_all(pre_b, r_b, names_b)
    fs_b = _run_plan(_sibling_plan(fin_b), "sibling_last")
    for nm, a, b in zip(names_b, fin_b, fs_b):
        big[nm] = adam_of(nm, a, b)

    def pad_d(v):
        return jnp.pad(v, ((0, 0), (0, D - v.shape[1])))

    small = _pack_rows([d_ln1_g, d_ln1_b, d_ln2_g, d_ln2_b, pad_d(d_g_q), pad_d(d_g_kv), d_wconv,
                         d_shift1, d_scale1, d_gate1, d_shift2, d_scale2, d_gate2, pad_d(loss_acc)], 16, after=[pre_b[1]])
    small_all = _all_gather8(small, "gather_small")
    small_sum = _sum8(small_all)
    loss = small_sum[15, 0]
    g_ln1_g, g_ln1_b, g_ln2_g, g_ln2_b = (small_sum[k:k + 1] for k in range(4))
    g_g_q, g_g_kv = small_sum[4:5, :Q_LORA], small_sum[5:6, :KV_LORA]
    g_wconv = lax.dynamic_slice(small_sum[6:9], (0, chip * CW), (3, CW))
    g_b_ada = small_sum[9:15].reshape(1, 6 * D)
    dmod_all = small_all[:, 9:15, :].reshape(8, 6 * D)
    g_w_ada = _ada_bwd(c_all, lax.dynamic_slice(dmod_all, (0, chip * NA), (8, NA)))
    big["w_ada"] = [g_w_ada] + list(_adam("adam_w_ada", w_ada2, m_w_ada[0], v_w_ada[0], g_w_ada))
    sm = {}
    for nm, w_, m_, v_, g_ in [("b_ada", b_ada, m_b_ada, v_b_ada, g_b_ada), ("g_q_a", g_q_a, m_g_q_a, v_g_q_a, g_g_q),
                               ("g_kv_a", g_kv_a, m_g_kv_a, v_g_kv_a, g_g_kv),
                               ("w_conv", w_conv[0], m_w_conv[0], v_w_conv[0], g_wconv),
                               ("ln1_g", ln1_g, m_ln1_g, v_ln1_g, g_ln1_g), ("ln1_b", ln1_b, m_ln1_b, v_ln1_b, g_ln1_b),
                               ("ln2_g", ln2_g, m_ln2_g, v_ln2_g, g_ln2_g), ("ln2_b", ln2_b, m_ln2_b, v_ln2_b, g_ln2_b)]:
        sm[nm] = (g_,) + tuple(_adam_small("adam_" + nm, w_, m_, v_, g_))

    order = ["w_ada", "b_ada", "w_in", "g_q_a", "w_q_b", "g_kv_a", "w_kv_b", "w_o_a", "w_conv", "w_o_b", "w_o",
             "ln1_g", "ln1_b", "w_ffn_in", "w_ffn_out", "ln2_g", "ln2_b"]
    lead = {"b_ada", "g_q_a", "g_kv_a", "ln1_g", "ln1_b", "ln2_g", "ln2_b"}

    def leaf(nm, k):
        val = big[nm][k] if nm in big else sm[nm][k]
        if nm == "w_in":
            val = val.T
        return val if nm in lead else val[None]

    outs = [loss, grad_x[None]]
    for k in range(4):
        outs += [leaf(nm, k) for nm in order]
    return tuple(outs)
```

```python
import jax
import jax.numpy as jnp
from jax import lax
from jax.experimental import pallas as pl
from jax.experimental.pallas import tpu as pltpu

F32, BF16 = jnp.float32, jnp.bfloat16
N_HEADS, QK_NOPE, QK_ROPE, V_HEAD = 16, 128, 64, 128
Q_LORA, KV_LORA = 512, 512
QK_PAD = 256
QKV_A = 1152
CHUNK_SHIFT = 6
ATTN_SCALE = (QK_NOPE + QK_ROPE) ** -0.5
LOG2E = 1.4426950408889634
SCALE2 = ATTN_SCALE * LOG2E
ROPE_THETA = 10000.0
ALPHA = 2.0 ** 0.25
LN_EPS, RMS_EPS = 1e-5, 1e-6
ADAM_LR, ADAM_B1, ADAM_B2, ADAM_EPS, ADAM_WD, ADAM_STEP = 0.001, 0.9, 0.999, 1e-08, 0.01, 10
ADAM_C1 = 1.0 - ADAM_B1 ** ADAM_STEP
ADAM_C2 = 1.0 - ADAM_B2 ** ADAM_STEP
VMEM_LIMIT = 56 * 1024 * 1024
MESH = pl.DeviceIdType.MESH
ANY = pl.BlockSpec(memory_space=pl.ANY)
HBM_SPEC = pl.BlockSpec(memory_space=pltpu.HBM)
SEM_SPEC = pl.BlockSpec(memory_space=pltpu.SEMAPHORE)
NT = (((1,), (1,)), ((), ()))
TN = (((0,), (0,)), ((), ()))
NN = (((1,), (0,)), ((), ()))


def _params(sem=None):
    return pltpu.CompilerParams(dimension_semantics=sem, vmem_limit_bytes=VMEM_LIMIT)


def _pick(n, cands=(1408, 1024, 512, 384, 256, 128)):
    for t in cands:
        if n % t == 0:
            return t
    return n


def _row_tile(rows, row_bytes, budget, mult=8):
    best = mult
    for t in range(mult, rows + 1, mult):
        if rows % t == 0 and t * row_bytes <= budget:
            best = t
    return best


def _tile2(rows, cols, mult=8, budget=3 << 18):
    col_tiles = [t for t in range(128, cols + 1, 128) if cols % t == 0] or [cols]
    best = None
    for tc in col_tiles:
        for tr in range(mult, rows + 1, mult):
            if rows % tr == 0 and tr * tc <= budget and (best is None or (tr * tc, tc) > (best[0] * best[1], best[1])):
                best = (tr, tc)
    assert best is not None, (rows, cols)
    return best


def _sigmoid(x):
    return jax.nn.sigmoid(x)


class _Plan:
    def __init__(self, ins, outs, sems, start, finish, aliases=None):
        self.ins, self.outs, self.sems, self.start, self.finish = list(ins), list(outs), list(sems), start, finish
        self.aliases = dict(aliases or {})

    def io_aliases(self, first_in, first_out):
        return {first_in + i: first_out + o for i, o in self.aliases.items()}


def _token_plan(token):
    return _Plan([token], [], [], lambda *a: None, lambda *a: None)


def _run_plan(plan, name, ride=None):
    n_in, n_out = len(plan.ins), len(plan.outs)
    extra = [] if ride is None else list(ride)
    aliases = plan.io_aliases(0, 0)
    for k in range(len(extra)):
        aliases[n_in + k] = n_out + k

    def body(*refs):
        ins, outs, sems = refs[:n_in], refs[n_in + len(extra):n_in + len(extra) + n_out], refs[n_in + 2 * len(extra) + n_out:]
        plan.start(ins, outs, sems)
        plan.finish(ins, outs, sems)

    return pl.pallas_call(body, name=name, out_shape=plan.outs + [jax.ShapeDtypeStruct(r.shape, r.dtype) for r in extra],
                          in_specs=[ANY] * (n_in + len(extra)), out_specs=[ANY] * (n_out + len(extra)),
                          scratch_shapes=plan.sems, input_output_aliases=aliases,
                          compiler_params=_params())(*plan.ins, *extra)


def _matmul(a, b, mode, out_dtype, name, add=None, carry=None, shards=None, finish=None):
    if mode == "nn":
        (M, K), N, dims = a.shape, b.shape[-1] * (4 if shards else 1), NN
    elif mode == "nt":
        (M, K), N, dims = a.shape, b.shape[-2], NT
    else:
        (K, M), N, dims = a.shape, b.shape[1], TN
    split_n = shards and mode != "nt"
    tm = _pick(M)
    tn = _pick(N // 4) if split_n else _pick(N)
    deep = (2816, 2048, 1408, 1024, 512, 384, 256, 128)
    if shards and mode == "nt":
        tk = _pick(K // 4, deep)
    else:
        tk = K if K <= 2048 else _pick(K, deep)
    nk = K // tk
    per = (N // 4 // tn) if split_n else (K // 4 // tk if shards else 1)
    a_spec = (pl.BlockSpec((tk, tm), lambda i, j, k: (k, i)) if mode == "tn"
              else pl.BlockSpec((tm, tk), lambda i, j, k: (i, k)))
    if shards == "b" and mode == "nn":
        b_spec = pl.BlockSpec((None, tk, tn), lambda i, j, k: (j // per, k, j % per))
    elif shards == "b":
        b_spec = pl.BlockSpec((None, tn, tk), lambda i, j, k: (k // per, j, k % per))
    else:
        b_spec = (pl.BlockSpec((tn, tk), lambda i, j, k: (j, k)) if mode == "nt"
                  else pl.BlockSpec((tk, tn), lambda i, j, k: (k, j)))
    o_spec = pl.BlockSpec((tm, tn), lambda i, j, k: (i, j))
    o_shape = (M, N)
    if shards == "o":
        o_spec, o_shape = pl.BlockSpec((None, tm, tn), lambda i, j, k: (j // per, i, j % per)), (4, M, N // 4)
    has_add = add is not None
    has_fin = finish is not None
    n_ci = len(carry.ins) if carry else 0
    n_co = len(carry.outs) if carry else 0
    n_in = 2 + has_add + has_fin
    grid = (M // tm, N // tn, nk)

    def body(*refs):
        a_ref, b_ref = refs[0], refs[1]
        add_ref = refs[2] if has_add else None
        fin_ref = refs[2 + has_add] if has_fin else None

        def store(r):
            if has_add:
                r = r + add_ref[...]
            if has_fin:
                r = finish[0](r, fin_ref[...])
            o_ref[...] = r.astype(o_ref.dtype)

        o_ref = refs[n_in + n_ci]
        acc_ref = refs[n_in + n_ci + 1 + n_co] if nk > 1 else None
        c_ins = refs[n_in:n_in + n_ci]
        c_outs = refs[n_in + n_ci + 1:n_in + n_ci + 1 + n_co]
        c_sems = refs[n_in + n_ci + 1 + n_co + (nk > 1):]
        i, j, k = pl.program_id(0), pl.program_id(1), pl.program_id(2)

        if carry:
            @pl.when((i == 0) & (j == 0) & (k == 0))
            def _():
                carry.start(c_ins, c_outs, c_sems)

        part = lax.dot_general(a_ref[...], b_ref[...], dims, preferred_element_type=F32)
        if nk == 1:
            store(part)
        else:
            @pl.when(k == 0)
            def _():
                acc_ref[...] = part

            @pl.when((k > 0) & (k < nk - 1))
            def _():
                acc_ref[...] += part

            @pl.when(k == nk - 1)
            def _():
                store(acc_ref[...] + part)

        if carry:
            @pl.when((i == grid[0] - 1) & (j == grid[1] - 1) & (k == nk - 1))
            def _():
                carry.finish(c_ins, c_outs, c_sems)

    ins = [a, b] + ([add] if has_add else []) + ([finish[1]] if has_fin else []) + (carry.ins if carry else [])
    in_specs = ([a_spec, b_spec] + ([o_spec] if has_add else [])
                + ([pl.BlockSpec((tm, finish[1].shape[1]), lambda i, j, k: (i, 0))] if has_fin else []) + [ANY] * n_ci)
    res = pl.pallas_call(
        body, name=name, grid=grid,
        in_specs=in_specs, out_specs=[o_spec] + [ANY] * n_co,
        out_shape=[jax.ShapeDtypeStruct(o_shape, out_dtype)] + (carry.outs if carry else []),
        scratch_shapes=([pltpu.VMEM((tm, tn), F32)] if nk > 1 else []) + (carry.sems if carry else []),
        input_output_aliases=carry.io_aliases(n_in, 1) if carry else {},
        compiler_params=_params(("arbitrary",) * 3 if carry else ("parallel", "parallel", "arbitrary")),
    )(*ins)
    return (res[0], res[1:]) if carry else res[0]


def _rows(body, name, n_rows, tm, ins, outs, accs=(), carry=None):
    grid = (n_rows // tm,)

    def halo(arr):
        return 16 if arr.dtype == BF16 else 8

    arrays, in_specs = [], []
    for spec in ins:
        kind, arr = spec[0], spec[1]
        arrays.append(arr)
        if kind == "row":
            _, _, cb, w = spec
            in_specs.append(pl.BlockSpec((tm, w), lambda i, cb=cb: (i, cb)))
        elif kind == "full":
            in_specs.append(pl.BlockSpec(arr.shape, lambda i, nd=arr.ndim: (0,) * nd))
        elif kind == "prev":
            _, _, cb, w = spec
            h = halo(arr)
            in_specs.append(pl.BlockSpec((h, w), lambda i, cb=cb, per=tm // h: (jnp.maximum(i * per - 1, 0), cb)))
        else:
            _, _, cb, w = spec
            h = halo(arr)
            in_specs.append(pl.BlockSpec((h, w), lambda i, cb=cb, per=tm // h, last=n_rows // h - 1:
                                         (jnp.minimum((i + 1) * per, last), cb)))
    out_shape = [jax.ShapeDtypeStruct((n_rows, w), dt) for (w, dt) in outs]
    out_specs = [pl.BlockSpec((tm, w), lambda i: (i, 0)) for (w, _) in outs]
    out_shape += [jax.ShapeDtypeStruct(s, F32) for s in accs]
    out_specs += [pl.BlockSpec(s, lambda i, nd=len(s): (0,) * nd) for s in accs]
    n_in, n_out, n_acc = len(ins), len(outs), len(accs)
    n_ci = len(carry.ins) if carry else 0
    n_co = len(carry.outs) if carry else 0

    def kernel_body(*refs):
        first = n_in + n_ci
        c_ins, c_outs, c_sems = refs[n_in:first], refs[first + n_out + n_acc:first + n_out + n_acc + n_co], refs[first + n_out + n_acc + n_co:]
        if carry:
            @pl.when(pl.program_id(0) == 0)
            def _():
                carry.start(c_ins, c_outs, c_sems)

        body(pl.program_id(0), refs[:n_in], refs[first:first + n_out], refs[first + n_out:first + n_out + n_acc])
        if carry:
            @pl.when(pl.program_id(0) == grid[0] - 1)
            def _():
                carry.finish(c_ins, c_outs, c_sems)

    res = pl.pallas_call(
        kernel_body, name=name, grid=grid, in_specs=in_specs + [ANY] * n_ci, out_specs=out_specs + [ANY] * n_co,
        out_shape=out_shape + (carry.outs if carry else []), scratch_shapes=carry.sems if carry else [],
        input_output_aliases=carry.io_aliases(n_in, n_out + n_acc) if carry else {},
        compiler_params=_params(("arbitrary",)),
    )(*arrays, *(carry.ins if carry else []))
    return (res[:n_out + n_acc], res[n_out + n_acc:]) if carry else res


def _acc_add(i, ref, val):
    @pl.when(i == 0)
    def _():
        ref[...] = val

    @pl.when(i > 0)
    def _():
        ref[...] += val


def _rope(t, tab, sign):
    c, sa, sb = tab[:, 0:128], tab[:, 128:256], tab[:, 256:384]
    rot = pltpu.roll(t, 96, 1) * sa + pltpu.roll(t, 32, 1) * sb
    return t * c + rot if sign > 0 else t * c - rot


def _ln_stats(r):
    mu = jnp.mean(r, axis=-1, keepdims=True)
    d = r - mu
    var = jnp.mean(d * d, axis=-1, keepdims=True)
    rstd = lax.rsqrt(var + LN_EPS)
    return d * rstd, rstd


def _ln_bwd(dxh, xh, rstd):
    m1 = jnp.mean(dxh, axis=-1, keepdims=True)
    m2 = jnp.mean(dxh * xh, axis=-1, keepdims=True)
    return rstd * (dxh - m1 - xh * m2)


def _modulate(x, scale, shift, name, carry=None):
    S, D = x.shape

    def body(i, ins, outs, accs):
        outs[0][...] = (ins[0][...] * (1.0 + ins[1][...]) + ins[2][...]).astype(BF16)

    res = _rows(body, name, S, _pick(S, (256, 128)), [("row", x, 0, D), ("full", scale), ("full", shift)], [(D, BF16)],
                carry=carry)
    return (res[0][0], res[1]) if carry else res[0]


def _rms_fwd(pq, tab, g_q, g_kv):
    S = pq.shape[0]

    def body(i, ins, outs, accs):
        pq_ref, tab_ref, gq_ref, gkv_ref = ins

        def rms(x, g):
            return x * lax.rsqrt(jnp.mean(x * x, axis=-1, keepdims=True) + RMS_EPS) * g

        outs[0][...] = rms(pq_ref[:, 0:Q_LORA], gq_ref[...]).astype(BF16)
        outs[1][...] = rms(pq_ref[:, Q_LORA:Q_LORA + KV_LORA], gkv_ref[...]).astype(BF16)
        outs[2][...] = _rope(pq_ref[:, Q_LORA + KV_LORA:QKV_A], tab_ref[...], 1).astype(BF16)

    return _rows(body, "rms_fwd", S, _pick(S, (256, 128)),
                 [("row", pq, 0, QKV_A), ("row", tab, 0, 384), ("full", g_q), ("full", g_kv)],
                 [(Q_LORA, BF16), (KV_LORA, BF16), (128, BF16)])


def _allowed(q0, k0, bq):
    row = q0 + lax.broadcasted_iota(jnp.int32, (bq, bq), 0)
    col = k0 + lax.broadcasted_iota(jnp.int32, (bq, bq), 1)
    return (col >> CHUNK_SHIFT) <= (row >> CHUNK_SHIFT)


ATTN_BLOCK = 512


HEADS_PER_STEP = 2


def _attn_fwd(q, kv, kr):
    S = q.shape[0]
    bq = min(ATTN_BLOCK, S)
    nq = S // bq
    G = HEADS_PER_STEP

    def body(q_ref, kv_ref, kr_ref, o_ref, lse_ref, kcat):
        qi = pl.program_id(1)

        @pl.when(qi == 0)
        def _():
            for g in range(G):
                kcat[g, :, 0:128] = kv_ref[:, g * 256:g * 256 + 128]
                kcat[g, :, 128:256] = kr_ref[...]

        qs = [q_ref[:, g * QK_PAD:(g + 1) * QK_PAD] for g in range(G)]

        def step(j, carry, masked):
            off = pl.multiple_of(j * bq, bq)
            rows = pl.ds(off, bq)
            mask = _allowed(qi * bq, off, bq) if masked else None
            out = []
            for g in range(G):
                m, l, acc = carry[g]
                s = lax.dot_general(qs[g], kcat[g, rows, :], NT, preferred_element_type=F32) * SCALE2
                if masked:
                    s = jnp.where(mask, s, -1e30)
                m_new = jnp.maximum(m, jnp.max(s, axis=1, keepdims=True))
                a = jnp.exp2(m - m_new)
                p = jnp.exp2(s - m_new)
                l = a * l + jnp.sum(p, axis=1, keepdims=True)
                acc = a * acc + jnp.dot(p.astype(BF16), kv_ref[rows, g * 256 + 128:(g + 1) * 256],
                                        preferred_element_type=F32)
                out.append((m_new, l, acc))
            return tuple(out)

        init = tuple((jnp.full((bq, 1), -1e30, F32), jnp.zeros((bq, 1), F32), jnp.zeros((bq, V_HEAD), F32))
                     for _ in range(G))
        below = lax.fori_loop(0, qi, lambda j, cr: step(j, cr, False), init)
        for g, (m, l, acc) in enumerate(step(qi, below, True)):
            o_ref[:, g * V_HEAD:(g + 1) * V_HEAD] = (acc / l).astype(BF16)
            lse_ref[g] = m + jnp.log2(l)

    return pl.pallas_call(
        body, name="attn_fwd", grid=(N_HEADS // G, nq),
        in_specs=[pl.BlockSpec((bq, G * QK_PAD), lambda h, i: (i, h)),
                  pl.BlockSpec((S, G * 256), lambda h, i: (0, h)),
                  pl.BlockSpec((S, 128), lambda h, i: (0, 0))],
        out_specs=[pl.BlockSpec((bq, G * V_HEAD), lambda h, i: (i, h)),
                   pl.BlockSpec((G, bq, 1), lambda h, i: (h, i, 0))],
        out_shape=[jax.ShapeDtypeStruct((S, N_HEADS * V_HEAD), BF16),
                   jax.ShapeDtypeStruct((N_HEADS, S, 1), F32)],
        scratch_shapes=[pltpu.VMEM((G, S, QK_PAD), BF16)],
        compiler_params=_params(("arbitrary", "arbitrary")),
    )(q, kv, kr)


def _attn_bwd(q, kv, kr, do, o, lse, tab, carry=None):
    S = q.shape[0]
    bq = min(ATTN_BLOCK, S)
    nq = S // bq
    G = HEADS_PER_STEP
    n_ci = len(carry.ins) if carry else 0
    n_co = len(carry.outs) if carry else 0

    def body(*refs):
        q_ref, kv_ref, kr_ref, do_ref, o_ref, lse_ref, tab_ref = refs[:7]
        dq_ref, dkv_ref, dkr_ref = refs[7 + n_ci:10 + n_ci]
        dq_acc, dk_acc, dv_acc, kcat = refs[10 + n_ci + n_co:14 + n_ci + n_co]
        c_ins, c_outs, c_sems = refs[7:7 + n_ci], refs[10 + n_ci:10 + n_ci + n_co], refs[14 + n_ci + n_co:]
        h = pl.program_id(0)
        if carry:
            @pl.when(h == 0)
            def _():
                carry.start(c_ins, c_outs, c_sems)

        dq_acc[...] = jnp.zeros_like(dq_acc)
        dk_acc[...] = jnp.zeros_like(dk_acc)
        dv_acc[...] = jnp.zeros_like(dv_acc)
        for g in range(G):
            kcat[g, :, 0:128] = kv_ref[:, g * 256:g * 256 + 128]
            kcat[g, :, 128:256] = kr_ref[...]

        def pair(i, j, masked):
            rows_i = pl.ds(pl.multiple_of(i * bq, bq), bq)
            rows_j = pl.ds(pl.multiple_of(j * bq, bq), bq)
            mask = _allowed(i * bq, j * bq, bq) if masked else None
            for g in range(G):
                qv, k = q_ref[rows_i, g * QK_PAD:(g + 1) * QK_PAD], kcat[g, rows_j, :]
                dov = do_ref[rows_i, g * V_HEAD:(g + 1) * V_HEAD]
                delta = jnp.sum(dov.astype(F32) * o_ref[rows_i, g * V_HEAD:(g + 1) * V_HEAD].astype(F32),
                                axis=1, keepdims=True)
                s = lax.dot_general(qv, k, NT, preferred_element_type=F32) * SCALE2
                if masked:
                    s = jnp.where(mask, s, -1e30)
                p = jnp.exp2(s - lse_ref[g, rows_i, :])
                dv_acc[g, rows_j, :] += lax.dot_general(p.astype(BF16), dov, TN, preferred_element_type=F32)
                dp = lax.dot_general(dov, kv_ref[rows_j, g * 256 + 128:(g + 1) * 256], NT, preferred_element_type=F32)
                ds = (p * (dp - delta) * ATTN_SCALE).astype(BF16)
                dk_acc[g, rows_j, :] += lax.dot_general(ds, qv, TN, preferred_element_type=F32)
                dq_acc[g, rows_i, :] += jnp.dot(ds, k, preferred_element_type=F32)

        def kv_step(j, _):
            pair(j, j, True)

            def q_step(i, _):
                pair(i, j, False)
                return 0

            lax.fori_loop(j + 1, nq, q_step, 0)
            return 0

        lax.fori_loop(0, nq, kv_step, 0)

        for g in range(G):
            lo = g * QK_PAD
            for r in range(nq):
                rows = slice(r * bq, (r + 1) * bq)
                dq_ref[rows, lo:lo + 128] = dq_acc[g, rows, 0:128].astype(BF16)
                dq_ref[rows, lo + 128:lo + 256] = _rope(dq_acc[g, rows, 128:256], tab_ref[rows, :], -1).astype(BF16)
            dkv_ref[:, lo:lo + 128] = dk_acc[g, :, 0:128].astype(BF16)
            dkv_ref[:, lo + 128:lo + 256] = dv_acc[g].astype(BF16)
        dkr_sum = dk_acc[0, :, 128:256]
        for g in range(1, G):
            dkr_sum = dkr_sum + dk_acc[g, :, 128:256]

        @pl.when(h == 0)
        def _():
            dkr_ref[...] = dkr_sum

        @pl.when(h > 0)
        def _():
            dkr_ref[...] += dkr_sum

        @pl.when(h == N_HEADS // G - 1)
        def _():
            for r in range(nq):
                rows = slice(r * bq, (r + 1) * bq)
                dkr_ref[rows, :] = _rope(dkr_ref[rows, :], tab_ref[rows, :], -1)
            if carry:
                carry.finish(c_ins, c_outs, c_sems)

    W = N_HEADS * QK_PAD
    res = pl.pallas_call(
        body, name="attn_bwd", grid=(N_HEADS // G,),
        in_specs=[pl.BlockSpec((S, G * QK_PAD), lambda h: (0, h)),
                  pl.BlockSpec((S, G * 256), lambda h: (0, h)),
                  pl.BlockSpec((S, 128), lambda h: (0, 0)),
                  pl.BlockSpec((S, G * V_HEAD), lambda h: (0, h)),
                  pl.BlockSpec((S, G * V_HEAD), lambda h: (0, h)),
                  pl.BlockSpec((G, S, 1), lambda h: (h, 0, 0)),
                  pl.BlockSpec((S, 384), lambda h: (0, 0))] + [ANY] * n_ci,
        out_specs=[pl.BlockSpec((S, G * QK_PAD), lambda h: (0, h)),
                   pl.BlockSpec((S, G * QK_PAD), lambda h: (0, h)),
                   pl.BlockSpec((S, 128), lambda h: (0, 0))] + [ANY] * n_co,
        out_shape=[jax.ShapeDtypeStruct((S, W), BF16), jax.ShapeDtypeStruct((S, W), BF16),
                   jax.ShapeDtypeStruct((S, 128), F32)] + (carry.outs if carry else []),
        scratch_shapes=[pltpu.VMEM((G, S, QK_PAD), F32), pltpu.VMEM((G, S, QK_PAD), F32), pltpu.VMEM((G, S, V_HEAD), F32),
                        pltpu.VMEM((G, S, QK_PAD), BF16)]
        + (carry.sems if carry else []),
        input_output_aliases=carry.io_aliases(7, 3) if carry else {},
        compiler_params=_params(("arbitrary",)),
    )(q, kv, kr, do, o, lse, tab, *(carry.ins if carry else []))
    return res[0], res[1], res[2], res[3:]


def _shift_down(cur, prev, i, n):
    tm, h = cur.shape[0], prev.shape[0]
    prev = jnp.where(i == 0, jnp.zeros_like(prev), prev)
    full = jnp.concatenate([prev, cur], axis=0)
    return pltpu.roll(full, n, 0)[h:h + tm, :]


def _shift_up(cur, nxt, i, last, n):
    tm, h = cur.shape[0], nxt.shape[0]
    nxt = jnp.where(i == last, jnp.zeros_like(nxt), nxt)
    full = jnp.concatenate([cur, nxt], axis=0)
    return pltpu.roll(full, tm + h - n, 0)[0:tm, :]


def _conv_fwd(pc, w_conv):
    S, D = pc.shape[0], pc.shape[1] // 3
    tm = _pick(S, (256, 128))

    def body(i, ins, outs, accs):
        b_ref, c_ref, x_ref, cp_ref, xp_ref, w_ref = ins
        z = c_ref[...].astype(F32) * x_ref[...].astype(F32)
        zp = cp_ref[...].astype(F32) * xp_ref[...].astype(F32)
        cz = w_ref[0:1, :] * _shift_down(z, zp, i, 2) + w_ref[1:2, :] * _shift_down(z, zp, i, 1) + w_ref[2:3, :] * z
        outs[0][...] = (b_ref[...].astype(F32) * cz).astype(BF16)

    return _rows(body, "conv_fwd", S, tm,
                 [("row", pc, 0, D), ("row", pc, 1, D), ("row", pc, 2, D), ("prev", pc, 1, D), ("prev", pc, 2, D),
                  ("full", w_conv)], [(D, BF16)])[0]


def _conv_bwd(dhb, pc, w_conv):
    S, D = dhb.shape
    tm = _pick(S, (256, 128))
    last = S // tm - 1

    def body(i, ins, outs, accs):
        g_ref, b_ref, c_ref, x_ref, cp_ref, xp_ref, gn_ref, bn_ref, w_ref = ins
        w0, w1, w2 = w_ref[0:1, :], w_ref[1:2, :], w_ref[2:3, :]
        c, x, g = c_ref[...].astype(F32), x_ref[...].astype(F32), g_ref[...].astype(F32)
        z = c * x
        zp = cp_ref[...].astype(F32) * xp_ref[...].astype(F32)
        z1, z2 = _shift_down(z, zp, i, 1), _shift_down(z, zp, i, 2)
        cz = w0 * z2 + w1 * z1 + w2 * z
        dcz = g * b_ref[...].astype(F32)
        dczn = gn_ref[...].astype(F32) * bn_ref[...].astype(F32)
        dz = w2 * dcz + w1 * _shift_up(dcz, dczn, i, last, 1) + w0 * _shift_up(dcz, dczn, i, last, 2)
        outs[0][:, 0:D] = (g * cz).astype(BF16)
        outs[0][:, D:2 * D] = (dz * x).astype(BF16)
        outs[0][:, 2 * D:3 * D] = (dz * c).astype(BF16)
        dw = jnp.concatenate([jnp.sum(dcz * z2, axis=0, keepdims=True), jnp.sum(dcz * z1, axis=0, keepdims=True),
                              jnp.sum(dcz * z, axis=0, keepdims=True)], axis=0)
        _acc_add(i, accs[0], dw)

    return _rows(body, "conv_bwd", S, tm,
                 [("row", dhb, 0, D), ("row", pc, 0, D), ("row", pc, 1, D), ("row", pc, 2, D),
                  ("prev", pc, 1, D), ("prev", pc, 2, D), ("next", dhb, 0, D), ("next", pc, 0, D), ("full", w_conv)],
                 [(3 * D, BF16)], [(3, D)])


def _merge_fwd(y_a, y_b, pg):
    S, D = y_a.shape

    def body(i, ins, outs, accs):
        ya, yb, ga, gb = ins
        outs[0][...] = (_sigmoid(ga[...].astype(F32)) * ya[...].astype(F32)
                        + _sigmoid(gb[...].astype(F32)) * yb[...].astype(F32)).astype(BF16)

    return _rows(body, "merge_fwd", S, _pick(S, (256, 128)),
                 [("row", y_a, 0, D), ("row", y_b, 0, D), ("row", pg, 0, D), ("row", pg, 1, D)], [(D, BF16)])[0]


def _merge_bwd(dm, y_a, y_b, pg):
    S, D = dm.shape

    def body(i, ins, outs, accs):
        d, ya, yb = ins[0][...].astype(F32), ins[1][...].astype(F32), ins[2][...].astype(F32)
        sa, sb = _sigmoid(ins[3][...].astype(F32)), _sigmoid(ins[4][...].astype(F32))
        outs[0][...] = (d * sa).astype(BF16)
        outs[1][...] = (d * sb).astype(BF16)
        outs[2][:, 0:D] = (d * ya * (sa * (1.0 - sa))).astype(BF16)
        outs[2][:, D:2 * D] = (d * yb * (sb * (1.0 - sb))).astype(BF16)

    return _rows(body, "merge_bwd", S, _pick(S, (256, 128)),
                 [("row", dm, 0, D), ("row", y_a, 0, D), ("row", y_b, 0, D), ("row", pg, 0, D), ("row", pg, 1, D)],
                 [(D, BF16), (D, BF16), (2 * D, BF16)])


def _ln1_fwd(x, mix, gate1, g, b, scale2, shift2, carry=None):
    S, D = x.shape

    def body(i, ins, outs, accs):
        x_ref, mix_ref, gate_ref, g_ref, b_ref, sc_ref, sh_ref = ins
        xh, _ = _ln_stats(ALPHA * x_ref[...] + gate_ref[...] * mix_ref[...])
        x1 = xh * g_ref[...] + b_ref[...]
        outs[0][...] = x1
        outs[1][...] = (x1 * (1.0 + sc_ref[...]) + sh_ref[...]).astype(BF16)

    return _rows(body, "ln1_fwd", S, _pick(S, (256, 128)),
                 [("row", x, 0, D), ("row", mix, 0, D), ("full", gate1), ("full", g), ("full", b),
                  ("full", scale2), ("full", shift2)], [(D, F32), (D, BF16)], carry=carry)


def _swiglu_fwd(hh, carry=None):
    S, F = hh.shape[0], hh.shape[1] // 2

    def body(i, ins, outs, accs):
        hg = ins[0][...].astype(F32)
        outs[0][...] = (hg * _sigmoid(hg) * ins[1][...].astype(F32)).astype(BF16)

    res = _rows(body, "swiglu_fwd", S, _pick(S, (128,)), [("row", hh, 0, F), ("row", hh, 1, F)], [(F, BF16)], carry=carry)
    return (res[0][0], res[1]) if carry else res[0]


def _swiglu_bwd(dact, hh):
    S, F = dact.shape

    def body(i, ins, outs, accs):
        d, hg, hu = ins[0][...].astype(F32), ins[1][...].astype(F32), ins[2][...].astype(F32)
        sg = _sigmoid(hg)
        outs[0][:, 0:F] = (d * hu * (sg * (1.0 + hg * (1.0 - sg)))).astype(BF16)
        outs[0][:, F:2 * F] = (d * (hg * sg)).astype(BF16)

    return _rows(body, "swiglu_bwd", S, _pick(S, (128,)),
                 [("row", dact, 0, F), ("row", hh, 0, F), ("row", hh, 1, F)], [(2 * F, BF16)])[0]


def _ln2_loss_bwd(x1, ffn, gate2, g, b, target):
    S, D = x1.shape

    def body(i, ins, outs, accs):
        x1_ref, f_ref, gate_ref, g_ref, b_ref, t_ref = ins
        f = f_ref[...]
        xh, rstd = _ln_stats(ALPHA * x1_ref[...] + gate_ref[...] * f)
        e = xh * g_ref[...] + b_ref[...] - t_ref[...]
        dy = e * (1.0 / D)
        dr = _ln_bwd(dy * g_ref[...], xh, rstd)
        outs[0][...] = (gate_ref[...] * dr).astype(BF16)
        outs[1][...] = ALPHA * dr
        _acc_add(i, accs[0], jnp.full((1, 128), (0.5 / D) * jnp.sum(e * e), F32))
        _acc_add(i, accs[1], jnp.sum(dy * xh, axis=0, keepdims=True))
        _acc_add(i, accs[2], jnp.sum(dy, axis=0, keepdims=True))
        _acc_add(i, accs[3], jnp.sum(dr * f, axis=0, keepdims=True))

    return _rows(body, "ln2_loss_bwd", S, _pick(S, (256, 128)),
                 [("row", x1, 0, D), ("row", ffn, 0, D), ("full", gate2), ("full", g), ("full", b), ("row", target, 0, D)],
                 [(D, BF16), (D, F32)], [(1, 128), (1, D), (1, D), (1, D)])


def _ln1_bwd(x, mix, dx1a, du2, gate1, g, b, scale2):
    S, D = x.shape

    def body(i, ins, outs, accs):
        x_ref, mix_ref, da_ref, du_ref, gate_ref, g_ref, b_ref, sc_ref = ins
        mix, du = mix_ref[...], du_ref[...]
        xh, rstd = _ln_stats(ALPHA * x_ref[...] + gate_ref[...] * mix)
        x1 = xh * g_ref[...] + b_ref[...]
        dx1 = da_ref[...] + du * (1.0 + sc_ref[...])
        dr = _ln_bwd(dx1 * g_ref[...], xh, rstd)
        outs[0][...] = (gate_ref[...] * dr).astype(BF16)
        outs[1][...] = ALPHA * dr
        _acc_add(i, accs[0], jnp.sum(du, axis=0, keepdims=True))
        _acc_add(i, accs[1], jnp.sum(du * x1, axis=0, keepdims=True))
        _acc_add(i, accs[2], jnp.sum(dx1 * xh, axis=0, keepdims=True))
        _acc_add(i, accs[3], jnp.sum(dx1, axis=0, keepdims=True))
        _acc_add(i, accs[4], jnp.sum(dr * mix, axis=0, keepdims=True))

    return _rows(body, "ln1_bwd", S, _pick(S, (256, 128)),
                 [("row", x, 0, D), ("row", mix, 0, D), ("row", dx1a, 0, D), ("row", du2, 0, D),
                  ("full", gate1), ("full", g), ("full", b), ("full", scale2)],
                 [(D, BF16), (D, F32)], [(1, D)] * 5)


def _rms_bwd(d_rq, d_rkv, pq, dkr, g_q, g_kv):
    S = pq.shape[0]

    def body(i, ins, outs, accs):
        dq_ref, dkv_ref, pq_ref, dkr_ref, gq_ref, gkv_ref = ins

        def rms_bwd(dy, x, g):
            r = lax.rsqrt(jnp.mean(x * x, axis=-1, keepdims=True) + RMS_EPS)
            dyg = dy * g
            dx = r * dyg - x * (r * r * r) * jnp.mean(dyg * x, axis=-1, keepdims=True)
            return dx, jnp.sum(dy * (x * r), axis=0, keepdims=True)

        dxq, dgq = rms_bwd(dq_ref[...], pq_ref[:, 0:Q_LORA], gq_ref[...])
        dxkv, dgkv = rms_bwd(dkv_ref[...], pq_ref[:, Q_LORA:Q_LORA + KV_LORA], gkv_ref[...])
        outs[0][:, 0:Q_LORA] = dxq.astype(BF16)
        outs[0][:, Q_LORA:Q_LORA + KV_LORA] = dxkv.astype(BF16)
        outs[0][:, Q_LORA + KV_LORA:QKV_A] = dkr_ref[...].astype(BF16)
        _acc_add(i, accs[0], dgq)
        _acc_add(i, accs[1], dgkv)

    return _rows(body, "rms_bwd", S, _pick(S, (256, 128)),
                 [("row", d_rq, 0, Q_LORA), ("row", d_rkv, 0, KV_LORA), ("row", pq, 0, QKV_A), ("row", dkr, 0, 128),
                  ("full", g_q), ("full", g_kv)], [(QKV_A, BF16)], [(1, Q_LORA), (1, KV_LORA)])


def _dx_final(dxa, du, x, scale1):
    S, D = x.shape

    def body(i, ins, outs, accs):
        du = ins[1][...]
        outs[0][...] = ins[0][...] + du * (1.0 + ins[3][...])
        _acc_add(i, accs[0], jnp.sum(du, axis=0, keepdims=True))
        _acc_add(i, accs[1], jnp.sum(du * ins[2][...], axis=0, keepdims=True))

    return _rows(body, "dx_final", S, _pick(S, (256, 128)),
                 [("row", dxa, 0, D), ("row", du, 0, D), ("row", x, 0, D), ("full", scale1)],
                 [(D, F32)], [(1, D), (1, D)])


def _ada_fwd(c_all, w, bias):
    B, D = c_all.shape
    NA = w.shape[1]
    tn = _pick(NA, (512, 256, 128))

    def body(c_ref, w_ref, b_ref, o_ref):
        cv = c_ref[...]
        ca = (cv * _sigmoid(cv)).astype(BF16)
        o_ref[...] = jnp.dot(ca, w_ref[...].astype(BF16), preferred_element_type=F32) + b_ref[...]

    return pl.pallas_call(
        body, name="ada_fwd", grid=(NA // tn,),
        in_specs=[pl.BlockSpec((B, D), lambda j: (0, 0)), pl.BlockSpec((D, tn), lambda j: (0, j)),
                  pl.BlockSpec((1, tn), lambda j: (0, j))],
        out_specs=pl.BlockSpec((B, tn), lambda j: (0, j)),
        out_shape=jax.ShapeDtypeStruct((B, NA), F32),
        compiler_params=_params(("arbitrary",)),
    )(c_all, w, bias)


def _ada_bwd(c_all, dmod):
    B, D = c_all.shape
    NA = dmod.shape[1]
    tn = _pick(NA, (512, 256, 128))

    def body(c_ref, d_ref, o_ref):
        cv = c_ref[...]
        ca = (cv * _sigmoid(cv)).astype(BF16)
        o_ref[...] = lax.dot_general(ca, d_ref[...].astype(BF16), TN, preferred_element_type=F32)

    return pl.pallas_call(
        body, name="ada_bwd", grid=(NA // tn,),
        in_specs=[pl.BlockSpec((B, D), lambda j: (0, 0)), pl.BlockSpec((B, tn), lambda j: (0, j))],
        out_specs=pl.BlockSpec((D, tn), lambda j: (0, j)),
        out_shape=jax.ShapeDtypeStruct((D, NA), F32),
        compiler_params=_params(("arbitrary",)),
    )(c_all, dmod)


def _pack_rows(parts, n_rows, after=()):
    N = parts[0].shape[1]
    n = len(parts)

    def body(*refs):
        o_ref = refs[-1]
        o_ref[...] = jnp.zeros_like(o_ref)
        at = 0
        for r in refs[:n]:
            o_ref[at:at + r.shape[0], :] = r[...]
            at += r.shape[0]

    vmem = pl.BlockSpec(memory_space=pltpu.VMEM)
    return pl.pallas_call(body, name="pack_small", out_shape=jax.ShapeDtypeStruct((n_rows, N), F32),
                          in_specs=[vmem] * n + [ANY] * len(after), out_specs=vmem,
                          compiler_params=_params())(*parts, *after)


def _sum8(parts):
    _, R, N = parts.shape

    def body(p_ref, o_ref):
        acc = p_ref[0]
        for d in range(1, 8):
            acc = acc + p_ref[d]
        o_ref[...] = acc

    return pl.pallas_call(body, name="sum8", out_shape=jax.ShapeDtypeStruct((R, N), F32),
                          compiler_params=_params())(parts)


def _adam_math(w, g, m, v):
    m = ADAM_B1 * m + (1.0 - ADAM_B1) * g
    v = ADAM_B2 * v + (1.0 - ADAM_B2) * (g * g)
    delta = -ADAM_LR * ((m / ADAM_C1) / (jnp.sqrt(v / ADAM_C2) + ADAM_EPS) + ADAM_WD * w)
    return delta, m, v


def _adam(name, w, m, v, g, carry=None):
    R, C = w.shape
    tm = _row_tile(R, C * 4, 1 << 20)
    steps = R // tm
    n_ci = len(carry.ins) if carry else 0
    n_co = len(carry.outs) if carry else 0

    def body(*refs):
        w_ref, m_ref, v_ref, g_ref = refs[:4]
        d_ref, nm_ref, nv_ref = refs[4 + n_ci:7 + n_ci]
        c_ins, c_outs, c_sems = refs[4:4 + n_ci], refs[7 + n_ci:7 + n_ci + n_co], refs[7 + n_ci + n_co:]
        if carry:
            @pl.when(pl.program_id(0) == 0)
            def _():
                carry.start(c_ins, c_outs, c_sems)

        delta, nm, nv = _adam_math(w_ref[...], g_ref[...], m_ref[...], v_ref[...])
        d_ref[...] = delta
        nm_ref[...] = nm
        nv_ref[...] = nv
        if carry:
            @pl.when(pl.program_id(0) == steps - 1)
            def _():
                carry.finish(c_ins, c_outs, c_sems)

    spec = pl.BlockSpec((tm, C), lambda i: (i, 0))
    res = pl.pallas_call(
        body, name=name, grid=(steps,), in_specs=[spec] * 4 + [ANY] * n_ci, out_specs=[spec] * 3 + [ANY] * n_co,
        out_shape=[jax.ShapeDtypeStruct((R, C), F32)] * 3 + (carry.outs if carry else []),
        scratch_shapes=carry.sems if carry else [],
        input_output_aliases=carry.io_aliases(4, 3) if carry else {},
        compiler_params=_params(("arbitrary",)),
    )(w, m, v, g, *(carry.ins if carry else []))
    return (res[:3], res[3:]) if carry else res


def _adam_halves(name, w, m, v, mine, other, core, carry=None):
    R, C = w.shape
    Rh = mine.shape[0]
    tc = max(t for t in range(128, C + 1, 128) if C % t == 0 and R * t <= (3 << 17))
    steps = C // tc
    n_ci = len(carry.ins) if carry else 0
    n_co = len(carry.outs) if carry else 0

    def body(*refs):
        c_ref, w_ref, m_ref, v_ref, a_ref, b_ref = refs[:6]
        g_ref, d_ref, nm_ref, nv_ref = refs[6 + n_ci:10 + n_ci]
        c_ins, c_outs, c_sems = refs[6:6 + n_ci], refs[10 + n_ci:10 + n_ci + n_co], refs[10 + n_ci + n_co:]
        if carry:
            @pl.when(pl.program_id(0) == 0)
            def _():
                carry.start(c_ins, c_outs, c_sems)

        first = c_ref[0] == 0
        g = jnp.concatenate([jnp.where(first, a_ref[...], b_ref[...]),
                             jnp.where(first, b_ref[0:R - Rh, :], a_ref[0:R - Rh, :])], axis=0)
        delta, nm, nv = _adam_math(w_ref[...], g, m_ref[...], v_ref[...])
        g_ref[...] = g
        d_ref[...] = delta
        nm_ref[...] = nm
        nv_ref[...] = nv
        if carry:
            @pl.when(pl.program_id(0) == steps - 1)
            def _():
                carry.finish(c_ins, c_outs, c_sems)

    spec = pl.BlockSpec((R, tc), lambda i, c_ref: (0, i))
    h_spec = pl.BlockSpec((Rh, tc), lambda i, c_ref: (0, i))
    res = pl.pallas_call(
        body, name=name, out_shape=[jax.ShapeDtypeStruct((R, C), F32)] * 4 + (carry.outs if carry else []),
        grid_spec=pltpu.PrefetchScalarGridSpec(
            num_scalar_prefetch=1, grid=(steps,), in_specs=[spec, spec, spec, h_spec, h_spec] + [ANY] * n_ci,
            out_specs=[spec] * 4 + [ANY] * n_co, scratch_shapes=carry.sems if carry else []),
        input_output_aliases=carry.io_aliases(6, 4) if carry else {},
        compiler_params=_params(("arbitrary",)),
    )(core, w, m, v, mine, other, *(carry.ins if carry else []))
    return (res[:4], res[4:]) if carry else res


def _adam_small(name, w, m, v, g):
    def body(w_ref, m_ref, v_ref, g_ref, d_ref, nm_ref, nv_ref):
        delta, nm, nv = _adam_math(w_ref[...], g_ref[...], m_ref[...], v_ref[...])
        d_ref[...] = delta
        nm_ref[...] = nm
        nv_ref[...] = nv

    return pl.pallas_call(body, name=name, out_shape=[jax.ShapeDtypeStruct(w.shape, F32)] * 3,
                          compiler_params=_params())(w, m, v, g)


def _place():
    return lax.axis_index("x"), lax.axis_index("y"), lax.axis_index("c")


def _other_chips(x, y):
    return [(1 - x, y), (x, 1 - y), (1 - x, 1 - y)]


def _all_gather8(blk, name):
    R, N = blk.shape

    def body(x_ref, out_ref, send_sems, recv_sems, local_sem):
        x, y, c = _place()
        me = 4 * x + 2 * y + c
        mine = pltpu.make_async_copy(x_ref, out_ref.at[me], local_sem)
        mine.start()
        flips = [(j >> 2 & 1, j >> 1 & 1, j & 1) for j in range(1, 8)]
        peers = [((1 - x) if fx else x, (1 - y) if fy else y, (1 - c) if fc else c) for fx, fy, fc in flips]
        sends = []
        for j, peer in enumerate(peers):
            cp = pltpu.make_async_remote_copy(src_ref=x_ref, dst_ref=out_ref.at[me], send_sem=send_sems.at[j],
                                              recv_sem=recv_sems.at[j], device_id=peer, device_id_type=MESH)
            cp.start()
            sends.append(cp)
        for j, (px, py, pc) in enumerate(peers):
            pltpu.make_async_remote_copy(src_ref=x_ref, dst_ref=out_ref.at[4 * px + 2 * py + pc],
                                         send_sem=send_sems.at[j], recv_sem=recv_sems.at[j],
                                         device_id=(px, py, pc), device_id_type=MESH).wait_recv()
        for cp in sends:
            cp.wait_send()
        mine.wait()

    return pl.pallas_call(
        body, name=name, out_shape=jax.ShapeDtypeStruct((8, R, N), F32),
        in_specs=[pl.BlockSpec(memory_space=pltpu.VMEM)], out_specs=pl.BlockSpec(memory_space=pltpu.VMEM),
        scratch_shapes=[pltpu.SemaphoreType.DMA((7,)), pltpu.SemaphoreType.DMA((7,)), pltpu.SemaphoreType.DMA],
        compiler_params=_params(),
    )(blk)


def _piece(rows, piece):
    i, n, k = piece if len(piece) == 3 else (piece[0], piece[1], 1)
    assert rows % 16 == 0 and rows // 16 >= n, (rows, piece)
    lo, hi = (rows // 16 * i // n) * 16, (rows // 16 * (i + k) // n) * 16
    return pl.ds(lo, hi - lo)


def _gather_plan(shards, piece=(0, 1), into=None, ici=True):
    n = len(shards)

    def parts(ins, outs, sems):
        s1, r1, s2, r2, loc = sems
        x, y, c = _place()
        me = 2 * x + y
        chips = _other_chips(x, y)
        sib = (x, y, 1 - c)

        def rows(k):
            return _piece(shards[k].shape[1], piece)

        def ici_copy(k, j, slab, to):
            return pltpu.make_async_remote_copy(src_ref=ins[k].at[c, rows(k)], dst_ref=outs[k].at[slab, c, rows(k)],
                                                send_sem=s1.at[3 * k + j], recv_sem=r1.at[3 * k + j],
                                                device_id=to, device_id_type=MESH)

        def d2d(k, j, slab, half):
            return pltpu.make_async_remote_copy(src_ref=outs[k].at[slab, half, rows(k)],
                                                dst_ref=outs[k].at[slab, half, rows(k)],
                                                send_sem=s2.at[3 * k + j], recv_sem=r2.at[3 * k + j],
                                                device_id=sib, device_id_type=MESH)

        def own(k):
            return pltpu.make_async_remote_copy(src_ref=ins[k].at[:, rows(k)], dst_ref=outs[k].at[me, :, rows(k)],
                                                send_sem=loc.at[2 * k], recv_sem=loc.at[2 * k + 1],
                                                device_id=sib, device_id_type=MESH)

        return c, me, chips, ici_copy, d2d, own

    def start(ins, outs, sems):
        c, me, chips, ici_copy, d2d, own = parts(ins, outs, sems)
        for k in range(n):
            for j, (px, py) in enumerate(chips):
                (ici_copy(k, j, me, (px, py, c)) if ici else d2d(k, j, 2 * px + py, c)).start()
        for k in range(n):
            own(k).start()

    def finish(ins, outs, sems):
        c, me, chips, ici_copy, d2d, own = parts(ins, outs, sems)
        if ici:
            for k in range(n):
                for j, (px, py) in enumerate(chips):
                    ici_copy(k, j, 2 * px + py, (px, py, c)).wait_recv()
                    d2d(k, j, 2 * px + py, c).start()
        for k in range(n):
            for j, (px, py) in enumerate(chips):
                d2d(k, j, 2 * px + py, 1 - c).wait_recv()
        for k in range(n):
            own(k).wait()
            for j, (px, py) in enumerate(chips):
                if ici:
                    ici_copy(k, j, me, (px, py, c)).wait_send()
                d2d(k, j, 2 * px + py, c).wait_send()

    return _Plan(list(shards) + list(into or []), [jax.ShapeDtypeStruct((4,) + a.shape, a.dtype) for a in shards],
                 [pltpu.SemaphoreType.DMA((3 * n,))] * 4 + [pltpu.SemaphoreType.DMA((2 * n,))], start, finish,
                 aliases={n + k: k for k in range(n)} if into else None)


def _pair_plan(parts):
    n = len(parts)

    def copies(ins, outs, sems):
        send_sems, recv_sems = sems
        x, y, c = _place()
        return [pltpu.make_async_remote_copy(src_ref=ins[k].at[p, 1 - c], dst_ref=outs[k].at[p],
                                             send_sem=send_sems.at[4 * k + p], recv_sem=recv_sems.at[4 * k + p],
                                             device_id=(x, y, 1 - c), device_id_type=MESH)
                for k in range(n) for p in range(4)]

    def start(ins, outs, sems):
        for cp in copies(ins, outs, sems):
            cp.start()

    def finish(ins, outs, sems):
        for cp in copies(ins, outs, sems):
            cp.wait()

    return _Plan(parts, [jax.ShapeDtypeStruct((4,) + a.shape[2:], a.dtype) for a in parts],
                 [pltpu.SemaphoreType.DMA((4 * n,))] * 2, start, finish)


def _sibling_plan(arrs):
    n = len(arrs)

    def copies(ins, outs, sems):
        send_sems, recv_sems = sems
        x, y, c = _place()
        return [pltpu.make_async_remote_copy(src_ref=ins[k], dst_ref=outs[k], send_sem=send_sems.at[k],
                                             recv_sem=recv_sems.at[k], device_id=(x, y, 1 - c), device_id_type=MESH)
                for k in range(n)]

    def start(ins, outs, sems):
        for cp in copies(ins, outs, sems):
            cp.start()

    def finish(ins, outs, sems):
        for cp in copies(ins, outs, sems):
            cp.wait()

    return _Plan(arrs, [jax.ShapeDtypeStruct(a.shape, a.dtype) for a in arrs],
                 [pltpu.SemaphoreType.DMA((n,))] * 2, start, finish)


def _scatter_copies(arrs):
    def copies(ins, land, send_sems, recv_sems):
        x, y, c = _place()
        return [pltpu.make_async_remote_copy(src_ref=ins[k].at[2 * px + py], dst_ref=land[k].at[j],
                                             send_sem=send_sems.at[3 * k + j], recv_sem=recv_sems.at[3 * k + j],
                                             device_id=(px, py, c), device_id_type=MESH)
                for k in range(len(arrs)) for j, (px, py) in enumerate(_other_chips(x, y))]

    return copies, [lax.empty((3,) + a.shape[1:], a.dtype) for a in arrs]


def _gather_copies(shards, piece=(0, 1), lands=None):
    def copies(ins, land, send_sems, recv_sems):
        x, y, c = _place()
        return [pltpu.make_async_remote_copy(
                    src_ref=ins[k].at[c, _piece(shards[k].shape[1], piece)],
                    dst_ref=land[k].at[2 * x + y, c, _piece(shards[k].shape[1], piece)],
                    send_sem=send_sems.at[3 * k + j], recv_sem=recv_sems.at[3 * k + j],
                    device_id=(px, py, c), device_id_type=MESH)
                for k in range(len(shards)) for j, (px, py) in enumerate(_other_chips(x, y))]

    return copies, list(lands) if lands is not None else [lax.empty((4,) + a.shape, a.dtype) for a in shards]


def _split_start(arrs, copies_lands, ride, name, after=()):
    copies, lands = copies_lands
    n = len(arrs)
    rides = list(ride) if isinstance(ride, (list, tuple)) else [ride]
    n_thru = 2 * n + len(rides)

    def body(*refs):
        first_out = n_thru + len(after)
        for cp in copies(refs[:n], refs[n:2 * n], refs[first_out], refs[first_out + 1]):
            cp.start()

    hbm = [pltpu.with_memory_space_constraint(a, pltpu.HBM) for a in list(arrs) + lands + rides]
    res = pl.pallas_call(
        body, name=name,
        out_shape=[pltpu.SemaphoreType.DMA((3 * n,)), pltpu.SemaphoreType.DMA((3 * n,))]
        + [pltpu.HBM(a.shape, a.dtype) for a in hbm],
        in_specs=[HBM_SPEC] * n_thru + [ANY] * len(after),
        out_specs=[SEM_SPEC, SEM_SPEC] + [HBM_SPEC] * n_thru,
        input_output_aliases={i: 2 + i for i in range(n_thru)},
        compiler_params=pltpu.CompilerParams(has_side_effects=pltpu.SideEffectType.DATAFLOW_SIDE_EFFECTING),
    )(*hbm, *after)
    return res[0], res[1], res[2:2 + n], res[2 + n:2 + 2 * n], list(res[2 + 2 * n:])


def _split_wait(started, copies_lands, after, name):
    send_sems, recv_sems, arrs, lands, _ = started
    copies = copies_lands[0]
    n = len(arrs)

    def body(*refs):
        for cp in copies(refs[:n], refs[n:2 * n], refs[2 * n], refs[2 * n + 1]):
            cp.wait_send()
            cp.wait_recv()

    res = pl.pallas_call(
        body, name=name, out_shape=[pltpu.HBM(a.shape, a.dtype) for a in list(arrs) + list(lands)],
        in_specs=[HBM_SPEC] * (2 * n) + [SEM_SPEC, SEM_SPEC] + [ANY] * len(after), out_specs=[HBM_SPEC] * (2 * n),
        input_output_aliases={i: i for i in range(2 * n)},
        compiler_params=pltpu.CompilerParams(has_side_effects=pltpu.SideEffectType.DATAFLOW_SIDE_EFFECTING),
    )(*arrs, *lands, send_sems, recv_sems, *after)
    return list(res[:n]), list(res[n:])


def _add_pair(parts, sib, core, name):
    P4, _, Rh, C = parts.shape
    tm, tc = _tile2(Rh, C, 16)

    def body(c_ref, a_ref, b_ref, o_ref):
        o_ref[...] = (a_ref[0].astype(F32) + b_ref[...].astype(F32)).astype(BF16)

    spec = pl.BlockSpec((1, tm, tc), lambda p, i, j, c_ref: (p, i, j))
    return pl.pallas_call(
        body, name=name, out_shape=jax.ShapeDtypeStruct((P4, Rh, C), BF16),
        grid_spec=pltpu.PrefetchScalarGridSpec(
            num_scalar_prefetch=1, grid=(P4, Rh // tm, C // tc),
            in_specs=[pl.BlockSpec((1, 1, tm, tc), lambda p, i, j, c_ref: (p, c_ref[0], i, j)), spec], out_specs=spec),
        compiler_params=_params(("parallel",) * 3),
    )(core, parts, sib)


def _sum_slabs(pre, recv, chip, name):
    _, Rh, C = pre.shape
    tm, tc = _tile2(Rh, C, 16)

    def body(me_ref, own_ref, r_ref, o_ref):
        acc = own_ref[0].astype(F32)
        for j in range(3):
            acc = acc + r_ref[j].astype(F32)
        o_ref[...] = acc

    return pl.pallas_call(
        body, name=name, out_shape=jax.ShapeDtypeStruct((Rh, C), F32),
        grid_spec=pltpu.PrefetchScalarGridSpec(
            num_scalar_prefetch=1, grid=(Rh // tm, C // tc),
            in_specs=[pl.BlockSpec((1, tm, tc), lambda i, j, me_ref: (me_ref[0], i, j)),
                      pl.BlockSpec((3, tm, tc), lambda i, j, me_ref: (0, i, j))],
            out_specs=pl.BlockSpec((tm, tc), lambda i, j, me_ref: (i, j))),
        compiler_params=_params(("parallel", "parallel")),
    )(chip, pre, recv)


def kernel(x, c, positions, w_ada, b_ada, w_in, g_q_a, w_q_b, g_kv_a, w_kv_b, w_o_a, w_conv, w_o_b, w_o, ln1_g, ln1_b, w_ffn_in, w_ffn_out, ln2_g, ln2_b, loss_target, m_w_ada, m_b_ada, m_w_in, m_g_q_a, m_w_q_b, m_g_kv_a, m_w_kv_b, m_w_o_a, m_w_conv, m_w_o_b, m_w_o, m_ln1_g, m_ln1_b, m_w_ffn_in, m_w_ffn_out, m_ln2_g, m_ln2_b, v_w_ada, v_b_ada, v_w_in, v_g_q_a, v_w_q_b, v_g_kv_a, v_w_kv_b, v_w_o_a, v_w_conv, v_w_o_b, v_w_o, v_ln1_g, v_ln1_b, v_w_ffn_in, v_w_ffn_out, v_ln2_g, v_ln2_b):
    S, D = x.shape[1], x.shape[2]
    F = w_ffn_out.shape[1] * 4
    ax, ay, ac = _place()
    chip = 2 * ax + ay
    dev = 4 * ax + 2 * ay + ac
    x2, tgt = x[0], loss_target[0]
    w_ada2, w_in2, w_q_b2, w_kv_b2 = w_ada[0], w_in[0], w_q_b[0], w_kv_b[0]
    w_o_a2, w_o_b2, w_o2, w_ffn_in2, w_ffn_out2 = w_o_a[0], w_o_b[0], w_o[0], w_ffn_in[0], w_ffn_out[0]
    NA = w_ada2.shape[1]
    CW = w_conv.shape[2]

    inv_freq = 1.0 / (ROPE_THETA ** (jnp.arange(0, QK_ROPE, 2, dtype=F32) / QK_ROPE))
    ang = positions[0].astype(F32)[:, None] * inv_freq
    cos, sin = jnp.cos(ang), jnp.sin(ang)
    z32, z64, z96 = jnp.zeros((S, 32), F32), jnp.zeros((S, 64), F32), jnp.zeros((S, 96), F32)
    tab = jnp.concatenate([cos, cos, z64, -sin, z96, z32, sin, z64], axis=1)

    def halves(a):
        return a.reshape(2, a.shape[0] // 2, a.shape[1])

    def whole(g):
        return g.reshape(4, 2 * g.shape[2], g.shape[3])

    def cols(g):
        return jnp.transpose(g, (1, 0, 2)).reshape(g.shape[1], 4 * g.shape[2])

    w_inT, m_w_inT, v_w_inT = w_in2.T, m_w_in[0].T, v_w_in[0].T
    CS = w_inT.shape[0]
    CSP = -(-CS // 32) * 32
    sh_in = halves(jnp.pad(w_inT.astype(BF16), ((0, CSP - CS), (0, 0))))
    c_all = _all_gather8(c, "gather_c").reshape(8, D)
    wconv_all = _all_gather8(w_conv[0], "gather_wconv")
    w_conv_full = jnp.transpose(wconv_all[0::2], (1, 0, 2)).reshape(3, D)
    b_sh = lax.dynamic_slice(b_ada, (0, chip * NA), (1, NA))
    mod_sh = _ada_fwd(c_all, w_ada2, b_sh)
    mod_all = _all_gather8(mod_sh, "gather_mod")
    mod = lax.dynamic_slice(mod_all[0::2], (0, dev, 0), (4, 1, NA)).reshape(6, D)
    shift1, scale1, gate1, shift2, scale2, gate2 = (mod[k:k + 1] for k in range(6))

    n_pc = 8
    st_in, cl_in, sh_in_t, l_in, rides = [], [], [sh_in], None, [shift1, w_conv_full]
    for i in range(n_pc):
        cl_in.append(_gather_copies(sh_in_t, (i, n_pc), l_in))
        st_in.append(_split_start(sh_in_t, cl_in[i], rides, "gather_in%d_start" % i))
        sh_in_t, l_in, rides = st_in[i][2], st_in[i][3], st_in[i][4]
    shift1, w_conv_full = rides
    others = lax.optimization_barrier((w_q_b2, w_kv_b2, w_o_a2, w_o_b2, w_o2, w_ffn_in2, w_ffn_out2, shift1))
    sh_qb, sh_kvb, sh_oa, sh_ob, sh_o, sh_fi, sh_fo = (halves(w.astype(BF16)) for w in others[:7])
    shift1 = others[7]
    for i in range(n_pc):
        casts = [sh_qb, sh_kvb, sh_oa, sh_ob, sh_o, sh_fi, sh_fo] if i == 0 else []
        sh_in_t, l_in = _split_wait(st_in[i][:2] + (sh_in_t, l_in, None), cl_in[i], casts, "gather_in%d_wait" % i)
        if i < n_pc - 1:
            l_in = _run_plan(_gather_plan(sh_in_t, (i, n_pc), into=l_in, ici=False), "handon_in%d" % i)
    sh_a1, sh_a2 = [sh_qb, sh_kvb], [sh_oa, sh_ob, sh_o]
    cl_a1, cl_a2, cl_fi, cl_fo = (_gather_copies(g) for g in (sh_a1, sh_a2, [sh_fi], [sh_fo]))
    st_a1 = _split_start(sh_a1, cl_a1, shift1, "gather_a1_start", after=[l_in[0]])
    st_a2 = _split_start(sh_a2, cl_a2, st_a1[4], "gather_a2_start")
    u, (g_in,) = _modulate(x2, scale1, st_a2[4][0], "modulate1",
                           carry=_gather_plan(sh_in_t, (n_pc - 1, n_pc), into=l_in, ici=False))
    g_in = whole(g_in)

    def in_rows(lo, hi):
        parts = [g_in[p, max(lo, p * CS) - p * CS:min(hi, (p + 1) * CS) - p * CS]
                 for p in range(4) if max(lo, p * CS) < min(hi, (p + 1) * CS)]
        return parts[0] if len(parts) == 1 else jnp.concatenate(parts, axis=0)

    n_qkv = Q_LORA + KV_LORA + QK_ROPE
    W_qkvT = jnp.pad(in_rows(0, n_qkv), ((0, QKV_A - n_qkv), (0, 0)))
    W_convT = in_rows(n_qkv, n_qkv + 3 * D)
    W_gateT = in_rows(n_qkv + 3 * D, n_qkv + 5 * D)

    pq = _matmul(u, W_qkvT, "nt", F32, "proj_qkv")
    pc = _matmul(u, W_convT, "nt", BF16, "proj_conv")
    sh_a1, la1 = _split_wait(st_a1, cl_a1, [pc], "gather_a1_wait")
    pg, (g_qb, g_kvb) = _matmul(u, W_gateT, "nt", BF16, "proj_gate", carry=_gather_plan(sh_a1, into=la1, ici=False))
    st_fi = _split_start([sh_fi], cl_fi, g_q_a, "gather_fi_start", after=[pg])
    W_qb = jnp.pad(cols(whole(g_qb)).reshape(Q_LORA, N_HEADS, QK_NOPE + QK_ROPE),
                   ((0, 0), (0, 0), (0, QK_PAD - QK_NOPE - QK_ROPE))).reshape(Q_LORA, N_HEADS * QK_PAD)
    W_kvb = cols(whole(g_kvb))
    rq, rkv, kr = _rms_fwd(pq, tab, st_fi[4][0], g_kv_a)
    kv = _matmul(rkv, W_kvb, "nn", BF16, "kv_b")
    sh_a2, la2 = _split_wait(st_a2, cl_a2, [kv], "gather_a2_wait")
    def rope_heads(r, t):
        return jnp.concatenate([r[:, lo:lo + 128] if lo % QK_PAD == 0 else _rope(r[:, lo:lo + 128], t, 1)
                                for lo in range(0, r.shape[1], 128)], axis=1)

    q, (g_oa, g_ob, g_o) = _matmul(rq, W_qb, "nn", BF16, "q_b", carry=_gather_plan(sh_a2, into=la2, ici=False),
                                   finish=(rope_heads, tab))
    o, lse = _attn_fwd(q, kv, kr)
    W_oa, W_ob, W_o = (g.reshape(-1, D) for g in (g_oa, g_ob, g_o))
    hb = _conv_fwd(pc, w_conv_full)
    st_fo = _split_start([sh_fo], cl_fo, ln1_g, "gather_fo_start", after=[o])
    y_b = _matmul(hb, W_ob, "nn", BF16, "o_b")
    y_a = _matmul(o, W_oa, "nn", BF16, "o_a")
    merged = _merge_fwd(y_a, y_b, pg)
    sh_fi_t, lfi = _split_wait(st_fi, cl_fi, [merged], "gather_fi_wait")
    mix, g_fi = _matmul(merged, W_o, "nn", F32, "w_o", carry=_gather_plan(sh_fi_t, (0, 2), into=lfi, ici=False))
    (x1, u2), (g_fi,) = _ln1_fwd(x2, mix, gate1, st_fo[4][0], ln1_b, scale2, shift2,
                                 carry=_gather_plan(sh_fi_t, (1, 2), into=g_fi, ici=False))
    W_fi = whole(g_fi)
    hh = _matmul(u2, W_fi, "nn", BF16, "ffn_in", shards="b")
    sh_fo_t, lfo = _split_wait(st_fo, cl_fo, [hh], "gather_fo_wait")
    act, (g_fo,) = _swiglu_fwd(hh, carry=_gather_plan(sh_fo_t, into=lfo, ici=False))
    W_fo = g_fo.reshape(F, D)
    ffn = _matmul(act, W_fo, "nn", F32, "ffn_out")

    core_i = ac.astype(jnp.int32).reshape(1)
    chip_i = chip.astype(jnp.int32).reshape(1)

    def uncols(g):
        return jnp.transpose(g.reshape(g.shape[0], 4, g.shape[1] // 4), (1, 0, 2))

    def slabs(p):
        return p.reshape(4, 2, p.shape[1] // 2, p.shape[2])

    def add_pairs(parts, sibs, nms):
        return [_add_pair(a, b, core_i, "add_pair_" + nm) for a, b, nm in zip(parts, sibs, nms)]

    def sum_all(pre, recv, nms):
        return [_sum_slabs(a, r, chip_i, "sum_slabs_" + nm) for a, r, nm in zip(pre, recv, nms)]

    dffn, dx1a, loss_acc, d_ln2_g, d_ln2_b, d_gate2 = _ln2_loss_bwd(x1, ffn, gate2, ln2_g, ln2_b, tgt)
    dW_fo = _matmul(act, dffn, "tn", BF16, "d_w_ffn_out")
    p_fo = [slabs(dW_fo.reshape(4, -1, D))]
    dact, s_fo = _matmul(dffn, W_fo, "nt", BF16, "d_act", carry=_pair_plan(p_fo))
    pre_fo = add_pairs(p_fo, s_fo, ["w_ffn_out"])
    cs_fo = _scatter_copies(pre_fo)
    st_sfo = _split_start(pre_fo, cs_fo, scale2, "scatter_fo_start")
    dhh = _swiglu_bwd(dact, hh)
    dW_fi = _matmul(u2, dhh, "tn", BF16, "d_w_ffn_in", shards="o")
    p_fi = [slabs(dW_fi)]
    du2, s_fi = _matmul(dhh, W_fi, "nt", F32, "d_u2", carry=_pair_plan(p_fi), shards="b")
    pre_fi = add_pairs(p_fi, s_fi, ["w_ffn_in"])
    cs_fi = _scatter_copies(pre_fi)
    st_sfi = _split_start(pre_fi, cs_fi, st_sfo[4], "scatter_fi_start")
    dmix, dxa, d_shift2, d_scale2, d_ln1_g, d_ln1_b, d_gate1 = _ln1_bwd(x2, mix, dx1a, du2, gate1, ln1_g, ln1_b, st_sfi[4][0])
    dW_o = _matmul(merged, dmix, "tn", BF16, "d_w_o")
    dmerged = _matmul(dmix, W_o, "nt", BF16, "d_merged")
    dy_a, dy_b, dgate = _merge_bwd(dmerged, y_a, y_b, pg)
    dW_oa = _matmul(o, dy_a, "tn", BF16, "d_w_o_a")
    do = _matmul(dy_a, W_oa, "nt", BF16, "d_o")
    dW_ob = _matmul(hb, dy_b, "tn", BF16, "d_w_o_b")
    p_mid = [slabs(g.reshape(4, -1, D)) for g in (dW_oa, dW_ob, dW_o)]
    dhb, s_mid = _matmul(dy_b, W_ob, "nt", BF16, "d_hb", carry=_pair_plan(p_mid))
    pre_mid = add_pairs(p_mid, s_mid, ["w_o_a", "w_o_b", "w_o"])
    cs_mid = _scatter_copies(pre_mid)
    st_smid = _split_start(pre_mid, cs_mid, w_conv_full, "scatter_mid_start")
    dconv, d_wconv = _conv_bwd(dhb, pc, st_smid[4][0])
    dq, dkv, dkr, _ = _attn_bwd(q, kv, kr, do, o, lse, tab, carry=_token_plan(st_smid[4][0]))
    names_a = ["w_ffn_out", "w_ffn_in", "w_o_a", "w_o_b", "w_o"]
    dW_qb = _matmul(rq, dq, "tn", BF16, "d_w_q_b")
    d_rq = _matmul(dq, W_qb, "nt", F32, "d_rq")
    dW_kvb = _matmul(rkv, dkv, "tn", BF16, "d_w_kv_b")
    d_rkv = _matmul(dkv, W_kvb, "nt", F32, "d_rkv")
    dqkv, d_g_q, d_g_kv = _rms_bwd(d_rq, d_rkv, pq, dkr, g_q_a, g_kv_a)
    dW_qkvT = _matmul(dqkv, u, "tn", BF16, "d_w_qkv")
    dW_convT = _matmul(dconv, u, "tn", BF16, "d_w_conv")
    dW_gateT = _matmul(dgate, u, "tn", BF16, "d_w_gate")
    pre_fo, r_fo = _split_wait(st_sfo, cs_fo, [dW_qkvT], "scatter_fo_wait")
    pre_fi, r_fi = _split_wait(st_sfi, cs_fi, [dW_qkvT], "scatter_fi_wait")
    pre_mid, r_mid = _split_wait(st_smid, cs_mid, [dW_qkvT], "scatter_mid_wait")
    fin_a = sum_all(pre_fo + pre_fi + pre_mid, r_fo + r_fi + r_mid, names_a)
    srcs = [(0, dW_qkvT[:n_qkv]), (n_qkv, dW_convT), (n_qkv + 3 * D, dW_gateT)]
    rows_of = []
    for p in range(4):
        for lo, src in srcs:
            a, b = max(lo, p * CS), min(lo + src.shape[0], (p + 1) * CS)
            if a < b:
                rows_of.append(src[a - lo:b - lo])
        rows_of.append(jnp.zeros((CSP - CS, D), BF16))
    dW_inT = jnp.concatenate(rows_of, axis=0).reshape(4, CSP, D)
    dW_qb_u = dW_qb.reshape(Q_LORA, N_HEADS, QK_PAD)[:, :, :QK_NOPE + QK_ROPE].reshape(Q_LORA, -1)
    names_b = ["w_in", "w_q_b", "w_kv_b"]
    p_b = [slabs(dW_inT), slabs(uncols(dW_qb_u)), slabs(uncols(dW_kvb))]
    du, s_b = _matmul(dqkv, W_qkvT, "nn", F32, "d_u_qkv", carry=_pair_plan(p_b))
    pre_b = add_pairs(p_b, s_b, names_b)
    cs_b = _scatter_copies(pre_b)
    st_b = _split_start(pre_b, cs_b, scale1, "scatter_last_start")
    du, fs_a = _matmul(dconv, W_convT, "nn", F32, "d_u_conv", add=du, carry=_sibling_plan(fin_a))
    du = _matmul(dgate, W_gateT, "nn", F32, "d_u_gate", add=du)
    grad_x, d_shift1, d_scale1 = _dx_final(dxa, du, x2, st_b[4][0])

    big = {}
    ws = dict(w_in=(w_inT, m_w_inT, v_w_inT), w_q_b=(w_q_b2, m_w_q_b[0], v_w_q_b[0]),
              w_kv_b=(w_kv_b2, m_w_kv_b[0], v_w_kv_b[0]), w_o_a=(w_o_a2, m_w_o_a[0], v_w_o_a[0]),
              w_o_b=(w_o_b2, m_w_o_b[0], v_w_o_b[0]), w_o=(w_o2, m_w_o[0], v_w_o[0]),
              w_ffn_in=(w_ffn_in2, m_w_ffn_in[0], v_w_ffn_in[0]), w_ffn_out=(w_ffn_out2, m_w_ffn_out[0], v_w_ffn_out[0]))

    def adam_of(nm, a, b, carry=None):
        w_, m_, v_ = ws[nm]
        return _adam_halves("adam_" + nm, w_, m_, v_, a, b, core_i, carry)

    for nm, a, b in zip(names_a, fin_a, fs_a):
        big[nm] = adam_of(nm, a, b, _token_plan(st_b[4][0]))[0]
    done = [big[nm][1] for nm in names_a] + [grad_x]
    pre_b, r_b = _split_wait(st_b, cs_b, done, "scatter_last_wait")
    fin_b = sum_all(pre_b, r_b, names_b)
    fs_b = _run_plan(_sibling_plan(fin_b), "sibling_last")
    for nm, a, b in zip(names_b, fin_b, fs_b):
        big[nm] = adam_of(nm, a, b)

    def pad_d(v):
        return jnp.pad(v, ((0, 0), (0, D - v.shape[1])))

    small = _pack_rows([d_ln1_g, d_ln1_b, d_ln2_g, d_ln2_b, pad_d(d_g_q), pad_d(d_g_kv), d_wconv,
                         d_shift1, d_scale1, d_gate1, d_shift2, d_scale2, d_gate2, pad_d(loss_acc)], 16, after=[pre_b[1]])
    small_all = _all_gather8(small, "gather_small")
    small_sum = _sum8(small_all)
    loss = small_sum[15, 0]
    g_ln1_g, g_ln1_b, g_ln2_g, g_ln2_b = (small_sum[k:k + 1] for k in range(4))
    g_g_q, g_g_kv = small_sum[4:5, :Q_LORA], small_sum[5:6, :KV_LORA]
    g_wconv = lax.dynamic_slice(small_sum[6:9], (0, chip * CW), (3, CW))
    g_b_ada = small_sum[9:15].reshape(1, 6 * D)
    dmod_all = small_all[:, 9:15, :].reshape(8, 6 * D)
    g_w_ada = _ada_bwd(c_all, lax.dynamic_slice(dmod_all, (0, chip * NA), (8, NA)))
    big["w_ada"] = [g_w_ada] + list(_adam("adam_w_ada", w_ada2, m_w_ada[0], v_w_ada[0], g_w_ada))
    sm = {}
    for nm, w_, m_, v_, g_ in [("b_ada", b_ada, m_b_ada, v_b_ada, g_b_ada), ("g_q_a", g_q_a, m_g_q_a, v_g_q_a, g_g_q),
                               ("g_kv_a", g_kv_a, m_g_kv_a, v_g_kv_a, g_g_kv),
                               ("w_conv", w_conv[0], m_w_conv[0], v_w_conv[0], g_wconv),
                               ("ln1_g", ln1_g, m_ln1_g, v_ln1_g, g_ln1_g), ("ln1_b", ln1_b, m_ln1_b, v_ln1_b, g_ln1_b),
                               ("ln2_g", ln2_g, m_ln2_g, v_ln2_g, g_ln2_g), ("ln2_b", ln2_b, m_ln2_b, v_ln2_b, g_ln2_b)]:
        sm[nm] = (g_,) + tuple(_adam_small("adam_" + nm, w_, m_, v_, g_))

    order = ["w_ada", "b_ada", "w_in", "g_q_a", "w_q_b", "g_kv_a", "w_kv_b", "w_o_a", "w_conv", "w_o_b", "w_o",
             "ln1_g", "ln1_b", "w_ffn_in", "w_ffn_out", "ln2_g", "ln2_b"]
    lead = {"b_ada", "g_q_a", "g_kv_a", "ln1_g", "ln1_b", "ln2_g", "ln2_b"}

    def leaf(nm, k):
        val = big[nm][k] if nm in big else sm[nm][k]
        if nm == "w_in":
            val = val.T
        return val if nm in lead else val[None]

    outs = [loss, grad_x[None]]
    for k in range(4):
        outs += [leaf(nm, k) for nm in order]
    return tuple(outs)
```

```python
import jax
import jax.numpy as jnp
from jax import lax
from jax.experimental import pallas as pl
from jax.experimental.pallas import tpu as pltpu

F32, BF16 = jnp.float32, jnp.bfloat16
N_HEADS, QK_NOPE, QK_ROPE, V_HEAD = 16, 128, 64, 128
Q_LORA, KV_LORA = 512, 512
QK_PAD = 256
QKV_A = 1152
CHUNK_SHIFT = 6
ATTN_SCALE = (QK_NOPE + QK_ROPE) ** -0.5
LOG2E = 1.4426950408889634
SCALE2 = ATTN_SCALE * LOG2E
ROPE_THETA = 10000.0
ALPHA = 2.0 ** 0.25
LN_EPS, RMS_EPS = 1e-5, 1e-6
ADAM_LR, ADAM_B1, ADAM_B2, ADAM_EPS, ADAM_WD, ADAM_STEP = 0.001, 0.9, 0.999, 1e-08, 0.01, 10
ADAM_C1 = 1.0 - ADAM_B1 ** ADAM_STEP
ADAM_C2 = 1.0 - ADAM_B2 ** ADAM_STEP
VMEM_LIMIT = 56 * 1024 * 1024
MESH = pl.DeviceIdType.MESH
ANY = pl.BlockSpec(memory_space=pl.ANY)
HBM_SPEC = pl.BlockSpec(memory_space=pltpu.HBM)
SEM_SPEC = pl.BlockSpec(memory_space=pltpu.SEMAPHORE)
NT = (((1,), (1,)), ((), ()))
TN = (((0,), (0,)), ((), ()))
NN = (((1,), (0,)), ((), ()))


def _params(sem=None):
    return pltpu.CompilerParams(dimension_semantics=sem, vmem_limit_bytes=VMEM_LIMIT)


def _pick(n, cands=(1408, 1024, 512, 384, 256, 128)):
    for t in cands:
        if n % t == 0:
            return t
    return n


def _row_tile(rows, row_bytes, budget, mult=8):
    best = mult
    for t in range(mult, rows + 1, mult):
        if rows % t == 0 and t * row_bytes <= budget:
            best = t
    return best


def _tile2(rows, cols, mult=8, budget=3 << 18):
    col_tiles = [t for t in range(128, cols + 1, 128) if cols % t == 0] or [cols]
    best = None
    for tc in col_tiles:
        for tr in range(mult, rows + 1, mult):
            if rows % tr == 0 and tr * tc <= budget and (best is None or (tr * tc, tc) > (best[0] * best[1], best[1])):
                best = (tr, tc)
    assert best is not None, (rows, cols)
    return best


def _sigmoid(x):
    return jax.nn.sigmoid(x)


class _Plan:
    def __init__(self, ins, outs, sems, start, finish, aliases=None):
        self.ins, self.outs, self.sems, self.start, self.finish = list(ins), list(outs), list(sems), start, finish
        self.aliases = dict(aliases or {})

    def io_aliases(self, first_in, first_out):
        return {first_in + i: first_out + o for i, o in self.aliases.items()}


def _token_plan(token):
    return _Plan([token], [], [], lambda *a: None, lambda *a: None)


def _run_plan(plan, name, ride=None):
    n_in, n_out = len(plan.ins), len(plan.outs)
    extra = [] if ride is None else list(ride)
    aliases = plan.io_aliases(0, 0)
    for k in range(len(extra)):
        aliases[n_in + k] = n_out + k

    def body(*refs):
        ins, outs, sems = refs[:n_in], refs[n_in + len(extra):n_in + len(extra) + n_out], refs[n_in + 2 * len(extra) + n_out:]
        plan.start(ins, outs, sems)
        plan.finish(ins, outs, sems)

    return pl.pallas_call(body, name=name, out_shape=plan.outs + [jax.ShapeDtypeStruct(r.shape, r.dtype) for r in extra],
                          in_specs=[ANY] * (n_in + len(extra)), out_specs=[ANY] * (n_out + len(extra)),
                          scratch_shapes=plan.sems, input_output_aliases=aliases,
                          compiler_params=_params())(*plan.ins, *extra)


def _matmul(a, b, mode, out_dtype, name, add=None, carry=None, shards=None, finish=None):
    if mode == "nn":
        (M, K), N, dims = a.shape, b.shape[-1] * (4 if shards else 1), NN
    elif mode == "nt":
        (M, K), N, dims = a.shape, b.shape[-2], NT
    else:
        (K, M), N, dims = a.shape, b.shape[1], TN
    split_n = shards and mode != "nt"
    tm = _pick(M)
    tn = _pick(N // 4) if split_n else _pick(N)
    deep = (2816, 2048, 1408, 1024, 512, 384, 256, 128)
    if shards and mode == "nt":
        tk = _pick(K // 4, deep)
    else:
        tk = K if K <= 2048 else _pick(K, deep)
    nk = K // tk
    per = (N // 4 // tn) if split_n else (K // 4 // tk if shards else 1)
    a_spec = (pl.BlockSpec((tk, tm), lambda i, j, k: (k, i)) if mode == "tn"
              else pl.BlockSpec((tm, tk), lambda i, j, k: (i, k)))
    if shards == "b" and mode == "nn":
        b_spec = pl.BlockSpec((None, tk, tn), lambda i, j, k: (j // per, k, j % per))
    elif shards == "b":
        b_spec = pl.BlockSpec((None, tn, tk), lambda i, j, k: (k // per, j, k % per))
    else:
        b_spec = (pl.BlockSpec((tn, tk), lambda i, j, k: (j, k)) if mode == "nt"
                  else pl.BlockSpec((tk, tn), lambda i, j, k: (k, j)))
    o_spec = pl.BlockSpec((tm, tn), lambda i, j, k: (i, j))
    o_shape = (M, N)
    if shards == "o":
        o_spec, o_shape = pl.BlockSpec((None, tm, tn), lambda i, j, k: (j // per, i, j % per)), (4, M, N // 4)
    has_add = add is not None
    has_fin = finish is not None
    n_ci = len(carry.ins) if carry else 0
    n_co = len(carry.outs) if carry else 0
    n_in = 2 + has_add + has_fin
    grid = (M // tm, N // tn, nk)

    def body(*refs):
        a_ref, b_ref = refs[0], refs[1]
        add_ref = refs[2] if has_add else None
        fin_ref = refs[2 + has_add] if has_fin else None

        def store(r):
            if has_add:
                r = r + add_ref[...]
            if has_fin:
                r = finish[0](r, fin_ref[...])
            o_ref[...] = r.astype(o_ref.dtype)

        o_ref = refs[n_in + n_ci]
        acc_ref = refs[n_in + n_ci + 1 + n_co] if nk > 1 else None
        c_ins = refs[n_in:n_in + n_ci]
        c_outs = refs[n_in + n_ci + 1:n_in + n_ci + 1 + n_co]
        c_sems = refs[n_in + n_ci + 1 + n_co + (nk > 1):]
        i, j, k = pl.program_id(0), pl.program_id(1), pl.program_id(2)

        if carry:
            @pl.when((i == 0) & (j == 0) & (k == 0))
            def _():
                carry.start(c_ins, c_outs, c_sems)

        part = lax.dot_general(a_ref[...], b_ref[...], dims, preferred_element_type=F32)
        if nk == 1:
            store(part)
        else:
            @pl.when(k == 0)
            def _():
                acc_ref[...] = part

            @pl.when((k > 0) & (k < nk - 1))
            def _():
                acc_ref[...] += part

            @pl.when(k == nk - 1)
            def _():
                store(acc_ref[...] + part)

        if carry:
            @pl.when((i == grid[0] - 1) & (j == grid[1] - 1) & (k == nk - 1))
            def _():
                carry.finish(c_ins, c_outs, c_sems)

    ins = [a, b] + ([add] if has_add else []) + ([finish[1]] if has_fin else []) + (carry.ins if carry else [])
    in_specs = ([a_spec, b_spec] + ([o_spec] if has_add else [])
                + ([pl.BlockSpec((tm, finish[1].shape[1]), lambda i, j, k: (i, 0))] if has_fin else []) + [ANY] * n_ci)
    res = pl.pallas_call(
        body, name=name, grid=grid,
        in_specs=in_specs, out_specs=[o_spec] + [ANY] * n_co,
        out_shape=[jax.ShapeDtypeStruct(o_shape, out_dtype)] + (carry.outs if carry else []),
        scratch_shapes=([pltpu.VMEM((tm, tn), F32)] if nk > 1 else []) + (carry.sems if carry else []),
        input_output_aliases=carry.io_aliases(n_in, 1) if carry else {},
        compiler_params=_params(("arbitrary",) * 3 if carry else ("parallel", "parallel", "arbitrary")),
    )(*ins)
    return (res[0], res[1:]) if carry else res[0]


def _rows(body, name, n_rows, tm, ins, outs, accs=(), carry=None):
    grid = (n_rows // tm,)

    def halo(arr):
        return 16 if arr.dtype == BF16 else 8

    arrays, in_specs = [], []
    for spec in ins:
        kind, arr = spec[0], spec[1]
        arrays.append(arr)
        if kind == "row":
            _, _, cb, w = spec
            in_specs.append(pl.BlockSpec((tm, w), lambda i, cb=cb: (i, cb)))
        elif kind == "full":
            in_specs.append(pl.BlockSpec(arr.shape, lambda i, nd=arr.ndim: (0,) * nd))
        elif kind == "prev":
            _, _, cb, w = spec
            h = halo(arr)
            in_specs.append(pl.BlockSpec((h, w), lambda i, cb=cb, per=tm // h: (jnp.maximum(i * per - 1, 0), cb)))
        else:
            _, _, cb, w = spec
            h = halo(arr)
            in_specs.append(pl.BlockSpec((h, w), lambda i, cb=cb, per=tm // h, last=n_rows // h - 1:
                                         (jnp.minimum((i + 1) * per, last), cb)))
    out_shape = [jax.ShapeDtypeStruct((n_rows, w), dt) for (w, dt) in outs]
    out_specs = [pl.BlockSpec((tm, w), lambda i: (i, 0)) for (w, _) in outs]
    out_shape += [jax.ShapeDtypeStruct(s, F32) for s in accs]
    out_specs += [pl.BlockSpec(s, lambda i, nd=len(s): (0,) * nd) for s in accs]
    n_in, n_out, n_acc = len(ins), len(outs), len(accs)
    n_ci = len(carry.ins) if carry else 0
    n_co = len(carry.outs) if carry else 0

    def kernel_body(*refs):
        first = n_in + n_ci
        c_ins, c_outs, c_sems = refs[n_in:first], refs[first + n_out + n_acc:first + n_out + n_acc + n_co], refs[first + n_out + n_acc + n_co:]
        if carry:
            @pl.when(pl.program_id(0) == 0)
            def _():
                carry.start(c_ins, c_outs, c_sems)

        body(pl.program_id(0), refs[:n_in], refs[first:first + n_out], refs[first + n_out:first + n_out + n_acc])
        if carry:
            @pl.when(pl.program_id(0) == grid[0] - 1)
            def _():
                carry.finish(c_ins, c_outs, c_sems)

    res = pl.pallas_call(
        kernel_body, name=name, grid=grid, in_specs=in_specs + [ANY] * n_ci, out_specs=out_specs + [ANY] * n_co,
        out_shape=out_shape + (carry.outs if carry else []), scratch_shapes=carry.sems if carry else [],
        input_output_aliases=carry.io_aliases(n_in, n_out + n_acc) if carry else {},
        compiler_params=_params(("arbitrary",)),
    )(*arrays, *(carry.ins if carry else []))
    return (res[:n_out + n_acc], res[n_out + n_acc:]) if carry else res


def _acc_add(i, ref, val):
    @pl.when(i == 0)
    def _():
        ref[...] = val

    @pl.when(i > 0)
    def _():
        ref[...] += val


def _rope(t, tab, sign):
    c, sa, sb = tab[:, 0:128], tab[:, 128:256], tab[:, 256:384]
    rot = pltpu.roll(t, 96, 1) * sa + pltpu.roll(t, 32, 1) * sb
    return t * c + rot if sign > 0 else t * c - rot


def _ln_stats(r):
    mu = jnp.mean(r, axis=-1, keepdims=True)
    d = r - mu
    var = jnp.mean(d * d, axis=-1, keepdims=True)
    rstd = lax.rsqrt(var + LN_EPS)
    return d * rstd, rstd


def _ln_bwd(dxh, xh, rstd):
    m1 = jnp.mean(dxh, axis=-1, keepdims=True)
    m2 = jnp.mean(dxh * xh, axis=-1, keepdims=True)
    return rstd * (dxh - m1 - xh * m2)


def _modulate(x, scale, shift, name, carry=None):
    S, D = x.shape

    def body(i, ins, outs, accs):
        outs[0][...] = (ins[0][...] * (1.0 + ins[1][...]) + ins[2][...]).astype(BF16)

    res = _rows(body, name, S, _pick(S, (256, 128)), [("row", x, 0, D), ("full", scale), ("full", shift)], [(D, BF16)],
                carry=carry)
    return (res[0][0], res[1]) if carry else res[0]


def _rms_fwd(pq, tab, g_q, g_kv):
    S = pq.shape[0]

    def body(i, ins, outs, accs):
        pq_ref, tab_ref, gq_ref, gkv_ref = ins

        def rms(x, g):
            return x * lax.rsqrt(jnp.mean(x * x, axis=-1, keepdims=True) + RMS_EPS) * g

        outs[0][...] = rms(pq_ref[:, 0:Q_LORA], gq_ref[...]).astype(BF16)
        outs[1][...] = rms(pq_ref[:, Q_LORA:Q_LORA + KV_LORA], gkv_ref[...]).astype(BF16)
        outs[2][...] = _rope(pq_ref[:, Q_LORA + KV_LORA:QKV_A], tab_ref[...], 1).astype(BF16)

    return _rows(body, "rms_fwd", S, _pick(S, (256, 128)),
                 [("row", pq, 0, QKV_A), ("row", tab, 0, 384), ("full", g_q), ("full", g_kv)],
                 [(Q_LORA, BF16), (KV_LORA, BF16), (128, BF16)])


def _allowed(q0, k0, bq):
    row = q0 + lax.broadcasted_iota(jnp.int32, (bq, bq), 0)
    col = k0 + lax.broadcasted_iota(jnp.int32, (bq, bq), 1)
    return (col >> CHUNK_SHIFT) <= (row >> CHUNK_SHIFT)


ATTN_BLOCK = 512


HEADS_PER_STEP = 2


def _attn_fwd(q, kv, kr):
    S = q.shape[0]
    bq = min(ATTN_BLOCK, S)
    nq = S // bq
    G = 2 * HEADS_PER_STEP

    def body(q_ref, kv_ref, kr_ref, o_ref, lse_ref, kcat):
        qi = pl.program_id(1)

        @pl.when(qi == 0)
        def _():
            for g in range(G):
                kcat[g, :, 0:128] = kv_ref[:, g * 256:g * 256 + 128]
                kcat[g, :, 128:256] = kr_ref[...]

        qs = [q_ref[:, g * QK_PAD:(g + 1) * QK_PAD] for g in range(G)]

        def step(j, carry, masked):
            off = pl.multiple_of(j * bq, bq)
            rows = pl.ds(off, bq)
            mask = _allowed(qi * bq, off, bq) if masked else None
            out = []
            for g in range(G):
                m, l, acc = carry[g]
                s = lax.dot_general(qs[g], kcat[g, rows, :], NT, preferred_element_type=F32) * SCALE2
                if masked:
                    s = jnp.where(mask, s, -1e30)
                m_new = jnp.maximum(m, jnp.max(s, axis=1, keepdims=True))
                a = jnp.exp2(m - m_new)
                p = jnp.exp2(s - m_new)
                l = a * l + jnp.sum(p, axis=1, keepdims=True)
                acc = a * acc + jnp.dot(p.astype(BF16), kv_ref[rows, g * 256 + 128:(g + 1) * 256],
                                        preferred_element_type=F32)
                out.append((m_new, l, acc))
            return tuple(out)

        init = tuple((jnp.full((bq, 1), -1e30, F32), jnp.zeros((bq, 1), F32), jnp.zeros((bq, V_HEAD), F32))
                     for _ in range(G))
        below = lax.fori_loop(0, qi, lambda j, cr: step(j, cr, False), init)
        for g, (m, l, acc) in enumerate(step(qi, below, True)):
            o_ref[:, g * V_HEAD:(g + 1) * V_HEAD] = (acc / l).astype(BF16)
            lse_ref[g] = m + jnp.log2(l)

    return pl.pallas_call(
        body, name="attn_fwd", grid=(N_HEADS // G, nq),
        in_specs=[pl.BlockSpec((bq, G * QK_PAD), lambda h, i: (i, h)),
                  pl.BlockSpec((S, G * 256), lambda h, i: (0, h)),
                  pl.BlockSpec((S, 128), lambda h, i: (0, 0))],
        out_specs=[pl.BlockSpec((bq, G * V_HEAD), lambda h, i: (i, h)),
                   pl.BlockSpec((G, bq, 1), lambda h, i: (h, i, 0))],
        out_shape=[jax.ShapeDtypeStruct((S, N_HEADS * V_HEAD), BF16),
                   jax.ShapeDtypeStruct((N_HEADS, S, 1), F32)],
        scratch_shapes=[pltpu.VMEM((G, S, QK_PAD), BF16)],
        compiler_params=_params(("arbitrary", "arbitrary")),
    )(q, kv, kr)


def _attn_bwd(q, kv, kr, do, o, lse, tab, carry=None):
    S = q.shape[0]
    bq = min(ATTN_BLOCK, S)
    nq = S // bq
    G = HEADS_PER_STEP
    n_ci = len(carry.ins) if carry else 0
    n_co = len(carry.outs) if carry else 0

    def body(*refs):
        q_ref, kv_ref, kr_ref, do_ref, o_ref, lse_ref, tab_ref = refs[:7]
        dq_ref, dkv_ref, dkr_ref = refs[7 + n_ci:10 + n_ci]
        dq_acc, dk_acc, dv_acc, kcat = refs[10 + n_ci + n_co:14 + n_ci + n_co]
        c_ins, c_outs, c_sems = refs[7:7 + n_ci], refs[10 + n_ci:10 + n_ci + n_co], refs[14 + n_ci + n_co:]
        h = pl.program_id(0)
        if carry:
            @pl.when(h == 0)
            def _():
                carry.start(c_ins, c_outs, c_sems)

        dq_acc[...] = jnp.zeros_like(dq_acc)
        dk_acc[...] = jnp.zeros_like(dk_acc)
        dv_acc[...] = jnp.zeros_like(dv_acc)
        for g in range(G):
            kcat[g, :, 0:128] = kv_ref[:, g * 256:g * 256 + 128]
            kcat[g, :, 128:256] = kr_ref[...]

        def pair(i, j, masked):
            rows_i = pl.ds(pl.multiple_of(i * bq, bq), bq)
            rows_j = pl.ds(pl.multiple_of(j * bq, bq), bq)
            mask = _allowed(i * bq, j * bq, bq) if masked else None
            for g in range(G):
                qv, k = q_ref[rows_i, g * QK_PAD:(g + 1) * QK_PAD], kcat[g, rows_j, :]
                dov = do_ref[rows_i, g * V_HEAD:(g + 1) * V_HEAD]
                delta = jnp.sum(dov.astype(F32) * o_ref[rows_i, g * V_HEAD:(g + 1) * V_HEAD].astype(F32),
                                axis=1, keepdims=True)
                s = lax.dot_general(qv, k, NT, preferred_element_type=F32) * SCALE2
                if masked:
                    s = jnp.where(mask, s, -1e30)
                p = jnp.exp2(s - lse_ref[g, rows_i, :])
                dv_acc[g, rows_j, :] += lax.dot_general(p.astype(BF16), dov, TN, preferred_element_type=F32)
                dp = lax.dot_general(dov, kv_ref[rows_j, g * 256 + 128:(g + 1) * 256], NT, preferred_element_type=F32)
                ds = (p * (dp - delta) * ATTN_SCALE).astype(BF16)
                dk_acc[g, rows_j, :] += lax.dot_general(ds, qv, TN, preferred_element_type=F32)
                dq_acc[g, rows_i, :] += jnp.dot(ds, k, preferred_element_type=F32)

        def kv_step(j, _):
            pair(j, j, True)

            def q_step(i, _):
                pair(i, j, False)
                return 0

            lax.fori_loop(j + 1, nq, q_step, 0)
            return 0

        lax.fori_loop(0, nq, kv_step, 0)

        for g in range(G):
            lo = g * QK_PAD
            for r in range(nq):
                rows = slice(r * bq, (r + 1) * bq)
                dq_ref[rows, lo:lo + 128] = dq_acc[g, rows, 0:128].astype(BF16)
                dq_ref[rows, lo + 128:lo + 256] = _rope(dq_acc[g, rows, 128:256], tab_ref[rows, :], -1).astype(BF16)
            dkv_ref[:, lo:lo + 128] = dk_acc[g, :, 0:128].astype(BF16)
            dkv_ref[:, lo + 128:lo + 256] = dv_acc[g].astype(BF16)
        dkr_sum = dk_acc[0, :, 128:256]
        for g in range(1, G):
            dkr_sum = dkr_sum + dk_acc[g, :, 128:256]

        @pl.when(h == 0)
        def _():
            dkr_ref[...] = dkr_sum

        @pl.when(h > 0)
        def _():
            dkr_ref[...] += dkr_sum

        @pl.when(h == N_HEADS // G - 1)
        def _():
            for r in range(nq):
                rows = slice(r * bq, (r + 1) * bq)
                dkr_ref[rows, :] = _rope(dkr_ref[rows, :], tab_ref[rows, :], -1)
            if carry:
                carry.finish(c_ins, c_outs, c_sems)

    W = N_HEADS * QK_PAD
    res = pl.pallas_call(
        body, name="attn_bwd", grid=(N_HEADS // G,),
        in_specs=[pl.BlockSpec((S, G * QK_PAD), lambda h: (0, h)),
                  pl.BlockSpec((S, G * 256), lambda h: (0, h)),
                  pl.BlockSpec((S, 128), lambda h: (0, 0)),
                  pl.BlockSpec((S, G * V_HEAD), lambda h: (0, h)),
                  pl.BlockSpec((S, G * V_HEAD), lambda h: (0, h)),
                  pl.BlockSpec((G, S, 1), lambda h: (h, 0, 0)),
                  pl.BlockSpec((S, 384), lambda h: (0, 0))] + [ANY] * n_ci,
        out_specs=[pl.BlockSpec((S, G * QK_PAD), lambda h: (0, h)),
                   pl.BlockSpec((S, G * QK_PAD), lambda h: (0, h)),
                   pl.BlockSpec((S, 128), lambda h: (0, 0))] + [ANY] * n_co,
        out_shape=[jax.ShapeDtypeStruct((S, W), BF16), jax.ShapeDtypeStruct((S, W), BF16),
                   jax.ShapeDtypeStruct((S, 128), F32)] + (carry.outs if carry else []),
        scratch_shapes=[pltpu.VMEM((G, S, QK_PAD), F32), pltpu.VMEM((G, S, QK_PAD), F32), pltpu.VMEM((G, S, V_HEAD), F32),
                        pltpu.VMEM((G, S, QK_PAD), BF16)]
        + (carry.sems if carry else []),
        input_output_aliases=carry.io_aliases(7, 3) if carry else {},
        compiler_params=_params(("arbitrary",)),
    )(q, kv, kr, do, o, lse, tab, *(carry.ins if carry else []))
    return res[0], res[1], res[2], res[3:]


def _shift_down(cur, prev, i, n):
    tm, h = cur.shape[0], prev.shape[0]
    prev = jnp.where(i == 0, jnp.zeros_like(prev), prev)
    full = jnp.concatenate([prev, cur], axis=0)
    return pltpu.roll(full, n, 0)[h:h + tm, :]


def _shift_up(cur, nxt, i, last, n):
    tm, h = cur.shape[0], nxt.shape[0]
    nxt = jnp.where(i == last, jnp.zeros_like(nxt), nxt)
    full = jnp.concatenate([cur, nxt], axis=0)
    return pltpu.roll(full, tm + h - n, 0)[0:tm, :]


def _conv_fwd(pc, w_conv):
    S, D = pc.shape[0], pc.shape[1] // 3
    tm = _pick(S, (256, 128))

    def body(i, ins, outs, accs):
        b_ref, c_ref, x_ref, cp_ref, xp_ref, w_ref = ins
        z = c_ref[...].astype(F32) * x_ref[...].astype(F32)
        zp = cp_ref[...].astype(F32) * xp_ref[...].astype(F32)
        cz = w_ref[0:1, :] * _shift_down(z, zp, i, 2) + w_ref[1:2, :] * _shift_down(z, zp, i, 1) + w_ref[2:3, :] * z
        outs[0][...] = (b_ref[...].astype(F32) * cz).astype(BF16)

    return _rows(body, "conv_fwd", S, tm,
                 [("row", pc, 0, D), ("row", pc, 1, D), ("row", pc, 2, D), ("prev", pc, 1, D), ("prev", pc, 2, D),
                  ("full", w_conv)], [(D, BF16)])[0]


def _conv_bwd(dhb, pc, w_conv):
    S, D = dhb.shape
    tm = _pick(S, (256, 128))
    last = S // tm - 1

    def body(i, ins, outs, accs):
        g_ref, b_ref, c_ref, x_ref, cp_ref, xp_ref, gn_ref, bn_ref, w_ref = ins
        w0, w1, w2 = w_ref[0:1, :], w_ref[1:2, :], w_ref[2:3, :]
        c, x, g = c_ref[...].astype(F32), x_ref[...].astype(F32), g_ref[...].astype(F32)
        z = c * x
        zp = cp_ref[...].astype(F32) * xp_ref[...].astype(F32)
        z1, z2 = _shift_down(z, zp, i, 1), _shift_down(z, zp, i, 2)
        cz = w0 * z2 + w1 * z1 + w2 * z
        dcz = g * b_ref[...].astype(F32)
        dczn = gn_ref[...].astype(F32) * bn_ref[...].astype(F32)
        dz = w2 * dcz + w1 * _shift_up(dcz, dczn, i, last, 1) + w0 * _shift_up(dcz, dczn, i, last, 2)
        outs[0][:, 0:D] = (g * cz).astype(BF16)
        outs[0][:, D:2 * D] = (dz * x).astype(BF16)
        outs[0][:, 2 * D:3 * D] = (dz * c).astype(BF16)
        dw = jnp.concatenate([jnp.sum(dcz * z2, axis=0, keepdims=True), jnp.sum(dcz * z1, axis=0, keepdims=True),
                              jnp.sum(dcz * z, axis=0, keepdims=True)], axis=0)
        _acc_add(i, accs[0], dw)

    return _rows(body, "conv_bwd", S, tm,
                 [("row", dhb, 0, D), ("row", pc, 0, D), ("row", pc, 1, D), ("row", pc, 2, D),
                  ("prev", pc, 1, D), ("prev", pc, 2, D), ("next", dhb, 0, D), ("next", pc, 0, D), ("full", w_conv)],
                 [(3 * D, BF16)], [(3, D)])


def _merge_fwd(y_a, y_b, pg):
    S, D = y_a.shape

    def body(i, ins, outs, accs):
        ya, yb, ga, gb = ins
        outs[0][...] = (_sigmoid(ga[...].astype(F32)) * ya[...].astype(F32)
                        + _sigmoid(gb[...].astype(F32)) * yb[...].astype(F32)).astype(BF16)

    return _rows(body, "merge_fwd", S, _pick(S, (256, 128)),
                 [("row", y_a, 0, D), ("row", y_b, 0, D), ("row", pg, 0, D), ("row", pg, 1, D)], [(D, BF16)])[0]


def _merge_bwd(dm, y_a, y_b, pg):
    S, D = dm.shape

    def body(i, ins, outs, accs):
        d, ya, yb = ins[0][...].astype(F32), ins[1][...].astype(F32), ins[2][...].astype(F32)
        sa, sb = _sigmoid(ins[3][...].astype(F32)), _sigmoid(ins[4][...].astype(F32))
        outs[0][...] = (d * sa).astype(BF16)
        outs[1][...] = (d * sb).astype(BF16)
        outs[2][:, 0:D] = (d * ya * (sa * (1.0 - sa))).astype(BF16)
        outs[2][:, D:2 * D] = (d * yb * (sb * (1.0 - sb))).astype(BF16)

    return _rows(body, "merge_bwd", S, _pick(S, (256, 128)),
                 [("row", dm, 0, D), ("row", y_a, 0, D), ("row", y_b, 0, D), ("row", pg, 0, D), ("row", pg, 1, D)],
                 [(D, BF16), (D, BF16), (2 * D, BF16)])


def _ln1_fwd(x, mix, gate1, g, b, scale2, shift2, carry=None):
    S, D = x.shape

    def body(i, ins, outs, accs):
        x_ref, mix_ref, gate_ref, g_ref, b_ref, sc_ref, sh_ref = ins
        xh, _ = _ln_stats(ALPHA * x_ref[...] + gate_ref[...] * mix_ref[...])
        x1 = xh * g_ref[...] + b_ref[...]
        outs[0][...] = x1
        outs[1][...] = (x1 * (1.0 + sc_ref[...]) + sh_ref[...]).astype(BF16)

    return _rows(body, "ln1_fwd", S, _pick(S, (256, 128)),
                 [("row", x, 0, D), ("row", mix, 0, D), ("full", gate1), ("full", g), ("full", b),
                  ("full", scale2), ("full", shift2)], [(D, F32), (D, BF16)], carry=carry)


def _swiglu_fwd(hh, carry=None):
    S, F = hh.shape[0], hh.shape[1] // 2

    def body(i, ins, outs, accs):
        hg = ins[0][...].astype(F32)
        outs[0][...] = (hg * _sigmoid(hg) * ins[1][...].astype(F32)).astype(BF16)

    res = _rows(body, "swiglu_fwd", S, _pick(S, (128,)), [("row", hh, 0, F), ("row", hh, 1, F)], [(F, BF16)], carry=carry)
    return (res[0][0], res[1]) if carry else res[0]


def _swiglu_bwd(dact, hh):
    S, F = dact.shape

    def body(i, ins, outs, accs):
        d, hg, hu = ins[0][...].astype(F32), ins[1][...].astype(F32), ins[2][...].astype(F32)
        sg = _sigmoid(hg)
        outs[0][:, 0:F] = (d * hu * (sg * (1.0 + hg * (1.0 - sg)))).astype(BF16)
        outs[0][:, F:2 * F] = (d * (hg * sg)).astype(BF16)

    return _rows(body, "swiglu_bwd", S, _pick(S, (128,)),
                 [("row", dact, 0, F), ("row", hh, 0, F), ("row", hh, 1, F)], [(2 * F, BF16)])[0]


def _ln2_loss_bwd(x1, ffn, gate2, g, b, target):
    S, D = x1.shape

    def body(i, ins, outs, accs):
        x1_ref, f_ref, gate_ref, g_ref, b_ref, t_ref = ins
        f = f_ref[...]
        xh, rstd = _ln_stats(ALPHA * x1_ref[...] + gate_ref[...] * f)
        e = xh * g_ref[...] + b_ref[...] - t_ref[...]
        dy = e * (1.0 / D)
        dr = _ln_bwd(dy * g_ref[...], xh, rstd)
        outs[0][...] = (gate_ref[...] * dr).astype(BF16)
        outs[1][...] = ALPHA * dr
        _acc_add(i, accs[0], jnp.full((1, 128), (0.5 / D) * jnp.sum(e * e), F32))
        _acc_add(i, accs[1], jnp.sum(dy * xh, axis=0, keepdims=True))
        _acc_add(i, accs[2], jnp.sum(dy, axis=0, keepdims=True))
        _acc_add(i, accs[3], jnp.sum(dr * f, axis=0, keepdims=True))

    return _rows(body, "ln2_loss_bwd", S, _pick(S, (256, 128)),
                 [("row", x1, 0, D), ("row", ffn, 0, D), ("full", gate2), ("full", g), ("full", b), ("row", target, 0, D)],
                 [(D, BF16), (D, F32)], [(1, 128), (1, D), (1, D), (1, D)])


def _ln1_bwd(x, mix, dx1a, du2, gate1, g, b, scale2):
    S, D = x.shape

    def body(i, ins, outs, accs):
        x_ref, mix_ref, da_ref, du_ref, gate_ref, g_ref, b_ref, sc_ref = ins
        mix, du = mix_ref[...], du_ref[...]
        xh, rstd = _ln_stats(ALPHA * x_ref[...] + gate_ref[...] * mix)
        x1 = xh * g_ref[...] + b_ref[...]
        dx1 = da_ref[...] + du * (1.0 + sc_ref[...])
        dr = _ln_bwd(dx1 * g_ref[...], xh, rstd)
        outs[0][...] = (gate_ref[...] * dr).astype(BF16)
        outs[1][...] = ALPHA * dr
        _acc_add(i, accs[0], jnp.sum(du, axis=0, keepdims=True))
        _acc_add(i, accs[1], jnp.sum(du * x1, axis=0, keepdims=True))
        _acc_add(i, accs[2], jnp.sum(dx1 * xh, axis=0, keepdims=True))
        _acc_add(i, accs[3], jnp.sum(dx1, axis=0, keepdims=True))
        _acc_add(i, accs[4], jnp.sum(dr * mix, axis=0, keepdims=True))

    return _rows(body, "ln1_bwd", S, _pick(S, (256, 128)),
                 [("row", x, 0, D), ("row", mix, 0, D), ("row", dx1a, 0, D), ("row", du2, 0, D),
                  ("full", gate1), ("full", g), ("full", b), ("full", scale2)],
                 [(D, BF16), (D, F32)], [(1, D)] * 5)


def _rms_bwd(d_rq, d_rkv, pq, dkr, g_q, g_kv):
    S = pq.shape[0]

    def body(i, ins, outs, accs):
        dq_ref, dkv_ref, pq_ref, dkr_ref, gq_ref, gkv_ref = ins

        def rms_bwd(dy, x, g):
            r = lax.rsqrt(jnp.mean(x * x, axis=-1, keepdims=True) + RMS_EPS)
            dyg = dy * g
            dx = r * dyg - x * (r * r * r) * jnp.mean(dyg * x, axis=-1, keepdims=True)
            return dx, jnp.sum(dy * (x * r), axis=0, keepdims=True)

        dxq, dgq = rms_bwd(dq_ref[...], pq_ref[:, 0:Q_LORA], gq_ref[...])
        dxkv, dgkv = rms_bwd(dkv_ref[...], pq_ref[:, Q_LORA:Q_LORA + KV_LORA], gkv_ref[...])
        outs[0][:, 0:Q_LORA] = dxq.astype(BF16)
        outs[0][:, Q_LORA:Q_LORA + KV_LORA] = dxkv.astype(BF16)
        outs[0][:, Q_LORA + KV_LORA:QKV_A] = dkr_ref[...].astype(BF16)
        _acc_add(i, accs[0], dgq)
        _acc_add(i, accs[1], dgkv)

    return _rows(body, "rms_bwd", S, _pick(S, (256, 128)),
                 [("row", d_rq, 0, Q_LORA), ("row", d_rkv, 0, KV_LORA), ("row", pq, 0, QKV_A), ("row", dkr, 0, 128),
                  ("full", g_q), ("full", g_kv)], [(QKV_A, BF16)], [(1, Q_LORA), (1, KV_LORA)])


def _dx_final(dxa, du, x, scale1):
    S, D = x.shape

    def body(i, ins, outs, accs):
        du = ins[1][...]
        outs[0][...] = ins[0][...] + du * (1.0 + ins[3][...])
        _acc_add(i, accs[0], jnp.sum(du, axis=0, keepdims=True))
        _acc_add(i, accs[1], jnp.sum(du * ins[2][...], axis=0, keepdims=True))

    return _rows(body, "dx_final", S, _pick(S, (256, 128)),
                 [("row", dxa, 0, D), ("row", du, 0, D), ("row", x, 0, D), ("full", scale1)],
                 [(D, F32)], [(1, D), (1, D)])


def _ada_fwd(c_all, w, bias):
    B, D = c_all.shape
    NA = w.shape[1]
    tn = _pick(NA, (512, 256, 128))

    def body(c_ref, w_ref, b_ref, o_ref):
        cv = c_ref[...]
        ca = (cv * _sigmoid(cv)).astype(BF16)
        o_ref[...] = jnp.dot(ca, w_ref[...].astype(BF16), preferred_element_type=F32) + b_ref[...]

    return pl.pallas_call(
        body, name="ada_fwd", grid=(NA // tn,),
        in_specs=[pl.BlockSpec((B, D), lambda j: (0, 0)), pl.BlockSpec((D, tn), lambda j: (0, j)),
                  pl.BlockSpec((1, tn), lambda j: (0, j))],
        out_specs=pl.BlockSpec((B, tn), lambda j: (0, j)),
        out_shape=jax.ShapeDtypeStruct((B, NA), F32),
        compiler_params=_params(("arbitrary",)),
    )(c_all, w, bias)


def _ada_bwd(c_all, dmod):
    B, D = c_all.shape
    NA = dmod.shape[1]
    tn = _pick(NA, (512, 256, 128))

    def body(c_ref, d_ref, o_ref):
        cv = c_ref[...]
        ca = (cv * _sigmoid(cv)).astype(BF16)
        o_ref[...] = lax.dot_general(ca, d_ref[...].astype(BF16), TN, preferred_element_type=F32)

    return pl.pallas_call(
        body, name="ada_bwd", grid=(NA // tn,),
        in_specs=[pl.BlockSpec((B, D), lambda j: (0, 0)), pl.BlockSpec((B, tn), lambda j: (0, j))],
        out_specs=pl.BlockSpec((D, tn), lambda j: (0, j)),
        out_shape=jax.ShapeDtypeStruct((D, NA), F32),
        compiler_params=_params(("arbitrary",)),
    )(c_all, dmod)


def _pack_rows(parts, n_rows, after=()):
    N = parts[0].shape[1]
    n = len(parts)

    def body(*refs):
        o_ref = refs[-1]
        o_ref[...] = jnp.zeros_like(o_ref)
        at = 0
        for r in refs[:n]:
            o_ref[at:at + r.shape[0], :] = r[...]
            at += r.shape[0]

    vmem = pl.BlockSpec(memory_space=pltpu.VMEM)
    return pl.pallas_call(body, name="pack_small", out_shape=jax.ShapeDtypeStruct((n_rows, N), F32),
                          in_specs=[vmem] * n + [ANY] * len(after), out_specs=vmem,
                          compiler_params=_params())(*parts, *after)


def _sum8(parts):
    _, R, N = parts.shape

    def body(p_ref, o_ref):
        acc = p_ref[0]
        for d in range(1, 8):
            acc = acc + p_ref[d]
        o_ref[...] = acc

    return pl.pallas_call(body, name="sum8", out_shape=jax.ShapeDtypeStruct((R, N), F32),
                          compiler_params=_params())(parts)


def _adam_math(w, g, m, v):
    m = ADAM_B1 * m + (1.0 - ADAM_B1) * g
    v = ADAM_B2 * v + (1.0 - ADAM_B2) * (g * g)
    delta = -ADAM_LR * ((m / ADAM_C1) / (jnp.sqrt(v / ADAM_C2) + ADAM_EPS) + ADAM_WD * w)
    return delta, m, v


def _adam(name, w, m, v, g, carry=None):
    R, C = w.shape
    tm = _row_tile(R, C * 4, 1 << 20)
    steps = R // tm
    n_ci = len(carry.ins) if carry else 0
    n_co = len(carry.outs) if carry else 0

    def body(*refs):
        w_ref, m_ref, v_ref, g_ref = refs[:4]
        d_ref, nm_ref, nv_ref = refs[4 + n_ci:7 + n_ci]
        c_ins, c_outs, c_sems = refs[4:4 + n_ci], refs[7 + n_ci:7 + n_ci + n_co], refs[7 + n_ci + n_co:]
        if carry:
            @pl.when(pl.program_id(0) == 0)
            def _():
                carry.start(c_ins, c_outs, c_sems)

        delta, nm, nv = _adam_math(w_ref[...], g_ref[...], m_ref[...], v_ref[...])
        d_ref[...] = delta
        nm_ref[...] = nm
        nv_ref[...] = nv
        if carry:
            @pl.when(pl.program_id(0) == steps - 1)
            def _():
                carry.finish(c_ins, c_outs, c_sems)

    spec = pl.BlockSpec((tm, C), lambda i: (i, 0))
    res = pl.pallas_call(
        body, name=name, grid=(steps,), in_specs=[spec] * 4 + [ANY] * n_ci, out_specs=[spec] * 3 + [ANY] * n_co,
        out_shape=[jax.ShapeDtypeStruct((R, C), F32)] * 3 + (carry.outs if carry else []),
        scratch_shapes=carry.sems if carry else [],
        input_output_aliases=carry.io_aliases(4, 3) if carry else {},
        compiler_params=_params(("arbitrary",)),
    )(w, m, v, g, *(carry.ins if carry else []))
    return (res[:3], res[3:]) if carry else res


def _adam_halves(name, w, m, v, mine, other, core, carry=None):
    R, C = w.shape
    Rh = mine.shape[0]
    tc = max(t for t in range(128, C + 1, 128) if C % t == 0 and R * t <= (3 << 17))
    steps = C // tc
    n_ci = len(carry.ins) if carry else 0
    n_co = len(carry.outs) if carry else 0

    def body(*refs):
        c_ref, w_ref, m_ref, v_ref, a_ref, b_ref = refs[:6]
        g_ref, d_ref, nm_ref, nv_ref = refs[6 + n_ci:10 + n_ci]
        c_ins, c_outs, c_sems = refs[6:6 + n_ci], refs[10 + n_ci:10 + n_ci + n_co], refs[10 + n_ci + n_co:]
        if carry:
            @pl.when(pl.program_id(0) == 0)
            def _():
                carry.start(c_ins, c_outs, c_sems)

        first = c_ref[0] == 0
        g = jnp.concatenate([jnp.where(first, a_ref[...], b_ref[...]),
                             jnp.where(first, b_ref[0:R - Rh, :], a_ref[0:R - Rh, :])], axis=0)
        delta, nm, nv = _adam_math(w_ref[...], g, m_ref[...], v_ref[...])
        g_ref[...] = g
        d_ref[...] = delta
        nm_ref[...] = nm
        nv_ref[...] = nv
        if carry:
            @pl.when(pl.program_id(0) == steps - 1)
            def _():
                carry.finish(c_ins, c_outs, c_sems)

    spec = pl.BlockSpec((R, tc), lambda i, c_ref: (0, i))
    h_spec = pl.BlockSpec((Rh, tc), lambda i, c_ref: (0, i))
    res = pl.pallas_call(
        body, name=name, out_shape=[jax.ShapeDtypeStruct((R, C), F32)] * 4 + (carry.outs if carry else []),
        grid_spec=pltpu.PrefetchScalarGridSpec(
            num_scalar_prefetch=1, grid=(steps,), in_specs=[spec, spec, spec, h_spec, h_spec] + [ANY] * n_ci,
            out_specs=[spec] * 4 + [ANY] * n_co, scratch_shapes=carry.sems if carry else []),
        input_output_aliases=carry.io_aliases(6, 4) if carry else {},
        compiler_params=_params(("arbitrary",)),
    )(core, w, m, v, mine, other, *(carry.ins if carry else []))
    return (res[:4], res[4:]) if carry else res


def _adam_small(name, w, m, v, g):
    def body(w_ref, m_ref, v_ref, g_ref, d_ref, nm_ref, nv_ref):
        delta, nm, nv = _adam_math(w_ref[...], g_ref[...], m_ref[...], v_ref[...])
        d_ref[...] = delta
        nm_ref[...] = nm
        nv_ref[...] = nv

    return pl.pallas_call(body, name=name, out_shape=[jax.ShapeDtypeStruct(w.shape, F32)] * 3,
                          compiler_params=_params())(w, m, v, g)


def _place():
    return lax.axis_index("x"), lax.axis_index("y"), lax.axis_index("c")


def _other_chips(x, y):
    return [(1 - x, y), (x, 1 - y), (1 - x, 1 - y)]


def _all_gather8(blk, name):
    R, N = blk.shape

    def body(x_ref, out_ref, send_sems, recv_sems, local_sem):
        x, y, c = _place()
        me = 4 * x + 2 * y + c
        mine = pltpu.make_async_copy(x_ref, out_ref.at[me], local_sem)
        mine.start()
        flips = [(j >> 2 & 1, j >> 1 & 1, j & 1) for j in range(1, 8)]
        peers = [((1 - x) if fx else x, (1 - y) if fy else y, (1 - c) if fc else c) for fx, fy, fc in flips]
        sends = []
        for j, peer in enumerate(peers):
            cp = pltpu.make_async_remote_copy(src_ref=x_ref, dst_ref=out_ref.at[me], send_sem=send_sems.at[j],
                                              recv_sem=recv_sems.at[j], device_id=peer, device_id_type=MESH)
            cp.start()
            sends.append(cp)
        for j, (px, py, pc) in enumerate(peers):
            pltpu.make_async_remote_copy(src_ref=x_ref, dst_ref=out_ref.at[4 * px + 2 * py + pc],
                                         send_sem=send_sems.at[j], recv_sem=recv_sems.at[j],
                                         device_id=(px, py, pc), device_id_type=MESH).wait_recv()
        for cp in sends:
            cp.wait_send()
        mine.wait()

    return pl.pallas_call(
        body, name=name, out_shape=jax.ShapeDtypeStruct((8, R, N), F32),
        in_specs=[pl.BlockSpec(memory_space=pltpu.VMEM)], out_specs=pl.BlockSpec(memory_space=pltpu.VMEM),
        scratch_shapes=[pltpu.SemaphoreType.DMA((7,)), pltpu.SemaphoreType.DMA((7,)), pltpu.SemaphoreType.DMA],
        compiler_params=_params(),
    )(blk)


def _piece(rows, piece):
    i, n, k = piece if len(piece) == 3 else (piece[0], piece[1], 1)
    assert rows % 16 == 0 and rows // 16 >= n, (rows, piece)
    lo, hi = (rows // 16 * i // n) * 16, (rows // 16 * (i + k) // n) * 16
    return pl.ds(lo, hi - lo)


def _gather_plan(shards, piece=(0, 1), into=None, ici=True):
    n = len(shards)

    def parts(ins, outs, sems):
        s1, r1, s2, r2, loc = sems
        x, y, c = _place()
        me = 2 * x + y
        chips = _other_chips(x, y)
        sib = (x, y, 1 - c)

        def rows(k):
            return _piece(shards[k].shape[1], piece)

        def ici_copy(k, j, slab, to):
            return pltpu.make_async_remote_copy(src_ref=ins[k].at[c, rows(k)], dst_ref=outs[k].at[slab, c, rows(k)],
                                                send_sem=s1.at[3 * k + j], recv_sem=r1.at[3 * k + j],
                                                device_id=to, device_id_type=MESH)

        def d2d(k, j, slab, half):
            return pltpu.make_async_remote_copy(src_ref=outs[k].at[slab, half, rows(k)],
                                                dst_ref=outs[k].at[slab, half, rows(k)],
                                                send_sem=s2.at[3 * k + j], recv_sem=r2.at[3 * k + j],
                                                device_id=sib, device_id_type=MESH)

        def own(k):
            return pltpu.make_async_remote_copy(src_ref=ins[k].at[:, rows(k)], dst_ref=outs[k].at[me, :, rows(k)],
                                                send_sem=loc.at[2 * k], recv_sem=loc.at[2 * k + 1],
                                                device_id=sib, device_id_type=MESH)

        return c, me, chips, ici_copy, d2d, own

    def start(ins, outs, sems):
        c, me, chips, ici_copy, d2d, own = parts(ins, outs, sems)
        for k in range(n):
            for j, (px, py) in enumerate(chips):
                (ici_copy(k, j, me, (px, py, c)) if ici else d2d(k, j, 2 * px + py, c)).start()
        for k in range(n):
            own(k).start()

    def finish(ins, outs, sems):
        c, me, chips, ici_copy, d2d, own = parts(ins, outs, sems)
        if ici:
            for k in range(n):
                for j, (px, py) in enumerate(chips):
                    ici_copy(k, j, 2 * px + py, (px, py, c)).wait_recv()
                    d2d(k, j, 2 * px + py, c).start()
        for k in range(n):
            for j, (px, py) in enumerate(chips):
                d2d(k, j, 2 * px + py, 1 - c).wait_recv()
        for k in range(n):
            own(k).wait()
            for j, (px, py) in enumerate(chips):
                if ici:
                    ici_copy(k, j, me, (px, py, c)).wait_send()
                d2d(k, j, 2 * px + py, c).wait_send()

    return _Plan(list(shards) + list(into or []), [jax.ShapeDtypeStruct((4,) + a.shape, a.dtype) for a in shards],
                 [pltpu.SemaphoreType.DMA((3 * n,))] * 4 + [pltpu.SemaphoreType.DMA((2 * n,))], start, finish,
                 aliases={n + k: k for k in range(n)} if into else None)


def _pair_plan(parts):
    n = len(parts)

    def copies(ins, outs, sems):
        send_sems, recv_sems = sems
        x, y, c = _place()
        return [pltpu.make_async_remote_copy(src_ref=ins[k].at[p, 1 - c], dst_ref=outs[k].at[p],
                                             send_sem=send_sems.at[4 * k + p], recv_sem=recv_sems.at[4 * k + p],
                                             device_id=(x, y, 1 - c), device_id_type=MESH)
                for k in range(n) for p in range(4)]

    def start(ins, outs, sems):
        for cp in copies(ins, outs, sems):
            cp.start()

    def finish(ins, outs, sems):
        for cp in copies(ins, outs, sems):
            cp.wait()

    return _Plan(parts, [jax.ShapeDtypeStruct((4,) + a.shape[2:], a.dtype) for a in parts],
                 [pltpu.SemaphoreType.DMA((4 * n,))] * 2, start, finish)


def _sibling_plan(arrs):
    n = len(arrs)

    def copies(ins, outs, sems):
        send_sems, recv_sems = sems
        x, y, c = _place()
        return [pltpu.make_async_remote_copy(src_ref=ins[k], dst_ref=outs[k], send_sem=send_sems.at[k],
                                             recv_sem=recv_sems.at[k], device_id=(x, y, 1 - c), device_id_type=MESH)
                for k in range(n)]

    def start(ins, outs, sems):
        for cp in copies(ins, outs, sems):
            cp.start()

    def finish(ins, outs, sems):
        for cp in copies(ins, outs, sems):
            cp.wait()

    return _Plan(arrs, [jax.ShapeDtypeStruct(a.shape, a.dtype) for a in arrs],
                 [pltpu.SemaphoreType.DMA((n,))] * 2, start, finish)


def _scatter_copies(arrs):
    def copies(ins, land, send_sems, recv_sems):
        x, y, c = _place()
        return [pltpu.make_async_remote_copy(src_ref=ins[k].at[2 * px + py], dst_ref=land[k].at[j],
                                             send_sem=send_sems.at[3 * k + j], recv_sem=recv_sems.at[3 * k + j],
                                             device_id=(px, py, c), device_id_type=MESH)
                for k in range(len(arrs)) for j, (px, py) in enumerate(_other_chips(x, y))]

    return copies, [lax.empty((3,) + a.shape[1:], a.dtype) for a in arrs]


def _gather_copies(shards, piece=(0, 1), lands=None):
    def copies(ins, land, send_sems, recv_sems):
        x, y, c = _place()
        return [pltpu.make_async_remote_copy(
                    src_ref=ins[k].at[c, _piece(shards[k].shape[1], piece)],
                    dst_ref=land[k].at[2 * x + y, c, _piece(shards[k].shape[1], piece)],
                    send_sem=send_sems.at[3 * k + j], recv_sem=recv_sems.at[3 * k + j],
                    device_id=(px, py, c), device_id_type=MESH)
                for k in range(len(shards)) for j, (px, py) in enumerate(_other_chips(x, y))]

    return copies, list(lands) if lands is not None else [lax.empty((4,) + a.shape, a.dtype) for a in shards]


def _split_start(arrs, copies_lands, ride, name, after=()):
    copies, lands = copies_lands
    n = len(arrs)
    rides = list(ride) if isinstance(ride, (list, tuple)) else [ride]
    n_thru = 2 * n + len(rides)

    def body(*refs):
        first_out = n_thru + len(after)
        for cp in copies(refs[:n], refs[n:2 * n], refs[first_out], refs[first_out + 1]):
            cp.start()

    hbm = [pltpu.with_memory_space_constraint(a, pltpu.HBM) for a in list(arrs) + lands + rides]
    res = pl.pallas_call(
        body, name=name,
        out_shape=[pltpu.SemaphoreType.DMA((3 * n,)), pltpu.SemaphoreType.DMA((3 * n,))]
        + [pltpu.HBM(a.shape, a.dtype) for a in hbm],
        in_specs=[HBM_SPEC] * n_thru + [ANY] * len(after),
        out_specs=[SEM_SPEC, SEM_SPEC] + [HBM_SPEC] * n_thru,
        input_output_aliases={i: 2 + i for i in range(n_thru)},
        compiler_params=pltpu.CompilerParams(has_side_effects=pltpu.SideEffectType.DATAFLOW_SIDE_EFFECTING),
    )(*hbm, *after)
    return res[0], res[1], res[2:2 + n], res[2 + n:2 + 2 * n], list(res[2 + 2 * n:])


def _split_wait(started, copies_lands, after, name):
    send_sems, recv_sems, arrs, lands, _ = started
    copies = copies_lands[0]
    n = len(arrs)

    def body(*refs):
        for cp in copies(refs[:n], refs[n:2 * n], refs[2 * n], refs[2 * n + 1]):
            cp.wait_send()
            cp.wait_recv()

    res = pl.pallas_call(
        body, name=name, out_shape=[pltpu.HBM(a.shape, a.dtype) for a in list(arrs) + list(lands)],
        in_specs=[HBM_SPEC] * (2 * n) + [SEM_SPEC, SEM_SPEC] + [ANY] * len(after), out_specs=[HBM_SPEC] * (2 * n),
        input_output_aliases={i: i for i in range(2 * n)},
        compiler_params=pltpu.CompilerParams(has_side_effects=pltpu.SideEffectType.DATAFLOW_SIDE_EFFECTING),
    )(*arrs, *lands, send_sems, recv_sems, *after)
    return list(res[:n]), list(res[n:])


def _add_pair(parts, sib, core, name):
    P4, _, Rh, C = parts.shape
    tm, tc = _tile2(Rh, C, 16)

    def body(c_ref, a_ref, b_ref, o_ref):
        o_ref[...] = (a_ref[0].astype(F32) + b_ref[...].astype(F32)).astype(BF16)

    spec = pl.BlockSpec((1, tm, tc), lambda p, i, j, c_ref: (p, i, j))
    return pl.pallas_call(
        body, name=name, out_shape=jax.ShapeDtypeStruct((P4, Rh, C), BF16),
        grid_spec=pltpu.PrefetchScalarGridSpec(
            num_scalar_prefetch=1, grid=(P4, Rh // tm, C // tc),
            in_specs=[pl.BlockSpec((1, 1, tm, tc), lambda p, i, j, c_ref: (p, c_ref[0], i, j)), spec], out_specs=spec),
        compiler_params=_params(("parallel",) * 3),
    )(core, parts, sib)


def _sum_slabs(pre, recv, chip, name):
    _, Rh, C = pre.shape
    tm, tc = _tile2(Rh, C, 16)

    def body(me_ref, own_ref, r_ref, o_ref):
        acc = own_ref[0].astype(F32)
        for j in range(3):
            acc = acc + r_ref[j].astype(F32)
        o_ref[...] = acc

    return pl.pallas_call(
        body, name=name, out_shape=jax.ShapeDtypeStruct((Rh, C), F32),
        grid_spec=pltpu.PrefetchScalarGridSpec(
            num_scalar_prefetch=1, grid=(Rh // tm, C // tc),
            in_specs=[pl.BlockSpec((1, tm, tc), lambda i, j, me_ref: (me_ref[0], i, j)),
                      pl.BlockSpec((3, tm, tc), lambda i, j, me_ref: (0, i, j))],
            out_specs=pl.BlockSpec((tm, tc), lambda i, j, me_ref: (i, j))),
        compiler_params=_params(("parallel", "parallel")),
    )(chip, pre, recv)


def kernel(x, c, positions, w_ada, b_ada, w_in, g_q_a, w_q_b, g_kv_a, w_kv_b, w_o_a, w_conv, w_o_b, w_o, ln1_g, ln1_b, w_ffn_in, w_ffn_out, ln2_g, ln2_b, loss_target, m_w_ada, m_b_ada, m_w_in, m_g_q_a, m_w_q_b, m_g_kv_a, m_w_kv_b, m_w_o_a, m_w_conv, m_w_o_b, m_w_o, m_ln1_g, m_ln1_b, m_w_ffn_in, m_w_ffn_out, m_ln2_g, m_ln2_b, v_w_ada, v_b_ada, v_w_in, v_g_q_a, v_w_q_b, v_g_kv_a, v_w_kv_b, v_w_o_a, v_w_conv, v_w_o_b, v_w_o, v_ln1_g, v_ln1_b, v_w_ffn_in, v_w_ffn_out, v_ln2_g, v_ln2_b):
    S, D = x.shape[1], x.shape[2]
    F = w_ffn_out.shape[1] * 4
    ax, ay, ac = _place()
    chip = 2 * ax + ay
    dev = 4 * ax + 2 * ay + ac
    x2, tgt = x[0], loss_target[0]
    w_ada2, w_in2, w_q_b2, w_kv_b2 = w_ada[0], w_in[0], w_q_b[0], w_kv_b[0]
    w_o_a2, w_o_b2, w_o2, w_ffn_in2, w_ffn_out2 = w_o_a[0], w_o_b[0], w_o[0], w_ffn_in[0], w_ffn_out[0]
    NA = w_ada2.shape[1]
    CW = w_conv.shape[2]

    inv_freq = 1.0 / (ROPE_THETA ** (jnp.arange(0, QK_ROPE, 2, dtype=F32) / QK_ROPE))
    ang = positions[0].astype(F32)[:, None] * inv_freq
    cos, sin = jnp.cos(ang), jnp.sin(ang)
    z32, z64, z96 = jnp.zeros((S, 32), F32), jnp.zeros((S, 64), F32), jnp.zeros((S, 96), F32)
    tab = jnp.concatenate([cos, cos, z64, -sin, z96, z32, sin, z64], axis=1)

    def halves(a):
        return a.reshape(2, a.shape[0] // 2, a.shape[1])

    def whole(g):
        return g.reshape(4, 2 * g.shape[2], g.shape[3])

    def cols(g):
        return jnp.transpose(g, (1, 0, 2)).reshape(g.shape[1], 4 * g.shape[2])

    w_inT, m_w_inT, v_w_inT = w_in2.T, m_w_in[0].T, v_w_in[0].T
    CS = w_inT.shape[0]
    CSP = -(-CS // 32) * 32
    sh_in = halves(jnp.pad(w_inT.astype(BF16), ((0, CSP - CS), (0, 0))))
    c_all = _all_gather8(c, "gather_c").reshape(8, D)
    wconv_all = _all_gather8(w_conv[0], "gather_wconv")
    w_conv_full = jnp.transpose(wconv_all[0::2], (1, 0, 2)).reshape(3, D)
    b_sh = lax.dynamic_slice(b_ada, (0, chip * NA), (1, NA))
    mod_sh = _ada_fwd(c_all, w_ada2, b_sh)
    mod_all = _all_gather8(mod_sh, "gather_mod")
    mod = lax.dynamic_slice(mod_all[0::2], (0, dev, 0), (4, 1, NA)).reshape(6, D)
    shift1, scale1, gate1, shift2, scale2, gate2 = (mod[k:k + 1] for k in range(6))

    n_pc = 8
    st_in, cl_in, sh_in_t, l_in, rides = [], [], [sh_in], None, [shift1, w_conv_full]
    for i in range(n_pc):
        cl_in.append(_gather_copies(sh_in_t, (i, n_pc), l_in))
        st_in.append(_split_start(sh_in_t, cl_in[i], rides, "gather_in%d_start" % i))
        sh_in_t, l_in, rides = st_in[i][2], st_in[i][3], st_in[i][4]
    shift1, w_conv_full = rides
    others = lax.optimization_barrier((w_q_b2, w_kv_b2, w_o_a2, w_o_b2, w_o2, w_ffn_in2, w_ffn_out2, shift1))
    sh_qb, sh_kvb, sh_oa, sh_ob, sh_o, sh_fi, sh_fo = (halves(w.astype(BF16)) for w in others[:7])
    shift1 = others[7]
    for i in range(n_pc):
        casts = [sh_qb, sh_kvb, sh_oa, sh_ob, sh_o, sh_fi, sh_fo] if i == 0 else []
        sh_in_t, l_in = _split_wait(st_in[i][:2] + (sh_in_t, l_in, None), cl_in[i], casts, "gather_in%d_wait" % i)
        if i < n_pc - 1:
            l_in = _run_plan(_gather_plan(sh_in_t, (i, n_pc), into=l_in, ici=False), "handon_in%d" % i)
    sh_a1, sh_a2 = [sh_qb, sh_kvb], [sh_oa, sh_ob, sh_o]
    cl_a1, cl_a2, cl_fi, cl_fo = (_gather_copies(g) for g in (sh_a1, sh_a2, [sh_fi], [sh_fo]))
    st_a1 = _split_start(sh_a1, cl_a1, shift1, "gather_a1_start", after=[l_in[0]])
    st_a2 = _split_start(sh_a2, cl_a2, st_a1[4], "gather_a2_start")
    u, (g_in,) = _modulate(x2, scale1, st_a2[4][0], "modulate1",
                           carry=_gather_plan(sh_in_t, (n_pc - 1, n_pc), into=l_in, ici=False))
    g_in = whole(g_in)

    def in_rows(lo, hi):
        parts = [g_in[p, max(lo, p * CS) - p * CS:min(hi, (p + 1) * CS) - p * CS]
                 for p in range(4) if max(lo, p * CS) < min(hi, (p + 1) * CS)]
        return parts[0] if len(parts) == 1 else jnp.concatenate(parts, axis=0)

    n_qkv = Q_LORA + KV_LORA + QK_ROPE
    W_qkvT = jnp.pad(in_rows(0, n_qkv), ((0, QKV_A - n_qkv), (0, 0)))
    W_convT = in_rows(n_qkv, n_qkv + 3 * D)
    W_gateT = in_rows(n_qkv + 3 * D, n_qkv + 5 * D)

    pq = _matmul(u, W_qkvT, "nt", F32, "proj_qkv")
    pc = _matmul(u, W_convT, "nt", BF16, "proj_conv")
    sh_a1, la1 = _split_wait(st_a1, cl_a1, [pc], "gather_a1_wait")
    pg, (g_qb, g_kvb) = _matmul(u, W_gateT, "nt", BF16, "proj_gate", carry=_gather_plan(sh_a1, into=la1, ici=False))
    st_fi = _split_start([sh_fi], cl_fi, g_q_a, "gather_fi_start", after=[pg])
    W_qb = jnp.pad(cols(whole(g_qb)).reshape(Q_LORA, N_HEADS, QK_NOPE + QK_ROPE),
                   ((0, 0), (0, 0), (0, QK_PAD - QK_NOPE - QK_ROPE))).reshape(Q_LORA, N_HEADS * QK_PAD)
    W_kvb = cols(whole(g_kvb))
    rq, rkv, kr = _rms_fwd(pq, tab, st_fi[4][0], g_kv_a)
    kv = _matmul(rkv, W_kvb, "nn", BF16, "kv_b")
    sh_a2, la2 = _split_wait(st_a2, cl_a2, [kv], "gather_a2_wait")
    def rope_heads(r, t):
        return jnp.concatenate([r[:, lo:lo + 128] if lo % QK_PAD == 0 else _rope(r[:, lo:lo + 128], t, 1)
                                for lo in range(0, r.shape[1], 128)], axis=1)

    q, (g_oa, g_ob, g_o) = _matmul(rq, W_qb, "nn", BF16, "q_b", carry=_gather_plan(sh_a2, into=la2, ici=False),
                                   finish=(rope_heads, tab))
    o, lse = _attn_fwd(q, kv, kr)
    W_oa, W_ob, W_o = (g.reshape(-1, D) for g in (g_oa, g_ob, g_o))
    hb = _conv_fwd(pc, w_conv_full)
    st_fo = _split_start([sh_fo], cl_fo, ln1_g, "gather_fo_start", after=[o])
    y_b = _matmul(hb, W_ob, "nn", BF16, "o_b")
    y_a = _matmul(o, W_oa, "nn", BF16, "o_a")
    merged = _merge_fwd(y_a, y_b, pg)
    sh_fi_t, lfi = _split_wait(st_fi, cl_fi, [merged], "gather_fi_wait")
    mix, g_fi = _matmul(merged, W_o, "nn", F32, "w_o", carry=_gather_plan(sh_fi_t, (0, 2), into=lfi, ici=False))
    (x1, u2), (g_fi,) = _ln1_fwd(x2, mix, gate1, st_fo[4][0], ln1_b, scale2, shift2,
                                 carry=_gather_plan(sh_fi_t, (1, 2), into=g_fi, ici=False))
    W_fi = whole(g_fi)
    hh = _matmul(u2, W_fi, "nn", BF16, "ffn_in", shards="b")
    sh_fo_t, lfo = _split_wait(st_fo, cl_fo, [hh], "gather_fo_wait")
    act, (g_fo,) = _swiglu_fwd(hh, carry=_gather_plan(sh_fo_t, into=lfo, ici=False))
    W_fo = g_fo.reshape(F, D)
    ffn = _matmul(act, W_fo, "nn", F32, "ffn_out")

    core_i = ac.astype(jnp.int32).reshape(1)
    chip_i = chip.astype(jnp.int32).reshape(1)

    def uncols(g):
        return jnp.transpose(g.reshape(g.shape[0], 4, g.shape[1] // 4), (1, 0, 2))

    def slabs(p):
        return p.reshape(4, 2, p.shape[1] // 2, p.shape[2])

    def add_pairs(parts, sibs, nms):
        return [_add_pair(a, b, core_i, "add_pair_" + nm) for a, b, nm in zip(parts, sibs, nms)]

    def sum_all(pre, recv, nms):
        return [_sum_slabs(a, r, chip_i, "sum_slabs_" + nm) for a, r, nm in zip(pre, recv, nms)]

    dffn, dx1a, loss_acc, d_ln2_g, d_ln2_b, d_gate2 = _ln2_loss_bwd(x1, ffn, gate2, ln2_g, ln2_b, tgt)
    dW_fo = _matmul(act, dffn, "tn", BF16, "d_w_ffn_out")
    p_fo = [slabs(dW_fo.reshape(4, -1, D))]
    dact, s_fo = _matmul(dffn, W_fo, "nt", BF16, "d_act", carry=_pair_plan(p_fo))
    pre_fo = add_pairs(p_fo, s_fo, ["w_ffn_out"])
    cs_fo = _scatter_copies(pre_fo)
    st_sfo = _split_start(pre_fo, cs_fo, scale2, "scatter_fo_start")
    dhh = _swiglu_bwd(dact, hh)
    dW_fi = _matmul(u2, dhh, "tn", BF16, "d_w_ffn_in", shards="o")
    p_fi = [slabs(dW_fi)]
    du2, s_fi = _matmul(dhh, W_fi, "nt", F32, "d_u2", carry=_pair_plan(p_fi), shards="b")
    pre_fi = add_pairs(p_fi, s_fi, ["w_ffn_in"])
    cs_fi = _scatter_copies(pre_fi)
    st_sfi = _split_start(pre_fi, cs_fi, st_sfo[4], "scatter_fi_start")
    dmix, dxa, d_shift2, d_scale2, d_ln1_g, d_ln1_b, d_gate1 = _ln1_bwd(x2, mix, dx1a, du2, gate1, ln1_g, ln1_b, st_sfi[4][0])
    dW_o = _matmul(merged, dmix, "tn", BF16, "d_w_o")
    dmerged = _matmul(dmix, W_o, "nt", BF16, "d_merged")
    dy_a, dy_b, dgate = _merge_bwd(dmerged, y_a, y_b, pg)
    dW_oa = _matmul(o, dy_a, "tn", BF16, "d_w_o_a")
    do = _matmul(dy_a, W_oa, "nt", BF16, "d_o")
    dW_ob = _matmul(hb, dy_b, "tn", BF16, "d_w_o_b")
    p_mid = [slabs(g.reshape(4, -1, D)) for g in (dW_oa, dW_ob, dW_o)]
    dhb, s_mid = _matmul(dy_b, W_ob, "nt", BF16, "d_hb", carry=_pair_plan(p_mid))
    pre_mid = add_pairs(p_mid, s_mid, ["w_o_a", "w_o_b", "w_o"])
    cs_mid = _scatter_copies(pre_mid)
    st_smid = _split_start(pre_mid, cs_mid, w_conv_full, "scatter_mid_start")
    dconv, d_wconv = _conv_bwd(dhb, pc, st_smid[4][0])
    dq, dkv, dkr, _ = _attn_bwd(q, kv, kr, do, o, lse, tab, carry=_token_plan(st_smid[4][0]))
    names_a = ["w_ffn_out", "w_ffn_in", "w_o_a", "w_o_b", "w_o"]
    dW_qb = _matmul(rq, dq, "tn", BF16, "d_w_q_b")
    d_rq = _matmul(dq, W_qb, "nt", F32, "d_rq")
    dW_kvb = _matmul(rkv, dkv, "tn", BF16, "d_w_kv_b")
    d_rkv = _matmul(dkv, W_kvb, "nt", F32, "d_rkv")
    dqkv, d_g_q, d_g_kv = _rms_bwd(d_rq, d_rkv, pq, dkr, g_q_a, g_kv_a)
    dW_qkvT = _matmul(dqkv, u, "tn", BF16, "d_w_qkv")
    dW_convT = _matmul(dconv, u, "tn", BF16, "d_w_conv")
    dW_gateT = _matmul(dgate, u, "tn", BF16, "d_w_gate")
    pre_fo, r_fo = _split_wait(st_sfo, cs_fo, [dW_qkvT], "scatter_fo_wait")
    pre_fi, r_fi = _split_wait(st_sfi, cs_fi, [dW_qkvT], "scatter_fi_wait")
    pre_mid, r_mid = _split_wait(st_smid, cs_mid, [dW_qkvT], "scatter_mid_wait")
    fin_a = sum_all(pre_fo + pre_fi + pre_mid, r_fo + r_fi + r_mid, names_a)
    srcs = [(0, dW_qkvT[:n_qkv]), (n_qkv, dW_convT), (n_qkv + 3 * D, dW_gateT)]
    rows_of = []
    for p in range(4):
        for lo, src in srcs:
            a, b = max(lo, p * CS), min(lo + src.shape[0], (p + 1) * CS)
            if a < b:
                rows_of.append(src[a - lo:b - lo])
        rows_of.append(jnp.zeros((CSP - CS, D), BF16))
    dW_inT = jnp.concatenate(rows_of, axis=0).reshape(4, CSP, D)
    dW_qb_u = dW_qb.reshape(Q_LORA, N_HEADS, QK_PAD)[:, :, :QK_NOPE + QK_ROPE].reshape(Q_LORA, -1)
    names_b = ["w_in", "w_q_b", "w_kv_b"]
    p_b = [slabs(dW_inT), slabs(uncols(dW_qb_u)), slabs(uncols(dW_kvb))]
    du, s_b = _matmul(dqkv, W_qkvT, "nn", F32, "d_u_qkv", carry=_pair_plan(p_b))
    pre_b = add_pairs(p_b, s_b, names_b)
    cs_b = _scatter_copies(pre_b)
    st_b = _split_start(pre_b, cs_b, scale1, "scatter_last_start")
    du, fs_a = _matmul(dconv, W_convT, "nn", F32, "d_u_conv", add=du, carry=_sibling_plan(fin_a))
    du = _matmul(dgate, W_gateT, "nn", F32, "d_u_gate", add=du)
    grad_x, d_shift1, d_scale1 = _dx_final(dxa, du, x2, st_b[4][0])

    big = {}
    ws = dict(w_in=(w_inT, m_w_inT, v_w_inT), w_q_b=(w_q_b2, m_w_q_b[0], v_w_q_b[0]),
              w_kv_b=(w_kv_b2, m_w_kv_b[0], v_w_kv_b[0]), w_o_a=(w_o_a2, m_w_o_a[0], v_w_o_a[0]),
              w_o_b=(w_o_b2, m_w_o_b[0], v_w_o_b[0]), w_o=(w_o2, m_w_o[0], v_w_o[0]),
              w_ffn_in=(w_ffn_in2, m_w_ffn_in[0], v_w_ffn_in[0]), w_ffn_out=(w_ffn_out2, m_w_ffn_out[0], v_w_ffn_out[0]))

    def adam_of(nm, a, b, carry=None):
        w_, m_, v_ = ws[nm]
        return _adam_halves("adam_" + nm, w_, m_, v_, a, b, core_i, carry)

    for nm, a, b in zip(names_a, fin_a, fs_a):
        big[nm] = adam_of(nm, a, b, _token_plan(st_b[4][0]))[0]
    done = [big[nm][1] for nm in names_a] + [grad_x]
    pre_b, r_b = _split_wait(st_b, cs_b, done, "scatter_last_wait")
    fin_b = sum_all(pre_b, r_b, names_b)
    fs_b = _run_plan(_sibling_plan(fin_b), "sibling_last")
    for nm, a, b in zip(names_b, fin_b, fs_b):
        big[nm] = adam_of(nm, a, b)

    def pad_d(v):
        return jnp.pad(v, ((0, 0), (0, D - v.shape[1])))

    small = _pack_rows([d_ln1_g, d_ln1_b, d_ln2_g, d_ln2_b, pad_d(d_g_q), pad_d(d_g_kv), d_wconv,
                         d_shift1, d_scale1, d_gate1, d_shift2, d_scale2, d_gate2, pad_d(loss_acc)], 16, after=[pre_b[1]])
    small_all = _all_gather8(small, "gather_small")
    small_sum = _sum8(small_all)
    loss = small_sum[15, 0]
    g_ln1_g, g_ln1_b, g_ln2_g, g_ln2_b = (small_sum[k:k + 1] for k in range(4))
    g_g_q, g_g_kv = small_sum[4:5, :Q_LORA], small_sum[5:6, :KV_LORA]
    g_wconv = lax.dynamic_slice(small_sum[6:9], (0, chip * CW), (3, CW))
    g_b_ada = small_sum[9:15].reshape(1, 6 * D)
    dmod_all = small_all[:, 9:15, :].reshape(8, 6 * D)
    g_w_ada = _ada_bwd(c_all, lax.dynamic_slice(dmod_all, (0, chip * NA), (8, NA)))
    big["w_ada"] = [g_w_ada] + list(_adam("adam_w_ada", w_ada2, m_w_ada[0], v_w_ada[0], g_w_ada))
    sm = {}
    for nm, w_, m_, v_, g_ in [("b_ada", b_ada, m_b_ada, v_b_ada, g_b_ada), ("g_q_a", g_q_a, m_g_q_a, v_g_q_a, g_g_q),
                               ("g_kv_a", g_kv_a, m_g_kv_a, v_g_kv_a, g_g_kv),
                               ("w_conv", w_conv[0], m_w_conv[0], v_w_conv[0], g_wconv),
                               ("ln1_g", ln1_g, m_ln1_g, v_ln1_g, g_ln1_g), ("ln1_b", ln1_b, m_ln1_b, v_ln1_b, g_ln1_b),
                               ("ln2_g", ln2_g, m_ln2_g, v_ln2_g, g_ln2_g), ("ln2_b", ln2_b, m_ln2_b, v_ln2_b, g_ln2_b)]:
        sm[nm] = (g_,) + tuple(_adam_small("adam_" + nm, w_, m_, v_, g_))

    order = ["w_ada", "b_ada", "w_in", "g_q_a", "w_q_b", "g_kv_a", "w_kv_b", "w_o_a", "w_conv", "w_o_b", "w_o",
             "ln1_g", "ln1_b", "w_ffn_in", "w_ffn_out", "ln2_g", "ln2_b"]
    lead = {"b_ada", "g_q_a", "g_kv_a", "ln1_g", "ln1_b", "ln2_g", "ln2_b"}

    def leaf(nm, k):
        val = big[nm][k] if nm in big else sm[nm][k]
        if nm == "w_in":
            val = val.T
        return val if nm in lead else val[None]

    outs = [loss, grad_x[None]]
    for k in range(4):
        outs += [leaf(nm, k) for nm in order]
    return tuple(outs)
```
